```python
import math
import jax, jax.numpy as jnp
from jax import lax
import numpy as np

D_MODEL = 1024
BATCH = 32
SEQ = 2048
DEPTH = 1

N_MEM = 256
HEAD_DIM = 64
N_DIL_HEADS = 8
N_FOX_HEADS = 8
DIL_WIDTH = N_DIL_HEADS * HEAD_DIM
FOX_WIDTH = N_FOX_HEADS * HEAD_DIM
MIX_WIDTH = DIL_WIDTH + FOX_WIDTH
IN_WIDTH = 3 * DIL_WIDTH + 3 * FOX_WIDTH + N_FOX_HEADS
DIL_CONFIGS = ((128, 1), (512, 4), (2048, 16))
BLOCK = 128
N_XATTN_HEADS = 4
XATTN_HEAD_DIM = D_MODEL // N_XATTN_HEADS
D_FF = 4 * D_MODEL
EPS = 1e-6
NEG = -1e30

kernel_name = "hybrid_dilated_fox_block"


def _rmsnorm(x, g):
    x32 = x.astype(jnp.float32)
    y = x32 * lax.rsqrt(jnp.mean(x32 * x32, axis=-1, keepdims=True) + EPS) * g.astype(jnp.float32)
    return y.astype(x.dtype)


def _alibi_slopes(n):
    return 2.0 ** (-(jnp.arange(1, n + 1, dtype=jnp.float32) * (8.0 / n)))


def _dilated_branch(q, k, v, slopes, window, dilation):
    B, S, H, Dh = q.shape
    span = dilation * BLOCK
    s_pad = -(-S // span) * span
    pad = ((0, 0), (0, s_pad - S), (0, 0), (0, 0))
    q, k, v = (jnp.pad(a, pad) for a in (q, k, v))
    L = s_pad // dilation
    nb = L // BLOCK

    def to_res(a):
        return a.reshape(B, L, dilation, H, Dh).transpose(0, 2, 1, 3, 4).reshape(B, dilation, nb, BLOCK, H, Dh)

    def with_prev(a):
        prev = jnp.pad(a, ((0, 0), (0, 0), (1, 0), (0, 0), (0, 0), (0, 0)))[:, :, :-1]
        return jnp.concatenate([prev, a], axis=3)

    qr = to_res(q)
    kb = with_prev(to_res(k))
    vb = with_prev(to_res(v))

    steps = window // dilation
    qi = jnp.arange(BLOCK)[:, None]
    kj = jnp.arange(2 * BLOCK)[None, :]
    delta = qi + BLOCK - kj
    blk = jnp.arange(nb)[:, None, None]
    valid = (delta >= 0) & (delta <= steps) & ((blk > 0) | (kj >= BLOCK))
    dist = (delta * dilation).astype(jnp.float32)
    bias = jnp.where(valid[:, None], -slopes[None, :, None, None] * dist[None, None], NEG)

    scale = 1.0 / math.sqrt(Dh)
    s = jnp.einsum('brnqhd,brnkhd->brnhqk', qr, kb).astype(jnp.float32) * scale + bias[None, None]
    lse = jax.nn.logsumexp(s, axis=-1)
    p = jnp.exp(s - lse[..., None])
    o = jnp.einsum('brnhqk,brnkhd->brnqhd', p, vb.astype(jnp.float32))

    o = o.reshape(B, dilation, L, H, Dh).transpose(0, 2, 1, 3, 4).reshape(B, s_pad, H, Dh)[:, :S]
    lse = lse.transpose(0, 1, 2, 4, 3).reshape(B, dilation, L, H).transpose(0, 2, 1, 3).reshape(B, s_pad, H)[:, :S]
    return o, lse


def _dilated_attention(q, k, v):
    slopes = _alibi_slopes(q.shape[2])
    outs, lses = [], []
    for window, dilation in DIL_CONFIGS:
        o, lse = _dilated_branch(q, k, v, slopes, window, dilation)
        outs.append(o)
        lses.append(lse)
    w = jax.nn.softmax(jnp.stack(lses, axis=0), axis=0)
    o = jnp.sum(w[..., None] * jnp.stack(outs, axis=0), axis=0)
    return o.astype(q.dtype)


def _forgetting_attention(q, k, v, log_f):
    B, S, H, Dh = q.shape
    nb = S // BLOCK
    c = jnp.cumsum(log_f, axis=1).transpose(0, 2, 1)
    qb = q.reshape(B, nb, BLOCK, H, Dh).transpose(1, 0, 2, 3, 4)
    cq = c.reshape(B, H, nb, BLOCK).transpose(2, 0, 1, 3)
    kpos = jnp.arange(S)
    scale = 1.0 / math.sqrt(Dh)

    def one_block(args):
        qblk, cblk, n = args
        s = jnp.einsum('bqhd,bkhd->bhqk', qblk, k).astype(jnp.float32) * scale
        s = s + (cblk[..., :, None] - c[:, :, None, :])
        qpos = n * BLOCK + jnp.arange(BLOCK)
        s = jnp.where((kpos[None, :] <= qpos[:, None])[None, None], s, NEG)
        p = jax.nn.softmax(s, axis=-1)
        return jnp.einsum('bhqk,bkhd->bqhd', p.astype(v.dtype), v)

    o = lax.map(one_block, (qb, cq, jnp.arange(nb)))
    return o.transpose(1, 0, 2, 3, 4).reshape(B, S, H, Dh)


def _hybrid_mixer(h, w_in, b_forget, w_out):
    B, S, _ = h.shape
    z = h @ w_in
    o0 = 3 * DIL_WIDTH
    o1 = o0 + 3 * FOX_WIDTH
    qa, ka, va = jnp.split(z[..., :o0], 3, axis=-1)
    qf, kf, vf = jnp.split(z[..., o0:o1], 3, axis=-1)
    gate = z[..., o1:]
    shp_a = (B, S, N_DIL_HEADS, HEAD_DIM)
    shp_f = (B, S, N_FOX_HEADS, HEAD_DIM)
    ya = _dilated_attention(qa.reshape(shp_a), ka.reshape(shp_a), va.reshape(shp_a))
    log_f = jax.nn.log_sigmoid(gate.astype(jnp.float32) + b_forget.astype(jnp.float32))
    yf = _forgetting_attention(qf.reshape(shp_f), kf.reshape(shp_f), vf.reshape(shp_f), log_f)
    y = jnp.concatenate([ya.reshape(B, S, DIL_WIDTH), yf.reshape(B, S, FOX_WIDTH).astype(ya.dtype)], axis=-1)
    return y @ w_out


def _cross_attention(h, m, w_xq, w_xk, w_xv, w_xo):
    B, S, _ = h.shape
    M = m.shape[1]
    q = (h @ w_xq).reshape(B, S, N_XATTN_HEADS, XATTN_HEAD_DIM)
    k = (m @ w_xk).reshape(B, M, N_XATTN_HEADS, XATTN_HEAD_DIM)
    v = (m @ w_xv).reshape(B, M, N_XATTN_HEADS, XATTN_HEAD_DIM)
    s = jnp.einsum('bqhd,bkhd->bhqk', q, k).astype(jnp.float32) / math.sqrt(XATTN_HEAD_DIM)
    p = jax.nn.softmax(s, axis=-1)
    o = jnp.einsum('bhqk,bkhd->bqhd', p.astype(v.dtype), v).reshape(B, S, N_XATTN_HEADS * XATTN_HEAD_DIM)
    return o @ w_xo


def _sq_relu_mlp(h, w_up, w_down):
    a = jax.nn.relu(h @ w_up)
    return (a * a) @ w_down


def _fwd_setup_inputs(seed: int = 0) -> dict:
    key = jax.random.key(seed)
    ks = jax.random.split(key, 16)
    f32 = jnp.float32

    def w(k, shape):
        return jax.random.normal(k, shape, f32) * shape[0] ** -0.5

    def gain(k):
        return 1.0 + 0.05 * jax.random.normal(k, (D_MODEL,), f32)

    return {
        "x": jax.random.normal(ks[0], (BATCH, SEQ, D_MODEL), f32),
        "mem": jax.random.normal(ks[1], (BATCH, N_MEM, D_MODEL), f32),
        "g_mix": gain(ks[2]),
        "w_in": w(ks[3], (D_MODEL, IN_WIDTH)),
        "b_forget": 0.1 * jax.random.normal(ks[4], (N_FOX_HEADS,), f32),
        "w_out": w(ks[5], (MIX_WIDTH, D_MODEL)),
        "g_xattn": gain(ks[6]),
        "g_mem": gain(ks[7]),
        "w_xq": w(ks[8], (D_MODEL, N_XATTN_HEADS * XATTN_HEAD_DIM)),
        "w_xk": w(ks[9], (D_MODEL, N_XATTN_HEADS * XATTN_HEAD_DIM)),
        "w_xv": w(ks[10], (D_MODEL, N_XATTN_HEADS * XATTN_HEAD_DIM)),
        "w_xo": w(ks[11], (N_XATTN_HEADS * XATTN_HEAD_DIM, D_MODEL)),
        "g_mlp": gain(ks[12]),
        "w_up": w(ks[13], (D_MODEL, D_FF)),
        "w_down": w(ks[14], (D_FF, D_MODEL)),
        "g_final": gain(ks[15]),
    }


def _fwd_reference(x, mem, g_mix, w_in, b_forget, w_out, g_xattn, g_mem, w_xq, w_xk, w_xv, w_xo,
              g_mlp, w_up, w_down, g_final):
    m = _rmsnorm(mem, g_mem)
    for _ in range(DEPTH):
        x = x + _hybrid_mixer(_rmsnorm(x, g_mix), w_in, b_forget, w_out)
        x = x + _cross_attention(_rmsnorm(x, g_xattn), m, w_xq, w_xk, w_xv, w_xo)
        x = x + _sq_relu_mlp(_rmsnorm(x, g_mlp), w_up, w_down)
    return _rmsnorm(x, g_final)


import jax as _jax
import jax.numpy as _jnp

TWIN_FORMAT = 'train_step'
FWD_PARAMS = ['x', 'mem', 'g_mix', 'w_in', 'b_forget', 'w_out', 'g_xattn', 'g_mem', 'w_xq', 'w_xk', 'w_xv', 'w_xo', 'g_mlp', 'w_up', 'w_down', 'g_final']
TWIN_WEIGHTS = ['g_mix', 'w_in', 'b_forget', 'w_out', 'g_xattn', 'g_mem', 'w_xq', 'w_xk', 'w_xv', 'w_xo', 'g_mlp', 'w_up', 'w_down', 'g_final']
TWIN_DIFF_INPUT = 'x'
TWIN_INPUTS = ['x', 'mem', 'g_mix', 'w_in', 'b_forget', 'w_out', 'g_xattn', 'g_mem', 'w_xq', 'w_xk', 'w_xv', 'w_xo', 'g_mlp', 'w_up', 'w_down', 'g_final', 'loss_target', 'm_g_mix', 'm_w_in', 'm_b_forget', 'm_w_out', 'm_g_xattn', 'm_g_mem', 'm_w_xq', 'm_w_xk', 'm_w_xv', 'm_w_xo', 'm_g_mlp', 'm_w_up', 'm_w_down', 'm_g_final', 'v_g_mix', 'v_w_in', 'v_b_forget', 'v_w_out', 'v_g_xattn', 'v_g_mem', 'v_w_xq', 'v_w_xk', 'v_w_xv', 'v_w_xo', 'v_g_mlp', 'v_w_up', 'v_w_down', 'v_g_final']
TWIN_OUTPUTS = ['loss', 'grad_x', 'grad_g_mix', 'grad_w_in', 'grad_b_forget', 'grad_w_out', 'grad_g_xattn', 'grad_g_mem', 'grad_w_xq', 'grad_w_xk', 'grad_w_xv', 'grad_w_xo', 'grad_g_mlp', 'grad_w_up', 'grad_w_down', 'grad_g_final', 'delta_g_mix', 'delta_w_in', 'delta_b_forget', 'delta_w_out', 'delta_g_xattn', 'delta_g_mem', 'delta_w_xq', 'delta_w_xk', 'delta_w_xv', 'delta_w_xo', 'delta_g_mlp', 'delta_w_up', 'delta_w_down', 'delta_g_final', 'new_m_g_mix', 'new_m_w_in', 'new_m_b_forget', 'new_m_w_out', 'new_m_g_xattn', 'new_m_g_mem', 'new_m_w_xq', 'new_m_w_xk', 'new_m_w_xv', 'new_m_w_xo', 'new_m_g_mlp', 'new_m_w_up', 'new_m_w_down', 'new_m_g_final', 'new_v_g_mix', 'new_v_w_in', 'new_v_b_forget', 'new_v_w_out', 'new_v_g_xattn', 'new_v_g_mem', 'new_v_w_xq', 'new_v_w_xk', 'new_v_w_xv', 'new_v_w_xo', 'new_v_g_mlp', 'new_v_w_up', 'new_v_w_down', 'new_v_g_final']
TWIN_LEAF_KINDS = {'loss': 'loss', 'grad_x': 'grad_x', 'grad_g_mix': 'grad_w', 'grad_w_in': 'grad_w', 'grad_b_forget': 'grad_w', 'grad_w_out': 'grad_w', 'grad_g_xattn': 'grad_w', 'grad_g_mem': 'grad_w', 'grad_w_xq': 'grad_w', 'grad_w_xk': 'grad_w', 'grad_w_xv': 'grad_w', 'grad_w_xo': 'grad_w', 'grad_g_mlp': 'grad_w', 'grad_w_up': 'grad_w', 'grad_w_down': 'grad_w', 'grad_g_final': 'grad_w', 'delta_g_mix': 'delta_w', 'delta_w_in': 'delta_w', 'delta_b_forget': 'delta_w', 'delta_w_out': 'delta_w', 'delta_g_xattn': 'delta_w', 'delta_g_mem': 'delta_w', 'delta_w_xq': 'delta_w', 'delta_w_xk': 'delta_w', 'delta_w_xv': 'delta_w', 'delta_w_xo': 'delta_w', 'delta_g_mlp': 'delta_w', 'delta_w_up': 'delta_w', 'delta_w_down': 'delta_w', 'delta_g_final': 'delta_w', 'new_m_g_mix': 'new_m', 'new_m_w_in': 'new_m', 'new_m_b_forget': 'new_m', 'new_m_w_out': 'new_m', 'new_m_g_xattn': 'new_m', 'new_m_g_mem': 'new_m', 'new_m_w_xq': 'new_m', 'new_m_w_xk': 'new_m', 'new_m_w_xv': 'new_m', 'new_m_w_xo': 'new_m', 'new_m_g_mlp': 'new_m', 'new_m_w_up': 'new_m', 'new_m_w_down': 'new_m', 'new_m_g_final': 'new_m', 'new_v_g_mix': 'new_v', 'new_v_w_in': 'new_v', 'new_v_b_forget': 'new_v', 'new_v_w_out': 'new_v', 'new_v_g_xattn': 'new_v', 'new_v_g_mem': 'new_v', 'new_v_w_xq': 'new_v', 'new_v_w_xk': 'new_v', 'new_v_w_xv': 'new_v', 'new_v_w_xo': 'new_v', 'new_v_g_mlp': 'new_v', 'new_v_w_up': 'new_v', 'new_v_w_down': 'new_v', 'new_v_g_final': 'new_v'}


def _forward(args):
    return _fwd_reference(*[args[k] for k in FWD_PARAMS])


def _output_shape():
    out = _jax.eval_shape(lambda: _forward(_fwd_setup_inputs(0)))
    return out.shape, out.dtype

N_MICROBATCH = 1
ADAM_LR = 0.001
ADAM_B1 = 0.9
ADAM_B2 = 0.999
ADAM_EPS = 1e-08
ADAM_WD = 0.01
ADAM_STEP = 10
PER_EXAMPLE_BATCH_AXIS = {'x': 0, 'mem': 0, 'loss_target': 0}
SHARED_INPUTS = []
_WEIGHT_DTYPES = {'g_mix': _jnp.float32, 'w_in': _jnp.float32, 'b_forget': _jnp.float32, 'w_out': _jnp.float32, 'g_xattn': _jnp.float32, 'g_mem': _jnp.float32, 'w_xq': _jnp.float32, 'w_xk': _jnp.float32, 'w_xv': _jnp.float32, 'w_xo': _jnp.float32, 'g_mlp': _jnp.float32, 'w_up': _jnp.float32, 'w_down': _jnp.float32, 'g_final': _jnp.float32}
MOMENT_SCALE = {'g_mix': 1.696494e-01, 'w_in': 9.913452e-02, 'b_forget': 8.738445e-01, 'w_out': 1.317166e-01, 'g_xattn': 2.653664e-02, 'g_mem': 4.019712e-02, 'w_xq': 2.509948e-02, 'w_xk': 2.514435e-02, 'w_xv': 3.031717e-02, 'w_xo': 2.990252e-02, 'g_mlp': 2.317749e-01, 'w_up': 1.089154e-01, 'w_down': 3.280071e-01, 'g_final': 6.431468e+01}


def _to_microbatches(a, axis):
    t = _jnp.moveaxis(a, axis, 0)
    t = t.reshape((N_MICROBATCH, t.shape[0] // N_MICROBATCH) + t.shape[1:])
    return _jnp.moveaxis(t, 1, axis + 1)


def setup_inputs(seed: int = 0) -> dict:
    inp = _fwd_setup_inputs(seed)
    key = _jax.random.fold_in(_jax.random.key(seed), 7919)
    shape, _ = _output_shape()
    out = dict(inp)
    out["loss_target"] = _jax.random.normal(_jax.random.fold_in(key, 0), shape, _jnp.float32)
    for i, name in enumerate(TWIN_WEIGHTS):
        w = inp[name].astype(_jnp.float32)
        if MOMENT_SCALE is None:
            s = _jnp.sqrt(_jnp.mean(_jnp.square(w)) + 1e-30)
        else:
            s = MOMENT_SCALE[name]
        km, kv = _jax.random.split(_jax.random.fold_in(key, i + 1))
        out[name] = w
        out["m_" + name] = s * _jax.random.normal(km, w.shape, _jnp.float32)
        out["v_" + name] = (s * s) * _jax.random.uniform(kv, w.shape, _jnp.float32, 0.5, 1.5)
    if N_MICROBATCH > 1:
        for name, axis in PER_EXAMPLE_BATCH_AXIS.items():
            out[name] = _to_microbatches(out[name], axis)
    return {'x': out['x'], 'mem': out['mem'], 'g_mix': out['g_mix'], 'w_in': out['w_in'], 'b_forget': out['b_forget'], 'w_out': out['w_out'], 'g_xattn': out['g_xattn'], 'g_mem': out['g_mem'], 'w_xq': out['w_xq'], 'w_xk': out['w_xk'], 'w_xv': out['w_xv'], 'w_xo': out['w_xo'], 'g_mlp': out['g_mlp'], 'w_up': out['w_up'], 'w_down': out['w_down'], 'g_final': out['g_final'], 'loss_target': out['loss_target'], 'm_g_mix': out['m_g_mix'], 'm_w_in': out['m_w_in'], 'm_b_forget': out['m_b_forget'], 'm_w_out': out['m_w_out'], 'm_g_xattn': out['m_g_xattn'], 'm_g_mem': out['m_g_mem'], 'm_w_xq': out['m_w_xq'], 'm_w_xk': out['m_w_xk'], 'm_w_xv': out['m_w_xv'], 'm_w_xo': out['m_w_xo'], 'm_g_mlp': out['m_g_mlp'], 'm_w_up': out['m_w_up'], 'm_w_down': out['m_w_down'], 'm_g_final': out['m_g_final'], 'v_g_mix': out['v_g_mix'], 'v_w_in': out['v_w_in'], 'v_b_forget': out['v_b_forget'], 'v_w_out': out['v_w_out'], 'v_g_xattn': out['v_g_xattn'], 'v_g_mem': out['v_g_mem'], 'v_w_xq': out['v_w_xq'], 'v_w_xk': out['v_w_xk'], 'v_w_xv': out['v_w_xv'], 'v_w_xo': out['v_w_xo'], 'v_g_mlp': out['v_g_mlp'], 'v_w_up': out['v_w_up'], 'v_w_down': out['v_w_down'], 'v_g_final': out['v_g_final']}


def _loss(weights, diff, rest, loss_target):
    with _jax.named_scope("forward"):
        args = {**rest, TWIN_DIFF_INPUT: diff, **{k: w.astype(_WEIGHT_DTYPES[k]) for k, w in weights.items()}}
        y = _forward(args)
    with _jax.named_scope("loss_head"):
        err = _jnp.square(y.astype(_jnp.float32) - loss_target)
        return 0.5 * _jnp.sum(_jnp.mean(err, axis=-1)) if err.ndim else 0.5 * err


def _adamw(w, g, m, v):
    m = ADAM_B1 * m + (1.0 - ADAM_B1) * g
    v = ADAM_B2 * v + (1.0 - ADAM_B2) * _jnp.square(g)
    m_hat = m / (1.0 - ADAM_B1 ** ADAM_STEP)
    v_hat = v / (1.0 - ADAM_B2 ** ADAM_STEP)
    delta = -ADAM_LR * (m_hat / (_jnp.sqrt(v_hat) + ADAM_EPS) + ADAM_WD * w)
    return delta, m, v


def reference(x, mem, g_mix, w_in, b_forget, w_out, g_xattn, g_mem, w_xq, w_xk, w_xv, w_xo, g_mlp, w_up, w_down, g_final, loss_target, m_g_mix, m_w_in, m_b_forget, m_w_out, m_g_xattn, m_g_mem, m_w_xq, m_w_xk, m_w_xv, m_w_xo, m_g_mlp, m_w_up, m_w_down, m_g_final, v_g_mix, v_w_in, v_b_forget, v_w_out, v_g_xattn, v_g_mem, v_w_xq, v_w_xk, v_w_xv, v_w_xo, v_g_mlp, v_w_up, v_w_down, v_g_final):
    given = dict(x=x, mem=mem, g_mix=g_mix, w_in=w_in, b_forget=b_forget, w_out=w_out, g_xattn=g_xattn, g_mem=g_mem, w_xq=w_xq, w_xk=w_xk, w_xv=w_xv, w_xo=w_xo, g_mlp=g_mlp, w_up=w_up, w_down=w_down, g_final=g_final, loss_target=loss_target, m_g_mix=m_g_mix, m_w_in=m_w_in, m_b_forget=m_b_forget, m_w_out=m_w_out, m_g_xattn=m_g_xattn, m_g_mem=m_g_mem, m_w_xq=m_w_xq, m_w_xk=m_w_xk, m_w_xv=m_w_xv, m_w_xo=m_w_xo, m_g_mlp=m_g_mlp, m_w_up=m_w_up, m_w_down=m_w_down, m_g_final=m_g_final, v_g_mix=v_g_mix, v_w_in=v_w_in, v_b_forget=v_b_forget, v_w_out=v_w_out, v_g_xattn=v_g_xattn, v_g_mem=v_g_mem, v_w_xq=v_w_xq, v_w_xk=v_w_xk, v_w_xv=v_w_xv, v_w_xo=v_w_xo, v_g_mlp=v_g_mlp, v_w_up=v_w_up, v_w_down=v_w_down, v_g_final=v_g_final)
    weights = {n: given[n] for n in TWIN_WEIGHTS}
    shared = {n: given[n] for n in SHARED_INPUTS}
    per_example = {n: given[n] for n in ['x', 'mem']}
    grad_fn = _jax.value_and_grad(_loss, argnums=(0, 1))

    def one_microbatch(ex, loss_target):
        ex = dict(ex)
        diff = ex.pop(TWIN_DIFF_INPUT)
        return grad_fn(weights, diff, {**shared, **ex}, loss_target)

    if N_MICROBATCH == 1:
        loss, (grad_w, grad_x) = one_microbatch(per_example, given["loss_target"])
    else:
        def body(carry, xs):
            loss_sum, grad_sum = carry
            l_k, (gw_k, gx_k) = one_microbatch(xs[0], xs[1])
            with _jax.named_scope("update"):
                return (loss_sum + l_k, _jax.tree.map(_jnp.add, grad_sum, gw_k)), gx_k

        init = (_jnp.zeros((), _jnp.float32), _jax.tree.map(_jnp.zeros_like, weights))
        (loss, grad_w), grad_x = _jax.lax.scan(body, init, (per_example, given["loss_target"]))
    with _jax.named_scope("update"):
        delta_w, new_m, new_v = {}, {}, {}
        for n in TWIN_WEIGHTS:
            delta_w[n], new_m[n], new_v[n] = _adamw(weights[n], grad_w[n], given["m_" + n], given["v_" + n])
    return (loss, grad_x, *[grad_w[n] for n in TWIN_WEIGHTS], *[delta_w[n] for n in TWIN_WEIGHTS],
            *[new_m[n] for n in TWIN_WEIGHTS], *[new_v[n] for n in TWIN_WEIGHTS])
```

```python
import functools
import math

import jax
import jax.numpy as jnp
from jax import lax
from jax.experimental import pallas as pl
from jax.experimental.pallas import tpu as pltpu

F32 = jnp.float32
BF16 = jnp.bfloat16

D_MODEL = 1024
SEQ = 2048
N_MEM = 256
HEAD_DIM = 64
N_HEADS = 8
MIX_HALF = N_HEADS * HEAD_DIM
QKV_WIDTH = 6 * MIX_HALF
IN_WIDTH = QKV_WIDTH + N_HEADS
GATE_PAD = 128
BLOCK = 128
DILATIONS = (1, 4, 16)
X_HEADS = 4
X_HEAD_DIM = 256
D_FF = 4096
EPS = 1e-6
NEG = -1e30
ATT_SCALE = 1.0 / math.sqrt(HEAD_DIM)
X_SCALE = 1.0 / math.sqrt(X_HEAD_DIM)
LANES = 128
N_CHIPS = 4
N_DEV = 8

ADAM_LR = 0.001
ADAM_B1 = 0.9
ADAM_B2 = 0.999
ADAM_EPS = 1e-08
ADAM_WD = 0.01
ADAM_STEP = 10

VMEM_LIMIT = 48 * 1024 * 1024

PACK_SEGS = (("w_in", 770), ("w_out", 256), ("w_xq", 256), ("w_xk", 256), ("w_xv", 256),
             ("w_xo", 256), ("w_up", 1024), ("w_down", 1024))
PACK_ROWS = 4128
PACK_HALF = PACK_ROWS // 2


def _params(*sem):
    return pltpu.CompilerParams(dimension_semantics=sem or None, vmem_limit_bytes=VMEM_LIMIT)


def _dot(a, b):
    return jnp.dot(a, b, preferred_element_type=F32)


def _dot_nt(a, b):
    return lax.dot_general(a, b, (((1,), (1,)), ((), ())), preferred_element_type=F32)


def _dot_tn(a, b):
    return lax.dot_general(a, b, (((0,), (0,)), ((), ())), preferred_element_type=F32)


def _dot_exact(x, e):
    hi = x.astype(BF16)
    r1 = x - hi.astype(F32)
    mid = r1.astype(BF16)
    lo = (r1 - mid.astype(F32)).astype(BF16)
    return _dot(hi, e) + _dot(mid, e) + _dot(lo, e)


def _head_mask(e):
    lane = lax.broadcasted_iota(jnp.int32, (1, LANES), 1)
    return (lane >= HEAD_DIM * e) & (lane < HEAD_DIM * (e + 1))


def _matmul(a, w, name, out_dtypes=(F32,), extras=(), epilogue=None, tm=512, tn=512):
    m, k = a.shape
    _, n = w.shape
    tm, tn = min(tm, m), min(tn, n)
    assert m % tm == 0 and n % tn == 0, (name, a.shape, w.shape)
    n_ex = len(extras)

    def body(a_ref, w_ref, *rest):
        acc = _dot(a_ref[...], w_ref[...])
        res = (acc,) if epilogue is None else epilogue(acc, *[r[...] for r in rest[:n_ex]])
        for o_ref, r in zip(rest[n_ex:], res):
            o_ref[...] = r.astype(o_ref.dtype)

    tile = pl.BlockSpec((tm, tn), lambda i, j: (i, j))
    return pl.pallas_call(
        body, grid=(m // tm, n // tn),
        in_specs=[pl.BlockSpec((tm, k), lambda i, j: (i, 0)), pl.BlockSpec((k, tn), lambda i, j: (0, j))] + [tile] * n_ex,
        out_specs=[tile] * len(out_dtypes),
        out_shape=[jax.ShapeDtypeStruct((m, n), dt) for dt in out_dtypes],
        name=name, compiler_params=_params("parallel", "arbitrary"),
    )(a, w, *extras)


def _matmul_res(a, w, res, name):
    return _matmul(a, w, name, extras=(res,), epilogue=lambda acc, r: (r + acc,))[0]


def _matmul_tn(x, y, name, tm=1024, tn=1024, tk=512):
    t, m = x.shape
    _, n = y.shape
    tm, tn, tk = min(tm, m), min(tn, n), min(tk, t)
    assert m % tm == 0 and n % tn == 0 and t % tk == 0, (name, x.shape, y.shape)

    def body(x_ref, y_ref, o_ref):
        @pl.when(pl.program_id(2) == 0)
        def _():
            o_ref[...] = jnp.zeros_like(o_ref)

        o_ref[...] += _dot_tn(x_ref[...], y_ref[...])

    return pl.pallas_call(
        body, grid=(m // tm, n // tn, t // tk),
        in_specs=[pl.BlockSpec((tk, tm), lambda i, j, k: (k, i)), pl.BlockSpec((tk, tn), lambda i, j, k: (k, j))],
        out_specs=pl.BlockSpec((tm, tn), lambda i, j, k: (i, j)),
        out_shape=jax.ShapeDtypeStruct((m, n), F32),
        name=name, compiler_params=_params("parallel", "parallel", "arbitrary"),
    )(x, y)


def _rmsnorm(x, g, name, tm=512):
    t, d = x.shape
    tm = min(tm, t)

    def body(x_ref, g_ref, h_ref):
        xv = x_ref[...]
        r = lax.rsqrt(jnp.mean(xv * xv, axis=-1, keepdims=True) + EPS)
        h_ref[...] = (xv * r * g_ref[...]).astype(BF16)

    return pl.pallas_call(
        body, grid=(t // tm,),
        in_specs=[pl.BlockSpec((tm, d), lambda i: (i, 0)), pl.BlockSpec((1, d), lambda i: (0, 0))],
        out_specs=pl.BlockSpec((tm, d), lambda i: (i, 0)),
        out_shape=jax.ShapeDtypeStruct((t, d), BF16),
        name=name, compiler_params=_params("arbitrary"),
    )(x, g.reshape(1, d))


def _rms_bwd_tile(xv, dh, g):
    d = xv.shape[-1]
    r = lax.rsqrt(jnp.mean(xv * xv, axis=-1, keepdims=True) + EPS)
    dyg = dh * g
    proj = jnp.sum(dyg * xv, axis=-1, keepdims=True)
    dx = r * dyg - xv * (r * r * r * (1.0 / d)) * proj
    return dx, dh * (xv * r)


def _rms_bwd(x, dh, g, dres, name, tm=512):
    t, d = x.shape
    tm = min(tm, t)
    has_res = dres is not None

    def body(x_ref, dh_ref, g_ref, *rest):
        if has_res:
            res_ref, dx_ref, dxb_ref, dg_ref = rest
        else:
            dx_ref, dxb_ref, dg_ref = rest
        dx, dg_rows = _rms_bwd_tile(x_ref[...], dh_ref[...], g_ref[...])
        if has_res:
            dx = res_ref[...] + dx
        dx_ref[...] = dx
        dxb_ref[...] = dx.astype(BF16)

        @pl.when(pl.program_id(0) == 0)
        def _():
            dg_ref[...] = jnp.zeros_like(dg_ref)

        dg_ref[...] += jnp.sum(dg_rows, axis=0, keepdims=True)

    row = pl.BlockSpec((tm, d), lambda i: (i, 0))
    vec = pl.BlockSpec((1, d), lambda i: (0, 0))
    return pl.pallas_call(
        body, grid=(t // tm,),
        in_specs=[row, row, vec] + ([row] if has_res else []),
        out_specs=[row, row, vec],
        out_shape=[jax.ShapeDtypeStruct((t, d), F32), jax.ShapeDtypeStruct((t, d), BF16), jax.ShapeDtypeStruct((1, d), F32)],
        name=name, compiler_params=_params("arbitrary"),
    )(x, dh, g.reshape(1, d), *((dres,) if has_res else ()))


def _loss_bwd(x, g, target, name, tm=512):
    t, d = x.shape

    def body(x_ref, g_ref, t_ref, loss_ref, dx_ref, dxb_ref, dg_ref):
        xv = x_ref[...]
        gv = g_ref[...]
        r = lax.rsqrt(jnp.mean(xv * xv, axis=-1, keepdims=True) + EPS)
        err = xv * r * gv - t_ref[...]
        dx, dg_rows = _rms_bwd_tile(xv, err * (1.0 / d), gv)
        dx_ref[...] = dx
        dxb_ref[...] = dx.astype(BF16)

        @pl.when(pl.program_id(0) == 0)
        def _():
            dg_ref[...] = jnp.zeros_like(dg_ref)
            loss_ref[...] = jnp.zeros_like(loss_ref)

        dg_ref[...] += jnp.sum(dg_rows, axis=0, keepdims=True)
        part = jnp.sum(jnp.sum(err * err, axis=0, keepdims=True), axis=1, keepdims=True) * (0.5 / d)
        loss_ref[...] += jnp.broadcast_to(part, loss_ref.shape)

    row = pl.BlockSpec((tm, d), lambda i: (i, 0))
    vec = pl.BlockSpec((1, d), lambda i: (0, 0))
    return pl.pallas_call(
        body, grid=(t // tm,),
        in_specs=[row, vec, row],
        out_specs=[pl.BlockSpec((1, LANES), lambda i: (0, 0)), row, row, vec],
        out_shape=[jax.ShapeDtypeStruct((1, LANES), F32), jax.ShapeDtypeStruct((t, d), F32),
                   jax.ShapeDtypeStruct((t, d), BF16), jax.ShapeDtypeStruct((1, d), F32)],
        name=name, compiler_params=_params("arbitrary"),
    )(x, g.reshape(1, d), target)


def _tri(upper):
    r = lax.broadcasted_iota(jnp.int32, (LANES, LANES), 0)
    c = lax.broadcasted_iota(jnp.int32, (LANES, LANES), 1)
    return jnp.where((r <= c) if upper else (r >= c), 1.0, 0.0).astype(BF16)


def _gate_fwd(gate, b_pad, n_batch, name):
    s = SEQ
    nblk = s // LANES

    def body(g_ref, b_ref, cbc_ref, crow_ref, sg_ref, ct_ref):
        gz = g_ref[...] + b_ref[...]
        logf = jnp.minimum(gz, 0.0) - jnp.log(1.0 + jnp.exp(-jnp.abs(gz)))
        logf_t = logf.T
        sg_ref[...] = (1.0 / (1.0 + jnp.exp(gz))).T[0:N_HEADS]
        upper = _tri(True)
        carry = jnp.zeros((LANES, 1), F32)
        for blk in range(nblk):
            seg = _dot_exact(logf_t[:, blk * LANES:(blk + 1) * LANES], upper) + carry
            carry = seg[:, LANES - 1:LANES]
            ct_ref[:, blk * LANES:(blk + 1) * LANES] = seg
        ct = ct_ref[...]
        crow_ref[...] = ct[0:N_HEADS]
        c_col = ct.T
        lane = lax.broadcasted_iota(jnp.int32, (1, MIX_HALF), 1)
        acc = jnp.zeros((s, MIX_HALF), F32)
        for h in range(N_HEADS):
            acc = jnp.where((lane >= HEAD_DIM * h) & (lane < HEAD_DIM * (h + 1)), c_col[:, h:h + 1], acc)
        cbc_ref[...] = acc

    return pl.pallas_call(
        body, grid=(n_batch,),
        in_specs=[pl.BlockSpec((s, GATE_PAD), lambda b: (b, 0)), pl.BlockSpec((1, GATE_PAD), lambda b: (0, 0))],
        out_specs=[pl.BlockSpec((s, MIX_HALF), lambda b: (b, 0)),
                   pl.BlockSpec((None, N_HEADS, s), lambda b: (b, 0, 0)),
                   pl.BlockSpec((None, N_HEADS, s), lambda b: (b, 0, 0))],
        out_shape=[jax.ShapeDtypeStruct((n_batch * s, MIX_HALF), F32),
                   jax.ShapeDtypeStruct((n_batch, N_HEADS, s), F32),
                   jax.ShapeDtypeStruct((n_batch, N_HEADS, s), F32)],
        scratch_shapes=[pltpu.VMEM((LANES, s), F32)],
        name=name, compiler_params=_params("arbitrary"),
    )(gate, b_pad)


def _gate_bwd(dc, sg, name):
    n_batch, _, s = dc.shape
    nblk = s // LANES

    def body(dc_ref, sg_ref, dz_ref, db_ref, dt_ref):
        lower = _tri(False)
        dcv = dc_ref[...]
        carry = jnp.zeros((N_HEADS, 1), F32)
        dt_ref[...] = jnp.zeros_like(dt_ref)
        for blk in reversed(range(nblk)):
            seg = _dot_exact(dcv[:, blk * LANES:(blk + 1) * LANES], lower) + carry
            carry = seg[:, 0:1]
            dt_ref[0:N_HEADS, blk * LANES:(blk + 1) * LANES] = seg * sg_ref[:, blk * LANES:(blk + 1) * LANES]
        dg_t = dt_ref[...]
        dz_ref[...] = dg_t.T.astype(BF16)

        @pl.when(pl.program_id(0) == 0)
        def _():
            db_ref[...] = jnp.zeros_like(db_ref)

        db_ref[...] += jnp.broadcast_to(jnp.sum(dg_t[0:N_HEADS], axis=1, keepdims=True), db_ref.shape)

    return pl.pallas_call(
        body, grid=(n_batch,),
        in_specs=[pl.BlockSpec((None, N_HEADS, s), lambda b: (b, 0, 0)), pl.BlockSpec((None, N_HEADS, s), lambda b: (b, 0, 0))],
        out_specs=[pl.BlockSpec((s, GATE_PAD), lambda b: (b, 0)), pl.BlockSpec((N_HEADS, LANES), lambda b: (0, 0))],
        out_shape=[jax.ShapeDtypeStruct((n_batch * s, GATE_PAD), BF16), jax.ShapeDtypeStruct((N_HEADS, LANES), F32)],
        scratch_shapes=[pltpu.VMEM((LANES, s), F32)],
        name=name, compiler_params=_params("arbitrary"),
    )(dc, sg)


FOX_BQ = 256
FOX_BK = 256
QF_COL, KF_COL, VF_COL = 12, 16, 20


def _causal(i, j, bq, bk):
    qpos = i * bq + lax.broadcasted_iota(jnp.int32, (bq, 1), 0)
    kpos = j * bk + lax.broadcasted_iota(jnp.int32, (1, bk), 1)
    return kpos <= qpos


def _fox_fwd(zqkv, c_bc, c_row, n_batch, name):
    s, bq, bk = SEQ, FOX_BQ, FOX_BK
    nq = s // bq
    t = n_batch * s

    def body(q_ref, k_ref, v_ref, cq_ref, cr_ref, o_ref, lse_ref):
        hp, i = pl.program_id(1), pl.program_id(2)
        q = q_ref[...]
        outs, lses = [], []
        for e in range(2):
            qh = jnp.where(_head_mask(e), q, jnp.zeros_like(q))
            cq = cq_ref[:, HEAD_DIM * e:HEAD_DIM * e + 1]
            h = 2 * hp + e

            def step(j, carry, qh=qh, cq=cq, h=h):
                m, l, acc = carry
                rows = pl.ds(pl.multiple_of(j * bk, bk), bk)
                sc = _dot_nt(qh, k_ref[rows, :]) * ATT_SCALE + (cq - cr_ref[pl.ds(h, 1), rows])
                sc = jnp.where(_causal(i, j, bq, bk), sc, NEG)
                m_new = jnp.maximum(m, jnp.max(sc, axis=1, keepdims=True))
                alpha = jnp.exp(m - m_new)
                p = jnp.exp(sc - m_new)
                l = alpha * l + jnp.sum(p, axis=1, keepdims=True)
                acc = alpha * acc + _dot(p.astype(BF16), v_ref[rows, :])
                return m_new, l, acc

            init = (jnp.full((bq, 1), NEG, F32), jnp.zeros((bq, 1), F32), jnp.zeros((bq, LANES), F32))
            m, l, acc = lax.fori_loop(0, (i * bq + bq + bk - 1) // bk, step, init)
            outs.append(acc / l)
            lses.append(m + jnp.log(l))
        o_ref[...] = jnp.where(_head_mask(0), outs[0], outs[1]).astype(BF16)
        lse_ref[...] = jnp.where(_head_mask(0), lses[0], lses[1])

    def col(c0):
        return lambda b, hp, i: (b, c0 + hp)

    blk = lambda b, hp, i: (b * nq + i, hp)
    return pl.pallas_call(
        body, grid=(n_batch, 4, nq),
        in_specs=[pl.BlockSpec((bq, LANES), lambda b, hp, i: (b * nq + i, QF_COL + hp)),
                  pl.BlockSpec((s, LANES), col(KF_COL)), pl.BlockSpec((s, LANES), col(VF_COL)),
                  pl.BlockSpec((bq, LANES), blk),
                  pl.BlockSpec((None, N_HEADS, s), lambda b, hp, i: (b, 0, 0))],
        out_specs=[pl.BlockSpec((bq, LANES), blk), pl.BlockSpec((bq, LANES), blk)],
        out_shape=[jax.ShapeDtypeStruct((t, MIX_HALF), BF16), jax.ShapeDtypeStruct((t, MIX_HALF), F32)],
        name=name, compiler_params=_params("parallel", "parallel", "arbitrary"),
    )(zqkv, zqkv, zqkv, c_bc, c_row)


def _fox_delta(zqkv, dy, lse, c_bc, c_row, n_batch, name):
    s, bq, bk = SEQ, FOX_BQ, FOX_BK
    nq = s // bq
    t = n_batch * s

    def body(q_ref, k_ref, v_ref, do_ref, lse_ref, cq_ref, cr_ref, d_ref):
        hp, i = pl.program_id(1), pl.program_id(2)
        qi, doi = q_ref[...], do_ref[...]
        deltas = []
        for e in range(2):
            mask = _head_mask(e)
            dom = jnp.where(mask, doi, jnp.zeros_like(doi))
            lse_i = lse_ref[:, HEAD_DIM * e:HEAD_DIM * e + 1]
            cq = cq_ref[:, HEAD_DIM * e:HEAD_DIM * e + 1]
            h = 2 * hp + e

            def step(j, acc, mask=mask, dom=dom, lse_i=lse_i, cq=cq, h=h):
                rows = pl.ds(pl.multiple_of(j * bk, bk), bk)
                kj = k_ref[rows, :]
                km = jnp.where(mask, kj, jnp.zeros_like(kj))
                sc = _dot_nt(qi, km) * ATT_SCALE + (cq - cr_ref[pl.ds(h, 1), rows])
                sc = jnp.where(_causal(i, j, bq, bk), sc, NEG)
                p = jnp.exp(sc - lse_i)
                return acc + jnp.sum(p * _dot_nt(dom, v_ref[rows, :]), axis=1, keepdims=True)

            deltas.append(lax.fori_loop(0, (i * bq + bq + bk - 1) // bk, step, jnp.zeros((bq, 1), F32)))
        d_ref[...] = jnp.where(_head_mask(0), deltas[0], deltas[1])

    def col(c0):
        return lambda b, hp, i: (b, c0 + hp)

    def qblk(c0):
        return lambda b, hp, i: (b * nq + i, c0 + hp)

    return pl.pallas_call(
        body, grid=(n_batch, 4, nq),
        in_specs=[pl.BlockSpec((bq, LANES), qblk(QF_COL)), pl.BlockSpec((s, LANES), col(KF_COL)),
                  pl.BlockSpec((s, LANES), col(VF_COL)), pl.BlockSpec((bq, LANES), qblk(4)),
                  pl.BlockSpec((bq, LANES), qblk(0)), pl.BlockSpec((bq, LANES), qblk(0)),
                  pl.BlockSpec((None, N_HEADS, s), lambda b, hp, i: (b, 0, 0))],
        out_specs=pl.BlockSpec((bq, LANES), qblk(0)),
        out_shape=jax.ShapeDtypeStruct((t, MIX_HALF), F32),
        name=name, compiler_params=_params("parallel", "parallel", "arbitrary"),
    )(zqkv, zqkv, zqkv, dy, lse, c_bc, c_row)


def _fox_bwd(zqkv, delta, dy, lse, c_bc, c_row, n_batch, name):
    s, bq, bk = SEQ, FOX_BQ, FOX_BK
    nq, nk = s // bq, s // bk
    t = n_batch * s

    def body(q_ref, k_ref, v_ref, dl_ref, do_ref, lse_ref, cq_ref, cr_ref, dq_ref, dk_ref, dv_ref, dc_ref, dq_acc):
        hp, j = pl.program_id(1), pl.program_id(2)

        @pl.when(j == 0)
        def _():
            dq_acc[...] = jnp.zeros_like(dq_acc)

        kj, vj = k_ref[...], v_ref[...]
        dks, dvs = [], []
        for e in range(2):
            mask = _head_mask(e)
            km = jnp.where(mask, kj, jnp.zeros_like(kj))
            ck = cr_ref[pl.ds(2 * hp + e, 1), pl.ds(pl.multiple_of(j * bk, bk), bk)]

            def step(i, carry, mask=mask, km=km, ck=ck, e=e):
                dk_a, dv_a, dc_a = carry
                rows = pl.ds(pl.multiple_of(i * bq, bq), bq)
                qi, doi = q_ref[rows, :], do_ref[rows, :]
                dom = jnp.where(mask, doi, jnp.zeros_like(doi))
                delta = dl_ref[rows, HEAD_DIM * e:HEAD_DIM * e + 1]
                lse_i = lse_ref[rows, HEAD_DIM * e:HEAD_DIM * e + 1]
                cq = cq_ref[rows, HEAD_DIM * e:HEAD_DIM * e + 1]
                sc = _dot_nt(qi, km) * ATT_SCALE + (cq - ck)
                sc = jnp.where(_causal(i, j, bq, bk), sc, NEG)
                p = jnp.exp(sc - lse_i)
                ds = p * (_dot_nt(dom, vj) - delta)
                dsb = ds.astype(BF16)
                dq_acc[rows, :] += _dot(dsb, km) * ATT_SCALE
                dk_a = dk_a + _dot_tn(dsb, qi) * ATT_SCALE
                dv_a = dv_a + _dot_tn(p.astype(BF16), dom)
                dc_a = dc_a - jnp.sum(ds, axis=0, keepdims=True)
                return dk_a, dv_a, dc_a

            init = (jnp.zeros((bk, LANES), F32), jnp.zeros((bk, LANES), F32), jnp.zeros((1, bk), F32))
            dk_e, dv_e, dc_e = lax.fori_loop((j * bk) // bq, nq, step, init)
            dks.append(dk_e)
            dvs.append(dv_e)
            dc_ref[e:e + 1, :] = dc_e
        dk_ref[...] = jnp.where(_head_mask(0), dks[0], dks[1]).astype(BF16)
        dv_ref[...] = (dvs[0] + dvs[1]).astype(BF16)

        @pl.when(j == nk - 1)
        def _():
            dq_ref[...] = dq_acc[...].astype(BF16)

    def seq(c0):
        return lambda b, hp, j: (b, c0 + hp)

    def kblk(c0):
        return lambda b, hp, j: (b * nk + j, c0 + hp)

    return pl.pallas_call(
        body, grid=(n_batch, 4, nk),
        in_specs=[pl.BlockSpec((s, LANES), seq(QF_COL)), pl.BlockSpec((bk, LANES), kblk(KF_COL)),
                  pl.BlockSpec((bk, LANES), kblk(VF_COL)),
                  pl.BlockSpec((s, LANES), seq(0)), pl.BlockSpec((s, LANES), seq(4)),
                  pl.BlockSpec((s, LANES), seq(0)), pl.BlockSpec((s, LANES), seq(0)),
                  pl.BlockSpec((None, N_HEADS, s), lambda b, hp, j: (b, 0, 0))],
        out_specs=[pl.BlockSpec((s, LANES), seq(0)), pl.BlockSpec((bk, LANES), kblk(0)), pl.BlockSpec((bk, LANES), kblk(0)),
                   pl.BlockSpec((None, None, 2, bk), lambda b, hp, j: (b, hp, 0, j))],
        out_shape=[jax.ShapeDtypeStruct((t, MIX_HALF), BF16)] * 3 + [jax.ShapeDtypeStruct((n_batch, 4, 2, s), F32)],
        scratch_shapes=[pltpu.VMEM((s, LANES), F32)],
        name=name, compiler_params=_params("parallel", "parallel", "arbitrary"),
    )(zqkv, zqkv, zqkv, delta, dy, lse, c_bc, c_row)


def _dil_bias(slope, dil, first):
    nk = BLOCK if first else 2 * BLOCK
    qi = lax.broadcasted_iota(jnp.int32, (BLOCK, nk), 0)
    kj = lax.broadcasted_iota(jnp.int32, (BLOCK, nk), 1)
    delta = qi - kj if first else qi + BLOCK - kj
    valid = (delta >= 0) & (delta <= BLOCK)
    return jnp.where(valid, (-slope * dil) * delta.astype(F32), NEG)


def _dil_width(dil):
    return LANES if dil == 1 else MIX_HALF


def _dil_head_pairs(width):
    n_local = width // LANES
    first = 0 if width == MIX_HALF else pl.program_id(2) * n_local
    return [(slice(k * LANES, (k + 1) * LANES), first + k) for k in range(n_local)]


def _alibi_slope(hp, e):
    if isinstance(hp, int):
        return 2.0 ** -(2 * hp + e + 1)
    slope = jnp.float32(0.0)
    for k in range(N_HEADS // 2):
        slope = jnp.where(hp == k, jnp.float32(2.0 ** -(2 * k + e + 1)), slope)
    return slope


def _dil_fwd(zqkv, dil, n_batch, name):
    seq_l = SEQ // dil
    nb = seq_l // BLOCK
    width = _dil_width(dil)
    zv = zqkv.reshape(n_batch * seq_l, dil * QKV_WIDTH)

    def body(q_ref, k_ref, v_ref, o_ref, lse_ref):
        for cols, hp in _dil_head_pairs(width):

            def block(n, first, cols=cols, hp=hp):
                q0 = pl.multiple_of(n * BLOCK, BLOCK)
                qn = q_ref[pl.ds(q0, BLOCK), cols]
                krows = pl.ds(q0, BLOCK) if first else pl.ds(pl.multiple_of(q0 - BLOCK, BLOCK), 2 * BLOCK)
                kc, vc = k_ref[krows, cols], v_ref[krows, cols]
                outs, lses = [], []
                for e in range(2):
                    slope = _alibi_slope(hp, e)
                    qh = jnp.where(_head_mask(e), qn, jnp.zeros_like(qn))
                    sc = _dot_nt(qh, kc) * ATT_SCALE + _dil_bias(slope, dil, first)
                    m = jnp.max(sc, axis=1, keepdims=True)
                    pe = jnp.exp(sc - m)
                    l = jnp.sum(pe, axis=1, keepdims=True)
                    outs.append(_dot((pe / l).astype(BF16), vc))
                    lses.append(m + jnp.log(l))
                o_ref[pl.ds(q0, BLOCK), cols] = jnp.where(_head_mask(0), outs[0], outs[1])
                lse_ref[pl.ds(q0, BLOCK), cols] = jnp.where(_head_mask(0), lses[0], lses[1])

            block(0, True)
            if nb > 1:
                def loop_body(n, carry):
                    block(n, False)
                    return carry

                lax.fori_loop(1, nb, loop_body, 0)

    per = MIX_HALF // width

    def col(c0):
        return lambda b, r, w: (b, (6 * r + c0) * per + w)

    out_blk = pl.BlockSpec((seq_l, width), lambda b, r, w: (b, r * per + w))
    o, lse = pl.pallas_call(
        body, grid=(n_batch, dil, per),
        in_specs=[pl.BlockSpec((seq_l, width), col(0)), pl.BlockSpec((seq_l, width), col(1)),
                  pl.BlockSpec((seq_l, width), col(2))],
        out_specs=[out_blk, out_blk],
        out_shape=[jax.ShapeDtypeStruct((n_batch * seq_l, dil * MIX_HALF), F32)] * 2,
        name=name, compiler_params=_params("parallel", "arbitrary", "arbitrary"),
    )(zv, zv, zv)
    t = n_batch * SEQ
    return o.reshape(t, MIX_HALF), lse.reshape(t, MIX_HALF)


def _dil_bwd(zqkv, do, dl, lse, dil, n_batch, name):
    seq_l = SEQ // dil
    nb = seq_l // BLOCK
    width = _dil_width(dil)
    zv = zqkv.reshape(n_batch * seq_l, dil * QKV_WIDTH)
    view = lambda a: a.reshape(n_batch * seq_l, dil * MIX_HALF)

    def body(q_ref, k_ref, v_ref, do_ref, dl_ref, lse_ref, dq_ref, dk_ref, dv_ref):
        dk_ref[...] = jnp.zeros_like(dk_ref)
        dv_ref[...] = jnp.zeros_like(dv_ref)
        for cols, hp in _dil_head_pairs(width):

            def block(n, first, cols=cols, hp=hp):
                q0 = pl.multiple_of(n * BLOCK, BLOCK)
                qrows = pl.ds(q0, BLOCK)
                krows = qrows if first else pl.ds(pl.multiple_of(q0 - BLOCK, BLOCK), 2 * BLOCK)
                qn, don = q_ref[qrows, cols], do_ref[qrows, cols]
                kc, vc = k_ref[krows, cols], v_ref[krows, cols]
                dq = jnp.zeros((BLOCK, LANES), F32)
                for e in range(2):
                    mask = _head_mask(e)
                    slope = _alibi_slope(hp, e)
                    qh = jnp.where(mask, qn, jnp.zeros_like(qn))
                    doh = jnp.where(mask, don, jnp.zeros_like(don))
                    kh = jnp.where(mask, kc, jnp.zeros_like(kc))
                    lane0 = HEAD_DIM * e
                    sc = _dot_nt(qh, kc) * ATT_SCALE + _dil_bias(slope, dil, first)
                    p = jnp.exp(sc - lse_ref[qrows, cols][:, lane0:lane0 + 1])
                    ds = p * (_dot_nt(doh, vc) - dl_ref[qrows, cols][:, lane0:lane0 + 1])
                    dsb = ds.astype(BF16)
                    dq = dq + _dot(dsb, kh) * ATT_SCALE
                    dk_ref[krows, cols] += _dot_tn(dsb, qh) * ATT_SCALE
                    dv_ref[krows, cols] += _dot_tn(p.astype(BF16), doh)
                dq_ref[qrows, cols] = dq

            block(0, True)
            if nb > 1:
                def loop_body(n, carry):
                    block(n, False)
                    return carry

                lax.fori_loop(1, nb, loop_body, 0)

    per = MIX_HALF // width

    def col(c0):
        return lambda b, r, w: (b, (6 * r + c0) * per + w)

    blk = pl.BlockSpec((seq_l, width), lambda b, r, w: (b, r * per + w))
    outs = pl.pallas_call(
        body, grid=(n_batch, dil, per),
        in_specs=[pl.BlockSpec((seq_l, width), col(0)), pl.BlockSpec((seq_l, width), col(1)),
                  pl.BlockSpec((seq_l, width), col(2)), blk, blk, blk],
        out_specs=[blk, blk, blk],
        out_shape=[jax.ShapeDtypeStruct((n_batch * seq_l, dil * MIX_HALF), F32)] * 3,
        name=name, compiler_params=_params("parallel", "arbitrary", "arbitrary"),
    )(zv, zv, zv, view(do), view(dl), view(lse))
    t = n_batch * SEQ
    return [a.reshape(t, MIX_HALF) for a in outs]


def _mix_weights(l1, l2, l3):
    m = jnp.maximum(jnp.maximum(l1, l2), l3)
    e1, e2, e3 = jnp.exp(l1 - m), jnp.exp(l2 - m), jnp.exp(l3 - m)
    inv = 1.0 / (e1 + e2 + e3)
    return e1 * inv, e2 * inv, e3 * inv


def _mix_fwd(outs, lses, name, tm=512):
    t = outs[0].shape[0]

    def body(o1, o2, o3, l1, l2, l3, y_ref):
        w1, w2, w3 = _mix_weights(l1[...], l2[...], l3[...])
        y_ref[...] = (w1 * o1[...] + w2 * o2[...] + w3 * o3[...]).astype(BF16)

    blk = pl.BlockSpec((tm, MIX_HALF), lambda i: (i, 0))
    return pl.pallas_call(
        body, grid=(t // tm,), in_specs=[blk] * 6, out_specs=blk,
        out_shape=jax.ShapeDtypeStruct((t, MIX_HALF), BF16),
        name=name, compiler_params=_params("arbitrary"),
    )(*outs, *lses)


def _mix_bwd(dy, outs, lses, name, tm=256):
    t = outs[0].shape[0]

    def body(dy_ref, o1, o2, o3, l1, l2, l3, d1, d2, d3, s1, s2, s3):
        w = _mix_weights(l1[...], l2[...], l3[...])
        dya = dy_ref[...].astype(F32)
        ya = w[0] * o1[...] + w[1] * o2[...] + w[2] * o3[...]
        r = lax.broadcasted_iota(jnp.int32, (MIX_HALF, MIX_HALF), 0) // HEAD_DIM
        c = lax.broadcasted_iota(jnp.int32, (MIX_HALF, MIX_HALF), 1) // HEAD_DIM
        per_head = _dot_exact(dya * ya, jnp.where(r == c, 1.0, 0.0).astype(BF16))
        for wi, d_ref, s_ref in zip(w, (d1, d2, d3), (s1, s2, s3)):
            d_ref[...] = (wi * dya).astype(BF16)
            s_ref[...] = wi * per_head

    blk = pl.BlockSpec((tm, MIX_HALF), lambda i: (i, 0))
    res = pl.pallas_call(
        body, grid=(t // tm,), in_specs=[blk] * 7, out_specs=[blk] * 6,
        out_shape=[jax.ShapeDtypeStruct((t, MIX_HALF), BF16)] * 3 + [jax.ShapeDtypeStruct((t, MIX_HALF), F32)] * 3,
        name=name, compiler_params=_params("arbitrary"),
    )(dy, *outs, *lses)
    return res[:3], res[3:]


def _assemble_dz(dil_grads, fox_grads, name, tm=256):
    t = fox_grads[0].shape[0]

    def body(*refs):
        dil, fox, out = refs[:9], refs[9:12], refs[12]
        for k in range(3):
            total = dil[k][...] + dil[3 + k][...] + dil[6 + k][...]
            out[:, k * MIX_HALF:(k + 1) * MIX_HALF] = total.astype(BF16)
            out[:, (3 + k) * MIX_HALF:(4 + k) * MIX_HALF] = fox[k][...]

    blk = pl.BlockSpec((tm, MIX_HALF), lambda i: (i, 0))
    return pl.pallas_call(
        body, grid=(t // tm,), in_specs=[blk] * 12, out_specs=pl.BlockSpec((tm, QKV_WIDTH), lambda i: (i, 0)),
        out_shape=jax.ShapeDtypeStruct((t, QKV_WIDTH), BF16),
        name=name, compiler_params=_params("arbitrary"),
    )(*dil_grads, *fox_grads)


X_BQ = 512


def _xattn_probs(q, k):
    sc = _dot_nt(q, k) * X_SCALE
    pe = jnp.exp(sc - jnp.max(sc, axis=1, keepdims=True))
    return pe / jnp.sum(pe, axis=1, keepdims=True)


def _xattn_fwd(qx, kx, vx, n_batch, name):
    nq = SEQ // X_BQ

    def body(q_ref, k_ref, v_ref, o_ref):
        p = _xattn_probs(q_ref[...], k_ref[...])
        o_ref[...] = _dot(p.astype(BF16), v_ref[...]).astype(BF16)

    qblk = pl.BlockSpec((X_BQ, X_HEAD_DIM), lambda b, h, i: (b * nq + i, h))
    kblk = pl.BlockSpec((N_MEM, X_HEAD_DIM), lambda b, h, i: (b, h))
    return pl.pallas_call(
        body, grid=(n_batch, X_HEADS, nq), in_specs=[qblk, kblk, kblk], out_specs=qblk,
        out_shape=jax.ShapeDtypeStruct(qx.shape, BF16),
        name=name, compiler_params=_params("parallel", "parallel", "arbitrary"),
    )(qx, kx, vx)


def _xattn_bwd(qx, kx, vx, dox, n_batch, name):
    nq = SEQ // X_BQ

    def body(q_ref, k_ref, v_ref, do_ref, dq_ref, dk_ref, dv_ref, dk_acc, dv_acc):
        i = pl.program_id(2)

        @pl.when(i == 0)
        def _():
            dk_acc[...] = jnp.zeros_like(dk_acc)
            dv_acc[...] = jnp.zeros_like(dv_acc)

        q, k, do = q_ref[...], k_ref[...], do_ref[...]
        p = _xattn_probs(q, k)
        dp = _dot_nt(do, v_ref[...])
        dsb = (p * (dp - jnp.sum(p * dp, axis=1, keepdims=True))).astype(BF16)
        dq_ref[...] = (_dot(dsb, k) * X_SCALE).astype(BF16)
        dk_acc[...] += _dot_tn(dsb, q) * X_SCALE
        dv_acc[...] += _dot_tn(p.astype(BF16), do)

        @pl.when(i == nq - 1)
        def _():
            dk_ref[...] = dk_acc[...].astype(BF16)
            dv_ref[...] = dv_acc[...].astype(BF16)

    qblk = pl.BlockSpec((X_BQ, X_HEAD_DIM), lambda b, h, i: (b * nq + i, h))
    kblk = pl.BlockSpec((N_MEM, X_HEAD_DIM), lambda b, h, i: (b, h))
    return pl.pallas_call(
        body, grid=(n_batch, X_HEADS, nq), in_specs=[qblk, kblk, kblk, qblk], out_specs=[qblk, kblk, kblk],
        out_shape=[jax.ShapeDtypeStruct(qx.shape, BF16), jax.ShapeDtypeStruct(kx.shape, BF16), jax.ShapeDtypeStruct(kx.shape, BF16)],
        scratch_shapes=[pltpu.VMEM((N_MEM, X_HEAD_DIM), F32)] * 2,
        name=name, compiler_params=_params("parallel", "parallel", "arbitrary"),
    )(qx, kx, vx, dox)


def _adamw(w, g, m, v, name, rows):
    r, c = w.shape
    assert r % rows == 0, (name, w.shape, rows)

    def body(w_ref, g_ref, m_ref, v_ref, d_ref, nm_ref, nv_ref):
        gv = g_ref[...]
        m1 = ADAM_B1 * m_ref[...] + (1.0 - ADAM_B1) * gv
        v1 = ADAM_B2 * v_ref[...] + (1.0 - ADAM_B2) * jnp.square(gv)
        m_hat = m1 / (1.0 - ADAM_B1 ** ADAM_STEP)
        v_hat = v1 / (1.0 - ADAM_B2 ** ADAM_STEP)
        d_ref[...] = -ADAM_LR * (m_hat / (jnp.sqrt(v_hat) + ADAM_EPS) + ADAM_WD * w_ref[...])
        nm_ref[...] = m1
        nv_ref[...] = v1

    blk = pl.BlockSpec((rows, c), lambda i: (i, 0))
    return pl.pallas_call(
        body, grid=(r // rows,), in_specs=[blk] * 4, out_specs=[blk] * 3,
        out_shape=[jax.ShapeDtypeStruct((r, c), F32)] * 3,
        name=name, compiler_params=_params("arbitrary"),
    )(w, g, m, v)


def _relu2(acc):
    a = jnp.maximum(acc, 0.0)
    return acc, a * a


def _relu2_bwd(acc, u):
    return (2.0 * jnp.maximum(u.astype(F32), 0.0) * acc,)


def _local_step(x, mem, target, vecs, wts):
    n_batch = x.shape[0]
    t = n_batch * SEQ
    x0 = x.reshape(t, D_MODEL)
    mem2 = mem.reshape(n_batch * N_MEM, D_MODEL)
    tgt = target.reshape(t, D_MODEL)

    w_in = wts["w_in"]
    w_qkv = w_in[:, :QKV_WIDTH]
    w_gate = jnp.pad(w_in[:, QKV_WIDTH:], ((0, 0), (0, GATE_PAD - N_HEADS)))
    w_out = wts["w_out"]
    b_pad = jnp.pad(vecs["b_forget"], (0, GATE_PAD - N_HEADS)).reshape(1, GATE_PAD)

    h1 = _rmsnorm(x0, vecs["g_mix"], "norm_mix")
    mn = _rmsnorm(mem2, vecs["g_mem"], "norm_mem")
    zqkv = _matmul(h1, w_qkv, "in_qkv", out_dtypes=(BF16,), tn=768)[0]
    gate = _matmul(h1, w_gate, "in_gate")[0]
    c_bc, c_row, sg = _gate_fwd(gate, b_pad, n_batch, "gate_fwd")
    dil = [_dil_fwd(zqkv, d, n_batch, "dil_fwd_%d" % d) for d in DILATIONS]
    outs, lses = [o for o, _ in dil], [l for _, l in dil]
    ya = _mix_fwd(outs, lses, "mix_fwd")
    yf, lse_f = _fox_fwd(zqkv, c_bc, c_row, n_batch, "fox_fwd")
    x1 = _matmul_res(ya, w_out[:MIX_HALF], x0, "out_a")
    x1 = _matmul_res(yf, w_out[MIX_HALF:], x1, "out_f")
    h2 = _rmsnorm(x1, vecs["g_xattn"], "norm_xattn")
    qx = _matmul(h2, wts["w_xq"], "xq", out_dtypes=(BF16,))[0]
    kx = _matmul(mn, wts["w_xk"], "xk", out_dtypes=(BF16,))[0]
    vx = _matmul(mn, wts["w_xv"], "xv", out_dtypes=(BF16,))[0]
    ox = _xattn_fwd(qx, kx, vx, n_batch, "xattn_fwd")
    x2 = _matmul_res(ox, wts["w_xo"], x1, "xo")
    h3 = _rmsnorm(x2, vecs["g_mlp"], "norm_mlp")
    u, a2 = _matmul(h3, wts["w_up"], "mlp_up", out_dtypes=(BF16, BF16), epilogue=_relu2)
    x3 = _matmul_res(a2, wts["w_down"], x2, "mlp_down")
    loss, dx3, dx3b, dg_final = _loss_bwd(x3, vecs["g_final"], tgt, "loss")

    du = _matmul(dx3b, wts["w_down"].T, "mlp_down_bwd", out_dtypes=(BF16,), extras=(u,), epilogue=_relu2_bwd)[0]
    gw_down = _matmul_tn(a2, dx3b, "gw_down")
    gw_up = _matmul_tn(h3, du, "gw_up")
    dh3 = _matmul(du, wts["w_up"].T, "mlp_up_bwd")[0]
    dx2, dx2b, dg_mlp = _rms_bwd(x2, dh3, vecs["g_mlp"], dx3, "norm_mlp_bwd")

    gw_xo = _matmul_tn(ox, dx2b, "gw_xo")
    dox = _matmul(dx2b, wts["w_xo"].T, "xo_bwd", out_dtypes=(BF16,))[0]
    dqx, dkx, dvx = _xattn_bwd(qx, kx, vx, dox, n_batch, "xattn_bwd")
    gw_xq = _matmul_tn(h2, dqx, "gw_xq")
    gw_xk = _matmul_tn(mn, dkx, "gw_xk")
    gw_xv = _matmul_tn(mn, dvx, "gw_xv")
    dh2 = _matmul(dqx, wts["w_xq"].T, "xq_bwd")[0]
    dmn = _matmul(dkx, wts["w_xk"].T, "xk_bwd")[0]
    dmn = _matmul_res(dvx, wts["w_xv"].T, dmn, "xv_bwd")
    _, _, dg_mem = _rms_bwd(mem2, dmn, vecs["g_mem"], None, "norm_mem_bwd")
    dx1, dx1b, dg_xattn = _rms_bwd(x1, dh2, vecs["g_xattn"], dx2, "norm_xattn_bwd")

    gw_out = jnp.concatenate([_matmul_tn(ya, dx1b, "gw_out_a"), _matmul_tn(yf, dx1b, "gw_out_f")], axis=0)
    dy = _matmul(dx1b, w_out.T, "out_bwd", out_dtypes=(BF16,))[0]
    dos, dls = _mix_bwd(dy, outs, lses, "mix_bwd")
    dil_grads = []
    for d, do_i, dl_i, lse_i in zip(DILATIONS, dos, dls, lses):
        dil_grads += _dil_bwd(zqkv, do_i, dl_i, lse_i, d, n_batch, "dil_bwd_%d" % d)
    delta_f = _fox_delta(zqkv, dy, lse_f, c_bc, c_row, n_batch, "fox_delta")
    *fox_grads, dc = _fox_bwd(zqkv, delta_f, dy, lse_f, c_bc, c_row, n_batch, "fox_bwd")
    dzg, db = _gate_bwd(dc.reshape(n_batch, N_HEADS, SEQ), sg, "gate_bwd")
    dz = _assemble_dz(dil_grads, fox_grads, "assemble_dz")
    gw_in = jnp.concatenate([_matmul_tn(h1, dz, "gw_in_qkv"), _matmul_tn(h1, dzg, "gw_in_gate")[:, :N_HEADS]], axis=1)
    dh1 = _matmul(dz, w_qkv.T, "in_qkv_bwd")[0]
    dh1 = _matmul_res(dzg, w_gate.T, dh1, "in_gate_bwd")
    dx0, _, dg_mix = _rms_bwd(x0, dh1, vecs["g_mix"], dx1, "norm_mix_bwd")

    gw = dict(w_in=gw_in, w_out=gw_out, w_xq=gw_xq, w_xk=gw_xk, w_xv=gw_xv, w_xo=gw_xo, w_up=gw_up, w_down=gw_down)
    gv = dict(g_mix=dg_mix, g_xattn=dg_xattn, g_mem=dg_mem, g_mlp=dg_mlp, g_final=dg_final, b_forget=db)
    return loss, dx0.reshape(x.shape), gw, gv


MESH = pl.DeviceIdType.MESH
ANY = pl.BlockSpec(memory_space=pl.ANY)


def _place():
    x, y, c = lax.axis_index("x"), lax.axis_index("y"), lax.axis_index("c")
    other_chips = [(1 - x, y), (x, 1 - y), (1 - x, 1 - y)]
    return x, y, c, other_chips


def _gather_weights(pack):
    def body(p_ref, out_ref, send_sems, recv_sems, pass_send, pass_recv, local_sem):
        x, y, c, chips = _place()
        me = 2 * x + y
        mine = pl.ds(pl.multiple_of(c * PACK_HALF, 16), PACK_HALF)
        theirs = pl.ds(pl.multiple_of((1 - c) * PACK_HALF, 16), PACK_HALF)

        def from_chip(k, chip, rows):
            src = out_ref.at[2 * chip[0] + chip[1], rows]
            return pltpu.make_async_remote_copy(src_ref=src, dst_ref=src, send_sem=send_sems.at[k], recv_sem=recv_sems.at[k],
                                                device_id=(chip[0], chip[1], c), device_id_type=MESH)

        def passed(k, chip, rows):
            src = out_ref.at[2 * chip[0] + chip[1], rows]
            return pltpu.make_async_remote_copy(src_ref=src, dst_ref=src, send_sem=pass_send.at[k], recv_sem=pass_recv.at[k],
                                                device_id=(x, y, 1 - c), device_id_type=MESH)

        local = pltpu.make_async_copy(p_ref, out_ref.at[me], local_sem)
        local.start()
        sends = []
        for k, chip in enumerate(chips):
            cp = pltpu.make_async_remote_copy(src_ref=p_ref.at[mine], dst_ref=out_ref.at[me, mine], send_sem=send_sems.at[k],
                                              recv_sem=recv_sems.at[k], device_id=(chip[0], chip[1], c), device_id_type=MESH)
            cp.start()
            sends.append(cp)
        for k, chip in enumerate(chips):
            from_chip(k, chip, mine).wait_recv()
            cp = passed(k, chip, mine)
            cp.start()
            sends.append(cp)
        for k, chip in enumerate(chips):
            passed(k, chip, theirs).wait_recv()
        for cp in sends:
            cp.wait_send()
        local.wait()

    return pl.pallas_call(
        body, in_specs=[ANY], out_specs=ANY,
        out_shape=jax.ShapeDtypeStruct((N_CHIPS,) + pack.shape, pack.dtype),
        scratch_shapes=[pltpu.SemaphoreType.DMA((3,))] * 4 + [pltpu.SemaphoreType.DMA],
        name="gather_weights",
    )(pack)


def _swap_halves(g):
    def body(g_ref, out_ref, send_sem, recv_sem):
        x, y, c, _ = _place()
        theirs = pl.ds(pl.multiple_of((1 - c) * PACK_HALF, 8), PACK_HALF)
        cp = pltpu.make_async_remote_copy(src_ref=g_ref.at[:, theirs], dst_ref=out_ref, send_sem=send_sem, recv_sem=recv_sem,
                                          device_id=(x, y, 1 - c), device_id_type=MESH)
        cp.start()
        cp.wait()

    return pl.pallas_call(
        body, in_specs=[ANY], out_specs=ANY,
        out_shape=jax.ShapeDtypeStruct((N_CHIPS, PACK_HALF, D_MODEL), F32),
        scratch_shapes=[pltpu.SemaphoreType.DMA, pltpu.SemaphoreType.DMA],
        name="swap_halves",
    )(g)


PACK_TILE = 688


def _add_sibling(g, got):
    n_tiles = PACK_HALF // PACK_TILE
    c = lax.axis_index("c").astype(jnp.int32).reshape(1)

    def body(c_ref, g_ref, got_ref, o_ref):
        o_ref[...] = g_ref[...] + got_ref[...]

    blk = pl.BlockSpec((None, PACK_TILE, D_MODEL), lambda s, i, c_ref: (s, i, 0))
    return pl.pallas_call(
        body,
        grid_spec=pltpu.PrefetchScalarGridSpec(
            num_scalar_prefetch=1, grid=(N_CHIPS, n_tiles),
            in_specs=[pl.BlockSpec((None, PACK_TILE, D_MODEL), lambda s, i, c_ref: (s, c_ref[0] * n_tiles + i, 0)), blk],
            out_specs=blk),
        out_shape=jax.ShapeDtypeStruct((N_CHIPS, PACK_HALF, D_MODEL), F32),
        name="add_sibling", compiler_params=_params("arbitrary", "arbitrary"),
    )(c, g, got)


def _exchange_chips(part):
    def body(p_ref, out_ref, send_sems, recv_sems, local_sem):
        x, y, c, chips = _place()
        me = 2 * x + y
        local = pltpu.make_async_copy(p_ref.at[me], out_ref.at[me], local_sem)
        local.start()
        sends = []
        for k, chip in enumerate(chips):
            cp = pltpu.make_async_remote_copy(src_ref=p_ref.at[2 * chip[0] + chip[1]], dst_ref=out_ref.at[me],
                                              send_sem=send_sems.at[k], recv_sem=recv_sems.at[k],
                                              device_id=(chip[0], chip[1], c), device_id_type=MESH)
            cp.start()
            sends.append(cp)
        for k, chip in enumerate(chips):
            slab = out_ref.at[2 * chip[0] + chip[1]]
            pltpu.make_async_remote_copy(src_ref=slab, dst_ref=slab, send_sem=send_sems.at[k], recv_sem=recv_sems.at[k],
                                         device_id=(chip[0], chip[1], c), device_id_type=MESH).wait_recv()
        for cp in sends:
            cp.wait_send()
        local.wait()

    return pl.pallas_call(
        body, in_specs=[ANY], out_specs=ANY,
        out_shape=jax.ShapeDtypeStruct(part.shape, part.dtype),
        scratch_shapes=[pltpu.SemaphoreType.DMA((3,)), pltpu.SemaphoreType.DMA((3,)), pltpu.SemaphoreType.DMA],
        name="exchange_chips",
    )(part)


def _sum_chips(parts):
    def body(p0, p1, p2, p3, o_ref):
        o_ref[...] = ((p0[...] + p1[...]) + p2[...]) + p3[...]

    def slab(s):
        return pl.BlockSpec((None, PACK_TILE, D_MODEL), lambda i, s=s: (s, i, 0))

    return pl.pallas_call(
        body, grid=(PACK_HALF // PACK_TILE,),
        in_specs=[slab(s) for s in range(N_CHIPS)], out_specs=pl.BlockSpec((PACK_TILE, D_MODEL), lambda i: (i, 0)),
        out_shape=jax.ShapeDtypeStruct((PACK_HALF, D_MODEL), F32),
        name="sum_chips", compiler_params=_params("arbitrary"),
    )(parts, parts, parts, parts)


def _share_halves(half):
    def body(h_ref, out_ref, send_sem, recv_sem, local_sem):
        x, y, c, _ = _place()
        local = pltpu.make_async_copy(h_ref, out_ref.at[c], local_sem)
        local.start()
        cp = pltpu.make_async_remote_copy(src_ref=h_ref, dst_ref=out_ref.at[c], send_sem=send_sem, recv_sem=recv_sem,
                                          device_id=(x, y, 1 - c), device_id_type=MESH)
        cp.start()
        pltpu.make_async_remote_copy(src_ref=h_ref, dst_ref=out_ref.at[1 - c], send_sem=send_sem, recv_sem=recv_sem,
                                     device_id=(x, y, 1 - c), device_id_type=MESH).wait_recv()
        cp.wait_send()
        local.wait()

    return pl.pallas_call(
        body, in_specs=[ANY], out_specs=ANY,
        out_shape=jax.ShapeDtypeStruct((2,) + half.shape, half.dtype),
        scratch_shapes=[pltpu.SemaphoreType.DMA] * 3,
        name="share_halves",
    )(half)


def _reduce_scatter(g):
    part = _add_sibling(g, _swap_halves(g))
    half = _sum_chips(_exchange_chips(part))
    return _share_halves(half).reshape(PACK_ROWS, D_MODEL)


SMALL_ROWS = 8


def _allreduce_small(v):
    def body(v_ref, out_ref, buf, send_sems, recv_sems):
        x, y, c, _ = _place()
        buf[4 * x + 2 * y + c] = v_ref[...]
        sends = []
        for k in range(1, N_DEV):
            px = 1 - x if k & 4 else x
            py = 1 - y if k & 2 else y
            pc = 1 - c if k & 1 else c
            cp = pltpu.make_async_remote_copy(src_ref=v_ref, dst_ref=buf.at[4 * x + 2 * y + c], send_sem=send_sems.at[k - 1],
                                              recv_sem=recv_sems.at[k - 1], device_id=(px, py, pc), device_id_type=MESH)
            cp.start()
            sends.append((cp, 4 * px + 2 * py + pc))
        for k, (cp, peer) in enumerate(sends):
            pltpu.make_async_remote_copy(src_ref=v_ref, dst_ref=buf.at[peer], send_sem=send_sems.at[k], recv_sem=recv_sems.at[k],
                                         device_id=(x, y, c), device_id_type=MESH).wait_recv()
        for cp, _ in sends:
            cp.wait_send()
        total = buf[0]
        for d in range(1, N_DEV):
            total = total + buf[d]
        out_ref[...] = total

    vmem = pl.BlockSpec(memory_space=pltpu.VMEM)
    return pl.pallas_call(
        body, in_specs=[vmem], out_specs=vmem,
        out_shape=jax.ShapeDtypeStruct(v.shape, v.dtype),
        scratch_shapes=[pltpu.VMEM((N_DEV,) + v.shape, v.dtype), pltpu.SemaphoreType.DMA((N_DEV - 1,)),
                        pltpu.SemaphoreType.DMA((N_DEV - 1,))],
        name="allreduce_small",
    )(v)


MATRICES = ("w_in", "w_out", "w_xq", "w_xk", "w_xv", "w_xo", "w_up", "w_down")
VECTORS = ("g_mix", "g_xattn", "g_mem", "g_mlp", "g_final", "b_forget")
WEIGHT_ORDER = ("g_mix", "w_in", "b_forget", "w_out", "g_xattn", "g_mem", "w_xq", "w_xk", "w_xv", "w_xo",
                "g_mlp", "w_up", "w_down", "g_final")
W_IN_SHARD = IN_WIDTH // N_CHIPS
W_IN_ROWS = W_IN_SHARD
PACK_AT = {"w_in": 0, "w_out": 784, "w_xq": 1040, "w_xk": 1296, "w_xv": 1552, "w_xo": 1808, "w_up": 2064, "w_down": 3088}
PACK_LEN = {"w_in": W_IN_ROWS, "w_out": 256, "w_xq": 256, "w_xk": 256, "w_xv": 256, "w_xo": 256, "w_up": 1024, "w_down": 1024}
ADAM_ROWS = {"w_in": 440, "w_out": 256, "w_xq": 256, "w_xk": 256, "w_xv": 256, "w_xo": 256, "w_up": 256, "w_down": 256}


def _pack(parts):
    segs, pos = [], 0
    for name in MATRICES:
        nxt = PACK_AT[MATRICES[MATRICES.index(name) + 1]] if name != MATRICES[-1] else PACK_ROWS
        segs.append(jnp.pad(parts[name], ((0, nxt - pos - PACK_LEN[name]), (0, 0))))
        pos = nxt
    return jnp.concatenate(segs, axis=0)


def _seg(a, name):
    return a[..., PACK_AT[name]:PACK_AT[name] + PACK_LEN[name], :]


def _full_weights(wall):
    cols = lambda a: a.transpose(1, 0, 2).reshape(a.shape[1], -1)
    rows = lambda a: a.reshape(-1, a.shape[-1])
    out = {n: rows(_seg(wall, n)) for n in ("w_out", "w_xq", "w_xk", "w_xv", "w_xo", "w_down")}
    out["w_in"] = cols(_seg(wall, "w_in").reshape(N_CHIPS, D_MODEL, W_IN_SHARD))
    out["w_up"] = cols(_seg(wall, "w_up"))
    return out


def _shard_of(g, name, s):
    if name == "w_in":
        return g[:, s * W_IN_SHARD:(s + 1) * W_IN_SHARD].reshape(W_IN_ROWS, D_MODEL)
    if name == "w_up":
        return g[:, s * D_MODEL:(s + 1) * D_MODEL]
    n = PACK_LEN[name]
    return g[s * n:(s + 1) * n]


def kernel(x, mem, g_mix, w_in, b_forget, w_out, g_xattn, g_mem, w_xq, w_xk, w_xv, w_xo, g_mlp, w_up, w_down, g_final, loss_target, m_g_mix, m_w_in, m_b_forget, m_w_out, m_g_xattn, m_g_mem, m_w_xq, m_w_xk, m_w_xv, m_w_xo, m_g_mlp, m_w_up, m_w_down, m_g_final, v_g_mix, v_w_in, v_b_forget, v_w_out, v_g_xattn, v_g_mem, v_w_xq, v_w_xk, v_w_xv, v_w_xo, v_g_mlp, v_w_up, v_w_down, v_g_final):
    given = dict(locals())
    weights = {n: given[n] for n in WEIGHT_ORDER}
    vecs = {n: weights[n] for n in VECTORS}

    shard = {n: weights[n].astype(BF16) for n in MATRICES}
    shard["w_in"] = shard["w_in"].reshape(W_IN_ROWS, D_MODEL)
    wts = _full_weights(_gather_weights(_pack(shard)))

    loss, grad_x, gw, gv = _local_step(x, mem, loss_target, vecs, wts)

    g_all = jnp.stack([_pack({n: _shard_of(gw[n], n, s) for n in MATRICES}) for s in range(N_CHIPS)])
    red = _reduce_scatter(g_all)
    grads = {n: _seg(red, n).reshape(weights[n].shape) for n in MATRICES}

    row = lambda a: jnp.pad(a.reshape(-1), (0, D_MODEL - a.size)).reshape(1, D_MODEL)
    small = jnp.concatenate([gv[n] for n in VECTORS[:5]] + [row(gv["b_forget"][:, 0]), row(loss[0, :1]),
                             jnp.zeros((1, D_MODEL), F32)], axis=0)
    small = _allreduce_small(small)
    for k, n in enumerate(VECTORS[:5]):
        grads[n] = small[k]
    grads["b_forget"] = small[5, :N_HEADS]
    loss_total = small[6, 0]

    delta, new_m, new_v = {}, {}, {}
    for n in MATRICES:
        flat = lambda a, n=n: a.reshape(-1, 256 if n == "w_in" else D_MODEL)
        d, m1, v1 = _adamw(flat(weights[n]), flat(grads[n]), flat(given["m_" + n]), flat(given["v_" + n]), "adamw_" + n, ADAM_ROWS[n])
        delta[n], new_m[n], new_v[n] = (a.reshape(weights[n].shape) for a in (d, m1, v1))
    stack = lambda prefix: jnp.concatenate([row(given[prefix + n]) for n in VECTORS] + [jnp.zeros((2, D_MODEL), F32)], axis=0)
    g_small = jnp.concatenate([small[:6], jnp.zeros((2, D_MODEL), F32)], axis=0)
    d, m1, v1 = _adamw(stack(""), g_small, stack("m_"), stack("v_"), "adamw_vectors", SMALL_ROWS)
    for k, n in enumerate(VECTORS):
        width = weights[n].shape[0]
        delta[n], new_m[n], new_v[n] = d[k, :width], m1[k, :width], v1[k, :width]

    return (loss_total, grad_x, *[grads[n] for n in WEIGHT_ORDER], *[delta[n] for n in WEIGHT_ORDER],
            *[new_m[n] for n in WEIGHT_ORDER], *[new_v[n] for n in WEIGHT_ORDER])
```

```python
import functools
import math

import jax
import jax.numpy as jnp
from jax import lax
from jax.experimental import pallas as pl
from jax.experimental.pallas import tpu as pltpu

F32 = jnp.float32
BF16 = jnp.bfloat16

D_MODEL = 1024
SEQ = 2048
N_MEM = 256
HEAD_DIM = 64
N_HEADS = 8
MIX_HALF = N_HEADS * HEAD_DIM
QKV_WIDTH = 6 * MIX_HALF
IN_WIDTH = QKV_WIDTH + N_HEADS
GATE_PAD = 128
BLOCK = 128
DILATIONS = (1, 4, 16)
X_HEADS = 4
X_HEAD_DIM = 256
D_FF = 4096
EPS = 1e-6
NEG = -1e30
ATT_SCALE = 1.0 / math.sqrt(HEAD_DIM)
X_SCALE = 1.0 / math.sqrt(X_HEAD_DIM)
LANES = 128
N_CHIPS = 4
N_DEV = 8

ADAM_LR = 0.001
ADAM_B1 = 0.9
ADAM_B2 = 0.999
ADAM_EPS = 1e-08
ADAM_WD = 0.01
ADAM_STEP = 10

VMEM_LIMIT = 48 * 1024 * 1024

PACK_SEGS = (("w_in", 770), ("w_out", 256), ("w_xq", 256), ("w_xk", 256), ("w_xv", 256),
             ("w_xo", 256), ("w_up", 1024), ("w_down", 1024))
PACK_ROWS = 4128
PACK_HALF = PACK_ROWS // 2


def _params(*sem):
    return pltpu.CompilerParams(dimension_semantics=sem or None, vmem_limit_bytes=VMEM_LIMIT)


def _dot(a, b):
    return jnp.dot(a, b, preferred_element_type=F32)


def _dot_nt(a, b):
    return lax.dot_general(a, b, (((1,), (1,)), ((), ())), preferred_element_type=F32)


def _dot_tn(a, b):
    return lax.dot_general(a, b, (((0,), (0,)), ((), ())), preferred_element_type=F32)


def _dot_exact(x, e):
    hi = x.astype(BF16)
    r1 = x - hi.astype(F32)
    mid = r1.astype(BF16)
    lo = (r1 - mid.astype(F32)).astype(BF16)
    return _dot(hi, e) + _dot(mid, e) + _dot(lo, e)


def _head_mask(e):
    lane = lax.broadcasted_iota(jnp.int32, (1, LANES), 1)
    return (lane >= HEAD_DIM * e) & (lane < HEAD_DIM * (e + 1))


def _matmul(a, w, name, out_dtypes=(F32,), extras=(), epilogue=None, tm=512, tn=512):
    m, k = a.shape
    _, n = w.shape
    tm, tn = min(tm, m), min(tn, n)
    assert m % tm == 0 and n % tn == 0, (name, a.shape, w.shape)
    n_ex = len(extras)

    def body(a_ref, w_ref, *rest):
        acc = _dot(a_ref[...], w_ref[...])
        res = (acc,) if epilogue is None else epilogue(acc, *[r[...] for r in rest[:n_ex]])
        for o_ref, r in zip(rest[n_ex:], res):
            o_ref[...] = r.astype(o_ref.dtype)

    tile = pl.BlockSpec((tm, tn), lambda i, j: (i, j))
    return pl.pallas_call(
        body, grid=(m // tm, n // tn),
        in_specs=[pl.BlockSpec((tm, k), lambda i, j: (i, 0)), pl.BlockSpec((k, tn), lambda i, j: (0, j))] + [tile] * n_ex,
        out_specs=[tile] * len(out_dtypes),
        out_shape=[jax.ShapeDtypeStruct((m, n), dt) for dt in out_dtypes],
        name=name, compiler_params=_params("parallel", "arbitrary"),
    )(a, w, *extras)


def _matmul_res(a, w, res, name):
    return _matmul(a, w, name, extras=(res,), epilogue=lambda acc, r: (r + acc,))[0]


def _matmul_tn(x, y, name, tm=1024, tn=1024, tk=512):
    t, m = x.shape
    _, n = y.shape
    tm, tn, tk = min(tm, m), min(tn, n), min(tk, t)
    assert m % tm == 0 and n % tn == 0 and t % tk == 0, (name, x.shape, y.shape)

    def body(x_ref, y_ref, o_ref):
        @pl.when(pl.program_id(2) == 0)
        def _():
            o_ref[...] = jnp.zeros_like(o_ref)

        o_ref[...] += _dot_tn(x_ref[...], y_ref[...])

    return pl.pallas_call(
        body, grid=(m // tm, n // tn, t // tk),
        in_specs=[pl.BlockSpec((tk, tm), lambda i, j, k: (k, i)), pl.BlockSpec((tk, tn), lambda i, j, k: (k, j))],
        out_specs=pl.BlockSpec((tm, tn), lambda i, j, k: (i, j)),
        out_shape=jax.ShapeDtypeStruct((m, n), F32),
        name=name, compiler_params=_params("parallel", "parallel", "arbitrary"),
    )(x, y)


def _rmsnorm(x, g, name, tm=512):
    t, d = x.shape
    tm = min(tm, t)

    def body(x_ref, g_ref, h_ref):
        xv = x_ref[...]
        r = lax.rsqrt(jnp.mean(xv * xv, axis=-1, keepdims=True) + EPS)
        h_ref[...] = (xv * r * g_ref[...]).astype(BF16)

    return pl.pallas_call(
        body, grid=(t // tm,),
        in_specs=[pl.BlockSpec((tm, d), lambda i: (i, 0)), pl.BlockSpec((1, d), lambda i: (0, 0))],
        out_specs=pl.BlockSpec((tm, d), lambda i: (i, 0)),
        out_shape=jax.ShapeDtypeStruct((t, d), BF16),
        name=name, compiler_params=_params("arbitrary"),
    )(x, g.reshape(1, d))


def _rms_bwd_tile(xv, dh, g):
    d = xv.shape[-1]
    r = lax.rsqrt(jnp.mean(xv * xv, axis=-1, keepdims=True) + EPS)
    dyg = dh * g
    proj = jnp.sum(dyg * xv, axis=-1, keepdims=True)
    dx = r * dyg - xv * (r * r * r * (1.0 / d)) * proj
    return dx, dh * (xv * r)


def _rms_bwd(x, dh, g, dres, name, tm=512):
    t, d = x.shape
    tm = min(tm, t)
    has_res = dres is not None

    def body(x_ref, dh_ref, g_ref, *rest):
        if has_res:
            res_ref, dx_ref, dxb_ref, dg_ref = rest
        else:
            dx_ref, dxb_ref, dg_ref = rest
        dx, dg_rows = _rms_bwd_tile(x_ref[...], dh_ref[...], g_ref[...])
        if has_res:
            dx = res_ref[...] + dx
        dx_ref[...] = dx
        dxb_ref[...] = dx.astype(BF16)

        @pl.when(pl.program_id(0) == 0)
        def _():
            dg_ref[...] = jnp.zeros_like(dg_ref)

        dg_ref[...] += jnp.sum(dg_rows, axis=0, keepdims=True)

    row = pl.BlockSpec((tm, d), lambda i: (i, 0))
    vec = pl.BlockSpec((1, d), lambda i: (0, 0))
    return pl.pallas_call(
        body, grid=(t // tm,),
        in_specs=[row, row, vec] + ([row] if has_res else []),
        out_specs=[row, row, vec],
        out_shape=[jax.ShapeDtypeStruct((t, d), F32), jax.ShapeDtypeStruct((t, d), BF16), jax.ShapeDtypeStruct((1, d), F32)],
        name=name, compiler_params=_params("arbitrary"),
    )(x, dh, g.reshape(1, d), *((dres,) if has_res else ()))


def _loss_bwd(x, g, target, name, tm=512):
    t, d = x.shape

    def body(x_ref, g_ref, t_ref, loss_ref, dx_ref, dxb_ref, dg_ref):
        xv = x_ref[...]
        gv = g_ref[...]
        r = lax.rsqrt(jnp.mean(xv * xv, axis=-1, keepdims=True) + EPS)
        err = xv * r * gv - t_ref[...]
        dx, dg_rows = _rms_bwd_tile(xv, err * (1.0 / d), gv)
        dx_ref[...] = dx
        dxb_ref[...] = dx.astype(BF16)

        @pl.when(pl.program_id(0) == 0)
        def _():
            dg_ref[...] = jnp.zeros_like(dg_ref)
            loss_ref[...] = jnp.zeros_like(loss_ref)

        dg_ref[...] += jnp.sum(dg_rows, axis=0, keepdims=True)
        part = jnp.sum(jnp.sum(err * err, axis=0, keepdims=True), axis=1, keepdims=True) * (0.5 / d)
        loss_ref[...] += jnp.broadcast_to(part, loss_ref.shape)

    row = pl.BlockSpec((tm, d), lambda i: (i, 0))
    vec = pl.BlockSpec((1, d), lambda i: (0, 0))
    return pl.pallas_call(
        body, grid=(t // tm,),
        in_specs=[row, vec, row],
        out_specs=[pl.BlockSpec((1, LANES), lambda i: (0, 0)), row, row, vec],
        out_shape=[jax.ShapeDtypeStruct((1, LANES), F32), jax.ShapeDtypeStruct((t, d), F32),
                   jax.ShapeDtypeStruct((t, d), BF16), jax.ShapeDtypeStruct((1, d), F32)],
        name=name, compiler_params=_params("arbitrary"),
    )(x, g.reshape(1, d), target)


def _tri(upper):
    r = lax.broadcasted_iota(jnp.int32, (LANES, LANES), 0)
    c = lax.broadcasted_iota(jnp.int32, (LANES, LANES), 1)
    return jnp.where((r <= c) if upper else (r >= c), 1.0, 0.0).astype(BF16)


def _gate_fwd(gate, b_pad, n_batch, name):
    s = SEQ
    nblk = s // LANES

    def body(g_ref, b_ref, cbc_ref, crow_ref, sg_ref, ct_ref):
        gz = g_ref[...] + b_ref[...]
        logf = jnp.minimum(gz, 0.0) - jnp.log(1.0 + jnp.exp(-jnp.abs(gz)))
        logf_t = logf.T
        sg_ref[...] = (1.0 / (1.0 + jnp.exp(gz))).T[0:N_HEADS]
        upper = _tri(True)
        carry = jnp.zeros((LANES, 1), F32)
        for blk in range(nblk):
            seg = _dot_exact(logf_t[:, blk * LANES:(blk + 1) * LANES], upper) + carry
            carry = seg[:, LANES - 1:LANES]
            ct_ref[:, blk * LANES:(blk + 1) * LANES] = seg
        ct = ct_ref[...]
        crow_ref[...] = ct[0:N_HEADS]
        c_col = ct.T
        lane = lax.broadcasted_iota(jnp.int32, (1, MIX_HALF), 1)
        acc = jnp.zeros((s, MIX_HALF), F32)
        for h in range(N_HEADS):
            acc = jnp.where((lane >= HEAD_DIM * h) & (lane < HEAD_DIM * (h + 1)), c_col[:, h:h + 1], acc)
        cbc_ref[...] = acc

    return pl.pallas_call(
        body, grid=(n_batch,),
        in_specs=[pl.BlockSpec((s, GATE_PAD), lambda b: (b, 0)), pl.BlockSpec((1, GATE_PAD), lambda b: (0, 0))],
        out_specs=[pl.BlockSpec((s, MIX_HALF), lambda b: (b, 0)),
                   pl.BlockSpec((None, N_HEADS, s), lambda b: (b, 0, 0)),
                   pl.BlockSpec((None, N_HEADS, s), lambda b: (b, 0, 0))],
        out_shape=[jax.ShapeDtypeStruct((n_batch * s, MIX_HALF), F32),
                   jax.ShapeDtypeStruct((n_batch, N_HEADS, s), F32),
                   jax.ShapeDtypeStruct((n_batch, N_HEADS, s), F32)],
        scratch_shapes=[pltpu.VMEM((LANES, s), F32)],
        name=name, compiler_params=_params("arbitrary"),
    )(gate, b_pad)


def _gate_bwd(dc, sg, name):
    n_batch, _, s = dc.shape
    nblk = s // LANES

    def body(dc_ref, sg_ref, dz_ref, db_ref, dt_ref):
        lower = _tri(False)
        dcv = dc_ref[...]
        carry = jnp.zeros((N_HEADS, 1), F32)
        dt_ref[...] = jnp.zeros_like(dt_ref)
        for blk in reversed(range(nblk)):
            seg = _dot_exact(dcv[:, blk * LANES:(blk + 1) * LANES], lower) + carry
            carry = seg[:, 0:1]
            dt_ref[0:N_HEADS, blk * LANES:(blk + 1) * LANES] = seg * sg_ref[:, blk * LANES:(blk + 1) * LANES]
        dg_t = dt_ref[...]
        dz_ref[...] = dg_t.T.astype(BF16)

        @pl.when(pl.program_id(0) == 0)
        def _():
            db_ref[...] = jnp.zeros_like(db_ref)

        db_ref[...] += jnp.broadcast_to(jnp.sum(dg_t[0:N_HEADS], axis=1, keepdims=True), db_ref.shape)

    return pl.pallas_call(
        body, grid=(n_batch,),
        in_specs=[pl.BlockSpec((None, N_HEADS, s), lambda b: (b, 0, 0)), pl.BlockSpec((None, N_HEADS, s), lambda b: (b, 0, 0))],
        out_specs=[pl.BlockSpec((s, GATE_PAD), lambda b: (b, 0)), pl.BlockSpec((N_HEADS, LANES), lambda b: (0, 0))],
        out_shape=[jax.ShapeDtypeStruct((n_batch * s, GATE_PAD), BF16), jax.ShapeDtypeStruct((N_HEADS, LANES), F32)],
        scratch_shapes=[pltpu.VMEM((LANES, s), F32)],
        name=name, compiler_params=_params("arbitrary"),
    )(dc, sg)


FOX_BQ = 256
FOX_BK = 256
PAIR_WIDTH = 3 * LANES
N_PAIRS = N_HEADS // 2


def _pair_major(w):
    return w.reshape(w.shape[0], 3, N_PAIRS, LANES).transpose(0, 2, 1, 3).reshape(w.shape[0], 3 * MIX_HALF)


def _pair_major_inv(w):
    return w.reshape(w.shape[0], N_PAIRS, 3, LANES).transpose(0, 2, 1, 3).reshape(w.shape[0], 3 * MIX_HALF)


def _causal(i, j, bq, bk):
    qpos = i * bq + lax.broadcasted_iota(jnp.int32, (bq, 1), 0)
    kpos = j * bk + lax.broadcasted_iota(jnp.int32, (1, bk), 1)
    return kpos <= qpos


def _split_bf16(p):
    hi = p.astype(BF16)
    return hi, (p - hi.astype(F32)).astype(BF16)


def _fox_fwd(zf, c_bc, c_row, n_batch, name):
    s, bq, bk = SEQ, FOX_BQ, FOX_BK
    nq = s // bq
    t = n_batch * s

    def body(q_ref, k_ref, v_ref, cq_ref, cr_ref, o_ref, o32_ref, lse_ref):
        hp, i = pl.program_id(1), pl.program_id(2)
        q = q_ref[...] * ATT_SCALE
        qh = [jnp.where(_head_mask(e), q, jnp.zeros_like(q)) for e in range(2)]
        cq = [cq_ref[:, HEAD_DIM * e:HEAD_DIM * e + 1] for e in range(2)]

        def step(j, carry, masked):
            rows = pl.ds(pl.multiple_of(j * bk, bk), bk)
            kj, vj = k_ref[rows, :], v_ref[rows, :]
            out = []
            for e in range(2):
                m, l, acc = carry[3 * e:3 * e + 3]
                sc = _dot_nt(qh[e], kj) + (cq[e] - cr_ref[pl.ds(2 * hp + e, 1), rows])
                if masked:
                    sc = jnp.where(_causal(i, j, bq, bk), sc, NEG)
                m_new = jnp.maximum(m, jnp.max(sc, axis=1, keepdims=True))
                alpha = jnp.exp(m - m_new)
                p = jnp.exp(sc - m_new)
                p_hi, p_lo = _split_bf16(p)
                out += [m_new, alpha * l + jnp.sum(p, axis=1, keepdims=True), alpha * acc + (_dot(p_hi, vj) + _dot(p_lo, vj))]
            return tuple(out)

        init = (jnp.full((bq, 1), NEG, F32), jnp.zeros((bq, 1), F32), jnp.zeros((bq, LANES), F32)) * 2
        n_clear = (i * bq) // bk
        carry = lax.fori_loop(0, n_clear, functools.partial(step, masked=False), init)
        carry = lax.fori_loop(n_clear, (i * bq + bq + bk - 1) // bk, functools.partial(step, masked=True), carry)
        outs = [carry[3 * e + 2] / carry[3 * e + 1] for e in range(2)]
        lses = [carry[3 * e] + jnp.log(carry[3 * e + 1]) for e in range(2)]
        o = jnp.where(_head_mask(0), outs[0], outs[1])
        o_ref[...] = o.astype(BF16)
        o32_ref[...] = o
        lse_ref[...] = jnp.where(_head_mask(0), lses[0], lses[1])

    def col(c0):
        return lambda b, hp, i: (b, 3 * hp + c0)

    blk = pl.BlockSpec((bq, LANES), lambda b, hp, i: (b * nq + i, hp))
    return pl.pallas_call(
        body, grid=(n_batch, N_PAIRS, nq),
        in_specs=[pl.BlockSpec((bq, LANES), lambda b, hp, i: (b * nq + i, 3 * hp)),
                  pl.BlockSpec((s, LANES), col(1)), pl.BlockSpec((s, LANES), col(2)), blk,
                  pl.BlockSpec((None, N_HEADS, s), lambda b, hp, i: (b, 0, 0))],
        out_specs=[blk, blk, blk],
        out_shape=[jax.ShapeDtypeStruct((t, MIX_HALF), BF16), jax.ShapeDtypeStruct((t, MIX_HALF), F32),
                   jax.ShapeDtypeStruct((t, MIX_HALF), F32)],
        name=name, compiler_params=_params("parallel", "parallel", "arbitrary"),
    )(zf, zf, zf, c_bc, c_row)


def _fox_bwd(zf, o32, dy, lse, c_bc, c_row, dz, n_batch, name):
    s, bq, bk = SEQ, FOX_BQ, FOX_BK
    nq, nk = s // bq, s // bk

    def body(q_ref, k_ref, v_ref, o_ref, do_ref, lse_ref, cq_ref, cr_ref, dz_in, dz_ref, dc_ref, dq_acc):
        del dz_in
        hp, j = pl.program_id(1), pl.program_id(2)

        @pl.when(j == 0)
        def _():
            dq_acc[...] = jnp.zeros_like(dq_acc)

        kj, vj = k_ref[...], v_ref[...]
        cols = pl.ds(pl.multiple_of(j * bk, bk), bk)
        km = [jnp.where(_head_mask(e), kj, jnp.zeros_like(kj)) for e in range(2)]
        ck = [cr_ref[pl.ds(2 * hp + e, 1), cols] for e in range(2)]

        def step(i, carry, masked):
            rows = pl.ds(pl.multiple_of(i * bq, bq), bq)
            qi, doi = q_ref[rows, :] * ATT_SCALE, do_ref[rows, :]
            prod = doi.astype(F32) * o_ref[rows, :]
            out = []
            dq = jnp.zeros((bq, LANES), F32)
            for e in range(2):
                dk_a, dv_a, dc_a = carry[3 * e:3 * e + 3]
                mask = _head_mask(e)
                lane0 = HEAD_DIM * e
                dom = jnp.where(mask, doi, jnp.zeros_like(doi))
                delta = jnp.sum(jnp.where(mask, prod, 0.0), axis=1, keepdims=True)
                sc = _dot_nt(qi, km[e]) + (cq_ref[rows, lane0:lane0 + 1] - ck[e])
                if masked:
                    sc = jnp.where(_causal(i, j, bq, bk), sc, NEG)
                p = jnp.exp(sc - lse_ref[rows, lane0:lane0 + 1])
                ds = p * (_dot_nt(dom, vj) - delta)
                dsb = ds.astype(BF16)
                dq = dq + _dot(dsb, km[e])
                out += [dk_a + _dot_tn(dsb, qi), dv_a + _dot_tn(p.astype(BF16), dom), dc_a - jnp.sum(ds, axis=0, keepdims=True)]
            dq_acc[rows, :] += dq * ATT_SCALE
            return tuple(out)

        init = (jnp.zeros((bk, LANES), F32), jnp.zeros((bk, LANES), F32), jnp.zeros((1, bk), F32)) * 2
        first = (j * bk) // bq
        n_diag = (j * bk + bk + bq - 1) // bq
        carry = lax.fori_loop(first, n_diag, functools.partial(step, masked=True), init)
        carry = lax.fori_loop(n_diag, nq, functools.partial(step, masked=False), carry)
        for e in range(2):
            dc_ref[e:e + 1, :] = carry[3 * e + 2]
        dz_ref[cols, LANES:2 * LANES] = jnp.where(_head_mask(0), carry[0], carry[3]).astype(BF16)
        dz_ref[cols, 2 * LANES:3 * LANES] = (carry[1] + carry[4]).astype(BF16)

        @pl.when(j == nk - 1)
        def _():
            dz_ref[:, 0:LANES] = dq_acc[...].astype(BF16)

    def seq(idx):
        return pl.BlockSpec((s, LANES), lambda b, hp, j: (b, idx(hp)))

    def kblk(c0):
        return pl.BlockSpec((bk, LANES), lambda b, hp, j: (b * nk + j, 3 * hp + c0))

    return pl.pallas_call(
        body, grid=(n_batch, N_PAIRS, nk),
        in_specs=[seq(lambda hp: 3 * hp), kblk(1), kblk(2), seq(lambda hp: hp), seq(lambda hp: N_PAIRS + hp),
                  seq(lambda hp: hp), seq(lambda hp: hp),
                  pl.BlockSpec((None, N_HEADS, s), lambda b, hp, j: (b, 0, 0)), pl.BlockSpec(memory_space=pl.ANY)],
        out_specs=[pl.BlockSpec((s, PAIR_WIDTH), lambda b, hp, j: (b, N_PAIRS + hp)),
                   pl.BlockSpec((None, None, 2, bk), lambda b, hp, j: (b, hp, 0, j))],
        out_shape=[jax.ShapeDtypeStruct(dz.shape, dz.dtype), jax.ShapeDtypeStruct((n_batch, N_PAIRS, 2, s), F32)],
        scratch_shapes=[pltpu.VMEM((s, LANES), F32)],
        input_output_aliases={8: 0},
        name=name, compiler_params=_params("parallel", "parallel", "arbitrary"),
    )(zf, zf, zf, o32, dy, lse, c_bc, c_row, dz)


def _dil_bias(slope, dil):
    qi = lax.broadcasted_iota(jnp.int32, (BLOCK, 2 * BLOCK), 0)
    kj = lax.broadcasted_iota(jnp.int32, (BLOCK, 2 * BLOCK), 1)
    delta = qi + BLOCK - kj
    return jnp.where((delta >= 0) & (delta <= BLOCK), (-slope * dil) * delta.astype(F32), NEG)


def _alibi_slope(hp, e):
    slope = jnp.float32(0.0)
    for k in range(N_PAIRS):
        slope = jnp.where(hp == k, jnp.float32(2.0 ** -(2 * k + e + 1)), slope)
    return slope


def _fill_bias(bias_scr, hp):
    for di, dil in enumerate(DILATIONS):
        for e in range(2):
            bias_scr[2 * di + e] = _dil_bias(_alibi_slope(hp, e), dil)


def _pair_specs(rows):
    return [pl.BlockSpec((rows, LANES), lambda b, hp, c0=c0: (b, 3 * hp + c0)) for c0 in range(3)]


def _strided(start, size, dil):
    return pl.ds(start, size) if dil == 1 else pl.ds(start, size, stride=dil)


def _for_each_block(dil, unit):
    nb = SEQ // dil // BLOCK

    def residue(r, carry):
        unit(r, True)
        if nb > 1:
            def blocks(n, c):
                unit(n * (BLOCK * dil) + r, False)
                return c

            lax.fori_loop(1, nb, blocks, 0)
        return carry

    if dil == 1:
        residue(0, 0)
    else:
        lax.fori_loop(0, dil, residue, 0)


def _mix_weights(l1, l2, l3):
    m = jnp.maximum(jnp.maximum(l1, l2), l3)
    e1, e2, e3 = jnp.exp(l1 - m), jnp.exp(l2 - m), jnp.exp(l3 - m)
    inv = 1.0 / (e1 + e2 + e3)
    return e1 * inv, e2 * inv, e3 * inv


def _dil_fwd(zd, n_batch, name):
    s = SEQ
    t = n_batch * s

    def body(q_ref, k_ref, v_ref, y_ref, l1_ref, l2_ref, l3_ref, o_scr, bias_scr):
        _fill_bias(bias_scr, pl.program_id(1))
        lse_refs = (l1_ref, l2_ref, l3_ref)
        for di, dil in enumerate(DILATIONS):

            def unit(start, first, di=di, dil=dil):
                qrows = _strided(start, BLOCK, dil)
                krows = qrows if first else _strided(start - BLOCK * dil, 2 * BLOCK, dil)
                q = (q_ref[qrows, :] * ATT_SCALE).astype(BF16)
                kc = k_ref[krows, :].astype(BF16)
                vc = v_ref[krows, :].astype(BF16)
                outs, lses = [], []
                for e in range(2):
                    bias = bias_scr[2 * di + e]
                    sc = _dot_nt(jnp.where(_head_mask(e), q, jnp.zeros_like(q)), kc) + (bias[:, BLOCK:] if first else bias)
                    m = jnp.max(sc, axis=1, keepdims=True)
                    pe = jnp.exp(sc - m)
                    l = jnp.sum(pe, axis=1, keepdims=True)
                    outs.append(_dot((pe * (1.0 / l)).astype(BF16), vc))
                    lses.append(m + jnp.log(l))
                o_scr.at[di][qrows, :] = jnp.where(_head_mask(0), outs[0], outs[1])
                lse_refs[di][qrows, :] = jnp.where(_head_mask(0), lses[0], lses[1])

            _for_each_block(dil, unit)
        w = _mix_weights(l1_ref[...], l2_ref[...], l3_ref[...])
        y_ref[...] = (w[0] * o_scr[0] + w[1] * o_scr[1] + w[2] * o_scr[2]).astype(BF16)

    blk = pl.BlockSpec((s, LANES), lambda b, hp: (b, hp))
    res = pl.pallas_call(
        body, grid=(n_batch, N_PAIRS),
        in_specs=_pair_specs(s),
        out_specs=[blk] * 4,
        out_shape=[jax.ShapeDtypeStruct((t, MIX_HALF), BF16)] + [jax.ShapeDtypeStruct((t, MIX_HALF), F32)] * 3,
        scratch_shapes=[pltpu.VMEM((3, s, LANES), F32), pltpu.VMEM((6, BLOCK, 2 * BLOCK), F32)],
        name=name, compiler_params=_params("parallel", "arbitrary"),
    )(zd, zd, zd)
    return res[0], res[1:]


def _dil_bwd(zd, dy, ya, lses, n_batch, name):
    s = SEQ
    t = n_batch * s

    def body(q_ref, k_ref, v_ref, dy_ref, ya_ref, l1_ref, l2_ref, l3_ref, dz_ref, w_scr, dy_scr, dot_scr, acc, bias_scr):
        _fill_bias(bias_scr, pl.program_id(1))
        for di, w in enumerate(_mix_weights(l1_ref[...], l2_ref[...], l3_ref[...])):
            w_scr[di] = w
        dya = dy_ref[...].astype(F32)
        prod = dya * ya_ref[...].astype(F32)
        per_head = [jnp.sum(jnp.where(_head_mask(e), prod, 0.0), axis=1, keepdims=True) for e in range(2)]
        dy_scr[...] = dya
        dot_scr[...] = jnp.where(_head_mask(0), per_head[0], per_head[1])
        acc[...] = jnp.zeros_like(acc)
        lse_refs = (l1_ref, l2_ref, l3_ref)
        for di, dil in enumerate(DILATIONS):

            def unit(start, first, di=di, dil=dil):
                qrows = _strided(start, BLOCK, dil)
                krows = qrows if first else _strided(start - BLOCK * dil, 2 * BLOCK, dil)
                q = (q_ref[qrows, :] * ATT_SCALE).astype(BF16)
                kc = k_ref[krows, :].astype(BF16)
                vc = v_ref[krows, :].astype(BF16)
                wq = w_scr.at[di][qrows, :]
                do = (wq * dy_scr[qrows, :]).astype(BF16)
                sub = wq * dot_scr[qrows, :]
                lse = lse_refs[di][qrows, :]
                dq = jnp.zeros((BLOCK, LANES), F32)
                dk = jnp.zeros((krows.size, LANES), F32)
                dv = jnp.zeros((krows.size, LANES), F32)
                for e in range(2):
                    mask = _head_mask(e)
                    lane0 = HEAD_DIM * e
                    qh = jnp.where(mask, q, jnp.zeros_like(q))
                    doh = jnp.where(mask, do, jnp.zeros_like(do))
                    bias = bias_scr[2 * di + e]
                    sc = _dot_nt(qh, kc) + (bias[:, BLOCK:] if first else bias)
                    p = jnp.exp(sc - lse[:, lane0:lane0 + 1])
                    dsb = (p * (_dot_nt(doh, vc) - sub[:, lane0:lane0 + 1])).astype(BF16)
                    dq = dq + _dot(dsb, jnp.where(mask, kc, jnp.zeros_like(kc)))
                    dk = dk + _dot_tn(dsb, qh)
                    dv = dv + _dot_tn(p.astype(BF16), doh)
                acc.at[0][qrows, :] += dq * ATT_SCALE
                acc.at[1][krows, :] += dk
                acc.at[2][krows, :] += dv

            _for_each_block(dil, unit)
        for k in range(3):
            dz_ref[:, k * LANES:(k + 1) * LANES] = acc[k].astype(BF16)

    blk = pl.BlockSpec((s, LANES), lambda b, hp: (b, hp))
    pair = pl.BlockSpec((s, PAIR_WIDTH), lambda b, hp: (b, hp))
    return pl.pallas_call(
        body, grid=(n_batch, N_PAIRS),
        in_specs=_pair_specs(s) + [blk] * 5,
        out_specs=pair,
        out_shape=jax.ShapeDtypeStruct((t, 2 * 3 * MIX_HALF), BF16),
        scratch_shapes=[pltpu.VMEM((3, s, LANES), F32), pltpu.VMEM((s, LANES), F32), pltpu.VMEM((s, LANES), F32),
                        pltpu.VMEM((3, s, LANES), F32), pltpu.VMEM((6, BLOCK, 2 * BLOCK), F32)],
        name=name, compiler_params=_params("parallel", "arbitrary"),
    )(zd, zd, zd, dy, ya, *lses)


X_BQ = 512


def _xattn_probs(q, k):
    sc = _dot_nt(q, k) * X_SCALE
    pe = jnp.exp(sc - jnp.max(sc, axis=1, keepdims=True))
    return pe / jnp.sum(pe, axis=1, keepdims=True)


def _xattn_fwd(qx, kx, vx, n_batch, name):
    nq = SEQ // X_BQ

    def body(q_ref, k_ref, v_ref, o_ref):
        p = _xattn_probs(q_ref[...], k_ref[...])
        o_ref[...] = _dot(p.astype(BF16), v_ref[...]).astype(BF16)

    qblk = pl.BlockSpec((X_BQ, X_HEAD_DIM), lambda b, h, i: (b * nq + i, h))
    kblk = pl.BlockSpec((N_MEM, X_HEAD_DIM), lambda b, h, i: (b, h))
    return pl.pallas_call(
        body, grid=(n_batch, X_HEADS, nq), in_specs=[qblk, kblk, kblk], out_specs=qblk,
        out_shape=jax.ShapeDtypeStruct(qx.shape, BF16),
        name=name, compiler_params=_params("parallel", "parallel", "arbitrary"),
    )(qx, kx, vx)


def _xattn_bwd(qx, kx, vx, dox, n_batch, name):
    nq = SEQ // X_BQ

    def body(q_ref, k_ref, v_ref, do_ref, dq_ref, dk_ref, dv_ref, dk_acc, dv_acc):
        i = pl.program_id(2)

        @pl.when(i == 0)
        def _():
            dk_acc[...] = jnp.zeros_like(dk_acc)
            dv_acc[...] = jnp.zeros_like(dv_acc)

        q, k, do = q_ref[...], k_ref[...], do_ref[...]
        p = _xattn_probs(q, k)
        dp = _dot_nt(do, v_ref[...])
        dsb = (p * (dp - jnp.sum(p * dp, axis=1, keepdims=True))).astype(BF16)
        dq_ref[...] = (_dot(dsb, k) * X_SCALE).astype(BF16)
        dk_acc[...] += _dot_tn(dsb, q) * X_SCALE
        dv_acc[...] += _dot_tn(p.astype(BF16), do)

        @pl.when(i == nq - 1)
        def _():
            dk_ref[...] = dk_acc[...].astype(BF16)
            dv_ref[...] = dv_acc[...].astype(BF16)

    qblk = pl.BlockSpec((X_BQ, X_HEAD_DIM), lambda b, h, i: (b * nq + i, h))
    kblk = pl.BlockSpec((N_MEM, X_HEAD_DIM), lambda b, h, i: (b, h))
    return pl.pallas_call(
        body, grid=(n_batch, X_HEADS, nq), in_specs=[qblk, kblk, kblk, qblk], out_specs=[qblk, kblk, kblk],
        out_shape=[jax.ShapeDtypeStruct(qx.shape, BF16), jax.ShapeDtypeStruct(kx.shape, BF16), jax.ShapeDtypeStruct(kx.shape, BF16)],
        scratch_shapes=[pltpu.VMEM((N_MEM, X_HEAD_DIM), F32)] * 2,
        name=name, compiler_params=_params("parallel", "parallel", "arbitrary"),
    )(qx, kx, vx, dox)


def _adamw(w, g, m, v, name, rows):
    r, c = w.shape
    assert r % rows == 0, (name, w.shape, rows)

    def body(w_ref, g_ref, m_ref, v_ref, d_ref, nm_ref, nv_ref):
        gv = g_ref[...]
        m1 = ADAM_B1 * m_ref[...] + (1.0 - ADAM_B1) * gv
        v1 = ADAM_B2 * v_ref[...] + (1.0 - ADAM_B2) * jnp.square(gv)
        m_hat = m1 / (1.0 - ADAM_B1 ** ADAM_STEP)
        v_hat = v1 / (1.0 - ADAM_B2 ** ADAM_STEP)
        d_ref[...] = -ADAM_LR * (m_hat / (jnp.sqrt(v_hat) + ADAM_EPS) + ADAM_WD * w_ref[...])
        nm_ref[...] = m1
        nv_ref[...] = v1

    blk = pl.BlockSpec((rows, c), lambda i: (i, 0))
    return pl.pallas_call(
        body, grid=(r // rows,), in_specs=[blk] * 4, out_specs=[blk] * 3,
        out_shape=[jax.ShapeDtypeStruct((r, c), F32)] * 3,
        name=name, compiler_params=_params("arbitrary"),
    )(w, g, m, v)


def _relu2(acc):
    a = jnp.maximum(acc, 0.0)
    return acc, a * a


def _relu2_bwd(acc, u):
    return (2.0 * jnp.maximum(u.astype(F32), 0.0) * acc,)


def _local_step(x, mem, target, vecs, wts):
    n_batch = x.shape[0]
    t = n_batch * SEQ
    x0 = x.reshape(t, D_MODEL)
    mem2 = mem.reshape(n_batch * N_MEM, D_MODEL)
    tgt = target.reshape(t, D_MODEL)

    w_in = wts["w_in"]
    half = 3 * MIX_HALF
    w_dil, w_fox = _pair_major(w_in[:, :half]), _pair_major(w_in[:, half:QKV_WIDTH])
    w_gate = jnp.pad(w_in[:, QKV_WIDTH:], ((0, 0), (0, GATE_PAD - N_HEADS)))
    w_out = wts["w_out"]
    b_pad = jnp.pad(vecs["b_forget"], (0, GATE_PAD - N_HEADS)).reshape(1, GATE_PAD)

    h1 = _rmsnorm(x0, vecs["g_mix"], "norm_mix")
    mn = _rmsnorm(mem2, vecs["g_mem"], "norm_mem")
    zd = _matmul(h1, w_dil, "in_dil", tn=768)[0]
    zf = _matmul(h1, w_fox, "in_fox", out_dtypes=(BF16,), tn=768)[0]
    gate = _matmul(h1, w_gate, "in_gate")[0]
    c_bc, c_row, sg = _gate_fwd(gate, b_pad, n_batch, "gate_fwd")
    ya, lses = _dil_fwd(zd, n_batch, "dil_fwd")
    yf, of32, lse_f = _fox_fwd(zf, c_bc, c_row, n_batch, "fox_fwd")
    x1 = _matmul_res(ya, w_out[:MIX_HALF], x0, "out_a")
    x1 = _matmul_res(yf, w_out[MIX_HALF:], x1, "out_f")
    h2 = _rmsnorm(x1, vecs["g_xattn"], "norm_xattn")
    qx = _matmul(h2, wts["w_xq"], "xq", out_dtypes=(BF16,))[0]
    kx = _matmul(mn, wts["w_xk"], "xk", out_dtypes=(BF16,))[0]
    vx = _matmul(mn, wts["w_xv"], "xv", out_dtypes=(BF16,))[0]
    ox = _xattn_fwd(qx, kx, vx, n_batch, "xattn_fwd")
    x2 = _matmul_res(ox, wts["w_xo"], x1, "xo")
    h3 = _rmsnorm(x2, vecs["g_mlp"], "norm_mlp")
    u, a2 = _matmul(h3, wts["w_up"], "mlp_up", out_dtypes=(BF16, BF16), epilogue=_relu2)
    x3 = _matmul_res(a2, wts["w_down"], x2, "mlp_down")
    loss, dx3, dx3b, dg_final = _loss_bwd(x3, vecs["g_final"], tgt, "loss")

    du = _matmul(dx3b, wts["w_down"].T, "mlp_down_bwd", out_dtypes=(BF16,), extras=(u,), epilogue=_relu2_bwd)[0]
    gw_down = _matmul_tn(a2, dx3b, "gw_down")
    gw_up = _matmul_tn(h3, du, "gw_up")
    dh3 = _matmul(du, wts["w_up"].T, "mlp_up_bwd")[0]
    dx2, dx2b, dg_mlp = _rms_bwd(x2, dh3, vecs["g_mlp"], dx3, "norm_mlp_bwd")

    gw_xo = _matmul_tn(ox, dx2b, "gw_xo")
    dox = _matmul(dx2b, wts["w_xo"].T, "xo_bwd", out_dtypes=(BF16,))[0]
    dqx, dkx, dvx = _xattn_bwd(qx, kx, vx, dox, n_batch, "xattn_bwd")
    gw_xq = _matmul_tn(h2, dqx, "gw_xq")
    gw_xk = _matmul_tn(mn, dkx, "gw_xk")
    gw_xv = _matmul_tn(mn, dvx, "gw_xv")
    dh2 = _matmul(dqx, wts["w_xq"].T, "xq_bwd")[0]
    dmn = _matmul(dkx, wts["w_xk"].T, "xk_bwd")[0]
    dmn = _matmul_res(dvx, wts["w_xv"].T, dmn, "xv_bwd")
    _, _, dg_mem = _rms_bwd(mem2, dmn, vecs["g_mem"], None, "norm_mem_bwd")
    dx1, dx1b, dg_xattn = _rms_bwd(x1, dh2, vecs["g_xattn"], dx2, "norm_xattn_bwd")

    gw_out = jnp.concatenate([_matmul_tn(ya, dx1b, "gw_out_a"), _matmul_tn(yf, dx1b, "gw_out_f")], axis=0)
    dy = _matmul(dx1b, w_out.T, "out_bwd", out_dtypes=(BF16,))[0]
    dz = _dil_bwd(zd, dy, ya, lses, n_batch, "dil_bwd")
    dz, dc = _fox_bwd(zf, of32, dy, lse_f, c_bc, c_row, dz, n_batch, "fox_bwd")
    dzg, db = _gate_bwd(dc.reshape(n_batch, N_HEADS, SEQ), sg, "gate_bwd")
    gw_pm = _matmul_tn(h1, dz, "gw_in_qkv")
    gw_in = jnp.concatenate([_pair_major_inv(gw_pm[:, :half]), _pair_major_inv(gw_pm[:, half:]),
                             _matmul_tn(h1, dzg, "gw_in_gate")[:, :N_HEADS]], axis=1)
    dh1 = _matmul(dz, jnp.concatenate([w_dil, w_fox], axis=1).T, "in_qkv_bwd")[0]
    dh1 = _matmul_res(dzg, w_gate.T, dh1, "in_gate_bwd")
    dx0, _, dg_mix = _rms_bwd(x0, dh1, vecs["g_mix"], dx1, "norm_mix_bwd")

    gw = dict(w_in=gw_in, w_out=gw_out, w_xq=gw_xq, w_xk=gw_xk, w_xv=gw_xv, w_xo=gw_xo, w_up=gw_up, w_down=gw_down)
    gv = dict(g_mix=dg_mix, g_xattn=dg_xattn, g_mem=dg_mem, g_mlp=dg_mlp, g_final=dg_final, b_forget=db)
    return loss, dx0.reshape(x.shape), gw, gv


MESH = pl.DeviceIdType.MESH
ANY = pl.BlockSpec(memory_space=pl.ANY)


def _place():
    x, y, c = lax.axis_index("x"), lax.axis_index("y"), lax.axis_index("c")
    other_chips = [(1 - x, y), (x, 1 - y), (1 - x, 1 - y)]
    return x, y, c, other_chips


def _gather_weights(pack):
    def body(p_ref, out_ref, send_sems, recv_sems, pass_send, pass_recv, local_sem):
        x, y, c, chips = _place()
        me = 2 * x + y
        mine = pl.ds(pl.multiple_of(c * PACK_HALF, 16), PACK_HALF)
        theirs = pl.ds(pl.multiple_of((1 - c) * PACK_HALF, 16), PACK_HALF)

        def from_chip(k, chip, rows):
            src = out_ref.at[2 * chip[0] + chip[1], rows]
            return pltpu.make_async_remote_copy(src_ref=src, dst_ref=src, send_sem=send_sems.at[k], recv_sem=recv_sems.at[k],
                                                device_id=(chip[0], chip[1], c), device_id_type=MESH)

        def passed(k, chip, rows):
            src = out_ref.at[2 * chip[0] + chip[1], rows]
            return pltpu.make_async_remote_copy(src_ref=src, dst_ref=src, send_sem=pass_send.at[k], recv_sem=pass_recv.at[k],
                                                device_id=(x, y, 1 - c), device_id_type=MESH)

        local = pltpu.make_async_copy(p_ref, out_ref.at[me], local_sem)
        local.start()
        sends = []
        for k, chip in enumerate(chips):
            cp = pltpu.make_async_remote_copy(src_ref=p_ref.at[mine], dst_ref=out_ref.at[me, mine], send_sem=send_sems.at[k],
                                              recv_sem=recv_sems.at[k], device_id=(chip[0], chip[1], c), device_id_type=MESH)
            cp.start()
            sends.append(cp)
        for k, chip in enumerate(chips):
            from_chip(k, chip, mine).wait_recv()
            cp = passed(k, chip, mine)
            cp.start()
            sends.append(cp)
        for k, chip in enumerate(chips):
            passed(k, chip, theirs).wait_recv()
        for cp in sends:
            cp.wait_send()
        local.wait()

    return pl.pallas_call(
        body, in_specs=[ANY], out_specs=ANY,
        out_shape=jax.ShapeDtypeStruct((N_CHIPS,) + pack.shape, pack.dtype),
        scratch_shapes=[pltpu.SemaphoreType.DMA((3,))] * 4 + [pltpu.SemaphoreType.DMA],
        name="gather_weights",
    )(pack)


def _swap_halves(g):
    def body(g_ref, out_ref, send_sem, recv_sem):
        x, y, c, _ = _place()
        theirs = pl.ds(pl.multiple_of((1 - c) * PACK_HALF, 8), PACK_HALF)
        cp = pltpu.make_async_remote_copy(src_ref=g_ref.at[:, theirs], dst_ref=out_ref, send_sem=send_sem, recv_sem=recv_sem,
                                          device_id=(x, y, 1 - c), device_id_type=MESH)
        cp.start()
        cp.wait()

    return pl.pallas_call(
        body, in_specs=[ANY], out_specs=ANY,
        out_shape=jax.ShapeDtypeStruct((N_CHIPS, PACK_HALF, D_MODEL), F32),
        scratch_shapes=[pltpu.SemaphoreType.DMA, pltpu.SemaphoreType.DMA],
        name="swap_halves",
    )(g)


PACK_TILE = 688


def _add_sibling(g, got):
    n_tiles = PACK_HALF // PACK_TILE
    c = lax.axis_index("c").astype(jnp.int32).reshape(1)

    def body(c_ref, g_ref, got_ref, o_ref):
        o_ref[...] = g_ref[...] + got_ref[...]

    blk = pl.BlockSpec((None, PACK_TILE, D_MODEL), lambda s, i, c_ref: (s, i, 0))
    return pl.pallas_call(
        body,
        grid_spec=pltpu.PrefetchScalarGridSpec(
            num_scalar_prefetch=1, grid=(N_CHIPS, n_tiles),
            in_specs=[pl.BlockSpec((None, PACK_TILE, D_MODEL), lambda s, i, c_ref: (s, c_ref[0] * n_tiles + i, 0)), blk],
            out_specs=blk),
        out_shape=jax.ShapeDtypeStruct((N_CHIPS, PACK_HALF, D_MODEL), F32),
        name="add_sibling", compiler_params=_params("arbitrary", "arbitrary"),
    )(c, g, got)


def _exchange_chips(part):
    def body(p_ref, out_ref, send_sems, recv_sems, local_sem):
        x, y, c, chips = _place()
        me = 2 * x + y
        local = pltpu.make_async_copy(p_ref.at[me], out_ref.at[me], local_sem)
        local.start()
        sends = []
        for k, chip in enumerate(chips):
            cp = pltpu.make_async_remote_copy(src_ref=p_ref.at[2 * chip[0] + chip[1]], dst_ref=out_ref.at[me],
                                              send_sem=send_sems.at[k], recv_sem=recv_sems.at[k],
                                              device_id=(chip[0], chip[1], c), device_id_type=MESH)
            cp.start()
            sends.append(cp)
        for k, chip in enumerate(chips):
            slab = out_ref.at[2 * chip[0] + chip[1]]
            pltpu.make_async_remote_copy(src_ref=slab, dst_ref=slab, send_sem=send_sems.at[k], recv_sem=recv_sems.at[k],
                                         device_id=(chip[0], chip[1], c), device_id_type=MESH).wait_recv()
        for cp in sends:
            cp.wait_send()
        local.wait()

    return pl.pallas_call(
        body, in_specs=[ANY], out_specs=ANY,
        out_shape=jax.ShapeDtypeStruct(part.shape, part.dtype),
        scratch_shapes=[pltpu.SemaphoreType.DMA((3,)), pltpu.SemaphoreType.DMA((3,)), pltpu.SemaphoreType.DMA],
        name="exchange_chips",
    )(part)


def _sum_chips(parts):
    def body(p0, p1, p2, p3, o_ref):
        o_ref[...] = ((p0[...] + p1[...]) + p2[...]) + p3[...]

    def slab(s):
        return pl.BlockSpec((None, PACK_TILE, D_MODEL), lambda i, s=s: (s, i, 0))

    return pl.pallas_call(
        body, grid=(PACK_HALF // PACK_TILE,),
        in_specs=[slab(s) for s in range(N_CHIPS)], out_specs=pl.BlockSpec((PACK_TILE, D_MODEL), lambda i: (i, 0)),
        out_shape=jax.ShapeDtypeStruct((PACK_HALF, D_MODEL), F32),
        name="sum_chips", compiler_params=_params("arbitrary"),
    )(parts, parts, parts, parts)


def _share_halves(half):
    def body(h_ref, out_ref, send_sem, recv_sem, local_sem):
        x, y, c, _ = _place()
        local = pltpu.make_async_copy(h_ref, out_ref.at[c], local_sem)
        local.start()
        cp = pltpu.make_async_remote_copy(src_ref=h_ref, dst_ref=out_ref.at[c], send_sem=send_sem, recv_sem=recv_sem,
                                          device_id=(x, y, 1 - c), device_id_type=MESH)
        cp.start()
        pltpu.make_async_remote_copy(src_ref=h_ref, dst_ref=out_ref.at[1 - c], send_sem=send_sem, recv_sem=recv_sem,
                                     device_id=(x, y, 1 - c), device_id_type=MESH).wait_recv()
        cp.wait_send()
        local.wait()

    return pl.pallas_call(
        body, in_specs=[ANY], out_specs=ANY,
        out_shape=jax.ShapeDtypeStruct((2,) + half.shape, half.dtype),
        scratch_shapes=[pltpu.SemaphoreType.DMA] * 3,
        name="share_halves",
    )(half)


def _reduce_scatter(g):
    part = _add_sibling(g, _swap_halves(g))
    half = _sum_chips(_exchange_chips(part))
    return _share_halves(half).reshape(PACK_ROWS, D_MODEL)


SMALL_ROWS = 8


def _allreduce_small(v):
    def body(v_ref, out_ref, buf, send_sems, recv_sems):
        x, y, c, _ = _place()
        buf[4 * x + 2 * y + c] = v_ref[...]
        sends = []
        for k in range(1, N_DEV):
            px = 1 - x if k & 4 else x
            py = 1 - y if k & 2 else y
            pc = 1 - c if k & 1 else c
            cp = pltpu.make_async_remote_copy(src_ref=v_ref, dst_ref=buf.at[4 * x + 2 * y + c], send_sem=send_sems.at[k - 1],
                                              recv_sem=recv_sems.at[k - 1], device_id=(px, py, pc), device_id_type=MESH)
            cp.start()
            sends.append((cp, 4 * px + 2 * py + pc))
        for k, (cp, peer) in enumerate(sends):
            pltpu.make_async_remote_copy(src_ref=v_ref, dst_ref=buf.at[peer], send_sem=send_sems.at[k], recv_sem=recv_sems.at[k],
                                         device_id=(x, y, c), device_id_type=MESH).wait_recv()
        for cp, _ in sends:
            cp.wait_send()
        total = buf[0]
        for d in range(1, N_DEV):
            total = total + buf[d]
        out_ref[...] = total

    vmem = pl.BlockSpec(memory_space=pltpu.VMEM)
    return pl.pallas_call(
        body, in_specs=[vmem], out_specs=vmem,
        out_shape=jax.ShapeDtypeStruct(v.shape, v.dtype),
        scratch_shapes=[pltpu.VMEM((N_DEV,) + v.shape, v.dtype), pltpu.SemaphoreType.DMA((N_DEV - 1,)),
                        pltpu.SemaphoreType.DMA((N_DEV - 1,))],
        name="allreduce_small",
    )(v)


MATRICES = ("w_in", "w_out", "w_xq", "w_xk", "w_xv", "w_xo", "w_up", "w_down")
VECTORS = ("g_mix", "g_xattn", "g_mem", "g_mlp", "g_final", "b_forget")
WEIGHT_ORDER = ("g_mix", "w_in", "b_forget", "w_out", "g_xattn", "g_mem", "w_xq", "w_xk", "w_xv", "w_xo",
                "g_mlp", "w_up", "w_down", "g_final")
W_IN_SHARD = IN_WIDTH // N_CHIPS
W_IN_ROWS = W_IN_SHARD
PACK_AT = {"w_in": 0, "w_out": 784, "w_xq": 1040, "w_xk": 1296, "w_xv": 1552, "w_xo": 1808, "w_up": 2064, "w_down": 3088}
PACK_LEN = {"w_in": W_IN_ROWS, "w_out": 256, "w_xq": 256, "w_xk": 256, "w_xv": 256, "w_xo": 256, "w_up": 1024, "w_down": 1024}
ADAM_ROWS = {"w_in": 440, "w_out": 256, "w_xq": 256, "w_xk": 256, "w_xv": 256, "w_xo": 256, "w_up": 256, "w_down": 256}


def _pack(parts):
    segs, pos = [], 0
    for name in MATRICES:
        nxt = PACK_AT[MATRICES[MATRICES.index(name) + 1]] if name != MATRICES[-1] else PACK_ROWS
        segs.append(jnp.pad(parts[name], ((0, nxt - pos - PACK_LEN[name]), (0, 0))))
        pos = nxt
    return jnp.concatenate(segs, axis=0)


def _seg(a, name):
    return a[..., PACK_AT[name]:PACK_AT[name] + PACK_LEN[name], :]


def _full_weights(wall):
    cols = lambda a: a.transpose(1, 0, 2).reshape(a.shape[1], -1)
    rows = lambda a: a.reshape(-1, a.shape[-1])
    out = {n: rows(_seg(wall, n)) for n in ("w_out", "w_xq", "w_xk", "w_xv", "w_xo", "w_down")}
    out["w_in"] = cols(_seg(wall, "w_in").reshape(N_CHIPS, D_MODEL, W_IN_SHARD))
    out["w_up"] = cols(_seg(wall, "w_up"))
    return out


def _shard_of(g, name, s):
    if name == "w_in":
        return g[:, s * W_IN_SHARD:(s + 1) * W_IN_SHARD].reshape(W_IN_ROWS, D_MODEL)
    if name == "w_up":
        return g[:, s * D_MODEL:(s + 1) * D_MODEL]
    n = PACK_LEN[name]
    return g[s * n:(s + 1) * n]


def kernel(x, mem, g_mix, w_in, b_forget, w_out, g_xattn, g_mem, w_xq, w_xk, w_xv, w_xo, g_mlp, w_up, w_down, g_final, loss_target, m_g_mix, m_w_in, m_b_forget, m_w_out, m_g_xattn, m_g_mem, m_w_xq, m_w_xk, m_w_xv, m_w_xo, m_g_mlp, m_w_up, m_w_down, m_g_final, v_g_mix, v_w_in, v_b_forget, v_w_out, v_g_xattn, v_g_mem, v_w_xq, v_w_xk, v_w_xv, v_w_xo, v_g_mlp, v_w_up, v_w_down, v_g_final):
    given = dict(locals())
    weights = {n: given[n] for n in WEIGHT_ORDER}
    vecs = {n: weights[n] for n in VECTORS}

    shard = {n: weights[n].astype(BF16) for n in MATRICES}
    shard["w_in"] = shard["w_in"].reshape(W_IN_ROWS, D_MODEL)
    wts = _full_weights(_gather_weights(_pack(shard)))

    loss, grad_x, gw, gv = _local_step(x, mem, loss_target, vecs, wts)

    g_all = jnp.stack([_pack({n: _shard_of(gw[n], n, s) for n in MATRICES}) for s in range(N_CHIPS)])
    red = _reduce_scatter(g_all)
    grads = {n: _seg(red, n).reshape(weights[n].shape) for n in MATRICES}

    row = lambda a: jnp.pad(a.reshape(-1), (0, D_MODEL - a.size)).reshape(1, D_MODEL)
    small = jnp.concatenate([gv[n] for n in VECTORS[:5]] + [row(gv["b_forget"][:, 0]), row(loss[0, :1]),
                             jnp.zeros((1, D_MODEL), F32)], axis=0)
    small = _allreduce_small(small)
    for k, n in enumerate(VECTORS[:5]):
        grads[n] = small[k]
    grads["b_forget"] = small[5, :N_HEADS]
    loss_total = small[6, 0]

    delta, new_m, new_v = {}, {}, {}
    for n in MATRICES:
        flat = lambda a, n=n: a.reshape(-1, 256 if n == "w_in" else D_MODEL)
        d, m1, v1 = _adamw(flat(weights[n]), flat(grads[n]), flat(given["m_" + n]), flat(given["v_" + n]), "adamw_" + n, ADAM_ROWS[n])
        delta[n], new_m[n], new_v[n] = (a.reshape(weights[n].shape) for a in (d, m1, v1))
    stack = lambda prefix: jnp.concatenate([row(given[prefix + n]) for n in VECTORS] + [jnp.zeros((2, D_MODEL), F32)], axis=0)
    g_small = jnp.concatenate([small[:6], jnp.zeros((2, D_MODEL), F32)], axis=0)
    d, m1, v1 = _adamw(stack(""), g_small, stack("m_"), stack("v_"), "adamw_vectors", SMALL_ROWS)
    for k, n in enumerate(VECTORS):
        width = weights[n].shape[0]
        delta[n], new_m[n], new_v[n] = d[k, :width], m1[k, :width], v1[k, :width]

    return (loss_total, grad_x, *[grads[n] for n in WEIGHT_ORDER], *[delta[n] for n in WEIGHT_ORDER],
            *[new_m[n] for n in WEIGHT_ORDER], *[new_v[n] for n in WEIGHT_ORDER])
```

```python
import functools
import math

import jax
import jax.numpy as jnp
from jax import lax
from jax.experimental import pallas as pl
from jax.experimental.pallas import tpu as pltpu

F32 = jnp.float32
BF16 = jnp.bfloat16

D_MODEL = 1024
SEQ = 2048
N_MEM = 256
HEAD_DIM = 64
N_HEADS = 8
MIX_HALF = N_HEADS * HEAD_DIM
QKV_WIDTH = 6 * MIX_HALF
IN_WIDTH = QKV_WIDTH + N_HEADS
GATE_PAD = 128
BLOCK = 128
DILATIONS = (1, 4, 16)
X_HEADS = 4
X_HEAD_DIM = 256
D_FF = 4096
EPS = 1e-6
NEG = -1e30
ATT_SCALE = 1.0 / math.sqrt(HEAD_DIM)
X_SCALE = 1.0 / math.sqrt(X_HEAD_DIM)
LANES = 128
N_CHIPS = 4
N_DEV = 8

ADAM_LR = 0.001
ADAM_B1 = 0.9
ADAM_B2 = 0.999
ADAM_EPS = 1e-08
ADAM_WD = 0.01
ADAM_STEP = 10

VMEM_LIMIT = 48 * 1024 * 1024

PACK_SEGS = (("w_in", 770), ("w_out", 256), ("w_xq", 256), ("w_xk", 256), ("w_xv", 256),
             ("w_xo", 256), ("w_up", 1024), ("w_down", 1024))
PACK_ROWS = 4128
PACK_HALF = PACK_ROWS // 2


def _params(*sem):
    return pltpu.CompilerParams(dimension_semantics=sem or None, vmem_limit_bytes=VMEM_LIMIT)


def _dot(a, b):
    return jnp.dot(a, b, preferred_element_type=F32)


def _dot_nt(a, b):
    return lax.dot_general(a, b, (((1,), (1,)), ((), ())), preferred_element_type=F32)


def _dot_tn(a, b):
    return lax.dot_general(a, b, (((0,), (0,)), ((), ())), preferred_element_type=F32)


def _dot_exact(x, e):
    hi = x.astype(BF16)
    r1 = x - hi.astype(F32)
    mid = r1.astype(BF16)
    lo = (r1 - mid.astype(F32)).astype(BF16)
    return _dot(hi, e) + _dot(mid, e) + _dot(lo, e)


def _head_mask(e):
    lane = lax.broadcasted_iota(jnp.int32, (1, LANES), 1)
    return (lane >= HEAD_DIM * e) & (lane < HEAD_DIM * (e + 1))


def _matmul(a, w, name, out_dtypes=(F32,), extras=(), epilogue=None, tm=512, tn=512):
    m, k = a.shape
    _, n = w.shape
    tm, tn = min(tm, m), min(tn, n)
    assert m % tm == 0 and n % tn == 0, (name, a.shape, w.shape)
    n_ex = len(extras)

    def body(a_ref, w_ref, *rest):
        acc = _dot(a_ref[...], w_ref[...])
        res = (acc,) if epilogue is None else epilogue(acc, *[r[...] for r in rest[:n_ex]])
        for o_ref, r in zip(rest[n_ex:], res):
            o_ref[...] = r.astype(o_ref.dtype)

    tile = pl.BlockSpec((tm, tn), lambda i, j: (i, j))
    return pl.pallas_call(
        body, grid=(m // tm, n // tn),
        in_specs=[pl.BlockSpec((tm, k), lambda i, j: (i, 0)), pl.BlockSpec((k, tn), lambda i, j: (0, j))] + [tile] * n_ex,
        out_specs=[tile] * len(out_dtypes),
        out_shape=[jax.ShapeDtypeStruct((m, n), dt) for dt in out_dtypes],
        name=name, compiler_params=_params("parallel", "arbitrary"),
    )(a, w, *extras)


def _matmul_res(a, w, res, name):
    return _matmul(a, w, name, extras=(res,), epilogue=lambda acc, r: (r + acc,))[0]


def _matmul_tn(x, y, name, tm=1024, tn=1024, tk=512):
    t, m = x.shape
    _, n = y.shape
    tm, tn, tk = min(tm, m), min(tn, n), min(tk, t)
    assert m % tm == 0 and n % tn == 0 and t % tk == 0, (name, x.shape, y.shape)

    def body(x_ref, y_ref, o_ref):
        @pl.when(pl.program_id(2) == 0)
        def _():
            o_ref[...] = jnp.zeros_like(o_ref)

        o_ref[...] += _dot_tn(x_ref[...], y_ref[...])

    return pl.pallas_call(
        body, grid=(m // tm, n // tn, t // tk),
        in_specs=[pl.BlockSpec((tk, tm), lambda i, j, k: (k, i)), pl.BlockSpec((tk, tn), lambda i, j, k: (k, j))],
        out_specs=pl.BlockSpec((tm, tn), lambda i, j, k: (i, j)),
        out_shape=jax.ShapeDtypeStruct((m, n), F32),
        name=name, compiler_params=_params("parallel", "parallel", "arbitrary"),
    )(x, y)


def _rmsnorm(x, g, name, tm=512):
    t, d = x.shape
    tm = min(tm, t)

    def body(x_ref, g_ref, h_ref):
        xv = x_ref[...]
        r = lax.rsqrt(jnp.mean(xv * xv, axis=-1, keepdims=True) + EPS)
        h_ref[...] = (xv * r * g_ref[...]).astype(BF16)

    return pl.pallas_call(
        body, grid=(t // tm,),
        in_specs=[pl.BlockSpec((tm, d), lambda i: (i, 0)), pl.BlockSpec((1, d), lambda i: (0, 0))],
        out_specs=pl.BlockSpec((tm, d), lambda i: (i, 0)),
        out_shape=jax.ShapeDtypeStruct((t, d), BF16),
        name=name, compiler_params=_params("arbitrary"),
    )(x, g.reshape(1, d))


def _rms_bwd_tile(xv, dh, g):
    d = xv.shape[-1]
    r = lax.rsqrt(jnp.mean(xv * xv, axis=-1, keepdims=True) + EPS)
    dyg = dh * g
    proj = jnp.sum(dyg * xv, axis=-1, keepdims=True)
    dx = r * dyg - xv * (r * r * r * (1.0 / d)) * proj
    return dx, dh * (xv * r)


def _rms_bwd(x, dh, g, dres, name, tm=512):
    t, d = x.shape
    tm = min(tm, t)
    has_res = dres is not None

    def body(x_ref, dh_ref, g_ref, *rest):
        if has_res:
            res_ref, dx_ref, dxb_ref, dg_ref = rest
        else:
            dx_ref, dxb_ref, dg_ref = rest
        dx, dg_rows = _rms_bwd_tile(x_ref[...], dh_ref[...], g_ref[...])
        if has_res:
            dx = res_ref[...] + dx
        dx_ref[...] = dx
        dxb_ref[...] = dx.astype(BF16)

        @pl.when(pl.program_id(0) == 0)
        def _():
            dg_ref[...] = jnp.zeros_like(dg_ref)

        dg_ref[...] += jnp.sum(dg_rows, axis=0, keepdims=True)

    row = pl.BlockSpec((tm, d), lambda i: (i, 0))
    vec = pl.BlockSpec((1, d), lambda i: (0, 0))
    return pl.pallas_call(
        body, grid=(t // tm,),
        in_specs=[row, row, vec] + ([row] if has_res else []),
        out_specs=[row, row, vec],
        out_shape=[jax.ShapeDtypeStruct((t, d), F32), jax.ShapeDtypeStruct((t, d), BF16), jax.ShapeDtypeStruct((1, d), F32)],
        name=name, compiler_params=_params("arbitrary"),
    )(x, dh, g.reshape(1, d), *((dres,) if has_res else ()))


def _loss_bwd(x, g, target, name, tm=512):
    t, d = x.shape

    def body(x_ref, g_ref, t_ref, loss_ref, dx_ref, dxb_ref, dg_ref):
        xv = x_ref[...]
        gv = g_ref[...]
        r = lax.rsqrt(jnp.mean(xv * xv, axis=-1, keepdims=True) + EPS)
        err = xv * r * gv - t_ref[...]
        dx, dg_rows = _rms_bwd_tile(xv, err * (1.0 / d), gv)
        dx_ref[...] = dx
        dxb_ref[...] = dx.astype(BF16)

        @pl.when(pl.program_id(0) == 0)
        def _():
            dg_ref[...] = jnp.zeros_like(dg_ref)
            loss_ref[...] = jnp.zeros_like(loss_ref)

        dg_ref[...] += jnp.sum(dg_rows, axis=0, keepdims=True)
        part = jnp.sum(jnp.sum(err * err, axis=0, keepdims=True), axis=1, keepdims=True) * (0.5 / d)
        loss_ref[...] += jnp.broadcast_to(part, loss_ref.shape)

    row = pl.BlockSpec((tm, d), lambda i: (i, 0))
    vec = pl.BlockSpec((1, d), lambda i: (0, 0))
    return pl.pallas_call(
        body, grid=(t // tm,),
        in_specs=[row, vec, row],
        out_specs=[pl.BlockSpec((1, LANES), lambda i: (0, 0)), row, row, vec],
        out_shape=[jax.ShapeDtypeStruct((1, LANES), F32), jax.ShapeDtypeStruct((t, d), F32),
                   jax.ShapeDtypeStruct((t, d), BF16), jax.ShapeDtypeStruct((1, d), F32)],
        name=name, compiler_params=_params("arbitrary"),
    )(x, g.reshape(1, d), target)


def _tri(upper):
    r = lax.broadcasted_iota(jnp.int32, (LANES, LANES), 0)
    c = lax.broadcasted_iota(jnp.int32, (LANES, LANES), 1)
    return jnp.where((r <= c) if upper else (r >= c), 1.0, 0.0).astype(BF16)


def _gate_fwd(gate, b_pad, n_batch, name):
    s = SEQ
    nblk = s // LANES

    def body(g_ref, b_ref, cbc_ref, crow_ref, sg_ref, ct_ref):
        gz = g_ref[...] + b_ref[...]
        logf = jnp.minimum(gz, 0.0) - jnp.log(1.0 + jnp.exp(-jnp.abs(gz)))
        logf_t = logf.T
        sg_ref[...] = (1.0 / (1.0 + jnp.exp(gz))).T[0:N_HEADS]
        upper = _tri(True)
        carry = jnp.zeros((LANES, 1), F32)
        for blk in range(nblk):
            seg = _dot_exact(logf_t[:, blk * LANES:(blk + 1) * LANES], upper) + carry
            carry = seg[:, LANES - 1:LANES]
            ct_ref[:, blk * LANES:(blk + 1) * LANES] = seg
        ct = ct_ref[...]
        crow_ref[...] = ct[0:N_HEADS]
        c_col = ct.T
        lane = lax.broadcasted_iota(jnp.int32, (1, MIX_HALF), 1)
        acc = jnp.zeros((s, MIX_HALF), F32)
        for h in range(N_HEADS):
            acc = jnp.where((lane >= HEAD_DIM * h) & (lane < HEAD_DIM * (h + 1)), c_col[:, h:h + 1], acc)
        cbc_ref[...] = acc

    return pl.pallas_call(
        body, grid=(n_batch,),
        in_specs=[pl.BlockSpec((s, GATE_PAD), lambda b: (b, 0)), pl.BlockSpec((1, GATE_PAD), lambda b: (0, 0))],
        out_specs=[pl.BlockSpec((s, MIX_HALF), lambda b: (b, 0)),
                   pl.BlockSpec((None, N_HEADS, s), lambda b: (b, 0, 0)),
                   pl.BlockSpec((None, N_HEADS, s), lambda b: (b, 0, 0))],
        out_shape=[jax.ShapeDtypeStruct((n_batch * s, MIX_HALF), F32),
                   jax.ShapeDtypeStruct((n_batch, N_HEADS, s), F32),
                   jax.ShapeDtypeStruct((n_batch, N_HEADS, s), F32)],
        scratch_shapes=[pltpu.VMEM((LANES, s), F32)],
        name=name, compiler_params=_params("arbitrary"),
    )(gate, b_pad)


def _gate_bwd(dc, sg, name):
    n_batch, _, s = dc.shape
    nblk = s // LANES

    def body(dc_ref, sg_ref, dz_ref, db_ref, dt_ref):
        lower = _tri(False)
        dcv = dc_ref[...]
        carry = jnp.zeros((N_HEADS, 1), F32)
        dt_ref[...] = jnp.zeros_like(dt_ref)
        for blk in reversed(range(nblk)):
            seg = _dot_exact(dcv[:, blk * LANES:(blk + 1) * LANES], lower) + carry
            carry = seg[:, 0:1]
            dt_ref[0:N_HEADS, blk * LANES:(blk + 1) * LANES] = seg * sg_ref[:, blk * LANES:(blk + 1) * LANES]
        dg_t = dt_ref[...]
        dz_ref[...] = dg_t.T.astype(BF16)

        @pl.when(pl.program_id(0) == 0)
        def _():
            db_ref[...] = jnp.zeros_like(db_ref)

        db_ref[...] += jnp.broadcast_to(jnp.sum(dg_t[0:N_HEADS], axis=1, keepdims=True), db_ref.shape)

    return pl.pallas_call(
        body, grid=(n_batch,),
        in_specs=[pl.BlockSpec((None, N_HEADS, s), lambda b: (b, 0, 0)), pl.BlockSpec((None, N_HEADS, s), lambda b: (b, 0, 0))],
        out_specs=[pl.BlockSpec((s, GATE_PAD), lambda b: (b, 0)), pl.BlockSpec((N_HEADS, LANES), lambda b: (0, 0))],
        out_shape=[jax.ShapeDtypeStruct((n_batch * s, GATE_PAD), BF16), jax.ShapeDtypeStruct((N_HEADS, LANES), F32)],
        scratch_shapes=[pltpu.VMEM((LANES, s), F32)],
        name=name, compiler_params=_params("arbitrary"),
    )(dc, sg)


FOX_BQ = 512
FOX_BK = 512
PAIR_WIDTH = 3 * LANES
N_PAIRS = N_HEADS // 2


def _pair_major(w):
    return w.reshape(w.shape[0], 3, N_PAIRS, LANES).transpose(0, 2, 1, 3).reshape(w.shape[0], 3 * MIX_HALF)


def _pair_major_inv(w):
    return w.reshape(w.shape[0], N_PAIRS, 3, LANES).transpose(0, 2, 1, 3).reshape(w.shape[0], 3 * MIX_HALF)


def _causal(i, j, bq, bk):
    qpos = i * bq + lax.broadcasted_iota(jnp.int32, (bq, 1), 0)
    kpos = j * bk + lax.broadcasted_iota(jnp.int32, (1, bk), 1)
    return kpos <= qpos


def _split_bf16(p):
    hi = p.astype(BF16)
    return hi, (p - hi.astype(F32)).astype(BF16)


def _fox_fwd(zf, c_bc, c_row, n_batch, name):
    s, bq, bk = SEQ, FOX_BQ, FOX_BK
    nq = s // bq
    t = n_batch * s

    def body(q_ref, k_ref, v_ref, cq_ref, cr_ref, o_ref, o32_ref, lse_ref):
        hp, i = pl.program_id(1), pl.program_id(2)
        q = q_ref[...] * ATT_SCALE
        qh = [jnp.where(_head_mask(e), q, jnp.zeros_like(q)) for e in range(2)]
        cq = [cq_ref[:, HEAD_DIM * e:HEAD_DIM * e + 1] for e in range(2)]

        def step(j, carry, masked):
            rows = pl.ds(pl.multiple_of(j * bk, bk), bk)
            kj, vj = k_ref[rows, :], v_ref[rows, :]
            out = []
            for e in range(2):
                m, l, acc = carry[3 * e:3 * e + 3]
                sc = _dot_nt(qh[e], kj) + (cq[e] - cr_ref[pl.ds(2 * hp + e, 1), rows])
                if masked:
                    sc = jnp.where(_causal(i, j, bq, bk), sc, NEG)
                m_new = jnp.maximum(m, jnp.max(sc, axis=1, keepdims=True))
                alpha = jnp.exp(m - m_new)
                p = jnp.exp(sc - m_new)
                p_hi, p_lo = _split_bf16(p)
                out += [m_new, alpha * l + jnp.sum(p, axis=1, keepdims=True), alpha * acc + (_dot(p_hi, vj) + _dot(p_lo, vj))]
            return tuple(out)

        init = (jnp.full((bq, 1), NEG, F32), jnp.zeros((bq, 1), F32), jnp.zeros((bq, LANES), F32)) * 2
        n_clear = (i * bq) // bk
        carry = lax.fori_loop(0, n_clear, functools.partial(step, masked=False), init)
        carry = lax.fori_loop(n_clear, (i * bq + bq + bk - 1) // bk, functools.partial(step, masked=True), carry)
        outs = [carry[3 * e + 2] / carry[3 * e + 1] for e in range(2)]
        lses = [carry[3 * e] + jnp.log(carry[3 * e + 1]) for e in range(2)]
        o = jnp.where(_head_mask(0), outs[0], outs[1])
        o_ref[...] = o.astype(BF16)
        o32_ref[...] = o
        lse_ref[...] = jnp.where(_head_mask(0), lses[0], lses[1])

    def col(c0):
        return lambda b, hp, i: (b, 3 * hp + c0)

    blk = pl.BlockSpec((bq, LANES), lambda b, hp, i: (b * nq + i, hp))
    return pl.pallas_call(
        body, grid=(n_batch, N_PAIRS, nq),
        in_specs=[pl.BlockSpec((bq, LANES), lambda b, hp, i: (b * nq + i, 3 * hp)),
                  pl.BlockSpec((s, LANES), col(1)), pl.BlockSpec((s, LANES), col(2)), blk,
                  pl.BlockSpec((None, N_HEADS, s), lambda b, hp, i: (b, 0, 0))],
        out_specs=[blk, blk, blk],
        out_shape=[jax.ShapeDtypeStruct((t, MIX_HALF), BF16), jax.ShapeDtypeStruct((t, MIX_HALF), F32),
                   jax.ShapeDtypeStruct((t, MIX_HALF), F32)],
        name=name, compiler_params=_params("parallel", "parallel", "arbitrary"),
    )(zf, zf, zf, c_bc, c_row)


def _fox_bwd(zf, o32, dy, lse, c_bc, c_row, dz, n_batch, name):
    s, bq, bk = SEQ, FOX_BQ, FOX_BK
    nq, nk = s // bq, s // bk

    def body(q_ref, k_ref, v_ref, o_ref, do_ref, lse_ref, cq_ref, cr_ref, dz_in, dz_ref, dc_ref, dq_acc):
        del dz_in
        hp, j = pl.program_id(1), pl.program_id(2)

        @pl.when(j == 0)
        def _():
            dq_acc[...] = jnp.zeros_like(dq_acc)

        kj, vj = k_ref[...], v_ref[...]
        cols = pl.ds(pl.multiple_of(j * bk, bk), bk)
        km = [jnp.where(_head_mask(e), kj, jnp.zeros_like(kj)) for e in range(2)]
        ck = [cr_ref[pl.ds(2 * hp + e, 1), cols] for e in range(2)]

        def step(i, carry, masked):
            rows = pl.ds(pl.multiple_of(i * bq, bq), bq)
            qi, doi = q_ref[rows, :] * ATT_SCALE, do_ref[rows, :]
            prod = doi.astype(F32) * o_ref[rows, :]
            out = []
            dq = jnp.zeros((bq, LANES), F32)
            for e in range(2):
                dk_a, dv_a, dc_a = carry[3 * e:3 * e + 3]
                mask = _head_mask(e)
                lane0 = HEAD_DIM * e
                dom = jnp.where(mask, doi, jnp.zeros_like(doi))
                delta = jnp.sum(jnp.where(mask, prod, 0.0), axis=1, keepdims=True)
                sc = _dot_nt(qi, km[e]) + (cq_ref[rows, lane0:lane0 + 1] - ck[e])
                if masked:
                    sc = jnp.where(_causal(i, j, bq, bk), sc, NEG)
                p = jnp.exp(sc - lse_ref[rows, lane0:lane0 + 1])
                ds = p * (_dot_nt(dom, vj) - delta)
                dsb = ds.astype(BF16)
                dq = dq + _dot(dsb, km[e])
                out += [dk_a + _dot_tn(dsb, qi), dv_a + _dot_tn(p.astype(BF16), dom), dc_a - jnp.sum(ds, axis=0, keepdims=True)]
            dq_acc[rows, :] += dq * ATT_SCALE
            return tuple(out)

        init = (jnp.zeros((bk, LANES), F32), jnp.zeros((bk, LANES), F32), jnp.zeros((1, bk), F32)) * 2
        first = (j * bk) // bq
        n_diag = (j * bk + bk + bq - 1) // bq
        carry = lax.fori_loop(first, n_diag, functools.partial(step, masked=True), init)
        carry = lax.fori_loop(n_diag, nq, functools.partial(step, masked=False), carry)
        for e in range(2):
            dc_ref[e:e + 1, :] = carry[3 * e + 2]
        dz_ref[cols, LANES:2 * LANES] = jnp.where(_head_mask(0), carry[0], carry[3]).astype(BF16)
        dz_ref[cols, 2 * LANES:3 * LANES] = (carry[1] + carry[4]).astype(BF16)

        @pl.when(j == nk - 1)
        def _():
            dz_ref[:, 0:LANES] = dq_acc[...].astype(BF16)

    def seq(idx):
        return pl.BlockSpec((s, LANES), lambda b, hp, j: (b, idx(hp)))

    def kblk(c0):
        return pl.BlockSpec((bk, LANES), lambda b, hp, j: (b * nk + j, 3 * hp + c0))

    return pl.pallas_call(
        body, grid=(n_batch, N_PAIRS, nk),
        in_specs=[seq(lambda hp: 3 * hp), kblk(1), kblk(2), seq(lambda hp: hp), seq(lambda hp: N_PAIRS + hp),
                  seq(lambda hp: hp), seq(lambda hp: hp),
                  pl.BlockSpec((None, N_HEADS, s), lambda b, hp, j: (b, 0, 0)), pl.BlockSpec(memory_space=pl.ANY)],
        out_specs=[pl.BlockSpec((s, PAIR_WIDTH), lambda b, hp, j: (b, N_PAIRS + hp)),
                   pl.BlockSpec((None, None, 2, bk), lambda b, hp, j: (b, hp, 0, j))],
        out_shape=[jax.ShapeDtypeStruct(dz.shape, dz.dtype), jax.ShapeDtypeStruct((n_batch, N_PAIRS, 2, s), F32)],
        scratch_shapes=[pltpu.VMEM((s, LANES), F32)],
        input_output_aliases={8: 0},
        name=name, compiler_params=_params("parallel", "parallel", "arbitrary"),
    )(zf, zf, zf, o32, dy, lse, c_bc, c_row, dz)


def _dil_bias(slope, dil):
    qi = lax.broadcasted_iota(jnp.int32, (BLOCK, 2 * BLOCK), 0)
    kj = lax.broadcasted_iota(jnp.int32, (BLOCK, 2 * BLOCK), 1)
    delta = qi + BLOCK - kj
    return jnp.where((delta >= 0) & (delta <= BLOCK), (-slope * dil) * delta.astype(F32), NEG)


def _alibi_slope(hp, e):
    slope = jnp.float32(0.0)
    for k in range(N_PAIRS):
        slope = jnp.where(hp == k, jnp.float32(2.0 ** -(2 * k + e + 1)), slope)
    return slope


def _fill_bias(bias_scr, hp):
    for di, dil in enumerate(DILATIONS):
        for e in range(2):
            bias_scr[2 * di + e] = _dil_bias(_alibi_slope(hp, e), dil)


def _pair_specs(rows):
    return [pl.BlockSpec((rows, LANES), lambda b, hp, c0=c0: (b, 3 * hp + c0)) for c0 in range(3)]


def _strided(start, size, dil):
    return pl.ds(start, size) if dil == 1 else pl.ds(start, size, stride=dil)


def _for_each_block(dil, unit):
    span = BLOCK * dil
    nb = SEQ // span
    if dil == 1:
        group = 3
        assert (nb - 1) % group == 0
        unit(0, True)

        def later(g, c):
            for u in range(group):
                unit((1 + g * group + u) * span, False)
            return c

        lax.fori_loop(0, (nb - 1) // group, later, 0)
        return
    group = 4
    per = dil // group

    def firsts(g, c):
        for u in range(group):
            unit(g * group + u, True)
        return c

    lax.fori_loop(0, per, firsts, 0)
    if nb > 1:
        def later(i, c):
            for u in range(group):
                unit((1 + i // per) * span + (i % per) * group + u, False)
            return c

        lax.fori_loop(0, (nb - 1) * per, later, 0)


def _mix_weights(l1, l2, l3):
    m = jnp.maximum(jnp.maximum(l1, l2), l3)
    e1, e2, e3 = jnp.exp(l1 - m), jnp.exp(l2 - m), jnp.exp(l3 - m)
    inv = 1.0 / (e1 + e2 + e3)
    return e1 * inv, e2 * inv, e3 * inv


def _dil_fwd(zd, n_batch, name):
    s = SEQ
    t = n_batch * s

    def body(q_ref, k_ref, v_ref, y_ref, l1_ref, l2_ref, l3_ref, o_scr, bias_scr):
        _fill_bias(bias_scr, pl.program_id(1))
        lse_refs = (l1_ref, l2_ref, l3_ref)
        for di, dil in enumerate(DILATIONS):

            def unit(start, first, di=di, dil=dil):
                qrows = _strided(start, BLOCK, dil)
                krows = qrows if first else _strided(start - BLOCK * dil, 2 * BLOCK, dil)
                q = (q_ref[qrows, :] * ATT_SCALE).astype(BF16)
                kc = k_ref[krows, :].astype(BF16)
                vc = v_ref[krows, :].astype(BF16)
                outs, lses = [], []
                for e in range(2):
                    bias = bias_scr[2 * di + e]
                    sc = _dot_nt(jnp.where(_head_mask(e), q, jnp.zeros_like(q)), kc) + (bias[:, BLOCK:] if first else bias)
                    m = jnp.max(sc, axis=1, keepdims=True)
                    pe = jnp.exp(sc - m)
                    l = jnp.sum(pe, axis=1, keepdims=True)
                    outs.append(_dot((pe * (1.0 / l)).astype(BF16), vc))
                    lses.append(m + jnp.log(l))
                o_scr.at[di][qrows, :] = jnp.where(_head_mask(0), outs[0], outs[1])
                lse_refs[di][qrows, :] = jnp.where(_head_mask(0), lses[0], lses[1])

            _for_each_block(dil, unit)
        w = _mix_weights(l1_ref[...], l2_ref[...], l3_ref[...])
        y_ref[...] = (w[0] * o_scr[0] + w[1] * o_scr[1] + w[2] * o_scr[2]).astype(BF16)

    blk = pl.BlockSpec((s, LANES), lambda b, hp: (b, hp))
    res = pl.pallas_call(
        body, grid=(n_batch, N_PAIRS),
        in_specs=_pair_specs(s),
        out_specs=[blk] * 4,
        out_shape=[jax.ShapeDtypeStruct((t, MIX_HALF), BF16)] + [jax.ShapeDtypeStruct((t, MIX_HALF), F32)] * 3,
        scratch_shapes=[pltpu.VMEM((3, s, LANES), F32), pltpu.VMEM((6, BLOCK, 2 * BLOCK), F32)],
        name=name, compiler_params=_params("parallel", "arbitrary"),
    )(zd, zd, zd)
    return res[0], res[1:]


def _dil_bwd(zd, dy, ya, lses, n_batch, name):
    s = SEQ
    t = n_batch * s

    def body(q_ref, k_ref, v_ref, dy_ref, ya_ref, l1_ref, l2_ref, l3_ref, dz_ref, w_scr, dy_scr, dot_scr, acc, bias_scr):
        _fill_bias(bias_scr, pl.program_id(1))
        for di, w in enumerate(_mix_weights(l1_ref[...], l2_ref[...], l3_ref[...])):
            w_scr[di] = w
        dya = dy_ref[...].astype(F32)
        prod = dya * ya_ref[...].astype(F32)
        per_head = [jnp.sum(jnp.where(_head_mask(e), prod, 0.0), axis=1, keepdims=True) for e in range(2)]
        dy_scr[...] = dya
        dot_scr[...] = jnp.where(_head_mask(0), per_head[0], per_head[1])
        acc[...] = jnp.zeros_like(acc)
        lse_refs = (l1_ref, l2_ref, l3_ref)
        for di, dil in enumerate(DILATIONS):

            def unit(start, first, di=di, dil=dil):
                qrows = _strided(start, BLOCK, dil)
                krows = qrows if first else _strided(start - BLOCK * dil, 2 * BLOCK, dil)
                q = (q_ref[qrows, :] * ATT_SCALE).astype(BF16)
                kc = k_ref[krows, :].astype(BF16)
                vc = v_ref[krows, :].astype(BF16)
                wq = w_scr.at[di][qrows, :]
                do = (wq * dy_scr[qrows, :]).astype(BF16)
                sub = wq * dot_scr[qrows, :]
                lse = lse_refs[di][qrows, :]
                dq = jnp.zeros((BLOCK, LANES), F32)
                dk = jnp.zeros((krows.size, LANES), F32)
                dv = jnp.zeros((krows.size, LANES), F32)
                for e in range(2):
                    mask = _head_mask(e)
                    lane0 = HEAD_DIM * e
                    qh = jnp.where(mask, q, jnp.zeros_like(q))
                    doh = jnp.where(mask, do, jnp.zeros_like(do))
                    bias = bias_scr[2 * di + e]
                    sc = _dot_nt(qh, kc) + (bias[:, BLOCK:] if first else bias)
                    p = jnp.exp(sc - lse[:, lane0:lane0 + 1])
                    dsb = (p * (_dot_nt(doh, vc) - sub[:, lane0:lane0 + 1])).astype(BF16)
                    dq = dq + _dot(dsb, jnp.where(mask, kc, jnp.zeros_like(kc)))
                    dk = dk + _dot_tn(dsb, qh)
                    dv = dv + _dot_tn(p.astype(BF16), doh)
                acc.at[0][qrows, :] += dq * ATT_SCALE
                acc.at[1][krows, :] += dk
                acc.at[2][krows, :] += dv

            _for_each_block(dil, unit)
        for k in range(3):
            dz_ref[:, k * LANES:(k + 1) * LANES] = acc[k].astype(BF16)

    blk = pl.BlockSpec((s, LANES), lambda b, hp: (b, hp))
    pair = pl.BlockSpec((s, PAIR_WIDTH), lambda b, hp: (b, hp))
    return pl.pallas_call(
        body, grid=(n_batch, N_PAIRS),
        in_specs=_pair_specs(s) + [blk] * 5,
        out_specs=pair,
        out_shape=jax.ShapeDtypeStruct((t, 2 * 3 * MIX_HALF), BF16),
        scratch_shapes=[pltpu.VMEM((3, s, LANES), F32), pltpu.VMEM((s, LANES), F32), pltpu.VMEM((s, LANES), F32),
                        pltpu.VMEM((3, s, LANES), F32), pltpu.VMEM((6, BLOCK, 2 * BLOCK), F32)],
        name=name, compiler_params=_params("parallel", "arbitrary"),
    )(zd, zd, zd, dy, ya, *lses)


X_BQ = 512


def _xattn_probs(q, k):
    sc = _dot_nt(q, k) * X_SCALE
    pe = jnp.exp(sc - jnp.max(sc, axis=1, keepdims=True))
    return pe / jnp.sum(pe, axis=1, keepdims=True)


def _xattn_fwd(qx, kx, vx, n_batch, name):
    nq = SEQ // X_BQ

    def body(q_ref, k_ref, v_ref, o_ref):
        p = _xattn_probs(q_ref[...], k_ref[...])
        o_ref[...] = _dot(p.astype(BF16), v_ref[...]).astype(BF16)

    qblk = pl.BlockSpec((X_BQ, X_HEAD_DIM), lambda b, h, i: (b * nq + i, h))
    kblk = pl.BlockSpec((N_MEM, X_HEAD_DIM), lambda b, h, i: (b, h))
    return pl.pallas_call(
        body, grid=(n_batch, X_HEADS, nq), in_specs=[qblk, kblk, kblk], out_specs=qblk,
        out_shape=jax.ShapeDtypeStruct(qx.shape, BF16),
        name=name, compiler_params=_params("parallel", "parallel", "arbitrary"),
    )(qx, kx, vx)


def _xattn_bwd(qx, kx, vx, dox, n_batch, name):
    nq = SEQ // X_BQ

    def body(q_ref, k_ref, v_ref, do_ref, dq_ref, dk_ref, dv_ref, dk_acc, dv_acc):
        i = pl.program_id(2)

        @pl.when(i == 0)
        def _():
            dk_acc[...] = jnp.zeros_like(dk_acc)
            dv_acc[...] = jnp.zeros_like(dv_acc)

        q, k, do = q_ref[...], k_ref[...], do_ref[...]
        p = _xattn_probs(q, k)
        dp = _dot_nt(do, v_ref[...])
        dsb = (p * (dp - jnp.sum(p * dp, axis=1, keepdims=True))).astype(BF16)
        dq_ref[...] = (_dot(dsb, k) * X_SCALE).astype(BF16)
        dk_acc[...] += _dot_tn(dsb, q) * X_SCALE
        dv_acc[...] += _dot_tn(p.astype(BF16), do)

        @pl.when(i == nq - 1)
        def _():
            dk_ref[...] = dk_acc[...].astype(BF16)
            dv_ref[...] = dv_acc[...].astype(BF16)

    qblk = pl.BlockSpec((X_BQ, X_HEAD_DIM), lambda b, h, i: (b * nq + i, h))
    kblk = pl.BlockSpec((N_MEM, X_HEAD_DIM), lambda b, h, i: (b, h))
    return pl.pallas_call(
        body, grid=(n_batch, X_HEADS, nq), in_specs=[qblk, kblk, kblk, qblk], out_specs=[qblk, kblk, kblk],
        out_shape=[jax.ShapeDtypeStruct(qx.shape, BF16), jax.ShapeDtypeStruct(kx.shape, BF16), jax.ShapeDtypeStruct(kx.shape, BF16)],
        scratch_shapes=[pltpu.VMEM((N_MEM, X_HEAD_DIM), F32)] * 2,
        name=name, compiler_params=_params("parallel", "parallel", "arbitrary"),
    )(qx, kx, vx, dox)


def _adamw(w, g, m, v, name, rows):
    r, c = w.shape
    assert r % rows == 0, (name, w.shape, rows)

    def body(w_ref, g_ref, m_ref, v_ref, d_ref, nm_ref, nv_ref):
        gv = g_ref[...]
        m1 = ADAM_B1 * m_ref[...] + (1.0 - ADAM_B1) * gv
        v1 = ADAM_B2 * v_ref[...] + (1.0 - ADAM_B2) * jnp.square(gv)
        m_hat = m1 / (1.0 - ADAM_B1 ** ADAM_STEP)
        v_hat = v1 / (1.0 - ADAM_B2 ** ADAM_STEP)
        d_ref[...] = -ADAM_LR * (m_hat / (jnp.sqrt(v_hat) + ADAM_EPS) + ADAM_WD * w_ref[...])
        nm_ref[...] = m1
        nv_ref[...] = v1

    blk = pl.BlockSpec((rows, c), lambda i: (i, 0))
    return pl.pallas_call(
        body, grid=(r // rows,), in_specs=[blk] * 4, out_specs=[blk] * 3,
        out_shape=[jax.ShapeDtypeStruct((r, c), F32)] * 3,
        name=name, compiler_params=_params("arbitrary"),
    )(w, g, m, v)


def _relu2(acc):
    a = jnp.maximum(acc, 0.0)
    return acc, a * a


def _relu2_bwd(acc, u):
    return (2.0 * jnp.maximum(u.astype(F32), 0.0) * acc,)


def _local_step(x, mem, target, vecs, wts):
    n_batch = x.shape[0]
    t = n_batch * SEQ
    x0 = x.reshape(t, D_MODEL)
    mem2 = mem.reshape(n_batch * N_MEM, D_MODEL)
    tgt = target.reshape(t, D_MODEL)

    w_in = wts["w_in"]
    half = 3 * MIX_HALF
    w_dil, w_fox = _pair_major(w_in[:, :half]), _pair_major(w_in[:, half:QKV_WIDTH])
    w_gate = jnp.pad(w_in[:, QKV_WIDTH:], ((0, 0), (0, GATE_PAD - N_HEADS)))
    w_out = wts["w_out"]
    b_pad = jnp.pad(vecs["b_forget"], (0, GATE_PAD - N_HEADS)).reshape(1, GATE_PAD)

    h1 = _rmsnorm(x0, vecs["g_mix"], "norm_mix")
    mn = _rmsnorm(mem2, vecs["g_mem"], "norm_mem")
    zd = _matmul(h1, w_dil, "in_dil", tn=768)[0]
    zf = _matmul(h1, w_fox, "in_fox", out_dtypes=(BF16,), tn=768)[0]
    gate = _matmul(h1, w_gate, "in_gate")[0]
    c_bc, c_row, sg = _gate_fwd(gate, b_pad, n_batch, "gate_fwd")
    ya, lses = _dil_fwd(zd, n_batch, "dil_fwd")
    yf, of32, lse_f = _fox_fwd(zf, c_bc, c_row, n_batch, "fox_fwd")
    x1 = _matmul_res(ya, w_out[:MIX_HALF], x0, "out_a")
    x1 = _matmul_res(yf, w_out[MIX_HALF:], x1, "out_f")
    h2 = _rmsnorm(x1, vecs["g_xattn"], "norm_xattn")
    qx = _matmul(h2, wts["w_xq"], "xq", out_dtypes=(BF16,))[0]
    kx = _matmul(mn, wts["w_xk"], "xk", out_dtypes=(BF16,))[0]
    vx = _matmul(mn, wts["w_xv"], "xv", out_dtypes=(BF16,))[0]
    ox = _xattn_fwd(qx, kx, vx, n_batch, "xattn_fwd")
    x2 = _matmul_res(ox, wts["w_xo"], x1, "xo")
    h3 = _rmsnorm(x2, vecs["g_mlp"], "norm_mlp")
    u, a2 = _matmul(h3, wts["w_up"], "mlp_up", out_dtypes=(BF16, BF16), epilogue=_relu2)
    x3 = _matmul_res(a2, wts["w_down"], x2, "mlp_down")
    loss, dx3, dx3b, dg_final = _loss_bwd(x3, vecs["g_final"], tgt, "loss")

    du = _matmul(dx3b, wts["w_down"].T, "mlp_down_bwd", out_dtypes=(BF16,), extras=(u,), epilogue=_relu2_bwd)[0]
    gw_down = _matmul_tn(a2, dx3b, "gw_down")
    gw_up = _matmul_tn(h3, du, "gw_up")
    dh3 = _matmul(du, wts["w_up"].T, "mlp_up_bwd")[0]
    dx2, dx2b, dg_mlp = _rms_bwd(x2, dh3, vecs["g_mlp"], dx3, "norm_mlp_bwd")

    gw_xo = _matmul_tn(ox, dx2b, "gw_xo")
    dox = _matmul(dx2b, wts["w_xo"].T, "xo_bwd", out_dtypes=(BF16,))[0]
    dqx, dkx, dvx = _xattn_bwd(qx, kx, vx, dox, n_batch, "xattn_bwd")
    gw_xq = _matmul_tn(h2, dqx, "gw_xq")
    gw_xk = _matmul_tn(mn, dkx, "gw_xk")
    gw_xv = _matmul_tn(mn, dvx, "gw_xv")
    dh2 = _matmul(dqx, wts["w_xq"].T, "xq_bwd")[0]
    dmn = _matmul(dkx, wts["w_xk"].T, "xk_bwd")[0]
    dmn = _matmul_res(dvx, wts["w_xv"].T, dmn, "xv_bwd")
    _, _, dg_mem = _rms_bwd(mem2, dmn, vecs["g_mem"], None, "norm_mem_bwd")
    dx1, dx1b, dg_xattn = _rms_bwd(x1, dh2, vecs["g_xattn"], dx2, "norm_xattn_bwd")

    gw_out = jnp.concatenate([_matmul_tn(ya, dx1b, "gw_out_a"), _matmul_tn(yf, dx1b, "gw_out_f")], axis=0)
    dy = _matmul(dx1b, w_out.T, "out_bwd", out_dtypes=(BF16,))[0]
    dz = _dil_bwd(zd, dy, ya, lses, n_batch, "dil_bwd")
    dz, dc = _fox_bwd(zf, of32, dy, lse_f, c_bc, c_row, dz, n_batch, "fox_bwd")
    dzg, db = _gate_bwd(dc.reshape(n_batch, N_HEADS, SEQ), sg, "gate_bwd")
    gw_pm = _matmul_tn(h1, dz, "gw_in_qkv")
    gw_in = jnp.concatenate([_pair_major_inv(gw_pm[:, :half]), _pair_major_inv(gw_pm[:, half:]),
                             _matmul_tn(h1, dzg, "gw_in_gate")[:, :N_HEADS]], axis=1)
    dh1 = _matmul(dz, jnp.concatenate([w_dil, w_fox], axis=1).T, "in_qkv_bwd")[0]
    dh1 = _matmul_res(dzg, w_gate.T, dh1, "in_gate_bwd")
    dx0, _, dg_mix = _rms_bwd(x0, dh1, vecs["g_mix"], dx1, "norm_mix_bwd")

    gw = dict(w_in=gw_in, w_out=gw_out, w_xq=gw_xq, w_xk=gw_xk, w_xv=gw_xv, w_xo=gw_xo, w_up=gw_up, w_down=gw_down)
    gv = dict(g_mix=dg_mix, g_xattn=dg_xattn, g_mem=dg_mem, g_mlp=dg_mlp, g_final=dg_final, b_forget=db)
    return loss, dx0.reshape(x.shape), gw, gv


MESH = pl.DeviceIdType.MESH
ANY = pl.BlockSpec(memory_space=pl.ANY)


def _place():
    x, y, c = lax.axis_index("x"), lax.axis_index("y"), lax.axis_index("c")
    other_chips = [(1 - x, y), (x, 1 - y), (1 - x, 1 - y)]
    return x, y, c, other_chips


def _gather_weights(pack):
    def body(p_ref, out_ref, send_sems, recv_sems, pass_send, pass_recv, local_sem):
        x, y, c, chips = _place()
        me = 2 * x + y
        mine = pl.ds(pl.multiple_of(c * PACK_HALF, 16), PACK_HALF)
        theirs = pl.ds(pl.multiple_of((1 - c) * PACK_HALF, 16), PACK_HALF)

        def from_chip(k, chip, rows):
            src = out_ref.at[2 * chip[0] + chip[1], rows]
            return pltpu.make_async_remote_copy(src_ref=src, dst_ref=src, send_sem=send_sems.at[k], recv_sem=recv_sems.at[k],
                                                device_id=(chip[0], chip[1], c), device_id_type=MESH)

        def passed(k, chip, rows):
            src = out_ref.at[2 * chip[0] + chip[1], rows]
            return pltpu.make_async_remote_copy(src_ref=src, dst_ref=src, send_sem=pass_send.at[k], recv_sem=pass_recv.at[k],
                                                device_id=(x, y, 1 - c), device_id_type=MESH)

        local = pltpu.make_async_copy(p_ref, out_ref.at[me], local_sem)
        local.start()
        sends = []
        for k, chip in enumerate(chips):
            cp = pltpu.make_async_remote_copy(src_ref=p_ref.at[mine], dst_ref=out_ref.at[me, mine], send_sem=send_sems.at[k],
                                              recv_sem=recv_sems.at[k], device_id=(chip[0], chip[1], c), device_id_type=MESH)
            cp.start()
            sends.append(cp)
        for k, chip in enumerate(chips):
            from_chip(k, chip, mine).wait_recv()
            cp = passed(k, chip, mine)
            cp.start()
            sends.append(cp)
        for k, chip in enumerate(chips):
            passed(k, chip, theirs).wait_recv()
        for cp in sends:
            cp.wait_send()
        local.wait()

    return pl.pallas_call(
        body, in_specs=[ANY], out_specs=ANY,
        out_shape=jax.ShapeDtypeStruct((N_CHIPS,) + pack.shape, pack.dtype),
        scratch_shapes=[pltpu.SemaphoreType.DMA((3,))] * 4 + [pltpu.SemaphoreType.DMA],
        name="gather_weights",
    )(pack)


def _swap_halves(g):
    def body(g_ref, out_ref, send_sem, recv_sem):
        x, y, c, _ = _place()
        theirs = pl.ds(pl.multiple_of((1 - c) * PACK_HALF, 8), PACK_HALF)
        cp = pltpu.make_async_remote_copy(src_ref=g_ref.at[:, theirs], dst_ref=out_ref, send_sem=send_sem, recv_sem=recv_sem,
                                          device_id=(x, y, 1 - c), device_id_type=MESH)
        cp.start()
        cp.wait()

    return pl.pallas_call(
        body, in_specs=[ANY], out_specs=ANY,
        out_shape=jax.ShapeDtypeStruct((N_CHIPS, PACK_HALF, D_MODEL), F32),
        scratch_shapes=[pltpu.SemaphoreType.DMA, pltpu.SemaphoreType.DMA],
        name="swap_halves",
    )(g)


PACK_TILE = 688


def _add_sibling(g, got):
    n_tiles = PACK_HALF // PACK_TILE
    c = lax.axis_index("c").astype(jnp.int32).reshape(1)

    def body(c_ref, g_ref, got_ref, o_ref):
        o_ref[...] = g_ref[...] + got_ref[...]

    blk = pl.BlockSpec((None, PACK_TILE, D_MODEL), lambda s, i, c_ref: (s, i, 0))
    return pl.pallas_call(
        body,
        grid_spec=pltpu.PrefetchScalarGridSpec(
            num_scalar_prefetch=1, grid=(N_CHIPS, n_tiles),
            in_specs=[pl.BlockSpec((None, PACK_TILE, D_MODEL), lambda s, i, c_ref: (s, c_ref[0] * n_tiles + i, 0)), blk],
            out_specs=blk),
        out_shape=jax.ShapeDtypeStruct((N_CHIPS, PACK_HALF, D_MODEL), F32),
        name="add_sibling", compiler_params=_params("arbitrary", "arbitrary"),
    )(c, g, got)


def _exchange_chips(part):
    def body(p_ref, out_ref, send_sems, recv_sems, local_sem):
        x, y, c, chips = _place()
        me = 2 * x + y
        local = pltpu.make_async_copy(p_ref.at[me], out_ref.at[me], local_sem)
        local.start()
        sends = []
        for k, chip in enumerate(chips):
            cp = pltpu.make_async_remote_copy(src_ref=p_ref.at[2 * chip[0] + chip[1]], dst_ref=out_ref.at[me],
                                              send_sem=send_sems.at[k], recv_sem=recv_sems.at[k],
                                              device_id=(chip[0], chip[1], c), device_id_type=MESH)
            cp.start()
            sends.append(cp)
        for k, chip in enumerate(chips):
            slab = out_ref.at[2 * chip[0] + chip[1]]
            pltpu.make_async_remote_copy(src_ref=slab, dst_ref=slab, send_sem=send_sems.at[k], recv_sem=recv_sems.at[k],
                                         device_id=(chip[0], chip[1], c), device_id_type=MESH).wait_recv()
        for cp in sends:
            cp.wait_send()
        local.wait()

    return pl.pallas_call(
        body, in_specs=[ANY], out_specs=ANY,
        out_shape=jax.ShapeDtypeStruct(part.shape, part.dtype),
        scratch_shapes=[pltpu.SemaphoreType.DMA((3,)), pltpu.SemaphoreType.DMA((3,)), pltpu.SemaphoreType.DMA],
        name="exchange_chips",
    )(part)


def _sum_chips(parts):
    def body(p0, p1, p2, p3, o_ref):
        o_ref[...] = ((p0[...] + p1[...]) + p2[...]) + p3[...]

    def slab(s):
        return pl.BlockSpec((None, PACK_TILE, D_MODEL), lambda i, s=s: (s, i, 0))

    return pl.pallas_call(
        body, grid=(PACK_HALF // PACK_TILE,),
        in_specs=[slab(s) for s in range(N_CHIPS)], out_specs=pl.BlockSpec((PACK_TILE, D_MODEL), lambda i: (i, 0)),
        out_shape=jax.ShapeDtypeStruct((PACK_HALF, D_MODEL), F32),
        name="sum_chips", compiler_params=_params("arbitrary"),
    )(parts, parts, parts, parts)


def _share_halves(half):
    def body(h_ref, out_ref, send_sem, recv_sem, local_sem):
        x, y, c, _ = _place()
        local = pltpu.make_async_copy(h_ref, out_ref.at[c], local_sem)
        local.start()
        cp = pltpu.make_async_remote_copy(src_ref=h_ref, dst_ref=out_ref.at[c], send_sem=send_sem, recv_sem=recv_sem,
                                          device_id=(x, y, 1 - c), device_id_type=MESH)
        cp.start()
        pltpu.make_async_remote_copy(src_ref=h_ref, dst_ref=out_ref.at[1 - c], send_sem=send_sem, recv_sem=recv_sem,
                                     device_id=(x, y, 1 - c), device_id_type=MESH).wait_recv()
        cp.wait_send()
        local.wait()

    return pl.pallas_call(
        body, in_specs=[ANY], out_specs=ANY,
        out_shape=jax.ShapeDtypeStruct((2,) + half.shape, half.dtype),
        scratch_shapes=[pltpu.SemaphoreType.DMA] * 3,
        name="share_halves",
    )(half)


def _reduce_scatter(g):
    part = _add_sibling(g, _swap_halves(g))
    half = _sum_chips(_exchange_chips(part))
    return _share_halves(half).reshape(PACK_ROWS, D_MODEL)


SMALL_ROWS = 8


def _allreduce_small(v):
    def body(v_ref, out_ref, buf, send_sems, recv_sems):
        x, y, c, _ = _place()
        buf[4 * x + 2 * y + c] = v_ref[...]
        sends = []
        for k in range(1, N_DEV):
            px = 1 - x if k & 4 else x
            py = 1 - y if k & 2 else y
            pc = 1 - c if k & 1 else c
            cp = pltpu.make_async_remote_copy(src_ref=v_ref, dst_ref=buf.at[4 * x + 2 * y + c], send_sem=send_sems.at[k - 1],
                                              recv_sem=recv_sems.at[k - 1], device_id=(px, py, pc), device_id_type=MESH)
            cp.start()
            sends.append((cp, 4 * px + 2 * py + pc))
        for k, (cp, peer) in enumerate(sends):
            pltpu.make_async_remote_copy(src_ref=v_ref, dst_ref=buf.at[peer], send_sem=send_sems.at[k], recv_sem=recv_sems.at[k],
                                         device_id=(x, y, c), device_id_type=MESH).wait_recv()
        for cp, _ in sends:
            cp.wait_send()
        total = buf[0]
        for d in range(1, N_DEV):
            total = total + buf[d]
        out_ref[...] = total

    vmem = pl.BlockSpec(memory_space=pltpu.VMEM)
    return pl.pallas_call(
        body, in_specs=[vmem], out_specs=vmem,
        out_shape=jax.ShapeDtypeStruct(v.shape, v.dtype),
        scratch_shapes=[pltpu.VMEM((N_DEV,) + v.shape, v.dtype), pltpu.SemaphoreType.DMA((N_DEV - 1,)),
                        pltpu.SemaphoreType.DMA((N_DEV - 1,))],
        name="allreduce_small",
    )(v)


MATRICES = ("w_in", "w_out", "w_xq", "w_xk", "w_xv", "w_xo", "w_up", "w_down")
VECTORS = ("g_mix", "g_xattn", "g_mem", "g_mlp", "g_final", "b_forget")
WEIGHT_ORDER = ("g_mix", "w_in", "b_forget", "w_out", "g_xattn", "g_mem", "w_xq", "w_xk", "w_xv", "w_xo",
                "g_mlp", "w_up", "w_down", "g_final")
W_IN_SHARD = IN_WIDTH // N_CHIPS
W_IN_ROWS = W_IN_SHARD
PACK_AT = {"w_in": 0, "w_out": 784, "w_xq": 1040, "w_xk": 1296, "w_xv": 1552, "w_xo": 1808, "w_up": 2064, "w_down": 3088}
PACK_LEN = {"w_in": W_IN_ROWS, "w_out": 256, "w_xq": 256, "w_xk": 256, "w_xv": 256, "w_xo": 256, "w_up": 1024, "w_down": 1024}
ADAM_ROWS = {"w_in": 440, "w_out": 256, "w_xq": 256, "w_xk": 256, "w_xv": 256, "w_xo": 256, "w_up": 256, "w_down": 256}


def _pack(parts):
    segs, pos = [], 0
    for name in MATRICES:
        nxt = PACK_AT[MATRICES[MATRICES.index(name) + 1]] if name != MATRICES[-1] else PACK_ROWS
        segs.append(jnp.pad(parts[name], ((0, nxt - pos - PACK_LEN[name]), (0, 0))))
        pos = nxt
    return jnp.concatenate(segs, axis=0)


def _seg(a, name):
    return a[..., PACK_AT[name]:PACK_AT[name] + PACK_LEN[name], :]


def _full_weights(wall):
    cols = lambda a: a.transpose(1, 0, 2).reshape(a.shape[1], -1)
    rows = lambda a: a.reshape(-1, a.shape[-1])
    out = {n: rows(_seg(wall, n)) for n in ("w_out", "w_xq", "w_xk", "w_xv", "w_xo", "w_down")}
    out["w_in"] = cols(_seg(wall, "w_in").reshape(N_CHIPS, D_MODEL, W_IN_SHARD))
    out["w_up"] = cols(_seg(wall, "w_up"))
    return out


def _shard_of(g, name, s):
    if name == "w_in":
        return g[:, s * W_IN_SHARD:(s + 1) * W_IN_SHARD].reshape(W_IN_ROWS, D_MODEL)
    if name == "w_up":
        return g[:, s * D_MODEL:(s + 1) * D_MODEL]
    n = PACK_LEN[name]
    return g[s * n:(s + 1) * n]


def kernel(x, mem, g_mix, w_in, b_forget, w_out, g_xattn, g_mem, w_xq, w_xk, w_xv, w_xo, g_mlp, w_up, w_down, g_final, loss_target, m_g_mix, m_w_in, m_b_forget, m_w_out, m_g_xattn, m_g_mem, m_w_xq, m_w_xk, m_w_xv, m_w_xo, m_g_mlp, m_w_up, m_w_down, m_g_final, v_g_mix, v_w_in, v_b_forget, v_w_out, v_g_xattn, v_g_mem, v_w_xq, v_w_xk, v_w_xv, v_w_xo, v_g_mlp, v_w_up, v_w_down, v_g_final):
    given = dict(locals())
    weights = {n: given[n] for n in WEIGHT_ORDER}
    vecs = {n: weights[n] for n in VECTORS}

    shard = {n: weights[n].astype(BF16) for n in MATRICES}
    shard["w_in"] = shard["w_in"].reshape(W_IN_ROWS, D_MODEL)
    wts = _full_weights(_gather_weights(_pack(shard)))

    loss, grad_x, gw, gv = _local_step(x, mem, loss_target, vecs, wts)

    g_all = jnp.stack([_pack({n: _shard_of(gw[n], n, s) for n in MATRICES}) for s in range(N_CHIPS)])
    red = _reduce_scatter(g_all)
    grads = {n: _seg(red, n).reshape(weights[n].shape) for n in MATRICES}

    row = lambda a: jnp.pad(a.reshape(-1), (0, D_MODEL - a.size)).reshape(1, D_MODEL)
    small = jnp.concatenate([gv[n] for n in VECTORS[:5]] + [row(gv["b_forget"][:, 0]), row(loss[0, :1]),
                             jnp.zeros((1, D_MODEL), F32)], axis=0)
    small = _allreduce_small(small)
    for k, n in enumerate(VECTORS[:5]):
        grads[n] = small[k]
    grads["b_forget"] = small[5, :N_HEADS]
    loss_total = small[6, 0]

    delta, new_m, new_v = {}, {}, {}
    for n in MATRICES:
        flat = lambda a, n=n: a.reshape(-1, 256 if n == "w_in" else D_MODEL)
        d, m1, v1 = _adamw(flat(weights[n]), flat(grads[n]), flat(given["m_" + n]), flat(given["v_" + n]), "adamw_" + n, ADAM_ROWS[n])
        delta[n], new_m[n], new_v[n] = (a.reshape(weights[n].shape) for a in (d, m1, v1))
    stack = lambda prefix: jnp.concatenate([row(given[prefix + n]) for n in VECTORS] + [jnp.zeros((2, D_MODEL), F32)], axis=0)
    g_small = jnp.concatenate([small[:6], jnp.zeros((2, D_MODEL), F32)], axis=0)
    d, m1, v1 = _adamw(stack(""), g_small, stack("m_"), stack("v_"), "adamw_vectors", SMALL_ROWS)
    for k, n in enumerate(VECTORS):
        width = weights[n].shape[0]
        delta[n], new_m[n], new_v[n] = d[k, :width], m1[k, :width], v1[k, :width]

    return (loss_total, grad_x, *[grads[n] for n in WEIGHT_ORDER], *[delta[n] for n in WEIGHT_ORDER],
            *[new_m[n] for n in WEIGHT_ORDER], *[new_v[n] for n in WEIGHT_ORDER])
```

```python
import functools
import math

import jax
import jax.numpy as jnp
from jax import lax
from jax.experimental import pallas as pl
from jax.experimental.pallas import tpu as pltpu

F32 = jnp.float32
BF16 = jnp.bfloat16

D_MODEL = 1024
SEQ = 2048
N_MEM = 256
HEAD_DIM = 64
N_HEADS = 8
MIX_HALF = N_HEADS * HEAD_DIM
QKV_WIDTH = 6 * MIX_HALF
IN_WIDTH = QKV_WIDTH + N_HEADS
GATE_PAD = 128
BLOCK = 128
DILATIONS = (1, 4, 16)
X_HEADS = 4
X_HEAD_DIM = 256
D_FF = 4096
EPS = 1e-6
NEG = -1e30
ATT_SCALE = 1.0 / math.sqrt(HEAD_DIM)
X_SCALE = 1.0 / math.sqrt(X_HEAD_DIM)
LANES = 128
N_CHIPS = 4
N_DEV = 8

ADAM_LR = 0.001
ADAM_B1 = 0.9
ADAM_B2 = 0.999
ADAM_EPS = 1e-08
ADAM_WD = 0.01
ADAM_STEP = 10

VMEM_LIMIT = 48 * 1024 * 1024

PACK_SEGS = (("w_in", 770), ("w_out", 256), ("w_xq", 256), ("w_xk", 256), ("w_xv", 256),
             ("w_xo", 256), ("w_up", 1024), ("w_down", 1024))
PACK_ROWS = 4128
PACK_HALF = PACK_ROWS // 2


def _params(*sem):
    return pltpu.CompilerParams(dimension_semantics=sem or None, vmem_limit_bytes=VMEM_LIMIT)


def _dot(a, b):
    return jnp.dot(a, b, preferred_element_type=F32)


def _dot_nt(a, b):
    return lax.dot_general(a, b, (((1,), (1,)), ((), ())), preferred_element_type=F32)


def _dot_tn(a, b):
    return lax.dot_general(a, b, (((0,), (0,)), ((), ())), preferred_element_type=F32)


def _dot_exact(x, e):
    hi = x.astype(BF16)
    r1 = x - hi.astype(F32)
    mid = r1.astype(BF16)
    lo = (r1 - mid.astype(F32)).astype(BF16)
    return _dot(hi, e) + _dot(mid, e) + _dot(lo, e)


def _head_mask(e):
    lane = lax.broadcasted_iota(jnp.int32, (1, LANES), 1)
    return (lane >= HEAD_DIM * e) & (lane < HEAD_DIM * (e + 1))


def _matmul(a, w, name, out_dtypes=(F32,), extras=(), epilogue=None, tm=512, tn=512):
    m, k = a.shape
    _, n = w.shape
    tm, tn = min(tm, m), min(tn, n)
    assert m % tm == 0 and n % tn == 0, (name, a.shape, w.shape)
    n_ex = len(extras)

    def body(a_ref, w_ref, *rest):
        acc = _dot(a_ref[...], w_ref[...])
        res = (acc,) if epilogue is None else epilogue(acc, *[r[...] for r in rest[:n_ex]])
        for o_ref, r in zip(rest[n_ex:], res):
            o_ref[...] = r.astype(o_ref.dtype)

    tile = pl.BlockSpec((tm, tn), lambda i, j: (i, j))
    return pl.pallas_call(
        body, grid=(m // tm, n // tn),
        in_specs=[pl.BlockSpec((tm, k), lambda i, j: (i, 0)), pl.BlockSpec((k, tn), lambda i, j: (0, j))] + [tile] * n_ex,
        out_specs=[tile] * len(out_dtypes),
        out_shape=[jax.ShapeDtypeStruct((m, n), dt) for dt in out_dtypes],
        name=name, compiler_params=_params("parallel", "arbitrary"),
    )(a, w, *extras)


def _matmul_res(a, w, res, name):
    return _matmul(a, w, name, extras=(res,), epilogue=lambda acc, r: (r + acc,))[0]


def _matmul_tn(x, y, name, tm=1024, tn=1024, tk=512):
    t, m = x.shape
    _, n = y.shape
    tm, tn, tk = min(tm, m), min(tn, n), min(tk, t)
    assert m % tm == 0 and n % tn == 0 and t % tk == 0, (name, x.shape, y.shape)

    def body(x_ref, y_ref, o_ref):
        @pl.when(pl.program_id(2) == 0)
        def _():
            o_ref[...] = jnp.zeros_like(o_ref)

        o_ref[...] += _dot_tn(x_ref[...], y_ref[...])

    return pl.pallas_call(
        body, grid=(m // tm, n // tn, t // tk),
        in_specs=[pl.BlockSpec((tk, tm), lambda i, j, k: (k, i)), pl.BlockSpec((tk, tn), lambda i, j, k: (k, j))],
        out_specs=pl.BlockSpec((tm, tn), lambda i, j, k: (i, j)),
        out_shape=jax.ShapeDtypeStruct((m, n), F32),
        name=name, compiler_params=_params("parallel", "parallel", "arbitrary"),
    )(x, y)


def _rmsnorm(x, g, name, tm=512):
    t, d = x.shape
    tm = min(tm, t)

    def body(x_ref, g_ref, h_ref):
        xv = x_ref[...]
        r = lax.rsqrt(jnp.mean(xv * xv, axis=-1, keepdims=True) + EPS)
        h_ref[...] = (xv * r * g_ref[...]).astype(BF16)

    return pl.pallas_call(
        body, grid=(t // tm,),
        in_specs=[pl.BlockSpec((tm, d), lambda i: (i, 0)), pl.BlockSpec((1, d), lambda i: (0, 0))],
        out_specs=pl.BlockSpec((tm, d), lambda i: (i, 0)),
        out_shape=jax.ShapeDtypeStruct((t, d), BF16),
        name=name, compiler_params=_params("arbitrary"),
    )(x, g.reshape(1, d))


def _rms_bwd_tile(xv, dh, g):
    d = xv.shape[-1]
    r = lax.rsqrt(jnp.mean(xv * xv, axis=-1, keepdims=True) + EPS)
    dyg = dh * g
    proj = jnp.sum(dyg * xv, axis=-1, keepdims=True)
    dx = r * dyg - xv * (r * r * r * (1.0 / d)) * proj
    return dx, dh * (xv * r)


def _rms_bwd(x, dh, g, dres, name, tm=512):
    t, d = x.shape
    tm = min(tm, t)
    has_res = dres is not None

    def body(x_ref, dh_ref, g_ref, *rest):
        if has_res:
            res_ref, dx_ref, dxb_ref, dg_ref = rest
        else:
            dx_ref, dxb_ref, dg_ref = rest
        dx, dg_rows = _rms_bwd_tile(x_ref[...], dh_ref[...], g_ref[...])
        if has_res:
            dx = res_ref[...] + dx
        dx_ref[...] = dx
        dxb_ref[...] = dx.astype(BF16)

        @pl.when(pl.program_id(0) == 0)
        def _():
            dg_ref[...] = jnp.zeros_like(dg_ref)

        dg_ref[...] += jnp.sum(dg_rows, axis=0, keepdims=True)

    row = pl.BlockSpec((tm, d), lambda i: (i, 0))
    vec = pl.BlockSpec((1, d), lambda i: (0, 0))
    return pl.pallas_call(
        body, grid=(t // tm,),
        in_specs=[row, row, vec] + ([row] if has_res else []),
        out_specs=[row, row, vec],
        out_shape=[jax.ShapeDtypeStruct((t, d), F32), jax.ShapeDtypeStruct((t, d), BF16), jax.ShapeDtypeStruct((1, d), F32)],
        name=name, compiler_params=_params("arbitrary"),
    )(x, dh, g.reshape(1, d), *((dres,) if has_res else ()))


def _loss_bwd(x, g, target, name, tm=512):
    t, d = x.shape

    def body(x_ref, g_ref, t_ref, loss_ref, dx_ref, dxb_ref, dg_ref):
        xv = x_ref[...]
        gv = g_ref[...]
        r = lax.rsqrt(jnp.mean(xv * xv, axis=-1, keepdims=True) + EPS)
        err = xv * r * gv - t_ref[...]
        dx, dg_rows = _rms_bwd_tile(xv, err * (1.0 / d), gv)
        dx_ref[...] = dx
        dxb_ref[...] = dx.astype(BF16)

        @pl.when(pl.program_id(0) == 0)
        def _():
            dg_ref[...] = jnp.zeros_like(dg_ref)
            loss_ref[...] = jnp.zeros_like(loss_ref)

        dg_ref[...] += jnp.sum(dg_rows, axis=0, keepdims=True)
        part = jnp.sum(jnp.sum(err * err, axis=0, keepdims=True), axis=1, keepdims=True) * (0.5 / d)
        loss_ref[...] += jnp.broadcast_to(part, loss_ref.shape)

    row = pl.BlockSpec((tm, d), lambda i: (i, 0))
    vec = pl.BlockSpec((1, d), lambda i: (0, 0))
    return pl.pallas_call(
        body, grid=(t // tm,),
        in_specs=[row, vec, row],
        out_specs=[pl.BlockSpec((1, LANES), lambda i: (0, 0)), row, row, vec],
        out_shape=[jax.ShapeDtypeStruct((1, LANES), F32), jax.ShapeDtypeStruct((t, d), F32),
                   jax.ShapeDtypeStruct((t, d), BF16), jax.ShapeDtypeStruct((1, d), F32)],
        name=name, compiler_params=_params("arbitrary"),
    )(x, g.reshape(1, d), target)


def _tri(upper):
    r = lax.broadcasted_iota(jnp.int32, (LANES, LANES), 0)
    c = lax.broadcasted_iota(jnp.int32, (LANES, LANES), 1)
    return jnp.where((r <= c) if upper else (r >= c), 1.0, 0.0).astype(BF16)


def _gate_fwd(gate, b_pad, n_batch, name):
    s = SEQ
    nblk = s // LANES

    def body(g_ref, b_ref, cbc_ref, crow_ref, sg_ref, ct_ref):
        gz = g_ref[...] + b_ref[...]
        logf = jnp.minimum(gz, 0.0) - jnp.log(1.0 + jnp.exp(-jnp.abs(gz)))
        logf_t = logf.T
        sg_ref[...] = (1.0 / (1.0 + jnp.exp(gz))).T[0:N_HEADS]
        upper = _tri(True)
        carry = jnp.zeros((LANES, 1), F32)
        for blk in range(nblk):
            seg = _dot_exact(logf_t[:, blk * LANES:(blk + 1) * LANES], upper) + carry
            carry = seg[:, LANES - 1:LANES]
            ct_ref[:, blk * LANES:(blk + 1) * LANES] = seg
        ct = ct_ref[...]
        crow_ref[...] = ct[0:N_HEADS]
        c_col = ct.T
        lane = lax.broadcasted_iota(jnp.int32, (1, MIX_HALF), 1)
        acc = jnp.zeros((s, MIX_HALF), F32)
        for h in range(N_HEADS):
            acc = jnp.where((lane >= HEAD_DIM * h) & (lane < HEAD_DIM * (h + 1)), c_col[:, h:h + 1], acc)
        cbc_ref[...] = acc

    return pl.pallas_call(
        body, grid=(n_batch,),
        in_specs=[pl.BlockSpec((s, GATE_PAD), lambda b: (b, 0)), pl.BlockSpec((1, GATE_PAD), lambda b: (0, 0))],
        out_specs=[pl.BlockSpec((s, MIX_HALF), lambda b: (b, 0)),
                   pl.BlockSpec((None, N_HEADS, s), lambda b: (b, 0, 0)),
                   pl.BlockSpec((None, N_HEADS, s), lambda b: (b, 0, 0))],
        out_shape=[jax.ShapeDtypeStruct((n_batch * s, MIX_HALF), F32),
                   jax.ShapeDtypeStruct((n_batch, N_HEADS, s), F32),
                   jax.ShapeDtypeStruct((n_batch, N_HEADS, s), F32)],
        scratch_shapes=[pltpu.VMEM((LANES, s), F32)],
        name=name, compiler_params=_params("arbitrary"),
    )(gate, b_pad)


def _gate_bwd(dc, sg, name):
    n_batch, _, s = dc.shape
    nblk = s // LANES

    def body(dc_ref, sg_ref, dz_ref, db_ref, dt_ref):
        lower = _tri(False)
        dcv = dc_ref[...]
        carry = jnp.zeros((N_HEADS, 1), F32)
        dt_ref[...] = jnp.zeros_like(dt_ref)
        for blk in reversed(range(nblk)):
            seg = _dot_exact(dcv[:, blk * LANES:(blk + 1) * LANES], lower) + carry
            carry = seg[:, 0:1]
            dt_ref[0:N_HEADS, blk * LANES:(blk + 1) * LANES] = seg * sg_ref[:, blk * LANES:(blk + 1) * LANES]
        dg_t = dt_ref[...]
        dz_ref[...] = dg_t.T.astype(BF16)

        @pl.when(pl.program_id(0) == 0)
        def _():
            db_ref[...] = jnp.zeros_like(db_ref)

        db_ref[...] += jnp.broadcast_to(jnp.sum(dg_t[0:N_HEADS], axis=1, keepdims=True), db_ref.shape)

    return pl.pallas_call(
        body, grid=(n_batch,),
        in_specs=[pl.BlockSpec((None, N_HEADS, s), lambda b: (b, 0, 0)), pl.BlockSpec((None, N_HEADS, s), lambda b: (b, 0, 0))],
        out_specs=[pl.BlockSpec((s, GATE_PAD), lambda b: (b, 0)), pl.BlockSpec((N_HEADS, LANES), lambda b: (0, 0))],
        out_shape=[jax.ShapeDtypeStruct((n_batch * s, GATE_PAD), BF16), jax.ShapeDtypeStruct((N_HEADS, LANES), F32)],
        scratch_shapes=[pltpu.VMEM((LANES, s), F32)],
        name=name, compiler_params=_params("arbitrary"),
    )(dc, sg)


FOX_BQ = 512
FOX_BK = 512
PAIR_WIDTH = 3 * LANES
N_PAIRS = N_HEADS // 2


def _pair_major(w):
    return w.reshape(w.shape[0], 3, N_PAIRS, LANES).transpose(0, 2, 1, 3).reshape(w.shape[0], 3 * MIX_HALF)


def _pair_major_inv(w):
    return w.reshape(w.shape[0], N_PAIRS, 3, LANES).transpose(0, 2, 1, 3).reshape(w.shape[0], 3 * MIX_HALF)


def _causal(i, j, bq, bk):
    qpos = i * bq + lax.broadcasted_iota(jnp.int32, (bq, 1), 0)
    kpos = j * bk + lax.broadcasted_iota(jnp.int32, (1, bk), 1)
    return kpos <= qpos


def _split_bf16(p):
    hi = p.astype(BF16)
    return hi, (p - hi.astype(F32)).astype(BF16)


def _fox_fwd(zf, c_bc, c_row, n_batch, name):
    s, bq, bk = SEQ, FOX_BQ, FOX_BK
    nq = s // bq
    t = n_batch * s

    def body(q_ref, k_ref, v_ref, cq_ref, cr_ref, o_ref, o32_ref, lse_ref):
        hp, i = pl.program_id(1), pl.program_id(2)
        q = q_ref[...] * ATT_SCALE
        qh = [jnp.where(_head_mask(e), q, jnp.zeros_like(q)) for e in range(2)]
        cq = [cq_ref[:, HEAD_DIM * e:HEAD_DIM * e + 1] for e in range(2)]

        def step(j, carry, masked):
            rows = pl.ds(pl.multiple_of(j * bk, bk), bk)
            kj, vj = k_ref[rows, :], v_ref[rows, :]
            out = []
            for e in range(2):
                m, l, acc = carry[3 * e:3 * e + 3]
                sc = _dot_nt(qh[e], kj) + (cq[e] - cr_ref[pl.ds(2 * hp + e, 1), rows])
                if masked:
                    sc = jnp.where(_causal(i, j, bq, bk), sc, NEG)
                m_new = jnp.maximum(m, jnp.max(sc, axis=1, keepdims=True))
                alpha = jnp.exp(m - m_new)
                p = jnp.exp(sc - m_new)
                p_hi, p_lo = _split_bf16(p)
                out += [m_new, alpha * l + jnp.sum(p, axis=1, keepdims=True), alpha * acc + (_dot(p_hi, vj) + _dot(p_lo, vj))]
            return tuple(out)

        init = (jnp.full((bq, 1), NEG, F32), jnp.zeros((bq, 1), F32), jnp.zeros((bq, LANES), F32)) * 2
        n_clear = (i * bq) // bk
        carry = lax.fori_loop(0, n_clear, functools.partial(step, masked=False), init)
        carry = lax.fori_loop(n_clear, (i * bq + bq + bk - 1) // bk, functools.partial(step, masked=True), carry)
        outs = [carry[3 * e + 2] / carry[3 * e + 1] for e in range(2)]
        lses = [carry[3 * e] + jnp.log(carry[3 * e + 1]) for e in range(2)]
        o = jnp.where(_head_mask(0), outs[0], outs[1])
        o_ref[...] = o.astype(BF16)
        o32_ref[...] = o
        lse_ref[...] = jnp.where(_head_mask(0), lses[0], lses[1])

    def col(c0):
        return lambda b, hp, i: (b, 3 * hp + c0)

    blk = pl.BlockSpec((bq, LANES), lambda b, hp, i: (b * nq + i, hp))
    return pl.pallas_call(
        body, grid=(n_batch, N_PAIRS, nq),
        in_specs=[pl.BlockSpec((bq, LANES), lambda b, hp, i: (b * nq + i, 3 * hp)),
                  pl.BlockSpec((s, LANES), col(1)), pl.BlockSpec((s, LANES), col(2)), blk,
                  pl.BlockSpec((None, N_HEADS, s), lambda b, hp, i: (b, 0, 0))],
        out_specs=[blk, blk, blk],
        out_shape=[jax.ShapeDtypeStruct((t, MIX_HALF), BF16), jax.ShapeDtypeStruct((t, MIX_HALF), F32),
                   jax.ShapeDtypeStruct((t, MIX_HALF), F32)],
        name=name, compiler_params=_params("parallel", "parallel", "arbitrary"),
    )(zf, zf, zf, c_bc, c_row)


def _fox_bwd(zf, o32, dy, lse, c_bc, c_row, dz, n_batch, name):
    s, bq, bk = SEQ, FOX_BQ, FOX_BK
    nq, nk = s // bq, s // bk

    def body(q_ref, k_ref, v_ref, o_ref, do_ref, lse_ref, cq_ref, cr_ref, dz_in, dz_ref, dc_ref, dq_acc):
        del dz_in
        hp, j = pl.program_id(1), pl.program_id(2)

        @pl.when(j == 0)
        def _():
            dq_acc[...] = jnp.zeros_like(dq_acc)

        kj, vj = k_ref[...], v_ref[...]
        cols = pl.ds(pl.multiple_of(j * bk, bk), bk)
        km = [jnp.where(_head_mask(e), kj, jnp.zeros_like(kj)) for e in range(2)]
        ck = [cr_ref[pl.ds(2 * hp + e, 1), cols] for e in range(2)]

        def step(i, carry, masked):
            rows = pl.ds(pl.multiple_of(i * bq, bq), bq)
            qi, doi = q_ref[rows, :] * ATT_SCALE, do_ref[rows, :]
            prod = doi.astype(F32) * o_ref[rows, :]
            out = []
            dq = jnp.zeros((bq, LANES), F32)
            for e in range(2):
                dk_a, dv_a, dc_a = carry[3 * e:3 * e + 3]
                mask = _head_mask(e)
                lane0 = HEAD_DIM * e
                dom = jnp.where(mask, doi, jnp.zeros_like(doi))
                delta = jnp.sum(jnp.where(mask, prod, 0.0), axis=1, keepdims=True)
                sc = _dot_nt(qi, km[e]) + (cq_ref[rows, lane0:lane0 + 1] - ck[e])
                if masked:
                    sc = jnp.where(_causal(i, j, bq, bk), sc, NEG)
                p = jnp.exp(sc - lse_ref[rows, lane0:lane0 + 1])
                ds = p * (_dot_nt(dom, vj) - delta)
                dsb = ds.astype(BF16)
                dq = dq + _dot(dsb, km[e])
                out += [dk_a + _dot_tn(dsb, qi), dv_a + _dot_tn(p.astype(BF16), dom), dc_a - jnp.sum(ds, axis=0, keepdims=True)]
            dq_acc[rows, :] += dq * ATT_SCALE
            return tuple(out)

        init = (jnp.zeros((bk, LANES), F32), jnp.zeros((bk, LANES), F32), jnp.zeros((1, bk), F32)) * 2
        first = (j * bk) // bq
        n_diag = (j * bk + bk + bq - 1) // bq
        carry = lax.fori_loop(first, n_diag, functools.partial(step, masked=True), init)
        carry = lax.fori_loop(n_diag, nq, functools.partial(step, masked=False), carry)
        for e in range(2):
            dc_ref[e:e + 1, :] = carry[3 * e + 2]
        dz_ref[cols, LANES:2 * LANES] = jnp.where(_head_mask(0), carry[0], carry[3]).astype(BF16)
        dz_ref[cols, 2 * LANES:3 * LANES] = (carry[1] + carry[4]).astype(BF16)

        @pl.when(j == nk - 1)
        def _():
            dz_ref[:, 0:LANES] = dq_acc[...].astype(BF16)

    def seq(idx):
        return pl.BlockSpec((s, LANES), lambda b, hp, j: (b, idx(hp)))

    def kblk(c0):
        return pl.BlockSpec((bk, LANES), lambda b, hp, j: (b * nk + j, 3 * hp + c0))

    return pl.pallas_call(
        body, grid=(n_batch, N_PAIRS, nk),
        in_specs=[seq(lambda hp: 3 * hp), kblk(1), kblk(2), seq(lambda hp: hp), seq(lambda hp: N_PAIRS + hp),
                  seq(lambda hp: hp), seq(lambda hp: hp),
                  pl.BlockSpec((None, N_HEADS, s), lambda b, hp, j: (b, 0, 0)), pl.BlockSpec(memory_space=pl.ANY)],
        out_specs=[pl.BlockSpec((s, PAIR_WIDTH), lambda b, hp, j: (b, N_PAIRS + hp)),
                   pl.BlockSpec((None, None, 2, bk), lambda b, hp, j: (b, hp, 0, j))],
        out_shape=[jax.ShapeDtypeStruct(dz.shape, dz.dtype), jax.ShapeDtypeStruct((n_batch, N_PAIRS, 2, s), F32)],
        scratch_shapes=[pltpu.VMEM((s, LANES), F32)],
        input_output_aliases={8: 0},
        name=name, compiler_params=_params("parallel", "parallel", "arbitrary"),
    )(zf, zf, zf, o32, dy, lse, c_bc, c_row, dz)


def _dil_bias(slope, dil):
    qi = lax.broadcasted_iota(jnp.int32, (BLOCK, 2 * BLOCK), 0)
    kj = lax.broadcasted_iota(jnp.int32, (BLOCK, 2 * BLOCK), 1)
    delta = qi + BLOCK - kj
    return jnp.where((delta >= 0) & (delta <= BLOCK), (-slope * dil) * delta.astype(F32), NEG)


def _alibi_slope(hp, e):
    slope = jnp.float32(0.0)
    for k in range(N_PAIRS):
        slope = jnp.where(hp == k, jnp.float32(2.0 ** -(2 * k + e + 1)), slope)
    return slope


def _fill_bias(bias_scr, hp):
    for di, dil in enumerate(DILATIONS):
        for e in range(2):
            bias_scr[2 * di + e] = _dil_bias(_alibi_slope(hp, e), dil)


def _pair_specs(rows):
    return [pl.BlockSpec((rows, LANES), lambda b, hp, c0=c0: (b, 3 * hp + c0)) for c0 in range(3)]


def _strided(start, size, dil):
    return pl.ds(start, size) if dil == 1 else pl.ds(start, size, stride=dil)


def _for_each_block(dil, unit):
    span = BLOCK * dil
    nb = SEQ // span
    if dil == 1:
        group = 3
        assert (nb - 1) % group == 0
        unit(0, True)

        def later(g, c):
            for u in range(group):
                unit((1 + g * group + u) * span, False)
            return c

        lax.fori_loop(0, (nb - 1) // group, later, 0)
        return
    group = 4
    per = dil // group

    def firsts(g, c):
        for u in range(group):
            unit(g * group + u, True)
        return c

    lax.fori_loop(0, per, firsts, 0)
    if nb > 1:
        def later(i, c):
            for u in range(group):
                unit((1 + i // per) * span + (i % per) * group + u, False)
            return c

        lax.fori_loop(0, (nb - 1) * per, later, 0)


def _mix_weights(l1, l2, l3):
    m = jnp.maximum(jnp.maximum(l1, l2), l3)
    e1, e2, e3 = jnp.exp(l1 - m), jnp.exp(l2 - m), jnp.exp(l3 - m)
    inv = 1.0 / (e1 + e2 + e3)
    return e1 * inv, e2 * inv, e3 * inv


def _dil_fwd(zd, n_batch, name):
    s = SEQ
    t = n_batch * s

    def body(q_ref, k_ref, v_ref, y_ref, l1_ref, l2_ref, l3_ref, o_scr, bias_scr):
        _fill_bias(bias_scr, pl.program_id(1))
        lse_refs = (l1_ref, l2_ref, l3_ref)
        for di, dil in enumerate(DILATIONS):

            def unit(start, first, di=di, dil=dil):
                qrows = _strided(start, BLOCK, dil)
                krows = qrows if first else _strided(start - BLOCK * dil, 2 * BLOCK, dil)
                q = (q_ref[qrows, :] * ATT_SCALE).astype(BF16)
                kc = k_ref[krows, :].astype(BF16)
                vc = v_ref[krows, :].astype(BF16)
                outs, lses = [], []
                for e in range(2):
                    bias = bias_scr[2 * di + e]
                    sc = _dot_nt(jnp.where(_head_mask(e), q, jnp.zeros_like(q)), kc) + (bias[:, BLOCK:] if first else bias)
                    m = jnp.max(sc, axis=1, keepdims=True)
                    pe = jnp.exp(sc - m)
                    l = jnp.sum(pe, axis=1, keepdims=True)
                    outs.append(_dot((pe * (1.0 / l)).astype(BF16), vc))
                    lses.append(m + jnp.log(l))
                o_scr.at[di][qrows, :] = jnp.where(_head_mask(0), outs[0], outs[1])
                lse_refs[di][qrows, :] = jnp.where(_head_mask(0), lses[0], lses[1])

            _for_each_block(dil, unit)
        w = _mix_weights(l1_ref[...], l2_ref[...], l3_ref[...])
        y_ref[...] = (w[0] * o_scr[0] + w[1] * o_scr[1] + w[2] * o_scr[2]).astype(BF16)

    blk = pl.BlockSpec((s, LANES), lambda b, hp: (b, hp))
    res = pl.pallas_call(
        body, grid=(n_batch, N_PAIRS),
        in_specs=_pair_specs(s),
        out_specs=[blk] * 4,
        out_shape=[jax.ShapeDtypeStruct((t, MIX_HALF), BF16)] + [jax.ShapeDtypeStruct((t, MIX_HALF), F32)] * 3,
        scratch_shapes=[pltpu.VMEM((3, s, LANES), F32), pltpu.VMEM((6, BLOCK, 2 * BLOCK), F32)],
        name=name, compiler_params=_params("parallel", "arbitrary"),
    )(zd, zd, zd)
    return res[0], res[1:]


def _dil_bwd(zd, dy, ya, lses, n_batch, name):
    s = SEQ
    t = n_batch * s

    def body(q_ref, k_ref, v_ref, dy_ref, ya_ref, l1_ref, l2_ref, l3_ref, dz_ref, w_scr, dy_scr, dot_scr, acc, bias_scr):
        _fill_bias(bias_scr, pl.program_id(1))
        for di, w in enumerate(_mix_weights(l1_ref[...], l2_ref[...], l3_ref[...])):
            w_scr[di] = w
        dya = dy_ref[...].astype(F32)
        prod = dya * ya_ref[...].astype(F32)
        per_head = [jnp.sum(jnp.where(_head_mask(e), prod, 0.0), axis=1, keepdims=True) for e in range(2)]
        dy_scr[...] = dya
        dot_scr[...] = jnp.where(_head_mask(0), per_head[0], per_head[1])
        acc[...] = jnp.zeros_like(acc)
        lse_refs = (l1_ref, l2_ref, l3_ref)
        for di, dil in enumerate(DILATIONS):

            def unit(start, first, di=di, dil=dil):
                qrows = _strided(start, BLOCK, dil)
                krows = qrows if first else _strided(start - BLOCK * dil, 2 * BLOCK, dil)
                q = (q_ref[qrows, :] * ATT_SCALE).astype(BF16)
                kc = k_ref[krows, :].astype(BF16)
                vc = v_ref[krows, :].astype(BF16)
                wq = w_scr.at[di][qrows, :]
                do = (wq * dy_scr[qrows, :]).astype(BF16)
                sub = wq * dot_scr[qrows, :]
                lse = lse_refs[di][qrows, :]
                dq = jnp.zeros((BLOCK, LANES), F32)
                dk = jnp.zeros((krows.size, LANES), F32)
                dv = jnp.zeros((krows.size, LANES), F32)
                for e in range(2):
                    mask = _head_mask(e)
                    lane0 = HEAD_DIM * e
                    qh = jnp.where(mask, q, jnp.zeros_like(q))
                    doh = jnp.where(mask, do, jnp.zeros_like(do))
                    bias = bias_scr[2 * di + e]
                    sc = _dot_nt(qh, kc) + (bias[:, BLOCK:] if first else bias)
                    p = jnp.exp(sc - lse[:, lane0:lane0 + 1])
                    dsb = (p * (_dot_nt(doh, vc) - sub[:, lane0:lane0 + 1])).astype(BF16)
                    dq = dq + _dot(dsb, jnp.where(mask, kc, jnp.zeros_like(kc)))
                    dk = dk + _dot_tn(dsb, qh)
                    dv = dv + _dot_tn(p.astype(BF16), doh)
                acc.at[0][qrows, :] += dq * ATT_SCALE
                acc.at[1][krows, :] += dk
                acc.at[2][krows, :] += dv

            _for_each_block(dil, unit)
        for k in range(3):
            dz_ref[:, k * LANES:(k + 1) * LANES] = acc[k].astype(BF16)

    blk = pl.BlockSpec((s, LANES), lambda b, hp: (b, hp))
    pair = pl.BlockSpec((s, PAIR_WIDTH), lambda b, hp: (b, hp))
    return pl.pallas_call(
        body, grid=(n_batch, N_PAIRS),
        in_specs=_pair_specs(s) + [blk] * 5,
        out_specs=pair,
        out_shape=jax.ShapeDtypeStruct((t, 2 * 3 * MIX_HALF), BF16),
        scratch_shapes=[pltpu.VMEM((3, s, LANES), F32), pltpu.VMEM((s, LANES), F32), pltpu.VMEM((s, LANES), F32),
                        pltpu.VMEM((3, s, LANES), F32), pltpu.VMEM((6, BLOCK, 2 * BLOCK), F32)],
        name=name, compiler_params=_params("parallel", "arbitrary"),
    )(zd, zd, zd, dy, ya, *lses)


X_BQ = 512


def _xattn_probs(q, k):
    sc = _dot_nt(q, k) * X_SCALE
    pe = jnp.exp(sc - jnp.max(sc, axis=1, keepdims=True))
    return pe / jnp.sum(pe, axis=1, keepdims=True)


def _xattn_fwd(qx, kx, vx, n_batch, name):
    nq = SEQ // X_BQ

    def body(q_ref, k_ref, v_ref, o_ref):
        p = _xattn_probs(q_ref[...], k_ref[...])
        o_ref[...] = _dot(p.astype(BF16), v_ref[...]).astype(BF16)

    qblk = pl.BlockSpec((X_BQ, X_HEAD_DIM), lambda b, h, i: (b * nq + i, h))
    kblk = pl.BlockSpec((N_MEM, X_HEAD_DIM), lambda b, h, i: (b, h))
    return pl.pallas_call(
        body, grid=(n_batch, X_HEADS, nq), in_specs=[qblk, kblk, kblk], out_specs=qblk,
        out_shape=jax.ShapeDtypeStruct(qx.shape, BF16),
        name=name, compiler_params=_params("parallel", "parallel", "arbitrary"),
    )(qx, kx, vx)


def _xattn_bwd(qx, kx, vx, dox, n_batch, name):
    nq = SEQ // X_BQ

    def body(q_ref, k_ref, v_ref, do_ref, dq_ref, dk_ref, dv_ref, dk_acc, dv_acc):
        i = pl.program_id(2)

        @pl.when(i == 0)
        def _():
            dk_acc[...] = jnp.zeros_like(dk_acc)
            dv_acc[...] = jnp.zeros_like(dv_acc)

        q, k, do = q_ref[...], k_ref[...], do_ref[...]
        p = _xattn_probs(q, k)
        dp = _dot_nt(do, v_ref[...])
        dsb = (p * (dp - jnp.sum(p * dp, axis=1, keepdims=True))).astype(BF16)
        dq_ref[...] = (_dot(dsb, k) * X_SCALE).astype(BF16)
        dk_acc[...] += _dot_tn(dsb, q) * X_SCALE
        dv_acc[...] += _dot_tn(p.astype(BF16), do)

        @pl.when(i == nq - 1)
        def _():
            dk_ref[...] = dk_acc[...].astype(BF16)
            dv_ref[...] = dv_acc[...].astype(BF16)

    qblk = pl.BlockSpec((X_BQ, X_HEAD_DIM), lambda b, h, i: (b * nq + i, h))
    kblk = pl.BlockSpec((N_MEM, X_HEAD_DIM), lambda b, h, i: (b, h))
    return pl.pallas_call(
        body, grid=(n_batch, X_HEADS, nq), in_specs=[qblk, kblk, kblk, qblk], out_specs=[qblk, kblk, kblk],
        out_shape=[jax.ShapeDtypeStruct(qx.shape, BF16), jax.ShapeDtypeStruct(kx.shape, BF16), jax.ShapeDtypeStruct(kx.shape, BF16)],
        scratch_shapes=[pltpu.VMEM((N_MEM, X_HEAD_DIM), F32)] * 2,
        name=name, compiler_params=_params("parallel", "parallel", "arbitrary"),
    )(qx, kx, vx, dox)


def _adamw(w, g, m, v, name, rows):
    r, c = w.shape
    assert r % rows == 0, (name, w.shape, rows)

    def body(w_ref, g_ref, m_ref, v_ref, d_ref, nm_ref, nv_ref):
        gv = g_ref[...]
        m1 = ADAM_B1 * m_ref[...] + (1.0 - ADAM_B1) * gv
        v1 = ADAM_B2 * v_ref[...] + (1.0 - ADAM_B2) * jnp.square(gv)
        m_hat = m1 / (1.0 - ADAM_B1 ** ADAM_STEP)
        v_hat = v1 / (1.0 - ADAM_B2 ** ADAM_STEP)
        d_ref[...] = -ADAM_LR * (m_hat / (jnp.sqrt(v_hat) + ADAM_EPS) + ADAM_WD * w_ref[...])
        nm_ref[...] = m1
        nv_ref[...] = v1

    blk = pl.BlockSpec((rows, c), lambda i: (i, 0))
    return pl.pallas_call(
        body, grid=(r // rows,), in_specs=[blk] * 4, out_specs=[blk] * 3,
        out_shape=[jax.ShapeDtypeStruct((r, c), F32)] * 3,
        name=name, compiler_params=_params("arbitrary"),
    )(w, g, m, v)


def _relu2(acc):
    a = jnp.maximum(acc, 0.0)
    return acc, a * a


def _relu2_bwd(acc, u):
    return (2.0 * jnp.maximum(u.astype(F32), 0.0) * acc,)


def _local_step(x, mem, target, vecs, wts):
    n_batch = x.shape[0]
    t = n_batch * SEQ
    x0 = x.reshape(t, D_MODEL)
    mem2 = mem.reshape(n_batch * N_MEM, D_MODEL)
    tgt = target.reshape(t, D_MODEL)

    w_in = wts["w_in"]
    half = 3 * MIX_HALF
    w_dil, w_fox = _pair_major(w_in[:, :half]), _pair_major(w_in[:, half:QKV_WIDTH])
    w_gate = jnp.pad(w_in[:, QKV_WIDTH:], ((0, 0), (0, GATE_PAD - N_HEADS)))
    w_out = wts["w_out"]
    b_pad = jnp.pad(vecs["b_forget"], (0, GATE_PAD - N_HEADS)).reshape(1, GATE_PAD)

    h1 = _rmsnorm(x0, vecs["g_mix"], "norm_mix")
    mn = _rmsnorm(mem2, vecs["g_mem"], "norm_mem")
    zd = _matmul(h1, w_dil, "in_dil", tn=768)[0]
    zf = _matmul(h1, w_fox, "in_fox", out_dtypes=(BF16,), tn=768)[0]
    gate = _matmul(h1, w_gate, "in_gate")[0]
    c_bc, c_row, sg = _gate_fwd(gate, b_pad, n_batch, "gate_fwd")
    ya, lses = _dil_fwd(zd, n_batch, "dil_fwd")
    yf, of32, lse_f = _fox_fwd(zf, c_bc, c_row, n_batch, "fox_fwd")
    x1 = _matmul_res(ya, w_out[:MIX_HALF], x0, "out_a")
    x1 = _matmul_res(yf, w_out[MIX_HALF:], x1, "out_f")
    h2 = _rmsnorm(x1, vecs["g_xattn"], "norm_xattn")
    qx = _matmul(h2, wts["w_xq"], "xq", out_dtypes=(BF16,))[0]
    kx = _matmul(mn, wts["w_xk"], "xk", out_dtypes=(BF16,))[0]
    vx = _matmul(mn, wts["w_xv"], "xv", out_dtypes=(BF16,))[0]
    ox = _xattn_fwd(qx, kx, vx, n_batch, "xattn_fwd")
    x2 = _matmul_res(ox, wts["w_xo"], x1, "xo")
    h3 = _rmsnorm(x2, vecs["g_mlp"], "norm_mlp")
    u, a2 = _matmul(h3, wts["w_up"], "mlp_up", out_dtypes=(BF16, BF16), epilogue=_relu2)
    x3 = _matmul_res(a2, wts["w_down"], x2, "mlp_down")
    loss, dx3, dx3b, dg_final = _loss_bwd(x3, vecs["g_final"], tgt, "loss")

    du = _matmul(dx3b, wts["w_down"].T, "mlp_down_bwd", out_dtypes=(BF16,), extras=(u,), epilogue=_relu2_bwd)[0]
    gw_down = _matmul_tn(a2, dx3b, "gw_down")
    gw_up = _matmul_tn(h3, du, "gw_up")
    dh3 = _matmul(du, wts["w_up"].T, "mlp_up_bwd")[0]
    dx2, dx2b, dg_mlp = _rms_bwd(x2, dh3, vecs["g_mlp"], dx3, "norm_mlp_bwd")

    gw_xo = _matmul_tn(ox, dx2b, "gw_xo")
    dox = _matmul(dx2b, wts["w_xo"].T, "xo_bwd", out_dtypes=(BF16,))[0]
    dqx, dkx, dvx = _xattn_bwd(qx, kx, vx, dox, n_batch, "xattn_bwd")
    gw_xq = _matmul_tn(h2, dqx, "gw_xq")
    gw_xk = _matmul_tn(mn, dkx, "gw_xk")
    gw_xv = _matmul_tn(mn, dvx, "gw_xv")
    dh2 = _matmul(dqx, wts["w_xq"].T, "xq_bwd")[0]
    dmn = _matmul(dkx, wts["w_xk"].T, "xk_bwd")[0]
    dmn = _matmul_res(dvx, wts["w_xv"].T, dmn, "xv_bwd")
    _, _, dg_mem = _rms_bwd(mem2, dmn, vecs["g_mem"], None, "norm_mem_bwd")
    dx1, dx1b, dg_xattn = _rms_bwd(x1, dh2, vecs["g_xattn"], dx2, "norm_xattn_bwd")

    gw_out = jnp.concatenate([_matmul_tn(ya, dx1b, "gw_out_a"), _matmul_tn(yf, dx1b, "gw_out_f")], axis=0)
    dy = _matmul(dx1b, w_out.T, "out_bwd", out_dtypes=(BF16,))[0]
    dz = _dil_bwd(zd, dy, ya, lses, n_batch, "dil_bwd")
    dz, dc = _fox_bwd(zf, of32, dy, lse_f, c_bc, c_row, dz, n_batch, "fox_bwd")
    dzg, db = _gate_bwd(dc.reshape(n_batch, N_HEADS, SEQ), sg, "gate_bwd")
    gw_pm = _matmul_tn(h1, dz, "gw_in_qkv")
    gw_in = jnp.concatenate([_pair_major_inv(gw_pm[:, :half]), _pair_major_inv(gw_pm[:, half:]),
                             _matmul_tn(h1, dzg, "gw_in_gate")[:, :N_HEADS]], axis=1)
    dh1 = _matmul(dz, jnp.concatenate([w_dil, w_fox], axis=1).T, "in_qkv_bwd")[0]
    dh1 = _matmul_res(dzg, w_gate.T, dh1, "in_gate_bwd")
    dx0, _, dg_mix = _rms_bwd(x0, dh1, vecs["g_mix"], dx1, "norm_mix_bwd")

    gw = dict(w_in=gw_in, w_out=gw_out, w_xq=gw_xq, w_xk=gw_xk, w_xv=gw_xv, w_xo=gw_xo, w_up=gw_up, w_down=gw_down)
    gv = dict(g_mix=dg_mix, g_xattn=dg_xattn, g_mem=dg_mem, g_mlp=dg_mlp, g_final=dg_final, b_forget=db)
    return loss, dx0.reshape(x.shape), gw, gv


MESH = pl.DeviceIdType.MESH
ANY = pl.BlockSpec(memory_space=pl.ANY)


def _place():
    x, y, c = lax.axis_index("x"), lax.axis_index("y"), lax.axis_index("c")
    other_chips = [(1 - x, y), (x, 1 - y), (1 - x, 1 - y)]
    return x, y, c, other_chips


def _my_chip():
    return 2 * lax.axis_index("x") + lax.axis_index("y")


def _gather_weights(pack):
    def body(p_ref, out_ref, send_sems, recv_sems, pass_send, pass_recv):
        x, y, c, chips = _place()
        me = 2 * x + y
        mine = pl.ds(pl.multiple_of(c * PACK_HALF, 16), PACK_HALF)
        theirs = pl.ds(pl.multiple_of((1 - c) * PACK_HALF, 16), PACK_HALF)

        def from_chip(k, chip, rows):
            src = out_ref.at[2 * chip[0] + chip[1], rows]
            return pltpu.make_async_remote_copy(src_ref=src, dst_ref=src, send_sem=send_sems.at[k], recv_sem=recv_sems.at[k],
                                                device_id=(chip[0], chip[1], c), device_id_type=MESH)

        def passed(k, chip, rows):
            src = out_ref.at[2 * chip[0] + chip[1], rows]
            return pltpu.make_async_remote_copy(src_ref=src, dst_ref=src, send_sem=pass_send.at[k], recv_sem=pass_recv.at[k],
                                                device_id=(x, y, 1 - c), device_id_type=MESH)

        sends = []
        for k, chip in enumerate(chips):
            cp = pltpu.make_async_remote_copy(src_ref=p_ref.at[mine], dst_ref=out_ref.at[me, mine], send_sem=send_sems.at[k],
                                              recv_sem=recv_sems.at[k], device_id=(chip[0], chip[1], c), device_id_type=MESH)
            cp.start()
            sends.append(cp)
        for k, chip in enumerate(chips):
            from_chip(k, chip, mine).wait_recv()
            cp = passed(k, chip, mine)
            cp.start()
            sends.append(cp)
        for k, chip in enumerate(chips):
            passed(k, chip, theirs).wait_recv()
        for cp in sends:
            cp.wait_send()

    wall = pl.pallas_call(
        body, in_specs=[ANY], out_specs=ANY,
        out_shape=jax.ShapeDtypeStruct((N_CHIPS,) + pack.shape, pack.dtype),
        scratch_shapes=[pltpu.SemaphoreType.DMA((3,))] * 4,
        name="gather_weights",
    )(pack)
    return lax.dynamic_update_slice(wall, pack[None], (_my_chip(), 0, 0))


def _swap_halves(g):
    def body(g_ref, out_ref, send_sem, recv_sem):
        x, y, c, _ = _place()
        theirs = pl.ds(pl.multiple_of((1 - c) * PACK_HALF, 8), PACK_HALF)
        cp = pltpu.make_async_remote_copy(src_ref=g_ref.at[:, theirs], dst_ref=out_ref, send_sem=send_sem, recv_sem=recv_sem,
                                          device_id=(x, y, 1 - c), device_id_type=MESH)
        cp.start()
        cp.wait()

    return pl.pallas_call(
        body, in_specs=[ANY], out_specs=ANY,
        out_shape=jax.ShapeDtypeStruct((N_CHIPS, PACK_HALF, D_MODEL), F32),
        scratch_shapes=[pltpu.SemaphoreType.DMA, pltpu.SemaphoreType.DMA],
        name="swap_halves",
    )(g)


PACK_TILE = 688


def _core_index():
    return lax.axis_index("c").astype(jnp.int32).reshape(1)


def _add_sibling(g, got):
    n_tiles = PACK_HALF // PACK_TILE

    def body(c_ref, g_ref, got_ref, o_ref):
        o_ref[...] = (g_ref[...] + got_ref[...]).astype(BF16)

    blk = pl.BlockSpec((None, PACK_TILE, D_MODEL), lambda s, i, c_ref: (s, i, 0))
    return pl.pallas_call(
        body,
        grid_spec=pltpu.PrefetchScalarGridSpec(
            num_scalar_prefetch=1, grid=(N_CHIPS, n_tiles),
            in_specs=[pl.BlockSpec((None, PACK_TILE, D_MODEL), lambda s, i, c_ref: (s, c_ref[0] * n_tiles + i, 0)), blk],
            out_specs=blk),
        out_shape=jax.ShapeDtypeStruct((N_CHIPS, PACK_HALF, D_MODEL), BF16),
        name="add_sibling", compiler_params=_params("arbitrary", "arbitrary"),
    )(_core_index(), g, got)


def _exchange_chips(part):
    def body(p_ref, out_ref, send_sems, recv_sems):
        x, y, c, chips = _place()
        me = 2 * x + y
        sends = []
        for k, chip in enumerate(chips):
            cp = pltpu.make_async_remote_copy(src_ref=p_ref.at[2 * chip[0] + chip[1]], dst_ref=out_ref.at[me],
                                              send_sem=send_sems.at[k], recv_sem=recv_sems.at[k],
                                              device_id=(chip[0], chip[1], c), device_id_type=MESH)
            cp.start()
            sends.append(cp)
        for k, chip in enumerate(chips):
            slab = out_ref.at[2 * chip[0] + chip[1]]
            pltpu.make_async_remote_copy(src_ref=slab, dst_ref=slab, send_sem=send_sems.at[k], recv_sem=recv_sems.at[k],
                                         device_id=(chip[0], chip[1], c), device_id_type=MESH).wait_recv()
        for cp in sends:
            cp.wait_send()

    got = pl.pallas_call(
        body, in_specs=[ANY], out_specs=ANY,
        out_shape=jax.ShapeDtypeStruct(part.shape, part.dtype),
        scratch_shapes=[pltpu.SemaphoreType.DMA((3,)), pltpu.SemaphoreType.DMA((3,))],
        name="exchange_chips",
    )(part)
    me = _my_chip()
    return lax.dynamic_update_slice(got, lax.dynamic_slice(part, (me, 0, 0), (1,) + part.shape[1:]), (me, 0, 0))


def _sum_chips(parts):
    def body(c_ref, p0, p1, p2, p3, o_ref):
        f32 = lambda p: p[...].astype(F32)
        o_ref[...] = ((f32(p0) + f32(p1)) + f32(p2)) + f32(p3)

    def slab(s):
        return pl.BlockSpec((None, PACK_TILE, D_MODEL), lambda i, c_ref, s=s: (s, i, 0))

    return pl.pallas_call(
        body,
        grid_spec=pltpu.PrefetchScalarGridSpec(
            num_scalar_prefetch=1, grid=(PACK_HALF // PACK_TILE,),
            in_specs=[slab(s) for s in range(N_CHIPS)],
            out_specs=pl.BlockSpec((None, PACK_TILE, D_MODEL), lambda i, c_ref: (c_ref[0], i, 0))),
        out_shape=jax.ShapeDtypeStruct((2, PACK_HALF, D_MODEL), F32),
        name="sum_chips", compiler_params=_params("arbitrary"),
    )(_core_index(), parts, parts, parts, parts)


def _share_halves(halves):
    def body(h_ref, out_ref, send_sem, recv_sem):
        del h_ref
        x, y, c, _ = _place()
        cp = pltpu.make_async_remote_copy(src_ref=out_ref.at[c], dst_ref=out_ref.at[c], send_sem=send_sem, recv_sem=recv_sem,
                                          device_id=(x, y, 1 - c), device_id_type=MESH)
        cp.start()
        pltpu.make_async_remote_copy(src_ref=out_ref.at[1 - c], dst_ref=out_ref.at[1 - c], send_sem=send_sem, recv_sem=recv_sem,
                                     device_id=(x, y, 1 - c), device_id_type=MESH).wait_recv()
        cp.wait_send()

    return pl.pallas_call(
        body, in_specs=[ANY], out_specs=ANY,
        out_shape=jax.ShapeDtypeStruct(halves.shape, halves.dtype),
        scratch_shapes=[pltpu.SemaphoreType.DMA] * 2,
        input_output_aliases={0: 0},
        name="share_halves",
    )(halves)


def _reduce_scatter(g):
    part = _add_sibling(g, _swap_halves(g))
    halves = _sum_chips(_exchange_chips(part))
    return _share_halves(halves).reshape(PACK_ROWS, D_MODEL)


SMALL_ROWS = 8


def _allreduce_small(v):
    def body(v_ref, out_ref, buf, send_sems, recv_sems):
        x, y, c, _ = _place()
        buf[4 * x + 2 * y + c] = v_ref[...]
        sends = []
        for k in range(1, N_DEV):
            px = 1 - x if k & 4 else x
            py = 1 - y if k & 2 else y
            pc = 1 - c if k & 1 else c
            cp = pltpu.make_async_remote_copy(src_ref=v_ref, dst_ref=buf.at[4 * x + 2 * y + c], send_sem=send_sems.at[k - 1],
                                              recv_sem=recv_sems.at[k - 1], device_id=(px, py, pc), device_id_type=MESH)
            cp.start()
            sends.append((cp, 4 * px + 2 * py + pc))
        for k, (cp, peer) in enumerate(sends):
            pltpu.make_async_remote_copy(src_ref=v_ref, dst_ref=buf.at[peer], send_sem=send_sems.at[k], recv_sem=recv_sems.at[k],
                                         device_id=(x, y, c), device_id_type=MESH).wait_recv()
        for cp, _ in sends:
            cp.wait_send()
        total = buf[0]
        for d in range(1, N_DEV):
            total = total + buf[d]
        out_ref[...] = total

    vmem = pl.BlockSpec(memory_space=pltpu.VMEM)
    return pl.pallas_call(
        body, in_specs=[vmem], out_specs=vmem,
        out_shape=jax.ShapeDtypeStruct(v.shape, v.dtype),
        scratch_shapes=[pltpu.VMEM((N_DEV,) + v.shape, v.dtype), pltpu.SemaphoreType.DMA((N_DEV - 1,)),
                        pltpu.SemaphoreType.DMA((N_DEV - 1,))],
        name="allreduce_small",
    )(v)


MATRICES = ("w_in", "w_out", "w_xq", "w_xk", "w_xv", "w_xo", "w_up", "w_down")
VECTORS = ("g_mix", "g_xattn", "g_mem", "g_mlp", "g_final", "b_forget")
WEIGHT_ORDER = ("g_mix", "w_in", "b_forget", "w_out", "g_xattn", "g_mem", "w_xq", "w_xk", "w_xv", "w_xo",
                "g_mlp", "w_up", "w_down", "g_final")
W_IN_SHARD = IN_WIDTH // N_CHIPS
W_IN_ROWS = W_IN_SHARD
PACK_AT = {"w_in": 0, "w_out": 784, "w_xq": 1040, "w_xk": 1296, "w_xv": 1552, "w_xo": 1808, "w_up": 2064, "w_down": 3088}
PACK_LEN = {"w_in": W_IN_ROWS, "w_out": 256, "w_xq": 256, "w_xk": 256, "w_xv": 256, "w_xo": 256, "w_up": 1024, "w_down": 1024}
ADAM_ROWS = {"w_in": 440, "w_out": 256, "w_xq": 256, "w_xk": 256, "w_xv": 256, "w_xo": 256, "w_up": 256, "w_down": 256}


def _pack(parts):
    segs, pos = [], 0
    for name in MATRICES:
        nxt = PACK_AT[MATRICES[MATRICES.index(name) + 1]] if name != MATRICES[-1] else PACK_ROWS
        segs.append(jnp.pad(parts[name], ((0, nxt - pos - PACK_LEN[name]), (0, 0))))
        pos = nxt
    return jnp.concatenate(segs, axis=0)


def _seg(a, name):
    return a[..., PACK_AT[name]:PACK_AT[name] + PACK_LEN[name], :]


def _full_weights(wall):
    cols = lambda a: a.transpose(1, 0, 2).reshape(a.shape[1], -1)
    rows = lambda a: a.reshape(-1, a.shape[-1])
    out = {n: rows(_seg(wall, n)) for n in ("w_out", "w_xq", "w_xk", "w_xv", "w_xo", "w_down")}
    out["w_in"] = cols(_seg(wall, "w_in").reshape(N_CHIPS, D_MODEL, W_IN_SHARD))
    out["w_up"] = cols(_seg(wall, "w_up"))
    return out


def _shard_of(g, name, s):
    if name == "w_in":
        return g[:, s * W_IN_SHARD:(s + 1) * W_IN_SHARD].reshape(W_IN_ROWS, D_MODEL)
    if name == "w_up":
        return g[:, s * D_MODEL:(s + 1) * D_MODEL]
    n = PACK_LEN[name]
    return g[s * n:(s + 1) * n]


def kernel(x, mem, g_mix, w_in, b_forget, w_out, g_xattn, g_mem, w_xq, w_xk, w_xv, w_xo, g_mlp, w_up, w_down, g_final, loss_target, m_g_mix, m_w_in, m_b_forget, m_w_out, m_g_xattn, m_g_mem, m_w_xq, m_w_xk, m_w_xv, m_w_xo, m_g_mlp, m_w_up, m_w_down, m_g_final, v_g_mix, v_w_in, v_b_forget, v_w_out, v_g_xattn, v_g_mem, v_w_xq, v_w_xk, v_w_xv, v_w_xo, v_g_mlp, v_w_up, v_w_down, v_g_final):
    given = dict(locals())
    weights = {n: given[n] for n in WEIGHT_ORDER}
    vecs = {n: weights[n] for n in VECTORS}

    shard = {n: weights[n].astype(BF16) for n in MATRICES}
    shard["w_in"] = shard["w_in"].reshape(W_IN_ROWS, D_MODEL)
    wts = _full_weights(_gather_weights(_pack(shard)))

    loss, grad_x, gw, gv = _local_step(x, mem, loss_target, vecs, wts)

    g_all = jnp.stack([_pack({n: _shard_of(gw[n], n, s) for n in MATRICES}) for s in range(N_CHIPS)])
    red = _reduce_scatter(g_all)
    grads = {n: _seg(red, n).reshape(weights[n].shape) for n in MATRICES}

    row = lambda a: jnp.pad(a.reshape(-1), (0, D_MODEL - a.size)).reshape(1, D_MODEL)
    small = jnp.concatenate([gv[n] for n in VECTORS[:5]] + [row(gv["b_forget"][:, 0]), row(loss[0, :1]),
                             jnp.zeros((1, D_MODEL), F32)], axis=0)
    small = _allreduce_small(small)
    for k, n in enumerate(VECTORS[:5]):
        grads[n] = small[k]
    grads["b_forget"] = small[5, :N_HEADS]
    loss_total = small[6, 0]

    delta, new_m, new_v = {}, {}, {}
    for n in MATRICES:
        flat = lambda a, n=n: a.reshape(-1, 256 if n == "w_in" else D_MODEL)
        d, m1, v1 = _adamw(flat(weights[n]), flat(grads[n]), flat(given["m_" + n]), flat(given["v_" + n]), "adamw_" + n, ADAM_ROWS[n])
        delta[n], new_m[n], new_v[n] = (a.reshape(weights[n].shape) for a in (d, m1, v1))
    stack = lambda prefix: jnp.concatenate([row(given[prefix + n]) for n in VECTORS] + [jnp.zeros((2, D_MODEL), F32)], axis=0)
    g_small = jnp.concatenate([small[:6], jnp.zeros((2, D_MODEL), F32)], axis=0)
    d, m1, v1 = _adamw(stack(""), g_small, stack("m_"), stack("v_"), "adamw_vectors", SMALL_ROWS)
    for k, n in enumerate(VECTORS):
        width = weights[n].shape[0]
        delta[n], new_m[n], new_v[n] = d[k, :width], m1[k, :width], v1[k, :width]

    return (loss_total, grad_x, *[grads[n] for n in WEIGHT_ORDER], *[delta[n] for n in WEIGHT_ORDER],
            *[new_m[n] for n in WEIGHT_ORDER], *[new_v[n] for n in WEIGHT_ORDER])
```

```python
import functools
import math

import jax
import jax.numpy as jnp
from jax import lax
from jax.experimental import pallas as pl
from jax.experimental.pallas import tpu as pltpu

F32 = jnp.float32
BF16 = jnp.bfloat16

D_MODEL = 1024
SEQ = 2048
N_MEM = 256
HEAD_DIM = 64
N_HEADS = 8
MIX_HALF = N_HEADS * HEAD_DIM
QKV_WIDTH = 6 * MIX_HALF
IN_WIDTH = QKV_WIDTH + N_HEADS
GATE_PAD = 128
BLOCK = 128
DILATIONS = (1, 4, 16)
X_HEADS = 4
X_HEAD_DIM = 256
D_FF = 4096
EPS = 1e-6
NEG = -1e30
ATT_SCALE = 1.0 / math.sqrt(HEAD_DIM)
X_SCALE = 1.0 / math.sqrt(X_HEAD_DIM)
LANES = 128
N_CHIPS = 4
N_DEV = 8

ADAM_LR = 0.001
ADAM_B1 = 0.9
ADAM_B2 = 0.999
ADAM_EPS = 1e-08
ADAM_WD = 0.01
ADAM_STEP = 10

VMEM_LIMIT = 48 * 1024 * 1024

PACK_SEGS = (("w_in", 770), ("w_out", 256), ("w_xq", 256), ("w_xk", 256), ("w_xv", 256),
             ("w_xo", 256), ("w_up", 1024), ("w_down", 1024))
PACK_ROWS = 4128
PACK_HALF = PACK_ROWS // 2


def _params(*sem):
    return pltpu.CompilerParams(dimension_semantics=sem or None, vmem_limit_bytes=VMEM_LIMIT)


def _dot(a, b):
    return jnp.dot(a, b, preferred_element_type=F32)


def _dot_nt(a, b):
    return lax.dot_general(a, b, (((1,), (1,)), ((), ())), preferred_element_type=F32)


def _dot_tn(a, b):
    return lax.dot_general(a, b, (((0,), (0,)), ((), ())), preferred_element_type=F32)


def _dot_exact(x, e):
    hi = x.astype(BF16)
    r1 = x - hi.astype(F32)
    mid = r1.astype(BF16)
    lo = (r1 - mid.astype(F32)).astype(BF16)
    return _dot(hi, e) + _dot(mid, e) + _dot(lo, e)


def _head_mask(e):
    lane = lax.broadcasted_iota(jnp.int32, (1, LANES), 1)
    return (lane >= HEAD_DIM * e) & (lane < HEAD_DIM * (e + 1))


def _matmul(a, w, name, out_dtypes=(F32,), extras=(), epilogue=None, tm=1024, tn=512, w_t=False):
    m, k = a.shape
    n = w.shape[0] if w_t else w.shape[1]
    tm, tn = min(tm, m), min(tn, n)
    assert m % tm == 0 and n % tn == 0, (name, a.shape, w.shape)
    n_ex = len(extras)

    def body(a_ref, w_ref, *rest):
        acc = (_dot_nt if w_t else _dot)(a_ref[...], w_ref[...])
        res = (acc,) if epilogue is None else epilogue(acc, *[r[...] for r in rest[:n_ex]])
        for o_ref, r in zip(rest[n_ex:], res):
            o_ref[...] = r.astype(o_ref.dtype)

    tile = pl.BlockSpec((tm, tn), lambda i, j: (i, j))
    w_spec = pl.BlockSpec((tn, k), lambda i, j: (j, 0)) if w_t else pl.BlockSpec((k, tn), lambda i, j: (0, j))
    return pl.pallas_call(
        body, grid=(m // tm, n // tn),
        in_specs=[pl.BlockSpec((tm, k), lambda i, j: (i, 0)), w_spec] + [tile] * n_ex,
        out_specs=[tile] * len(out_dtypes),
        out_shape=[jax.ShapeDtypeStruct((m, n), dt) for dt in out_dtypes],
        name=name, compiler_params=_params("parallel", "arbitrary"),
    )(a, w, *extras)


def _matmul_res(a, w, res, name, w_t=False):
    return _matmul(a, w, name, extras=(res,), epilogue=lambda acc, r: (r + acc,), w_t=w_t)[0]


def _matmul_tn(x, y, name, tm=1024, tn=1024, tk=512):
    t, m = x.shape
    _, n = y.shape
    tm, tn, tk = min(tm, m), min(tn, n), min(tk, t)
    assert m % tm == 0 and n % tn == 0 and t % tk == 0, (name, x.shape, y.shape)

    def body(x_ref, y_ref, o_ref):
        @pl.when(pl.program_id(2) == 0)
        def _():
            o_ref[...] = jnp.zeros_like(o_ref)

        o_ref[...] += _dot_tn(x_ref[...], y_ref[...])

    return pl.pallas_call(
        body, grid=(m // tm, n // tn, t // tk),
        in_specs=[pl.BlockSpec((tk, tm), lambda i, j, k: (k, i)), pl.BlockSpec((tk, tn), lambda i, j, k: (k, j))],
        out_specs=pl.BlockSpec((tm, tn), lambda i, j, k: (i, j)),
        out_shape=jax.ShapeDtypeStruct((m, n), F32),
        name=name, compiler_params=_params("parallel", "parallel", "arbitrary"),
    )(x, y)


def _rmsnorm(x, g, name, tm=512):
    t, d = x.shape
    tm = min(tm, t)

    def body(x_ref, g_ref, h_ref):
        xv = x_ref[...]
        r = lax.rsqrt(jnp.mean(xv * xv, axis=-1, keepdims=True) + EPS)
        h_ref[...] = (xv * r * g_ref[...]).astype(BF16)

    return pl.pallas_call(
        body, grid=(t // tm,),
        in_specs=[pl.BlockSpec((tm, d), lambda i: (i, 0)), pl.BlockSpec((1, d), lambda i: (0, 0))],
        out_specs=pl.BlockSpec((tm, d), lambda i: (i, 0)),
        out_shape=jax.ShapeDtypeStruct((t, d), BF16),
        name=name, compiler_params=_params("arbitrary"),
    )(x, g.reshape(1, d))


def _rms_bwd_tile(xv, dh, g):
    d = xv.shape[-1]
    r = lax.rsqrt(jnp.mean(xv * xv, axis=-1, keepdims=True) + EPS)
    dyg = dh * g
    proj = jnp.sum(dyg * xv, axis=-1, keepdims=True)
    dx = r * dyg - xv * (r * r * r * (1.0 / d)) * proj
    return dx, dh * (xv * r)


def _rms_bwd(x, dh, g, dres, name, tm=512):
    t, d = x.shape
    tm = min(tm, t)
    has_res = dres is not None

    def body(x_ref, dh_ref, g_ref, *rest):
        if has_res:
            res_ref, dx_ref, dxb_ref, dg_ref = rest
        else:
            dx_ref, dxb_ref, dg_ref = rest
        dx, dg_rows = _rms_bwd_tile(x_ref[...], dh_ref[...], g_ref[...])
        if has_res:
            dx = res_ref[...] + dx
        dx_ref[...] = dx
        dxb_ref[...] = dx.astype(BF16)

        @pl.when(pl.program_id(0) == 0)
        def _():
            dg_ref[...] = jnp.zeros_like(dg_ref)

        dg_ref[...] += jnp.sum(dg_rows, axis=0, keepdims=True)

    row = pl.BlockSpec((tm, d), lambda i: (i, 0))
    vec = pl.BlockSpec((1, d), lambda i: (0, 0))
    return pl.pallas_call(
        body, grid=(t // tm,),
        in_specs=[row, row, vec] + ([row] if has_res else []),
        out_specs=[row, row, vec],
        out_shape=[jax.ShapeDtypeStruct((t, d), F32), jax.ShapeDtypeStruct((t, d), BF16), jax.ShapeDtypeStruct((1, d), F32)],
        name=name, compiler_params=_params("arbitrary"),
    )(x, dh, g.reshape(1, d), *((dres,) if has_res else ()))


def _loss_bwd(x, g, target, name, tm=512):
    t, d = x.shape

    def body(x_ref, g_ref, t_ref, loss_ref, dx_ref, dxb_ref, dg_ref):
        xv = x_ref[...]
        gv = g_ref[...]
        r = lax.rsqrt(jnp.mean(xv * xv, axis=-1, keepdims=True) + EPS)
        err = xv * r * gv - t_ref[...]
        dx, dg_rows = _rms_bwd_tile(xv, err * (1.0 / d), gv)
        dx_ref[...] = dx
        dxb_ref[...] = dx.astype(BF16)

        @pl.when(pl.program_id(0) == 0)
        def _():
            dg_ref[...] = jnp.zeros_like(dg_ref)
            loss_ref[...] = jnp.zeros_like(loss_ref)

        dg_ref[...] += jnp.sum(dg_rows, axis=0, keepdims=True)
        part = jnp.sum(jnp.sum(err * err, axis=0, keepdims=True), axis=1, keepdims=True) * (0.5 / d)
        loss_ref[...] += jnp.broadcast_to(part, loss_ref.shape)

    row = pl.BlockSpec((tm, d), lambda i: (i, 0))
    vec = pl.BlockSpec((1, d), lambda i: (0, 0))
    return pl.pallas_call(
        body, grid=(t // tm,),
        in_specs=[row, vec, row],
        out_specs=[pl.BlockSpec((1, LANES), lambda i: (0, 0)), row, row, vec],
        out_shape=[jax.ShapeDtypeStruct((1, LANES), F32), jax.ShapeDtypeStruct((t, d), F32),
                   jax.ShapeDtypeStruct((t, d), BF16), jax.ShapeDtypeStruct((1, d), F32)],
        name=name, compiler_params=_params("arbitrary"),
    )(x, g.reshape(1, d), target)


def _tri(upper):
    r = lax.broadcasted_iota(jnp.int32, (LANES, LANES), 0)
    c = lax.broadcasted_iota(jnp.int32, (LANES, LANES), 1)
    return jnp.where((r <= c) if upper else (r >= c), 1.0, 0.0).astype(BF16)


def _gate_fwd(gate, b_pad, n_batch, name):
    s = SEQ
    nblk = s // LANES

    def body(g_ref, b_ref, cbc_ref, crow_ref, sg_ref, ct_ref):
        gz = g_ref[...] + b_ref[...]
        logf = jnp.minimum(gz, 0.0) - jnp.log(1.0 + jnp.exp(-jnp.abs(gz)))
        logf_t = logf.T
        sg_ref[...] = (1.0 / (1.0 + jnp.exp(gz))).T[0:N_HEADS]
        upper = _tri(True)
        carry = jnp.zeros((LANES, 1), F32)
        for blk in range(nblk):
            seg = _dot_exact(logf_t[:, blk * LANES:(blk + 1) * LANES], upper) + carry
            carry = seg[:, LANES - 1:LANES]
            ct_ref[:, blk * LANES:(blk + 1) * LANES] = seg
        ct = ct_ref[...]
        crow_ref[...] = ct[0:N_HEADS]
        c_col = ct.T
        lane = lax.broadcasted_iota(jnp.int32, (1, MIX_HALF), 1)
        acc = jnp.zeros((s, MIX_HALF), F32)
        for h in range(N_HEADS):
            acc = jnp.where((lane >= HEAD_DIM * h) & (lane < HEAD_DIM * (h + 1)), c_col[:, h:h + 1], acc)
        cbc_ref[...] = acc

    return pl.pallas_call(
        body, grid=(n_batch,),
        in_specs=[pl.BlockSpec((s, GATE_PAD), lambda b: (b, 0)), pl.BlockSpec((1, GATE_PAD), lambda b: (0, 0))],
        out_specs=[pl.BlockSpec((s, MIX_HALF), lambda b: (b, 0)),
                   pl.BlockSpec((None, N_HEADS, s), lambda b: (b, 0, 0)),
                   pl.BlockSpec((None, N_HEADS, s), lambda b: (b, 0, 0))],
        out_shape=[jax.ShapeDtypeStruct((n_batch * s, MIX_HALF), F32),
                   jax.ShapeDtypeStruct((n_batch, N_HEADS, s), F32),
                   jax.ShapeDtypeStruct((n_batch, N_HEADS, s), F32)],
        scratch_shapes=[pltpu.VMEM((LANES, s), F32)],
        name=name, compiler_params=_params("arbitrary"),
    )(gate, b_pad)


def _gate_bwd(dc, sg, name):
    n_batch, _, s = dc.shape
    nblk = s // LANES

    def body(dc_ref, sg_ref, dz_ref, db_ref, dt_ref):
        lower = _tri(False)
        dcv = dc_ref[...]
        carry = jnp.zeros((N_HEADS, 1), F32)
        dt_ref[...] = jnp.zeros_like(dt_ref)
        for blk in reversed(range(nblk)):
            seg = _dot_exact(dcv[:, blk * LANES:(blk + 1) * LANES], lower) + carry
            carry = seg[:, 0:1]
            dt_ref[0:N_HEADS, blk * LANES:(blk + 1) * LANES] = seg * sg_ref[:, blk * LANES:(blk + 1) * LANES]
        dg_t = dt_ref[...]
        dz_ref[...] = dg_t.T.astype(BF16)

        @pl.when(pl.program_id(0) == 0)
        def _():
            db_ref[...] = jnp.zeros_like(db_ref)

        db_ref[...] += jnp.broadcast_to(jnp.sum(dg_t[0:N_HEADS], axis=1, keepdims=True), db_ref.shape)

    return pl.pallas_call(
        body, grid=(n_batch,),
        in_specs=[pl.BlockSpec((None, N_HEADS, s), lambda b: (b, 0, 0)), pl.BlockSpec((None, N_HEADS, s), lambda b: (b, 0, 0))],
        out_specs=[pl.BlockSpec((s, GATE_PAD), lambda b: (b, 0)), pl.BlockSpec((N_HEADS, LANES), lambda b: (0, 0))],
        out_shape=[jax.ShapeDtypeStruct((n_batch * s, GATE_PAD), BF16), jax.ShapeDtypeStruct((N_HEADS, LANES), F32)],
        scratch_shapes=[pltpu.VMEM((LANES, s), F32)],
        name=name, compiler_params=_params("arbitrary"),
    )(dc, sg)


FOX_BQ = 512
FOX_BK = 512
PAIR_WIDTH = 3 * LANES
N_PAIRS = N_HEADS // 2


def _pair_major(w):
    return w.reshape(w.shape[0], 3, N_PAIRS, LANES).transpose(0, 2, 1, 3).reshape(w.shape[0], 3 * MIX_HALF)


def _pair_major_inv(w):
    return w.reshape(w.shape[0], N_PAIRS, 3, LANES).transpose(0, 2, 1, 3).reshape(w.shape[0], 3 * MIX_HALF)


def _causal(i, j, bq, bk):
    qpos = i * bq + lax.broadcasted_iota(jnp.int32, (bq, 1), 0)
    kpos = j * bk + lax.broadcasted_iota(jnp.int32, (1, bk), 1)
    return kpos <= qpos


def _split_bf16(p):
    hi = p.astype(BF16)
    return hi, (p - hi.astype(F32)).astype(BF16)


def _fox_fwd(zf, c_bc, c_row, n_batch, name):
    s, bq, bk = SEQ, FOX_BQ, FOX_BK
    nq = s // bq
    t = n_batch * s

    def body(q_ref, k_ref, v_ref, cq_ref, cr_ref, o_ref, o32_ref, lse_ref):
        hp, i = pl.program_id(1), pl.program_id(2)
        q = q_ref[...] * ATT_SCALE
        qh = [jnp.where(_head_mask(e), q, jnp.zeros_like(q)) for e in range(2)]
        cq = [cq_ref[:, HEAD_DIM * e:HEAD_DIM * e + 1] for e in range(2)]

        def step(j, carry, masked):
            rows = pl.ds(pl.multiple_of(j * bk, bk), bk)
            kj, vj = k_ref[rows, :], v_ref[rows, :]
            out = []
            for e in range(2):
                m, l, acc = carry[3 * e:3 * e + 3]
                sc = _dot_nt(qh[e], kj) + (cq[e] - cr_ref[pl.ds(2 * hp + e, 1), rows])
                if masked:
                    sc = jnp.where(_causal(i, j, bq, bk), sc, NEG)
                m_new = jnp.maximum(m, jnp.max(sc, axis=1, keepdims=True))
                alpha = jnp.exp(m - m_new)
                p = jnp.exp(sc - m_new)
                p_hi, p_lo = _split_bf16(p)
                out += [m_new, alpha * l + jnp.sum(p, axis=1, keepdims=True), alpha * acc + (_dot(p_hi, vj) + _dot(p_lo, vj))]
            return tuple(out)

        init = (jnp.full((bq, 1), NEG, F32), jnp.zeros((bq, 1), F32), jnp.zeros((bq, LANES), F32)) * 2
        n_clear = (i * bq) // bk
        carry = lax.fori_loop(0, n_clear, functools.partial(step, masked=False), init)
        carry = lax.fori_loop(n_clear, (i * bq + bq + bk - 1) // bk, functools.partial(step, masked=True), carry)
        outs = [carry[3 * e + 2] / carry[3 * e + 1] for e in range(2)]
        lses = [carry[3 * e] + jnp.log(carry[3 * e + 1]) for e in range(2)]
        o = jnp.where(_head_mask(0), outs[0], outs[1])
        o_ref[...] = o.astype(BF16)
        o32_ref[...] = o
        lse_ref[...] = jnp.where(_head_mask(0), lses[0], lses[1])

    def col(c0):
        return lambda b, hp, i: (b, 3 * hp + c0)

    blk = pl.BlockSpec((bq, LANES), lambda b, hp, i: (b * nq + i, hp))
    return pl.pallas_call(
        body, grid=(n_batch, N_PAIRS, nq),
        in_specs=[pl.BlockSpec((bq, LANES), lambda b, hp, i: (b * nq + i, 3 * hp)),
                  pl.BlockSpec((s, LANES), col(1)), pl.BlockSpec((s, LANES), col(2)), blk,
                  pl.BlockSpec((None, N_HEADS, s), lambda b, hp, i: (b, 0, 0))],
        out_specs=[blk, blk, blk],
        out_shape=[jax.ShapeDtypeStruct((t, MIX_HALF), BF16), jax.ShapeDtypeStruct((t, MIX_HALF), F32),
                   jax.ShapeDtypeStruct((t, MIX_HALF), F32)],
        name=name, compiler_params=_params("parallel", "parallel", "arbitrary"),
    )(zf, zf, zf, c_bc, c_row)


def _fox_bwd(zf, o32, dy, lse, c_bc, c_row, dz, n_batch, name):
    s, bq, bk = SEQ, FOX_BQ, FOX_BK
    nq, nk = s // bq, s // bk

    def body(q_ref, k_ref, v_ref, o_ref, do_ref, lse_ref, cq_ref, cr_ref, dz_in, dz_ref, dc_ref, dq_acc):
        del dz_in
        hp, j = pl.program_id(1), pl.program_id(2)

        @pl.when(j == 0)
        def _():
            dq_acc[...] = jnp.zeros_like(dq_acc)

        kj, vj = k_ref[...], v_ref[...]
        cols = pl.ds(pl.multiple_of(j * bk, bk), bk)
        km = [jnp.where(_head_mask(e), kj, jnp.zeros_like(kj)) for e in range(2)]
        ck = [cr_ref[pl.ds(2 * hp + e, 1), cols] for e in range(2)]

        def step(i, carry, masked):
            rows = pl.ds(pl.multiple_of(i * bq, bq), bq)
            qi, doi = q_ref[rows, :] * ATT_SCALE, do_ref[rows, :]
            prod = doi.astype(F32) * o_ref[rows, :]
            out = []
            dq = jnp.zeros((bq, LANES), F32)
            for e in range(2):
                dk_a, dv_a, dc_a = carry[3 * e:3 * e + 3]
                mask = _head_mask(e)
                lane0 = HEAD_DIM * e
                dom = jnp.where(mask, doi, jnp.zeros_like(doi))
                delta = jnp.sum(jnp.where(mask, prod, 0.0), axis=1, keepdims=True)
                sc = _dot_nt(qi, km[e]) + (cq_ref[rows, lane0:lane0 + 1] - ck[e])
                if masked:
                    sc = jnp.where(_causal(i, j, bq, bk), sc, NEG)
                p = jnp.exp(sc - lse_ref[rows, lane0:lane0 + 1])
                ds = p * (_dot_nt(dom, vj) - delta)
                dsb = ds.astype(BF16)
                dq = dq + _dot(dsb, km[e])
                out += [dk_a + _dot_tn(dsb, qi), dv_a + _dot_tn(p.astype(BF16), dom), dc_a - jnp.sum(ds, axis=0, keepdims=True)]
            dq_acc[rows, :] += dq * ATT_SCALE
            return tuple(out)

        init = (jnp.zeros((bk, LANES), F32), jnp.zeros((bk, LANES), F32), jnp.zeros((1, bk), F32)) * 2
        first = (j * bk) // bq
        n_diag = (j * bk + bk + bq - 1) // bq
        carry = lax.fori_loop(first, n_diag, functools.partial(step, masked=True), init)
        carry = lax.fori_loop(n_diag, nq, functools.partial(step, masked=False), carry)
        for e in range(2):
            dc_ref[e:e + 1, :] = carry[3 * e + 2]
        dz_ref[cols, LANES:2 * LANES] = jnp.where(_head_mask(0), carry[0], carry[3]).astype(BF16)
        dz_ref[cols, 2 * LANES:3 * LANES] = (carry[1] + carry[4]).astype(BF16)

        @pl.when(j == nk - 1)
        def _():
            dz_ref[:, 0:LANES] = dq_acc[...].astype(BF16)

    def seq(idx):
        return pl.BlockSpec((s, LANES), lambda b, hp, j: (b, idx(hp)))

    def kblk(c0):
        return pl.BlockSpec((bk, LANES), lambda b, hp, j: (b * nk + j, 3 * hp + c0))

    return pl.pallas_call(
        body, grid=(n_batch, N_PAIRS, nk),
        in_specs=[seq(lambda hp: 3 * hp), kblk(1), kblk(2), seq(lambda hp: hp), seq(lambda hp: N_PAIRS + hp),
                  seq(lambda hp: hp), seq(lambda hp: hp),
                  pl.BlockSpec((None, N_HEADS, s), lambda b, hp, j: (b, 0, 0)), pl.BlockSpec(memory_space=pl.ANY)],
        out_specs=[pl.BlockSpec((s, PAIR_WIDTH), lambda b, hp, j: (b, N_PAIRS + hp)),
                   pl.BlockSpec((None, None, 2, bk), lambda b, hp, j: (b, hp, 0, j))],
        out_shape=[jax.ShapeDtypeStruct(dz.shape, dz.dtype), jax.ShapeDtypeStruct((n_batch, N_PAIRS, 2, s), F32)],
        scratch_shapes=[pltpu.VMEM((s, LANES), F32)],
        input_output_aliases={8: 0},
        name=name, compiler_params=_params("parallel", "parallel", "arbitrary"),
    )(zf, zf, zf, o32, dy, lse, c_bc, c_row, dz)


def _dil_bias(slope, dil):
    qi = lax.broadcasted_iota(jnp.int32, (BLOCK, 2 * BLOCK), 0)
    kj = lax.broadcasted_iota(jnp.int32, (BLOCK, 2 * BLOCK), 1)
    delta = qi + BLOCK - kj
    return jnp.where((delta >= 0) & (delta <= BLOCK), (-slope * dil) * delta.astype(F32), NEG)


def _alibi_slope(hp, e):
    slope = jnp.float32(0.0)
    for k in range(N_PAIRS):
        slope = jnp.where(hp == k, jnp.float32(2.0 ** -(2 * k + e + 1)), slope)
    return slope


def _fill_bias(bias_scr, hp):
    for di, dil in enumerate(DILATIONS):
        for e in range(2):
            bias_scr[2 * di + e] = _dil_bias(_alibi_slope(hp, e), dil)


def _pair_specs(rows):
    return [pl.BlockSpec((rows, LANES), lambda b, hp, c0=c0: (b, 3 * hp + c0)) for c0 in range(3)]


def _strided(start, size, dil):
    return pl.ds(start, size) if dil == 1 else pl.ds(start, size, stride=dil)


def _for_each_block(dil, unit):
    span = BLOCK * dil
    nb = SEQ // span
    if dil == 1:
        group = 3
        assert (nb - 1) % group == 0
        unit(0, True)

        def later(g, c):
            for u in range(group):
                unit((1 + g * group + u) * span, False)
            return c

        lax.fori_loop(0, (nb - 1) // group, later, 0)
        return
    group = 4
    per = dil // group

    def firsts(g, c):
        for u in range(group):
            unit(g * group + u, True)
        return c

    lax.fori_loop(0, per, firsts, 0)
    if nb > 1:
        def later(i, c):
            for u in range(group):
                unit((1 + i // per) * span + (i % per) * group + u, False)
            return c

        lax.fori_loop(0, (nb - 1) * per, later, 0)


def _mix_weights(l1, l2, l3):
    m = jnp.maximum(jnp.maximum(l1, l2), l3)
    e1, e2, e3 = jnp.exp(l1 - m), jnp.exp(l2 - m), jnp.exp(l3 - m)
    inv = 1.0 / (e1 + e2 + e3)
    return e1 * inv, e2 * inv, e3 * inv


def _dil_fwd(zd, n_batch, name):
    s = SEQ
    t = n_batch * s

    def body(q_ref, k_ref, v_ref, y_ref, l1_ref, l2_ref, l3_ref, o_scr, bias_scr):
        _fill_bias(bias_scr, pl.program_id(1))
        lse_refs = (l1_ref, l2_ref, l3_ref)
        for di, dil in enumerate(DILATIONS):

            def unit(start, first, di=di, dil=dil):
                qrows = _strided(start, BLOCK, dil)
                krows = qrows if first else _strided(start - BLOCK * dil, 2 * BLOCK, dil)
                q = (q_ref[qrows, :] * ATT_SCALE).astype(BF16)
                kc = k_ref[krows, :].astype(BF16)
                vc = v_ref[krows, :].astype(BF16)
                outs, lses = [], []
                for e in range(2):
                    bias = bias_scr[2 * di + e]
                    sc = _dot_nt(jnp.where(_head_mask(e), q, jnp.zeros_like(q)), kc) + (bias[:, BLOCK:] if first else bias)
                    m = jnp.max(sc, axis=1, keepdims=True)
                    pe = jnp.exp(sc - m)
                    l = jnp.sum(pe, axis=1, keepdims=True)
                    outs.append(_dot((pe * (1.0 / l)).astype(BF16), vc))
                    lses.append(m + jnp.log(l))
                o_scr.at[di][qrows, :] = jnp.where(_head_mask(0), outs[0], outs[1])
                lse_refs[di][qrows, :] = jnp.where(_head_mask(0), lses[0], lses[1])

            _for_each_block(dil, unit)
        w = _mix_weights(l1_ref[...], l2_ref[...], l3_ref[...])
        y_ref[...] = (w[0] * o_scr[0] + w[1] * o_scr[1] + w[2] * o_scr[2]).astype(BF16)

    blk = pl.BlockSpec((s, LANES), lambda b, hp: (b, hp))
    res = pl.pallas_call(
        body, grid=(n_batch, N_PAIRS),
        in_specs=_pair_specs(s),
        out_specs=[blk] * 4,
        out_shape=[jax.ShapeDtypeStruct((t, MIX_HALF), BF16)] + [jax.ShapeDtypeStruct((t, MIX_HALF), F32)] * 3,
        scratch_shapes=[pltpu.VMEM((3, s, LANES), F32), pltpu.VMEM((6, BLOCK, 2 * BLOCK), F32)],
        name=name, compiler_params=_params("parallel", "arbitrary"),
    )(zd, zd, zd)
    return res[0], res[1:]


def _dil_bwd(zd, dy, ya, lses, n_batch, name):
    s = SEQ
    t = n_batch * s

    def body(q_ref, k_ref, v_ref, dy_ref, ya_ref, l1_ref, l2_ref, l3_ref, dz_ref, w_scr, dy_scr, dot_scr, acc, bias_scr):
        _fill_bias(bias_scr, pl.program_id(1))
        for di, w in enumerate(_mix_weights(l1_ref[...], l2_ref[...], l3_ref[...])):
            w_scr[di] = w
        dya = dy_ref[...].astype(F32)
        prod = dya * ya_ref[...].astype(F32)
        per_head = [jnp.sum(jnp.where(_head_mask(e), prod, 0.0), axis=1, keepdims=True) for e in range(2)]
        dy_scr[...] = dya
        dot_scr[...] = jnp.where(_head_mask(0), per_head[0], per_head[1])
        acc[...] = jnp.zeros_like(acc)
        lse_refs = (l1_ref, l2_ref, l3_ref)
        for di, dil in enumerate(DILATIONS):

            def unit(start, first, di=di, dil=dil):
                qrows = _strided(start, BLOCK, dil)
                krows = qrows if first else _strided(start - BLOCK * dil, 2 * BLOCK, dil)
                q = (q_ref[qrows, :] * ATT_SCALE).astype(BF16)
                kc = k_ref[krows, :].astype(BF16)
                vc = v_ref[krows, :].astype(BF16)
                wq = w_scr.at[di][qrows, :]
                do = (wq * dy_scr[qrows, :]).astype(BF16)
                sub = wq * dot_scr[qrows, :]
                lse = lse_refs[di][qrows, :]
                dq = jnp.zeros((BLOCK, LANES), F32)
                dk = jnp.zeros((krows.size, LANES), F32)
                dv = jnp.zeros((krows.size, LANES), F32)
                for e in range(2):
                    mask = _head_mask(e)
                    lane0 = HEAD_DIM * e
                    qh = jnp.where(mask, q, jnp.zeros_like(q))
                    doh = jnp.where(mask, do, jnp.zeros_like(do))
                    bias = bias_scr[2 * di + e]
                    sc = _dot_nt(qh, kc) + (bias[:, BLOCK:] if first else bias)
                    p = jnp.exp(sc - lse[:, lane0:lane0 + 1])
                    dsb = (p * (_dot_nt(doh, vc) - sub[:, lane0:lane0 + 1])).astype(BF16)
                    dq = dq + _dot(dsb, jnp.where(mask, kc, jnp.zeros_like(kc)))
                    dk = dk + _dot_tn(dsb, qh)
                    dv = dv + _dot_tn(p.astype(BF16), doh)
                acc.at[0][qrows, :] += dq * ATT_SCALE
                acc.at[1][krows, :] += dk
                acc.at[2][krows, :] += dv

            _for_each_block(dil, unit)
        for k in range(3):
            dz_ref[:, k * LANES:(k + 1) * LANES] = acc[k].astype(BF16)

    blk = pl.BlockSpec((s, LANES), lambda b, hp: (b, hp))
    pair = pl.BlockSpec((s, PAIR_WIDTH), lambda b, hp: (b, hp))
    return pl.pallas_call(
        body, grid=(n_batch, N_PAIRS),
        in_specs=_pair_specs(s) + [blk] * 5,
        out_specs=pair,
        out_shape=jax.ShapeDtypeStruct((t, 2 * 3 * MIX_HALF), BF16),
        scratch_shapes=[pltpu.VMEM((3, s, LANES), F32), pltpu.VMEM((s, LANES), F32), pltpu.VMEM((s, LANES), F32),
                        pltpu.VMEM((3, s, LANES), F32), pltpu.VMEM((6, BLOCK, 2 * BLOCK), F32)],
        name=name, compiler_params=_params("parallel", "arbitrary"),
    )(zd, zd, zd, dy, ya, *lses)


X_BQ = 512


def _xattn_probs(q, k):
    sc = _dot_nt(q, k) * X_SCALE
    pe = jnp.exp(sc - jnp.max(sc, axis=1, keepdims=True))
    return pe / jnp.sum(pe, axis=1, keepdims=True)


def _xattn_fwd(qx, kx, vx, n_batch, name):
    nq = SEQ // X_BQ

    def body(q_ref, k_ref, v_ref, o_ref):
        p = _xattn_probs(q_ref[...], k_ref[...])
        o_ref[...] = _dot(p.astype(BF16), v_ref[...]).astype(BF16)

    qblk = pl.BlockSpec((X_BQ, X_HEAD_DIM), lambda b, h, i: (b * nq + i, h))
    kblk = pl.BlockSpec((N_MEM, X_HEAD_DIM), lambda b, h, i: (b, h))
    return pl.pallas_call(
        body, grid=(n_batch, X_HEADS, nq), in_specs=[qblk, kblk, kblk], out_specs=qblk,
        out_shape=jax.ShapeDtypeStruct(qx.shape, BF16),
        name=name, compiler_params=_params("parallel", "parallel", "arbitrary"),
    )(qx, kx, vx)


def _xattn_bwd(qx, kx, vx, dox, n_batch, name):
    nq = SEQ // X_BQ

    def body(q_ref, k_ref, v_ref, do_ref, dq_ref, dk_ref, dv_ref, dk_acc, dv_acc):
        i = pl.program_id(2)

        @pl.when(i == 0)
        def _():
            dk_acc[...] = jnp.zeros_like(dk_acc)
            dv_acc[...] = jnp.zeros_like(dv_acc)

        q, k, do = q_ref[...], k_ref[...], do_ref[...]
        p = _xattn_probs(q, k)
        dp = _dot_nt(do, v_ref[...])
        dsb = (p * (dp - jnp.sum(p * dp, axis=1, keepdims=True))).astype(BF16)
        dq_ref[...] = (_dot(dsb, k) * X_SCALE).astype(BF16)
        dk_acc[...] += _dot_tn(dsb, q) * X_SCALE
        dv_acc[...] += _dot_tn(p.astype(BF16), do)

        @pl.when(i == nq - 1)
        def _():
            dk_ref[...] = dk_acc[...].astype(BF16)
            dv_ref[...] = dv_acc[...].astype(BF16)

    qblk = pl.BlockSpec((X_BQ, X_HEAD_DIM), lambda b, h, i: (b * nq + i, h))
    kblk = pl.BlockSpec((N_MEM, X_HEAD_DIM), lambda b, h, i: (b, h))
    return pl.pallas_call(
        body, grid=(n_batch, X_HEADS, nq), in_specs=[qblk, kblk, kblk, qblk], out_specs=[qblk, kblk, kblk],
        out_shape=[jax.ShapeDtypeStruct(qx.shape, BF16), jax.ShapeDtypeStruct(kx.shape, BF16), jax.ShapeDtypeStruct(kx.shape, BF16)],
        scratch_shapes=[pltpu.VMEM((N_MEM, X_HEAD_DIM), F32)] * 2,
        name=name, compiler_params=_params("parallel", "parallel", "arbitrary"),
    )(qx, kx, vx, dox)


def _adamw(w, g, m, v, name, rows):
    r, c = w.shape
    assert r % rows == 0, (name, w.shape, rows)

    def body(w_ref, g_ref, m_ref, v_ref, d_ref, nm_ref, nv_ref):
        gv = g_ref[...]
        m1 = ADAM_B1 * m_ref[...] + (1.0 - ADAM_B1) * gv
        v1 = ADAM_B2 * v_ref[...] + (1.0 - ADAM_B2) * jnp.square(gv)
        m_hat = m1 / (1.0 - ADAM_B1 ** ADAM_STEP)
        v_hat = v1 / (1.0 - ADAM_B2 ** ADAM_STEP)
        d_ref[...] = -ADAM_LR * (m_hat / (jnp.sqrt(v_hat) + ADAM_EPS) + ADAM_WD * w_ref[...])
        nm_ref[...] = m1
        nv_ref[...] = v1

    blk = pl.BlockSpec((rows, c), lambda i: (i, 0))
    return pl.pallas_call(
        body, grid=(r // rows,), in_specs=[blk] * 4, out_specs=[blk] * 3,
        out_shape=[jax.ShapeDtypeStruct((r, c), F32)] * 3,
        name=name, compiler_params=_params("arbitrary"),
    )(w, g, m, v)


def _relu2(acc):
    a = jnp.maximum(acc, 0.0)
    return acc, a * a


def _relu2_bwd(acc, u):
    return (2.0 * jnp.maximum(u.astype(F32), 0.0) * acc,)


def _local_step(x, mem, target, vecs, wts):
    n_batch = x.shape[0]
    t = n_batch * SEQ
    x0 = x.reshape(t, D_MODEL)
    mem2 = mem.reshape(n_batch * N_MEM, D_MODEL)
    tgt = target.reshape(t, D_MODEL)

    w_in = wts["w_in"]
    half = 3 * MIX_HALF
    w_dil, w_fox = _pair_major(w_in[:, :half]), _pair_major(w_in[:, half:QKV_WIDTH])
    w_gate = jnp.pad(w_in[:, QKV_WIDTH:], ((0, 0), (0, GATE_PAD - N_HEADS)))
    w_out = wts["w_out"]
    b_pad = jnp.pad(vecs["b_forget"], (0, GATE_PAD - N_HEADS)).reshape(1, GATE_PAD)

    h1 = _rmsnorm(x0, vecs["g_mix"], "norm_mix")
    mn = _rmsnorm(mem2, vecs["g_mem"], "norm_mem")
    zd = _matmul(h1, w_dil, "in_dil", tn=768)[0]
    zf = _matmul(h1, w_fox, "in_fox", out_dtypes=(BF16,), tn=768)[0]
    gate = _matmul(h1, w_gate, "in_gate")[0]
    c_bc, c_row, sg = _gate_fwd(gate, b_pad, n_batch, "gate_fwd")
    ya, lses = _dil_fwd(zd, n_batch, "dil_fwd")
    yf, of32, lse_f = _fox_fwd(zf, c_bc, c_row, n_batch, "fox_fwd")
    x1 = _matmul_res(ya, w_out[:MIX_HALF], x0, "out_a")
    x1 = _matmul_res(yf, w_out[MIX_HALF:], x1, "out_f")
    h2 = _rmsnorm(x1, vecs["g_xattn"], "norm_xattn")
    qx = _matmul(h2, wts["w_xq"], "xq", out_dtypes=(BF16,))[0]
    kx = _matmul(mn, wts["w_xk"], "xk", out_dtypes=(BF16,))[0]
    vx = _matmul(mn, wts["w_xv"], "xv", out_dtypes=(BF16,))[0]
    ox = _xattn_fwd(qx, kx, vx, n_batch, "xattn_fwd")
    x2 = _matmul_res(ox, wts["w_xo"], x1, "xo")
    h3 = _rmsnorm(x2, vecs["g_mlp"], "norm_mlp")
    u, a2 = _matmul(h3, wts["w_up"], "mlp_up", out_dtypes=(BF16, BF16), epilogue=_relu2, tn=1024)
    x3 = _matmul_res(a2, wts["w_down"], x2, "mlp_down")
    loss, dx3, dx3b, dg_final = _loss_bwd(x3, vecs["g_final"], tgt, "loss")

    du = _matmul(dx3b, wts["w_down"], "mlp_down_bwd", out_dtypes=(BF16,), extras=(u,), epilogue=_relu2_bwd, tn=1024, w_t=True)[0]
    gw_down = _matmul_tn(a2, dx3b, "gw_down")
    gw_up = _matmul_tn(h3, du, "gw_up")
    dh3 = _matmul(du, wts["w_up"], "mlp_up_bwd", w_t=True)[0]
    dx2, dx2b, dg_mlp = _rms_bwd(x2, dh3, vecs["g_mlp"], dx3, "norm_mlp_bwd")

    gw_xo = _matmul_tn(ox, dx2b, "gw_xo")
    dox = _matmul(dx2b, wts["w_xo"], "xo_bwd", out_dtypes=(BF16,), w_t=True)[0]
    dqx, dkx, dvx = _xattn_bwd(qx, kx, vx, dox, n_batch, "xattn_bwd")
    gw_xq = _matmul_tn(h2, dqx, "gw_xq")
    gw_xk = _matmul_tn(mn, dkx, "gw_xk")
    gw_xv = _matmul_tn(mn, dvx, "gw_xv")
    dh2 = _matmul(dqx, wts["w_xq"], "xq_bwd", w_t=True)[0]
    dmn = _matmul(dkx, wts["w_xk"], "xk_bwd", w_t=True)[0]
    dmn = _matmul_res(dvx, wts["w_xv"], dmn, "xv_bwd", w_t=True)
    _, _, dg_mem = _rms_bwd(mem2, dmn, vecs["g_mem"], None, "norm_mem_bwd")
    dx1, dx1b, dg_xattn = _rms_bwd(x1, dh2, vecs["g_xattn"], dx2, "norm_xattn_bwd")

    gw_out = jnp.concatenate([_matmul_tn(ya, dx1b, "gw_out_a"), _matmul_tn(yf, dx1b, "gw_out_f")], axis=0)
    dy = _matmul(dx1b, w_out, "out_bwd", out_dtypes=(BF16,), w_t=True)[0]
    dz = _dil_bwd(zd, dy, ya, lses, n_batch, "dil_bwd")
    dz, dc = _fox_bwd(zf, of32, dy, lse_f, c_bc, c_row, dz, n_batch, "fox_bwd")
    dzg, db = _gate_bwd(dc.reshape(n_batch, N_HEADS, SEQ), sg, "gate_bwd")
    gw_pm = _matmul_tn(h1, dz, "gw_in_qkv")
    gw_in = jnp.concatenate([_pair_major_inv(gw_pm[:, :half]), _pair_major_inv(gw_pm[:, half:]),
                             _matmul_tn(h1, dzg, "gw_in_gate")[:, :N_HEADS]], axis=1)
    dh1 = _matmul(dz, jnp.concatenate([w_dil, w_fox], axis=1), "in_qkv_bwd", w_t=True)[0]
    dh1 = _matmul_res(dzg, w_gate, dh1, "in_gate_bwd", w_t=True)
    dx0, _, dg_mix = _rms_bwd(x0, dh1, vecs["g_mix"], dx1, "norm_mix_bwd")

    gw = dict(w_in=gw_in, w_out=gw_out, w_xq=gw_xq, w_xk=gw_xk, w_xv=gw_xv, w_xo=gw_xo, w_up=gw_up, w_down=gw_down)
    gv = dict(g_mix=dg_mix, g_xattn=dg_xattn, g_mem=dg_mem, g_mlp=dg_mlp, g_final=dg_final, b_forget=db)
    return loss, dx0.reshape(x.shape), gw, gv


MESH = pl.DeviceIdType.MESH
ANY = pl.BlockSpec(memory_space=pl.ANY)


def _place():
    x, y, c = lax.axis_index("x"), lax.axis_index("y"), lax.axis_index("c")
    other_chips = [(1 - x, y), (x, 1 - y), (1 - x, 1 - y)]
    return x, y, c, other_chips


def _my_chip():
    return 2 * lax.axis_index("x") + lax.axis_index("y")


def _gather_weights(pack):
    def body(p_ref, out_ref, send_sems, recv_sems, pass_send, pass_recv):
        x, y, c, chips = _place()
        me = 2 * x + y
        mine = pl.ds(pl.multiple_of(c * PACK_HALF, 16), PACK_HALF)
        theirs = pl.ds(pl.multiple_of((1 - c) * PACK_HALF, 16), PACK_HALF)

        def from_chip(k, chip, rows):
            src = out_ref.at[2 * chip[0] + chip[1], rows]
            return pltpu.make_async_remote_copy(src_ref=src, dst_ref=src, send_sem=send_sems.at[k], recv_sem=recv_sems.at[k],
                                                device_id=(chip[0], chip[1], c), device_id_type=MESH)

        def passed(k, chip, rows):
            src = out_ref.at[2 * chip[0] + chip[1], rows]
            return pltpu.make_async_remote_copy(src_ref=src, dst_ref=src, send_sem=pass_send.at[k], recv_sem=pass_recv.at[k],
                                                device_id=(x, y, 1 - c), device_id_type=MESH)

        sends = []
        for k, chip in enumerate(chips):
            cp = pltpu.make_async_remote_copy(src_ref=p_ref.at[mine], dst_ref=out_ref.at[me, mine], send_sem=send_sems.at[k],
                                              recv_sem=recv_sems.at[k], device_id=(chip[0], chip[1], c), device_id_type=MESH)
            cp.start()
            sends.append(cp)
        for k, chip in enumerate(chips):
            from_chip(k, chip, mine).wait_recv()
            cp = passed(k, chip, mine)
            cp.start()
            sends.append(cp)
        for k, chip in enumerate(chips):
            passed(k, chip, theirs).wait_recv()
        for cp in sends:
            cp.wait_send()

    wall = pl.pallas_call(
        body, in_specs=[ANY], out_specs=ANY,
        out_shape=jax.ShapeDtypeStruct((N_CHIPS,) + pack.shape, pack.dtype),
        scratch_shapes=[pltpu.SemaphoreType.DMA((3,))] * 4,
        name="gather_weights",
    )(pack)
    return lax.dynamic_update_slice(wall, pack[None], (_my_chip(), 0, 0))


def _swap_halves(g):
    def body(g_ref, out_ref, send_sem, recv_sem):
        x, y, c, _ = _place()
        theirs = pl.ds(pl.multiple_of((1 - c) * PACK_HALF, 8), PACK_HALF)
        cp = pltpu.make_async_remote_copy(src_ref=g_ref.at[:, theirs], dst_ref=out_ref, send_sem=send_sem, recv_sem=recv_sem,
                                          device_id=(x, y, 1 - c), device_id_type=MESH)
        cp.start()
        cp.wait()

    return pl.pallas_call(
        body, in_specs=[ANY], out_specs=ANY,
        out_shape=jax.ShapeDtypeStruct((N_CHIPS, PACK_HALF, D_MODEL), F32),
        scratch_shapes=[pltpu.SemaphoreType.DMA, pltpu.SemaphoreType.DMA],
        name="swap_halves",
    )(g)


PACK_TILE = 688


def _core_index():
    return lax.axis_index("c").astype(jnp.int32).reshape(1)


def _add_sibling(g, got):
    n_tiles = PACK_HALF // PACK_TILE

    def body(c_ref, g_ref, got_ref, o_ref):
        o_ref[...] = (g_ref[...] + got_ref[...]).astype(BF16)

    blk = pl.BlockSpec((None, PACK_TILE, D_MODEL), lambda s, i, c_ref: (s, i, 0))
    return pl.pallas_call(
        body,
        grid_spec=pltpu.PrefetchScalarGridSpec(
            num_scalar_prefetch=1, grid=(N_CHIPS, n_tiles),
            in_specs=[pl.BlockSpec((None, PACK_TILE, D_MODEL), lambda s, i, c_ref: (s, c_ref[0] * n_tiles + i, 0)), blk],
            out_specs=blk),
        out_shape=jax.ShapeDtypeStruct((N_CHIPS, PACK_HALF, D_MODEL), BF16),
        name="add_sibling", compiler_params=_params("arbitrary", "arbitrary"),
    )(_core_index(), g, got)


def _exchange_chips(part):
    def body(p_ref, out_ref, send_sems, recv_sems):
        x, y, c, chips = _place()
        me = 2 * x + y
        sends = []
        for k, chip in enumerate(chips):
            cp = pltpu.make_async_remote_copy(src_ref=p_ref.at[2 * chip[0] + chip[1]], dst_ref=out_ref.at[me],
                                              send_sem=send_sems.at[k], recv_sem=recv_sems.at[k],
                                              device_id=(chip[0], chip[1], c), device_id_type=MESH)
            cp.start()
            sends.append(cp)
        for k, chip in enumerate(chips):
            slab = out_ref.at[2 * chip[0] + chip[1]]
            pltpu.make_async_remote_copy(src_ref=slab, dst_ref=slab, send_sem=send_sems.at[k], recv_sem=recv_sems.at[k],
                                         device_id=(chip[0], chip[1], c), device_id_type=MESH).wait_recv()
        for cp in sends:
            cp.wait_send()

    got = pl.pallas_call(
        body, in_specs=[ANY], out_specs=ANY,
        out_shape=jax.ShapeDtypeStruct(part.shape, part.dtype),
        scratch_shapes=[pltpu.SemaphoreType.DMA((3,)), pltpu.SemaphoreType.DMA((3,))],
        name="exchange_chips",
    )(part)
    me = _my_chip()
    return lax.dynamic_update_slice(got, lax.dynamic_slice(part, (me, 0, 0), (1,) + part.shape[1:]), (me, 0, 0))


def _sum_chips(parts):
    def body(c_ref, p0, p1, p2, p3, o_ref):
        f32 = lambda p: p[...].astype(F32)
        o_ref[...] = ((f32(p0) + f32(p1)) + f32(p2)) + f32(p3)

    def slab(s):
        return pl.BlockSpec((None, PACK_TILE, D_MODEL), lambda i, c_ref, s=s: (s, i, 0))

    return pl.pallas_call(
        body,
        grid_spec=pltpu.PrefetchScalarGridSpec(
            num_scalar_prefetch=1, grid=(PACK_HALF // PACK_TILE,),
            in_specs=[slab(s) for s in range(N_CHIPS)],
            out_specs=pl.BlockSpec((None, PACK_TILE, D_MODEL), lambda i, c_ref: (c_ref[0], i, 0))),
        out_shape=jax.ShapeDtypeStruct((2, PACK_HALF, D_MODEL), F32),
        name="sum_chips", compiler_params=_params("arbitrary"),
    )(_core_index(), parts, parts, parts, parts)


def _share_halves(halves):
    def body(h_ref, out_ref, send_sem, recv_sem):
        del h_ref
        x, y, c, _ = _place()
        cp = pltpu.make_async_remote_copy(src_ref=out_ref.at[c], dst_ref=out_ref.at[c], send_sem=send_sem, recv_sem=recv_sem,
                                          device_id=(x, y, 1 - c), device_id_type=MESH)
        cp.start()
        pltpu.make_async_remote_copy(src_ref=out_ref.at[1 - c], dst_ref=out_ref.at[1 - c], send_sem=send_sem, recv_sem=recv_sem,
                                     device_id=(x, y, 1 - c), device_id_type=MESH).wait_recv()
        cp.wait_send()

    return pl.pallas_call(
        body, in_specs=[ANY], out_specs=ANY,
        out_shape=jax.ShapeDtypeStruct(halves.shape, halves.dtype),
        scratch_shapes=[pltpu.SemaphoreType.DMA] * 2,
        input_output_aliases={0: 0},
        name="share_halves",
    )(halves)


def _reduce_scatter(g):
    part = _add_sibling(g, _swap_halves(g))
    halves = _sum_chips(_exchange_chips(part))
    return _share_halves(halves).reshape(PACK_ROWS, D_MODEL)


SMALL_ROWS = 8


def _allreduce_small(v):
    def body(v_ref, out_ref, buf, send_sems, recv_sems):
        x, y, c, _ = _place()
        buf[4 * x + 2 * y + c] = v_ref[...]
        sends = []
        for k in range(1, N_DEV):
            px = 1 - x if k & 4 else x
            py = 1 - y if k & 2 else y
            pc = 1 - c if k & 1 else c
            cp = pltpu.make_async_remote_copy(src_ref=v_ref, dst_ref=buf.at[4 * x + 2 * y + c], send_sem=send_sems.at[k - 1],
                                              recv_sem=recv_sems.at[k - 1], device_id=(px, py, pc), device_id_type=MESH)
            cp.start()
            sends.append((cp, 4 * px + 2 * py + pc))
        for k, (cp, peer) in enumerate(sends):
            pltpu.make_async_remote_copy(src_ref=v_ref, dst_ref=buf.at[peer], send_sem=send_sems.at[k], recv_sem=recv_sems.at[k],
                                         device_id=(x, y, c), device_id_type=MESH).wait_recv()
        for cp, _ in sends:
            cp.wait_send()
        total = buf[0]
        for d in range(1, N_DEV):
            total = total + buf[d]
        out_ref[...] = total

    vmem = pl.BlockSpec(memory_space=pltpu.VMEM)
    return pl.pallas_call(
        body, in_specs=[vmem], out_specs=vmem,
        out_shape=jax.ShapeDtypeStruct(v.shape, v.dtype),
        scratch_shapes=[pltpu.VMEM((N_DEV,) + v.shape, v.dtype), pltpu.SemaphoreType.DMA((N_DEV - 1,)),
                        pltpu.SemaphoreType.DMA((N_DEV - 1,))],
        name="allreduce_small",
    )(v)


MATRICES = ("w_in", "w_out", "w_xq", "w_xk", "w_xv", "w_xo", "w_up", "w_down")
VECTORS = ("g_mix", "g_xattn", "g_mem", "g_mlp", "g_final", "b_forget")
WEIGHT_ORDER = ("g_mix", "w_in", "b_forget", "w_out", "g_xattn", "g_mem", "w_xq", "w_xk", "w_xv", "w_xo",
                "g_mlp", "w_up", "w_down", "g_final")
W_IN_SHARD = IN_WIDTH // N_CHIPS
W_IN_ROWS = W_IN_SHARD
PACK_AT = {"w_in": 0, "w_out": 784, "w_xq": 1040, "w_xk": 1296, "w_xv": 1552, "w_xo": 1808, "w_up": 2064, "w_down": 3088}
PACK_LEN = {"w_in": W_IN_ROWS, "w_out": 256, "w_xq": 256, "w_xk": 256, "w_xv": 256, "w_xo": 256, "w_up": 1024, "w_down": 1024}
ADAM_ROWS = 128


def _pack(parts):
    segs, pos = [], 0
    for name in MATRICES:
        nxt = PACK_AT[MATRICES[MATRICES.index(name) + 1]] if name != MATRICES[-1] else PACK_ROWS
        segs.append(jnp.pad(parts[name], ((0, nxt - pos - PACK_LEN[name]), (0, 0))))
        pos = nxt
    return jnp.concatenate(segs, axis=0)


def _seg(a, name):
    return a[..., PACK_AT[name]:PACK_AT[name] + PACK_LEN[name], :]


def _full_weights(wall):
    cols = lambda a: a.transpose(1, 0, 2).reshape(a.shape[1], -1)
    rows = lambda a: a.reshape(-1, a.shape[-1])
    out = {n: rows(_seg(wall, n)) for n in ("w_out", "w_xq", "w_xk", "w_xv", "w_xo", "w_down")}
    out["w_in"] = cols(_seg(wall, "w_in").reshape(N_CHIPS, D_MODEL, W_IN_SHARD))
    out["w_up"] = cols(_seg(wall, "w_up"))
    return out


def _shard_of(g, name, s):
    if name == "w_in":
        return g[:, s * W_IN_SHARD:(s + 1) * W_IN_SHARD].reshape(W_IN_ROWS, D_MODEL)
    if name == "w_up":
        return g[:, s * D_MODEL:(s + 1) * D_MODEL]
    n = PACK_LEN[name]
    return g[s * n:(s + 1) * n]


def kernel(x, mem, g_mix, w_in, b_forget, w_out, g_xattn, g_mem, w_xq, w_xk, w_xv, w_xo, g_mlp, w_up, w_down, g_final, loss_target, m_g_mix, m_w_in, m_b_forget, m_w_out, m_g_xattn, m_g_mem, m_w_xq, m_w_xk, m_w_xv, m_w_xo, m_g_mlp, m_w_up, m_w_down, m_g_final, v_g_mix, v_w_in, v_b_forget, v_w_out, v_g_xattn, v_g_mem, v_w_xq, v_w_xk, v_w_xv, v_w_xo, v_g_mlp, v_w_up, v_w_down, v_g_final):
    given = dict(locals())
    weights = {n: given[n] for n in WEIGHT_ORDER}
    vecs = {n: weights[n] for n in VECTORS}

    shard = {n: weights[n].astype(BF16) for n in MATRICES}
    shard["w_in"] = shard["w_in"].reshape(W_IN_ROWS, D_MODEL)
    wts = _full_weights(_gather_weights(_pack(shard)))

    loss, grad_x, gw, gv = _local_step(x, mem, loss_target, vecs, wts)

    g_all = jnp.stack([_pack({n: _shard_of(gw[n], n, s) for n in MATRICES}) for s in range(N_CHIPS)])
    red = _reduce_scatter(g_all)
    grads = {n: _seg(red, n).reshape(weights[n].shape) for n in MATRICES}

    row = lambda a: jnp.pad(a.reshape(-1), (0, D_MODEL - a.size)).reshape(1, D_MODEL)
    small = jnp.concatenate([gv[n] for n in VECTORS[:5]] + [row(gv["b_forget"][:, 0]), row(loss[0, :1]),
                             jnp.zeros((1, D_MODEL), F32)], axis=0)
    small = _allreduce_small(small)
    for k, n in enumerate(VECTORS[:5]):
        grads[n] = small[k]
    grads["b_forget"] = small[5, :N_HEADS]
    loss_total = small[6, 0]

    delta, new_m, new_v = {}, {}, {}
    for n in MATRICES:
        delta[n], new_m[n], new_v[n] = _adamw(weights[n], grads[n], given["m_" + n], given["v_" + n], "adamw_" + n, ADAM_ROWS)
    stack = lambda prefix: jnp.concatenate([row(given[prefix + n]) for n in VECTORS] + [jnp.zeros((2, D_MODEL), F32)], axis=0)
    g_small = jnp.concatenate([small[:6], jnp.zeros((2, D_MODEL), F32)], axis=0)
    d, m1, v1 = _adamw(stack(""), g_small, stack("m_"), stack("v_"), "adamw_vectors", SMALL_ROWS)
    for k, n in enumerate(VECTORS):
        width = weights[n].shape[0]
        delta[n], new_m[n], new_v[n] = d[k, :width], m1[k, :width], v1[k, :width]

    return (loss_total, grad_x, *[grads[n] for n in WEIGHT_ORDER], *[delta[n] for n in WEIGHT_ORDER],
            *[new_m[n] for n in WEIGHT_ORDER], *[new_v[n] for n in WEIGHT_ORDER])
```

```python
import functools
import math

import jax
import jax.numpy as jnp
from jax import lax
from jax.experimental import pallas as pl
from jax.experimental.pallas import tpu as pltpu

F32 = jnp.float32
BF16 = jnp.bfloat16

D_MODEL = 1024
SEQ = 2048
N_MEM = 256
HEAD_DIM = 64
N_HEADS = 8
MIX_HALF = N_HEADS * HEAD_DIM
QKV_WIDTH = 6 * MIX_HALF
IN_WIDTH = QKV_WIDTH + N_HEADS
GATE_PAD = 128
BLOCK = 128
DILATIONS = (1, 4, 16)
X_HEADS = 4
X_HEAD_DIM = 256
D_FF = 4096
EPS = 1e-6
NEG = -1e30
ATT_SCALE = 1.0 / math.sqrt(HEAD_DIM)
X_SCALE = 1.0 / math.sqrt(X_HEAD_DIM)
LANES = 128
N_CHIPS = 4
N_DEV = 8

ADAM_LR = 0.001
ADAM_B1 = 0.9
ADAM_B2 = 0.999
ADAM_EPS = 1e-08
ADAM_WD = 0.01
ADAM_STEP = 10

VMEM_LIMIT = 48 * 1024 * 1024


def _params(*sem):
    return pltpu.CompilerParams(dimension_semantics=sem or None, vmem_limit_bytes=VMEM_LIMIT)


def _dot(a, b):
    return jnp.dot(a, b, preferred_element_type=F32)


def _dot_nt(a, b):
    return lax.dot_general(a, b, (((1,), (1,)), ((), ())), preferred_element_type=F32)


def _dot_tn(a, b):
    return lax.dot_general(a, b, (((0,), (0,)), ((), ())), preferred_element_type=F32)


def _dot_exact(x, e):
    hi = x.astype(BF16)
    r1 = x - hi.astype(F32)
    mid = r1.astype(BF16)
    lo = (r1 - mid.astype(F32)).astype(BF16)
    return _dot(hi, e) + _dot(mid, e) + _dot(lo, e)


def _head_mask(e):
    lane = lax.broadcasted_iota(jnp.int32, (1, LANES), 1)
    return (lane >= HEAD_DIM * e) & (lane < HEAD_DIM * (e + 1))


def _matmul(a, w, name, out_dtypes=(F32,), extras=(), epilogue=None, tm=1024, tn=512, w_t=False):
    m, k = a.shape
    n = w.shape[0] if w_t else w.shape[1]
    tm, tn = min(tm, m), min(tn, n)
    assert m % tm == 0 and n % tn == 0, (name, a.shape, w.shape)
    n_ex = len(extras)

    def body(a_ref, w_ref, *rest):
        acc = (_dot_nt if w_t else _dot)(a_ref[...], w_ref[...])
        res = (acc,) if epilogue is None else epilogue(acc, *[r[...] for r in rest[:n_ex]])
        for o_ref, r in zip(rest[n_ex:], res):
            o_ref[...] = r.astype(o_ref.dtype)

    tile = pl.BlockSpec((tm, tn), lambda i, j: (i, j))
    w_spec = pl.BlockSpec((tn, k), lambda i, j: (j, 0)) if w_t else pl.BlockSpec((k, tn), lambda i, j: (0, j))
    return pl.pallas_call(
        body, grid=(m // tm, n // tn),
        in_specs=[pl.BlockSpec((tm, k), lambda i, j: (i, 0)), w_spec] + [tile] * n_ex,
        out_specs=[tile] * len(out_dtypes),
        out_shape=[jax.ShapeDtypeStruct((m, n), dt) for dt in out_dtypes],
        name=name, compiler_params=_params("parallel", "arbitrary"),
    )(a, w, *extras)


def _matmul_res(a, w, res, name, w_t=False):
    return _matmul(a, w, name, extras=(res,), epilogue=lambda acc, r: (r + acc,), w_t=w_t)[0]


def _matmul_tn(x, y, name, tm=1024, tn=1024, tk=512):
    t, m = x.shape
    _, n = y.shape
    tm, tn, tk = min(tm, m), min(tn, n), min(tk, t)
    assert m % tm == 0 and n % tn == 0 and t % tk == 0, (name, x.shape, y.shape)

    def body(x_ref, y_ref, o_ref):
        @pl.when(pl.program_id(2) == 0)
        def _():
            o_ref[...] = jnp.zeros_like(o_ref)

        o_ref[...] += _dot_tn(x_ref[...], y_ref[...])

    return pl.pallas_call(
        body, grid=(m // tm, n // tn, t // tk),
        in_specs=[pl.BlockSpec((tk, tm), lambda i, j, k: (k, i)), pl.BlockSpec((tk, tn), lambda i, j, k: (k, j))],
        out_specs=pl.BlockSpec((tm, tn), lambda i, j, k: (i, j)),
        out_shape=jax.ShapeDtypeStruct((m, n), F32),
        name=name, compiler_params=_params("parallel", "parallel", "arbitrary"),
    )(x, y)


def _rmsnorm(x, g, name, tm=512):
    t, d = x.shape
    tm = min(tm, t)

    def body(x_ref, g_ref, h_ref):
        xv = x_ref[...]
        r = lax.rsqrt(jnp.mean(xv * xv, axis=-1, keepdims=True) + EPS)
        h_ref[...] = (xv * r * g_ref[...]).astype(BF16)

    return pl.pallas_call(
        body, grid=(t // tm,),
        in_specs=[pl.BlockSpec((tm, d), lambda i: (i, 0)), pl.BlockSpec((1, d), lambda i: (0, 0))],
        out_specs=pl.BlockSpec((tm, d), lambda i: (i, 0)),
        out_shape=jax.ShapeDtypeStruct((t, d), BF16),
        name=name, compiler_params=_params("arbitrary"),
    )(x, g.reshape(1, d))


def _rms_bwd_tile(xv, dh, g):
    d = xv.shape[-1]
    r = lax.rsqrt(jnp.mean(xv * xv, axis=-1, keepdims=True) + EPS)
    dyg = dh * g
    proj = jnp.sum(dyg * xv, axis=-1, keepdims=True)
    dx = r * dyg - xv * (r * r * r * (1.0 / d)) * proj
    return dx, dh * (xv * r)


def _rms_bwd(x, dh, g, dres, name, tm=512):
    t, d = x.shape
    tm = min(tm, t)
    has_res = dres is not None

    def body(x_ref, dh_ref, g_ref, *rest):
        if has_res:
            res_ref, dx_ref, dxb_ref, dg_ref = rest
        else:
            dx_ref, dxb_ref, dg_ref = rest
        dx, dg_rows = _rms_bwd_tile(x_ref[...], dh_ref[...], g_ref[...])
        if has_res:
            dx = res_ref[...] + dx
        dx_ref[...] = dx
        dxb_ref[...] = dx.astype(BF16)

        @pl.when(pl.program_id(0) == 0)
        def _():
            dg_ref[...] = jnp.zeros_like(dg_ref)

        dg_ref[...] += jnp.sum(dg_rows, axis=0, keepdims=True)

    row = pl.BlockSpec((tm, d), lambda i: (i, 0))
    vec = pl.BlockSpec((1, d), lambda i: (0, 0))
    return pl.pallas_call(
        body, grid=(t // tm,),
        in_specs=[row, row, vec] + ([row] if has_res else []),
        out_specs=[row, row, vec],
        out_shape=[jax.ShapeDtypeStruct((t, d), F32), jax.ShapeDtypeStruct((t, d), BF16), jax.ShapeDtypeStruct((1, d), F32)],
        name=name, compiler_params=_params("arbitrary"),
    )(x, dh, g.reshape(1, d), *((dres,) if has_res else ()))


def _loss_bwd(x, g, target, name, tm=512):
    t, d = x.shape

    def body(x_ref, g_ref, t_ref, loss_ref, dx_ref, dxb_ref, dg_ref):
        xv = x_ref[...]
        gv = g_ref[...]
        r = lax.rsqrt(jnp.mean(xv * xv, axis=-1, keepdims=True) + EPS)
        err = xv * r * gv - t_ref[...]
        dx, dg_rows = _rms_bwd_tile(xv, err * (1.0 / d), gv)
        dx_ref[...] = dx
        dxb_ref[...] = dx.astype(BF16)

        @pl.when(pl.program_id(0) == 0)
        def _():
            dg_ref[...] = jnp.zeros_like(dg_ref)
            loss_ref[...] = jnp.zeros_like(loss_ref)

        dg_ref[...] += jnp.sum(dg_rows, axis=0, keepdims=True)
        part = jnp.sum(jnp.sum(err * err, axis=0, keepdims=True), axis=1, keepdims=True) * (0.5 / d)
        loss_ref[...] += jnp.broadcast_to(part, loss_ref.shape)

    row = pl.BlockSpec((tm, d), lambda i: (i, 0))
    vec = pl.BlockSpec((1, d), lambda i: (0, 0))
    return pl.pallas_call(
        body, grid=(t // tm,),
        in_specs=[row, vec, row],
        out_specs=[pl.BlockSpec((1, LANES), lambda i: (0, 0)), row, row, vec],
        out_shape=[jax.ShapeDtypeStruct((1, LANES), F32), jax.ShapeDtypeStruct((t, d), F32),
                   jax.ShapeDtypeStruct((t, d), BF16), jax.ShapeDtypeStruct((1, d), F32)],
        name=name, compiler_params=_params("arbitrary"),
    )(x, g.reshape(1, d), target)


def _tri(upper):
    r = lax.broadcasted_iota(jnp.int32, (LANES, LANES), 0)
    c = lax.broadcasted_iota(jnp.int32, (LANES, LANES), 1)
    return jnp.where((r <= c) if upper else (r >= c), 1.0, 0.0).astype(BF16)


def _gate_fwd(gate, b_pad, n_batch, name):
    s = SEQ
    nblk = s // LANES

    def body(g_ref, b_ref, cbc_ref, crow_ref, sg_ref, ct_ref):
        gz = g_ref[...] + b_ref[...]
        logf = jnp.minimum(gz, 0.0) - jnp.log(1.0 + jnp.exp(-jnp.abs(gz)))
        logf_t = logf.T
        sg_ref[...] = (1.0 / (1.0 + jnp.exp(gz))).T[0:N_HEADS]
        upper = _tri(True)
        carry = jnp.zeros((LANES, 1), F32)
        for blk in range(nblk):
            seg = _dot_exact(logf_t[:, blk * LANES:(blk + 1) * LANES], upper) + carry
            carry = seg[:, LANES - 1:LANES]
            ct_ref[:, blk * LANES:(blk + 1) * LANES] = seg
        ct = ct_ref[...]
        crow_ref[...] = ct[0:N_HEADS]
        c_col = ct.T
        lane = lax.broadcasted_iota(jnp.int32, (1, MIX_HALF), 1)
        acc = jnp.zeros((s, MIX_HALF), F32)
        for h in range(N_HEADS):
            acc = jnp.where((lane >= HEAD_DIM * h) & (lane < HEAD_DIM * (h + 1)), c_col[:, h:h + 1], acc)
        cbc_ref[...] = acc

    return pl.pallas_call(
        body, grid=(n_batch,),
        in_specs=[pl.BlockSpec((s, GATE_PAD), lambda b: (b, 0)), pl.BlockSpec((1, GATE_PAD), lambda b: (0, 0))],
        out_specs=[pl.BlockSpec((s, MIX_HALF), lambda b: (b, 0)),
                   pl.BlockSpec((None, N_HEADS, s), lambda b: (b, 0, 0)),
                   pl.BlockSpec((None, N_HEADS, s), lambda b: (b, 0, 0))],
        out_shape=[jax.ShapeDtypeStruct((n_batch * s, MIX_HALF), F32),
                   jax.ShapeDtypeStruct((n_batch, N_HEADS, s), F32),
                   jax.ShapeDtypeStruct((n_batch, N_HEADS, s), F32)],
        scratch_shapes=[pltpu.VMEM((LANES, s), F32)],
        name=name, compiler_params=_params("arbitrary"),
    )(gate, b_pad)


def _gate_bwd(dc, sg, name):
    n_batch, _, s = dc.shape
    nblk = s // LANES

    def body(dc_ref, sg_ref, dz_ref, db_ref, dt_ref):
        lower = _tri(False)
        dcv = dc_ref[...]
        carry = jnp.zeros((N_HEADS, 1), F32)
        dt_ref[...] = jnp.zeros_like(dt_ref)
        for blk in reversed(range(nblk)):
            seg = _dot_exact(dcv[:, blk * LANES:(blk + 1) * LANES], lower) + carry
            carry = seg[:, 0:1]
            dt_ref[0:N_HEADS, blk * LANES:(blk + 1) * LANES] = seg * sg_ref[:, blk * LANES:(blk + 1) * LANES]
        dg_t = dt_ref[...]
        dz_ref[...] = dg_t.T.astype(BF16)

        @pl.when(pl.program_id(0) == 0)
        def _():
            db_ref[...] = jnp.zeros_like(db_ref)

        db_ref[...] += jnp.broadcast_to(jnp.sum(dg_t[0:N_HEADS], axis=1, keepdims=True), db_ref.shape)

    return pl.pallas_call(
        body, grid=(n_batch,),
        in_specs=[pl.BlockSpec((None, N_HEADS, s), lambda b: (b, 0, 0)), pl.BlockSpec((None, N_HEADS, s), lambda b: (b, 0, 0))],
        out_specs=[pl.BlockSpec((s, GATE_PAD), lambda b: (b, 0)), pl.BlockSpec((N_HEADS, LANES), lambda b: (0, 0))],
        out_shape=[jax.ShapeDtypeStruct((n_batch * s, GATE_PAD), BF16), jax.ShapeDtypeStruct((N_HEADS, LANES), F32)],
        scratch_shapes=[pltpu.VMEM((LANES, s), F32)],
        name=name, compiler_params=_params("arbitrary"),
    )(dc, sg)


FOX_BQ = 512
FOX_BK = 512
PAIR_WIDTH = 3 * LANES
N_PAIRS = N_HEADS // 2


def _pair_major(w):
    return w.reshape(w.shape[0], 3, N_PAIRS, LANES).transpose(0, 2, 1, 3).reshape(w.shape[0], 3 * MIX_HALF)


def _pair_major_inv(w):
    return w.reshape(w.shape[0], N_PAIRS, 3, LANES).transpose(0, 2, 1, 3).reshape(w.shape[0], 3 * MIX_HALF)


def _causal(i, j, bq, bk):
    qpos = i * bq + lax.broadcasted_iota(jnp.int32, (bq, 1), 0)
    kpos = j * bk + lax.broadcasted_iota(jnp.int32, (1, bk), 1)
    return kpos <= qpos


def _split_bf16(p):
    hi = p.astype(BF16)
    return hi, (p - hi.astype(F32)).astype(BF16)


def _fox_fwd(zf, c_bc, c_row, n_batch, name):
    s, bq, bk = SEQ, FOX_BQ, FOX_BK
    nq = s // bq
    t = n_batch * s

    def body(q_ref, k_ref, v_ref, cq_ref, cr_ref, o_ref, o32_ref, lse_ref):
        hp, i = pl.program_id(1), pl.program_id(2)
        q = q_ref[...] * ATT_SCALE
        qh = [jnp.where(_head_mask(e), q, jnp.zeros_like(q)) for e in range(2)]
        cq = [cq_ref[:, HEAD_DIM * e:HEAD_DIM * e + 1] for e in range(2)]

        def step(j, carry, masked):
            rows = pl.ds(pl.multiple_of(j * bk, bk), bk)
            kj, vj = k_ref[rows, :], v_ref[rows, :]
            out = []
            for e in range(2):
                m, l, acc = carry[3 * e:3 * e + 3]
                sc = _dot_nt(qh[e], kj) + (cq[e] - cr_ref[pl.ds(2 * hp + e, 1), rows])
                if masked:
                    sc = jnp.where(_causal(i, j, bq, bk), sc, NEG)
                m_new = jnp.maximum(m, jnp.max(sc, axis=1, keepdims=True))
                alpha = jnp.exp(m - m_new)
                p = jnp.exp(sc - m_new)
                p_hi, p_lo = _split_bf16(p)
                out += [m_new, alpha * l + jnp.sum(p, axis=1, keepdims=True), alpha * acc + (_dot(p_hi, vj) + _dot(p_lo, vj))]
            return tuple(out)

        init = (jnp.full((bq, 1), NEG, F32), jnp.zeros((bq, 1), F32), jnp.zeros((bq, LANES), F32)) * 2
        n_clear = (i * bq) // bk
        carry = lax.fori_loop(0, n_clear, functools.partial(step, masked=False), init)
        carry = lax.fori_loop(n_clear, (i * bq + bq + bk - 1) // bk, functools.partial(step, masked=True), carry)
        outs = [carry[3 * e + 2] / carry[3 * e + 1] for e in range(2)]
        lses = [carry[3 * e] + jnp.log(carry[3 * e + 1]) for e in range(2)]
        o = jnp.where(_head_mask(0), outs[0], outs[1])
        o_ref[...] = o.astype(BF16)
        o32_ref[...] = o
        lse_ref[...] = jnp.where(_head_mask(0), lses[0], lses[1])

    def col(c0):
        return lambda b, hp, i: (b, 3 * hp + c0)

    blk = pl.BlockSpec((bq, LANES), lambda b, hp, i: (b * nq + i, hp))
    return pl.pallas_call(
        body, grid=(n_batch, N_PAIRS, nq),
        in_specs=[pl.BlockSpec((bq, LANES), lambda b, hp, i: (b * nq + i, 3 * hp)),
                  pl.BlockSpec((s, LANES), col(1)), pl.BlockSpec((s, LANES), col(2)), blk,
                  pl.BlockSpec((None, N_HEADS, s), lambda b, hp, i: (b, 0, 0))],
        out_specs=[blk, blk, blk],
        out_shape=[jax.ShapeDtypeStruct((t, MIX_HALF), BF16), jax.ShapeDtypeStruct((t, MIX_HALF), F32),
                   jax.ShapeDtypeStruct((t, MIX_HALF), F32)],
        name=name, compiler_params=_params("parallel", "parallel", "arbitrary"),
    )(zf, zf, zf, c_bc, c_row)


def _fox_bwd(zf, o32, dy, lse, c_bc, c_row, dz, n_batch, name):
    s, bq, bk = SEQ, FOX_BQ, FOX_BK
    nq, nk = s // bq, s // bk

    def body(q_ref, k_ref, v_ref, o_ref, do_ref, lse_ref, cq_ref, cr_ref, dz_in, dz_ref, dc_ref, dq_acc):
        del dz_in
        hp, j = pl.program_id(1), pl.program_id(2)

        @pl.when(j == 0)
        def _():
            dq_acc[...] = jnp.zeros_like(dq_acc)

        kj, vj = k_ref[...], v_ref[...]
        cols = pl.ds(pl.multiple_of(j * bk, bk), bk)
        km = [jnp.where(_head_mask(e), kj, jnp.zeros_like(kj)) for e in range(2)]
        ck = [cr_ref[pl.ds(2 * hp + e, 1), cols] for e in range(2)]

        def step(i, carry, masked):
            rows = pl.ds(pl.multiple_of(i * bq, bq), bq)
            qi, doi = q_ref[rows, :] * ATT_SCALE, do_ref[rows, :]
            prod = doi.astype(F32) * o_ref[rows, :]
            out = []
            dq = jnp.zeros((bq, LANES), F32)
            for e in range(2):
                dk_a, dv_a, dc_a = carry[3 * e:3 * e + 3]
                mask = _head_mask(e)
                lane0 = HEAD_DIM * e
                dom = jnp.where(mask, doi, jnp.zeros_like(doi))
                delta = jnp.sum(jnp.where(mask, prod, 0.0), axis=1, keepdims=True)
                sc = _dot_nt(qi, km[e]) + (cq_ref[rows, lane0:lane0 + 1] - ck[e])
                if masked:
                    sc = jnp.where(_causal(i, j, bq, bk), sc, NEG)
                p = jnp.exp(sc - lse_ref[rows, lane0:lane0 + 1])
                ds = p * (_dot_nt(dom, vj) - delta)
                dsb = ds.astype(BF16)
                dq = dq + _dot(dsb, km[e])
                out += [dk_a + _dot_tn(dsb, qi), dv_a + _dot_tn(p.astype(BF16), dom), dc_a - jnp.sum(ds, axis=0, keepdims=True)]
            dq_acc[rows, :] += dq * ATT_SCALE
            return tuple(out)

        init = (jnp.zeros((bk, LANES), F32), jnp.zeros((bk, LANES), F32), jnp.zeros((1, bk), F32)) * 2
        first = (j * bk) // bq
        n_diag = (j * bk + bk + bq - 1) // bq
        carry = lax.fori_loop(first, n_diag, functools.partial(step, masked=True), init)
        carry = lax.fori_loop(n_diag, nq, functools.partial(step, masked=False), carry)
        for e in range(2):
            dc_ref[e:e + 1, :] = carry[3 * e + 2]
        dz_ref[cols, LANES:2 * LANES] = jnp.where(_head_mask(0), carry[0], carry[3]).astype(BF16)
        dz_ref[cols, 2 * LANES:3 * LANES] = (carry[1] + carry[4]).astype(BF16)

        @pl.when(j == nk - 1)
        def _():
            dz_ref[:, 0:LANES] = dq_acc[...].astype(BF16)

    def seq(idx):
        return pl.BlockSpec((s, LANES), lambda b, hp, j: (b, idx(hp)))

    def kblk(c0):
        return pl.BlockSpec((bk, LANES), lambda b, hp, j: (b * nk + j, 3 * hp + c0))

    return pl.pallas_call(
        body, grid=(n_batch, N_PAIRS, nk),
        in_specs=[seq(lambda hp: 3 * hp), kblk(1), kblk(2), seq(lambda hp: hp), seq(lambda hp: N_PAIRS + hp),
                  seq(lambda hp: hp), seq(lambda hp: hp),
                  pl.BlockSpec((None, N_HEADS, s), lambda b, hp, j: (b, 0, 0)), pl.BlockSpec(memory_space=pl.ANY)],
        out_specs=[pl.BlockSpec((s, PAIR_WIDTH), lambda b, hp, j: (b, N_PAIRS + hp)),
                   pl.BlockSpec((None, None, 2, bk), lambda b, hp, j: (b, hp, 0, j))],
        out_shape=[jax.ShapeDtypeStruct(dz.shape, dz.dtype), jax.ShapeDtypeStruct((n_batch, N_PAIRS, 2, s), F32)],
        scratch_shapes=[pltpu.VMEM((s, LANES), F32)],
        input_output_aliases={8: 0},
        name=name, compiler_params=_params("parallel", "parallel", "arbitrary"),
    )(zf, zf, zf, o32, dy, lse, c_bc, c_row, dz)


def _dil_bias(slope, dil):
    qi = lax.broadcasted_iota(jnp.int32, (BLOCK, 2 * BLOCK), 0)
    kj = lax.broadcasted_iota(jnp.int32, (BLOCK, 2 * BLOCK), 1)
    delta = qi + BLOCK - kj
    return jnp.where((delta >= 0) & (delta <= BLOCK), (-slope * dil) * delta.astype(F32), NEG)


def _alibi_slope(hp, e):
    slope = jnp.float32(0.0)
    for k in range(N_PAIRS):
        slope = jnp.where(hp == k, jnp.float32(2.0 ** -(2 * k + e + 1)), slope)
    return slope


def _fill_bias(bias_scr, hp):
    for di, dil in enumerate(DILATIONS):
        for e in range(2):
            bias_scr[2 * di + e] = _dil_bias(_alibi_slope(hp, e), dil)


def _pair_specs(rows):
    return [pl.BlockSpec((rows, LANES), lambda b, hp, c0=c0: (b, 3 * hp + c0)) for c0 in range(3)]


def _strided(start, size, dil):
    return pl.ds(start, size) if dil == 1 else pl.ds(start, size, stride=dil)


def _for_each_block(dil, unit):
    span = BLOCK * dil
    nb = SEQ // span
    if dil == 1:
        group = 3
        assert (nb - 1) % group == 0
        unit(0, True)

        def later(g, c):
            for u in range(group):
                unit((1 + g * group + u) * span, False)
            return c

        lax.fori_loop(0, (nb - 1) // group, later, 0)
        return
    group = 4
    per = dil // group

    def firsts(g, c):
        for u in range(group):
            unit(g * group + u, True)
        return c

    lax.fori_loop(0, per, firsts, 0)
    if nb > 1:
        def later(i, c):
            for u in range(group):
                unit((1 + i // per) * span + (i % per) * group + u, False)
            return c

        lax.fori_loop(0, (nb - 1) * per, later, 0)


def _mix_weights(l1, l2, l3):
    m = jnp.maximum(jnp.maximum(l1, l2), l3)
    e1, e2, e3 = jnp.exp(l1 - m), jnp.exp(l2 - m), jnp.exp(l3 - m)
    inv = 1.0 / (e1 + e2 + e3)
    return e1 * inv, e2 * inv, e3 * inv


def _dil_fwd(zd, n_batch, name):
    s = SEQ
    t = n_batch * s

    def body(q_ref, k_ref, v_ref, y_ref, l1_ref, l2_ref, l3_ref, o_scr, bias_scr):
        _fill_bias(bias_scr, pl.program_id(1))
        lse_refs = (l1_ref, l2_ref, l3_ref)
        for di, dil in enumerate(DILATIONS):

            def unit(start, first, di=di, dil=dil):
                qrows = _strided(start, BLOCK, dil)
                krows = qrows if first else _strided(start - BLOCK * dil, 2 * BLOCK, dil)
                q = (q_ref[qrows, :] * ATT_SCALE).astype(BF16)
                kc = k_ref[krows, :].astype(BF16)
                vc = v_ref[krows, :].astype(BF16)
                outs, lses = [], []
                for e in range(2):
                    bias = bias_scr[2 * di + e]
                    sc = _dot_nt(jnp.where(_head_mask(e), q, jnp.zeros_like(q)), kc) + (bias[:, BLOCK:] if first else bias)
                    m = jnp.max(sc, axis=1, keepdims=True)
                    pe = jnp.exp(sc - m)
                    l = jnp.sum(pe, axis=1, keepdims=True)
                    outs.append(_dot((pe * (1.0 / l)).astype(BF16), vc))
                    lses.append(m + jnp.log(l))
                o_scr.at[di][qrows, :] = jnp.where(_head_mask(0), outs[0], outs[1])
                lse_refs[di][qrows, :] = jnp.where(_head_mask(0), lses[0], lses[1])

            _for_each_block(dil, unit)
        w = _mix_weights(l1_ref[...], l2_ref[...], l3_ref[...])
        y_ref[...] = (w[0] * o_scr[0] + w[1] * o_scr[1] + w[2] * o_scr[2]).astype(BF16)

    blk = pl.BlockSpec((s, LANES), lambda b, hp: (b, hp))
    res = pl.pallas_call(
        body, grid=(n_batch, N_PAIRS),
        in_specs=_pair_specs(s),
        out_specs=[blk] * 4,
        out_shape=[jax.ShapeDtypeStruct((t, MIX_HALF), BF16)] + [jax.ShapeDtypeStruct((t, MIX_HALF), F32)] * 3,
        scratch_shapes=[pltpu.VMEM((3, s, LANES), F32), pltpu.VMEM((6, BLOCK, 2 * BLOCK), F32)],
        name=name, compiler_params=_params("parallel", "arbitrary"),
    )(zd, zd, zd)
    return res[0], res[1:]


def _dil_bwd(zd, dy, ya, lses, n_batch, name):
    s = SEQ
    t = n_batch * s

    def body(q_ref, k_ref, v_ref, dy_ref, ya_ref, l1_ref, l2_ref, l3_ref, dz_ref, w_scr, dy_scr, dot_scr, acc, bias_scr):
        _fill_bias(bias_scr, pl.program_id(1))
        for di, w in enumerate(_mix_weights(l1_ref[...], l2_ref[...], l3_ref[...])):
            w_scr[di] = w
        dya = dy_ref[...].astype(F32)
        prod = dya * ya_ref[...].astype(F32)
        per_head = [jnp.sum(jnp.where(_head_mask(e), prod, 0.0), axis=1, keepdims=True) for e in range(2)]
        dy_scr[...] = dya
        dot_scr[...] = jnp.where(_head_mask(0), per_head[0], per_head[1])
        acc[...] = jnp.zeros_like(acc)
        lse_refs = (l1_ref, l2_ref, l3_ref)
        for di, dil in enumerate(DILATIONS):

            def unit(start, first, di=di, dil=dil):
                qrows = _strided(start, BLOCK, dil)
                krows = qrows if first else _strided(start - BLOCK * dil, 2 * BLOCK, dil)
                q = (q_ref[qrows, :] * ATT_SCALE).astype(BF16)
                kc = k_ref[krows, :].astype(BF16)
                vc = v_ref[krows, :].astype(BF16)
                wq = w_scr.at[di][qrows, :]
                do = (wq * dy_scr[qrows, :]).astype(BF16)
                sub = wq * dot_scr[qrows, :]
                lse = lse_refs[di][qrows, :]
                dq = jnp.zeros((BLOCK, LANES), F32)
                dk = jnp.zeros((krows.size, LANES), F32)
                dv = jnp.zeros((krows.size, LANES), F32)
                for e in range(2):
                    mask = _head_mask(e)
                    lane0 = HEAD_DIM * e
                    qh = jnp.where(mask, q, jnp.zeros_like(q))
                    doh = jnp.where(mask, do, jnp.zeros_like(do))
                    bias = bias_scr[2 * di + e]
                    sc = _dot_nt(qh, kc) + (bias[:, BLOCK:] if first else bias)
                    p = jnp.exp(sc - lse[:, lane0:lane0 + 1])
                    dsb = (p * (_dot_nt(doh, vc) - sub[:, lane0:lane0 + 1])).astype(BF16)
                    dq = dq + _dot(dsb, jnp.where(mask, kc, jnp.zeros_like(kc)))
                    dk = dk + _dot_tn(dsb, qh)
                    dv = dv + _dot_tn(p.astype(BF16), doh)
                acc.at[0][qrows, :] += dq * ATT_SCALE
                acc.at[1][krows, :] += dk
                acc.at[2][krows, :] += dv

            _for_each_block(dil, unit)
        for k in range(3):
            dz_ref[:, k * LANES:(k + 1) * LANES] = acc[k].astype(BF16)

    blk = pl.BlockSpec((s, LANES), lambda b, hp: (b, hp))
    pair = pl.BlockSpec((s, PAIR_WIDTH), lambda b, hp: (b, hp))
    return pl.pallas_call(
        body, grid=(n_batch, N_PAIRS),
        in_specs=_pair_specs(s) + [blk] * 5,
        out_specs=pair,
        out_shape=jax.ShapeDtypeStruct((t, 2 * 3 * MIX_HALF), BF16),
        scratch_shapes=[pltpu.VMEM((3, s, LANES), F32), pltpu.VMEM((s, LANES), F32), pltpu.VMEM((s, LANES), F32),
                        pltpu.VMEM((3, s, LANES), F32), pltpu.VMEM((6, BLOCK, 2 * BLOCK), F32)],
        name=name, compiler_params=_params("parallel", "arbitrary"),
    )(zd, zd, zd, dy, ya, *lses)


X_BQ = 512


def _xattn_probs(q, k):
    sc = _dot_nt(q, k) * X_SCALE
    pe = jnp.exp(sc - jnp.max(sc, axis=1, keepdims=True))
    return pe / jnp.sum(pe, axis=1, keepdims=True)


def _xattn_fwd(qx, kx, vx, n_batch, name):
    nq = SEQ // X_BQ

    def body(q_ref, k_ref, v_ref, o_ref):
        p = _xattn_probs(q_ref[...], k_ref[...])
        o_ref[...] = _dot(p.astype(BF16), v_ref[...]).astype(BF16)

    qblk = pl.BlockSpec((X_BQ, X_HEAD_DIM), lambda b, h, i: (b * nq + i, h))
    kblk = pl.BlockSpec((N_MEM, X_HEAD_DIM), lambda b, h, i: (b, h))
    return pl.pallas_call(
        body, grid=(n_batch, X_HEADS, nq), in_specs=[qblk, kblk, kblk], out_specs=qblk,
        out_shape=jax.ShapeDtypeStruct(qx.shape, BF16),
        name=name, compiler_params=_params("parallel", "parallel", "arbitrary"),
    )(qx, kx, vx)


def _xattn_bwd(qx, kx, vx, dox, n_batch, name):
    nq = SEQ // X_BQ

    def body(q_ref, k_ref, v_ref, do_ref, dq_ref, dk_ref, dv_ref, dk_acc, dv_acc):
        i = pl.program_id(2)

        @pl.when(i == 0)
        def _():
            dk_acc[...] = jnp.zeros_like(dk_acc)
            dv_acc[...] = jnp.zeros_like(dv_acc)

        q, k, do = q_ref[...], k_ref[...], do_ref[...]
        p = _xattn_probs(q, k)
        dp = _dot_nt(do, v_ref[...])
        dsb = (p * (dp - jnp.sum(p * dp, axis=1, keepdims=True))).astype(BF16)
        dq_ref[...] = (_dot(dsb, k) * X_SCALE).astype(BF16)
        dk_acc[...] += _dot_tn(dsb, q) * X_SCALE
        dv_acc[...] += _dot_tn(p.astype(BF16), do)

        @pl.when(i == nq - 1)
        def _():
            dk_ref[...] = dk_acc[...].astype(BF16)
            dv_ref[...] = dv_acc[...].astype(BF16)

    qblk = pl.BlockSpec((X_BQ, X_HEAD_DIM), lambda b, h, i: (b * nq + i, h))
    kblk = pl.BlockSpec((N_MEM, X_HEAD_DIM), lambda b, h, i: (b, h))
    return pl.pallas_call(
        body, grid=(n_batch, X_HEADS, nq), in_specs=[qblk, kblk, kblk, qblk], out_specs=[qblk, kblk, kblk],
        out_shape=[jax.ShapeDtypeStruct(qx.shape, BF16), jax.ShapeDtypeStruct(kx.shape, BF16), jax.ShapeDtypeStruct(kx.shape, BF16)],
        scratch_shapes=[pltpu.VMEM((N_MEM, X_HEAD_DIM), F32)] * 2,
        name=name, compiler_params=_params("parallel", "parallel", "arbitrary"),
    )(qx, kx, vx, dox)


def _adamw(w, g, m, v, name, rows):
    r, c = w.shape
    assert r % rows == 0, (name, w.shape, rows)

    def body(w_ref, g_ref, m_ref, v_ref, d_ref, nm_ref, nv_ref):
        gv = g_ref[...]
        m1 = ADAM_B1 * m_ref[...] + (1.0 - ADAM_B1) * gv
        v1 = ADAM_B2 * v_ref[...] + (1.0 - ADAM_B2) * jnp.square(gv)
        m_hat = m1 / (1.0 - ADAM_B1 ** ADAM_STEP)
        v_hat = v1 / (1.0 - ADAM_B2 ** ADAM_STEP)
        d_ref[...] = -ADAM_LR * (m_hat / (jnp.sqrt(v_hat) + ADAM_EPS) + ADAM_WD * w_ref[...])
        nm_ref[...] = m1
        nv_ref[...] = v1

    blk = pl.BlockSpec((rows, c), lambda i: (i, 0))
    return pl.pallas_call(
        body, grid=(r // rows,), in_specs=[blk] * 4, out_specs=[blk] * 3,
        out_shape=[jax.ShapeDtypeStruct((r, c), F32)] * 3,
        name=name, compiler_params=_params("arbitrary"),
    )(w, g, m, v)


def _relu2(acc):
    a = jnp.maximum(acc, 0.0)
    return acc, a * a


def _relu2_bwd(acc, u):
    return (2.0 * jnp.maximum(u.astype(F32), 0.0) * acc,)


def _tie(value, token):
    return value if token is None else lax.optimization_barrier((value, token))[0]


def _local_step(x, mem, target, vecs, w_in, late_weights, on_grads=None):
    n_batch = x.shape[0]
    t = n_batch * SEQ
    x0 = x.reshape(t, D_MODEL)
    mem2 = mem.reshape(n_batch * N_MEM, D_MODEL)
    tgt = target.reshape(t, D_MODEL)

    half = 3 * MIX_HALF
    w_dil, w_fox = _pair_major(w_in[:, :half]), _pair_major(w_in[:, half:QKV_WIDTH])
    w_gate = jnp.pad(w_in[:, QKV_WIDTH:], ((0, 0), (0, GATE_PAD - N_HEADS)))
    b_pad = jnp.pad(vecs["b_forget"], (0, GATE_PAD - N_HEADS)).reshape(1, GATE_PAD)

    h1 = _rmsnorm(x0, vecs["g_mix"], "norm_mix")
    mn = _rmsnorm(mem2, vecs["g_mem"], "norm_mem")
    zd = _matmul(h1, w_dil, "in_dil", tn=768)[0]
    zf = _matmul(h1, w_fox, "in_fox", out_dtypes=(BF16,), tn=768)[0]
    gate = _matmul(h1, w_gate, "in_gate")[0]
    c_bc, c_row, sg = _gate_fwd(gate, b_pad, n_batch, "gate_fwd")
    ya, lses = _dil_fwd(zd, n_batch, "dil_fwd")
    yf, of32, lse_f = _fox_fwd(zf, c_bc, c_row, n_batch, "fox_fwd")
    wts = late_weights(yf)
    w_out = wts["w_out"]
    x1 = _matmul_res(ya, w_out[:MIX_HALF], x0, "out_a")
    x1 = _matmul_res(yf, w_out[MIX_HALF:], x1, "out_f")
    h2 = _rmsnorm(x1, vecs["g_xattn"], "norm_xattn")
    qx = _matmul(h2, wts["w_xq"], "xq", out_dtypes=(BF16,))[0]
    kx = _matmul(mn, wts["w_xk"], "xk", out_dtypes=(BF16,))[0]
    vx = _matmul(mn, wts["w_xv"], "xv", out_dtypes=(BF16,))[0]
    ox = _xattn_fwd(qx, kx, vx, n_batch, "xattn_fwd")
    x2 = _matmul_res(ox, wts["w_xo"], x1, "xo")
    h3 = _rmsnorm(x2, vecs["g_mlp"], "norm_mlp")
    u, a2 = _matmul(h3, wts["w_up"], "mlp_up", out_dtypes=(BF16, BF16), epilogue=_relu2, tn=1024)
    x3 = _matmul_res(a2, wts["w_down"], x2, "mlp_down")
    loss, dx3, dx3b, dg_final = _loss_bwd(x3, vecs["g_final"], tgt, "loss")

    du = _matmul(dx3b, wts["w_down"], "mlp_down_bwd", out_dtypes=(BF16,), extras=(u,), epilogue=_relu2_bwd, tn=1024, w_t=True)[0]
    gw_down = _matmul_tn(a2, dx3b, "gw_down")
    gw_up = _matmul_tn(h3, du, "gw_up")
    token = on_grads("mlp", dict(w_up=gw_up, w_down=gw_down)) if on_grads else None
    dh3 = _matmul(_tie(du, token), wts["w_up"], "mlp_up_bwd", w_t=True)[0]
    dx2, dx2b, dg_mlp = _rms_bwd(x2, dh3, vecs["g_mlp"], dx3, "norm_mlp_bwd")

    gw_xo = _matmul_tn(ox, dx2b, "gw_xo")
    dox = _matmul(dx2b, wts["w_xo"], "xo_bwd", out_dtypes=(BF16,), w_t=True)[0]
    dqx, dkx, dvx = _xattn_bwd(qx, kx, vx, dox, n_batch, "xattn_bwd")
    gw_xq = _matmul_tn(h2, dqx, "gw_xq")
    gw_xk = _matmul_tn(mn, dkx, "gw_xk")
    gw_xv = _matmul_tn(mn, dvx, "gw_xv")
    dh2 = _matmul(dqx, wts["w_xq"], "xq_bwd", w_t=True)[0]
    dmn = _matmul(dkx, wts["w_xk"], "xk_bwd", w_t=True)[0]
    dmn = _matmul_res(dvx, wts["w_xv"], dmn, "xv_bwd", w_t=True)
    _, _, dg_mem = _rms_bwd(mem2, dmn, vecs["g_mem"], None, "norm_mem_bwd")
    dx1, dx1b, dg_xattn = _rms_bwd(x1, dh2, vecs["g_xattn"], dx2, "norm_xattn_bwd")

    gw_out = jnp.concatenate([_matmul_tn(ya, dx1b, "gw_out_a"), _matmul_tn(yf, dx1b, "gw_out_f")], axis=0)
    token = on_grads("mid", dict(w_out=gw_out, w_xq=gw_xq, w_xk=gw_xk, w_xv=gw_xv, w_xo=gw_xo)) if on_grads else None
    dy = _matmul(_tie(dx1b, token), w_out, "out_bwd", out_dtypes=(BF16,), w_t=True)[0]
    dz = _dil_bwd(zd, dy, ya, lses, n_batch, "dil_bwd")
    dz, dc = _fox_bwd(zf, of32, dy, lse_f, c_bc, c_row, dz, n_batch, "fox_bwd")
    dzg, db = _gate_bwd(dc.reshape(n_batch, N_HEADS, SEQ), sg, "gate_bwd")
    gw_pm = _matmul_tn(h1, dz, "gw_in_qkv")
    gw_in = jnp.concatenate([_pair_major_inv(gw_pm[:, :half]), _pair_major_inv(gw_pm[:, half:]),
                             _matmul_tn(h1, dzg, "gw_in_gate")[:, :N_HEADS]], axis=1)
    dh1 = _matmul(dz, jnp.concatenate([w_dil, w_fox], axis=1), "in_qkv_bwd", w_t=True)[0]
    dh1 = _matmul_res(dzg, w_gate, dh1, "in_gate_bwd", w_t=True)
    dx0, _, dg_mix = _rms_bwd(x0, dh1, vecs["g_mix"], dx1, "norm_mix_bwd")

    gw = dict(w_in=gw_in, w_out=gw_out, w_xq=gw_xq, w_xk=gw_xk, w_xv=gw_xv, w_xo=gw_xo, w_up=gw_up, w_down=gw_down)
    gv = dict(g_mix=dg_mix, g_xattn=dg_xattn, g_mem=dg_mem, g_mlp=dg_mlp, g_final=dg_final, b_forget=db)
    return loss, dx0.reshape(x.shape), gw, gv


MESH = pl.DeviceIdType.MESH
ANY = pl.BlockSpec(memory_space=pl.ANY)


def _place():
    x, y, c = lax.axis_index("x"), lax.axis_index("y"), lax.axis_index("c")
    other_chips = [(1 - x, y), (x, 1 - y), (1 - x, 1 - y)]
    return x, y, c, other_chips


def _my_chip():
    return 2 * lax.axis_index("x") + lax.axis_index("y")


def _halves(rows, c, align):
    half = rows // 2
    assert rows % (2 * align) == 0, rows
    return pl.ds(pl.multiple_of(c * half, align), half), pl.ds(pl.multiple_of((1 - c) * half, align), half)


def _place_own(wall, pack):
    return lax.dynamic_update_slice(wall, pack[None], (_my_chip(), 0, 0))


def _gather(pack, name, after):
    def body(p_ref, after_ref, out_ref, send_sems, recv_sems, pass_send, pass_recv):
        del after_ref
        x, y, c, chips = _place()
        me = 2 * x + y
        mine, theirs = _halves(pack.shape[0], c, 16)

        def from_chip(k, chip, rows):
            src = out_ref.at[2 * chip[0] + chip[1], rows]
            return pltpu.make_async_remote_copy(src_ref=src, dst_ref=src, send_sem=send_sems.at[k], recv_sem=recv_sems.at[k],
                                                device_id=(chip[0], chip[1], c), device_id_type=MESH)

        def passed(k, chip, rows):
            src = out_ref.at[2 * chip[0] + chip[1], rows]
            return pltpu.make_async_remote_copy(src_ref=src, dst_ref=src, send_sem=pass_send.at[k], recv_sem=pass_recv.at[k],
                                                device_id=(x, y, 1 - c), device_id_type=MESH)

        sends = []
        for k, chip in enumerate(chips):
            cp = pltpu.make_async_remote_copy(src_ref=p_ref.at[mine], dst_ref=out_ref.at[me, mine], send_sem=send_sems.at[k],
                                              recv_sem=recv_sems.at[k], device_id=(chip[0], chip[1], c), device_id_type=MESH)
            cp.start()
            sends.append(cp)
        for k, chip in enumerate(chips):
            from_chip(k, chip, mine).wait_recv()
            cp = passed(k, chip, mine)
            cp.start()
            sends.append(cp)
        for k, chip in enumerate(chips):
            passed(k, chip, theirs).wait_recv()
        for cp in sends:
            cp.wait_send()

    wall = pl.pallas_call(
        body, in_specs=[ANY, ANY], out_specs=ANY,
        out_shape=jax.ShapeDtypeStruct((N_CHIPS,) + pack.shape, pack.dtype),
        scratch_shapes=[pltpu.SemaphoreType.DMA((3,))] * 4,
        name=name,
    )(pack, after)
    return _place_own(wall, pack)


HBM = pl.BlockSpec(memory_space=pltpu.HBM)
SEM = pl.BlockSpec(memory_space=pltpu.SEMAPHORE)
SPLIT_COPY = pltpu.CompilerParams(has_side_effects=pltpu.SideEffectType.DATAFLOW_SIDE_EFFECTING)


def _in_hbm(a):
    return pltpu.with_memory_space_constraint(a, pltpu.HBM)


def _start_call(body, src, land_shape, name):
    land = lax.empty(land_shape, src.dtype)
    return pl.pallas_call(
        body, name=name,
        out_shape=(pltpu.SemaphoreType.DMA((3,)), pltpu.SemaphoreType.DMA((3,)), pltpu.HBM(src.shape, src.dtype),
                   pltpu.HBM(land_shape, src.dtype), jax.ShapeDtypeStruct((8, LANES), F32)),
        in_specs=(HBM, HBM), out_specs=(SEM, SEM, HBM, HBM, pl.BlockSpec(memory_space=pltpu.VMEM)),
        input_output_aliases={0: 2, 1: 3}, compiler_params=SPLIT_COPY,
    )(_in_hbm(src), _in_hbm(land))


def _wait_call(body, started, after, name):
    send_sems, recv_sems, src, land, _ = started
    return pl.pallas_call(
        body, name=name,
        out_shape=(pltpu.HBM(src.shape, src.dtype), pltpu.HBM(land.shape, land.dtype)),
        in_specs=(HBM, HBM, SEM, SEM, ANY), out_specs=(HBM, HBM),
        input_output_aliases={0: 0, 1: 1}, compiler_params=SPLIT_COPY,
    )(src, land, send_sems, recv_sems, after)[1]


def _gather_copies(p_ref, wall_ref, send_sems, recv_sems):
    x, y, c, chips = _place()
    me = 2 * x + y
    mine, _ = _halves(p_ref.shape[0], c, 16)
    out, back = [], []
    for k, chip in enumerate(chips):
        peer = dict(send_sem=send_sems.at[k], recv_sem=recv_sems.at[k], device_id=(chip[0], chip[1], c), device_id_type=MESH)
        out.append(pltpu.make_async_remote_copy(src_ref=p_ref.at[mine], dst_ref=wall_ref.at[me, mine], **peer))
        slab = wall_ref.at[2 * chip[0] + chip[1], mine]
        back.append(pltpu.make_async_remote_copy(src_ref=slab, dst_ref=slab, **peer))
    return out, back


def _gather_start(pack, name):
    def body(p_ref, wall_ref, send_sems, recv_sems, p_thru, wall_thru, token):
        del p_thru, wall_thru
        for cp in _gather_copies(p_ref, wall_ref, send_sems, recv_sems)[0]:
            cp.start()
        token[...] = jnp.zeros_like(token)

    return _start_call(body, pack, (N_CHIPS,) + pack.shape, name)


def _gather_wait(started, after, name):
    def body(p_ref, wall_ref, send_sems, recv_sems, after_ref, p_dead, wall_out):
        del after_ref, p_dead, wall_out
        out, back = _gather_copies(p_ref, wall_ref, send_sems, recv_sems)
        for cp_out, cp_back in zip(out, back):
            cp_out.wait_send()
            cp_back.wait_recv()

    return _wait_call(body, started, after, name)


def _pass_on(wall, name):
    def body(w_in_ref, out_ref, send_sems, recv_sems):
        del w_in_ref
        x, y, c, chips = _place()
        mine, theirs = _halves(wall.shape[1], c, 16)
        sends = []
        for k, chip in enumerate(chips):
            slab = out_ref.at[2 * chip[0] + chip[1]]
            peer = dict(send_sem=send_sems.at[k], recv_sem=recv_sems.at[k], device_id=(x, y, 1 - c), device_id_type=MESH)
            cp = pltpu.make_async_remote_copy(src_ref=slab.at[mine], dst_ref=slab.at[mine], **peer)
            cp.start()
            sends.append((cp, pltpu.make_async_remote_copy(src_ref=slab.at[theirs], dst_ref=slab.at[theirs], **peer)))
        for cp, back in sends:
            back.wait_recv()
            cp.wait_send()

    return pl.pallas_call(
        body, in_specs=[ANY], out_specs=ANY, out_shape=jax.ShapeDtypeStruct(wall.shape, wall.dtype),
        scratch_shapes=[pltpu.SemaphoreType.DMA((3,))] * 2, input_output_aliases={0: 0}, name=name,
    )(wall)


def _swap_halves(g, name):
    half = g.shape[1] // 2

    def body(g_ref, out_ref, send_sem, recv_sem):
        x, y, c, _ = _place()
        _, theirs = _halves(g.shape[1], c, 8)
        cp = pltpu.make_async_remote_copy(src_ref=g_ref.at[:, theirs], dst_ref=out_ref, send_sem=send_sem, recv_sem=recv_sem,
                                          device_id=(x, y, 1 - c), device_id_type=MESH)
        cp.start()
        cp.wait()

    return pl.pallas_call(
        body, in_specs=[ANY], out_specs=ANY,
        out_shape=jax.ShapeDtypeStruct((N_CHIPS, half, D_MODEL), F32),
        scratch_shapes=[pltpu.SemaphoreType.DMA, pltpu.SemaphoreType.DMA],
        name=name,
    )(g)


def _core_index():
    return lax.axis_index("c").astype(jnp.int32).reshape(1)


def _row_tile(half):
    tile = max(t for t in range(16, 1025, 16) if half % t == 0)
    return tile, half // tile


def _add_sibling(g, got, name):
    half = g.shape[1] // 2
    tile, n_tiles = _row_tile(half)

    def body(c_ref, g_ref, got_ref, o_ref):
        o_ref[...] = (g_ref[...] + got_ref[...]).astype(BF16)

    blk = pl.BlockSpec((None, tile, D_MODEL), lambda s, i, c_ref: (s, i, 0))
    return pl.pallas_call(
        body,
        grid_spec=pltpu.PrefetchScalarGridSpec(
            num_scalar_prefetch=1, grid=(N_CHIPS, n_tiles),
            in_specs=[pl.BlockSpec((None, tile, D_MODEL), lambda s, i, c_ref: (s, c_ref[0] * n_tiles + i, 0)), blk],
            out_specs=blk),
        out_shape=jax.ShapeDtypeStruct((N_CHIPS, half, D_MODEL), BF16),
        name=name, compiler_params=_params("arbitrary", "arbitrary"),
    )(_core_index(), g, got)


def _exchange_copies(p_ref, land_ref, send_sems, recv_sems):
    x, y, c, chips = _place()
    me = 2 * x + y
    out, back = [], []
    for k, chip in enumerate(chips):
        peer = dict(send_sem=send_sems.at[k], recv_sem=recv_sems.at[k], device_id=(chip[0], chip[1], c), device_id_type=MESH)
        out.append(pltpu.make_async_remote_copy(src_ref=p_ref.at[2 * chip[0] + chip[1]], dst_ref=land_ref.at[me], **peer))
        slab = land_ref.at[2 * chip[0] + chip[1]]
        back.append(pltpu.make_async_remote_copy(src_ref=slab, dst_ref=slab, **peer))
    return out, back


def _with_own(got, part):
    me = _my_chip()
    return lax.dynamic_update_slice(got, lax.dynamic_slice(part, (me, 0, 0), (1,) + part.shape[1:]), (me, 0, 0))


def _exchange_chips(part, name):
    def body(p_ref, out_ref, send_sems, recv_sems):
        out, back = _exchange_copies(p_ref, out_ref, send_sems, recv_sems)
        for cp in out:
            cp.start()
        for cp in back:
            cp.wait_recv()
        for cp in out:
            cp.wait_send()

    got = pl.pallas_call(
        body, in_specs=[ANY], out_specs=ANY,
        out_shape=jax.ShapeDtypeStruct(part.shape, part.dtype),
        scratch_shapes=[pltpu.SemaphoreType.DMA((3,)), pltpu.SemaphoreType.DMA((3,))],
        name=name,
    )(part)
    return _with_own(got, part)


def _exchange_start(part, name):
    def body(p_ref, land_ref, send_sems, recv_sems, p_thru, land_thru, token):
        del p_thru, land_thru
        for cp in _exchange_copies(p_ref, land_ref, send_sems, recv_sems)[0]:
            cp.start()
        token[...] = jnp.zeros_like(token)

    return _start_call(body, part, part.shape, name)


def _exchange_wait(started, after, name):
    def body(p_ref, land_ref, send_sems, recv_sems, after_ref, p_dead, land_out):
        del after_ref, p_dead, land_out
        out, back = _exchange_copies(p_ref, land_ref, send_sems, recv_sems)
        for cp_out, cp_back in zip(out, back):
            cp_out.wait_send()
            cp_back.wait_recv()

    return _with_own(_wait_call(body, started, after, name), started[2])


def _sum_chips(parts, name):
    half = parts.shape[1]
    tile, n_tiles = _row_tile(half)

    def body(c_ref, p0, p1, p2, p3, o_ref):
        f32 = lambda p: p[...].astype(F32)
        o_ref[...] = ((f32(p0) + f32(p1)) + f32(p2)) + f32(p3)

    def slab(s):
        return pl.BlockSpec((None, tile, D_MODEL), lambda i, c_ref, s=s: (s, i, 0))

    return pl.pallas_call(
        body,
        grid_spec=pltpu.PrefetchScalarGridSpec(
            num_scalar_prefetch=1, grid=(n_tiles,),
            in_specs=[slab(s) for s in range(N_CHIPS)],
            out_specs=pl.BlockSpec((None, tile, D_MODEL), lambda i, c_ref: (c_ref[0], i, 0))),
        out_shape=jax.ShapeDtypeStruct((2, half, D_MODEL), F32),
        name=name, compiler_params=_params("arbitrary"),
    )(_core_index(), parts, parts, parts, parts)


def _share_halves(halves, name):
    def body(h_ref, out_ref, send_sem, recv_sem):
        del h_ref
        x, y, c, _ = _place()
        cp = pltpu.make_async_remote_copy(src_ref=out_ref.at[c], dst_ref=out_ref.at[c], send_sem=send_sem, recv_sem=recv_sem,
                                          device_id=(x, y, 1 - c), device_id_type=MESH)
        cp.start()
        pltpu.make_async_remote_copy(src_ref=out_ref.at[1 - c], dst_ref=out_ref.at[1 - c], send_sem=send_sem, recv_sem=recv_sem,
                                     device_id=(x, y, 1 - c), device_id_type=MESH).wait_recv()
        cp.wait_send()

    return pl.pallas_call(
        body, in_specs=[ANY], out_specs=ANY,
        out_shape=jax.ShapeDtypeStruct(halves.shape, halves.dtype),
        scratch_shapes=[pltpu.SemaphoreType.DMA] * 2,
        input_output_aliases={0: 0},
        name=name,
    )(halves)


def _reduce_parts(g, tag):
    return _add_sibling(g, _swap_halves(g, "swap_" + tag), "add_" + tag)


def _reduce_finish(got, tag):
    halves = _share_halves(_sum_chips(got, "sum_" + tag), "share_" + tag)
    return halves.reshape(2 * halves.shape[1], D_MODEL)


SMALL_ROWS = 8


def _allreduce_small(v):
    def body(v_ref, out_ref, buf, send_sems, recv_sems):
        x, y, c, _ = _place()
        buf[4 * x + 2 * y + c] = v_ref[...]
        sends = []
        for k in range(1, N_DEV):
            px = 1 - x if k & 4 else x
            py = 1 - y if k & 2 else y
            pc = 1 - c if k & 1 else c
            cp = pltpu.make_async_remote_copy(src_ref=v_ref, dst_ref=buf.at[4 * x + 2 * y + c], send_sem=send_sems.at[k - 1],
                                              recv_sem=recv_sems.at[k - 1], device_id=(px, py, pc), device_id_type=MESH)
            cp.start()
            sends.append((cp, 4 * px + 2 * py + pc))
        for k, (cp, peer) in enumerate(sends):
            pltpu.make_async_remote_copy(src_ref=v_ref, dst_ref=buf.at[peer], send_sem=send_sems.at[k], recv_sem=recv_sems.at[k],
                                         device_id=(x, y, c), device_id_type=MESH).wait_recv()
        for cp, _ in sends:
            cp.wait_send()
        total = buf[0]
        for d in range(1, N_DEV):
            total = total + buf[d]
        out_ref[...] = total

    vmem = pl.BlockSpec(memory_space=pltpu.VMEM)
    return pl.pallas_call(
        body, in_specs=[vmem], out_specs=vmem,
        out_shape=jax.ShapeDtypeStruct(v.shape, v.dtype),
        scratch_shapes=[pltpu.VMEM((N_DEV,) + v.shape, v.dtype), pltpu.SemaphoreType.DMA((N_DEV - 1,)),
                        pltpu.SemaphoreType.DMA((N_DEV - 1,))],
        name="allreduce_small",
    )(v)


MATRICES = ("w_in", "w_out", "w_xq", "w_xk", "w_xv", "w_xo", "w_up", "w_down")
VECTORS = ("g_mix", "g_xattn", "g_mem", "g_mlp", "g_final", "b_forget")
WEIGHT_ORDER = ("g_mix", "w_in", "b_forget", "w_out", "g_xattn", "g_mem", "w_xq", "w_xk", "w_xv", "w_xo",
                "g_mlp", "w_up", "w_down", "g_final")
GROUPS = {"mlp": ("w_up", "w_down"), "mid": ("w_out", "w_xq", "w_xk", "w_xv", "w_xo"), "in": ("w_in",)}
LATE = GROUPS["mid"] + GROUPS["mlp"]
W_IN_SHARD = IN_WIDTH // N_CHIPS
SHARD_ROWS = {"w_in": W_IN_SHARD, "w_out": 256, "w_xq": 256, "w_xk": 256, "w_xv": 256, "w_xo": 256, "w_up": 1024, "w_down": 1024}
PACK_ROWS = {n: -(-r // 32) * 32 for n, r in SHARD_ROWS.items()}
ADAM_ROWS = 128


def _pack(parts, names):
    return jnp.concatenate([jnp.pad(parts[n], ((0, PACK_ROWS[n] - SHARD_ROWS[n]), (0, 0))) for n in names], axis=0)


def _unpack(a, names):
    out, pos = {}, 0
    for n in names:
        out[n] = a[..., pos:pos + SHARD_ROWS[n], :]
        pos += PACK_ROWS[n]
    return out


def _full_weights(wall, names):
    cols = lambda a: a.transpose(1, 0, 2).reshape(a.shape[1], -1)
    rows = lambda a: a.reshape(-1, a.shape[-1])
    out = {}
    for n, a in _unpack(wall, names).items():
        if n == "w_in":
            out[n] = cols(a.reshape(N_CHIPS, D_MODEL, W_IN_SHARD))
        else:
            out[n] = cols(a) if n == "w_up" else rows(a)
    return out


def _shard_of(g, name, s):
    if name == "w_in":
        return g[:, s * W_IN_SHARD:(s + 1) * W_IN_SHARD].reshape(W_IN_SHARD, D_MODEL)
    if name == "w_up":
        return g[:, s * D_MODEL:(s + 1) * D_MODEL]
    n = SHARD_ROWS[name]
    return g[s * n:(s + 1) * n]


def _pack_grads(gws, names):
    return jnp.stack([_pack({n: _shard_of(gws[n], n, s) for n in names}, names) for s in range(N_CHIPS)])


def kernel(x, mem, g_mix, w_in, b_forget, w_out, g_xattn, g_mem, w_xq, w_xk, w_xv, w_xo, g_mlp, w_up, w_down, g_final, loss_target, m_g_mix, m_w_in, m_b_forget, m_w_out, m_g_xattn, m_g_mem, m_w_xq, m_w_xk, m_w_xv, m_w_xo, m_g_mlp, m_w_up, m_w_down, m_g_final, v_g_mix, v_w_in, v_b_forget, v_w_out, v_g_xattn, v_g_mem, v_w_xq, v_w_xk, v_w_xv, v_w_xo, v_g_mlp, v_w_up, v_w_down, v_g_final):
    given = dict(locals())
    weights = {n: given[n] for n in WEIGHT_ORDER}
    vecs = {n: weights[n] for n in VECTORS}

    shard = {n: weights[n].astype(BF16) for n in MATRICES}
    shard["w_in"] = shard["w_in"].reshape(W_IN_SHARD, D_MODEL)
    late_pack = _pack(shard, LATE)
    late = _gather_start(late_pack, "gather_late_start")
    w_in_full = _full_weights(_gather(_pack(shard, GROUPS["in"]), "gather_in", late[4]), GROUPS["in"])["w_in"]

    def late_weights(after):
        wall = _pass_on(_gather_wait(late, after, "gather_late_wait"), "gather_late_pass")
        return _full_weights(_place_own(wall, late_pack), LATE)

    started = {}

    def on_grads(group, gws):
        part = _reduce_parts(_pack_grads(gws, GROUPS[group]), group)
        started[group] = _exchange_start(part, "exchange_%s_start" % group)
        return started[group][4]

    loss, grad_x, gw, gv = _local_step(x, mem, loss_target, vecs, w_in_full, late_weights, on_grads)

    part = _reduce_parts(_pack_grads(gw, GROUPS["in"]), "in")
    reduced = {"in": _reduce_finish(_exchange_chips(part, "exchange_in"), "in")}
    for group in ("mlp", "mid"):
        reduced[group] = _reduce_finish(_exchange_wait(started[group], grad_x, "exchange_%s_wait" % group), group)
    grads = {}
    for group, names in GROUPS.items():
        for n, a in _unpack(reduced[group], names).items():
            grads[n] = a.reshape(weights[n].shape)

    row = lambda a: jnp.pad(a.reshape(-1), (0, D_MODEL - a.size)).reshape(1, D_MODEL)
    small = jnp.concatenate([gv[n] for n in VECTORS[:5]] + [row(gv["b_forget"][:, 0]), row(loss[0, :1]),
                             jnp.zeros((1, D_MODEL), F32)], axis=0)
    small = _allreduce_small(small)
    for k, n in enumerate(VECTORS[:5]):
        grads[n] = small[k]
    grads["b_forget"] = small[5, :N_HEADS]
    loss_total = small[6, 0]

    delta, new_m, new_v = {}, {}, {}
    for n in MATRICES:
        delta[n], new_m[n], new_v[n] = _adamw(weights[n], grads[n], given["m_" + n], given["v_" + n], "adamw_" + n, ADAM_ROWS)
    stack = lambda prefix: jnp.concatenate([row(given[prefix + n]) for n in VECTORS] + [jnp.zeros((2, D_MODEL), F32)], axis=0)
    g_small = jnp.concatenate([small[:6], jnp.zeros((2, D_MODEL), F32)], axis=0)
    d, m1, v1 = _adamw(stack(""), g_small, stack("m_"), stack("v_"), "adamw_vectors", SMALL_ROWS)
    for k, n in enumerate(VECTORS):
        width = weights[n].shape[0]
        delta[n], new_m[n], new_v[n] = d[k, :width], m1[k, :width], v1[k, :width]

    return (loss_total, grad_x, *[grads[n] for n in WEIGHT_ORDER], *[delta[n] for n in WEIGHT_ORDER],
            *[new_m[n] for n in WEIGHT_ORDER], *[new_v[n] for n in WEIGHT_ORDER])
```

```python
import functools
import math

import jax
import jax.numpy as jnp
from jax import lax
from jax.experimental import pallas as pl
from jax.experimental.pallas import tpu as pltpu

F32 = jnp.float32
BF16 = jnp.bfloat16

D_MODEL = 1024
SEQ = 2048
N_MEM = 256
HEAD_DIM = 64
N_HEADS = 8
MIX_HALF = N_HEADS * HEAD_DIM
QKV_WIDTH = 6 * MIX_HALF
IN_WIDTH = QKV_WIDTH + N_HEADS
GATE_PAD = 128
BLOCK = 128
DILATIONS = (1, 4, 16)
X_HEADS = 4
X_HEAD_DIM = 256
D_FF = 4096
EPS = 1e-6
NEG = -1e30
ATT_SCALE = 1.0 / math.sqrt(HEAD_DIM)
X_SCALE = 1.0 / math.sqrt(X_HEAD_DIM)
LANES = 128
N_CHIPS = 4
N_DEV = 8

ADAM_LR = 0.001
ADAM_B1 = 0.9
ADAM_B2 = 0.999
ADAM_EPS = 1e-08
ADAM_WD = 0.01
ADAM_STEP = 10

VMEM_LIMIT = 48 * 1024 * 1024


def _params(*sem):
    return pltpu.CompilerParams(dimension_semantics=sem or None, vmem_limit_bytes=VMEM_LIMIT)


def _dot(a, b):
    return jnp.dot(a, b, preferred_element_type=F32)


def _dot_nt(a, b):
    return lax.dot_general(a, b, (((1,), (1,)), ((), ())), preferred_element_type=F32)


def _dot_tn(a, b):
    return lax.dot_general(a, b, (((0,), (0,)), ((), ())), preferred_element_type=F32)


def _dot_exact(x, e):
    hi = x.astype(BF16)
    r1 = x - hi.astype(F32)
    mid = r1.astype(BF16)
    lo = (r1 - mid.astype(F32)).astype(BF16)
    return _dot(hi, e) + _dot(mid, e) + _dot(lo, e)


def _head_mask(e):
    lane = lax.broadcasted_iota(jnp.int32, (1, LANES), 1)
    return (lane >= HEAD_DIM * e) & (lane < HEAD_DIM * (e + 1))


def _matmul(a, w, name, out_dtypes=(F32,), extras=(), epilogue=None, tm=1024, tn=512, w_t=False, after=None):
    m, k = a.shape
    n = w.shape[0] if w_t else w.shape[1]
    tm, tn = min(tm, m), min(tn, n)
    assert m % tm == 0 and n % tn == 0, (name, a.shape, w.shape)
    n_ex = len(extras)
    order = () if after is None else (after,)

    def body(a_ref, w_ref, *rest):
        rest = rest[len(order):]
        acc = (_dot_nt if w_t else _dot)(a_ref[...], w_ref[...])
        res = (acc,) if epilogue is None else epilogue(acc, *[r[...] for r in rest[:n_ex]])
        for o_ref, r in zip(rest[n_ex:], res):
            o_ref[...] = r.astype(o_ref.dtype)

    tile = pl.BlockSpec((tm, tn), lambda i, j: (i, j))
    w_spec = pl.BlockSpec((tn, k), lambda i, j: (j, 0)) if w_t else pl.BlockSpec((k, tn), lambda i, j: (0, j))
    return pl.pallas_call(
        body, grid=(m // tm, n // tn),
        in_specs=[pl.BlockSpec((tm, k), lambda i, j: (i, 0)), w_spec] + [pl.BlockSpec(memory_space=pl.ANY)] * len(order) + [tile] * n_ex,
        out_specs=[tile] * len(out_dtypes),
        out_shape=[jax.ShapeDtypeStruct((m, n), dt) for dt in out_dtypes],
        name=name, compiler_params=_params("parallel", "arbitrary"),
    )(a, w, *order, *extras)


def _matmul_res(a, w, res, name, w_t=False):
    return _matmul(a, w, name, extras=(res,), epilogue=lambda acc, r: (r + acc,), w_t=w_t)[0]


def _matmul_tn(x, y, name, tm=1024, tn=1024, tk=512):
    t, m = x.shape
    _, n = y.shape
    tm, tn, tk = min(tm, m), min(tn, n), min(tk, t)
    assert m % tm == 0 and n % tn == 0 and t % tk == 0, (name, x.shape, y.shape)

    def body(x_ref, y_ref, o_ref):
        @pl.when(pl.program_id(2) == 0)
        def _():
            o_ref[...] = jnp.zeros_like(o_ref)

        o_ref[...] += _dot_tn(x_ref[...], y_ref[...])

    return pl.pallas_call(
        body, grid=(m // tm, n // tn, t // tk),
        in_specs=[pl.BlockSpec((tk, tm), lambda i, j, k: (k, i)), pl.BlockSpec((tk, tn), lambda i, j, k: (k, j))],
        out_specs=pl.BlockSpec((tm, tn), lambda i, j, k: (i, j)),
        out_shape=jax.ShapeDtypeStruct((m, n), F32),
        name=name, compiler_params=_params("parallel", "parallel", "arbitrary"),
    )(x, y)


def _rmsnorm(x, g, name, tm=512):
    t, d = x.shape
    tm = min(tm, t)

    def body(x_ref, g_ref, h_ref):
        xv = x_ref[...]
        r = lax.rsqrt(jnp.mean(xv * xv, axis=-1, keepdims=True) + EPS)
        h_ref[...] = (xv * r * g_ref[...]).astype(BF16)

    return pl.pallas_call(
        body, grid=(t // tm,),
        in_specs=[pl.BlockSpec((tm, d), lambda i: (i, 0)), pl.BlockSpec((1, d), lambda i: (0, 0))],
        out_specs=pl.BlockSpec((tm, d), lambda i: (i, 0)),
        out_shape=jax.ShapeDtypeStruct((t, d), BF16),
        name=name, compiler_params=_params("arbitrary"),
    )(x, g.reshape(1, d))


def _rms_bwd_tile(xv, dh, g):
    d = xv.shape[-1]
    r = lax.rsqrt(jnp.mean(xv * xv, axis=-1, keepdims=True) + EPS)
    dyg = dh * g
    proj = jnp.sum(dyg * xv, axis=-1, keepdims=True)
    dx = r * dyg - xv * (r * r * r * (1.0 / d)) * proj
    return dx, dh * (xv * r)


def _rms_bwd(x, dh, g, dres, name, tm=512):
    t, d = x.shape
    tm = min(tm, t)
    has_res = dres is not None

    def body(x_ref, dh_ref, g_ref, *rest):
        if has_res:
            res_ref, dx_ref, dxb_ref, dg_ref = rest
        else:
            dx_ref, dxb_ref, dg_ref = rest
        dx, dg_rows = _rms_bwd_tile(x_ref[...], dh_ref[...], g_ref[...])
        if has_res:
            dx = res_ref[...] + dx
        dx_ref[...] = dx
        dxb_ref[...] = dx.astype(BF16)

        @pl.when(pl.program_id(0) == 0)
        def _():
            dg_ref[...] = jnp.zeros_like(dg_ref)

        dg_ref[...] += jnp.sum(dg_rows, axis=0, keepdims=True)

    row = pl.BlockSpec((tm, d), lambda i: (i, 0))
    vec = pl.BlockSpec((1, d), lambda i: (0, 0))
    return pl.pallas_call(
        body, grid=(t // tm,),
        in_specs=[row, row, vec] + ([row] if has_res else []),
        out_specs=[row, row, vec],
        out_shape=[jax.ShapeDtypeStruct((t, d), F32), jax.ShapeDtypeStruct((t, d), BF16), jax.ShapeDtypeStruct((1, d), F32)],
        name=name, compiler_params=_params("arbitrary"),
    )(x, dh, g.reshape(1, d), *((dres,) if has_res else ()))


def _loss_bwd(x, g, target, name, tm=512):
    t, d = x.shape

    def body(x_ref, g_ref, t_ref, loss_ref, dx_ref, dxb_ref, dg_ref):
        xv = x_ref[...]
        gv = g_ref[...]
        r = lax.rsqrt(jnp.mean(xv * xv, axis=-1, keepdims=True) + EPS)
        err = xv * r * gv - t_ref[...]
        dx, dg_rows = _rms_bwd_tile(xv, err * (1.0 / d), gv)
        dx_ref[...] = dx
        dxb_ref[...] = dx.astype(BF16)

        @pl.when(pl.program_id(0) == 0)
        def _():
            dg_ref[...] = jnp.zeros_like(dg_ref)
            loss_ref[...] = jnp.zeros_like(loss_ref)

        dg_ref[...] += jnp.sum(dg_rows, axis=0, keepdims=True)
        part = jnp.sum(jnp.sum(err * err, axis=0, keepdims=True), axis=1, keepdims=True) * (0.5 / d)
        loss_ref[...] += jnp.broadcast_to(part, loss_ref.shape)

    row = pl.BlockSpec((tm, d), lambda i: (i, 0))
    vec = pl.BlockSpec((1, d), lambda i: (0, 0))
    return pl.pallas_call(
        body, grid=(t // tm,),
        in_specs=[row, vec, row],
        out_specs=[pl.BlockSpec((1, LANES), lambda i: (0, 0)), row, row, vec],
        out_shape=[jax.ShapeDtypeStruct((1, LANES), F32), jax.ShapeDtypeStruct((t, d), F32),
                   jax.ShapeDtypeStruct((t, d), BF16), jax.ShapeDtypeStruct((1, d), F32)],
        name=name, compiler_params=_params("arbitrary"),
    )(x, g.reshape(1, d), target)


def _tri(upper):
    r = lax.broadcasted_iota(jnp.int32, (LANES, LANES), 0)
    c = lax.broadcasted_iota(jnp.int32, (LANES, LANES), 1)
    return jnp.where((r <= c) if upper else (r >= c), 1.0, 0.0).astype(BF16)


def _gate_fwd(gate, b_pad, n_batch, name):
    s = SEQ
    nblk = s // LANES

    def body(g_ref, b_ref, cbc_ref, crow_ref, sg_ref, ct_ref):
        gz = g_ref[...] + b_ref[...]
        logf = jnp.minimum(gz, 0.0) - jnp.log(1.0 + jnp.exp(-jnp.abs(gz)))
        logf_t = logf.T
        sg_ref[...] = (1.0 / (1.0 + jnp.exp(gz))).T[0:N_HEADS]
        upper = _tri(True)
        carry = jnp.zeros((LANES, 1), F32)
        for blk in range(nblk):
            seg = _dot_exact(logf_t[:, blk * LANES:(blk + 1) * LANES], upper) + carry
            carry = seg[:, LANES - 1:LANES]
            ct_ref[:, blk * LANES:(blk + 1) * LANES] = seg
        ct = ct_ref[...]
        crow_ref[...] = ct[0:N_HEADS]
        c_col = ct.T
        lane = lax.broadcasted_iota(jnp.int32, (1, MIX_HALF), 1)
        acc = jnp.zeros((s, MIX_HALF), F32)
        for h in range(N_HEADS):
            acc = jnp.where((lane >= HEAD_DIM * h) & (lane < HEAD_DIM * (h + 1)), c_col[:, h:h + 1], acc)
        cbc_ref[...] = acc

    return pl.pallas_call(
        body, grid=(n_batch,),
        in_specs=[pl.BlockSpec((s, GATE_PAD), lambda b: (b, 0)), pl.BlockSpec((1, GATE_PAD), lambda b: (0, 0))],
        out_specs=[pl.BlockSpec((s, MIX_HALF), lambda b: (b, 0)),
                   pl.BlockSpec((None, N_HEADS, s), lambda b: (b, 0, 0)),
                   pl.BlockSpec((None, N_HEADS, s), lambda b: (b, 0, 0))],
        out_shape=[jax.ShapeDtypeStruct((n_batch * s, MIX_HALF), F32),
                   jax.ShapeDtypeStruct((n_batch, N_HEADS, s), F32),
                   jax.ShapeDtypeStruct((n_batch, N_HEADS, s), F32)],
        scratch_shapes=[pltpu.VMEM((LANES, s), F32)],
        name=name, compiler_params=_params("arbitrary"),
    )(gate, b_pad)


def _gate_bwd(dc, sg, name):
    n_batch, _, s = dc.shape
    nblk = s // LANES

    def body(dc_ref, sg_ref, dz_ref, db_ref, dt_ref):
        lower = _tri(False)
        dcv = dc_ref[...]
        carry = jnp.zeros((N_HEADS, 1), F32)
        dt_ref[...] = jnp.zeros_like(dt_ref)
        for blk in reversed(range(nblk)):
            seg = _dot_exact(dcv[:, blk * LANES:(blk + 1) * LANES], lower) + carry
            carry = seg[:, 0:1]
            dt_ref[0:N_HEADS, blk * LANES:(blk + 1) * LANES] = seg * sg_ref[:, blk * LANES:(blk + 1) * LANES]
        dg_t = dt_ref[...]
        dz_ref[...] = dg_t.T.astype(BF16)

        @pl.when(pl.program_id(0) == 0)
        def _():
            db_ref[...] = jnp.zeros_like(db_ref)

        db_ref[...] += jnp.broadcast_to(jnp.sum(dg_t[0:N_HEADS], axis=1, keepdims=True), db_ref.shape)

    return pl.pallas_call(
        body, grid=(n_batch,),
        in_specs=[pl.BlockSpec((None, N_HEADS, s), lambda b: (b, 0, 0)), pl.BlockSpec((None, N_HEADS, s), lambda b: (b, 0, 0))],
        out_specs=[pl.BlockSpec((s, GATE_PAD), lambda b: (b, 0)), pl.BlockSpec((N_HEADS, LANES), lambda b: (0, 0))],
        out_shape=[jax.ShapeDtypeStruct((n_batch * s, GATE_PAD), BF16), jax.ShapeDtypeStruct((N_HEADS, LANES), F32)],
        scratch_shapes=[pltpu.VMEM((LANES, s), F32)],
        name=name, compiler_params=_params("arbitrary"),
    )(dc, sg)


FOX_BQ = 512
FOX_BK = 512
PAIR_WIDTH = 3 * LANES
N_PAIRS = N_HEADS // 2


def _pair_major(w):
    return w.reshape(w.shape[0], 3, N_PAIRS, LANES).transpose(0, 2, 1, 3).reshape(w.shape[0], 3 * MIX_HALF)


def _pair_major_inv(w):
    return w.reshape(w.shape[0], N_PAIRS, 3, LANES).transpose(0, 2, 1, 3).reshape(w.shape[0], 3 * MIX_HALF)


def _causal(i, j, bq, bk):
    qpos = i * bq + lax.broadcasted_iota(jnp.int32, (bq, 1), 0)
    kpos = j * bk + lax.broadcasted_iota(jnp.int32, (1, bk), 1)
    return kpos <= qpos


def _split_bf16(p):
    hi = p.astype(BF16)
    return hi, (p - hi.astype(F32)).astype(BF16)


def _fox_fwd(zf, c_bc, c_row, n_batch, name):
    s, bq, bk = SEQ, FOX_BQ, FOX_BK
    nq = s // bq
    t = n_batch * s

    def body(q_ref, k_ref, v_ref, cq_ref, cr_ref, o_ref, o32_ref, lse_ref):
        hp, i = pl.program_id(1), pl.program_id(2)
        q = q_ref[...] * ATT_SCALE
        qh = [jnp.where(_head_mask(e), q, jnp.zeros_like(q)) for e in range(2)]
        cq = [cq_ref[:, HEAD_DIM * e:HEAD_DIM * e + 1] for e in range(2)]

        def step(j, carry, masked):
            rows = pl.ds(pl.multiple_of(j * bk, bk), bk)
            kj, vj = k_ref[rows, :], v_ref[rows, :]
            out = []
            for e in range(2):
                m, l, acc = carry[3 * e:3 * e + 3]
                sc = _dot_nt(qh[e], kj) + (cq[e] - cr_ref[pl.ds(2 * hp + e, 1), rows])
                if masked:
                    sc = jnp.where(_causal(i, j, bq, bk), sc, NEG)
                m_new = jnp.maximum(m, jnp.max(sc, axis=1, keepdims=True))
                alpha = jnp.exp(m - m_new)
                p = jnp.exp(sc - m_new)
                p_hi, p_lo = _split_bf16(p)
                out += [m_new, alpha * l + jnp.sum(p, axis=1, keepdims=True), alpha * acc + (_dot(p_hi, vj) + _dot(p_lo, vj))]
            return tuple(out)

        init = (jnp.full((bq, 1), NEG, F32), jnp.zeros((bq, 1), F32), jnp.zeros((bq, LANES), F32)) * 2
        n_clear = (i * bq) // bk
        carry = lax.fori_loop(0, n_clear, functools.partial(step, masked=False), init)
        carry = lax.fori_loop(n_clear, (i * bq + bq + bk - 1) // bk, functools.partial(step, masked=True), carry)
        outs = [carry[3 * e + 2] / carry[3 * e + 1] for e in range(2)]
        lses = [carry[3 * e] + jnp.log(carry[3 * e + 1]) for e in range(2)]
        o = jnp.where(_head_mask(0), outs[0], outs[1])
        o_ref[...] = o.astype(BF16)
        o32_ref[...] = o
        lse_ref[...] = jnp.where(_head_mask(0), lses[0], lses[1])

    def col(c0):
        return lambda b, hp, i: (b, 3 * hp + c0)

    blk = pl.BlockSpec((bq, LANES), lambda b, hp, i: (b * nq + i, hp))
    return pl.pallas_call(
        body, grid=(n_batch, N_PAIRS, nq),
        in_specs=[pl.BlockSpec((bq, LANES), lambda b, hp, i: (b * nq + i, 3 * hp)),
                  pl.BlockSpec((s, LANES), col(1)), pl.BlockSpec((s, LANES), col(2)), blk,
                  pl.BlockSpec((None, N_HEADS, s), lambda b, hp, i: (b, 0, 0))],
        out_specs=[blk, blk, blk],
        out_shape=[jax.ShapeDtypeStruct((t, MIX_HALF), BF16), jax.ShapeDtypeStruct((t, MIX_HALF), F32),
                   jax.ShapeDtypeStruct((t, MIX_HALF), F32)],
        name=name, compiler_params=_params("parallel", "parallel", "arbitrary"),
    )(zf, zf, zf, c_bc, c_row)


def _fox_bwd(zf, o32, dy, lse, c_bc, c_row, dz, n_batch, name):
    s, bq, bk = SEQ, FOX_BQ, FOX_BK
    nq, nk = s // bq, s // bk

    def body(q_ref, k_ref, v_ref, o_ref, do_ref, lse_ref, cq_ref, cr_ref, dz_in, dz_ref, dc_ref, dq_acc):
        del dz_in
        hp, j = pl.program_id(1), pl.program_id(2)

        @pl.when(j == 0)
        def _():
            dq_acc[...] = jnp.zeros_like(dq_acc)

        kj, vj = k_ref[...], v_ref[...]
        cols = pl.ds(pl.multiple_of(j * bk, bk), bk)
        km = [jnp.where(_head_mask(e), kj, jnp.zeros_like(kj)) for e in range(2)]
        ck = [cr_ref[pl.ds(2 * hp + e, 1), cols] for e in range(2)]

        def step(i, carry, masked):
            rows = pl.ds(pl.multiple_of(i * bq, bq), bq)
            qi, doi = q_ref[rows, :] * ATT_SCALE, do_ref[rows, :]
            prod = doi.astype(F32) * o_ref[rows, :]
            out = []
            dq = jnp.zeros((bq, LANES), F32)
            for e in range(2):
                dk_a, dv_a, dc_a = carry[3 * e:3 * e + 3]
                mask = _head_mask(e)
                lane0 = HEAD_DIM * e
                dom = jnp.where(mask, doi, jnp.zeros_like(doi))
                delta = jnp.sum(jnp.where(mask, prod, 0.0), axis=1, keepdims=True)
                sc = _dot_nt(qi, km[e]) + (cq_ref[rows, lane0:lane0 + 1] - ck[e])
                if masked:
                    sc = jnp.where(_causal(i, j, bq, bk), sc, NEG)
                p = jnp.exp(sc - lse_ref[rows, lane0:lane0 + 1])
                ds = p * (_dot_nt(dom, vj) - delta)
                dsb = ds.astype(BF16)
                dq = dq + _dot(dsb, km[e])
                out += [dk_a + _dot_tn(dsb, qi), dv_a + _dot_tn(p.astype(BF16), dom), dc_a - jnp.sum(ds, axis=0, keepdims=True)]
            dq_acc[rows, :] += dq * ATT_SCALE
            return tuple(out)

        init = (jnp.zeros((bk, LANES), F32), jnp.zeros((bk, LANES), F32), jnp.zeros((1, bk), F32)) * 2
        first = (j * bk) // bq
        n_diag = (j * bk + bk + bq - 1) // bq
        carry = lax.fori_loop(first, n_diag, functools.partial(step, masked=True), init)
        carry = lax.fori_loop(n_diag, nq, functools.partial(step, masked=False), carry)
        for e in range(2):
            dc_ref[e:e + 1, :] = carry[3 * e + 2]
        dz_ref[cols, LANES:2 * LANES] = jnp.where(_head_mask(0), carry[0], carry[3]).astype(BF16)
        dz_ref[cols, 2 * LANES:3 * LANES] = (carry[1] + carry[4]).astype(BF16)

        @pl.when(j == nk - 1)
        def _():
            dz_ref[:, 0:LANES] = dq_acc[...].astype(BF16)

    def seq(idx):
        return pl.BlockSpec((s, LANES), lambda b, hp, j: (b, idx(hp)))

    def kblk(c0):
        return pl.BlockSpec((bk, LANES), lambda b, hp, j: (b * nk + j, 3 * hp + c0))

    return pl.pallas_call(
        body, grid=(n_batch, N_PAIRS, nk),
        in_specs=[seq(lambda hp: 3 * hp), kblk(1), kblk(2), seq(lambda hp: hp), seq(lambda hp: N_PAIRS + hp),
                  seq(lambda hp: hp), seq(lambda hp: hp),
                  pl.BlockSpec((None, N_HEADS, s), lambda b, hp, j: (b, 0, 0)), pl.BlockSpec(memory_space=pl.ANY)],
        out_specs=[pl.BlockSpec((s, PAIR_WIDTH), lambda b, hp, j: (b, N_PAIRS + hp)),
                   pl.BlockSpec((None, None, 2, bk), lambda b, hp, j: (b, hp, 0, j))],
        out_shape=[jax.ShapeDtypeStruct(dz.shape, dz.dtype), jax.ShapeDtypeStruct((n_batch, N_PAIRS, 2, s), F32)],
        scratch_shapes=[pltpu.VMEM((s, LANES), F32)],
        input_output_aliases={8: 0},
        name=name, compiler_params=_params("parallel", "parallel", "arbitrary"),
    )(zf, zf, zf, o32, dy, lse, c_bc, c_row, dz)


def _dil_bias(slope, dil):
    qi = lax.broadcasted_iota(jnp.int32, (BLOCK, 2 * BLOCK), 0)
    kj = lax.broadcasted_iota(jnp.int32, (BLOCK, 2 * BLOCK), 1)
    delta = qi + BLOCK - kj
    return jnp.where((delta >= 0) & (delta <= BLOCK), (-slope * dil) * delta.astype(F32), NEG)


def _alibi_slope(hp, e):
    slope = jnp.float32(0.0)
    for k in range(N_PAIRS):
        slope = jnp.where(hp == k, jnp.float32(2.0 ** -(2 * k + e + 1)), slope)
    return slope


def _fill_bias(bias_scr, hp):
    for di, dil in enumerate(DILATIONS):
        for e in range(2):
            bias_scr[2 * di + e] = _dil_bias(_alibi_slope(hp, e), dil)


def _pair_specs(rows):
    return [pl.BlockSpec((rows, LANES), lambda b, hp, c0=c0: (b, 3 * hp + c0)) for c0 in range(3)]


def _strided(start, size, dil):
    return pl.ds(start, size) if dil == 1 else pl.ds(start, size, stride=dil)


def _for_each_block(dil, unit):
    span = BLOCK * dil
    nb = SEQ // span
    if dil == 1:
        group = 3
        assert (nb - 1) % group == 0
        unit(0, True)

        def later(g, c):
            for u in range(group):
                unit((1 + g * group + u) * span, False)
            return c

        lax.fori_loop(0, (nb - 1) // group, later, 0)
        return
    group = 4
    per = dil // group

    def firsts(g, c):
        for u in range(group):
            unit(g * group + u, True)
        return c

    lax.fori_loop(0, per, firsts, 0)
    if nb > 1:
        def later(i, c):
            for u in range(group):
                unit((1 + i // per) * span + (i % per) * group + u, False)
            return c

        lax.fori_loop(0, (nb - 1) * per, later, 0)


def _mix_weights(l1, l2, l3):
    m = jnp.maximum(jnp.maximum(l1, l2), l3)
    e1, e2, e3 = jnp.exp(l1 - m), jnp.exp(l2 - m), jnp.exp(l3 - m)
    inv = 1.0 / (e1 + e2 + e3)
    return e1 * inv, e2 * inv, e3 * inv


def _dil_fwd(zd, n_batch, name):
    s = SEQ
    t = n_batch * s

    def body(q_ref, k_ref, v_ref, y_ref, l1_ref, l2_ref, l3_ref, o_scr, bias_scr):
        _fill_bias(bias_scr, pl.program_id(1))
        lse_refs = (l1_ref, l2_ref, l3_ref)
        for di, dil in enumerate(DILATIONS):

            def unit(start, first, di=di, dil=dil):
                qrows = _strided(start, BLOCK, dil)
                krows = qrows if first else _strided(start - BLOCK * dil, 2 * BLOCK, dil)
                q = (q_ref[qrows, :] * ATT_SCALE).astype(BF16)
                kc = k_ref[krows, :].astype(BF16)
                vc = v_ref[krows, :].astype(BF16)
                outs, lses = [], []
                for e in range(2):
                    bias = bias_scr[2 * di + e]
                    sc = _dot_nt(jnp.where(_head_mask(e), q, jnp.zeros_like(q)), kc) + (bias[:, BLOCK:] if first else bias)
                    m = jnp.max(sc, axis=1, keepdims=True)
                    pe = jnp.exp(sc - m)
                    l = jnp.sum(pe, axis=1, keepdims=True)
                    outs.append(_dot((pe * (1.0 / l)).astype(BF16), vc))
                    lses.append(m + jnp.log(l))
                o_scr.at[di][qrows, :] = jnp.where(_head_mask(0), outs[0], outs[1])
                lse_refs[di][qrows, :] = jnp.where(_head_mask(0), lses[0], lses[1])

            _for_each_block(dil, unit)
        w = _mix_weights(l1_ref[...], l2_ref[...], l3_ref[...])
        y_ref[...] = (w[0] * o_scr[0] + w[1] * o_scr[1] + w[2] * o_scr[2]).astype(BF16)

    blk = pl.BlockSpec((s, LANES), lambda b, hp: (b, hp))
    res = pl.pallas_call(
        body, grid=(n_batch, N_PAIRS),
        in_specs=_pair_specs(s),
        out_specs=[blk] * 4,
        out_shape=[jax.ShapeDtypeStruct((t, MIX_HALF), BF16)] + [jax.ShapeDtypeStruct((t, MIX_HALF), F32)] * 3,
        scratch_shapes=[pltpu.VMEM((3, s, LANES), F32), pltpu.VMEM((6, BLOCK, 2 * BLOCK), F32)],
        name=name, compiler_params=_params("parallel", "arbitrary"),
    )(zd, zd, zd)
    return res[0], res[1:]


def _dil_bwd(zd, dy, ya, lses, n_batch, name):
    s = SEQ
    t = n_batch * s

    def body(q_ref, k_ref, v_ref, dy_ref, ya_ref, l1_ref, l2_ref, l3_ref, dz_ref, w_scr, dy_scr, dot_scr, acc, bias_scr):
        _fill_bias(bias_scr, pl.program_id(1))
        for di, w in enumerate(_mix_weights(l1_ref[...], l2_ref[...], l3_ref[...])):
            w_scr[di] = w
        dya = dy_ref[...].astype(F32)
        prod = dya * ya_ref[...].astype(F32)
        per_head = [jnp.sum(jnp.where(_head_mask(e), prod, 0.0), axis=1, keepdims=True) for e in range(2)]
        dy_scr[...] = dya
        dot_scr[...] = jnp.where(_head_mask(0), per_head[0], per_head[1])
        acc[...] = jnp.zeros_like(acc)
        lse_refs = (l1_ref, l2_ref, l3_ref)
        for di, dil in enumerate(DILATIONS):

            def unit(start, first, di=di, dil=dil):
                qrows = _strided(start, BLOCK, dil)
                krows = qrows if first else _strided(start - BLOCK * dil, 2 * BLOCK, dil)
                q = (q_ref[qrows, :] * ATT_SCALE).astype(BF16)
                kc = k_ref[krows, :].astype(BF16)
                vc = v_ref[krows, :].astype(BF16)
                wq = w_scr.at[di][qrows, :]
                do = (wq * dy_scr[qrows, :]).astype(BF16)
                sub = wq * dot_scr[qrows, :]
                lse = lse_refs[di][qrows, :]
                dq = jnp.zeros((BLOCK, LANES), F32)
                dk = jnp.zeros((krows.size, LANES), F32)
                dv = jnp.zeros((krows.size, LANES), F32)
                for e in range(2):
                    mask = _head_mask(e)
                    lane0 = HEAD_DIM * e
                    qh = jnp.where(mask, q, jnp.zeros_like(q))
                    doh = jnp.where(mask, do, jnp.zeros_like(do))
                    bias = bias_scr[2 * di + e]
                    sc = _dot_nt(qh, kc) + (bias[:, BLOCK:] if first else bias)
                    p = jnp.exp(sc - lse[:, lane0:lane0 + 1])
                    dsb = (p * (_dot_nt(doh, vc) - sub[:, lane0:lane0 + 1])).astype(BF16)
                    dq = dq + _dot(dsb, jnp.where(mask, kc, jnp.zeros_like(kc)))
                    dk = dk + _dot_tn(dsb, qh)
                    dv = dv + _dot_tn(p.astype(BF16), doh)
                acc.at[0][qrows, :] += dq * ATT_SCALE
                acc.at[1][krows, :] += dk
                acc.at[2][krows, :] += dv

            _for_each_block(dil, unit)
        for k in range(3):
            dz_ref[:, k * LANES:(k + 1) * LANES] = acc[k].astype(BF16)

    blk = pl.BlockSpec((s, LANES), lambda b, hp: (b, hp))
    pair = pl.BlockSpec((s, PAIR_WIDTH), lambda b, hp: (b, hp))
    return pl.pallas_call(
        body, grid=(n_batch, N_PAIRS),
        in_specs=_pair_specs(s) + [blk] * 5,
        out_specs=pair,
        out_shape=jax.ShapeDtypeStruct((t, 2 * 3 * MIX_HALF), BF16),
        scratch_shapes=[pltpu.VMEM((3, s, LANES), F32), pltpu.VMEM((s, LANES), F32), pltpu.VMEM((s, LANES), F32),
                        pltpu.VMEM((3, s, LANES), F32), pltpu.VMEM((6, BLOCK, 2 * BLOCK), F32)],
        name=name, compiler_params=_params("parallel", "arbitrary"),
    )(zd, zd, zd, dy, ya, *lses)


X_BQ = 512


def _xattn_probs(q, k):
    sc = _dot_nt(q, k) * X_SCALE
    pe = jnp.exp(sc - jnp.max(sc, axis=1, keepdims=True))
    return pe / jnp.sum(pe, axis=1, keepdims=True)


def _xattn_fwd(qx, kx, vx, n_batch, name):
    nq = SEQ // X_BQ

    def body(q_ref, k_ref, v_ref, o_ref):
        p = _xattn_probs(q_ref[...], k_ref[...])
        o_ref[...] = _dot(p.astype(BF16), v_ref[...]).astype(BF16)

    qblk = pl.BlockSpec((X_BQ, X_HEAD_DIM), lambda b, h, i: (b * nq + i, h))
    kblk = pl.BlockSpec((N_MEM, X_HEAD_DIM), lambda b, h, i: (b, h))
    return pl.pallas_call(
        body, grid=(n_batch, X_HEADS, nq), in_specs=[qblk, kblk, kblk], out_specs=qblk,
        out_shape=jax.ShapeDtypeStruct(qx.shape, BF16),
        name=name, compiler_params=_params("parallel", "parallel", "arbitrary"),
    )(qx, kx, vx)


def _xattn_bwd(qx, kx, vx, dox, n_batch, name):
    nq = SEQ // X_BQ

    def body(q_ref, k_ref, v_ref, do_ref, dq_ref, dk_ref, dv_ref, dk_acc, dv_acc):
        i = pl.program_id(2)

        @pl.when(i == 0)
        def _():
            dk_acc[...] = jnp.zeros_like(dk_acc)
            dv_acc[...] = jnp.zeros_like(dv_acc)

        q, k, do = q_ref[...], k_ref[...], do_ref[...]
        p = _xattn_probs(q, k)
        dp = _dot_nt(do, v_ref[...])
        dsb = (p * (dp - jnp.sum(p * dp, axis=1, keepdims=True))).astype(BF16)
        dq_ref[...] = (_dot(dsb, k) * X_SCALE).astype(BF16)
        dk_acc[...] += _dot_tn(dsb, q) * X_SCALE
        dv_acc[...] += _dot_tn(p.astype(BF16), do)

        @pl.when(i == nq - 1)
        def _():
            dk_ref[...] = dk_acc[...].astype(BF16)
            dv_ref[...] = dv_acc[...].astype(BF16)

    qblk = pl.BlockSpec((X_BQ, X_HEAD_DIM), lambda b, h, i: (b * nq + i, h))
    kblk = pl.BlockSpec((N_MEM, X_HEAD_DIM), lambda b, h, i: (b, h))
    return pl.pallas_call(
        body, grid=(n_batch, X_HEADS, nq), in_specs=[qblk, kblk, kblk, qblk], out_specs=[qblk, kblk, kblk],
        out_shape=[jax.ShapeDtypeStruct(qx.shape, BF16), jax.ShapeDtypeStruct(kx.shape, BF16), jax.ShapeDtypeStruct(kx.shape, BF16)],
        scratch_shapes=[pltpu.VMEM((N_MEM, X_HEAD_DIM), F32)] * 2,
        name=name, compiler_params=_params("parallel", "parallel", "arbitrary"),
    )(qx, kx, vx, dox)


def _adamw(w, g, m, v, name, rows):
    r, c = w.shape
    assert r % rows == 0, (name, w.shape, rows)

    def body(w_ref, g_ref, m_ref, v_ref, d_ref, nm_ref, nv_ref):
        gv = g_ref[...]
        m1 = ADAM_B1 * m_ref[...] + (1.0 - ADAM_B1) * gv
        v1 = ADAM_B2 * v_ref[...] + (1.0 - ADAM_B2) * jnp.square(gv)
        m_hat = m1 / (1.0 - ADAM_B1 ** ADAM_STEP)
        v_hat = v1 / (1.0 - ADAM_B2 ** ADAM_STEP)
        d_ref[...] = -ADAM_LR * (m_hat / (jnp.sqrt(v_hat) + ADAM_EPS) + ADAM_WD * w_ref[...])
        nm_ref[...] = m1
        nv_ref[...] = v1

    blk = pl.BlockSpec((rows, c), lambda i: (i, 0))
    return pl.pallas_call(
        body, grid=(r // rows,), in_specs=[blk] * 4, out_specs=[blk] * 3,
        out_shape=[jax.ShapeDtypeStruct((r, c), F32)] * 3,
        name=name, compiler_params=_params("arbitrary"),
    )(w, g, m, v)


def _relu2(acc):
    a = jnp.maximum(acc, 0.0)
    return acc, a * a


def _relu2_bwd(acc, u):
    return (2.0 * jnp.maximum(u.astype(F32), 0.0) * acc,)


def _local_step(x, mem, target, vecs, w_in, late_weights, on_grads=None):
    n_batch = x.shape[0]
    t = n_batch * SEQ
    x0 = x.reshape(t, D_MODEL)
    mem2 = mem.reshape(n_batch * N_MEM, D_MODEL)
    tgt = target.reshape(t, D_MODEL)

    half = 3 * MIX_HALF
    w_dil, w_fox = _pair_major(w_in[:, :half]), _pair_major(w_in[:, half:QKV_WIDTH])
    w_gate = jnp.pad(w_in[:, QKV_WIDTH:], ((0, 0), (0, GATE_PAD - N_HEADS)))
    b_pad = jnp.pad(vecs["b_forget"], (0, GATE_PAD - N_HEADS)).reshape(1, GATE_PAD)

    h1 = _rmsnorm(x0, vecs["g_mix"], "norm_mix")
    mn = _rmsnorm(mem2, vecs["g_mem"], "norm_mem")
    zd = _matmul(h1, w_dil, "in_dil", tn=768)[0]
    zf = _matmul(h1, w_fox, "in_fox", out_dtypes=(BF16,), tn=768)[0]
    gate = _matmul(h1, w_gate, "in_gate")[0]
    c_bc, c_row, sg = _gate_fwd(gate, b_pad, n_batch, "gate_fwd")
    ya, lses = _dil_fwd(zd, n_batch, "dil_fwd")
    yf, of32, lse_f = _fox_fwd(zf, c_bc, c_row, n_batch, "fox_fwd")
    wts = late_weights(yf)
    w_out = wts["w_out"]
    x1 = _matmul_res(ya, w_out[:MIX_HALF], x0, "out_a")
    x1 = _matmul_res(yf, w_out[MIX_HALF:], x1, "out_f")
    h2 = _rmsnorm(x1, vecs["g_xattn"], "norm_xattn")
    qx = _matmul(h2, wts["w_xq"], "xq", out_dtypes=(BF16,))[0]
    kx = _matmul(mn, wts["w_xk"], "xk", out_dtypes=(BF16,))[0]
    vx = _matmul(mn, wts["w_xv"], "xv", out_dtypes=(BF16,))[0]
    ox = _xattn_fwd(qx, kx, vx, n_batch, "xattn_fwd")
    x2 = _matmul_res(ox, wts["w_xo"], x1, "xo")
    h3 = _rmsnorm(x2, vecs["g_mlp"], "norm_mlp")
    u, a2 = _matmul(h3, wts["w_up"], "mlp_up", out_dtypes=(BF16, BF16), epilogue=_relu2, tn=1024)
    x3 = _matmul_res(a2, wts["w_down"], x2, "mlp_down")
    loss, dx3, dx3b, dg_final = _loss_bwd(x3, vecs["g_final"], tgt, "loss")

    du = _matmul(dx3b, wts["w_down"], "mlp_down_bwd", out_dtypes=(BF16,), extras=(u,), epilogue=_relu2_bwd, tn=1024, w_t=True)[0]
    gw_down = _matmul_tn(a2, dx3b, "gw_down")
    gw_up = _matmul_tn(h3, du, "gw_up")
    token = on_grads("mlp", dict(w_up=gw_up, w_down=gw_down)) if on_grads else None
    dh3 = _matmul(du, wts["w_up"], "mlp_up_bwd", w_t=True, after=token)[0]
    dx2, dx2b, dg_mlp = _rms_bwd(x2, dh3, vecs["g_mlp"], dx3, "norm_mlp_bwd")

    gw_xo = _matmul_tn(ox, dx2b, "gw_xo")
    dox = _matmul(dx2b, wts["w_xo"], "xo_bwd", out_dtypes=(BF16,), w_t=True)[0]
    dqx, dkx, dvx = _xattn_bwd(qx, kx, vx, dox, n_batch, "xattn_bwd")
    gw_xq = _matmul_tn(h2, dqx, "gw_xq")
    gw_xk = _matmul_tn(mn, dkx, "gw_xk")
    gw_xv = _matmul_tn(mn, dvx, "gw_xv")
    dh2 = _matmul(dqx, wts["w_xq"], "xq_bwd", w_t=True)[0]
    dmn = _matmul(dkx, wts["w_xk"], "xk_bwd", w_t=True)[0]
    dmn = _matmul_res(dvx, wts["w_xv"], dmn, "xv_bwd", w_t=True)
    _, _, dg_mem = _rms_bwd(mem2, dmn, vecs["g_mem"], None, "norm_mem_bwd")
    dx1, dx1b, dg_xattn = _rms_bwd(x1, dh2, vecs["g_xattn"], dx2, "norm_xattn_bwd")

    gw_out = jnp.concatenate([_matmul_tn(ya, dx1b, "gw_out_a"), _matmul_tn(yf, dx1b, "gw_out_f")], axis=0)
    token = on_grads("mid", dict(w_out=gw_out, w_xq=gw_xq, w_xk=gw_xk, w_xv=gw_xv, w_xo=gw_xo)) if on_grads else None
    dy = _matmul(dx1b, w_out, "out_bwd", out_dtypes=(BF16,), w_t=True, after=token)[0]
    dz = _dil_bwd(zd, dy, ya, lses, n_batch, "dil_bwd")
    dz, dc = _fox_bwd(zf, of32, dy, lse_f, c_bc, c_row, dz, n_batch, "fox_bwd")
    dzg, db = _gate_bwd(dc.reshape(n_batch, N_HEADS, SEQ), sg, "gate_bwd")
    gw_pm = _matmul_tn(h1, dz, "gw_in_qkv")
    gw_in = jnp.concatenate([_pair_major_inv(gw_pm[:, :half]), _pair_major_inv(gw_pm[:, half:]),
                             _matmul_tn(h1, dzg, "gw_in_gate")[:, :N_HEADS]], axis=1)
    dh1 = _matmul(dz, jnp.concatenate([w_dil, w_fox], axis=1), "in_qkv_bwd", w_t=True)[0]
    dh1 = _matmul_res(dzg, w_gate, dh1, "in_gate_bwd", w_t=True)
    dx0, _, dg_mix = _rms_bwd(x0, dh1, vecs["g_mix"], dx1, "norm_mix_bwd")

    gw = dict(w_in=gw_in, w_out=gw_out, w_xq=gw_xq, w_xk=gw_xk, w_xv=gw_xv, w_xo=gw_xo, w_up=gw_up, w_down=gw_down)
    gv = dict(g_mix=dg_mix, g_xattn=dg_xattn, g_mem=dg_mem, g_mlp=dg_mlp, g_final=dg_final, b_forget=db)
    return loss, dx0.reshape(x.shape), gw, gv


MESH = pl.DeviceIdType.MESH
ANY = pl.BlockSpec(memory_space=pl.ANY)


def _place():
    x, y, c = lax.axis_index("x"), lax.axis_index("y"), lax.axis_index("c")
    other_chips = [(1 - x, y), (x, 1 - y), (1 - x, 1 - y)]
    return x, y, c, other_chips


def _my_chip():
    return 2 * lax.axis_index("x") + lax.axis_index("y")


def _halves(rows, c, align):
    half = rows // 2
    assert rows % (2 * align) == 0, rows
    return pl.ds(pl.multiple_of(c * half, align), half), pl.ds(pl.multiple_of((1 - c) * half, align), half)


def _place_own(wall, pack):
    return lax.dynamic_update_slice(wall, pack[None], (_my_chip(), 0, 0))


def _gather(pack, name, after):
    def body(p_ref, after_ref, out_ref, send_sems, recv_sems, pass_send, pass_recv):
        del after_ref
        x, y, c, chips = _place()
        me = 2 * x + y
        mine, theirs = _halves(pack.shape[0], c, 16)

        def from_chip(k, chip, rows):
            src = out_ref.at[2 * chip[0] + chip[1], rows]
            return pltpu.make_async_remote_copy(src_ref=src, dst_ref=src, send_sem=send_sems.at[k], recv_sem=recv_sems.at[k],
                                                device_id=(chip[0], chip[1], c), device_id_type=MESH)

        def passed(k, chip, rows):
            src = out_ref.at[2 * chip[0] + chip[1], rows]
            return pltpu.make_async_remote_copy(src_ref=src, dst_ref=src, send_sem=pass_send.at[k], recv_sem=pass_recv.at[k],
                                                device_id=(x, y, 1 - c), device_id_type=MESH)

        sends = []
        for k, chip in enumerate(chips):
            cp = pltpu.make_async_remote_copy(src_ref=p_ref.at[mine], dst_ref=out_ref.at[me, mine], send_sem=send_sems.at[k],
                                              recv_sem=recv_sems.at[k], device_id=(chip[0], chip[1], c), device_id_type=MESH)
            cp.start()
            sends.append(cp)
        for k, chip in enumerate(chips):
            from_chip(k, chip, mine).wait_recv()
            cp = passed(k, chip, mine)
            cp.start()
            sends.append(cp)
        for k, chip in enumerate(chips):
            passed(k, chip, theirs).wait_recv()
        for cp in sends:
            cp.wait_send()

    wall = pl.pallas_call(
        body, in_specs=[ANY, ANY], out_specs=ANY,
        out_shape=jax.ShapeDtypeStruct((N_CHIPS,) + pack.shape, pack.dtype),
        scratch_shapes=[pltpu.SemaphoreType.DMA((3,))] * 4,
        name=name,
    )(pack, after)
    return _place_own(wall, pack)


HBM = pl.BlockSpec(memory_space=pltpu.HBM)
SEM = pl.BlockSpec(memory_space=pltpu.SEMAPHORE)
SPLIT_COPY = pltpu.CompilerParams(has_side_effects=pltpu.SideEffectType.DATAFLOW_SIDE_EFFECTING)


def _in_hbm(a):
    return pltpu.with_memory_space_constraint(a, pltpu.HBM)


def _start_call(start, src, land_shape, after, name):
    land = lax.empty(land_shape, src.dtype)

    def body(src_ref, land_ref, after_ref, send_sems, recv_sems, src_thru, land_thru, token):
        del after_ref, src_thru, land_thru
        start(src_ref, land_ref, send_sems, recv_sems)
        token[...] = jnp.zeros_like(token)

    return pl.pallas_call(
        body, name=name,
        out_shape=(pltpu.SemaphoreType.DMA((3,)), pltpu.SemaphoreType.DMA((3,)), pltpu.HBM(src.shape, src.dtype),
                   pltpu.HBM(land_shape, src.dtype), jax.ShapeDtypeStruct((8, LANES), F32)),
        in_specs=(HBM, HBM, ANY), out_specs=(SEM, SEM, HBM, HBM, pl.BlockSpec(memory_space=pltpu.VMEM)),
        input_output_aliases={0: 2, 1: 3}, compiler_params=SPLIT_COPY,
    )(_in_hbm(src), _in_hbm(land), after)


def _wait_call(body, started, after, name):
    send_sems, recv_sems, src, land, _ = started
    return pl.pallas_call(
        body, name=name,
        out_shape=(pltpu.HBM(src.shape, src.dtype), pltpu.HBM(land.shape, land.dtype)),
        in_specs=(HBM, HBM, SEM, SEM, ANY), out_specs=(HBM, HBM),
        input_output_aliases={0: 0, 1: 1}, compiler_params=SPLIT_COPY,
    )(src, land, send_sems, recv_sems, after)[1]


def _gather_copies(p_ref, wall_ref, send_sems, recv_sems):
    x, y, c, chips = _place()
    me = 2 * x + y
    mine, _ = _halves(p_ref.shape[0], c, 16)
    out, back = [], []
    for k, chip in enumerate(chips):
        peer = dict(send_sem=send_sems.at[k], recv_sem=recv_sems.at[k], device_id=(chip[0], chip[1], c), device_id_type=MESH)
        out.append(pltpu.make_async_remote_copy(src_ref=p_ref.at[mine], dst_ref=wall_ref.at[me, mine], **peer))
        slab = wall_ref.at[2 * chip[0] + chip[1], mine]
        back.append(pltpu.make_async_remote_copy(src_ref=slab, dst_ref=slab, **peer))
    return out, back


def _gather_start(pack, after, name):
    def start(p_ref, wall_ref, send_sems, recv_sems):
        for cp in _gather_copies(p_ref, wall_ref, send_sems, recv_sems)[0]:
            cp.start()

    return _start_call(start, pack, (N_CHIPS,) + pack.shape, after, name)


def _gather_wait(started, after, name):
    def body(p_ref, wall_ref, send_sems, recv_sems, after_ref, p_dead, wall_out):
        del after_ref, p_dead, wall_out
        out, back = _gather_copies(p_ref, wall_ref, send_sems, recv_sems)
        for cp_out, cp_back in zip(out, back):
            cp_out.wait_send()
            cp_back.wait_recv()

    return _wait_call(body, started, after, name)


def _pass_on(wall, name):
    def body(w_in_ref, out_ref, send_sems, recv_sems):
        del w_in_ref
        x, y, c, chips = _place()
        mine, theirs = _halves(wall.shape[1], c, 16)
        sends = []
        for k, chip in enumerate(chips):
            slab = out_ref.at[2 * chip[0] + chip[1]]
            peer = dict(send_sem=send_sems.at[k], recv_sem=recv_sems.at[k], device_id=(x, y, 1 - c), device_id_type=MESH)
            cp = pltpu.make_async_remote_copy(src_ref=slab.at[mine], dst_ref=slab.at[mine], **peer)
            cp.start()
            sends.append((cp, pltpu.make_async_remote_copy(src_ref=slab.at[theirs], dst_ref=slab.at[theirs], **peer)))
        for cp, back in sends:
            back.wait_recv()
            cp.wait_send()

    return pl.pallas_call(
        body, in_specs=[ANY], out_specs=ANY, out_shape=jax.ShapeDtypeStruct(wall.shape, wall.dtype),
        scratch_shapes=[pltpu.SemaphoreType.DMA((3,))] * 2, input_output_aliases={0: 0}, name=name,
    )(wall)


def _swap_halves(g, name):
    half = g.shape[1] // 2

    def body(g_ref, out_ref, send_sem, recv_sem):
        x, y, c, _ = _place()
        _, theirs = _halves(g.shape[1], c, 8)
        cp = pltpu.make_async_remote_copy(src_ref=g_ref.at[:, theirs], dst_ref=out_ref, send_sem=send_sem, recv_sem=recv_sem,
                                          device_id=(x, y, 1 - c), device_id_type=MESH)
        cp.start()
        cp.wait()

    return pl.pallas_call(
        body, in_specs=[ANY], out_specs=ANY,
        out_shape=jax.ShapeDtypeStruct((N_CHIPS, half, D_MODEL), F32),
        scratch_shapes=[pltpu.SemaphoreType.DMA, pltpu.SemaphoreType.DMA],
        name=name,
    )(g)


def _core_index():
    return lax.axis_index("c").astype(jnp.int32).reshape(1)


def _row_tile(half):
    tile = max(t for t in range(16, 1025, 16) if half % t == 0)
    return tile, half // tile


def _add_sibling(g, got, name):
    half = g.shape[1] // 2
    tile, n_tiles = _row_tile(half)

    def body(c_ref, g_ref, got_ref, o_ref):
        o_ref[...] = (g_ref[...] + got_ref[...]).astype(BF16)

    blk = pl.BlockSpec((None, tile, D_MODEL), lambda s, i, c_ref: (s, i, 0))
    return pl.pallas_call(
        body,
        grid_spec=pltpu.PrefetchScalarGridSpec(
            num_scalar_prefetch=1, grid=(N_CHIPS, n_tiles),
            in_specs=[pl.BlockSpec((None, tile, D_MODEL), lambda s, i, c_ref: (s, c_ref[0] * n_tiles + i, 0)), blk],
            out_specs=blk),
        out_shape=jax.ShapeDtypeStruct((N_CHIPS, half, D_MODEL), BF16),
        name=name, compiler_params=_params("arbitrary", "arbitrary"),
    )(_core_index(), g, got)


def _exchange_copies(p_ref, land_ref, send_sems, recv_sems):
    x, y, c, chips = _place()
    me = 2 * x + y
    out, back = [], []
    for k, chip in enumerate(chips):
        peer = dict(send_sem=send_sems.at[k], recv_sem=recv_sems.at[k], device_id=(chip[0], chip[1], c), device_id_type=MESH)
        out.append(pltpu.make_async_remote_copy(src_ref=p_ref.at[2 * chip[0] + chip[1]], dst_ref=land_ref.at[me], **peer))
        slab = land_ref.at[2 * chip[0] + chip[1]]
        back.append(pltpu.make_async_remote_copy(src_ref=slab, dst_ref=slab, **peer))
    return out, back


def _with_own(got, part):
    me = _my_chip()
    return lax.dynamic_update_slice(got, lax.dynamic_slice(part, (me, 0, 0), (1,) + part.shape[1:]), (me, 0, 0))


def _exchange_chips(part, name):
    def body(p_ref, out_ref, send_sems, recv_sems):
        out, back = _exchange_copies(p_ref, out_ref, send_sems, recv_sems)
        for cp in out:
            cp.start()
        for cp in back:
            cp.wait_recv()
        for cp in out:
            cp.wait_send()

    got = pl.pallas_call(
        body, in_specs=[ANY], out_specs=ANY,
        out_shape=jax.ShapeDtypeStruct(part.shape, part.dtype),
        scratch_shapes=[pltpu.SemaphoreType.DMA((3,)), pltpu.SemaphoreType.DMA((3,))],
        name=name,
    )(part)
    return _with_own(got, part)


def _exchange_start(part, name):
    def start(p_ref, land_ref, send_sems, recv_sems):
        for cp in _exchange_copies(p_ref, land_ref, send_sems, recv_sems)[0]:
            cp.start()

    return _start_call(start, part, part.shape, part, name)


def _exchange_wait(started, after, name):
    def body(p_ref, land_ref, send_sems, recv_sems, after_ref, p_dead, land_out):
        del after_ref, p_dead, land_out
        out, back = _exchange_copies(p_ref, land_ref, send_sems, recv_sems)
        for cp_out, cp_back in zip(out, back):
            cp_out.wait_send()
            cp_back.wait_recv()

    return _with_own(_wait_call(body, started, after, name), started[2])


def _sum_chips(parts, name):
    half = parts.shape[1]
    tile, n_tiles = _row_tile(half)

    def body(c_ref, p0, p1, p2, p3, o_ref):
        f32 = lambda p: p[...].astype(F32)
        o_ref[...] = ((f32(p0) + f32(p1)) + f32(p2)) + f32(p3)

    def slab(s):
        return pl.BlockSpec((None, tile, D_MODEL), lambda i, c_ref, s=s: (s, i, 0))

    return pl.pallas_call(
        body,
        grid_spec=pltpu.PrefetchScalarGridSpec(
            num_scalar_prefetch=1, grid=(n_tiles,),
            in_specs=[slab(s) for s in range(N_CHIPS)],
            out_specs=pl.BlockSpec((None, tile, D_MODEL), lambda i, c_ref: (c_ref[0], i, 0))),
        out_shape=jax.ShapeDtypeStruct((2, half, D_MODEL), F32),
        name=name, compiler_params=_params("arbitrary"),
    )(_core_index(), parts, parts, parts, parts)


def _share_halves(halves, name):
    def body(h_ref, out_ref, send_sem, recv_sem):
        del h_ref
        x, y, c, _ = _place()
        cp = pltpu.make_async_remote_copy(src_ref=out_ref.at[c], dst_ref=out_ref.at[c], send_sem=send_sem, recv_sem=recv_sem,
                                          device_id=(x, y, 1 - c), device_id_type=MESH)
        cp.start()
        pltpu.make_async_remote_copy(src_ref=out_ref.at[1 - c], dst_ref=out_ref.at[1 - c], send_sem=send_sem, recv_sem=recv_sem,
                                     device_id=(x, y, 1 - c), device_id_type=MESH).wait_recv()
        cp.wait_send()

    return pl.pallas_call(
        body, in_specs=[ANY], out_specs=ANY,
        out_shape=jax.ShapeDtypeStruct(halves.shape, halves.dtype),
        scratch_shapes=[pltpu.SemaphoreType.DMA] * 2,
        input_output_aliases={0: 0},
        name=name,
    )(halves)


def _reduce_parts(g, tag):
    return _add_sibling(g, _swap_halves(g, "swap_" + tag), "add_" + tag)


def _reduce_finish(got, tag):
    halves = _share_halves(_sum_chips(got, "sum_" + tag), "share_" + tag)
    return halves.reshape(2 * halves.shape[1], D_MODEL)


SMALL_ROWS = 8


def _allreduce_small(v):
    def body(v_ref, out_ref, buf, send_sems, recv_sems):
        x, y, c, _ = _place()
        buf[4 * x + 2 * y + c] = v_ref[...]
        sends = []
        for k in range(1, N_DEV):
            px = 1 - x if k & 4 else x
            py = 1 - y if k & 2 else y
            pc = 1 - c if k & 1 else c
            cp = pltpu.make_async_remote_copy(src_ref=v_ref, dst_ref=buf.at[4 * x + 2 * y + c], send_sem=send_sems.at[k - 1],
                                              recv_sem=recv_sems.at[k - 1], device_id=(px, py, pc), device_id_type=MESH)
            cp.start()
            sends.append((cp, 4 * px + 2 * py + pc))
        for k, (cp, peer) in enumerate(sends):
            pltpu.make_async_remote_copy(src_ref=v_ref, dst_ref=buf.at[peer], send_sem=send_sems.at[k], recv_sem=recv_sems.at[k],
                                         device_id=(x, y, c), device_id_type=MESH).wait_recv()
        for cp, _ in sends:
            cp.wait_send()
        total = buf[0]
        for d in range(1, N_DEV):
            total = total + buf[d]
        out_ref[...] = total

    vmem = pl.BlockSpec(memory_space=pltpu.VMEM)
    return pl.pallas_call(
        body, in_specs=[vmem], out_specs=vmem,
        out_shape=jax.ShapeDtypeStruct(v.shape, v.dtype),
        scratch_shapes=[pltpu.VMEM((N_DEV,) + v.shape, v.dtype), pltpu.SemaphoreType.DMA((N_DEV - 1,)),
                        pltpu.SemaphoreType.DMA((N_DEV - 1,))],
        name="allreduce_small",
    )(v)


MATRICES = ("w_in", "w_out", "w_xq", "w_xk", "w_xv", "w_xo", "w_up", "w_down")
VECTORS = ("g_mix", "g_xattn", "g_mem", "g_mlp", "g_final", "b_forget")
WEIGHT_ORDER = ("g_mix", "w_in", "b_forget", "w_out", "g_xattn", "g_mem", "w_xq", "w_xk", "w_xv", "w_xo",
                "g_mlp", "w_up", "w_down", "g_final")
GROUPS = {"mlp": ("w_up", "w_down"), "mid": ("w_out", "w_xq", "w_xk", "w_xv", "w_xo"), "in": ("w_in",)}
LATE = GROUPS["mid"] + GROUPS["mlp"]
W_IN_SHARD = IN_WIDTH // N_CHIPS
SHARD_ROWS = {"w_in": W_IN_SHARD, "w_out": 256, "w_xq": 256, "w_xk": 256, "w_xv": 256, "w_xo": 256, "w_up": 1024, "w_down": 1024}
PACK_ROWS = {n: -(-r // 32) * 32 for n, r in SHARD_ROWS.items()}
ADAM_ROWS = 128


def _pack(parts, names):
    return jnp.concatenate([jnp.pad(parts[n], ((0, PACK_ROWS[n] - SHARD_ROWS[n]), (0, 0))) for n in names], axis=0)


def _unpack(a, names):
    out, pos = {}, 0
    for n in names:
        out[n] = a[..., pos:pos + SHARD_ROWS[n], :]
        pos += PACK_ROWS[n]
    return out


def _full_weights(wall, names):
    cols = lambda a: a.transpose(1, 0, 2).reshape(a.shape[1], -1)
    rows = lambda a: a.reshape(-1, a.shape[-1])
    out = {}
    for n, a in _unpack(wall, names).items():
        if n == "w_in":
            out[n] = cols(a.reshape(N_CHIPS, D_MODEL, W_IN_SHARD))
        else:
            out[n] = cols(a) if n == "w_up" else rows(a)
    return out


def _shard_of(g, name, s):
    if name == "w_in":
        return g[:, s * W_IN_SHARD:(s + 1) * W_IN_SHARD].reshape(W_IN_SHARD, D_MODEL)
    if name == "w_up":
        return g[:, s * D_MODEL:(s + 1) * D_MODEL]
    n = SHARD_ROWS[name]
    return g[s * n:(s + 1) * n]


def _pack_grads(gws, names):
    return jnp.stack([_pack({n: _shard_of(gws[n], n, s) for n in names}, names) for s in range(N_CHIPS)])


def kernel(x, mem, g_mix, w_in, b_forget, w_out, g_xattn, g_mem, w_xq, w_xk, w_xv, w_xo, g_mlp, w_up, w_down, g_final, loss_target, m_g_mix, m_w_in, m_b_forget, m_w_out, m_g_xattn, m_g_mem, m_w_xq, m_w_xk, m_w_xv, m_w_xo, m_g_mlp, m_w_up, m_w_down, m_g_final, v_g_mix, v_w_in, v_b_forget, v_w_out, v_g_xattn, v_g_mem, v_w_xq, v_w_xk, v_w_xv, v_w_xo, v_g_mlp, v_w_up, v_w_down, v_g_final):
    given = dict(locals())
    weights = {n: given[n] for n in WEIGHT_ORDER}
    vecs = {n: weights[n] for n in VECTORS}

    shard = {n: weights[n].astype(BF16) for n in MATRICES}
    shard["w_in"] = shard["w_in"].reshape(W_IN_SHARD, D_MODEL)
    in_pack, late_pack = _pack(shard, GROUPS["in"]), _pack(shard, LATE)
    in_wall = _gather(in_pack, "gather_in", in_pack)
    late = _gather_start(late_pack, in_wall, "gather_late_start")
    w_in_full = _full_weights(in_wall, GROUPS["in"])["w_in"]

    def late_weights(after):
        wall = _pass_on(_gather_wait(late, after, "gather_late_wait"), "gather_late_pass")
        return _full_weights(_place_own(wall, late_pack), LATE)

    started = {}

    def on_grads(group, gws):
        part = _reduce_parts(_pack_grads(gws, GROUPS[group]), group)
        started[group] = _exchange_start(part, "exchange_%s_start" % group)
        return started[group][4]

    loss, grad_x, gw, gv = _local_step(x, mem, loss_target, vecs, w_in_full, late_weights, on_grads)

    part = _reduce_parts(_pack_grads(gw, GROUPS["in"]), "in")
    reduced = {"in": _reduce_finish(_exchange_chips(part, "exchange_in"), "in")}
    for group in ("mlp", "mid"):
        reduced[group] = _reduce_finish(_exchange_wait(started[group], grad_x, "exchange_%s_wait" % group), group)
    grads = {}
    for group, names in GROUPS.items():
        for n, a in _unpack(reduced[group], names).items():
            grads[n] = a.reshape(weights[n].shape)

    row = lambda a: jnp.pad(a.reshape(-1), (0, D_MODEL - a.size)).reshape(1, D_MODEL)
    small = jnp.concatenate([gv[n] for n in VECTORS[:5]] + [row(gv["b_forget"][:, 0]), row(loss[0, :1]),
                             jnp.zeros((1, D_MODEL), F32)], axis=0)
    small = _allreduce_small(small)
    for k, n in enumerate(VECTORS[:5]):
        grads[n] = small[k]
    grads["b_forget"] = small[5, :N_HEADS]
    loss_total = small[6, 0]

    delta, new_m, new_v = {}, {}, {}
    for n in MATRICES:
        delta[n], new_m[n], new_v[n] = _adamw(weights[n], grads[n], given["m_" + n], given["v_" + n], "adamw_" + n, ADAM_ROWS)
    stack = lambda prefix: jnp.concatenate([row(given[prefix + n]) for n in VECTORS] + [jnp.zeros((2, D_MODEL), F32)], axis=0)
    g_small = jnp.concatenate([small[:6], jnp.zeros((2, D_MODEL), F32)], axis=0)
    d, m1, v1 = _adamw(stack(""), g_small, stack("m_"), stack("v_"), "adamw_vectors", SMALL_ROWS)
    for k, n in enumerate(VECTORS):
        width = weights[n].shape[0]
        delta[n], new_m[n], new_v[n] = d[k, :width], m1[k, :width], v1[k, :width]

    return (loss_total, grad_x, *[grads[n] for n in WEIGHT_ORDER], *[delta[n] for n in WEIGHT_ORDER],
            *[new_m[n] for n in WEIGHT_ORDER], *[new_v[n] for n in WEIGHT_ORDER])
```

```python
import functools
import math

import jax
import jax.numpy as jnp
from jax import lax
from jax.experimental import pallas as pl
from jax.experimental.pallas import tpu as pltpu

F32 = jnp.float32
BF16 = jnp.bfloat16

D_MODEL = 1024
SEQ = 2048
N_MEM = 256
HEAD_DIM = 64
N_HEADS = 8
MIX_HALF = N_HEADS * HEAD_DIM
QKV_WIDTH = 6 * MIX_HALF
IN_WIDTH = QKV_WIDTH + N_HEADS
GATE_PAD = 128
BLOCK = 128
DILATIONS = (1, 4, 16)
X_HEADS = 4
X_HEAD_DIM = 256
D_FF = 4096
EPS = 1e-6
NEG = -1e30
ATT_SCALE = 1.0 / math.sqrt(HEAD_DIM)
X_SCALE = 1.0 / math.sqrt(X_HEAD_DIM)
LANES = 128
N_CHIPS = 4
N_DEV = 8

ADAM_LR = 0.001
ADAM_B1 = 0.9
ADAM_B2 = 0.999
ADAM_EPS = 1e-08
ADAM_WD = 0.01
ADAM_STEP = 10

VMEM_LIMIT = 48 * 1024 * 1024


def _params(*sem):
    return pltpu.CompilerParams(dimension_semantics=sem or None, vmem_limit_bytes=VMEM_LIMIT)


def _dot(a, b):
    return jnp.dot(a, b, preferred_element_type=F32)


def _dot_nt(a, b):
    return lax.dot_general(a, b, (((1,), (1,)), ((), ())), preferred_element_type=F32)


def _dot_tn(a, b):
    return lax.dot_general(a, b, (((0,), (0,)), ((), ())), preferred_element_type=F32)


def _dot_exact(x, e):
    hi = x.astype(BF16)
    r1 = x - hi.astype(F32)
    mid = r1.astype(BF16)
    lo = (r1 - mid.astype(F32)).astype(BF16)
    return _dot(hi, e) + _dot(mid, e) + _dot(lo, e)


def _head_mask(e):
    lane = lax.broadcasted_iota(jnp.int32, (1, LANES), 1)
    return (lane >= HEAD_DIM * e) & (lane < HEAD_DIM * (e + 1))


def _matmul(a, w, name, out_dtypes=(F32,), extras=(), epilogue=None, tm=1024, tn=512, w_t=False, after=None):
    m, k = a.shape
    n = w.shape[0] if w_t else w.shape[1]
    tm, tn = min(tm, m), min(tn, n)
    assert m % tm == 0 and n % tn == 0, (name, a.shape, w.shape)
    n_ex = len(extras)
    order = () if after is None else (after,)

    def body(a_ref, w_ref, *rest):
        rest = rest[len(order):]
        acc = (_dot_nt if w_t else _dot)(a_ref[...], w_ref[...])
        res = (acc,) if epilogue is None else epilogue(acc, *[r[...] for r in rest[:n_ex]])
        for o_ref, r in zip(rest[n_ex:], res):
            o_ref[...] = r.astype(o_ref.dtype)

    tile = pl.BlockSpec((tm, tn), lambda i, j: (i, j))
    w_spec = pl.BlockSpec((tn, k), lambda i, j: (j, 0)) if w_t else pl.BlockSpec((k, tn), lambda i, j: (0, j))
    return pl.pallas_call(
        body, grid=(m // tm, n // tn),
        in_specs=[pl.BlockSpec((tm, k), lambda i, j: (i, 0)), w_spec] + [pl.BlockSpec(memory_space=pl.ANY)] * len(order) + [tile] * n_ex,
        out_specs=[tile] * len(out_dtypes),
        out_shape=[jax.ShapeDtypeStruct((m, n), dt) for dt in out_dtypes],
        name=name, compiler_params=_params("parallel", "arbitrary"),
    )(a, w, *order, *extras)


def _matmul_res(a, w, res, name, w_t=False):
    return _matmul(a, w, name, extras=(res,), epilogue=lambda acc, r: (r + acc,), w_t=w_t)[0]


def _matmul_tn(x, y, name, tm=1024, tn=1024, tk=512):
    t, m = x.shape
    _, n = y.shape
    tm, tn, tk = min(tm, m), min(tn, n), min(tk, t)
    assert m % tm == 0 and n % tn == 0 and t % tk == 0, (name, x.shape, y.shape)

    def body(x_ref, y_ref, o_ref):
        @pl.when(pl.program_id(2) == 0)
        def _():
            o_ref[...] = jnp.zeros_like(o_ref)

        o_ref[...] += _dot_tn(x_ref[...], y_ref[...])

    return pl.pallas_call(
        body, grid=(m // tm, n // tn, t // tk),
        in_specs=[pl.BlockSpec((tk, tm), lambda i, j, k: (k, i)), pl.BlockSpec((tk, tn), lambda i, j, k: (k, j))],
        out_specs=pl.BlockSpec((tm, tn), lambda i, j, k: (i, j)),
        out_shape=jax.ShapeDtypeStruct((m, n), F32),
        name=name, compiler_params=_params("parallel", "parallel", "arbitrary"),
    )(x, y)


def _rmsnorm(x, g, name, tm=512):
    t, d = x.shape
    tm = min(tm, t)

    def body(x_ref, g_ref, h_ref):
        xv = x_ref[...]
        r = lax.rsqrt(jnp.mean(xv * xv, axis=-1, keepdims=True) + EPS)
        h_ref[...] = (xv * r * g_ref[...]).astype(BF16)

    return pl.pallas_call(
        body, grid=(t // tm,),
        in_specs=[pl.BlockSpec((tm, d), lambda i: (i, 0)), pl.BlockSpec((1, d), lambda i: (0, 0))],
        out_specs=pl.BlockSpec((tm, d), lambda i: (i, 0)),
        out_shape=jax.ShapeDtypeStruct((t, d), BF16),
        name=name, compiler_params=_params("arbitrary"),
    )(x, g.reshape(1, d))


def _rms_bwd_tile(xv, dh, g):
    d = xv.shape[-1]
    r = lax.rsqrt(jnp.mean(xv * xv, axis=-1, keepdims=True) + EPS)
    dyg = dh * g
    proj = jnp.sum(dyg * xv, axis=-1, keepdims=True)
    dx = r * dyg - xv * (r * r * r * (1.0 / d)) * proj
    return dx, dh * (xv * r)


def _rms_bwd(x, dh, g, dres, name, tm=512):
    t, d = x.shape
    tm = min(tm, t)
    has_res = dres is not None

    def body(x_ref, dh_ref, g_ref, *rest):
        if has_res:
            res_ref, dx_ref, dxb_ref, dg_ref = rest
        else:
            dx_ref, dxb_ref, dg_ref = rest
        dx, dg_rows = _rms_bwd_tile(x_ref[...], dh_ref[...], g_ref[...])
        if has_res:
            dx = res_ref[...] + dx
        dx_ref[...] = dx
        dxb_ref[...] = dx.astype(BF16)

        @pl.when(pl.program_id(0) == 0)
        def _():
            dg_ref[...] = jnp.zeros_like(dg_ref)

        dg_ref[...] += jnp.sum(dg_rows, axis=0, keepdims=True)

    row = pl.BlockSpec((tm, d), lambda i: (i, 0))
    vec = pl.BlockSpec((1, d), lambda i: (0, 0))
    return pl.pallas_call(
        body, grid=(t // tm,),
        in_specs=[row, row, vec] + ([row] if has_res else []),
        out_specs=[row, row, vec],
        out_shape=[jax.ShapeDtypeStruct((t, d), F32), jax.ShapeDtypeStruct((t, d), BF16), jax.ShapeDtypeStruct((1, d), F32)],
        name=name, compiler_params=_params("arbitrary"),
    )(x, dh, g.reshape(1, d), *((dres,) if has_res else ()))


def _loss_bwd(x, g, target, name, tm=512):
    t, d = x.shape

    def body(x_ref, g_ref, t_ref, loss_ref, dx_ref, dxb_ref, dg_ref):
        xv = x_ref[...]
        gv = g_ref[...]
        r = lax.rsqrt(jnp.mean(xv * xv, axis=-1, keepdims=True) + EPS)
        err = xv * r * gv - t_ref[...]
        dx, dg_rows = _rms_bwd_tile(xv, err * (1.0 / d), gv)
        dx_ref[...] = dx
        dxb_ref[...] = dx.astype(BF16)

        @pl.when(pl.program_id(0) == 0)
        def _():
            dg_ref[...] = jnp.zeros_like(dg_ref)
            loss_ref[...] = jnp.zeros_like(loss_ref)

        dg_ref[...] += jnp.sum(dg_rows, axis=0, keepdims=True)
        part = jnp.sum(jnp.sum(err * err, axis=0, keepdims=True), axis=1, keepdims=True) * (0.5 / d)
        loss_ref[...] += jnp.broadcast_to(part, loss_ref.shape)

    row = pl.BlockSpec((tm, d), lambda i: (i, 0))
    vec = pl.BlockSpec((1, d), lambda i: (0, 0))
    return pl.pallas_call(
        body, grid=(t // tm,),
        in_specs=[row, vec, row],
        out_specs=[pl.BlockSpec((1, LANES), lambda i: (0, 0)), row, row, vec],
        out_shape=[jax.ShapeDtypeStruct((1, LANES), F32), jax.ShapeDtypeStruct((t, d), F32),
                   jax.ShapeDtypeStruct((t, d), BF16), jax.ShapeDtypeStruct((1, d), F32)],
        name=name, compiler_params=_params("arbitrary"),
    )(x, g.reshape(1, d), target)


def _tri(upper):
    r = lax.broadcasted_iota(jnp.int32, (LANES, LANES), 0)
    c = lax.broadcasted_iota(jnp.int32, (LANES, LANES), 1)
    return jnp.where((r <= c) if upper else (r >= c), 1.0, 0.0).astype(BF16)


def _gate_fwd(gate, b_pad, n_batch, name):
    s = SEQ
    nblk = s // LANES

    def body(g_ref, b_ref, cbc_ref, crow_ref, sg_ref, ct_ref):
        gz = g_ref[...] + b_ref[...]
        logf = jnp.minimum(gz, 0.0) - jnp.log(1.0 + jnp.exp(-jnp.abs(gz)))
        logf_t = logf.T
        sg_ref[...] = (1.0 / (1.0 + jnp.exp(gz))).T[0:N_HEADS]
        upper = _tri(True)
        carry = jnp.zeros((LANES, 1), F32)
        for blk in range(nblk):
            seg = _dot_exact(logf_t[:, blk * LANES:(blk + 1) * LANES], upper) + carry
            carry = seg[:, LANES - 1:LANES]
            ct_ref[:, blk * LANES:(blk + 1) * LANES] = seg
        ct = ct_ref[...]
        crow_ref[...] = ct[0:N_HEADS]
        c_col = ct.T
        lane = lax.broadcasted_iota(jnp.int32, (1, MIX_HALF), 1)
        acc = jnp.zeros((s, MIX_HALF), F32)
        for h in range(N_HEADS):
            acc = jnp.where((lane >= HEAD_DIM * h) & (lane < HEAD_DIM * (h + 1)), c_col[:, h:h + 1], acc)
        cbc_ref[...] = acc

    return pl.pallas_call(
        body, grid=(n_batch,),
        in_specs=[pl.BlockSpec((s, GATE_PAD), lambda b: (b, 0)), pl.BlockSpec((1, GATE_PAD), lambda b: (0, 0))],
        out_specs=[pl.BlockSpec((s, MIX_HALF), lambda b: (b, 0)),
                   pl.BlockSpec((None, N_HEADS, s), lambda b: (b, 0, 0)),
                   pl.BlockSpec((None, N_HEADS, s), lambda b: (b, 0, 0))],
        out_shape=[jax.ShapeDtypeStruct((n_batch * s, MIX_HALF), F32),
                   jax.ShapeDtypeStruct((n_batch, N_HEADS, s), F32),
                   jax.ShapeDtypeStruct((n_batch, N_HEADS, s), F32)],
        scratch_shapes=[pltpu.VMEM((LANES, s), F32)],
        name=name, compiler_params=_params("arbitrary"),
    )(gate, b_pad)


def _gate_bwd(dc, sg, name):
    n_batch, _, s = dc.shape
    nblk = s // LANES

    def body(dc_ref, sg_ref, dz_ref, db_ref, dt_ref):
        lower = _tri(False)
        dcv = dc_ref[...]
        carry = jnp.zeros((N_HEADS, 1), F32)
        dt_ref[...] = jnp.zeros_like(dt_ref)
        for blk in reversed(range(nblk)):
            seg = _dot_exact(dcv[:, blk * LANES:(blk + 1) * LANES], lower) + carry
            carry = seg[:, 0:1]
            dt_ref[0:N_HEADS, blk * LANES:(blk + 1) * LANES] = seg * sg_ref[:, blk * LANES:(blk + 1) * LANES]
        dg_t = dt_ref[...]
        dz_ref[...] = dg_t.T.astype(BF16)

        @pl.when(pl.program_id(0) == 0)
        def _():
            db_ref[...] = jnp.zeros_like(db_ref)

        db_ref[...] += jnp.broadcast_to(jnp.sum(dg_t[0:N_HEADS], axis=1, keepdims=True), db_ref.shape)

    return pl.pallas_call(
        body, grid=(n_batch,),
        in_specs=[pl.BlockSpec((None, N_HEADS, s), lambda b: (b, 0, 0)), pl.BlockSpec((None, N_HEADS, s), lambda b: (b, 0, 0))],
        out_specs=[pl.BlockSpec((s, GATE_PAD), lambda b: (b, 0)), pl.BlockSpec((N_HEADS, LANES), lambda b: (0, 0))],
        out_shape=[jax.ShapeDtypeStruct((n_batch * s, GATE_PAD), BF16), jax.ShapeDtypeStruct((N_HEADS, LANES), F32)],
        scratch_shapes=[pltpu.VMEM((LANES, s), F32)],
        name=name, compiler_params=_params("arbitrary"),
    )(dc, sg)


FOX_BQ = 512
FOX_BK = 512
FOX_STRIP = 512
PAIR_WIDTH = 3 * LANES
N_PAIRS = N_HEADS // 2


def _pair_major(w):
    return w.reshape(w.shape[0], 3, N_PAIRS, LANES).transpose(0, 2, 1, 3).reshape(w.shape[0], 3 * MIX_HALF)


def _pair_major_inv(w):
    return w.reshape(w.shape[0], N_PAIRS, 3, LANES).transpose(0, 2, 1, 3).reshape(w.shape[0], 3 * MIX_HALF)


def _causal(i, j, bq, bk):
    qpos = i * bq + lax.broadcasted_iota(jnp.int32, (bq, 1), 0)
    kpos = j * bk + lax.broadcasted_iota(jnp.int32, (1, bk), 1)
    return kpos <= qpos


def _split_bf16(p):
    hi = p.astype(BF16)
    return hi, (p - hi.astype(F32)).astype(BF16)


def _fox_fwd(zf, c_bc, c_row, n_batch, name):
    s, bq, bk = SEQ, FOX_BQ, FOX_BK
    nq = s // bq
    t = n_batch * s

    n_strip = bq // FOX_STRIP

    def body(q_ref, k_ref, v_ref, cq_ref, cr_ref, o_ref, o32_ref, lse_ref):
        hp, i = pl.program_id(1), pl.program_id(2)
        strips = [slice(r * FOX_STRIP, (r + 1) * FOX_STRIP) for r in range(n_strip)]
        chains = [(e, r) for e in range(2) for r in range(n_strip)]
        qh, cq = {}, {}
        for e, r in chains:
            q = q_ref[strips[r], :] * ATT_SCALE
            qh[e, r] = jnp.where(_head_mask(e), q, jnp.zeros_like(q))
            cq[e, r] = cq_ref[strips[r], HEAD_DIM * e:HEAD_DIM * e + 1]

        def step(j, carry, masked):
            rows = pl.ds(pl.multiple_of(j * bk, bk), bk)
            kj, vj = k_ref[rows, :], v_ref[rows, :]
            ck = [cr_ref[pl.ds(2 * hp + e, 1), rows] for e in range(2)]
            out = []
            scores = [_dot_nt(qh[e, r], kj) for e, r in chains]
            for n, (e, r) in enumerate(chains):
                m, l, acc = carry[3 * n:3 * n + 3]
                sc = scores[n] + (cq[e, r] - ck[e])
                if masked:
                    qpos = i * bq + r * FOX_STRIP + lax.broadcasted_iota(jnp.int32, (FOX_STRIP, 1), 0)
                    kpos = j * bk + lax.broadcasted_iota(jnp.int32, (1, bk), 1)
                    sc = jnp.where(kpos <= qpos, sc, NEG)
                m_new = jnp.maximum(m, jnp.max(sc, axis=1, keepdims=True))
                alpha = jnp.exp(m - m_new)
                p = jnp.exp(sc - m_new)
                p_hi, p_lo = _split_bf16(p)
                out += [m_new, alpha * l + jnp.sum(p, axis=1, keepdims=True), alpha * acc + (_dot(p_hi, vj) + _dot(p_lo, vj))]
            return tuple(out)

        init = (jnp.full((FOX_STRIP, 1), NEG, F32), jnp.zeros((FOX_STRIP, 1), F32), jnp.zeros((FOX_STRIP, LANES), F32)) * len(chains)
        n_clear = (i * bq) // bk
        carry = lax.fori_loop(0, n_clear, functools.partial(step, masked=False), init)
        carry = lax.fori_loop(n_clear, (i * bq + bq + bk - 1) // bk, functools.partial(step, masked=True), carry)
        for r in range(n_strip):
            outs = [carry[3 * (e * n_strip + r) + 2] / carry[3 * (e * n_strip + r) + 1] for e in range(2)]
            lses = [carry[3 * (e * n_strip + r)] + jnp.log(carry[3 * (e * n_strip + r) + 1]) for e in range(2)]
            o = jnp.where(_head_mask(0), outs[0], outs[1])
            o_ref[strips[r], :] = o.astype(BF16)
            o32_ref[strips[r], :] = o
            lse_ref[strips[r], :] = jnp.where(_head_mask(0), lses[0], lses[1])

    def col(c0):
        return lambda b, hp, i: (b, 3 * hp + c0)

    blk = pl.BlockSpec((bq, LANES), lambda b, hp, i: (b * nq + i, hp))
    return pl.pallas_call(
        body, grid=(n_batch, N_PAIRS, nq),
        in_specs=[pl.BlockSpec((bq, LANES), lambda b, hp, i: (b * nq + i, 3 * hp)),
                  pl.BlockSpec((s, LANES), col(1)), pl.BlockSpec((s, LANES), col(2)), blk,
                  pl.BlockSpec((None, N_HEADS, s), lambda b, hp, i: (b, 0, 0))],
        out_specs=[blk, blk, blk],
        out_shape=[jax.ShapeDtypeStruct((t, MIX_HALF), BF16), jax.ShapeDtypeStruct((t, MIX_HALF), F32),
                   jax.ShapeDtypeStruct((t, MIX_HALF), F32)],
        name=name, compiler_params=_params("parallel", "parallel", "arbitrary"),
    )(zf, zf, zf, c_bc, c_row)


def _fox_bwd(zf, o32, dy, lse, c_bc, c_row, dz, n_batch, name):
    s, bq, bk = SEQ, FOX_BQ, FOX_BK
    nq, nk = s // bq, s // bk

    def body(q_ref, k_ref, v_ref, o_ref, do_ref, lse_ref, cq_ref, cr_ref, dz_in, dz_ref, dc_ref, dq_acc):
        del dz_in
        hp, j = pl.program_id(1), pl.program_id(2)

        @pl.when(j == 0)
        def _():
            dq_acc[...] = jnp.zeros_like(dq_acc)

        kj, vj = k_ref[...], v_ref[...]
        cols = pl.ds(pl.multiple_of(j * bk, bk), bk)
        km = [jnp.where(_head_mask(e), kj, jnp.zeros_like(kj)) for e in range(2)]
        ck = [cr_ref[pl.ds(2 * hp + e, 1), cols] for e in range(2)]

        def step(i, carry, masked):
            rows = pl.ds(pl.multiple_of(i * bq, bq), bq)
            qi, doi = q_ref[rows, :] * ATT_SCALE, do_ref[rows, :]
            prod = doi.astype(F32) * o_ref[rows, :]
            out = []
            dq = jnp.zeros((bq, LANES), F32)
            for e in range(2):
                dk_a, dv_a, dc_a = carry[3 * e:3 * e + 3]
                mask = _head_mask(e)
                lane0 = HEAD_DIM * e
                dom = jnp.where(mask, doi, jnp.zeros_like(doi))
                delta = jnp.sum(jnp.where(mask, prod, 0.0), axis=1, keepdims=True)
                sc = _dot_nt(qi, km[e]) + (cq_ref[rows, lane0:lane0 + 1] - ck[e])
                if masked:
                    sc = jnp.where(_causal(i, j, bq, bk), sc, NEG)
                p = jnp.exp(sc - lse_ref[rows, lane0:lane0 + 1])
                ds = p * (_dot_nt(dom, vj) - delta)
                dsb = ds.astype(BF16)
                dq = dq + _dot(dsb, km[e])
                out += [dk_a + _dot_tn(dsb, qi), dv_a + _dot_tn(p.astype(BF16), dom), dc_a - jnp.sum(ds, axis=0, keepdims=True)]
            dq_acc[rows, :] += dq * ATT_SCALE
            return tuple(out)

        init = (jnp.zeros((bk, LANES), F32), jnp.zeros((bk, LANES), F32), jnp.zeros((1, bk), F32)) * 2
        first = (j * bk) // bq
        n_diag = (j * bk + bk + bq - 1) // bq
        carry = lax.fori_loop(first, n_diag, functools.partial(step, masked=True), init)
        carry = lax.fori_loop(n_diag, nq, functools.partial(step, masked=False), carry)
        for e in range(2):
            dc_ref[e:e + 1, :] = carry[3 * e + 2]
        dz_ref[cols, LANES:2 * LANES] = jnp.where(_head_mask(0), carry[0], carry[3]).astype(BF16)
        dz_ref[cols, 2 * LANES:3 * LANES] = (carry[1] + carry[4]).astype(BF16)

        @pl.when(j == nk - 1)
        def _():
            dz_ref[:, 0:LANES] = dq_acc[...].astype(BF16)

    def seq(idx):
        return pl.BlockSpec((s, LANES), lambda b, hp, j: (b, idx(hp)))

    def kblk(c0):
        return pl.BlockSpec((bk, LANES), lambda b, hp, j: (b * nk + j, 3 * hp + c0))

    return pl.pallas_call(
        body, grid=(n_batch, N_PAIRS, nk),
        in_specs=[seq(lambda hp: 3 * hp), kblk(1), kblk(2), seq(lambda hp: hp), seq(lambda hp: N_PAIRS + hp),
                  seq(lambda hp: hp), seq(lambda hp: hp),
                  pl.BlockSpec((None, N_HEADS, s), lambda b, hp, j: (b, 0, 0)), pl.BlockSpec(memory_space=pl.ANY)],
        out_specs=[pl.BlockSpec((s, PAIR_WIDTH), lambda b, hp, j: (b, N_PAIRS + hp)),
                   pl.BlockSpec((None, None, 2, bk), lambda b, hp, j: (b, hp, 0, j))],
        out_shape=[jax.ShapeDtypeStruct(dz.shape, dz.dtype), jax.ShapeDtypeStruct((n_batch, N_PAIRS, 2, s), F32)],
        scratch_shapes=[pltpu.VMEM((s, LANES), F32)],
        input_output_aliases={8: 0},
        name=name, compiler_params=_params("parallel", "parallel", "arbitrary"),
    )(zf, zf, zf, o32, dy, lse, c_bc, c_row, dz)


def _dil_bias(slope, dil):
    qi = lax.broadcasted_iota(jnp.int32, (BLOCK, 2 * BLOCK), 0)
    kj = lax.broadcasted_iota(jnp.int32, (BLOCK, 2 * BLOCK), 1)
    delta = qi + BLOCK - kj
    return jnp.where((delta >= 0) & (delta <= BLOCK), (-slope * dil) * delta.astype(F32), NEG)


def _alibi_slope(hp, e):
    slope = jnp.float32(0.0)
    for k in range(N_PAIRS):
        slope = jnp.where(hp == k, jnp.float32(2.0 ** -(2 * k + e + 1)), slope)
    return slope


def _first_block_bias(bias):
    return jnp.where(lax.broadcasted_iota(jnp.int32, bias.shape, 1) < BLOCK, NEG, bias)


def _fill_bias(bias_scr, hp):
    for di, dil in enumerate(DILATIONS):
        for e in range(2):
            bias_scr[2 * di + e] = _dil_bias(_alibi_slope(hp, e), dil)


def _pair_specs(rows):
    return [pl.BlockSpec((rows, LANES), lambda b, hp, c0=c0: (b, 3 * hp + c0)) for c0 in range(3)]


def _strided(start, size, dil):
    return pl.ds(start, size) if dil == 1 else pl.ds(start, size, stride=dil)


def _for_each_block(dil, unit):
    span = BLOCK * dil
    nb = SEQ // span
    if dil == 1:
        group = 3
        assert (nb - 1) % group == 0
        unit(0, True)

        def later(g, c):
            for u in range(group):
                unit((1 + g * group + u) * span, False)
            return c

        lax.fori_loop(0, (nb - 1) // group, later, 0)
        return
    group = 4
    per = dil // group

    def firsts(g, c):
        for u in range(group):
            unit(g * group + u, True)
        return c

    lax.fori_loop(0, per, firsts, 0)
    if nb > 1:
        def later(i, c):
            for u in range(group):
                unit((1 + i // per) * span + (i % per) * group + u, False)
            return c

        lax.fori_loop(0, (nb - 1) * per, later, 0)


QUARTER = SEQ // 4


def _to_quarters(src, dst):
    for r in range(4):
        dst[r * QUARTER:(r + 1) * QUARTER, :] = src[pl.ds(r, QUARTER, stride=4), :]


def _from_quarters(src, dst):
    for r in range(4):
        dst[pl.ds(r, QUARTER, stride=4), :] = src[r * QUARTER:(r + 1) * QUARTER, :]


def _for_each_quarter_block(dil, unit):
    stride = dil // 4
    nb = QUARTER // (BLOCK * stride)

    def firsts(r, c):
        for g in range(stride):
            unit(r * QUARTER + g, True, stride)
        return c

    if stride == 1:
        for r in range(4):
            firsts(r, 0)
    else:
        lax.fori_loop(0, 4, firsts, 0)
    if nb > 1:
        def later(i, c):
            for r in range(4):
                start = r * QUARTER + (1 + i // stride) * BLOCK * stride + i % stride
                unit(pl.multiple_of(start, BLOCK) if stride == 1 else start, False, stride)
            return c

        lax.fori_loop(0, (nb - 1) * stride, later, 0)


def _mix_weights(l1, l2, l3):
    m = jnp.maximum(jnp.maximum(l1, l2), l3)
    e1, e2, e3 = jnp.exp(l1 - m), jnp.exp(l2 - m), jnp.exp(l3 - m)
    inv = 1.0 / (e1 + e2 + e3)
    return e1 * inv, e2 * inv, e3 * inv


def _dil_fwd(zd, n_batch, name):
    s = SEQ
    t = n_batch * s

    def body(q_ref, k_ref, v_ref, y_ref, l1_ref, l2_ref, l3_ref, o_scr, qkv4, o4, l4, bias_scr):
        _fill_bias(bias_scr, pl.program_id(1))
        for a, ref in enumerate((q_ref, k_ref, v_ref)):
            _to_quarters(ref, qkv4.at[a])

        def unit(srcs, start, first, stride, di, o_dst, l_dst):
            qrows = _strided(start, BLOCK, stride)
            krows = qrows if first else _strided(start - BLOCK * stride, 2 * BLOCK, stride)
            q = (srcs[0][qrows, :] * ATT_SCALE).astype(BF16)
            kc = srcs[1][krows, :].astype(BF16)
            vc = srcs[2][krows, :].astype(BF16)
            if first:
                kc, vc = jnp.concatenate([kc, kc]), jnp.concatenate([vc, vc])
            outs, lses = [], []
            for e in range(2):
                bias = _first_block_bias(bias_scr[2 * di + e]) if first else bias_scr[2 * di + e]
                sc = _dot_nt(jnp.where(_head_mask(e), q, jnp.zeros_like(q)), kc) + bias
                m = jnp.max(sc, axis=1, keepdims=True)
                pe = jnp.exp(sc - m)
                l = jnp.sum(pe, axis=1, keepdims=True)
                outs.append(_dot((pe * (1.0 / l)).astype(BF16), vc))
                lses.append(m + jnp.log(l))
            o_dst[qrows, :] = jnp.where(_head_mask(0), outs[0], outs[1])
            l_dst[qrows, :] = jnp.where(_head_mask(0), lses[0], lses[1])

        token_order = (q_ref, k_ref, v_ref)
        quarters = tuple(qkv4.at[a] for a in range(3))
        _for_each_block(1, lambda start, first: unit(token_order, start, first, 1, 0, o_scr.at[0], l1_ref))
        for di in (1, 2):
            _for_each_quarter_block(DILATIONS[di], lambda start, first, stride, di=di: unit(
                quarters, start, first, stride, di, o4.at[di - 1], l4.at[di - 1]))
        for di, l_ref in ((1, l2_ref), (2, l3_ref)):
            _from_quarters(o4.at[di - 1], o_scr.at[di])
            _from_quarters(l4.at[di - 1], l_ref)
        w = _mix_weights(l1_ref[...], l2_ref[...], l3_ref[...])
        y_ref[...] = (w[0] * o_scr[0] + w[1] * o_scr[1] + w[2] * o_scr[2]).astype(BF16)

    blk = pl.BlockSpec((s, LANES), lambda b, hp: (b, hp))
    res = pl.pallas_call(
        body, grid=(n_batch, N_PAIRS),
        in_specs=_pair_specs(s),
        out_specs=[blk] * 4,
        out_shape=[jax.ShapeDtypeStruct((t, MIX_HALF), BF16)] + [jax.ShapeDtypeStruct((t, MIX_HALF), F32)] * 3,
        scratch_shapes=[pltpu.VMEM((3, s, LANES), F32), pltpu.VMEM((3, s, LANES), F32), pltpu.VMEM((2, s, LANES), F32),
                        pltpu.VMEM((2, s, LANES), F32), pltpu.VMEM((6, BLOCK, 2 * BLOCK), F32)],
        name=name, compiler_params=_params("parallel", "arbitrary"),
    )(zd, zd, zd)
    return res[0], res[1:]


def _dil_bwd(zd, dy, ya, lses, n_batch, name):
    s = SEQ
    t = n_batch * s

    def body(q_ref, k_ref, v_ref, dy_ref, ya_ref, l1_ref, l2_ref, l3_ref, dz_ref, w_scr, dy_scr, dot_scr, acc, bias_scr):
        _fill_bias(bias_scr, pl.program_id(1))
        for di, w in enumerate(_mix_weights(l1_ref[...], l2_ref[...], l3_ref[...])):
            w_scr[di] = w
        dya = dy_ref[...].astype(F32)
        prod = dya * ya_ref[...].astype(F32)
        per_head = [jnp.sum(jnp.where(_head_mask(e), prod, 0.0), axis=1, keepdims=True) for e in range(2)]
        dy_scr[...] = dya
        dot_scr[...] = jnp.where(_head_mask(0), per_head[0], per_head[1])
        acc[...] = jnp.zeros_like(acc)
        lse_refs = (l1_ref, l2_ref, l3_ref)
        for di, dil in enumerate(DILATIONS):

            def unit(start, first, di=di, dil=dil):
                qrows = _strided(start, BLOCK, dil)
                krows = qrows if first else _strided(start - BLOCK * dil, 2 * BLOCK, dil)
                q = (q_ref[qrows, :] * ATT_SCALE).astype(BF16)
                kc = k_ref[krows, :].astype(BF16)
                vc = v_ref[krows, :].astype(BF16)
                wq = w_scr.at[di][qrows, :]
                do = (wq * dy_scr[qrows, :]).astype(BF16)
                sub = wq * dot_scr[qrows, :]
                lse = lse_refs[di][qrows, :]
                dq = jnp.zeros((BLOCK, LANES), F32)
                dk = jnp.zeros((krows.size, LANES), F32)
                dv = jnp.zeros((krows.size, LANES), F32)
                for e in range(2):
                    mask = _head_mask(e)
                    lane0 = HEAD_DIM * e
                    qh = jnp.where(mask, q, jnp.zeros_like(q))
                    doh = jnp.where(mask, do, jnp.zeros_like(do))
                    bias = bias_scr[2 * di + e]
                    sc = _dot_nt(qh, kc) + (bias[:, BLOCK:] if first else bias)
                    p = jnp.exp(sc - lse[:, lane0:lane0 + 1])
                    dsb = (p * (_dot_nt(doh, vc) - sub[:, lane0:lane0 + 1])).astype(BF16)
                    dq = dq + _dot(dsb, jnp.where(mask, kc, jnp.zeros_like(kc)))
                    dk = dk + _dot_tn(dsb, qh)
                    dv = dv + _dot_tn(p.astype(BF16), doh)
                acc.at[0][qrows, :] += dq * ATT_SCALE
                acc.at[1][krows, :] += dk
                acc.at[2][krows, :] += dv

            _for_each_block(dil, unit)
        for k in range(3):
            dz_ref[:, k * LANES:(k + 1) * LANES] = acc[k].astype(BF16)

    blk = pl.BlockSpec((s, LANES), lambda b, hp: (b, hp))
    pair = pl.BlockSpec((s, PAIR_WIDTH), lambda b, hp: (b, hp))
    return pl.pallas_call(
        body, grid=(n_batch, N_PAIRS),
        in_specs=_pair_specs(s) + [blk] * 5,
        out_specs=pair,
        out_shape=jax.ShapeDtypeStruct((t, 2 * 3 * MIX_HALF), BF16),
        scratch_shapes=[pltpu.VMEM((3, s, LANES), F32), pltpu.VMEM((s, LANES), F32), pltpu.VMEM((s, LANES), F32),
                        pltpu.VMEM((3, s, LANES), F32), pltpu.VMEM((6, BLOCK, 2 * BLOCK), F32)],
        name=name, compiler_params=_params("parallel", "arbitrary"),
    )(zd, zd, zd, dy, ya, *lses)


X_BQ = 512


def _xattn_probs(q, k):
    sc = _dot_nt(q, k) * X_SCALE
    pe = jnp.exp(sc - jnp.max(sc, axis=1, keepdims=True))
    return pe / jnp.sum(pe, axis=1, keepdims=True)


def _xattn_fwd(qx, kx, vx, n_batch, name):
    nq = SEQ // X_BQ

    def body(q_ref, k_ref, v_ref, o_ref):
        p = _xattn_probs(q_ref[...], k_ref[...])
        o_ref[...] = _dot(p.astype(BF16), v_ref[...]).astype(BF16)

    qblk = pl.BlockSpec((X_BQ, X_HEAD_DIM), lambda b, h, i: (b * nq + i, h))
    kblk = pl.BlockSpec((N_MEM, X_HEAD_DIM), lambda b, h, i: (b, h))
    return pl.pallas_call(
        body, grid=(n_batch, X_HEADS, nq), in_specs=[qblk, kblk, kblk], out_specs=qblk,
        out_shape=jax.ShapeDtypeStruct(qx.shape, BF16),
        name=name, compiler_params=_params("parallel", "parallel", "arbitrary"),
    )(qx, kx, vx)


def _xattn_bwd(qx, kx, vx, dox, n_batch, name):
    nq = SEQ // X_BQ

    def body(q_ref, k_ref, v_ref, do_ref, dq_ref, dk_ref, dv_ref, dk_acc, dv_acc):
        i = pl.program_id(2)

        @pl.when(i == 0)
        def _():
            dk_acc[...] = jnp.zeros_like(dk_acc)
            dv_acc[...] = jnp.zeros_like(dv_acc)

        q, k, do = q_ref[...], k_ref[...], do_ref[...]
        p = _xattn_probs(q, k)
        dp = _dot_nt(do, v_ref[...])
        dsb = (p * (dp - jnp.sum(p * dp, axis=1, keepdims=True))).astype(BF16)
        dq_ref[...] = (_dot(dsb, k) * X_SCALE).astype(BF16)
        dk_acc[...] += _dot_tn(dsb, q) * X_SCALE
        dv_acc[...] += _dot_tn(p.astype(BF16), do)

        @pl.when(i == nq - 1)
        def _():
            dk_ref[...] = dk_acc[...].astype(BF16)
            dv_ref[...] = dv_acc[...].astype(BF16)

    qblk = pl.BlockSpec((X_BQ, X_HEAD_DIM), lambda b, h, i: (b * nq + i, h))
    kblk = pl.BlockSpec((N_MEM, X_HEAD_DIM), lambda b, h, i: (b, h))
    return pl.pallas_call(
        body, grid=(n_batch, X_HEADS, nq), in_specs=[qblk, kblk, kblk, qblk], out_specs=[qblk, kblk, kblk],
        out_shape=[jax.ShapeDtypeStruct(qx.shape, BF16), jax.ShapeDtypeStruct(kx.shape, BF16), jax.ShapeDtypeStruct(kx.shape, BF16)],
        scratch_shapes=[pltpu.VMEM((N_MEM, X_HEAD_DIM), F32)] * 2,
        name=name, compiler_params=_params("parallel", "parallel", "arbitrary"),
    )(qx, kx, vx, dox)


def _adamw(w, g, m, v, name, rows):
    r, c = w.shape
    assert r % rows == 0, (name, w.shape, rows)

    def body(w_ref, g_ref, m_ref, v_ref, d_ref, nm_ref, nv_ref):
        gv = g_ref[...]
        m1 = ADAM_B1 * m_ref[...] + (1.0 - ADAM_B1) * gv
        v1 = ADAM_B2 * v_ref[...] + (1.0 - ADAM_B2) * jnp.square(gv)
        m_hat = m1 / (1.0 - ADAM_B1 ** ADAM_STEP)
        v_hat = v1 / (1.0 - ADAM_B2 ** ADAM_STEP)
        d_ref[...] = -ADAM_LR * (m_hat / (jnp.sqrt(v_hat) + ADAM_EPS) + ADAM_WD * w_ref[...])
        nm_ref[...] = m1
        nv_ref[...] = v1

    blk = pl.BlockSpec((rows, c), lambda i: (i, 0))
    return pl.pallas_call(
        body, grid=(r // rows,), in_specs=[blk] * 4, out_specs=[blk] * 3,
        out_shape=[jax.ShapeDtypeStruct((r, c), F32)] * 3,
        name=name, compiler_params=_params("arbitrary"),
    )(w, g, m, v)


def _relu2(acc):
    a = jnp.maximum(acc, 0.0)
    return acc, a * a


def _relu2_bwd(acc, u):
    return (2.0 * jnp.maximum(u.astype(F32), 0.0) * acc,)


def _local_step(x, mem, target, vecs, w_in, late_weights, on_grads=None):
    n_batch = x.shape[0]
    t = n_batch * SEQ
    x0 = x.reshape(t, D_MODEL)
    mem2 = mem.reshape(n_batch * N_MEM, D_MODEL)
    tgt = target.reshape(t, D_MODEL)

    half = 3 * MIX_HALF
    w_dil, w_fox = _pair_major(w_in[:, :half]), _pair_major(w_in[:, half:QKV_WIDTH])
    w_gate = jnp.pad(w_in[:, QKV_WIDTH:], ((0, 0), (0, GATE_PAD - N_HEADS)))
    b_pad = jnp.pad(vecs["b_forget"], (0, GATE_PAD - N_HEADS)).reshape(1, GATE_PAD)

    h1 = _rmsnorm(x0, vecs["g_mix"], "norm_mix")
    mn = _rmsnorm(mem2, vecs["g_mem"], "norm_mem")
    zd = _matmul(h1, w_dil, "in_dil", tn=768)[0]
    zf = _matmul(h1, w_fox, "in_fox", out_dtypes=(BF16,), tn=768)[0]
    gate = _matmul(h1, w_gate, "in_gate")[0]
    c_bc, c_row, sg = _gate_fwd(gate, b_pad, n_batch, "gate_fwd")
    ya, lses = _dil_fwd(zd, n_batch, "dil_fwd")
    yf, of32, lse_f = _fox_fwd(zf, c_bc, c_row, n_batch, "fox_fwd")
    wts = late_weights(yf)
    w_out = wts["w_out"]
    x1 = _matmul_res(ya, w_out[:MIX_HALF], x0, "out_a")
    x1 = _matmul_res(yf, w_out[MIX_HALF:], x1, "out_f")
    h2 = _rmsnorm(x1, vecs["g_xattn"], "norm_xattn")
    qx = _matmul(h2, wts["w_xq"], "xq", out_dtypes=(BF16,))[0]
    kx = _matmul(mn, wts["w_xk"], "xk", out_dtypes=(BF16,))[0]
    vx = _matmul(mn, wts["w_xv"], "xv", out_dtypes=(BF16,))[0]
    ox = _xattn_fwd(qx, kx, vx, n_batch, "xattn_fwd")
    x2 = _matmul_res(ox, wts["w_xo"], x1, "xo")
    h3 = _rmsnorm(x2, vecs["g_mlp"], "norm_mlp")
    u, a2 = _matmul(h3, wts["w_up"], "mlp_up", out_dtypes=(BF16, BF16), epilogue=_relu2, tn=1024)
    x3 = _matmul_res(a2, wts["w_down"], x2, "mlp_down")
    loss, dx3, dx3b, dg_final = _loss_bwd(x3, vecs["g_final"], tgt, "loss")

    du = _matmul(dx3b, wts["w_down"], "mlp_down_bwd", out_dtypes=(BF16,), extras=(u,), epilogue=_relu2_bwd, tn=1024, w_t=True)[0]
    gw_down = _matmul_tn(a2, dx3b, "gw_down")
    gw_up = _matmul_tn(h3, du, "gw_up")
    token = on_grads("mlp", dict(w_up=gw_up, w_down=gw_down)) if on_grads else None
    dh3 = _matmul(du, wts["w_up"], "mlp_up_bwd", w_t=True, after=token)[0]
    dx2, dx2b, dg_mlp = _rms_bwd(x2, dh3, vecs["g_mlp"], dx3, "norm_mlp_bwd")

    gw_xo = _matmul_tn(ox, dx2b, "gw_xo")
    dox = _matmul(dx2b, wts["w_xo"], "xo_bwd", out_dtypes=(BF16,), w_t=True)[0]
    dqx, dkx, dvx = _xattn_bwd(qx, kx, vx, dox, n_batch, "xattn_bwd")
    gw_xq = _matmul_tn(h2, dqx, "gw_xq")
    gw_xk = _matmul_tn(mn, dkx, "gw_xk")
    gw_xv = _matmul_tn(mn, dvx, "gw_xv")
    dh2 = _matmul(dqx, wts["w_xq"], "xq_bwd", w_t=True)[0]
    dmn = _matmul(dkx, wts["w_xk"], "xk_bwd", w_t=True)[0]
    dmn = _matmul_res(dvx, wts["w_xv"], dmn, "xv_bwd", w_t=True)
    _, _, dg_mem = _rms_bwd(mem2, dmn, vecs["g_mem"], None, "norm_mem_bwd")
    dx1, dx1b, dg_xattn = _rms_bwd(x1, dh2, vecs["g_xattn"], dx2, "norm_xattn_bwd")

    gw_out = jnp.concatenate([_matmul_tn(ya, dx1b, "gw_out_a"), _matmul_tn(yf, dx1b, "gw_out_f")], axis=0)
    token = on_grads("mid", dict(w_out=gw_out, w_xq=gw_xq, w_xk=gw_xk, w_xv=gw_xv, w_xo=gw_xo)) if on_grads else None
    dy = _matmul(dx1b, w_out, "out_bwd", out_dtypes=(BF16,), w_t=True, after=token)[0]
    dz = _dil_bwd(zd, dy, ya, lses, n_batch, "dil_bwd")
    dz, dc = _fox_bwd(zf, of32, dy, lse_f, c_bc, c_row, dz, n_batch, "fox_bwd")
    dzg, db = _gate_bwd(dc.reshape(n_batch, N_HEADS, SEQ), sg, "gate_bwd")
    gw_pm = _matmul_tn(h1, dz, "gw_in_qkv")
    gw_in = jnp.concatenate([_pair_major_inv(gw_pm[:, :half]), _pair_major_inv(gw_pm[:, half:]),
                             _matmul_tn(h1, dzg, "gw_in_gate")[:, :N_HEADS]], axis=1)
    dh1 = _matmul(dz, jnp.concatenate([w_dil, w_fox], axis=1), "in_qkv_bwd", w_t=True)[0]
    dh1 = _matmul_res(dzg, w_gate, dh1, "in_gate_bwd", w_t=True)
    dx0, _, dg_mix = _rms_bwd(x0, dh1, vecs["g_mix"], dx1, "norm_mix_bwd")

    gw = dict(w_in=gw_in, w_out=gw_out, w_xq=gw_xq, w_xk=gw_xk, w_xv=gw_xv, w_xo=gw_xo, w_up=gw_up, w_down=gw_down)
    gv = dict(g_mix=dg_mix, g_xattn=dg_xattn, g_mem=dg_mem, g_mlp=dg_mlp, g_final=dg_final, b_forget=db)
    return loss, dx0.reshape(x.shape), gw, gv


MESH = pl.DeviceIdType.MESH
ANY = pl.BlockSpec(memory_space=pl.ANY)


def _place():
    x, y, c = lax.axis_index("x"), lax.axis_index("y"), lax.axis_index("c")
    other_chips = [(1 - x, y), (x, 1 - y), (1 - x, 1 - y)]
    return x, y, c, other_chips


def _my_chip():
    return 2 * lax.axis_index("x") + lax.axis_index("y")


def _halves(rows, c, align):
    half = rows // 2
    assert rows % (2 * align) == 0, rows
    return pl.ds(pl.multiple_of(c * half, align), half), pl.ds(pl.multiple_of((1 - c) * half, align), half)


def _place_own(wall, pack):
    return lax.dynamic_update_slice(wall, pack[None], (_my_chip(), 0, 0))


def _gather(pack, name, after):
    def body(p_ref, after_ref, out_ref, send_sems, recv_sems, pass_send, pass_recv):
        del after_ref
        x, y, c, chips = _place()
        me = 2 * x + y
        mine, theirs = _halves(pack.shape[0], c, 16)

        def from_chip(k, chip, rows):
            src = out_ref.at[2 * chip[0] + chip[1], rows]
            return pltpu.make_async_remote_copy(src_ref=src, dst_ref=src, send_sem=send_sems.at[k], recv_sem=recv_sems.at[k],
                                                device_id=(chip[0], chip[1], c), device_id_type=MESH)

        def passed(k, chip, rows):
            src = out_ref.at[2 * chip[0] + chip[1], rows]
            return pltpu.make_async_remote_copy(src_ref=src, dst_ref=src, send_sem=pass_send.at[k], recv_sem=pass_recv.at[k],
                                                device_id=(x, y, 1 - c), device_id_type=MESH)

        sends = []
        for k, chip in enumerate(chips):
            cp = pltpu.make_async_remote_copy(src_ref=p_ref.at[mine], dst_ref=out_ref.at[me, mine], send_sem=send_sems.at[k],
                                              recv_sem=recv_sems.at[k], device_id=(chip[0], chip[1], c), device_id_type=MESH)
            cp.start()
            sends.append(cp)
        for k, chip in enumerate(chips):
            from_chip(k, chip, mine).wait_recv()
            cp = passed(k, chip, mine)
            cp.start()
            sends.append(cp)
        for k, chip in enumerate(chips):
            passed(k, chip, theirs).wait_recv()
        for cp in sends:
            cp.wait_send()

    wall = pl.pallas_call(
        body, in_specs=[ANY, ANY], out_specs=ANY,
        out_shape=jax.ShapeDtypeStruct((N_CHIPS,) + pack.shape, pack.dtype),
        scratch_shapes=[pltpu.SemaphoreType.DMA((3,))] * 4,
        name=name,
    )(pack, after)
    return _place_own(wall, pack)


HBM = pl.BlockSpec(memory_space=pltpu.HBM)
SEM = pl.BlockSpec(memory_space=pltpu.SEMAPHORE)
SPLIT_COPY = pltpu.CompilerParams(has_side_effects=pltpu.SideEffectType.DATAFLOW_SIDE_EFFECTING)


def _in_hbm(a):
    return pltpu.with_memory_space_constraint(a, pltpu.HBM)


def _start_call(start, src, land_shape, after, name):
    land = lax.empty(land_shape, src.dtype)

    def body(src_ref, land_ref, after_ref, send_sems, recv_sems, src_thru, land_thru, token):
        del after_ref, src_thru, land_thru
        start(src_ref, land_ref, send_sems, recv_sems)
        token[...] = jnp.zeros_like(token)

    return pl.pallas_call(
        body, name=name,
        out_shape=(pltpu.SemaphoreType.DMA((3,)), pltpu.SemaphoreType.DMA((3,)), pltpu.HBM(src.shape, src.dtype),
                   pltpu.HBM(land_shape, src.dtype), jax.ShapeDtypeStruct((8, LANES), F32)),
        in_specs=(HBM, HBM, ANY), out_specs=(SEM, SEM, HBM, HBM, pl.BlockSpec(memory_space=pltpu.VMEM)),
        input_output_aliases={0: 2, 1: 3}, compiler_params=SPLIT_COPY,
    )(_in_hbm(src), _in_hbm(land), after)


def _wait_call(body, started, after, name):
    send_sems, recv_sems, src, land, _ = started
    return pl.pallas_call(
        body, name=name,
        out_shape=(pltpu.HBM(src.shape, src.dtype), pltpu.HBM(land.shape, land.dtype)),
        in_specs=(HBM, HBM, SEM, SEM, ANY), out_specs=(HBM, HBM),
        input_output_aliases={0: 0, 1: 1}, compiler_params=SPLIT_COPY,
    )(src, land, send_sems, recv_sems, after)[1]


def _gather_copies(p_ref, wall_ref, send_sems, recv_sems):
    x, y, c, chips = _place()
    me = 2 * x + y
    mine, _ = _halves(p_ref.shape[0], c, 16)
    out, back = [], []
    for k, chip in enumerate(chips):
        peer = dict(send_sem=send_sems.at[k], recv_sem=recv_sems.at[k], device_id=(chip[0], chip[1], c), device_id_type=MESH)
        out.append(pltpu.make_async_remote_copy(src_ref=p_ref.at[mine], dst_ref=wall_ref.at[me, mine], **peer))
        slab = wall_ref.at[2 * chip[0] + chip[1], mine]
        back.append(pltpu.make_async_remote_copy(src_ref=slab, dst_ref=slab, **peer))
    return out, back


def _gather_start(pack, after, name):
    def start(p_ref, wall_ref, send_sems, recv_sems):
        for cp in _gather_copies(p_ref, wall_ref, send_sems, recv_sems)[0]:
            cp.start()

    return _start_call(start, pack, (N_CHIPS,) + pack.shape, after, name)


def _gather_wait(started, after, name):
    def body(p_ref, wall_ref, send_sems, recv_sems, after_ref, p_dead, wall_out):
        del after_ref, p_dead, wall_out
        out, back = _gather_copies(p_ref, wall_ref, send_sems, recv_sems)
        for cp_out, cp_back in zip(out, back):
            cp_out.wait_send()
            cp_back.wait_recv()

    return _wait_call(body, started, after, name)


def _pass_on(wall, name):
    def body(w_in_ref, out_ref, send_sems, recv_sems):
        del w_in_ref
        x, y, c, chips = _place()
        mine, theirs = _halves(wall.shape[1], c, 16)
        sends = []
        for k, chip in enumerate(chips):
            slab = out_ref.at[2 * chip[0] + chip[1]]
            peer = dict(send_sem=send_sems.at[k], recv_sem=recv_sems.at[k], device_id=(x, y, 1 - c), device_id_type=MESH)
            cp = pltpu.make_async_remote_copy(src_ref=slab.at[mine], dst_ref=slab.at[mine], **peer)
            cp.start()
            sends.append((cp, pltpu.make_async_remote_copy(src_ref=slab.at[theirs], dst_ref=slab.at[theirs], **peer)))
        for cp, back in sends:
            back.wait_recv()
            cp.wait_send()

    return pl.pallas_call(
        body, in_specs=[ANY], out_specs=ANY, out_shape=jax.ShapeDtypeStruct(wall.shape, wall.dtype),
        scratch_shapes=[pltpu.SemaphoreType.DMA((3,))] * 2, input_output_aliases={0: 0}, name=name,
    )(wall)


def _swap_halves(g, name):
    half = g.shape[1] // 2

    def body(g_ref, out_ref, send_sem, recv_sem):
        x, y, c, _ = _place()
        _, theirs = _halves(g.shape[1], c, 8)
        cp = pltpu.make_async_remote_copy(src_ref=g_ref.at[:, theirs], dst_ref=out_ref, send_sem=send_sem, recv_sem=recv_sem,
                                          device_id=(x, y, 1 - c), device_id_type=MESH)
        cp.start()
        cp.wait()

    return pl.pallas_call(
        body, in_specs=[ANY], out_specs=ANY,
        out_shape=jax.ShapeDtypeStruct((N_CHIPS, half, D_MODEL), F32),
        scratch_shapes=[pltpu.SemaphoreType.DMA, pltpu.SemaphoreType.DMA],
        name=name,
    )(g)


def _core_index():
    return lax.axis_index("c").astype(jnp.int32).reshape(1)


def _row_tile(half):
    tile = max(t for t in range(16, 1025, 16) if half % t == 0)
    return tile, half // tile


def _add_sibling(g, got, name):
    half = g.shape[1] // 2
    tile, n_tiles = _row_tile(half)

    def body(c_ref, g_ref, got_ref, o_ref):
        o_ref[...] = (g_ref[...] + got_ref[...]).astype(BF16)

    blk = pl.BlockSpec((None, tile, D_MODEL), lambda s, i, c_ref: (s, i, 0))
    return pl.pallas_call(
        body,
        grid_spec=pltpu.PrefetchScalarGridSpec(
            num_scalar_prefetch=1, grid=(N_CHIPS, n_tiles),
            in_specs=[pl.BlockSpec((None, tile, D_MODEL), lambda s, i, c_ref: (s, c_ref[0] * n_tiles + i, 0)), blk],
            out_specs=blk),
        out_shape=jax.ShapeDtypeStruct((N_CHIPS, half, D_MODEL), BF16),
        name=name, compiler_params=_params("arbitrary", "arbitrary"),
    )(_core_index(), g, got)


def _exchange_copies(p_ref, land_ref, send_sems, recv_sems):
    x, y, c, chips = _place()
    me = 2 * x + y
    out, back = [], []
    for k, chip in enumerate(chips):
        peer = dict(send_sem=send_sems.at[k], recv_sem=recv_sems.at[k], device_id=(chip[0], chip[1], c), device_id_type=MESH)
        out.append(pltpu.make_async_remote_copy(src_ref=p_ref.at[2 * chip[0] + chip[1]], dst_ref=land_ref.at[me], **peer))
        slab = land_ref.at[2 * chip[0] + chip[1]]
        back.append(pltpu.make_async_remote_copy(src_ref=slab, dst_ref=slab, **peer))
    return out, back


def _with_own(got, part):
    me = _my_chip()
    return lax.dynamic_update_slice(got, lax.dynamic_slice(part, (me, 0, 0), (1,) + part.shape[1:]), (me, 0, 0))


def _exchange_chips(part, name):
    def body(p_ref, out_ref, send_sems, recv_sems):
        out, back = _exchange_copies(p_ref, out_ref, send_sems, recv_sems)
        for cp in out:
            cp.start()
        for cp in back:
            cp.wait_recv()
        for cp in out:
            cp.wait_send()

    got = pl.pallas_call(
        body, in_specs=[ANY], out_specs=ANY,
        out_shape=jax.ShapeDtypeStruct(part.shape, part.dtype),
        scratch_shapes=[pltpu.SemaphoreType.DMA((3,)), pltpu.SemaphoreType.DMA((3,))],
        name=name,
    )(part)
    return _with_own(got, part)


def _exchange_start(part, name):
    def start(p_ref, land_ref, send_sems, recv_sems):
        for cp in _exchange_copies(p_ref, land_ref, send_sems, recv_sems)[0]:
            cp.start()

    return _start_call(start, part, part.shape, part, name)


def _exchange_wait(started, after, name):
    def body(p_ref, land_ref, send_sems, recv_sems, after_ref, p_dead, land_out):
        del after_ref, p_dead, land_out
        out, back = _exchange_copies(p_ref, land_ref, send_sems, recv_sems)
        for cp_out, cp_back in zip(out, back):
            cp_out.wait_send()
            cp_back.wait_recv()

    return _with_own(_wait_call(body, started, after, name), started[2])


def _sum_chips(parts, name):
    half = parts.shape[1]
    tile, n_tiles = _row_tile(half)

    def body(c_ref, p0, p1, p2, p3, o_ref):
        f32 = lambda p: p[...].astype(F32)
        o_ref[...] = ((f32(p0) + f32(p1)) + f32(p2)) + f32(p3)

    def slab(s):
        return pl.BlockSpec((None, tile, D_MODEL), lambda i, c_ref, s=s: (s, i, 0))

    return pl.pallas_call(
        body,
        grid_spec=pltpu.PrefetchScalarGridSpec(
            num_scalar_prefetch=1, grid=(n_tiles,),
            in_specs=[slab(s) for s in range(N_CHIPS)],
            out_specs=pl.BlockSpec((None, tile, D_MODEL), lambda i, c_ref: (c_ref[0], i, 0))),
        out_shape=jax.ShapeDtypeStruct((2, half, D_MODEL), F32),
        name=name, compiler_params=_params("arbitrary"),
    )(_core_index(), parts, parts, parts, parts)


def _share_halves(halves, name):
    def body(h_ref, out_ref, send_sem, recv_sem):
        del h_ref
        x, y, c, _ = _place()
        cp = pltpu.make_async_remote_copy(src_ref=out_ref.at[c], dst_ref=out_ref.at[c], send_sem=send_sem, recv_sem=recv_sem,
                                          device_id=(x, y, 1 - c), device_id_type=MESH)
        cp.start()
        pltpu.make_async_remote_copy(src_ref=out_ref.at[1 - c], dst_ref=out_ref.at[1 - c], send_sem=send_sem, recv_sem=recv_sem,
                                     device_id=(x, y, 1 - c), device_id_type=MESH).wait_recv()
        cp.wait_send()

    return pl.pallas_call(
        body, in_specs=[ANY], out_specs=ANY,
        out_shape=jax.ShapeDtypeStruct(halves.shape, halves.dtype),
        scratch_shapes=[pltpu.SemaphoreType.DMA] * 2,
        input_output_aliases={0: 0},
        name=name,
    )(halves)


def _reduce_parts(g, tag):
    return _add_sibling(g, _swap_halves(g, "swap_" + tag), "add_" + tag)


def _reduce_finish(got, tag):
    halves = _share_halves(_sum_chips(got, "sum_" + tag), "share_" + tag)
    return halves.reshape(2 * halves.shape[1], D_MODEL)


SMALL_ROWS = 8


def _allreduce_small(v):
    def body(v_ref, out_ref, buf, send_sems, recv_sems):
        x, y, c, _ = _place()
        buf[4 * x + 2 * y + c] = v_ref[...]
        sends = []
        for k in range(1, N_DEV):
            px = 1 - x if k & 4 else x
            py = 1 - y if k & 2 else y
            pc = 1 - c if k & 1 else c
            cp = pltpu.make_async_remote_copy(src_ref=v_ref, dst_ref=buf.at[4 * x + 2 * y + c], send_sem=send_sems.at[k - 1],
                                              recv_sem=recv_sems.at[k - 1], device_id=(px, py, pc), device_id_type=MESH)
            cp.start()
            sends.append((cp, 4 * px + 2 * py + pc))
        for k, (cp, peer) in enumerate(sends):
            pltpu.make_async_remote_copy(src_ref=v_ref, dst_ref=buf.at[peer], send_sem=send_sems.at[k], recv_sem=recv_sems.at[k],
                                         device_id=(x, y, c), device_id_type=MESH).wait_recv()
        for cp, _ in sends:
            cp.wait_send()
        total = buf[0]
        for d in range(1, N_DEV):
            total = total + buf[d]
        out_ref[...] = total

    vmem = pl.BlockSpec(memory_space=pltpu.VMEM)
    return pl.pallas_call(
        body, in_specs=[vmem], out_specs=vmem,
        out_shape=jax.ShapeDtypeStruct(v.shape, v.dtype),
        scratch_shapes=[pltpu.VMEM((N_DEV,) + v.shape, v.dtype), pltpu.SemaphoreType.DMA((N_DEV - 1,)),
                        pltpu.SemaphoreType.DMA((N_DEV - 1,))],
        name="allreduce_small",
    )(v)


MATRICES = ("w_in", "w_out", "w_xq", "w_xk", "w_xv", "w_xo", "w_up", "w_down")
VECTORS = ("g_mix", "g_xattn", "g_mem", "g_mlp", "g_final", "b_forget")
WEIGHT_ORDER = ("g_mix", "w_in", "b_forget", "w_out", "g_xattn", "g_mem", "w_xq", "w_xk", "w_xv", "w_xo",
                "g_mlp", "w_up", "w_down", "g_final")
GROUPS = {"mlp": ("w_up", "w_down"), "mid": ("w_out", "w_xq", "w_xk", "w_xv", "w_xo"), "in": ("w_in",)}
LATE = GROUPS["mid"] + GROUPS["mlp"]
W_IN_SHARD = IN_WIDTH // N_CHIPS
SHARD_ROWS = {"w_in": W_IN_SHARD, "w_out": 256, "w_xq": 256, "w_xk": 256, "w_xv": 256, "w_xo": 256, "w_up": 1024, "w_down": 1024}
PACK_ROWS = {n: -(-r // 32) * 32 for n, r in SHARD_ROWS.items()}
ADAM_ROWS = 128


def _pack(parts, names):
    return jnp.concatenate([jnp.pad(parts[n], ((0, PACK_ROWS[n] - SHARD_ROWS[n]), (0, 0))) for n in names], axis=0)


def _unpack(a, names):
    out, pos = {}, 0
    for n in names:
        out[n] = a[..., pos:pos + SHARD_ROWS[n], :]
        pos += PACK_ROWS[n]
    return out


def _full_weights(wall, names):
    cols = lambda a: a.transpose(1, 0, 2).reshape(a.shape[1], -1)
    rows = lambda a: a.reshape(-1, a.shape[-1])
    out = {}
    for n, a in _unpack(wall, names).items():
        if n == "w_in":
            out[n] = cols(a.reshape(N_CHIPS, D_MODEL, W_IN_SHARD))
        else:
            out[n] = cols(a) if n == "w_up" else rows(a)
    return out


def _shard_of(g, name, s):
    if name == "w_in":
        return g[:, s * W_IN_SHARD:(s + 1) * W_IN_SHARD].reshape(W_IN_SHARD, D_MODEL)
    if name == "w_up":
        return g[:, s * D_MODEL:(s + 1) * D_MODEL]
    n = SHARD_ROWS[name]
    return g[s * n:(s + 1) * n]


def _pack_grads(gws, names):
    return jnp.stack([_pack({n: _shard_of(gws[n], n, s) for n in names}, names) for s in range(N_CHIPS)])


def kernel(x, mem, g_mix, w_in, b_forget, w_out, g_xattn, g_mem, w_xq, w_xk, w_xv, w_xo, g_mlp, w_up, w_down, g_final, loss_target, m_g_mix, m_w_in, m_b_forget, m_w_out, m_g_xattn, m_g_mem, m_w_xq, m_w_xk, m_w_xv, m_w_xo, m_g_mlp, m_w_up, m_w_down, m_g_final, v_g_mix, v_w_in, v_b_forget, v_w_out, v_g_xattn, v_g_mem, v_w_xq, v_w_xk, v_w_xv, v_w_xo, v_g_mlp, v_w_up, v_w_down, v_g_final):
    given = dict(locals())
    weights = {n: given[n] for n in WEIGHT_ORDER}
    vecs = {n: weights[n] for n in VECTORS}

    shard = {n: weights[n].astype(BF16) for n in MATRICES}
    shard["w_in"] = shard["w_in"].reshape(W_IN_SHARD, D_MODEL)
    in_pack, late_pack = _pack(shard, GROUPS["in"]), _pack(shard, LATE)
    in_wall = _gather(in_pack, "gather_in", in_pack)
    late = _gather_start(late_pack, in_wall, "gather_late_start")
    w_in_full = _full_weights(in_wall, GROUPS["in"])["w_in"]

    def late_weights(after):
        wall = _pass_on(_gather_wait(late, after, "gather_late_wait"), "gather_late_pass")
        return _full_weights(_place_own(wall, late_pack), LATE)

    started = {}

    def on_grads(group, gws):
        part = _reduce_parts(_pack_grads(gws, GROUPS[group]), group)
        started[group] = _exchange_start(part, "exchange_%s_start" % group)
        return started[group][4]

    loss, grad_x, gw, gv = _local_step(x, mem, loss_target, vecs, w_in_full, late_weights, on_grads)

    part = _reduce_parts(_pack_grads(gw, GROUPS["in"]), "in")
    reduced = {"in": _reduce_finish(_exchange_chips(part, "exchange_in"), "in")}
    for group in ("mlp", "mid"):
        reduced[group] = _reduce_finish(_exchange_wait(started[group], grad_x, "exchange_%s_wait" % group), group)
    grads = {}
    for group, names in GROUPS.items():
        for n, a in _unpack(reduced[group], names).items():
            grads[n] = a.reshape(weights[n].shape)

    row = lambda a: jnp.pad(a.reshape(-1), (0, D_MODEL - a.size)).reshape(1, D_MODEL)
    small = jnp.concatenate([gv[n] for n in VECTORS[:5]] + [row(gv["b_forget"][:, 0]), row(loss[0, :1]),
                             jnp.zeros((1, D_MODEL), F32)], axis=0)
    small = _allreduce_small(small)
    for k, n in enumerate(VECTORS[:5]):
        grads[n] = small[k]
    grads["b_forget"] = small[5, :N_HEADS]
    loss_total = small[6, 0]

    delta, new_m, new_v = {}, {}, {}
    for n in MATRICES:
        delta[n], new_m[n], new_v[n] = _adamw(weights[n], grads[n], given["m_" + n], given["v_" + n], "adamw_" + n, ADAM_ROWS)
    stack = lambda prefix: jnp.concatenate([row(given[prefix + n]) for n in VECTORS] + [jnp.zeros((2, D_MODEL), F32)], axis=0)
    g_small = jnp.concatenate([small[:6], jnp.zeros((2, D_MODEL), F32)], axis=0)
    d, m1, v1 = _adamw(stack(""), g_small, stack("m_"), stack("v_"), "adamw_vectors", SMALL_ROWS)
    for k, n in enumerate(VECTORS):
        width = weights[n].shape[0]
        delta[n], new_m[n], new_v[n] = d[k, :width], m1[k, :width], v1[k, :width]

    return (loss_total, grad_x, *[grads[n] for n in WEIGHT_ORDER], *[delta[n] for n in WEIGHT_ORDER],
            *[new_m[n] for n in WEIGHT_ORDER], *[new_v[n] for n in WEIGHT_ORDER])
```

```python
import functools
import math

import jax
import jax.numpy as jnp
from jax import lax
from jax.experimental import pallas as pl
from jax.experimental.pallas import tpu as pltpu

F32 = jnp.float32
BF16 = jnp.bfloat16

D_MODEL = 1024
SEQ = 2048
N_MEM = 256
HEAD_DIM = 64
N_HEADS = 8
MIX_HALF = N_HEADS * HEAD_DIM
QKV_WIDTH = 6 * MIX_HALF
IN_WIDTH = QKV_WIDTH + N_HEADS
GATE_PAD = 128
BLOCK = 128
DILATIONS = (1, 4, 16)
X_HEADS = 4
X_HEAD_DIM = 256
D_FF = 4096
EPS = 1e-6
NEG = -1e30
ATT_SCALE = 1.0 / math.sqrt(HEAD_DIM)
X_SCALE = 1.0 / math.sqrt(X_HEAD_DIM)
LANES = 128
N_CHIPS = 4
N_DEV = 8

ADAM_LR = 0.001
ADAM_B1 = 0.9
ADAM_B2 = 0.999
ADAM_EPS = 1e-08
ADAM_WD = 0.01
ADAM_STEP = 10

VMEM_LIMIT = 48 * 1024 * 1024


def _params(*sem):
    return pltpu.CompilerParams(dimension_semantics=sem or None, vmem_limit_bytes=VMEM_LIMIT)


def _dot(a, b):
    return jnp.dot(a, b, preferred_element_type=F32)


def _dot_nt(a, b):
    return lax.dot_general(a, b, (((1,), (1,)), ((), ())), preferred_element_type=F32)


def _dot_tn(a, b):
    return lax.dot_general(a, b, (((0,), (0,)), ((), ())), preferred_element_type=F32)


def _dot_exact(x, e):
    hi = x.astype(BF16)
    r1 = x - hi.astype(F32)
    mid = r1.astype(BF16)
    lo = (r1 - mid.astype(F32)).astype(BF16)
    return _dot(hi, e) + _dot(mid, e) + _dot(lo, e)


def _head_mask(e):
    lane = lax.broadcasted_iota(jnp.int32, (1, LANES), 1)
    return (lane >= HEAD_DIM * e) & (lane < HEAD_DIM * (e + 1))


def _matmul(a, w, name, out_dtypes=(F32,), extras=(), epilogue=None, tm=1024, tn=512, w_t=False, after=None):
    m, k = a.shape
    n = w.shape[0] if w_t else w.shape[1]
    tm, tn = min(tm, m), min(tn, n)
    assert m % tm == 0 and n % tn == 0, (name, a.shape, w.shape)
    n_ex = len(extras)
    order = () if after is None else (after,)

    def body(a_ref, w_ref, *rest):
        rest = rest[len(order):]
        acc = (_dot_nt if w_t else _dot)(a_ref[...], w_ref[...])
        res = (acc,) if epilogue is None else epilogue(acc, *[r[...] for r in rest[:n_ex]])
        for o_ref, r in zip(rest[n_ex:], res):
            o_ref[...] = r.astype(o_ref.dtype)

    tile = pl.BlockSpec((tm, tn), lambda i, j: (i, j))
    w_spec = pl.BlockSpec((tn, k), lambda i, j: (j, 0)) if w_t else pl.BlockSpec((k, tn), lambda i, j: (0, j))
    return pl.pallas_call(
        body, grid=(m // tm, n // tn),
        in_specs=[pl.BlockSpec((tm, k), lambda i, j: (i, 0)), w_spec] + [pl.BlockSpec(memory_space=pl.ANY)] * len(order) + [tile] * n_ex,
        out_specs=[tile] * len(out_dtypes),
        out_shape=[jax.ShapeDtypeStruct((m, n), dt) for dt in out_dtypes],
        name=name, compiler_params=_params("parallel", "arbitrary"),
    )(a, w, *order, *extras)


def _matmul_res(a, w, res, name, w_t=False):
    return _matmul(a, w, name, extras=(res,), epilogue=lambda acc, r: (r + acc,), w_t=w_t)[0]


def _matmul_tn(x, y, name, tm=1024, tn=1024, tk=512, packed=None):
    t, m = x.shape
    _, n = y.shape
    tm, tn, tk = min(tm, m), min(tn, n), min(tk, t)
    assert m % tm == 0 and n % tn == 0 and t % tk == 0, (name, x.shape, y.shape)
    shape, place, into = packed or ((m, n), None, None)

    def body(x_ref, y_ref, *rest):
        o_ref = rest[-1]

        @pl.when(pl.program_id(2) == 0)
        def _():
            o_ref[...] = jnp.zeros_like(o_ref)

        o_ref[...] += _dot_tn(x_ref[...], y_ref[...])

    out_spec = (pl.BlockSpec((tm, tn), lambda i, j, k: (i, j)) if place is None
                else pl.BlockSpec((None, tm, tn), lambda i, j, k: place(i, j)))
    return pl.pallas_call(
        body, grid=(m // tm, n // tn, t // tk),
        in_specs=[pl.BlockSpec((tk, tm), lambda i, j, k: (k, i)), pl.BlockSpec((tk, tn), lambda i, j, k: (k, j))]
        + ([] if into is None else [pl.BlockSpec(memory_space=pl.ANY)]),
        out_specs=out_spec, out_shape=jax.ShapeDtypeStruct(shape, F32),
        input_output_aliases={} if into is None else {2: 0},
        name=name, compiler_params=_params("parallel", "parallel", "arbitrary"),
    )(x, y, *(() if into is None else (into,)))


def _rmsnorm(x, g, name, tm=512):
    t, d = x.shape
    tm = min(tm, t)

    def body(x_ref, g_ref, h_ref):
        xv = x_ref[...]
        r = lax.rsqrt(jnp.mean(xv * xv, axis=-1, keepdims=True) + EPS)
        h_ref[...] = (xv * r * g_ref[...]).astype(BF16)

    return pl.pallas_call(
        body, grid=(t // tm,),
        in_specs=[pl.BlockSpec((tm, d), lambda i: (i, 0)), pl.BlockSpec((1, d), lambda i: (0, 0))],
        out_specs=pl.BlockSpec((tm, d), lambda i: (i, 0)),
        out_shape=jax.ShapeDtypeStruct((t, d), BF16),
        name=name, compiler_params=_params("arbitrary"),
    )(x, g.reshape(1, d))


def _rms_bwd_tile(xv, dh, g):
    d = xv.shape[-1]
    r = lax.rsqrt(jnp.mean(xv * xv, axis=-1, keepdims=True) + EPS)
    dyg = dh * g
    proj = jnp.sum(dyg * xv, axis=-1, keepdims=True)
    dx = r * dyg - xv * (r * r * r * (1.0 / d)) * proj
    return dx, dh * (xv * r)


def _rms_bwd(x, dh, g, dres, name, tm=512):
    t, d = x.shape
    tm = min(tm, t)
    has_res = dres is not None

    def body(x_ref, dh_ref, g_ref, *rest):
        if has_res:
            res_ref, dx_ref, dxb_ref, dg_ref = rest
        else:
            dx_ref, dxb_ref, dg_ref = rest
        dx, dg_rows = _rms_bwd_tile(x_ref[...], dh_ref[...], g_ref[...])
        if has_res:
            dx = res_ref[...] + dx
        dx_ref[...] = dx
        dxb_ref[...] = dx.astype(BF16)

        @pl.when(pl.program_id(0) == 0)
        def _():
            dg_ref[...] = jnp.zeros_like(dg_ref)

        dg_ref[...] += jnp.sum(dg_rows, axis=0, keepdims=True)

    row = pl.BlockSpec((tm, d), lambda i: (i, 0))
    vec = pl.BlockSpec((1, d), lambda i: (0, 0))
    return pl.pallas_call(
        body, grid=(t // tm,),
        in_specs=[row, row, vec] + ([row] if has_res else []),
        out_specs=[row, row, vec],
        out_shape=[jax.ShapeDtypeStruct((t, d), F32), jax.ShapeDtypeStruct((t, d), BF16), jax.ShapeDtypeStruct((1, d), F32)],
        name=name, compiler_params=_params("arbitrary"),
    )(x, dh, g.reshape(1, d), *((dres,) if has_res else ()))


def _row_dots(a_refs, w_refs, w_t):
    acc = None
    for a_ref, w_ref in zip(a_refs, w_refs):
        part = (_dot_nt if w_t else _dot)(a_ref[...], w_ref[...])
        acc = part if acc is None else acc + part
    return acc


def _row_specs(a_parts, w_parts, tm):
    specs = [pl.BlockSpec((tm, a.shape[1]), lambda i: (i, 0)) for a in a_parts]
    return specs + [pl.BlockSpec(w.shape, lambda i: (0, 0)) for w in w_parts]


def _matmul_res_norm(a_parts, w_parts, res, g, name, tm=512):
    t, d = res.shape
    n = len(a_parts)

    def body(*refs):
        res_ref, g_ref, x_ref, h_ref = refs[2 * n:]
        xv = res_ref[...] + _row_dots(refs[:n], refs[n:2 * n], False)
        x_ref[...] = xv
        r = lax.rsqrt(jnp.mean(xv * xv, axis=-1, keepdims=True) + EPS)
        h_ref[...] = (xv * r * g_ref[...]).astype(BF16)

    row = pl.BlockSpec((tm, d), lambda i: (i, 0))
    return pl.pallas_call(
        body, grid=(t // tm,),
        in_specs=_row_specs(a_parts, w_parts, tm) + [row, pl.BlockSpec((1, d), lambda i: (0, 0))],
        out_specs=[row, row],
        out_shape=[jax.ShapeDtypeStruct((t, d), F32), jax.ShapeDtypeStruct((t, d), BF16)],
        name=name, compiler_params=_params("arbitrary"),
    )(*a_parts, *w_parts, res, g.reshape(1, d))


def _matmul_rms_bwd(a_parts, w_parts, x, g, dres, name, tm=512, after=None):
    t, d = x.shape
    n = len(a_parts)
    order = () if after is None else (after,)

    def body(*refs):
        x_ref, g_ref, res_ref = refs[2 * n:2 * n + 3]
        dx_ref, dxb_ref, dg_ref = refs[2 * n + 3 + len(order):]
        dx, dg_rows = _rms_bwd_tile(x_ref[...], _row_dots(refs[:n], refs[n:2 * n], True), g_ref[...])
        dx = res_ref[...] + dx
        dx_ref[...] = dx
        dxb_ref[...] = dx.astype(BF16)

        @pl.when(pl.program_id(0) == 0)
        def _():
            dg_ref[...] = jnp.zeros_like(dg_ref)

        dg_ref[...] += jnp.sum(dg_rows, axis=0, keepdims=True)

    row = pl.BlockSpec((tm, d), lambda i: (i, 0))
    vec = pl.BlockSpec((1, d), lambda i: (0, 0))
    return pl.pallas_call(
        body, grid=(t // tm,),
        in_specs=_row_specs(a_parts, w_parts, tm) + [row, vec, row] + [pl.BlockSpec(memory_space=pl.ANY)] * len(order),
        out_specs=[row, row, vec],
        out_shape=[jax.ShapeDtypeStruct((t, d), F32), jax.ShapeDtypeStruct((t, d), BF16), jax.ShapeDtypeStruct((1, d), F32)],
        name=name, compiler_params=_params("arbitrary"),
    )(*a_parts, *w_parts, x, g.reshape(1, d), dres, *order)


def _loss_bwd(x, g, target, name, tm=512):
    t, d = x.shape

    def body(x_ref, g_ref, t_ref, loss_ref, dx_ref, dxb_ref, dg_ref):
        xv = x_ref[...]
        gv = g_ref[...]
        r = lax.rsqrt(jnp.mean(xv * xv, axis=-1, keepdims=True) + EPS)
        err = xv * r * gv - t_ref[...]
        dx, dg_rows = _rms_bwd_tile(xv, err * (1.0 / d), gv)
        dx_ref[...] = dx
        dxb_ref[...] = dx.astype(BF16)

        @pl.when(pl.program_id(0) == 0)
        def _():
            dg_ref[...] = jnp.zeros_like(dg_ref)
            loss_ref[...] = jnp.zeros_like(loss_ref)

        dg_ref[...] += jnp.sum(dg_rows, axis=0, keepdims=True)
        part = jnp.sum(jnp.sum(err * err, axis=0, keepdims=True), axis=1, keepdims=True) * (0.5 / d)
        loss_ref[...] += jnp.broadcast_to(part, loss_ref.shape)

    row = pl.BlockSpec((tm, d), lambda i: (i, 0))
    vec = pl.BlockSpec((1, d), lambda i: (0, 0))
    return pl.pallas_call(
        body, grid=(t // tm,),
        in_specs=[row, vec, row],
        out_specs=[pl.BlockSpec((1, LANES), lambda i: (0, 0)), row, row, vec],
        out_shape=[jax.ShapeDtypeStruct((1, LANES), F32), jax.ShapeDtypeStruct((t, d), F32),
                   jax.ShapeDtypeStruct((t, d), BF16), jax.ShapeDtypeStruct((1, d), F32)],
        name=name, compiler_params=_params("arbitrary"),
    )(x, g.reshape(1, d), target)


def _tri(upper):
    r = lax.broadcasted_iota(jnp.int32, (LANES, LANES), 0)
    c = lax.broadcasted_iota(jnp.int32, (LANES, LANES), 1)
    return jnp.where((r <= c) if upper else (r >= c), 1.0, 0.0).astype(BF16)


def _gate_fwd(gate, b_pad, n_batch, name):
    s = SEQ
    nblk = s // LANES

    def body(g_ref, b_ref, cbc_ref, crow_ref, sg_ref, ct_ref):
        gz = g_ref[...] + b_ref[...]
        logf = jnp.minimum(gz, 0.0) - jnp.log(1.0 + jnp.exp(-jnp.abs(gz)))
        logf_t = logf.T
        sg_ref[...] = (1.0 / (1.0 + jnp.exp(gz))).T[0:N_HEADS]
        upper = _tri(True)
        carry = jnp.zeros((LANES, 1), F32)
        for blk in range(nblk):
            seg = _dot_exact(logf_t[:, blk * LANES:(blk + 1) * LANES], upper) + carry
            carry = seg[:, LANES - 1:LANES]
            ct_ref[:, blk * LANES:(blk + 1) * LANES] = seg
        ct = ct_ref[...]
        crow_ref[...] = ct[0:N_HEADS]
        c_col = ct.T
        lane = lax.broadcasted_iota(jnp.int32, (1, MIX_HALF), 1)
        acc = jnp.zeros((s, MIX_HALF), F32)
        for h in range(N_HEADS):
            acc = jnp.where((lane >= HEAD_DIM * h) & (lane < HEAD_DIM * (h + 1)), c_col[:, h:h + 1], acc)
        cbc_ref[...] = acc

    return pl.pallas_call(
        body, grid=(n_batch,),
        in_specs=[pl.BlockSpec((s, GATE_PAD), lambda b: (b, 0)), pl.BlockSpec((1, GATE_PAD), lambda b: (0, 0))],
        out_specs=[pl.BlockSpec((s, MIX_HALF), lambda b: (b, 0)),
                   pl.BlockSpec((None, N_HEADS, s), lambda b: (b, 0, 0)),
                   pl.BlockSpec((None, N_HEADS, s), lambda b: (b, 0, 0))],
        out_shape=[jax.ShapeDtypeStruct((n_batch * s, MIX_HALF), F32),
                   jax.ShapeDtypeStruct((n_batch, N_HEADS, s), F32),
                   jax.ShapeDtypeStruct((n_batch, N_HEADS, s), F32)],
        scratch_shapes=[pltpu.VMEM((LANES, s), F32)],
        name=name, compiler_params=_params("arbitrary"),
    )(gate, b_pad)


def _gate_bwd(dc, sg, name):
    n_batch, _, s = dc.shape
    nblk = s // LANES

    def body(dc_ref, sg_ref, dz_ref, db_ref, dt_ref):
        lower = _tri(False)
        dcv = dc_ref[...]
        carry = jnp.zeros((N_HEADS, 1), F32)
        dt_ref[...] = jnp.zeros_like(dt_ref)
        for blk in reversed(range(nblk)):
            seg = _dot_exact(dcv[:, blk * LANES:(blk + 1) * LANES], lower) + carry
            carry = seg[:, 0:1]
            dt_ref[0:N_HEADS, blk * LANES:(blk + 1) * LANES] = seg * sg_ref[:, blk * LANES:(blk + 1) * LANES]
        dg_t = dt_ref[...]
        dz_ref[...] = dg_t.T.astype(BF16)

        @pl.when(pl.program_id(0) == 0)
        def _():
            db_ref[...] = jnp.zeros_like(db_ref)

        db_ref[...] += jnp.broadcast_to(jnp.sum(dg_t[0:N_HEADS], axis=1, keepdims=True), db_ref.shape)

    return pl.pallas_call(
        body, grid=(n_batch,),
        in_specs=[pl.BlockSpec((None, N_HEADS, s), lambda b: (b, 0, 0)), pl.BlockSpec((None, N_HEADS, s), lambda b: (b, 0, 0))],
        out_specs=[pl.BlockSpec((s, GATE_PAD), lambda b: (b, 0)), pl.BlockSpec((N_HEADS, LANES), lambda b: (0, 0))],
        out_shape=[jax.ShapeDtypeStruct((n_batch * s, GATE_PAD), BF16), jax.ShapeDtypeStruct((N_HEADS, LANES), F32)],
        scratch_shapes=[pltpu.VMEM((LANES, s), F32)],
        name=name, compiler_params=_params("arbitrary"),
    )(dc, sg)


FOX_BQ = 512
FOX_BK = 512
FOX_STRIP = 512
PAIR_WIDTH = 3 * LANES
N_PAIRS = N_HEADS // 2


def _pair_major(w):
    return w.reshape(w.shape[0], 3, N_PAIRS, LANES).transpose(0, 2, 1, 3).reshape(w.shape[0], 3 * MIX_HALF)


def _pair_major_inv(w):
    return w.reshape(w.shape[0], N_PAIRS, 3, LANES).transpose(0, 2, 1, 3).reshape(w.shape[0], 3 * MIX_HALF)


def _causal(i, j, bq, bk):
    qpos = i * bq + lax.broadcasted_iota(jnp.int32, (bq, 1), 0)
    kpos = j * bk + lax.broadcasted_iota(jnp.int32, (1, bk), 1)
    return kpos <= qpos


def _split_bf16(p):
    hi = p.astype(BF16)
    return hi, (p - hi.astype(F32)).astype(BF16)


def _fox_fwd(zf, c_bc, c_row, n_batch, name):
    s, bq, bk = SEQ, FOX_BQ, FOX_BK
    nq = s // bq
    t = n_batch * s

    n_strip = bq // FOX_STRIP

    def body(q_ref, k_ref, v_ref, cq_ref, cr_ref, o_ref, o32_ref, lse_ref):
        hp, i = pl.program_id(1), pl.program_id(2)
        strips = [slice(r * FOX_STRIP, (r + 1) * FOX_STRIP) for r in range(n_strip)]
        chains = [(e, r) for e in range(2) for r in range(n_strip)]
        qh, cq = {}, {}
        for e, r in chains:
            q = q_ref[strips[r], :] * ATT_SCALE
            qh[e, r] = jnp.where(_head_mask(e), q, jnp.zeros_like(q))
            cq[e, r] = cq_ref[strips[r], HEAD_DIM * e:HEAD_DIM * e + 1]

        def step(j, carry, masked):
            rows = pl.ds(pl.multiple_of(j * bk, bk), bk)
            kj, vj = k_ref[rows, :], v_ref[rows, :]
            ck = [cr_ref[pl.ds(2 * hp + e, 1), rows] for e in range(2)]
            out = []
            scores = [_dot_nt(qh[e, r], kj) for e, r in chains]
            for n, (e, r) in enumerate(chains):
                m, l, acc = carry[3 * n:3 * n + 3]
                sc = scores[n] + (cq[e, r] - ck[e])
                if masked:
                    qpos = i * bq + r * FOX_STRIP + lax.broadcasted_iota(jnp.int32, (FOX_STRIP, 1), 0)
                    kpos = j * bk + lax.broadcasted_iota(jnp.int32, (1, bk), 1)
                    sc = jnp.where(kpos <= qpos, sc, NEG)
                m_new = jnp.maximum(m, jnp.max(sc, axis=1, keepdims=True))
                alpha = jnp.exp(m - m_new)
                p = jnp.exp(sc - m_new)
                p_hi, p_lo = _split_bf16(p)
                out += [m_new, alpha * l + jnp.sum(p, axis=1, keepdims=True), alpha * acc + (_dot(p_hi, vj) + _dot(p_lo, vj))]
            return tuple(out)

        init = (jnp.full((FOX_STRIP, 1), NEG, F32), jnp.zeros((FOX_STRIP, 1), F32), jnp.zeros((FOX_STRIP, LANES), F32)) * len(chains)
        n_clear = (i * bq) // bk
        carry = lax.fori_loop(0, n_clear, functools.partial(step, masked=False), init)
        carry = lax.fori_loop(n_clear, (i * bq + bq + bk - 1) // bk, functools.partial(step, masked=True), carry)
        for r in range(n_strip):
            outs = [carry[3 * (e * n_strip + r) + 2] / carry[3 * (e * n_strip + r) + 1] for e in range(2)]
            lses = [carry[3 * (e * n_strip + r)] + jnp.log(carry[3 * (e * n_strip + r) + 1]) for e in range(2)]
            o = jnp.where(_head_mask(0), outs[0], outs[1])
            o_ref[strips[r], :] = o.astype(BF16)
            o32_ref[strips[r], :] = o
            lse_ref[strips[r], :] = jnp.where(_head_mask(0), lses[0], lses[1])

    def col(c0):
        return lambda b, hp, i: (b, 3 * hp + c0)

    blk = pl.BlockSpec((bq, LANES), lambda b, hp, i: (b * nq + i, hp))
    return pl.pallas_call(
        body, grid=(n_batch, N_PAIRS, nq),
        in_specs=[pl.BlockSpec((bq, LANES), lambda b, hp, i: (b * nq + i, 3 * hp)),
                  pl.BlockSpec((s, LANES), col(1)), pl.BlockSpec((s, LANES), col(2)), blk,
                  pl.BlockSpec((None, N_HEADS, s), lambda b, hp, i: (b, 0, 0))],
        out_specs=[blk, blk, blk],
        out_shape=[jax.ShapeDtypeStruct((t, MIX_HALF), BF16), jax.ShapeDtypeStruct((t, MIX_HALF), F32),
                   jax.ShapeDtypeStruct((t, MIX_HALF), F32)],
        name=name, compiler_params=_params("parallel", "parallel", "arbitrary"),
    )(zf, zf, zf, c_bc, c_row)


def _fox_bwd(zf, o32, dy, lse, c_bc, c_row, dz, n_batch, name):
    s, bq, bk = SEQ, FOX_BQ, FOX_BK
    nq, nk = s // bq, s // bk

    def body(q_ref, k_ref, v_ref, o_ref, do_ref, lse_ref, cq_ref, cr_ref, dz_in, dz_ref, dc_ref, dq_acc):
        del dz_in
        hp, j = pl.program_id(1), pl.program_id(2)

        @pl.when(j == 0)
        def _():
            dq_acc[...] = jnp.zeros_like(dq_acc)

        kj, vj = k_ref[...], v_ref[...]
        cols = pl.ds(pl.multiple_of(j * bk, bk), bk)
        km = [jnp.where(_head_mask(e), kj, jnp.zeros_like(kj)) for e in range(2)]
        ck = [cr_ref[pl.ds(2 * hp + e, 1), cols] for e in range(2)]

        def step(i, carry, masked):
            rows = pl.ds(pl.multiple_of(i * bq, bq), bq)
            qi, doi = q_ref[rows, :] * ATT_SCALE, do_ref[rows, :]
            prod = doi.astype(F32) * o_ref[rows, :]
            out = []
            dq = jnp.zeros((bq, LANES), F32)
            for e in range(2):
                dk_a, dv_a, dc_a = carry[3 * e:3 * e + 3]
                mask = _head_mask(e)
                lane0 = HEAD_DIM * e
                dom = jnp.where(mask, doi, jnp.zeros_like(doi))
                delta = jnp.sum(jnp.where(mask, prod, 0.0), axis=1, keepdims=True)
                sc = _dot_nt(qi, km[e]) + (cq_ref[rows, lane0:lane0 + 1] - ck[e])
                if masked:
                    sc = jnp.where(_causal(i, j, bq, bk), sc, NEG)
                p = jnp.exp(sc - lse_ref[rows, lane0:lane0 + 1])
                ds = p * (_dot_nt(dom, vj) - delta)
                dsb = ds.astype(BF16)
                dq = dq + _dot(dsb, km[e])
                out += [dk_a + _dot_tn(dsb, qi), dv_a + _dot_tn(p.astype(BF16), dom), dc_a - jnp.sum(ds, axis=0, keepdims=True)]
            dq_acc[rows, :] += dq * ATT_SCALE
            return tuple(out)

        init = (jnp.zeros((bk, LANES), F32), jnp.zeros((bk, LANES), F32), jnp.zeros((1, bk), F32)) * 2
        first = (j * bk) // bq
        n_diag = (j * bk + bk + bq - 1) // bq
        carry = lax.fori_loop(first, n_diag, functools.partial(step, masked=True), init)
        carry = lax.fori_loop(n_diag, nq, functools.partial(step, masked=False), carry)
        for e in range(2):
            dc_ref[e:e + 1, :] = carry[3 * e + 2]
        dz_ref[cols, LANES:2 * LANES] = jnp.where(_head_mask(0), carry[0], carry[3]).astype(BF16)
        dz_ref[cols, 2 * LANES:3 * LANES] = (carry[1] + carry[4]).astype(BF16)

        @pl.when(j == nk - 1)
        def _():
            dz_ref[:, 0:LANES] = dq_acc[...].astype(BF16)

    def seq(idx):
        return pl.BlockSpec((s, LANES), lambda b, hp, j: (b, idx(hp)))

    def kblk(c0):
        return pl.BlockSpec((bk, LANES), lambda b, hp, j: (b * nk + j, 3 * hp + c0))

    return pl.pallas_call(
        body, grid=(n_batch, N_PAIRS, nk),
        in_specs=[seq(lambda hp: 3 * hp), kblk(1), kblk(2), seq(lambda hp: hp), seq(lambda hp: N_PAIRS + hp),
                  seq(lambda hp: hp), seq(lambda hp: hp),
                  pl.BlockSpec((None, N_HEADS, s), lambda b, hp, j: (b, 0, 0)), pl.BlockSpec(memory_space=pl.ANY)],
        out_specs=[pl.BlockSpec((s, PAIR_WIDTH), lambda b, hp, j: (b, N_PAIRS + hp)),
                   pl.BlockSpec((None, None, 2, bk), lambda b, hp, j: (b, hp, 0, j))],
        out_shape=[jax.ShapeDtypeStruct(dz.shape, dz.dtype), jax.ShapeDtypeStruct((n_batch, N_PAIRS, 2, s), F32)],
        scratch_shapes=[pltpu.VMEM((s, LANES), F32)],
        input_output_aliases={8: 0},
        name=name, compiler_params=_params("parallel", "parallel", "arbitrary"),
    )(zf, zf, zf, o32, dy, lse, c_bc, c_row, dz)


def _dil_bias(slope, dil):
    qi = lax.broadcasted_iota(jnp.int32, (BLOCK, 2 * BLOCK), 0)
    kj = lax.broadcasted_iota(jnp.int32, (BLOCK, 2 * BLOCK), 1)
    delta = qi + BLOCK - kj
    return jnp.where((delta >= 0) & (delta <= BLOCK), (-slope * dil) * delta.astype(F32), NEG)


def _alibi_slope(hp, e):
    slope = jnp.float32(0.0)
    for k in range(N_PAIRS):
        slope = jnp.where(hp == k, jnp.float32(2.0 ** -(2 * k + e + 1)), slope)
    return slope


def _first_block_bias(bias):
    return jnp.where(lax.broadcasted_iota(jnp.int32, bias.shape, 1) < BLOCK, NEG, bias)


def _fill_bias(bias_scr, hp):
    for di, dil in enumerate(DILATIONS):
        for e in range(2):
            bias_scr[2 * di + e] = _dil_bias(_alibi_slope(hp, e), dil)


def _pair_specs(rows):
    return [pl.BlockSpec((rows, LANES), lambda b, hp, c0=c0: (b, 3 * hp + c0)) for c0 in range(3)]


def _strided(start, size, dil):
    return pl.ds(start, size) if dil == 1 else pl.ds(start, size, stride=dil)


def _for_each_block(dil, unit):
    span = BLOCK * dil
    nb = SEQ // span
    if dil == 1:
        group = 3
        assert (nb - 1) % group == 0
        unit(0, True)

        def later(g, c):
            for u in range(group):
                unit((1 + g * group + u) * span, False)
            return c

        lax.fori_loop(0, (nb - 1) // group, later, 0)
        return
    group = 4
    per = dil // group

    def firsts(g, c):
        for u in range(group):
            unit(g * group + u, True)
        return c

    lax.fori_loop(0, per, firsts, 0)
    if nb > 1:
        def later(i, c):
            for u in range(group):
                unit((1 + i // per) * span + (i % per) * group + u, False)
            return c

        lax.fori_loop(0, (nb - 1) * per, later, 0)


QUARTER = SEQ // 4


def _to_quarters(src, dst):
    for r in range(4):
        dst[r * QUARTER:(r + 1) * QUARTER, :] = src[pl.ds(r, QUARTER, stride=4), :]


def _from_quarters(src, dst):
    for r in range(4):
        dst[pl.ds(r, QUARTER, stride=4), :] = src[r * QUARTER:(r + 1) * QUARTER, :]


def _for_each_quarter_block(dil, unit):
    stride = dil // 4
    nb = QUARTER // (BLOCK * stride)

    def firsts(r, c):
        for g in range(stride):
            unit(r * QUARTER + g, True, stride)
        return c

    if stride == 1:
        for r in range(4):
            firsts(r, 0)
    else:
        lax.fori_loop(0, 4, firsts, 0)
    if nb > 1:
        def later(i, c):
            for r in range(4):
                start = r * QUARTER + (1 + i // stride) * BLOCK * stride + i % stride
                unit(pl.multiple_of(start, BLOCK) if stride == 1 else start, False, stride)
            return c

        lax.fori_loop(0, (nb - 1) * stride, later, 0)


def _mix_weights(l1, l2, l3):
    m = jnp.maximum(jnp.maximum(l1, l2), l3)
    e1, e2, e3 = jnp.exp(l1 - m), jnp.exp(l2 - m), jnp.exp(l3 - m)
    inv = 1.0 / (e1 + e2 + e3)
    return e1 * inv, e2 * inv, e3 * inv


def _dil_fwd(zd, n_batch, name):
    s = SEQ
    t = n_batch * s

    def body(q_ref, k_ref, v_ref, y_ref, l1_ref, l2_ref, l3_ref, o_scr, qkv4, o4, l4, bias_scr):
        _fill_bias(bias_scr, pl.program_id(1))
        for a, ref in enumerate((q_ref, k_ref, v_ref)):
            _to_quarters(ref, qkv4.at[a])

        def unit(srcs, start, first, stride, di, o_dst, l_dst):
            qrows = _strided(start, BLOCK, stride)
            krows = qrows if first else _strided(start - BLOCK * stride, 2 * BLOCK, stride)
            q = (srcs[0][qrows, :] * ATT_SCALE).astype(BF16)
            kc = srcs[1][krows, :].astype(BF16)
            vc = srcs[2][krows, :].astype(BF16)
            if first:
                kc, vc = jnp.concatenate([kc, kc]), jnp.concatenate([vc, vc])
            outs, lses = [], []
            for e in range(2):
                bias = _first_block_bias(bias_scr[2 * di + e]) if first else bias_scr[2 * di + e]
                sc = _dot_nt(jnp.where(_head_mask(e), q, jnp.zeros_like(q)), kc) + bias
                m = jnp.max(sc, axis=1, keepdims=True)
                pe = jnp.exp(sc - m)
                l = jnp.sum(pe, axis=1, keepdims=True)
                outs.append(_dot((pe * (1.0 / l)).astype(BF16), vc))
                lses.append(m + jnp.log(l))
            o_dst[qrows, :] = jnp.where(_head_mask(0), outs[0], outs[1])
            l_dst[qrows, :] = jnp.where(_head_mask(0), lses[0], lses[1])

        token_order = (q_ref, k_ref, v_ref)
        quarters = tuple(qkv4.at[a] for a in range(3))
        _for_each_block(1, lambda start, first: unit(token_order, start, first, 1, 0, o_scr.at[0], l1_ref))
        for di in (1, 2):
            _for_each_quarter_block(DILATIONS[di], lambda start, first, stride, di=di: unit(
                quarters, start, first, stride, di, o4.at[di - 1], l4.at[di - 1]))
        for di, l_ref in ((1, l2_ref), (2, l3_ref)):
            _from_quarters(o4.at[di - 1], o_scr.at[di])
            _from_quarters(l4.at[di - 1], l_ref)
        w = _mix_weights(l1_ref[...], l2_ref[...], l3_ref[...])
        y_ref[...] = (w[0] * o_scr[0] + w[1] * o_scr[1] + w[2] * o_scr[2]).astype(BF16)

    blk = pl.BlockSpec((s, LANES), lambda b, hp: (b, hp))
    res = pl.pallas_call(
        body, grid=(n_batch, N_PAIRS),
        in_specs=_pair_specs(s),
        out_specs=[blk] * 4,
        out_shape=[jax.ShapeDtypeStruct((t, MIX_HALF), BF16)] + [jax.ShapeDtypeStruct((t, MIX_HALF), F32)] * 3,
        scratch_shapes=[pltpu.VMEM((3, s, LANES), F32), pltpu.VMEM((3, s, LANES), F32), pltpu.VMEM((2, s, LANES), F32),
                        pltpu.VMEM((2, s, LANES), F32), pltpu.VMEM((6, BLOCK, 2 * BLOCK), F32)],
        name=name, compiler_params=_params("parallel", "arbitrary"),
    )(zd, zd, zd)
    return res[0], res[1:]


def _dil_bwd(zd, dy, ya, lses, n_batch, name):
    s = SEQ
    t = n_batch * s

    def body(q_ref, k_ref, v_ref, dy_ref, ya_ref, l1_ref, l2_ref, l3_ref, dz_ref, w_scr, dy_scr, dot_scr, acc, bias_scr):
        _fill_bias(bias_scr, pl.program_id(1))
        for di, w in enumerate(_mix_weights(l1_ref[...], l2_ref[...], l3_ref[...])):
            w_scr[di] = w
        dya = dy_ref[...].astype(F32)
        prod = dya * ya_ref[...].astype(F32)
        per_head = [jnp.sum(jnp.where(_head_mask(e), prod, 0.0), axis=1, keepdims=True) for e in range(2)]
        dy_scr[...] = dya
        dot_scr[...] = jnp.where(_head_mask(0), per_head[0], per_head[1])
        acc[...] = jnp.zeros_like(acc)
        lse_refs = (l1_ref, l2_ref, l3_ref)
        for di, dil in enumerate(DILATIONS):

            def unit(start, first, di=di, dil=dil):
                qrows = _strided(start, BLOCK, dil)
                krows = qrows if first else _strided(start - BLOCK * dil, 2 * BLOCK, dil)
                q = (q_ref[qrows, :] * ATT_SCALE).astype(BF16)
                kc = k_ref[krows, :].astype(BF16)
                vc = v_ref[krows, :].astype(BF16)
                wq = w_scr.at[di][qrows, :]
                do = (wq * dy_scr[qrows, :]).astype(BF16)
                sub = wq * dot_scr[qrows, :]
                lse = lse_refs[di][qrows, :]
                dq = jnp.zeros((BLOCK, LANES), F32)
                dk = jnp.zeros((krows.size, LANES), F32)
                dv = jnp.zeros((krows.size, LANES), F32)
                for e in range(2):
                    mask = _head_mask(e)
                    lane0 = HEAD_DIM * e
                    qh = jnp.where(mask, q, jnp.zeros_like(q))
                    doh = jnp.where(mask, do, jnp.zeros_like(do))
                    bias = bias_scr[2 * di + e]
                    sc = _dot_nt(qh, kc) + (bias[:, BLOCK:] if first else bias)
                    p = jnp.exp(sc - lse[:, lane0:lane0 + 1])
                    dsb = (p * (_dot_nt(doh, vc) - sub[:, lane0:lane0 + 1])).astype(BF16)
                    dq = dq + _dot(dsb, jnp.where(mask, kc, jnp.zeros_like(kc)))
                    dk = dk + _dot_tn(dsb, qh)
                    dv = dv + _dot_tn(p.astype(BF16), doh)
                acc.at[0][qrows, :] += dq * ATT_SCALE
                acc.at[1][krows, :] += dk
                acc.at[2][krows, :] += dv

            _for_each_block(dil, unit)
        for k in range(3):
            dz_ref[:, k * LANES:(k + 1) * LANES] = acc[k].astype(BF16)

    blk = pl.BlockSpec((s, LANES), lambda b, hp: (b, hp))
    pair = pl.BlockSpec((s, PAIR_WIDTH), lambda b, hp: (b, hp))
    return pl.pallas_call(
        body, grid=(n_batch, N_PAIRS),
        in_specs=_pair_specs(s) + [blk] * 5,
        out_specs=pair,
        out_shape=jax.ShapeDtypeStruct((t, 2 * 3 * MIX_HALF), BF16),
        scratch_shapes=[pltpu.VMEM((3, s, LANES), F32), pltpu.VMEM((s, LANES), F32), pltpu.VMEM((s, LANES), F32),
                        pltpu.VMEM((3, s, LANES), F32), pltpu.VMEM((6, BLOCK, 2 * BLOCK), F32)],
        name=name, compiler_params=_params("parallel", "arbitrary"),
    )(zd, zd, zd, dy, ya, *lses)


X_BQ = 512


def _xattn_probs(q, k):
    sc = _dot_nt(q, k) * X_SCALE
    pe = jnp.exp(sc - jnp.max(sc, axis=1, keepdims=True))
    return pe / jnp.sum(pe, axis=1, keepdims=True)


def _xattn_fwd(qx, kx, vx, n_batch, name):
    nq = SEQ // X_BQ

    def body(q_ref, k_ref, v_ref, o_ref):
        p = _xattn_probs(q_ref[...], k_ref[...])
        o_ref[...] = _dot(p.astype(BF16), v_ref[...]).astype(BF16)

    qblk = pl.BlockSpec((X_BQ, X_HEAD_DIM), lambda b, h, i: (b * nq + i, h))
    kblk = pl.BlockSpec((N_MEM, X_HEAD_DIM), lambda b, h, i: (b, h))
    return pl.pallas_call(
        body, grid=(n_batch, X_HEADS, nq), in_specs=[qblk, kblk, kblk], out_specs=qblk,
        out_shape=jax.ShapeDtypeStruct(qx.shape, BF16),
        name=name, compiler_params=_params("parallel", "parallel", "arbitrary"),
    )(qx, kx, vx)


def _xattn_bwd(qx, kx, vx, dox, n_batch, name):
    nq = SEQ // X_BQ

    def body(q_ref, k_ref, v_ref, do_ref, dq_ref, dk_ref, dv_ref, dk_acc, dv_acc):
        i = pl.program_id(2)

        @pl.when(i == 0)
        def _():
            dk_acc[...] = jnp.zeros_like(dk_acc)
            dv_acc[...] = jnp.zeros_like(dv_acc)

        q, k, do = q_ref[...], k_ref[...], do_ref[...]
        p = _xattn_probs(q, k)
        dp = _dot_nt(do, v_ref[...])
        dsb = (p * (dp - jnp.sum(p * dp, axis=1, keepdims=True))).astype(BF16)
        dq_ref[...] = (_dot(dsb, k) * X_SCALE).astype(BF16)
        dk_acc[...] += _dot_tn(dsb, q) * X_SCALE
        dv_acc[...] += _dot_tn(p.astype(BF16), do)

        @pl.when(i == nq - 1)
        def _():
            dk_ref[...] = dk_acc[...].astype(BF16)
            dv_ref[...] = dv_acc[...].astype(BF16)

    qblk = pl.BlockSpec((X_BQ, X_HEAD_DIM), lambda b, h, i: (b * nq + i, h))
    kblk = pl.BlockSpec((N_MEM, X_HEAD_DIM), lambda b, h, i: (b, h))
    return pl.pallas_call(
        body, grid=(n_batch, X_HEADS, nq), in_specs=[qblk, kblk, kblk, qblk], out_specs=[qblk, kblk, kblk],
        out_shape=[jax.ShapeDtypeStruct(qx.shape, BF16), jax.ShapeDtypeStruct(kx.shape, BF16), jax.ShapeDtypeStruct(kx.shape, BF16)],
        scratch_shapes=[pltpu.VMEM((N_MEM, X_HEAD_DIM), F32)] * 2,
        name=name, compiler_params=_params("parallel", "parallel", "arbitrary"),
    )(qx, kx, vx, dox)


def _adamw(w, g, m, v, name, rows):
    r, c = w.shape
    assert r % rows == 0, (name, w.shape, rows)

    def body(w_ref, g_ref, m_ref, v_ref, d_ref, nm_ref, nv_ref):
        gv = g_ref[...]
        m1 = ADAM_B1 * m_ref[...] + (1.0 - ADAM_B1) * gv
        v1 = ADAM_B2 * v_ref[...] + (1.0 - ADAM_B2) * jnp.square(gv)
        m_hat = m1 / (1.0 - ADAM_B1 ** ADAM_STEP)
        v_hat = v1 / (1.0 - ADAM_B2 ** ADAM_STEP)
        d_ref[...] = -ADAM_LR * (m_hat / (jnp.sqrt(v_hat) + ADAM_EPS) + ADAM_WD * w_ref[...])
        nm_ref[...] = m1
        nv_ref[...] = v1

    blk = pl.BlockSpec((rows, c), lambda i: (i, 0))
    return pl.pallas_call(
        body, grid=(r // rows,), in_specs=[blk] * 4, out_specs=[blk] * 3,
        out_shape=[jax.ShapeDtypeStruct((r, c), F32)] * 3,
        name=name, compiler_params=_params("arbitrary"),
    )(w, g, m, v)


def _relu2(acc):
    a = jnp.maximum(acc, 0.0)
    return acc, a * a


def _relu2_bwd(acc, u):
    return (2.0 * jnp.maximum(u.astype(F32), 0.0) * acc,)


def _local_step(x, mem, target, vecs, w_in, late_weights, on_grads=None):
    n_batch = x.shape[0]
    t = n_batch * SEQ
    x0 = x.reshape(t, D_MODEL)
    mem2 = mem.reshape(n_batch * N_MEM, D_MODEL)
    tgt = target.reshape(t, D_MODEL)

    half = 3 * MIX_HALF
    w_dil, w_fox = _pair_major(w_in[:, :half]), _pair_major(w_in[:, half:QKV_WIDTH])
    w_gate = jnp.pad(w_in[:, QKV_WIDTH:], ((0, 0), (0, GATE_PAD - N_HEADS)))
    b_pad = jnp.pad(vecs["b_forget"], (0, GATE_PAD - N_HEADS)).reshape(1, GATE_PAD)

    h1 = _rmsnorm(x0, vecs["g_mix"], "norm_mix")
    mn = _rmsnorm(mem2, vecs["g_mem"], "norm_mem")
    zd = _matmul(h1, w_dil, "in_dil", tn=768)[0]
    zf = _matmul(h1, w_fox, "in_fox", out_dtypes=(BF16,), tn=768)[0]
    gate = _matmul(h1, w_gate, "in_gate")[0]
    c_bc, c_row, sg = _gate_fwd(gate, b_pad, n_batch, "gate_fwd")
    ya, lses = _dil_fwd(zd, n_batch, "dil_fwd")
    yf, of32, lse_f = _fox_fwd(zf, c_bc, c_row, n_batch, "fox_fwd")
    wts = late_weights(yf)
    w_out = wts["w_out"]
    x1, h2 = _matmul_res_norm([ya, yf], [w_out[:MIX_HALF], w_out[MIX_HALF:]], x0, vecs["g_xattn"], "out")
    qx = _matmul(h2, wts["w_xq"], "xq", out_dtypes=(BF16,))[0]
    kx = _matmul(mn, wts["w_xk"], "xk", out_dtypes=(BF16,))[0]
    vx = _matmul(mn, wts["w_xv"], "xv", out_dtypes=(BF16,))[0]
    ox = _xattn_fwd(qx, kx, vx, n_batch, "xattn_fwd")
    x2, h3 = _matmul_res_norm([ox], [wts["w_xo"]], x1, vecs["g_mlp"], "xo")
    u, a2 = _matmul(h3, wts["w_up"], "mlp_up", out_dtypes=(BF16, BF16), epilogue=_relu2, tn=1024)
    x3 = _matmul_res(a2, wts["w_down"], x2, "mlp_down")
    loss, dx3, dx3b, dg_final = _loss_bwd(x3, vecs["g_final"], tgt, "loss")

    du = _matmul(dx3b, wts["w_down"], "mlp_down_bwd", out_dtypes=(BF16,), extras=(u,), epilogue=_relu2_bwd, tn=1024, w_t=True)[0]
    shards = (N_CHIPS, 2 * D_MODEL, D_MODEL)
    g_mlp = _matmul_tn(h3, du, "gw_up", packed=(shards, lambda i, j: (j, 0, 0), None))
    g_mlp = _matmul_tn(a2, dx3b, "gw_down", packed=(shards, lambda i, j: (i, 1, 0), g_mlp))
    gw_up = g_mlp[:, :D_MODEL].transpose(1, 0, 2).reshape(D_MODEL, D_FF)
    gw_down = g_mlp[:, D_MODEL:].reshape(D_FF, D_MODEL)
    token = on_grads("mlp", g_mlp) if on_grads else None
    dx2, dx2b, dg_mlp = _matmul_rms_bwd([du], [wts["w_up"]], x2, vecs["g_mlp"], dx3, "mlp_up_bwd", after=token)

    gw_xo = _matmul_tn(ox, dx2b, "gw_xo")
    dox = _matmul(dx2b, wts["w_xo"], "xo_bwd", out_dtypes=(BF16,), w_t=True)[0]
    dqx, dkx, dvx = _xattn_bwd(qx, kx, vx, dox, n_batch, "xattn_bwd")
    gw_xq = _matmul_tn(h2, dqx, "gw_xq")
    gw_xk = _matmul_tn(mn, dkx, "gw_xk")
    gw_xv = _matmul_tn(mn, dvx, "gw_xv")
    dmn = _matmul(dkx, wts["w_xk"], "xk_bwd", w_t=True)[0]
    dmn = _matmul_res(dvx, wts["w_xv"], dmn, "xv_bwd", w_t=True)
    _, _, dg_mem = _rms_bwd(mem2, dmn, vecs["g_mem"], None, "norm_mem_bwd")
    dx1, dx1b, dg_xattn = _matmul_rms_bwd([dqx], [wts["w_xq"]], x1, vecs["g_xattn"], dx2, "xq_bwd")

    gw_out = jnp.concatenate([_matmul_tn(ya, dx1b, "gw_out_a"), _matmul_tn(yf, dx1b, "gw_out_f")], axis=0)
    token = on_grads("mid", dict(w_out=gw_out, w_xq=gw_xq, w_xk=gw_xk, w_xv=gw_xv, w_xo=gw_xo)) if on_grads else None
    dy = _matmul(dx1b, w_out, "out_bwd", out_dtypes=(BF16,), w_t=True, after=token)[0]
    dz = _dil_bwd(zd, dy, ya, lses, n_batch, "dil_bwd")
    dz, dc = _fox_bwd(zf, of32, dy, lse_f, c_bc, c_row, dz, n_batch, "fox_bwd")
    dzg, db = _gate_bwd(dc.reshape(n_batch, N_HEADS, SEQ), sg, "gate_bwd")
    gw_pm = _matmul_tn(h1, dz, "gw_in_qkv")
    gw_in = jnp.concatenate([_pair_major_inv(gw_pm[:, :half]), _pair_major_inv(gw_pm[:, half:]),
                             _matmul_tn(h1, dzg, "gw_in_gate")[:, :N_HEADS]], axis=1)
    dx0, _, dg_mix = _matmul_rms_bwd([dz, dzg], [jnp.concatenate([w_dil, w_fox], axis=1), w_gate], x0, vecs["g_mix"], dx1, "in_bwd")

    gw = dict(w_in=gw_in, w_out=gw_out, w_xq=gw_xq, w_xk=gw_xk, w_xv=gw_xv, w_xo=gw_xo, w_up=gw_up, w_down=gw_down)
    gv = dict(g_mix=dg_mix, g_xattn=dg_xattn, g_mem=dg_mem, g_mlp=dg_mlp, g_final=dg_final, b_forget=db)
    return loss, dx0.reshape(x.shape), gw, gv


MESH = pl.DeviceIdType.MESH
ANY = pl.BlockSpec(memory_space=pl.ANY)


def _place():
    x, y, c = lax.axis_index("x"), lax.axis_index("y"), lax.axis_index("c")
    other_chips = [(1 - x, y), (x, 1 - y), (1 - x, 1 - y)]
    return x, y, c, other_chips


def _my_chip():
    return 2 * lax.axis_index("x") + lax.axis_index("y")


def _halves(rows, c, align):
    half = rows // 2
    assert rows % (2 * align) == 0, rows
    return pl.ds(pl.multiple_of(c * half, align), half), pl.ds(pl.multiple_of((1 - c) * half, align), half)


def _place_own(wall, pack):
    return lax.dynamic_update_slice(wall, pack[None], (_my_chip(), 0, 0))


def _gather(pack, name, after):
    def body(p_ref, after_ref, out_ref, send_sems, recv_sems, pass_send, pass_recv):
        del after_ref
        x, y, c, chips = _place()
        me = 2 * x + y
        mine, theirs = _halves(pack.shape[0], c, 16)

        def from_chip(k, chip, rows):
            src = out_ref.at[2 * chip[0] + chip[1], rows]
            return pltpu.make_async_remote_copy(src_ref=src, dst_ref=src, send_sem=send_sems.at[k], recv_sem=recv_sems.at[k],
                                                device_id=(chip[0], chip[1], c), device_id_type=MESH)

        def passed(k, chip, rows):
            src = out_ref.at[2 * chip[0] + chip[1], rows]
            return pltpu.make_async_remote_copy(src_ref=src, dst_ref=src, send_sem=pass_send.at[k], recv_sem=pass_recv.at[k],
                                                device_id=(x, y, 1 - c), device_id_type=MESH)

        sends = []
        for k, chip in enumerate(chips):
            cp = pltpu.make_async_remote_copy(src_ref=p_ref.at[mine], dst_ref=out_ref.at[me, mine], send_sem=send_sems.at[k],
                                              recv_sem=recv_sems.at[k], device_id=(chip[0], chip[1], c), device_id_type=MESH)
            cp.start()
            sends.append(cp)
        for k, chip in enumerate(chips):
            from_chip(k, chip, mine).wait_recv()
            cp = passed(k, chip, mine)
            cp.start()
            sends.append(cp)
        for k, chip in enumerate(chips):
            passed(k, chip, theirs).wait_recv()
        for cp in sends:
            cp.wait_send()

    wall = pl.pallas_call(
        body, in_specs=[ANY, ANY], out_specs=ANY,
        out_shape=jax.ShapeDtypeStruct((N_CHIPS,) + pack.shape, pack.dtype),
        scratch_shapes=[pltpu.SemaphoreType.DMA((3,))] * 4,
        name=name,
    )(pack, after)
    return _place_own(wall, pack)


HBM = pl.BlockSpec(memory_space=pltpu.HBM)
SEM = pl.BlockSpec(memory_space=pltpu.SEMAPHORE)
SPLIT_COPY = pltpu.CompilerParams(has_side_effects=pltpu.SideEffectType.DATAFLOW_SIDE_EFFECTING)


def _in_hbm(a):
    return pltpu.with_memory_space_constraint(a, pltpu.HBM)


def _start_call(start, src, land_shape, after, name):
    land = lax.empty(land_shape, src.dtype)

    def body(src_ref, land_ref, after_ref, send_sems, recv_sems, src_thru, land_thru, token):
        del after_ref, src_thru, land_thru
        start(src_ref, land_ref, send_sems, recv_sems)
        token[...] = jnp.zeros_like(token)

    return pl.pallas_call(
        body, name=name,
        out_shape=(pltpu.SemaphoreType.DMA((3,)), pltpu.SemaphoreType.DMA((3,)), pltpu.HBM(src.shape, src.dtype),
                   pltpu.HBM(land_shape, src.dtype), jax.ShapeDtypeStruct((8, LANES), F32)),
        in_specs=(HBM, HBM, ANY), out_specs=(SEM, SEM, HBM, HBM, pl.BlockSpec(memory_space=pltpu.VMEM)),
        input_output_aliases={0: 2, 1: 3}, compiler_params=SPLIT_COPY,
    )(_in_hbm(src), _in_hbm(land), after)


def _wait_call(body, started, after, name):
    send_sems, recv_sems, src, land, _ = started
    return pl.pallas_call(
        body, name=name,
        out_shape=(pltpu.HBM(src.shape, src.dtype), pltpu.HBM(land.shape, land.dtype)),
        in_specs=(HBM, HBM, SEM, SEM, ANY), out_specs=(HBM, HBM),
        input_output_aliases={0: 0, 1: 1}, compiler_params=SPLIT_COPY,
    )(src, land, send_sems, recv_sems, after)[1]


def _gather_copies(p_ref, wall_ref, send_sems, recv_sems):
    x, y, c, chips = _place()
    me = 2 * x + y
    mine, _ = _halves(p_ref.shape[0], c, 16)
    out, back = [], []
    for k, chip in enumerate(chips):
        peer = dict(send_sem=send_sems.at[k], recv_sem=recv_sems.at[k], device_id=(chip[0], chip[1], c), device_id_type=MESH)
        out.append(pltpu.make_async_remote_copy(src_ref=p_ref.at[mine], dst_ref=wall_ref.at[me, mine], **peer))
        slab = wall_ref.at[2 * chip[0] + chip[1], mine]
        back.append(pltpu.make_async_remote_copy(src_ref=slab, dst_ref=slab, **peer))
    return out, back


def _gather_start(pack, after, name):
    def start(p_ref, wall_ref, send_sems, recv_sems):
        for cp in _gather_copies(p_ref, wall_ref, send_sems, recv_sems)[0]:
            cp.start()

    return _start_call(start, pack, (N_CHIPS,) + pack.shape, after, name)


def _gather_wait(started, after, name):
    def body(p_ref, wall_ref, send_sems, recv_sems, after_ref, p_dead, wall_out):
        del after_ref, p_dead, wall_out
        out, back = _gather_copies(p_ref, wall_ref, send_sems, recv_sems)
        for cp_out, cp_back in zip(out, back):
            cp_out.wait_send()
            cp_back.wait_recv()

    return _wait_call(body, started, after, name)


def _pass_on(wall, name):
    def body(w_in_ref, out_ref, send_sems, recv_sems):
        del w_in_ref
        x, y, c, chips = _place()
        mine, theirs = _halves(wall.shape[1], c, 16)
        sends = []
        for k, chip in enumerate(chips):
            slab = out_ref.at[2 * chip[0] + chip[1]]
            peer = dict(send_sem=send_sems.at[k], recv_sem=recv_sems.at[k], device_id=(x, y, 1 - c), device_id_type=MESH)
            cp = pltpu.make_async_remote_copy(src_ref=slab.at[mine], dst_ref=slab.at[mine], **peer)
            cp.start()
            sends.append((cp, pltpu.make_async_remote_copy(src_ref=slab.at[theirs], dst_ref=slab.at[theirs], **peer)))
        for cp, back in sends:
            back.wait_recv()
            cp.wait_send()

    return pl.pallas_call(
        body, in_specs=[ANY], out_specs=ANY, out_shape=jax.ShapeDtypeStruct(wall.shape, wall.dtype),
        scratch_shapes=[pltpu.SemaphoreType.DMA((3,))] * 2, input_output_aliases={0: 0}, name=name,
    )(wall)


def _swap_halves(g, name):
    half = g.shape[1] // 2

    def body(g_ref, out_ref, send_sem, recv_sem):
        x, y, c, _ = _place()
        _, theirs = _halves(g.shape[1], c, 8)
        cp = pltpu.make_async_remote_copy(src_ref=g_ref.at[:, theirs], dst_ref=out_ref, send_sem=send_sem, recv_sem=recv_sem,
                                          device_id=(x, y, 1 - c), device_id_type=MESH)
        cp.start()
        cp.wait()

    return pl.pallas_call(
        body, in_specs=[ANY], out_specs=ANY,
        out_shape=jax.ShapeDtypeStruct((N_CHIPS, half, D_MODEL), F32),
        scratch_shapes=[pltpu.SemaphoreType.DMA, pltpu.SemaphoreType.DMA],
        name=name,
    )(g)


def _core_index():
    return lax.axis_index("c").astype(jnp.int32).reshape(1)


def _row_tile(half):
    tile = max(t for t in range(16, 1025, 16) if half % t == 0)
    return tile, half // tile


def _add_sibling(g, got, name):
    half = g.shape[1] // 2
    tile, n_tiles = _row_tile(half)

    def body(c_ref, g_ref, got_ref, o_ref):
        o_ref[...] = (g_ref[...] + got_ref[...]).astype(BF16)

    blk = pl.BlockSpec((None, tile, D_MODEL), lambda s, i, c_ref: (s, i, 0))
    return pl.pallas_call(
        body,
        grid_spec=pltpu.PrefetchScalarGridSpec(
            num_scalar_prefetch=1, grid=(N_CHIPS, n_tiles),
            in_specs=[pl.BlockSpec((None, tile, D_MODEL), lambda s, i, c_ref: (s, c_ref[0] * n_tiles + i, 0)), blk],
            out_specs=blk),
        out_shape=jax.ShapeDtypeStruct((N_CHIPS, half, D_MODEL), BF16),
        name=name, compiler_params=_params("arbitrary", "arbitrary"),
    )(_core_index(), g, got)


def _exchange_copies(p_ref, land_ref, send_sems, recv_sems):
    x, y, c, chips = _place()
    me = 2 * x + y
    out, back = [], []
    for k, chip in enumerate(chips):
        peer = dict(send_sem=send_sems.at[k], recv_sem=recv_sems.at[k], device_id=(chip[0], chip[1], c), device_id_type=MESH)
        out.append(pltpu.make_async_remote_copy(src_ref=p_ref.at[2 * chip[0] + chip[1]], dst_ref=land_ref.at[me], **peer))
        slab = land_ref.at[2 * chip[0] + chip[1]]
        back.append(pltpu.make_async_remote_copy(src_ref=slab, dst_ref=slab, **peer))
    return out, back


def _with_own(got, part):
    me = _my_chip()
    return lax.dynamic_update_slice(got, lax.dynamic_slice(part, (me, 0, 0), (1,) + part.shape[1:]), (me, 0, 0))


def _exchange_chips(part, name):
    def body(p_ref, out_ref, send_sems, recv_sems):
        out, back = _exchange_copies(p_ref, out_ref, send_sems, recv_sems)
        for cp in out:
            cp.start()
        for cp in back:
            cp.wait_recv()
        for cp in out:
            cp.wait_send()

    got = pl.pallas_call(
        body, in_specs=[ANY], out_specs=ANY,
        out_shape=jax.ShapeDtypeStruct(part.shape, part.dtype),
        scratch_shapes=[pltpu.SemaphoreType.DMA((3,)), pltpu.SemaphoreType.DMA((3,))],
        name=name,
    )(part)
    return _with_own(got, part)


def _exchange_start(part, name):
    def start(p_ref, land_ref, send_sems, recv_sems):
        for cp in _exchange_copies(p_ref, land_ref, send_sems, recv_sems)[0]:
            cp.start()

    return _start_call(start, part, part.shape, part, name)


def _exchange_wait(started, after, name):
    def body(p_ref, land_ref, send_sems, recv_sems, after_ref, p_dead, land_out):
        del after_ref, p_dead, land_out
        out, back = _exchange_copies(p_ref, land_ref, send_sems, recv_sems)
        for cp_out, cp_back in zip(out, back):
            cp_out.wait_send()
            cp_back.wait_recv()

    return _with_own(_wait_call(body, started, after, name), started[2])


def _sum_chips(parts, name):
    half = parts.shape[1]
    tile, n_tiles = _row_tile(half)

    def body(c_ref, p0, p1, p2, p3, o_ref):
        f32 = lambda p: p[...].astype(F32)
        o_ref[...] = ((f32(p0) + f32(p1)) + f32(p2)) + f32(p3)

    def slab(s):
        return pl.BlockSpec((None, tile, D_MODEL), lambda i, c_ref, s=s: (s, i, 0))

    return pl.pallas_call(
        body,
        grid_spec=pltpu.PrefetchScalarGridSpec(
            num_scalar_prefetch=1, grid=(n_tiles,),
            in_specs=[slab(s) for s in range(N_CHIPS)],
            out_specs=pl.BlockSpec((None, tile, D_MODEL), lambda i, c_ref: (c_ref[0], i, 0))),
        out_shape=jax.ShapeDtypeStruct((2, half, D_MODEL), F32),
        name=name, compiler_params=_params("arbitrary"),
    )(_core_index(), parts, parts, parts, parts)


def _share_halves(halves, name):
    def body(h_ref, out_ref, send_sem, recv_sem):
        del h_ref
        x, y, c, _ = _place()
        cp = pltpu.make_async_remote_copy(src_ref=out_ref.at[c], dst_ref=out_ref.at[c], send_sem=send_sem, recv_sem=recv_sem,
                                          device_id=(x, y, 1 - c), device_id_type=MESH)
        cp.start()
        pltpu.make_async_remote_copy(src_ref=out_ref.at[1 - c], dst_ref=out_ref.at[1 - c], send_sem=send_sem, recv_sem=recv_sem,
                                     device_id=(x, y, 1 - c), device_id_type=MESH).wait_recv()
        cp.wait_send()

    return pl.pallas_call(
        body, in_specs=[ANY], out_specs=ANY,
        out_shape=jax.ShapeDtypeStruct(halves.shape, halves.dtype),
        scratch_shapes=[pltpu.SemaphoreType.DMA] * 2,
        input_output_aliases={0: 0},
        name=name,
    )(halves)


def _reduce_parts(g, tag):
    return _add_sibling(g, _swap_halves(g, "swap_" + tag), "add_" + tag)


def _reduce_finish(got, tag):
    halves = _share_halves(_sum_chips(got, "sum_" + tag), "share_" + tag)
    return halves.reshape(2 * halves.shape[1], D_MODEL)


SMALL_ROWS = 8


def _allreduce_small(v):
    def body(v_ref, out_ref, buf, send_sems, recv_sems):
        x, y, c, _ = _place()
        buf[4 * x + 2 * y + c] = v_ref[...]
        sends = []
        for k in range(1, N_DEV):
            px = 1 - x if k & 4 else x
            py = 1 - y if k & 2 else y
            pc = 1 - c if k & 1 else c
            cp = pltpu.make_async_remote_copy(src_ref=v_ref, dst_ref=buf.at[4 * x + 2 * y + c], send_sem=send_sems.at[k - 1],
                                              recv_sem=recv_sems.at[k - 1], device_id=(px, py, pc), device_id_type=MESH)
            cp.start()
            sends.append((cp, 4 * px + 2 * py + pc))
        for k, (cp, peer) in enumerate(sends):
            pltpu.make_async_remote_copy(src_ref=v_ref, dst_ref=buf.at[peer], send_sem=send_sems.at[k], recv_sem=recv_sems.at[k],
                                         device_id=(x, y, c), device_id_type=MESH).wait_recv()
        for cp, _ in sends:
            cp.wait_send()
        total = buf[0]
        for d in range(1, N_DEV):
            total = total + buf[d]
        out_ref[...] = total

    vmem = pl.BlockSpec(memory_space=pltpu.VMEM)
    return pl.pallas_call(
        body, in_specs=[vmem], out_specs=vmem,
        out_shape=jax.ShapeDtypeStruct(v.shape, v.dtype),
        scratch_shapes=[pltpu.VMEM((N_DEV,) + v.shape, v.dtype), pltpu.SemaphoreType.DMA((N_DEV - 1,)),
                        pltpu.SemaphoreType.DMA((N_DEV - 1,))],
        name="allreduce_small",
    )(v)


MATRICES = ("w_in", "w_out", "w_xq", "w_xk", "w_xv", "w_xo", "w_up", "w_down")
VECTORS = ("g_mix", "g_xattn", "g_mem", "g_mlp", "g_final", "b_forget")
WEIGHT_ORDER = ("g_mix", "w_in", "b_forget", "w_out", "g_xattn", "g_mem", "w_xq", "w_xk", "w_xv", "w_xo",
                "g_mlp", "w_up", "w_down", "g_final")
GROUPS = {"mlp": ("w_up", "w_down"), "mid": ("w_out", "w_xq", "w_xk", "w_xv", "w_xo"), "in": ("w_in",)}
LATE = GROUPS["mid"] + GROUPS["mlp"]
W_IN_SHARD = IN_WIDTH // N_CHIPS
SHARD_ROWS = {"w_in": W_IN_SHARD, "w_out": 256, "w_xq": 256, "w_xk": 256, "w_xv": 256, "w_xo": 256, "w_up": 1024, "w_down": 1024}
PACK_ROWS = {n: -(-r // 32) * 32 for n, r in SHARD_ROWS.items()}
ADAM_ROWS = 128


def _pack(parts, names):
    return jnp.concatenate([jnp.pad(parts[n], ((0, PACK_ROWS[n] - SHARD_ROWS[n]), (0, 0))) for n in names], axis=0)


def _unpack(a, names):
    out, pos = {}, 0
    for n in names:
        out[n] = a[..., pos:pos + SHARD_ROWS[n], :]
        pos += PACK_ROWS[n]
    return out


def _full_weights(wall, names):
    cols = lambda a: a.transpose(1, 0, 2).reshape(a.shape[1], -1)
    rows = lambda a: a.reshape(-1, a.shape[-1])
    out = {}
    for n, a in _unpack(wall, names).items():
        if n == "w_in":
            out[n] = cols(a.reshape(N_CHIPS, D_MODEL, W_IN_SHARD))
        else:
            out[n] = cols(a) if n == "w_up" else rows(a)
    return out


def _shard_of(g, name, s):
    if name == "w_in":
        return g[:, s * W_IN_SHARD:(s + 1) * W_IN_SHARD].reshape(W_IN_SHARD, D_MODEL)
    if name == "w_up":
        return g[:, s * D_MODEL:(s + 1) * D_MODEL]
    n = SHARD_ROWS[name]
    return g[s * n:(s + 1) * n]


def _pack_grads(gws, names):
    return jnp.stack([_pack({n: _shard_of(gws[n], n, s) for n in names}, names) for s in range(N_CHIPS)])


def kernel(x, mem, g_mix, w_in, b_forget, w_out, g_xattn, g_mem, w_xq, w_xk, w_xv, w_xo, g_mlp, w_up, w_down, g_final, loss_target, m_g_mix, m_w_in, m_b_forget, m_w_out, m_g_xattn, m_g_mem, m_w_xq, m_w_xk, m_w_xv, m_w_xo, m_g_mlp, m_w_up, m_w_down, m_g_final, v_g_mix, v_w_in, v_b_forget, v_w_out, v_g_xattn, v_g_mem, v_w_xq, v_w_xk, v_w_xv, v_w_xo, v_g_mlp, v_w_up, v_w_down, v_g_final):
    given = dict(locals())
    weights = {n: given[n] for n in WEIGHT_ORDER}
    vecs = {n: weights[n] for n in VECTORS}

    shard = {n: weights[n].astype(BF16) for n in MATRICES}
    shard["w_in"] = shard["w_in"].reshape(W_IN_SHARD, D_MODEL)
    in_pack, late_pack = _pack(shard, GROUPS["in"]), _pack(shard, LATE)
    in_wall = _gather(in_pack, "gather_in", in_pack)
    late = _gather_start(late_pack, in_wall, "gather_late_start")
    w_in_full = _full_weights(in_wall, GROUPS["in"])["w_in"]

    def late_weights(after):
        wall = _pass_on(_gather_wait(late, after, "gather_late_wait"), "gather_late_pass")
        return _full_weights(_place_own(wall, late_pack), LATE)

    started = {}

    def on_grads(group, gws):
        packed = gws if group == "mlp" else _pack_grads(gws, GROUPS[group])
        part = _reduce_parts(packed, group)
        started[group] = _exchange_start(part, "exchange_%s_start" % group)
        return started[group][4]

    loss, grad_x, gw, gv = _local_step(x, mem, loss_target, vecs, w_in_full, late_weights, on_grads)

    part = _reduce_parts(_pack_grads(gw, GROUPS["in"]), "in")
    reduced = {"in": _reduce_finish(_exchange_chips(part, "exchange_in"), "in")}
    for group in ("mlp", "mid"):
        reduced[group] = _reduce_finish(_exchange_wait(started[group], grad_x, "exchange_%s_wait" % group), group)
    grads = {}
    for group, names in GROUPS.items():
        for n, a in _unpack(reduced[group], names).items():
            grads[n] = a.reshape(weights[n].shape)

    row = lambda a: jnp.pad(a.reshape(-1), (0, D_MODEL - a.size)).reshape(1, D_MODEL)
    small = jnp.concatenate([gv[n] for n in VECTORS[:5]] + [row(gv["b_forget"][:, 0]), row(loss[0, :1]),
                             jnp.zeros((1, D_MODEL), F32)], axis=0)
    small = _allreduce_small(small)
    for k, n in enumerate(VECTORS[:5]):
        grads[n] = small[k]
    grads["b_forget"] = small[5, :N_HEADS]
    loss_total = small[6, 0]

    delta, new_m, new_v = {}, {}, {}
    for n in MATRICES:
        delta[n], new_m[n], new_v[n] = _adamw(weights[n], grads[n], given["m_" + n], given["v_" + n], "adamw_" + n, ADAM_ROWS)
    stack = lambda prefix: jnp.concatenate([row(given[prefix + n]) for n in VECTORS] + [jnp.zeros((2, D_MODEL), F32)], axis=0)
    g_small = jnp.concatenate([small[:6], jnp.zeros((2, D_MODEL), F32)], axis=0)
    d, m1, v1 = _adamw(stack(""), g_small, stack("m_"), stack("v_"), "adamw_vectors", SMALL_ROWS)
    for k, n in enumerate(VECTORS):
        width = weights[n].shape[0]
        delta[n], new_m[n], new_v[n] = d[k, :width], m1[k, :width], v1[k, :width]

    return (loss_total, grad_x, *[grads[n] for n in WEIGHT_ORDER], *[delta[n] for n in WEIGHT_ORDER],
            *[new_m[n] for n in WEIGHT_ORDER], *[new_v[n] for n in WEIGHT_ORDER])
```

```python
import functools
import math

import jax
import jax.numpy as jnp
from jax import lax
from jax.experimental import pallas as pl
from jax.experimental.pallas import tpu as pltpu

F32 = jnp.float32
BF16 = jnp.bfloat16

D_MODEL = 1024
SEQ = 2048
N_MEM = 256
HEAD_DIM = 64
N_HEADS = 8
MIX_HALF = N_HEADS * HEAD_DIM
QKV_WIDTH = 6 * MIX_HALF
IN_WIDTH = QKV_WIDTH + N_HEADS
GATE_PAD = 128
BLOCK = 128
DILATIONS = (1, 4, 16)
X_HEADS = 4
X_HEAD_DIM = 256
D_FF = 4096
EPS = 1e-6
NEG = -1e30
ATT_SCALE = 1.0 / math.sqrt(HEAD_DIM)
X_SCALE = 1.0 / math.sqrt(X_HEAD_DIM)
LANES = 128
N_CHIPS = 4
N_DEV = 8

ADAM_LR = 0.001
ADAM_B1 = 0.9
ADAM_B2 = 0.999
ADAM_EPS = 1e-08
ADAM_WD = 0.01
ADAM_STEP = 10

VMEM_LIMIT = 48 * 1024 * 1024


def _params(*sem):
    return pltpu.CompilerParams(dimension_semantics=sem or None, vmem_limit_bytes=VMEM_LIMIT)


def _dot(a, b):
    return jnp.dot(a, b, preferred_element_type=F32)


def _dot_nt(a, b):
    return lax.dot_general(a, b, (((1,), (1,)), ((), ())), preferred_element_type=F32)


def _dot_tn(a, b):
    return lax.dot_general(a, b, (((0,), (0,)), ((), ())), preferred_element_type=F32)


def _dot_exact(x, e):
    hi = x.astype(BF16)
    r1 = x - hi.astype(F32)
    mid = r1.astype(BF16)
    lo = (r1 - mid.astype(F32)).astype(BF16)
    return _dot(hi, e) + _dot(mid, e) + _dot(lo, e)


def _head_mask(e):
    lane = lax.broadcasted_iota(jnp.int32, (1, LANES), 1)
    return (lane >= HEAD_DIM * e) & (lane < HEAD_DIM * (e + 1))


def _matmul(a, w, name, out_dtypes=(F32,), extras=(), epilogue=None, tm=1024, tn=512, w_t=False, after=None):
    m, k = a.shape
    n = w.shape[0] if w_t else w.shape[1]
    tm, tn = min(tm, m), min(tn, n)
    assert m % tm == 0 and n % tn == 0, (name, a.shape, w.shape)
    n_ex = len(extras)
    order = () if after is None else (after,)

    def body(a_ref, w_ref, *rest):
        rest = rest[len(order):]
        acc = (_dot_nt if w_t else _dot)(a_ref[...], w_ref[...])
        res = (acc,) if epilogue is None else epilogue(acc, *[r[...] for r in rest[:n_ex]])
        for o_ref, r in zip(rest[n_ex:], res):
            o_ref[...] = r.astype(o_ref.dtype)

    tile = pl.BlockSpec((tm, tn), lambda i, j: (i, j))
    w_spec = pl.BlockSpec((tn, k), lambda i, j: (j, 0)) if w_t else pl.BlockSpec((k, tn), lambda i, j: (0, j))
    return pl.pallas_call(
        body, grid=(m // tm, n // tn),
        in_specs=[pl.BlockSpec((tm, k), lambda i, j: (i, 0)), w_spec] + [pl.BlockSpec(memory_space=pl.ANY)] * len(order) + [tile] * n_ex,
        out_specs=[tile] * len(out_dtypes),
        out_shape=[jax.ShapeDtypeStruct((m, n), dt) for dt in out_dtypes],
        name=name, compiler_params=_params("parallel", "arbitrary"),
    )(a, w, *order, *extras)


def _matmul_res(a, w, res, name, w_t=False):
    return _matmul(a, w, name, extras=(res,), epilogue=lambda acc, r: (r + acc,), w_t=w_t)[0]


def _matmul_tn(x, y, name, tm=1024, tn=1024, tk=512, packed=None):
    t, m = x.shape
    _, n = y.shape
    tm, tn, tk = min(tm, m), min(tn, n), min(tk, t)
    assert m % tm == 0 and n % tn == 0 and t % tk == 0, (name, x.shape, y.shape)
    shape, place, into = packed or ((m, n), None, None)

    def body(x_ref, y_ref, *rest):
        o_ref = rest[-1]

        @pl.when(pl.program_id(2) == 0)
        def _():
            o_ref[...] = jnp.zeros_like(o_ref)

        o_ref[...] += _dot_tn(x_ref[...], y_ref[...])

    out_spec = (pl.BlockSpec((tm, tn), lambda i, j, k: (i, j)) if place is None
                else pl.BlockSpec((None, tm, tn), lambda i, j, k: place(i, j)))
    return pl.pallas_call(
        body, grid=(m // tm, n // tn, t // tk),
        in_specs=[pl.BlockSpec((tk, tm), lambda i, j, k: (k, i)), pl.BlockSpec((tk, tn), lambda i, j, k: (k, j))]
        + ([] if into is None else [pl.BlockSpec(memory_space=pl.ANY)]),
        out_specs=out_spec, out_shape=jax.ShapeDtypeStruct(shape, F32),
        input_output_aliases={} if into is None else {2: 0},
        name=name, compiler_params=_params("parallel", "parallel", "arbitrary"),
    )(x, y, *(() if into is None else (into,)))


def _rmsnorm(x, g, name, tm=512):
    t, d = x.shape
    tm = min(tm, t)

    def body(x_ref, g_ref, h_ref):
        xv = x_ref[...]
        r = lax.rsqrt(jnp.mean(xv * xv, axis=-1, keepdims=True) + EPS)
        h_ref[...] = (xv * r * g_ref[...]).astype(BF16)

    return pl.pallas_call(
        body, grid=(t // tm,),
        in_specs=[pl.BlockSpec((tm, d), lambda i: (i, 0)), pl.BlockSpec((1, d), lambda i: (0, 0))],
        out_specs=pl.BlockSpec((tm, d), lambda i: (i, 0)),
        out_shape=jax.ShapeDtypeStruct((t, d), BF16),
        name=name, compiler_params=_params("arbitrary"),
    )(x, g.reshape(1, d))


def _rms_bwd_tile(xv, dh, g):
    d = xv.shape[-1]
    r = lax.rsqrt(jnp.mean(xv * xv, axis=-1, keepdims=True) + EPS)
    dyg = dh * g
    proj = jnp.sum(dyg * xv, axis=-1, keepdims=True)
    dx = r * dyg - xv * (r * r * r * (1.0 / d)) * proj
    return dx, dh * (xv * r)


def _rms_bwd(x, dh, g, dres, name, tm=512):
    t, d = x.shape
    tm = min(tm, t)
    has_res = dres is not None

    def body(x_ref, dh_ref, g_ref, *rest):
        if has_res:
            res_ref, dx_ref, dxb_ref, dg_ref = rest
        else:
            dx_ref, dxb_ref, dg_ref = rest
        dx, dg_rows = _rms_bwd_tile(x_ref[...], dh_ref[...], g_ref[...])
        if has_res:
            dx = res_ref[...] + dx
        dx_ref[...] = dx
        dxb_ref[...] = dx.astype(BF16)

        @pl.when(pl.program_id(0) == 0)
        def _():
            dg_ref[...] = jnp.zeros_like(dg_ref)

        dg_ref[...] += jnp.sum(dg_rows, axis=0, keepdims=True)

    row = pl.BlockSpec((tm, d), lambda i: (i, 0))
    vec = pl.BlockSpec((1, d), lambda i: (0, 0))
    return pl.pallas_call(
        body, grid=(t // tm,),
        in_specs=[row, row, vec] + ([row] if has_res else []),
        out_specs=[row, row, vec],
        out_shape=[jax.ShapeDtypeStruct((t, d), F32), jax.ShapeDtypeStruct((t, d), BF16), jax.ShapeDtypeStruct((1, d), F32)],
        name=name, compiler_params=_params("arbitrary"),
    )(x, dh, g.reshape(1, d), *((dres,) if has_res else ()))


def _row_dots(a_refs, w_refs, w_t):
    acc = None
    for a_ref, w_ref in zip(a_refs, w_refs):
        part = (_dot_nt if w_t else _dot)(a_ref[...], w_ref[...])
        acc = part if acc is None else acc + part
    return acc


def _row_specs(a_parts, w_parts, tm):
    specs = [pl.BlockSpec((tm, a.shape[1]), lambda i: (i, 0)) for a in a_parts]
    return specs + [pl.BlockSpec(w.shape, lambda i: (0, 0)) for w in w_parts]


def _matmul_res_norm(a_parts, w_parts, res, g, name, tm=512):
    t, d = res.shape
    n = len(a_parts)

    def body(*refs):
        res_ref, g_ref, x_ref, h_ref = refs[2 * n:]
        xv = res_ref[...] + _row_dots(refs[:n], refs[n:2 * n], False)
        x_ref[...] = xv
        r = lax.rsqrt(jnp.mean(xv * xv, axis=-1, keepdims=True) + EPS)
        h_ref[...] = (xv * r * g_ref[...]).astype(BF16)

    row = pl.BlockSpec((tm, d), lambda i: (i, 0))
    return pl.pallas_call(
        body, grid=(t // tm,),
        in_specs=_row_specs(a_parts, w_parts, tm) + [row, pl.BlockSpec((1, d), lambda i: (0, 0))],
        out_specs=[row, row],
        out_shape=[jax.ShapeDtypeStruct((t, d), F32), jax.ShapeDtypeStruct((t, d), BF16)],
        name=name, compiler_params=_params("arbitrary"),
    )(*a_parts, *w_parts, res, g.reshape(1, d))


def _matmul_rms_bwd(a_parts, w_parts, x, g, dres, name, tm=512, after=None):
    t, d = x.shape
    n = len(a_parts)
    order = () if after is None else (after,)

    def body(*refs):
        x_ref, g_ref, res_ref = refs[2 * n:2 * n + 3]
        dx_ref, dxb_ref, dg_ref = refs[2 * n + 3 + len(order):]
        dx, dg_rows = _rms_bwd_tile(x_ref[...], _row_dots(refs[:n], refs[n:2 * n], True), g_ref[...])
        dx = res_ref[...] + dx
        dx_ref[...] = dx
        dxb_ref[...] = dx.astype(BF16)

        @pl.when(pl.program_id(0) == 0)
        def _():
            dg_ref[...] = jnp.zeros_like(dg_ref)

        dg_ref[...] += jnp.sum(dg_rows, axis=0, keepdims=True)

    row = pl.BlockSpec((tm, d), lambda i: (i, 0))
    vec = pl.BlockSpec((1, d), lambda i: (0, 0))
    return pl.pallas_call(
        body, grid=(t // tm,),
        in_specs=_row_specs(a_parts, w_parts, tm) + [row, vec, row] + [pl.BlockSpec(memory_space=pl.ANY)] * len(order),
        out_specs=[row, row, vec],
        out_shape=[jax.ShapeDtypeStruct((t, d), F32), jax.ShapeDtypeStruct((t, d), BF16), jax.ShapeDtypeStruct((1, d), F32)],
        name=name, compiler_params=_params("arbitrary"),
    )(*a_parts, *w_parts, x, g.reshape(1, d), dres, *order)


def _loss_bwd(a, w, res, g, target, name, tm=512):
    t, d = res.shape

    def body(a_ref, w_ref, x_ref, g_ref, t_ref, loss_ref, dx_ref, dxb_ref, dg_ref):
        xv = x_ref[...] + _dot(a_ref[...], w_ref[...])
        gv = g_ref[...]
        r = lax.rsqrt(jnp.mean(xv * xv, axis=-1, keepdims=True) + EPS)
        err = xv * r * gv - t_ref[...]
        dx, dg_rows = _rms_bwd_tile(xv, err * (1.0 / d), gv)
        dx_ref[...] = dx
        dxb_ref[...] = dx.astype(BF16)

        @pl.when(pl.program_id(0) == 0)
        def _():
            dg_ref[...] = jnp.zeros_like(dg_ref)
            loss_ref[...] = jnp.zeros_like(loss_ref)

        dg_ref[...] += jnp.sum(dg_rows, axis=0, keepdims=True)
        part = jnp.sum(jnp.sum(err * err, axis=0, keepdims=True), axis=1, keepdims=True) * (0.5 / d)
        loss_ref[...] += jnp.broadcast_to(part, loss_ref.shape)

    row = pl.BlockSpec((tm, d), lambda i: (i, 0))
    vec = pl.BlockSpec((1, d), lambda i: (0, 0))
    return pl.pallas_call(
        body, grid=(t // tm,),
        in_specs=_row_specs([a], [w], tm) + [row, vec, row],
        out_specs=[pl.BlockSpec((1, LANES), lambda i: (0, 0)), row, row, vec],
        out_shape=[jax.ShapeDtypeStruct((1, LANES), F32), jax.ShapeDtypeStruct((t, d), F32),
                   jax.ShapeDtypeStruct((t, d), BF16), jax.ShapeDtypeStruct((1, d), F32)],
        name=name, compiler_params=_params("arbitrary"),
    )(a, w, res, g.reshape(1, d), target)


def _tri(upper):
    r = lax.broadcasted_iota(jnp.int32, (LANES, LANES), 0)
    c = lax.broadcasted_iota(jnp.int32, (LANES, LANES), 1)
    return jnp.where((r <= c) if upper else (r >= c), 1.0, 0.0).astype(BF16)


def _gate_fwd(gate, b_pad, n_batch, name):
    s = SEQ
    nblk = s // LANES

    def body(g_ref, b_ref, cbc_ref, crow_ref, sg_ref, ct_ref):
        gz = g_ref[...] + b_ref[...]
        logf = jnp.minimum(gz, 0.0) - jnp.log(1.0 + jnp.exp(-jnp.abs(gz)))
        logf_t = logf.T
        sg_ref[...] = (1.0 / (1.0 + jnp.exp(gz))).T[0:N_HEADS]
        upper = _tri(True)
        carry = jnp.zeros((LANES, 1), F32)
        for blk in range(nblk):
            seg = _dot_exact(logf_t[:, blk * LANES:(blk + 1) * LANES], upper) + carry
            carry = seg[:, LANES - 1:LANES]
            ct_ref[:, blk * LANES:(blk + 1) * LANES] = seg
        ct = ct_ref[...]
        crow_ref[...] = ct[0:N_HEADS]
        c_col = ct.T
        lane = lax.broadcasted_iota(jnp.int32, (1, MIX_HALF), 1)
        acc = jnp.zeros((s, MIX_HALF), F32)
        for h in range(N_HEADS):
            acc = jnp.where((lane >= HEAD_DIM * h) & (lane < HEAD_DIM * (h + 1)), c_col[:, h:h + 1], acc)
        cbc_ref[...] = acc

    return pl.pallas_call(
        body, grid=(n_batch,),
        in_specs=[pl.BlockSpec((s, GATE_PAD), lambda b: (b, 0)), pl.BlockSpec((1, GATE_PAD), lambda b: (0, 0))],
        out_specs=[pl.BlockSpec((s, MIX_HALF), lambda b: (b, 0)),
                   pl.BlockSpec((None, N_HEADS, s), lambda b: (b, 0, 0)),
                   pl.BlockSpec((None, N_HEADS, s), lambda b: (b, 0, 0))],
        out_shape=[jax.ShapeDtypeStruct((n_batch * s, MIX_HALF), F32),
                   jax.ShapeDtypeStruct((n_batch, N_HEADS, s), F32),
                   jax.ShapeDtypeStruct((n_batch, N_HEADS, s), F32)],
        scratch_shapes=[pltpu.VMEM((LANES, s), F32)],
        name=name, compiler_params=_params("arbitrary"),
    )(gate, b_pad)


def _gate_bwd(dc, sg, name):
    n_batch, _, s = dc.shape
    nblk = s // LANES

    def body(dc_ref, sg_ref, dz_ref, db_ref, dt_ref):
        lower = _tri(False)
        dcv = dc_ref[...]
        carry = jnp.zeros((N_HEADS, 1), F32)
        dt_ref[...] = jnp.zeros_like(dt_ref)
        for blk in reversed(range(nblk)):
            seg = _dot_exact(dcv[:, blk * LANES:(blk + 1) * LANES], lower) + carry
            carry = seg[:, 0:1]
            dt_ref[0:N_HEADS, blk * LANES:(blk + 1) * LANES] = seg * sg_ref[:, blk * LANES:(blk + 1) * LANES]
        dg_t = dt_ref[...]
        dz_ref[...] = dg_t.T.astype(BF16)

        @pl.when(pl.program_id(0) == 0)
        def _():
            db_ref[...] = jnp.zeros_like(db_ref)

        db_ref[...] += jnp.broadcast_to(jnp.sum(dg_t[0:N_HEADS], axis=1, keepdims=True), db_ref.shape)

    return pl.pallas_call(
        body, grid=(n_batch,),
        in_specs=[pl.BlockSpec((None, N_HEADS, s), lambda b: (b, 0, 0)), pl.BlockSpec((None, N_HEADS, s), lambda b: (b, 0, 0))],
        out_specs=[pl.BlockSpec((s, GATE_PAD), lambda b: (b, 0)), pl.BlockSpec((N_HEADS, LANES), lambda b: (0, 0))],
        out_shape=[jax.ShapeDtypeStruct((n_batch * s, GATE_PAD), BF16), jax.ShapeDtypeStruct((N_HEADS, LANES), F32)],
        scratch_shapes=[pltpu.VMEM((LANES, s), F32)],
        name=name, compiler_params=_params("arbitrary"),
    )(dc, sg)


FOX_BQ = 512
FOX_BK = 512
FOX_STRIP = 512
PAIR_WIDTH = 3 * LANES
N_PAIRS = N_HEADS // 2


def _pair_major(w):
    return w.reshape(w.shape[0], 3, N_PAIRS, LANES).transpose(0, 2, 1, 3).reshape(w.shape[0], 3 * MIX_HALF)


def _pair_major_inv(w):
    return w.reshape(w.shape[0], N_PAIRS, 3, LANES).transpose(0, 2, 1, 3).reshape(w.shape[0], 3 * MIX_HALF)


def _causal(i, j, bq, bk):
    qpos = i * bq + lax.broadcasted_iota(jnp.int32, (bq, 1), 0)
    kpos = j * bk + lax.broadcasted_iota(jnp.int32, (1, bk), 1)
    return kpos <= qpos


def _split_bf16(p):
    hi = p.astype(BF16)
    return hi, (p - hi.astype(F32)).astype(BF16)


def _fox_fwd(zf, c_bc, c_row, n_batch, name):
    s, bq, bk = SEQ, FOX_BQ, FOX_BK
    nq = s // bq
    t = n_batch * s

    n_strip = bq // FOX_STRIP

    def body(q_ref, k_ref, v_ref, cq_ref, cr_ref, o_ref, o32_ref, lse_ref):
        hp, i = pl.program_id(1), pl.program_id(2)
        strips = [slice(r * FOX_STRIP, (r + 1) * FOX_STRIP) for r in range(n_strip)]
        chains = [(e, r) for e in range(2) for r in range(n_strip)]
        qh, cq = {}, {}
        for e, r in chains:
            q = q_ref[strips[r], :] * ATT_SCALE
            qh[e, r] = jnp.where(_head_mask(e), q, jnp.zeros_like(q))
            cq[e, r] = cq_ref[strips[r], HEAD_DIM * e:HEAD_DIM * e + 1]

        def step(j, carry, masked):
            rows = pl.ds(pl.multiple_of(j * bk, bk), bk)
            kj, vj = k_ref[rows, :], v_ref[rows, :]
            ck = [cr_ref[pl.ds(2 * hp + e, 1), rows] for e in range(2)]
            out = []
            scores = [_dot_nt(qh[e, r], kj) for e, r in chains]
            for n, (e, r) in enumerate(chains):
                m, l, acc = carry[3 * n:3 * n + 3]
                sc = scores[n] + (cq[e, r] - ck[e])
                if masked:
                    qpos = i * bq + r * FOX_STRIP + lax.broadcasted_iota(jnp.int32, (FOX_STRIP, 1), 0)
                    kpos = j * bk + lax.broadcasted_iota(jnp.int32, (1, bk), 1)
                    sc = jnp.where(kpos <= qpos, sc, NEG)
                m_new = jnp.maximum(m, jnp.max(sc, axis=1, keepdims=True))
                alpha = jnp.exp(m - m_new)
                p = jnp.exp(sc - m_new)
                p_hi, p_lo = _split_bf16(p)
                out += [m_new, alpha * l + jnp.sum(p, axis=1, keepdims=True), alpha * acc + (_dot(p_hi, vj) + _dot(p_lo, vj))]
            return tuple(out)

        init = (jnp.full((FOX_STRIP, 1), NEG, F32), jnp.zeros((FOX_STRIP, 1), F32), jnp.zeros((FOX_STRIP, LANES), F32)) * len(chains)
        n_clear = (i * bq) // bk
        carry = lax.fori_loop(0, n_clear, functools.partial(step, masked=False), init)
        carry = lax.fori_loop(n_clear, (i * bq + bq + bk - 1) // bk, functools.partial(step, masked=True), carry)
        for r in range(n_strip):
            outs = [carry[3 * (e * n_strip + r) + 2] / carry[3 * (e * n_strip + r) + 1] for e in range(2)]
            lses = [carry[3 * (e * n_strip + r)] + jnp.log(carry[3 * (e * n_strip + r) + 1]) for e in range(2)]
            o = jnp.where(_head_mask(0), outs[0], outs[1])
            o_ref[strips[r], :] = o.astype(BF16)
            o32_ref[strips[r], :] = o
            lse_ref[strips[r], :] = jnp.where(_head_mask(0), lses[0], lses[1])

    def col(c0):
        return lambda b, hp, i: (b, 3 * hp + c0)

    blk = pl.BlockSpec((bq, LANES), lambda b, hp, i: (b * nq + i, hp))
    return pl.pallas_call(
        body, grid=(n_batch, N_PAIRS, nq),
        in_specs=[pl.BlockSpec((bq, LANES), lambda b, hp, i: (b * nq + i, 3 * hp)),
                  pl.BlockSpec((s, LANES), col(1)), pl.BlockSpec((s, LANES), col(2)), blk,
                  pl.BlockSpec((None, N_HEADS, s), lambda b, hp, i: (b, 0, 0))],
        out_specs=[blk, blk, blk],
        out_shape=[jax.ShapeDtypeStruct((t, MIX_HALF), BF16), jax.ShapeDtypeStruct((t, MIX_HALF), F32),
                   jax.ShapeDtypeStruct((t, MIX_HALF), F32)],
        name=name, compiler_params=_params("parallel", "parallel", "arbitrary"),
    )(zf, zf, zf, c_bc, c_row)


def _fox_bwd(zf, o32, dy, lse, c_bc, c_row, dz, n_batch, name):
    s, bq, bk = SEQ, FOX_BQ, FOX_BK
    nq, nk = s // bq, s // bk

    def body(q_ref, k_ref, v_ref, o_ref, do_ref, lse_ref, cq_ref, cr_ref, dz_in, dz_ref, dc_ref, dq_acc):
        del dz_in
        hp, j = pl.program_id(1), pl.program_id(2)

        @pl.when(j == 0)
        def _():
            dq_acc[...] = jnp.zeros_like(dq_acc)

        kj, vj = k_ref[...], v_ref[...]
        cols = pl.ds(pl.multiple_of(j * bk, bk), bk)
        km = [jnp.where(_head_mask(e), kj, jnp.zeros_like(kj)) for e in range(2)]
        ck = [cr_ref[pl.ds(2 * hp + e, 1), cols] for e in range(2)]

        def step(i, carry, masked):
            rows = pl.ds(pl.multiple_of(i * bq, bq), bq)
            qi, doi = q_ref[rows, :] * ATT_SCALE, do_ref[rows, :]
            prod = doi.astype(F32) * o_ref[rows, :]
            out = []
            dq = jnp.zeros((bq, LANES), F32)
            for e in range(2):
                dk_a, dv_a, dc_a = carry[3 * e:3 * e + 3]
                mask = _head_mask(e)
                lane0 = HEAD_DIM * e
                dom = jnp.where(mask, doi, jnp.zeros_like(doi))
                delta = jnp.sum(jnp.where(mask, prod, 0.0), axis=1, keepdims=True)
                sc = _dot_nt(qi, km[e]) + (cq_ref[rows, lane0:lane0 + 1] - ck[e])
                if masked:
                    sc = jnp.where(_causal(i, j, bq, bk), sc, NEG)
                p = jnp.exp(sc - lse_ref[rows, lane0:lane0 + 1])
                ds = p * (_dot_nt(dom, vj) - delta)
                dsb = ds.astype(BF16)
                dq = dq + _dot(dsb, km[e])
                out += [dk_a + _dot_tn(dsb, qi), dv_a + _dot_tn(p.astype(BF16), dom), dc_a - jnp.sum(ds, axis=0, keepdims=True)]
            dq_acc[rows, :] += dq * ATT_SCALE
            return tuple(out)

        init = (jnp.zeros((bk, LANES), F32), jnp.zeros((bk, LANES), F32), jnp.zeros((1, bk), F32)) * 2
        first = (j * bk) // bq
        n_diag = (j * bk + bk + bq - 1) // bq
        carry = lax.fori_loop(first, n_diag, functools.partial(step, masked=True), init)
        carry = lax.fori_loop(n_diag, nq, functools.partial(step, masked=False), carry)
        for e in range(2):
            dc_ref[e:e + 1, :] = carry[3 * e + 2]
        dz_ref[cols, LANES:2 * LANES] = jnp.where(_head_mask(0), carry[0], carry[3]).astype(BF16)
        dz_ref[cols, 2 * LANES:3 * LANES] = (carry[1] + carry[4]).astype(BF16)

        @pl.when(j == nk - 1)
        def _():
            dz_ref[:, 0:LANES] = dq_acc[...].astype(BF16)

    def seq(idx):
        return pl.BlockSpec((s, LANES), lambda b, hp, j: (b, idx(hp)))

    def kblk(c0):
        return pl.BlockSpec((bk, LANES), lambda b, hp, j: (b * nk + j, 3 * hp + c0))

    return pl.pallas_call(
        body, grid=(n_batch, N_PAIRS, nk),
        in_specs=[seq(lambda hp: 3 * hp), kblk(1), kblk(2), seq(lambda hp: hp), seq(lambda hp: N_PAIRS + hp),
                  seq(lambda hp: hp), seq(lambda hp: hp),
                  pl.BlockSpec((None, N_HEADS, s), lambda b, hp, j: (b, 0, 0)), pl.BlockSpec(memory_space=pl.ANY)],
        out_specs=[pl.BlockSpec((s, PAIR_WIDTH), lambda b, hp, j: (b, N_PAIRS + hp)),
                   pl.BlockSpec((None, None, 2, bk), lambda b, hp, j: (b, hp, 0, j))],
        out_shape=[jax.ShapeDtypeStruct(dz.shape, dz.dtype), jax.ShapeDtypeStruct((n_batch, N_PAIRS, 2, s), F32)],
        scratch_shapes=[pltpu.VMEM((s, LANES), F32)],
        input_output_aliases={8: 0},
        name=name, compiler_params=_params("parallel", "parallel", "arbitrary"),
    )(zf, zf, zf, o32, dy, lse, c_bc, c_row, dz)


def _dil_bias(slope, dil):
    qi = lax.broadcasted_iota(jnp.int32, (BLOCK, 2 * BLOCK), 0)
    kj = lax.broadcasted_iota(jnp.int32, (BLOCK, 2 * BLOCK), 1)
    delta = qi + BLOCK - kj
    return jnp.where((delta >= 0) & (delta <= BLOCK), (-slope * dil) * delta.astype(F32), NEG)


def _alibi_slope(hp, e):
    slope = jnp.float32(0.0)
    for k in range(N_PAIRS):
        slope = jnp.where(hp == k, jnp.float32(2.0 ** -(2 * k + e + 1)), slope)
    return slope


def _first_block_bias(bias):
    return jnp.where(lax.broadcasted_iota(jnp.int32, bias.shape, 1) < BLOCK, NEG, bias)


def _fill_bias(bias_scr, hp):
    for di, dil in enumerate(DILATIONS):
        for e in range(2):
            bias_scr[2 * di + e] = _dil_bias(_alibi_slope(hp, e), dil)


def _pair_specs(rows):
    return [pl.BlockSpec((rows, LANES), lambda b, hp, c0=c0: (b, 3 * hp + c0)) for c0 in range(3)]


def _strided(start, size, dil):
    return pl.ds(start, size) if dil == 1 else pl.ds(start, size, stride=dil)


def _for_each_block(dil, unit):
    span = BLOCK * dil
    nb = SEQ // span
    if dil == 1:
        group = 3
        assert (nb - 1) % group == 0
        unit(0, True)

        def later(g, c):
            for u in range(group):
                unit((1 + g * group + u) * span, False)
            return c

        lax.fori_loop(0, (nb - 1) // group, later, 0)
        return
    group = 4
    per = dil // group

    def firsts(g, c):
        for u in range(group):
            unit(g * group + u, True)
        return c

    lax.fori_loop(0, per, firsts, 0)
    if nb > 1:
        def later(i, c):
            for u in range(group):
                unit((1 + i // per) * span + (i % per) * group + u, False)
            return c

        lax.fori_loop(0, (nb - 1) * per, later, 0)


QUARTER = SEQ // 4


def _to_quarters(src, dst):
    for r in range(4):
        dst[r * QUARTER:(r + 1) * QUARTER, :] = src[pl.ds(r, QUARTER, stride=4), :]


def _from_quarters(src, dst):
    for r in range(4):
        dst[pl.ds(r, QUARTER, stride=4), :] = src[r * QUARTER:(r + 1) * QUARTER, :]


def _for_each_quarter_block(dil, unit):
    stride = dil // 4
    nb = QUARTER // (BLOCK * stride)

    def firsts(r, c):
        for g in range(stride):
            unit(r * QUARTER + g, True, stride)
        return c

    if stride == 1:
        for r in range(4):
            firsts(r, 0)
    else:
        lax.fori_loop(0, 4, firsts, 0)
    if nb > 1:
        def later(i, c):
            for r in range(4):
                start = r * QUARTER + (1 + i // stride) * BLOCK * stride + i % stride
                unit(pl.multiple_of(start, BLOCK) if stride == 1 else start, False, stride)
            return c

        lax.fori_loop(0, (nb - 1) * stride, later, 0)


def _mix_weights(l1, l2, l3):
    m = jnp.maximum(jnp.maximum(l1, l2), l3)
    e1, e2, e3 = jnp.exp(l1 - m), jnp.exp(l2 - m), jnp.exp(l3 - m)
    inv = 1.0 / (e1 + e2 + e3)
    return e1 * inv, e2 * inv, e3 * inv


def _dil_fwd(zd, n_batch, name):
    s = SEQ
    t = n_batch * s

    def body(q_ref, k_ref, v_ref, y_ref, l1_ref, l2_ref, l3_ref, o_scr, qkv4, o4, l4, bias_scr):
        _fill_bias(bias_scr, pl.program_id(1))
        for a, ref in enumerate((q_ref, k_ref, v_ref)):
            _to_quarters(ref, qkv4.at[a])

        def unit(srcs, start, first, stride, di, o_dst, l_dst):
            qrows = _strided(start, BLOCK, stride)
            krows = qrows if first else _strided(start - BLOCK * stride, 2 * BLOCK, stride)
            q = (srcs[0][qrows, :] * ATT_SCALE).astype(BF16)
            kc = srcs[1][krows, :].astype(BF16)
            vc = srcs[2][krows, :].astype(BF16)
            if first:
                kc, vc = jnp.concatenate([kc, kc]), jnp.concatenate([vc, vc])
            outs, lses = [], []
            for e in range(2):
                bias = _first_block_bias(bias_scr[2 * di + e]) if first else bias_scr[2 * di + e]
                sc = _dot_nt(jnp.where(_head_mask(e), q, jnp.zeros_like(q)), kc) + bias
                m = jnp.max(sc, axis=1, keepdims=True)
                pe = jnp.exp(sc - m)
                l = jnp.sum(pe, axis=1, keepdims=True)
                outs.append(_dot((pe * (1.0 / l)).astype(BF16), vc))
                lses.append(m + jnp.log(l))
            o_dst[qrows, :] = jnp.where(_head_mask(0), outs[0], outs[1])
            l_dst[qrows, :] = jnp.where(_head_mask(0), lses[0], lses[1])

        token_order = (q_ref, k_ref, v_ref)
        quarters = tuple(qkv4.at[a] for a in range(3))
        _for_each_block(1, lambda start, first: unit(token_order, start, first, 1, 0, o_scr.at[0], l1_ref))
        for di in (1, 2):
            _for_each_quarter_block(DILATIONS[di], lambda start, first, stride, di=di: unit(
                quarters, start, first, stride, di, o4.at[di - 1], l4.at[di - 1]))
        for di, l_ref in ((1, l2_ref), (2, l3_ref)):
            _from_quarters(o4.at[di - 1], o_scr.at[di])
            _from_quarters(l4.at[di - 1], l_ref)
        w = _mix_weights(l1_ref[...], l2_ref[...], l3_ref[...])
        y_ref[...] = (w[0] * o_scr[0] + w[1] * o_scr[1] + w[2] * o_scr[2]).astype(BF16)

    blk = pl.BlockSpec((s, LANES), lambda b, hp: (b, hp))
    res = pl.pallas_call(
        body, grid=(n_batch, N_PAIRS),
        in_specs=_pair_specs(s),
        out_specs=[blk] * 4,
        out_shape=[jax.ShapeDtypeStruct((t, MIX_HALF), BF16)] + [jax.ShapeDtypeStruct((t, MIX_HALF), F32)] * 3,
        scratch_shapes=[pltpu.VMEM((3, s, LANES), F32), pltpu.VMEM((3, s, LANES), F32), pltpu.VMEM((2, s, LANES), F32),
                        pltpu.VMEM((2, s, LANES), F32), pltpu.VMEM((6, BLOCK, 2 * BLOCK), F32)],
        name=name, compiler_params=_params("parallel", "arbitrary"),
    )(zd, zd, zd)
    return res[0], res[1:]


def _dil_bwd(zd, dy, ya, lses, n_batch, name):
    s = SEQ
    t = n_batch * s

    def body(q_ref, k_ref, v_ref, dy_ref, ya_ref, l1_ref, l2_ref, l3_ref, dz_ref, w_scr, dy_scr, dot_scr, acc, bias_scr):
        _fill_bias(bias_scr, pl.program_id(1))
        for di, w in enumerate(_mix_weights(l1_ref[...], l2_ref[...], l3_ref[...])):
            w_scr[di] = w
        dya = dy_ref[...].astype(F32)
        prod = dya * ya_ref[...].astype(F32)
        per_head = [jnp.sum(jnp.where(_head_mask(e), prod, 0.0), axis=1, keepdims=True) for e in range(2)]
        dy_scr[...] = dya
        dot_scr[...] = jnp.where(_head_mask(0), per_head[0], per_head[1])
        acc[...] = jnp.zeros_like(acc)
        lse_refs = (l1_ref, l2_ref, l3_ref)
        for di, dil in enumerate(DILATIONS):

            def unit(start, first, di=di, dil=dil):
                qrows = _strided(start, BLOCK, dil)
                krows = qrows if first else _strided(start - BLOCK * dil, 2 * BLOCK, dil)
                q = (q_ref[qrows, :] * ATT_SCALE).astype(BF16)
                kc = k_ref[krows, :].astype(BF16)
                vc = v_ref[krows, :].astype(BF16)
                wq = w_scr.at[di][qrows, :]
                do = (wq * dy_scr[qrows, :]).astype(BF16)
                sub = wq * dot_scr[qrows, :]
                lse = lse_refs[di][qrows, :]
                dq = jnp.zeros((BLOCK, LANES), F32)
                dk = jnp.zeros((krows.size, LANES), F32)
                dv = jnp.zeros((krows.size, LANES), F32)
                for e in range(2):
                    mask = _head_mask(e)
                    lane0 = HEAD_DIM * e
                    qh = jnp.where(mask, q, jnp.zeros_like(q))
                    doh = jnp.where(mask, do, jnp.zeros_like(do))
                    bias = bias_scr[2 * di + e]
                    sc = _dot_nt(qh, kc) + (bias[:, BLOCK:] if first else bias)
                    p = jnp.exp(sc - lse[:, lane0:lane0 + 1])
                    dsb = (p * (_dot_nt(doh, vc) - sub[:, lane0:lane0 + 1])).astype(BF16)
                    dq = dq + _dot(dsb, jnp.where(mask, kc, jnp.zeros_like(kc)))
                    dk = dk + _dot_tn(dsb, qh)
                    dv = dv + _dot_tn(p.astype(BF16), doh)
                acc.at[0][qrows, :] += dq * ATT_SCALE
                acc.at[1][krows, :] += dk
                acc.at[2][krows, :] += dv

            _for_each_block(dil, unit)
        for k in range(3):
            dz_ref[:, k * LANES:(k + 1) * LANES] = acc[k].astype(BF16)

    blk = pl.BlockSpec((s, LANES), lambda b, hp: (b, hp))
    pair = pl.BlockSpec((s, PAIR_WIDTH), lambda b, hp: (b, hp))
    return pl.pallas_call(
        body, grid=(n_batch, N_PAIRS),
        in_specs=_pair_specs(s) + [blk] * 5,
        out_specs=pair,
        out_shape=jax.ShapeDtypeStruct((t, 2 * 3 * MIX_HALF), BF16),
        scratch_shapes=[pltpu.VMEM((3, s, LANES), F32), pltpu.VMEM((s, LANES), F32), pltpu.VMEM((s, LANES), F32),
                        pltpu.VMEM((3, s, LANES), F32), pltpu.VMEM((6, BLOCK, 2 * BLOCK), F32)],
        name=name, compiler_params=_params("parallel", "arbitrary"),
    )(zd, zd, zd, dy, ya, *lses)


X_BQ = 2048


def _xattn_probs(q, k):
    sc = _dot_nt(q, k) * X_SCALE
    pe = jnp.exp(sc - jnp.max(sc, axis=1, keepdims=True))
    return pe / jnp.sum(pe, axis=1, keepdims=True)


def _xattn_fwd(qx, kx, vx, n_batch, name):
    nq = SEQ // X_BQ

    def body(q_ref, k_ref, v_ref, o_ref):
        p = _xattn_probs(q_ref[...], k_ref[...])
        o_ref[...] = _dot(p.astype(BF16), v_ref[...]).astype(BF16)

    qblk = pl.BlockSpec((X_BQ, X_HEAD_DIM), lambda b, h, i: (b * nq + i, h))
    kblk = pl.BlockSpec((N_MEM, X_HEAD_DIM), lambda b, h, i: (b, h))
    return pl.pallas_call(
        body, grid=(n_batch, X_HEADS, nq), in_specs=[qblk, kblk, kblk], out_specs=qblk,
        out_shape=jax.ShapeDtypeStruct(qx.shape, BF16),
        name=name, compiler_params=_params("parallel", "parallel", "arbitrary"),
    )(qx, kx, vx)


def _xattn_bwd(qx, kx, vx, dox, n_batch, name):
    nq = SEQ // X_BQ

    def body(q_ref, k_ref, v_ref, do_ref, dq_ref, dk_ref, dv_ref, dk_acc, dv_acc):
        i = pl.program_id(2)

        @pl.when(i == 0)
        def _():
            dk_acc[...] = jnp.zeros_like(dk_acc)
            dv_acc[...] = jnp.zeros_like(dv_acc)

        q, k, do = q_ref[...], k_ref[...], do_ref[...]
        p = _xattn_probs(q, k)
        dp = _dot_nt(do, v_ref[...])
        dsb = (p * (dp - jnp.sum(p * dp, axis=1, keepdims=True))).astype(BF16)
        dq_ref[...] = (_dot(dsb, k) * X_SCALE).astype(BF16)
        dk_acc[...] += _dot_tn(dsb, q) * X_SCALE
        dv_acc[...] += _dot_tn(p.astype(BF16), do)

        @pl.when(i == nq - 1)
        def _():
            dk_ref[...] = dk_acc[...].astype(BF16)
            dv_ref[...] = dv_acc[...].astype(BF16)

    qblk = pl.BlockSpec((X_BQ, X_HEAD_DIM), lambda b, h, i: (b * nq + i, h))
    kblk = pl.BlockSpec((N_MEM, X_HEAD_DIM), lambda b, h, i: (b, h))
    return pl.pallas_call(
        body, grid=(n_batch, X_HEADS, nq), in_specs=[qblk, kblk, kblk, qblk], out_specs=[qblk, kblk, kblk],
        out_shape=[jax.ShapeDtypeStruct(qx.shape, BF16), jax.ShapeDtypeStruct(kx.shape, BF16), jax.ShapeDtypeStruct(kx.shape, BF16)],
        scratch_shapes=[pltpu.VMEM((N_MEM, X_HEAD_DIM), F32)] * 2,
        name=name, compiler_params=_params("parallel", "parallel", "arbitrary"),
    )(qx, kx, vx, dox)


def _adamw(w, g, m, v, name, rows):
    r, c = w.shape
    assert r % rows == 0, (name, w.shape, rows)

    def body(w_ref, g_ref, m_ref, v_ref, d_ref, nm_ref, nv_ref):
        gv = g_ref[...]
        m1 = ADAM_B1 * m_ref[...] + (1.0 - ADAM_B1) * gv
        v1 = ADAM_B2 * v_ref[...] + (1.0 - ADAM_B2) * jnp.square(gv)
        m_hat = m1 / (1.0 - ADAM_B1 ** ADAM_STEP)
        v_hat = v1 / (1.0 - ADAM_B2 ** ADAM_STEP)
        d_ref[...] = -ADAM_LR * (m_hat / (jnp.sqrt(v_hat) + ADAM_EPS) + ADAM_WD * w_ref[...])
        nm_ref[...] = m1
        nv_ref[...] = v1

    blk = pl.BlockSpec((rows, c), lambda i: (i, 0))
    return pl.pallas_call(
        body, grid=(r // rows,), in_specs=[blk] * 4, out_specs=[blk] * 3,
        out_shape=[jax.ShapeDtypeStruct((r, c), F32)] * 3,
        name=name, compiler_params=_params("arbitrary"),
    )(w, g, m, v)


def _relu2(acc):
    a = jnp.maximum(acc, 0.0)
    return acc, a * a


def _relu2_bwd(acc, u):
    return (2.0 * jnp.maximum(u.astype(F32), 0.0) * acc,)


def _local_step(x, mem, target, vecs, w_in, late_weights, on_grads=None):
    n_batch = x.shape[0]
    t = n_batch * SEQ
    x0 = x.reshape(t, D_MODEL)
    mem2 = mem.reshape(n_batch * N_MEM, D_MODEL)
    tgt = target.reshape(t, D_MODEL)

    half = 3 * MIX_HALF
    w_dil, w_fox = _pair_major(w_in[:, :half]), _pair_major(w_in[:, half:QKV_WIDTH])
    w_gate = jnp.pad(w_in[:, QKV_WIDTH:], ((0, 0), (0, GATE_PAD - N_HEADS)))
    b_pad = jnp.pad(vecs["b_forget"], (0, GATE_PAD - N_HEADS)).reshape(1, GATE_PAD)

    h1 = _rmsnorm(x0, vecs["g_mix"], "norm_mix")
    mn = _rmsnorm(mem2, vecs["g_mem"], "norm_mem")
    zd = _matmul(h1, w_dil, "in_dil", tn=768)[0]
    zf = _matmul(h1, w_fox, "in_fox", out_dtypes=(BF16,), tn=768)[0]
    gate = _matmul(h1, w_gate, "in_gate")[0]
    c_bc, c_row, sg = _gate_fwd(gate, b_pad, n_batch, "gate_fwd")
    ya, lses = _dil_fwd(zd, n_batch, "dil_fwd")
    yf, of32, lse_f = _fox_fwd(zf, c_bc, c_row, n_batch, "fox_fwd")
    wts = late_weights(yf)
    w_out = wts["w_out"]
    x1, h2 = _matmul_res_norm([ya, yf], [w_out[:MIX_HALF], w_out[MIX_HALF:]], x0, vecs["g_xattn"], "out")
    qx = _matmul(h2, wts["w_xq"], "xq", out_dtypes=(BF16,))[0]
    kx = _matmul(mn, wts["w_xk"], "xk", out_dtypes=(BF16,))[0]
    vx = _matmul(mn, wts["w_xv"], "xv", out_dtypes=(BF16,))[0]
    ox = _xattn_fwd(qx, kx, vx, n_batch, "xattn_fwd")
    x2, h3 = _matmul_res_norm([ox], [wts["w_xo"]], x1, vecs["g_mlp"], "xo")
    u, a2 = _matmul(h3, wts["w_up"], "mlp_up", out_dtypes=(BF16, BF16), epilogue=_relu2, tn=1024)
    loss, dx3, dx3b, dg_final = _loss_bwd(a2, wts["w_down"], x2, vecs["g_final"], tgt, "mlp_down_loss")

    du = _matmul(dx3b, wts["w_down"], "mlp_down_bwd", out_dtypes=(BF16,), extras=(u,), epilogue=_relu2_bwd, tn=1024, w_t=True)[0]
    shards = (N_CHIPS, 2 * D_MODEL, D_MODEL)
    g_mlp = _matmul_tn(h3, du, "gw_up", packed=(shards, lambda i, j: (j, 0, 0), None))
    g_mlp = _matmul_tn(a2, dx3b, "gw_down", packed=(shards, lambda i, j: (i, 1, 0), g_mlp))
    gw_up = g_mlp[:, :D_MODEL].transpose(1, 0, 2).reshape(D_MODEL, D_FF)
    gw_down = g_mlp[:, D_MODEL:].reshape(D_FF, D_MODEL)
    token = on_grads("mlp", g_mlp) if on_grads else None
    dx2, dx2b, dg_mlp = _matmul_rms_bwd([du], [wts["w_up"]], x2, vecs["g_mlp"], dx3, "mlp_up_bwd", after=token)

    gw_xo = _matmul_tn(ox, dx2b, "gw_xo")
    dox = _matmul(dx2b, wts["w_xo"], "xo_bwd", out_dtypes=(BF16,), w_t=True)[0]
    dqx, dkx, dvx = _xattn_bwd(qx, kx, vx, dox, n_batch, "xattn_bwd")
    gw_xq = _matmul_tn(h2, dqx, "gw_xq")
    gw_xk = _matmul_tn(mn, dkx, "gw_xk")
    gw_xv = _matmul_tn(mn, dvx, "gw_xv")
    dmn = _matmul(dkx, wts["w_xk"], "xk_bwd", w_t=True)[0]
    dmn = _matmul_res(dvx, wts["w_xv"], dmn, "xv_bwd", w_t=True)
    _, _, dg_mem = _rms_bwd(mem2, dmn, vecs["g_mem"], None, "norm_mem_bwd")
    dx1, dx1b, dg_xattn = _matmul_rms_bwd([dqx], [wts["w_xq"]], x1, vecs["g_xattn"], dx2, "xq_bwd")

    gw_out = jnp.concatenate([_matmul_tn(ya, dx1b, "gw_out_a"), _matmul_tn(yf, dx1b, "gw_out_f")], axis=0)
    token = on_grads("mid", dict(w_out=gw_out, w_xq=gw_xq, w_xk=gw_xk, w_xv=gw_xv, w_xo=gw_xo)) if on_grads else None
    dy = _matmul(dx1b, w_out, "out_bwd", out_dtypes=(BF16,), w_t=True, after=token)[0]
    dz = _dil_bwd(zd, dy, ya, lses, n_batch, "dil_bwd")
    dz, dc = _fox_bwd(zf, of32, dy, lse_f, c_bc, c_row, dz, n_batch, "fox_bwd")
    dzg, db = _gate_bwd(dc.reshape(n_batch, N_HEADS, SEQ), sg, "gate_bwd")
    gw_pm = _matmul_tn(h1, dz, "gw_in_qkv")
    gw_in = jnp.concatenate([_pair_major_inv(gw_pm[:, :half]), _pair_major_inv(gw_pm[:, half:]),
                             _matmul_tn(h1, dzg, "gw_in_gate")[:, :N_HEADS]], axis=1)
    dx0, _, dg_mix = _matmul_rms_bwd([dz, dzg], [jnp.concatenate([w_dil, w_fox], axis=1), w_gate], x0, vecs["g_mix"], dx1, "in_bwd")

    gw = dict(w_in=gw_in, w_out=gw_out, w_xq=gw_xq, w_xk=gw_xk, w_xv=gw_xv, w_xo=gw_xo, w_up=gw_up, w_down=gw_down)
    gv = dict(g_mix=dg_mix, g_xattn=dg_xattn, g_mem=dg_mem, g_mlp=dg_mlp, g_final=dg_final, b_forget=db)
    return loss, dx0.reshape(x.shape), gw, gv


MESH = pl.DeviceIdType.MESH
ANY = pl.BlockSpec(memory_space=pl.ANY)


def _place():
    x, y, c = lax.axis_index("x"), lax.axis_index("y"), lax.axis_index("c")
    other_chips = [(1 - x, y), (x, 1 - y), (1 - x, 1 - y)]
    return x, y, c, other_chips


def _my_chip():
    return 2 * lax.axis_index("x") + lax.axis_index("y")


def _halves(rows, c, align):
    half = rows // 2
    assert rows % (2 * align) == 0, rows
    return pl.ds(pl.multiple_of(c * half, align), half), pl.ds(pl.multiple_of((1 - c) * half, align), half)


def _place_own(wall, pack):
    return lax.dynamic_update_slice(wall, pack[None], (_my_chip(), 0, 0))


def _gather(pack, name, after):
    def body(p_ref, after_ref, out_ref, send_sems, recv_sems, pass_send, pass_recv):
        del after_ref
        x, y, c, chips = _place()
        me = 2 * x + y
        mine, theirs = _halves(pack.shape[0], c, 16)

        def from_chip(k, chip, rows):
            src = out_ref.at[2 * chip[0] + chip[1], rows]
            return pltpu.make_async_remote_copy(src_ref=src, dst_ref=src, send_sem=send_sems.at[k], recv_sem=recv_sems.at[k],
                                                device_id=(chip[0], chip[1], c), device_id_type=MESH)

        def passed(k, chip, rows):
            src = out_ref.at[2 * chip[0] + chip[1], rows]
            return pltpu.make_async_remote_copy(src_ref=src, dst_ref=src, send_sem=pass_send.at[k], recv_sem=pass_recv.at[k],
                                                device_id=(x, y, 1 - c), device_id_type=MESH)

        sends = []
        for k, chip in enumerate(chips):
            cp = pltpu.make_async_remote_copy(src_ref=p_ref.at[mine], dst_ref=out_ref.at[me, mine], send_sem=send_sems.at[k],
                                              recv_sem=recv_sems.at[k], device_id=(chip[0], chip[1], c), device_id_type=MESH)
            cp.start()
            sends.append(cp)
        for k, chip in enumerate(chips):
            from_chip(k, chip, mine).wait_recv()
            cp = passed(k, chip, mine)
            cp.start()
            sends.append(cp)
        for k, chip in enumerate(chips):
            passed(k, chip, theirs).wait_recv()
        for cp in sends:
            cp.wait_send()

    wall = pl.pallas_call(
        body, in_specs=[ANY, ANY], out_specs=ANY,
        out_shape=jax.ShapeDtypeStruct((N_CHIPS,) + pack.shape, pack.dtype),
        scratch_shapes=[pltpu.SemaphoreType.DMA((3,))] * 4,
        name=name,
    )(pack, after)
    return _place_own(wall, pack)


HBM = pl.BlockSpec(memory_space=pltpu.HBM)
SEM = pl.BlockSpec(memory_space=pltpu.SEMAPHORE)
SPLIT_COPY = pltpu.CompilerParams(has_side_effects=pltpu.SideEffectType.DATAFLOW_SIDE_EFFECTING)


def _in_hbm(a):
    return pltpu.with_memory_space_constraint(a, pltpu.HBM)


def _start_call(start, src, land_shape, after, name):
    land = lax.empty(land_shape, src.dtype)

    def body(src_ref, land_ref, after_ref, send_sems, recv_sems, src_thru, land_thru, token):
        del after_ref, src_thru, land_thru
        start(src_ref, land_ref, send_sems, recv_sems)
        token[...] = jnp.zeros_like(token)

    return pl.pallas_call(
        body, name=name,
        out_shape=(pltpu.SemaphoreType.DMA((3,)), pltpu.SemaphoreType.DMA((3,)), pltpu.HBM(src.shape, src.dtype),
                   pltpu.HBM(land_shape, src.dtype), jax.ShapeDtypeStruct((8, LANES), F32)),
        in_specs=(HBM, HBM, ANY), out_specs=(SEM, SEM, HBM, HBM, pl.BlockSpec(memory_space=pltpu.VMEM)),
        input_output_aliases={0: 2, 1: 3}, compiler_params=SPLIT_COPY,
    )(_in_hbm(src), _in_hbm(land), after)


def _wait_call(body, started, after, name):
    send_sems, recv_sems, src, land, _ = started
    return pl.pallas_call(
        body, name=name,
        out_shape=(pltpu.HBM(src.shape, src.dtype), pltpu.HBM(land.shape, land.dtype)),
        in_specs=(HBM, HBM, SEM, SEM, ANY), out_specs=(HBM, HBM),
        input_output_aliases={0: 0, 1: 1}, compiler_params=SPLIT_COPY,
    )(src, land, send_sems, recv_sems, after)[1]


def _gather_copies(p_ref, wall_ref, send_sems, recv_sems):
    x, y, c, chips = _place()
    me = 2 * x + y
    mine, _ = _halves(p_ref.shape[0], c, 16)
    out, back = [], []
    for k, chip in enumerate(chips):
        peer = dict(send_sem=send_sems.at[k], recv_sem=recv_sems.at[k], device_id=(chip[0], chip[1], c), device_id_type=MESH)
        out.append(pltpu.make_async_remote_copy(src_ref=p_ref.at[mine], dst_ref=wall_ref.at[me, mine], **peer))
        slab = wall_ref.at[2 * chip[0] + chip[1], mine]
        back.append(pltpu.make_async_remote_copy(src_ref=slab, dst_ref=slab, **peer))
    return out, back


def _gather_start(pack, after, name):
    def start(p_ref, wall_ref, send_sems, recv_sems):
        for cp in _gather_copies(p_ref, wall_ref, send_sems, recv_sems)[0]:
            cp.start()

    return _start_call(start, pack, (N_CHIPS,) + pack.shape, after, name)


def _gather_wait(started, after, name):
    def body(p_ref, wall_ref, send_sems, recv_sems, after_ref, p_dead, wall_out):
        del after_ref, p_dead, wall_out
        out, back = _gather_copies(p_ref, wall_ref, send_sems, recv_sems)
        for cp_out, cp_back in zip(out, back):
            cp_out.wait_send()
            cp_back.wait_recv()

    return _wait_call(body, started, after, name)


def _pass_on(wall, name):
    def body(w_in_ref, out_ref, send_sems, recv_sems):
        del w_in_ref
        x, y, c, chips = _place()
        mine, theirs = _halves(wall.shape[1], c, 16)
        sends = []
        for k, chip in enumerate(chips):
            slab = out_ref.at[2 * chip[0] + chip[1]]
            peer = dict(send_sem=send_sems.at[k], recv_sem=recv_sems.at[k], device_id=(x, y, 1 - c), device_id_type=MESH)
            cp = pltpu.make_async_remote_copy(src_ref=slab.at[mine], dst_ref=slab.at[mine], **peer)
            cp.start()
            sends.append((cp, pltpu.make_async_remote_copy(src_ref=slab.at[theirs], dst_ref=slab.at[theirs], **peer)))
        for cp, back in sends:
            back.wait_recv()
            cp.wait_send()

    return pl.pallas_call(
        body, in_specs=[ANY], out_specs=ANY, out_shape=jax.ShapeDtypeStruct(wall.shape, wall.dtype),
        scratch_shapes=[pltpu.SemaphoreType.DMA((3,))] * 2, input_output_aliases={0: 0}, name=name,
    )(wall)


def _swap_halves(g, name):
    half = g.shape[1] // 2

    def body(g_ref, out_ref, send_sem, recv_sem):
        x, y, c, _ = _place()
        _, theirs = _halves(g.shape[1], c, 8)
        cp = pltpu.make_async_remote_copy(src_ref=g_ref.at[:, theirs], dst_ref=out_ref, send_sem=send_sem, recv_sem=recv_sem,
                                          device_id=(x, y, 1 - c), device_id_type=MESH)
        cp.start()
        cp.wait()

    return pl.pallas_call(
        body, in_specs=[ANY], out_specs=ANY,
        out_shape=jax.ShapeDtypeStruct((N_CHIPS, half, D_MODEL), F32),
        scratch_shapes=[pltpu.SemaphoreType.DMA, pltpu.SemaphoreType.DMA],
        name=name,
    )(g)


def _core_index():
    return lax.axis_index("c").astype(jnp.int32).reshape(1)


def _row_tile(half):
    tile = max(t for t in range(16, 1025, 16) if half % t == 0)
    return tile, half // tile


def _add_sibling(g, got, name):
    half = g.shape[1] // 2
    tile, n_tiles = _row_tile(half)

    def body(c_ref, g_ref, got_ref, o_ref):
        o_ref[...] = (g_ref[...] + got_ref[...]).astype(BF16)

    blk = pl.BlockSpec((None, tile, D_MODEL), lambda s, i, c_ref: (s, i, 0))
    return pl.pallas_call(
        body,
        grid_spec=pltpu.PrefetchScalarGridSpec(
            num_scalar_prefetch=1, grid=(N_CHIPS, n_tiles),
            in_specs=[pl.BlockSpec((None, tile, D_MODEL), lambda s, i, c_ref: (s, c_ref[0] * n_tiles + i, 0)), blk],
            out_specs=blk),
        out_shape=jax.ShapeDtypeStruct((N_CHIPS, half, D_MODEL), BF16),
        name=name, compiler_params=_params("arbitrary", "arbitrary"),
    )(_core_index(), g, got)


def _exchange_copies(p_ref, land_ref, send_sems, recv_sems):
    x, y, c, chips = _place()
    me = 2 * x + y
    out, back = [], []
    for k, chip in enumerate(chips):
        peer = dict(send_sem=send_sems.at[k], recv_sem=recv_sems.at[k], device_id=(chip[0], chip[1], c), device_id_type=MESH)
        out.append(pltpu.make_async_remote_copy(src_ref=p_ref.at[2 * chip[0] + chip[1]], dst_ref=land_ref.at[me], **peer))
        slab = land_ref.at[2 * chip[0] + chip[1]]
        back.append(pltpu.make_async_remote_copy(src_ref=slab, dst_ref=slab, **peer))
    return out, back


def _with_own(got, part):
    me = _my_chip()
    return lax.dynamic_update_slice(got, lax.dynamic_slice(part, (me, 0, 0), (1,) + part.shape[1:]), (me, 0, 0))


def _exchange_chips(part, name):
    def body(p_ref, out_ref, send_sems, recv_sems):
        out, back = _exchange_copies(p_ref, out_ref, send_sems, recv_sems)
        for cp in out:
            cp.start()
        for cp in back:
            cp.wait_recv()
        for cp in out:
            cp.wait_send()

    got = pl.pallas_call(
        body, in_specs=[ANY], out_specs=ANY,
        out_shape=jax.ShapeDtypeStruct(part.shape, part.dtype),
        scratch_shapes=[pltpu.SemaphoreType.DMA((3,)), pltpu.SemaphoreType.DMA((3,))],
        name=name,
    )(part)
    return _with_own(got, part)


def _exchange_start(part, name):
    def start(p_ref, land_ref, send_sems, recv_sems):
        for cp in _exchange_copies(p_ref, land_ref, send_sems, recv_sems)[0]:
            cp.start()

    return _start_call(start, part, part.shape, part, name)


def _exchange_wait(started, after, name):
    def body(p_ref, land_ref, send_sems, recv_sems, after_ref, p_dead, land_out):
        del after_ref, p_dead, land_out
        out, back = _exchange_copies(p_ref, land_ref, send_sems, recv_sems)
        for cp_out, cp_back in zip(out, back):
            cp_out.wait_send()
            cp_back.wait_recv()

    return _with_own(_wait_call(body, started, after, name), started[2])


def _sum_chips(parts, name):
    half = parts.shape[1]
    tile, n_tiles = _row_tile(half)

    def body(c_ref, p0, p1, p2, p3, o_ref):
        f32 = lambda p: p[...].astype(F32)
        o_ref[...] = ((f32(p0) + f32(p1)) + f32(p2)) + f32(p3)

    def slab(s):
        return pl.BlockSpec((None, tile, D_MODEL), lambda i, c_ref, s=s: (s, i, 0))

    return pl.pallas_call(
        body,
        grid_spec=pltpu.PrefetchScalarGridSpec(
            num_scalar_prefetch=1, grid=(n_tiles,),
            in_specs=[slab(s) for s in range(N_CHIPS)],
            out_specs=pl.BlockSpec((None, tile, D_MODEL), lambda i, c_ref: (c_ref[0], i, 0))),
        out_shape=jax.ShapeDtypeStruct((2, half, D_MODEL), F32),
        name=name, compiler_params=_params("arbitrary"),
    )(_core_index(), parts, parts, parts, parts)


def _share_halves(halves, name):
    def body(h_ref, out_ref, send_sem, recv_sem):
        del h_ref
        x, y, c, _ = _place()
        cp = pltpu.make_async_remote_copy(src_ref=out_ref.at[c], dst_ref=out_ref.at[c], send_sem=send_sem, recv_sem=recv_sem,
                                          device_id=(x, y, 1 - c), device_id_type=MESH)
        cp.start()
        pltpu.make_async_remote_copy(src_ref=out_ref.at[1 - c], dst_ref=out_ref.at[1 - c], send_sem=send_sem, recv_sem=recv_sem,
                                     device_id=(x, y, 1 - c), device_id_type=MESH).wait_recv()
        cp.wait_send()

    return pl.pallas_call(
        body, in_specs=[ANY], out_specs=ANY,
        out_shape=jax.ShapeDtypeStruct(halves.shape, halves.dtype),
        scratch_shapes=[pltpu.SemaphoreType.DMA] * 2,
        input_output_aliases={0: 0},
        name=name,
    )(halves)


def _reduce_parts(g, tag):
    return _add_sibling(g, _swap_halves(g, "swap_" + tag), "add_" + tag)


def _reduce_finish(got, tag):
    halves = _share_halves(_sum_chips(got, "sum_" + tag), "share_" + tag)
    return halves.reshape(2 * halves.shape[1], D_MODEL)


SMALL_ROWS = 8


def _allreduce_small(v):
    def body(v_ref, out_ref, buf, send_sems, recv_sems):
        x, y, c, _ = _place()
        buf[4 * x + 2 * y + c] = v_ref[...]
        sends = []
        for k in range(1, N_DEV):
            px = 1 - x if k & 4 else x
            py = 1 - y if k & 2 else y
            pc = 1 - c if k & 1 else c
            cp = pltpu.make_async_remote_copy(src_ref=v_ref, dst_ref=buf.at[4 * x + 2 * y + c], send_sem=send_sems.at[k - 1],
                                              recv_sem=recv_sems.at[k - 1], device_id=(px, py, pc), device_id_type=MESH)
            cp.start()
            sends.append((cp, 4 * px + 2 * py + pc))
        for k, (cp, peer) in enumerate(sends):
            pltpu.make_async_remote_copy(src_ref=v_ref, dst_ref=buf.at[peer], send_sem=send_sems.at[k], recv_sem=recv_sems.at[k],
                                         device_id=(x, y, c), device_id_type=MESH).wait_recv()
        for cp, _ in sends:
            cp.wait_send()
        total = buf[0]
        for d in range(1, N_DEV):
            total = total + buf[d]
        out_ref[...] = total

    vmem = pl.BlockSpec(memory_space=pltpu.VMEM)
    return pl.pallas_call(
        body, in_specs=[vmem], out_specs=vmem,
        out_shape=jax.ShapeDtypeStruct(v.shape, v.dtype),
        scratch_shapes=[pltpu.VMEM((N_DEV,) + v.shape, v.dtype), pltpu.SemaphoreType.DMA((N_DEV - 1,)),
                        pltpu.SemaphoreType.DMA((N_DEV - 1,))],
        name="allreduce_small",
    )(v)


MATRICES = ("w_in", "w_out", "w_xq", "w_xk", "w_xv", "w_xo", "w_up", "w_down")
VECTORS = ("g_mix", "g_xattn", "g_mem", "g_mlp", "g_final", "b_forget")
WEIGHT_ORDER = ("g_mix", "w_in", "b_forget", "w_out", "g_xattn", "g_mem", "w_xq", "w_xk", "w_xv", "w_xo",
                "g_mlp", "w_up", "w_down", "g_final")
GROUPS = {"mlp": ("w_up", "w_down"), "mid": ("w_out", "w_xq", "w_xk", "w_xv", "w_xo"), "in": ("w_in",)}
LATE = GROUPS["mid"] + GROUPS["mlp"]
W_IN_SHARD = IN_WIDTH // N_CHIPS
SHARD_ROWS = {"w_in": W_IN_SHARD, "w_out": 256, "w_xq": 256, "w_xk": 256, "w_xv": 256, "w_xo": 256, "w_up": 1024, "w_down": 1024}
PACK_ROWS = {n: -(-r // 32) * 32 for n, r in SHARD_ROWS.items()}
ADAM_ROWS = 128


def _pack(parts, names):
    return jnp.concatenate([jnp.pad(parts[n], ((0, PACK_ROWS[n] - SHARD_ROWS[n]), (0, 0))) for n in names], axis=0)


def _unpack(a, names):
    out, pos = {}, 0
    for n in names:
        out[n] = a[..., pos:pos + SHARD_ROWS[n], :]
        pos += PACK_ROWS[n]
    return out


def _full_weights(wall, names):
    cols = lambda a: a.transpose(1, 0, 2).reshape(a.shape[1], -1)
    rows = lambda a: a.reshape(-1, a.shape[-1])
    out = {}
    for n, a in _unpack(wall, names).items():
        if n == "w_in":
            out[n] = cols(a.reshape(N_CHIPS, D_MODEL, W_IN_SHARD))
        else:
            out[n] = cols(a) if n == "w_up" else rows(a)
    return out


def _shard_of(g, name, s):
    if name == "w_in":
        return g[:, s * W_IN_SHARD:(s + 1) * W_IN_SHARD].reshape(W_IN_SHARD, D_MODEL)
    if name == "w_up":
        return g[:, s * D_MODEL:(s + 1) * D_MODEL]
    n = SHARD_ROWS[name]
    return g[s * n:(s + 1) * n]


def _pack_grads(gws, names):
    return jnp.stack([_pack({n: _shard_of(gws[n], n, s) for n in names}, names) for s in range(N_CHIPS)])


def kernel(x, mem, g_mix, w_in, b_forget, w_out, g_xattn, g_mem, w_xq, w_xk, w_xv, w_xo, g_mlp, w_up, w_down, g_final, loss_target, m_g_mix, m_w_in, m_b_forget, m_w_out, m_g_xattn, m_g_mem, m_w_xq, m_w_xk, m_w_xv, m_w_xo, m_g_mlp, m_w_up, m_w_down, m_g_final, v_g_mix, v_w_in, v_b_forget, v_w_out, v_g_xattn, v_g_mem, v_w_xq, v_w_xk, v_w_xv, v_w_xo, v_g_mlp, v_w_up, v_w_down, v_g_final):
    given = dict(locals())
    weights = {n: given[n] for n in WEIGHT_ORDER}
    vecs = {n: weights[n] for n in VECTORS}

    shard = {n: weights[n].astype(BF16) for n in MATRICES}
    shard["w_in"] = shard["w_in"].reshape(W_IN_SHARD, D_MODEL)
    in_pack, late_pack = _pack(shard, GROUPS["in"]), _pack(shard, LATE)
    in_wall = _gather(in_pack, "gather_in", in_pack)
    late = _gather_start(late_pack, in_wall, "gather_late_start")
    w_in_full = _full_weights(in_wall, GROUPS["in"])["w_in"]

    def late_weights(after):
        wall = _pass_on(_gather_wait(late, after, "gather_late_wait"), "gather_late_pass")
        return _full_weights(_place_own(wall, late_pack), LATE)

    started = {}

    def on_grads(group, gws):
        packed = gws if group == "mlp" else _pack_grads(gws, GROUPS[group])
        part = _reduce_parts(packed, group)
        started[group] = _exchange_start(part, "exchange_%s_start" % group)
        return started[group][4]

    loss, grad_x, gw, gv = _local_step(x, mem, loss_target, vecs, w_in_full, late_weights, on_grads)

    on_grads("in", gw)
    grads, delta, new_m, new_v = {}, {}, {}, {}

    def finish(group, after):
        got = _exchange_wait(started[group], after, "exchange_%s_wait" % group)
        for n, a in _unpack(_reduce_finish(got, group), GROUPS[group]).items():
            grads[n] = a.reshape(weights[n].shape)
            delta[n], new_m[n], new_v[n] = _adamw(weights[n], grads[n], given["m_" + n], given["v_" + n], "adamw_" + n, ADAM_ROWS)
        return new_v[GROUPS[group][-1]]

    after = finish("mlp", started["in"][4])
    after = finish("mid", after)

    row = lambda a: jnp.pad(a.reshape(-1), (0, D_MODEL - a.size)).reshape(1, D_MODEL)
    small = jnp.concatenate([gv[n] for n in VECTORS[:5]] + [row(gv["b_forget"][:, 0]), row(loss[0, :1]),
                             jnp.zeros((1, D_MODEL), F32)], axis=0)
    small = _allreduce_small(small)
    for k, n in enumerate(VECTORS[:5]):
        grads[n] = small[k]
    grads["b_forget"] = small[5, :N_HEADS]
    loss_total = small[6, 0]
    finish("in", after)

    stack = lambda prefix: jnp.concatenate([row(given[prefix + n]) for n in VECTORS] + [jnp.zeros((2, D_MODEL), F32)], axis=0)
    g_small = jnp.concatenate([small[:6], jnp.zeros((2, D_MODEL), F32)], axis=0)
    d, m1, v1 = _adamw(stack(""), g_small, stack("m_"), stack("v_"), "adamw_vectors", SMALL_ROWS)
    for k, n in enumerate(VECTORS):
        width = weights[n].shape[0]
        delta[n], new_m[n], new_v[n] = d[k, :width], m1[k, :width], v1[k, :width]

    return (loss_total, grad_x, *[grads[n] for n in WEIGHT_ORDER], *[delta[n] for n in WEIGHT_ORDER],
            *[new_m[n] for n in WEIGHT_ORDER], *[new_v[n] for n in WEIGHT_ORDER])
```

```python
import functools
import math

import jax
import jax.numpy as jnp
from jax import lax
from jax.experimental import pallas as pl
from jax.experimental.pallas import tpu as pltpu

F32 = jnp.float32
BF16 = jnp.bfloat16

D_MODEL = 1024
SEQ = 2048
N_MEM = 256
HEAD_DIM = 64
N_HEADS = 8
MIX_HALF = N_HEADS * HEAD_DIM
QKV_WIDTH = 6 * MIX_HALF
IN_WIDTH = QKV_WIDTH + N_HEADS
GATE_PAD = 128
BLOCK = 128
DILATIONS = (1, 4, 16)
X_HEADS = 4
X_HEAD_DIM = 256
D_FF = 4096
EPS = 1e-6
NEG = -1e30
ATT_SCALE = 1.0 / math.sqrt(HEAD_DIM)
X_SCALE = 1.0 / math.sqrt(X_HEAD_DIM)
LANES = 128
N_CHIPS = 4
N_DEV = 8

ADAM_LR = 0.001
ADAM_B1 = 0.9
ADAM_B2 = 0.999
ADAM_EPS = 1e-08
ADAM_WD = 0.01
ADAM_STEP = 10

VMEM_LIMIT = 48 * 1024 * 1024


def _params(*sem):
    return pltpu.CompilerParams(dimension_semantics=sem or None, vmem_limit_bytes=VMEM_LIMIT)


def _dot(a, b):
    return jnp.dot(a, b, preferred_element_type=F32)


def _dot_nt(a, b):
    return lax.dot_general(a, b, (((1,), (1,)), ((), ())), preferred_element_type=F32)


def _dot_tn(a, b):
    return lax.dot_general(a, b, (((0,), (0,)), ((), ())), preferred_element_type=F32)


def _dot_exact(x, e):
    hi = x.astype(BF16)
    r1 = x - hi.astype(F32)
    mid = r1.astype(BF16)
    lo = (r1 - mid.astype(F32)).astype(BF16)
    return _dot(hi, e) + _dot(mid, e) + _dot(lo, e)


def _head_mask(e):
    lane = lax.broadcasted_iota(jnp.int32, (1, LANES), 1)
    return (lane >= HEAD_DIM * e) & (lane < HEAD_DIM * (e + 1))


def _matmul(a, w, name, out_dtypes=(F32,), extras=(), epilogue=None, tm=1024, tn=512, w_t=False, after=None):
    m, k = a.shape
    n = w.shape[0] if w_t else w.shape[1]
    tm, tn = min(tm, m), min(tn, n)
    assert m % tm == 0 and n % tn == 0, (name, a.shape, w.shape)
    n_ex = len(extras)
    order = () if after is None else (after,)

    def body(a_ref, w_ref, *rest):
        rest = rest[len(order):]
        acc = (_dot_nt if w_t else _dot)(a_ref[...], w_ref[...])
        res = (acc,) if epilogue is None else epilogue(acc, *[r[...] for r in rest[:n_ex]])
        for o_ref, r in zip(rest[n_ex:], res):
            o_ref[...] = r.astype(o_ref.dtype)

    tile = pl.BlockSpec((tm, tn), lambda i, j: (i, j))
    w_spec = pl.BlockSpec((tn, k), lambda i, j: (j, 0)) if w_t else pl.BlockSpec((k, tn), lambda i, j: (0, j))
    return pl.pallas_call(
        body, grid=(m // tm, n // tn),
        in_specs=[pl.BlockSpec((tm, k), lambda i, j: (i, 0)), w_spec] + [pl.BlockSpec(memory_space=pl.ANY)] * len(order) + [tile] * n_ex,
        out_specs=[tile] * len(out_dtypes),
        out_shape=[jax.ShapeDtypeStruct((m, n), dt) for dt in out_dtypes],
        name=name, compiler_params=_params("parallel", "arbitrary"),
    )(a, w, *order, *extras)


def _matmul_res(a, w, res, name, w_t=False):
    return _matmul(a, w, name, extras=(res,), epilogue=lambda acc, r: (r + acc,), w_t=w_t)[0]


def _matmul_tn(x, y, name, tm=1024, tn=1024, tk=512, packed=None):
    t, m = x.shape
    _, n = y.shape
    tm, tn, tk = min(tm, m), min(tn, n), min(tk, t)
    assert m % tm == 0 and n % tn == 0 and t % tk == 0, (name, x.shape, y.shape)
    shape, place, into = packed or ((m, n), None, None)

    def body(x_ref, y_ref, *rest):
        o_ref = rest[-1]

        @pl.when(pl.program_id(2) == 0)
        def _():
            o_ref[...] = jnp.zeros_like(o_ref)

        o_ref[...] += _dot_tn(x_ref[...], y_ref[...])

    out_spec = (pl.BlockSpec((tm, tn), lambda i, j, k: (i, j)) if place is None
                else pl.BlockSpec((None, tm, tn), lambda i, j, k: place(i, j)))
    return pl.pallas_call(
        body, grid=(m // tm, n // tn, t // tk),
        in_specs=[pl.BlockSpec((tk, tm), lambda i, j, k: (k, i)), pl.BlockSpec((tk, tn), lambda i, j, k: (k, j))]
        + ([] if into is None else [pl.BlockSpec(memory_space=pl.ANY)]),
        out_specs=out_spec, out_shape=jax.ShapeDtypeStruct(shape, F32),
        input_output_aliases={} if into is None else {2: 0},
        name=name, compiler_params=_params("parallel", "parallel", "arbitrary"),
    )(x, y, *(() if into is None else (into,)))


def _rmsnorm(x, g, name, tm=512):
    t, d = x.shape
    tm = min(tm, t)

    def body(x_ref, g_ref, h_ref):
        xv = x_ref[...]
        r = lax.rsqrt(jnp.mean(xv * xv, axis=-1, keepdims=True) + EPS)
        h_ref[...] = (xv * r * g_ref[...]).astype(BF16)

    return pl.pallas_call(
        body, grid=(t // tm,),
        in_specs=[pl.BlockSpec((tm, d), lambda i: (i, 0)), pl.BlockSpec((1, d), lambda i: (0, 0))],
        out_specs=pl.BlockSpec((tm, d), lambda i: (i, 0)),
        out_shape=jax.ShapeDtypeStruct((t, d), BF16),
        name=name, compiler_params=_params("arbitrary"),
    )(x, g.reshape(1, d))


def _in_proj(x, g, w_all, name, tm=512):
    t, d = x.shape
    half = 3 * MIX_HALF

    def body(x_ref, g_ref, w_ref, h_ref, zd_ref, zf_ref, gate_ref):
        xv = x_ref[...]
        r = lax.rsqrt(jnp.mean(xv * xv, axis=-1, keepdims=True) + EPS)
        h = (xv * r * g_ref[...]).astype(BF16)
        h_ref[...] = h
        zd_ref[...] = _dot(h, w_ref[:, 0:half])
        zf_ref[...] = _dot(h, w_ref[:, half:2 * half]).astype(BF16)
        gate_ref[...] = _dot(h, w_ref[:, 2 * half:])

    row = lambda width: pl.BlockSpec((tm, width), lambda i: (i, 0))
    return pl.pallas_call(
        body, grid=(t // tm,),
        in_specs=[row(d), pl.BlockSpec((1, d), lambda i: (0, 0)), pl.BlockSpec(w_all.shape, lambda i: (0, 0))],
        out_specs=[row(d), row(half), row(half), row(GATE_PAD)],
        out_shape=[jax.ShapeDtypeStruct((t, d), BF16), jax.ShapeDtypeStruct((t, half), F32),
                   jax.ShapeDtypeStruct((t, half), BF16), jax.ShapeDtypeStruct((t, GATE_PAD), F32)],
        name=name, compiler_params=_params("arbitrary"),
    )(x, g.reshape(1, d), w_all)


def _rms_bwd_tile(xv, dh, g):
    d = xv.shape[-1]
    r = lax.rsqrt(jnp.mean(xv * xv, axis=-1, keepdims=True) + EPS)
    dyg = dh * g
    proj = jnp.sum(dyg * xv, axis=-1, keepdims=True)
    dx = r * dyg - xv * (r * r * r * (1.0 / d)) * proj
    return dx, dh * (xv * r)


def _rms_bwd(x, dh, g, dres, name, tm=512):
    t, d = x.shape
    tm = min(tm, t)
    has_res = dres is not None

    def body(x_ref, dh_ref, g_ref, *rest):
        if has_res:
            res_ref, dx_ref, dxb_ref, dg_ref = rest
        else:
            dx_ref, dxb_ref, dg_ref = rest
        dx, dg_rows = _rms_bwd_tile(x_ref[...], dh_ref[...], g_ref[...])
        if has_res:
            dx = res_ref[...] + dx
        dx_ref[...] = dx
        dxb_ref[...] = dx.astype(BF16)

        @pl.when(pl.program_id(0) == 0)
        def _():
            dg_ref[...] = jnp.zeros_like(dg_ref)

        dg_ref[...] += jnp.sum(dg_rows, axis=0, keepdims=True)

    row = pl.BlockSpec((tm, d), lambda i: (i, 0))
    vec = pl.BlockSpec((1, d), lambda i: (0, 0))
    return pl.pallas_call(
        body, grid=(t // tm,),
        in_specs=[row, row, vec] + ([row] if has_res else []),
        out_specs=[row, row, vec],
        out_shape=[jax.ShapeDtypeStruct((t, d), F32), jax.ShapeDtypeStruct((t, d), BF16), jax.ShapeDtypeStruct((1, d), F32)],
        name=name, compiler_params=_params("arbitrary"),
    )(x, dh, g.reshape(1, d), *((dres,) if has_res else ()))


def _row_dots(a_refs, w_refs, w_t):
    acc = None
    for a_ref, w_ref in zip(a_refs, w_refs):
        part = (_dot_nt if w_t else _dot)(a_ref[...], w_ref[...])
        acc = part if acc is None else acc + part
    return acc


def _row_specs(a_parts, w_parts, tm):
    specs = [pl.BlockSpec((tm, a.shape[1]), lambda i: (i, 0)) for a in a_parts]
    return specs + [pl.BlockSpec(w.shape, lambda i: (0, 0)) for w in w_parts]


def _matmul_res_norm(a_parts, w_parts, res, g, name, tm=512):
    t, d = res.shape
    n = len(a_parts)

    def body(*refs):
        res_ref, g_ref, x_ref, h_ref = refs[2 * n:]
        xv = res_ref[...] + _row_dots(refs[:n], refs[n:2 * n], False)
        x_ref[...] = xv
        r = lax.rsqrt(jnp.mean(xv * xv, axis=-1, keepdims=True) + EPS)
        h_ref[...] = (xv * r * g_ref[...]).astype(BF16)

    row = pl.BlockSpec((tm, d), lambda i: (i, 0))
    return pl.pallas_call(
        body, grid=(t // tm,),
        in_specs=_row_specs(a_parts, w_parts, tm) + [row, pl.BlockSpec((1, d), lambda i: (0, 0))],
        out_specs=[row, row],
        out_shape=[jax.ShapeDtypeStruct((t, d), F32), jax.ShapeDtypeStruct((t, d), BF16)],
        name=name, compiler_params=_params("arbitrary"),
    )(*a_parts, *w_parts, res, g.reshape(1, d))


def _matmul_rms_bwd(a_parts, w_parts, x, g, dres, name, tm=512, after=None):
    t, d = x.shape
    n = len(a_parts)
    order = () if after is None else (after,)

    def body(*refs):
        x_ref, g_ref, res_ref = refs[2 * n:2 * n + 3]
        dx_ref, dxb_ref, dg_ref = refs[2 * n + 3 + len(order):]
        dx, dg_rows = _rms_bwd_tile(x_ref[...], _row_dots(refs[:n], refs[n:2 * n], True), g_ref[...])
        dx = res_ref[...] + dx
        dx_ref[...] = dx
        dxb_ref[...] = dx.astype(BF16)

        @pl.when(pl.program_id(0) == 0)
        def _():
            dg_ref[...] = jnp.zeros_like(dg_ref)

        dg_ref[...] += jnp.sum(dg_rows, axis=0, keepdims=True)

    row = pl.BlockSpec((tm, d), lambda i: (i, 0))
    vec = pl.BlockSpec((1, d), lambda i: (0, 0))
    return pl.pallas_call(
        body, grid=(t // tm,),
        in_specs=_row_specs(a_parts, w_parts, tm) + [row, vec, row] + [pl.BlockSpec(memory_space=pl.ANY)] * len(order),
        out_specs=[row, row, vec],
        out_shape=[jax.ShapeDtypeStruct((t, d), F32), jax.ShapeDtypeStruct((t, d), BF16), jax.ShapeDtypeStruct((1, d), F32)],
        name=name, compiler_params=_params("arbitrary"),
    )(*a_parts, *w_parts, x, g.reshape(1, d), dres, *order)


def _loss_bwd(a, w, res, g, target, name, tm=512):
    t, d = res.shape

    def body(a_ref, w_ref, x_ref, g_ref, t_ref, loss_ref, dx_ref, dxb_ref, dg_ref):
        xv = x_ref[...] + _dot(a_ref[...], w_ref[...])
        gv = g_ref[...]
        r = lax.rsqrt(jnp.mean(xv * xv, axis=-1, keepdims=True) + EPS)
        err = xv * r * gv - t_ref[...]
        dx, dg_rows = _rms_bwd_tile(xv, err * (1.0 / d), gv)
        dx_ref[...] = dx
        dxb_ref[...] = dx.astype(BF16)

        @pl.when(pl.program_id(0) == 0)
        def _():
            dg_ref[...] = jnp.zeros_like(dg_ref)
            loss_ref[...] = jnp.zeros_like(loss_ref)

        dg_ref[...] += jnp.sum(dg_rows, axis=0, keepdims=True)
        part = jnp.sum(jnp.sum(err * err, axis=0, keepdims=True), axis=1, keepdims=True) * (0.5 / d)
        loss_ref[...] += jnp.broadcast_to(part, loss_ref.shape)

    row = pl.BlockSpec((tm, d), lambda i: (i, 0))
    vec = pl.BlockSpec((1, d), lambda i: (0, 0))
    return pl.pallas_call(
        body, grid=(t // tm,),
        in_specs=_row_specs([a], [w], tm) + [row, vec, row],
        out_specs=[pl.BlockSpec((1, LANES), lambda i: (0, 0)), row, row, vec],
        out_shape=[jax.ShapeDtypeStruct((1, LANES), F32), jax.ShapeDtypeStruct((t, d), F32),
                   jax.ShapeDtypeStruct((t, d), BF16), jax.ShapeDtypeStruct((1, d), F32)],
        name=name, compiler_params=_params("arbitrary"),
    )(a, w, res, g.reshape(1, d), target)


def _tri(upper):
    r = lax.broadcasted_iota(jnp.int32, (LANES, LANES), 0)
    c = lax.broadcasted_iota(jnp.int32, (LANES, LANES), 1)
    return jnp.where((r <= c) if upper else (r >= c), 1.0, 0.0).astype(BF16)


def _gate_fwd(gate, b_pad, n_batch, name):
    s = SEQ
    nblk = s // LANES

    def body(g_ref, b_ref, cbc_ref, crow_ref, sg_ref, ct_ref):
        gz = g_ref[...] + b_ref[...]
        logf = jnp.minimum(gz, 0.0) - jnp.log(1.0 + jnp.exp(-jnp.abs(gz)))
        logf_t = logf.T
        sg_ref[...] = (1.0 / (1.0 + jnp.exp(gz))).T[0:N_HEADS]
        upper = _tri(True)
        carry = jnp.zeros((LANES, 1), F32)
        for blk in range(nblk):
            seg = _dot_exact(logf_t[:, blk * LANES:(blk + 1) * LANES], upper) + carry
            carry = seg[:, LANES - 1:LANES]
            ct_ref[:, blk * LANES:(blk + 1) * LANES] = seg
        ct = ct_ref[...]
        crow_ref[...] = ct[0:N_HEADS]
        c_col = ct.T
        lane = lax.broadcasted_iota(jnp.int32, (1, MIX_HALF), 1)
        acc = jnp.zeros((s, MIX_HALF), F32)
        for h in range(N_HEADS):
            acc = jnp.where((lane >= HEAD_DIM * h) & (lane < HEAD_DIM * (h + 1)), c_col[:, h:h + 1], acc)
        cbc_ref[...] = acc

    return pl.pallas_call(
        body, grid=(n_batch,),
        in_specs=[pl.BlockSpec((s, GATE_PAD), lambda b: (b, 0)), pl.BlockSpec((1, GATE_PAD), lambda b: (0, 0))],
        out_specs=[pl.BlockSpec((s, MIX_HALF), lambda b: (b, 0)),
                   pl.BlockSpec((None, N_HEADS, s), lambda b: (b, 0, 0)),
                   pl.BlockSpec((None, N_HEADS, s), lambda b: (b, 0, 0))],
        out_shape=[jax.ShapeDtypeStruct((n_batch * s, MIX_HALF), F32),
                   jax.ShapeDtypeStruct((n_batch, N_HEADS, s), F32),
                   jax.ShapeDtypeStruct((n_batch, N_HEADS, s), F32)],
        scratch_shapes=[pltpu.VMEM((LANES, s), F32)],
        name=name, compiler_params=_params("arbitrary"),
    )(gate, b_pad)


def _gate_bwd(dc, sg, name):
    n_batch, _, s = dc.shape
    nblk = s // LANES

    def body(dc_ref, sg_ref, dz_ref, db_ref, dt_ref):
        lower = _tri(False)
        dcv = dc_ref[...]
        carry = jnp.zeros((N_HEADS, 1), F32)
        dt_ref[...] = jnp.zeros_like(dt_ref)
        for blk in reversed(range(nblk)):
            seg = _dot_exact(dcv[:, blk * LANES:(blk + 1) * LANES], lower) + carry
            carry = seg[:, 0:1]
            dt_ref[0:N_HEADS, blk * LANES:(blk + 1) * LANES] = seg * sg_ref[:, blk * LANES:(blk + 1) * LANES]
        dg_t = dt_ref[...]
        dz_ref[...] = dg_t.T.astype(BF16)

        @pl.when(pl.program_id(0) == 0)
        def _():
            db_ref[...] = jnp.zeros_like(db_ref)

        db_ref[...] += jnp.broadcast_to(jnp.sum(dg_t[0:N_HEADS], axis=1, keepdims=True), db_ref.shape)

    return pl.pallas_call(
        body, grid=(n_batch,),
        in_specs=[pl.BlockSpec((None, N_HEADS, s), lambda b: (b, 0, 0)), pl.BlockSpec((None, N_HEADS, s), lambda b: (b, 0, 0))],
        out_specs=[pl.BlockSpec((s, GATE_PAD), lambda b: (b, 0)), pl.BlockSpec((N_HEADS, LANES), lambda b: (0, 0))],
        out_shape=[jax.ShapeDtypeStruct((n_batch * s, GATE_PAD), BF16), jax.ShapeDtypeStruct((N_HEADS, LANES), F32)],
        scratch_shapes=[pltpu.VMEM((LANES, s), F32)],
        name=name, compiler_params=_params("arbitrary"),
    )(dc, sg)


FOX_BQ = 512
FOX_BK = 512
FOX_STRIP = 512
PAIR_WIDTH = 3 * LANES
N_PAIRS = N_HEADS // 2


def _pair_major(w):
    return w.reshape(w.shape[0], 3, N_PAIRS, LANES).transpose(0, 2, 1, 3).reshape(w.shape[0], 3 * MIX_HALF)


def _pair_major_inv(w):
    return w.reshape(w.shape[0], N_PAIRS, 3, LANES).transpose(0, 2, 1, 3).reshape(w.shape[0], 3 * MIX_HALF)


def _causal(i, j, bq, bk):
    qpos = i * bq + lax.broadcasted_iota(jnp.int32, (bq, 1), 0)
    kpos = j * bk + lax.broadcasted_iota(jnp.int32, (1, bk), 1)
    return kpos <= qpos


def _split_bf16(p):
    hi = p.astype(BF16)
    return hi, (p - hi.astype(F32)).astype(BF16)


def _fox_fwd(zf, c_bc, c_row, n_batch, name):
    s, bq, bk = SEQ, FOX_BQ, FOX_BK
    nq = s // bq
    t = n_batch * s

    n_strip = bq // FOX_STRIP

    def body(q_ref, k_ref, v_ref, cq_ref, cr_ref, o_ref, o32_ref, lse_ref):
        hp, i = pl.program_id(1), pl.program_id(2)
        strips = [slice(r * FOX_STRIP, (r + 1) * FOX_STRIP) for r in range(n_strip)]
        chains = [(e, r) for e in range(2) for r in range(n_strip)]
        qh, cq = {}, {}
        for e, r in chains:
            q = q_ref[strips[r], :] * ATT_SCALE
            qh[e, r] = jnp.where(_head_mask(e), q, jnp.zeros_like(q))
            cq[e, r] = cq_ref[strips[r], HEAD_DIM * e:HEAD_DIM * e + 1]

        def step(j, carry, masked):
            rows = pl.ds(pl.multiple_of(j * bk, bk), bk)
            kj, vj = k_ref[rows, :], v_ref[rows, :]
            ck = [cr_ref[pl.ds(2 * hp + e, 1), rows] for e in range(2)]
            out = []
            scores = [_dot_nt(qh[e, r], kj) for e, r in chains]
            for n, (e, r) in enumerate(chains):
                m, l, acc = carry[3 * n:3 * n + 3]
                sc = scores[n] + (cq[e, r] - ck[e])
                if masked:
                    qpos = i * bq + r * FOX_STRIP + lax.broadcasted_iota(jnp.int32, (FOX_STRIP, 1), 0)
                    kpos = j * bk + lax.broadcasted_iota(jnp.int32, (1, bk), 1)
                    sc = jnp.where(kpos <= qpos, sc, NEG)
                m_new = jnp.maximum(m, jnp.max(sc, axis=1, keepdims=True))
                alpha = jnp.exp(m - m_new)
                p = jnp.exp(sc - m_new)
                p_hi, p_lo = _split_bf16(p)
                out += [m_new, alpha * l + jnp.sum(p, axis=1, keepdims=True), alpha * acc + (_dot(p_hi, vj) + _dot(p_lo, vj))]
            return tuple(out)

        init = (jnp.full((FOX_STRIP, 1), NEG, F32), jnp.zeros((FOX_STRIP, 1), F32), jnp.zeros((FOX_STRIP, LANES), F32)) * len(chains)
        n_clear = (i * bq) // bk
        carry = lax.fori_loop(0, n_clear, functools.partial(step, masked=False), init)
        carry = lax.fori_loop(n_clear, (i * bq + bq + bk - 1) // bk, functools.partial(step, masked=True), carry)
        for r in range(n_strip):
            outs = [carry[3 * (e * n_strip + r) + 2] / carry[3 * (e * n_strip + r) + 1] for e in range(2)]
            lses = [carry[3 * (e * n_strip + r)] + jnp.log(carry[3 * (e * n_strip + r) + 1]) for e in range(2)]
            o = jnp.where(_head_mask(0), outs[0], outs[1])
            o_ref[strips[r], :] = o.astype(BF16)
            o32_ref[strips[r], :] = o
            lse_ref[strips[r], :] = jnp.where(_head_mask(0), lses[0], lses[1])

    def col(c0):
        return lambda b, hp, i: (b, 3 * hp + c0)

    blk = pl.BlockSpec((bq, LANES), lambda b, hp, i: (b * nq + i, hp))
    return pl.pallas_call(
        body, grid=(n_batch, N_PAIRS, nq),
        in_specs=[pl.BlockSpec((bq, LANES), lambda b, hp, i: (b * nq + i, 3 * hp)),
                  pl.BlockSpec((s, LANES), col(1)), pl.BlockSpec((s, LANES), col(2)), blk,
                  pl.BlockSpec((None, N_HEADS, s), lambda b, hp, i: (b, 0, 0))],
        out_specs=[blk, blk, blk],
        out_shape=[jax.ShapeDtypeStruct((t, MIX_HALF), BF16), jax.ShapeDtypeStruct((t, MIX_HALF), F32),
                   jax.ShapeDtypeStruct((t, MIX_HALF), F32)],
        name=name, compiler_params=_params("parallel", "parallel", "arbitrary"),
    )(zf, zf, zf, c_bc, c_row)


def _fox_bwd(zf, o32, dy, lse, c_bc, c_row, dz, n_batch, name):
    s, bq, bk = SEQ, FOX_BQ, FOX_BK
    nq, nk = s // bq, s // bk

    def body(q_ref, k_ref, v_ref, o_ref, do_ref, lse_ref, cq_ref, cr_ref, dz_in, dz_ref, dc_ref, dq_acc):
        del dz_in
        hp, j = pl.program_id(1), pl.program_id(2)

        @pl.when(j == 0)
        def _():
            dq_acc[...] = jnp.zeros_like(dq_acc)

        kj, vj = k_ref[...], v_ref[...]
        cols = pl.ds(pl.multiple_of(j * bk, bk), bk)
        km = [jnp.where(_head_mask(e), kj, jnp.zeros_like(kj)) for e in range(2)]
        ck = [cr_ref[pl.ds(2 * hp + e, 1), cols] for e in range(2)]

        def step(i, carry, masked):
            rows = pl.ds(pl.multiple_of(i * bq, bq), bq)
            qi, doi = q_ref[rows, :] * ATT_SCALE, do_ref[rows, :]
            prod = doi.astype(F32) * o_ref[rows, :]
            out = []
            dq = jnp.zeros((bq, LANES), F32)
            for e in range(2):
                dk_a, dv_a, dc_a = carry[3 * e:3 * e + 3]
                mask = _head_mask(e)
                lane0 = HEAD_DIM * e
                dom = jnp.where(mask, doi, jnp.zeros_like(doi))
                delta = jnp.sum(jnp.where(mask, prod, 0.0), axis=1, keepdims=True)
                sc = _dot_nt(qi, km[e]) + (cq_ref[rows, lane0:lane0 + 1] - ck[e])
                if masked:
                    sc = jnp.where(_causal(i, j, bq, bk), sc, NEG)
                p = jnp.exp(sc - lse_ref[rows, lane0:lane0 + 1])
                ds = p * (_dot_nt(dom, vj) - delta)
                dsb = ds.astype(BF16)
                dq = dq + _dot(dsb, km[e])
                out += [dk_a + _dot_tn(dsb, qi), dv_a + _dot_tn(p.astype(BF16), dom), dc_a - jnp.sum(ds, axis=0, keepdims=True)]
            dq_acc[rows, :] += dq * ATT_SCALE
            return tuple(out)

        init = (jnp.zeros((bk, LANES), F32), jnp.zeros((bk, LANES), F32), jnp.zeros((1, bk), F32)) * 2
        first = (j * bk) // bq
        n_diag = (j * bk + bk + bq - 1) // bq
        carry = lax.fori_loop(first, n_diag, functools.partial(step, masked=True), init)
        carry = lax.fori_loop(n_diag, nq, functools.partial(step, masked=False), carry)
        for e in range(2):
            dc_ref[e:e + 1, :] = carry[3 * e + 2]
        dz_ref[cols, LANES:2 * LANES] = jnp.where(_head_mask(0), carry[0], carry[3]).astype(BF16)
        dz_ref[cols, 2 * LANES:3 * LANES] = (carry[1] + carry[4]).astype(BF16)

        @pl.when(j == nk - 1)
        def _():
            dz_ref[:, 0:LANES] = dq_acc[...].astype(BF16)

    def seq(idx):
        return pl.BlockSpec((s, LANES), lambda b, hp, j: (b, idx(hp)))

    def kblk(c0):
        return pl.BlockSpec((bk, LANES), lambda b, hp, j: (b * nk + j, 3 * hp + c0))

    return pl.pallas_call(
        body, grid=(n_batch, N_PAIRS, nk),
        in_specs=[seq(lambda hp: 3 * hp), kblk(1), kblk(2), seq(lambda hp: hp), seq(lambda hp: N_PAIRS + hp),
                  seq(lambda hp: hp), seq(lambda hp: hp),
                  pl.BlockSpec((None, N_HEADS, s), lambda b, hp, j: (b, 0, 0)), pl.BlockSpec(memory_space=pl.ANY)],
        out_specs=[pl.BlockSpec((s, PAIR_WIDTH), lambda b, hp, j: (b, N_PAIRS + hp)),
                   pl.BlockSpec((None, None, 2, bk), lambda b, hp, j: (b, hp, 0, j))],
        out_shape=[jax.ShapeDtypeStruct(dz.shape, dz.dtype), jax.ShapeDtypeStruct((n_batch, N_PAIRS, 2, s), F32)],
        scratch_shapes=[pltpu.VMEM((s, LANES), F32)],
        input_output_aliases={8: 0},
        name=name, compiler_params=_params("parallel", "parallel", "arbitrary"),
    )(zf, zf, zf, o32, dy, lse, c_bc, c_row, dz)


def _dil_bias(slope, dil):
    qi = lax.broadcasted_iota(jnp.int32, (BLOCK, 2 * BLOCK), 0)
    kj = lax.broadcasted_iota(jnp.int32, (BLOCK, 2 * BLOCK), 1)
    delta = qi + BLOCK - kj
    return jnp.where((delta >= 0) & (delta <= BLOCK), (-slope * dil) * delta.astype(F32), NEG)


def _alibi_slope(hp, e):
    slope = jnp.float32(0.0)
    for k in range(N_PAIRS):
        slope = jnp.where(hp == k, jnp.float32(2.0 ** -(2 * k + e + 1)), slope)
    return slope


def _first_block_bias(bias):
    return jnp.where(lax.broadcasted_iota(jnp.int32, bias.shape, 1) < BLOCK, NEG, bias)


def _fill_bias(bias_scr, hp):
    for di, dil in enumerate(DILATIONS):
        for e in range(2):
            bias_scr[2 * di + e] = _dil_bias(_alibi_slope(hp, e), dil)


def _pair_specs(rows):
    return [pl.BlockSpec((rows, LANES), lambda b, hp, c0=c0: (b, 3 * hp + c0)) for c0 in range(3)]


def _strided(start, size, dil):
    return pl.ds(start, size) if dil == 1 else pl.ds(start, size, stride=dil)


def _for_each_block(dil, unit):
    span = BLOCK * dil
    nb = SEQ // span
    if dil == 1:
        group = 3
        assert (nb - 1) % group == 0
        unit(0, True)

        def later(g, c):
            for u in range(group):
                unit((1 + g * group + u) * span, False)
            return c

        lax.fori_loop(0, (nb - 1) // group, later, 0)
        return
    group = 4
    per = dil // group

    def firsts(g, c):
        for u in range(group):
            unit(g * group + u, True)
        return c

    lax.fori_loop(0, per, firsts, 0)
    if nb > 1:
        def later(i, c):
            for u in range(group):
                unit((1 + i // per) * span + (i % per) * group + u, False)
            return c

        lax.fori_loop(0, (nb - 1) * per, later, 0)


QUARTER = SEQ // 4


def _to_quarters(src, dst):
    for r in range(4):
        dst[r * QUARTER:(r + 1) * QUARTER, :] = src[pl.ds(r, QUARTER, stride=4), :]


def _from_quarters(src, dst):
    for r in range(4):
        dst[pl.ds(r, QUARTER, stride=4), :] = src[r * QUARTER:(r + 1) * QUARTER, :]


def _for_each_quarter_block(dil, unit):
    stride = dil // 4
    nb = QUARTER // (BLOCK * stride)

    def firsts(r, c):
        for g in range(stride):
            unit(r * QUARTER + g, True, stride)
        return c

    if stride == 1:
        for r in range(4):
            firsts(r, 0)
    else:
        lax.fori_loop(0, 4, firsts, 0)
    if nb > 1:
        def later(i, c):
            for r in range(4):
                start = r * QUARTER + (1 + i // stride) * BLOCK * stride + i % stride
                unit(pl.multiple_of(start, BLOCK) if stride == 1 else start, False, stride)
            return c

        lax.fori_loop(0, (nb - 1) * stride, later, 0)


def _mix_weights(l1, l2, l3):
    m = jnp.maximum(jnp.maximum(l1, l2), l3)
    e1, e2, e3 = jnp.exp(l1 - m), jnp.exp(l2 - m), jnp.exp(l3 - m)
    inv = 1.0 / (e1 + e2 + e3)
    return e1 * inv, e2 * inv, e3 * inv


def _dil_fwd(zd, n_batch, name):
    s = SEQ
    t = n_batch * s

    def body(q_ref, k_ref, v_ref, y_ref, l1_ref, l2_ref, l3_ref, o_scr, qkv4, o4, l4, bias_scr):
        _fill_bias(bias_scr, pl.program_id(1))
        for a, ref in enumerate((q_ref, k_ref, v_ref)):
            _to_quarters(ref, qkv4.at[a])

        def unit(srcs, start, first, stride, di, o_dst, l_dst):
            qrows = _strided(start, BLOCK, stride)
            krows = qrows if first else _strided(start - BLOCK * stride, 2 * BLOCK, stride)
            q = (srcs[0][qrows, :] * ATT_SCALE).astype(BF16)
            kc = srcs[1][krows, :].astype(BF16)
            vc = srcs[2][krows, :].astype(BF16)
            if first:
                kc, vc = jnp.concatenate([kc, kc]), jnp.concatenate([vc, vc])
            outs, lses = [], []
            for e in range(2):
                bias = _first_block_bias(bias_scr[2 * di + e]) if first else bias_scr[2 * di + e]
                sc = _dot_nt(jnp.where(_head_mask(e), q, jnp.zeros_like(q)), kc) + bias
                m = jnp.max(sc, axis=1, keepdims=True)
                pe = jnp.exp(sc - m)
                l = jnp.sum(pe, axis=1, keepdims=True)
                outs.append(_dot((pe * (1.0 / l)).astype(BF16), vc))
                lses.append(m + jnp.log(l))
            o_dst[qrows, :] = jnp.where(_head_mask(0), outs[0], outs[1])
            l_dst[qrows, :] = jnp.where(_head_mask(0), lses[0], lses[1])

        token_order = (q_ref, k_ref, v_ref)
        quarters = tuple(qkv4.at[a] for a in range(3))
        _for_each_block(1, lambda start, first: unit(token_order, start, first, 1, 0, o_scr.at[0], l1_ref))
        for di in (1, 2):
            _for_each_quarter_block(DILATIONS[di], lambda start, first, stride, di=di: unit(
                quarters, start, first, stride, di, o4.at[di - 1], l4.at[di - 1]))
        for di, l_ref in ((1, l2_ref), (2, l3_ref)):
            _from_quarters(o4.at[di - 1], o_scr.at[di])
            _from_quarters(l4.at[di - 1], l_ref)
        w = _mix_weights(l1_ref[...], l2_ref[...], l3_ref[...])
        y_ref[...] = (w[0] * o_scr[0] + w[1] * o_scr[1] + w[2] * o_scr[2]).astype(BF16)

    blk = pl.BlockSpec((s, LANES), lambda b, hp: (b, hp))
    res = pl.pallas_call(
        body, grid=(n_batch, N_PAIRS),
        in_specs=_pair_specs(s),
        out_specs=[blk] * 4,
        out_shape=[jax.ShapeDtypeStruct((t, MIX_HALF), BF16)] + [jax.ShapeDtypeStruct((t, MIX_HALF), F32)] * 3,
        scratch_shapes=[pltpu.VMEM((3, s, LANES), F32), pltpu.VMEM((3, s, LANES), F32), pltpu.VMEM((2, s, LANES), F32),
                        pltpu.VMEM((2, s, LANES), F32), pltpu.VMEM((6, BLOCK, 2 * BLOCK), F32)],
        name=name, compiler_params=_params("parallel", "arbitrary"),
    )(zd, zd, zd)
    return res[0], res[1:]


def _dil_bwd(zd, dy, ya, lses, n_batch, name):
    s = SEQ
    t = n_batch * s

    def body(q_ref, k_ref, v_ref, dy_ref, ya_ref, l1_ref, l2_ref, l3_ref, dz_ref, w_scr, dy_scr, dot_scr, acc, bias_scr):
        _fill_bias(bias_scr, pl.program_id(1))
        for di, w in enumerate(_mix_weights(l1_ref[...], l2_ref[...], l3_ref[...])):
            w_scr[di] = w
        dya = dy_ref[...].astype(F32)
        prod = dya * ya_ref[...].astype(F32)
        per_head = [jnp.sum(jnp.where(_head_mask(e), prod, 0.0), axis=1, keepdims=True) for e in range(2)]
        dy_scr[...] = dya
        dot_scr[...] = jnp.where(_head_mask(0), per_head[0], per_head[1])
        acc[...] = jnp.zeros_like(acc)
        lse_refs = (l1_ref, l2_ref, l3_ref)
        for di, dil in enumerate(DILATIONS):

            def unit(start, first, di=di, dil=dil):
                qrows = _strided(start, BLOCK, dil)
                krows = qrows if first else _strided(start - BLOCK * dil, 2 * BLOCK, dil)
                q = (q_ref[qrows, :] * ATT_SCALE).astype(BF16)
                kc = k_ref[krows, :].astype(BF16)
                vc = v_ref[krows, :].astype(BF16)
                wq = w_scr.at[di][qrows, :]
                do = (wq * dy_scr[qrows, :]).astype(BF16)
                sub = wq * dot_scr[qrows, :]
                lse = lse_refs[di][qrows, :]
                dq = jnp.zeros((BLOCK, LANES), F32)
                dk = jnp.zeros((krows.size, LANES), F32)
                dv = jnp.zeros((krows.size, LANES), F32)
                for e in range(2):
                    mask = _head_mask(e)
                    lane0 = HEAD_DIM * e
                    qh = jnp.where(mask, q, jnp.zeros_like(q))
                    doh = jnp.where(mask, do, jnp.zeros_like(do))
                    bias = bias_scr[2 * di + e]
                    sc = _dot_nt(qh, kc) + (bias[:, BLOCK:] if first else bias)
                    p = jnp.exp(sc - lse[:, lane0:lane0 + 1])
                    dsb = (p * (_dot_nt(doh, vc) - sub[:, lane0:lane0 + 1])).astype(BF16)
                    dq = dq + _dot(dsb, jnp.where(mask, kc, jnp.zeros_like(kc)))
                    dk = dk + _dot_tn(dsb, qh)
                    dv = dv + _dot_tn(p.astype(BF16), doh)
                acc.at[0][qrows, :] += dq * ATT_SCALE
                acc.at[1][krows, :] += dk
                acc.at[2][krows, :] += dv

            _for_each_block(dil, unit)
        for k in range(3):
            dz_ref[:, k * LANES:(k + 1) * LANES] = acc[k].astype(BF16)

    blk = pl.BlockSpec((s, LANES), lambda b, hp: (b, hp))
    pair = pl.BlockSpec((s, PAIR_WIDTH), lambda b, hp: (b, hp))
    return pl.pallas_call(
        body, grid=(n_batch, N_PAIRS),
        in_specs=_pair_specs(s) + [blk] * 5,
        out_specs=pair,
        out_shape=jax.ShapeDtypeStruct((t, 2 * 3 * MIX_HALF), BF16),
        scratch_shapes=[pltpu.VMEM((3, s, LANES), F32), pltpu.VMEM((s, LANES), F32), pltpu.VMEM((s, LANES), F32),
                        pltpu.VMEM((3, s, LANES), F32), pltpu.VMEM((6, BLOCK, 2 * BLOCK), F32)],
        name=name, compiler_params=_params("parallel", "arbitrary"),
    )(zd, zd, zd, dy, ya, *lses)


X_BQ = 2048


def _xattn_probs(q, k):
    sc = _dot_nt(q, k) * X_SCALE
    pe = jnp.exp(sc - jnp.max(sc, axis=1, keepdims=True))
    return pe / jnp.sum(pe, axis=1, keepdims=True)


def _xattn_fwd(qx, kx, vx, n_batch, name):
    nq = SEQ // X_BQ

    def body(q_ref, k_ref, v_ref, o_ref):
        p = _xattn_probs(q_ref[...], k_ref[...])
        o_ref[...] = _dot(p.astype(BF16), v_ref[...]).astype(BF16)

    qblk = pl.BlockSpec((X_BQ, X_HEAD_DIM), lambda b, h, i: (b * nq + i, h))
    kblk = pl.BlockSpec((N_MEM, X_HEAD_DIM), lambda b, h, i: (b, h))
    return pl.pallas_call(
        body, grid=(n_batch, X_HEADS, nq), in_specs=[qblk, kblk, kblk], out_specs=qblk,
        out_shape=jax.ShapeDtypeStruct(qx.shape, BF16),
        name=name, compiler_params=_params("parallel", "parallel", "arbitrary"),
    )(qx, kx, vx)


def _xattn_bwd(qx, kx, vx, dox, n_batch, name):
    nq = SEQ // X_BQ

    def body(q_ref, k_ref, v_ref, do_ref, dq_ref, dk_ref, dv_ref, dk_acc, dv_acc):
        i = pl.program_id(2)

        @pl.when(i == 0)
        def _():
            dk_acc[...] = jnp.zeros_like(dk_acc)
            dv_acc[...] = jnp.zeros_like(dv_acc)

        q, k, do = q_ref[...], k_ref[...], do_ref[...]
        p = _xattn_probs(q, k)
        dp = _dot_nt(do, v_ref[...])
        dsb = (p * (dp - jnp.sum(p * dp, axis=1, keepdims=True))).astype(BF16)
        dq_ref[...] = (_dot(dsb, k) * X_SCALE).astype(BF16)
        dk_acc[...] += _dot_tn(dsb, q) * X_SCALE
        dv_acc[...] += _dot_tn(p.astype(BF16), do)

        @pl.when(i == nq - 1)
        def _():
            dk_ref[...] = dk_acc[...].astype(BF16)
            dv_ref[...] = dv_acc[...].astype(BF16)

    qblk = pl.BlockSpec((X_BQ, X_HEAD_DIM), lambda b, h, i: (b * nq + i, h))
    kblk = pl.BlockSpec((N_MEM, X_HEAD_DIM), lambda b, h, i: (b, h))
    return pl.pallas_call(
        body, grid=(n_batch, X_HEADS, nq), in_specs=[qblk, kblk, kblk, qblk], out_specs=[qblk, kblk, kblk],
        out_shape=[jax.ShapeDtypeStruct(qx.shape, BF16), jax.ShapeDtypeStruct(kx.shape, BF16), jax.ShapeDtypeStruct(kx.shape, BF16)],
        scratch_shapes=[pltpu.VMEM((N_MEM, X_HEAD_DIM), F32)] * 2,
        name=name, compiler_params=_params("parallel", "parallel", "arbitrary"),
    )(qx, kx, vx, dox)


def _adamw(w, g, m, v, name, rows):
    r, c = w.shape
    assert r % rows == 0, (name, w.shape, rows)

    def body(w_ref, g_ref, m_ref, v_ref, d_ref, nm_ref, nv_ref):
        gv = g_ref[...]
        m1 = ADAM_B1 * m_ref[...] + (1.0 - ADAM_B1) * gv
        v1 = ADAM_B2 * v_ref[...] + (1.0 - ADAM_B2) * jnp.square(gv)
        m_hat = m1 / (1.0 - ADAM_B1 ** ADAM_STEP)
        v_hat = v1 / (1.0 - ADAM_B2 ** ADAM_STEP)
        d_ref[...] = -ADAM_LR * (m_hat / (jnp.sqrt(v_hat) + ADAM_EPS) + ADAM_WD * w_ref[...])
        nm_ref[...] = m1
        nv_ref[...] = v1

    blk = pl.BlockSpec((rows, c), lambda i: (i, 0))
    return pl.pallas_call(
        body, grid=(r // rows,), in_specs=[blk] * 4, out_specs=[blk] * 3,
        out_shape=[jax.ShapeDtypeStruct((r, c), F32)] * 3,
        name=name, compiler_params=_params("arbitrary"),
    )(w, g, m, v)


def _relu2(acc):
    a = jnp.maximum(acc, 0.0)
    return acc, a * a


def _relu2_bwd(acc, u):
    return (2.0 * jnp.maximum(u.astype(F32), 0.0) * acc,)


def _local_step(x, mem, target, vecs, w_in, late_weights, on_grads=None):
    n_batch = x.shape[0]
    t = n_batch * SEQ
    x0 = x.reshape(t, D_MODEL)
    mem2 = mem.reshape(n_batch * N_MEM, D_MODEL)
    tgt = target.reshape(t, D_MODEL)

    half = 3 * MIX_HALF
    w_qkv = jnp.concatenate([_pair_major(w_in[:, :half]), _pair_major(w_in[:, half:QKV_WIDTH])], axis=1)
    w_gate = jnp.pad(w_in[:, QKV_WIDTH:], ((0, 0), (0, GATE_PAD - N_HEADS)))
    b_pad = jnp.pad(vecs["b_forget"], (0, GATE_PAD - N_HEADS)).reshape(1, GATE_PAD)

    h1, zd, zf, gate = _in_proj(x0, vecs["g_mix"], jnp.concatenate([w_qkv, w_gate], axis=1), "in_proj")
    mn = _rmsnorm(mem2, vecs["g_mem"], "norm_mem")
    c_bc, c_row, sg = _gate_fwd(gate, b_pad, n_batch, "gate_fwd")
    ya, lses = _dil_fwd(zd, n_batch, "dil_fwd")
    yf, of32, lse_f = _fox_fwd(zf, c_bc, c_row, n_batch, "fox_fwd")
    wts = late_weights(yf)
    w_out = wts["w_out"]
    x1, h2 = _matmul_res_norm([ya, yf], [w_out[:MIX_HALF], w_out[MIX_HALF:]], x0, vecs["g_xattn"], "out")
    qx = _matmul(h2, wts["w_xq"], "xq", out_dtypes=(BF16,))[0]
    kx = _matmul(mn, wts["w_xk"], "xk", out_dtypes=(BF16,))[0]
    vx = _matmul(mn, wts["w_xv"], "xv", out_dtypes=(BF16,))[0]
    ox = _xattn_fwd(qx, kx, vx, n_batch, "xattn_fwd")
    x2, h3 = _matmul_res_norm([ox], [wts["w_xo"]], x1, vecs["g_mlp"], "xo")
    u, a2 = _matmul(h3, wts["w_up"], "mlp_up", out_dtypes=(BF16, BF16), epilogue=_relu2, tn=1024)
    loss, dx3, dx3b, dg_final = _loss_bwd(a2, wts["w_down"], x2, vecs["g_final"], tgt, "mlp_down_loss")

    du = _matmul(dx3b, wts["w_down"], "mlp_down_bwd", out_dtypes=(BF16,), extras=(u,), epilogue=_relu2_bwd, tn=1024, w_t=True)[0]
    shards = (N_CHIPS, 2 * D_MODEL, D_MODEL)
    g_mlp = _matmul_tn(h3, du, "gw_up", packed=(shards, lambda i, j: (j, 0, 0), None))
    g_mlp = _matmul_tn(a2, dx3b, "gw_down", packed=(shards, lambda i, j: (i, 1, 0), g_mlp))
    gw_up = g_mlp[:, :D_MODEL].transpose(1, 0, 2).reshape(D_MODEL, D_FF)
    gw_down = g_mlp[:, D_MODEL:].reshape(D_FF, D_MODEL)
    token = on_grads("mlp", g_mlp) if on_grads else None
    dx2, dx2b, dg_mlp = _matmul_rms_bwd([du], [wts["w_up"]], x2, vecs["g_mlp"], dx3, "mlp_up_bwd", after=token)

    gw_xo = _matmul_tn(ox, dx2b, "gw_xo")
    dox = _matmul(dx2b, wts["w_xo"], "xo_bwd", out_dtypes=(BF16,), w_t=True)[0]
    dqx, dkx, dvx = _xattn_bwd(qx, kx, vx, dox, n_batch, "xattn_bwd")
    gw_xq = _matmul_tn(h2, dqx, "gw_xq")
    gw_xk = _matmul_tn(mn, dkx, "gw_xk")
    gw_xv = _matmul_tn(mn, dvx, "gw_xv")
    dmn = _matmul(dkx, wts["w_xk"], "xk_bwd", w_t=True)[0]
    dmn = _matmul_res(dvx, wts["w_xv"], dmn, "xv_bwd", w_t=True)
    _, _, dg_mem = _rms_bwd(mem2, dmn, vecs["g_mem"], None, "norm_mem_bwd")
    dx1, dx1b, dg_xattn = _matmul_rms_bwd([dqx], [wts["w_xq"]], x1, vecs["g_xattn"], dx2, "xq_bwd")

    gw_out = jnp.concatenate([_matmul_tn(ya, dx1b, "gw_out_a"), _matmul_tn(yf, dx1b, "gw_out_f")], axis=0)
    token = on_grads("mid", dict(w_out=gw_out, w_xq=gw_xq, w_xk=gw_xk, w_xv=gw_xv, w_xo=gw_xo)) if on_grads else None
    dy = _matmul(dx1b, w_out, "out_bwd", out_dtypes=(BF16,), w_t=True, after=token)[0]
    dz = _dil_bwd(zd, dy, ya, lses, n_batch, "dil_bwd")
    dz, dc = _fox_bwd(zf, of32, dy, lse_f, c_bc, c_row, dz, n_batch, "fox_bwd")
    dzg, db = _gate_bwd(dc.reshape(n_batch, N_HEADS, SEQ), sg, "gate_bwd")
    gw_pm = _matmul_tn(h1, dz, "gw_in_qkv")
    gw_in = jnp.concatenate([_pair_major_inv(gw_pm[:, :half]), _pair_major_inv(gw_pm[:, half:]),
                             _matmul_tn(h1, dzg, "gw_in_gate")[:, :N_HEADS]], axis=1)
    dx0, _, dg_mix = _matmul_rms_bwd([dz, dzg], [w_qkv, w_gate], x0, vecs["g_mix"], dx1, "in_bwd")

    gw = dict(w_in=gw_in, w_out=gw_out, w_xq=gw_xq, w_xk=gw_xk, w_xv=gw_xv, w_xo=gw_xo, w_up=gw_up, w_down=gw_down)
    gv = dict(g_mix=dg_mix, g_xattn=dg_xattn, g_mem=dg_mem, g_mlp=dg_mlp, g_final=dg_final, b_forget=db)
    return loss, dx0.reshape(x.shape), gw, gv


MESH = pl.DeviceIdType.MESH
ANY = pl.BlockSpec(memory_space=pl.ANY)


def _place():
    x, y, c = lax.axis_index("x"), lax.axis_index("y"), lax.axis_index("c")
    other_chips = [(1 - x, y), (x, 1 - y), (1 - x, 1 - y)]
    return x, y, c, other_chips


def _my_chip():
    return 2 * lax.axis_index("x") + lax.axis_index("y")


def _halves(rows, c, align):
    half = rows // 2
    assert rows % (2 * align) == 0, rows
    return pl.ds(pl.multiple_of(c * half, align), half), pl.ds(pl.multiple_of((1 - c) * half, align), half)


def _place_own(wall, pack):
    return lax.dynamic_update_slice(wall, pack[None], (_my_chip(), 0, 0))


def _gather(pack, name, after):
    def body(p_ref, after_ref, out_ref, send_sems, recv_sems, pass_send, pass_recv):
        del after_ref
        x, y, c, chips = _place()
        me = 2 * x + y
        mine, theirs = _halves(pack.shape[0], c, 16)

        def from_chip(k, chip, rows):
            src = out_ref.at[2 * chip[0] + chip[1], rows]
            return pltpu.make_async_remote_copy(src_ref=src, dst_ref=src, send_sem=send_sems.at[k], recv_sem=recv_sems.at[k],
                                                device_id=(chip[0], chip[1], c), device_id_type=MESH)

        def passed(k, chip, rows):
            src = out_ref.at[2 * chip[0] + chip[1], rows]
            return pltpu.make_async_remote_copy(src_ref=src, dst_ref=src, send_sem=pass_send.at[k], recv_sem=pass_recv.at[k],
                                                device_id=(x, y, 1 - c), device_id_type=MESH)

        sends = []
        for k, chip in enumerate(chips):
            cp = pltpu.make_async_remote_copy(src_ref=p_ref.at[mine], dst_ref=out_ref.at[me, mine], send_sem=send_sems.at[k],
                                              recv_sem=recv_sems.at[k], device_id=(chip[0], chip[1], c), device_id_type=MESH)
            cp.start()
            sends.append(cp)
        for k, chip in enumerate(chips):
            from_chip(k, chip, mine).wait_recv()
            cp = passed(k, chip, mine)
            cp.start()
            sends.append(cp)
        for k, chip in enumerate(chips):
            passed(k, chip, theirs).wait_recv()
        for cp in sends:
            cp.wait_send()

    wall = pl.pallas_call(
        body, in_specs=[ANY, ANY], out_specs=ANY,
        out_shape=jax.ShapeDtypeStruct((N_CHIPS,) + pack.shape, pack.dtype),
        scratch_shapes=[pltpu.SemaphoreType.DMA((3,))] * 4,
        name=name,
    )(pack, after)
    return _place_own(wall, pack)


HBM = pl.BlockSpec(memory_space=pltpu.HBM)
SEM = pl.BlockSpec(memory_space=pltpu.SEMAPHORE)
SPLIT_COPY = pltpu.CompilerParams(has_side_effects=pltpu.SideEffectType.DATAFLOW_SIDE_EFFECTING)


def _in_hbm(a):
    return pltpu.with_memory_space_constraint(a, pltpu.HBM)


def _start_call(start, src, land_shape, after, name):
    land = lax.empty(land_shape, src.dtype)

    def body(src_ref, land_ref, after_ref, send_sems, recv_sems, src_thru, land_thru, token):
        del after_ref, src_thru, land_thru
        start(src_ref, land_ref, send_sems, recv_sems)
        token[...] = jnp.zeros_like(token)

    return pl.pallas_call(
        body, name=name,
        out_shape=(pltpu.SemaphoreType.DMA((3,)), pltpu.SemaphoreType.DMA((3,)), pltpu.HBM(src.shape, src.dtype),
                   pltpu.HBM(land_shape, src.dtype), jax.ShapeDtypeStruct((8, LANES), F32)),
        in_specs=(HBM, HBM, ANY), out_specs=(SEM, SEM, HBM, HBM, pl.BlockSpec(memory_space=pltpu.VMEM)),
        input_output_aliases={0: 2, 1: 3}, compiler_params=SPLIT_COPY,
    )(_in_hbm(src), _in_hbm(land), after)


def _wait_call(body, started, after, name):
    send_sems, recv_sems, src, land, _ = started
    return pl.pallas_call(
        body, name=name,
        out_shape=(pltpu.HBM(src.shape, src.dtype), pltpu.HBM(land.shape, land.dtype)),
        in_specs=(HBM, HBM, SEM, SEM, ANY), out_specs=(HBM, HBM),
        input_output_aliases={0: 0, 1: 1}, compiler_params=SPLIT_COPY,
    )(src, land, send_sems, recv_sems, after)


def _gather_copies(p_ref, wall_ref, send_sems, recv_sems):
    x, y, c, chips = _place()
    me = 2 * x + y
    mine, _ = _halves(p_ref.shape[0], c, 16)
    out, back = [], []
    for k, chip in enumerate(chips):
        peer = dict(send_sem=send_sems.at[k], recv_sem=recv_sems.at[k], device_id=(chip[0], chip[1], c), device_id_type=MESH)
        out.append(pltpu.make_async_remote_copy(src_ref=p_ref.at[mine], dst_ref=wall_ref.at[me, mine], **peer))
        slab = wall_ref.at[2 * chip[0] + chip[1], mine]
        back.append(pltpu.make_async_remote_copy(src_ref=slab, dst_ref=slab, **peer))
    return out, back


def _gather_start(pack, after, name):
    def start(p_ref, wall_ref, send_sems, recv_sems):
        for cp in _gather_copies(p_ref, wall_ref, send_sems, recv_sems)[0]:
            cp.start()

    return _start_call(start, pack, (N_CHIPS,) + pack.shape, after, name)


def _gather_wait(started, after, name):
    def body(p_ref, wall_ref, send_sems, recv_sems, after_ref, p_dead, wall_out):
        del after_ref, p_dead, wall_out
        out, back = _gather_copies(p_ref, wall_ref, send_sems, recv_sems)
        for cp_out, cp_back in zip(out, back):
            cp_out.wait_send()
            cp_back.wait_recv()

    return _wait_call(body, started, after, name)


def _pass_on(wall, name):
    def body(w_in_ref, out_ref, send_sems, recv_sems):
        del w_in_ref
        x, y, c, chips = _place()
        mine, theirs = _halves(wall.shape[1], c, 16)
        sends = []
        for k, chip in enumerate(chips):
            slab = out_ref.at[2 * chip[0] + chip[1]]
            peer = dict(send_sem=send_sems.at[k], recv_sem=recv_sems.at[k], device_id=(x, y, 1 - c), device_id_type=MESH)
            cp = pltpu.make_async_remote_copy(src_ref=slab.at[mine], dst_ref=slab.at[mine], **peer)
            cp.start()
            sends.append((cp, pltpu.make_async_remote_copy(src_ref=slab.at[theirs], dst_ref=slab.at[theirs], **peer)))
        for cp, back in sends:
            back.wait_recv()
            cp.wait_send()

    return pl.pallas_call(
        body, in_specs=[ANY], out_specs=ANY, out_shape=jax.ShapeDtypeStruct(wall.shape, wall.dtype),
        scratch_shapes=[pltpu.SemaphoreType.DMA((3,))] * 2, input_output_aliases={0: 0}, name=name,
    )(wall)


def _swap_halves(g, name):
    half = g.shape[1] // 2

    def body(g_ref, out_ref, send_sem, recv_sem):
        x, y, c, _ = _place()
        _, theirs = _halves(g.shape[1], c, 8)
        cp = pltpu.make_async_remote_copy(src_ref=g_ref.at[:, theirs], dst_ref=out_ref, send_sem=send_sem, recv_sem=recv_sem,
                                          device_id=(x, y, 1 - c), device_id_type=MESH)
        cp.start()
        cp.wait()

    return pl.pallas_call(
        body, in_specs=[ANY], out_specs=ANY,
        out_shape=jax.ShapeDtypeStruct((N_CHIPS, half, D_MODEL), F32),
        scratch_shapes=[pltpu.SemaphoreType.DMA, pltpu.SemaphoreType.DMA],
        name=name,
    )(g)


def _core_index():
    return lax.axis_index("c").astype(jnp.int32).reshape(1)


def _row_tile(half):
    tile = max(t for t in range(16, 1025, 16) if half % t == 0)
    return tile, half // tile


def _add_sibling(g, got, name):
    half = g.shape[1] // 2
    tile, n_tiles = _row_tile(half)

    def body(c_ref, g_ref, got_ref, o_ref):
        o_ref[...] = (g_ref[...] + got_ref[...]).astype(BF16)

    blk = pl.BlockSpec((None, tile, D_MODEL), lambda s, i, c_ref: (s, i, 0))
    return pl.pallas_call(
        body,
        grid_spec=pltpu.PrefetchScalarGridSpec(
            num_scalar_prefetch=1, grid=(N_CHIPS, n_tiles),
            in_specs=[pl.BlockSpec((None, tile, D_MODEL), lambda s, i, c_ref: (s, c_ref[0] * n_tiles + i, 0)), blk],
            out_specs=blk),
        out_shape=jax.ShapeDtypeStruct((N_CHIPS, half, D_MODEL), BF16),
        name=name, compiler_params=_params("arbitrary", "arbitrary"),
    )(_core_index(), g, got)


def _exchange_copies(p_ref, land_ref, send_sems, recv_sems):
    x, y, c, chips = _place()
    me = 2 * x + y
    out, back = [], []
    for k, chip in enumerate(chips):
        peer = dict(send_sem=send_sems.at[k], recv_sem=recv_sems.at[k], device_id=(chip[0], chip[1], c), device_id_type=MESH)
        out.append(pltpu.make_async_remote_copy(src_ref=p_ref.at[2 * chip[0] + chip[1]], dst_ref=land_ref.at[me], **peer))
        slab = land_ref.at[2 * chip[0] + chip[1]]
        back.append(pltpu.make_async_remote_copy(src_ref=slab, dst_ref=slab, **peer))
    return out, back


def _with_own(got, part):
    me = _my_chip()
    return lax.dynamic_update_slice(got, lax.dynamic_slice(part, (me, 0, 0), (1,) + part.shape[1:]), (me, 0, 0))


def _exchange_chips(part, name):
    def body(p_ref, out_ref, send_sems, recv_sems):
        out, back = _exchange_copies(p_ref, out_ref, send_sems, recv_sems)
        for cp in out:
            cp.start()
        for cp in back:
            cp.wait_recv()
        for cp in out:
            cp.wait_send()

    got = pl.pallas_call(
        body, in_specs=[ANY], out_specs=ANY,
        out_shape=jax.ShapeDtypeStruct(part.shape, part.dtype),
        scratch_shapes=[pltpu.SemaphoreType.DMA((3,)), pltpu.SemaphoreType.DMA((3,))],
        name=name,
    )(part)
    return _with_own(got, part)


def _exchange_start(part, name):
    def start(p_ref, land_ref, send_sems, recv_sems):
        for cp in _exchange_copies(p_ref, land_ref, send_sems, recv_sems)[0]:
            cp.start()

    return _start_call(start, part, part.shape, _core_index(), name)


def _exchange_wait(started, after, name):
    def body(p_ref, land_ref, send_sems, recv_sems, after_ref, p_dead, land_out):
        del after_ref, p_dead, land_out
        out, back = _exchange_copies(p_ref, land_ref, send_sems, recv_sems)
        for cp_out, cp_back in zip(out, back):
            cp_out.wait_send()
            cp_back.wait_recv()

    part, got = _wait_call(body, started, after, name)
    return _with_own(got, part)


def _sum_chips(parts, name):
    half = parts.shape[1]
    tile, n_tiles = _row_tile(half)

    def body(c_ref, p0, p1, p2, p3, o_ref):
        f32 = lambda p: p[...].astype(F32)
        o_ref[...] = ((f32(p0) + f32(p1)) + f32(p2)) + f32(p3)

    def slab(s):
        return pl.BlockSpec((None, tile, D_MODEL), lambda i, c_ref, s=s: (s, i, 0))

    return pl.pallas_call(
        body,
        grid_spec=pltpu.PrefetchScalarGridSpec(
            num_scalar_prefetch=1, grid=(n_tiles,),
            in_specs=[slab(s) for s in range(N_CHIPS)],
            out_specs=pl.BlockSpec((None, tile, D_MODEL), lambda i, c_ref: (c_ref[0], i, 0))),
        out_shape=jax.ShapeDtypeStruct((2, half, D_MODEL), F32),
        name=name, compiler_params=_params("arbitrary"),
    )(_core_index(), parts, parts, parts, parts)


def _share_halves(halves, name):
    def body(h_ref, out_ref, send_sem, recv_sem):
        del h_ref
        x, y, c, _ = _place()
        cp = pltpu.make_async_remote_copy(src_ref=out_ref.at[c], dst_ref=out_ref.at[c], send_sem=send_sem, recv_sem=recv_sem,
                                          device_id=(x, y, 1 - c), device_id_type=MESH)
        cp.start()
        pltpu.make_async_remote_copy(src_ref=out_ref.at[1 - c], dst_ref=out_ref.at[1 - c], send_sem=send_sem, recv_sem=recv_sem,
                                     device_id=(x, y, 1 - c), device_id_type=MESH).wait_recv()
        cp.wait_send()

    return pl.pallas_call(
        body, in_specs=[ANY], out_specs=ANY,
        out_shape=jax.ShapeDtypeStruct(halves.shape, halves.dtype),
        scratch_shapes=[pltpu.SemaphoreType.DMA] * 2,
        input_output_aliases={0: 0},
        name=name,
    )(halves)


def _reduce_parts(g, tag):
    return _add_sibling(g, _swap_halves(g, "swap_" + tag), "add_" + tag)


def _reduce_finish(got, tag):
    halves = _share_halves(_sum_chips(got, "sum_" + tag), "share_" + tag)
    return halves.reshape(2 * halves.shape[1], D_MODEL)


SMALL_ROWS = 8


def _allreduce_small(v):
    def body(v_ref, out_ref, buf, send_sems, recv_sems):
        x, y, c, _ = _place()
        buf[4 * x + 2 * y + c] = v_ref[...]
        sends = []
        for k in range(1, N_DEV):
            px = 1 - x if k & 4 else x
            py = 1 - y if k & 2 else y
            pc = 1 - c if k & 1 else c
            cp = pltpu.make_async_remote_copy(src_ref=v_ref, dst_ref=buf.at[4 * x + 2 * y + c], send_sem=send_sems.at[k - 1],
                                              recv_sem=recv_sems.at[k - 1], device_id=(px, py, pc), device_id_type=MESH)
            cp.start()
            sends.append((cp, 4 * px + 2 * py + pc))
        for k, (cp, peer) in enumerate(sends):
            pltpu.make_async_remote_copy(src_ref=v_ref, dst_ref=buf.at[peer], send_sem=send_sems.at[k], recv_sem=recv_sems.at[k],
                                         device_id=(x, y, c), device_id_type=MESH).wait_recv()
        for cp, _ in sends:
            cp.wait_send()
        total = buf[0]
        for d in range(1, N_DEV):
            total = total + buf[d]
        out_ref[...] = total

    vmem = pl.BlockSpec(memory_space=pltpu.VMEM)
    return pl.pallas_call(
        body, in_specs=[vmem], out_specs=vmem,
        out_shape=jax.ShapeDtypeStruct(v.shape, v.dtype),
        scratch_shapes=[pltpu.VMEM((N_DEV,) + v.shape, v.dtype), pltpu.SemaphoreType.DMA((N_DEV - 1,)),
                        pltpu.SemaphoreType.DMA((N_DEV - 1,))],
        name="allreduce_small",
    )(v)


MATRICES = ("w_in", "w_out", "w_xq", "w_xk", "w_xv", "w_xo", "w_up", "w_down")
VECTORS = ("g_mix", "g_xattn", "g_mem", "g_mlp", "g_final", "b_forget")
WEIGHT_ORDER = ("g_mix", "w_in", "b_forget", "w_out", "g_xattn", "g_mem", "w_xq", "w_xk", "w_xv", "w_xo",
                "g_mlp", "w_up", "w_down", "g_final")
GROUPS = {"mlp": ("w_up", "w_down"), "mid": ("w_out", "w_xq", "w_xk", "w_xv", "w_xo"), "in": ("w_in",)}
LATE = GROUPS["mid"] + GROUPS["mlp"]
W_IN_SHARD = IN_WIDTH // N_CHIPS
SHARD_ROWS = {"w_in": W_IN_SHARD, "w_out": 256, "w_xq": 256, "w_xk": 256, "w_xv": 256, "w_xo": 256, "w_up": 1024, "w_down": 1024}
PACK_ROWS = {n: -(-r // 32) * 32 for n, r in SHARD_ROWS.items()}
ADAM_ROWS = 128


def _pack(parts, names):
    return jnp.concatenate([jnp.pad(parts[n], ((0, PACK_ROWS[n] - SHARD_ROWS[n]), (0, 0))) for n in names], axis=0)


def _unpack(a, names):
    out, pos = {}, 0
    for n in names:
        out[n] = a[..., pos:pos + SHARD_ROWS[n], :]
        pos += PACK_ROWS[n]
    return out


def _full_weights(wall, names):
    cols = lambda a: a.transpose(1, 0, 2).reshape(a.shape[1], -1)
    rows = lambda a: a.reshape(-1, a.shape[-1])
    out = {}
    for n, a in _unpack(wall, names).items():
        if n == "w_in":
            out[n] = cols(a.reshape(N_CHIPS, D_MODEL, W_IN_SHARD))
        else:
            out[n] = cols(a) if n == "w_up" else rows(a)
    return out


def _shard_of(g, name, s):
    if name == "w_in":
        return g[:, s * W_IN_SHARD:(s + 1) * W_IN_SHARD].reshape(W_IN_SHARD, D_MODEL)
    if name == "w_up":
        return g[:, s * D_MODEL:(s + 1) * D_MODEL]
    n = SHARD_ROWS[name]
    return g[s * n:(s + 1) * n]


def _pack_grads(gws, names):
    return jnp.stack([_pack({n: _shard_of(gws[n], n, s) for n in names}, names) for s in range(N_CHIPS)])


def kernel(x, mem, g_mix, w_in, b_forget, w_out, g_xattn, g_mem, w_xq, w_xk, w_xv, w_xo, g_mlp, w_up, w_down, g_final, loss_target, m_g_mix, m_w_in, m_b_forget, m_w_out, m_g_xattn, m_g_mem, m_w_xq, m_w_xk, m_w_xv, m_w_xo, m_g_mlp, m_w_up, m_w_down, m_g_final, v_g_mix, v_w_in, v_b_forget, v_w_out, v_g_xattn, v_g_mem, v_w_xq, v_w_xk, v_w_xv, v_w_xo, v_g_mlp, v_w_up, v_w_down, v_g_final):
    given = dict(locals())
    weights = {n: given[n] for n in WEIGHT_ORDER}
    vecs = {n: weights[n] for n in VECTORS}

    shard = {n: weights[n].astype(BF16) for n in MATRICES}
    shard["w_in"] = shard["w_in"].reshape(W_IN_SHARD, D_MODEL)
    in_pack, late_pack = _pack(shard, GROUPS["in"]), _pack(shard, LATE)
    in_wall = _gather(in_pack, "gather_in", in_pack)
    late = _gather_start(late_pack, in_wall, "gather_late_start")
    w_in_full = _full_weights(in_wall, GROUPS["in"])["w_in"]

    def late_weights(after):
        pack, wall = _gather_wait(late, after, "gather_late_wait")
        return _full_weights(_place_own(_pass_on(wall, "gather_late_pass"), pack), LATE)

    started = {}

    def on_grads(group, gws):
        packed = gws if group == "mlp" else _pack_grads(gws, GROUPS[group])
        part = _reduce_parts(packed, group)
        started[group] = _exchange_start(part, "exchange_%s_start" % group)
        return started[group][4]

    loss, grad_x, gw, gv = _local_step(x, mem, loss_target, vecs, w_in_full, late_weights, on_grads)

    on_grads("in", gw)
    grads, delta, new_m, new_v = {}, {}, {}, {}

    def finish(group, after):
        got = _exchange_wait(started[group], after, "exchange_%s_wait" % group)
        for n, a in _unpack(_reduce_finish(got, group), GROUPS[group]).items():
            grads[n] = a.reshape(weights[n].shape)
            delta[n], new_m[n], new_v[n] = _adamw(weights[n], grads[n], given["m_" + n], given["v_" + n], "adamw_" + n, ADAM_ROWS)
        return new_v[GROUPS[group][-1]]

    after = finish("mlp", started["in"][4])
    after = finish("mid", after)

    row = lambda a: jnp.pad(a.reshape(-1), (0, D_MODEL - a.size)).reshape(1, D_MODEL)
    small = jnp.concatenate([gv[n] for n in VECTORS[:5]] + [row(gv["b_forget"][:, 0]), row(loss[0, :1]),
                             jnp.zeros((1, D_MODEL), F32)], axis=0)
    small = _allreduce_small(small)
    for k, n in enumerate(VECTORS[:5]):
        grads[n] = small[k]
    grads["b_forget"] = small[5, :N_HEADS]
    loss_total = small[6, 0]
    finish("in", after)

    stack = lambda prefix: jnp.concatenate([row(given[prefix + n]) for n in VECTORS] + [jnp.zeros((2, D_MODEL), F32)], axis=0)
    g_small = jnp.concatenate([small[:6], jnp.zeros((2, D_MODEL), F32)], axis=0)
    d, m1, v1 = _adamw(stack(""), g_small, stack("m_"), stack("v_"), "adamw_vectors", SMALL_ROWS)
    for k, n in enumerate(VECTORS):
        width = weights[n].shape[0]
        delta[n], new_m[n], new_v[n] = d[k, :width], m1[k, :width], v1[k, :width]

    return (loss_total, grad_x, *[grads[n] for n in WEIGHT_ORDER], *[delta[n] for n in WEIGHT_ORDER],
            *[new_m[n] for n in WEIGHT_ORDER], *[new_v[n] for n in WEIGHT_ORDER])
```

```python
import functools
import math

import jax
import jax.numpy as jnp
from jax import lax
from jax.experimental import pallas as pl
from jax.experimental.pallas import tpu as pltpu

F32 = jnp.float32
BF16 = jnp.bfloat16

D_MODEL = 1024
SEQ = 2048
N_MEM = 256
HEAD_DIM = 64
N_HEADS = 8
MIX_HALF = N_HEADS * HEAD_DIM
QKV_WIDTH = 6 * MIX_HALF
IN_WIDTH = QKV_WIDTH + N_HEADS
GATE_PAD = 128
BLOCK = 128
DILATIONS = (1, 4, 16)
X_HEADS = 4
X_HEAD_DIM = 256
D_FF = 4096
EPS = 1e-6
NEG = -1e30
ATT_SCALE = 1.0 / math.sqrt(HEAD_DIM)
X_SCALE = 1.0 / math.sqrt(X_HEAD_DIM)
LANES = 128
N_CHIPS = 4
N_DEV = 8

ADAM_LR = 0.001
ADAM_B1 = 0.9
ADAM_B2 = 0.999
ADAM_EPS = 1e-08
ADAM_WD = 0.01
ADAM_STEP = 10

VMEM_LIMIT = 48 * 1024 * 1024


def _params(*sem):
    return pltpu.CompilerParams(dimension_semantics=sem or None, vmem_limit_bytes=VMEM_LIMIT)


def _dot(a, b):
    return jnp.dot(a, b, preferred_element_type=F32)


def _dot_nt(a, b):
    return lax.dot_general(a, b, (((1,), (1,)), ((), ())), preferred_element_type=F32)


def _dot_tn(a, b):
    return lax.dot_general(a, b, (((0,), (0,)), ((), ())), preferred_element_type=F32)


def _dot_exact(x, e):
    hi = x.astype(BF16)
    r1 = x - hi.astype(F32)
    mid = r1.astype(BF16)
    lo = (r1 - mid.astype(F32)).astype(BF16)
    return _dot(hi, e) + _dot(mid, e) + _dot(lo, e)


def _head_mask(e):
    lane = lax.broadcasted_iota(jnp.int32, (1, LANES), 1)
    return (lane >= HEAD_DIM * e) & (lane < HEAD_DIM * (e + 1))


def _matmul(a, w, name, out_dtypes=(F32,), extras=(), epilogue=None, tm=1024, tn=512, w_t=False, after=None):
    m, k = a.shape
    n = w.shape[0] if w_t else w.shape[1]
    tm, tn = min(tm, m), min(tn, n)
    assert m % tm == 0 and n % tn == 0, (name, a.shape, w.shape)
    n_ex = len(extras)
    order = () if after is None else (after,)

    def body(a_ref, w_ref, *rest):
        rest = rest[len(order):]
        acc = (_dot_nt if w_t else _dot)(a_ref[...], w_ref[...])
        res = (acc,) if epilogue is None else epilogue(acc, *[r[...] for r in rest[:n_ex]])
        for o_ref, r in zip(rest[n_ex:], res):
            o_ref[...] = r.astype(o_ref.dtype)

    tile = pl.BlockSpec((tm, tn), lambda i, j: (i, j))
    w_spec = pl.BlockSpec((tn, k), lambda i, j: (j, 0)) if w_t else pl.BlockSpec((k, tn), lambda i, j: (0, j))
    return pl.pallas_call(
        body, grid=(m // tm, n // tn),
        in_specs=[pl.BlockSpec((tm, k), lambda i, j: (i, 0)), w_spec] + [pl.BlockSpec(memory_space=pl.ANY)] * len(order) + [tile] * n_ex,
        out_specs=[tile] * len(out_dtypes),
        out_shape=[jax.ShapeDtypeStruct((m, n), dt) for dt in out_dtypes],
        name=name, compiler_params=_params("parallel", "arbitrary"),
    )(a, w, *order, *extras)


def _matmul_res(a, w, res, name, w_t=False):
    return _matmul(a, w, name, extras=(res,), epilogue=lambda acc, r: (r + acc,), w_t=w_t)[0]


def _matmul_tn(x, y, name, tm=1024, tn=1024, tk=512, packed=None):
    t, m = x.shape
    _, n = y.shape
    tm, tn, tk = min(tm, m), min(tn, n), min(tk, t)
    assert m % tm == 0 and n % tn == 0 and t % tk == 0, (name, x.shape, y.shape)
    shape, place, into = packed or ((m, n), None, None)

    def body(x_ref, y_ref, *rest):
        o_ref = rest[-1]

        @pl.when(pl.program_id(2) == 0)
        def _():
            o_ref[...] = jnp.zeros_like(o_ref)

        o_ref[...] += _dot_tn(x_ref[...], y_ref[...])

    out_spec = (pl.BlockSpec((tm, tn), lambda i, j, k: (i, j)) if place is None
                else pl.BlockSpec((None, tm, tn), lambda i, j, k: place(i, j)))
    return pl.pallas_call(
        body, grid=(m // tm, n // tn, t // tk),
        in_specs=[pl.BlockSpec((tk, tm), lambda i, j, k: (k, i)), pl.BlockSpec((tk, tn), lambda i, j, k: (k, j))]
        + ([] if into is None else [pl.BlockSpec(memory_space=pl.ANY)]),
        out_specs=out_spec, out_shape=jax.ShapeDtypeStruct(shape, F32),
        input_output_aliases={} if into is None else {2: 0},
        name=name, compiler_params=_params("parallel", "parallel", "arbitrary"),
    )(x, y, *(() if into is None else (into,)))


def _rmsnorm(x, g, name, tm=512):
    t, d = x.shape
    tm = min(tm, t)

    def body(x_ref, g_ref, h_ref):
        xv = x_ref[...]
        r = lax.rsqrt(jnp.mean(xv * xv, axis=-1, keepdims=True) + EPS)
        h_ref[...] = (xv * r * g_ref[...]).astype(BF16)

    return pl.pallas_call(
        body, grid=(t // tm,),
        in_specs=[pl.BlockSpec((tm, d), lambda i: (i, 0)), pl.BlockSpec((1, d), lambda i: (0, 0))],
        out_specs=pl.BlockSpec((tm, d), lambda i: (i, 0)),
        out_shape=jax.ShapeDtypeStruct((t, d), BF16),
        name=name, compiler_params=_params("arbitrary"),
    )(x, g.reshape(1, d))


def _in_proj(x, g, w_all, name, tm=512):
    t, d = x.shape
    half = 3 * MIX_HALF

    def body(x_ref, g_ref, w_ref, h_ref, zd_ref, zf_ref, gate_ref):
        xv = x_ref[...]
        r = lax.rsqrt(jnp.mean(xv * xv, axis=-1, keepdims=True) + EPS)
        h = (xv * r * g_ref[...]).astype(BF16)
        h_ref[...] = h
        zd_ref[...] = _dot(h, w_ref[:, 0:half])
        zf_ref[...] = _dot(h, w_ref[:, half:2 * half]).astype(BF16)
        gate_ref[...] = _dot(h, w_ref[:, 2 * half:])

    row = lambda width: pl.BlockSpec((tm, width), lambda i: (i, 0))
    return pl.pallas_call(
        body, grid=(t // tm,),
        in_specs=[row(d), pl.BlockSpec((1, d), lambda i: (0, 0)), pl.BlockSpec(w_all.shape, lambda i: (0, 0))],
        out_specs=[row(d), row(half), row(half), row(GATE_PAD)],
        out_shape=[jax.ShapeDtypeStruct((t, d), BF16), jax.ShapeDtypeStruct((t, half), F32),
                   jax.ShapeDtypeStruct((t, half), BF16), jax.ShapeDtypeStruct((t, GATE_PAD), F32)],
        name=name, compiler_params=_params("arbitrary"),
    )(x, g.reshape(1, d), w_all)


def _rms_bwd_tile(xv, dh, g):
    d = xv.shape[-1]
    r = lax.rsqrt(jnp.mean(xv * xv, axis=-1, keepdims=True) + EPS)
    dyg = dh * g
    proj = jnp.sum(dyg * xv, axis=-1, keepdims=True)
    dx = r * dyg - xv * (r * r * r * (1.0 / d)) * proj
    return dx, dh * (xv * r)


def _rms_bwd(x, dh, g, dres, name, tm=512):
    t, d = x.shape
    tm = min(tm, t)
    has_res = dres is not None

    def body(x_ref, dh_ref, g_ref, *rest):
        if has_res:
            res_ref, dx_ref, dxb_ref, dg_ref = rest
        else:
            dx_ref, dxb_ref, dg_ref = rest
        dx, dg_rows = _rms_bwd_tile(x_ref[...], dh_ref[...], g_ref[...])
        if has_res:
            dx = res_ref[...] + dx
        dx_ref[...] = dx
        dxb_ref[...] = dx.astype(BF16)

        @pl.when(pl.program_id(0) == 0)
        def _():
            dg_ref[...] = jnp.zeros_like(dg_ref)

        dg_ref[...] += jnp.sum(dg_rows, axis=0, keepdims=True)

    row = pl.BlockSpec((tm, d), lambda i: (i, 0))
    vec = pl.BlockSpec((1, d), lambda i: (0, 0))
    return pl.pallas_call(
        body, grid=(t // tm,),
        in_specs=[row, row, vec] + ([row] if has_res else []),
        out_specs=[row, row, vec],
        out_shape=[jax.ShapeDtypeStruct((t, d), F32), jax.ShapeDtypeStruct((t, d), BF16), jax.ShapeDtypeStruct((1, d), F32)],
        name=name, compiler_params=_params("arbitrary"),
    )(x, dh, g.reshape(1, d), *((dres,) if has_res else ()))


def _row_dots(a_refs, w_refs, w_t):
    acc = None
    for a_ref, w_ref in zip(a_refs, w_refs):
        part = (_dot_nt if w_t else _dot)(a_ref[...], w_ref[...])
        acc = part if acc is None else acc + part
    return acc


def _row_specs(a_parts, w_parts, tm):
    specs = [pl.BlockSpec((tm, a.shape[1]), lambda i: (i, 0)) for a in a_parts]
    return specs + [pl.BlockSpec(w.shape, lambda i: (0, 0)) for w in w_parts]


def _matmul_res_norm(a_parts, w_parts, res, g, name, tm=512):
    t, d = res.shape
    n = len(a_parts)

    def body(*refs):
        res_ref, g_ref, x_ref, h_ref = refs[2 * n:]
        xv = res_ref[...] + _row_dots(refs[:n], refs[n:2 * n], False)
        x_ref[...] = xv
        r = lax.rsqrt(jnp.mean(xv * xv, axis=-1, keepdims=True) + EPS)
        h_ref[...] = (xv * r * g_ref[...]).astype(BF16)

    row = pl.BlockSpec((tm, d), lambda i: (i, 0))
    return pl.pallas_call(
        body, grid=(t // tm,),
        in_specs=_row_specs(a_parts, w_parts, tm) + [row, pl.BlockSpec((1, d), lambda i: (0, 0))],
        out_specs=[row, row],
        out_shape=[jax.ShapeDtypeStruct((t, d), F32), jax.ShapeDtypeStruct((t, d), BF16)],
        name=name, compiler_params=_params("arbitrary"),
    )(*a_parts, *w_parts, res, g.reshape(1, d))


def _matmul_rms_bwd(a_parts, w_parts, x, g, dres, name, tm=512, after=None):
    t, d = x.shape
    n = len(a_parts)
    order = () if after is None else (after,)

    def body(*refs):
        x_ref, g_ref, res_ref = refs[2 * n:2 * n + 3]
        dx_ref, dxb_ref, dg_ref = refs[2 * n + 3 + len(order):]
        dx, dg_rows = _rms_bwd_tile(x_ref[...], _row_dots(refs[:n], refs[n:2 * n], True), g_ref[...])
        dx = res_ref[...] + dx
        dx_ref[...] = dx
        dxb_ref[...] = dx.astype(BF16)

        @pl.when(pl.program_id(0) == 0)
        def _():
            dg_ref[...] = jnp.zeros_like(dg_ref)

        dg_ref[...] += jnp.sum(dg_rows, axis=0, keepdims=True)

    row = pl.BlockSpec((tm, d), lambda i: (i, 0))
    vec = pl.BlockSpec((1, d), lambda i: (0, 0))
    return pl.pallas_call(
        body, grid=(t // tm,),
        in_specs=_row_specs(a_parts, w_parts, tm) + [row, vec, row] + [pl.BlockSpec(memory_space=pl.ANY)] * len(order),
        out_specs=[row, row, vec],
        out_shape=[jax.ShapeDtypeStruct((t, d), F32), jax.ShapeDtypeStruct((t, d), BF16), jax.ShapeDtypeStruct((1, d), F32)],
        name=name, compiler_params=_params("arbitrary"),
    )(*a_parts, *w_parts, x, g.reshape(1, d), dres, *order)


def _loss_bwd(a, w, res, g, target, name, tm=512):
    t, d = res.shape

    def body(a_ref, w_ref, x_ref, g_ref, t_ref, loss_ref, dx_ref, dxb_ref, dg_ref):
        xv = x_ref[...] + _dot(a_ref[...], w_ref[...])
        gv = g_ref[...]
        r = lax.rsqrt(jnp.mean(xv * xv, axis=-1, keepdims=True) + EPS)
        err = xv * r * gv - t_ref[...]
        dx, dg_rows = _rms_bwd_tile(xv, err * (1.0 / d), gv)
        dx_ref[...] = dx
        dxb_ref[...] = dx.astype(BF16)

        @pl.when(pl.program_id(0) == 0)
        def _():
            dg_ref[...] = jnp.zeros_like(dg_ref)
            loss_ref[...] = jnp.zeros_like(loss_ref)

        dg_ref[...] += jnp.sum(dg_rows, axis=0, keepdims=True)
        part = jnp.sum(jnp.sum(err * err, axis=0, keepdims=True), axis=1, keepdims=True) * (0.5 / d)
        loss_ref[...] += jnp.broadcast_to(part, loss_ref.shape)

    row = pl.BlockSpec((tm, d), lambda i: (i, 0))
    vec = pl.BlockSpec((1, d), lambda i: (0, 0))
    return pl.pallas_call(
        body, grid=(t // tm,),
        in_specs=_row_specs([a], [w], tm) + [row, vec, row],
        out_specs=[pl.BlockSpec((1, LANES), lambda i: (0, 0)), row, row, vec],
        out_shape=[jax.ShapeDtypeStruct((1, LANES), F32), jax.ShapeDtypeStruct((t, d), F32),
                   jax.ShapeDtypeStruct((t, d), BF16), jax.ShapeDtypeStruct((1, d), F32)],
        name=name, compiler_params=_params("arbitrary"),
    )(a, w, res, g.reshape(1, d), target)


def _tri(upper):
    r = lax.broadcasted_iota(jnp.int32, (LANES, LANES), 0)
    c = lax.broadcasted_iota(jnp.int32, (LANES, LANES), 1)
    return jnp.where((r <= c) if upper else (r >= c), 1.0, 0.0).astype(BF16)


def _gate_fwd(gate, b_pad, n_batch, name):
    s = SEQ
    nblk = s // LANES

    def body(g_ref, b_ref, cbc_ref, crow_ref, sg_ref, ct_ref):
        gz = g_ref[...] + b_ref[...]
        logf = jnp.minimum(gz, 0.0) - jnp.log(1.0 + jnp.exp(-jnp.abs(gz)))
        logf_t = logf.T
        sg_ref[...] = (1.0 / (1.0 + jnp.exp(gz))).T[0:N_HEADS]
        upper = _tri(True)
        carry = jnp.zeros((LANES, 1), F32)
        for blk in range(nblk):
            seg = _dot_exact(logf_t[:, blk * LANES:(blk + 1) * LANES], upper) + carry
            carry = seg[:, LANES - 1:LANES]
            ct_ref[:, blk * LANES:(blk + 1) * LANES] = seg
        ct = ct_ref[...]
        crow_ref[...] = ct[0:N_HEADS]
        c_col = ct.T
        lane = lax.broadcasted_iota(jnp.int32, (1, MIX_HALF), 1)
        acc = jnp.zeros((s, MIX_HALF), F32)
        for h in range(N_HEADS):
            acc = jnp.where((lane >= HEAD_DIM * h) & (lane < HEAD_DIM * (h + 1)), c_col[:, h:h + 1], acc)
        cbc_ref[...] = acc

    return pl.pallas_call(
        body, grid=(n_batch,),
        in_specs=[pl.BlockSpec((s, GATE_PAD), lambda b: (b, 0)), pl.BlockSpec((1, GATE_PAD), lambda b: (0, 0))],
        out_specs=[pl.BlockSpec((s, MIX_HALF), lambda b: (b, 0)),
                   pl.BlockSpec((None, N_HEADS, s), lambda b: (b, 0, 0)),
                   pl.BlockSpec((None, N_HEADS, s), lambda b: (b, 0, 0))],
        out_shape=[jax.ShapeDtypeStruct((n_batch * s, MIX_HALF), F32),
                   jax.ShapeDtypeStruct((n_batch, N_HEADS, s), F32),
                   jax.ShapeDtypeStruct((n_batch, N_HEADS, s), F32)],
        scratch_shapes=[pltpu.VMEM((LANES, s), F32)],
        name=name, compiler_params=_params("arbitrary"),
    )(gate, b_pad)


def _gate_bwd(dc, sg, name):
    n_batch, _, s = dc.shape
    nblk = s // LANES

    def body(dc_ref, sg_ref, dz_ref, db_ref, dt_ref):
        lower = _tri(False)
        dcv = dc_ref[...]
        carry = jnp.zeros((N_HEADS, 1), F32)
        dt_ref[...] = jnp.zeros_like(dt_ref)
        for blk in reversed(range(nblk)):
            seg = _dot_exact(dcv[:, blk * LANES:(blk + 1) * LANES], lower) + carry
            carry = seg[:, 0:1]
            dt_ref[0:N_HEADS, blk * LANES:(blk + 1) * LANES] = seg * sg_ref[:, blk * LANES:(blk + 1) * LANES]
        dg_t = dt_ref[...]
        dz_ref[...] = dg_t.T.astype(BF16)

        @pl.when(pl.program_id(0) == 0)
        def _():
            db_ref[...] = jnp.zeros_like(db_ref)

        db_ref[...] += jnp.broadcast_to(jnp.sum(dg_t[0:N_HEADS], axis=1, keepdims=True), db_ref.shape)

    return pl.pallas_call(
        body, grid=(n_batch,),
        in_specs=[pl.BlockSpec((None, N_HEADS, s), lambda b: (b, 0, 0)), pl.BlockSpec((None, N_HEADS, s), lambda b: (b, 0, 0))],
        out_specs=[pl.BlockSpec((s, GATE_PAD), lambda b: (b, 0)), pl.BlockSpec((N_HEADS, LANES), lambda b: (0, 0))],
        out_shape=[jax.ShapeDtypeStruct((n_batch * s, GATE_PAD), BF16), jax.ShapeDtypeStruct((N_HEADS, LANES), F32)],
        scratch_shapes=[pltpu.VMEM((LANES, s), F32)],
        name=name, compiler_params=_params("arbitrary"),
    )(dc, sg)


FOX_BQ = 512
FOX_BK = 512
FOX_STRIP = 512
PAIR_WIDTH = 3 * LANES
N_PAIRS = N_HEADS // 2


def _pair_major(w):
    return w.reshape(w.shape[0], 3, N_PAIRS, LANES).transpose(0, 2, 1, 3).reshape(w.shape[0], 3 * MIX_HALF)


def _pair_major_inv(w):
    return w.reshape(w.shape[0], N_PAIRS, 3, LANES).transpose(0, 2, 1, 3).reshape(w.shape[0], 3 * MIX_HALF)


def _causal(i, j, bq, bk):
    qpos = i * bq + lax.broadcasted_iota(jnp.int32, (bq, 1), 0)
    kpos = j * bk + lax.broadcasted_iota(jnp.int32, (1, bk), 1)
    return kpos <= qpos


def _split_bf16(p):
    hi = p.astype(BF16)
    return hi, (p - hi.astype(F32)).astype(BF16)


def _fox_fwd(zf, c_bc, c_row, n_batch, name):
    s, bq, bk = SEQ, FOX_BQ, FOX_BK
    nq = s // bq
    t = n_batch * s

    n_strip = bq // FOX_STRIP

    def body(q_ref, k_ref, v_ref, cq_ref, cr_ref, o_ref, o32_ref, lse_ref):
        hp = pl.program_id(1)
        strips = [slice(r * FOX_STRIP, (r + 1) * FOX_STRIP) for r in range(n_strip)]
        chains = [(e, r) for e in range(2) for r in range(n_strip)]
        qh, cq = {}, {}
        for e, r in chains:
            q = q_ref[strips[r], :] * ATT_SCALE
            qh[e, r] = jnp.where(_head_mask(e), q, jnp.zeros_like(q))
            cq[e, r] = cq_ref[strips[r], HEAD_DIM * e:HEAD_DIM * e + 1]

        def step(i, j, carry, masked):
            rows = pl.ds(j * bk, bk)
            kj, vj = k_ref[rows, :], v_ref[rows, :]
            ck = [cr_ref[pl.ds(2 * hp + e, 1), rows] for e in range(2)]
            out = []
            scores = [_dot_nt(qh[e, r], kj) for e, r in chains]
            for n, (e, r) in enumerate(chains):
                m, l, acc = carry[3 * n:3 * n + 3]
                sc = scores[n] + (cq[e, r] - ck[e])
                if masked:
                    qpos = i * bq + r * FOX_STRIP + lax.broadcasted_iota(jnp.int32, (FOX_STRIP, 1), 0)
                    kpos = j * bk + lax.broadcasted_iota(jnp.int32, (1, bk), 1)
                    sc = jnp.where(kpos <= qpos, sc, NEG)
                m_new = jnp.maximum(m, jnp.max(sc, axis=1, keepdims=True))
                alpha = jnp.exp(m - m_new)
                p = jnp.exp(sc - m_new)
                p_hi, p_lo = _split_bf16(p)
                out += [m_new, alpha * l + jnp.sum(p, axis=1, keepdims=True), alpha * acc + (_dot(p_hi, vj) + _dot(p_lo, vj))]
            return tuple(out)

        def run(i):
            carry = (jnp.full((FOX_STRIP, 1), NEG, F32), jnp.zeros((FOX_STRIP, 1), F32), jnp.zeros((FOX_STRIP, LANES), F32)) * len(chains)
            n_clear = (i * bq) // bk
            for j in range((i * bq + bq + bk - 1) // bk):
                carry = step(i, j, carry, masked=j >= n_clear)
            for r in range(n_strip):
                outs = [carry[3 * (e * n_strip + r) + 2] / carry[3 * (e * n_strip + r) + 1] for e in range(2)]
                lses = [carry[3 * (e * n_strip + r)] + jnp.log(carry[3 * (e * n_strip + r) + 1]) for e in range(2)]
                o = jnp.where(_head_mask(0), outs[0], outs[1])
                o_ref[strips[r], :] = o.astype(BF16)
                o32_ref[strips[r], :] = o
                lse_ref[strips[r], :] = jnp.where(_head_mask(0), lses[0], lses[1])

        for k in range(nq):
            pl.when(pl.program_id(2) == k)(functools.partial(run, k))

    def col(c0):
        return lambda b, hp, i: (b, 3 * hp + c0)

    blk = pl.BlockSpec((bq, LANES), lambda b, hp, i: (b * nq + i, hp))
    return pl.pallas_call(
        body, grid=(n_batch, N_PAIRS, nq),
        in_specs=[pl.BlockSpec((bq, LANES), lambda b, hp, i: (b * nq + i, 3 * hp)),
                  pl.BlockSpec((s, LANES), col(1)), pl.BlockSpec((s, LANES), col(2)), blk,
                  pl.BlockSpec((None, N_HEADS, s), lambda b, hp, i: (b, 0, 0))],
        out_specs=[blk, blk, blk],
        out_shape=[jax.ShapeDtypeStruct((t, MIX_HALF), BF16), jax.ShapeDtypeStruct((t, MIX_HALF), F32),
                   jax.ShapeDtypeStruct((t, MIX_HALF), F32)],
        name=name, compiler_params=_params("parallel", "parallel", "arbitrary"),
    )(zf, zf, zf, c_bc, c_row)


def _fox_bwd(zf, o32, dy, lse, c_bc, c_row, dz, n_batch, name):
    s, bq, bk = SEQ, FOX_BQ, FOX_BK
    nq, nk = s // bq, s // bk

    def body(q_ref, k_ref, v_ref, o_ref, do_ref, lse_ref, cq_ref, cr_ref, dz_in, dz_ref, dc_ref, dq_acc):
        del dz_in
        hp = pl.program_id(1)

        @pl.when(pl.program_id(2) == 0)
        def _():
            dq_acc[...] = jnp.zeros_like(dq_acc)

        kj, vj = k_ref[...], v_ref[...]
        km = [jnp.where(_head_mask(e), kj, jnp.zeros_like(kj)) for e in range(2)]

        def step(i, j, ck, carry, masked):
            rows = pl.ds(i * bq, bq)
            qi, doi = q_ref[rows, :] * ATT_SCALE, do_ref[rows, :]
            prod = doi.astype(F32) * o_ref[rows, :]
            out = []
            dq = jnp.zeros((bq, LANES), F32)
            for e in range(2):
                dk_a, dv_a, dc_a = carry[3 * e:3 * e + 3]
                mask = _head_mask(e)
                lane0 = HEAD_DIM * e
                dom = jnp.where(mask, doi, jnp.zeros_like(doi))
                delta = jnp.sum(jnp.where(mask, prod, 0.0), axis=1, keepdims=True)
                sc = _dot_nt(qi, km[e]) + (cq_ref[rows, lane0:lane0 + 1] - ck[e])
                if masked:
                    sc = jnp.where(_causal(i, j, bq, bk), sc, NEG)
                p = jnp.exp(sc - lse_ref[rows, lane0:lane0 + 1])
                ds = p * (_dot_nt(dom, vj) - delta)
                dsb = ds.astype(BF16)
                dq = dq + _dot(dsb, km[e])
                out += [dk_a + _dot_tn(dsb, qi), dv_a + _dot_tn(p.astype(BF16), dom), dc_a - jnp.sum(ds, axis=0, keepdims=True)]
            dq_acc[rows, :] += dq * ATT_SCALE
            return tuple(out)

        def run(j):
            cols = pl.ds(j * bk, bk)
            ck = [cr_ref[pl.ds(2 * hp + e, 1), cols] for e in range(2)]
            carry = (jnp.zeros((bk, LANES), F32), jnp.zeros((bk, LANES), F32), jnp.zeros((1, bk), F32)) * 2
            n_diag = (j * bk + bk + bq - 1) // bq
            for i in range((j * bk) // bq, nq):
                carry = step(i, j, ck, carry, masked=i < n_diag)
            for e in range(2):
                dc_ref[e:e + 1, :] = carry[3 * e + 2]
            dz_ref[cols, LANES:2 * LANES] = jnp.where(_head_mask(0), carry[0], carry[3]).astype(BF16)
            dz_ref[cols, 2 * LANES:3 * LANES] = (carry[1] + carry[4]).astype(BF16)
            if j == nk - 1:
                dz_ref[:, 0:LANES] = dq_acc[...].astype(BF16)

        for k in range(nk):
            pl.when(pl.program_id(2) == k)(functools.partial(run, k))

    def seq(idx):
        return pl.BlockSpec((s, LANES), lambda b, hp, j: (b, idx(hp)))

    def kblk(c0):
        return pl.BlockSpec((bk, LANES), lambda b, hp, j: (b * nk + j, 3 * hp + c0))

    return pl.pallas_call(
        body, grid=(n_batch, N_PAIRS, nk),
        in_specs=[seq(lambda hp: 3 * hp), kblk(1), kblk(2), seq(lambda hp: hp), seq(lambda hp: N_PAIRS + hp),
                  seq(lambda hp: hp), seq(lambda hp: hp),
                  pl.BlockSpec((None, N_HEADS, s), lambda b, hp, j: (b, 0, 0)), pl.BlockSpec(memory_space=pl.ANY)],
        out_specs=[pl.BlockSpec((s, PAIR_WIDTH), lambda b, hp, j: (b, N_PAIRS + hp)),
                   pl.BlockSpec((None, None, 2, bk), lambda b, hp, j: (b, hp, 0, j))],
        out_shape=[jax.ShapeDtypeStruct(dz.shape, dz.dtype), jax.ShapeDtypeStruct((n_batch, N_PAIRS, 2, s), F32)],
        scratch_shapes=[pltpu.VMEM((s, LANES), F32)],
        input_output_aliases={8: 0},
        name=name, compiler_params=_params("parallel", "parallel", "arbitrary"),
    )(zf, zf, zf, o32, dy, lse, c_bc, c_row, dz)


def _dil_bias(slope, dil):
    qi = lax.broadcasted_iota(jnp.int32, (BLOCK, 2 * BLOCK), 0)
    kj = lax.broadcasted_iota(jnp.int32, (BLOCK, 2 * BLOCK), 1)
    delta = qi + BLOCK - kj
    return jnp.where((delta >= 0) & (delta <= BLOCK), (-slope * dil) * delta.astype(F32), NEG)


def _alibi_slope(hp, e):
    slope = jnp.float32(0.0)
    for k in range(N_PAIRS):
        slope = jnp.where(hp == k, jnp.float32(2.0 ** -(2 * k + e + 1)), slope)
    return slope


def _first_block_bias(bias):
    return jnp.where(lax.broadcasted_iota(jnp.int32, bias.shape, 1) < BLOCK, NEG, bias)


def _fill_bias(bias_scr, hp):
    for di, dil in enumerate(DILATIONS):
        for e in range(2):
            bias_scr[2 * di + e] = _dil_bias(_alibi_slope(hp, e), dil)


def _pair_specs(rows):
    return [pl.BlockSpec((rows, LANES), lambda b, hp, c0=c0: (b, 3 * hp + c0)) for c0 in range(3)]


def _strided(start, size, dil):
    return pl.ds(start, size) if dil == 1 else pl.ds(start, size, stride=dil)


def _for_each_block(dil, unit, group=3):
    span = BLOCK * dil
    nb = SEQ // span
    if dil == 1:
        assert (nb - 1) % group == 0
        unit(0, True)

        def later(g, c):
            for u in range(group):
                unit((1 + g * group + u) * span, False)
            return c

        if (nb - 1) // group == 1:
            later(0, 0)
        else:
            lax.fori_loop(0, (nb - 1) // group, later, 0)
        return
    group = 4
    per = dil // group

    def firsts(g, c):
        for u in range(group):
            unit(g * group + u, True)
        return c

    lax.fori_loop(0, per, firsts, 0)
    if nb > 1:
        def later(i, c):
            for u in range(group):
                unit((1 + i // per) * span + (i % per) * group + u, False)
            return c

        lax.fori_loop(0, (nb - 1) * per, later, 0)


QUARTER = SEQ // 4


def _to_quarters(src, dst):
    for r in range(4):
        dst[r * QUARTER:(r + 1) * QUARTER, :] = src[pl.ds(r, QUARTER, stride=4), :]


def _from_quarters(src, dst):
    for r in range(4):
        dst[pl.ds(r, QUARTER, stride=4), :] = src[r * QUARTER:(r + 1) * QUARTER, :]


def _for_each_quarter_block(dil, unit):
    stride = dil // 4
    nb = QUARTER // (BLOCK * stride)

    def firsts(r, c):
        for g in range(stride):
            unit(r * QUARTER + g, True, stride)
        return c

    if stride == 1:
        for r in range(4):
            firsts(r, 0)
    else:
        for r in range(4):
            firsts(r, 0)
    if nb > 1:
        assert stride == 1
        for n in range(1, nb):
            for r in range(4):
                unit(r * QUARTER + n * BLOCK, False, stride)


def _mix_weights(l1, l2, l3):
    m = jnp.maximum(jnp.maximum(l1, l2), l3)
    e1, e2, e3 = jnp.exp(l1 - m), jnp.exp(l2 - m), jnp.exp(l3 - m)
    inv = 1.0 / (e1 + e2 + e3)
    return e1 * inv, e2 * inv, e3 * inv


def _dil_fwd(zd, n_batch, name):
    s = SEQ
    t = n_batch * s

    def body(q_ref, k_ref, v_ref, y_ref, l1_ref, l2_ref, l3_ref, o_scr, qkv4, o4, l4, bias_scr):
        _fill_bias(bias_scr, pl.program_id(1))
        for a, ref in enumerate((q_ref, k_ref, v_ref)):
            _to_quarters(ref, qkv4.at[a])

        def unit(srcs, start, first, stride, di, o_dst, l_dst):
            qrows = _strided(start, BLOCK, stride)
            krows = qrows if first else _strided(start - BLOCK * stride, 2 * BLOCK, stride)
            q = (srcs[0][qrows, :] * ATT_SCALE).astype(BF16)
            kc = srcs[1][krows, :].astype(BF16)
            vc = srcs[2][krows, :].astype(BF16)
            if first:
                kc, vc = jnp.concatenate([kc, kc]), jnp.concatenate([vc, vc])
            outs, lses = [], []
            for e in range(2):
                bias = _first_block_bias(bias_scr[2 * di + e]) if first else bias_scr[2 * di + e]
                sc = _dot_nt(jnp.where(_head_mask(e), q, jnp.zeros_like(q)), kc) + bias
                m = jnp.max(sc, axis=1, keepdims=True)
                pe = jnp.exp(sc - m)
                l = jnp.sum(pe, axis=1, keepdims=True)
                outs.append(_dot((pe * (1.0 / l)).astype(BF16), vc))
                lses.append(m + jnp.log(l))
            o_dst[qrows, :] = jnp.where(_head_mask(0), outs[0], outs[1])
            l_dst[qrows, :] = jnp.where(_head_mask(0), lses[0], lses[1])

        token_order = (q_ref, k_ref, v_ref)
        quarters = tuple(qkv4.at[a] for a in range(3))
        _for_each_block(1, lambda start, first: unit(token_order, start, first, 1, 0, o_scr.at[0], l1_ref), group=15)
        for di in (1, 2):
            _for_each_quarter_block(DILATIONS[di], lambda start, first, stride, di=di: unit(
                quarters, start, first, stride, di, o4.at[di - 1], l4.at[di - 1]))
        for di, l_ref in ((1, l2_ref), (2, l3_ref)):
            _from_quarters(o4.at[di - 1], o_scr.at[di])
            _from_quarters(l4.at[di - 1], l_ref)
        w = _mix_weights(l1_ref[...], l2_ref[...], l3_ref[...])
        y_ref[...] = (w[0] * o_scr[0] + w[1] * o_scr[1] + w[2] * o_scr[2]).astype(BF16)

    blk = pl.BlockSpec((s, LANES), lambda b, hp: (b, hp))
    res = pl.pallas_call(
        body, grid=(n_batch, N_PAIRS),
        in_specs=_pair_specs(s),
        out_specs=[blk] * 4,
        out_shape=[jax.ShapeDtypeStruct((t, MIX_HALF), BF16)] + [jax.ShapeDtypeStruct((t, MIX_HALF), F32)] * 3,
        scratch_shapes=[pltpu.VMEM((3, s, LANES), F32), pltpu.VMEM((3, s, LANES), F32), pltpu.VMEM((2, s, LANES), F32),
                        pltpu.VMEM((2, s, LANES), F32), pltpu.VMEM((6, BLOCK, 2 * BLOCK), F32)],
        name=name, compiler_params=_params("parallel", "arbitrary"),
    )(zd, zd, zd)
    return res[0], res[1:]


def _dil_bwd(zd, dy, ya, lses, n_batch, name):
    s = SEQ
    t = n_batch * s

    def body(q_ref, k_ref, v_ref, dy_ref, ya_ref, l1_ref, l2_ref, l3_ref, dz_ref, w_scr, dy_scr, dot_scr, acc, bias_scr):
        _fill_bias(bias_scr, pl.program_id(1))
        for di, w in enumerate(_mix_weights(l1_ref[...], l2_ref[...], l3_ref[...])):
            w_scr[di] = w
        dya = dy_ref[...].astype(F32)
        prod = dya * ya_ref[...].astype(F32)
        per_head = [jnp.sum(jnp.where(_head_mask(e), prod, 0.0), axis=1, keepdims=True) for e in range(2)]
        dy_scr[...] = dya
        dot_scr[...] = jnp.where(_head_mask(0), per_head[0], per_head[1])
        acc[...] = jnp.zeros_like(acc)
        lse_refs = (l1_ref, l2_ref, l3_ref)
        for di, dil in enumerate(DILATIONS):

            def unit(start, first, di=di, dil=dil):
                qrows = _strided(start, BLOCK, dil)
                krows = qrows if first else _strided(start - BLOCK * dil, 2 * BLOCK, dil)
                q = (q_ref[qrows, :] * ATT_SCALE).astype(BF16)
                kc = k_ref[krows, :].astype(BF16)
                vc = v_ref[krows, :].astype(BF16)
                wq = w_scr.at[di][qrows, :]
                do = (wq * dy_scr[qrows, :]).astype(BF16)
                sub = wq * dot_scr[qrows, :]
                lse = lse_refs[di][qrows, :]
                dq = jnp.zeros((BLOCK, LANES), F32)
                dk = jnp.zeros((krows.size, LANES), F32)
                dv = jnp.zeros((krows.size, LANES), F32)
                for e in range(2):
                    mask = _head_mask(e)
                    lane0 = HEAD_DIM * e
                    qh = jnp.where(mask, q, jnp.zeros_like(q))
                    doh = jnp.where(mask, do, jnp.zeros_like(do))
                    bias = bias_scr[2 * di + e]
                    sc = _dot_nt(qh, kc) + (bias[:, BLOCK:] if first else bias)
                    p = jnp.exp(sc - lse[:, lane0:lane0 + 1])
                    dsb = (p * (_dot_nt(doh, vc) - sub[:, lane0:lane0 + 1])).astype(BF16)
                    dq = dq + _dot(dsb, jnp.where(mask, kc, jnp.zeros_like(kc)))
                    dk = dk + _dot_tn(dsb, qh)
                    dv = dv + _dot_tn(p.astype(BF16), doh)
                acc.at[0][qrows, :] += dq * ATT_SCALE
                acc.at[1][krows, :] += dk
                acc.at[2][krows, :] += dv

            for r in range(dil):
                for n in range(SEQ // (BLOCK * dil)):
                    unit(n * BLOCK * dil + r, n == 0)
        for k in range(3):
            dz_ref[:, k * LANES:(k + 1) * LANES] = acc[k].astype(BF16)

    blk = pl.BlockSpec((s, LANES), lambda b, hp: (b, hp))
    pair = pl.BlockSpec((s, PAIR_WIDTH), lambda b, hp: (b, hp))
    return pl.pallas_call(
        body, grid=(n_batch, N_PAIRS),
        in_specs=_pair_specs(s) + [blk] * 5,
        out_specs=pair,
        out_shape=jax.ShapeDtypeStruct((t, 2 * 3 * MIX_HALF), BF16),
        scratch_shapes=[pltpu.VMEM((3, s, LANES), F32), pltpu.VMEM((s, LANES), F32), pltpu.VMEM((s, LANES), F32),
                        pltpu.VMEM((3, s, LANES), F32), pltpu.VMEM((6, BLOCK, 2 * BLOCK), F32)],
        name=name, compiler_params=_params("parallel", "arbitrary"),
    )(zd, zd, zd, dy, ya, *lses)


X_BQ = 2048


def _xattn_probs(q, k):
    sc = _dot_nt(q, k) * X_SCALE
    pe = jnp.exp(sc - jnp.max(sc, axis=1, keepdims=True))
    return pe / jnp.sum(pe, axis=1, keepdims=True)


def _xattn_fwd(qx, kx, vx, n_batch, name):
    nq = SEQ // X_BQ

    def body(q_ref, k_ref, v_ref, o_ref):
        p = _xattn_probs(q_ref[...], k_ref[...])
        o_ref[...] = _dot(p.astype(BF16), v_ref[...]).astype(BF16)

    qblk = pl.BlockSpec((X_BQ, X_HEAD_DIM), lambda b, h, i: (b * nq + i, h))
    kblk = pl.BlockSpec((N_MEM, X_HEAD_DIM), lambda b, h, i: (b, h))
    return pl.pallas_call(
        body, grid=(n_batch, X_HEADS, nq), in_specs=[qblk, kblk, kblk], out_specs=qblk,
        out_shape=jax.ShapeDtypeStruct(qx.shape, BF16),
        name=name, compiler_params=_params("parallel", "parallel", "arbitrary"),
    )(qx, kx, vx)


def _xattn_bwd(qx, kx, vx, dox, n_batch, name):
    nq = SEQ // X_BQ

    def body(q_ref, k_ref, v_ref, do_ref, dq_ref, dk_ref, dv_ref, dk_acc, dv_acc):
        i = pl.program_id(2)

        @pl.when(i == 0)
        def _():
            dk_acc[...] = jnp.zeros_like(dk_acc)
            dv_acc[...] = jnp.zeros_like(dv_acc)

        q, k, do = q_ref[...], k_ref[...], do_ref[...]
        p = _xattn_probs(q, k)
        dp = _dot_nt(do, v_ref[...])
        dsb = (p * (dp - jnp.sum(p * dp, axis=1, keepdims=True))).astype(BF16)
        dq_ref[...] = (_dot(dsb, k) * X_SCALE).astype(BF16)
        dk_acc[...] += _dot_tn(dsb, q) * X_SCALE
        dv_acc[...] += _dot_tn(p.astype(BF16), do)

        @pl.when(i == nq - 1)
        def _():
            dk_ref[...] = dk_acc[...].astype(BF16)
            dv_ref[...] = dv_acc[...].astype(BF16)

    qblk = pl.BlockSpec((X_BQ, X_HEAD_DIM), lambda b, h, i: (b * nq + i, h))
    kblk = pl.BlockSpec((N_MEM, X_HEAD_DIM), lambda b, h, i: (b, h))
    return pl.pallas_call(
        body, grid=(n_batch, X_HEADS, nq), in_specs=[qblk, kblk, kblk, qblk], out_specs=[qblk, kblk, kblk],
        out_shape=[jax.ShapeDtypeStruct(qx.shape, BF16), jax.ShapeDtypeStruct(kx.shape, BF16), jax.ShapeDtypeStruct(kx.shape, BF16)],
        scratch_shapes=[pltpu.VMEM((N_MEM, X_HEAD_DIM), F32)] * 2,
        name=name, compiler_params=_params("parallel", "parallel", "arbitrary"),
    )(qx, kx, vx, dox)


def _adamw(w, g, m, v, name, rows):
    r, c = w.shape
    assert r % rows == 0, (name, w.shape, rows)

    def body(w_ref, g_ref, m_ref, v_ref, d_ref, nm_ref, nv_ref):
        gv = g_ref[...]
        m1 = ADAM_B1 * m_ref[...] + (1.0 - ADAM_B1) * gv
        v1 = ADAM_B2 * v_ref[...] + (1.0 - ADAM_B2) * jnp.square(gv)
        m_hat = m1 / (1.0 - ADAM_B1 ** ADAM_STEP)
        v_hat = v1 / (1.0 - ADAM_B2 ** ADAM_STEP)
        d_ref[...] = -ADAM_LR * (m_hat / (jnp.sqrt(v_hat) + ADAM_EPS) + ADAM_WD * w_ref[...])
        nm_ref[...] = m1
        nv_ref[...] = v1

    blk = pl.BlockSpec((rows, c), lambda i: (i, 0))
    return pl.pallas_call(
        body, grid=(r // rows,), in_specs=[blk] * 4, out_specs=[blk] * 3,
        out_shape=[jax.ShapeDtypeStruct((r, c), F32)] * 3,
        name=name, compiler_params=_params("arbitrary"),
    )(w, g, m, v)


def _relu2(acc):
    a = jnp.maximum(acc, 0.0)
    return acc, a * a


def _relu2_bwd(acc, u):
    return (2.0 * jnp.maximum(u.astype(F32), 0.0) * acc,)


def _local_step(x, mem, target, vecs, w_in, late_weights, on_grads=None):
    n_batch = x.shape[0]
    t = n_batch * SEQ
    x0 = x.reshape(t, D_MODEL)
    mem2 = mem.reshape(n_batch * N_MEM, D_MODEL)
    tgt = target.reshape(t, D_MODEL)

    half = 3 * MIX_HALF
    w_qkv = jnp.concatenate([_pair_major(w_in[:, :half]), _pair_major(w_in[:, half:QKV_WIDTH])], axis=1)
    w_gate = jnp.pad(w_in[:, QKV_WIDTH:], ((0, 0), (0, GATE_PAD - N_HEADS)))
    b_pad = jnp.pad(vecs["b_forget"], (0, GATE_PAD - N_HEADS)).reshape(1, GATE_PAD)

    h1, zd, zf, gate = _in_proj(x0, vecs["g_mix"], jnp.concatenate([w_qkv, w_gate], axis=1), "in_proj")
    mn = _rmsnorm(mem2, vecs["g_mem"], "norm_mem")
    c_bc, c_row, sg = _gate_fwd(gate, b_pad, n_batch, "gate_fwd")
    ya, lses = _dil_fwd(zd, n_batch, "dil_fwd")
    yf, of32, lse_f = _fox_fwd(zf, c_bc, c_row, n_batch, "fox_fwd")
    wts = late_weights(yf)
    w_out = wts["w_out"]
    x1, h2 = _matmul_res_norm([ya, yf], [w_out[:MIX_HALF], w_out[MIX_HALF:]], x0, vecs["g_xattn"], "out")
    qx = _matmul(h2, wts["w_xq"], "xq", out_dtypes=(BF16,))[0]
    kx = _matmul(mn, wts["w_xk"], "xk", out_dtypes=(BF16,))[0]
    vx = _matmul(mn, wts["w_xv"], "xv", out_dtypes=(BF16,))[0]
    ox = _xattn_fwd(qx, kx, vx, n_batch, "xattn_fwd")
    x2, h3 = _matmul_res_norm([ox], [wts["w_xo"]], x1, vecs["g_mlp"], "xo")
    u, a2 = _matmul(h3, wts["w_up"], "mlp_up", out_dtypes=(BF16, BF16), epilogue=_relu2, tn=1024)
    loss, dx3, dx3b, dg_final = _loss_bwd(a2, wts["w_down"], x2, vecs["g_final"], tgt, "mlp_down_loss")

    du = _matmul(dx3b, wts["w_down"], "mlp_down_bwd", out_dtypes=(BF16,), extras=(u,), epilogue=_relu2_bwd, tn=1024, w_t=True)[0]
    shards = (N_CHIPS, 2 * D_MODEL, D_MODEL)
    g_mlp = _matmul_tn(h3, du, "gw_up", packed=(shards, lambda i, j: (j, 0, 0), None))
    g_mlp = _matmul_tn(a2, dx3b, "gw_down", packed=(shards, lambda i, j: (i, 1, 0), g_mlp))
    gw_up = g_mlp[:, :D_MODEL].transpose(1, 0, 2).reshape(D_MODEL, D_FF)
    gw_down = g_mlp[:, D_MODEL:].reshape(D_FF, D_MODEL)
    token = on_grads("mlp", g_mlp) if on_grads else None
    dx2, dx2b, dg_mlp = _matmul_rms_bwd([du], [wts["w_up"]], x2, vecs["g_mlp"], dx3, "mlp_up_bwd", after=token)

    gw_xo = _matmul_tn(ox, dx2b, "gw_xo")
    dox = _matmul(dx2b, wts["w_xo"], "xo_bwd", out_dtypes=(BF16,), w_t=True)[0]
    dqx, dkx, dvx = _xattn_bwd(qx, kx, vx, dox, n_batch, "xattn_bwd")
    gw_xq = _matmul_tn(h2, dqx, "gw_xq")
    gw_xk = _matmul_tn(mn, dkx, "gw_xk")
    gw_xv = _matmul_tn(mn, dvx, "gw_xv")
    dmn = _matmul(dkx, wts["w_xk"], "xk_bwd", w_t=True)[0]
    dmn = _matmul_res(dvx, wts["w_xv"], dmn, "xv_bwd", w_t=True)
    _, _, dg_mem = _rms_bwd(mem2, dmn, vecs["g_mem"], None, "norm_mem_bwd")
    dx1, dx1b, dg_xattn = _matmul_rms_bwd([dqx], [wts["w_xq"]], x1, vecs["g_xattn"], dx2, "xq_bwd")

    gw_out = jnp.concatenate([_matmul_tn(ya, dx1b, "gw_out_a"), _matmul_tn(yf, dx1b, "gw_out_f")], axis=0)
    token = on_grads("mid", dict(w_out=gw_out, w_xq=gw_xq, w_xk=gw_xk, w_xv=gw_xv, w_xo=gw_xo)) if on_grads else None
    dy = _matmul(dx1b, w_out, "out_bwd", out_dtypes=(BF16,), w_t=True, after=token)[0]
    dz = _dil_bwd(zd, dy, ya, lses, n_batch, "dil_bwd")
    dz, dc = _fox_bwd(zf, of32, dy, lse_f, c_bc, c_row, dz, n_batch, "fox_bwd")
    dzg, db = _gate_bwd(dc.reshape(n_batch, N_HEADS, SEQ), sg, "gate_bwd")
    gw_pm = _matmul_tn(h1, dz, "gw_in_qkv")
    gw_in = jnp.concatenate([_pair_major_inv(gw_pm[:, :half]), _pair_major_inv(gw_pm[:, half:]),
                             _matmul_tn(h1, dzg, "gw_in_gate")[:, :N_HEADS]], axis=1)
    dx0, _, dg_mix = _matmul_rms_bwd([dz, dzg], [w_qkv, w_gate], x0, vecs["g_mix"], dx1, "in_bwd")

    gw = dict(w_in=gw_in, w_out=gw_out, w_xq=gw_xq, w_xk=gw_xk, w_xv=gw_xv, w_xo=gw_xo, w_up=gw_up, w_down=gw_down)
    gv = dict(g_mix=dg_mix, g_xattn=dg_xattn, g_mem=dg_mem, g_mlp=dg_mlp, g_final=dg_final, b_forget=db)
    return loss, dx0.reshape(x.shape), gw, gv


MESH = pl.DeviceIdType.MESH
ANY = pl.BlockSpec(memory_space=pl.ANY)


def _place():
    x, y, c = lax.axis_index("x"), lax.axis_index("y"), lax.axis_index("c")
    other_chips = [(1 - x, y), (x, 1 - y), (1 - x, 1 - y)]
    return x, y, c, other_chips


def _my_chip():
    return 2 * lax.axis_index("x") + lax.axis_index("y")


def _halves(rows, c, align):
    half = rows // 2
    assert rows % (2 * align) == 0, rows
    return pl.ds(pl.multiple_of(c * half, align), half), pl.ds(pl.multiple_of((1 - c) * half, align), half)


def _place_own(wall, pack):
    return lax.dynamic_update_slice(wall, pack[None], (_my_chip(), 0, 0))


def _gather(pack, name, after):
    def body(p_ref, after_ref, out_ref, send_sems, recv_sems, pass_send, pass_recv):
        del after_ref
        x, y, c, chips = _place()
        me = 2 * x + y
        mine, theirs = _halves(pack.shape[0], c, 16)

        def from_chip(k, chip, rows):
            src = out_ref.at[2 * chip[0] + chip[1], rows]
            return pltpu.make_async_remote_copy(src_ref=src, dst_ref=src, send_sem=send_sems.at[k], recv_sem=recv_sems.at[k],
                                                device_id=(chip[0], chip[1], c), device_id_type=MESH)

        def passed(k, chip, rows):
            src = out_ref.at[2 * chip[0] + chip[1], rows]
            return pltpu.make_async_remote_copy(src_ref=src, dst_ref=src, send_sem=pass_send.at[k], recv_sem=pass_recv.at[k],
                                                device_id=(x, y, 1 - c), device_id_type=MESH)

        sends = []
        for k, chip in enumerate(chips):
            cp = pltpu.make_async_remote_copy(src_ref=p_ref.at[mine], dst_ref=out_ref.at[me, mine], send_sem=send_sems.at[k],
                                              recv_sem=recv_sems.at[k], device_id=(chip[0], chip[1], c), device_id_type=MESH)
            cp.start()
            sends.append(cp)
        for k, chip in enumerate(chips):
            from_chip(k, chip, mine).wait_recv()
            cp = passed(k, chip, mine)
            cp.start()
            sends.append(cp)
        for k, chip in enumerate(chips):
            passed(k, chip, theirs).wait_recv()
        for cp in sends:
            cp.wait_send()

    wall = pl.pallas_call(
        body, in_specs=[ANY, ANY], out_specs=ANY,
        out_shape=jax.ShapeDtypeStruct((N_CHIPS,) + pack.shape, pack.dtype),
        scratch_shapes=[pltpu.SemaphoreType.DMA((3,))] * 4,
        name=name,
    )(pack, after)
    return _place_own(wall, pack)


HBM = pl.BlockSpec(memory_space=pltpu.HBM)
SEM = pl.BlockSpec(memory_space=pltpu.SEMAPHORE)
SPLIT_COPY = pltpu.CompilerParams(has_side_effects=pltpu.SideEffectType.DATAFLOW_SIDE_EFFECTING)


def _in_hbm(a):
    return pltpu.with_memory_space_constraint(a, pltpu.HBM)


def _start_call(start, src, land_shape, after, name):
    land = lax.empty(land_shape, src.dtype)

    def body(src_ref, land_ref, after_ref, send_sems, recv_sems, src_thru, land_thru, token):
        del after_ref, src_thru, land_thru
        start(src_ref, land_ref, send_sems, recv_sems)
        token[...] = jnp.zeros_like(token)

    return pl.pallas_call(
        body, name=name,
        out_shape=(pltpu.SemaphoreType.DMA((3,)), pltpu.SemaphoreType.DMA((3,)), pltpu.HBM(src.shape, src.dtype),
                   pltpu.HBM(land_shape, src.dtype), jax.ShapeDtypeStruct((8, LANES), F32)),
        in_specs=(HBM, HBM, ANY), out_specs=(SEM, SEM, HBM, HBM, pl.BlockSpec(memory_space=pltpu.VMEM)),
        input_output_aliases={0: 2, 1: 3}, compiler_params=SPLIT_COPY,
    )(_in_hbm(src), _in_hbm(land), after)


def _wait_call(body, started, after, name):
    send_sems, recv_sems, src, land, _ = started
    return pl.pallas_call(
        body, name=name,
        out_shape=(pltpu.HBM(src.shape, src.dtype), pltpu.HBM(land.shape, land.dtype)),
        in_specs=(HBM, HBM, SEM, SEM, ANY), out_specs=(HBM, HBM),
        input_output_aliases={0: 0, 1: 1}, compiler_params=SPLIT_COPY,
    )(src, land, send_sems, recv_sems, after)


def _gather_copies(p_ref, wall_ref, send_sems, recv_sems):
    x, y, c, chips = _place()
    me = 2 * x + y
    mine, _ = _halves(p_ref.shape[0], c, 16)
    out, back = [], []
    for k, chip in enumerate(chips):
        peer = dict(send_sem=send_sems.at[k], recv_sem=recv_sems.at[k], device_id=(chip[0], chip[1], c), device_id_type=MESH)
        out.append(pltpu.make_async_remote_copy(src_ref=p_ref.at[mine], dst_ref=wall_ref.at[me, mine], **peer))
        slab = wall_ref.at[2 * chip[0] + chip[1], mine]
        back.append(pltpu.make_async_remote_copy(src_ref=slab, dst_ref=slab, **peer))
    return out, back


def _gather_start(pack, after, name):
    def start(p_ref, wall_ref, send_sems, recv_sems):
        for cp in _gather_copies(p_ref, wall_ref, send_sems, recv_sems)[0]:
            cp.start()

    return _start_call(start, pack, (N_CHIPS,) + pack.shape, after, name)


def _gather_wait(started, after, name):
    def body(p_ref, wall_ref, send_sems, recv_sems, after_ref, p_dead, wall_out):
        del after_ref, p_dead, wall_out
        out, back = _gather_copies(p_ref, wall_ref, send_sems, recv_sems)
        for cp_out, cp_back in zip(out, back):
            cp_out.wait_send()
            cp_back.wait_recv()

    return _wait_call(body, started, after, name)


def _pass_on(wall, name):
    def body(w_in_ref, out_ref, send_sems, recv_sems):
        del w_in_ref
        x, y, c, chips = _place()
        mine, theirs = _halves(wall.shape[1], c, 16)
        sends = []
        for k, chip in enumerate(chips):
            slab = out_ref.at[2 * chip[0] + chip[1]]
            peer = dict(send_sem=send_sems.at[k], recv_sem=recv_sems.at[k], device_id=(x, y, 1 - c), device_id_type=MESH)
            cp = pltpu.make_async_remote_copy(src_ref=slab.at[mine], dst_ref=slab.at[mine], **peer)
            cp.start()
            sends.append((cp, pltpu.make_async_remote_copy(src_ref=slab.at[theirs], dst_ref=slab.at[theirs], **peer)))
        for cp, back in sends:
            back.wait_recv()
            cp.wait_send()

    return pl.pallas_call(
        body, in_specs=[ANY], out_specs=ANY, out_shape=jax.ShapeDtypeStruct(wall.shape, wall.dtype),
        scratch_shapes=[pltpu.SemaphoreType.DMA((3,))] * 2, input_output_aliases={0: 0}, name=name,
    )(wall)


def _swap_halves(g, name):
    half = g.shape[1] // 2

    def body(g_ref, out_ref, send_sem, recv_sem):
        x, y, c, _ = _place()
        _, theirs = _halves(g.shape[1], c, 8)
        cp = pltpu.make_async_remote_copy(src_ref=g_ref.at[:, theirs], dst_ref=out_ref, send_sem=send_sem, recv_sem=recv_sem,
                                          device_id=(x, y, 1 - c), device_id_type=MESH)
        cp.start()
        cp.wait()

    return pl.pallas_call(
        body, in_specs=[ANY], out_specs=ANY,
        out_shape=jax.ShapeDtypeStruct((N_CHIPS, half, D_MODEL), F32),
        scratch_shapes=[pltpu.SemaphoreType.DMA, pltpu.SemaphoreType.DMA],
        name=name,
    )(g)


def _core_index():
    return lax.axis_index("c").astype(jnp.int32).reshape(1)


def _row_tile(half):
    tile = max(t for t in range(16, 1025, 16) if half % t == 0)
    return tile, half // tile


def _add_sibling(g, got, name):
    half = g.shape[1] // 2
    tile, n_tiles = _row_tile(half)

    def body(c_ref, g_ref, got_ref, o_ref):
        o_ref[...] = (g_ref[...] + got_ref[...]).astype(BF16)

    blk = pl.BlockSpec((None, tile, D_MODEL), lambda s, i, c_ref: (s, i, 0))
    return pl.pallas_call(
        body,
        grid_spec=pltpu.PrefetchScalarGridSpec(
            num_scalar_prefetch=1, grid=(N_CHIPS, n_tiles),
            in_specs=[pl.BlockSpec((None, tile, D_MODEL), lambda s, i, c_ref: (s, c_ref[0] * n_tiles + i, 0)), blk],
            out_specs=blk),
        out_shape=jax.ShapeDtypeStruct((N_CHIPS, half, D_MODEL), BF16),
        name=name, compiler_params=_params("arbitrary", "arbitrary"),
    )(_core_index(), g, got)


def _exchange_copies(p_ref, land_ref, send_sems, recv_sems):
    x, y, c, chips = _place()
    me = 2 * x + y
    out, back = [], []
    for k, chip in enumerate(chips):
        peer = dict(send_sem=send_sems.at[k], recv_sem=recv_sems.at[k], device_id=(chip[0], chip[1], c), device_id_type=MESH)
        out.append(pltpu.make_async_remote_copy(src_ref=p_ref.at[2 * chip[0] + chip[1]], dst_ref=land_ref.at[me], **peer))
        slab = land_ref.at[2 * chip[0] + chip[1]]
        back.append(pltpu.make_async_remote_copy(src_ref=slab, dst_ref=slab, **peer))
    return out, back


def _with_own(got, part):
    me = _my_chip()
    return lax.dynamic_update_slice(got, lax.dynamic_slice(part, (me, 0, 0), (1,) + part.shape[1:]), (me, 0, 0))


def _exchange_chips(part, name):
    def body(p_ref, out_ref, send_sems, recv_sems):
        out, back = _exchange_copies(p_ref, out_ref, send_sems, recv_sems)
        for cp in out:
            cp.start()
        for cp in back:
            cp.wait_recv()
        for cp in out:
            cp.wait_send()

    got = pl.pallas_call(
        body, in_specs=[ANY], out_specs=ANY,
        out_shape=jax.ShapeDtypeStruct(part.shape, part.dtype),
        scratch_shapes=[pltpu.SemaphoreType.DMA((3,)), pltpu.SemaphoreType.DMA((3,))],
        name=name,
    )(part)
    return _with_own(got, part)


def _exchange_start(part, name):
    def start(p_ref, land_ref, send_sems, recv_sems):
        for cp in _exchange_copies(p_ref, land_ref, send_sems, recv_sems)[0]:
            cp.start()

    return _start_call(start, part, part.shape, _core_index(), name)


def _exchange_wait(started, after, name):
    def body(p_ref, land_ref, send_sems, recv_sems, after_ref, p_dead, land_out):
        del after_ref, p_dead, land_out
        out, back = _exchange_copies(p_ref, land_ref, send_sems, recv_sems)
        for cp_out, cp_back in zip(out, back):
            cp_out.wait_send()
            cp_back.wait_recv()

    part, got = _wait_call(body, started, after, name)
    return _with_own(got, part)


def _sum_chips(parts, name):
    half = parts.shape[1]
    tile, n_tiles = _row_tile(half)

    def body(c_ref, p0, p1, p2, p3, o_ref):
        f32 = lambda p: p[...].astype(F32)
        o_ref[...] = ((f32(p0) + f32(p1)) + f32(p2)) + f32(p3)

    def slab(s):
        return pl.BlockSpec((None, tile, D_MODEL), lambda i, c_ref, s=s: (s, i, 0))

    return pl.pallas_call(
        body,
        grid_spec=pltpu.PrefetchScalarGridSpec(
            num_scalar_prefetch=1, grid=(n_tiles,),
            in_specs=[slab(s) for s in range(N_CHIPS)],
            out_specs=pl.BlockSpec((None, tile, D_MODEL), lambda i, c_ref: (c_ref[0], i, 0))),
        out_shape=jax.ShapeDtypeStruct((2, half, D_MODEL), F32),
        name=name, compiler_params=_params("arbitrary"),
    )(_core_index(), parts, parts, parts, parts)


def _share_halves(halves, name):
    def body(h_ref, out_ref, send_sem, recv_sem):
        del h_ref
        x, y, c, _ = _place()
        cp = pltpu.make_async_remote_copy(src_ref=out_ref.at[c], dst_ref=out_ref.at[c], send_sem=send_sem, recv_sem=recv_sem,
                                          device_id=(x, y, 1 - c), device_id_type=MESH)
        cp.start()
        pltpu.make_async_remote_copy(src_ref=out_ref.at[1 - c], dst_ref=out_ref.at[1 - c], send_sem=send_sem, recv_sem=recv_sem,
                                     device_id=(x, y, 1 - c), device_id_type=MESH).wait_recv()
        cp.wait_send()

    return pl.pallas_call(
        body, in_specs=[ANY], out_specs=ANY,
        out_shape=jax.ShapeDtypeStruct(halves.shape, halves.dtype),
        scratch_shapes=[pltpu.SemaphoreType.DMA] * 2,
        input_output_aliases={0: 0},
        name=name,
    )(halves)


def _reduce_parts(g, tag):
    return _add_sibling(g, _swap_halves(g, "swap_" + tag), "add_" + tag)


def _reduce_finish(got, tag):
    halves = _share_halves(_sum_chips(got, "sum_" + tag), "share_" + tag)
    return halves.reshape(2 * halves.shape[1], D_MODEL)


SMALL_ROWS = 8


def _allreduce_small(v):
    def body(v_ref, out_ref, buf, send_sems, recv_sems):
        x, y, c, _ = _place()
        buf[4 * x + 2 * y + c] = v_ref[...]
        sends = []
        for k in range(1, N_DEV):
            px = 1 - x if k & 4 else x
            py = 1 - y if k & 2 else y
            pc = 1 - c if k & 1 else c
            cp = pltpu.make_async_remote_copy(src_ref=v_ref, dst_ref=buf.at[4 * x + 2 * y + c], send_sem=send_sems.at[k - 1],
                                              recv_sem=recv_sems.at[k - 1], device_id=(px, py, pc), device_id_type=MESH)
            cp.start()
            sends.append((cp, 4 * px + 2 * py + pc))
        for k, (cp, peer) in enumerate(sends):
            pltpu.make_async_remote_copy(src_ref=v_ref, dst_ref=buf.at[peer], send_sem=send_sems.at[k], recv_sem=recv_sems.at[k],
                                         device_id=(x, y, c), device_id_type=MESH).wait_recv()
        for cp, _ in sends:
            cp.wait_send()
        total = buf[0]
        for d in range(1, N_DEV):
            total = total + buf[d]
        out_ref[...] = total

    vmem = pl.BlockSpec(memory_space=pltpu.VMEM)
    return pl.pallas_call(
        body, in_specs=[vmem], out_specs=vmem,
        out_shape=jax.ShapeDtypeStruct(v.shape, v.dtype),
        scratch_shapes=[pltpu.VMEM((N_DEV,) + v.shape, v.dtype), pltpu.SemaphoreType.DMA((N_DEV - 1,)),
                        pltpu.SemaphoreType.DMA((N_DEV - 1,))],
        name="allreduce_small",
    )(v)


MATRICES = ("w_in", "w_out", "w_xq", "w_xk", "w_xv", "w_xo", "w_up", "w_down")
VECTORS = ("g_mix", "g_xattn", "g_mem", "g_mlp", "g_final", "b_forget")
WEIGHT_ORDER = ("g_mix", "w_in", "b_forget", "w_out", "g_xattn", "g_mem", "w_xq", "w_xk", "w_xv", "w_xo",
                "g_mlp", "w_up", "w_down", "g_final")
GROUPS = {"mlp": ("w_up", "w_down"), "mid": ("w_out", "w_xq", "w_xk", "w_xv", "w_xo"), "in": ("w_in",)}
LATE = GROUPS["mid"] + GROUPS["mlp"]
W_IN_SHARD = IN_WIDTH // N_CHIPS
SHARD_ROWS = {"w_in": W_IN_SHARD, "w_out": 256, "w_xq": 256, "w_xk": 256, "w_xv": 256, "w_xo": 256, "w_up": 1024, "w_down": 1024}
PACK_ROWS = {n: -(-r // 32) * 32 for n, r in SHARD_ROWS.items()}
ADAM_ROWS = 128


def _pack(parts, names):
    return jnp.concatenate([jnp.pad(parts[n], ((0, PACK_ROWS[n] - SHARD_ROWS[n]), (0, 0))) for n in names], axis=0)


def _unpack(a, names):
    out, pos = {}, 0
    for n in names:
        out[n] = a[..., pos:pos + SHARD_ROWS[n], :]
        pos += PACK_ROWS[n]
    return out


def _full_weights(wall, names):
    cols = lambda a: a.transpose(1, 0, 2).reshape(a.shape[1], -1)
    rows = lambda a: a.reshape(-1, a.shape[-1])
    out = {}
    for n, a in _unpack(wall, names).items():
        if n == "w_in":
            out[n] = cols(a.reshape(N_CHIPS, D_MODEL, W_IN_SHARD))
        else:
            out[n] = cols(a) if n == "w_up" else rows(a)
    return out


def _shard_of(g, name, s):
    if name == "w_in":
        return g[:, s * W_IN_SHARD:(s + 1) * W_IN_SHARD].reshape(W_IN_SHARD, D_MODEL)
    if name == "w_up":
        return g[:, s * D_MODEL:(s + 1) * D_MODEL]
    n = SHARD_ROWS[name]
    return g[s * n:(s + 1) * n]


def _pack_grads(gws, names):
    return jnp.stack([_pack({n: _shard_of(gws[n], n, s) for n in names}, names) for s in range(N_CHIPS)])


def kernel(x, mem, g_mix, w_in, b_forget, w_out, g_xattn, g_mem, w_xq, w_xk, w_xv, w_xo, g_mlp, w_up, w_down, g_final, loss_target, m_g_mix, m_w_in, m_b_forget, m_w_out, m_g_xattn, m_g_mem, m_w_xq, m_w_xk, m_w_xv, m_w_xo, m_g_mlp, m_w_up, m_w_down, m_g_final, v_g_mix, v_w_in, v_b_forget, v_w_out, v_g_xattn, v_g_mem, v_w_xq, v_w_xk, v_w_xv, v_w_xo, v_g_mlp, v_w_up, v_w_down, v_g_final):
    given = dict(locals())
    weights = {n: given[n] for n in WEIGHT_ORDER}
    vecs = {n: weights[n] for n in VECTORS}

    shard = {n: weights[n].astype(BF16) for n in MATRICES}
    shard["w_in"] = shard["w_in"].reshape(W_IN_SHARD, D_MODEL)
    in_pack, late_pack = _pack(shard, GROUPS["in"]), _pack(shard, LATE)
    in_wall = _gather(in_pack, "gather_in", in_pack)
    late = _gather_start(late_pack, in_wall, "gather_late_start")
    w_in_full = _full_weights(in_wall, GROUPS["in"])["w_in"]

    def late_weights(after):
        pack, wall = _gather_wait(late, after, "gather_late_wait")
        return _full_weights(_place_own(_pass_on(wall, "gather_late_pass"), pack), LATE)

    started = {}

    def on_grads(group, gws):
        packed = gws if group == "mlp" else _pack_grads(gws, GROUPS[group])
        part = _reduce_parts(packed, group)
        started[group] = _exchange_start(part, "exchange_%s_start" % group)
        return started[group][4]

    loss, grad_x, gw, gv = _local_step(x, mem, loss_target, vecs, w_in_full, late_weights, on_grads)

    on_grads("in", gw)
    grads, delta, new_m, new_v = {}, {}, {}, {}

    def finish(group, after):
        got = _exchange_wait(started[group], after, "exchange_%s_wait" % group)
        for n, a in _unpack(_reduce_finish(got, group), GROUPS[group]).items():
            grads[n] = a.reshape(weights[n].shape)
            delta[n], new_m[n], new_v[n] = _adamw(weights[n], grads[n], given["m_" + n], given["v_" + n], "adamw_" + n, ADAM_ROWS)
        return new_v[GROUPS[group][-1]]

    after = finish("mlp", started["in"][4])
    after = finish("mid", after)

    row = lambda a: jnp.pad(a.reshape(-1), (0, D_MODEL - a.size)).reshape(1, D_MODEL)
    small = jnp.concatenate([gv[n] for n in VECTORS[:5]] + [row(gv["b_forget"][:, 0]), row(loss[0, :1]),
                             jnp.zeros((1, D_MODEL), F32)], axis=0)
    small = _allreduce_small(small)
    for k, n in enumerate(VECTORS[:5]):
        grads[n] = small[k]
    grads["b_forget"] = small[5, :N_HEADS]
    loss_total = small[6, 0]
    finish("in", after)

    stack = lambda prefix: jnp.concatenate([row(given[prefix + n]) for n in VECTORS] + [jnp.zeros((2, D_MODEL), F32)], axis=0)
    g_small = jnp.concatenate([small[:6], jnp.zeros((2, D_MODEL), F32)], axis=0)
    d, m1, v1 = _adamw(stack(""), g_small, stack("m_"), stack("v_"), "adamw_vectors", SMALL_ROWS)
    for k, n in enumerate(VECTORS):
        width = weights[n].shape[0]
        delta[n], new_m[n], new_v[n] = d[k, :width], m1[k, :width], v1[k, :width]

    return (loss_total, grad_x, *[grads[n] for n in WEIGHT_ORDER], *[delta[n] for n in WEIGHT_ORDER],
            *[new_m[n] for n in WEIGHT_ORDER], *[new_v[n] for n in WEIGHT_ORDER])
```

```python
import functools
import math

import jax
import jax.numpy as jnp
from jax import lax
from jax.experimental import pallas as pl
from jax.experimental.pallas import tpu as pltpu

F32 = jnp.float32
BF16 = jnp.bfloat16

D_MODEL = 1024
SEQ = 2048
N_MEM = 256
HEAD_DIM = 64
N_HEADS = 8
MIX_HALF = N_HEADS * HEAD_DIM
QKV_WIDTH = 6 * MIX_HALF
IN_WIDTH = QKV_WIDTH + N_HEADS
GATE_PAD = 128
BLOCK = 128
DILATIONS = (1, 4, 16)
X_HEADS = 4
X_HEAD_DIM = 256
D_FF = 4096
EPS = 1e-6
NEG = -1e30
ATT_SCALE = 1.0 / math.sqrt(HEAD_DIM)
X_SCALE = 1.0 / math.sqrt(X_HEAD_DIM)
LANES = 128
N_CHIPS = 4
N_DEV = 8

ADAM_LR = 0.001
ADAM_B1 = 0.9
ADAM_B2 = 0.999
ADAM_EPS = 1e-08
ADAM_WD = 0.01
ADAM_STEP = 10

VMEM_LIMIT = 48 * 1024 * 1024


def _params(*sem):
    return pltpu.CompilerParams(dimension_semantics=sem or None, vmem_limit_bytes=VMEM_LIMIT)


def _dot(a, b):
    return jnp.dot(a, b, preferred_element_type=F32)


def _dot_nt(a, b):
    return lax.dot_general(a, b, (((1,), (1,)), ((), ())), preferred_element_type=F32)


def _dot_tn(a, b):
    return lax.dot_general(a, b, (((0,), (0,)), ((), ())), preferred_element_type=F32)


def _dot_exact(x, e):
    hi = x.astype(BF16)
    r1 = x - hi.astype(F32)
    mid = r1.astype(BF16)
    lo = (r1 - mid.astype(F32)).astype(BF16)
    return _dot(hi, e) + _dot(mid, e) + _dot(lo, e)


def _head_mask(e):
    lane = lax.broadcasted_iota(jnp.int32, (1, LANES), 1)
    return (lane >= HEAD_DIM * e) & (lane < HEAD_DIM * (e + 1))


def _matmul(a, w, name, out_dtypes=(F32,), extras=(), epilogue=None, tm=1024, tn=512, w_t=False, after=None):
    m, k = a.shape
    n = w.shape[0] if w_t else w.shape[1]
    tm, tn = min(tm, m), min(tn, n)
    assert m % tm == 0 and n % tn == 0, (name, a.shape, w.shape)
    n_ex = len(extras)
    order = () if after is None else (after,)

    def body(a_ref, w_ref, *rest):
        rest = rest[len(order):]
        acc = (_dot_nt if w_t else _dot)(a_ref[...], w_ref[...])
        res = (acc,) if epilogue is None else epilogue(acc, *[r[...] for r in rest[:n_ex]])
        for o_ref, r in zip(rest[n_ex:], res):
            o_ref[...] = r.astype(o_ref.dtype)

    tile = pl.BlockSpec((tm, tn), lambda i, j: (i, j))
    w_spec = pl.BlockSpec((tn, k), lambda i, j: (j, 0)) if w_t else pl.BlockSpec((k, tn), lambda i, j: (0, j))
    return pl.pallas_call(
        body, grid=(m // tm, n // tn),
        in_specs=[pl.BlockSpec((tm, k), lambda i, j: (i, 0)), w_spec] + [pl.BlockSpec(memory_space=pl.ANY)] * len(order) + [tile] * n_ex,
        out_specs=[tile] * len(out_dtypes),
        out_shape=[jax.ShapeDtypeStruct((m, n), dt) for dt in out_dtypes],
        name=name, compiler_params=_params("parallel", "arbitrary"),
    )(a, w, *order, *extras)


def _matmul_res(a, w, res, name, w_t=False):
    return _matmul(a, w, name, extras=(res,), epilogue=lambda acc, r: (r + acc,), w_t=w_t)[0]


def _matmul_tn(x, y, name, tm=1024, tn=1024, tk=512, packed=None):
    t, m = x.shape
    _, n = y.shape
    tm, tn, tk = min(tm, m), min(tn, n), min(tk, t)
    assert m % tm == 0 and n % tn == 0 and t % tk == 0, (name, x.shape, y.shape)
    shape, place, into = packed or ((m, n), None, None)

    def body(x_ref, y_ref, *rest):
        o_ref = rest[-1]

        @pl.when(pl.program_id(2) == 0)
        def _():
            o_ref[...] = jnp.zeros_like(o_ref)

        o_ref[...] += _dot_tn(x_ref[...], y_ref[...])

    out_spec = (pl.BlockSpec((tm, tn), lambda i, j, k: (i, j)) if place is None
                else pl.BlockSpec((None, tm, tn), lambda i, j, k: place(i, j)))
    return pl.pallas_call(
        body, grid=(m // tm, n // tn, t // tk),
        in_specs=[pl.BlockSpec((tk, tm), lambda i, j, k: (k, i)), pl.BlockSpec((tk, tn), lambda i, j, k: (k, j))]
        + ([] if into is None else [pl.BlockSpec(memory_space=pl.ANY)]),
        out_specs=out_spec, out_shape=jax.ShapeDtypeStruct(shape, F32),
        input_output_aliases={} if into is None else {2: 0},
        name=name, compiler_params=_params("parallel", "parallel", "arbitrary"),
    )(x, y, *(() if into is None else (into,)))


def _rmsnorm(x, g, name, tm=512):
    t, d = x.shape
    tm = min(tm, t)

    def body(x_ref, g_ref, h_ref):
        xv = x_ref[...]
        r = lax.rsqrt(jnp.mean(xv * xv, axis=-1, keepdims=True) + EPS)
        h_ref[...] = (xv * r * g_ref[...]).astype(BF16)

    return pl.pallas_call(
        body, grid=(t // tm,),
        in_specs=[pl.BlockSpec((tm, d), lambda i: (i, 0)), pl.BlockSpec((1, d), lambda i: (0, 0))],
        out_specs=pl.BlockSpec((tm, d), lambda i: (i, 0)),
        out_shape=jax.ShapeDtypeStruct((t, d), BF16),
        name=name, compiler_params=_params("arbitrary"),
    )(x, g.reshape(1, d))


def _in_proj(x, g, w_all, name, tm=512):
    t, d = x.shape
    half = 3 * MIX_HALF

    def body(x_ref, g_ref, w_ref, h_ref, zd_ref, zf_ref, gate_ref):
        xv = x_ref[...]
        r = lax.rsqrt(jnp.mean(xv * xv, axis=-1, keepdims=True) + EPS)
        h = (xv * r * g_ref[...]).astype(BF16)
        h_ref[...] = h
        zd_ref[...] = _dot(h, w_ref[:, 0:half])
        zf_ref[...] = _dot(h, w_ref[:, half:2 * half]).astype(BF16)
        gate_ref[...] = _dot(h, w_ref[:, 2 * half:])

    row = lambda width: pl.BlockSpec((tm, width), lambda i: (i, 0))
    return pl.pallas_call(
        body, grid=(t // tm,),
        in_specs=[row(d), pl.BlockSpec((1, d), lambda i: (0, 0)), pl.BlockSpec(w_all.shape, lambda i: (0, 0))],
        out_specs=[row(d), row(half), row(half), row(GATE_PAD)],
        out_shape=[jax.ShapeDtypeStruct((t, d), BF16), jax.ShapeDtypeStruct((t, half), F32),
                   jax.ShapeDtypeStruct((t, half), BF16), jax.ShapeDtypeStruct((t, GATE_PAD), F32)],
        name=name, compiler_params=_params("arbitrary"),
    )(x, g.reshape(1, d), w_all)


def _rms_bwd_tile(xv, dh, g):
    d = xv.shape[-1]
    r = lax.rsqrt(jnp.mean(xv * xv, axis=-1, keepdims=True) + EPS)
    dyg = dh * g
    proj = jnp.sum(dyg * xv, axis=-1, keepdims=True)
    dx = r * dyg - xv * (r * r * r * (1.0 / d)) * proj
    return dx, dh * (xv * r)


def _rms_bwd(x, dh, g, dres, name, tm=512):
    t, d = x.shape
    tm = min(tm, t)
    has_res = dres is not None

    def body(x_ref, dh_ref, g_ref, *rest):
        if has_res:
            res_ref, dx_ref, dxb_ref, dg_ref = rest
        else:
            dx_ref, dxb_ref, dg_ref = rest
        dx, dg_rows = _rms_bwd_tile(x_ref[...], dh_ref[...], g_ref[...])
        if has_res:
            dx = res_ref[...] + dx
        dx_ref[...] = dx
        dxb_ref[...] = dx.astype(BF16)

        @pl.when(pl.program_id(0) == 0)
        def _():
            dg_ref[...] = jnp.zeros_like(dg_ref)

        dg_ref[...] += jnp.sum(dg_rows, axis=0, keepdims=True)

    row = pl.BlockSpec((tm, d), lambda i: (i, 0))
    vec = pl.BlockSpec((1, d), lambda i: (0, 0))
    return pl.pallas_call(
        body, grid=(t // tm,),
        in_specs=[row, row, vec] + ([row] if has_res else []),
        out_specs=[row, row, vec],
        out_shape=[jax.ShapeDtypeStruct((t, d), F32), jax.ShapeDtypeStruct((t, d), BF16), jax.ShapeDtypeStruct((1, d), F32)],
        name=name, compiler_params=_params("arbitrary"),
    )(x, dh, g.reshape(1, d), *((dres,) if has_res else ()))


def _row_dots(a_refs, w_refs, w_t):
    acc = None
    for a_ref, w_ref in zip(a_refs, w_refs):
        part = (_dot_nt if w_t else _dot)(a_ref[...], w_ref[...])
        acc = part if acc is None else acc + part
    return acc


def _row_specs(a_parts, w_parts, tm):
    specs = [pl.BlockSpec((tm, a.shape[1]), lambda i: (i, 0)) for a in a_parts]
    return specs + [pl.BlockSpec(w.shape, lambda i: (0, 0)) for w in w_parts]


def _matmul_res_norm(a_parts, w_parts, res, g, name, tm=512):
    t, d = res.shape
    n = len(a_parts)

    def body(*refs):
        res_ref, g_ref, x_ref, h_ref = refs[2 * n:]
        xv = res_ref[...] + _row_dots(refs[:n], refs[n:2 * n], False)
        x_ref[...] = xv
        r = lax.rsqrt(jnp.mean(xv * xv, axis=-1, keepdims=True) + EPS)
        h_ref[...] = (xv * r * g_ref[...]).astype(BF16)

    row = pl.BlockSpec((tm, d), lambda i: (i, 0))
    return pl.pallas_call(
        body, grid=(t // tm,),
        in_specs=_row_specs(a_parts, w_parts, tm) + [row, pl.BlockSpec((1, d), lambda i: (0, 0))],
        out_specs=[row, row],
        out_shape=[jax.ShapeDtypeStruct((t, d), F32), jax.ShapeDtypeStruct((t, d), BF16)],
        name=name, compiler_params=_params("arbitrary"),
    )(*a_parts, *w_parts, res, g.reshape(1, d))


def _matmul_rms_bwd(a_parts, w_parts, x, g, dres, name, tm=512, after=None):
    t, d = x.shape
    n = len(a_parts)
    order = () if after is None else (after,)

    def body(*refs):
        x_ref, g_ref, res_ref = refs[2 * n:2 * n + 3]
        dx_ref, dxb_ref, dg_ref = refs[2 * n + 3 + len(order):]
        dx, dg_rows = _rms_bwd_tile(x_ref[...], _row_dots(refs[:n], refs[n:2 * n], True), g_ref[...])
        dx = res_ref[...] + dx
        dx_ref[...] = dx
        dxb_ref[...] = dx.astype(BF16)

        @pl.when(pl.program_id(0) == 0)
        def _():
            dg_ref[...] = jnp.zeros_like(dg_ref)

        dg_ref[...] += jnp.sum(dg_rows, axis=0, keepdims=True)

    row = pl.BlockSpec((tm, d), lambda i: (i, 0))
    vec = pl.BlockSpec((1, d), lambda i: (0, 0))
    return pl.pallas_call(
        body, grid=(t // tm,),
        in_specs=_row_specs(a_parts, w_parts, tm) + [row, vec, row] + [pl.BlockSpec(memory_space=pl.ANY)] * len(order),
        out_specs=[row, row, vec],
        out_shape=[jax.ShapeDtypeStruct((t, d), F32), jax.ShapeDtypeStruct((t, d), BF16), jax.ShapeDtypeStruct((1, d), F32)],
        name=name, compiler_params=_params("arbitrary"),
    )(*a_parts, *w_parts, x, g.reshape(1, d), dres, *order)


def _loss_bwd(a, w, res, g, target, name, tm=512):
    t, d = res.shape

    def body(a_ref, w_ref, x_ref, g_ref, t_ref, loss_ref, dx_ref, dxb_ref, dg_ref):
        xv = x_ref[...] + _dot(a_ref[...], w_ref[...])
        gv = g_ref[...]
        r = lax.rsqrt(jnp.mean(xv * xv, axis=-1, keepdims=True) + EPS)
        err = xv * r * gv - t_ref[...]
        dx, dg_rows = _rms_bwd_tile(xv, err * (1.0 / d), gv)
        dx_ref[...] = dx
        dxb_ref[...] = dx.astype(BF16)

        @pl.when(pl.program_id(0) == 0)
        def _():
            dg_ref[...] = jnp.zeros_like(dg_ref)
            loss_ref[...] = jnp.zeros_like(loss_ref)

        dg_ref[...] += jnp.sum(dg_rows, axis=0, keepdims=True)
        part = jnp.sum(jnp.sum(err * err, axis=0, keepdims=True), axis=1, keepdims=True) * (0.5 / d)
        loss_ref[...] += jnp.broadcast_to(part, loss_ref.shape)

    row = pl.BlockSpec((tm, d), lambda i: (i, 0))
    vec = pl.BlockSpec((1, d), lambda i: (0, 0))
    return pl.pallas_call(
        body, grid=(t // tm,),
        in_specs=_row_specs([a], [w], tm) + [row, vec, row],
        out_specs=[pl.BlockSpec((1, LANES), lambda i: (0, 0)), row, row, vec],
        out_shape=[jax.ShapeDtypeStruct((1, LANES), F32), jax.ShapeDtypeStruct((t, d), F32),
                   jax.ShapeDtypeStruct((t, d), BF16), jax.ShapeDtypeStruct((1, d), F32)],
        name=name, compiler_params=_params("arbitrary"),
    )(a, w, res, g.reshape(1, d), target)


def _tri(upper):
    r = lax.broadcasted_iota(jnp.int32, (LANES, LANES), 0)
    c = lax.broadcasted_iota(jnp.int32, (LANES, LANES), 1)
    return jnp.where((r <= c) if upper else (r >= c), 1.0, 0.0).astype(BF16)


def _gate_fwd(gate, b_pad, n_batch, name):
    s = SEQ
    nblk = s // LANES

    def body(g_ref, b_ref, cbc_ref, crow_ref, sg_ref, ct_ref):
        gz = g_ref[...] + b_ref[...]
        logf = jnp.minimum(gz, 0.0) - jnp.log(1.0 + jnp.exp(-jnp.abs(gz)))
        logf_t = logf.T
        sg_ref[...] = (1.0 / (1.0 + jnp.exp(gz))).T[0:N_HEADS]
        upper = _tri(True)
        carry = jnp.zeros((LANES, 1), F32)
        for blk in range(nblk):
            seg = _dot_exact(logf_t[:, blk * LANES:(blk + 1) * LANES], upper) + carry
            carry = seg[:, LANES - 1:LANES]
            ct_ref[:, blk * LANES:(blk + 1) * LANES] = seg
        ct = ct_ref[...]
        crow_ref[...] = ct[0:N_HEADS]
        c_col = ct.T
        lane = lax.broadcasted_iota(jnp.int32, (1, MIX_HALF), 1)
        acc = jnp.zeros((s, MIX_HALF), F32)
        for h in range(N_HEADS):
            acc = jnp.where((lane >= HEAD_DIM * h) & (lane < HEAD_DIM * (h + 1)), c_col[:, h:h + 1], acc)
        cbc_ref[...] = acc

    return pl.pallas_call(
        body, grid=(n_batch,),
        in_specs=[pl.BlockSpec((s, GATE_PAD), lambda b: (b, 0)), pl.BlockSpec((1, GATE_PAD), lambda b: (0, 0))],
        out_specs=[pl.BlockSpec((s, MIX_HALF), lambda b: (b, 0)),
                   pl.BlockSpec((None, N_HEADS, s), lambda b: (b, 0, 0)),
                   pl.BlockSpec((None, N_HEADS, s), lambda b: (b, 0, 0))],
        out_shape=[jax.ShapeDtypeStruct((n_batch * s, MIX_HALF), F32),
                   jax.ShapeDtypeStruct((n_batch, N_HEADS, s), F32),
                   jax.ShapeDtypeStruct((n_batch, N_HEADS, s), F32)],
        scratch_shapes=[pltpu.VMEM((LANES, s), F32)],
        name=name, compiler_params=_params("arbitrary"),
    )(gate, b_pad)


def _gate_bwd(dc, sg, name):
    n_batch, _, s = dc.shape
    nblk = s // LANES

    def body(dc_ref, sg_ref, dz_ref, db_ref, dt_ref):
        lower = _tri(False)
        dcv = dc_ref[...]
        carry = jnp.zeros((N_HEADS, 1), F32)
        dt_ref[...] = jnp.zeros_like(dt_ref)
        for blk in reversed(range(nblk)):
            seg = _dot_exact(dcv[:, blk * LANES:(blk + 1) * LANES], lower) + carry
            carry = seg[:, 0:1]
            dt_ref[0:N_HEADS, blk * LANES:(blk + 1) * LANES] = seg * sg_ref[:, blk * LANES:(blk + 1) * LANES]
        dg_t = dt_ref[...]
        dz_ref[...] = dg_t.T.astype(BF16)

        @pl.when(pl.program_id(0) == 0)
        def _():
            db_ref[...] = jnp.zeros_like(db_ref)

        db_ref[...] += jnp.broadcast_to(jnp.sum(dg_t[0:N_HEADS], axis=1, keepdims=True), db_ref.shape)

    return pl.pallas_call(
        body, grid=(n_batch,),
        in_specs=[pl.BlockSpec((None, N_HEADS, s), lambda b: (b, 0, 0)), pl.BlockSpec((None, N_HEADS, s), lambda b: (b, 0, 0))],
        out_specs=[pl.BlockSpec((s, GATE_PAD), lambda b: (b, 0)), pl.BlockSpec((N_HEADS, LANES), lambda b: (0, 0))],
        out_shape=[jax.ShapeDtypeStruct((n_batch * s, GATE_PAD), BF16), jax.ShapeDtypeStruct((N_HEADS, LANES), F32)],
        scratch_shapes=[pltpu.VMEM((LANES, s), F32)],
        name=name, compiler_params=_params("arbitrary"),
    )(dc, sg)


FOX_BQ = 512
FOX_BK = 512
FOX_STRIP = 512
PAIR_WIDTH = 3 * LANES
N_PAIRS = N_HEADS // 2


def _pair_major(w):
    return w.reshape(w.shape[0], 3, N_PAIRS, LANES).transpose(0, 2, 1, 3).reshape(w.shape[0], 3 * MIX_HALF)


def _pair_major_inv(w):
    return w.reshape(w.shape[0], N_PAIRS, 3, LANES).transpose(0, 2, 1, 3).reshape(w.shape[0], 3 * MIX_HALF)


def _causal(i, j, bq, bk):
    qpos = i * bq + lax.broadcasted_iota(jnp.int32, (bq, 1), 0)
    kpos = j * bk + lax.broadcasted_iota(jnp.int32, (1, bk), 1)
    return kpos <= qpos


def _split_bf16(p):
    hi = p.astype(BF16)
    return hi, (p - hi.astype(F32)).astype(BF16)


def _fox_fwd(zf, c_bc, c_row, n_batch, name):
    s, bq, bk = SEQ, FOX_BQ, FOX_BK
    nq = s // bq
    t = n_batch * s

    n_strip = bq // FOX_STRIP

    def body(q_ref, k_ref, v_ref, cq_ref, cr_ref, o_ref, o32_ref, lse_ref):
        hp = pl.program_id(1)
        strips = [slice(r * FOX_STRIP, (r + 1) * FOX_STRIP) for r in range(n_strip)]
        chains = [(e, r) for e in range(2) for r in range(n_strip)]
        qh, cq = {}, {}
        for e, r in chains:
            q = q_ref[strips[r], :] * ATT_SCALE
            qh[e, r] = jnp.where(_head_mask(e), q, jnp.zeros_like(q))
            cq[e, r] = cq_ref[strips[r], HEAD_DIM * e:HEAD_DIM * e + 1]

        def step(i, j, carry, masked):
            rows = pl.ds(j * bk, bk)
            kj, vj = k_ref[rows, :], v_ref[rows, :]
            ck = [cr_ref[pl.ds(2 * hp + e, 1), rows] for e in range(2)]
            out = []
            scores = [_dot_nt(qh[e, r], kj) for e, r in chains]
            for n, (e, r) in enumerate(chains):
                m, l, acc = carry[3 * n:3 * n + 3]
                sc = scores[n] + (cq[e, r] - ck[e])
                if masked:
                    qpos = i * bq + r * FOX_STRIP + lax.broadcasted_iota(jnp.int32, (FOX_STRIP, 1), 0)
                    kpos = j * bk + lax.broadcasted_iota(jnp.int32, (1, bk), 1)
                    sc = jnp.where(kpos <= qpos, sc, NEG)
                m_new = jnp.maximum(m, jnp.max(sc, axis=1, keepdims=True))
                alpha = jnp.exp(m - m_new)
                p = jnp.exp(sc - m_new)
                p_hi, p_lo = _split_bf16(p)
                out += [m_new, alpha * l + jnp.sum(p, axis=1, keepdims=True), alpha * acc + (_dot(p_hi, vj) + _dot(p_lo, vj))]
            return tuple(out)

        def run(i):
            carry = (jnp.full((FOX_STRIP, 1), NEG, F32), jnp.zeros((FOX_STRIP, 1), F32), jnp.zeros((FOX_STRIP, LANES), F32)) * len(chains)
            n_clear = (i * bq) // bk
            for j in range((i * bq + bq + bk - 1) // bk):
                carry = step(i, j, carry, masked=j >= n_clear)
            for r in range(n_strip):
                outs = [carry[3 * (e * n_strip + r) + 2] / carry[3 * (e * n_strip + r) + 1] for e in range(2)]
                lses = [carry[3 * (e * n_strip + r)] + jnp.log(carry[3 * (e * n_strip + r) + 1]) for e in range(2)]
                o = jnp.where(_head_mask(0), outs[0], outs[1])
                o_ref[strips[r], :] = o.astype(BF16)
                o32_ref[strips[r], :] = o
                lse_ref[strips[r], :] = jnp.where(_head_mask(0), lses[0], lses[1])

        for k in range(nq):
            pl.when(pl.program_id(2) == k)(functools.partial(run, k))

    def col(c0):
        return lambda b, hp, i: (b, 3 * hp + c0)

    blk = pl.BlockSpec((bq, LANES), lambda b, hp, i: (b * nq + i, hp))
    return pl.pallas_call(
        body, grid=(n_batch, N_PAIRS, nq),
        in_specs=[pl.BlockSpec((bq, LANES), lambda b, hp, i: (b * nq + i, 3 * hp)),
                  pl.BlockSpec((s, LANES), col(1)), pl.BlockSpec((s, LANES), col(2)), blk,
                  pl.BlockSpec((None, N_HEADS, s), lambda b, hp, i: (b, 0, 0))],
        out_specs=[blk, blk, blk],
        out_shape=[jax.ShapeDtypeStruct((t, MIX_HALF), BF16), jax.ShapeDtypeStruct((t, MIX_HALF), F32),
                   jax.ShapeDtypeStruct((t, MIX_HALF), F32)],
        name=name, compiler_params=_params("parallel", "parallel", "arbitrary"),
    )(zf, zf, zf, c_bc, c_row)


def _fox_bwd(zf, o32, dy, lse, c_bc, c_row, dz, n_batch, name):
    s, bq, bk = SEQ, FOX_BQ, FOX_BK
    nq, nk = s // bq, s // bk

    def body(q_ref, k_ref, v_ref, o_ref, do_ref, lse_ref, cq_ref, cr_ref, dz_in, dz_ref, dc_ref, dq_acc):
        del dz_in
        hp = pl.program_id(1)

        @pl.when(pl.program_id(2) == 0)
        def _():
            dq_acc[...] = jnp.zeros_like(dq_acc)

        kj, vj = k_ref[...], v_ref[...]
        km = [jnp.where(_head_mask(e), kj, jnp.zeros_like(kj)) for e in range(2)]

        def step(i, j, ck, carry, masked):
            rows = pl.ds(i * bq, bq)
            qi, doi = q_ref[rows, :] * ATT_SCALE, do_ref[rows, :]
            prod = doi.astype(F32) * o_ref[rows, :]
            out = []
            dq = jnp.zeros((bq, LANES), F32)
            for e in range(2):
                dk_a, dv_a, dc_a = carry[3 * e:3 * e + 3]
                mask = _head_mask(e)
                lane0 = HEAD_DIM * e
                dom = jnp.where(mask, doi, jnp.zeros_like(doi))
                delta = jnp.sum(jnp.where(mask, prod, 0.0), axis=1, keepdims=True)
                sc = _dot_nt(qi, km[e]) + (cq_ref[rows, lane0:lane0 + 1] - ck[e])
                if masked:
                    sc = jnp.where(_causal(i, j, bq, bk), sc, NEG)
                p = jnp.exp(sc - lse_ref[rows, lane0:lane0 + 1])
                ds = p * (_dot_nt(dom, vj) - delta)
                dsb = ds.astype(BF16)
                dq = dq + _dot(dsb, km[e])
                out += [dk_a + _dot_tn(dsb, qi), dv_a + _dot_tn(p.astype(BF16), dom), dc_a - jnp.sum(ds, axis=0, keepdims=True)]
            dq_acc[rows, :] += dq * ATT_SCALE
            return tuple(out)

        def run(j):
            cols = pl.ds(j * bk, bk)
            ck = [cr_ref[pl.ds(2 * hp + e, 1), cols] for e in range(2)]
            carry = (jnp.zeros((bk, LANES), F32), jnp.zeros((bk, LANES), F32), jnp.zeros((1, bk), F32)) * 2
            n_diag = (j * bk + bk + bq - 1) // bq
            for i in range((j * bk) // bq, nq):
                carry = step(i, j, ck, carry, masked=i < n_diag)
            for e in range(2):
                dc_ref[e:e + 1, :] = carry[3 * e + 2]
            dz_ref[cols, LANES:2 * LANES] = jnp.where(_head_mask(0), carry[0], carry[3]).astype(BF16)
            dz_ref[cols, 2 * LANES:3 * LANES] = (carry[1] + carry[4]).astype(BF16)
            if j == nk - 1:
                dz_ref[:, 0:LANES] = dq_acc[...].astype(BF16)

        for k in range(nk):
            pl.when(pl.program_id(2) == k)(functools.partial(run, k))

    def seq(idx):
        return pl.BlockSpec((s, LANES), lambda b, hp, j: (b, idx(hp)))

    def kblk(c0):
        return pl.BlockSpec((bk, LANES), lambda b, hp, j: (b * nk + j, 3 * hp + c0))

    return pl.pallas_call(
        body, grid=(n_batch, N_PAIRS, nk),
        in_specs=[seq(lambda hp: 3 * hp), kblk(1), kblk(2), seq(lambda hp: hp), seq(lambda hp: N_PAIRS + hp),
                  seq(lambda hp: hp), seq(lambda hp: hp),
                  pl.BlockSpec((None, N_HEADS, s), lambda b, hp, j: (b, 0, 0)), pl.BlockSpec(memory_space=pl.ANY)],
        out_specs=[pl.BlockSpec((s, PAIR_WIDTH), lambda b, hp, j: (b, N_PAIRS + hp)),
                   pl.BlockSpec((None, None, 2, bk), lambda b, hp, j: (b, hp, 0, j))],
        out_shape=[jax.ShapeDtypeStruct(dz.shape, dz.dtype), jax.ShapeDtypeStruct((n_batch, N_PAIRS, 2, s), F32)],
        scratch_shapes=[pltpu.VMEM((s, LANES), F32)],
        input_output_aliases={8: 0},
        name=name, compiler_params=_params("parallel", "parallel", "arbitrary"),
    )(zf, zf, zf, o32, dy, lse, c_bc, c_row, dz)


def _dil_bias(slope, dil):
    qi = lax.broadcasted_iota(jnp.int32, (BLOCK, 2 * BLOCK), 0)
    kj = lax.broadcasted_iota(jnp.int32, (BLOCK, 2 * BLOCK), 1)
    delta = qi + BLOCK - kj
    return jnp.where((delta >= 0) & (delta <= BLOCK), (-slope * dil) * delta.astype(F32), NEG)


def _alibi_slope(hp, e):
    slope = jnp.float32(0.0)
    for k in range(N_PAIRS):
        slope = jnp.where(hp == k, jnp.float32(2.0 ** -(2 * k + e + 1)), slope)
    return slope


def _first_block_bias(bias):
    return jnp.where(lax.broadcasted_iota(jnp.int32, bias.shape, 1) < BLOCK, NEG, bias)


def _fill_bias(bias_scr, hp):
    for di, dil in enumerate(DILATIONS):
        for e in range(2):
            bias_scr[2 * di + e] = _dil_bias(_alibi_slope(hp, e), dil)


def _pair_specs(rows):
    return [pl.BlockSpec((rows, LANES), lambda b, hp, c0=c0: (b, 3 * hp + c0)) for c0 in range(3)]


def _strided(start, size, dil):
    return pl.ds(start, size) if dil == 1 else pl.ds(start, size, stride=dil)


def _for_each_block(dil, unit, group=3):
    span = BLOCK * dil
    nb = SEQ // span
    if dil == 1:
        assert (nb - 1) % group == 0
        unit(0, True)

        def later(g, c):
            for u in range(group):
                unit((1 + g * group + u) * span, False)
            return c

        if (nb - 1) // group == 1:
            later(0, 0)
        else:
            lax.fori_loop(0, (nb - 1) // group, later, 0)
        return
    group = 4
    per = dil // group

    def firsts(g, c):
        for u in range(group):
            unit(g * group + u, True)
        return c

    lax.fori_loop(0, per, firsts, 0)
    if nb > 1:
        def later(i, c):
            for u in range(group):
                unit((1 + i // per) * span + (i % per) * group + u, False)
            return c

        lax.fori_loop(0, (nb - 1) * per, later, 0)


QUARTER = SEQ // 4


def _to_quarters(src, dst):
    for r in range(4):
        dst[r * QUARTER:(r + 1) * QUARTER, :] = src[pl.ds(r, QUARTER, stride=4), :]


def _from_quarters(src, dst):
    for r in range(4):
        dst[pl.ds(r, QUARTER, stride=4), :] = src[r * QUARTER:(r + 1) * QUARTER, :]


def _for_each_quarter_block(dil, unit):
    stride = dil // 4
    nb = QUARTER // (BLOCK * stride)

    def firsts(r, c):
        for g in range(stride):
            unit(r * QUARTER + g, True, stride)
        return c

    if stride == 1:
        for r in range(4):
            firsts(r, 0)
    else:
        for r in range(4):
            firsts(r, 0)
    if nb > 1:
        assert stride == 1
        for n in range(1, nb):
            for r in range(4):
                unit(r * QUARTER + n * BLOCK, False, stride)


def _mix_weights(l1, l2, l3):
    m = jnp.maximum(jnp.maximum(l1, l2), l3)
    e1, e2, e3 = jnp.exp(l1 - m), jnp.exp(l2 - m), jnp.exp(l3 - m)
    inv = 1.0 / (e1 + e2 + e3)
    return e1 * inv, e2 * inv, e3 * inv


def _dil_fwd(zd, n_batch, name):
    s = SEQ
    t = n_batch * s

    def body(q_ref, k_ref, v_ref, y_ref, l1_ref, l2_ref, l3_ref, o_scr, qkv4, o4, l4, bias_scr):
        _fill_bias(bias_scr, pl.program_id(1))
        for a, ref in enumerate((q_ref, k_ref, v_ref)):
            _to_quarters(ref, qkv4.at[a])

        def unit(srcs, start, first, stride, di, o_dst, l_dst):
            qrows = _strided(start, BLOCK, stride)
            krows = qrows if first else _strided(start - BLOCK * stride, 2 * BLOCK, stride)
            q = (srcs[0][qrows, :] * ATT_SCALE).astype(BF16)
            kc = srcs[1][krows, :].astype(BF16)
            vc = srcs[2][krows, :].astype(BF16)
            if first:
                kc, vc = jnp.concatenate([kc, kc]), jnp.concatenate([vc, vc])
            outs, lses = [], []
            for e in range(2):
                bias = _first_block_bias(bias_scr[2 * di + e]) if first else bias_scr[2 * di + e]
                sc = _dot_nt(jnp.where(_head_mask(e), q, jnp.zeros_like(q)), kc) + bias
                m = jnp.max(sc, axis=1, keepdims=True)
                pe = jnp.exp(sc - m)
                l = jnp.sum(pe, axis=1, keepdims=True)
                outs.append(_dot((pe * (1.0 / l)).astype(BF16), vc))
                lses.append(m + jnp.log(l))
            o_dst[qrows, :] = jnp.where(_head_mask(0), outs[0], outs[1])
            l_dst[qrows, :] = jnp.where(_head_mask(0), lses[0], lses[1])

        token_order = (q_ref, k_ref, v_ref)
        quarters = tuple(qkv4.at[a] for a in range(3))
        _for_each_block(1, lambda start, first: unit(token_order, start, first, 1, 0, o_scr.at[0], l1_ref), group=15)
        for di in (1, 2):
            _for_each_quarter_block(DILATIONS[di], lambda start, first, stride, di=di: unit(
                quarters, start, first, stride, di, o4.at[di - 1], l4.at[di - 1]))
        for di, l_ref in ((1, l2_ref), (2, l3_ref)):
            _from_quarters(o4.at[di - 1], o_scr.at[di])
            _from_quarters(l4.at[di - 1], l_ref)
        w = _mix_weights(l1_ref[...], l2_ref[...], l3_ref[...])
        y_ref[...] = (w[0] * o_scr[0] + w[1] * o_scr[1] + w[2] * o_scr[2]).astype(BF16)

    blk = pl.BlockSpec((s, LANES), lambda b, hp: (b, hp))
    res = pl.pallas_call(
        body, grid=(n_batch, N_PAIRS),
        in_specs=_pair_specs(s),
        out_specs=[blk] * 4,
        out_shape=[jax.ShapeDtypeStruct((t, MIX_HALF), BF16)] + [jax.ShapeDtypeStruct((t, MIX_HALF), F32)] * 3,
        scratch_shapes=[pltpu.VMEM((3, s, LANES), F32), pltpu.VMEM((3, s, LANES), F32), pltpu.VMEM((2, s, LANES), F32),
                        pltpu.VMEM((2, s, LANES), F32), pltpu.VMEM((6, BLOCK, 2 * BLOCK), F32)],
        name=name, compiler_params=_params("parallel", "arbitrary"),
    )(zd, zd, zd)
    return res[0], res[1:]


def _dil_bwd(zd, dy, ya, lses, n_batch, name):
    s = SEQ
    t = n_batch * s

    def body(q_ref, k_ref, v_ref, dy_ref, ya_ref, l1_ref, l2_ref, l3_ref, dz_ref, w_scr, dy_scr, dot_scr, acc, st4, acc4, bias_scr):
        _fill_bias(bias_scr, pl.program_id(1))
        for di, w in enumerate(_mix_weights(l1_ref[...], l2_ref[...], l3_ref[...])):
            w_scr[di] = w
        dya = dy_ref[...].astype(F32)
        prod = dya * ya_ref[...].astype(F32)
        per_head = [jnp.sum(jnp.where(_head_mask(e), prod, 0.0), axis=1, keepdims=True) for e in range(2)]
        dy_scr[...] = dya
        dot_scr[...] = jnp.where(_head_mask(0), per_head[0], per_head[1])
        acc[...] = jnp.zeros_like(acc)
        acc4[...] = jnp.zeros_like(acc4)
        staged = (q_ref, k_ref, v_ref, w_scr.at[1], w_scr.at[2], l2_ref, l3_ref, dy_scr, dot_scr)
        for a, ref in enumerate(staged):
            _to_quarters(ref, st4.at[a])

        def unit(srcs, dst, start, first, stride, di):
            qrows = _strided(start, BLOCK, stride)
            krows = qrows if first else _strided(start - BLOCK * stride, 2 * BLOCK, stride)
            q = (srcs[0][qrows, :] * ATT_SCALE).astype(BF16)
            kc = srcs[1][krows, :].astype(BF16)
            vc = srcs[2][krows, :].astype(BF16)
            wq = srcs[3][qrows, :]
            lse = srcs[4][qrows, :]
            do = (wq * srcs[5][qrows, :]).astype(BF16)
            sub = wq * srcs[6][qrows, :]
            dq = jnp.zeros((BLOCK, LANES), F32)
            dk = jnp.zeros((krows.size, LANES), F32)
            dv = jnp.zeros((krows.size, LANES), F32)
            for e in range(2):
                mask = _head_mask(e)
                lane0 = HEAD_DIM * e
                qh = jnp.where(mask, q, jnp.zeros_like(q))
                doh = jnp.where(mask, do, jnp.zeros_like(do))
                bias = bias_scr[2 * di + e]
                sc = _dot_nt(qh, kc) + (bias[:, BLOCK:] if first else bias)
                p = jnp.exp(sc - lse[:, lane0:lane0 + 1])
                dsb = (p * (_dot_nt(doh, vc) - sub[:, lane0:lane0 + 1])).astype(BF16)
                dq = dq + _dot(dsb, jnp.where(mask, kc, jnp.zeros_like(kc)))
                dk = dk + _dot_tn(dsb, qh)
                dv = dv + _dot_tn(p.astype(BF16), doh)
            dst.at[0][qrows, :] += dq * ATT_SCALE
            dst.at[1][krows, :] += dk
            dst.at[2][krows, :] += dv

        token_order = (q_ref, k_ref, v_ref, w_scr.at[0], l1_ref, dy_scr, dot_scr)
        for n in range(SEQ // BLOCK):
            unit(token_order, acc, n * BLOCK, n == 0, 1, 0)
        for di in (1, 2):
            quarters = (st4.at[0], st4.at[1], st4.at[2], st4.at[2 + di], st4.at[4 + di], st4.at[7], st4.at[8])
            stride = DILATIONS[di] // 4
            for r in range(4):
                for g in range(stride):
                    for n in range(QUARTER // (BLOCK * stride)):
                        unit(quarters, acc4, r * QUARTER + n * BLOCK * stride + g, n == 0, stride, di)
        for k in range(3):
            for r in range(4):
                acc.at[k][pl.ds(r, QUARTER, stride=4), :] += acc4[k, r * QUARTER:(r + 1) * QUARTER, :]
            dz_ref[:, k * LANES:(k + 1) * LANES] = acc[k].astype(BF16)

    blk = pl.BlockSpec((s, LANES), lambda b, hp: (b, hp))
    pair = pl.BlockSpec((s, PAIR_WIDTH), lambda b, hp: (b, hp))
    return pl.pallas_call(
        body, grid=(n_batch, N_PAIRS),
        in_specs=_pair_specs(s) + [blk] * 5,
        out_specs=pair,
        out_shape=jax.ShapeDtypeStruct((t, 2 * 3 * MIX_HALF), BF16),
        scratch_shapes=[pltpu.VMEM((3, s, LANES), F32), pltpu.VMEM((s, LANES), F32), pltpu.VMEM((s, LANES), F32),
                        pltpu.VMEM((3, s, LANES), F32), pltpu.VMEM((9, s, LANES), F32), pltpu.VMEM((3, s, LANES), F32),
                        pltpu.VMEM((6, BLOCK, 2 * BLOCK), F32)],
        name=name, compiler_params=_params("parallel", "arbitrary"),
    )(zd, zd, zd, dy, ya, *lses)


X_BQ = 2048


def _xattn_probs(q, k):
    sc = _dot_nt(q, k) * X_SCALE
    pe = jnp.exp(sc - jnp.max(sc, axis=1, keepdims=True))
    return pe / jnp.sum(pe, axis=1, keepdims=True)


def _xattn_fwd(qx, kx, vx, n_batch, name):
    nq = SEQ // X_BQ

    def body(q_ref, k_ref, v_ref, o_ref):
        p = _xattn_probs(q_ref[...], k_ref[...])
        o_ref[...] = _dot(p.astype(BF16), v_ref[...]).astype(BF16)

    qblk = pl.BlockSpec((X_BQ, X_HEAD_DIM), lambda b, h, i: (b * nq + i, h))
    kblk = pl.BlockSpec((N_MEM, X_HEAD_DIM), lambda b, h, i: (b, h))
    return pl.pallas_call(
        body, grid=(n_batch, X_HEADS, nq), in_specs=[qblk, kblk, kblk], out_specs=qblk,
        out_shape=jax.ShapeDtypeStruct(qx.shape, BF16),
        name=name, compiler_params=_params("parallel", "parallel", "arbitrary"),
    )(qx, kx, vx)


def _xattn_bwd(qx, kx, vx, dox, n_batch, name):
    nq = SEQ // X_BQ

    def body(q_ref, k_ref, v_ref, do_ref, dq_ref, dk_ref, dv_ref, dk_acc, dv_acc):
        i = pl.program_id(2)

        @pl.when(i == 0)
        def _():
            dk_acc[...] = jnp.zeros_like(dk_acc)
            dv_acc[...] = jnp.zeros_like(dv_acc)

        q, k, do = q_ref[...], k_ref[...], do_ref[...]
        p = _xattn_probs(q, k)
        dp = _dot_nt(do, v_ref[...])
        dsb = (p * (dp - jnp.sum(p * dp, axis=1, keepdims=True))).astype(BF16)
        dq_ref[...] = (_dot(dsb, k) * X_SCALE).astype(BF16)
        dk_acc[...] += _dot_tn(dsb, q) * X_SCALE
        dv_acc[...] += _dot_tn(p.astype(BF16), do)

        @pl.when(i == nq - 1)
        def _():
            dk_ref[...] = dk_acc[...].astype(BF16)
            dv_ref[...] = dv_acc[...].astype(BF16)

    qblk = pl.BlockSpec((X_BQ, X_HEAD_DIM), lambda b, h, i: (b * nq + i, h))
    kblk = pl.BlockSpec((N_MEM, X_HEAD_DIM), lambda b, h, i: (b, h))
    return pl.pallas_call(
        body, grid=(n_batch, X_HEADS, nq), in_specs=[qblk, kblk, kblk, qblk], out_specs=[qblk, kblk, kblk],
        out_shape=[jax.ShapeDtypeStruct(qx.shape, BF16), jax.ShapeDtypeStruct(kx.shape, BF16), jax.ShapeDtypeStruct(kx.shape, BF16)],
        scratch_shapes=[pltpu.VMEM((N_MEM, X_HEAD_DIM), F32)] * 2,
        name=name, compiler_params=_params("parallel", "parallel", "arbitrary"),
    )(qx, kx, vx, dox)


def _adamw(w, g, m, v, name, rows):
    r, c = w.shape
    assert r % rows == 0, (name, w.shape, rows)

    def body(w_ref, g_ref, m_ref, v_ref, d_ref, nm_ref, nv_ref):
        gv = g_ref[...]
        m1 = ADAM_B1 * m_ref[...] + (1.0 - ADAM_B1) * gv
        v1 = ADAM_B2 * v_ref[...] + (1.0 - ADAM_B2) * jnp.square(gv)
        m_hat = m1 / (1.0 - ADAM_B1 ** ADAM_STEP)
        v_hat = v1 / (1.0 - ADAM_B2 ** ADAM_STEP)
        d_ref[...] = -ADAM_LR * (m_hat / (jnp.sqrt(v_hat) + ADAM_EPS) + ADAM_WD * w_ref[...])
        nm_ref[...] = m1
        nv_ref[...] = v1

    blk = pl.BlockSpec((rows, c), lambda i: (i, 0))
    return pl.pallas_call(
        body, grid=(r // rows,), in_specs=[blk] * 4, out_specs=[blk] * 3,
        out_shape=[jax.ShapeDtypeStruct((r, c), F32)] * 3,
        name=name, compiler_params=_params("arbitrary"),
    )(w, g, m, v)


def _relu2(acc):
    a = jnp.maximum(acc, 0.0)
    return acc, a * a


def _relu2_bwd(acc, u):
    return (2.0 * jnp.maximum(u.astype(F32), 0.0) * acc,)


def _local_step(x, mem, target, vecs, w_in, late_weights, on_grads=None):
    n_batch = x.shape[0]
    t = n_batch * SEQ
    x0 = x.reshape(t, D_MODEL)
    mem2 = mem.reshape(n_batch * N_MEM, D_MODEL)
    tgt = target.reshape(t, D_MODEL)

    half = 3 * MIX_HALF
    w_qkv = jnp.concatenate([_pair_major(w_in[:, :half]), _pair_major(w_in[:, half:QKV_WIDTH])], axis=1)
    w_gate = jnp.pad(w_in[:, QKV_WIDTH:], ((0, 0), (0, GATE_PAD - N_HEADS)))
    b_pad = jnp.pad(vecs["b_forget"], (0, GATE_PAD - N_HEADS)).reshape(1, GATE_PAD)

    h1, zd, zf, gate = _in_proj(x0, vecs["g_mix"], jnp.concatenate([w_qkv, w_gate], axis=1), "in_proj")
    mn = _rmsnorm(mem2, vecs["g_mem"], "norm_mem")
    c_bc, c_row, sg = _gate_fwd(gate, b_pad, n_batch, "gate_fwd")
    ya, lses = _dil_fwd(zd, n_batch, "dil_fwd")
    yf, of32, lse_f = _fox_fwd(zf, c_bc, c_row, n_batch, "fox_fwd")
    wts = late_weights(yf)
    w_out = wts["w_out"]
    x1, h2 = _matmul_res_norm([ya, yf], [w_out[:MIX_HALF], w_out[MIX_HALF:]], x0, vecs["g_xattn"], "out")
    qx = _matmul(h2, wts["w_xq"], "xq", out_dtypes=(BF16,))[0]
    kx = _matmul(mn, wts["w_xk"], "xk", out_dtypes=(BF16,))[0]
    vx = _matmul(mn, wts["w_xv"], "xv", out_dtypes=(BF16,))[0]
    ox = _xattn_fwd(qx, kx, vx, n_batch, "xattn_fwd")
    x2, h3 = _matmul_res_norm([ox], [wts["w_xo"]], x1, vecs["g_mlp"], "xo")
    u, a2 = _matmul(h3, wts["w_up"], "mlp_up", out_dtypes=(BF16, BF16), epilogue=_relu2, tn=1024)
    loss, dx3, dx3b, dg_final = _loss_bwd(a2, wts["w_down"], x2, vecs["g_final"], tgt, "mlp_down_loss")

    du = _matmul(dx3b, wts["w_down"], "mlp_down_bwd", out_dtypes=(BF16,), extras=(u,), epilogue=_relu2_bwd, tn=1024, w_t=True)[0]
    shards = (N_CHIPS, 2 * D_MODEL, D_MODEL)
    g_mlp = _matmul_tn(h3, du, "gw_up", packed=(shards, lambda i, j: (j, 0, 0), None))
    g_mlp = _matmul_tn(a2, dx3b, "gw_down", packed=(shards, lambda i, j: (i, 1, 0), g_mlp))
    gw_up = g_mlp[:, :D_MODEL].transpose(1, 0, 2).reshape(D_MODEL, D_FF)
    gw_down = g_mlp[:, D_MODEL:].reshape(D_FF, D_MODEL)
    token = on_grads("mlp", g_mlp) if on_grads else None
    dx2, dx2b, dg_mlp = _matmul_rms_bwd([du], [wts["w_up"]], x2, vecs["g_mlp"], dx3, "mlp_up_bwd", after=token)

    gw_xo = _matmul_tn(ox, dx2b, "gw_xo")
    dox = _matmul(dx2b, wts["w_xo"], "xo_bwd", out_dtypes=(BF16,), w_t=True)[0]
    dqx, dkx, dvx = _xattn_bwd(qx, kx, vx, dox, n_batch, "xattn_bwd")
    gw_xq = _matmul_tn(h2, dqx, "gw_xq")
    gw_xk = _matmul_tn(mn, dkx, "gw_xk")
    gw_xv = _matmul_tn(mn, dvx, "gw_xv")
    dmn = _matmul(dkx, wts["w_xk"], "xk_bwd", w_t=True)[0]
    dmn = _matmul_res(dvx, wts["w_xv"], dmn, "xv_bwd", w_t=True)
    _, _, dg_mem = _rms_bwd(mem2, dmn, vecs["g_mem"], None, "norm_mem_bwd")
    dx1, dx1b, dg_xattn = _matmul_rms_bwd([dqx], [wts["w_xq"]], x1, vecs["g_xattn"], dx2, "xq_bwd")

    gw_out = jnp.concatenate([_matmul_tn(ya, dx1b, "gw_out_a"), _matmul_tn(yf, dx1b, "gw_out_f")], axis=0)
    token = on_grads("mid", dict(w_out=gw_out, w_xq=gw_xq, w_xk=gw_xk, w_xv=gw_xv, w_xo=gw_xo)) if on_grads else None
    dy = _matmul(dx1b, w_out, "out_bwd", out_dtypes=(BF16,), w_t=True, after=token)[0]
    dz = _dil_bwd(zd, dy, ya, lses, n_batch, "dil_bwd")
    dz, dc = _fox_bwd(zf, of32, dy, lse_f, c_bc, c_row, dz, n_batch, "fox_bwd")
    dzg, db = _gate_bwd(dc.reshape(n_batch, N_HEADS, SEQ), sg, "gate_bwd")
    gw_pm = _matmul_tn(h1, dz, "gw_in_qkv")
    gw_in = jnp.concatenate([_pair_major_inv(gw_pm[:, :half]), _pair_major_inv(gw_pm[:, half:]),
                             _matmul_tn(h1, dzg, "gw_in_gate")[:, :N_HEADS]], axis=1)
    dx0, _, dg_mix = _matmul_rms_bwd([dz, dzg], [w_qkv, w_gate], x0, vecs["g_mix"], dx1, "in_bwd")

    gw = dict(w_in=gw_in, w_out=gw_out, w_xq=gw_xq, w_xk=gw_xk, w_xv=gw_xv, w_xo=gw_xo, w_up=gw_up, w_down=gw_down)
    gv = dict(g_mix=dg_mix, g_xattn=dg_xattn, g_mem=dg_mem, g_mlp=dg_mlp, g_final=dg_final, b_forget=db)
    return loss, dx0.reshape(x.shape), gw, gv


MESH = pl.DeviceIdType.MESH
ANY = pl.BlockSpec(memory_space=pl.ANY)


def _place():
    x, y, c = lax.axis_index("x"), lax.axis_index("y"), lax.axis_index("c")
    other_chips = [(1 - x, y), (x, 1 - y), (1 - x, 1 - y)]
    return x, y, c, other_chips


def _my_chip():
    return 2 * lax.axis_index("x") + lax.axis_index("y")


def _halves(rows, c, align):
    half = rows // 2
    assert rows % (2 * align) == 0, rows
    return pl.ds(pl.multiple_of(c * half, align), half), pl.ds(pl.multiple_of((1 - c) * half, align), half)


def _place_own(wall, pack):
    return lax.dynamic_update_slice(wall, pack[None], (_my_chip(), 0, 0))


def _gather(pack, name, after):
    def body(p_ref, after_ref, out_ref, send_sems, recv_sems, pass_send, pass_recv):
        del after_ref
        x, y, c, chips = _place()
        me = 2 * x + y
        mine, theirs = _halves(pack.shape[0], c, 16)

        def from_chip(k, chip, rows):
            src = out_ref.at[2 * chip[0] + chip[1], rows]
            return pltpu.make_async_remote_copy(src_ref=src, dst_ref=src, send_sem=send_sems.at[k], recv_sem=recv_sems.at[k],
                                                device_id=(chip[0], chip[1], c), device_id_type=MESH)

        def passed(k, chip, rows):
            src = out_ref.at[2 * chip[0] + chip[1], rows]
            return pltpu.make_async_remote_copy(src_ref=src, dst_ref=src, send_sem=pass_send.at[k], recv_sem=pass_recv.at[k],
                                                device_id=(x, y, 1 - c), device_id_type=MESH)

        sends = []
        for k, chip in enumerate(chips):
            cp = pltpu.make_async_remote_copy(src_ref=p_ref.at[mine], dst_ref=out_ref.at[me, mine], send_sem=send_sems.at[k],
                                              recv_sem=recv_sems.at[k], device_id=(chip[0], chip[1], c), device_id_type=MESH)
            cp.start()
            sends.append(cp)
        for k, chip in enumerate(chips):
            from_chip(k, chip, mine).wait_recv()
            cp = passed(k, chip, mine)
            cp.start()
            sends.append(cp)
        for k, chip in enumerate(chips):
            passed(k, chip, theirs).wait_recv()
        for cp in sends:
            cp.wait_send()

    wall = pl.pallas_call(
        body, in_specs=[ANY, ANY], out_specs=ANY,
        out_shape=jax.ShapeDtypeStruct((N_CHIPS,) + pack.shape, pack.dtype),
        scratch_shapes=[pltpu.SemaphoreType.DMA((3,))] * 4,
        name=name,
    )(pack, after)
    return _place_own(wall, pack)


HBM = pl.BlockSpec(memory_space=pltpu.HBM)
SEM = pl.BlockSpec(memory_space=pltpu.SEMAPHORE)
SPLIT_COPY = pltpu.CompilerParams(has_side_effects=pltpu.SideEffectType.DATAFLOW_SIDE_EFFECTING)


def _in_hbm(a):
    return pltpu.with_memory_space_constraint(a, pltpu.HBM)


def _start_call(start, src, land_shape, after, name):
    land = lax.empty(land_shape, src.dtype)

    def body(src_ref, land_ref, after_ref, send_sems, recv_sems, src_thru, land_thru, token):
        del after_ref, src_thru, land_thru
        start(src_ref, land_ref, send_sems, recv_sems)
        token[...] = jnp.zeros_like(token)

    return pl.pallas_call(
        body, name=name,
        out_shape=(pltpu.SemaphoreType.DMA((3,)), pltpu.SemaphoreType.DMA((3,)), pltpu.HBM(src.shape, src.dtype),
                   pltpu.HBM(land_shape, src.dtype), jax.ShapeDtypeStruct((8, LANES), F32)),
        in_specs=(HBM, HBM, ANY), out_specs=(SEM, SEM, HBM, HBM, pl.BlockSpec(memory_space=pltpu.VMEM)),
        input_output_aliases={0: 2, 1: 3}, compiler_params=SPLIT_COPY,
    )(_in_hbm(src), _in_hbm(land), after)


def _wait_call(body, started, after, name):
    send_sems, recv_sems, src, land, _ = started
    return pl.pallas_call(
        body, name=name,
        out_shape=(pltpu.HBM(src.shape, src.dtype), pltpu.HBM(land.shape, land.dtype)),
        in_specs=(HBM, HBM, SEM, SEM, ANY), out_specs=(HBM, HBM),
        input_output_aliases={0: 0, 1: 1}, compiler_params=SPLIT_COPY,
    )(src, land, send_sems, recv_sems, after)


def _gather_copies(p_ref, wall_ref, send_sems, recv_sems):
    x, y, c, chips = _place()
    me = 2 * x + y
    mine, _ = _halves(p_ref.shape[0], c, 16)
    out, back = [], []
    for k, chip in enumerate(chips):
        peer = dict(send_sem=send_sems.at[k], recv_sem=recv_sems.at[k], device_id=(chip[0], chip[1], c), device_id_type=MESH)
        out.append(pltpu.make_async_remote_copy(src_ref=p_ref.at[mine], dst_ref=wall_ref.at[me, mine], **peer))
        slab = wall_ref.at[2 * chip[0] + chip[1], mine]
        back.append(pltpu.make_async_remote_copy(src_ref=slab, dst_ref=slab, **peer))
    return out, back


def _gather_start(pack, after, name):
    def start(p_ref, wall_ref, send_sems, recv_sems):
        for cp in _gather_copies(p_ref, wall_ref, send_sems, recv_sems)[0]:
            cp.start()

    return _start_call(start, pack, (N_CHIPS,) + pack.shape, after, name)


def _gather_wait(started, after, name):
    def body(p_ref, wall_ref, send_sems, recv_sems, after_ref, p_dead, wall_out):
        del after_ref, p_dead, wall_out
        out, back = _gather_copies(p_ref, wall_ref, send_sems, recv_sems)
        for cp_out, cp_back in zip(out, back):
            cp_out.wait_send()
            cp_back.wait_recv()

    return _wait_call(body, started, after, name)


def _pass_on(wall, name):
    def body(w_in_ref, out_ref, send_sems, recv_sems):
        del w_in_ref
        x, y, c, chips = _place()
        mine, theirs = _halves(wall.shape[1], c, 16)
        sends = []
        for k, chip in enumerate(chips):
            slab = out_ref.at[2 * chip[0] + chip[1]]
            peer = dict(send_sem=send_sems.at[k], recv_sem=recv_sems.at[k], device_id=(x, y, 1 - c), device_id_type=MESH)
            cp = pltpu.make_async_remote_copy(src_ref=slab.at[mine], dst_ref=slab.at[mine], **peer)
            cp.start()
            sends.append((cp, pltpu.make_async_remote_copy(src_ref=slab.at[theirs], dst_ref=slab.at[theirs], **peer)))
        for cp, back in sends:
            back.wait_recv()
            cp.wait_send()

    return pl.pallas_call(
        body, in_specs=[ANY], out_specs=ANY, out_shape=jax.ShapeDtypeStruct(wall.shape, wall.dtype),
        scratch_shapes=[pltpu.SemaphoreType.DMA((3,))] * 2, input_output_aliases={0: 0}, name=name,
    )(wall)


def _swap_halves(g, name):
    half = g.shape[1] // 2

    def body(g_ref, out_ref, send_sem, recv_sem):
        x, y, c, _ = _place()
        _, theirs = _halves(g.shape[1], c, 8)
        cp = pltpu.make_async_remote_copy(src_ref=g_ref.at[:, theirs], dst_ref=out_ref, send_sem=send_sem, recv_sem=recv_sem,
                                          device_id=(x, y, 1 - c), device_id_type=MESH)
        cp.start()
        cp.wait()

    return pl.pallas_call(
        body, in_specs=[ANY], out_specs=ANY,
        out_shape=jax.ShapeDtypeStruct((N_CHIPS, half, D_MODEL), F32),
        scratch_shapes=[pltpu.SemaphoreType.DMA, pltpu.SemaphoreType.DMA],
        name=name,
    )(g)


def _core_index():
    return lax.axis_index("c").astype(jnp.int32).reshape(1)


def _row_tile(half):
    tile = max(t for t in range(16, 1025, 16) if half % t == 0)
    return tile, half // tile


def _add_sibling(g, got, name):
    half = g.shape[1] // 2
    tile, n_tiles = _row_tile(half)

    def body(c_ref, g_ref, got_ref, o_ref):
        o_ref[...] = (g_ref[...] + got_ref[...]).astype(BF16)

    blk = pl.BlockSpec((None, tile, D_MODEL), lambda s, i, c_ref: (s, i, 0))
    return pl.pallas_call(
        body,
        grid_spec=pltpu.PrefetchScalarGridSpec(
            num_scalar_prefetch=1, grid=(N_CHIPS, n_tiles),
            in_specs=[pl.BlockSpec((None, tile, D_MODEL), lambda s, i, c_ref: (s, c_ref[0] * n_tiles + i, 0)), blk],
            out_specs=blk),
        out_shape=jax.ShapeDtypeStruct((N_CHIPS, half, D_MODEL), BF16),
        name=name, compiler_params=_params("arbitrary", "arbitrary"),
    )(_core_index(), g, got)


def _exchange_copies(p_ref, land_ref, send_sems, recv_sems):
    x, y, c, chips = _place()
    me = 2 * x + y
    out, back = [], []
    for k, chip in enumerate(chips):
        peer = dict(send_sem=send_sems.at[k], recv_sem=recv_sems.at[k], device_id=(chip[0], chip[1], c), device_id_type=MESH)
        out.append(pltpu.make_async_remote_copy(src_ref=p_ref.at[2 * chip[0] + chip[1]], dst_ref=land_ref.at[me], **peer))
        slab = land_ref.at[2 * chip[0] + chip[1]]
        back.append(pltpu.make_async_remote_copy(src_ref=slab, dst_ref=slab, **peer))
    return out, back


def _with_own(got, part):
    me = _my_chip()
    return lax.dynamic_update_slice(got, lax.dynamic_slice(part, (me, 0, 0), (1,) + part.shape[1:]), (me, 0, 0))


def _exchange_chips(part, name):
    def body(p_ref, out_ref, send_sems, recv_sems):
        out, back = _exchange_copies(p_ref, out_ref, send_sems, recv_sems)
        for cp in out:
            cp.start()
        for cp in back:
            cp.wait_recv()
        for cp in out:
            cp.wait_send()

    got = pl.pallas_call(
        body, in_specs=[ANY], out_specs=ANY,
        out_shape=jax.ShapeDtypeStruct(part.shape, part.dtype),
        scratch_shapes=[pltpu.SemaphoreType.DMA((3,)), pltpu.SemaphoreType.DMA((3,))],
        name=name,
    )(part)
    return _with_own(got, part)


def _exchange_start(part, name):
    def start(p_ref, land_ref, send_sems, recv_sems):
        for cp in _exchange_copies(p_ref, land_ref, send_sems, recv_sems)[0]:
            cp.start()

    return _start_call(start, part, part.shape, _core_index(), name)


def _exchange_wait(started, after, name):
    def body(p_ref, land_ref, send_sems, recv_sems, after_ref, p_dead, land_out):
        del after_ref, p_dead, land_out
        out, back = _exchange_copies(p_ref, land_ref, send_sems, recv_sems)
        for cp_out, cp_back in zip(out, back):
            cp_out.wait_send()
            cp_back.wait_recv()

    part, got = _wait_call(body, started, after, name)
    return _with_own(got, part)


def _sum_chips(parts, name):
    half = parts.shape[1]
    tile, n_tiles = _row_tile(half)

    def body(c_ref, p0, p1, p2, p3, o_ref):
        f32 = lambda p: p[...].astype(F32)
        o_ref[...] = ((f32(p0) + f32(p1)) + f32(p2)) + f32(p3)

    def slab(s):
        return pl.BlockSpec((None, tile, D_MODEL), lambda i, c_ref, s=s: (s, i, 0))

    return pl.pallas_call(
        body,
        grid_spec=pltpu.PrefetchScalarGridSpec(
            num_scalar_prefetch=1, grid=(n_tiles,),
            in_specs=[slab(s) for s in range(N_CHIPS)],
            out_specs=pl.BlockSpec((None, tile, D_MODEL), lambda i, c_ref: (c_ref[0], i, 0))),
        out_shape=jax.ShapeDtypeStruct((2, half, D_MODEL), F32),
        name=name, compiler_params=_params("arbitrary"),
    )(_core_index(), parts, parts, parts, parts)


def _share_halves(halves, name):
    def body(h_ref, out_ref, send_sem, recv_sem):
        del h_ref
        x, y, c, _ = _place()
        cp = pltpu.make_async_remote_copy(src_ref=out_ref.at[c], dst_ref=out_ref.at[c], send_sem=send_sem, recv_sem=recv_sem,
                                          device_id=(x, y, 1 - c), device_id_type=MESH)
        cp.start()
        pltpu.make_async_remote_copy(src_ref=out_ref.at[1 - c], dst_ref=out_ref.at[1 - c], send_sem=send_sem, recv_sem=recv_sem,
                                     device_id=(x, y, 1 - c), device_id_type=MESH).wait_recv()
        cp.wait_send()

    return pl.pallas_call(
        body, in_specs=[ANY], out_specs=ANY,
        out_shape=jax.ShapeDtypeStruct(halves.shape, halves.dtype),
        scratch_shapes=[pltpu.SemaphoreType.DMA] * 2,
        input_output_aliases={0: 0},
        name=name,
    )(halves)


def _reduce_parts(g, tag):
    return _add_sibling(g, _swap_halves(g, "swap_" + tag), "add_" + tag)


def _reduce_finish(got, tag):
    halves = _share_halves(_sum_chips(got, "sum_" + tag), "share_" + tag)
    return halves.reshape(2 * halves.shape[1], D_MODEL)


SMALL_ROWS = 8


def _allreduce_small(v):
    def body(v_ref, out_ref, buf, send_sems, recv_sems):
        x, y, c, _ = _place()
        buf[4 * x + 2 * y + c] = v_ref[...]
        sends = []
        for k in range(1, N_DEV):
            px = 1 - x if k & 4 else x
            py = 1 - y if k & 2 else y
            pc = 1 - c if k & 1 else c
            cp = pltpu.make_async_remote_copy(src_ref=v_ref, dst_ref=buf.at[4 * x + 2 * y + c], send_sem=send_sems.at[k - 1],
                                              recv_sem=recv_sems.at[k - 1], device_id=(px, py, pc), device_id_type=MESH)
            cp.start()
            sends.append((cp, 4 * px + 2 * py + pc))
        for k, (cp, peer) in enumerate(sends):
            pltpu.make_async_remote_copy(src_ref=v_ref, dst_ref=buf.at[peer], send_sem=send_sems.at[k], recv_sem=recv_sems.at[k],
                                         device_id=(x, y, c), device_id_type=MESH).wait_recv()
        for cp, _ in sends:
            cp.wait_send()
        total = buf[0]
        for d in range(1, N_DEV):
            total = total + buf[d]
        out_ref[...] = total

    vmem = pl.BlockSpec(memory_space=pltpu.VMEM)
    return pl.pallas_call(
        body, in_specs=[vmem], out_specs=vmem,
        out_shape=jax.ShapeDtypeStruct(v.shape, v.dtype),
        scratch_shapes=[pltpu.VMEM((N_DEV,) + v.shape, v.dtype), pltpu.SemaphoreType.DMA((N_DEV - 1,)),
                        pltpu.SemaphoreType.DMA((N_DEV - 1,))],
        name="allreduce_small",
    )(v)


MATRICES = ("w_in", "w_out", "w_xq", "w_xk", "w_xv", "w_xo", "w_up", "w_down")
VECTORS = ("g_mix", "g_xattn", "g_mem", "g_mlp", "g_final", "b_forget")
WEIGHT_ORDER = ("g_mix", "w_in", "b_forget", "w_out", "g_xattn", "g_mem", "w_xq", "w_xk", "w_xv", "w_xo",
                "g_mlp", "w_up", "w_down", "g_final")
GROUPS = {"mlp": ("w_up", "w_down"), "mid": ("w_out", "w_xq", "w_xk", "w_xv", "w_xo"), "in": ("w_in",)}
LATE = GROUPS["mid"] + GROUPS["mlp"]
W_IN_SHARD = IN_WIDTH // N_CHIPS
SHARD_ROWS = {"w_in": W_IN_SHARD, "w_out": 256, "w_xq": 256, "w_xk": 256, "w_xv": 256, "w_xo": 256, "w_up": 1024, "w_down": 1024}
PACK_ROWS = {n: -(-r // 32) * 32 for n, r in SHARD_ROWS.items()}
ADAM_ROWS = 128


def _pack(parts, names):
    return jnp.concatenate([jnp.pad(parts[n], ((0, PACK_ROWS[n] - SHARD_ROWS[n]), (0, 0))) for n in names], axis=0)


def _unpack(a, names):
    out, pos = {}, 0
    for n in names:
        out[n] = a[..., pos:pos + SHARD_ROWS[n], :]
        pos += PACK_ROWS[n]
    return out


def _full_weights(wall, names):
    cols = lambda a: a.transpose(1, 0, 2).reshape(a.shape[1], -1)
    rows = lambda a: a.reshape(-1, a.shape[-1])
    out = {}
    for n, a in _unpack(wall, names).items():
        if n == "w_in":
            out[n] = cols(a.reshape(N_CHIPS, D_MODEL, W_IN_SHARD))
        else:
            out[n] = cols(a) if n == "w_up" else rows(a)
    return out


def _shard_of(g, name, s):
    if name == "w_in":
        return g[:, s * W_IN_SHARD:(s + 1) * W_IN_SHARD].reshape(W_IN_SHARD, D_MODEL)
    if name == "w_up":
        return g[:, s * D_MODEL:(s + 1) * D_MODEL]
    n = SHARD_ROWS[name]
    return g[s * n:(s + 1) * n]


def _pack_grads(gws, names):
    return jnp.stack([_pack({n: _shard_of(gws[n], n, s) for n in names}, names) for s in range(N_CHIPS)])


def kernel(x, mem, g_mix, w_in, b_forget, w_out, g_xattn, g_mem, w_xq, w_xk, w_xv, w_xo, g_mlp, w_up, w_down, g_final, loss_target, m_g_mix, m_w_in, m_b_forget, m_w_out, m_g_xattn, m_g_mem, m_w_xq, m_w_xk, m_w_xv, m_w_xo, m_g_mlp, m_w_up, m_w_down, m_g_final, v_g_mix, v_w_in, v_b_forget, v_w_out, v_g_xattn, v_g_mem, v_w_xq, v_w_xk, v_w_xv, v_w_xo, v_g_mlp, v_w_up, v_w_down, v_g_final):
    given = dict(locals())
    weights = {n: given[n] for n in WEIGHT_ORDER}
    vecs = {n: weights[n] for n in VECTORS}

    shard = {n: weights[n].astype(BF16) for n in MATRICES}
    shard["w_in"] = shard["w_in"].reshape(W_IN_SHARD, D_MODEL)
    in_pack, late_pack = _pack(shard, GROUPS["in"]), _pack(shard, LATE)
    in_wall = _gather(in_pack, "gather_in", in_pack)
    late = _gather_start(late_pack, in_wall, "gather_late_start")
    w_in_full = _full_weights(in_wall, GROUPS["in"])["w_in"]

    def late_weights(after):
        pack, wall = _gather_wait(late, after, "gather_late_wait")
        return _full_weights(_place_own(_pass_on(wall, "gather_late_pass"), pack), LATE)

    started = {}

    def on_grads(group, gws):
        packed = gws if group == "mlp" else _pack_grads(gws, GROUPS[group])
        part = _reduce_parts(packed, group)
        started[group] = _exchange_start(part, "exchange_%s_start" % group)
        return started[group][4]

    loss, grad_x, gw, gv = _local_step(x, mem, loss_target, vecs, w_in_full, late_weights, on_grads)

    on_grads("in", gw)
    grads, delta, new_m, new_v = {}, {}, {}, {}

    def finish(group, after):
        got = _exchange_wait(started[group], after, "exchange_%s_wait" % group)
        for n, a in _unpack(_reduce_finish(got, group), GROUPS[group]).items():
            grads[n] = a.reshape(weights[n].shape)
            delta[n], new_m[n], new_v[n] = _adamw(weights[n], grads[n], given["m_" + n], given["v_" + n], "adamw_" + n, ADAM_ROWS)
        return new_v[GROUPS[group][-1]]

    after = finish("mlp", started["in"][4])
    after = finish("mid", after)

    row = lambda a: jnp.pad(a.reshape(-1), (0, D_MODEL - a.size)).reshape(1, D_MODEL)
    small = jnp.concatenate([gv[n] for n in VECTORS[:5]] + [row(gv["b_forget"][:, 0]), row(loss[0, :1]),
                             jnp.zeros((1, D_MODEL), F32)], axis=0)
    small = _allreduce_small(small)
    for k, n in enumerate(VECTORS[:5]):
        grads[n] = small[k]
    grads["b_forget"] = small[5, :N_HEADS]
    loss_total = small[6, 0]
    finish("in", after)

    stack = lambda prefix: jnp.concatenate([row(given[prefix + n]) for n in VECTORS] + [jnp.zeros((2, D_MODEL), F32)], axis=0)
    g_small = jnp.concatenate([small[:6], jnp.zeros((2, D_MODEL), F32)], axis=0)
    d, m1, v1 = _adamw(stack(""), g_small, stack("m_"), stack("v_"), "adamw_vectors", SMALL_ROWS)
    for k, n in enumerate(VECTORS):
        width = weights[n].shape[0]
        delta[n], new_m[n], new_v[n] = d[k, :width], m1[k, :width], v1[k, :width]

    return (loss_total, grad_x, *[grads[n] for n in WEIGHT_ORDER], *[delta[n] for n in WEIGHT_ORDER],
            *[new_m[n] for n in WEIGHT_ORDER], *[new_v[n] for n in WEIGHT_ORDER])
```

```python
import functools
import math

import jax
import jax.numpy as jnp
from jax import lax
from jax.experimental import pallas as pl
from jax.experimental.pallas import tpu as pltpu

F32 = jnp.float32
BF16 = jnp.bfloat16

D_MODEL = 1024
SEQ = 2048
N_MEM = 256
HEAD_DIM = 64
N_HEADS = 8
MIX_HALF = N_HEADS * HEAD_DIM
QKV_WIDTH = 6 * MIX_HALF
IN_WIDTH = QKV_WIDTH + N_HEADS
GATE_PAD = 128
BLOCK = 128
DILATIONS = (1, 4, 16)
X_HEADS = 4
X_HEAD_DIM = 256
D_FF = 4096
EPS = 1e-6
NEG = -1e30
ATT_SCALE = 1.0 / math.sqrt(HEAD_DIM)
X_SCALE = 1.0 / math.sqrt(X_HEAD_DIM)
LANES = 128
N_CHIPS = 4
N_DEV = 8

ADAM_LR = 0.001
ADAM_B1 = 0.9
ADAM_B2 = 0.999
ADAM_EPS = 1e-08
ADAM_WD = 0.01
ADAM_STEP = 10

VMEM_LIMIT = 48 * 1024 * 1024


def _params(*sem):
    return pltpu.CompilerParams(dimension_semantics=sem or None, vmem_limit_bytes=VMEM_LIMIT)


def _dot(a, b):
    return jnp.dot(a, b, preferred_element_type=F32)


def _dot_nt(a, b):
    return lax.dot_general(a, b, (((1,), (1,)), ((), ())), preferred_element_type=F32)


def _dot_tn(a, b):
    return lax.dot_general(a, b, (((0,), (0,)), ((), ())), preferred_element_type=F32)


def _dot_exact(x, e):
    hi = x.astype(BF16)
    r1 = x - hi.astype(F32)
    mid = r1.astype(BF16)
    lo = (r1 - mid.astype(F32)).astype(BF16)
    return _dot(hi, e) + _dot(mid, e) + _dot(lo, e)


def _head_mask(e):
    lane = lax.broadcasted_iota(jnp.int32, (1, LANES), 1)
    return (lane >= HEAD_DIM * e) & (lane < HEAD_DIM * (e + 1))


def _matmul(a, w, name, out_dtypes=(F32,), extras=(), epilogue=None, tm=1024, tn=512, w_t=False, after=None):
    m, k = a.shape
    n = w.shape[0] if w_t else w.shape[1]
    tm, tn = min(tm, m), min(tn, n)
    assert m % tm == 0 and n % tn == 0, (name, a.shape, w.shape)
    n_ex = len(extras)
    order = () if after is None else (after,)

    def body(a_ref, w_ref, *rest):
        rest = rest[len(order):]
        acc = (_dot_nt if w_t else _dot)(a_ref[...], w_ref[...])
        res = (acc,) if epilogue is None else epilogue(acc, *[r[...] for r in rest[:n_ex]])
        for o_ref, r in zip(rest[n_ex:], res):
            o_ref[...] = r.astype(o_ref.dtype)

    tile = pl.BlockSpec((tm, tn), lambda i, j: (i, j))
    w_spec = pl.BlockSpec((tn, k), lambda i, j: (j, 0)) if w_t else pl.BlockSpec((k, tn), lambda i, j: (0, j))
    return pl.pallas_call(
        body, grid=(m // tm, n // tn),
        in_specs=[pl.BlockSpec((tm, k), lambda i, j: (i, 0)), w_spec] + [pl.BlockSpec(memory_space=pl.ANY)] * len(order) + [tile] * n_ex,
        out_specs=[tile] * len(out_dtypes),
        out_shape=[jax.ShapeDtypeStruct((m, n), dt) for dt in out_dtypes],
        name=name, compiler_params=_params("parallel", "arbitrary"),
    )(a, w, *order, *extras)


def _matmul_res(a, w, res, name, w_t=False):
    return _matmul(a, w, name, extras=(res,), epilogue=lambda acc, r: (r + acc,), w_t=w_t)[0]


def _matmul_tn(x, y, name, tm=1024, tn=1024, tk=512, packed=None):
    t, m = x.shape
    _, n = y.shape
    tm, tn, tk = min(tm, m), min(tn, n), min(tk, t)
    assert m % tm == 0 and n % tn == 0 and t % tk == 0, (name, x.shape, y.shape)
    shape, place, into = packed or ((m, n), None, None)

    def body(x_ref, y_ref, *rest):
        o_ref = rest[-1]

        @pl.when(pl.program_id(2) == 0)
        def _():
            o_ref[...] = jnp.zeros_like(o_ref)

        o_ref[...] += _dot_tn(x_ref[...], y_ref[...])

    out_spec = (pl.BlockSpec((tm, tn), lambda i, j, k: (i, j)) if place is None
                else pl.BlockSpec((None, tm, tn), lambda i, j, k: place(i, j)))
    return pl.pallas_call(
        body, grid=(m // tm, n // tn, t // tk),
        in_specs=[pl.BlockSpec((tk, tm), lambda i, j, k: (k, i)), pl.BlockSpec((tk, tn), lambda i, j, k: (k, j))]
        + ([] if into is None else [pl.BlockSpec(memory_space=pl.ANY)]),
        out_specs=out_spec, out_shape=jax.ShapeDtypeStruct(shape, F32),
        input_output_aliases={} if into is None else {2: 0},
        name=name, compiler_params=_params("parallel", "parallel", "arbitrary"),
    )(x, y, *(() if into is None else (into,)))


def _rmsnorm(x, g, name, tm=512):
    t, d = x.shape
    tm = min(tm, t)

    def body(x_ref, g_ref, h_ref):
        xv = x_ref[...]
        r = lax.rsqrt(jnp.mean(xv * xv, axis=-1, keepdims=True) + EPS)
        h_ref[...] = (xv * r * g_ref[...]).astype(BF16)

    return pl.pallas_call(
        body, grid=(t // tm,),
        in_specs=[pl.BlockSpec((tm, d), lambda i: (i, 0)), pl.BlockSpec((1, d), lambda i: (0, 0))],
        out_specs=pl.BlockSpec((tm, d), lambda i: (i, 0)),
        out_shape=jax.ShapeDtypeStruct((t, d), BF16),
        name=name, compiler_params=_params("arbitrary"),
    )(x, g.reshape(1, d))


def _in_proj(x, g, w_all, name, tm=512):
    t, d = x.shape
    half = 3 * MIX_HALF

    def body(x_ref, g_ref, w_ref, h_ref, zd_ref, zf_ref, gate_ref):
        xv = x_ref[...]
        r = lax.rsqrt(jnp.mean(xv * xv, axis=-1, keepdims=True) + EPS)
        h = (xv * r * g_ref[...]).astype(BF16)
        h_ref[...] = h
        zd_ref[...] = _dot(h, w_ref[:, 0:half])
        zf_ref[...] = _dot(h, w_ref[:, half:2 * half]).astype(BF16)
        gate_ref[...] = _dot(h, w_ref[:, 2 * half:])

    row = lambda width: pl.BlockSpec((tm, width), lambda i: (i, 0))
    return pl.pallas_call(
        body, grid=(t // tm,),
        in_specs=[row(d), pl.BlockSpec((1, d), lambda i: (0, 0)), pl.BlockSpec(w_all.shape, lambda i: (0, 0))],
        out_specs=[row(d), row(half), row(half), row(GATE_PAD)],
        out_shape=[jax.ShapeDtypeStruct((t, d), BF16), jax.ShapeDtypeStruct((t, half), F32),
                   jax.ShapeDtypeStruct((t, half), BF16), jax.ShapeDtypeStruct((t, GATE_PAD), F32)],
        name=name, compiler_params=_params("arbitrary"),
    )(x, g.reshape(1, d), w_all)


def _rms_bwd_tile(xv, dh, g):
    d = xv.shape[-1]
    r = lax.rsqrt(jnp.mean(xv * xv, axis=-1, keepdims=True) + EPS)
    dyg = dh * g
    proj = jnp.sum(dyg * xv, axis=-1, keepdims=True)
    dx = r * dyg - xv * (r * r * r * (1.0 / d)) * proj
    return dx, dh * (xv * r)


def _rms_bwd(x, dh, g, dres, name, tm=512):
    t, d = x.shape
    tm = min(tm, t)
    has_res = dres is not None

    def body(x_ref, dh_ref, g_ref, *rest):
        if has_res:
            res_ref, dx_ref, dxb_ref, dg_ref = rest
        else:
            dx_ref, dxb_ref, dg_ref = rest
        dx, dg_rows = _rms_bwd_tile(x_ref[...], dh_ref[...], g_ref[...])
        if has_res:
            dx = res_ref[...] + dx
        dx_ref[...] = dx
        dxb_ref[...] = dx.astype(BF16)

        @pl.when(pl.program_id(0) == 0)
        def _():
            dg_ref[...] = jnp.zeros_like(dg_ref)

        dg_ref[...] += jnp.sum(dg_rows, axis=0, keepdims=True)

    row = pl.BlockSpec((tm, d), lambda i: (i, 0))
    vec = pl.BlockSpec((1, d), lambda i: (0, 0))
    return pl.pallas_call(
        body, grid=(t // tm,),
        in_specs=[row, row, vec] + ([row] if has_res else []),
        out_specs=[row, row, vec],
        out_shape=[jax.ShapeDtypeStruct((t, d), F32), jax.ShapeDtypeStruct((t, d), BF16), jax.ShapeDtypeStruct((1, d), F32)],
        name=name, compiler_params=_params("arbitrary"),
    )(x, dh, g.reshape(1, d), *((dres,) if has_res else ()))


def _row_dots(a_refs, w_refs, w_t):
    acc = None
    for a_ref, w_ref in zip(a_refs, w_refs):
        part = (_dot_nt if w_t else _dot)(a_ref[...], w_ref[...])
        acc = part if acc is None else acc + part
    return acc


def _row_specs(a_parts, w_parts, tm):
    specs = [pl.BlockSpec((tm, a.shape[1]), lambda i: (i, 0)) for a in a_parts]
    return specs + [pl.BlockSpec(w.shape, lambda i: (0, 0)) for w in w_parts]


def _matmul_res_norm(a_parts, w_parts, res, g, name, tm=512):
    t, d = res.shape
    n = len(a_parts)

    def body(*refs):
        res_ref, g_ref, x_ref, h_ref = refs[2 * n:]
        xv = res_ref[...] + _row_dots(refs[:n], refs[n:2 * n], False)
        x_ref[...] = xv
        r = lax.rsqrt(jnp.mean(xv * xv, axis=-1, keepdims=True) + EPS)
        h_ref[...] = (xv * r * g_ref[...]).astype(BF16)

    row = pl.BlockSpec((tm, d), lambda i: (i, 0))
    return pl.pallas_call(
        body, grid=(t // tm,),
        in_specs=_row_specs(a_parts, w_parts, tm) + [row, pl.BlockSpec((1, d), lambda i: (0, 0))],
        out_specs=[row, row],
        out_shape=[jax.ShapeDtypeStruct((t, d), F32), jax.ShapeDtypeStruct((t, d), BF16)],
        name=name, compiler_params=_params("arbitrary"),
    )(*a_parts, *w_parts, res, g.reshape(1, d))


def _matmul_rms_bwd(a_parts, w_parts, x, g, dres, name, tm=512, after=None):
    t, d = x.shape
    n = len(a_parts)
    order = () if after is None else (after,)

    def body(*refs):
        x_ref, g_ref, res_ref = refs[2 * n:2 * n + 3]
        dx_ref, dxb_ref, dg_ref = refs[2 * n + 3 + len(order):]
        dx, dg_rows = _rms_bwd_tile(x_ref[...], _row_dots(refs[:n], refs[n:2 * n], True), g_ref[...])
        dx = res_ref[...] + dx
        dx_ref[...] = dx
        dxb_ref[...] = dx.astype(BF16)

        @pl.when(pl.program_id(0) == 0)
        def _():
            dg_ref[...] = jnp.zeros_like(dg_ref)

        dg_ref[...] += jnp.sum(dg_rows, axis=0, keepdims=True)

    row = pl.BlockSpec((tm, d), lambda i: (i, 0))
    vec = pl.BlockSpec((1, d), lambda i: (0, 0))
    return pl.pallas_call(
        body, grid=(t // tm,),
        in_specs=_row_specs(a_parts, w_parts, tm) + [row, vec, row] + [pl.BlockSpec(memory_space=pl.ANY)] * len(order),
        out_specs=[row, row, vec],
        out_shape=[jax.ShapeDtypeStruct((t, d), F32), jax.ShapeDtypeStruct((t, d), BF16), jax.ShapeDtypeStruct((1, d), F32)],
        name=name, compiler_params=_params("arbitrary"),
    )(*a_parts, *w_parts, x, g.reshape(1, d), dres, *order)


def _loss_bwd(a, w, res, g, target, name, tm=512):
    t, d = res.shape

    def body(a_ref, w_ref, x_ref, g_ref, t_ref, loss_ref, dx_ref, dxb_ref, dg_ref):
        xv = x_ref[...] + _dot(a_ref[...], w_ref[...])
        gv = g_ref[...]
        r = lax.rsqrt(jnp.mean(xv * xv, axis=-1, keepdims=True) + EPS)
        err = xv * r * gv - t_ref[...]
        dx, dg_rows = _rms_bwd_tile(xv, err * (1.0 / d), gv)
        dx_ref[...] = dx
        dxb_ref[...] = dx.astype(BF16)

        @pl.when(pl.program_id(0) == 0)
        def _():
            dg_ref[...] = jnp.zeros_like(dg_ref)
            loss_ref[...] = jnp.zeros_like(loss_ref)

        dg_ref[...] += jnp.sum(dg_rows, axis=0, keepdims=True)
        part = jnp.sum(jnp.sum(err * err, axis=0, keepdims=True), axis=1, keepdims=True) * (0.5 / d)
        loss_ref[...] += jnp.broadcast_to(part, loss_ref.shape)

    row = pl.BlockSpec((tm, d), lambda i: (i, 0))
    vec = pl.BlockSpec((1, d), lambda i: (0, 0))
    return pl.pallas_call(
        body, grid=(t // tm,),
        in_specs=_row_specs([a], [w], tm) + [row, vec, row],
        out_specs=[pl.BlockSpec((1, LANES), lambda i: (0, 0)), row, row, vec],
        out_shape=[jax.ShapeDtypeStruct((1, LANES), F32), jax.ShapeDtypeStruct((t, d), F32),
                   jax.ShapeDtypeStruct((t, d), BF16), jax.ShapeDtypeStruct((1, d), F32)],
        name=name, compiler_params=_params("arbitrary"),
    )(a, w, res, g.reshape(1, d), target)


def _tri(upper):
    r = lax.broadcasted_iota(jnp.int32, (LANES, LANES), 0)
    c = lax.broadcasted_iota(jnp.int32, (LANES, LANES), 1)
    return jnp.where((r <= c) if upper else (r >= c), 1.0, 0.0).astype(BF16)


def _gate_fwd(gate, b_pad, n_batch, name):
    s = SEQ
    nblk = s // LANES

    def body(g_ref, b_ref, cbc_ref, crow_ref, sg_ref, ct_ref):
        gz = g_ref[...] + b_ref[...]
        logf = jnp.minimum(gz, 0.0) - jnp.log(1.0 + jnp.exp(-jnp.abs(gz)))
        logf_t = logf.T
        sg_ref[...] = (1.0 / (1.0 + jnp.exp(gz))).T[0:N_HEADS]
        upper = _tri(True)
        carry = jnp.zeros((LANES, 1), F32)
        for blk in range(nblk):
            seg = _dot_exact(logf_t[:, blk * LANES:(blk + 1) * LANES], upper) + carry
            carry = seg[:, LANES - 1:LANES]
            ct_ref[:, blk * LANES:(blk + 1) * LANES] = seg
        ct = ct_ref[...]
        crow_ref[...] = ct[0:N_HEADS]
        c_col = ct.T
        lane = lax.broadcasted_iota(jnp.int32, (1, MIX_HALF), 1)
        acc = jnp.zeros((s, MIX_HALF), F32)
        for h in range(N_HEADS):
            acc = jnp.where((lane >= HEAD_DIM * h) & (lane < HEAD_DIM * (h + 1)), c_col[:, h:h + 1], acc)
        cbc_ref[...] = acc

    return pl.pallas_call(
        body, grid=(n_batch,),
        in_specs=[pl.BlockSpec((s, GATE_PAD), lambda b: (b, 0)), pl.BlockSpec((1, GATE_PAD), lambda b: (0, 0))],
        out_specs=[pl.BlockSpec((s, MIX_HALF), lambda b: (b, 0)),
                   pl.BlockSpec((None, N_HEADS, s), lambda b: (b, 0, 0)),
                   pl.BlockSpec((None, N_HEADS, s), lambda b: (b, 0, 0))],
        out_shape=[jax.ShapeDtypeStruct((n_batch * s, MIX_HALF), F32),
                   jax.ShapeDtypeStruct((n_batch, N_HEADS, s), F32),
                   jax.ShapeDtypeStruct((n_batch, N_HEADS, s), F32)],
        scratch_shapes=[pltpu.VMEM((LANES, s), F32)],
        name=name, compiler_params=_params("arbitrary"),
    )(gate, b_pad)


def _gate_bwd(dc, sg, name):
    n_batch, _, s = dc.shape
    nblk = s // LANES

    def body(dc_ref, sg_ref, dz_ref, db_ref, dt_ref):
        lower = _tri(False)
        dcv = dc_ref[...]
        carry = jnp.zeros((N_HEADS, 1), F32)
        dt_ref[...] = jnp.zeros_like(dt_ref)
        for blk in reversed(range(nblk)):
            seg = _dot_exact(dcv[:, blk * LANES:(blk + 1) * LANES], lower) + carry
            carry = seg[:, 0:1]
            dt_ref[0:N_HEADS, blk * LANES:(blk + 1) * LANES] = seg * sg_ref[:, blk * LANES:(blk + 1) * LANES]
        dg_t = dt_ref[...]
        dz_ref[...] = dg_t.T.astype(BF16)

        @pl.when(pl.program_id(0) == 0)
        def _():
            db_ref[...] = jnp.zeros_like(db_ref)

        db_ref[...] += jnp.broadcast_to(jnp.sum(dg_t[0:N_HEADS], axis=1, keepdims=True), db_ref.shape)

    return pl.pallas_call(
        body, grid=(n_batch,),
        in_specs=[pl.BlockSpec((None, N_HEADS, s), lambda b: (b, 0, 0)), pl.BlockSpec((None, N_HEADS, s), lambda b: (b, 0, 0))],
        out_specs=[pl.BlockSpec((s, GATE_PAD), lambda b: (b, 0)), pl.BlockSpec((N_HEADS, LANES), lambda b: (0, 0))],
        out_shape=[jax.ShapeDtypeStruct((n_batch * s, GATE_PAD), BF16), jax.ShapeDtypeStruct((N_HEADS, LANES), F32)],
        scratch_shapes=[pltpu.VMEM((LANES, s), F32)],
        name=name, compiler_params=_params("arbitrary"),
    )(dc, sg)


FOX_BQ = 512
FOX_BK = 512
FOX_STRIP = 512
PAIR_WIDTH = 3 * LANES
N_PAIRS = N_HEADS // 2


def _pair_major(w):
    return w.reshape(w.shape[0], 3, N_PAIRS, LANES).transpose(0, 2, 1, 3).reshape(w.shape[0], 3 * MIX_HALF)


def _pair_major_inv(w):
    return w.reshape(w.shape[0], N_PAIRS, 3, LANES).transpose(0, 2, 1, 3).reshape(w.shape[0], 3 * MIX_HALF)


def _causal(i, j, bq, bk):
    qpos = i * bq + lax.broadcasted_iota(jnp.int32, (bq, 1), 0)
    kpos = j * bk + lax.broadcasted_iota(jnp.int32, (1, bk), 1)
    return kpos <= qpos


def _split_bf16(p):
    hi = p.astype(BF16)
    return hi, (p - hi.astype(F32)).astype(BF16)


def _fox_fwd(zf, c_bc, c_row, n_batch, name):
    s, bq, bk = SEQ, FOX_BQ, FOX_BK
    nq = s // bq
    t = n_batch * s

    n_strip = bq // FOX_STRIP

    def body(q_ref, k_ref, v_ref, cq_ref, cr_ref, o_ref, o32_ref, lse_ref):
        hp = pl.program_id(1)
        strips = [slice(r * FOX_STRIP, (r + 1) * FOX_STRIP) for r in range(n_strip)]
        chains = [(e, r) for e in range(2) for r in range(n_strip)]
        qh, cq = {}, {}
        for e, r in chains:
            q = q_ref[strips[r], :] * ATT_SCALE
            qh[e, r] = jnp.where(_head_mask(e), q, jnp.zeros_like(q))
            cq[e, r] = cq_ref[strips[r], HEAD_DIM * e:HEAD_DIM * e + 1]

        def step(i, j, carry, masked):
            rows = pl.ds(j * bk, bk)
            kj, vj = k_ref[rows, :], v_ref[rows, :]
            ck = [cr_ref[pl.ds(2 * hp + e, 1), rows] for e in range(2)]
            out = []
            scores = [_dot_nt(qh[e, r], kj) for e, r in chains]
            for n, (e, r) in enumerate(chains):
                m, l, acc = carry[3 * n:3 * n + 3]
                sc = scores[n] + (cq[e, r] - ck[e])
                if masked:
                    qpos = i * bq + r * FOX_STRIP + lax.broadcasted_iota(jnp.int32, (FOX_STRIP, 1), 0)
                    kpos = j * bk + lax.broadcasted_iota(jnp.int32, (1, bk), 1)
                    sc = jnp.where(kpos <= qpos, sc, NEG)
                m_new = jnp.maximum(m, jnp.max(sc, axis=1, keepdims=True))
                alpha = jnp.exp(m - m_new)
                p = jnp.exp(sc - m_new)
                p_hi, p_lo = _split_bf16(p)
                out += [m_new, alpha * l + jnp.sum(p, axis=1, keepdims=True), alpha * acc + (_dot(p_hi, vj) + _dot(p_lo, vj))]
            return tuple(out)

        def run(i):
            carry = (jnp.full((FOX_STRIP, 1), NEG, F32), jnp.zeros((FOX_STRIP, 1), F32), jnp.zeros((FOX_STRIP, LANES), F32)) * len(chains)
            n_clear = (i * bq) // bk
            for j in range((i * bq + bq + bk - 1) // bk):
                carry = step(i, j, carry, masked=j >= n_clear)
            for r in range(n_strip):
                outs = [carry[3 * (e * n_strip + r) + 2] / carry[3 * (e * n_strip + r) + 1] for e in range(2)]
                lses = [carry[3 * (e * n_strip + r)] + jnp.log(carry[3 * (e * n_strip + r) + 1]) for e in range(2)]
                o = jnp.where(_head_mask(0), outs[0], outs[1])
                o_ref[strips[r], :] = o.astype(BF16)
                o32_ref[strips[r], :] = o
                lse_ref[strips[r], :] = jnp.where(_head_mask(0), lses[0], lses[1])

        for k in range(nq):
            pl.when(pl.program_id(2) == k)(functools.partial(run, k))

    def col(c0):
        return lambda b, hp, i: (b, 3 * hp + c0)

    blk = pl.BlockSpec((bq, LANES), lambda b, hp, i: (b * nq + i, hp))
    return pl.pallas_call(
        body, grid=(n_batch, N_PAIRS, nq),
        in_specs=[pl.BlockSpec((bq, LANES), lambda b, hp, i: (b * nq + i, 3 * hp)),
                  pl.BlockSpec((s, LANES), col(1)), pl.BlockSpec((s, LANES), col(2)), blk,
                  pl.BlockSpec((None, N_HEADS, s), lambda b, hp, i: (b, 0, 0))],
        out_specs=[blk, blk, blk],
        out_shape=[jax.ShapeDtypeStruct((t, MIX_HALF), BF16), jax.ShapeDtypeStruct((t, MIX_HALF), F32),
                   jax.ShapeDtypeStruct((t, MIX_HALF), F32)],
        name=name, compiler_params=_params("parallel", "parallel", "arbitrary"),
    )(zf, zf, zf, c_bc, c_row)


def _fox_bwd(zf, o32, dy, lse, c_bc, c_row, dz, n_batch, name):
    s, bq, bk = SEQ, FOX_BQ, FOX_BK
    nq, nk = s // bq, s // bk

    def body(q_ref, k_ref, v_ref, o_ref, do_ref, lse_ref, cq_ref, cr_ref, dz_in, dz_ref, dc_ref, dq_acc):
        del dz_in
        hp = pl.program_id(1)

        @pl.when(pl.program_id(2) == 0)
        def _():
            dq_acc[...] = jnp.zeros_like(dq_acc)

        kj, vj = k_ref[...], v_ref[...]
        km = [jnp.where(_head_mask(e), kj, jnp.zeros_like(kj)) for e in range(2)]

        def step(i, j, ck, carry, masked):
            rows = pl.ds(i * bq, bq)
            qi, doi = q_ref[rows, :] * ATT_SCALE, do_ref[rows, :]
            prod = doi.astype(F32) * o_ref[rows, :]
            out = []
            dq = jnp.zeros((bq, LANES), F32)
            for e in range(2):
                dk_a, dv_a, dc_a = carry[3 * e:3 * e + 3]
                mask = _head_mask(e)
                lane0 = HEAD_DIM * e
                dom = jnp.where(mask, doi, jnp.zeros_like(doi))
                delta = jnp.sum(jnp.where(mask, prod, 0.0), axis=1, keepdims=True)
                sc = _dot_nt(qi, km[e]) + (cq_ref[rows, lane0:lane0 + 1] - ck[e])
                if masked:
                    sc = jnp.where(_causal(i, j, bq, bk), sc, NEG)
                p = jnp.exp(sc - lse_ref[rows, lane0:lane0 + 1])
                ds = p * (_dot_nt(dom, vj) - delta)
                dsb = ds.astype(BF16)
                dq = dq + _dot(dsb, km[e])
                out += [dk_a + _dot_tn(dsb, qi), dv_a + _dot_tn(p.astype(BF16), dom), dc_a - jnp.sum(ds, axis=0, keepdims=True)]
            dq_acc[rows, :] += dq * ATT_SCALE
            return tuple(out)

        def run(j):
            cols = pl.ds(j * bk, bk)
            ck = [cr_ref[pl.ds(2 * hp + e, 1), cols] for e in range(2)]
            carry = (jnp.zeros((bk, LANES), F32), jnp.zeros((bk, LANES), F32), jnp.zeros((1, bk), F32)) * 2
            n_diag = (j * bk + bk + bq - 1) // bq
            for i in range((j * bk) // bq, nq):
                carry = step(i, j, ck, carry, masked=i < n_diag)
            for e in range(2):
                dc_ref[e:e + 1, :] = carry[3 * e + 2]
            dz_ref[cols, LANES:2 * LANES] = jnp.where(_head_mask(0), carry[0], carry[3]).astype(BF16)
            dz_ref[cols, 2 * LANES:3 * LANES] = (carry[1] + carry[4]).astype(BF16)
            if j == nk - 1:
                dz_ref[:, 0:LANES] = dq_acc[...].astype(BF16)

        for k in range(nk):
            pl.when(pl.program_id(2) == k)(functools.partial(run, k))

    def seq(idx):
        return pl.BlockSpec((s, LANES), lambda b, hp, j: (b, idx(hp)))

    def kblk(c0):
        return pl.BlockSpec((bk, LANES), lambda b, hp, j: (b * nk + j, 3 * hp + c0))

    return pl.pallas_call(
        body, grid=(n_batch, N_PAIRS, nk),
        in_specs=[seq(lambda hp: 3 * hp), kblk(1), kblk(2), seq(lambda hp: hp), seq(lambda hp: N_PAIRS + hp),
                  seq(lambda hp: hp), seq(lambda hp: hp),
                  pl.BlockSpec((None, N_HEADS, s), lambda b, hp, j: (b, 0, 0)), pl.BlockSpec(memory_space=pl.ANY)],
        out_specs=[pl.BlockSpec((s, PAIR_WIDTH), lambda b, hp, j: (b, N_PAIRS + hp)),
                   pl.BlockSpec((None, None, 2, bk), lambda b, hp, j: (b, hp, 0, j))],
        out_shape=[jax.ShapeDtypeStruct(dz.shape, dz.dtype), jax.ShapeDtypeStruct((n_batch, N_PAIRS, 2, s), F32)],
        scratch_shapes=[pltpu.VMEM((s, LANES), F32)],
        input_output_aliases={8: 0},
        name=name, compiler_params=_params("parallel", "parallel", "arbitrary"),
    )(zf, zf, zf, o32, dy, lse, c_bc, c_row, dz)


def _dil_bias(slope, dil):
    qi = lax.broadcasted_iota(jnp.int32, (BLOCK, 2 * BLOCK), 0)
    kj = lax.broadcasted_iota(jnp.int32, (BLOCK, 2 * BLOCK), 1)
    delta = qi + BLOCK - kj
    return jnp.where((delta >= 0) & (delta <= BLOCK), (-slope * dil) * delta.astype(F32), NEG)


def _alibi_slope(hp, e):
    slope = jnp.float32(0.0)
    for k in range(N_PAIRS):
        slope = jnp.where(hp == k, jnp.float32(2.0 ** -(2 * k + e + 1)), slope)
    return slope


def _first_block_bias(bias):
    return jnp.where(lax.broadcasted_iota(jnp.int32, bias.shape, 1) < BLOCK, NEG, bias)


def _fill_bias(bias_scr, hp):
    for di, dil in enumerate(DILATIONS):
        for e in range(2):
            bias_scr[2 * di + e] = _dil_bias(_alibi_slope(hp, e), dil)


def _pair_specs(rows):
    return [pl.BlockSpec((rows, LANES), lambda b, hp, c0=c0: (b, 3 * hp + c0)) for c0 in range(3)]


def _strided(start, size, dil):
    return pl.ds(start, size) if dil == 1 else pl.ds(start, size, stride=dil)


def _for_each_block(dil, unit, group=3):
    span = BLOCK * dil
    nb = SEQ // span
    if dil == 1:
        assert (nb - 1) % group == 0
        unit(0, True)

        def later(g, c):
            for u in range(group):
                unit((1 + g * group + u) * span, False)
            return c

        if (nb - 1) // group == 1:
            later(0, 0)
        else:
            lax.fori_loop(0, (nb - 1) // group, later, 0)
        return
    group = 4
    per = dil // group

    def firsts(g, c):
        for u in range(group):
            unit(g * group + u, True)
        return c

    lax.fori_loop(0, per, firsts, 0)
    if nb > 1:
        def later(i, c):
            for u in range(group):
                unit((1 + i // per) * span + (i % per) * group + u, False)
            return c

        lax.fori_loop(0, (nb - 1) * per, later, 0)


QUARTER = SEQ // 4


def _to_quarters(src, dst):
    for r in range(4):
        dst[r * QUARTER:(r + 1) * QUARTER, :] = src[pl.ds(r, QUARTER, stride=4), :]


def _from_quarters(src, dst):
    for r in range(4):
        dst[pl.ds(r, QUARTER, stride=4), :] = src[r * QUARTER:(r + 1) * QUARTER, :]


def _for_each_quarter_block(dil, unit):
    stride = dil // 4
    nb = QUARTER // (BLOCK * stride)

    def firsts(r, c):
        for g in range(stride):
            unit(r * QUARTER + g, True, stride)
        return c

    if stride == 1:
        for r in range(4):
            firsts(r, 0)
    else:
        for r in range(4):
            firsts(r, 0)
    if nb > 1:
        assert stride == 1
        for n in range(1, nb):
            for r in range(4):
                unit(r * QUARTER + n * BLOCK, False, stride)


def _mix_weights(l1, l2, l3):
    m = jnp.maximum(jnp.maximum(l1, l2), l3)
    e1, e2, e3 = jnp.exp(l1 - m), jnp.exp(l2 - m), jnp.exp(l3 - m)
    inv = 1.0 / (e1 + e2 + e3)
    return e1 * inv, e2 * inv, e3 * inv


def _dil_fwd(zd, n_batch, name):
    s = SEQ
    t = n_batch * s

    def body(q_ref, k_ref, v_ref, y_ref, l1_ref, l2_ref, l3_ref, o_scr, qkv4, o4, l4, bias_scr):
        _fill_bias(bias_scr, pl.program_id(1))
        for a, ref in enumerate((q_ref, k_ref, v_ref)):
            _to_quarters(ref, qkv4.at[a])

        def unit(srcs, start, first, stride, di, o_dst, l_dst):
            qrows = _strided(start, BLOCK, stride)
            krows = qrows if first else _strided(start - BLOCK * stride, 2 * BLOCK, stride)
            q = (srcs[0][qrows, :] * ATT_SCALE).astype(BF16)
            kc = srcs[1][krows, :].astype(BF16)
            vc = srcs[2][krows, :].astype(BF16)
            if first:
                kc, vc = jnp.concatenate([kc, kc]), jnp.concatenate([vc, vc])
            outs, lses = [], []
            for e in range(2):
                bias = _first_block_bias(bias_scr[2 * di + e]) if first else bias_scr[2 * di + e]
                sc = _dot_nt(jnp.where(_head_mask(e), q, jnp.zeros_like(q)), kc) + bias
                m = jnp.max(sc, axis=1, keepdims=True)
                pe = jnp.exp(sc - m)
                l = jnp.sum(pe, axis=1, keepdims=True)
                outs.append(_dot((pe * (1.0 / l)).astype(BF16), vc))
                lses.append(m + jnp.log(l))
            o_dst[qrows, :] = jnp.where(_head_mask(0), outs[0], outs[1])
            l_dst[qrows, :] = jnp.where(_head_mask(0), lses[0], lses[1])

        token_order = (q_ref, k_ref, v_ref)
        quarters = tuple(qkv4.at[a] for a in range(3))
        _for_each_block(1, lambda start, first: unit(token_order, start, first, 1, 0, o_scr.at[0], l1_ref), group=15)
        for di in (1, 2):
            _for_each_quarter_block(DILATIONS[di], lambda start, first, stride, di=di: unit(
                quarters, start, first, stride, di, o4.at[di - 1], l4.at[di - 1]))
        for di, l_ref in ((1, l2_ref), (2, l3_ref)):
            _from_quarters(o4.at[di - 1], o_scr.at[di])
            _from_quarters(l4.at[di - 1], l_ref)
        w = _mix_weights(l1_ref[...], l2_ref[...], l3_ref[...])
        y_ref[...] = (w[0] * o_scr[0] + w[1] * o_scr[1] + w[2] * o_scr[2]).astype(BF16)

    blk = pl.BlockSpec((s, LANES), lambda b, hp: (b, hp))
    res = pl.pallas_call(
        body, grid=(n_batch, N_PAIRS),
        in_specs=_pair_specs(s),
        out_specs=[blk] * 4,
        out_shape=[jax.ShapeDtypeStruct((t, MIX_HALF), BF16)] + [jax.ShapeDtypeStruct((t, MIX_HALF), F32)] * 3,
        scratch_shapes=[pltpu.VMEM((3, s, LANES), F32), pltpu.VMEM((3, s, LANES), F32), pltpu.VMEM((2, s, LANES), F32),
                        pltpu.VMEM((2, s, LANES), F32), pltpu.VMEM((6, BLOCK, 2 * BLOCK), F32)],
        name=name, compiler_params=_params("parallel", "arbitrary"),
    )(zd, zd, zd)
    return res[0], res[1:]


def _dil_bwd(zd, dy, ya, lses, n_batch, name):
    s = SEQ
    t = n_batch * s

    def body(q_ref, k_ref, v_ref, dy_ref, ya_ref, l1_ref, l2_ref, l3_ref, dz_ref, w_scr, dy_scr, dot_scr, acc, st4, acc4, bias_scr):
        _fill_bias(bias_scr, pl.program_id(1))
        for di, w in enumerate(_mix_weights(l1_ref[...], l2_ref[...], l3_ref[...])):
            w_scr[di] = w
        dya = dy_ref[...].astype(F32)
        prod = dya * ya_ref[...].astype(F32)
        per_head = [jnp.sum(jnp.where(_head_mask(e), prod, 0.0), axis=1, keepdims=True) for e in range(2)]
        dy_scr[...] = dya
        dot_scr[...] = jnp.where(_head_mask(0), per_head[0], per_head[1])
        acc[...] = jnp.zeros_like(acc)
        acc4[...] = jnp.zeros_like(acc4)
        staged = (q_ref, k_ref, v_ref, w_scr.at[1], w_scr.at[2], l2_ref, l3_ref, dy_scr, dot_scr)
        for a, ref in enumerate(staged):
            _to_quarters(ref, st4.at[a])

        def unit(srcs, dst, start, first, stride, di):
            qrows = _strided(start, BLOCK, stride)
            krows = qrows if first else _strided(start - BLOCK * stride, 2 * BLOCK, stride)
            q = (srcs[0][qrows, :] * ATT_SCALE).astype(BF16)
            kc = srcs[1][krows, :].astype(BF16)
            vc = srcs[2][krows, :].astype(BF16)
            wq = srcs[3][qrows, :]
            lse = srcs[4][qrows, :]
            do = (wq * srcs[5][qrows, :]).astype(BF16)
            sub = wq * srcs[6][qrows, :]
            dq = jnp.zeros((BLOCK, LANES), F32)
            dk = jnp.zeros((krows.size, LANES), F32)
            dv = jnp.zeros((krows.size, LANES), F32)
            for e in range(2):
                mask = _head_mask(e)
                lane0 = HEAD_DIM * e
                qh = jnp.where(mask, q, jnp.zeros_like(q))
                doh = jnp.where(mask, do, jnp.zeros_like(do))
                bias = bias_scr[2 * di + e]
                sc = _dot_nt(qh, kc) + (bias[:, BLOCK:] if first else bias)
                p = jnp.exp(sc - lse[:, lane0:lane0 + 1])
                dsb = (p * (_dot_nt(doh, vc) - sub[:, lane0:lane0 + 1])).astype(BF16)
                dq = dq + _dot(dsb, jnp.where(mask, kc, jnp.zeros_like(kc)))
                dk = dk + _dot_tn(dsb, qh)
                dv = dv + _dot_tn(p.astype(BF16), doh)
            dst.at[0][qrows, :] += dq * ATT_SCALE
            dst.at[1][krows, :] += dk
            dst.at[2][krows, :] += dv

        token_order = (q_ref, k_ref, v_ref, w_scr.at[0], l1_ref, dy_scr, dot_scr)
        for n in range(SEQ // BLOCK):
            unit(token_order, acc, n * BLOCK, n == 0, 1, 0)
        for di in (1, 2):
            quarters = (st4.at[0], st4.at[1], st4.at[2], st4.at[2 + di], st4.at[4 + di], st4.at[7], st4.at[8])
            stride = DILATIONS[di] // 4
            for r in range(4):
                for g in range(stride):
                    for n in range(QUARTER // (BLOCK * stride)):
                        unit(quarters, acc4, r * QUARTER + n * BLOCK * stride + g, n == 0, stride, di)
        for k in range(3):
            for r in range(4):
                acc.at[k][pl.ds(r, QUARTER, stride=4), :] += acc4[k, r * QUARTER:(r + 1) * QUARTER, :]
            dz_ref[:, k * LANES:(k + 1) * LANES] = acc[k].astype(BF16)

    blk = pl.BlockSpec((s, LANES), lambda b, hp: (b, hp))
    pair = pl.BlockSpec((s, PAIR_WIDTH), lambda b, hp: (b, hp))
    return pl.pallas_call(
        body, grid=(n_batch, N_PAIRS),
        in_specs=_pair_specs(s) + [blk] * 5,
        out_specs=pair,
        out_shape=jax.ShapeDtypeStruct((t, 2 * 3 * MIX_HALF), BF16),
        scratch_shapes=[pltpu.VMEM((3, s, LANES), F32), pltpu.VMEM((s, LANES), F32), pltpu.VMEM((s, LANES), F32),
                        pltpu.VMEM((3, s, LANES), F32), pltpu.VMEM((9, s, LANES), F32), pltpu.VMEM((3, s, LANES), F32),
                        pltpu.VMEM((6, BLOCK, 2 * BLOCK), F32)],
        name=name, compiler_params=_params("parallel", "arbitrary"),
    )(zd, zd, zd, dy, ya, *lses)


X_BQ = 2048


def _xattn_probs(q, k):
    sc = _dot_nt(q, k) * X_SCALE
    pe = jnp.exp(sc - jnp.max(sc, axis=1, keepdims=True))
    return pe / jnp.sum(pe, axis=1, keepdims=True)


def _xattn_fwd(qx, kx, vx, n_batch, name):
    nq = SEQ // X_BQ

    def body(q_ref, k_ref, v_ref, o_ref):
        p = _xattn_probs(q_ref[...], k_ref[...])
        o_ref[...] = _dot(p.astype(BF16), v_ref[...]).astype(BF16)

    qblk = pl.BlockSpec((X_BQ, X_HEAD_DIM), lambda b, h, i: (b * nq + i, h))
    kblk = pl.BlockSpec((N_MEM, X_HEAD_DIM), lambda b, h, i: (b, h))
    return pl.pallas_call(
        body, grid=(n_batch, X_HEADS, nq), in_specs=[qblk, kblk, kblk], out_specs=qblk,
        out_shape=jax.ShapeDtypeStruct(qx.shape, BF16),
        name=name, compiler_params=_params("parallel", "parallel", "arbitrary"),
    )(qx, kx, vx)


def _xattn_bwd(qx, kx, vx, dox, n_batch, name):
    nq = SEQ // X_BQ

    def body(q_ref, k_ref, v_ref, do_ref, dq_ref, dk_ref, dv_ref, dk_acc, dv_acc):
        i = pl.program_id(2)

        @pl.when(i == 0)
        def _():
            dk_acc[...] = jnp.zeros_like(dk_acc)
            dv_acc[...] = jnp.zeros_like(dv_acc)

        q, k, do = q_ref[...], k_ref[...], do_ref[...]
        p = _xattn_probs(q, k)
        dp = _dot_nt(do, v_ref[...])
        dsb = (p * (dp - jnp.sum(p * dp, axis=1, keepdims=True))).astype(BF16)
        dq_ref[...] = (_dot(dsb, k) * X_SCALE).astype(BF16)
        dk_acc[...] += _dot_tn(dsb, q) * X_SCALE
        dv_acc[...] += _dot_tn(p.astype(BF16), do)

        @pl.when(i == nq - 1)
        def _():
            dk_ref[...] = dk_acc[...].astype(BF16)
            dv_ref[...] = dv_acc[...].astype(BF16)

    qblk = pl.BlockSpec((X_BQ, X_HEAD_DIM), lambda b, h, i: (b * nq + i, h))
    kblk = pl.BlockSpec((N_MEM, X_HEAD_DIM), lambda b, h, i: (b, h))
    return pl.pallas_call(
        body, grid=(n_batch, X_HEADS, nq), in_specs=[qblk, kblk, kblk, qblk], out_specs=[qblk, kblk, kblk],
        out_shape=[jax.ShapeDtypeStruct(qx.shape, BF16), jax.ShapeDtypeStruct(kx.shape, BF16), jax.ShapeDtypeStruct(kx.shape, BF16)],
        scratch_shapes=[pltpu.VMEM((N_MEM, X_HEAD_DIM), F32)] * 2,
        name=name, compiler_params=_params("parallel", "parallel", "arbitrary"),
    )(qx, kx, vx, dox)


def _adamw(w, g, m, v, name, rows):
    r, c = w.shape
    assert r % rows == 0, (name, w.shape, rows)

    def body(w_ref, g_ref, m_ref, v_ref, d_ref, nm_ref, nv_ref):
        gv = g_ref[...]
        m1 = ADAM_B1 * m_ref[...] + (1.0 - ADAM_B1) * gv
        v1 = ADAM_B2 * v_ref[...] + (1.0 - ADAM_B2) * jnp.square(gv)
        m_hat = m1 / (1.0 - ADAM_B1 ** ADAM_STEP)
        v_hat = v1 / (1.0 - ADAM_B2 ** ADAM_STEP)
        d_ref[...] = -ADAM_LR * (m_hat / (jnp.sqrt(v_hat) + ADAM_EPS) + ADAM_WD * w_ref[...])
        nm_ref[...] = m1
        nv_ref[...] = v1

    blk = pl.BlockSpec((rows, c), lambda i: (i, 0))
    return pl.pallas_call(
        body, grid=(r // rows,), in_specs=[blk] * 4, out_specs=[blk] * 3,
        out_shape=[jax.ShapeDtypeStruct((r, c), F32)] * 3,
        name=name, compiler_params=_params("arbitrary"),
    )(w, g, m, v)


def _relu2(acc):
    a = jnp.maximum(acc, 0.0)
    return acc, a * a


def _relu2_bwd(acc, u):
    return (2.0 * jnp.maximum(u.astype(F32), 0.0) * acc,)


def _local_step(x, mem, target, vecs, w_in, late_weights, on_grads=None):
    n_batch = x.shape[0]
    t = n_batch * SEQ
    x0 = x.reshape(t, D_MODEL)
    mem2 = mem.reshape(n_batch * N_MEM, D_MODEL)
    tgt = target.reshape(t, D_MODEL)

    half = 3 * MIX_HALF
    w_qkv = jnp.concatenate([_pair_major(w_in[:, :half]), _pair_major(w_in[:, half:QKV_WIDTH])], axis=1)
    w_gate = jnp.pad(w_in[:, QKV_WIDTH:], ((0, 0), (0, GATE_PAD - N_HEADS)))
    b_pad = jnp.pad(vecs["b_forget"], (0, GATE_PAD - N_HEADS)).reshape(1, GATE_PAD)

    h1, zd, zf, gate = _in_proj(x0, vecs["g_mix"], jnp.concatenate([w_qkv, w_gate], axis=1), "in_proj")
    mn = _rmsnorm(mem2, vecs["g_mem"], "norm_mem")
    c_bc, c_row, sg = _gate_fwd(gate, b_pad, n_batch, "gate_fwd")
    ya, lses = _dil_fwd(zd, n_batch, "dil_fwd")
    yf, of32, lse_f = _fox_fwd(zf, c_bc, c_row, n_batch, "fox_fwd")
    wts = late_weights(yf)
    w_out = wts["w_out"]
    x1, h2 = _matmul_res_norm([ya, yf], [w_out[:MIX_HALF], w_out[MIX_HALF:]], x0, vecs["g_xattn"], "out")
    qx = _matmul(h2, wts["w_xq"], "xq", out_dtypes=(BF16,))[0]
    kx = _matmul(mn, wts["w_xk"], "xk", out_dtypes=(BF16,))[0]
    vx = _matmul(mn, wts["w_xv"], "xv", out_dtypes=(BF16,))[0]
    ox = _xattn_fwd(qx, kx, vx, n_batch, "xattn_fwd")
    x2, h3 = _matmul_res_norm([ox], [wts["w_xo"]], x1, vecs["g_mlp"], "xo")
    u, a2 = _matmul(h3, wts["w_up"], "mlp_up", out_dtypes=(BF16, BF16), epilogue=_relu2, tn=1024)
    loss, dx3, dx3b, dg_final = _loss_bwd(a2, wts["w_down"], x2, vecs["g_final"], tgt, "mlp_down_loss")

    du = _matmul(dx3b, wts["w_down"], "mlp_down_bwd", out_dtypes=(BF16,), extras=(u,), epilogue=_relu2_bwd, tn=1024, w_t=True)[0]
    shards = (N_CHIPS, 2 * D_MODEL, D_MODEL)
    g_mlp = _matmul_tn(h3, du, "gw_up", packed=(shards, lambda i, j: (j, 0, 0), None))
    g_mlp = _matmul_tn(a2, dx3b, "gw_down", packed=(shards, lambda i, j: (i, 1, 0), g_mlp))
    gw_up = g_mlp[:, :D_MODEL].transpose(1, 0, 2).reshape(D_MODEL, D_FF)
    gw_down = g_mlp[:, D_MODEL:].reshape(D_FF, D_MODEL)
    token = on_grads("mlp", g_mlp) if on_grads else None
    dx2, dx2b, dg_mlp = _matmul_rms_bwd([du], [wts["w_up"]], x2, vecs["g_mlp"], dx3, "mlp_up_bwd", after=token)

    gw_xo = _matmul_tn(ox, dx2b, "gw_xo")
    dox = _matmul(dx2b, wts["w_xo"], "xo_bwd", out_dtypes=(BF16,), w_t=True)[0]
    dqx, dkx, dvx = _xattn_bwd(qx, kx, vx, dox, n_batch, "xattn_bwd")
    gw_xq = _matmul_tn(h2, dqx, "gw_xq")
    gw_xk = _matmul_tn(mn, dkx, "gw_xk")
    gw_xv = _matmul_tn(mn, dvx, "gw_xv")
    dmn = _matmul(dkx, wts["w_xk"], "xk_bwd", w_t=True)[0]
    dmn = _matmul_res(dvx, wts["w_xv"], dmn, "xv_bwd", w_t=True)
    _, _, dg_mem = _rms_bwd(mem2, dmn, vecs["g_mem"], None, "norm_mem_bwd")
    dx1, dx1b, dg_xattn = _matmul_rms_bwd([dqx], [wts["w_xq"]], x1, vecs["g_xattn"], dx2, "xq_bwd")

    gw_out = jnp.concatenate([_matmul_tn(ya, dx1b, "gw_out_a"), _matmul_tn(yf, dx1b, "gw_out_f")], axis=0)
    token = on_grads("mid", dict(w_out=gw_out, w_xq=gw_xq, w_xk=gw_xk, w_xv=gw_xv, w_xo=gw_xo)) if on_grads else None
    dy = _matmul(dx1b, w_out, "out_bwd", out_dtypes=(BF16,), w_t=True, after=token)[0]
    dz = _dil_bwd(zd, dy, ya, lses, n_batch, "dil_bwd")
    dz, dc = _fox_bwd(zf, of32, dy, lse_f, c_bc, c_row, dz, n_batch, "fox_bwd")
    dzg, db = _gate_bwd(dc.reshape(n_batch, N_HEADS, SEQ), sg, "gate_bwd")
    gw_pm = _matmul_tn(h1, dz, "gw_in_qkv")
    gw_in = jnp.concatenate([_pair_major_inv(gw_pm[:, :half]), _pair_major_inv(gw_pm[:, half:]),
                             _matmul_tn(h1, dzg, "gw_in_gate")[:, :N_HEADS]], axis=1)
    dx0, _, dg_mix = _matmul_rms_bwd([dz, dzg], [w_qkv, w_gate], x0, vecs["g_mix"], dx1, "in_bwd")

    gw = dict(w_in=gw_in, w_out=gw_out, w_xq=gw_xq, w_xk=gw_xk, w_xv=gw_xv, w_xo=gw_xo, w_up=gw_up, w_down=gw_down)
    gv = dict(g_mix=dg_mix, g_xattn=dg_xattn, g_mem=dg_mem, g_mlp=dg_mlp, g_final=dg_final, b_forget=db)
    return loss, dx0.reshape(x.shape), gw, gv


MESH = pl.DeviceIdType.MESH
ANY = pl.BlockSpec(memory_space=pl.ANY)


def _place():
    x, y, c = lax.axis_index("x"), lax.axis_index("y"), lax.axis_index("c")
    other_chips = [(1 - x, y), (x, 1 - y), (1 - x, 1 - y)]
    return x, y, c, other_chips


def _my_chip():
    return 2 * lax.axis_index("x") + lax.axis_index("y")


def _halves(rows, c, align):
    half = rows // 2
    assert rows % (2 * align) == 0, rows
    return pl.ds(pl.multiple_of(c * half, align), half), pl.ds(pl.multiple_of((1 - c) * half, align), half)


def _place_own(wall, pack):
    return lax.dynamic_update_slice(wall, pack[None], (_my_chip(), 0, 0))


def _gather(pack, name, after):
    def body(p_ref, after_ref, out_ref, send_sems, recv_sems, pass_send, pass_recv):
        del after_ref
        x, y, c, chips = _place()
        me = 2 * x + y
        mine, theirs = _halves(pack.shape[0], c, 16)

        def from_chip(k, chip, rows):
            src = out_ref.at[2 * chip[0] + chip[1], rows]
            return pltpu.make_async_remote_copy(src_ref=src, dst_ref=src, send_sem=send_sems.at[k], recv_sem=recv_sems.at[k],
                                                device_id=(chip[0], chip[1], c), device_id_type=MESH)

        def passed(k, chip, rows):
            src = out_ref.at[2 * chip[0] + chip[1], rows]
            return pltpu.make_async_remote_copy(src_ref=src, dst_ref=src, send_sem=pass_send.at[k], recv_sem=pass_recv.at[k],
                                                device_id=(x, y, 1 - c), device_id_type=MESH)

        sends = []
        for k, chip in enumerate(chips):
            cp = pltpu.make_async_remote_copy(src_ref=p_ref.at[mine], dst_ref=out_ref.at[me, mine], send_sem=send_sems.at[k],
                                              recv_sem=recv_sems.at[k], device_id=(chip[0], chip[1], c), device_id_type=MESH)
            cp.start()
            sends.append(cp)
        for k, chip in enumerate(chips):
            from_chip(k, chip, mine).wait_recv()
            cp = passed(k, chip, mine)
            cp.start()
            sends.append(cp)
        for k, chip in enumerate(chips):
            passed(k, chip, theirs).wait_recv()
        for cp in sends:
            cp.wait_send()

    wall = pl.pallas_call(
        body, in_specs=[ANY, ANY], out_specs=ANY,
        out_shape=jax.ShapeDtypeStruct((N_CHIPS,) + pack.shape, pack.dtype),
        scratch_shapes=[pltpu.SemaphoreType.DMA((3,))] * 4,
        name=name,
    )(pack, after)
    return _place_own(wall, pack)


HBM = pl.BlockSpec(memory_space=pltpu.HBM)
SEM = pl.BlockSpec(memory_space=pltpu.SEMAPHORE)
SPLIT_COPY = pltpu.CompilerParams(has_side_effects=pltpu.SideEffectType.DATAFLOW_SIDE_EFFECTING)


def _in_hbm(a):
    return pltpu.with_memory_space_constraint(a, pltpu.HBM)


def _start_call(start, src, land_shape, after, name):
    land = lax.empty(land_shape, src.dtype)

    def body(src_ref, land_ref, after_ref, send_sems, recv_sems, src_thru, land_thru, token):
        del after_ref, src_thru, land_thru
        start(src_ref, land_ref, send_sems, recv_sems)
        token[...] = jnp.zeros_like(token)

    return pl.pallas_call(
        body, name=name,
        out_shape=(pltpu.SemaphoreType.DMA((3,)), pltpu.SemaphoreType.DMA((3,)), pltpu.HBM(src.shape, src.dtype),
                   pltpu.HBM(land_shape, src.dtype), jax.ShapeDtypeStruct((8, LANES), F32)),
        in_specs=(HBM, HBM, ANY), out_specs=(SEM, SEM, HBM, HBM, pl.BlockSpec(memory_space=pltpu.VMEM)),
        input_output_aliases={0: 2, 1: 3}, compiler_params=SPLIT_COPY,
    )(_in_hbm(src), _in_hbm(land), after)


def _wait_call(body, started, after, name):
    send_sems, recv_sems, src, land, _ = started
    return pl.pallas_call(
        body, name=name,
        out_shape=(pltpu.HBM(src.shape, src.dtype), pltpu.HBM(land.shape, land.dtype)),
        in_specs=(HBM, HBM, SEM, SEM, ANY), out_specs=(HBM, HBM),
        input_output_aliases={0: 0, 1: 1}, compiler_params=SPLIT_COPY,
    )(src, land, send_sems, recv_sems, after)


def _gather_copies(p_ref, wall_ref, send_sems, recv_sems):
    x, y, c, chips = _place()
    me = 2 * x + y
    mine, _ = _halves(p_ref.shape[0], c, 16)
    out, back = [], []
    for k, chip in enumerate(chips):
        peer = dict(send_sem=send_sems.at[k], recv_sem=recv_sems.at[k], device_id=(chip[0], chip[1], c), device_id_type=MESH)
        out.append(pltpu.make_async_remote_copy(src_ref=p_ref.at[mine], dst_ref=wall_ref.at[me, mine], **peer))
        slab = wall_ref.at[2 * chip[0] + chip[1], mine]
        back.append(pltpu.make_async_remote_copy(src_ref=slab, dst_ref=slab, **peer))
    return out, back


def _gather_start(pack, after, name):
    def start(p_ref, wall_ref, send_sems, recv_sems):
        for cp in _gather_copies(p_ref, wall_ref, send_sems, recv_sems)[0]:
            cp.start()

    return _start_call(start, pack, (N_CHIPS,) + pack.shape, after, name)


def _gather_wait(started, after, name):
    def body(p_ref, wall_ref, send_sems, recv_sems, after_ref, p_dead, wall_out):
        del after_ref, p_dead, wall_out
        out, back = _gather_copies(p_ref, wall_ref, send_sems, recv_sems)
        for cp_out, cp_back in zip(out, back):
            cp_out.wait_send()
            cp_back.wait_recv()

    return _wait_call(body, started, after, name)


def _pass_on(wall, name):
    def body(w_in_ref, out_ref, send_sems, recv_sems):
        del w_in_ref
        x, y, c, chips = _place()
        mine, theirs = _halves(wall.shape[1], c, 16)
        sends = []
        for k, chip in enumerate(chips):
            slab = out_ref.at[2 * chip[0] + chip[1]]
            peer = dict(send_sem=send_sems.at[k], recv_sem=recv_sems.at[k], device_id=(x, y, 1 - c), device_id_type=MESH)
            cp = pltpu.make_async_remote_copy(src_ref=slab.at[mine], dst_ref=slab.at[mine], **peer)
            cp.start()
            sends.append((cp, pltpu.make_async_remote_copy(src_ref=slab.at[theirs], dst_ref=slab.at[theirs], **peer)))
        for cp, back in sends:
            back.wait_recv()
            cp.wait_send()

    return pl.pallas_call(
        body, in_specs=[ANY], out_specs=ANY, out_shape=jax.ShapeDtypeStruct(wall.shape, wall.dtype),
        scratch_shapes=[pltpu.SemaphoreType.DMA((3,))] * 2, input_output_aliases={0: 0}, name=name,
    )(wall)


def _swap_halves(g, name):
    half = g.shape[1] // 2

    def body(g_ref, out_ref, send_sem, recv_sem):
        x, y, c, _ = _place()
        _, theirs = _halves(g.shape[1], c, 8)
        cp = pltpu.make_async_remote_copy(src_ref=g_ref.at[:, theirs], dst_ref=out_ref, send_sem=send_sem, recv_sem=recv_sem,
                                          device_id=(x, y, 1 - c), device_id_type=MESH)
        cp.start()
        cp.wait()

    return pl.pallas_call(
        body, in_specs=[ANY], out_specs=ANY,
        out_shape=jax.ShapeDtypeStruct((N_CHIPS, half, g.shape[2]), F32),
        scratch_shapes=[pltpu.SemaphoreType.DMA, pltpu.SemaphoreType.DMA],
        name=name,
    )(g)


def _core_index():
    return lax.axis_index("c").astype(jnp.int32).reshape(1)


def _row_tile(half):
    tile = max(t for t in range(16, 1025, 16) if half % t == 0)
    return tile, half // tile


def _add_sibling(g, got, name):
    half = g.shape[1] // 2
    tile, n_tiles = _row_tile(half)

    def body(c_ref, g_ref, got_ref, o_ref):
        o_ref[...] = (g_ref[...] + got_ref[...]).astype(BF16)

    width = g.shape[2]
    blk = pl.BlockSpec((None, tile, width), lambda s, i, c_ref: (s, i, 0))
    return pl.pallas_call(
        body,
        grid_spec=pltpu.PrefetchScalarGridSpec(
            num_scalar_prefetch=1, grid=(N_CHIPS, n_tiles),
            in_specs=[pl.BlockSpec((None, tile, width), lambda s, i, c_ref: (s, c_ref[0] * n_tiles + i, 0)), blk],
            out_specs=blk),
        out_shape=jax.ShapeDtypeStruct((N_CHIPS, half, width), BF16),
        name=name, compiler_params=_params("arbitrary", "arbitrary"),
    )(_core_index(), g, got)


def _exchange_copies(p_ref, land_ref, send_sems, recv_sems):
    x, y, c, chips = _place()
    me = 2 * x + y
    out, back = [], []
    for k, chip in enumerate(chips):
        peer = dict(send_sem=send_sems.at[k], recv_sem=recv_sems.at[k], device_id=(chip[0], chip[1], c), device_id_type=MESH)
        out.append(pltpu.make_async_remote_copy(src_ref=p_ref.at[2 * chip[0] + chip[1]], dst_ref=land_ref.at[me], **peer))
        slab = land_ref.at[2 * chip[0] + chip[1]]
        back.append(pltpu.make_async_remote_copy(src_ref=slab, dst_ref=slab, **peer))
    return out, back


def _with_own(got, part):
    me = _my_chip()
    return lax.dynamic_update_slice(got, lax.dynamic_slice(part, (me, 0, 0), (1,) + part.shape[1:]), (me, 0, 0))


def _exchange_chips(part, name):
    def body(p_ref, out_ref, send_sems, recv_sems):
        out, back = _exchange_copies(p_ref, out_ref, send_sems, recv_sems)
        for cp in out:
            cp.start()
        for cp in back:
            cp.wait_recv()
        for cp in out:
            cp.wait_send()

    got = pl.pallas_call(
        body, in_specs=[ANY], out_specs=ANY,
        out_shape=jax.ShapeDtypeStruct(part.shape, part.dtype),
        scratch_shapes=[pltpu.SemaphoreType.DMA((3,)), pltpu.SemaphoreType.DMA((3,))],
        name=name,
    )(part)
    return _with_own(got, part)


def _exchange_start(part, name):
    def start(p_ref, land_ref, send_sems, recv_sems):
        for cp in _exchange_copies(p_ref, land_ref, send_sems, recv_sems)[0]:
            cp.start()

    return _start_call(start, part, part.shape, _core_index(), name)


def _exchange_wait(started, after, name):
    def body(p_ref, land_ref, send_sems, recv_sems, after_ref, p_dead, land_out):
        del after_ref, p_dead, land_out
        out, back = _exchange_copies(p_ref, land_ref, send_sems, recv_sems)
        for cp_out, cp_back in zip(out, back):
            cp_out.wait_send()
            cp_back.wait_recv()

    part, got = _wait_call(body, started, after, name)
    return _with_own(got, part)


def _sum_chips(parts, name):
    half, width = parts.shape[1:]
    tile, n_tiles = _row_tile(half)

    def body(c_ref, p0, p1, p2, p3, o_ref):
        f32 = lambda p: p[...].astype(F32)
        o_ref[...] = ((f32(p0) + f32(p1)) + f32(p2)) + f32(p3)

    def slab(s):
        return pl.BlockSpec((None, tile, width), lambda i, c_ref, s=s: (s, i, 0))

    return pl.pallas_call(
        body,
        grid_spec=pltpu.PrefetchScalarGridSpec(
            num_scalar_prefetch=1, grid=(n_tiles,),
            in_specs=[slab(s) for s in range(N_CHIPS)],
            out_specs=pl.BlockSpec((None, tile, width), lambda i, c_ref: (c_ref[0], i, 0))),
        out_shape=jax.ShapeDtypeStruct((2, half, width), F32),
        name=name, compiler_params=_params("arbitrary"),
    )(_core_index(), parts, parts, parts, parts)


def _share_halves(halves, name):
    def body(h_ref, out_ref, send_sem, recv_sem):
        del h_ref
        x, y, c, _ = _place()
        cp = pltpu.make_async_remote_copy(src_ref=out_ref.at[c], dst_ref=out_ref.at[c], send_sem=send_sem, recv_sem=recv_sem,
                                          device_id=(x, y, 1 - c), device_id_type=MESH)
        cp.start()
        pltpu.make_async_remote_copy(src_ref=out_ref.at[1 - c], dst_ref=out_ref.at[1 - c], send_sem=send_sem, recv_sem=recv_sem,
                                     device_id=(x, y, 1 - c), device_id_type=MESH).wait_recv()
        cp.wait_send()

    return pl.pallas_call(
        body, in_specs=[ANY], out_specs=ANY,
        out_shape=jax.ShapeDtypeStruct(halves.shape, halves.dtype),
        scratch_shapes=[pltpu.SemaphoreType.DMA] * 2,
        input_output_aliases={0: 0},
        name=name,
    )(halves)


def _reduce_parts(g, tag):
    return _add_sibling(g, _swap_halves(g, "swap_" + tag), "add_" + tag)


def _reduce_finish(got, tag):
    halves = _share_halves(_sum_chips(got, "sum_" + tag), "share_" + tag)
    return halves.reshape(2 * halves.shape[1], halves.shape[2])


SMALL_ROWS = 8


def _allreduce_small(v):
    def body(v_ref, out_ref, buf, send_sems, recv_sems):
        x, y, c, _ = _place()
        buf[4 * x + 2 * y + c] = v_ref[...]
        sends = []
        for k in range(1, N_DEV):
            px = 1 - x if k & 4 else x
            py = 1 - y if k & 2 else y
            pc = 1 - c if k & 1 else c
            cp = pltpu.make_async_remote_copy(src_ref=v_ref, dst_ref=buf.at[4 * x + 2 * y + c], send_sem=send_sems.at[k - 1],
                                              recv_sem=recv_sems.at[k - 1], device_id=(px, py, pc), device_id_type=MESH)
            cp.start()
            sends.append((cp, 4 * px + 2 * py + pc))
        for k, (cp, peer) in enumerate(sends):
            pltpu.make_async_remote_copy(src_ref=v_ref, dst_ref=buf.at[peer], send_sem=send_sems.at[k], recv_sem=recv_sems.at[k],
                                         device_id=(x, y, c), device_id_type=MESH).wait_recv()
        for cp, _ in sends:
            cp.wait_send()
        total = buf[0]
        for d in range(1, N_DEV):
            total = total + buf[d]
        out_ref[...] = total

    vmem = pl.BlockSpec(memory_space=pltpu.VMEM)
    return pl.pallas_call(
        body, in_specs=[vmem], out_specs=vmem,
        out_shape=jax.ShapeDtypeStruct(v.shape, v.dtype),
        scratch_shapes=[pltpu.VMEM((N_DEV,) + v.shape, v.dtype), pltpu.SemaphoreType.DMA((N_DEV - 1,)),
                        pltpu.SemaphoreType.DMA((N_DEV - 1,))],
        name="allreduce_small",
    )(v)


MATRICES = ("w_in", "w_out", "w_xq", "w_xk", "w_xv", "w_xo", "w_up", "w_down")
VECTORS = ("g_mix", "g_xattn", "g_mem", "g_mlp", "g_final", "b_forget")
WEIGHT_ORDER = ("g_mix", "w_in", "b_forget", "w_out", "g_xattn", "g_mem", "w_xq", "w_xk", "w_xv", "w_xo",
                "g_mlp", "w_up", "w_down", "g_final")
GROUPS = {"mlp": ("w_up", "w_down"), "mid": ("w_out", "w_xq", "w_xk", "w_xv", "w_xo"), "in": ("w_in",)}
LATE = GROUPS["mid"] + GROUPS["mlp"]
W_IN_SHARD = IN_WIDTH // N_CHIPS
SHARD_ROWS = {"w_out": 256, "w_xq": 256, "w_xk": 256, "w_xv": 256, "w_xo": 256, "w_up": 1024, "w_down": 1024}
PACK_ROWS = SHARD_ROWS
W_IN_PAD = -(-W_IN_SHARD // LANES) * LANES
ADAM_ROWS = 128


def _pack(parts, names):
    return jnp.concatenate([jnp.pad(parts[n], ((0, PACK_ROWS[n] - SHARD_ROWS[n]), (0, 0))) for n in names], axis=0)


def _unpack(a, names):
    out, pos = {}, 0
    for n in names:
        out[n] = a[..., pos:pos + SHARD_ROWS[n], :]
        pos += PACK_ROWS[n]
    return out


def _full_weights(wall, names):
    cols = lambda a: a.transpose(1, 0, 2).reshape(a.shape[1], -1)
    rows = lambda a: a.reshape(-1, a.shape[-1])
    if names == GROUPS["in"]:
        return {"w_in": cols(wall[:, :, :W_IN_SHARD])}
    return {n: cols(a) if n == "w_up" else rows(a) for n, a in _unpack(wall, names).items()}


def _shard_of(g, name, s):
    if name == "w_up":
        return g[:, s * D_MODEL:(s + 1) * D_MODEL]
    n = SHARD_ROWS[name]
    return g[s * n:(s + 1) * n]


def _pad_w_in(a):
    return jnp.pad(a, [(0, 0)] * (a.ndim - 1) + [(0, W_IN_PAD - W_IN_SHARD)])


def _pack_grads(gws, names):
    if names == GROUPS["in"]:
        return _pad_w_in(gws["w_in"].reshape(D_MODEL, N_CHIPS, W_IN_SHARD).transpose(1, 0, 2))
    return jnp.stack([_pack({n: _shard_of(gws[n], n, s) for n in names}, names) for s in range(N_CHIPS)])


def kernel(x, mem, g_mix, w_in, b_forget, w_out, g_xattn, g_mem, w_xq, w_xk, w_xv, w_xo, g_mlp, w_up, w_down, g_final, loss_target, m_g_mix, m_w_in, m_b_forget, m_w_out, m_g_xattn, m_g_mem, m_w_xq, m_w_xk, m_w_xv, m_w_xo, m_g_mlp, m_w_up, m_w_down, m_g_final, v_g_mix, v_w_in, v_b_forget, v_w_out, v_g_xattn, v_g_mem, v_w_xq, v_w_xk, v_w_xv, v_w_xo, v_g_mlp, v_w_up, v_w_down, v_g_final):
    given = dict(locals())
    weights = {n: given[n] for n in WEIGHT_ORDER}
    vecs = {n: weights[n] for n in VECTORS}

    shard = {n: weights[n].astype(BF16) for n in MATRICES}
    in_pack, late_pack = _pad_w_in(shard["w_in"]), _pack(shard, LATE)
    in_wall = _gather(in_pack, "gather_in", in_pack)
    late = _gather_start(late_pack, in_wall, "gather_late_start")
    w_in_full = _full_weights(in_wall, GROUPS["in"])["w_in"]

    def late_weights(after):
        pack, wall = _gather_wait(late, after, "gather_late_wait")
        return _full_weights(_place_own(_pass_on(wall, "gather_late_pass"), pack), LATE)

    started = {}

    def on_grads(group, gws):
        packed = gws if group == "mlp" else _pack_grads(gws, GROUPS[group])
        part = _reduce_parts(packed, group)
        started[group] = _exchange_start(part, "exchange_%s_start" % group)
        return started[group][4]

    loss, grad_x, gw, gv = _local_step(x, mem, loss_target, vecs, w_in_full, late_weights, on_grads)

    on_grads("in", gw)
    grads, delta, new_m, new_v = {}, {}, {}, {}

    def finish(group, after):
        got = _exchange_wait(started[group], after, "exchange_%s_wait" % group)
        done = _reduce_finish(got, group)
        for n, a in ({"w_in": done[:, :W_IN_SHARD]} if group == "in" else _unpack(done, GROUPS[group])).items():
            grads[n] = a.reshape(weights[n].shape)
            delta[n], new_m[n], new_v[n] = _adamw(weights[n], grads[n], given["m_" + n], given["v_" + n], "adamw_" + n, ADAM_ROWS)
        return new_v[GROUPS[group][-1]]

    after = finish("mlp", started["in"][4])
    after = finish("mid", after)

    row = lambda a: jnp.pad(a.reshape(-1), (0, D_MODEL - a.size)).reshape(1, D_MODEL)
    small = jnp.concatenate([gv[n] for n in VECTORS[:5]] + [row(gv["b_forget"][:, 0]), row(loss[0, :1]),
                             jnp.zeros((1, D_MODEL), F32)], axis=0)
    small = _allreduce_small(small)
    for k, n in enumerate(VECTORS[:5]):
        grads[n] = small[k]
    grads["b_forget"] = small[5, :N_HEADS]
    loss_total = small[6, 0]
    finish("in", after)

    stack = lambda prefix: jnp.concatenate([row(given[prefix + n]) for n in VECTORS] + [jnp.zeros((2, D_MODEL), F32)], axis=0)
    g_small = jnp.concatenate([small[:6], jnp.zeros((2, D_MODEL), F32)], axis=0)
    d, m1, v1 = _adamw(stack(""), g_small, stack("m_"), stack("v_"), "adamw_vectors", SMALL_ROWS)
    for k, n in enumerate(VECTORS):
        width = weights[n].shape[0]
        delta[n], new_m[n], new_v[n] = d[k, :width], m1[k, :width], v1[k, :width]

    return (loss_total, grad_x, *[grads[n] for n in WEIGHT_ORDER], *[delta[n] for n in WEIGHT_ORDER],
            *[new_m[n] for n in WEIGHT_ORDER], *[new_v[n] for n in WEIGHT_ORDER])
```

```python
import functools
import math

import jax
import jax.numpy as jnp
from jax import lax
from jax.experimental import pallas as pl
from jax.experimental.pallas import tpu as pltpu

F32 = jnp.float32
BF16 = jnp.bfloat16

D_MODEL = 1024
SEQ = 2048
N_MEM = 256
HEAD_DIM = 64
N_HEADS = 8
MIX_HALF = N_HEADS * HEAD_DIM
QKV_WIDTH = 6 * MIX_HALF
IN_WIDTH = QKV_WIDTH + N_HEADS
GATE_PAD = 128
BLOCK = 128
DILATIONS = (1, 4, 16)
X_HEADS = 4
X_HEAD_DIM = 256
D_FF = 4096
EPS = 1e-6
NEG = -1e30
ATT_SCALE = 1.0 / math.sqrt(HEAD_DIM)
X_SCALE = 1.0 / math.sqrt(X_HEAD_DIM)
LANES = 128
N_CHIPS = 4
N_DEV = 8

ADAM_LR = 0.001
ADAM_B1 = 0.9
ADAM_B2 = 0.999
ADAM_EPS = 1e-08
ADAM_WD = 0.01
ADAM_STEP = 10

VMEM_LIMIT = 48 * 1024 * 1024


def _params(*sem):
    return pltpu.CompilerParams(dimension_semantics=sem or None, vmem_limit_bytes=VMEM_LIMIT)


def _dot(a, b):
    return jnp.dot(a, b, preferred_element_type=F32)


def _dot_nt(a, b):
    return lax.dot_general(a, b, (((1,), (1,)), ((), ())), preferred_element_type=F32)


def _dot_tn(a, b):
    return lax.dot_general(a, b, (((0,), (0,)), ((), ())), preferred_element_type=F32)


def _dot_exact(x, e):
    hi = x.astype(BF16)
    r1 = x - hi.astype(F32)
    mid = r1.astype(BF16)
    lo = (r1 - mid.astype(F32)).astype(BF16)
    return _dot(hi, e) + _dot(mid, e) + _dot(lo, e)


def _head_mask(e):
    lane = lax.broadcasted_iota(jnp.int32, (1, LANES), 1)
    return (lane >= HEAD_DIM * e) & (lane < HEAD_DIM * (e + 1))


def _matmul(a, w, name, out_dtypes=(F32,), extras=(), epilogue=None, tm=1024, tn=512, w_t=False, after=None):
    m, k = a.shape
    n = w.shape[0] if w_t else w.shape[1]
    tm, tn = min(tm, m), min(tn, n)
    assert m % tm == 0 and n % tn == 0, (name, a.shape, w.shape)
    n_ex = len(extras)
    order = () if after is None else (after,)

    def body(a_ref, w_ref, *rest):
        rest = rest[len(order):]
        acc = (_dot_nt if w_t else _dot)(a_ref[...], w_ref[...])
        res = (acc,) if epilogue is None else epilogue(acc, *[r[...] for r in rest[:n_ex]])
        for o_ref, r in zip(rest[n_ex:], res):
            o_ref[...] = r.astype(o_ref.dtype)

    tile = pl.BlockSpec((tm, tn), lambda i, j: (i, j))
    w_spec = pl.BlockSpec((tn, k), lambda i, j: (j, 0)) if w_t else pl.BlockSpec((k, tn), lambda i, j: (0, j))
    return pl.pallas_call(
        body, grid=(m // tm, n // tn),
        in_specs=[pl.BlockSpec((tm, k), lambda i, j: (i, 0)), w_spec] + [pl.BlockSpec(memory_space=pl.ANY)] * len(order) + [tile] * n_ex,
        out_specs=[tile] * len(out_dtypes),
        out_shape=[jax.ShapeDtypeStruct((m, n), dt) for dt in out_dtypes],
        name=name, compiler_params=_params("parallel", "arbitrary"),
    )(a, w, *order, *extras)


def _matmul_res(a, w, res, name, w_t=False):
    return _matmul(a, w, name, extras=(res,), epilogue=lambda acc, r: (r + acc,), w_t=w_t)[0]


def _matmul_tn(x, y, name, tm=1024, tn=1024, tk=512, packed=None, after=None):
    t, m = x.shape
    _, n = y.shape
    tm, tn, tk = min(tm, m), min(tn, n), min(tk, t)
    assert m % tm == 0 and n % tn == 0 and t % tk == 0, (name, x.shape, y.shape)
    shape, place, into = packed or ((m, n), None, None)

    def body(x_ref, y_ref, *rest):
        o_ref = rest[-1]

        @pl.when(pl.program_id(2) == 0)
        def _():
            o_ref[...] = jnp.zeros_like(o_ref)

        o_ref[...] += _dot_tn(x_ref[...], y_ref[...])

    out_spec = (pl.BlockSpec((tm, tn), lambda i, j, k: (i, j)) if place is None
                else pl.BlockSpec((None, tm, tn), lambda i, j, k: place(i, j)))
    return pl.pallas_call(
        body, grid=(m // tm, n // tn, t // tk),
        in_specs=[pl.BlockSpec((tk, tm), lambda i, j, k: (k, i)), pl.BlockSpec((tk, tn), lambda i, j, k: (k, j))]
        + [pl.BlockSpec(memory_space=pl.ANY)] * ((into is not None) + (after is not None)),
        out_specs=out_spec, out_shape=jax.ShapeDtypeStruct(shape, F32),
        input_output_aliases={} if into is None else {2: 0},
        name=name, compiler_params=_params("parallel", "parallel", "arbitrary"),
    )(x, y, *(() if into is None else (into,)), *(() if after is None else (after,)))


def _rmsnorm(x, g, name, tm=512):
    t, d = x.shape
    tm = min(tm, t)

    def body(x_ref, g_ref, h_ref):
        xv = x_ref[...]
        r = lax.rsqrt(jnp.mean(xv * xv, axis=-1, keepdims=True) + EPS)
        h_ref[...] = (xv * r * g_ref[...]).astype(BF16)

    return pl.pallas_call(
        body, grid=(t // tm,),
        in_specs=[pl.BlockSpec((tm, d), lambda i: (i, 0)), pl.BlockSpec((1, d), lambda i: (0, 0))],
        out_specs=pl.BlockSpec((tm, d), lambda i: (i, 0)),
        out_shape=jax.ShapeDtypeStruct((t, d), BF16),
        name=name, compiler_params=_params("arbitrary"),
    )(x, g.reshape(1, d))


def _in_proj(x, g, w_all, name, tm=512):
    t, d = x.shape
    half = 3 * MIX_HALF

    def body(x_ref, g_ref, w_ref, h_ref, zd_ref, zf_ref, gate_ref):
        xv = x_ref[...]
        r = lax.rsqrt(jnp.mean(xv * xv, axis=-1, keepdims=True) + EPS)
        h = (xv * r * g_ref[...]).astype(BF16)
        h_ref[...] = h
        zd_ref[...] = _dot(h, w_ref[:, 0:half])
        zf_ref[...] = _dot(h, w_ref[:, half:2 * half]).astype(BF16)
        gate_ref[...] = _dot(h, w_ref[:, 2 * half:])

    row = lambda width: pl.BlockSpec((tm, width), lambda i: (i, 0))
    return pl.pallas_call(
        body, grid=(t // tm,),
        in_specs=[row(d), pl.BlockSpec((1, d), lambda i: (0, 0)), pl.BlockSpec(w_all.shape, lambda i: (0, 0))],
        out_specs=[row(d), row(half), row(half), row(GATE_PAD)],
        out_shape=[jax.ShapeDtypeStruct((t, d), BF16), jax.ShapeDtypeStruct((t, half), F32),
                   jax.ShapeDtypeStruct((t, half), BF16), jax.ShapeDtypeStruct((t, GATE_PAD), F32)],
        name=name, compiler_params=_params("arbitrary"),
    )(x, g.reshape(1, d), w_all)


def _rms_bwd_tile(xv, dh, g):
    d = xv.shape[-1]
    r = lax.rsqrt(jnp.mean(xv * xv, axis=-1, keepdims=True) + EPS)
    dyg = dh * g
    proj = jnp.sum(dyg * xv, axis=-1, keepdims=True)
    dx = r * dyg - xv * (r * r * r * (1.0 / d)) * proj
    return dx, dh * (xv * r)


def _rms_bwd(x, dh, g, dres, name, tm=512):
    t, d = x.shape
    tm = min(tm, t)
    has_res = dres is not None

    def body(x_ref, dh_ref, g_ref, *rest):
        if has_res:
            res_ref, dx_ref, dxb_ref, dg_ref = rest
        else:
            dx_ref, dxb_ref, dg_ref = rest
        dx, dg_rows = _rms_bwd_tile(x_ref[...], dh_ref[...], g_ref[...])
        if has_res:
            dx = res_ref[...] + dx
        dx_ref[...] = dx
        dxb_ref[...] = dx.astype(BF16)

        @pl.when(pl.program_id(0) == 0)
        def _():
            dg_ref[...] = jnp.zeros_like(dg_ref)

        dg_ref[...] += jnp.sum(dg_rows, axis=0, keepdims=True)

    row = pl.BlockSpec((tm, d), lambda i: (i, 0))
    vec = pl.BlockSpec((1, d), lambda i: (0, 0))
    return pl.pallas_call(
        body, grid=(t // tm,),
        in_specs=[row, row, vec] + ([row] if has_res else []),
        out_specs=[row, row, vec],
        out_shape=[jax.ShapeDtypeStruct((t, d), F32), jax.ShapeDtypeStruct((t, d), BF16), jax.ShapeDtypeStruct((1, d), F32)],
        name=name, compiler_params=_params("arbitrary"),
    )(x, dh, g.reshape(1, d), *((dres,) if has_res else ()))


def _row_dots(a_refs, w_refs, w_t):
    acc = None
    for a_ref, w_ref in zip(a_refs, w_refs):
        part = (_dot_nt if w_t else _dot)(a_ref[...], w_ref[...])
        acc = part if acc is None else acc + part
    return acc


def _row_specs(a_parts, w_parts, tm):
    specs = [pl.BlockSpec((tm, a.shape[1]), lambda i: (i, 0)) for a in a_parts]
    return specs + [pl.BlockSpec(w.shape, lambda i: (0, 0)) for w in w_parts]


def _matmul_res_norm(a_parts, w_parts, res, g, name, tm=512):
    t, d = res.shape
    n = len(a_parts)

    def body(*refs):
        res_ref, g_ref, x_ref, h_ref = refs[2 * n:]
        xv = res_ref[...] + _row_dots(refs[:n], refs[n:2 * n], False)
        x_ref[...] = xv
        r = lax.rsqrt(jnp.mean(xv * xv, axis=-1, keepdims=True) + EPS)
        h_ref[...] = (xv * r * g_ref[...]).astype(BF16)

    row = pl.BlockSpec((tm, d), lambda i: (i, 0))
    return pl.pallas_call(
        body, grid=(t // tm,),
        in_specs=_row_specs(a_parts, w_parts, tm) + [row, pl.BlockSpec((1, d), lambda i: (0, 0))],
        out_specs=[row, row],
        out_shape=[jax.ShapeDtypeStruct((t, d), F32), jax.ShapeDtypeStruct((t, d), BF16)],
        name=name, compiler_params=_params("arbitrary"),
    )(*a_parts, *w_parts, res, g.reshape(1, d))


def _matmul_rms_bwd(a_parts, w_parts, x, g, dres, name, tm=512, after=None):
    t, d = x.shape
    n = len(a_parts)
    order = () if after is None else (after,)

    def body(*refs):
        x_ref, g_ref, res_ref = refs[2 * n:2 * n + 3]
        dx_ref, dxb_ref, dg_ref = refs[2 * n + 3 + len(order):]
        dx, dg_rows = _rms_bwd_tile(x_ref[...], _row_dots(refs[:n], refs[n:2 * n], True), g_ref[...])
        dx = res_ref[...] + dx
        dx_ref[...] = dx
        dxb_ref[...] = dx.astype(BF16)

        @pl.when(pl.program_id(0) == 0)
        def _():
            dg_ref[...] = jnp.zeros_like(dg_ref)

        dg_ref[...] += jnp.sum(dg_rows, axis=0, keepdims=True)

    row = pl.BlockSpec((tm, d), lambda i: (i, 0))
    vec = pl.BlockSpec((1, d), lambda i: (0, 0))
    return pl.pallas_call(
        body, grid=(t // tm,),
        in_specs=_row_specs(a_parts, w_parts, tm) + [row, vec, row] + [pl.BlockSpec(memory_space=pl.ANY)] * len(order),
        out_specs=[row, row, vec],
        out_shape=[jax.ShapeDtypeStruct((t, d), F32), jax.ShapeDtypeStruct((t, d), BF16), jax.ShapeDtypeStruct((1, d), F32)],
        name=name, compiler_params=_params("arbitrary"),
    )(*a_parts, *w_parts, x, g.reshape(1, d), dres, *order)


def _loss_bwd(a, w, res, g, target, name, tm=512):
    t, d = res.shape

    def body(a_ref, w_ref, x_ref, g_ref, t_ref, loss_ref, dx_ref, dxb_ref, dg_ref):
        xv = x_ref[...] + _dot(a_ref[...], w_ref[...])
        gv = g_ref[...]
        r = lax.rsqrt(jnp.mean(xv * xv, axis=-1, keepdims=True) + EPS)
        err = xv * r * gv - t_ref[...]
        dx, dg_rows = _rms_bwd_tile(xv, err * (1.0 / d), gv)
        dx_ref[...] = dx
        dxb_ref[...] = dx.astype(BF16)

        @pl.when(pl.program_id(0) == 0)
        def _():
            dg_ref[...] = jnp.zeros_like(dg_ref)
            loss_ref[...] = jnp.zeros_like(loss_ref)

        dg_ref[...] += jnp.sum(dg_rows, axis=0, keepdims=True)
        part = jnp.sum(jnp.sum(err * err, axis=0, keepdims=True), axis=1, keepdims=True) * (0.5 / d)
        loss_ref[...] += jnp.broadcast_to(part, loss_ref.shape)

    row = pl.BlockSpec((tm, d), lambda i: (i, 0))
    vec = pl.BlockSpec((1, d), lambda i: (0, 0))
    return pl.pallas_call(
        body, grid=(t // tm,),
        in_specs=_row_specs([a], [w], tm) + [row, vec, row],
        out_specs=[pl.BlockSpec((1, LANES), lambda i: (0, 0)), row, row, vec],
        out_shape=[jax.ShapeDtypeStruct((1, LANES), F32), jax.ShapeDtypeStruct((t, d), F32),
                   jax.ShapeDtypeStruct((t, d), BF16), jax.ShapeDtypeStruct((1, d), F32)],
        name=name, compiler_params=_params("arbitrary"),
    )(a, w, res, g.reshape(1, d), target)


def _tri(upper):
    r = lax.broadcasted_iota(jnp.int32, (LANES, LANES), 0)
    c = lax.broadcasted_iota(jnp.int32, (LANES, LANES), 1)
    return jnp.where((r <= c) if upper else (r >= c), 1.0, 0.0).astype(BF16)


def _gate_fwd(gate, b_pad, n_batch, name):
    s = SEQ
    nblk = s // LANES

    def body(g_ref, b_ref, cbc_ref, crow_ref, sg_ref, ct_ref):
        gz = g_ref[...] + b_ref[...]
        logf = jnp.minimum(gz, 0.0) - jnp.log(1.0 + jnp.exp(-jnp.abs(gz)))
        logf_t = logf.T
        sg_ref[...] = (1.0 / (1.0 + jnp.exp(gz))).T[0:N_HEADS]
        upper = _tri(True)
        carry = jnp.zeros((LANES, 1), F32)
        for blk in range(nblk):
            seg = _dot_exact(logf_t[:, blk * LANES:(blk + 1) * LANES], upper) + carry
            carry = seg[:, LANES - 1:LANES]
            ct_ref[:, blk * LANES:(blk + 1) * LANES] = seg
        ct = ct_ref[...]
        crow_ref[...] = ct[0:N_HEADS]
        c_col = ct.T
        lane = lax.broadcasted_iota(jnp.int32, (1, MIX_HALF), 1)
        acc = jnp.zeros((s, MIX_HALF), F32)
        for h in range(N_HEADS):
            acc = jnp.where((lane >= HEAD_DIM * h) & (lane < HEAD_DIM * (h + 1)), c_col[:, h:h + 1], acc)
        cbc_ref[...] = acc

    return pl.pallas_call(
        body, grid=(n_batch,),
        in_specs=[pl.BlockSpec((s, GATE_PAD), lambda b: (b, 0)), pl.BlockSpec((1, GATE_PAD), lambda b: (0, 0))],
        out_specs=[pl.BlockSpec((s, MIX_HALF), lambda b: (b, 0)),
                   pl.BlockSpec((None, N_HEADS, s), lambda b: (b, 0, 0)),
                   pl.BlockSpec((None, N_HEADS, s), lambda b: (b, 0, 0))],
        out_shape=[jax.ShapeDtypeStruct((n_batch * s, MIX_HALF), F32),
                   jax.ShapeDtypeStruct((n_batch, N_HEADS, s), F32),
                   jax.ShapeDtypeStruct((n_batch, N_HEADS, s), F32)],
        scratch_shapes=[pltpu.VMEM((LANES, s), F32)],
        name=name, compiler_params=_params("arbitrary"),
    )(gate, b_pad)


def _gate_bwd(dc, sg, name):
    n_batch, _, s = dc.shape
    nblk = s // LANES

    def body(dc_ref, sg_ref, dz_ref, db_ref, dt_ref):
        lower = _tri(False)
        dcv = dc_ref[...]
        carry = jnp.zeros((N_HEADS, 1), F32)
        dt_ref[...] = jnp.zeros_like(dt_ref)
        for blk in reversed(range(nblk)):
            seg = _dot_exact(dcv[:, blk * LANES:(blk + 1) * LANES], lower) + carry
            carry = seg[:, 0:1]
            dt_ref[0:N_HEADS, blk * LANES:(blk + 1) * LANES] = seg * sg_ref[:, blk * LANES:(blk + 1) * LANES]
        dg_t = dt_ref[...]
        dz_ref[...] = dg_t.T.astype(BF16)

        @pl.when(pl.program_id(0) == 0)
        def _():
            db_ref[...] = jnp.zeros_like(db_ref)

        db_ref[...] += jnp.broadcast_to(jnp.sum(dg_t[0:N_HEADS], axis=1, keepdims=True), db_ref.shape)

    return pl.pallas_call(
        body, grid=(n_batch,),
        in_specs=[pl.BlockSpec((None, N_HEADS, s), lambda b: (b, 0, 0)), pl.BlockSpec((None, N_HEADS, s), lambda b: (b, 0, 0))],
        out_specs=[pl.BlockSpec((s, GATE_PAD), lambda b: (b, 0)), pl.BlockSpec((N_HEADS, LANES), lambda b: (0, 0))],
        out_shape=[jax.ShapeDtypeStruct((n_batch * s, GATE_PAD), BF16), jax.ShapeDtypeStruct((N_HEADS, LANES), F32)],
        scratch_shapes=[pltpu.VMEM((LANES, s), F32)],
        name=name, compiler_params=_params("arbitrary"),
    )(dc, sg)


FOX_BQ = 512
FOX_BK = 512
FOX_STRIP = 512
PAIR_WIDTH = 3 * LANES
N_PAIRS = N_HEADS // 2


def _pair_major(w):
    return w.reshape(w.shape[0], 3, N_PAIRS, LANES).transpose(0, 2, 1, 3).reshape(w.shape[0], 3 * MIX_HALF)


def _pair_major_inv(w):
    return w.reshape(w.shape[0], N_PAIRS, 3, LANES).transpose(0, 2, 1, 3).reshape(w.shape[0], 3 * MIX_HALF)


def _causal(i, j, bq, bk):
    qpos = i * bq + lax.broadcasted_iota(jnp.int32, (bq, 1), 0)
    kpos = j * bk + lax.broadcasted_iota(jnp.int32, (1, bk), 1)
    return kpos <= qpos


def _split_bf16(p):
    hi = p.astype(BF16)
    return hi, (p - hi.astype(F32)).astype(BF16)


def _fox_fwd(zf, c_bc, c_row, n_batch, name):
    s, bq, bk = SEQ, FOX_BQ, FOX_BK
    nq = s // bq
    t = n_batch * s

    n_strip = bq // FOX_STRIP

    def body(q_ref, k_ref, v_ref, cq_ref, cr_ref, o_ref, o32_ref, lse_ref):
        hp = pl.program_id(1)
        strips = [slice(r * FOX_STRIP, (r + 1) * FOX_STRIP) for r in range(n_strip)]
        chains = [(e, r) for e in range(2) for r in range(n_strip)]
        qh, cq = {}, {}
        for e, r in chains:
            q = q_ref[strips[r], :] * ATT_SCALE
            qh[e, r] = jnp.where(_head_mask(e), q, jnp.zeros_like(q))
            cq[e, r] = cq_ref[strips[r], HEAD_DIM * e:HEAD_DIM * e + 1]

        def step(i, j, carry, masked):
            rows = pl.ds(j * bk, bk)
            kj, vj = k_ref[rows, :], v_ref[rows, :]
            ck = [cr_ref[pl.ds(2 * hp + e, 1), rows] for e in range(2)]
            out = []
            scores = [_dot_nt(qh[e, r], kj) for e, r in chains]
            for n, (e, r) in enumerate(chains):
                m, l, acc = carry[3 * n:3 * n + 3]
                sc = scores[n] + (cq[e, r] - ck[e])
                if masked:
                    qpos = i * bq + r * FOX_STRIP + lax.broadcasted_iota(jnp.int32, (FOX_STRIP, 1), 0)
                    kpos = j * bk + lax.broadcasted_iota(jnp.int32, (1, bk), 1)
                    sc = jnp.where(kpos <= qpos, sc, NEG)
                m_new = jnp.maximum(m, jnp.max(sc, axis=1, keepdims=True))
                alpha = jnp.exp(m - m_new)
                p = jnp.exp(sc - m_new)
                p_hi, p_lo = _split_bf16(p)
                out += [m_new, alpha * l + jnp.sum(p, axis=1, keepdims=True), alpha * acc + (_dot(p_hi, vj) + _dot(p_lo, vj))]
            return tuple(out)

        def run(i):
            carry = (jnp.full((FOX_STRIP, 1), NEG, F32), jnp.zeros((FOX_STRIP, 1), F32), jnp.zeros((FOX_STRIP, LANES), F32)) * len(chains)
            n_clear = (i * bq) // bk
            for j in range((i * bq + bq + bk - 1) // bk):
                carry = step(i, j, carry, masked=j >= n_clear)
            for r in range(n_strip):
                outs = [carry[3 * (e * n_strip + r) + 2] / carry[3 * (e * n_strip + r) + 1] for e in range(2)]
                lses = [carry[3 * (e * n_strip + r)] + jnp.log(carry[3 * (e * n_strip + r) + 1]) for e in range(2)]
                o = jnp.where(_head_mask(0), outs[0], outs[1])
                o_ref[strips[r], :] = o.astype(BF16)
                o32_ref[strips[r], :] = o
                lse_ref[strips[r], :] = jnp.where(_head_mask(0), lses[0], lses[1])

        for k in range(nq):
            pl.when(pl.program_id(2) == k)(functools.partial(run, k))

    def col(c0):
        return lambda b, hp, i: (b, 3 * hp + c0)

    blk = pl.BlockSpec((bq, LANES), lambda b, hp, i: (b * nq + i, hp))
    return pl.pallas_call(
        body, grid=(n_batch, N_PAIRS, nq),
        in_specs=[pl.BlockSpec((bq, LANES), lambda b, hp, i: (b * nq + i, 3 * hp)),
                  pl.BlockSpec((s, LANES), col(1)), pl.BlockSpec((s, LANES), col(2)), blk,
                  pl.BlockSpec((None, N_HEADS, s), lambda b, hp, i: (b, 0, 0))],
        out_specs=[blk, blk, blk],
        out_shape=[jax.ShapeDtypeStruct((t, MIX_HALF), BF16), jax.ShapeDtypeStruct((t, MIX_HALF), F32),
                   jax.ShapeDtypeStruct((t, MIX_HALF), F32)],
        name=name, compiler_params=_params("parallel", "parallel", "arbitrary"),
    )(zf, zf, zf, c_bc, c_row)


def _fox_bwd(zf, o32, dy, lse, c_bc, c_row, dz, n_batch, name):
    s, bq, bk = SEQ, FOX_BQ, FOX_BK
    nq, nk = s // bq, s // bk

    def body(q_ref, k_ref, v_ref, o_ref, do_ref, lse_ref, cq_ref, cr_ref, dz_in, dz_ref, dc_ref, dq_acc):
        del dz_in
        hp = pl.program_id(1)

        @pl.when(pl.program_id(2) == 0)
        def _():
            dq_acc[...] = jnp.zeros_like(dq_acc)

        kj, vj = k_ref[...], v_ref[...]
        km = [jnp.where(_head_mask(e), kj, jnp.zeros_like(kj)) for e in range(2)]

        def step(i, j, ck, carry, masked):
            rows = pl.ds(i * bq, bq)
            qi, doi = q_ref[rows, :] * ATT_SCALE, do_ref[rows, :]
            prod = doi.astype(F32) * o_ref[rows, :]
            out = []
            dq = jnp.zeros((bq, LANES), F32)
            for e in range(2):
                dk_a, dv_a, dc_a = carry[3 * e:3 * e + 3]
                mask = _head_mask(e)
                lane0 = HEAD_DIM * e
                dom = jnp.where(mask, doi, jnp.zeros_like(doi))
                delta = jnp.sum(jnp.where(mask, prod, 0.0), axis=1, keepdims=True)
                sc = _dot_nt(qi, km[e]) + (cq_ref[rows, lane0:lane0 + 1] - ck[e])
                if masked:
                    sc = jnp.where(_causal(i, j, bq, bk), sc, NEG)
                p = jnp.exp(sc - lse_ref[rows, lane0:lane0 + 1])
                ds = p * (_dot_nt(dom, vj) - delta)
                dsb = ds.astype(BF16)
                dq = dq + _dot(dsb, km[e])
                out += [dk_a + _dot_tn(dsb, qi), dv_a + _dot_tn(p.astype(BF16), dom), dc_a - jnp.sum(ds, axis=0, keepdims=True)]
            dq_acc[rows, :] += dq * ATT_SCALE
            return tuple(out)

        def run(j):
            cols = pl.ds(j * bk, bk)
            ck = [cr_ref[pl.ds(2 * hp + e, 1), cols] for e in range(2)]
            carry = (jnp.zeros((bk, LANES), F32), jnp.zeros((bk, LANES), F32), jnp.zeros((1, bk), F32)) * 2
            n_diag = (j * bk + bk + bq - 1) // bq
            for i in range((j * bk) // bq, nq):
                carry = step(i, j, ck, carry, masked=i < n_diag)
            for e in range(2):
                dc_ref[e:e + 1, :] = carry[3 * e + 2]
            dz_ref[cols, LANES:2 * LANES] = jnp.where(_head_mask(0), carry[0], carry[3]).astype(BF16)
            dz_ref[cols, 2 * LANES:3 * LANES] = (carry[1] + carry[4]).astype(BF16)
            if j == nk - 1:
                dz_ref[:, 0:LANES] = dq_acc[...].astype(BF16)

        for k in range(nk):
            pl.when(pl.program_id(2) == k)(functools.partial(run, k))

    def seq(idx):
        return pl.BlockSpec((s, LANES), lambda b, hp, j: (b, idx(hp)))

    def kblk(c0):
        return pl.BlockSpec((bk, LANES), lambda b, hp, j: (b * nk + j, 3 * hp + c0))

    return pl.pallas_call(
        body, grid=(n_batch, N_PAIRS, nk),
        in_specs=[seq(lambda hp: 3 * hp), kblk(1), kblk(2), seq(lambda hp: hp), seq(lambda hp: N_PAIRS + hp),
                  seq(lambda hp: hp), seq(lambda hp: hp),
                  pl.BlockSpec((None, N_HEADS, s), lambda b, hp, j: (b, 0, 0)), pl.BlockSpec(memory_space=pl.ANY)],
        out_specs=[pl.BlockSpec((s, PAIR_WIDTH), lambda b, hp, j: (b, N_PAIRS + hp)),
                   pl.BlockSpec((None, None, 2, bk), lambda b, hp, j: (b, hp, 0, j))],
        out_shape=[jax.ShapeDtypeStruct(dz.shape, dz.dtype), jax.ShapeDtypeStruct((n_batch, N_PAIRS, 2, s), F32)],
        scratch_shapes=[pltpu.VMEM((s, LANES), F32)],
        input_output_aliases={8: 0},
        name=name, compiler_params=_params("parallel", "parallel", "arbitrary"),
    )(zf, zf, zf, o32, dy, lse, c_bc, c_row, dz)


def _dil_bias(slope, dil):
    qi = lax.broadcasted_iota(jnp.int32, (BLOCK, 2 * BLOCK), 0)
    kj = lax.broadcasted_iota(jnp.int32, (BLOCK, 2 * BLOCK), 1)
    delta = qi + BLOCK - kj
    return jnp.where((delta >= 0) & (delta <= BLOCK), (-slope * dil) * delta.astype(F32), NEG)


def _alibi_slope(hp, e):
    slope = jnp.float32(0.0)
    for k in range(N_PAIRS):
        slope = jnp.where(hp == k, jnp.float32(2.0 ** -(2 * k + e + 1)), slope)
    return slope


def _first_block_bias(bias):
    return jnp.where(lax.broadcasted_iota(jnp.int32, bias.shape, 1) < BLOCK, NEG, bias)


def _fill_bias(bias_scr, hp):
    for di, dil in enumerate(DILATIONS):
        for e in range(2):
            bias_scr[2 * di + e] = _dil_bias(_alibi_slope(hp, e), dil)


def _pair_specs(rows):
    return [pl.BlockSpec((rows, LANES), lambda b, hp, c0=c0: (b, 3 * hp + c0)) for c0 in range(3)]


def _strided(start, size, dil):
    return pl.ds(start, size) if dil == 1 else pl.ds(start, size, stride=dil)


QUARTER = SEQ // 4


def _to_quarters(src, dst):
    for r in range(4):
        dst[r * QUARTER:(r + 1) * QUARTER, :] = src[pl.ds(r, QUARTER, stride=4), :]


def _from_quarters(src, dst):
    for r in range(4):
        dst[pl.ds(r, QUARTER, stride=4), :] = src[r * QUARTER:(r + 1) * QUARTER, :]


def _mix_weights(l1, l2, l3):
    m = jnp.maximum(jnp.maximum(l1, l2), l3)
    e1, e2, e3 = jnp.exp(l1 - m), jnp.exp(l2 - m), jnp.exp(l3 - m)
    inv = 1.0 / (e1 + e2 + e3)
    return e1 * inv, e2 * inv, e3 * inv


def _dil_fwd(zd, n_batch, name):
    s = SEQ
    t = n_batch * s

    def body(q_ref, k_ref, v_ref, y_ref, l1_ref, l2_ref, l3_ref, o_scr, qkv4, o4, l4, bias_scr):
        _fill_bias(bias_scr, pl.program_id(1))
        for a, ref in enumerate((q_ref, k_ref, v_ref)):
            _to_quarters(ref, qkv4.at[a])

        def unit(srcs, start, first, stride, di, o_dst, l_dst):
            qrows = _strided(start, BLOCK, stride)
            krows = qrows if first else _strided(start - BLOCK * stride, 2 * BLOCK, stride)
            q = (srcs[0][qrows, :] * ATT_SCALE).astype(BF16)
            kc = srcs[1][krows, :].astype(BF16)
            vc = srcs[2][krows, :].astype(BF16)
            if first:
                kc, vc = jnp.concatenate([kc, kc]), jnp.concatenate([vc, vc])
            outs, lses = [], []
            for e in range(2):
                bias = _first_block_bias(bias_scr[2 * di + e]) if first else bias_scr[2 * di + e]
                sc = _dot_nt(jnp.where(_head_mask(e), q, jnp.zeros_like(q)), kc) + bias
                m = jnp.max(sc, axis=1, keepdims=True)
                pe = jnp.exp(sc - m)
                l = jnp.sum(pe, axis=1, keepdims=True)
                outs.append(_dot((pe * (1.0 / l)).astype(BF16), vc))
                lses.append(m + jnp.log(l))
            o_dst[qrows, :] = jnp.where(_head_mask(0), outs[0], outs[1])
            l_dst[qrows, :] = jnp.where(_head_mask(0), lses[0], lses[1])

        for n in range(SEQ // BLOCK):
            unit((q_ref, k_ref, v_ref), n * BLOCK, n == 0, 1, 0, o_scr.at[0], l1_ref)
        quarters = tuple(qkv4.at[a] for a in range(3))
        for di in (1, 2):
            stride = DILATIONS[di] // 4
            for r in range(4):
                for g in range(stride):
                    for n in range(QUARTER // (BLOCK * stride)):
                        unit(quarters, r * QUARTER + n * BLOCK * stride + g, n == 0, stride, di, o4.at[di - 1], l4.at[di - 1])
        for di, l_ref in ((1, l2_ref), (2, l3_ref)):
            _from_quarters(o4.at[di - 1], o_scr.at[di])
            _from_quarters(l4.at[di - 1], l_ref)
        w = _mix_weights(l1_ref[...], l2_ref[...], l3_ref[...])
        y_ref[...] = (w[0] * o_scr[0] + w[1] * o_scr[1] + w[2] * o_scr[2]).astype(BF16)

    blk = pl.BlockSpec((s, LANES), lambda b, hp: (b, hp))
    res = pl.pallas_call(
        body, grid=(n_batch, N_PAIRS),
        in_specs=_pair_specs(s),
        out_specs=[blk] * 4,
        out_shape=[jax.ShapeDtypeStruct((t, MIX_HALF), BF16)] + [jax.ShapeDtypeStruct((t, MIX_HALF), F32)] * 3,
        scratch_shapes=[pltpu.VMEM((3, s, LANES), F32), pltpu.VMEM((3, s, LANES), F32), pltpu.VMEM((2, s, LANES), F32),
                        pltpu.VMEM((2, s, LANES), F32), pltpu.VMEM((6, BLOCK, 2 * BLOCK), F32)],
        name=name, compiler_params=_params("parallel", "arbitrary"),
    )(zd, zd, zd)
    return res[0], res[1:]


def _dil_bwd(zd, dy, ya, lses, n_batch, name):
    s = SEQ
    t = n_batch * s

    def body(q_ref, k_ref, v_ref, dy_ref, ya_ref, l1_ref, l2_ref, l3_ref, dz_ref, w_scr, dy_scr, dot_scr, acc, st4, acc4, bias_scr):
        _fill_bias(bias_scr, pl.program_id(1))
        for di, w in enumerate(_mix_weights(l1_ref[...], l2_ref[...], l3_ref[...])):
            w_scr[di] = w
        dya = dy_ref[...].astype(F32)
        prod = dya * ya_ref[...].astype(F32)
        per_head = [jnp.sum(jnp.where(_head_mask(e), prod, 0.0), axis=1, keepdims=True) for e in range(2)]
        dy_scr[...] = dya
        dot_scr[...] = jnp.where(_head_mask(0), per_head[0], per_head[1])
        acc[...] = jnp.zeros_like(acc)
        acc4[...] = jnp.zeros_like(acc4)
        staged = (q_ref, k_ref, v_ref, w_scr.at[1], w_scr.at[2], l2_ref, l3_ref, dy_scr, dot_scr)
        for a, ref in enumerate(staged):
            _to_quarters(ref, st4.at[a])

        def unit(srcs, dst, start, first, stride, di):
            qrows = _strided(start, BLOCK, stride)
            krows = qrows if first else _strided(start - BLOCK * stride, 2 * BLOCK, stride)
            q = (srcs[0][qrows, :] * ATT_SCALE).astype(BF16)
            kc = srcs[1][krows, :].astype(BF16)
            vc = srcs[2][krows, :].astype(BF16)
            wq = srcs[3][qrows, :]
            lse = srcs[4][qrows, :]
            do = (wq * srcs[5][qrows, :]).astype(BF16)
            sub = wq * srcs[6][qrows, :]
            dq = jnp.zeros((BLOCK, LANES), F32)
            dk = jnp.zeros((krows.size, LANES), F32)
            dv = jnp.zeros((krows.size, LANES), F32)
            for e in range(2):
                mask = _head_mask(e)
                lane0 = HEAD_DIM * e
                qh = jnp.where(mask, q, jnp.zeros_like(q))
                doh = jnp.where(mask, do, jnp.zeros_like(do))
                bias = bias_scr[2 * di + e]
                sc = _dot_nt(qh, kc) + (bias[:, BLOCK:] if first else bias)
                p = jnp.exp(sc - lse[:, lane0:lane0 + 1])
                dsb = (p * (_dot_nt(doh, vc) - sub[:, lane0:lane0 + 1])).astype(BF16)
                dq = dq + _dot(dsb, jnp.where(mask, kc, jnp.zeros_like(kc)))
                dk = dk + _dot_tn(dsb, qh)
                dv = dv + _dot_tn(p.astype(BF16), doh)
            dst.at[0][qrows, :] += dq * ATT_SCALE
            dst.at[1][krows, :] += dk
            dst.at[2][krows, :] += dv

        token_order = (q_ref, k_ref, v_ref, w_scr.at[0], l1_ref, dy_scr, dot_scr)
        for n in range(SEQ // BLOCK):
            unit(token_order, acc, n * BLOCK, n == 0, 1, 0)
        for di in (1, 2):
            quarters = (st4.at[0], st4.at[1], st4.at[2], st4.at[2 + di], st4.at[4 + di], st4.at[7], st4.at[8])
            stride = DILATIONS[di] // 4
            for r in range(4):
                for g in range(stride):
                    for n in range(QUARTER // (BLOCK * stride)):
                        unit(quarters, acc4, r * QUARTER + n * BLOCK * stride + g, n == 0, stride, di)
        for k in range(3):
            for r in range(4):
                acc.at[k][pl.ds(r, QUARTER, stride=4), :] += acc4[k, r * QUARTER:(r + 1) * QUARTER, :]
            dz_ref[:, k * LANES:(k + 1) * LANES] = acc[k].astype(BF16)

    blk = pl.BlockSpec((s, LANES), lambda b, hp: (b, hp))
    pair = pl.BlockSpec((s, PAIR_WIDTH), lambda b, hp: (b, hp))
    return pl.pallas_call(
        body, grid=(n_batch, N_PAIRS),
        in_specs=_pair_specs(s) + [blk] * 5,
        out_specs=pair,
        out_shape=jax.ShapeDtypeStruct((t, 2 * 3 * MIX_HALF), BF16),
        scratch_shapes=[pltpu.VMEM((3, s, LANES), F32), pltpu.VMEM((s, LANES), F32), pltpu.VMEM((s, LANES), F32),
                        pltpu.VMEM((3, s, LANES), F32), pltpu.VMEM((9, s, LANES), F32), pltpu.VMEM((3, s, LANES), F32),
                        pltpu.VMEM((6, BLOCK, 2 * BLOCK), F32)],
        name=name, compiler_params=_params("parallel", "arbitrary"),
    )(zd, zd, zd, dy, ya, *lses)


X_BQ = 2048


def _xattn_probs(q, k):
    sc = _dot_nt(q, k) * X_SCALE
    pe = jnp.exp(sc - jnp.max(sc, axis=1, keepdims=True))
    return pe / jnp.sum(pe, axis=1, keepdims=True)


def _xattn_fwd(qx, kx, vx, n_batch, name):
    nq = SEQ // X_BQ

    def body(q_ref, k_ref, v_ref, o_ref):
        p = _xattn_probs(q_ref[...], k_ref[...])
        o_ref[...] = _dot(p.astype(BF16), v_ref[...]).astype(BF16)

    qblk = pl.BlockSpec((X_BQ, X_HEAD_DIM), lambda b, h, i: (b * nq + i, h))
    kblk = pl.BlockSpec((N_MEM, X_HEAD_DIM), lambda b, h, i: (b, h))
    return pl.pallas_call(
        body, grid=(n_batch, X_HEADS, nq), in_specs=[qblk, kblk, kblk], out_specs=qblk,
        out_shape=jax.ShapeDtypeStruct(qx.shape, BF16),
        name=name, compiler_params=_params("parallel", "parallel", "arbitrary"),
    )(qx, kx, vx)


def _xattn_bwd(qx, kx, vx, dox, n_batch, name):
    nq = SEQ // X_BQ

    def body(q_ref, k_ref, v_ref, do_ref, dq_ref, dk_ref, dv_ref, dk_acc, dv_acc):
        i = pl.program_id(2)

        @pl.when(i == 0)
        def _():
            dk_acc[...] = jnp.zeros_like(dk_acc)
            dv_acc[...] = jnp.zeros_like(dv_acc)

        q, k, do = q_ref[...], k_ref[...], do_ref[...]
        p = _xattn_probs(q, k)
        dp = _dot_nt(do, v_ref[...])
        dsb = (p * (dp - jnp.sum(p * dp, axis=1, keepdims=True))).astype(BF16)
        dq_ref[...] = (_dot(dsb, k) * X_SCALE).astype(BF16)
        dk_acc[...] += _dot_tn(dsb, q) * X_SCALE
        dv_acc[...] += _dot_tn(p.astype(BF16), do)

        @pl.when(i == nq - 1)
        def _():
            dk_ref[...] = dk_acc[...].astype(BF16)
            dv_ref[...] = dv_acc[...].astype(BF16)

    qblk = pl.BlockSpec((X_BQ, X_HEAD_DIM), lambda b, h, i: (b * nq + i, h))
    kblk = pl.BlockSpec((N_MEM, X_HEAD_DIM), lambda b, h, i: (b, h))
    return pl.pallas_call(
        body, grid=(n_batch, X_HEADS, nq), in_specs=[qblk, kblk, kblk, qblk], out_specs=[qblk, kblk, kblk],
        out_shape=[jax.ShapeDtypeStruct(qx.shape, BF16), jax.ShapeDtypeStruct(kx.shape, BF16), jax.ShapeDtypeStruct(kx.shape, BF16)],
        scratch_shapes=[pltpu.VMEM((N_MEM, X_HEAD_DIM), F32)] * 2,
        name=name, compiler_params=_params("parallel", "parallel", "arbitrary"),
    )(qx, kx, vx, dox)


def _adamw(w, g, m, v, name, rows):
    r, c = w.shape
    assert r % rows == 0, (name, w.shape, rows)

    def body(w_ref, g_ref, m_ref, v_ref, d_ref, nm_ref, nv_ref):
        gv = g_ref[...]
        m1 = ADAM_B1 * m_ref[...] + (1.0 - ADAM_B1) * gv
        v1 = ADAM_B2 * v_ref[...] + (1.0 - ADAM_B2) * jnp.square(gv)
        m_hat = m1 / (1.0 - ADAM_B1 ** ADAM_STEP)
        v_hat = v1 / (1.0 - ADAM_B2 ** ADAM_STEP)
        d_ref[...] = -ADAM_LR * (m_hat / (jnp.sqrt(v_hat) + ADAM_EPS) + ADAM_WD * w_ref[...])
        nm_ref[...] = m1
        nv_ref[...] = v1

    blk = pl.BlockSpec((rows, c), lambda i: (i, 0))
    return pl.pallas_call(
        body, grid=(r // rows,), in_specs=[blk] * 4, out_specs=[blk] * 3,
        out_shape=[jax.ShapeDtypeStruct((r, c), F32)] * 3,
        name=name, compiler_params=_params("arbitrary"),
    )(w, g, m, v)


def _relu2(acc):
    a = jnp.maximum(acc, 0.0)
    return acc, a * a


def _relu2_bwd(acc, u):
    return (2.0 * jnp.maximum(u.astype(F32), 0.0) * acc,)


def _local_step(x, mem, target, vecs, w_in, late_weights, hooks=None):
    n_batch = x.shape[0]
    t = n_batch * SEQ
    x0 = x.reshape(t, D_MODEL)
    mem2 = mem.reshape(n_batch * N_MEM, D_MODEL)
    tgt = target.reshape(t, D_MODEL)

    half = 3 * MIX_HALF
    w_qkv = jnp.concatenate([_pair_major(w_in[:, :half]), _pair_major(w_in[:, half:QKV_WIDTH])], axis=1)
    w_gate = jnp.pad(w_in[:, QKV_WIDTH:], ((0, 0), (0, GATE_PAD - N_HEADS)))
    b_pad = jnp.pad(vecs["b_forget"], (0, GATE_PAD - N_HEADS)).reshape(1, GATE_PAD)

    h1, zd, zf, gate = _in_proj(x0, vecs["g_mix"], jnp.concatenate([w_qkv, w_gate], axis=1), "in_proj")
    mn = _rmsnorm(mem2, vecs["g_mem"], "norm_mem")
    c_bc, c_row, sg = _gate_fwd(gate, b_pad, n_batch, "gate_fwd")
    ya, lses = _dil_fwd(zd, n_batch, "dil_fwd")
    yf, of32, lse_f = _fox_fwd(zf, c_bc, c_row, n_batch, "fox_fwd")
    wts = late_weights(yf)
    w_out = wts["w_out"]
    x1, h2 = _matmul_res_norm([ya, yf], [w_out[:MIX_HALF], w_out[MIX_HALF:]], x0, vecs["g_xattn"], "out")
    qx = _matmul(h2, wts["w_xq"], "xq", out_dtypes=(BF16,))[0]
    kx = _matmul(mn, wts["w_xk"], "xk", out_dtypes=(BF16,))[0]
    vx = _matmul(mn, wts["w_xv"], "xv", out_dtypes=(BF16,))[0]
    ox = _xattn_fwd(qx, kx, vx, n_batch, "xattn_fwd")
    x2, h3 = _matmul_res_norm([ox], [wts["w_xo"]], x1, vecs["g_mlp"], "xo")
    u, a2 = _matmul(h3, wts["w_up"], "mlp_up", out_dtypes=(BF16, BF16), epilogue=_relu2, tn=1024)
    loss, dx3, dx3b, dg_final = _loss_bwd(a2, wts["w_down"], x2, vecs["g_final"], tgt, "mlp_down_loss")

    du = _matmul(dx3b, wts["w_down"], "mlp_down_bwd", out_dtypes=(BF16,), extras=(u,), epilogue=_relu2_bwd, tn=1024, w_t=True)[0]
    shards = (N_CHIPS, 2 * D_MODEL, D_MODEL)
    g_mlp = _matmul_tn(h3, du, "gw_up", packed=(shards, lambda i, j: (j, 0, 0), None))
    g_mlp = _matmul_tn(a2, dx3b, "gw_down", packed=(shards, lambda i, j: (i, 1, 0), g_mlp))
    gw_up = g_mlp[:, :D_MODEL].transpose(1, 0, 2).reshape(D_MODEL, D_FF)
    gw_down = g_mlp[:, D_MODEL:].reshape(D_FF, D_MODEL)
    on_grads, on_swapped = hooks or (None, None)
    token = on_grads("mlp", g_mlp) if hooks else None
    dx2, dx2b, dg_mlp = _matmul_rms_bwd([du], [wts["w_up"]], x2, vecs["g_mlp"], dx3, "mlp_up_bwd", after=token)

    gw_xo = _matmul_tn(ox, dx2b, "gw_xo")
    token = on_swapped("mlp", dx2b) if hooks else None
    dox = _matmul(dx2b, wts["w_xo"], "xo_bwd", out_dtypes=(BF16,), w_t=True, after=token)[0]
    dqx, dkx, dvx = _xattn_bwd(qx, kx, vx, dox, n_batch, "xattn_bwd")
    gw_xq = _matmul_tn(h2, dqx, "gw_xq")
    gw_xk = _matmul_tn(mn, dkx, "gw_xk")
    gw_xv = _matmul_tn(mn, dvx, "gw_xv")
    dmn = _matmul(dkx, wts["w_xk"], "xk_bwd", w_t=True)[0]
    dmn = _matmul_res(dvx, wts["w_xv"], dmn, "xv_bwd", w_t=True)
    _, _, dg_mem = _rms_bwd(mem2, dmn, vecs["g_mem"], None, "norm_mem_bwd")
    dx1, dx1b, dg_xattn = _matmul_rms_bwd([dqx], [wts["w_xq"]], x1, vecs["g_xattn"], dx2, "xq_bwd")

    gw_out = jnp.concatenate([_matmul_tn(ya, dx1b, "gw_out_a"), _matmul_tn(yf, dx1b, "gw_out_f")], axis=0)
    token = on_grads("mid", dict(w_out=gw_out, w_xq=gw_xq, w_xk=gw_xk, w_xv=gw_xv, w_xo=gw_xo)) if hooks else None
    dy = _matmul(dx1b, w_out, "out_bwd", out_dtypes=(BF16,), w_t=True, after=token)[0]
    dz = _dil_bwd(zd, dy, ya, lses, n_batch, "dil_bwd")
    dz, dc = _fox_bwd(zf, of32, dy, lse_f, c_bc, c_row, dz, n_batch, "fox_bwd")
    dzg, db = _gate_bwd(dc.reshape(n_batch, N_HEADS, SEQ), sg, "gate_bwd")
    token = on_swapped("mid", dz) if hooks else None
    gw_pm = _matmul_tn(h1, dz, "gw_in_qkv", after=token)
    gw_in = jnp.concatenate([_pair_major_inv(gw_pm[:, :half]), _pair_major_inv(gw_pm[:, half:]),
                             _matmul_tn(h1, dzg, "gw_in_gate")[:, :N_HEADS]], axis=1)
    dx0, _, dg_mix = _matmul_rms_bwd([dz, dzg], [w_qkv, w_gate], x0, vecs["g_mix"], dx1, "in_bwd")

    gw = dict(w_in=gw_in, w_out=gw_out, w_xq=gw_xq, w_xk=gw_xk, w_xv=gw_xv, w_xo=gw_xo, w_up=gw_up, w_down=gw_down)
    gv = dict(g_mix=dg_mix, g_xattn=dg_xattn, g_mem=dg_mem, g_mlp=dg_mlp, g_final=dg_final, b_forget=db)
    return loss, dx0.reshape(x.shape), gw, gv


MESH = pl.DeviceIdType.MESH
ANY = pl.BlockSpec(memory_space=pl.ANY)


def _place():
    x, y, c = lax.axis_index("x"), lax.axis_index("y"), lax.axis_index("c")
    other_chips = [(1 - x, y), (x, 1 - y), (1 - x, 1 - y)]
    return x, y, c, other_chips


def _my_chip():
    return 2 * lax.axis_index("x") + lax.axis_index("y")


def _halves(rows, c, align):
    half = rows // 2
    assert rows % (2 * align) == 0, rows
    return pl.ds(pl.multiple_of(c * half, align), half), pl.ds(pl.multiple_of((1 - c) * half, align), half)


def _place_own(wall, pack):
    return lax.dynamic_update_slice(wall, pack[None], (_my_chip(), 0, 0))


def _gather(pack, name, after):
    def body(p_ref, after_ref, out_ref, send_sems, recv_sems, pass_send, pass_recv):
        del after_ref
        x, y, c, chips = _place()
        me = 2 * x + y
        mine, theirs = _halves(pack.shape[0], c, 16)

        def from_chip(k, chip, rows):
            src = out_ref.at[2 * chip[0] + chip[1], rows]
            return pltpu.make_async_remote_copy(src_ref=src, dst_ref=src, send_sem=send_sems.at[k], recv_sem=recv_sems.at[k],
                                                device_id=(chip[0], chip[1], c), device_id_type=MESH)

        def passed(k, chip, rows):
            src = out_ref.at[2 * chip[0] + chip[1], rows]
            return pltpu.make_async_remote_copy(src_ref=src, dst_ref=src, send_sem=pass_send.at[k], recv_sem=pass_recv.at[k],
                                                device_id=(x, y, 1 - c), device_id_type=MESH)

        sends = []
        for k, chip in enumerate(chips):
            cp = pltpu.make_async_remote_copy(src_ref=p_ref.at[mine], dst_ref=out_ref.at[me, mine], send_sem=send_sems.at[k],
                                              recv_sem=recv_sems.at[k], device_id=(chip[0], chip[1], c), device_id_type=MESH)
            cp.start()
            sends.append(cp)
        for k, chip in enumerate(chips):
            from_chip(k, chip, mine).wait_recv()
            cp = passed(k, chip, mine)
            cp.start()
            sends.append(cp)
        for k, chip in enumerate(chips):
            passed(k, chip, theirs).wait_recv()
        for cp in sends:
            cp.wait_send()

    wall = pl.pallas_call(
        body, in_specs=[ANY, ANY], out_specs=ANY,
        out_shape=jax.ShapeDtypeStruct((N_CHIPS,) + pack.shape, pack.dtype),
        scratch_shapes=[pltpu.SemaphoreType.DMA((3,))] * 4,
        name=name,
    )(pack, after)
    return _place_own(wall, pack)


HBM = pl.BlockSpec(memory_space=pltpu.HBM)
SEM = pl.BlockSpec(memory_space=pltpu.SEMAPHORE)
SPLIT_COPY = pltpu.CompilerParams(has_side_effects=pltpu.SideEffectType.DATAFLOW_SIDE_EFFECTING)


def _in_hbm(a):
    return pltpu.with_memory_space_constraint(a, pltpu.HBM)


def _start_call(start, src, land_shape, after, name):
    land = lax.empty(land_shape, src.dtype)

    def body(src_ref, land_ref, after_ref, send_sems, recv_sems, src_thru, land_thru, token):
        del after_ref, src_thru, land_thru
        start(src_ref, land_ref, send_sems, recv_sems)
        token[...] = jnp.zeros_like(token)

    return pl.pallas_call(
        body, name=name,
        out_shape=(pltpu.SemaphoreType.DMA((3,)), pltpu.SemaphoreType.DMA((3,)), pltpu.HBM(src.shape, src.dtype),
                   pltpu.HBM(land_shape, src.dtype), jax.ShapeDtypeStruct((8, LANES), F32)),
        in_specs=(HBM, HBM, ANY), out_specs=(SEM, SEM, HBM, HBM, pl.BlockSpec(memory_space=pltpu.VMEM)),
        input_output_aliases={0: 2, 1: 3}, compiler_params=SPLIT_COPY,
    )(_in_hbm(src), _in_hbm(land), after)


def _wait_call(body, started, after, name):
    send_sems, recv_sems, src, land, _ = started
    return pl.pallas_call(
        body, name=name,
        out_shape=(pltpu.HBM(src.shape, src.dtype), pltpu.HBM(land.shape, land.dtype)),
        in_specs=(HBM, HBM, SEM, SEM, ANY), out_specs=(HBM, HBM),
        input_output_aliases={0: 0, 1: 1}, compiler_params=SPLIT_COPY,
    )(src, land, send_sems, recv_sems, after)


def _gather_copies(p_ref, wall_ref, send_sems, recv_sems):
    x, y, c, chips = _place()
    me = 2 * x + y
    mine, _ = _halves(p_ref.shape[0], c, 16)
    out, back = [], []
    for k, chip in enumerate(chips):
        peer = dict(send_sem=send_sems.at[k], recv_sem=recv_sems.at[k], device_id=(chip[0], chip[1], c), device_id_type=MESH)
        out.append(pltpu.make_async_remote_copy(src_ref=p_ref.at[mine], dst_ref=wall_ref.at[me, mine], **peer))
        slab = wall_ref.at[2 * chip[0] + chip[1], mine]
        back.append(pltpu.make_async_remote_copy(src_ref=slab, dst_ref=slab, **peer))
    return out, back


def _gather_start(pack, after, name):
    def start(p_ref, wall_ref, send_sems, recv_sems):
        for cp in _gather_copies(p_ref, wall_ref, send_sems, recv_sems)[0]:
            cp.start()

    return _start_call(start, pack, (N_CHIPS,) + pack.shape, after, name)


def _gather_wait(started, after, name):
    def body(p_ref, wall_ref, send_sems, recv_sems, after_ref, p_dead, wall_out):
        del after_ref, p_dead, wall_out
        out, back = _gather_copies(p_ref, wall_ref, send_sems, recv_sems)
        for cp_out, cp_back in zip(out, back):
            cp_out.wait_send()
            cp_back.wait_recv()

    return _wait_call(body, started, after, name)


def _pass_on(wall, name):
    def body(w_in_ref, out_ref, send_sems, recv_sems):
        del w_in_ref
        x, y, c, chips = _place()
        mine, theirs = _halves(wall.shape[1], c, 16)
        sends = []
        for k, chip in enumerate(chips):
            slab = out_ref.at[2 * chip[0] + chip[1]]
            peer = dict(send_sem=send_sems.at[k], recv_sem=recv_sems.at[k], device_id=(x, y, 1 - c), device_id_type=MESH)
            cp = pltpu.make_async_remote_copy(src_ref=slab.at[mine], dst_ref=slab.at[mine], **peer)
            cp.start()
            sends.append((cp, pltpu.make_async_remote_copy(src_ref=slab.at[theirs], dst_ref=slab.at[theirs], **peer)))
        for cp, back in sends:
            back.wait_recv()
            cp.wait_send()

    return pl.pallas_call(
        body, in_specs=[ANY], out_specs=ANY, out_shape=jax.ShapeDtypeStruct(wall.shape, wall.dtype),
        scratch_shapes=[pltpu.SemaphoreType.DMA((3,))] * 2, input_output_aliases={0: 0}, name=name,
    )(wall)


def _swap_halves(g, name):
    half = g.shape[1] // 2

    def body(g_ref, out_ref, send_sem, recv_sem):
        x, y, c, _ = _place()
        _, theirs = _halves(g.shape[1], c, 8)
        cp = pltpu.make_async_remote_copy(src_ref=g_ref.at[:, theirs], dst_ref=out_ref, send_sem=send_sem, recv_sem=recv_sem,
                                          device_id=(x, y, 1 - c), device_id_type=MESH)
        cp.start()
        cp.wait()

    return pl.pallas_call(
        body, in_specs=[ANY], out_specs=ANY,
        out_shape=jax.ShapeDtypeStruct((N_CHIPS, half, g.shape[2]), F32),
        scratch_shapes=[pltpu.SemaphoreType.DMA, pltpu.SemaphoreType.DMA],
        name=name,
    )(g)


def _swap_copy(g_ref, land_ref, send_sems, recv_sems):
    x, y, c, _ = _place()
    _, theirs = _halves(g_ref.shape[1], c, 8)
    return pltpu.make_async_remote_copy(src_ref=g_ref.at[:, theirs], dst_ref=land_ref, send_sem=send_sems.at[0],
                                        recv_sem=recv_sems.at[0], device_id=(x, y, 1 - c), device_id_type=MESH)


def _swap_start(g, name):
    def start(g_ref, land_ref, send_sems, recv_sems):
        _swap_copy(g_ref, land_ref, send_sems, recv_sems).start()

    return _start_call(start, g, (N_CHIPS, g.shape[1] // 2, g.shape[2]), _core_index(), name)


def _swap_wait(started, after, name):
    def body(g_ref, land_ref, send_sems, recv_sems, after_ref, g_out, land_out):
        del after_ref, g_out, land_out
        cp = _swap_copy(g_ref, land_ref, send_sems, recv_sems)
        cp.wait_send()
        cp.wait_recv()

    return _wait_call(body, started, after, name)


def _core_index():
    return lax.axis_index("c").astype(jnp.int32).reshape(1)


def _row_tile(half):
    tile = max(t for t in range(16, 1025, 16) if half % t == 0)
    return tile, half // tile


def _add_sibling(g, got, name):
    half = g.shape[1] // 2
    tile, n_tiles = _row_tile(half)

    def body(c_ref, g_ref, got_ref, o_ref):
        o_ref[...] = (g_ref[...] + got_ref[...]).astype(BF16)

    width = g.shape[2]
    blk = pl.BlockSpec((None, tile, width), lambda s, i, c_ref: (s, i, 0))
    return pl.pallas_call(
        body,
        grid_spec=pltpu.PrefetchScalarGridSpec(
            num_scalar_prefetch=1, grid=(N_CHIPS, n_tiles),
            in_specs=[pl.BlockSpec((None, tile, width), lambda s, i, c_ref: (s, c_ref[0] * n_tiles + i, 0)), blk],
            out_specs=blk),
        out_shape=jax.ShapeDtypeStruct((N_CHIPS, half, width), BF16),
        name=name, compiler_params=_params("arbitrary", "arbitrary"),
    )(_core_index(), g, got)


def _exchange_copies(p_ref, land_ref, send_sems, recv_sems):
    x, y, c, chips = _place()
    me = 2 * x + y
    out, back = [], []
    for k, chip in enumerate(chips):
        peer = dict(send_sem=send_sems.at[k], recv_sem=recv_sems.at[k], device_id=(chip[0], chip[1], c), device_id_type=MESH)
        out.append(pltpu.make_async_remote_copy(src_ref=p_ref.at[2 * chip[0] + chip[1]], dst_ref=land_ref.at[me], **peer))
        slab = land_ref.at[2 * chip[0] + chip[1]]
        back.append(pltpu.make_async_remote_copy(src_ref=slab, dst_ref=slab, **peer))
    return out, back


def _with_own(got, part):
    me = _my_chip()
    return lax.dynamic_update_slice(got, lax.dynamic_slice(part, (me, 0, 0), (1,) + part.shape[1:]), (me, 0, 0))


def _exchange_start(part, name):
    def start(p_ref, land_ref, send_sems, recv_sems):
        for cp in _exchange_copies(p_ref, land_ref, send_sems, recv_sems)[0]:
            cp.start()

    return _start_call(start, part, part.shape, _core_index(), name)


def _exchange_wait(started, after, name):
    def body(p_ref, land_ref, send_sems, recv_sems, after_ref, p_dead, land_out):
        del after_ref, p_dead, land_out
        out, back = _exchange_copies(p_ref, land_ref, send_sems, recv_sems)
        for cp_out, cp_back in zip(out, back):
            cp_out.wait_send()
            cp_back.wait_recv()

    part, got = _wait_call(body, started, after, name)
    return _with_own(got, part)


def _sum_chips(parts, name):
    half, width = parts.shape[1:]
    tile, n_tiles = _row_tile(half)

    def body(c_ref, p0, p1, p2, p3, o_ref):
        f32 = lambda p: p[...].astype(F32)
        o_ref[...] = ((f32(p0) + f32(p1)) + f32(p2)) + f32(p3)

    def slab(s):
        return pl.BlockSpec((None, tile, width), lambda i, c_ref, s=s: (s, i, 0))

    return pl.pallas_call(
        body,
        grid_spec=pltpu.PrefetchScalarGridSpec(
            num_scalar_prefetch=1, grid=(n_tiles,),
            in_specs=[slab(s) for s in range(N_CHIPS)],
            out_specs=pl.BlockSpec((None, tile, width), lambda i, c_ref: (c_ref[0], i, 0))),
        out_shape=jax.ShapeDtypeStruct((2, half, width), F32),
        name=name, compiler_params=_params("arbitrary"),
    )(_core_index(), parts, parts, parts, parts)


def _share_halves(halves, name):
    def body(h_ref, out_ref, send_sem, recv_sem):
        del h_ref
        x, y, c, _ = _place()
        cp = pltpu.make_async_remote_copy(src_ref=out_ref.at[c], dst_ref=out_ref.at[c], send_sem=send_sem, recv_sem=recv_sem,
                                          device_id=(x, y, 1 - c), device_id_type=MESH)
        cp.start()
        pltpu.make_async_remote_copy(src_ref=out_ref.at[1 - c], dst_ref=out_ref.at[1 - c], send_sem=send_sem, recv_sem=recv_sem,
                                     device_id=(x, y, 1 - c), device_id_type=MESH).wait_recv()
        cp.wait_send()

    return pl.pallas_call(
        body, in_specs=[ANY], out_specs=ANY,
        out_shape=jax.ShapeDtypeStruct(halves.shape, halves.dtype),
        scratch_shapes=[pltpu.SemaphoreType.DMA] * 2,
        input_output_aliases={0: 0},
        name=name,
    )(halves)


def _reduce_parts(g, tag):
    return _add_sibling(g, _swap_halves(g, "swap_" + tag), "add_" + tag)


def _reduce_finish(got, tag):
    halves = _share_halves(_sum_chips(got, "sum_" + tag), "share_" + tag)
    return halves.reshape(2 * halves.shape[1], halves.shape[2])


SMALL_ROWS = 8


def _allreduce_small(v):
    def body(v_ref, out_ref, buf, send_sems, recv_sems):
        x, y, c, _ = _place()
        buf[4 * x + 2 * y + c] = v_ref[...]
        sends = []
        for k in range(1, N_DEV):
            px = 1 - x if k & 4 else x
            py = 1 - y if k & 2 else y
            pc = 1 - c if k & 1 else c
            cp = pltpu.make_async_remote_copy(src_ref=v_ref, dst_ref=buf.at[4 * x + 2 * y + c], send_sem=send_sems.at[k - 1],
                                              recv_sem=recv_sems.at[k - 1], device_id=(px, py, pc), device_id_type=MESH)
            cp.start()
            sends.append((cp, 4 * px + 2 * py + pc))
        for k, (cp, peer) in enumerate(sends):
            pltpu.make_async_remote_copy(src_ref=v_ref, dst_ref=buf.at[peer], send_sem=send_sems.at[k], recv_sem=recv_sems.at[k],
                                         device_id=(x, y, c), device_id_type=MESH).wait_recv()
        for cp, _ in sends:
            cp.wait_send()
        total = buf[0]
        for d in range(1, N_DEV):
            total = total + buf[d]
        out_ref[...] = total

    vmem = pl.BlockSpec(memory_space=pltpu.VMEM)
    return pl.pallas_call(
        body, in_specs=[vmem], out_specs=vmem,
        out_shape=jax.ShapeDtypeStruct(v.shape, v.dtype),
        scratch_shapes=[pltpu.VMEM((N_DEV,) + v.shape, v.dtype), pltpu.SemaphoreType.DMA((N_DEV - 1,)),
                        pltpu.SemaphoreType.DMA((N_DEV - 1,))],
        name="allreduce_small",
    )(v)


MATRICES = ("w_in", "w_out", "w_xq", "w_xk", "w_xv", "w_xo", "w_up", "w_down")
VECTORS = ("g_mix", "g_xattn", "g_mem", "g_mlp", "g_final", "b_forget")
WEIGHT_ORDER = ("g_mix", "w_in", "b_forget", "w_out", "g_xattn", "g_mem", "w_xq", "w_xk", "w_xv", "w_xo",
                "g_mlp", "w_up", "w_down", "g_final")
GROUPS = {"mlp": ("w_up", "w_down"), "mid": ("w_out", "w_xq", "w_xk", "w_xv", "w_xo"), "in": ("w_in",)}
LATE = GROUPS["mid"] + GROUPS["mlp"]
W_IN_SHARD = IN_WIDTH // N_CHIPS
SHARD_ROWS = {"w_out": 256, "w_xq": 256, "w_xk": 256, "w_xv": 256, "w_xo": 256, "w_up": 1024, "w_down": 1024}
PACK_ROWS = SHARD_ROWS
W_IN_PAD = -(-W_IN_SHARD // LANES) * LANES
ADAM_ROWS = 128


def _pack(parts, names):
    return jnp.concatenate([jnp.pad(parts[n], ((0, PACK_ROWS[n] - SHARD_ROWS[n]), (0, 0))) for n in names], axis=0)


def _unpack(a, names):
    out, pos = {}, 0
    for n in names:
        out[n] = a[..., pos:pos + SHARD_ROWS[n], :]
        pos += PACK_ROWS[n]
    return out


def _full_weights(wall, names):
    cols = lambda a: a.transpose(1, 0, 2).reshape(a.shape[1], -1)
    rows = lambda a: a.reshape(-1, a.shape[-1])
    if names == GROUPS["in"]:
        return {"w_in": cols(wall[:, :, :W_IN_SHARD])}
    return {n: cols(a) if n == "w_up" else rows(a) for n, a in _unpack(wall, names).items()}


def _shard_of(g, name, s):
    if name == "w_up":
        return g[:, s * D_MODEL:(s + 1) * D_MODEL]
    n = SHARD_ROWS[name]
    return g[s * n:(s + 1) * n]


def _pad_w_in(a):
    return jnp.pad(a, [(0, 0)] * (a.ndim - 1) + [(0, W_IN_PAD - W_IN_SHARD)])


def _pack_grads(gws, names):
    if names == GROUPS["in"]:
        return _pad_w_in(gws["w_in"].reshape(D_MODEL, N_CHIPS, W_IN_SHARD).transpose(1, 0, 2))
    return jnp.stack([_pack({n: _shard_of(gws[n], n, s) for n in names}, names) for s in range(N_CHIPS)])


def kernel(x, mem, g_mix, w_in, b_forget, w_out, g_xattn, g_mem, w_xq, w_xk, w_xv, w_xo, g_mlp, w_up, w_down, g_final, loss_target, m_g_mix, m_w_in, m_b_forget, m_w_out, m_g_xattn, m_g_mem, m_w_xq, m_w_xk, m_w_xv, m_w_xo, m_g_mlp, m_w_up, m_w_down, m_g_final, v_g_mix, v_w_in, v_b_forget, v_w_out, v_g_xattn, v_g_mem, v_w_xq, v_w_xk, v_w_xv, v_w_xo, v_g_mlp, v_w_up, v_w_down, v_g_final):
    given = dict(locals())
    weights = {n: given[n] for n in WEIGHT_ORDER}
    vecs = {n: weights[n] for n in VECTORS}

    shard = {n: weights[n].astype(BF16) for n in MATRICES}
    in_pack, late_pack = _pad_w_in(shard["w_in"]), _pack(shard, LATE)
    in_wall = _gather(in_pack, "gather_in", in_pack)
    late = _gather_start(late_pack, in_wall, "gather_late_start")
    w_in_full = _full_weights(in_wall, GROUPS["in"])["w_in"]

    def late_weights(after):
        pack, wall = _gather_wait(late, after, "gather_late_wait")
        return _full_weights(_place_own(_pass_on(wall, "gather_late_pass"), pack), LATE)

    started = {}

    swapping = {}

    def on_grads(group, gws):
        packed = gws if group == "mlp" else _pack_grads(gws, GROUPS[group])
        swapping[group] = _swap_start(packed, "swap_%s_start" % group)
        return swapping[group][4]

    def on_swapped(group, after):
        g, got = _swap_wait(swapping[group], after, "swap_%s_wait" % group)
        started[group] = _exchange_start(_add_sibling(g, got, "add_" + group), "exchange_%s_start" % group)
        return started[group][4]

    loss, grad_x, gw, gv = _local_step(x, mem, loss_target, vecs, w_in_full, late_weights, (on_grads, on_swapped))

    part = _reduce_parts(_pack_grads(gw, GROUPS["in"]), "in")
    started["in"] = _exchange_start(part, "exchange_in_start")
    grads, delta, new_m, new_v = {}, {}, {}, {}

    def finish(group, after):
        got = _exchange_wait(started[group], after, "exchange_%s_wait" % group)
        done = _reduce_finish(got, group)
        for n, a in ({"w_in": done[:, :W_IN_SHARD]} if group == "in" else _unpack(done, GROUPS[group])).items():
            grads[n] = a.reshape(weights[n].shape)
            delta[n], new_m[n], new_v[n] = _adamw(weights[n], grads[n], given["m_" + n], given["v_" + n], "adamw_" + n, ADAM_ROWS)
        return new_v[GROUPS[group][-1]]

    after = finish("mlp", started["in"][4])
    after = finish("mid", after)

    row = lambda a: jnp.pad(a.reshape(-1), (0, D_MODEL - a.size)).reshape(1, D_MODEL)
    small = jnp.concatenate([gv[n] for n in VECTORS[:5]] + [row(gv["b_forget"][:, 0]), row(loss[0, :1]),
                             jnp.zeros((1, D_MODEL), F32)], axis=0)
    small = _allreduce_small(small)
    for k, n in enumerate(VECTORS[:5]):
        grads[n] = small[k]
    grads["b_forget"] = small[5, :N_HEADS]
    loss_total = small[6, 0]
    finish("in", after)

    stack = lambda prefix: jnp.concatenate([row(given[prefix + n]) for n in VECTORS] + [jnp.zeros((2, D_MODEL), F32)], axis=0)
    g_small = jnp.concatenate([small[:6], jnp.zeros((2, D_MODEL), F32)], axis=0)
    d, m1, v1 = _adamw(stack(""), g_small, stack("m_"), stack("v_"), "adamw_vectors", SMALL_ROWS)
    for k, n in enumerate(VECTORS):
        width = weights[n].shape[0]
        delta[n], new_m[n], new_v[n] = d[k, :width], m1[k, :width], v1[k, :width]

    return (loss_total, grad_x, *[grads[n] for n in WEIGHT_ORDER], *[delta[n] for n in WEIGHT_ORDER],
            *[new_m[n] for n in WEIGHT_ORDER], *[new_v[n] for n in WEIGHT_ORDER])
```

```python
import functools
import math

import jax
import jax.numpy as jnp
from jax import lax
from jax.experimental import pallas as pl
from jax.experimental.pallas import tpu as pltpu

F32 = jnp.float32
BF16 = jnp.bfloat16

D_MODEL = 1024
SEQ = 2048
N_MEM = 256
HEAD_DIM = 64
N_HEADS = 8
MIX_HALF = N_HEADS * HEAD_DIM
QKV_WIDTH = 6 * MIX_HALF
IN_WIDTH = QKV_WIDTH + N_HEADS
GATE_PAD = 128
BLOCK = 128
DILATIONS = (1, 4, 16)
X_HEADS = 4
X_HEAD_DIM = 256
D_FF = 4096
EPS = 1e-6
NEG = -1e30
ATT_SCALE = 1.0 / math.sqrt(HEAD_DIM)
X_SCALE = 1.0 / math.sqrt(X_HEAD_DIM)
LANES = 128
N_CHIPS = 4
N_DEV = 8

ADAM_LR = 0.001
ADAM_B1 = 0.9
ADAM_B2 = 0.999
ADAM_EPS = 1e-08
ADAM_WD = 0.01
ADAM_STEP = 10

VMEM_LIMIT = 48 * 1024 * 1024


def _params(*sem):
    return pltpu.CompilerParams(dimension_semantics=sem or None, vmem_limit_bytes=VMEM_LIMIT)


def _dot(a, b):
    return jnp.dot(a, b, preferred_element_type=F32)


def _dot_nt(a, b):
    return lax.dot_general(a, b, (((1,), (1,)), ((), ())), preferred_element_type=F32)


def _dot_tn(a, b):
    return lax.dot_general(a, b, (((0,), (0,)), ((), ())), preferred_element_type=F32)


def _dot_exact(x, e):
    hi = x.astype(BF16)
    r1 = x - hi.astype(F32)
    mid = r1.astype(BF16)
    lo = (r1 - mid.astype(F32)).astype(BF16)
    return _dot(hi, e) + _dot(mid, e) + _dot(lo, e)


def _head_mask(e):
    lane = lax.broadcasted_iota(jnp.int32, (1, LANES), 1)
    return (lane >= HEAD_DIM * e) & (lane < HEAD_DIM * (e + 1))


def _matmul(a, w, name, out_dtypes=(F32,), extras=(), epilogue=None, tm=1024, tn=512, w_t=False, after=None):
    m, k = a.shape
    n = w.shape[0] if w_t else w.shape[1]
    tm, tn = min(tm, m), min(tn, n)
    assert m % tm == 0 and n % tn == 0, (name, a.shape, w.shape)
    n_ex = len(extras)
    order = () if after is None else (after,)

    def body(a_ref, w_ref, *rest):
        rest = rest[len(order):]
        acc = (_dot_nt if w_t else _dot)(a_ref[...], w_ref[...])
        res = (acc,) if epilogue is None else epilogue(acc, *[r[...] for r in rest[:n_ex]])
        for o_ref, r in zip(rest[n_ex:], res):
            o_ref[...] = r.astype(o_ref.dtype)

    tile = pl.BlockSpec((tm, tn), lambda i, j: (i, j))
    w_spec = pl.BlockSpec((tn, k), lambda i, j: (j, 0)) if w_t else pl.BlockSpec((k, tn), lambda i, j: (0, j))
    return pl.pallas_call(
        body, grid=(m // tm, n // tn),
        in_specs=[pl.BlockSpec((tm, k), lambda i, j: (i, 0)), w_spec] + [pl.BlockSpec(memory_space=pl.ANY)] * len(order) + [tile] * n_ex,
        out_specs=[tile] * len(out_dtypes),
        out_shape=[jax.ShapeDtypeStruct((m, n), dt) for dt in out_dtypes],
        name=name, compiler_params=_params("parallel", "arbitrary"),
    )(a, w, *order, *extras)


def _matmul_res(a, w, res, name, w_t=False):
    return _matmul(a, w, name, extras=(res,), epilogue=lambda acc, r: (r + acc,), w_t=w_t)[0]


def _matmul_tn(x, y, name, tm=1024, tn=1024, tk=512, packed=None, after=None):
    t, m = x.shape
    _, n = y.shape
    tm, tn, tk = min(tm, m), min(tn, n), min(tk, t)
    assert m % tm == 0 and n % tn == 0 and t % tk == 0, (name, x.shape, y.shape)
    shape, place, into = packed or ((m, n), None, None)

    def body(x_ref, y_ref, *rest):
        o_ref = rest[-1]

        @pl.when(pl.program_id(2) == 0)
        def _():
            o_ref[...] = jnp.zeros_like(o_ref)

        o_ref[...] += _dot_tn(x_ref[...], y_ref[...])

    out_spec = (pl.BlockSpec((tm, tn), lambda i, j, k: (i, j)) if place is None
                else pl.BlockSpec((None, tm, tn), lambda i, j, k: place(i, j)))
    return pl.pallas_call(
        body, grid=(m // tm, n // tn, t // tk),
        in_specs=[pl.BlockSpec((tk, tm), lambda i, j, k: (k, i)), pl.BlockSpec((tk, tn), lambda i, j, k: (k, j))]
        + [pl.BlockSpec(memory_space=pl.ANY)] * ((into is not None) + (after is not None)),
        out_specs=out_spec, out_shape=jax.ShapeDtypeStruct(shape, F32),
        input_output_aliases={} if into is None else {2: 0},
        name=name, compiler_params=_params("parallel", "parallel", "arbitrary"),
    )(x, y, *(() if into is None else (into,)), *(() if after is None else (after,)))


def _rmsnorm(x, g, name, tm=512):
    t, d = x.shape
    tm = min(tm, t)

    def body(x_ref, g_ref, h_ref):
        xv = x_ref[...]
        r = lax.rsqrt(jnp.mean(xv * xv, axis=-1, keepdims=True) + EPS)
        h_ref[...] = (xv * r * g_ref[...]).astype(BF16)

    return pl.pallas_call(
        body, grid=(t // tm,),
        in_specs=[pl.BlockSpec((tm, d), lambda i: (i, 0)), pl.BlockSpec((1, d), lambda i: (0, 0))],
        out_specs=pl.BlockSpec((tm, d), lambda i: (i, 0)),
        out_shape=jax.ShapeDtypeStruct((t, d), BF16),
        name=name, compiler_params=_params("arbitrary"),
    )(x, g.reshape(1, d))


def _in_proj(x, g, w_all, name, tm=512):
    t, d = x.shape
    half = 3 * MIX_HALF

    def body(x_ref, g_ref, w_ref, h_ref, zd_ref, zf_ref, gate_ref):
        xv = x_ref[...]
        r = lax.rsqrt(jnp.mean(xv * xv, axis=-1, keepdims=True) + EPS)
        h = (xv * r * g_ref[...]).astype(BF16)
        h_ref[...] = h
        zd_ref[...] = _dot(h, w_ref[:, 0:half])
        zf_ref[...] = _dot(h, w_ref[:, half:2 * half]).astype(BF16)
        gate_ref[...] = _dot(h, w_ref[:, 2 * half:])

    row = lambda width: pl.BlockSpec((tm, width), lambda i: (i, 0))
    return pl.pallas_call(
        body, grid=(t // tm,),
        in_specs=[row(d), pl.BlockSpec((1, d), lambda i: (0, 0)), pl.BlockSpec(w_all.shape, lambda i: (0, 0))],
        out_specs=[row(d), row(half), row(half), row(GATE_PAD)],
        out_shape=[jax.ShapeDtypeStruct((t, d), BF16), jax.ShapeDtypeStruct((t, half), F32),
                   jax.ShapeDtypeStruct((t, half), BF16), jax.ShapeDtypeStruct((t, GATE_PAD), F32)],
        name=name, compiler_params=_params("arbitrary"),
    )(x, g.reshape(1, d), w_all)


def _rms_bwd_tile(xv, dh, g):
    d = xv.shape[-1]
    r = lax.rsqrt(jnp.mean(xv * xv, axis=-1, keepdims=True) + EPS)
    dyg = dh * g
    proj = jnp.sum(dyg * xv, axis=-1, keepdims=True)
    dx = r * dyg - xv * (r * r * r * (1.0 / d)) * proj
    return dx, dh * (xv * r)


def _rms_bwd(x, dh, g, dres, name, tm=512):
    t, d = x.shape
    tm = min(tm, t)
    has_res = dres is not None

    def body(x_ref, dh_ref, g_ref, *rest):
        if has_res:
            res_ref, dx_ref, dxb_ref, dg_ref = rest
        else:
            dx_ref, dxb_ref, dg_ref = rest
        dx, dg_rows = _rms_bwd_tile(x_ref[...], dh_ref[...], g_ref[...])
        if has_res:
            dx = res_ref[...] + dx
        dx_ref[...] = dx
        dxb_ref[...] = dx.astype(BF16)

        @pl.when(pl.program_id(0) == 0)
        def _():
            dg_ref[...] = jnp.zeros_like(dg_ref)

        dg_ref[...] += jnp.sum(dg_rows, axis=0, keepdims=True)

    row = pl.BlockSpec((tm, d), lambda i: (i, 0))
    vec = pl.BlockSpec((1, d), lambda i: (0, 0))
    return pl.pallas_call(
        body, grid=(t // tm,),
        in_specs=[row, row, vec] + ([row] if has_res else []),
        out_specs=[row, row, vec],
        out_shape=[jax.ShapeDtypeStruct((t, d), F32), jax.ShapeDtypeStruct((t, d), BF16), jax.ShapeDtypeStruct((1, d), F32)],
        name=name, compiler_params=_params("arbitrary"),
    )(x, dh, g.reshape(1, d), *((dres,) if has_res else ()))


def _row_dots(a_refs, w_refs, w_t):
    acc = None
    for a_ref, w_ref in zip(a_refs, w_refs):
        part = (_dot_nt if w_t else _dot)(a_ref[...], w_ref[...])
        acc = part if acc is None else acc + part
    return acc


def _row_specs(a_parts, w_parts, tm):
    specs = [pl.BlockSpec((tm, a.shape[1]), lambda i: (i, 0)) for a in a_parts]
    return specs + [pl.BlockSpec(w.shape, lambda i: (0, 0)) for w in w_parts]


def _matmul_res_norm(a_parts, w_parts, res, g, name, tm=512):
    t, d = res.shape
    n = len(a_parts)

    def body(*refs):
        res_ref, g_ref, x_ref, h_ref = refs[2 * n:]
        xv = res_ref[...] + _row_dots(refs[:n], refs[n:2 * n], False)
        x_ref[...] = xv
        r = lax.rsqrt(jnp.mean(xv * xv, axis=-1, keepdims=True) + EPS)
        h_ref[...] = (xv * r * g_ref[...]).astype(BF16)

    row = pl.BlockSpec((tm, d), lambda i: (i, 0))
    return pl.pallas_call(
        body, grid=(t // tm,),
        in_specs=_row_specs(a_parts, w_parts, tm) + [row, pl.BlockSpec((1, d), lambda i: (0, 0))],
        out_specs=[row, row],
        out_shape=[jax.ShapeDtypeStruct((t, d), F32), jax.ShapeDtypeStruct((t, d), BF16)],
        name=name, compiler_params=_params("arbitrary"),
    )(*a_parts, *w_parts, res, g.reshape(1, d))


def _matmul_rms_bwd(a_parts, w_parts, x, g, dres, name, tm=512, after=None):
    t, d = x.shape
    n = len(a_parts)
    order = () if after is None else (after,)

    def body(*refs):
        x_ref, g_ref, res_ref = refs[2 * n:2 * n + 3]
        dx_ref, dxb_ref, dg_ref = refs[2 * n + 3 + len(order):]
        dx, dg_rows = _rms_bwd_tile(x_ref[...], _row_dots(refs[:n], refs[n:2 * n], True), g_ref[...])
        dx = res_ref[...] + dx
        dx_ref[...] = dx
        dxb_ref[...] = dx.astype(BF16)

        @pl.when(pl.program_id(0) == 0)
        def _():
            dg_ref[...] = jnp.zeros_like(dg_ref)

        dg_ref[...] += jnp.sum(dg_rows, axis=0, keepdims=True)

    row = pl.BlockSpec((tm, d), lambda i: (i, 0))
    vec = pl.BlockSpec((1, d), lambda i: (0, 0))
    return pl.pallas_call(
        body, grid=(t // tm,),
        in_specs=_row_specs(a_parts, w_parts, tm) + [row, vec, row] + [pl.BlockSpec(memory_space=pl.ANY)] * len(order),
        out_specs=[row, row, vec],
        out_shape=[jax.ShapeDtypeStruct((t, d), F32), jax.ShapeDtypeStruct((t, d), BF16), jax.ShapeDtypeStruct((1, d), F32)],
        name=name, compiler_params=_params("arbitrary"),
    )(*a_parts, *w_parts, x, g.reshape(1, d), dres, *order)


def _loss_bwd(a, w, res, g, target, name, tm=512):
    t, d = res.shape

    def body(a_ref, w_ref, x_ref, g_ref, t_ref, loss_ref, dx_ref, dxb_ref, dg_ref):
        xv = x_ref[...] + _dot(a_ref[...], w_ref[...])
        gv = g_ref[...]
        r = lax.rsqrt(jnp.mean(xv * xv, axis=-1, keepdims=True) + EPS)
        err = xv * r * gv - t_ref[...]
        dx, dg_rows = _rms_bwd_tile(xv, err * (1.0 / d), gv)
        dx_ref[...] = dx
        dxb_ref[...] = dx.astype(BF16)

        @pl.when(pl.program_id(0) == 0)
        def _():
            dg_ref[...] = jnp.zeros_like(dg_ref)
            loss_ref[...] = jnp.zeros_like(loss_ref)

        dg_ref[...] += jnp.sum(dg_rows, axis=0, keepdims=True)
        part = jnp.sum(jnp.sum(err * err, axis=0, keepdims=True), axis=1, keepdims=True) * (0.5 / d)
        loss_ref[...] += jnp.broadcast_to(part, loss_ref.shape)

    row = pl.BlockSpec((tm, d), lambda i: (i, 0))
    vec = pl.BlockSpec((1, d), lambda i: (0, 0))
    return pl.pallas_call(
        body, grid=(t // tm,),
        in_specs=_row_specs([a], [w], tm) + [row, vec, row],
        out_specs=[pl.BlockSpec((1, LANES), lambda i: (0, 0)), row, row, vec],
        out_shape=[jax.ShapeDtypeStruct((1, LANES), F32), jax.ShapeDtypeStruct((t, d), F32),
                   jax.ShapeDtypeStruct((t, d), BF16), jax.ShapeDtypeStruct((1, d), F32)],
        name=name, compiler_params=_params("arbitrary"),
    )(a, w, res, g.reshape(1, d), target)


def _tri(upper):
    r = lax.broadcasted_iota(jnp.int32, (LANES, LANES), 0)
    c = lax.broadcasted_iota(jnp.int32, (LANES, LANES), 1)
    return jnp.where((r <= c) if upper else (r >= c), 1.0, 0.0).astype(BF16)


def _gate_fwd(gate, b_pad, n_batch, name):
    s = SEQ
    nblk = s // LANES

    def body(g_ref, b_ref, cbc_ref, crow_ref, sg_ref, ct_ref):
        gz = g_ref[...] + b_ref[...]
        logf = jnp.minimum(gz, 0.0) - jnp.log(1.0 + jnp.exp(-jnp.abs(gz)))
        logf_t = logf.T
        sg_ref[...] = (1.0 / (1.0 + jnp.exp(gz))).T[0:N_HEADS]
        upper = _tri(True)
        carry = jnp.zeros((LANES, 1), F32)
        for blk in range(nblk):
            seg = _dot_exact(logf_t[:, blk * LANES:(blk + 1) * LANES], upper) + carry
            carry = seg[:, LANES - 1:LANES]
            ct_ref[:, blk * LANES:(blk + 1) * LANES] = seg
        ct = ct_ref[...]
        crow_ref[...] = ct[0:N_HEADS]
        c_col = ct.T
        lane = lax.broadcasted_iota(jnp.int32, (1, MIX_HALF), 1)
        acc = jnp.zeros((s, MIX_HALF), F32)
        for h in range(N_HEADS):
            acc = jnp.where((lane >= HEAD_DIM * h) & (lane < HEAD_DIM * (h + 1)), c_col[:, h:h + 1], acc)
        cbc_ref[...] = acc

    return pl.pallas_call(
        body, grid=(n_batch,),
        in_specs=[pl.BlockSpec((s, GATE_PAD), lambda b: (b, 0)), pl.BlockSpec((1, GATE_PAD), lambda b: (0, 0))],
        out_specs=[pl.BlockSpec((s, MIX_HALF), lambda b: (b, 0)),
                   pl.BlockSpec((None, N_HEADS, s), lambda b: (b, 0, 0)),
                   pl.BlockSpec((None, N_HEADS, s), lambda b: (b, 0, 0))],
        out_shape=[jax.ShapeDtypeStruct((n_batch * s, MIX_HALF), F32),
                   jax.ShapeDtypeStruct((n_batch, N_HEADS, s), F32),
                   jax.ShapeDtypeStruct((n_batch, N_HEADS, s), F32)],
        scratch_shapes=[pltpu.VMEM((LANES, s), F32)],
        name=name, compiler_params=_params("arbitrary"),
    )(gate, b_pad)


def _gate_bwd(dc, sg, name):
    n_batch, _, s = dc.shape
    nblk = s // LANES

    def body(dc_ref, sg_ref, dz_ref, db_ref, dt_ref):
        lower = _tri(False)
        dcv = dc_ref[...]
        carry = jnp.zeros((N_HEADS, 1), F32)
        dt_ref[...] = jnp.zeros_like(dt_ref)
        for blk in reversed(range(nblk)):
            seg = _dot_exact(dcv[:, blk * LANES:(blk + 1) * LANES], lower) + carry
            carry = seg[:, 0:1]
            dt_ref[0:N_HEADS, blk * LANES:(blk + 1) * LANES] = seg * sg_ref[:, blk * LANES:(blk + 1) * LANES]
        dg_t = dt_ref[...]
        dz_ref[...] = dg_t.T.astype(BF16)

        @pl.when(pl.program_id(0) == 0)
        def _():
            db_ref[...] = jnp.zeros_like(db_ref)

        db_ref[...] += jnp.broadcast_to(jnp.sum(dg_t[0:N_HEADS], axis=1, keepdims=True), db_ref.shape)

    return pl.pallas_call(
        body, grid=(n_batch,),
        in_specs=[pl.BlockSpec((None, N_HEADS, s), lambda b: (b, 0, 0)), pl.BlockSpec((None, N_HEADS, s), lambda b: (b, 0, 0))],
        out_specs=[pl.BlockSpec((s, GATE_PAD), lambda b: (b, 0)), pl.BlockSpec((N_HEADS, LANES), lambda b: (0, 0))],
        out_shape=[jax.ShapeDtypeStruct((n_batch * s, GATE_PAD), BF16), jax.ShapeDtypeStruct((N_HEADS, LANES), F32)],
        scratch_shapes=[pltpu.VMEM((LANES, s), F32)],
        name=name, compiler_params=_params("arbitrary"),
    )(dc, sg)


FOX_BQ = 512
FOX_BK = 512
FOX_STRIP = 512
PAIR_WIDTH = 3 * LANES
N_PAIRS = N_HEADS // 2


def _pair_major(w):
    return w.reshape(w.shape[0], 3, N_PAIRS, LANES).transpose(0, 2, 1, 3).reshape(w.shape[0], 3 * MIX_HALF)


def _pair_major_inv(w):
    return w.reshape(w.shape[0], N_PAIRS, 3, LANES).transpose(0, 2, 1, 3).reshape(w.shape[0], 3 * MIX_HALF)


def _causal(i, j, bq, bk):
    qpos = i * bq + lax.broadcasted_iota(jnp.int32, (bq, 1), 0)
    kpos = j * bk + lax.broadcasted_iota(jnp.int32, (1, bk), 1)
    return kpos <= qpos


def _split_bf16(p):
    hi = p.astype(BF16)
    return hi, (p - hi.astype(F32)).astype(BF16)


def _fox_fwd(zf, c_bc, c_row, n_batch, name):
    s, bq, bk = SEQ, FOX_BQ, FOX_BK
    nq = s // bq
    t = n_batch * s

    n_strip = bq // FOX_STRIP

    def body(q_ref, k_ref, v_ref, cq_ref, cr_ref, o_ref, o32_ref, lse_ref):
        hp = pl.program_id(1)
        strips = [slice(r * FOX_STRIP, (r + 1) * FOX_STRIP) for r in range(n_strip)]
        chains = [(e, r) for e in range(2) for r in range(n_strip)]
        qh, cq = {}, {}
        for e, r in chains:
            q = q_ref[strips[r], :] * ATT_SCALE
            qh[e, r] = jnp.where(_head_mask(e), q, jnp.zeros_like(q))
            cq[e, r] = cq_ref[strips[r], HEAD_DIM * e:HEAD_DIM * e + 1]

        def step(i, j, carry, masked):
            rows = pl.ds(j * bk, bk)
            kj, vj = k_ref[rows, :], v_ref[rows, :]
            ck = [cr_ref[pl.ds(2 * hp + e, 1), rows] for e in range(2)]
            out = []
            scores = [_dot_nt(qh[e, r], kj) for e, r in chains]
            for n, (e, r) in enumerate(chains):
                m, l, acc = carry[3 * n:3 * n + 3]
                sc = scores[n] + (cq[e, r] - ck[e])
                if masked:
                    qpos = i * bq + r * FOX_STRIP + lax.broadcasted_iota(jnp.int32, (FOX_STRIP, 1), 0)
                    kpos = j * bk + lax.broadcasted_iota(jnp.int32, (1, bk), 1)
                    sc = jnp.where(kpos <= qpos, sc, NEG)
                m_new = jnp.maximum(m, jnp.max(sc, axis=1, keepdims=True))
                alpha = jnp.exp(m - m_new)
                p = jnp.exp(sc - m_new)
                p_hi, p_lo = _split_bf16(p)
                out += [m_new, alpha * l + jnp.sum(p, axis=1, keepdims=True), alpha * acc + (_dot(p_hi, vj) + _dot(p_lo, vj))]
            return tuple(out)

        def run(i):
            carry = (jnp.full((FOX_STRIP, 1), NEG, F32), jnp.zeros((FOX_STRIP, 1), F32), jnp.zeros((FOX_STRIP, LANES), F32)) * len(chains)
            n_clear = (i * bq) // bk
            for j in range((i * bq + bq + bk - 1) // bk):
                carry = step(i, j, carry, masked=j >= n_clear)
            for r in range(n_strip):
                outs = [carry[3 * (e * n_strip + r) + 2] / carry[3 * (e * n_strip + r) + 1] for e in range(2)]
                lses = [carry[3 * (e * n_strip + r)] + jnp.log(carry[3 * (e * n_strip + r) + 1]) for e in range(2)]
                o = jnp.where(_head_mask(0), outs[0], outs[1])
                o_ref[strips[r], :] = o.astype(BF16)
                o32_ref[strips[r], :] = o
                lse_ref[strips[r], :] = jnp.where(_head_mask(0), lses[0], lses[1])

        for k in range(nq):
            pl.when(pl.program_id(2) == k)(functools.partial(run, k))

    def col(c0):
        return lambda b, hp, i: (b, 3 * hp + c0)

    blk = pl.BlockSpec((bq, LANES), lambda b, hp, i: (b * nq + i, hp))
    return pl.pallas_call(
        body, grid=(n_batch, N_PAIRS, nq),
        in_specs=[pl.BlockSpec((bq, LANES), lambda b, hp, i: (b * nq + i, 3 * hp)),
                  pl.BlockSpec((s, LANES), col(1)), pl.BlockSpec((s, LANES), col(2)), blk,
                  pl.BlockSpec((None, N_HEADS, s), lambda b, hp, i: (b, 0, 0))],
        out_specs=[blk, blk, blk],
        out_shape=[jax.ShapeDtypeStruct((t, MIX_HALF), BF16), jax.ShapeDtypeStruct((t, MIX_HALF), F32),
                   jax.ShapeDtypeStruct((t, MIX_HALF), F32)],
        name=name, compiler_params=_params("parallel", "parallel", "arbitrary"),
    )(zf, zf, zf, c_bc, c_row)


def _fox_bwd(zf, o32, dy, lse, c_bc, c_row, dz, n_batch, name):
    s, bq, bk = SEQ, FOX_BQ, FOX_BK
    nq, nk = s // bq, s // bk

    def body(q_ref, k_ref, v_ref, o_ref, do_ref, lse_ref, cq_ref, cr_ref, dz_in, dz_ref, dc_ref, dq_acc):
        del dz_in
        hp = pl.program_id(1)

        @pl.when(pl.program_id(2) == 0)
        def _():
            dq_acc[...] = jnp.zeros_like(dq_acc)

        kj, vj = k_ref[...], v_ref[...]
        km = [jnp.where(_head_mask(e), kj, jnp.zeros_like(kj)) for e in range(2)]

        def step(i, j, ck, carry, masked):
            rows = pl.ds(i * bq, bq)
            qi, doi = q_ref[rows, :] * ATT_SCALE, do_ref[rows, :]
            prod = doi.astype(F32) * o_ref[rows, :]
            out = []
            dq = jnp.zeros((bq, LANES), F32)
            for e in range(2):
                dk_a, dv_a, dc_a = carry[3 * e:3 * e + 3]
                mask = _head_mask(e)
                lane0 = HEAD_DIM * e
                dom = jnp.where(mask, doi, jnp.zeros_like(doi))
                delta = jnp.sum(jnp.where(mask, prod, 0.0), axis=1, keepdims=True)
                sc = _dot_nt(qi, km[e]) + (cq_ref[rows, lane0:lane0 + 1] - ck[e])
                if masked:
                    sc = jnp.where(_causal(i, j, bq, bk), sc, NEG)
                p = jnp.exp(sc - lse_ref[rows, lane0:lane0 + 1])
                ds = p * (_dot_nt(dom, vj) - delta)
                dsb = ds.astype(BF16)
                dq = dq + _dot(dsb, km[e])
                out += [dk_a + _dot_tn(dsb, qi), dv_a + _dot_tn(p.astype(BF16), dom), dc_a - jnp.sum(ds, axis=0, keepdims=True)]
            dq_acc[rows, :] += dq * ATT_SCALE
            return tuple(out)

        def run(j):
            cols = pl.ds(j * bk, bk)
            ck = [cr_ref[pl.ds(2 * hp + e, 1), cols] for e in range(2)]
            carry = (jnp.zeros((bk, LANES), F32), jnp.zeros((bk, LANES), F32), jnp.zeros((1, bk), F32)) * 2
            n_diag = (j * bk + bk + bq - 1) // bq
            for i in range((j * bk) // bq, nq):
                carry = step(i, j, ck, carry, masked=i < n_diag)
            for e in range(2):
                dc_ref[e:e + 1, :] = carry[3 * e + 2]
            dz_ref[cols, LANES:2 * LANES] = jnp.where(_head_mask(0), carry[0], carry[3]).astype(BF16)
            dz_ref[cols, 2 * LANES:3 * LANES] = (carry[1] + carry[4]).astype(BF16)
            if j == nk - 1:
                dz_ref[:, 0:LANES] = dq_acc[...].astype(BF16)

        for k in range(nk):
            pl.when(pl.program_id(2) == k)(functools.partial(run, k))

    def seq(idx):
        return pl.BlockSpec((s, LANES), lambda b, hp, j: (b, idx(hp)))

    def kblk(c0):
        return pl.BlockSpec((bk, LANES), lambda b, hp, j: (b * nk + j, 3 * hp + c0))

    return pl.pallas_call(
        body, grid=(n_batch, N_PAIRS, nk),
        in_specs=[seq(lambda hp: 3 * hp), kblk(1), kblk(2), seq(lambda hp: hp), seq(lambda hp: N_PAIRS + hp),
                  seq(lambda hp: hp), seq(lambda hp: hp),
                  pl.BlockSpec((None, N_HEADS, s), lambda b, hp, j: (b, 0, 0)), pl.BlockSpec(memory_space=pl.ANY)],
        out_specs=[pl.BlockSpec((s, PAIR_WIDTH), lambda b, hp, j: (b, N_PAIRS + hp)),
                   pl.BlockSpec((None, None, 2, bk), lambda b, hp, j: (b, hp, 0, j))],
        out_shape=[jax.ShapeDtypeStruct(dz.shape, dz.dtype), jax.ShapeDtypeStruct((n_batch, N_PAIRS, 2, s), F32)],
        scratch_shapes=[pltpu.VMEM((s, LANES), F32)],
        input_output_aliases={8: 0},
        name=name, compiler_params=_params("parallel", "parallel", "arbitrary"),
    )(zf, zf, zf, o32, dy, lse, c_bc, c_row, dz)


def _dil_bias(slope, dil):
    qi = lax.broadcasted_iota(jnp.int32, (BLOCK, 2 * BLOCK), 0)
    kj = lax.broadcasted_iota(jnp.int32, (BLOCK, 2 * BLOCK), 1)
    delta = qi + BLOCK - kj
    return jnp.where((delta >= 0) & (delta <= BLOCK), (-slope * dil) * delta.astype(F32), NEG)


def _alibi_slope(hp, e):
    slope = jnp.float32(0.0)
    for k in range(N_PAIRS):
        slope = jnp.where(hp == k, jnp.float32(2.0 ** -(2 * k + e + 1)), slope)
    return slope


def _first_block_bias(bias):
    return jnp.where(lax.broadcasted_iota(jnp.int32, bias.shape, 1) < BLOCK, NEG, bias)


def _fill_bias(bias_scr, hp):
    for di, dil in enumerate(DILATIONS):
        for e in range(2):
            bias_scr[2 * di + e] = _dil_bias(_alibi_slope(hp, e), dil)


def _pair_specs(rows):
    return [pl.BlockSpec((rows, LANES), lambda b, hp, c0=c0: (b, 3 * hp + c0)) for c0 in range(3)]


def _strided(start, size, dil):
    return pl.ds(start, size) if dil == 1 else pl.ds(start, size, stride=dil)


QUARTER = SEQ // 4


def _to_quarters(src, dst):
    for r in range(4):
        dst[r * QUARTER:(r + 1) * QUARTER, :] = src[pl.ds(r, QUARTER, stride=4), :]


def _from_quarters(src, dst):
    for r in range(4):
        dst[pl.ds(r, QUARTER, stride=4), :] = src[r * QUARTER:(r + 1) * QUARTER, :]


def _mix_weights(l1, l2, l3):
    m = jnp.maximum(jnp.maximum(l1, l2), l3)
    e1, e2, e3 = jnp.exp(l1 - m), jnp.exp(l2 - m), jnp.exp(l3 - m)
    inv = 1.0 / (e1 + e2 + e3)
    return e1 * inv, e2 * inv, e3 * inv


def _dil_fwd(zd, n_batch, name):
    s = SEQ
    t = n_batch * s

    def body(q_ref, k_ref, v_ref, y_ref, l1_ref, l2_ref, l3_ref, o_scr, qkv4, o4, l4, bias_scr):
        _fill_bias(bias_scr, pl.program_id(1))
        for a, ref in enumerate((q_ref, k_ref, v_ref)):
            _to_quarters(ref, qkv4.at[a])

        def unit(srcs, start, first, stride, di, o_dst, l_dst):
            qrows = _strided(start, BLOCK, stride)
            krows = qrows if first else _strided(start - BLOCK * stride, 2 * BLOCK, stride)
            q = (srcs[0][qrows, :] * ATT_SCALE).astype(BF16)
            kc = srcs[1][krows, :].astype(BF16)
            vc = srcs[2][krows, :].astype(BF16)
            if first:
                kc, vc = jnp.concatenate([kc, kc]), jnp.concatenate([vc, vc])
            outs, lses = [], []
            for e in range(2):
                bias = _first_block_bias(bias_scr[2 * di + e]) if first else bias_scr[2 * di + e]
                sc = _dot_nt(jnp.where(_head_mask(e), q, jnp.zeros_like(q)), kc) + bias
                m = jnp.max(sc, axis=1, keepdims=True)
                pe = jnp.exp(sc - m)
                l = jnp.sum(pe, axis=1, keepdims=True)
                outs.append(_dot((pe * (1.0 / l)).astype(BF16), vc))
                lses.append(m + jnp.log(l))
            o_dst[qrows, :] = jnp.where(_head_mask(0), outs[0], outs[1])
            l_dst[qrows, :] = jnp.where(_head_mask(0), lses[0], lses[1])

        for n in range(SEQ // BLOCK):
            unit((q_ref, k_ref, v_ref), n * BLOCK, n == 0, 1, 0, o_scr.at[0], l1_ref)
        quarters = tuple(qkv4.at[a] for a in range(3))
        for di in (1, 2):
            stride = DILATIONS[di] // 4
            for r in range(4):
                for g in range(stride):
                    for n in range(QUARTER // (BLOCK * stride)):
                        unit(quarters, r * QUARTER + n * BLOCK * stride + g, n == 0, stride, di, o4.at[di - 1], l4.at[di - 1])
        for di, l_ref in ((1, l2_ref), (2, l3_ref)):
            _from_quarters(o4.at[di - 1], o_scr.at[di])
            _from_quarters(l4.at[di - 1], l_ref)
        w = _mix_weights(l1_ref[...], l2_ref[...], l3_ref[...])
        y_ref[...] = (w[0] * o_scr[0] + w[1] * o_scr[1] + w[2] * o_scr[2]).astype(BF16)

    blk = pl.BlockSpec((s, LANES), lambda b, hp: (b, hp))
    res = pl.pallas_call(
        body, grid=(n_batch, N_PAIRS),
        in_specs=_pair_specs(s),
        out_specs=[blk] * 4,
        out_shape=[jax.ShapeDtypeStruct((t, MIX_HALF), BF16)] + [jax.ShapeDtypeStruct((t, MIX_HALF), F32)] * 3,
        scratch_shapes=[pltpu.VMEM((3, s, LANES), F32), pltpu.VMEM((3, s, LANES), F32), pltpu.VMEM((2, s, LANES), F32),
                        pltpu.VMEM((2, s, LANES), F32), pltpu.VMEM((6, BLOCK, 2 * BLOCK), F32)],
        name=name, compiler_params=_params("parallel", "arbitrary"),
    )(zd, zd, zd)
    return res[0], res[1:]


def _dil_bwd(zd, dy, ya, lses, n_batch, name):
    s = SEQ
    t = n_batch * s

    def body(q_ref, k_ref, v_ref, dy_ref, ya_ref, l1_ref, l2_ref, l3_ref, dz_ref, w_scr, dy_scr, dot_scr, acc, st4, acc4, bias_scr):
        for di, dil in enumerate(DILATIONS):
            bias_scr[di] = jnp.concatenate([_dil_bias(_alibi_slope(pl.program_id(1), e), dil) for e in range(2)])
        for di, w in enumerate(_mix_weights(l1_ref[...], l2_ref[...], l3_ref[...])):
            w_scr[di] = w
        dya = dy_ref[...].astype(F32)
        prod = dya * ya_ref[...].astype(F32)
        per_head = [jnp.sum(jnp.where(_head_mask(e), prod, 0.0), axis=1, keepdims=True) for e in range(2)]
        dy_scr[...] = dya
        dot_scr[...] = jnp.where(_head_mask(0), per_head[0], per_head[1])
        acc[...] = jnp.zeros_like(acc)
        acc4[...] = jnp.zeros_like(acc4)
        staged = (q_ref, k_ref, v_ref, w_scr.at[1], w_scr.at[2], l2_ref, l3_ref, dy_scr, dot_scr)
        for a, ref in enumerate(staged):
            _to_quarters(ref, st4.at[a])

        def unit(srcs, dst, start, first, stride, di):
            qrows = _strided(start, BLOCK, stride)
            krows = qrows if first else _strided(start - BLOCK * stride, 2 * BLOCK, stride)
            q = (srcs[0][qrows, :] * ATT_SCALE).astype(BF16)
            kc = srcs[1][krows, :].astype(BF16)
            vc = srcs[2][krows, :].astype(BF16)
            wq = srcs[3][qrows, :]
            lse = srcs[4][qrows, :]
            do = (wq * srcs[5][qrows, :]).astype(BF16)
            sub = wq * srcs[6][qrows, :]
            heads = lambda a: jnp.concatenate([jnp.where(_head_mask(e), a, jnp.zeros_like(a)) for e in range(2)])
            column = lambda a: jnp.concatenate([a[:, HEAD_DIM * e:HEAD_DIM * e + 1] for e in range(2)])
            qq, dd = heads(q), heads(do)
            bias = bias_scr[di]
            p = jnp.exp(_dot_nt(qq, kc) + (bias[:, BLOCK:] if first else bias) - column(lse))
            dsb = (p * (_dot_nt(dd, vc) - column(sub))).astype(BF16)
            dq = _dot(jnp.concatenate([dsb[:BLOCK], dsb[BLOCK:]], axis=1), heads(kc))
            dst.at[0][qrows, :] += dq * ATT_SCALE
            dst.at[1][krows, :] += _dot_tn(dsb, qq)
            dst.at[2][krows, :] += _dot_tn(p.astype(BF16), dd)

        token_order = (q_ref, k_ref, v_ref, w_scr.at[0], l1_ref, dy_scr, dot_scr)
        for n in range(SEQ // BLOCK):
            unit(token_order, acc, n * BLOCK, n == 0, 1, 0)
        for di in (1, 2):
            quarters = (st4.at[0], st4.at[1], st4.at[2], st4.at[2 + di], st4.at[4 + di], st4.at[7], st4.at[8])
            stride = DILATIONS[di] // 4
            for r in range(4):
                for g in range(stride):
                    for n in range(QUARTER // (BLOCK * stride)):
                        unit(quarters, acc4, r * QUARTER + n * BLOCK * stride + g, n == 0, stride, di)
        for k in range(3):
            for r in range(4):
                acc.at[k][pl.ds(r, QUARTER, stride=4), :] += acc4[k, r * QUARTER:(r + 1) * QUARTER, :]
            dz_ref[:, k * LANES:(k + 1) * LANES] = acc[k].astype(BF16)

    blk = pl.BlockSpec((s, LANES), lambda b, hp: (b, hp))
    pair = pl.BlockSpec((s, PAIR_WIDTH), lambda b, hp: (b, hp))
    return pl.pallas_call(
        body, grid=(n_batch, N_PAIRS),
        in_specs=_pair_specs(s) + [blk] * 5,
        out_specs=pair,
        out_shape=jax.ShapeDtypeStruct((t, 2 * 3 * MIX_HALF), BF16),
        scratch_shapes=[pltpu.VMEM((3, s, LANES), F32), pltpu.VMEM((s, LANES), F32), pltpu.VMEM((s, LANES), F32),
                        pltpu.VMEM((3, s, LANES), F32), pltpu.VMEM((9, s, LANES), F32), pltpu.VMEM((3, s, LANES), F32),
                        pltpu.VMEM((3, 2 * BLOCK, 2 * BLOCK), F32)],
        name=name, compiler_params=_params("parallel", "arbitrary"),
    )(zd, zd, zd, dy, ya, *lses)


X_BQ = 2048


def _xattn_probs(q, k):
    sc = _dot_nt(q, k) * X_SCALE
    pe = jnp.exp(sc - jnp.max(sc, axis=1, keepdims=True))
    return pe / jnp.sum(pe, axis=1, keepdims=True)


def _xattn_fwd(qx, kx, vx, n_batch, name):
    nq = SEQ // X_BQ

    def body(q_ref, k_ref, v_ref, o_ref):
        p = _xattn_probs(q_ref[...], k_ref[...])
        o_ref[...] = _dot(p.astype(BF16), v_ref[...]).astype(BF16)

    qblk = pl.BlockSpec((X_BQ, X_HEAD_DIM), lambda b, h, i: (b * nq + i, h))
    kblk = pl.BlockSpec((N_MEM, X_HEAD_DIM), lambda b, h, i: (b, h))
    return pl.pallas_call(
        body, grid=(n_batch, X_HEADS, nq), in_specs=[qblk, kblk, kblk], out_specs=qblk,
        out_shape=jax.ShapeDtypeStruct(qx.shape, BF16),
        name=name, compiler_params=_params("parallel", "parallel", "arbitrary"),
    )(qx, kx, vx)


def _xattn_bwd(qx, kx, vx, dox, n_batch, name):
    nq = SEQ // X_BQ

    def body(q_ref, k_ref, v_ref, do_ref, dq_ref, dk_ref, dv_ref, dk_acc, dv_acc):
        i = pl.program_id(2)

        @pl.when(i == 0)
        def _():
            dk_acc[...] = jnp.zeros_like(dk_acc)
            dv_acc[...] = jnp.zeros_like(dv_acc)

        q, k, do = q_ref[...], k_ref[...], do_ref[...]
        p = _xattn_probs(q, k)
        dp = _dot_nt(do, v_ref[...])
        dsb = (p * (dp - jnp.sum(p * dp, axis=1, keepdims=True))).astype(BF16)
        dq_ref[...] = (_dot(dsb, k) * X_SCALE).astype(BF16)
        dk_acc[...] += _dot_tn(dsb, q) * X_SCALE
        dv_acc[...] += _dot_tn(p.astype(BF16), do)

        @pl.when(i == nq - 1)
        def _():
            dk_ref[...] = dk_acc[...].astype(BF16)
            dv_ref[...] = dv_acc[...].astype(BF16)

    qblk = pl.BlockSpec((X_BQ, X_HEAD_DIM), lambda b, h, i: (b * nq + i, h))
    kblk = pl.BlockSpec((N_MEM, X_HEAD_DIM), lambda b, h, i: (b, h))
    return pl.pallas_call(
        body, grid=(n_batch, X_HEADS, nq), in_specs=[qblk, kblk, kblk, qblk], out_specs=[qblk, kblk, kblk],
        out_shape=[jax.ShapeDtypeStruct(qx.shape, BF16), jax.ShapeDtypeStruct(kx.shape, BF16), jax.ShapeDtypeStruct(kx.shape, BF16)],
        scratch_shapes=[pltpu.VMEM((N_MEM, X_HEAD_DIM), F32)] * 2,
        name=name, compiler_params=_params("parallel", "parallel", "arbitrary"),
    )(qx, kx, vx, dox)


def _adamw(w, g, m, v, name, rows):
    r, c = w.shape
    assert r % rows == 0, (name, w.shape, rows)

    def body(w_ref, g_ref, m_ref, v_ref, d_ref, nm_ref, nv_ref):
        gv = g_ref[...]
        m1 = ADAM_B1 * m_ref[...] + (1.0 - ADAM_B1) * gv
        v1 = ADAM_B2 * v_ref[...] + (1.0 - ADAM_B2) * jnp.square(gv)
        m_hat = m1 / (1.0 - ADAM_B1 ** ADAM_STEP)
        v_hat = v1 / (1.0 - ADAM_B2 ** ADAM_STEP)
        d_ref[...] = -ADAM_LR * (m_hat / (jnp.sqrt(v_hat) + ADAM_EPS) + ADAM_WD * w_ref[...])
        nm_ref[...] = m1
        nv_ref[...] = v1

    blk = pl.BlockSpec((rows, c), lambda i: (i, 0))
    return pl.pallas_call(
        body, grid=(r // rows,), in_specs=[blk] * 4, out_specs=[blk] * 3,
        out_shape=[jax.ShapeDtypeStruct((r, c), F32)] * 3,
        name=name, compiler_params=_params("arbitrary"),
    )(w, g, m, v)


def _relu2(acc):
    a = jnp.maximum(acc, 0.0)
    return acc, a * a


def _relu2_bwd(acc, u):
    return (2.0 * jnp.maximum(u.astype(F32), 0.0) * acc,)


def _local_step(x, mem, target, vecs, w_in, late_weights, hooks=None):
    n_batch = x.shape[0]
    t = n_batch * SEQ
    x0 = x.reshape(t, D_MODEL)
    mem2 = mem.reshape(n_batch * N_MEM, D_MODEL)
    tgt = target.reshape(t, D_MODEL)

    half = 3 * MIX_HALF
    w_qkv = jnp.concatenate([_pair_major(w_in[:, :half]), _pair_major(w_in[:, half:QKV_WIDTH])], axis=1)
    w_gate = jnp.pad(w_in[:, QKV_WIDTH:], ((0, 0), (0, GATE_PAD - N_HEADS)))
    b_pad = jnp.pad(vecs["b_forget"], (0, GATE_PAD - N_HEADS)).reshape(1, GATE_PAD)

    h1, zd, zf, gate = _in_proj(x0, vecs["g_mix"], jnp.concatenate([w_qkv, w_gate], axis=1), "in_proj")
    mn = _rmsnorm(mem2, vecs["g_mem"], "norm_mem")
    c_bc, c_row, sg = _gate_fwd(gate, b_pad, n_batch, "gate_fwd")
    ya, lses = _dil_fwd(zd, n_batch, "dil_fwd")
    yf, of32, lse_f = _fox_fwd(zf, c_bc, c_row, n_batch, "fox_fwd")
    wts = late_weights(yf)
    w_out = wts["w_out"]
    x1, h2 = _matmul_res_norm([ya, yf], [w_out[:MIX_HALF], w_out[MIX_HALF:]], x0, vecs["g_xattn"], "out")
    qx = _matmul(h2, wts["w_xq"], "xq", out_dtypes=(BF16,))[0]
    kx = _matmul(mn, wts["w_xk"], "xk", out_dtypes=(BF16,))[0]
    vx = _matmul(mn, wts["w_xv"], "xv", out_dtypes=(BF16,))[0]
    ox = _xattn_fwd(qx, kx, vx, n_batch, "xattn_fwd")
    x2, h3 = _matmul_res_norm([ox], [wts["w_xo"]], x1, vecs["g_mlp"], "xo")
    u, a2 = _matmul(h3, wts["w_up"], "mlp_up", out_dtypes=(BF16, BF16), epilogue=_relu2, tn=1024)
    loss, dx3, dx3b, dg_final = _loss_bwd(a2, wts["w_down"], x2, vecs["g_final"], tgt, "mlp_down_loss")

    du = _matmul(dx3b, wts["w_down"], "mlp_down_bwd", out_dtypes=(BF16,), extras=(u,), epilogue=_relu2_bwd, tn=1024, w_t=True)[0]
    shards = (N_CHIPS, 2 * D_MODEL, D_MODEL)
    g_mlp = _matmul_tn(h3, du, "gw_up", packed=(shards, lambda i, j: (j, 0, 0), None))
    g_mlp = _matmul_tn(a2, dx3b, "gw_down", packed=(shards, lambda i, j: (i, 1, 0), g_mlp))
    gw_up = g_mlp[:, :D_MODEL].transpose(1, 0, 2).reshape(D_MODEL, D_FF)
    gw_down = g_mlp[:, D_MODEL:].reshape(D_FF, D_MODEL)
    on_grads, on_swapped = hooks or (None, None)
    token = on_grads("mlp", g_mlp) if hooks else None
    dx2, dx2b, dg_mlp = _matmul_rms_bwd([du], [wts["w_up"]], x2, vecs["g_mlp"], dx3, "mlp_up_bwd", after=token)

    gw_xo = _matmul_tn(ox, dx2b, "gw_xo")
    token = on_swapped("mlp", dx2b) if hooks else None
    dox = _matmul(dx2b, wts["w_xo"], "xo_bwd", out_dtypes=(BF16,), w_t=True, after=token)[0]
    dqx, dkx, dvx = _xattn_bwd(qx, kx, vx, dox, n_batch, "xattn_bwd")
    gw_xq = _matmul_tn(h2, dqx, "gw_xq")
    gw_xk = _matmul_tn(mn, dkx, "gw_xk")
    gw_xv = _matmul_tn(mn, dvx, "gw_xv")
    dmn = _matmul(dkx, wts["w_xk"], "xk_bwd", w_t=True)[0]
    dmn = _matmul_res(dvx, wts["w_xv"], dmn, "xv_bwd", w_t=True)
    _, _, dg_mem = _rms_bwd(mem2, dmn, vecs["g_mem"], None, "norm_mem_bwd")
    dx1, dx1b, dg_xattn = _matmul_rms_bwd([dqx], [wts["w_xq"]], x1, vecs["g_xattn"], dx2, "xq_bwd")

    gw_out = jnp.concatenate([_matmul_tn(ya, dx1b, "gw_out_a"), _matmul_tn(yf, dx1b, "gw_out_f")], axis=0)
    token = on_grads("mid", dict(w_out=gw_out, w_xq=gw_xq, w_xk=gw_xk, w_xv=gw_xv, w_xo=gw_xo)) if hooks else None
    dy = _matmul(dx1b, w_out, "out_bwd", out_dtypes=(BF16,), w_t=True, after=token)[0]
    dz = _dil_bwd(zd, dy, ya, lses, n_batch, "dil_bwd")
    dz, dc = _fox_bwd(zf, of32, dy, lse_f, c_bc, c_row, dz, n_batch, "fox_bwd")
    dzg, db = _gate_bwd(dc.reshape(n_batch, N_HEADS, SEQ), sg, "gate_bwd")
    token = on_swapped("mid", dz) if hooks else None
    gw_pm = _matmul_tn(h1, dz, "gw_in_qkv", after=token)
    gw_in = jnp.concatenate([_pair_major_inv(gw_pm[:, :half]), _pair_major_inv(gw_pm[:, half:]),
                             _matmul_tn(h1, dzg, "gw_in_gate")[:, :N_HEADS]], axis=1)
    dx0, _, dg_mix = _matmul_rms_bwd([dz, dzg], [w_qkv, w_gate], x0, vecs["g_mix"], dx1, "in_bwd")

    gw = dict(w_in=gw_in, w_out=gw_out, w_xq=gw_xq, w_xk=gw_xk, w_xv=gw_xv, w_xo=gw_xo, w_up=gw_up, w_down=gw_down)
    gv = dict(g_mix=dg_mix, g_xattn=dg_xattn, g_mem=dg_mem, g_mlp=dg_mlp, g_final=dg_final, b_forget=db)
    return loss, dx0.reshape(x.shape), gw, gv


MESH = pl.DeviceIdType.MESH
ANY = pl.BlockSpec(memory_space=pl.ANY)


def _place():
    x, y, c = lax.axis_index("x"), lax.axis_index("y"), lax.axis_index("c")
    other_chips = [(1 - x, y), (x, 1 - y), (1 - x, 1 - y)]
    return x, y, c, other_chips


def _my_chip():
    return 2 * lax.axis_index("x") + lax.axis_index("y")


def _halves(rows, c, align):
    half = rows // 2
    assert rows % (2 * align) == 0, rows
    return pl.ds(pl.multiple_of(c * half, align), half), pl.ds(pl.multiple_of((1 - c) * half, align), half)


def _place_own(wall, pack):
    return lax.dynamic_update_slice(wall, pack[None], (_my_chip(), 0, 0))


HBM = pl.BlockSpec(memory_space=pltpu.HBM)
SEM = pl.BlockSpec(memory_space=pltpu.SEMAPHORE)
SPLIT_COPY = pltpu.CompilerParams(has_side_effects=pltpu.SideEffectType.DATAFLOW_SIDE_EFFECTING)


def _in_hbm(a):
    return pltpu.with_memory_space_constraint(a, pltpu.HBM)


def _start_call(start, src, land_shape, after, name):
    land = lax.empty(land_shape, src.dtype)

    def body(src_ref, land_ref, after_ref, send_sems, recv_sems, src_thru, land_thru, token):
        del after_ref, src_thru, land_thru
        start(src_ref, land_ref, send_sems, recv_sems)
        token[...] = jnp.zeros_like(token)

    return pl.pallas_call(
        body, name=name,
        out_shape=(pltpu.SemaphoreType.DMA((3,)), pltpu.SemaphoreType.DMA((3,)), pltpu.HBM(src.shape, src.dtype),
                   pltpu.HBM(land_shape, src.dtype), jax.ShapeDtypeStruct((8, LANES), F32)),
        in_specs=(HBM, HBM, ANY), out_specs=(SEM, SEM, HBM, HBM, pl.BlockSpec(memory_space=pltpu.VMEM)),
        input_output_aliases={0: 2, 1: 3}, compiler_params=SPLIT_COPY,
    )(_in_hbm(src), _in_hbm(land), after)


def _wait_call(body, started, after, name):
    send_sems, recv_sems, src, land, _ = started
    return pl.pallas_call(
        body, name=name,
        out_shape=(pltpu.HBM(src.shape, src.dtype), pltpu.HBM(land.shape, land.dtype)),
        in_specs=(HBM, HBM, SEM, SEM, ANY), out_specs=(HBM, HBM),
        input_output_aliases={0: 0, 1: 1}, compiler_params=SPLIT_COPY,
    )(src, land, send_sems, recv_sems, after)


def _gather_copies(p_ref, wall_ref, send_sems, recv_sems):
    x, y, c, chips = _place()
    me = 2 * x + y
    mine, _ = _halves(p_ref.shape[0], c, 16)
    out, back = [], []
    for k, chip in enumerate(chips):
        peer = dict(send_sem=send_sems.at[k], recv_sem=recv_sems.at[k], device_id=(chip[0], chip[1], c), device_id_type=MESH)
        out.append(pltpu.make_async_remote_copy(src_ref=p_ref.at[mine], dst_ref=wall_ref.at[me, mine], **peer))
        slab = wall_ref.at[2 * chip[0] + chip[1], mine]
        back.append(pltpu.make_async_remote_copy(src_ref=slab, dst_ref=slab, **peer))
    return out, back


def _gather_start(pack, after, name):
    def start(p_ref, wall_ref, send_sems, recv_sems):
        for cp in _gather_copies(p_ref, wall_ref, send_sems, recv_sems)[0]:
            cp.start()

    return _start_call(start, pack, (N_CHIPS,) + pack.shape, after, name)


def _gather_wait(started, after, name):
    def body(p_ref, wall_ref, send_sems, recv_sems, after_ref, p_dead, wall_out):
        del after_ref, p_dead, wall_out
        out, back = _gather_copies(p_ref, wall_ref, send_sems, recv_sems)
        for cp_out, cp_back in zip(out, back):
            cp_out.wait_send()
            cp_back.wait_recv()

    return _wait_call(body, started, after, name)


def _pass_on(wall, name):
    def body(w_in_ref, out_ref, send_sems, recv_sems):
        del w_in_ref
        x, y, c, chips = _place()
        mine, theirs = _halves(wall.shape[1], c, 16)
        sends = []
        for k, chip in enumerate(chips):
            slab = out_ref.at[2 * chip[0] + chip[1]]
            peer = dict(send_sem=send_sems.at[k], recv_sem=recv_sems.at[k], device_id=(x, y, 1 - c), device_id_type=MESH)
            cp = pltpu.make_async_remote_copy(src_ref=slab.at[mine], dst_ref=slab.at[mine], **peer)
            cp.start()
            sends.append((cp, pltpu.make_async_remote_copy(src_ref=slab.at[theirs], dst_ref=slab.at[theirs], **peer)))
        for cp, back in sends:
            back.wait_recv()
            cp.wait_send()

    return pl.pallas_call(
        body, in_specs=[ANY], out_specs=ANY, out_shape=jax.ShapeDtypeStruct(wall.shape, wall.dtype),
        scratch_shapes=[pltpu.SemaphoreType.DMA((3,))] * 2, input_output_aliases={0: 0}, name=name,
    )(wall)


def _swap_halves(g, name):
    half = g.shape[1] // 2

    def body(g_ref, out_ref, send_sem, recv_sem):
        x, y, c, _ = _place()
        _, theirs = _halves(g.shape[1], c, 8)
        cp = pltpu.make_async_remote_copy(src_ref=g_ref.at[:, theirs], dst_ref=out_ref, send_sem=send_sem, recv_sem=recv_sem,
                                          device_id=(x, y, 1 - c), device_id_type=MESH)
        cp.start()
        cp.wait()

    return pl.pallas_call(
        body, in_specs=[ANY], out_specs=ANY,
        out_shape=jax.ShapeDtypeStruct((N_CHIPS, half, g.shape[2]), F32),
        scratch_shapes=[pltpu.SemaphoreType.DMA, pltpu.SemaphoreType.DMA],
        name=name,
    )(g)


def _swap_copy(g_ref, land_ref, send_sems, recv_sems):
    x, y, c, _ = _place()
    _, theirs = _halves(g_ref.shape[1], c, 8)
    return pltpu.make_async_remote_copy(src_ref=g_ref.at[:, theirs], dst_ref=land_ref, send_sem=send_sems.at[0],
                                        recv_sem=recv_sems.at[0], device_id=(x, y, 1 - c), device_id_type=MESH)


def _swap_start(g, name):
    def start(g_ref, land_ref, send_sems, recv_sems):
        _swap_copy(g_ref, land_ref, send_sems, recv_sems).start()

    return _start_call(start, g, (N_CHIPS, g.shape[1] // 2, g.shape[2]), _core_index(), name)


def _swap_wait(started, after, name):
    def body(g_ref, land_ref, send_sems, recv_sems, after_ref, g_out, land_out):
        del after_ref, g_out, land_out
        cp = _swap_copy(g_ref, land_ref, send_sems, recv_sems)
        cp.wait_send()
        cp.wait_recv()

    return _wait_call(body, started, after, name)


def _core_index():
    return lax.axis_index("c").astype(jnp.int32).reshape(1)


def _row_tile(half):
    tile = max(t for t in range(16, 1025, 16) if half % t == 0)
    return tile, half // tile


def _add_sibling(g, got, name):
    half = g.shape[1] // 2
    tile, n_tiles = _row_tile(half)

    def body(c_ref, g_ref, got_ref, o_ref):
        o_ref[...] = (g_ref[...] + got_ref[...]).astype(BF16)

    width = g.shape[2]
    blk = pl.BlockSpec((None, tile, width), lambda s, i, c_ref: (s, i, 0))
    return pl.pallas_call(
        body,
        grid_spec=pltpu.PrefetchScalarGridSpec(
            num_scalar_prefetch=1, grid=(N_CHIPS, n_tiles),
            in_specs=[pl.BlockSpec((None, tile, width), lambda s, i, c_ref: (s, c_ref[0] * n_tiles + i, 0)), blk],
            out_specs=blk),
        out_shape=jax.ShapeDtypeStruct((N_CHIPS, half, width), BF16),
        name=name, compiler_params=_params("arbitrary", "arbitrary"),
    )(_core_index(), g, got)


def _exchange_copies(p_ref, land_ref, send_sems, recv_sems):
    x, y, c, chips = _place()
    me = 2 * x + y
    out, back = [], []
    for k, chip in enumerate(chips):
        peer = dict(send_sem=send_sems.at[k], recv_sem=recv_sems.at[k], device_id=(chip[0], chip[1], c), device_id_type=MESH)
        out.append(pltpu.make_async_remote_copy(src_ref=p_ref.at[2 * chip[0] + chip[1]], dst_ref=land_ref.at[me], **peer))
        slab = land_ref.at[2 * chip[0] + chip[1]]
        back.append(pltpu.make_async_remote_copy(src_ref=slab, dst_ref=slab, **peer))
    return out, back


def _with_own(got, part):
    me = _my_chip()
    return lax.dynamic_update_slice(got, lax.dynamic_slice(part, (me, 0, 0), (1,) + part.shape[1:]), (me, 0, 0))


def _exchange_start(part, name):
    def start(p_ref, land_ref, send_sems, recv_sems):
        for cp in _exchange_copies(p_ref, land_ref, send_sems, recv_sems)[0]:
            cp.start()

    return _start_call(start, part, part.shape, _core_index(), name)


def _exchange_wait(started, after, name):
    def body(p_ref, land_ref, send_sems, recv_sems, after_ref, p_dead, land_out):
        del after_ref, p_dead, land_out
        out, back = _exchange_copies(p_ref, land_ref, send_sems, recv_sems)
        for cp_out, cp_back in zip(out, back):
            cp_out.wait_send()
            cp_back.wait_recv()

    part, got = _wait_call(body, started, after, name)
    return _with_own(got, part)


def _sum_chips(parts, name):
    half, width = parts.shape[1:]
    tile, n_tiles = _row_tile(half)

    def body(c_ref, p0, p1, p2, p3, o_ref):
        f32 = lambda p: p[...].astype(F32)
        o_ref[...] = ((f32(p0) + f32(p1)) + f32(p2)) + f32(p3)

    def slab(s):
        return pl.BlockSpec((None, tile, width), lambda i, c_ref, s=s: (s, i, 0))

    return pl.pallas_call(
        body,
        grid_spec=pltpu.PrefetchScalarGridSpec(
            num_scalar_prefetch=1, grid=(n_tiles,),
            in_specs=[slab(s) for s in range(N_CHIPS)],
            out_specs=pl.BlockSpec((None, tile, width), lambda i, c_ref: (c_ref[0], i, 0))),
        out_shape=jax.ShapeDtypeStruct((2, half, width), F32),
        name=name, compiler_params=_params("arbitrary"),
    )(_core_index(), parts, parts, parts, parts)


def _share_halves(halves, name):
    def body(h_ref, out_ref, send_sem, recv_sem):
        del h_ref
        x, y, c, _ = _place()
        cp = pltpu.make_async_remote_copy(src_ref=out_ref.at[c], dst_ref=out_ref.at[c], send_sem=send_sem, recv_sem=recv_sem,
                                          device_id=(x, y, 1 - c), device_id_type=MESH)
        cp.start()
        pltpu.make_async_remote_copy(src_ref=out_ref.at[1 - c], dst_ref=out_ref.at[1 - c], send_sem=send_sem, recv_sem=recv_sem,
                                     device_id=(x, y, 1 - c), device_id_type=MESH).wait_recv()
        cp.wait_send()

    return pl.pallas_call(
        body, in_specs=[ANY], out_specs=ANY,
        out_shape=jax.ShapeDtypeStruct(halves.shape, halves.dtype),
        scratch_shapes=[pltpu.SemaphoreType.DMA] * 2,
        input_output_aliases={0: 0},
        name=name,
    )(halves)


def _reduce_parts(g, tag):
    return _add_sibling(g, _swap_halves(g, "swap_" + tag), "add_" + tag)


def _reduce_finish(got, tag):
    halves = _share_halves(_sum_chips(got, "sum_" + tag), "share_" + tag)
    return halves.reshape(2 * halves.shape[1], halves.shape[2])


SMALL_ROWS = 8


def _allreduce_small(v):
    def body(v_ref, out_ref, buf, send_sems, recv_sems):
        x, y, c, _ = _place()
        buf[4 * x + 2 * y + c] = v_ref[...]
        sends = []
        for k in range(1, N_DEV):
            px = 1 - x if k & 4 else x
            py = 1 - y if k & 2 else y
            pc = 1 - c if k & 1 else c
            cp = pltpu.make_async_remote_copy(src_ref=v_ref, dst_ref=buf.at[4 * x + 2 * y + c], send_sem=send_sems.at[k - 1],
                                              recv_sem=recv_sems.at[k - 1], device_id=(px, py, pc), device_id_type=MESH)
            cp.start()
            sends.append((cp, 4 * px + 2 * py + pc))
        for k, (cp, peer) in enumerate(sends):
            pltpu.make_async_remote_copy(src_ref=v_ref, dst_ref=buf.at[peer], send_sem=send_sems.at[k], recv_sem=recv_sems.at[k],
                                         device_id=(x, y, c), device_id_type=MESH).wait_recv()
        for cp, _ in sends:
            cp.wait_send()
        total = buf[0]
        for d in range(1, N_DEV):
            total = total + buf[d]
        out_ref[...] = total

    vmem = pl.BlockSpec(memory_space=pltpu.VMEM)
    return pl.pallas_call(
        body, in_specs=[vmem], out_specs=vmem,
        out_shape=jax.ShapeDtypeStruct(v.shape, v.dtype),
        scratch_shapes=[pltpu.VMEM((N_DEV,) + v.shape, v.dtype), pltpu.SemaphoreType.DMA((N_DEV - 1,)),
                        pltpu.SemaphoreType.DMA((N_DEV - 1,))],
        name="allreduce_small",
    )(v)


MATRICES = ("w_in", "w_out", "w_xq", "w_xk", "w_xv", "w_xo", "w_up", "w_down")
VECTORS = ("g_mix", "g_xattn", "g_mem", "g_mlp", "g_final", "b_forget")
WEIGHT_ORDER = ("g_mix", "w_in", "b_forget", "w_out", "g_xattn", "g_mem", "w_xq", "w_xk", "w_xv", "w_xo",
                "g_mlp", "w_up", "w_down", "g_final")
GROUPS = {"mlp": ("w_up", "w_down"), "mid": ("w_out", "w_xq", "w_xk", "w_xv", "w_xo"), "in": ("w_in",)}
LATE = GROUPS["mid"] + GROUPS["mlp"]
W_IN_SHARD = IN_WIDTH // N_CHIPS
SHARD_ROWS = {"w_out": 256, "w_xq": 256, "w_xk": 256, "w_xv": 256, "w_xo": 256, "w_up": 1024, "w_down": 1024}
PACK_ROWS = SHARD_ROWS
W_IN_PAD = -(-W_IN_SHARD // LANES) * LANES
ADAM_ROWS = 128


def _pack(parts, names):
    return jnp.concatenate([jnp.pad(parts[n], ((0, PACK_ROWS[n] - SHARD_ROWS[n]), (0, 0))) for n in names], axis=0)


def _unpack(a, names):
    out, pos = {}, 0
    for n in names:
        out[n] = a[..., pos:pos + SHARD_ROWS[n], :]
        pos += PACK_ROWS[n]
    return out


def _full_weights(wall, names):
    cols = lambda a: a.transpose(1, 0, 2).reshape(a.shape[1], -1)
    rows = lambda a: a.reshape(-1, a.shape[-1])
    if names == GROUPS["in"]:
        return {"w_in": cols(wall[:, :, :W_IN_SHARD])}
    return {n: cols(a) if n == "w_up" else rows(a) for n, a in _unpack(wall, names).items()}


def _shard_of(g, name, s):
    if name == "w_up":
        return g[:, s * D_MODEL:(s + 1) * D_MODEL]
    n = SHARD_ROWS[name]
    return g[s * n:(s + 1) * n]


def _pad_w_in(a):
    return jnp.pad(a, [(0, 0)] * (a.ndim - 1) + [(0, W_IN_PAD - W_IN_SHARD)])


def _pack_grads(gws, names):
    if names == GROUPS["in"]:
        return _pad_w_in(gws["w_in"].reshape(D_MODEL, N_CHIPS, W_IN_SHARD).transpose(1, 0, 2))
    return jnp.stack([_pack({n: _shard_of(gws[n], n, s) for n in names}, names) for s in range(N_CHIPS)])


def kernel(x, mem, g_mix, w_in, b_forget, w_out, g_xattn, g_mem, w_xq, w_xk, w_xv, w_xo, g_mlp, w_up, w_down, g_final, loss_target, m_g_mix, m_w_in, m_b_forget, m_w_out, m_g_xattn, m_g_mem, m_w_xq, m_w_xk, m_w_xv, m_w_xo, m_g_mlp, m_w_up, m_w_down, m_g_final, v_g_mix, v_w_in, v_b_forget, v_w_out, v_g_xattn, v_g_mem, v_w_xq, v_w_xk, v_w_xv, v_w_xo, v_g_mlp, v_w_up, v_w_down, v_g_final):
    given = dict(locals())
    weights = {n: given[n] for n in WEIGHT_ORDER}
    vecs = {n: weights[n] for n in VECTORS}

    shard = {n: weights[n].astype(BF16) for n in MATRICES}
    in_started = _gather_start(_pad_w_in(shard["w_in"]), _core_index(), "gather_in_start")
    late_pack = _pack(shard, LATE)
    in_pack, in_wall = _gather_wait(in_started, late_pack, "gather_in_wait")
    in_wall = _place_own(_pass_on(in_wall, "gather_in_pass"), in_pack)
    late = _gather_start(late_pack, in_wall, "gather_late_start")
    w_in_full = _full_weights(in_wall, GROUPS["in"])["w_in"]

    def late_weights(after):
        pack, wall = _gather_wait(late, after, "gather_late_wait")
        return _full_weights(_place_own(_pass_on(wall, "gather_late_pass"), pack), LATE)

    started = {}

    swapping = {}

    def on_grads(group, gws):
        packed = gws if group == "mlp" else _pack_grads(gws, GROUPS[group])
        swapping[group] = _swap_start(packed, "swap_%s_start" % group)
        return swapping[group][4]

    def on_swapped(group, after):
        g, got = _swap_wait(swapping[group], after, "swap_%s_wait" % group)
        started[group] = _exchange_start(_add_sibling(g, got, "add_" + group), "exchange_%s_start" % group)
        return started[group][4]

    loss, grad_x, gw, gv = _local_step(x, mem, loss_target, vecs, w_in_full, late_weights, (on_grads, on_swapped))

    part = _reduce_parts(_pack_grads(gw, GROUPS["in"]), "in")
    started["in"] = _exchange_start(part, "exchange_in_start")
    grads, delta, new_m, new_v = {}, {}, {}, {}

    def finish(group, after):
        got = _exchange_wait(started[group], after, "exchange_%s_wait" % group)
        done = _reduce_finish(got, group)
        for n, a in ({"w_in": done[:, :W_IN_SHARD]} if group == "in" else _unpack(done, GROUPS[group])).items():
            grads[n] = a.reshape(weights[n].shape)
            delta[n], new_m[n], new_v[n] = _adamw(weights[n], grads[n], given["m_" + n], given["v_" + n], "adamw_" + n, ADAM_ROWS)
        return new_v[GROUPS[group][-1]]

    after = finish("mlp", started["in"][4])
    after = finish("mid", after)

    row = lambda a: jnp.pad(a.reshape(-1), (0, D_MODEL - a.size)).reshape(1, D_MODEL)
    small = jnp.concatenate([gv[n] for n in VECTORS[:5]] + [row(gv["b_forget"][:, 0]), row(loss[0, :1]),
                             jnp.zeros((1, D_MODEL), F32)], axis=0)
    small = _allreduce_small(small)
    for k, n in enumerate(VECTORS[:5]):
        grads[n] = small[k]
    grads["b_forget"] = small[5, :N_HEADS]
    loss_total = small[6, 0]
    finish("in", after)

    stack = lambda prefix: jnp.concatenate([row(given[prefix + n]) for n in VECTORS] + [jnp.zeros((2, D_MODEL), F32)], axis=0)
    g_small = jnp.concatenate([small[:6], jnp.zeros((2, D_MODEL), F32)], axis=0)
    d, m1, v1 = _adamw(stack(""), g_small, stack("m_"), stack("v_"), "adamw_vectors", SMALL_ROWS)
    for k, n in enumerate(VECTORS):
        width = weights[n].shape[0]
        delta[n], new_m[n], new_v[n] = d[k, :width], m1[k, :width], v1[k, :width]

    return (loss_total, grad_x, *[grads[n] for n in WEIGHT_ORDER], *[delta[n] for n in WEIGHT_ORDER],
            *[new_m[n] for n in WEIGHT_ORDER], *[new_v[n] for n in WEIGHT_ORDER])
```

```python
import functools
import math

import jax
import jax.numpy as jnp
from jax import lax
from jax.experimental import pallas as pl
from jax.experimental.pallas import tpu as pltpu

F32 = jnp.float32
BF16 = jnp.bfloat16

D_MODEL = 1024
SEQ = 2048
N_MEM = 256
HEAD_DIM = 64
N_HEADS = 8
MIX_HALF = N_HEADS * HEAD_DIM
QKV_WIDTH = 6 * MIX_HALF
IN_WIDTH = QKV_WIDTH + N_HEADS
GATE_PAD = 128
BLOCK = 128
DILATIONS = (1, 4, 16)
X_HEADS = 4
X_HEAD_DIM = 256
D_FF = 4096
EPS = 1e-6
NEG = -1e30
ATT_SCALE = 1.0 / math.sqrt(HEAD_DIM)
X_SCALE = 1.0 / math.sqrt(X_HEAD_DIM)
LANES = 128
N_CHIPS = 4
N_DEV = 8

ADAM_LR = 0.001
ADAM_B1 = 0.9
ADAM_B2 = 0.999
ADAM_EPS = 1e-08
ADAM_WD = 0.01
ADAM_STEP = 10

VMEM_LIMIT = 48 * 1024 * 1024


def _params(*sem):
    return pltpu.CompilerParams(dimension_semantics=sem or None, vmem_limit_bytes=VMEM_LIMIT)


def _dot(a, b):
    return jnp.dot(a, b, preferred_element_type=F32)


def _dot_nt(a, b):
    return lax.dot_general(a, b, (((1,), (1,)), ((), ())), preferred_element_type=F32)


def _dot_tn(a, b):
    return lax.dot_general(a, b, (((0,), (0,)), ((), ())), preferred_element_type=F32)


def _dot_exact(x, e):
    hi = x.astype(BF16)
    r1 = x - hi.astype(F32)
    mid = r1.astype(BF16)
    lo = (r1 - mid.astype(F32)).astype(BF16)
    return _dot(hi, e) + _dot(mid, e) + _dot(lo, e)


def _head_mask(e):
    lane = lax.broadcasted_iota(jnp.int32, (1, LANES), 1)
    return (lane >= HEAD_DIM * e) & (lane < HEAD_DIM * (e + 1))


def _matmul(a, w, name, out_dtypes=(F32,), extras=(), epilogue=None, tm=1024, tn=1024, w_t=False, after=None):
    m, k = a.shape
    n = w.shape[0] if w_t else w.shape[1]
    tm, tn = min(tm, m), min(tn, n)
    assert m % tm == 0 and n % tn == 0, (name, a.shape, w.shape)
    n_ex = len(extras)
    order = () if after is None else (after,)

    def body(a_ref, w_ref, *rest):
        rest = rest[len(order):]
        acc = (_dot_nt if w_t else _dot)(a_ref[...], w_ref[...])
        res = (acc,) if epilogue is None else epilogue(acc, *[r[...] for r in rest[:n_ex]])
        for o_ref, r in zip(rest[n_ex:], res):
            o_ref[...] = r.astype(o_ref.dtype)

    tile = pl.BlockSpec((tm, tn), lambda i, j: (i, j))
    w_spec = pl.BlockSpec((tn, k), lambda i, j: (j, 0)) if w_t else pl.BlockSpec((k, tn), lambda i, j: (0, j))
    return pl.pallas_call(
        body, grid=(m // tm, n // tn),
        in_specs=[pl.BlockSpec((tm, k), lambda i, j: (i, 0)), w_spec] + [pl.BlockSpec(memory_space=pl.ANY)] * len(order) + [tile] * n_ex,
        out_specs=[tile] * len(out_dtypes),
        out_shape=[jax.ShapeDtypeStruct((m, n), dt) for dt in out_dtypes],
        name=name, compiler_params=_params("parallel", "arbitrary"),
    )(a, w, *order, *extras)


def _matmul_res(a, w, res, name, w_t=False):
    return _matmul(a, w, name, extras=(res,), epilogue=lambda acc, r: (r + acc,), w_t=w_t)[0]


def _matmul_tn(x, y, name, tm=1024, tn=1024, tk=1024, packed=None, after=None):
    t, m = x.shape
    _, n = y.shape
    tm, tn, tk = min(tm, m), min(tn, n), min(tk, t)
    assert m % tm == 0 and n % tn == 0 and t % tk == 0, (name, x.shape, y.shape)
    shape, place, into = packed or ((m, n), None, None)

    def body(x_ref, y_ref, *rest):
        o_ref = rest[-1]

        @pl.when(pl.program_id(2) == 0)
        def _():
            o_ref[...] = jnp.zeros_like(o_ref)

        o_ref[...] += _dot_tn(x_ref[...], y_ref[...])

    out_spec = (pl.BlockSpec((tm, tn), lambda i, j, k: (i, j)) if place is None
                else pl.BlockSpec((None, tm, tn), lambda i, j, k: place(i, j)))
    return pl.pallas_call(
        body, grid=(m // tm, n // tn, t // tk),
        in_specs=[pl.BlockSpec((tk, tm), lambda i, j, k: (k, i)), pl.BlockSpec((tk, tn), lambda i, j, k: (k, j))]
        + [pl.BlockSpec(memory_space=pl.ANY)] * ((into is not None) + (after is not None)),
        out_specs=out_spec, out_shape=jax.ShapeDtypeStruct(shape, F32),
        input_output_aliases={} if into is None else {2: 0},
        name=name, compiler_params=_params("parallel", "parallel", "arbitrary"),
    )(x, y, *(() if into is None else (into,)), *(() if after is None else (after,)))


def _rmsnorm(x, g, name, tm=512):
    t, d = x.shape
    tm = min(tm, t)

    def body(x_ref, g_ref, h_ref):
        xv = x_ref[...]
        r = lax.rsqrt(jnp.mean(xv * xv, axis=-1, keepdims=True) + EPS)
        h_ref[...] = (xv * r * g_ref[...]).astype(BF16)

    return pl.pallas_call(
        body, grid=(t // tm,),
        in_specs=[pl.BlockSpec((tm, d), lambda i: (i, 0)), pl.BlockSpec((1, d), lambda i: (0, 0))],
        out_specs=pl.BlockSpec((tm, d), lambda i: (i, 0)),
        out_shape=jax.ShapeDtypeStruct((t, d), BF16),
        name=name, compiler_params=_params("arbitrary"),
    )(x, g.reshape(1, d))


def _in_proj(x, g, w_all, name, tm=512):
    t, d = x.shape
    half = 3 * MIX_HALF

    def body(x_ref, g_ref, w_ref, h_ref, zd_ref, zf_ref, gate_ref):
        xv = x_ref[...]
        r = lax.rsqrt(jnp.mean(xv * xv, axis=-1, keepdims=True) + EPS)
        h = (xv * r * g_ref[...]).astype(BF16)
        h_ref[...] = h
        zd_ref[...] = _dot(h, w_ref[:, 0:half])
        zf_ref[...] = _dot(h, w_ref[:, half:2 * half]).astype(BF16)
        gate_ref[...] = _dot(h, w_ref[:, 2 * half:])

    row = lambda width: pl.BlockSpec((tm, width), lambda i: (i, 0))
    return pl.pallas_call(
        body, grid=(t // tm,),
        in_specs=[row(d), pl.BlockSpec((1, d), lambda i: (0, 0)), pl.BlockSpec(w_all.shape, lambda i: (0, 0))],
        out_specs=[row(d), row(half), row(half), row(GATE_PAD)],
        out_shape=[jax.ShapeDtypeStruct((t, d), BF16), jax.ShapeDtypeStruct((t, half), F32),
                   jax.ShapeDtypeStruct((t, half), BF16), jax.ShapeDtypeStruct((t, GATE_PAD), F32)],
        name=name, compiler_params=_params("arbitrary"),
    )(x, g.reshape(1, d), w_all)


def _rms_bwd_tile(xv, dh, g):
    d = xv.shape[-1]
    r = lax.rsqrt(jnp.mean(xv * xv, axis=-1, keepdims=True) + EPS)
    dyg = dh * g
    proj = jnp.sum(dyg * xv, axis=-1, keepdims=True)
    dx = r * dyg - xv * (r * r * r * (1.0 / d)) * proj
    return dx, dh * (xv * r)


def _rms_bwd(x, dh, g, dres, name, tm=512):
    t, d = x.shape
    tm = min(tm, t)
    has_res = dres is not None

    def body(x_ref, dh_ref, g_ref, *rest):
        if has_res:
            res_ref, dx_ref, dxb_ref, dg_ref = rest
        else:
            dx_ref, dxb_ref, dg_ref = rest
        dx, dg_rows = _rms_bwd_tile(x_ref[...], dh_ref[...], g_ref[...])
        if has_res:
            dx = res_ref[...] + dx
        dx_ref[...] = dx
        dxb_ref[...] = dx.astype(BF16)

        @pl.when(pl.program_id(0) == 0)
        def _():
            dg_ref[...] = jnp.zeros_like(dg_ref)

        dg_ref[...] += jnp.sum(dg_rows, axis=0, keepdims=True)

    row = pl.BlockSpec((tm, d), lambda i: (i, 0))
    vec = pl.BlockSpec((1, d), lambda i: (0, 0))
    return pl.pallas_call(
        body, grid=(t // tm,),
        in_specs=[row, row, vec] + ([row] if has_res else []),
        out_specs=[row, row, vec],
        out_shape=[jax.ShapeDtypeStruct((t, d), F32), jax.ShapeDtypeStruct((t, d), BF16), jax.ShapeDtypeStruct((1, d), F32)],
        name=name, compiler_params=_params("arbitrary"),
    )(x, dh, g.reshape(1, d), *((dres,) if has_res else ()))


def _row_dots(a_refs, w_refs, w_t):
    acc = None
    for a_ref, w_ref in zip(a_refs, w_refs):
        part = (_dot_nt if w_t else _dot)(a_ref[...], w_ref[...])
        acc = part if acc is None else acc + part
    return acc


def _row_specs(a_parts, w_parts, tm):
    specs = [pl.BlockSpec((tm, a.shape[1]), lambda i: (i, 0)) for a in a_parts]
    return specs + [pl.BlockSpec(w.shape, lambda i: (0, 0)) for w in w_parts]


def _matmul_res_norm(a_parts, w_parts, res, g, name, tm=512):
    t, d = res.shape
    n = len(a_parts)

    def body(*refs):
        res_ref, g_ref, x_ref, h_ref = refs[2 * n:]
        xv = res_ref[...] + _row_dots(refs[:n], refs[n:2 * n], False)
        x_ref[...] = xv
        r = lax.rsqrt(jnp.mean(xv * xv, axis=-1, keepdims=True) + EPS)
        h_ref[...] = (xv * r * g_ref[...]).astype(BF16)

    row = pl.BlockSpec((tm, d), lambda i: (i, 0))
    return pl.pallas_call(
        body, grid=(t // tm,),
        in_specs=_row_specs(a_parts, w_parts, tm) + [row, pl.BlockSpec((1, d), lambda i: (0, 0))],
        out_specs=[row, row],
        out_shape=[jax.ShapeDtypeStruct((t, d), F32), jax.ShapeDtypeStruct((t, d), BF16)],
        name=name, compiler_params=_params("arbitrary"),
    )(*a_parts, *w_parts, res, g.reshape(1, d))


def _matmul_rms_bwd(a_parts, w_parts, x, g, dres, name, tm=512, after=None):
    t, d = x.shape
    n = len(a_parts)
    order = () if after is None else (after,)

    def body(*refs):
        x_ref, g_ref, res_ref = refs[2 * n:2 * n + 3]
        dx_ref, dxb_ref, dg_ref = refs[2 * n + 3 + len(order):]
        dx, dg_rows = _rms_bwd_tile(x_ref[...], _row_dots(refs[:n], refs[n:2 * n], True), g_ref[...])
        dx = res_ref[...] + dx
        dx_ref[...] = dx
        dxb_ref[...] = dx.astype(BF16)

        @pl.when(pl.program_id(0) == 0)
        def _():
            dg_ref[...] = jnp.zeros_like(dg_ref)

        dg_ref[...] += jnp.sum(dg_rows, axis=0, keepdims=True)

    row = pl.BlockSpec((tm, d), lambda i: (i, 0))
    vec = pl.BlockSpec((1, d), lambda i: (0, 0))
    return pl.pallas_call(
        body, grid=(t // tm,),
        in_specs=_row_specs(a_parts, w_parts, tm) + [row, vec, row] + [pl.BlockSpec(memory_space=pl.ANY)] * len(order),
        out_specs=[row, row, vec],
        out_shape=[jax.ShapeDtypeStruct((t, d), F32), jax.ShapeDtypeStruct((t, d), BF16), jax.ShapeDtypeStruct((1, d), F32)],
        name=name, compiler_params=_params("arbitrary"),
    )(*a_parts, *w_parts, x, g.reshape(1, d), dres, *order)


def _loss_bwd(a, w, res, g, target, name, tm=512):
    t, d = res.shape

    def body(a_ref, w_ref, x_ref, g_ref, t_ref, loss_ref, dx_ref, dxb_ref, dg_ref):
        xv = x_ref[...] + _dot(a_ref[...], w_ref[...])
        gv = g_ref[...]
        r = lax.rsqrt(jnp.mean(xv * xv, axis=-1, keepdims=True) + EPS)
        err = xv * r * gv - t_ref[...]
        dx, dg_rows = _rms_bwd_tile(xv, err * (1.0 / d), gv)
        dx_ref[...] = dx
        dxb_ref[...] = dx.astype(BF16)

        @pl.when(pl.program_id(0) == 0)
        def _():
            dg_ref[...] = jnp.zeros_like(dg_ref)
            loss_ref[...] = jnp.zeros_like(loss_ref)

        dg_ref[...] += jnp.sum(dg_rows, axis=0, keepdims=True)
        part = jnp.sum(jnp.sum(err * err, axis=0, keepdims=True), axis=1, keepdims=True) * (0.5 / d)
        loss_ref[...] += jnp.broadcast_to(part, loss_ref.shape)

    row = pl.BlockSpec((tm, d), lambda i: (i, 0))
    vec = pl.BlockSpec((1, d), lambda i: (0, 0))
    return pl.pallas_call(
        body, grid=(t // tm,),
        in_specs=_row_specs([a], [w], tm) + [row, vec, row],
        out_specs=[pl.BlockSpec((1, LANES), lambda i: (0, 0)), row, row, vec],
        out_shape=[jax.ShapeDtypeStruct((1, LANES), F32), jax.ShapeDtypeStruct((t, d), F32),
                   jax.ShapeDtypeStruct((t, d), BF16), jax.ShapeDtypeStruct((1, d), F32)],
        name=name, compiler_params=_params("arbitrary"),
    )(a, w, res, g.reshape(1, d), target)


def _tri(upper):
    r = lax.broadcasted_iota(jnp.int32, (LANES, LANES), 0)
    c = lax.broadcasted_iota(jnp.int32, (LANES, LANES), 1)
    return jnp.where((r <= c) if upper else (r >= c), 1.0, 0.0).astype(BF16)


def _gate_fwd(gate, b_pad, n_batch, name):
    s = SEQ
    nblk = s // LANES

    def body(g_ref, b_ref, cbc_ref, crow_ref, sg_ref, ct_ref):
        gz = g_ref[...] + b_ref[...]
        logf = jnp.minimum(gz, 0.0) - jnp.log(1.0 + jnp.exp(-jnp.abs(gz)))
        logf_t = logf.T
        sg_ref[...] = (1.0 / (1.0 + jnp.exp(gz))).T[0:N_HEADS]
        upper = _tri(True)
        carry = jnp.zeros((LANES, 1), F32)
        for blk in range(nblk):
            seg = _dot_exact(logf_t[:, blk * LANES:(blk + 1) * LANES], upper) + carry
            carry = seg[:, LANES - 1:LANES]
            ct_ref[:, blk * LANES:(blk + 1) * LANES] = seg
        ct = ct_ref[...]
        crow_ref[...] = ct[0:N_HEADS]
        c_col = ct.T
        lane = lax.broadcasted_iota(jnp.int32, (1, MIX_HALF), 1)
        acc = jnp.zeros((s, MIX_HALF), F32)
        for h in range(N_HEADS):
            acc = jnp.where((lane >= HEAD_DIM * h) & (lane < HEAD_DIM * (h + 1)), c_col[:, h:h + 1], acc)
        cbc_ref[...] = acc

    return pl.pallas_call(
        body, grid=(n_batch,),
        in_specs=[pl.BlockSpec((s, GATE_PAD), lambda b: (b, 0)), pl.BlockSpec((1, GATE_PAD), lambda b: (0, 0))],
        out_specs=[pl.BlockSpec((s, MIX_HALF), lambda b: (b, 0)),
                   pl.BlockSpec((None, N_HEADS, s), lambda b: (b, 0, 0)),
                   pl.BlockSpec((None, N_HEADS, s), lambda b: (b, 0, 0))],
        out_shape=[jax.ShapeDtypeStruct((n_batch * s, MIX_HALF), F32),
                   jax.ShapeDtypeStruct((n_batch, N_HEADS, s), F32),
                   jax.ShapeDtypeStruct((n_batch, N_HEADS, s), F32)],
        scratch_shapes=[pltpu.VMEM((LANES, s), F32)],
        name=name, compiler_params=_params("arbitrary"),
    )(gate, b_pad)


def _gate_bwd(dc, sg, name):
    n_batch, _, s = dc.shape
    nblk = s // LANES

    def body(dc_ref, sg_ref, dz_ref, db_ref, dt_ref):
        lower = _tri(False)
        dcv = dc_ref[...]
        carry = jnp.zeros((N_HEADS, 1), F32)
        dt_ref[...] = jnp.zeros_like(dt_ref)
        for blk in reversed(range(nblk)):
            seg = _dot_exact(dcv[:, blk * LANES:(blk + 1) * LANES], lower) + carry
            carry = seg[:, 0:1]
            dt_ref[0:N_HEADS, blk * LANES:(blk + 1) * LANES] = seg * sg_ref[:, blk * LANES:(blk + 1) * LANES]
        dg_t = dt_ref[...]
        dz_ref[...] = dg_t.T.astype(BF16)

        @pl.when(pl.program_id(0) == 0)
        def _():
            db_ref[...] = jnp.zeros_like(db_ref)

        db_ref[...] += jnp.broadcast_to(jnp.sum(dg_t[0:N_HEADS], axis=1, keepdims=True), db_ref.shape)

    return pl.pallas_call(
        body, grid=(n_batch,),
        in_specs=[pl.BlockSpec((None, N_HEADS, s), lambda b: (b, 0, 0)), pl.BlockSpec((None, N_HEADS, s), lambda b: (b, 0, 0))],
        out_specs=[pl.BlockSpec((s, GATE_PAD), lambda b: (b, 0)), pl.BlockSpec((N_HEADS, LANES), lambda b: (0, 0))],
        out_shape=[jax.ShapeDtypeStruct((n_batch * s, GATE_PAD), BF16), jax.ShapeDtypeStruct((N_HEADS, LANES), F32)],
        scratch_shapes=[pltpu.VMEM((LANES, s), F32)],
        name=name, compiler_params=_params("arbitrary"),
    )(dc, sg)


FOX_BQ = 512
FOX_BK = 512
FOX_STRIP = 512
PAIR_WIDTH = 3 * LANES
N_PAIRS = N_HEADS // 2


def _pair_major(w):
    return w.reshape(w.shape[0], 3, N_PAIRS, LANES).transpose(0, 2, 1, 3).reshape(w.shape[0], 3 * MIX_HALF)


def _pair_major_inv(w):
    return w.reshape(w.shape[0], N_PAIRS, 3, LANES).transpose(0, 2, 1, 3).reshape(w.shape[0], 3 * MIX_HALF)


def _causal(i, j, bq, bk):
    qpos = i * bq + lax.broadcasted_iota(jnp.int32, (bq, 1), 0)
    kpos = j * bk + lax.broadcasted_iota(jnp.int32, (1, bk), 1)
    return kpos <= qpos


def _split_bf16(p):
    hi = p.astype(BF16)
    return hi, (p - hi.astype(F32)).astype(BF16)


def _fox_fwd(zf, c_bc, c_row, n_batch, name):
    s, bq, bk = SEQ, FOX_BQ, FOX_BK
    nq = s // bq
    t = n_batch * s

    n_strip = bq // FOX_STRIP

    def body(q_ref, k_ref, v_ref, cq_ref, cr_ref, o_ref, o32_ref, lse_ref):
        hp = pl.program_id(1)
        strips = [slice(r * FOX_STRIP, (r + 1) * FOX_STRIP) for r in range(n_strip)]
        chains = [(e, r) for e in range(2) for r in range(n_strip)]
        qh, cq = {}, {}
        for e, r in chains:
            q = q_ref[strips[r], :] * ATT_SCALE
            qh[e, r] = jnp.where(_head_mask(e), q, jnp.zeros_like(q))
            cq[e, r] = cq_ref[strips[r], HEAD_DIM * e:HEAD_DIM * e + 1]

        def step(i, j, carry, masked):
            rows = pl.ds(j * bk, bk)
            kj, vj = k_ref[rows, :], v_ref[rows, :]
            ck = [cr_ref[pl.ds(2 * hp + e, 1), rows] for e in range(2)]
            out = []
            scores = [_dot_nt(qh[e, r], kj) for e, r in chains]
            for n, (e, r) in enumerate(chains):
                m, l, acc = carry[3 * n:3 * n + 3]
                sc = scores[n] + (cq[e, r] - ck[e])
                if masked:
                    qpos = i * bq + r * FOX_STRIP + lax.broadcasted_iota(jnp.int32, (FOX_STRIP, 1), 0)
                    kpos = j * bk + lax.broadcasted_iota(jnp.int32, (1, bk), 1)
                    sc = jnp.where(kpos <= qpos, sc, NEG)
                m_new = jnp.maximum(m, jnp.max(sc, axis=1, keepdims=True))
                alpha = jnp.exp(m - m_new)
                p = jnp.exp(sc - m_new)
                p_hi, p_lo = _split_bf16(p)
                out += [m_new, alpha * l + jnp.sum(p, axis=1, keepdims=True), alpha * acc + (_dot(p_hi, vj) + _dot(p_lo, vj))]
            return tuple(out)

        def run(i):
            carry = (jnp.full((FOX_STRIP, 1), NEG, F32), jnp.zeros((FOX_STRIP, 1), F32), jnp.zeros((FOX_STRIP, LANES), F32)) * len(chains)
            n_clear = (i * bq) // bk
            for j in range((i * bq + bq + bk - 1) // bk):
                carry = step(i, j, carry, masked=j >= n_clear)
            for r in range(n_strip):
                outs = [carry[3 * (e * n_strip + r) + 2] / carry[3 * (e * n_strip + r) + 1] for e in range(2)]
                lses = [carry[3 * (e * n_strip + r)] + jnp.log(carry[3 * (e * n_strip + r) + 1]) for e in range(2)]
                o = jnp.where(_head_mask(0), outs[0], outs[1])
                o_ref[strips[r], :] = o.astype(BF16)
                o32_ref[strips[r], :] = o
                lse_ref[strips[r], :] = jnp.where(_head_mask(0), lses[0], lses[1])

        for k in range(nq):
            pl.when(pl.program_id(2) == k)(functools.partial(run, k))

    def col(c0):
        return lambda b, hp, i: (b, 3 * hp + c0)

    blk = pl.BlockSpec((bq, LANES), lambda b, hp, i: (b * nq + i, hp))
    return pl.pallas_call(
        body, grid=(n_batch, N_PAIRS, nq),
        in_specs=[pl.BlockSpec((bq, LANES), lambda b, hp, i: (b * nq + i, 3 * hp)),
                  pl.BlockSpec((s, LANES), col(1)), pl.BlockSpec((s, LANES), col(2)), blk,
                  pl.BlockSpec((None, N_HEADS, s), lambda b, hp, i: (b, 0, 0))],
        out_specs=[blk, blk, blk],
        out_shape=[jax.ShapeDtypeStruct((t, MIX_HALF), BF16), jax.ShapeDtypeStruct((t, MIX_HALF), F32),
                   jax.ShapeDtypeStruct((t, MIX_HALF), F32)],
        name=name, compiler_params=_params("parallel", "parallel", "arbitrary"),
    )(zf, zf, zf, c_bc, c_row)


def _fox_bwd(zf, o32, dy, lse, c_bc, c_row, dz, n_batch, name):
    s, bq, bk = SEQ, FOX_BQ, FOX_BK
    nq, nk = s // bq, s // bk

    def body(q_ref, k_ref, v_ref, o_ref, do_ref, lse_ref, cq_ref, cr_ref, dz_in, dz_ref, dc_ref, dq_acc):
        del dz_in
        hp = pl.program_id(1)

        @pl.when(pl.program_id(2) == 0)
        def _():
            dq_acc[...] = jnp.zeros_like(dq_acc)

        kj, vj = k_ref[...], v_ref[...]
        km = [jnp.where(_head_mask(e), kj, jnp.zeros_like(kj)) for e in range(2)]

        def step(i, j, ck, carry, masked):
            rows = pl.ds(i * bq, bq)
            qi, doi = q_ref[rows, :] * ATT_SCALE, do_ref[rows, :]
            prod = doi.astype(F32) * o_ref[rows, :]
            out = []
            dq = jnp.zeros((bq, LANES), F32)
            for e in range(2):
                dk_a, dv_a, dc_a = carry[3 * e:3 * e + 3]
                mask = _head_mask(e)
                lane0 = HEAD_DIM * e
                dom = jnp.where(mask, doi, jnp.zeros_like(doi))
                delta = jnp.sum(jnp.where(mask, prod, 0.0), axis=1, keepdims=True)
                sc = _dot_nt(qi, km[e]) + (cq_ref[rows, lane0:lane0 + 1] - ck[e])
                if masked:
                    sc = jnp.where(_causal(i, j, bq, bk), sc, NEG)
                p = jnp.exp(sc - lse_ref[rows, lane0:lane0 + 1])
                ds = p * (_dot_nt(dom, vj) - delta)
                dsb = ds.astype(BF16)
                dq = dq + _dot(dsb, km[e])
                out += [dk_a + _dot_tn(dsb, qi), dv_a + _dot_tn(p.astype(BF16), dom), dc_a - jnp.sum(ds, axis=0, keepdims=True)]
            dq_acc[rows, :] += dq * ATT_SCALE
            return tuple(out)

        def run(j):
            cols = pl.ds(j * bk, bk)
            ck = [cr_ref[pl.ds(2 * hp + e, 1), cols] for e in range(2)]
            carry = (jnp.zeros((bk, LANES), F32), jnp.zeros((bk, LANES), F32), jnp.zeros((1, bk), F32)) * 2
            n_diag = (j * bk + bk + bq - 1) // bq
            for i in range((j * bk) // bq, nq):
                carry = step(i, j, ck, carry, masked=i < n_diag)
            for e in range(2):
                dc_ref[e:e + 1, :] = carry[3 * e + 2]
            dz_ref[cols, LANES:2 * LANES] = jnp.where(_head_mask(0), carry[0], carry[3]).astype(BF16)
            dz_ref[cols, 2 * LANES:3 * LANES] = (carry[1] + carry[4]).astype(BF16)
            if j == nk - 1:
                dz_ref[:, 0:LANES] = dq_acc[...].astype(BF16)

        for k in range(nk):
            pl.when(pl.program_id(2) == k)(functools.partial(run, k))

    def seq(idx):
        return pl.BlockSpec((s, LANES), lambda b, hp, j: (b, idx(hp)))

    def kblk(c0):
        return pl.BlockSpec((bk, LANES), lambda b, hp, j: (b * nk + j, 3 * hp + c0))

    return pl.pallas_call(
        body, grid=(n_batch, N_PAIRS, nk),
        in_specs=[seq(lambda hp: 3 * hp), kblk(1), kblk(2), seq(lambda hp: hp), seq(lambda hp: N_PAIRS + hp),
                  seq(lambda hp: hp), seq(lambda hp: hp),
                  pl.BlockSpec((None, N_HEADS, s), lambda b, hp, j: (b, 0, 0)), pl.BlockSpec(memory_space=pl.ANY)],
        out_specs=[pl.BlockSpec((s, PAIR_WIDTH), lambda b, hp, j: (b, N_PAIRS + hp)),
                   pl.BlockSpec((None, None, 2, bk), lambda b, hp, j: (b, hp, 0, j))],
        out_shape=[jax.ShapeDtypeStruct(dz.shape, dz.dtype), jax.ShapeDtypeStruct((n_batch, N_PAIRS, 2, s), F32)],
        scratch_shapes=[pltpu.VMEM((s, LANES), F32)],
        input_output_aliases={8: 0},
        name=name, compiler_params=_params("parallel", "parallel", "arbitrary"),
    )(zf, zf, zf, o32, dy, lse, c_bc, c_row, dz)


def _dil_bias(slope, dil):
    qi = lax.broadcasted_iota(jnp.int32, (BLOCK, 2 * BLOCK), 0)
    kj = lax.broadcasted_iota(jnp.int32, (BLOCK, 2 * BLOCK), 1)
    delta = qi + BLOCK - kj
    return jnp.where((delta >= 0) & (delta <= BLOCK), (-slope * dil) * delta.astype(F32), NEG)


def _alibi_slope(hp, e):
    slope = jnp.float32(0.0)
    for k in range(N_PAIRS):
        slope = jnp.where(hp == k, jnp.float32(2.0 ** -(2 * k + e + 1)), slope)
    return slope


def _first_block_bias(bias):
    return jnp.where(lax.broadcasted_iota(jnp.int32, bias.shape, 1) < BLOCK, NEG, bias)


def _fill_bias(bias_scr, hp):
    for di, dil in enumerate(DILATIONS):
        for e in range(2):
            bias_scr[2 * di + e] = _dil_bias(_alibi_slope(hp, e), dil)


def _pair_specs(rows):
    return [pl.BlockSpec((rows, LANES), lambda b, hp, c0=c0: (b, 3 * hp + c0)) for c0 in range(3)]


def _strided(start, size, dil):
    return pl.ds(start, size) if dil == 1 else pl.ds(start, size, stride=dil)


QUARTER = SEQ // 4


def _to_quarters(src, dst):
    for r in range(4):
        dst[r * QUARTER:(r + 1) * QUARTER, :] = src[pl.ds(r, QUARTER, stride=4), :]


def _from_quarters(src, dst):
    for r in range(4):
        dst[pl.ds(r, QUARTER, stride=4), :] = src[r * QUARTER:(r + 1) * QUARTER, :]


def _mix_weights(l1, l2, l3):
    m = jnp.maximum(jnp.maximum(l1, l2), l3)
    e1, e2, e3 = jnp.exp(l1 - m), jnp.exp(l2 - m), jnp.exp(l3 - m)
    inv = 1.0 / (e1 + e2 + e3)
    return e1 * inv, e2 * inv, e3 * inv


def _dil_fwd(zd, n_batch, name):
    s = SEQ
    t = n_batch * s

    def body(q_ref, k_ref, v_ref, y_ref, l1_ref, l2_ref, l3_ref, o_scr, qkv4, o4, l4, bias_scr):
        _fill_bias(bias_scr, pl.program_id(1))
        for a, ref in enumerate((q_ref, k_ref, v_ref)):
            _to_quarters(ref, qkv4.at[a])

        def unit(srcs, start, first, stride, di, o_dst, l_dst):
            qrows = _strided(start, BLOCK, stride)
            krows = qrows if first else _strided(start - BLOCK * stride, 2 * BLOCK, stride)
            q = (srcs[0][qrows, :] * ATT_SCALE).astype(BF16)
            kc = srcs[1][krows, :].astype(BF16)
            vc = srcs[2][krows, :].astype(BF16)
            if first:
                kc, vc = jnp.concatenate([kc, kc]), jnp.concatenate([vc, vc])
            outs, lses = [], []
            for e in range(2):
                bias = _first_block_bias(bias_scr[2 * di + e]) if first else bias_scr[2 * di + e]
                sc = _dot_nt(jnp.where(_head_mask(e), q, jnp.zeros_like(q)), kc) + bias
                m = jnp.max(sc, axis=1, keepdims=True)
                pe = jnp.exp(sc - m)
                l = jnp.sum(pe, axis=1, keepdims=True)
                outs.append(_dot((pe * (1.0 / l)).astype(BF16), vc))
                lses.append(m + jnp.log(l))
            o_dst[qrows, :] = jnp.where(_head_mask(0), outs[0], outs[1])
            l_dst[qrows, :] = jnp.where(_head_mask(0), lses[0], lses[1])

        for n in range(SEQ // BLOCK):
            unit((q_ref, k_ref, v_ref), n * BLOCK, n == 0, 1, 0, o_scr.at[0], l1_ref)
        quarters = tuple(qkv4.at[a] for a in range(3))
        for di in (1, 2):
            stride = DILATIONS[di] // 4
            for r in range(4):
                for g in range(stride):
                    for n in range(QUARTER // (BLOCK * stride)):
                        unit(quarters, r * QUARTER + n * BLOCK * stride + g, n == 0, stride, di, o4.at[di - 1], l4.at[di - 1])
        for di, l_ref in ((1, l2_ref), (2, l3_ref)):
            _from_quarters(o4.at[di - 1], o_scr.at[di])
            _from_quarters(l4.at[di - 1], l_ref)
        w = _mix_weights(l1_ref[...], l2_ref[...], l3_ref[...])
        y_ref[...] = (w[0] * o_scr[0] + w[1] * o_scr[1] + w[2] * o_scr[2]).astype(BF16)

    blk = pl.BlockSpec((s, LANES), lambda b, hp: (b, hp))
    res = pl.pallas_call(
        body, grid=(n_batch, N_PAIRS),
        in_specs=_pair_specs(s),
        out_specs=[blk] * 4,
        out_shape=[jax.ShapeDtypeStruct((t, MIX_HALF), BF16)] + [jax.ShapeDtypeStruct((t, MIX_HALF), F32)] * 3,
        scratch_shapes=[pltpu.VMEM((3, s, LANES), F32), pltpu.VMEM((3, s, LANES), F32), pltpu.VMEM((2, s, LANES), F32),
                        pltpu.VMEM((2, s, LANES), F32), pltpu.VMEM((6, BLOCK, 2 * BLOCK), F32)],
        name=name, compiler_params=_params("parallel", "arbitrary"),
    )(zd, zd, zd)
    return res[0], res[1:]


def _dil_bwd(zd, dy, ya, lses, n_batch, name):
    s = SEQ
    t = n_batch * s

    def body(q_ref, k_ref, v_ref, dy_ref, ya_ref, l1_ref, l2_ref, l3_ref, dz_ref, w_scr, dy_scr, dot_scr, acc, st4, acc4, bias_scr):
        for di, dil in enumerate(DILATIONS):
            bias_scr[di] = jnp.concatenate([_dil_bias(_alibi_slope(pl.program_id(1), e), dil) for e in range(2)])
        for di, w in enumerate(_mix_weights(l1_ref[...], l2_ref[...], l3_ref[...])):
            w_scr[di] = w
        dya = dy_ref[...].astype(F32)
        prod = dya * ya_ref[...].astype(F32)
        per_head = [jnp.sum(jnp.where(_head_mask(e), prod, 0.0), axis=1, keepdims=True) for e in range(2)]
        dy_scr[...] = dya
        dot_scr[...] = jnp.where(_head_mask(0), per_head[0], per_head[1])
        acc[...] = jnp.zeros_like(acc)
        acc4[...] = jnp.zeros_like(acc4)
        staged = (q_ref, k_ref, v_ref, w_scr.at[1], w_scr.at[2], l2_ref, l3_ref, dy_scr, dot_scr)
        for a, ref in enumerate(staged):
            _to_quarters(ref, st4.at[a])

        def unit(srcs, dst, start, first, stride, di):
            qrows = _strided(start, BLOCK, stride)
            krows = qrows if first else _strided(start - BLOCK * stride, 2 * BLOCK, stride)
            q = (srcs[0][qrows, :] * ATT_SCALE).astype(BF16)
            kc = srcs[1][krows, :].astype(BF16)
            vc = srcs[2][krows, :].astype(BF16)
            wq = srcs[3][qrows, :]
            lse = srcs[4][qrows, :]
            do = (wq * srcs[5][qrows, :]).astype(BF16)
            sub = wq * srcs[6][qrows, :]
            heads = lambda a: jnp.concatenate([jnp.where(_head_mask(e), a, jnp.zeros_like(a)) for e in range(2)])
            column = lambda a: jnp.concatenate([a[:, HEAD_DIM * e:HEAD_DIM * e + 1] for e in range(2)])
            qq, dd = heads(q), heads(do)
            bias = bias_scr[di]
            p = jnp.exp(_dot_nt(qq, kc) + (bias[:, BLOCK:] if first else bias) - column(lse))
            dsb = (p * (_dot_nt(dd, vc) - column(sub))).astype(BF16)
            dq = _dot(jnp.concatenate([dsb[:BLOCK], dsb[BLOCK:]], axis=1), heads(kc))
            dst.at[0][qrows, :] += dq * ATT_SCALE
            dst.at[1][krows, :] += _dot_tn(dsb, qq)
            dst.at[2][krows, :] += _dot_tn(p.astype(BF16), dd)

        token_order = (q_ref, k_ref, v_ref, w_scr.at[0], l1_ref, dy_scr, dot_scr)
        for n in range(SEQ // BLOCK):
            unit(token_order, acc, n * BLOCK, n == 0, 1, 0)
        for di in (1, 2):
            quarters = (st4.at[0], st4.at[1], st4.at[2], st4.at[2 + di], st4.at[4 + di], st4.at[7], st4.at[8])
            stride = DILATIONS[di] // 4
            for r in range(4):
                for g in range(stride):
                    for n in range(QUARTER // (BLOCK * stride)):
                        unit(quarters, acc4, r * QUARTER + n * BLOCK * stride + g, n == 0, stride, di)
        for k in range(3):
            for r in range(4):
                acc.at[k][pl.ds(r, QUARTER, stride=4), :] += acc4[k, r * QUARTER:(r + 1) * QUARTER, :]
            dz_ref[:, k * LANES:(k + 1) * LANES] = acc[k].astype(BF16)

    blk = pl.BlockSpec((s, LANES), lambda b, hp: (b, hp))
    pair = pl.BlockSpec((s, PAIR_WIDTH), lambda b, hp: (b, hp))
    return pl.pallas_call(
        body, grid=(n_batch, N_PAIRS),
        in_specs=_pair_specs(s) + [blk] * 5,
        out_specs=pair,
        out_shape=jax.ShapeDtypeStruct((t, 2 * 3 * MIX_HALF), BF16),
        scratch_shapes=[pltpu.VMEM((3, s, LANES), F32), pltpu.VMEM((s, LANES), F32), pltpu.VMEM((s, LANES), F32),
                        pltpu.VMEM((3, s, LANES), F32), pltpu.VMEM((9, s, LANES), F32), pltpu.VMEM((3, s, LANES), F32),
                        pltpu.VMEM((3, 2 * BLOCK, 2 * BLOCK), F32)],
        name=name, compiler_params=_params("parallel", "arbitrary"),
    )(zd, zd, zd, dy, ya, *lses)


X_BQ = 2048


def _xattn_probs(q, k):
    sc = _dot_nt(q, k) * X_SCALE
    pe = jnp.exp(sc - jnp.max(sc, axis=1, keepdims=True))
    return pe / jnp.sum(pe, axis=1, keepdims=True)


def _xattn_fwd(qx, kx, vx, n_batch, name):
    nq = SEQ // X_BQ

    def body(q_ref, k_ref, v_ref, o_ref):
        p = _xattn_probs(q_ref[...], k_ref[...])
        o_ref[...] = _dot(p.astype(BF16), v_ref[...]).astype(BF16)

    qblk = pl.BlockSpec((X_BQ, X_HEAD_DIM), lambda b, h, i: (b * nq + i, h))
    kblk = pl.BlockSpec((N_MEM, X_HEAD_DIM), lambda b, h, i: (b, h))
    return pl.pallas_call(
        body, grid=(n_batch, X_HEADS, nq), in_specs=[qblk, kblk, kblk], out_specs=qblk,
        out_shape=jax.ShapeDtypeStruct(qx.shape, BF16),
        name=name, compiler_params=_params("parallel", "parallel", "arbitrary"),
    )(qx, kx, vx)


def _xattn_bwd(qx, kx, vx, dox, n_batch, name):
    nq = SEQ // X_BQ

    def body(q_ref, k_ref, v_ref, do_ref, dq_ref, dk_ref, dv_ref, dk_acc, dv_acc):
        i = pl.program_id(2)

        @pl.when(i == 0)
        def _():
            dk_acc[...] = jnp.zeros_like(dk_acc)
            dv_acc[...] = jnp.zeros_like(dv_acc)

        q, k, do = q_ref[...], k_ref[...], do_ref[...]
        p = _xattn_probs(q, k)
        dp = _dot_nt(do, v_ref[...])
        dsb = (p * (dp - jnp.sum(p * dp, axis=1, keepdims=True))).astype(BF16)
        dq_ref[...] = (_dot(dsb, k) * X_SCALE).astype(BF16)
        dk_acc[...] += _dot_tn(dsb, q) * X_SCALE
        dv_acc[...] += _dot_tn(p.astype(BF16), do)

        @pl.when(i == nq - 1)
        def _():
            dk_ref[...] = dk_acc[...].astype(BF16)
            dv_ref[...] = dv_acc[...].astype(BF16)

    qblk = pl.BlockSpec((X_BQ, X_HEAD_DIM), lambda b, h, i: (b * nq + i, h))
    kblk = pl.BlockSpec((N_MEM, X_HEAD_DIM), lambda b, h, i: (b, h))
    return pl.pallas_call(
        body, grid=(n_batch, X_HEADS, nq), in_specs=[qblk, kblk, kblk, qblk], out_specs=[qblk, kblk, kblk],
        out_shape=[jax.ShapeDtypeStruct(qx.shape, BF16), jax.ShapeDtypeStruct(kx.shape, BF16), jax.ShapeDtypeStruct(kx.shape, BF16)],
        scratch_shapes=[pltpu.VMEM((N_MEM, X_HEAD_DIM), F32)] * 2,
        name=name, compiler_params=_params("parallel", "parallel", "arbitrary"),
    )(qx, kx, vx, dox)


def _adamw(w, g, m, v, name, rows):
    r, c = w.shape
    assert r % rows == 0, (name, w.shape, rows)

    def body(w_ref, g_ref, m_ref, v_ref, d_ref, nm_ref, nv_ref):
        gv = g_ref[...]
        m1 = ADAM_B1 * m_ref[...] + (1.0 - ADAM_B1) * gv
        v1 = ADAM_B2 * v_ref[...] + (1.0 - ADAM_B2) * jnp.square(gv)
        m_hat = m1 / (1.0 - ADAM_B1 ** ADAM_STEP)
        v_hat = v1 / (1.0 - ADAM_B2 ** ADAM_STEP)
        d_ref[...] = -ADAM_LR * (m_hat / (jnp.sqrt(v_hat) + ADAM_EPS) + ADAM_WD * w_ref[...])
        nm_ref[...] = m1
        nv_ref[...] = v1

    blk = pl.BlockSpec((rows, c), lambda i: (i, 0))
    return pl.pallas_call(
        body, grid=(r // rows,), in_specs=[blk] * 4, out_specs=[blk] * 3,
        out_shape=[jax.ShapeDtypeStruct((r, c), F32)] * 3,
        name=name, compiler_params=_params("arbitrary"),
    )(w, g, m, v)


def _relu2(acc):
    a = jnp.maximum(acc, 0.0)
    return acc, a * a


def _relu2_bwd(acc, u):
    return (2.0 * jnp.maximum(u.astype(F32), 0.0) * acc,)


def _local_step(x, mem, target, vecs, w_in, late_weights, hooks=None):
    n_batch = x.shape[0]
    t = n_batch * SEQ
    x0 = x.reshape(t, D_MODEL)
    mem2 = mem.reshape(n_batch * N_MEM, D_MODEL)
    tgt = target.reshape(t, D_MODEL)

    half = 3 * MIX_HALF
    w_qkv = jnp.concatenate([_pair_major(w_in[:, :half]), _pair_major(w_in[:, half:QKV_WIDTH])], axis=1)
    w_gate = jnp.pad(w_in[:, QKV_WIDTH:], ((0, 0), (0, GATE_PAD - N_HEADS)))
    b_pad = jnp.pad(vecs["b_forget"], (0, GATE_PAD - N_HEADS)).reshape(1, GATE_PAD)

    h1, zd, zf, gate = _in_proj(x0, vecs["g_mix"], jnp.concatenate([w_qkv, w_gate], axis=1), "in_proj")
    mn = _rmsnorm(mem2, vecs["g_mem"], "norm_mem")
    c_bc, c_row, sg = _gate_fwd(gate, b_pad, n_batch, "gate_fwd")
    ya, lses = _dil_fwd(zd, n_batch, "dil_fwd")
    yf, of32, lse_f = _fox_fwd(zf, c_bc, c_row, n_batch, "fox_fwd")
    wts = late_weights(yf)
    w_out = wts["w_out"]
    x1, h2 = _matmul_res_norm([ya, yf], [w_out[:MIX_HALF], w_out[MIX_HALF:]], x0, vecs["g_xattn"], "out")
    qx = _matmul(h2, wts["w_xq"], "xq", out_dtypes=(BF16,))[0]
    kx = _matmul(mn, wts["w_xk"], "xk", out_dtypes=(BF16,))[0]
    vx = _matmul(mn, wts["w_xv"], "xv", out_dtypes=(BF16,))[0]
    ox = _xattn_fwd(qx, kx, vx, n_batch, "xattn_fwd")
    x2, h3 = _matmul_res_norm([ox], [wts["w_xo"]], x1, vecs["g_mlp"], "xo")
    u, a2 = _matmul(h3, wts["w_up"], "mlp_up", out_dtypes=(BF16, BF16), epilogue=_relu2, tn=1024)
    loss, dx3, dx3b, dg_final = _loss_bwd(a2, wts["w_down"], x2, vecs["g_final"], tgt, "mlp_down_loss")

    du = _matmul(dx3b, wts["w_down"], "mlp_down_bwd", out_dtypes=(BF16,), extras=(u,), epilogue=_relu2_bwd, tn=1024, w_t=True)[0]
    shards = (N_CHIPS, 2 * D_MODEL, D_MODEL)
    g_mlp = _matmul_tn(h3, du, "gw_up", packed=(shards, lambda i, j: (j, 0, 0), None))
    g_mlp = _matmul_tn(a2, dx3b, "gw_down", packed=(shards, lambda i, j: (i, 1, 0), g_mlp))
    gw_up = g_mlp[:, :D_MODEL].transpose(1, 0, 2).reshape(D_MODEL, D_FF)
    gw_down = g_mlp[:, D_MODEL:].reshape(D_FF, D_MODEL)
    on_grads, on_swapped = hooks or (None, None)
    token = on_grads("mlp", g_mlp) if hooks else None
    dx2, dx2b, dg_mlp = _matmul_rms_bwd([du], [wts["w_up"]], x2, vecs["g_mlp"], dx3, "mlp_up_bwd", after=token)

    gw_xo = _matmul_tn(ox, dx2b, "gw_xo")
    token = on_swapped("mlp", dx2b) if hooks else None
    dox = _matmul(dx2b, wts["w_xo"], "xo_bwd", out_dtypes=(BF16,), w_t=True, after=token)[0]
    dqx, dkx, dvx = _xattn_bwd(qx, kx, vx, dox, n_batch, "xattn_bwd")
    gw_xq = _matmul_tn(h2, dqx, "gw_xq")
    gw_xk = _matmul_tn(mn, dkx, "gw_xk")
    gw_xv = _matmul_tn(mn, dvx, "gw_xv")
    dmn = _matmul(dkx, wts["w_xk"], "xk_bwd", w_t=True)[0]
    dmn = _matmul_res(dvx, wts["w_xv"], dmn, "xv_bwd", w_t=True)
    _, _, dg_mem = _rms_bwd(mem2, dmn, vecs["g_mem"], None, "norm_mem_bwd")
    dx1, dx1b, dg_xattn = _matmul_rms_bwd([dqx], [wts["w_xq"]], x1, vecs["g_xattn"], dx2, "xq_bwd")

    gw_out = jnp.concatenate([_matmul_tn(ya, dx1b, "gw_out_a"), _matmul_tn(yf, dx1b, "gw_out_f")], axis=0)
    token = on_grads("mid", dict(w_out=gw_out, w_xq=gw_xq, w_xk=gw_xk, w_xv=gw_xv, w_xo=gw_xo)) if hooks else None
    dy = _matmul(dx1b, w_out, "out_bwd", out_dtypes=(BF16,), w_t=True, after=token)[0]
    dz = _dil_bwd(zd, dy, ya, lses, n_batch, "dil_bwd")
    dz, dc = _fox_bwd(zf, of32, dy, lse_f, c_bc, c_row, dz, n_batch, "fox_bwd")
    dzg, db = _gate_bwd(dc.reshape(n_batch, N_HEADS, SEQ), sg, "gate_bwd")
    token = on_swapped("mid", dz) if hooks else None
    gw_pm = _matmul_tn(h1, dz, "gw_in_qkv", after=token)
    gw_in = jnp.concatenate([_pair_major_inv(gw_pm[:, :half]), _pair_major_inv(gw_pm[:, half:]),
                             _matmul_tn(h1, dzg, "gw_in_gate")[:, :N_HEADS]], axis=1)
    dx0, _, dg_mix = _matmul_rms_bwd([dz, dzg], [w_qkv, w_gate], x0, vecs["g_mix"], dx1, "in_bwd")

    gw = dict(w_in=gw_in, w_out=gw_out, w_xq=gw_xq, w_xk=gw_xk, w_xv=gw_xv, w_xo=gw_xo, w_up=gw_up, w_down=gw_down)
    gv = dict(g_mix=dg_mix, g_xattn=dg_xattn, g_mem=dg_mem, g_mlp=dg_mlp, g_final=dg_final, b_forget=db)
    return loss, dx0.reshape(x.shape), gw, gv


MESH = pl.DeviceIdType.MESH
ANY = pl.BlockSpec(memory_space=pl.ANY)


def _place():
    x, y, c = lax.axis_index("x"), lax.axis_index("y"), lax.axis_index("c")
    other_chips = [(1 - x, y), (x, 1 - y), (1 - x, 1 - y)]
    return x, y, c, other_chips


def _my_chip():
    return 2 * lax.axis_index("x") + lax.axis_index("y")


def _halves(rows, c, align):
    half = rows // 2
    assert rows % (2 * align) == 0, rows
    return pl.ds(pl.multiple_of(c * half, align), half), pl.ds(pl.multiple_of((1 - c) * half, align), half)


def _place_own(wall, pack):
    return lax.dynamic_update_slice(wall, pack[None], (_my_chip(), 0, 0))


HBM = pl.BlockSpec(memory_space=pltpu.HBM)
SEM = pl.BlockSpec(memory_space=pltpu.SEMAPHORE)
SPLIT_COPY = pltpu.CompilerParams(has_side_effects=pltpu.SideEffectType.DATAFLOW_SIDE_EFFECTING)


def _in_hbm(a):
    return pltpu.with_memory_space_constraint(a, pltpu.HBM)


def _start_call(start, src, land_shape, after, name):
    land = lax.empty(land_shape, src.dtype)

    def body(src_ref, land_ref, after_ref, send_sems, recv_sems, src_thru, land_thru, token):
        del after_ref, src_thru, land_thru
        start(src_ref, land_ref, send_sems, recv_sems)
        token[...] = jnp.zeros_like(token)

    return pl.pallas_call(
        body, name=name,
        out_shape=(pltpu.SemaphoreType.DMA((3,)), pltpu.SemaphoreType.DMA((3,)), pltpu.HBM(src.shape, src.dtype),
                   pltpu.HBM(land_shape, src.dtype), jax.ShapeDtypeStruct((8, LANES), F32)),
        in_specs=(HBM, HBM, ANY), out_specs=(SEM, SEM, HBM, HBM, pl.BlockSpec(memory_space=pltpu.VMEM)),
        input_output_aliases={0: 2, 1: 3}, compiler_params=SPLIT_COPY,
    )(_in_hbm(src), _in_hbm(land), after)


def _wait_call(body, started, after, name):
    send_sems, recv_sems, src, land, _ = started
    return pl.pallas_call(
        body, name=name,
        out_shape=(pltpu.HBM(src.shape, src.dtype), pltpu.HBM(land.shape, land.dtype)),
        in_specs=(HBM, HBM, SEM, SEM, ANY), out_specs=(HBM, HBM),
        input_output_aliases={0: 0, 1: 1}, compiler_params=SPLIT_COPY,
    )(src, land, send_sems, recv_sems, after)


def _gather_copies(p_ref, wall_ref, send_sems, recv_sems):
    x, y, c, chips = _place()
    me = 2 * x + y
    mine, _ = _halves(p_ref.shape[0], c, 16)
    out, back = [], []
    for k, chip in enumerate(chips):
        peer = dict(send_sem=send_sems.at[k], recv_sem=recv_sems.at[k], device_id=(chip[0], chip[1], c), device_id_type=MESH)
        out.append(pltpu.make_async_remote_copy(src_ref=p_ref.at[mine], dst_ref=wall_ref.at[me, mine], **peer))
        slab = wall_ref.at[2 * chip[0] + chip[1], mine]
        back.append(pltpu.make_async_remote_copy(src_ref=slab, dst_ref=slab, **peer))
    return out, back


def _gather_start(pack, after, name):
    def start(p_ref, wall_ref, send_sems, recv_sems):
        for cp in _gather_copies(p_ref, wall_ref, send_sems, recv_sems)[0]:
            cp.start()

    return _start_call(start, pack, (N_CHIPS,) + pack.shape, after, name)


def _gather_wait(started, after, name):
    def body(p_ref, wall_ref, send_sems, recv_sems, after_ref, p_dead, wall_out):
        del after_ref, p_dead, wall_out
        out, back = _gather_copies(p_ref, wall_ref, send_sems, recv_sems)
        for cp_out, cp_back in zip(out, back):
            cp_out.wait_send()
            cp_back.wait_recv()

    return _wait_call(body, started, after, name)


def _pass_on(wall, name):
    def body(w_in_ref, out_ref, send_sems, recv_sems):
        del w_in_ref
        x, y, c, chips = _place()
        mine, theirs = _halves(wall.shape[1], c, 16)
        sends = []
        for k, chip in enumerate(chips):
            slab = out_ref.at[2 * chip[0] + chip[1]]
            peer = dict(send_sem=send_sems.at[k], recv_sem=recv_sems.at[k], device_id=(x, y, 1 - c), device_id_type=MESH)
            cp = pltpu.make_async_remote_copy(src_ref=slab.at[mine], dst_ref=slab.at[mine], **peer)
            cp.start()
            sends.append((cp, pltpu.make_async_remote_copy(src_ref=slab.at[theirs], dst_ref=slab.at[theirs], **peer)))
        for cp, back in sends:
            back.wait_recv()
            cp.wait_send()

    return pl.pallas_call(
        body, in_specs=[ANY], out_specs=ANY, out_shape=jax.ShapeDtypeStruct(wall.shape, wall.dtype),
        scratch_shapes=[pltpu.SemaphoreType.DMA((3,))] * 2, input_output_aliases={0: 0}, name=name,
    )(wall)


def _swap_halves(g, name):
    half = g.shape[1] // 2

    def body(g_ref, out_ref, send_sem, recv_sem):
        x, y, c, _ = _place()
        _, theirs = _halves(g.shape[1], c, 8)
        cp = pltpu.make_async_remote_copy(src_ref=g_ref.at[:, theirs], dst_ref=out_ref, send_sem=send_sem, recv_sem=recv_sem,
                                          device_id=(x, y, 1 - c), device_id_type=MESH)
        cp.start()
        cp.wait()

    return pl.pallas_call(
        body, in_specs=[ANY], out_specs=ANY,
        out_shape=jax.ShapeDtypeStruct((N_CHIPS, half, g.shape[2]), F32),
        scratch_shapes=[pltpu.SemaphoreType.DMA, pltpu.SemaphoreType.DMA],
        name=name,
    )(g)


def _swap_copy(g_ref, land_ref, send_sems, recv_sems):
    x, y, c, _ = _place()
    _, theirs = _halves(g_ref.shape[1], c, 8)
    return pltpu.make_async_remote_copy(src_ref=g_ref.at[:, theirs], dst_ref=land_ref, send_sem=send_sems.at[0],
                                        recv_sem=recv_sems.at[0], device_id=(x, y, 1 - c), device_id_type=MESH)


def _swap_start(g, name):
    def start(g_ref, land_ref, send_sems, recv_sems):
        _swap_copy(g_ref, land_ref, send_sems, recv_sems).start()

    return _start_call(start, g, (N_CHIPS, g.shape[1] // 2, g.shape[2]), _core_index(), name)


def _swap_wait(started, after, name):
    def body(g_ref, land_ref, send_sems, recv_sems, after_ref, g_out, land_out):
        del after_ref, g_out, land_out
        cp = _swap_copy(g_ref, land_ref, send_sems, recv_sems)
        cp.wait_send()
        cp.wait_recv()

    return _wait_call(body, started, after, name)


def _core_index():
    return lax.axis_index("c").astype(jnp.int32).reshape(1)


def _row_tile(half):
    tile = max(t for t in range(16, 1025, 16) if half % t == 0)
    return tile, half // tile


def _add_sibling(g, got, name):
    half = g.shape[1] // 2
    tile, n_tiles = _row_tile(half)

    def body(c_ref, g_ref, got_ref, o_ref):
        o_ref[...] = (g_ref[...] + got_ref[...]).astype(BF16)

    width = g.shape[2]
    blk = pl.BlockSpec((None, tile, width), lambda s, i, c_ref: (s, i, 0))
    return pl.pallas_call(
        body,
        grid_spec=pltpu.PrefetchScalarGridSpec(
            num_scalar_prefetch=1, grid=(N_CHIPS, n_tiles),
            in_specs=[pl.BlockSpec((None, tile, width), lambda s, i, c_ref: (s, c_ref[0] * n_tiles + i, 0)), blk],
            out_specs=blk),
        out_shape=jax.ShapeDtypeStruct((N_CHIPS, half, width), BF16),
        name=name, compiler_params=_params("arbitrary", "arbitrary"),
    )(_core_index(), g, got)


def _exchange_copies(p_ref, land_ref, send_sems, recv_sems):
    x, y, c, chips = _place()
    me = 2 * x + y
    out, back = [], []
    for k, chip in enumerate(chips):
        peer = dict(send_sem=send_sems.at[k], recv_sem=recv_sems.at[k], device_id=(chip[0], chip[1], c), device_id_type=MESH)
        out.append(pltpu.make_async_remote_copy(src_ref=p_ref.at[2 * chip[0] + chip[1]], dst_ref=land_ref.at[me], **peer))
        slab = land_ref.at[2 * chip[0] + chip[1]]
        back.append(pltpu.make_async_remote_copy(src_ref=slab, dst_ref=slab, **peer))
    return out, back


def _with_own(got, part):
    me = _my_chip()
    return lax.dynamic_update_slice(got, lax.dynamic_slice(part, (me, 0, 0), (1,) + part.shape[1:]), (me, 0, 0))


def _exchange_start(part, name):
    def start(p_ref, land_ref, send_sems, recv_sems):
        for cp in _exchange_copies(p_ref, land_ref, send_sems, recv_sems)[0]:
            cp.start()

    return _start_call(start, part, part.shape, _core_index(), name)


def _exchange_wait(started, after, name):
    def body(p_ref, land_ref, send_sems, recv_sems, after_ref, p_dead, land_out):
        del after_ref, p_dead, land_out
        out, back = _exchange_copies(p_ref, land_ref, send_sems, recv_sems)
        for cp_out, cp_back in zip(out, back):
            cp_out.wait_send()
            cp_back.wait_recv()

    part, got = _wait_call(body, started, after, name)
    return _with_own(got, part)


def _sum_chips(parts, name):
    half, width = parts.shape[1:]
    tile, n_tiles = _row_tile(half)

    def body(c_ref, p0, p1, p2, p3, o_ref):
        f32 = lambda p: p[...].astype(F32)
        o_ref[...] = ((f32(p0) + f32(p1)) + f32(p2)) + f32(p3)

    def slab(s):
        return pl.BlockSpec((None, tile, width), lambda i, c_ref, s=s: (s, i, 0))

    return pl.pallas_call(
        body,
        grid_spec=pltpu.PrefetchScalarGridSpec(
            num_scalar_prefetch=1, grid=(n_tiles,),
            in_specs=[slab(s) for s in range(N_CHIPS)],
            out_specs=pl.BlockSpec((None, tile, width), lambda i, c_ref: (c_ref[0], i, 0))),
        out_shape=jax.ShapeDtypeStruct((2, half, width), F32),
        name=name, compiler_params=_params("arbitrary"),
    )(_core_index(), parts, parts, parts, parts)


def _share_halves(halves, name):
    def body(h_ref, out_ref, send_sem, recv_sem):
        del h_ref
        x, y, c, _ = _place()
        cp = pltpu.make_async_remote_copy(src_ref=out_ref.at[c], dst_ref=out_ref.at[c], send_sem=send_sem, recv_sem=recv_sem,
                                          device_id=(x, y, 1 - c), device_id_type=MESH)
        cp.start()
        pltpu.make_async_remote_copy(src_ref=out_ref.at[1 - c], dst_ref=out_ref.at[1 - c], send_sem=send_sem, recv_sem=recv_sem,
                                     device_id=(x, y, 1 - c), device_id_type=MESH).wait_recv()
        cp.wait_send()

    return pl.pallas_call(
        body, in_specs=[ANY], out_specs=ANY,
        out_shape=jax.ShapeDtypeStruct(halves.shape, halves.dtype),
        scratch_shapes=[pltpu.SemaphoreType.DMA] * 2,
        input_output_aliases={0: 0},
        name=name,
    )(halves)


def _reduce_parts(g, tag):
    return _add_sibling(g, _swap_halves(g, "swap_" + tag), "add_" + tag)


def _reduce_finish(got, tag):
    halves = _share_halves(_sum_chips(got, "sum_" + tag), "share_" + tag)
    return halves.reshape(2 * halves.shape[1], halves.shape[2])


SMALL_ROWS = 8


def _allreduce_small(v):
    def body(v_ref, out_ref, buf, send_sems, recv_sems):
        x, y, c, _ = _place()
        buf[4 * x + 2 * y + c] = v_ref[...]
        sends = []
        for k in range(1, N_DEV):
            px = 1 - x if k & 4 else x
            py = 1 - y if k & 2 else y
            pc = 1 - c if k & 1 else c
            cp = pltpu.make_async_remote_copy(src_ref=v_ref, dst_ref=buf.at[4 * x + 2 * y + c], send_sem=send_sems.at[k - 1],
                                              recv_sem=recv_sems.at[k - 1], device_id=(px, py, pc), device_id_type=MESH)
            cp.start()
            sends.append((cp, 4 * px + 2 * py + pc))
        for k, (cp, peer) in enumerate(sends):
            pltpu.make_async_remote_copy(src_ref=v_ref, dst_ref=buf.at[peer], send_sem=send_sems.at[k], recv_sem=recv_sems.at[k],
                                         device_id=(x, y, c), device_id_type=MESH).wait_recv()
        for cp, _ in sends:
            cp.wait_send()
        total = buf[0]
        for d in range(1, N_DEV):
            total = total + buf[d]
        out_ref[...] = total

    vmem = pl.BlockSpec(memory_space=pltpu.VMEM)
    return pl.pallas_call(
        body, in_specs=[vmem], out_specs=vmem,
        out_shape=jax.ShapeDtypeStruct(v.shape, v.dtype),
        scratch_shapes=[pltpu.VMEM((N_DEV,) + v.shape, v.dtype), pltpu.SemaphoreType.DMA((N_DEV - 1,)),
                        pltpu.SemaphoreType.DMA((N_DEV - 1,))],
        name="allreduce_small",
    )(v)


MATRICES = ("w_in", "w_out", "w_xq", "w_xk", "w_xv", "w_xo", "w_up", "w_down")
VECTORS = ("g_mix", "g_xattn", "g_mem", "g_mlp", "g_final", "b_forget")
WEIGHT_ORDER = ("g_mix", "w_in", "b_forget", "w_out", "g_xattn", "g_mem", "w_xq", "w_xk", "w_xv", "w_xo",
                "g_mlp", "w_up", "w_down", "g_final")
GROUPS = {"mlp": ("w_up", "w_down"), "mid": ("w_out", "w_xq", "w_xk", "w_xv", "w_xo"), "in": ("w_in",)}
LATE = GROUPS["mid"] + GROUPS["mlp"]
W_IN_SHARD = IN_WIDTH // N_CHIPS
SHARD_ROWS = {"w_out": 256, "w_xq": 256, "w_xk": 256, "w_xv": 256, "w_xo": 256, "w_up": 1024, "w_down": 1024}
PACK_ROWS = SHARD_ROWS
W_IN_PAD = -(-W_IN_SHARD // LANES) * LANES
ADAM_ROWS = 128


def _pack(parts, names):
    return jnp.concatenate([jnp.pad(parts[n], ((0, PACK_ROWS[n] - SHARD_ROWS[n]), (0, 0))) for n in names], axis=0)


def _unpack(a, names):
    out, pos = {}, 0
    for n in names:
        out[n] = a[..., pos:pos + SHARD_ROWS[n], :]
        pos += PACK_ROWS[n]
    return out


def _full_weights(wall, names):
    cols = lambda a: a.transpose(1, 0, 2).reshape(a.shape[1], -1)
    rows = lambda a: a.reshape(-1, a.shape[-1])
    if names == GROUPS["in"]:
        return {"w_in": cols(wall[:, :, :W_IN_SHARD])}
    return {n: cols(a) if n == "w_up" else rows(a) for n, a in _unpack(wall, names).items()}


def _shard_of(g, name, s):
    if name == "w_up":
        return g[:, s * D_MODEL:(s + 1) * D_MODEL]
    n = SHARD_ROWS[name]
    return g[s * n:(s + 1) * n]


def _pad_w_in(a):
    return jnp.pad(a, [(0, 0)] * (a.ndim - 1) + [(0, W_IN_PAD - W_IN_SHARD)])


def _pack_grads(gws, names):
    if names == GROUPS["in"]:
        return _pad_w_in(gws["w_in"].reshape(D_MODEL, N_CHIPS, W_IN_SHARD).transpose(1, 0, 2))
    return jnp.stack([_pack({n: _shard_of(gws[n], n, s) for n in names}, names) for s in range(N_CHIPS)])


def kernel(x, mem, g_mix, w_in, b_forget, w_out, g_xattn, g_mem, w_xq, w_xk, w_xv, w_xo, g_mlp, w_up, w_down, g_final, loss_target, m_g_mix, m_w_in, m_b_forget, m_w_out, m_g_xattn, m_g_mem, m_w_xq, m_w_xk, m_w_xv, m_w_xo, m_g_mlp, m_w_up, m_w_down, m_g_final, v_g_mix, v_w_in, v_b_forget, v_w_out, v_g_xattn, v_g_mem, v_w_xq, v_w_xk, v_w_xv, v_w_xo, v_g_mlp, v_w_up, v_w_down, v_g_final):
    given = dict(locals())
    weights = {n: given[n] for n in WEIGHT_ORDER}
    vecs = {n: weights[n] for n in VECTORS}

    shard = {n: weights[n].astype(BF16) for n in MATRICES}
    in_started = _gather_start(_pad_w_in(shard["w_in"]), _core_index(), "gather_in_start")
    late_pack = _pack(shard, LATE)
    in_pack, in_wall = _gather_wait(in_started, late_pack, "gather_in_wait")
    in_wall = _place_own(_pass_on(in_wall, "gather_in_pass"), in_pack)
    late = _gather_start(late_pack, in_wall, "gather_late_start")
    w_in_full = _full_weights(in_wall, GROUPS["in"])["w_in"]

    def late_weights(after):
        pack, wall = _gather_wait(late, after, "gather_late_wait")
        return _full_weights(_place_own(_pass_on(wall, "gather_late_pass"), pack), LATE)

    started = {}

    swapping = {}

    def on_grads(group, gws):
        packed = gws if group == "mlp" else _pack_grads(gws, GROUPS[group])
        swapping[group] = _swap_start(packed, "swap_%s_start" % group)
        return swapping[group][4]

    def on_swapped(group, after):
        g, got = _swap_wait(swapping[group], after, "swap_%s_wait" % group)
        started[group] = _exchange_start(_add_sibling(g, got, "add_" + group), "exchange_%s_start" % group)
        return started[group][4]

    loss, grad_x, gw, gv = _local_step(x, mem, loss_target, vecs, w_in_full, late_weights, (on_grads, on_swapped))

    part = _reduce_parts(_pack_grads(gw, GROUPS["in"]), "in")
    started["in"] = _exchange_start(part, "exchange_in_start")
    grads, delta, new_m, new_v = {}, {}, {}, {}

    def finish(group, after):
        got = _exchange_wait(started[group], after, "exchange_%s_wait" % group)
        done = _reduce_finish(got, group)
        for n, a in ({"w_in": done[:, :W_IN_SHARD]} if group == "in" else _unpack(done, GROUPS[group])).items():
            grads[n] = a.reshape(weights[n].shape)
            delta[n], new_m[n], new_v[n] = _adamw(weights[n], grads[n], given["m_" + n], given["v_" + n], "adamw_" + n, ADAM_ROWS)
        return new_v[GROUPS[group][-1]]

    after = finish("mlp", started["in"][4])
    after = finish("mid", after)

    row = lambda a: jnp.pad(a.reshape(-1), (0, D_MODEL - a.size)).reshape(1, D_MODEL)
    small = jnp.concatenate([gv[n] for n in VECTORS[:5]] + [row(gv["b_forget"][:, 0]), row(loss[0, :1]),
                             jnp.zeros((1, D_MODEL), F32)], axis=0)
    small = _allreduce_small(small)
    for k, n in enumerate(VECTORS[:5]):
        grads[n] = small[k]
    grads["b_forget"] = small[5, :N_HEADS]
    loss_total = small[6, 0]
    finish("in", after)

    stack = lambda prefix: jnp.concatenate([row(given[prefix + n]) for n in VECTORS] + [jnp.zeros((2, D_MODEL), F32)], axis=0)
    g_small = jnp.concatenate([small[:6], jnp.zeros((2, D_MODEL), F32)], axis=0)
    d, m1, v1 = _adamw(stack(""), g_small, stack("m_"), stack("v_"), "adamw_vectors", SMALL_ROWS)
    for k, n in enumerate(VECTORS):
        width = weights[n].shape[0]
        delta[n], new_m[n], new_v[n] = d[k, :width], m1[k, :width], v1[k, :width]

    return (loss_total, grad_x, *[grads[n] for n in WEIGHT_ORDER], *[delta[n] for n in WEIGHT_ORDER],
            *[new_m[n] for n in WEIGHT_ORDER], *[new_v[n] for n in WEIGHT_ORDER])
```

```python
import functools
import math

import jax
import jax.numpy as jnp
from jax import lax
from jax.experimental import pallas as pl
from jax.experimental.pallas import tpu as pltpu

F32 = jnp.float32
BF16 = jnp.bfloat16

D_MODEL = 1024
SEQ = 2048
N_MEM = 256
HEAD_DIM = 64
N_HEADS = 8
MIX_HALF = N_HEADS * HEAD_DIM
QKV_WIDTH = 6 * MIX_HALF
IN_WIDTH = QKV_WIDTH + N_HEADS
GATE_PAD = 128
BLOCK = 128
DILATIONS = (1, 4, 16)
X_HEADS = 4
X_HEAD_DIM = 256
D_FF = 4096
EPS = 1e-6
NEG = -1e30
ATT_SCALE = 1.0 / math.sqrt(HEAD_DIM)
X_SCALE = 1.0 / math.sqrt(X_HEAD_DIM)
LANES = 128
N_CHIPS = 4
N_DEV = 8

ADAM_LR = 0.001
ADAM_B1 = 0.9
ADAM_B2 = 0.999
ADAM_EPS = 1e-08
ADAM_WD = 0.01
ADAM_STEP = 10

VMEM_LIMIT = 48 * 1024 * 1024


def _params(*sem):
    return pltpu.CompilerParams(dimension_semantics=sem or None, vmem_limit_bytes=VMEM_LIMIT)


def _dot(a, b):
    return jnp.dot(a, b, preferred_element_type=F32)


def _dot_nt(a, b):
    return lax.dot_general(a, b, (((1,), (1,)), ((), ())), preferred_element_type=F32)


def _dot_tn(a, b):
    return lax.dot_general(a, b, (((0,), (0,)), ((), ())), preferred_element_type=F32)


def _dot_exact(x, e):
    hi = x.astype(BF16)
    r1 = x - hi.astype(F32)
    mid = r1.astype(BF16)
    lo = (r1 - mid.astype(F32)).astype(BF16)
    return _dot(hi, e) + _dot(mid, e) + _dot(lo, e)


def _head_mask(e):
    lane = lax.broadcasted_iota(jnp.int32, (1, LANES), 1)
    return (lane >= HEAD_DIM * e) & (lane < HEAD_DIM * (e + 1))


def _matmul(a, w, name, out_dtypes=(F32,), extras=(), epilogue=None, tm=1024, tn=1024, w_t=False, after=None):
    m, k = a.shape
    n = w.shape[0] if w_t else w.shape[1]
    tm, tn = min(tm, m), min(tn, n)
    assert m % tm == 0 and n % tn == 0, (name, a.shape, w.shape)
    n_ex = len(extras)
    order = () if after is None else (after,)

    def body(a_ref, w_ref, *rest):
        rest = rest[len(order):]
        acc = (_dot_nt if w_t else _dot)(a_ref[...], w_ref[...])
        res = (acc,) if epilogue is None else epilogue(acc, *[r[...] for r in rest[:n_ex]])
        for o_ref, r in zip(rest[n_ex:], res):
            o_ref[...] = r.astype(o_ref.dtype)

    tile = pl.BlockSpec((tm, tn), lambda i, j: (i, j))
    w_spec = pl.BlockSpec((tn, k), lambda i, j: (j, 0)) if w_t else pl.BlockSpec((k, tn), lambda i, j: (0, j))
    return pl.pallas_call(
        body, grid=(m // tm, n // tn),
        in_specs=[pl.BlockSpec((tm, k), lambda i, j: (i, 0)), w_spec] + [pl.BlockSpec(memory_space=pl.ANY)] * len(order) + [tile] * n_ex,
        out_specs=[tile] * len(out_dtypes),
        out_shape=[jax.ShapeDtypeStruct((m, n), dt) for dt in out_dtypes],
        name=name, compiler_params=_params("parallel", "arbitrary"),
    )(a, w, *order, *extras)


def _matmul_res(a, w, res, name, w_t=False):
    return _matmul(a, w, name, extras=(res,), epilogue=lambda acc, r: (r + acc,), w_t=w_t)[0]


def _matmul_tn(x, y, name, tm=1024, tn=1024, tk=2048, packed=None, after=None):
    t, m = x.shape
    _, n = y.shape
    tm, tn, tk = min(tm, m), min(tn, n), min(tk, t)
    assert m % tm == 0 and n % tn == 0 and t % tk == 0, (name, x.shape, y.shape)
    shape, place, into = packed or ((m, n), None, None)

    def body(x_ref, y_ref, *rest):
        o_ref = rest[-1]

        @pl.when(pl.program_id(2) == 0)
        def _():
            o_ref[...] = jnp.zeros_like(o_ref)

        o_ref[...] += _dot_tn(x_ref[...], y_ref[...])

    out_spec = (pl.BlockSpec((tm, tn), lambda i, j, k: (i, j)) if place is None
                else pl.BlockSpec((None, tm, tn), lambda i, j, k: place(i, j)))
    return pl.pallas_call(
        body, grid=(m // tm, n // tn, t // tk),
        in_specs=[pl.BlockSpec((tk, tm), lambda i, j, k: (k, i)), pl.BlockSpec((tk, tn), lambda i, j, k: (k, j))]
        + [pl.BlockSpec(memory_space=pl.ANY)] * ((into is not None) + (after is not None)),
        out_specs=out_spec, out_shape=jax.ShapeDtypeStruct(shape, F32),
        input_output_aliases={} if into is None else {2: 0},
        name=name, compiler_params=_params("parallel", "parallel", "arbitrary"),
    )(x, y, *(() if into is None else (into,)), *(() if after is None else (after,)))


def _rmsnorm(x, g, name, tm=512):
    t, d = x.shape
    tm = min(tm, t)

    def body(x_ref, g_ref, h_ref):
        xv = x_ref[...]
        r = lax.rsqrt(jnp.mean(xv * xv, axis=-1, keepdims=True) + EPS)
        h_ref[...] = (xv * r * g_ref[...]).astype(BF16)

    return pl.pallas_call(
        body, grid=(t // tm,),
        in_specs=[pl.BlockSpec((tm, d), lambda i: (i, 0)), pl.BlockSpec((1, d), lambda i: (0, 0))],
        out_specs=pl.BlockSpec((tm, d), lambda i: (i, 0)),
        out_shape=jax.ShapeDtypeStruct((t, d), BF16),
        name=name, compiler_params=_params("arbitrary"),
    )(x, g.reshape(1, d))


def _in_proj(x, g, w_all, name, tm=512):
    t, d = x.shape
    half = 3 * MIX_HALF

    def body(x_ref, g_ref, w_ref, h_ref, zd_ref, zf_ref, gate_ref):
        xv = x_ref[...]
        r = lax.rsqrt(jnp.mean(xv * xv, axis=-1, keepdims=True) + EPS)
        h = (xv * r * g_ref[...]).astype(BF16)
        h_ref[...] = h
        zd_ref[...] = _dot(h, w_ref[:, 0:half])
        zf_ref[...] = _dot(h, w_ref[:, half:2 * half]).astype(BF16)
        gate_ref[...] = _dot(h, w_ref[:, 2 * half:])

    row = lambda width: pl.BlockSpec((tm, width), lambda i: (i, 0))
    return pl.pallas_call(
        body, grid=(t // tm,),
        in_specs=[row(d), pl.BlockSpec((1, d), lambda i: (0, 0)), pl.BlockSpec(w_all.shape, lambda i: (0, 0))],
        out_specs=[row(d), row(half), row(half), row(GATE_PAD)],
        out_shape=[jax.ShapeDtypeStruct((t, d), BF16), jax.ShapeDtypeStruct((t, half), F32),
                   jax.ShapeDtypeStruct((t, half), BF16), jax.ShapeDtypeStruct((t, GATE_PAD), F32)],
        name=name, compiler_params=_params("arbitrary"),
    )(x, g.reshape(1, d), w_all)


def _rms_bwd_tile(xv, dh, g):
    d = xv.shape[-1]
    r = lax.rsqrt(jnp.mean(xv * xv, axis=-1, keepdims=True) + EPS)
    dyg = dh * g
    proj = jnp.sum(dyg * xv, axis=-1, keepdims=True)
    dx = r * dyg - xv * (r * r * r * (1.0 / d)) * proj
    return dx, dh * (xv * r)


def _rms_bwd(x, dh, g, dres, name, tm=512):
    t, d = x.shape
    tm = min(tm, t)
    has_res = dres is not None

    def body(x_ref, dh_ref, g_ref, *rest):
        if has_res:
            res_ref, dx_ref, dxb_ref, dg_ref = rest
        else:
            dx_ref, dxb_ref, dg_ref = rest
        dx, dg_rows = _rms_bwd_tile(x_ref[...], dh_ref[...], g_ref[...])
        if has_res:
            dx = res_ref[...] + dx
        dx_ref[...] = dx
        dxb_ref[...] = dx.astype(BF16)

        @pl.when(pl.program_id(0) == 0)
        def _():
            dg_ref[...] = jnp.zeros_like(dg_ref)

        dg_ref[...] += jnp.sum(dg_rows, axis=0, keepdims=True)

    row = pl.BlockSpec((tm, d), lambda i: (i, 0))
    vec = pl.BlockSpec((1, d), lambda i: (0, 0))
    return pl.pallas_call(
        body, grid=(t // tm,),
        in_specs=[row, row, vec] + ([row] if has_res else []),
        out_specs=[row, row, vec],
        out_shape=[jax.ShapeDtypeStruct((t, d), F32), jax.ShapeDtypeStruct((t, d), BF16), jax.ShapeDtypeStruct((1, d), F32)],
        name=name, compiler_params=_params("arbitrary"),
    )(x, dh, g.reshape(1, d), *((dres,) if has_res else ()))


def _row_dots(a_refs, w_refs, w_t):
    acc = None
    for a_ref, w_ref in zip(a_refs, w_refs):
        part = (_dot_nt if w_t else _dot)(a_ref[...], w_ref[...])
        acc = part if acc is None else acc + part
    return acc


def _row_specs(a_parts, w_parts, tm):
    specs = [pl.BlockSpec((tm, a.shape[1]), lambda i: (i, 0)) for a in a_parts]
    return specs + [pl.BlockSpec(w.shape, lambda i: (0, 0)) for w in w_parts]


def _matmul_res_norm(a_parts, w_parts, res, g, name, tm=1024):
    t, d = res.shape
    n = len(a_parts)

    def body(*refs):
        res_ref, g_ref, x_ref, h_ref = refs[2 * n:]
        xv = res_ref[...] + _row_dots(refs[:n], refs[n:2 * n], False)
        x_ref[...] = xv
        r = lax.rsqrt(jnp.mean(xv * xv, axis=-1, keepdims=True) + EPS)
        h_ref[...] = (xv * r * g_ref[...]).astype(BF16)

    row = pl.BlockSpec((tm, d), lambda i: (i, 0))
    return pl.pallas_call(
        body, grid=(t // tm,),
        in_specs=_row_specs(a_parts, w_parts, tm) + [row, pl.BlockSpec((1, d), lambda i: (0, 0))],
        out_specs=[row, row],
        out_shape=[jax.ShapeDtypeStruct((t, d), F32), jax.ShapeDtypeStruct((t, d), BF16)],
        name=name, compiler_params=_params("arbitrary"),
    )(*a_parts, *w_parts, res, g.reshape(1, d))


def _matmul_rms_bwd(a_parts, w_parts, x, g, dres, name, tm=512, after=None):
    t, d = x.shape
    n = len(a_parts)
    order = () if after is None else (after,)

    def body(*refs):
        x_ref, g_ref, res_ref = refs[2 * n:2 * n + 3]
        dx_ref, dxb_ref, dg_ref = refs[2 * n + 3 + len(order):]
        dx, dg_rows = _rms_bwd_tile(x_ref[...], _row_dots(refs[:n], refs[n:2 * n], True), g_ref[...])
        dx = res_ref[...] + dx
        dx_ref[...] = dx
        dxb_ref[...] = dx.astype(BF16)

        @pl.when(pl.program_id(0) == 0)
        def _():
            dg_ref[...] = jnp.zeros_like(dg_ref)

        dg_ref[...] += jnp.sum(dg_rows, axis=0, keepdims=True)

    row = pl.BlockSpec((tm, d), lambda i: (i, 0))
    vec = pl.BlockSpec((1, d), lambda i: (0, 0))
    return pl.pallas_call(
        body, grid=(t // tm,),
        in_specs=_row_specs(a_parts, w_parts, tm) + [row, vec, row] + [pl.BlockSpec(memory_space=pl.ANY)] * len(order),
        out_specs=[row, row, vec],
        out_shape=[jax.ShapeDtypeStruct((t, d), F32), jax.ShapeDtypeStruct((t, d), BF16), jax.ShapeDtypeStruct((1, d), F32)],
        name=name, compiler_params=_params("arbitrary"),
    )(*a_parts, *w_parts, x, g.reshape(1, d), dres, *order)


def _loss_bwd(a, w, res, g, target, name, tm=512):
    t, d = res.shape

    def body(a_ref, w_ref, x_ref, g_ref, t_ref, loss_ref, dx_ref, dxb_ref, dg_ref):
        xv = x_ref[...] + _dot(a_ref[...], w_ref[...])
        gv = g_ref[...]
        r = lax.rsqrt(jnp.mean(xv * xv, axis=-1, keepdims=True) + EPS)
        err = xv * r * gv - t_ref[...]
        dx, dg_rows = _rms_bwd_tile(xv, err * (1.0 / d), gv)
        dx_ref[...] = dx
        dxb_ref[...] = dx.astype(BF16)

        @pl.when(pl.program_id(0) == 0)
        def _():
            dg_ref[...] = jnp.zeros_like(dg_ref)
            loss_ref[...] = jnp.zeros_like(loss_ref)

        dg_ref[...] += jnp.sum(dg_rows, axis=0, keepdims=True)
        part = jnp.sum(jnp.sum(err * err, axis=0, keepdims=True), axis=1, keepdims=True) * (0.5 / d)
        loss_ref[...] += jnp.broadcast_to(part, loss_ref.shape)

    row = pl.BlockSpec((tm, d), lambda i: (i, 0))
    vec = pl.BlockSpec((1, d), lambda i: (0, 0))
    return pl.pallas_call(
        body, grid=(t // tm,),
        in_specs=_row_specs([a], [w], tm) + [row, vec, row],
        out_specs=[pl.BlockSpec((1, LANES), lambda i: (0, 0)), row, row, vec],
        out_shape=[jax.ShapeDtypeStruct((1, LANES), F32), jax.ShapeDtypeStruct((t, d), F32),
                   jax.ShapeDtypeStruct((t, d), BF16), jax.ShapeDtypeStruct((1, d), F32)],
        name=name, compiler_params=_params("arbitrary"),
    )(a, w, res, g.reshape(1, d), target)


def _tri(upper):
    r = lax.broadcasted_iota(jnp.int32, (LANES, LANES), 0)
    c = lax.broadcasted_iota(jnp.int32, (LANES, LANES), 1)
    return jnp.where((r <= c) if upper else (r >= c), 1.0, 0.0).astype(BF16)


def _gate_fwd(gate, b_pad, n_batch, name):
    s = SEQ
    nblk = s // LANES

    def body(g_ref, b_ref, cbc_ref, crow_ref, sg_ref, ct_ref):
        gz = g_ref[...] + b_ref[...]
        logf = jnp.minimum(gz, 0.0) - jnp.log(1.0 + jnp.exp(-jnp.abs(gz)))
        logf_t = logf.T
        sg_ref[...] = (1.0 / (1.0 + jnp.exp(gz))).T[0:N_HEADS]
        upper = _tri(True)
        carry = jnp.zeros((LANES, 1), F32)
        for blk in range(nblk):
            seg = _dot_exact(logf_t[:, blk * LANES:(blk + 1) * LANES], upper) + carry
            carry = seg[:, LANES - 1:LANES]
            ct_ref[:, blk * LANES:(blk + 1) * LANES] = seg
        ct = ct_ref[...]
        crow_ref[...] = ct[0:N_HEADS]
        c_col = ct.T
        lane = lax.broadcasted_iota(jnp.int32, (1, MIX_HALF), 1)
        acc = jnp.zeros((s, MIX_HALF), F32)
        for h in range(N_HEADS):
            acc = jnp.where((lane >= HEAD_DIM * h) & (lane < HEAD_DIM * (h + 1)), c_col[:, h:h + 1], acc)
        cbc_ref[...] = acc

    return pl.pallas_call(
        body, grid=(n_batch,),
        in_specs=[pl.BlockSpec((s, GATE_PAD), lambda b: (b, 0)), pl.BlockSpec((1, GATE_PAD), lambda b: (0, 0))],
        out_specs=[pl.BlockSpec((s, MIX_HALF), lambda b: (b, 0)),
                   pl.BlockSpec((None, N_HEADS, s), lambda b: (b, 0, 0)),
                   pl.BlockSpec((None, N_HEADS, s), lambda b: (b, 0, 0))],
        out_shape=[jax.ShapeDtypeStruct((n_batch * s, MIX_HALF), F32),
                   jax.ShapeDtypeStruct((n_batch, N_HEADS, s), F32),
                   jax.ShapeDtypeStruct((n_batch, N_HEADS, s), F32)],
        scratch_shapes=[pltpu.VMEM((LANES, s), F32)],
        name=name, compiler_params=_params("arbitrary"),
    )(gate, b_pad)


def _gate_bwd(dc, sg, name):
    n_batch, _, s = dc.shape
    nblk = s // LANES

    def body(dc_ref, sg_ref, dz_ref, db_ref, dt_ref):
        lower = _tri(False)
        dcv = dc_ref[...]
        carry = jnp.zeros((N_HEADS, 1), F32)
        dt_ref[...] = jnp.zeros_like(dt_ref)
        for blk in reversed(range(nblk)):
            seg = _dot_exact(dcv[:, blk * LANES:(blk + 1) * LANES], lower) + carry
            carry = seg[:, 0:1]
            dt_ref[0:N_HEADS, blk * LANES:(blk + 1) * LANES] = seg * sg_ref[:, blk * LANES:(blk + 1) * LANES]
        dg_t = dt_ref[...]
        dz_ref[...] = dg_t.T.astype(BF16)

        @pl.when(pl.program_id(0) == 0)
        def _():
            db_ref[...] = jnp.zeros_like(db_ref)

        db_ref[...] += jnp.broadcast_to(jnp.sum(dg_t[0:N_HEADS], axis=1, keepdims=True), db_ref.shape)

    return pl.pallas_call(
        body, grid=(n_batch,),
        in_specs=[pl.BlockSpec((None, N_HEADS, s), lambda b: (b, 0, 0)), pl.BlockSpec((None, N_HEADS, s), lambda b: (b, 0, 0))],
        out_specs=[pl.BlockSpec((s, GATE_PAD), lambda b: (b, 0)), pl.BlockSpec((N_HEADS, LANES), lambda b: (0, 0))],
        out_shape=[jax.ShapeDtypeStruct((n_batch * s, GATE_PAD), BF16), jax.ShapeDtypeStruct((N_HEADS, LANES), F32)],
        scratch_shapes=[pltpu.VMEM((LANES, s), F32)],
        name=name, compiler_params=_params("arbitrary"),
    )(dc, sg)


FOX_BQ = 512
FOX_BK = 512
FOX_STRIP = 512
PAIR_WIDTH = 3 * LANES
N_PAIRS = N_HEADS // 2


def _pair_major(w):
    return w.reshape(w.shape[0], 3, N_PAIRS, LANES).transpose(0, 2, 1, 3).reshape(w.shape[0], 3 * MIX_HALF)


def _pair_major_inv(w):
    return w.reshape(w.shape[0], N_PAIRS, 3, LANES).transpose(0, 2, 1, 3).reshape(w.shape[0], 3 * MIX_HALF)


def _causal(i, j, bq, bk):
    qpos = i * bq + lax.broadcasted_iota(jnp.int32, (bq, 1), 0)
    kpos = j * bk + lax.broadcasted_iota(jnp.int32, (1, bk), 1)
    return kpos <= qpos


def _split_bf16(p):
    hi = p.astype(BF16)
    return hi, (p - hi.astype(F32)).astype(BF16)


def _fox_fwd(zf, c_bc, c_row, n_batch, name):
    s, bq, bk = SEQ, FOX_BQ, FOX_BK
    nq = s // bq
    t = n_batch * s

    n_strip = bq // FOX_STRIP

    def body(q_ref, k_ref, v_ref, cq_ref, cr_ref, o_ref, o32_ref, lse_ref):
        hp = pl.program_id(1)
        strips = [slice(r * FOX_STRIP, (r + 1) * FOX_STRIP) for r in range(n_strip)]
        chains = [(e, r) for e in range(2) for r in range(n_strip)]
        qh, cq = {}, {}
        for e, r in chains:
            q = q_ref[strips[r], :] * ATT_SCALE
            qh[e, r] = jnp.where(_head_mask(e), q, jnp.zeros_like(q))
            cq[e, r] = cq_ref[strips[r], HEAD_DIM * e:HEAD_DIM * e + 1]

        def step(i, j, carry, masked):
            rows = pl.ds(j * bk, bk)
            kj, vj = k_ref[rows, :], v_ref[rows, :]
            ck = [cr_ref[pl.ds(2 * hp + e, 1), rows] for e in range(2)]
            out = []
            scores = [_dot_nt(qh[e, r], kj) for e, r in chains]
            for n, (e, r) in enumerate(chains):
                m, l, acc = carry[3 * n:3 * n + 3]
                sc = scores[n] + (cq[e, r] - ck[e])
                if masked:
                    qpos = i * bq + r * FOX_STRIP + lax.broadcasted_iota(jnp.int32, (FOX_STRIP, 1), 0)
                    kpos = j * bk + lax.broadcasted_iota(jnp.int32, (1, bk), 1)
                    sc = jnp.where(kpos <= qpos, sc, NEG)
                m_new = jnp.maximum(m, jnp.max(sc, axis=1, keepdims=True))
                alpha = jnp.exp(m - m_new)
                p = jnp.exp(sc - m_new)
                p_hi, p_lo = _split_bf16(p)
                out += [m_new, alpha * l + jnp.sum(p, axis=1, keepdims=True), alpha * acc + (_dot(p_hi, vj) + _dot(p_lo, vj))]
            return tuple(out)

        def run(i):
            carry = (jnp.full((FOX_STRIP, 1), NEG, F32), jnp.zeros((FOX_STRIP, 1), F32), jnp.zeros((FOX_STRIP, LANES), F32)) * len(chains)
            n_clear = (i * bq) // bk
            for j in range((i * bq + bq + bk - 1) // bk):
                carry = step(i, j, carry, masked=j >= n_clear)
            for r in range(n_strip):
                outs = [carry[3 * (e * n_strip + r) + 2] / carry[3 * (e * n_strip + r) + 1] for e in range(2)]
                lses = [carry[3 * (e * n_strip + r)] + jnp.log(carry[3 * (e * n_strip + r) + 1]) for e in range(2)]
                o = jnp.where(_head_mask(0), outs[0], outs[1])
                o_ref[strips[r], :] = o.astype(BF16)
                o32_ref[strips[r], :] = o
                lse_ref[strips[r], :] = jnp.where(_head_mask(0), lses[0], lses[1])

        for k in range(nq):
            pl.when(pl.program_id(2) == k)(functools.partial(run, k))

    def col(c0):
        return lambda b, hp, i: (b, 3 * hp + c0)

    blk = pl.BlockSpec((bq, LANES), lambda b, hp, i: (b * nq + i, hp))
    return pl.pallas_call(
        body, grid=(n_batch, N_PAIRS, nq),
        in_specs=[pl.BlockSpec((bq, LANES), lambda b, hp, i: (b * nq + i, 3 * hp)),
                  pl.BlockSpec((s, LANES), col(1)), pl.BlockSpec((s, LANES), col(2)), blk,
                  pl.BlockSpec((None, N_HEADS, s), lambda b, hp, i: (b, 0, 0))],
        out_specs=[blk, blk, blk],
        out_shape=[jax.ShapeDtypeStruct((t, MIX_HALF), BF16), jax.ShapeDtypeStruct((t, MIX_HALF), F32),
                   jax.ShapeDtypeStruct((t, MIX_HALF), F32)],
        name=name, compiler_params=_params("parallel", "parallel", "arbitrary"),
    )(zf, zf, zf, c_bc, c_row)


def _fox_bwd(zf, o32, dy, lse, c_bc, c_row, dz, n_batch, name):
    s, bq, bk = SEQ, FOX_BQ, FOX_BK
    nq, nk = s // bq, s // bk

    def body(q_ref, k_ref, v_ref, o_ref, do_ref, lse_ref, cq_ref, cr_ref, dz_in, dz_ref, dc_ref, dq_acc):
        del dz_in
        hp = pl.program_id(1)

        @pl.when(pl.program_id(2) == 0)
        def _():
            dq_acc[...] = jnp.zeros_like(dq_acc)

        kj, vj = k_ref[...], v_ref[...]
        km = [jnp.where(_head_mask(e), kj, jnp.zeros_like(kj)) for e in range(2)]

        def step(i, j, ck, carry, masked):
            rows = pl.ds(i * bq, bq)
            qi, doi = q_ref[rows, :] * ATT_SCALE, do_ref[rows, :]
            prod = doi.astype(F32) * o_ref[rows, :]
            out = []
            dq = jnp.zeros((bq, LANES), F32)
            for e in range(2):
                dk_a, dv_a, dc_a = carry[3 * e:3 * e + 3]
                mask = _head_mask(e)
                lane0 = HEAD_DIM * e
                dom = jnp.where(mask, doi, jnp.zeros_like(doi))
                delta = jnp.sum(jnp.where(mask, prod, 0.0), axis=1, keepdims=True)
                sc = _dot_nt(qi, km[e]) + (cq_ref[rows, lane0:lane0 + 1] - ck[e])
                if masked:
                    sc = jnp.where(_causal(i, j, bq, bk), sc, NEG)
                p = jnp.exp(sc - lse_ref[rows, lane0:lane0 + 1])
                ds = p * (_dot_nt(dom, vj) - delta)
                dsb = ds.astype(BF16)
                dq = dq + _dot(dsb, km[e])
                out += [dk_a + _dot_tn(dsb, qi), dv_a + _dot_tn(p.astype(BF16), dom), dc_a - jnp.sum(ds, axis=0, keepdims=True)]
            dq_acc[rows, :] += dq * ATT_SCALE
            return tuple(out)

        def run(j):
            cols = pl.ds(j * bk, bk)
            ck = [cr_ref[pl.ds(2 * hp + e, 1), cols] for e in range(2)]
            carry = (jnp.zeros((bk, LANES), F32), jnp.zeros((bk, LANES), F32), jnp.zeros((1, bk), F32)) * 2
            n_diag = (j * bk + bk + bq - 1) // bq
            for i in range((j * bk) // bq, nq):
                carry = step(i, j, ck, carry, masked=i < n_diag)
            for e in range(2):
                dc_ref[e:e + 1, :] = carry[3 * e + 2]
            dz_ref[cols, LANES:2 * LANES] = jnp.where(_head_mask(0), carry[0], carry[3]).astype(BF16)
            dz_ref[cols, 2 * LANES:3 * LANES] = (carry[1] + carry[4]).astype(BF16)
            if j == nk - 1:
                dz_ref[:, 0:LANES] = dq_acc[...].astype(BF16)

        for k in range(nk):
            pl.when(pl.program_id(2) == k)(functools.partial(run, k))

    def seq(idx):
        return pl.BlockSpec((s, LANES), lambda b, hp, j: (b, idx(hp)))

    def kblk(c0):
        return pl.BlockSpec((bk, LANES), lambda b, hp, j: (b * nk + j, 3 * hp + c0))

    return pl.pallas_call(
        body, grid=(n_batch, N_PAIRS, nk),
        in_specs=[seq(lambda hp: 3 * hp), kblk(1), kblk(2), seq(lambda hp: hp), seq(lambda hp: N_PAIRS + hp),
                  seq(lambda hp: hp), seq(lambda hp: hp),
                  pl.BlockSpec((None, N_HEADS, s), lambda b, hp, j: (b, 0, 0)), pl.BlockSpec(memory_space=pl.ANY)],
        out_specs=[pl.BlockSpec((s, PAIR_WIDTH), lambda b, hp, j: (b, N_PAIRS + hp)),
                   pl.BlockSpec((None, None, 2, bk), lambda b, hp, j: (b, hp, 0, j))],
        out_shape=[jax.ShapeDtypeStruct(dz.shape, dz.dtype), jax.ShapeDtypeStruct((n_batch, N_PAIRS, 2, s), F32)],
        scratch_shapes=[pltpu.VMEM((s, LANES), F32)],
        input_output_aliases={8: 0},
        name=name, compiler_params=_params("parallel", "parallel", "arbitrary"),
    )(zf, zf, zf, o32, dy, lse, c_bc, c_row, dz)


def _dil_bias(slope, dil):
    qi = lax.broadcasted_iota(jnp.int32, (BLOCK, 2 * BLOCK), 0)
    kj = lax.broadcasted_iota(jnp.int32, (BLOCK, 2 * BLOCK), 1)
    delta = qi + BLOCK - kj
    return jnp.where((delta >= 0) & (delta <= BLOCK), (-slope * dil) * delta.astype(F32), NEG)


def _alibi_slope(hp, e):
    slope = jnp.float32(0.0)
    for k in range(N_PAIRS):
        slope = jnp.where(hp == k, jnp.float32(2.0 ** -(2 * k + e + 1)), slope)
    return slope


def _first_block_bias(bias):
    return jnp.where(lax.broadcasted_iota(jnp.int32, bias.shape, 1) < BLOCK, NEG, bias)


def _fill_bias(bias_scr, hp):
    for di, dil in enumerate(DILATIONS):
        for e in range(2):
            bias_scr[2 * di + e] = _dil_bias(_alibi_slope(hp, e), dil)


def _pair_specs(rows):
    return [pl.BlockSpec((rows, LANES), lambda b, hp, c0=c0: (b, 3 * hp + c0)) for c0 in range(3)]


def _strided(start, size, dil):
    return pl.ds(start, size) if dil == 1 else pl.ds(start, size, stride=dil)


QUARTER = SEQ // 4


def _to_quarters(src, dst):
    for r in range(4):
        dst[r * QUARTER:(r + 1) * QUARTER, :] = src[pl.ds(r, QUARTER, stride=4), :]


def _from_quarters(src, dst):
    for r in range(4):
        dst[pl.ds(r, QUARTER, stride=4), :] = src[r * QUARTER:(r + 1) * QUARTER, :]


def _mix_weights(l1, l2, l3):
    m = jnp.maximum(jnp.maximum(l1, l2), l3)
    e1, e2, e3 = jnp.exp(l1 - m), jnp.exp(l2 - m), jnp.exp(l3 - m)
    inv = 1.0 / (e1 + e2 + e3)
    return e1 * inv, e2 * inv, e3 * inv


def _dil_fwd(zd, n_batch, name):
    s = SEQ
    t = n_batch * s

    def body(q_ref, k_ref, v_ref, y_ref, l1_ref, l2_ref, l3_ref, o_scr, qkv4, o4, l4, bias_scr):
        _fill_bias(bias_scr, pl.program_id(1))
        for a, ref in enumerate((q_ref, k_ref, v_ref)):
            _to_quarters(ref, qkv4.at[a])

        def unit(srcs, start, first, stride, di, o_dst, l_dst):
            qrows = _strided(start, BLOCK, stride)
            krows = qrows if first else _strided(start - BLOCK * stride, 2 * BLOCK, stride)
            q = (srcs[0][qrows, :] * ATT_SCALE).astype(BF16)
            kc = srcs[1][krows, :].astype(BF16)
            vc = srcs[2][krows, :].astype(BF16)
            if first:
                kc, vc = jnp.concatenate([kc, kc]), jnp.concatenate([vc, vc])
            outs, lses = [], []
            for e in range(2):
                bias = _first_block_bias(bias_scr[2 * di + e]) if first else bias_scr[2 * di + e]
                sc = _dot_nt(jnp.where(_head_mask(e), q, jnp.zeros_like(q)), kc) + bias
                m = jnp.max(sc, axis=1, keepdims=True)
                pe = jnp.exp(sc - m)
                l = jnp.sum(pe, axis=1, keepdims=True)
                outs.append(_dot((pe * (1.0 / l)).astype(BF16), vc))
                lses.append(m + jnp.log(l))
            o_dst[qrows, :] = jnp.where(_head_mask(0), outs[0], outs[1])
            l_dst[qrows, :] = jnp.where(_head_mask(0), lses[0], lses[1])

        for n in range(SEQ // BLOCK):
            unit((q_ref, k_ref, v_ref), n * BLOCK, n == 0, 1, 0, o_scr.at[0], l1_ref)
        quarters = tuple(qkv4.at[a] for a in range(3))
        for di in (1, 2):
            stride = DILATIONS[di] // 4
            for r in range(4):
                for g in range(stride):
                    for n in range(QUARTER // (BLOCK * stride)):
                        unit(quarters, r * QUARTER + n * BLOCK * stride + g, n == 0, stride, di, o4.at[di - 1], l4.at[di - 1])
        for di, l_ref in ((1, l2_ref), (2, l3_ref)):
            _from_quarters(o4.at[di - 1], o_scr.at[di])
            _from_quarters(l4.at[di - 1], l_ref)
        w = _mix_weights(l1_ref[...], l2_ref[...], l3_ref[...])
        y_ref[...] = (w[0] * o_scr[0] + w[1] * o_scr[1] + w[2] * o_scr[2]).astype(BF16)

    blk = pl.BlockSpec((s, LANES), lambda b, hp: (b, hp))
    res = pl.pallas_call(
        body, grid=(n_batch, N_PAIRS),
        in_specs=_pair_specs(s),
        out_specs=[blk] * 4,
        out_shape=[jax.ShapeDtypeStruct((t, MIX_HALF), BF16)] + [jax.ShapeDtypeStruct((t, MIX_HALF), F32)] * 3,
        scratch_shapes=[pltpu.VMEM((3, s, LANES), F32), pltpu.VMEM((3, s, LANES), F32), pltpu.VMEM((2, s, LANES), F32),
                        pltpu.VMEM((2, s, LANES), F32), pltpu.VMEM((6, BLOCK, 2 * BLOCK), F32)],
        name=name, compiler_params=_params("parallel", "arbitrary"),
    )(zd, zd, zd)
    return res[0], res[1:]


def _dil_bwd(zd, dy, ya, lses, n_batch, name):
    s = SEQ
    t = n_batch * s

    def body(q_ref, k_ref, v_ref, dy_ref, ya_ref, l1_ref, l2_ref, l3_ref, dz_ref, w_scr, dy_scr, dot_scr, acc, st4, acc4, bias_scr):
        for di, dil in enumerate(DILATIONS):
            bias_scr[di] = jnp.concatenate([_dil_bias(_alibi_slope(pl.program_id(1), e), dil) for e in range(2)])
        for di, w in enumerate(_mix_weights(l1_ref[...], l2_ref[...], l3_ref[...])):
            w_scr[di] = w
        dya = dy_ref[...].astype(F32)
        prod = dya * ya_ref[...].astype(F32)
        per_head = [jnp.sum(jnp.where(_head_mask(e), prod, 0.0), axis=1, keepdims=True) for e in range(2)]
        dy_scr[...] = dya
        dot_scr[...] = jnp.where(_head_mask(0), per_head[0], per_head[1])
        acc[...] = jnp.zeros_like(acc)
        acc4[...] = jnp.zeros_like(acc4)
        staged = (q_ref, k_ref, v_ref, w_scr.at[1], w_scr.at[2], l2_ref, l3_ref, dy_scr, dot_scr)
        for a, ref in enumerate(staged):
            _to_quarters(ref, st4.at[a])

        def unit(srcs, dst, start, first, stride, di):
            qrows = _strided(start, BLOCK, stride)
            krows = qrows if first else _strided(start - BLOCK * stride, 2 * BLOCK, stride)
            q = (srcs[0][qrows, :] * ATT_SCALE).astype(BF16)
            kc = srcs[1][krows, :].astype(BF16)
            vc = srcs[2][krows, :].astype(BF16)
            wq = srcs[3][qrows, :]
            lse = srcs[4][qrows, :]
            do = (wq * srcs[5][qrows, :]).astype(BF16)
            sub = wq * srcs[6][qrows, :]
            heads = lambda a: jnp.concatenate([jnp.where(_head_mask(e), a, jnp.zeros_like(a)) for e in range(2)])
            column = lambda a: jnp.concatenate([a[:, HEAD_DIM * e:HEAD_DIM * e + 1] for e in range(2)])
            qq, dd = heads(q), heads(do)
            bias = bias_scr[di]
            p = jnp.exp(_dot_nt(qq, kc) + (bias[:, BLOCK:] if first else bias) - column(lse))
            dsb = (p * (_dot_nt(dd, vc) - column(sub))).astype(BF16)
            dq = _dot(jnp.concatenate([dsb[:BLOCK], dsb[BLOCK:]], axis=1), heads(kc))
            dst.at[0][qrows, :] += dq * ATT_SCALE
            dst.at[1][krows, :] += _dot_tn(dsb, qq)
            dst.at[2][krows, :] += _dot_tn(p.astype(BF16), dd)

        token_order = (q_ref, k_ref, v_ref, w_scr.at[0], l1_ref, dy_scr, dot_scr)
        for n in range(SEQ // BLOCK):
            unit(token_order, acc, n * BLOCK, n == 0, 1, 0)
        for di in (1, 2):
            quarters = (st4.at[0], st4.at[1], st4.at[2], st4.at[2 + di], st4.at[4 + di], st4.at[7], st4.at[8])
            stride = DILATIONS[di] // 4
            for r in range(4):
                for g in range(stride):
                    for n in range(QUARTER // (BLOCK * stride)):
                        unit(quarters, acc4, r * QUARTER + n * BLOCK * stride + g, n == 0, stride, di)
        for k in range(3):
            for r in range(4):
                acc.at[k][pl.ds(r, QUARTER, stride=4), :] += acc4[k, r * QUARTER:(r + 1) * QUARTER, :]
            dz_ref[:, k * LANES:(k + 1) * LANES] = acc[k].astype(BF16)

    blk = pl.BlockSpec((s, LANES), lambda b, hp: (b, hp))
    pair = pl.BlockSpec((s, PAIR_WIDTH), lambda b, hp: (b, hp))
    return pl.pallas_call(
        body, grid=(n_batch, N_PAIRS),
        in_specs=_pair_specs(s) + [blk] * 5,
        out_specs=pair,
        out_shape=jax.ShapeDtypeStruct((t, 2 * 3 * MIX_HALF), BF16),
        scratch_shapes=[pltpu.VMEM((3, s, LANES), F32), pltpu.VMEM((s, LANES), F32), pltpu.VMEM((s, LANES), F32),
                        pltpu.VMEM((3, s, LANES), F32), pltpu.VMEM((9, s, LANES), F32), pltpu.VMEM((3, s, LANES), F32),
                        pltpu.VMEM((3, 2 * BLOCK, 2 * BLOCK), F32)],
        name=name, compiler_params=_params("parallel", "arbitrary"),
    )(zd, zd, zd, dy, ya, *lses)


X_BQ = 2048


def _xattn_probs(q, k):
    sc = _dot_nt(q, k) * X_SCALE
    pe = jnp.exp(sc - jnp.max(sc, axis=1, keepdims=True))
    return pe / jnp.sum(pe, axis=1, keepdims=True)


def _xattn_fwd(qx, kx, vx, n_batch, name):
    nq = SEQ // X_BQ

    def body(q_ref, k_ref, v_ref, o_ref):
        p = _xattn_probs(q_ref[...], k_ref[...])
        o_ref[...] = _dot(p.astype(BF16), v_ref[...]).astype(BF16)

    qblk = pl.BlockSpec((X_BQ, X_HEAD_DIM), lambda b, h, i: (b * nq + i, h))
    kblk = pl.BlockSpec((N_MEM, X_HEAD_DIM), lambda b, h, i: (b, h))
    return pl.pallas_call(
        body, grid=(n_batch, X_HEADS, nq), in_specs=[qblk, kblk, kblk], out_specs=qblk,
        out_shape=jax.ShapeDtypeStruct(qx.shape, BF16),
        name=name, compiler_params=_params("parallel", "parallel", "arbitrary"),
    )(qx, kx, vx)


def _xattn_bwd(qx, kx, vx, dox, n_batch, name):
    nq = SEQ // X_BQ

    def body(q_ref, k_ref, v_ref, do_ref, dq_ref, dk_ref, dv_ref, dk_acc, dv_acc):
        i = pl.program_id(2)

        @pl.when(i == 0)
        def _():
            dk_acc[...] = jnp.zeros_like(dk_acc)
            dv_acc[...] = jnp.zeros_like(dv_acc)

        q, k, do = q_ref[...], k_ref[...], do_ref[...]
        p = _xattn_probs(q, k)
        dp = _dot_nt(do, v_ref[...])
        dsb = (p * (dp - jnp.sum(p * dp, axis=1, keepdims=True))).astype(BF16)
        dq_ref[...] = (_dot(dsb, k) * X_SCALE).astype(BF16)
        dk_acc[...] += _dot_tn(dsb, q) * X_SCALE
        dv_acc[...] += _dot_tn(p.astype(BF16), do)

        @pl.when(i == nq - 1)
        def _():
            dk_ref[...] = dk_acc[...].astype(BF16)
            dv_ref[...] = dv_acc[...].astype(BF16)

    qblk = pl.BlockSpec((X_BQ, X_HEAD_DIM), lambda b, h, i: (b * nq + i, h))
    kblk = pl.BlockSpec((N_MEM, X_HEAD_DIM), lambda b, h, i: (b, h))
    return pl.pallas_call(
        body, grid=(n_batch, X_HEADS, nq), in_specs=[qblk, kblk, kblk, qblk], out_specs=[qblk, kblk, kblk],
        out_shape=[jax.ShapeDtypeStruct(qx.shape, BF16), jax.ShapeDtypeStruct(kx.shape, BF16), jax.ShapeDtypeStruct(kx.shape, BF16)],
        scratch_shapes=[pltpu.VMEM((N_MEM, X_HEAD_DIM), F32)] * 2,
        name=name, compiler_params=_params("parallel", "parallel", "arbitrary"),
    )(qx, kx, vx, dox)


def _adamw(w, g, m, v, name, rows):
    r, c = w.shape
    assert r % rows == 0, (name, w.shape, rows)

    def body(w_ref, g_ref, m_ref, v_ref, d_ref, nm_ref, nv_ref):
        gv = g_ref[...]
        m1 = ADAM_B1 * m_ref[...] + (1.0 - ADAM_B1) * gv
        v1 = ADAM_B2 * v_ref[...] + (1.0 - ADAM_B2) * jnp.square(gv)
        m_hat = m1 / (1.0 - ADAM_B1 ** ADAM_STEP)
        v_hat = v1 / (1.0 - ADAM_B2 ** ADAM_STEP)
        d_ref[...] = -ADAM_LR * (m_hat / (jnp.sqrt(v_hat) + ADAM_EPS) + ADAM_WD * w_ref[...])
        nm_ref[...] = m1
        nv_ref[...] = v1

    blk = pl.BlockSpec((rows, c), lambda i: (i, 0))
    return pl.pallas_call(
        body, grid=(r // rows,), in_specs=[blk] * 4, out_specs=[blk] * 3,
        out_shape=[jax.ShapeDtypeStruct((r, c), F32)] * 3,
        name=name, compiler_params=_params("arbitrary"),
    )(w, g, m, v)


def _relu2(acc):
    a = jnp.maximum(acc, 0.0)
    return acc, a * a


def _relu2_bwd(acc, u):
    return (2.0 * jnp.maximum(u.astype(F32), 0.0) * acc,)


def _local_step(x, mem, target, vecs, w_in, late_weights, hooks=None):
    n_batch = x.shape[0]
    t = n_batch * SEQ
    x0 = x.reshape(t, D_MODEL)
    mem2 = mem.reshape(n_batch * N_MEM, D_MODEL)
    tgt = target.reshape(t, D_MODEL)

    half = 3 * MIX_HALF
    w_qkv = jnp.concatenate([_pair_major(w_in[:, :half]), _pair_major(w_in[:, half:QKV_WIDTH])], axis=1)
    w_gate = jnp.pad(w_in[:, QKV_WIDTH:], ((0, 0), (0, GATE_PAD - N_HEADS)))
    b_pad = jnp.pad(vecs["b_forget"], (0, GATE_PAD - N_HEADS)).reshape(1, GATE_PAD)

    h1, zd, zf, gate = _in_proj(x0, vecs["g_mix"], jnp.concatenate([w_qkv, w_gate], axis=1), "in_proj")
    mn = _rmsnorm(mem2, vecs["g_mem"], "norm_mem")
    c_bc, c_row, sg = _gate_fwd(gate, b_pad, n_batch, "gate_fwd")
    ya, lses = _dil_fwd(zd, n_batch, "dil_fwd")
    yf, of32, lse_f = _fox_fwd(zf, c_bc, c_row, n_batch, "fox_fwd")
    wts = late_weights(yf)
    w_out = wts["w_out"]
    x1, h2 = _matmul_res_norm([ya, yf], [w_out[:MIX_HALF], w_out[MIX_HALF:]], x0, vecs["g_xattn"], "out")
    qx = _matmul(h2, wts["w_xq"], "xq", out_dtypes=(BF16,))[0]
    kx = _matmul(mn, wts["w_xk"], "xk", out_dtypes=(BF16,))[0]
    vx = _matmul(mn, wts["w_xv"], "xv", out_dtypes=(BF16,))[0]
    ox = _xattn_fwd(qx, kx, vx, n_batch, "xattn_fwd")
    x2, h3 = _matmul_res_norm([ox], [wts["w_xo"]], x1, vecs["g_mlp"], "xo")
    u, a2 = _matmul(h3, wts["w_up"], "mlp_up", out_dtypes=(BF16, BF16), epilogue=_relu2, tn=1024)
    loss, dx3, dx3b, dg_final = _loss_bwd(a2, wts["w_down"], x2, vecs["g_final"], tgt, "mlp_down_loss")

    du = _matmul(dx3b, wts["w_down"], "mlp_down_bwd", out_dtypes=(BF16,), extras=(u,), epilogue=_relu2_bwd, tn=1024, w_t=True)[0]
    shards = (N_CHIPS, 2 * D_MODEL, D_MODEL)
    g_mlp = _matmul_tn(h3, du, "gw_up", packed=(shards, lambda i, j: (j, 0, 0), None))
    g_mlp = _matmul_tn(a2, dx3b, "gw_down", packed=(shards, lambda i, j: (i, 1, 0), g_mlp))
    gw_up = g_mlp[:, :D_MODEL].transpose(1, 0, 2).reshape(D_MODEL, D_FF)
    gw_down = g_mlp[:, D_MODEL:].reshape(D_FF, D_MODEL)
    on_grads, on_swapped = hooks or (None, None)
    token = on_grads("mlp", g_mlp) if hooks else None
    dx2, dx2b, dg_mlp = _matmul_rms_bwd([du], [wts["w_up"]], x2, vecs["g_mlp"], dx3, "mlp_up_bwd", after=token)

    gw_xo = _matmul_tn(ox, dx2b, "gw_xo")
    token = on_swapped("mlp", dx2b) if hooks else None
    dox = _matmul(dx2b, wts["w_xo"], "xo_bwd", out_dtypes=(BF16,), w_t=True, after=token)[0]
    dqx, dkx, dvx = _xattn_bwd(qx, kx, vx, dox, n_batch, "xattn_bwd")
    gw_xq = _matmul_tn(h2, dqx, "gw_xq")
    gw_xk = _matmul_tn(mn, dkx, "gw_xk")
    gw_xv = _matmul_tn(mn, dvx, "gw_xv")
    dmn = _matmul(dkx, wts["w_xk"], "xk_bwd", w_t=True)[0]
    dmn = _matmul_res(dvx, wts["w_xv"], dmn, "xv_bwd", w_t=True)
    _, _, dg_mem = _rms_bwd(mem2, dmn, vecs["g_mem"], None, "norm_mem_bwd")
    dx1, dx1b, dg_xattn = _matmul_rms_bwd([dqx], [wts["w_xq"]], x1, vecs["g_xattn"], dx2, "xq_bwd", tm=1024)

    gw_out = jnp.concatenate([_matmul_tn(ya, dx1b, "gw_out_a"), _matmul_tn(yf, dx1b, "gw_out_f")], axis=0)
    token = on_grads("mid", dict(w_out=gw_out, w_xq=gw_xq, w_xk=gw_xk, w_xv=gw_xv, w_xo=gw_xo)) if hooks else None
    dy = _matmul(dx1b, w_out, "out_bwd", out_dtypes=(BF16,), w_t=True, after=token)[0]
    dz = _dil_bwd(zd, dy, ya, lses, n_batch, "dil_bwd")
    dz, dc = _fox_bwd(zf, of32, dy, lse_f, c_bc, c_row, dz, n_batch, "fox_bwd")
    dzg, db = _gate_bwd(dc.reshape(n_batch, N_HEADS, SEQ), sg, "gate_bwd")
    token = on_swapped("mid", dz) if hooks else None
    gw_pm = _matmul_tn(h1, dz, "gw_in_qkv", after=token)
    gw_in = jnp.concatenate([_pair_major_inv(gw_pm[:, :half]), _pair_major_inv(gw_pm[:, half:]),
                             _matmul_tn(h1, dzg, "gw_in_gate")[:, :N_HEADS]], axis=1)
    dx0, _, dg_mix = _matmul_rms_bwd([dz, dzg], [w_qkv, w_gate], x0, vecs["g_mix"], dx1, "in_bwd")

    gw = dict(w_in=gw_in, w_out=gw_out, w_xq=gw_xq, w_xk=gw_xk, w_xv=gw_xv, w_xo=gw_xo, w_up=gw_up, w_down=gw_down)
    gv = dict(g_mix=dg_mix, g_xattn=dg_xattn, g_mem=dg_mem, g_mlp=dg_mlp, g_final=dg_final, b_forget=db)
    return loss, dx0.reshape(x.shape), gw, gv


MESH = pl.DeviceIdType.MESH
ANY = pl.BlockSpec(memory_space=pl.ANY)


def _place():
    x, y, c = lax.axis_index("x"), lax.axis_index("y"), lax.axis_index("c")
    other_chips = [(1 - x, y), (x, 1 - y), (1 - x, 1 - y)]
    return x, y, c, other_chips


def _my_chip():
    return 2 * lax.axis_index("x") + lax.axis_index("y")


def _halves(rows, c, align):
    half = rows // 2
    assert rows % (2 * align) == 0, rows
    return pl.ds(pl.multiple_of(c * half, align), half), pl.ds(pl.multiple_of((1 - c) * half, align), half)


def _place_own(wall, pack):
    return lax.dynamic_update_slice(wall, pack[None], (_my_chip(), 0, 0))


HBM = pl.BlockSpec(memory_space=pltpu.HBM)
SEM = pl.BlockSpec(memory_space=pltpu.SEMAPHORE)
SPLIT_COPY = pltpu.CompilerParams(has_side_effects=pltpu.SideEffectType.DATAFLOW_SIDE_EFFECTING)


def _in_hbm(a):
    return pltpu.with_memory_space_constraint(a, pltpu.HBM)


def _start_call(start, src, land_shape, after, name):
    land = lax.empty(land_shape, src.dtype)

    def body(src_ref, land_ref, after_ref, send_sems, recv_sems, src_thru, land_thru, token):
        del after_ref, src_thru, land_thru
        start(src_ref, land_ref, send_sems, recv_sems)
        token[...] = jnp.zeros_like(token)

    return pl.pallas_call(
        body, name=name,
        out_shape=(pltpu.SemaphoreType.DMA((3,)), pltpu.SemaphoreType.DMA((3,)), pltpu.HBM(src.shape, src.dtype),
                   pltpu.HBM(land_shape, src.dtype), jax.ShapeDtypeStruct((8, LANES), F32)),
        in_specs=(HBM, HBM, ANY), out_specs=(SEM, SEM, HBM, HBM, pl.BlockSpec(memory_space=pltpu.VMEM)),
        input_output_aliases={0: 2, 1: 3}, compiler_params=SPLIT_COPY,
    )(_in_hbm(src), _in_hbm(land), after)


def _wait_call(body, started, after, name):
    send_sems, recv_sems, src, land, _ = started
    return pl.pallas_call(
        body, name=name,
        out_shape=(pltpu.HBM(src.shape, src.dtype), pltpu.HBM(land.shape, land.dtype)),
        in_specs=(HBM, HBM, SEM, SEM, ANY), out_specs=(HBM, HBM),
        input_output_aliases={0: 0, 1: 1}, compiler_params=SPLIT_COPY,
    )(src, land, send_sems, recv_sems, after)


def _gather_copies(p_ref, wall_ref, send_sems, recv_sems):
    x, y, c, chips = _place()
    me = 2 * x + y
    mine, _ = _halves(p_ref.shape[0], c, 16)
    out, back = [], []
    for k, chip in enumerate(chips):
        peer = dict(send_sem=send_sems.at[k], recv_sem=recv_sems.at[k], device_id=(chip[0], chip[1], c), device_id_type=MESH)
        out.append(pltpu.make_async_remote_copy(src_ref=p_ref.at[mine], dst_ref=wall_ref.at[me, mine], **peer))
        slab = wall_ref.at[2 * chip[0] + chip[1], mine]
        back.append(pltpu.make_async_remote_copy(src_ref=slab, dst_ref=slab, **peer))
    return out, back


def _gather_start(pack, after, name):
    def start(p_ref, wall_ref, send_sems, recv_sems):
        for cp in _gather_copies(p_ref, wall_ref, send_sems, recv_sems)[0]:
            cp.start()

    return _start_call(start, pack, (N_CHIPS,) + pack.shape, after, name)


def _gather_wait(started, after, name):
    def body(p_ref, wall_ref, send_sems, recv_sems, after_ref, p_dead, wall_out):
        del after_ref, p_dead, wall_out
        out, back = _gather_copies(p_ref, wall_ref, send_sems, recv_sems)
        for cp_out, cp_back in zip(out, back):
            cp_out.wait_send()
            cp_back.wait_recv()

    return _wait_call(body, started, after, name)


def _pass_on(wall, name):
    def body(w_in_ref, out_ref, send_sems, recv_sems):
        del w_in_ref
        x, y, c, chips = _place()
        mine, theirs = _halves(wall.shape[1], c, 16)
        sends = []
        for k, chip in enumerate(chips):
            slab = out_ref.at[2 * chip[0] + chip[1]]
            peer = dict(send_sem=send_sems.at[k], recv_sem=recv_sems.at[k], device_id=(x, y, 1 - c), device_id_type=MESH)
            cp = pltpu.make_async_remote_copy(src_ref=slab.at[mine], dst_ref=slab.at[mine], **peer)
            cp.start()
            sends.append((cp, pltpu.make_async_remote_copy(src_ref=slab.at[theirs], dst_ref=slab.at[theirs], **peer)))
        for cp, back in sends:
            back.wait_recv()
            cp.wait_send()

    return pl.pallas_call(
        body, in_specs=[ANY], out_specs=ANY, out_shape=jax.ShapeDtypeStruct(wall.shape, wall.dtype),
        scratch_shapes=[pltpu.SemaphoreType.DMA((3,))] * 2, input_output_aliases={0: 0}, name=name,
    )(wall)


def _swap_halves(g, name):
    half = g.shape[1] // 2

    def body(g_ref, out_ref, send_sem, recv_sem):
        x, y, c, _ = _place()
        _, theirs = _halves(g.shape[1], c, 8)
        cp = pltpu.make_async_remote_copy(src_ref=g_ref.at[:, theirs], dst_ref=out_ref, send_sem=send_sem, recv_sem=recv_sem,
                                          device_id=(x, y, 1 - c), device_id_type=MESH)
        cp.start()
        cp.wait()

    return pl.pallas_call(
        body, in_specs=[ANY], out_specs=ANY,
        out_shape=jax.ShapeDtypeStruct((N_CHIPS, half, g.shape[2]), F32),
        scratch_shapes=[pltpu.SemaphoreType.DMA, pltpu.SemaphoreType.DMA],
        name=name,
    )(g)


def _swap_copy(g_ref, land_ref, send_sems, recv_sems):
    x, y, c, _ = _place()
    _, theirs = _halves(g_ref.shape[1], c, 8)
    return pltpu.make_async_remote_copy(src_ref=g_ref.at[:, theirs], dst_ref=land_ref, send_sem=send_sems.at[0],
                                        recv_sem=recv_sems.at[0], device_id=(x, y, 1 - c), device_id_type=MESH)


def _swap_start(g, name):
    def start(g_ref, land_ref, send_sems, recv_sems):
        _swap_copy(g_ref, land_ref, send_sems, recv_sems).start()

    return _start_call(start, g, (N_CHIPS, g.shape[1] // 2, g.shape[2]), _core_index(), name)


def _swap_wait(started, after, name):
    def body(g_ref, land_ref, send_sems, recv_sems, after_ref, g_out, land_out):
        del after_ref, g_out, land_out
        cp = _swap_copy(g_ref, land_ref, send_sems, recv_sems)
        cp.wait_send()
        cp.wait_recv()

    return _wait_call(body, started, after, name)


def _core_index():
    return lax.axis_index("c").astype(jnp.int32).reshape(1)


def _row_tile(half):
    tile = max(t for t in range(16, 1025, 16) if half % t == 0)
    return tile, half // tile


def _add_sibling(g, got, name):
    half = g.shape[1] // 2
    tile, n_tiles = _row_tile(half)

    def body(c_ref, g_ref, got_ref, o_ref):
        o_ref[...] = (g_ref[...] + got_ref[...]).astype(BF16)

    width = g.shape[2]
    blk = pl.BlockSpec((None, tile, width), lambda s, i, c_ref: (s, i, 0))
    return pl.pallas_call(
        body,
        grid_spec=pltpu.PrefetchScalarGridSpec(
            num_scalar_prefetch=1, grid=(N_CHIPS, n_tiles),
            in_specs=[pl.BlockSpec((None, tile, width), lambda s, i, c_ref: (s, c_ref[0] * n_tiles + i, 0)), blk],
            out_specs=blk),
        out_shape=jax.ShapeDtypeStruct((N_CHIPS, half, width), BF16),
        name=name, compiler_params=_params("arbitrary", "arbitrary"),
    )(_core_index(), g, got)


def _exchange_copies(p_ref, land_ref, send_sems, recv_sems):
    x, y, c, chips = _place()
    me = 2 * x + y
    out, back = [], []
    for k, chip in enumerate(chips):
        peer = dict(send_sem=send_sems.at[k], recv_sem=recv_sems.at[k], device_id=(chip[0], chip[1], c), device_id_type=MESH)
        out.append(pltpu.make_async_remote_copy(src_ref=p_ref.at[2 * chip[0] + chip[1]], dst_ref=land_ref.at[me], **peer))
        slab = land_ref.at[2 * chip[0] + chip[1]]
        back.append(pltpu.make_async_remote_copy(src_ref=slab, dst_ref=slab, **peer))
    return out, back


def _with_own(got, part):
    me = _my_chip()
    return lax.dynamic_update_slice(got, lax.dynamic_slice(part, (me, 0, 0), (1,) + part.shape[1:]), (me, 0, 0))


def _exchange_start(part, name):
    def start(p_ref, land_ref, send_sems, recv_sems):
        for cp in _exchange_copies(p_ref, land_ref, send_sems, recv_sems)[0]:
            cp.start()

    return _start_call(start, part, part.shape, _core_index(), name)


def _exchange_wait(started, after, name):
    def body(p_ref, land_ref, send_sems, recv_sems, after_ref, p_dead, land_out):
        del after_ref, p_dead, land_out
        out, back = _exchange_copies(p_ref, land_ref, send_sems, recv_sems)
        for cp_out, cp_back in zip(out, back):
            cp_out.wait_send()
            cp_back.wait_recv()

    part, got = _wait_call(body, started, after, name)
    return _with_own(got, part)


def _sum_chips(parts, name):
    half, width = parts.shape[1:]
    tile, n_tiles = _row_tile(half)

    def body(c_ref, p0, p1, p2, p3, o_ref):
        f32 = lambda p: p[...].astype(F32)
        o_ref[...] = ((f32(p0) + f32(p1)) + f32(p2)) + f32(p3)

    def slab(s):
        return pl.BlockSpec((None, tile, width), lambda i, c_ref, s=s: (s, i, 0))

    return pl.pallas_call(
        body,
        grid_spec=pltpu.PrefetchScalarGridSpec(
            num_scalar_prefetch=1, grid=(n_tiles,),
            in_specs=[slab(s) for s in range(N_CHIPS)],
            out_specs=pl.BlockSpec((None, tile, width), lambda i, c_ref: (c_ref[0], i, 0))),
        out_shape=jax.ShapeDtypeStruct((2, half, width), F32),
        name=name, compiler_params=_params("arbitrary"),
    )(_core_index(), parts, parts, parts, parts)


def _share_halves(halves, name):
    def body(h_ref, out_ref, send_sem, recv_sem):
        del h_ref
        x, y, c, _ = _place()
        cp = pltpu.make_async_remote_copy(src_ref=out_ref.at[c], dst_ref=out_ref.at[c], send_sem=send_sem, recv_sem=recv_sem,
                                          device_id=(x, y, 1 - c), device_id_type=MESH)
        cp.start()
        pltpu.make_async_remote_copy(src_ref=out_ref.at[1 - c], dst_ref=out_ref.at[1 - c], send_sem=send_sem, recv_sem=recv_sem,
                                     device_id=(x, y, 1 - c), device_id_type=MESH).wait_recv()
        cp.wait_send()

    return pl.pallas_call(
        body, in_specs=[ANY], out_specs=ANY,
        out_shape=jax.ShapeDtypeStruct(halves.shape, halves.dtype),
        scratch_shapes=[pltpu.SemaphoreType.DMA] * 2,
        input_output_aliases={0: 0},
        name=name,
    )(halves)


def _reduce_parts(g, tag):
    return _add_sibling(g, _swap_halves(g, "swap_" + tag), "add_" + tag)


def _reduce_finish(got, tag):
    halves = _share_halves(_sum_chips(got, "sum_" + tag), "share_" + tag)
    return halves.reshape(2 * halves.shape[1], halves.shape[2])


SMALL_ROWS = 8


def _allreduce_small(v):
    def body(v_ref, out_ref, buf, send_sems, recv_sems):
        x, y, c, _ = _place()
        buf[4 * x + 2 * y + c] = v_ref[...]
        sends = []
        for k in range(1, N_DEV):
            px = 1 - x if k & 4 else x
            py = 1 - y if k & 2 else y
            pc = 1 - c if k & 1 else c
            cp = pltpu.make_async_remote_copy(src_ref=v_ref, dst_ref=buf.at[4 * x + 2 * y + c], send_sem=send_sems.at[k - 1],
                                              recv_sem=recv_sems.at[k - 1], device_id=(px, py, pc), device_id_type=MESH)
            cp.start()
            sends.append((cp, 4 * px + 2 * py + pc))
        for k, (cp, peer) in enumerate(sends):
            pltpu.make_async_remote_copy(src_ref=v_ref, dst_ref=buf.at[peer], send_sem=send_sems.at[k], recv_sem=recv_sems.at[k],
                                         device_id=(x, y, c), device_id_type=MESH).wait_recv()
        for cp, _ in sends:
            cp.wait_send()
        total = buf[0]
        for d in range(1, N_DEV):
            total = total + buf[d]
        out_ref[...] = total

    vmem = pl.BlockSpec(memory_space=pltpu.VMEM)
    return pl.pallas_call(
        body, in_specs=[vmem], out_specs=vmem,
        out_shape=jax.ShapeDtypeStruct(v.shape, v.dtype),
        scratch_shapes=[pltpu.VMEM((N_DEV,) + v.shape, v.dtype), pltpu.SemaphoreType.DMA((N_DEV - 1,)),
                        pltpu.SemaphoreType.DMA((N_DEV - 1,))],
        name="allreduce_small",
    )(v)


MATRICES = ("w_in", "w_out", "w_xq", "w_xk", "w_xv", "w_xo", "w_up", "w_down")
VECTORS = ("g_mix", "g_xattn", "g_mem", "g_mlp", "g_final", "b_forget")
WEIGHT_ORDER = ("g_mix", "w_in", "b_forget", "w_out", "g_xattn", "g_mem", "w_xq", "w_xk", "w_xv", "w_xo",
                "g_mlp", "w_up", "w_down", "g_final")
GROUPS = {"mlp": ("w_up", "w_down"), "mid": ("w_out", "w_xq", "w_xk", "w_xv", "w_xo"), "in": ("w_in",)}
LATE = GROUPS["mid"] + GROUPS["mlp"]
W_IN_SHARD = IN_WIDTH // N_CHIPS
SHARD_ROWS = {"w_out": 256, "w_xq": 256, "w_xk": 256, "w_xv": 256, "w_xo": 256, "w_up": 1024, "w_down": 1024}
PACK_ROWS = SHARD_ROWS
W_IN_PAD = -(-W_IN_SHARD // LANES) * LANES
ADAM_ROWS = 128


def _pack(parts, names):
    return jnp.concatenate([jnp.pad(parts[n], ((0, PACK_ROWS[n] - SHARD_ROWS[n]), (0, 0))) for n in names], axis=0)


def _unpack(a, names):
    out, pos = {}, 0
    for n in names:
        out[n] = a[..., pos:pos + SHARD_ROWS[n], :]
        pos += PACK_ROWS[n]
    return out


def _full_weights(wall, names):
    cols = lambda a: a.transpose(1, 0, 2).reshape(a.shape[1], -1)
    rows = lambda a: a.reshape(-1, a.shape[-1])
    if names == GROUPS["in"]:
        return {"w_in": cols(wall[:, :, :W_IN_SHARD])}
    return {n: cols(a) if n == "w_up" else rows(a) for n, a in _unpack(wall, names).items()}


def _shard_of(g, name, s):
    if name == "w_up":
        return g[:, s * D_MODEL:(s + 1) * D_MODEL]
    n = SHARD_ROWS[name]
    return g[s * n:(s + 1) * n]


def _pad_w_in(a):
    return jnp.pad(a, [(0, 0)] * (a.ndim - 1) + [(0, W_IN_PAD - W_IN_SHARD)])


def _pack_grads(gws, names):
    if names == GROUPS["in"]:
        return _pad_w_in(gws["w_in"].reshape(D_MODEL, N_CHIPS, W_IN_SHARD).transpose(1, 0, 2))
    return jnp.stack([_pack({n: _shard_of(gws[n], n, s) for n in names}, names) for s in range(N_CHIPS)])


def kernel(x, mem, g_mix, w_in, b_forget, w_out, g_xattn, g_mem, w_xq, w_xk, w_xv, w_xo, g_mlp, w_up, w_down, g_final, loss_target, m_g_mix, m_w_in, m_b_forget, m_w_out, m_g_xattn, m_g_mem, m_w_xq, m_w_xk, m_w_xv, m_w_xo, m_g_mlp, m_w_up, m_w_down, m_g_final, v_g_mix, v_w_in, v_b_forget, v_w_out, v_g_xattn, v_g_mem, v_w_xq, v_w_xk, v_w_xv, v_w_xo, v_g_mlp, v_w_up, v_w_down, v_g_final):
    given = dict(locals())
    weights = {n: given[n] for n in WEIGHT_ORDER}
    vecs = {n: weights[n] for n in VECTORS}

    shard = {n: weights[n].astype(BF16) for n in MATRICES}
    in_started = _gather_start(_pad_w_in(shard["w_in"]), _core_index(), "gather_in_start")
    late_pack = _pack(shard, LATE)
    in_pack, in_wall = _gather_wait(in_started, late_pack, "gather_in_wait")
    in_wall = _place_own(_pass_on(in_wall, "gather_in_pass"), in_pack)
    late = _gather_start(late_pack, in_wall, "gather_late_start")
    w_in_full = _full_weights(in_wall, GROUPS["in"])["w_in"]

    def late_weights(after):
        pack, wall = _gather_wait(late, after, "gather_late_wait")
        return _full_weights(_place_own(_pass_on(wall, "gather_late_pass"), pack), LATE)

    started = {}

    swapping = {}

    def on_grads(group, gws):
        packed = gws if group == "mlp" else _pack_grads(gws, GROUPS[group])
        swapping[group] = _swap_start(packed, "swap_%s_start" % group)
        return swapping[group][4]

    def on_swapped(group, after):
        g, got = _swap_wait(swapping[group], after, "swap_%s_wait" % group)
        started[group] = _exchange_start(_add_sibling(g, got, "add_" + group), "exchange_%s_start" % group)
        return started[group][4]

    loss, grad_x, gw, gv = _local_step(x, mem, loss_target, vecs, w_in_full, late_weights, (on_grads, on_swapped))

    part = _reduce_parts(_pack_grads(gw, GROUPS["in"]), "in")
    started["in"] = _exchange_start(part, "exchange_in_start")
    grads, delta, new_m, new_v = {}, {}, {}, {}

    def finish(group, after):
        got = _exchange_wait(started[group], after, "exchange_%s_wait" % group)
        done = _reduce_finish(got, group)
        for n, a in ({"w_in": done[:, :W_IN_SHARD]} if group == "in" else _unpack(done, GROUPS[group])).items():
            grads[n] = a.reshape(weights[n].shape)
            delta[n], new_m[n], new_v[n] = _adamw(weights[n], grads[n], given["m_" + n], given["v_" + n], "adamw_" + n, ADAM_ROWS)
        return new_v[GROUPS[group][-1]]

    after = finish("mlp", started["in"][4])
    after = finish("mid", after)

    row = lambda a: jnp.pad(a.reshape(-1), (0, D_MODEL - a.size)).reshape(1, D_MODEL)
    small = jnp.concatenate([gv[n] for n in VECTORS[:5]] + [row(gv["b_forget"][:, 0]), row(loss[0, :1]),
                             jnp.zeros((1, D_MODEL), F32)], axis=0)
    small = _allreduce_small(small)
    for k, n in enumerate(VECTORS[:5]):
        grads[n] = small[k]
    grads["b_forget"] = small[5, :N_HEADS]
    loss_total = small[6, 0]
    finish("in", after)

    stack = lambda prefix: jnp.concatenate([row(given[prefix + n]) for n in VECTORS] + [jnp.zeros((2, D_MODEL), F32)], axis=0)
    g_small = jnp.concatenate([small[:6], jnp.zeros((2, D_MODEL), F32)], axis=0)
    d, m1, v1 = _adamw(stack(""), g_small, stack("m_"), stack("v_"), "adamw_vectors", SMALL_ROWS)
    for k, n in enumerate(VECTORS):
        width = weights[n].shape[0]
        delta[n], new_m[n], new_v[n] = d[k, :width], m1[k, :width], v1[k, :width]

    return (loss_total, grad_x, *[grads[n] for n in WEIGHT_ORDER], *[delta[n] for n in WEIGHT_ORDER],
            *[new_m[n] for n in WEIGHT_ORDER], *[new_v[n] for n in WEIGHT_ORDER])
```

```python
import functools
import math

import jax
import jax.numpy as jnp
from jax import lax
from jax.experimental import pallas as pl
from jax.experimental.pallas import tpu as pltpu

F32 = jnp.float32
BF16 = jnp.bfloat16

D_MODEL = 1024
SEQ = 2048
N_MEM = 256
HEAD_DIM = 64
N_HEADS = 8
MIX_HALF = N_HEADS * HEAD_DIM
QKV_WIDTH = 6 * MIX_HALF
IN_WIDTH = QKV_WIDTH + N_HEADS
GATE_PAD = 128
BLOCK = 128
DILATIONS = (1, 4, 16)
X_HEADS = 4
X_HEAD_DIM = 256
D_FF = 4096
EPS = 1e-6
NEG = -1e30
ATT_SCALE = 1.0 / math.sqrt(HEAD_DIM)
X_SCALE = 1.0 / math.sqrt(X_HEAD_DIM)
LANES = 128
N_CHIPS = 4
N_DEV = 8

ADAM_LR = 0.001
ADAM_B1 = 0.9
ADAM_B2 = 0.999
ADAM_EPS = 1e-08
ADAM_WD = 0.01
ADAM_STEP = 10

VMEM_LIMIT = 48 * 1024 * 1024


def _params(*sem):
    return pltpu.CompilerParams(dimension_semantics=sem or None, vmem_limit_bytes=VMEM_LIMIT)


def _dot(a, b):
    return jnp.dot(a, b, preferred_element_type=F32)


def _dot_nt(a, b):
    return lax.dot_general(a, b, (((1,), (1,)), ((), ())), preferred_element_type=F32)


def _dot_tn(a, b):
    return lax.dot_general(a, b, (((0,), (0,)), ((), ())), preferred_element_type=F32)


def _dot_exact(x, e):
    hi = x.astype(BF16)
    r1 = x - hi.astype(F32)
    mid = r1.astype(BF16)
    lo = (r1 - mid.astype(F32)).astype(BF16)
    return _dot(hi, e) + _dot(mid, e) + _dot(lo, e)


def _head_mask(e):
    lane = lax.broadcasted_iota(jnp.int32, (1, LANES), 1)
    return (lane >= HEAD_DIM * e) & (lane < HEAD_DIM * (e + 1))


def _matmul(a, w, name, out_dtypes=(F32,), extras=(), epilogue=None, tm=1024, tn=1024, w_t=False, after=None):
    m, k = a.shape
    n = w.shape[0] if w_t else w.shape[1]
    tm, tn = min(tm, m), min(tn, n)
    assert m % tm == 0 and n % tn == 0, (name, a.shape, w.shape)
    n_ex = len(extras)
    order = () if after is None else (after,)

    def body(a_ref, w_ref, *rest):
        rest = rest[len(order):]
        acc = (_dot_nt if w_t else _dot)(a_ref[...], w_ref[...])
        res = (acc,) if epilogue is None else epilogue(acc, *[r[...] for r in rest[:n_ex]])
        for o_ref, r in zip(rest[n_ex:], res):
            o_ref[...] = r.astype(o_ref.dtype)

    tile = pl.BlockSpec((tm, tn), lambda i, j: (i, j))
    w_spec = pl.BlockSpec((tn, k), lambda i, j: (j, 0)) if w_t else pl.BlockSpec((k, tn), lambda i, j: (0, j))
    return pl.pallas_call(
        body, grid=(m // tm, n // tn),
        in_specs=[pl.BlockSpec((tm, k), lambda i, j: (i, 0)), w_spec] + [pl.BlockSpec(memory_space=pl.ANY)] * len(order) + [tile] * n_ex,
        out_specs=[tile] * len(out_dtypes),
        out_shape=[jax.ShapeDtypeStruct((m, n), dt) for dt in out_dtypes],
        name=name, compiler_params=_params("parallel", "arbitrary"),
    )(a, w, *order, *extras)


def _matmul_res(a, w, res, name, w_t=False):
    return _matmul(a, w, name, extras=(res,), epilogue=lambda acc, r: (r + acc,), w_t=w_t)[0]


def _matmul_tn(x, y, name, tm=1024, tn=1024, tk=4096, packed=None, after=None):
    t, m = x.shape
    _, n = y.shape
    tm, tn, tk = min(tm, m), min(tn, n), min(tk, t)
    assert m % tm == 0 and n % tn == 0 and t % tk == 0, (name, x.shape, y.shape)
    shape, place, into = packed or ((m, n), None, None)

    def body(x_ref, y_ref, *rest):
        o_ref = rest[-1]

        @pl.when(pl.program_id(2) == 0)
        def _():
            o_ref[...] = jnp.zeros_like(o_ref)

        o_ref[...] += _dot_tn(x_ref[...], y_ref[...])

    out_spec = (pl.BlockSpec((tm, tn), lambda i, j, k: (i, j)) if place is None
                else pl.BlockSpec((None, tm, tn), lambda i, j, k: place(i, j)))
    return pl.pallas_call(
        body, grid=(m // tm, n // tn, t // tk),
        in_specs=[pl.BlockSpec((tk, tm), lambda i, j, k: (k, i)), pl.BlockSpec((tk, tn), lambda i, j, k: (k, j))]
        + [pl.BlockSpec(memory_space=pl.ANY)] * ((into is not None) + (after is not None)),
        out_specs=out_spec, out_shape=jax.ShapeDtypeStruct(shape, F32),
        input_output_aliases={} if into is None else {2: 0},
        name=name, compiler_params=_params("parallel", "parallel", "arbitrary"),
    )(x, y, *(() if into is None else (into,)), *(() if after is None else (after,)))


def _rmsnorm(x, g, name, tm=512):
    t, d = x.shape
    tm = min(tm, t)

    def body(x_ref, g_ref, h_ref):
        xv = x_ref[...]
        r = lax.rsqrt(jnp.mean(xv * xv, axis=-1, keepdims=True) + EPS)
        h_ref[...] = (xv * r * g_ref[...]).astype(BF16)

    return pl.pallas_call(
        body, grid=(t // tm,),
        in_specs=[pl.BlockSpec((tm, d), lambda i: (i, 0)), pl.BlockSpec((1, d), lambda i: (0, 0))],
        out_specs=pl.BlockSpec((tm, d), lambda i: (i, 0)),
        out_shape=jax.ShapeDtypeStruct((t, d), BF16),
        name=name, compiler_params=_params("arbitrary"),
    )(x, g.reshape(1, d))


def _in_proj(x, g, w_all, name, tm=512):
    t, d = x.shape
    half = 3 * MIX_HALF

    def body(x_ref, g_ref, w_ref, h_ref, zd_ref, zf_ref, gate_ref):
        xv = x_ref[...]
        r = lax.rsqrt(jnp.mean(xv * xv, axis=-1, keepdims=True) + EPS)
        h = (xv * r * g_ref[...]).astype(BF16)
        h_ref[...] = h
        zd_ref[...] = _dot(h, w_ref[:, 0:half])
        zf_ref[...] = _dot(h, w_ref[:, half:2 * half]).astype(BF16)
        gate_ref[...] = _dot(h, w_ref[:, 2 * half:])

    row = lambda width: pl.BlockSpec((tm, width), lambda i: (i, 0))
    return pl.pallas_call(
        body, grid=(t // tm,),
        in_specs=[row(d), pl.BlockSpec((1, d), lambda i: (0, 0)), pl.BlockSpec(w_all.shape, lambda i: (0, 0))],
        out_specs=[row(d), row(half), row(half), row(GATE_PAD)],
        out_shape=[jax.ShapeDtypeStruct((t, d), BF16), jax.ShapeDtypeStruct((t, half), F32),
                   jax.ShapeDtypeStruct((t, half), BF16), jax.ShapeDtypeStruct((t, GATE_PAD), F32)],
        name=name, compiler_params=_params("arbitrary"),
    )(x, g.reshape(1, d), w_all)


def _rms_bwd_tile(xv, dh, g):
    d = xv.shape[-1]
    r = lax.rsqrt(jnp.mean(xv * xv, axis=-1, keepdims=True) + EPS)
    dyg = dh * g
    proj = jnp.sum(dyg * xv, axis=-1, keepdims=True)
    dx = r * dyg - xv * (r * r * r * (1.0 / d)) * proj
    return dx, dh * (xv * r)


def _rms_bwd(x, dh, g, dres, name, tm=512):
    t, d = x.shape
    tm = min(tm, t)
    has_res = dres is not None

    def body(x_ref, dh_ref, g_ref, *rest):
        if has_res:
            res_ref, dx_ref, dxb_ref, dg_ref = rest
        else:
            dx_ref, dxb_ref, dg_ref = rest
        dx, dg_rows = _rms_bwd_tile(x_ref[...], dh_ref[...], g_ref[...])
        if has_res:
            dx = res_ref[...] + dx
        dx_ref[...] = dx
        dxb_ref[...] = dx.astype(BF16)

        @pl.when(pl.program_id(0) == 0)
        def _():
            dg_ref[...] = jnp.zeros_like(dg_ref)

        dg_ref[...] += jnp.sum(dg_rows, axis=0, keepdims=True)

    row = pl.BlockSpec((tm, d), lambda i: (i, 0))
    vec = pl.BlockSpec((1, d), lambda i: (0, 0))
    return pl.pallas_call(
        body, grid=(t // tm,),
        in_specs=[row, row, vec] + ([row] if has_res else []),
        out_specs=[row, row, vec],
        out_shape=[jax.ShapeDtypeStruct((t, d), F32), jax.ShapeDtypeStruct((t, d), BF16), jax.ShapeDtypeStruct((1, d), F32)],
        name=name, compiler_params=_params("arbitrary"),
    )(x, dh, g.reshape(1, d), *((dres,) if has_res else ()))


def _row_dots(a_refs, w_refs, w_t):
    acc = None
    for a_ref, w_ref in zip(a_refs, w_refs):
        part = (_dot_nt if w_t else _dot)(a_ref[...], w_ref[...])
        acc = part if acc is None else acc + part
    return acc


def _row_specs(a_parts, w_parts, tm):
    specs = [pl.BlockSpec((tm, a.shape[1]), lambda i: (i, 0)) for a in a_parts]
    return specs + [pl.BlockSpec(w.shape, lambda i: (0, 0)) for w in w_parts]


def _matmul_res_norm(a_parts, w_parts, res, g, name, tm=1024):
    t, d = res.shape
    n = len(a_parts)

    def body(*refs):
        res_ref, g_ref, x_ref, h_ref = refs[2 * n:]
        xv = res_ref[...] + _row_dots(refs[:n], refs[n:2 * n], False)
        x_ref[...] = xv
        r = lax.rsqrt(jnp.mean(xv * xv, axis=-1, keepdims=True) + EPS)
        h_ref[...] = (xv * r * g_ref[...]).astype(BF16)

    row = pl.BlockSpec((tm, d), lambda i: (i, 0))
    return pl.pallas_call(
        body, grid=(t // tm,),
        in_specs=_row_specs(a_parts, w_parts, tm) + [row, pl.BlockSpec((1, d), lambda i: (0, 0))],
        out_specs=[row, row],
        out_shape=[jax.ShapeDtypeStruct((t, d), F32), jax.ShapeDtypeStruct((t, d), BF16)],
        name=name, compiler_params=_params("arbitrary"),
    )(*a_parts, *w_parts, res, g.reshape(1, d))


def _matmul_rms_bwd(a_parts, w_parts, x, g, dres, name, tm=512, after=None):
    t, d = x.shape
    n = len(a_parts)
    order = () if after is None else (after,)

    def body(*refs):
        x_ref, g_ref, res_ref = refs[2 * n:2 * n + 3]
        dx_ref, dxb_ref, dg_ref = refs[2 * n + 3 + len(order):]
        dx, dg_rows = _rms_bwd_tile(x_ref[...], _row_dots(refs[:n], refs[n:2 * n], True), g_ref[...])
        dx = res_ref[...] + dx
        dx_ref[...] = dx
        dxb_ref[...] = dx.astype(BF16)

        @pl.when(pl.program_id(0) == 0)
        def _():
            dg_ref[...] = jnp.zeros_like(dg_ref)

        dg_ref[...] += jnp.sum(dg_rows, axis=0, keepdims=True)

    row = pl.BlockSpec((tm, d), lambda i: (i, 0))
    vec = pl.BlockSpec((1, d), lambda i: (0, 0))
    return pl.pallas_call(
        body, grid=(t // tm,),
        in_specs=_row_specs(a_parts, w_parts, tm) + [row, vec, row] + [pl.BlockSpec(memory_space=pl.ANY)] * len(order),
        out_specs=[row, row, vec],
        out_shape=[jax.ShapeDtypeStruct((t, d), F32), jax.ShapeDtypeStruct((t, d), BF16), jax.ShapeDtypeStruct((1, d), F32)],
        name=name, compiler_params=_params("arbitrary"),
    )(*a_parts, *w_parts, x, g.reshape(1, d), dres, *order)


def _loss_bwd(a, w, res, g, target, name, tm=512):
    t, d = res.shape

    def body(a_ref, w_ref, x_ref, g_ref, t_ref, loss_ref, dx_ref, dxb_ref, dg_ref):
        xv = x_ref[...] + _dot(a_ref[...], w_ref[...])
        gv = g_ref[...]
        r = lax.rsqrt(jnp.mean(xv * xv, axis=-1, keepdims=True) + EPS)
        err = xv * r * gv - t_ref[...]
        dx, dg_rows = _rms_bwd_tile(xv, err * (1.0 / d), gv)
        dx_ref[...] = dx
        dxb_ref[...] = dx.astype(BF16)

        @pl.when(pl.program_id(0) == 0)
        def _():
            dg_ref[...] = jnp.zeros_like(dg_ref)
            loss_ref[...] = jnp.zeros_like(loss_ref)

        dg_ref[...] += jnp.sum(dg_rows, axis=0, keepdims=True)
        part = jnp.sum(jnp.sum(err * err, axis=0, keepdims=True), axis=1, keepdims=True) * (0.5 / d)
        loss_ref[...] += jnp.broadcast_to(part, loss_ref.shape)

    row = pl.BlockSpec((tm, d), lambda i: (i, 0))
    vec = pl.BlockSpec((1, d), lambda i: (0, 0))
    return pl.pallas_call(
        body, grid=(t // tm,),
        in_specs=_row_specs([a], [w], tm) + [row, vec, row],
        out_specs=[pl.BlockSpec((1, LANES), lambda i: (0, 0)), row, row, vec],
        out_shape=[jax.ShapeDtypeStruct((1, LANES), F32), jax.ShapeDtypeStruct((t, d), F32),
                   jax.ShapeDtypeStruct((t, d), BF16), jax.ShapeDtypeStruct((1, d), F32)],
        name=name, compiler_params=_params("arbitrary"),
    )(a, w, res, g.reshape(1, d), target)


def _tri(upper):
    r = lax.broadcasted_iota(jnp.int32, (LANES, LANES), 0)
    c = lax.broadcasted_iota(jnp.int32, (LANES, LANES), 1)
    return jnp.where((r <= c) if upper else (r >= c), 1.0, 0.0).astype(BF16)


def _gate_fwd(gate, b_pad, n_batch, name):
    s = SEQ
    nblk = s // LANES

    def body(g_ref, b_ref, cbc_ref, crow_ref, sg_ref, ct_ref):
        gz = g_ref[...] + b_ref[...]
        logf = jnp.minimum(gz, 0.0) - jnp.log(1.0 + jnp.exp(-jnp.abs(gz)))
        logf_t = logf.T
        sg_ref[...] = (1.0 / (1.0 + jnp.exp(gz))).T[0:N_HEADS]
        upper = _tri(True)
        carry = jnp.zeros((LANES, 1), F32)
        for blk in range(nblk):
            seg = _dot_exact(logf_t[:, blk * LANES:(blk + 1) * LANES], upper) + carry
            carry = seg[:, LANES - 1:LANES]
            ct_ref[:, blk * LANES:(blk + 1) * LANES] = seg
        ct = ct_ref[...]
        crow_ref[...] = ct[0:N_HEADS]
        c_col = ct.T
        lane = lax.broadcasted_iota(jnp.int32, (1, MIX_HALF), 1)
        acc = jnp.zeros((s, MIX_HALF), F32)
        for h in range(N_HEADS):
            acc = jnp.where((lane >= HEAD_DIM * h) & (lane < HEAD_DIM * (h + 1)), c_col[:, h:h + 1], acc)
        cbc_ref[...] = acc

    return pl.pallas_call(
        body, grid=(n_batch,),
        in_specs=[pl.BlockSpec((s, GATE_PAD), lambda b: (b, 0)), pl.BlockSpec((1, GATE_PAD), lambda b: (0, 0))],
        out_specs=[pl.BlockSpec((s, MIX_HALF), lambda b: (b, 0)),
                   pl.BlockSpec((None, N_HEADS, s), lambda b: (b, 0, 0)),
                   pl.BlockSpec((None, N_HEADS, s), lambda b: (b, 0, 0))],
        out_shape=[jax.ShapeDtypeStruct((n_batch * s, MIX_HALF), F32),
                   jax.ShapeDtypeStruct((n_batch, N_HEADS, s), F32),
                   jax.ShapeDtypeStruct((n_batch, N_HEADS, s), F32)],
        scratch_shapes=[pltpu.VMEM((LANES, s), F32)],
        name=name, compiler_params=_params("arbitrary"),
    )(gate, b_pad)


def _gate_bwd(dc, sg, name):
    n_batch, _, s = dc.shape
    nblk = s // LANES

    def body(dc_ref, sg_ref, dz_ref, db_ref, dt_ref):
        lower = _tri(False)
        dcv = dc_ref[...]
        carry = jnp.zeros((N_HEADS, 1), F32)
        dt_ref[...] = jnp.zeros_like(dt_ref)
        for blk in reversed(range(nblk)):
            seg = _dot_exact(dcv[:, blk * LANES:(blk + 1) * LANES], lower) + carry
            carry = seg[:, 0:1]
            dt_ref[0:N_HEADS, blk * LANES:(blk + 1) * LANES] = seg * sg_ref[:, blk * LANES:(blk + 1) * LANES]
        dg_t = dt_ref[...]
        dz_ref[...] = dg_t.T.astype(BF16)

        @pl.when(pl.program_id(0) == 0)
        def _():
            db_ref[...] = jnp.zeros_like(db_ref)

        db_ref[...] += jnp.broadcast_to(jnp.sum(dg_t[0:N_HEADS], axis=1, keepdims=True), db_ref.shape)

    return pl.pallas_call(
        body, grid=(n_batch,),
        in_specs=[pl.BlockSpec((None, N_HEADS, s), lambda b: (b, 0, 0)), pl.BlockSpec((None, N_HEADS, s), lambda b: (b, 0, 0))],
        out_specs=[pl.BlockSpec((s, GATE_PAD), lambda b: (b, 0)), pl.BlockSpec((N_HEADS, LANES), lambda b: (0, 0))],
        out_shape=[jax.ShapeDtypeStruct((n_batch * s, GATE_PAD), BF16), jax.ShapeDtypeStruct((N_HEADS, LANES), F32)],
        scratch_shapes=[pltpu.VMEM((LANES, s), F32)],
        name=name, compiler_params=_params("arbitrary"),
    )(dc, sg)


FOX_BQ = 512
FOX_BK = 512
FOX_STRIP = 512
PAIR_WIDTH = 3 * LANES
N_PAIRS = N_HEADS // 2


def _pair_major(w):
    return w.reshape(w.shape[0], 3, N_PAIRS, LANES).transpose(0, 2, 1, 3).reshape(w.shape[0], 3 * MIX_HALF)


def _pair_major_inv(w):
    return w.reshape(w.shape[0], N_PAIRS, 3, LANES).transpose(0, 2, 1, 3).reshape(w.shape[0], 3 * MIX_HALF)


def _causal(i, j, bq, bk):
    qpos = i * bq + lax.broadcasted_iota(jnp.int32, (bq, 1), 0)
    kpos = j * bk + lax.broadcasted_iota(jnp.int32, (1, bk), 1)
    return kpos <= qpos


def _split_bf16(p):
    hi = p.astype(BF16)
    return hi, (p - hi.astype(F32)).astype(BF16)


def _fox_fwd(zf, c_bc, c_row, n_batch, name):
    s, bq, bk = SEQ, FOX_BQ, FOX_BK
    nq = s // bq
    t = n_batch * s

    n_strip = bq // FOX_STRIP

    def body(q_ref, k_ref, v_ref, cq_ref, cr_ref, o_ref, o32_ref, lse_ref):
        hp = pl.program_id(1)
        strips = [slice(r * FOX_STRIP, (r + 1) * FOX_STRIP) for r in range(n_strip)]
        chains = [(e, r) for e in range(2) for r in range(n_strip)]
        qh, cq = {}, {}
        for e, r in chains:
            q = q_ref[strips[r], :] * ATT_SCALE
            qh[e, r] = jnp.where(_head_mask(e), q, jnp.zeros_like(q))
            cq[e, r] = cq_ref[strips[r], HEAD_DIM * e:HEAD_DIM * e + 1]

        def step(i, j, carry, masked):
            rows = pl.ds(j * bk, bk)
            kj, vj = k_ref[rows, :], v_ref[rows, :]
            ck = [cr_ref[pl.ds(2 * hp + e, 1), rows] for e in range(2)]
            out = []
            scores = [_dot_nt(qh[e, r], kj) for e, r in chains]
            for n, (e, r) in enumerate(chains):
                m, l, acc = carry[3 * n:3 * n + 3]
                sc = scores[n] + (cq[e, r] - ck[e])
                if masked:
                    qpos = i * bq + r * FOX_STRIP + lax.broadcasted_iota(jnp.int32, (FOX_STRIP, 1), 0)
                    kpos = j * bk + lax.broadcasted_iota(jnp.int32, (1, bk), 1)
                    sc = jnp.where(kpos <= qpos, sc, NEG)
                m_new = jnp.maximum(m, jnp.max(sc, axis=1, keepdims=True))
                alpha = jnp.exp(m - m_new)
                p = jnp.exp(sc - m_new)
                p_hi, p_lo = _split_bf16(p)
                out += [m_new, alpha * l + jnp.sum(p, axis=1, keepdims=True), alpha * acc + (_dot(p_hi, vj) + _dot(p_lo, vj))]
            return tuple(out)

        def run(i):
            carry = (jnp.full((FOX_STRIP, 1), NEG, F32), jnp.zeros((FOX_STRIP, 1), F32), jnp.zeros((FOX_STRIP, LANES), F32)) * len(chains)
            n_clear = (i * bq) // bk
            for j in range((i * bq + bq + bk - 1) // bk):
                carry = step(i, j, carry, masked=j >= n_clear)
            for r in range(n_strip):
                outs = [carry[3 * (e * n_strip + r) + 2] / carry[3 * (e * n_strip + r) + 1] for e in range(2)]
                lses = [carry[3 * (e * n_strip + r)] + jnp.log(carry[3 * (e * n_strip + r) + 1]) for e in range(2)]
                o = jnp.where(_head_mask(0), outs[0], outs[1])
                o_ref[strips[r], :] = o.astype(BF16)
                o32_ref[strips[r], :] = o
                lse_ref[strips[r], :] = jnp.where(_head_mask(0), lses[0], lses[1])

        for k in range(nq):
            pl.when(pl.program_id(2) == k)(functools.partial(run, k))

    def col(c0):
        return lambda b, hp, i: (b, 3 * hp + c0)

    blk = pl.BlockSpec((bq, LANES), lambda b, hp, i: (b * nq + i, hp))
    return pl.pallas_call(
        body, grid=(n_batch, N_PAIRS, nq),
        in_specs=[pl.BlockSpec((bq, LANES), lambda b, hp, i: (b * nq + i, 3 * hp)),
                  pl.BlockSpec((s, LANES), col(1)), pl.BlockSpec((s, LANES), col(2)), blk,
                  pl.BlockSpec((None, N_HEADS, s), lambda b, hp, i: (b, 0, 0))],
        out_specs=[blk, blk, blk],
        out_shape=[jax.ShapeDtypeStruct((t, MIX_HALF), BF16), jax.ShapeDtypeStruct((t, MIX_HALF), F32),
                   jax.ShapeDtypeStruct((t, MIX_HALF), F32)],
        name=name, compiler_params=_params("parallel", "parallel", "arbitrary"),
    )(zf, zf, zf, c_bc, c_row)


def _fox_bwd(zf, o32, dy, lse, c_bc, c_row, dz, n_batch, name):
    s, bq, bk = SEQ, FOX_BQ, FOX_BK
    nq, nk = s // bq, s // bk

    def body(q_ref, k_ref, v_ref, o_ref, do_ref, lse_ref, cq_ref, cr_ref, dz_in, dz_ref, dc_ref, dq_acc):
        del dz_in
        hp = pl.program_id(1)

        @pl.when(pl.program_id(2) == 0)
        def _():
            dq_acc[...] = jnp.zeros_like(dq_acc)

        kj, vj = k_ref[...], v_ref[...]
        km = [jnp.where(_head_mask(e), kj, jnp.zeros_like(kj)) for e in range(2)]

        def step(i, j, ck, carry, masked):
            rows = pl.ds(i * bq, bq)
            qi, doi = q_ref[rows, :] * ATT_SCALE, do_ref[rows, :]
            prod = doi.astype(F32) * o_ref[rows, :]
            out = []
            dq = jnp.zeros((bq, LANES), F32)
            for e in range(2):
                dk_a, dv_a, dc_a = carry[3 * e:3 * e + 3]
                mask = _head_mask(e)
                lane0 = HEAD_DIM * e
                dom = jnp.where(mask, doi, jnp.zeros_like(doi))
                delta = jnp.sum(jnp.where(mask, prod, 0.0), axis=1, keepdims=True)
                sc = _dot_nt(qi, km[e]) + (cq_ref[rows, lane0:lane0 + 1] - ck[e])
                if masked:
                    sc = jnp.where(_causal(i, j, bq, bk), sc, NEG)
                p = jnp.exp(sc - lse_ref[rows, lane0:lane0 + 1])
                ds = p * (_dot_nt(dom, vj) - delta)
                dsb = ds.astype(BF16)
                dq = dq + _dot(dsb, km[e])
                out += [dk_a + _dot_tn(dsb, qi), dv_a + _dot_tn(p.astype(BF16), dom), dc_a - jnp.sum(ds, axis=0, keepdims=True)]
            dq_acc[rows, :] += dq * ATT_SCALE
            return tuple(out)

        def run(j):
            cols = pl.ds(j * bk, bk)
            ck = [cr_ref[pl.ds(2 * hp + e, 1), cols] for e in range(2)]
            carry = (jnp.zeros((bk, LANES), F32), jnp.zeros((bk, LANES), F32), jnp.zeros((1, bk), F32)) * 2
            n_diag = (j * bk + bk + bq - 1) // bq
            for i in range((j * bk) // bq, nq):
                carry = step(i, j, ck, carry, masked=i < n_diag)
            for e in range(2):
                dc_ref[e:e + 1, :] = carry[3 * e + 2]
            dz_ref[cols, LANES:2 * LANES] = jnp.where(_head_mask(0), carry[0], carry[3]).astype(BF16)
            dz_ref[cols, 2 * LANES:3 * LANES] = (carry[1] + carry[4]).astype(BF16)
            if j == nk - 1:
                dz_ref[:, 0:LANES] = dq_acc[...].astype(BF16)

        for k in range(nk):
            pl.when(pl.program_id(2) == k)(functools.partial(run, k))

    def seq(idx):
        return pl.BlockSpec((s, LANES), lambda b, hp, j: (b, idx(hp)))

    def kblk(c0):
        return pl.BlockSpec((bk, LANES), lambda b, hp, j: (b * nk + j, 3 * hp + c0))

    return pl.pallas_call(
        body, grid=(n_batch, N_PAIRS, nk),
        in_specs=[seq(lambda hp: 3 * hp), kblk(1), kblk(2), seq(lambda hp: hp), seq(lambda hp: N_PAIRS + hp),
                  seq(lambda hp: hp), seq(lambda hp: hp),
                  pl.BlockSpec((None, N_HEADS, s), lambda b, hp, j: (b, 0, 0)), pl.BlockSpec(memory_space=pl.ANY)],
        out_specs=[pl.BlockSpec((s, PAIR_WIDTH), lambda b, hp, j: (b, N_PAIRS + hp)),
                   pl.BlockSpec((None, None, 2, bk), lambda b, hp, j: (b, hp, 0, j))],
        out_shape=[jax.ShapeDtypeStruct(dz.shape, dz.dtype), jax.ShapeDtypeStruct((n_batch, N_PAIRS, 2, s), F32)],
        scratch_shapes=[pltpu.VMEM((s, LANES), F32)],
        input_output_aliases={8: 0},
        name=name, compiler_params=_params("parallel", "parallel", "arbitrary"),
    )(zf, zf, zf, o32, dy, lse, c_bc, c_row, dz)


def _dil_bias(slope, dil):
    qi = lax.broadcasted_iota(jnp.int32, (BLOCK, 2 * BLOCK), 0)
    kj = lax.broadcasted_iota(jnp.int32, (BLOCK, 2 * BLOCK), 1)
    delta = qi + BLOCK - kj
    return jnp.where((delta >= 0) & (delta <= BLOCK), (-slope * dil) * delta.astype(F32), NEG)


def _alibi_slope(hp, e):
    slope = jnp.float32(0.0)
    for k in range(N_PAIRS):
        slope = jnp.where(hp == k, jnp.float32(2.0 ** -(2 * k + e + 1)), slope)
    return slope


def _first_block_bias(bias):
    return jnp.where(lax.broadcasted_iota(jnp.int32, bias.shape, 1) < BLOCK, NEG, bias)


def _fill_bias(bias_scr, hp):
    for di, dil in enumerate(DILATIONS):
        for e in range(2):
            bias_scr[2 * di + e] = _dil_bias(_alibi_slope(hp, e), dil)


def _pair_specs(rows):
    return [pl.BlockSpec((rows, LANES), lambda b, hp, c0=c0: (b, 3 * hp + c0)) for c0 in range(3)]


def _strided(start, size, dil):
    return pl.ds(start, size) if dil == 1 else pl.ds(start, size, stride=dil)


QUARTER = SEQ // 4


def _to_quarters(src, dst):
    for r in range(4):
        dst[r * QUARTER:(r + 1) * QUARTER, :] = src[pl.ds(r, QUARTER, stride=4), :]


def _from_quarters(src, dst):
    for r in range(4):
        dst[pl.ds(r, QUARTER, stride=4), :] = src[r * QUARTER:(r + 1) * QUARTER, :]


def _mix_weights(l1, l2, l3):
    m = jnp.maximum(jnp.maximum(l1, l2), l3)
    e1, e2, e3 = jnp.exp(l1 - m), jnp.exp(l2 - m), jnp.exp(l3 - m)
    inv = 1.0 / (e1 + e2 + e3)
    return e1 * inv, e2 * inv, e3 * inv


def _dil_fwd(zd, n_batch, name):
    s = SEQ
    t = n_batch * s

    def body(q_ref, k_ref, v_ref, y_ref, l1_ref, l2_ref, l3_ref, o_scr, qkv4, o4, l4, bias_scr):
        _fill_bias(bias_scr, pl.program_id(1))
        for a, ref in enumerate((q_ref, k_ref, v_ref)):
            _to_quarters(ref, qkv4.at[a])

        def unit(srcs, start, first, stride, di, o_dst, l_dst):
            qrows = _strided(start, BLOCK, stride)
            krows = qrows if first else _strided(start - BLOCK * stride, 2 * BLOCK, stride)
            q = (srcs[0][qrows, :] * ATT_SCALE).astype(BF16)
            kc = srcs[1][krows, :].astype(BF16)
            vc = srcs[2][krows, :].astype(BF16)
            if first:
                kc, vc = jnp.concatenate([kc, kc]), jnp.concatenate([vc, vc])
            outs, lses = [], []
            for e in range(2):
                bias = _first_block_bias(bias_scr[2 * di + e]) if first else bias_scr[2 * di + e]
                sc = _dot_nt(jnp.where(_head_mask(e), q, jnp.zeros_like(q)), kc) + bias
                m = jnp.max(sc, axis=1, keepdims=True)
                pe = jnp.exp(sc - m)
                l = jnp.sum(pe, axis=1, keepdims=True)
                outs.append(_dot((pe * (1.0 / l)).astype(BF16), vc))
                lses.append(m + jnp.log(l))
            o_dst[qrows, :] = jnp.where(_head_mask(0), outs[0], outs[1])
            l_dst[qrows, :] = jnp.where(_head_mask(0), lses[0], lses[1])

        for n in range(SEQ // BLOCK):
            unit((q_ref, k_ref, v_ref), n * BLOCK, n == 0, 1, 0, o_scr.at[0], l1_ref)
        quarters = tuple(qkv4.at[a] for a in range(3))
        for di in (1, 2):
            stride = DILATIONS[di] // 4
            for r in range(4):
                for g in range(stride):
                    for n in range(QUARTER // (BLOCK * stride)):
                        unit(quarters, r * QUARTER + n * BLOCK * stride + g, n == 0, stride, di, o4.at[di - 1], l4.at[di - 1])
        for di, l_ref in ((1, l2_ref), (2, l3_ref)):
            _from_quarters(o4.at[di - 1], o_scr.at[di])
            _from_quarters(l4.at[di - 1], l_ref)
        w = _mix_weights(l1_ref[...], l2_ref[...], l3_ref[...])
        y_ref[...] = (w[0] * o_scr[0] + w[1] * o_scr[1] + w[2] * o_scr[2]).astype(BF16)

    blk = pl.BlockSpec((s, LANES), lambda b, hp: (b, hp))
    res = pl.pallas_call(
        body, grid=(n_batch, N_PAIRS),
        in_specs=_pair_specs(s),
        out_specs=[blk] * 4,
        out_shape=[jax.ShapeDtypeStruct((t, MIX_HALF), BF16)] + [jax.ShapeDtypeStruct((t, MIX_HALF), F32)] * 3,
        scratch_shapes=[pltpu.VMEM((3, s, LANES), F32), pltpu.VMEM((3, s, LANES), F32), pltpu.VMEM((2, s, LANES), F32),
                        pltpu.VMEM((2, s, LANES), F32), pltpu.VMEM((6, BLOCK, 2 * BLOCK), F32)],
        name=name, compiler_params=_params("parallel", "arbitrary"),
    )(zd, zd, zd)
    return res[0], res[1:]


def _dil_bwd(zd, dy, ya, lses, n_batch, name):
    s = SEQ
    t = n_batch * s

    def body(q_ref, k_ref, v_ref, dy_ref, ya_ref, l1_ref, l2_ref, l3_ref, dz_ref, w_scr, dy_scr, dot_scr, acc, st4, acc4, bias_scr):
        for di, dil in enumerate(DILATIONS):
            bias_scr[di] = jnp.concatenate([_dil_bias(_alibi_slope(pl.program_id(1), e), dil) for e in range(2)])
        for di, w in enumerate(_mix_weights(l1_ref[...], l2_ref[...], l3_ref[...])):
            w_scr[di] = w
        dya = dy_ref[...].astype(F32)
        prod = dya * ya_ref[...].astype(F32)
        per_head = [jnp.sum(jnp.where(_head_mask(e), prod, 0.0), axis=1, keepdims=True) for e in range(2)]
        dy_scr[...] = dya
        dot_scr[...] = jnp.where(_head_mask(0), per_head[0], per_head[1])
        acc[...] = jnp.zeros_like(acc)
        acc4[...] = jnp.zeros_like(acc4)
        staged = (q_ref, k_ref, v_ref, w_scr.at[1], w_scr.at[2], l2_ref, l3_ref, dy_scr, dot_scr)
        for a, ref in enumerate(staged):
            _to_quarters(ref, st4.at[a])

        def unit(srcs, dst, start, first, stride, di):
            qrows = _strided(start, BLOCK, stride)
            krows = qrows if first else _strided(start - BLOCK * stride, 2 * BLOCK, stride)
            q = (srcs[0][qrows, :] * ATT_SCALE).astype(BF16)
            kc = srcs[1][krows, :].astype(BF16)
            vc = srcs[2][krows, :].astype(BF16)
            wq = srcs[3][qrows, :]
            lse = srcs[4][qrows, :]
            do = (wq * srcs[5][qrows, :]).astype(BF16)
            sub = wq * srcs[6][qrows, :]
            heads = lambda a: jnp.concatenate([jnp.where(_head_mask(e), a, jnp.zeros_like(a)) for e in range(2)])
            column = lambda a: jnp.concatenate([a[:, HEAD_DIM * e:HEAD_DIM * e + 1] for e in range(2)])
            qq, dd = heads(q), heads(do)
            bias = bias_scr[di]
            p = jnp.exp(_dot_nt(qq, kc) + (bias[:, BLOCK:] if first else bias) - column(lse))
            dsb = (p * (_dot_nt(dd, vc) - column(sub))).astype(BF16)
            dq = _dot(jnp.concatenate([dsb[:BLOCK], dsb[BLOCK:]], axis=1), heads(kc))
            dst.at[0][qrows, :] += dq * ATT_SCALE
            dst.at[1][krows, :] += _dot_tn(dsb, qq)
            dst.at[2][krows, :] += _dot_tn(p.astype(BF16), dd)

        token_order = (q_ref, k_ref, v_ref, w_scr.at[0], l1_ref, dy_scr, dot_scr)
        for n in range(SEQ // BLOCK):
            unit(token_order, acc, n * BLOCK, n == 0, 1, 0)
        for di in (1, 2):
            quarters = (st4.at[0], st4.at[1], st4.at[2], st4.at[2 + di], st4.at[4 + di], st4.at[7], st4.at[8])
            stride = DILATIONS[di] // 4
            for r in range(4):
                for g in range(stride):
                    for n in range(QUARTER // (BLOCK * stride)):
                        unit(quarters, acc4, r * QUARTER + n * BLOCK * stride + g, n == 0, stride, di)
        for k in range(3):
            for r in range(4):
                acc.at[k][pl.ds(r, QUARTER, stride=4), :] += acc4[k, r * QUARTER:(r + 1) * QUARTER, :]
            dz_ref[:, k * LANES:(k + 1) * LANES] = acc[k].astype(BF16)

    blk = pl.BlockSpec((s, LANES), lambda b, hp: (b, hp))
    pair = pl.BlockSpec((s, PAIR_WIDTH), lambda b, hp: (b, hp))
    return pl.pallas_call(
        body, grid=(n_batch, N_PAIRS),
        in_specs=_pair_specs(s) + [blk] * 5,
        out_specs=pair,
        out_shape=jax.ShapeDtypeStruct((t, 2 * 3 * MIX_HALF), BF16),
        scratch_shapes=[pltpu.VMEM((3, s, LANES), F32), pltpu.VMEM((s, LANES), F32), pltpu.VMEM((s, LANES), F32),
                        pltpu.VMEM((3, s, LANES), F32), pltpu.VMEM((9, s, LANES), F32), pltpu.VMEM((3, s, LANES), F32),
                        pltpu.VMEM((3, 2 * BLOCK, 2 * BLOCK), F32)],
        name=name, compiler_params=_params("parallel", "arbitrary"),
    )(zd, zd, zd, dy, ya, *lses)


X_BQ = 2048


def _xattn_probs(q, k):
    sc = _dot_nt(q, k) * X_SCALE
    pe = jnp.exp(sc - jnp.max(sc, axis=1, keepdims=True))
    return pe / jnp.sum(pe, axis=1, keepdims=True)


def _xattn_fwd(qx, kx, vx, n_batch, name):
    nq = SEQ // X_BQ

    def body(q_ref, k_ref, v_ref, o_ref):
        p = _xattn_probs(q_ref[...], k_ref[...])
        o_ref[...] = _dot(p.astype(BF16), v_ref[...]).astype(BF16)

    qblk = pl.BlockSpec((X_BQ, X_HEAD_DIM), lambda b, h, i: (b * nq + i, h))
    kblk = pl.BlockSpec((N_MEM, X_HEAD_DIM), lambda b, h, i: (b, h))
    return pl.pallas_call(
        body, grid=(n_batch, X_HEADS, nq), in_specs=[qblk, kblk, kblk], out_specs=qblk,
        out_shape=jax.ShapeDtypeStruct(qx.shape, BF16),
        name=name, compiler_params=_params("parallel", "parallel", "arbitrary"),
    )(qx, kx, vx)


def _xattn_bwd(qx, kx, vx, dox, n_batch, name):
    nq = SEQ // X_BQ

    def body(q_ref, k_ref, v_ref, do_ref, dq_ref, dk_ref, dv_ref, dk_acc, dv_acc):
        i = pl.program_id(2)

        @pl.when(i == 0)
        def _():
            dk_acc[...] = jnp.zeros_like(dk_acc)
            dv_acc[...] = jnp.zeros_like(dv_acc)

        q, k, do = q_ref[...], k_ref[...], do_ref[...]
        p = _xattn_probs(q, k)
        dp = _dot_nt(do, v_ref[...])
        dsb = (p * (dp - jnp.sum(p * dp, axis=1, keepdims=True))).astype(BF16)
        dq_ref[...] = (_dot(dsb, k) * X_SCALE).astype(BF16)
        dk_acc[...] += _dot_tn(dsb, q) * X_SCALE
        dv_acc[...] += _dot_tn(p.astype(BF16), do)

        @pl.when(i == nq - 1)
        def _():
            dk_ref[...] = dk_acc[...].astype(BF16)
            dv_ref[...] = dv_acc[...].astype(BF16)

    qblk = pl.BlockSpec((X_BQ, X_HEAD_DIM), lambda b, h, i: (b * nq + i, h))
    kblk = pl.BlockSpec((N_MEM, X_HEAD_DIM), lambda b, h, i: (b, h))
    return pl.pallas_call(
        body, grid=(n_batch, X_HEADS, nq), in_specs=[qblk, kblk, kblk, qblk], out_specs=[qblk, kblk, kblk],
        out_shape=[jax.ShapeDtypeStruct(qx.shape, BF16), jax.ShapeDtypeStruct(kx.shape, BF16), jax.ShapeDtypeStruct(kx.shape, BF16)],
        scratch_shapes=[pltpu.VMEM((N_MEM, X_HEAD_DIM), F32)] * 2,
        name=name, compiler_params=_params("parallel", "parallel", "arbitrary"),
    )(qx, kx, vx, dox)


def _adamw(w, g, m, v, name, rows):
    r, c = w.shape
    assert r % rows == 0, (name, w.shape, rows)

    def body(w_ref, g_ref, m_ref, v_ref, d_ref, nm_ref, nv_ref):
        gv = g_ref[...]
        m1 = ADAM_B1 * m_ref[...] + (1.0 - ADAM_B1) * gv
        v1 = ADAM_B2 * v_ref[...] + (1.0 - ADAM_B2) * jnp.square(gv)
        m_hat = m1 / (1.0 - ADAM_B1 ** ADAM_STEP)
        v_hat = v1 / (1.0 - ADAM_B2 ** ADAM_STEP)
        d_ref[...] = -ADAM_LR * (m_hat / (jnp.sqrt(v_hat) + ADAM_EPS) + ADAM_WD * w_ref[...])
        nm_ref[...] = m1
        nv_ref[...] = v1

    blk = pl.BlockSpec((rows, c), lambda i: (i, 0))
    return pl.pallas_call(
        body, grid=(r // rows,), in_specs=[blk] * 4, out_specs=[blk] * 3,
        out_shape=[jax.ShapeDtypeStruct((r, c), F32)] * 3,
        name=name, compiler_params=_params("arbitrary"),
    )(w, g, m, v)


def _relu2(acc):
    a = jnp.maximum(acc, 0.0)
    return acc, a * a


def _relu2_bwd(acc, u):
    return (2.0 * jnp.maximum(u.astype(F32), 0.0) * acc,)


def _local_step(x, mem, target, vecs, w_in, late_weights, hooks=None):
    n_batch = x.shape[0]
    t = n_batch * SEQ
    x0 = x.reshape(t, D_MODEL)
    mem2 = mem.reshape(n_batch * N_MEM, D_MODEL)
    tgt = target.reshape(t, D_MODEL)

    half = 3 * MIX_HALF
    w_qkv = jnp.concatenate([_pair_major(w_in[:, :half]), _pair_major(w_in[:, half:QKV_WIDTH])], axis=1)
    w_gate = jnp.pad(w_in[:, QKV_WIDTH:], ((0, 0), (0, GATE_PAD - N_HEADS)))
    b_pad = jnp.pad(vecs["b_forget"], (0, GATE_PAD - N_HEADS)).reshape(1, GATE_PAD)

    h1, zd, zf, gate = _in_proj(x0, vecs["g_mix"], jnp.concatenate([w_qkv, w_gate], axis=1), "in_proj")
    mn = _rmsnorm(mem2, vecs["g_mem"], "norm_mem")
    c_bc, c_row, sg = _gate_fwd(gate, b_pad, n_batch, "gate_fwd")
    ya, lses = _dil_fwd(zd, n_batch, "dil_fwd")
    yf, of32, lse_f = _fox_fwd(zf, c_bc, c_row, n_batch, "fox_fwd")
    wts = late_weights(yf)
    w_out = wts["w_out"]
    x1, h2 = _matmul_res_norm([ya, yf], [w_out[:MIX_HALF], w_out[MIX_HALF:]], x0, vecs["g_xattn"], "out")
    qx = _matmul(h2, wts["w_xq"], "xq", out_dtypes=(BF16,))[0]
    kx = _matmul(mn, wts["w_xk"], "xk", out_dtypes=(BF16,))[0]
    vx = _matmul(mn, wts["w_xv"], "xv", out_dtypes=(BF16,))[0]
    ox = _xattn_fwd(qx, kx, vx, n_batch, "xattn_fwd")
    x2, h3 = _matmul_res_norm([ox], [wts["w_xo"]], x1, vecs["g_mlp"], "xo")
    u, a2 = _matmul(h3, wts["w_up"], "mlp_up", out_dtypes=(BF16, BF16), epilogue=_relu2)
    loss, dx3, dx3b, dg_final = _loss_bwd(a2, wts["w_down"], x2, vecs["g_final"], tgt, "mlp_down_loss")

    du = _matmul(dx3b, wts["w_down"], "mlp_down_bwd", out_dtypes=(BF16,), extras=(u,), epilogue=_relu2_bwd, w_t=True)[0]
    shards = (N_CHIPS, 2 * D_MODEL, D_MODEL)
    g_mlp = _matmul_tn(h3, du, "gw_up", packed=(shards, lambda i, j: (j, 0, 0), None))
    g_mlp = _matmul_tn(a2, dx3b, "gw_down", packed=(shards, lambda i, j: (i, 1, 0), g_mlp))
    gw_up = g_mlp[:, :D_MODEL].transpose(1, 0, 2).reshape(D_MODEL, D_FF)
    gw_down = g_mlp[:, D_MODEL:].reshape(D_FF, D_MODEL)
    on_grads, on_swapped = hooks or (None, None)
    token = on_grads("mlp", g_mlp) if hooks else None
    dx2, dx2b, dg_mlp = _matmul_rms_bwd([du], [wts["w_up"]], x2, vecs["g_mlp"], dx3, "mlp_up_bwd", after=token)

    gw_xo = _matmul_tn(ox, dx2b, "gw_xo")
    token = on_swapped("mlp", dx2b) if hooks else None
    dox = _matmul(dx2b, wts["w_xo"], "xo_bwd", out_dtypes=(BF16,), w_t=True, after=token)[0]
    dqx, dkx, dvx = _xattn_bwd(qx, kx, vx, dox, n_batch, "xattn_bwd")
    gw_xq = _matmul_tn(h2, dqx, "gw_xq")
    gw_xk = _matmul_tn(mn, dkx, "gw_xk")
    gw_xv = _matmul_tn(mn, dvx, "gw_xv")
    dmn = _matmul(dkx, wts["w_xk"], "xk_bwd", w_t=True)[0]
    dmn = _matmul_res(dvx, wts["w_xv"], dmn, "xv_bwd", w_t=True)
    _, _, dg_mem = _rms_bwd(mem2, dmn, vecs["g_mem"], None, "norm_mem_bwd")
    dx1, dx1b, dg_xattn = _matmul_rms_bwd([dqx], [wts["w_xq"]], x1, vecs["g_xattn"], dx2, "xq_bwd", tm=1024)

    gw_out = jnp.concatenate([_matmul_tn(ya, dx1b, "gw_out_a"), _matmul_tn(yf, dx1b, "gw_out_f")], axis=0)
    token = on_grads("mid", dict(w_out=gw_out, w_xq=gw_xq, w_xk=gw_xk, w_xv=gw_xv, w_xo=gw_xo)) if hooks else None
    dy = _matmul(dx1b, w_out, "out_bwd", out_dtypes=(BF16,), w_t=True, after=token)[0]
    dz = _dil_bwd(zd, dy, ya, lses, n_batch, "dil_bwd")
    dz, dc = _fox_bwd(zf, of32, dy, lse_f, c_bc, c_row, dz, n_batch, "fox_bwd")
    dzg, db = _gate_bwd(dc.reshape(n_batch, N_HEADS, SEQ), sg, "gate_bwd")
    token = on_swapped("mid", dz) if hooks else None
    gw_pm = _matmul_tn(h1, dz, "gw_in_qkv", after=token)
    gw_in = jnp.concatenate([_pair_major_inv(gw_pm[:, :half]), _pair_major_inv(gw_pm[:, half:]),
                             _matmul_tn(h1, dzg, "gw_in_gate")[:, :N_HEADS]], axis=1)
    dx0, _, dg_mix = _matmul_rms_bwd([dz, dzg], [w_qkv, w_gate], x0, vecs["g_mix"], dx1, "in_bwd")

    gw = dict(w_in=gw_in, w_out=gw_out, w_xq=gw_xq, w_xk=gw_xk, w_xv=gw_xv, w_xo=gw_xo, w_up=gw_up, w_down=gw_down)
    gv = dict(g_mix=dg_mix, g_xattn=dg_xattn, g_mem=dg_mem, g_mlp=dg_mlp, g_final=dg_final, b_forget=db)
    return loss, dx0.reshape(x.shape), gw, gv


MESH = pl.DeviceIdType.MESH
ANY = pl.BlockSpec(memory_space=pl.ANY)


def _place():
    x, y, c = lax.axis_index("x"), lax.axis_index("y"), lax.axis_index("c")
    other_chips = [(1 - x, y), (x, 1 - y), (1 - x, 1 - y)]
    return x, y, c, other_chips


def _my_chip():
    return 2 * lax.axis_index("x") + lax.axis_index("y")


def _halves(rows, c, align):
    half = rows // 2
    assert rows % (2 * align) == 0, rows
    return pl.ds(pl.multiple_of(c * half, align), half), pl.ds(pl.multiple_of((1 - c) * half, align), half)


def _place_own(wall, pack):
    return lax.dynamic_update_slice(wall, pack[None], (_my_chip(), 0, 0))


HBM = pl.BlockSpec(memory_space=pltpu.HBM)
SEM = pl.BlockSpec(memory_space=pltpu.SEMAPHORE)
SPLIT_COPY = pltpu.CompilerParams(has_side_effects=pltpu.SideEffectType.DATAFLOW_SIDE_EFFECTING)


def _in_hbm(a):
    return pltpu.with_memory_space_constraint(a, pltpu.HBM)


def _start_call(start, src, land_shape, after, name):
    land = lax.empty(land_shape, src.dtype)

    def body(src_ref, land_ref, after_ref, send_sems, recv_sems, src_thru, land_thru, token):
        del after_ref, src_thru, land_thru
        start(src_ref, land_ref, send_sems, recv_sems)
        token[...] = jnp.zeros_like(token)

    return pl.pallas_call(
        body, name=name,
        out_shape=(pltpu.SemaphoreType.DMA((3,)), pltpu.SemaphoreType.DMA((3,)), pltpu.HBM(src.shape, src.dtype),
                   pltpu.HBM(land_shape, src.dtype), jax.ShapeDtypeStruct((8, LANES), F32)),
        in_specs=(HBM, HBM, ANY), out_specs=(SEM, SEM, HBM, HBM, pl.BlockSpec(memory_space=pltpu.VMEM)),
        input_output_aliases={0: 2, 1: 3}, compiler_params=SPLIT_COPY,
    )(_in_hbm(src), _in_hbm(land), after)


def _wait_call(body, started, after, name):
    send_sems, recv_sems, src, land, _ = started
    return pl.pallas_call(
        body, name=name,
        out_shape=(pltpu.HBM(src.shape, src.dtype), pltpu.HBM(land.shape, land.dtype)),
        in_specs=(HBM, HBM, SEM, SEM, ANY), out_specs=(HBM, HBM),
        input_output_aliases={0: 0, 1: 1}, compiler_params=SPLIT_COPY,
    )(src, land, send_sems, recv_sems, after)


def _gather_copies(p_ref, wall_ref, send_sems, recv_sems):
    x, y, c, chips = _place()
    me = 2 * x + y
    mine, _ = _halves(p_ref.shape[0], c, 16)
    out, back = [], []
    for k, chip in enumerate(chips):
        peer = dict(send_sem=send_sems.at[k], recv_sem=recv_sems.at[k], device_id=(chip[0], chip[1], c), device_id_type=MESH)
        out.append(pltpu.make_async_remote_copy(src_ref=p_ref.at[mine], dst_ref=wall_ref.at[me, mine], **peer))
        slab = wall_ref.at[2 * chip[0] + chip[1], mine]
        back.append(pltpu.make_async_remote_copy(src_ref=slab, dst_ref=slab, **peer))
    return out, back


def _gather_start(pack, after, name):
    def start(p_ref, wall_ref, send_sems, recv_sems):
        for cp in _gather_copies(p_ref, wall_ref, send_sems, recv_sems)[0]:
            cp.start()

    return _start_call(start, pack, (N_CHIPS,) + pack.shape, after, name)


def _gather_wait(started, after, name):
    def body(p_ref, wall_ref, send_sems, recv_sems, after_ref, p_dead, wall_out):
        del after_ref, p_dead, wall_out
        out, back = _gather_copies(p_ref, wall_ref, send_sems, recv_sems)
        for cp_out, cp_back in zip(out, back):
            cp_out.wait_send()
            cp_back.wait_recv()

    return _wait_call(body, started, after, name)


def _pass_on(wall, name):
    def body(w_in_ref, out_ref, send_sems, recv_sems):
        del w_in_ref
        x, y, c, chips = _place()
        mine, theirs = _halves(wall.shape[1], c, 16)
        sends = []
        for k, chip in enumerate(chips):
            slab = out_ref.at[2 * chip[0] + chip[1]]
            peer = dict(send_sem=send_sems.at[k], recv_sem=recv_sems.at[k], device_id=(x, y, 1 - c), device_id_type=MESH)
            cp = pltpu.make_async_remote_copy(src_ref=slab.at[mine], dst_ref=slab.at[mine], **peer)
            cp.start()
            sends.append((cp, pltpu.make_async_remote_copy(src_ref=slab.at[theirs], dst_ref=slab.at[theirs], **peer)))
        for cp, back in sends:
            back.wait_recv()
            cp.wait_send()

    return pl.pallas_call(
        body, in_specs=[ANY], out_specs=ANY, out_shape=jax.ShapeDtypeStruct(wall.shape, wall.dtype),
        scratch_shapes=[pltpu.SemaphoreType.DMA((3,))] * 2, input_output_aliases={0: 0}, name=name,
    )(wall)


def _swap_halves(g, name):
    half = g.shape[1] // 2

    def body(g_ref, out_ref, send_sem, recv_sem):
        x, y, c, _ = _place()
        _, theirs = _halves(g.shape[1], c, 8)
        cp = pltpu.make_async_remote_copy(src_ref=g_ref.at[:, theirs], dst_ref=out_ref, send_sem=send_sem, recv_sem=recv_sem,
                                          device_id=(x, y, 1 - c), device_id_type=MESH)
        cp.start()
        cp.wait()

    return pl.pallas_call(
        body, in_specs=[ANY], out_specs=ANY,
        out_shape=jax.ShapeDtypeStruct((N_CHIPS, half, g.shape[2]), F32),
        scratch_shapes=[pltpu.SemaphoreType.DMA, pltpu.SemaphoreType.DMA],
        name=name,
    )(g)


def _swap_copy(g_ref, land_ref, send_sems, recv_sems):
    x, y, c, _ = _place()
    _, theirs = _halves(g_ref.shape[1], c, 8)
    return pltpu.make_async_remote_copy(src_ref=g_ref.at[:, theirs], dst_ref=land_ref, send_sem=send_sems.at[0],
                                        recv_sem=recv_sems.at[0], device_id=(x, y, 1 - c), device_id_type=MESH)


def _swap_start(g, name):
    def start(g_ref, land_ref, send_sems, recv_sems):
        _swap_copy(g_ref, land_ref, send_sems, recv_sems).start()

    return _start_call(start, g, (N_CHIPS, g.shape[1] // 2, g.shape[2]), _core_index(), name)


def _swap_wait(started, after, name):
    def body(g_ref, land_ref, send_sems, recv_sems, after_ref, g_out, land_out):
        del after_ref, g_out, land_out
        cp = _swap_copy(g_ref, land_ref, send_sems, recv_sems)
        cp.wait_send()
        cp.wait_recv()

    return _wait_call(body, started, after, name)


def _core_index():
    return lax.axis_index("c").astype(jnp.int32).reshape(1)


def _row_tile(half):
    tile = max(t for t in range(16, 1025, 16) if half % t == 0)
    return tile, half // tile


def _add_sibling(g, got, name):
    half = g.shape[1] // 2
    tile, n_tiles = _row_tile(half)

    def body(c_ref, g_ref, got_ref, o_ref):
        o_ref[...] = (g_ref[...] + got_ref[...]).astype(BF16)

    width = g.shape[2]
    blk = pl.BlockSpec((None, tile, width), lambda s, i, c_ref: (s, i, 0))
    return pl.pallas_call(
        body,
        grid_spec=pltpu.PrefetchScalarGridSpec(
            num_scalar_prefetch=1, grid=(N_CHIPS, n_tiles),
            in_specs=[pl.BlockSpec((None, tile, width), lambda s, i, c_ref: (s, c_ref[0] * n_tiles + i, 0)), blk],
            out_specs=blk),
        out_shape=jax.ShapeDtypeStruct((N_CHIPS, half, width), BF16),
        name=name, compiler_params=_params("arbitrary", "arbitrary"),
    )(_core_index(), g, got)


def _exchange_copies(p_ref, land_ref, send_sems, recv_sems):
    x, y, c, chips = _place()
    me = 2 * x + y
    out, back = [], []
    for k, chip in enumerate(chips):
        peer = dict(send_sem=send_sems.at[k], recv_sem=recv_sems.at[k], device_id=(chip[0], chip[1], c), device_id_type=MESH)
        out.append(pltpu.make_async_remote_copy(src_ref=p_ref.at[2 * chip[0] + chip[1]], dst_ref=land_ref.at[me], **peer))
        slab = land_ref.at[2 * chip[0] + chip[1]]
        back.append(pltpu.make_async_remote_copy(src_ref=slab, dst_ref=slab, **peer))
    return out, back


def _with_own(got, part):
    me = _my_chip()
    return lax.dynamic_update_slice(got, lax.dynamic_slice(part, (me, 0, 0), (1,) + part.shape[1:]), (me, 0, 0))


def _exchange_start(part, name):
    def start(p_ref, land_ref, send_sems, recv_sems):
        for cp in _exchange_copies(p_ref, land_ref, send_sems, recv_sems)[0]:
            cp.start()

    return _start_call(start, part, part.shape, _core_index(), name)


def _exchange_wait(started, after, name):
    def body(p_ref, land_ref, send_sems, recv_sems, after_ref, p_dead, land_out):
        del after_ref, p_dead, land_out
        out, back = _exchange_copies(p_ref, land_ref, send_sems, recv_sems)
        for cp_out, cp_back in zip(out, back):
            cp_out.wait_send()
            cp_back.wait_recv()

    part, got = _wait_call(body, started, after, name)
    return _with_own(got, part)


def _sum_chips(parts, name):
    half, width = parts.shape[1:]
    tile, n_tiles = _row_tile(half)

    def body(c_ref, p0, p1, p2, p3, o_ref):
        f32 = lambda p: p[...].astype(F32)
        o_ref[...] = ((f32(p0) + f32(p1)) + f32(p2)) + f32(p3)

    def slab(s):
        return pl.BlockSpec((None, tile, width), lambda i, c_ref, s=s: (s, i, 0))

    return pl.pallas_call(
        body,
        grid_spec=pltpu.PrefetchScalarGridSpec(
            num_scalar_prefetch=1, grid=(n_tiles,),
            in_specs=[slab(s) for s in range(N_CHIPS)],
            out_specs=pl.BlockSpec((None, tile, width), lambda i, c_ref: (c_ref[0], i, 0))),
        out_shape=jax.ShapeDtypeStruct((2, half, width), F32),
        name=name, compiler_params=_params("arbitrary"),
    )(_core_index(), parts, parts, parts, parts)


def _share_halves(halves, name):
    def body(h_ref, out_ref, send_sem, recv_sem):
        del h_ref
        x, y, c, _ = _place()
        cp = pltpu.make_async_remote_copy(src_ref=out_ref.at[c], dst_ref=out_ref.at[c], send_sem=send_sem, recv_sem=recv_sem,
                                          device_id=(x, y, 1 - c), device_id_type=MESH)
        cp.start()
        pltpu.make_async_remote_copy(src_ref=out_ref.at[1 - c], dst_ref=out_ref.at[1 - c], send_sem=send_sem, recv_sem=recv_sem,
                                     device_id=(x, y, 1 - c), device_id_type=MESH).wait_recv()
        cp.wait_send()

    return pl.pallas_call(
        body, in_specs=[ANY], out_specs=ANY,
        out_shape=jax.ShapeDtypeStruct(halves.shape, halves.dtype),
        scratch_shapes=[pltpu.SemaphoreType.DMA] * 2,
        input_output_aliases={0: 0},
        name=name,
    )(halves)


def _reduce_parts(g, tag):
    return _add_sibling(g, _swap_halves(g, "swap_" + tag), "add_" + tag)


def _reduce_finish(got, tag):
    halves = _share_halves(_sum_chips(got, "sum_" + tag), "share_" + tag)
    return halves.reshape(2 * halves.shape[1], halves.shape[2])


SMALL_ROWS = 8


def _allreduce_small(v):
    def body(v_ref, out_ref, buf, send_sems, recv_sems):
        x, y, c, _ = _place()
        buf[4 * x + 2 * y + c] = v_ref[...]
        sends = []
        for k in range(1, N_DEV):
            px = 1 - x if k & 4 else x
            py = 1 - y if k & 2 else y
            pc = 1 - c if k & 1 else c
            cp = pltpu.make_async_remote_copy(src_ref=v_ref, dst_ref=buf.at[4 * x + 2 * y + c], send_sem=send_sems.at[k - 1],
                                              recv_sem=recv_sems.at[k - 1], device_id=(px, py, pc), device_id_type=MESH)
            cp.start()
            sends.append((cp, 4 * px + 2 * py + pc))
        for k, (cp, peer) in enumerate(sends):
            pltpu.make_async_remote_copy(src_ref=v_ref, dst_ref=buf.at[peer], send_sem=send_sems.at[k], recv_sem=recv_sems.at[k],
                                         device_id=(x, y, c), device_id_type=MESH).wait_recv()
        for cp, _ in sends:
            cp.wait_send()
        total = buf[0]
        for d in range(1, N_DEV):
            total = total + buf[d]
        out_ref[...] = total

    vmem = pl.BlockSpec(memory_space=pltpu.VMEM)
    return pl.pallas_call(
        body, in_specs=[vmem], out_specs=vmem,
        out_shape=jax.ShapeDtypeStruct(v.shape, v.dtype),
        scratch_shapes=[pltpu.VMEM((N_DEV,) + v.shape, v.dtype), pltpu.SemaphoreType.DMA((N_DEV - 1,)),
                        pltpu.SemaphoreType.DMA((N_DEV - 1,))],
        name="allreduce_small",
    )(v)


MATRICES = ("w_in", "w_out", "w_xq", "w_xk", "w_xv", "w_xo", "w_up", "w_down")
VECTORS = ("g_mix", "g_xattn", "g_mem", "g_mlp", "g_final", "b_forget")
WEIGHT_ORDER = ("g_mix", "w_in", "b_forget", "w_out", "g_xattn", "g_mem", "w_xq", "w_xk", "w_xv", "w_xo",
                "g_mlp", "w_up", "w_down", "g_final")
GROUPS = {"mlp": ("w_up", "w_down"), "mid": ("w_out", "w_xq", "w_xk", "w_xv", "w_xo"), "in": ("w_in",)}
LATE = GROUPS["mid"] + GROUPS["mlp"]
W_IN_SHARD = IN_WIDTH // N_CHIPS
SHARD_ROWS = {"w_out": 256, "w_xq": 256, "w_xk": 256, "w_xv": 256, "w_xo": 256, "w_up": 1024, "w_down": 1024}
PACK_ROWS = SHARD_ROWS
W_IN_PAD = -(-W_IN_SHARD // LANES) * LANES
ADAM_ROWS = 256


def _pack(parts, names):
    return jnp.concatenate([jnp.pad(parts[n], ((0, PACK_ROWS[n] - SHARD_ROWS[n]), (0, 0))) for n in names], axis=0)


def _unpack(a, names):
    out, pos = {}, 0
    for n in names:
        out[n] = a[..., pos:pos + SHARD_ROWS[n], :]
        pos += PACK_ROWS[n]
    return out


def _full_weights(wall, names):
    cols = lambda a: a.transpose(1, 0, 2).reshape(a.shape[1], -1)
    rows = lambda a: a.reshape(-1, a.shape[-1])
    if names == GROUPS["in"]:
        return {"w_in": cols(wall[:, :, :W_IN_SHARD])}
    return {n: cols(a) if n == "w_up" else rows(a) for n, a in _unpack(wall, names).items()}


def _shard_of(g, name, s):
    if name == "w_up":
        return g[:, s * D_MODEL:(s + 1) * D_MODEL]
    n = SHARD_ROWS[name]
    return g[s * n:(s + 1) * n]


def _pad_w_in(a):
    return jnp.pad(a, [(0, 0)] * (a.ndim - 1) + [(0, W_IN_PAD - W_IN_SHARD)])


def _pack_grads(gws, names):
    if names == GROUPS["in"]:
        return _pad_w_in(gws["w_in"].reshape(D_MODEL, N_CHIPS, W_IN_SHARD).transpose(1, 0, 2))
    return jnp.stack([_pack({n: _shard_of(gws[n], n, s) for n in names}, names) for s in range(N_CHIPS)])


def kernel(x, mem, g_mix, w_in, b_forget, w_out, g_xattn, g_mem, w_xq, w_xk, w_xv, w_xo, g_mlp, w_up, w_down, g_final, loss_target, m_g_mix, m_w_in, m_b_forget, m_w_out, m_g_xattn, m_g_mem, m_w_xq, m_w_xk, m_w_xv, m_w_xo, m_g_mlp, m_w_up, m_w_down, m_g_final, v_g_mix, v_w_in, v_b_forget, v_w_out, v_g_xattn, v_g_mem, v_w_xq, v_w_xk, v_w_xv, v_w_xo, v_g_mlp, v_w_up, v_w_down, v_g_final):
    given = dict(locals())
    weights = {n: given[n] for n in WEIGHT_ORDER}
    vecs = {n: weights[n] for n in VECTORS}

    shard = {n: weights[n].astype(BF16) for n in MATRICES}
    in_started = _gather_start(_pad_w_in(shard["w_in"]), _core_index(), "gather_in_start")
    late_pack = _pack(shard, LATE)
    in_pack, in_wall = _gather_wait(in_started, late_pack, "gather_in_wait")
    in_wall = _place_own(_pass_on(in_wall, "gather_in_pass"), in_pack)
    late = _gather_start(late_pack, in_wall, "gather_late_start")
    w_in_full = _full_weights(in_wall, GROUPS["in"])["w_in"]

    def late_weights(after):
        pack, wall = _gather_wait(late, after, "gather_late_wait")
        return _full_weights(_place_own(_pass_on(wall, "gather_late_pass"), pack), LATE)

    started = {}

    swapping = {}

    def on_grads(group, gws):
        packed = gws if group == "mlp" else _pack_grads(gws, GROUPS[group])
        swapping[group] = _swap_start(packed, "swap_%s_start" % group)
        return swapping[group][4]

    def on_swapped(group, after):
        g, got = _swap_wait(swapping[group], after, "swap_%s_wait" % group)
        started[group] = _exchange_start(_add_sibling(g, got, "add_" + group), "exchange_%s_start" % group)
        return started[group][4]

    loss, grad_x, gw, gv = _local_step(x, mem, loss_target, vecs, w_in_full, late_weights, (on_grads, on_swapped))

    part = _reduce_parts(_pack_grads(gw, GROUPS["in"]), "in")
    started["in"] = _exchange_start(part, "exchange_in_start")
    grads, delta, new_m, new_v = {}, {}, {}, {}

    def finish(group, after):
        got = _exchange_wait(started[group], after, "exchange_%s_wait" % group)
        done = _reduce_finish(got, group)
        for n, a in ({"w_in": done[:, :W_IN_SHARD]} if group == "in" else _unpack(done, GROUPS[group])).items():
            grads[n] = a.reshape(weights[n].shape)
            delta[n], new_m[n], new_v[n] = _adamw(weights[n], grads[n], given["m_" + n], given["v_" + n], "adamw_" + n, ADAM_ROWS)
        return new_v[GROUPS[group][-1]]

    after = finish("mlp", started["in"][4])
    after = finish("mid", after)

    row = lambda a: jnp.pad(a.reshape(-1), (0, D_MODEL - a.size)).reshape(1, D_MODEL)
    small = jnp.concatenate([gv[n] for n in VECTORS[:5]] + [row(gv["b_forget"][:, 0]), row(loss[0, :1]),
                             jnp.zeros((1, D_MODEL), F32)], axis=0)
    small = _allreduce_small(small)
    for k, n in enumerate(VECTORS[:5]):
        grads[n] = small[k]
    grads["b_forget"] = small[5, :N_HEADS]
    loss_total = small[6, 0]
    finish("in", after)

    stack = lambda prefix: jnp.concatenate([row(given[prefix + n]) for n in VECTORS] + [jnp.zeros((2, D_MODEL), F32)], axis=0)
    g_small = jnp.concatenate([small[:6], jnp.zeros((2, D_MODEL), F32)], axis=0)
    d, m1, v1 = _adamw(stack(""), g_small, stack("m_"), stack("v_"), "adamw_vectors", SMALL_ROWS)
    for k, n in enumerate(VECTORS):
        width = weights[n].shape[0]
        delta[n], new_m[n], new_v[n] = d[k, :width], m1[k, :width], v1[k, :width]

    return (loss_total, grad_x, *[grads[n] for n in WEIGHT_ORDER], *[delta[n] for n in WEIGHT_ORDER],
            *[new_m[n] for n in WEIGHT_ORDER], *[new_v[n] for n in WEIGHT_ORDER])
```

```python
import functools
import math

import jax
import jax.numpy as jnp
from jax import lax
from jax.experimental import pallas as pl
from jax.experimental.pallas import tpu as pltpu

F32 = jnp.float32
BF16 = jnp.bfloat16

D_MODEL = 1024
SEQ = 2048
N_MEM = 256
HEAD_DIM = 64
N_HEADS = 8
MIX_HALF = N_HEADS * HEAD_DIM
QKV_WIDTH = 6 * MIX_HALF
IN_WIDTH = QKV_WIDTH + N_HEADS
GATE_PAD = 128
BLOCK = 128
DILATIONS = (1, 4, 16)
X_HEADS = 4
X_HEAD_DIM = 256
D_FF = 4096
EPS = 1e-6
NEG = -1e30
ATT_SCALE = 1.0 / math.sqrt(HEAD_DIM)
X_SCALE = 1.0 / math.sqrt(X_HEAD_DIM)
LANES = 128
N_CHIPS = 4
N_DEV = 8

ADAM_LR = 0.001
ADAM_B1 = 0.9
ADAM_B2 = 0.999
ADAM_EPS = 1e-08
ADAM_WD = 0.01
ADAM_STEP = 10

VMEM_LIMIT = 48 * 1024 * 1024


def _params(*sem):
    return pltpu.CompilerParams(dimension_semantics=sem or None, vmem_limit_bytes=VMEM_LIMIT)


def _dot(a, b):
    return jnp.dot(a, b, preferred_element_type=F32)


def _dot_nt(a, b):
    return lax.dot_general(a, b, (((1,), (1,)), ((), ())), preferred_element_type=F32)


def _dot_tn(a, b):
    return lax.dot_general(a, b, (((0,), (0,)), ((), ())), preferred_element_type=F32)


def _dot_exact(x, e):
    hi = x.astype(BF16)
    r1 = x - hi.astype(F32)
    mid = r1.astype(BF16)
    lo = (r1 - mid.astype(F32)).astype(BF16)
    return _dot(hi, e) + _dot(mid, e) + _dot(lo, e)


def _head_mask(e):
    lane = lax.broadcasted_iota(jnp.int32, (1, LANES), 1)
    return (lane >= HEAD_DIM * e) & (lane < HEAD_DIM * (e + 1))


def _matmul(a, w, name, out_dtypes=(F32,), extras=(), epilogue=None, tm=1024, tn=1024, w_t=False, after=None):
    m, k = a.shape
    n = w.shape[0] if w_t else w.shape[1]
    tm, tn = min(tm, m), min(tn, n)
    assert m % tm == 0 and n % tn == 0, (name, a.shape, w.shape)
    n_ex = len(extras)
    order = () if after is None else (after,)

    def body(a_ref, w_ref, *rest):
        rest = rest[len(order):]
        acc = (_dot_nt if w_t else _dot)(a_ref[...], w_ref[...])
        res = (acc,) if epilogue is None else epilogue(acc, *[r[...] for r in rest[:n_ex]])
        for o_ref, r in zip(rest[n_ex:], res):
            o_ref[...] = r.astype(o_ref.dtype)

    tile = pl.BlockSpec((tm, tn), lambda i, j: (i, j))
    w_spec = pl.BlockSpec((tn, k), lambda i, j: (j, 0)) if w_t else pl.BlockSpec((k, tn), lambda i, j: (0, j))
    return pl.pallas_call(
        body, grid=(m // tm, n // tn),
        in_specs=[pl.BlockSpec((tm, k), lambda i, j: (i, 0)), w_spec] + [pl.BlockSpec(memory_space=pl.ANY)] * len(order) + [tile] * n_ex,
        out_specs=[tile] * len(out_dtypes),
        out_shape=[jax.ShapeDtypeStruct((m, n), dt) for dt in out_dtypes],
        name=name, compiler_params=_params("parallel", "arbitrary"),
    )(a, w, *order, *extras)


def _matmul_res(a, w, res, name, w_t=False):
    return _matmul(a, w, name, extras=(res,), epilogue=lambda acc, r: (r + acc,), w_t=w_t)[0]


def _matmul_tn(x, y, name, tm=1024, tn=1024, tk=2048, packed=None, after=None):
    t, m = x.shape
    _, n = y.shape
    tm, tn, tk = min(tm, m), min(tn, n), min(tk, t)
    assert m % tm == 0 and n % tn == 0 and t % tk == 0, (name, x.shape, y.shape)
    shape, place, into = packed or ((m, n), None, None)

    def body(x_ref, y_ref, *rest):
        o_ref = rest[-1]

        @pl.when(pl.program_id(2) == 0)
        def _():
            o_ref[...] = jnp.zeros_like(o_ref)

        o_ref[...] += _dot_tn(x_ref[...], y_ref[...])

    out_spec = (pl.BlockSpec((tm, tn), lambda i, j, k: (i, j)) if place is None
                else pl.BlockSpec((None, tm, tn), lambda i, j, k: place(i, j)))
    return pl.pallas_call(
        body, grid=(m // tm, n // tn, t // tk),
        in_specs=[pl.BlockSpec((tk, tm), lambda i, j, k: (k, i)), pl.BlockSpec((tk, tn), lambda i, j, k: (k, j))]
        + [pl.BlockSpec(memory_space=pl.ANY)] * ((into is not None) + (after is not None)),
        out_specs=out_spec, out_shape=jax.ShapeDtypeStruct(shape, F32),
        input_output_aliases={} if into is None else {2: 0},
        name=name, compiler_params=_params("parallel", "parallel", "arbitrary"),
    )(x, y, *(() if into is None else (into,)), *(() if after is None else (after,)))


def _rmsnorm(x, g, name, tm=512):
    t, d = x.shape
    tm = min(tm, t)

    def body(x_ref, g_ref, h_ref):
        xv = x_ref[...]
        r = lax.rsqrt(jnp.mean(xv * xv, axis=-1, keepdims=True) + EPS)
        h_ref[...] = (xv * r * g_ref[...]).astype(BF16)

    return pl.pallas_call(
        body, grid=(t // tm,),
        in_specs=[pl.BlockSpec((tm, d), lambda i: (i, 0)), pl.BlockSpec((1, d), lambda i: (0, 0))],
        out_specs=pl.BlockSpec((tm, d), lambda i: (i, 0)),
        out_shape=jax.ShapeDtypeStruct((t, d), BF16),
        name=name, compiler_params=_params("arbitrary"),
    )(x, g.reshape(1, d))


def _in_proj(x, g, w_all, name, tm=512):
    t, d = x.shape
    half = 3 * MIX_HALF

    def body(x_ref, g_ref, w_ref, h_ref, zd_ref, zf_ref, gate_ref):
        xv = x_ref[...]
        r = lax.rsqrt(jnp.mean(xv * xv, axis=-1, keepdims=True) + EPS)
        h = (xv * r * g_ref[...]).astype(BF16)
        h_ref[...] = h
        zd_ref[...] = _dot(h, w_ref[:, 0:half])
        zf_ref[...] = _dot(h, w_ref[:, half:2 * half]).astype(BF16)
        gate_ref[...] = _dot(h, w_ref[:, 2 * half:])

    row = lambda width: pl.BlockSpec((tm, width), lambda i: (i, 0))
    return pl.pallas_call(
        body, grid=(t // tm,),
        in_specs=[row(d), pl.BlockSpec((1, d), lambda i: (0, 0)), pl.BlockSpec(w_all.shape, lambda i: (0, 0))],
        out_specs=[row(d), row(half), row(half), row(GATE_PAD)],
        out_shape=[jax.ShapeDtypeStruct((t, d), BF16), jax.ShapeDtypeStruct((t, half), F32),
                   jax.ShapeDtypeStruct((t, half), BF16), jax.ShapeDtypeStruct((t, GATE_PAD), F32)],
        name=name, compiler_params=_params("arbitrary"),
    )(x, g.reshape(1, d), w_all)


def _rms_bwd_tile(xv, dh, g):
    d = xv.shape[-1]
    r = lax.rsqrt(jnp.mean(xv * xv, axis=-1, keepdims=True) + EPS)
    dyg = dh * g
    proj = jnp.sum(dyg * xv, axis=-1, keepdims=True)
    dx = r * dyg - xv * (r * r * r * (1.0 / d)) * proj
    return dx, dh * (xv * r)


def _rms_bwd(x, dh, g, dres, name, tm=512):
    t, d = x.shape
    tm = min(tm, t)
    has_res = dres is not None

    def body(x_ref, dh_ref, g_ref, *rest):
        if has_res:
            res_ref, dx_ref, dxb_ref, dg_ref = rest
        else:
            dx_ref, dxb_ref, dg_ref = rest
        dx, dg_rows = _rms_bwd_tile(x_ref[...], dh_ref[...], g_ref[...])
        if has_res:
            dx = res_ref[...] + dx
        dx_ref[...] = dx
        dxb_ref[...] = dx.astype(BF16)

        @pl.when(pl.program_id(0) == 0)
        def _():
            dg_ref[...] = jnp.zeros_like(dg_ref)

        dg_ref[...] += jnp.sum(dg_rows, axis=0, keepdims=True)

    row = pl.BlockSpec((tm, d), lambda i: (i, 0))
    vec = pl.BlockSpec((1, d), lambda i: (0, 0))
    return pl.pallas_call(
        body, grid=(t // tm,),
        in_specs=[row, row, vec] + ([row] if has_res else []),
        out_specs=[row, row, vec],
        out_shape=[jax.ShapeDtypeStruct((t, d), F32), jax.ShapeDtypeStruct((t, d), BF16), jax.ShapeDtypeStruct((1, d), F32)],
        name=name, compiler_params=_params("arbitrary"),
    )(x, dh, g.reshape(1, d), *((dres,) if has_res else ()))


def _row_dots(a_refs, w_refs, w_t):
    acc = None
    for a_ref, w_ref in zip(a_refs, w_refs):
        part = (_dot_nt if w_t else _dot)(a_ref[...], w_ref[...])
        acc = part if acc is None else acc + part
    return acc


def _row_specs(a_parts, w_parts, tm):
    specs = [pl.BlockSpec((tm, a.shape[1]), lambda i: (i, 0)) for a in a_parts]
    return specs + [pl.BlockSpec(w.shape, lambda i: (0, 0)) for w in w_parts]


def _matmul_rms_bwd(a_parts, w_parts, x, g, dres, name, tm=512, after=None):
    t, d = x.shape
    n = len(a_parts)
    order = () if after is None else (after,)

    def body(*refs):
        x_ref, g_ref, res_ref = refs[2 * n:2 * n + 3]
        dx_ref, dxb_ref, dg_ref = refs[2 * n + 3 + len(order):]
        dx, dg_rows = _rms_bwd_tile(x_ref[...], _row_dots(refs[:n], refs[n:2 * n], True), g_ref[...])
        dx = res_ref[...] + dx
        dx_ref[...] = dx
        dxb_ref[...] = dx.astype(BF16)

        @pl.when(pl.program_id(0) == 0)
        def _():
            dg_ref[...] = jnp.zeros_like(dg_ref)

        dg_ref[...] += jnp.sum(dg_rows, axis=0, keepdims=True)

    row = pl.BlockSpec((tm, d), lambda i: (i, 0))
    vec = pl.BlockSpec((1, d), lambda i: (0, 0))
    return pl.pallas_call(
        body, grid=(t // tm,),
        in_specs=_row_specs(a_parts, w_parts, tm) + [row, vec, row] + [pl.BlockSpec(memory_space=pl.ANY)] * len(order),
        out_specs=[row, row, vec],
        out_shape=[jax.ShapeDtypeStruct((t, d), F32), jax.ShapeDtypeStruct((t, d), BF16), jax.ShapeDtypeStruct((1, d), F32)],
        name=name, compiler_params=_params("arbitrary"),
    )(*a_parts, *w_parts, x, g.reshape(1, d), dres, *order)


def _loss_bwd(a, w, res, g, target, name, tm=512):
    t, d = res.shape

    def body(a_ref, w_ref, x_ref, g_ref, t_ref, loss_ref, dx_ref, dxb_ref, dg_ref):
        xv = x_ref[...] + _dot(a_ref[...], w_ref[...])
        gv = g_ref[...]
        r = lax.rsqrt(jnp.mean(xv * xv, axis=-1, keepdims=True) + EPS)
        err = xv * r * gv - t_ref[...]
        dx, dg_rows = _rms_bwd_tile(xv, err * (1.0 / d), gv)
        dx_ref[...] = dx
        dxb_ref[...] = dx.astype(BF16)

        @pl.when(pl.program_id(0) == 0)
        def _():
            dg_ref[...] = jnp.zeros_like(dg_ref)
            loss_ref[...] = jnp.zeros_like(loss_ref)

        dg_ref[...] += jnp.sum(dg_rows, axis=0, keepdims=True)
        part = jnp.sum(jnp.sum(err * err, axis=0, keepdims=True), axis=1, keepdims=True) * (0.5 / d)
        loss_ref[...] += jnp.broadcast_to(part, loss_ref.shape)

    row = pl.BlockSpec((tm, d), lambda i: (i, 0))
    vec = pl.BlockSpec((1, d), lambda i: (0, 0))
    return pl.pallas_call(
        body, grid=(t // tm,),
        in_specs=_row_specs([a], [w], tm) + [row, vec, row],
        out_specs=[pl.BlockSpec((1, LANES), lambda i: (0, 0)), row, row, vec],
        out_shape=[jax.ShapeDtypeStruct((1, LANES), F32), jax.ShapeDtypeStruct((t, d), F32),
                   jax.ShapeDtypeStruct((t, d), BF16), jax.ShapeDtypeStruct((1, d), F32)],
        name=name, compiler_params=_params("arbitrary"),
    )(a, w, res, g.reshape(1, d), target)


def _tri(upper):
    r = lax.broadcasted_iota(jnp.int32, (LANES, LANES), 0)
    c = lax.broadcasted_iota(jnp.int32, (LANES, LANES), 1)
    return jnp.where((r <= c) if upper else (r >= c), 1.0, 0.0).astype(BF16)


def _gate_fwd(gate, b_pad, n_batch, name):
    s = SEQ
    nblk = s // LANES

    def body(g_ref, b_ref, cbc_ref, crow_ref, sg_ref, ct_ref):
        gz = g_ref[...] + b_ref[...]
        logf = jnp.minimum(gz, 0.0) - jnp.log(1.0 + jnp.exp(-jnp.abs(gz)))
        logf_t = logf.T
        sg_ref[...] = (1.0 / (1.0 + jnp.exp(gz))).T[0:N_HEADS]
        upper = _tri(True)
        carry = jnp.zeros((LANES, 1), F32)
        for blk in range(nblk):
            seg = _dot_exact(logf_t[:, blk * LANES:(blk + 1) * LANES], upper) + carry
            carry = seg[:, LANES - 1:LANES]
            ct_ref[:, blk * LANES:(blk + 1) * LANES] = seg
        ct = ct_ref[...]
        crow_ref[...] = ct[0:N_HEADS]
        c_col = ct.T
        lane = lax.broadcasted_iota(jnp.int32, (1, MIX_HALF), 1)
        acc = jnp.zeros((s, MIX_HALF), F32)
        for h in range(N_HEADS):
            acc = jnp.where((lane >= HEAD_DIM * h) & (lane < HEAD_DIM * (h + 1)), c_col[:, h:h + 1], acc)
        cbc_ref[...] = acc

    return pl.pallas_call(
        body, grid=(n_batch,),
        in_specs=[pl.BlockSpec((s, GATE_PAD), lambda b: (b, 0)), pl.BlockSpec((1, GATE_PAD), lambda b: (0, 0))],
        out_specs=[pl.BlockSpec((s, MIX_HALF), lambda b: (b, 0)),
                   pl.BlockSpec((None, N_HEADS, s), lambda b: (b, 0, 0)),
                   pl.BlockSpec((None, N_HEADS, s), lambda b: (b, 0, 0))],
        out_shape=[jax.ShapeDtypeStruct((n_batch * s, MIX_HALF), F32),
                   jax.ShapeDtypeStruct((n_batch, N_HEADS, s), F32),
                   jax.ShapeDtypeStruct((n_batch, N_HEADS, s), F32)],
        scratch_shapes=[pltpu.VMEM((LANES, s), F32)],
        name=name, compiler_params=_params("arbitrary"),
    )(gate, b_pad)


def _gate_bwd(dc, sg, name):
    n_batch, _, s = dc.shape
    nblk = s // LANES

    def body(dc_ref, sg_ref, dz_ref, db_ref, dt_ref):
        lower = _tri(False)
        dcv = dc_ref[...]
        carry = jnp.zeros((N_HEADS, 1), F32)
        dt_ref[...] = jnp.zeros_like(dt_ref)
        for blk in reversed(range(nblk)):
            seg = _dot_exact(dcv[:, blk * LANES:(blk + 1) * LANES], lower) + carry
            carry = seg[:, 0:1]
            dt_ref[0:N_HEADS, blk * LANES:(blk + 1) * LANES] = seg * sg_ref[:, blk * LANES:(blk + 1) * LANES]
        dg_t = dt_ref[...]
        dz_ref[...] = dg_t.T.astype(BF16)

        @pl.when(pl.program_id(0) == 0)
        def _():
            db_ref[...] = jnp.zeros_like(db_ref)

        db_ref[...] += jnp.broadcast_to(jnp.sum(dg_t[0:N_HEADS], axis=1, keepdims=True), db_ref.shape)

    return pl.pallas_call(
        body, grid=(n_batch,),
        in_specs=[pl.BlockSpec((None, N_HEADS, s), lambda b: (b, 0, 0)), pl.BlockSpec((None, N_HEADS, s), lambda b: (b, 0, 0))],
        out_specs=[pl.BlockSpec((s, GATE_PAD), lambda b: (b, 0)), pl.BlockSpec((N_HEADS, LANES), lambda b: (0, 0))],
        out_shape=[jax.ShapeDtypeStruct((n_batch * s, GATE_PAD), BF16), jax.ShapeDtypeStruct((N_HEADS, LANES), F32)],
        scratch_shapes=[pltpu.VMEM((LANES, s), F32)],
        name=name, compiler_params=_params("arbitrary"),
    )(dc, sg)


FOX_BQ = 512
FOX_BK = 512
FOX_STRIP = 512
PAIR_WIDTH = 3 * LANES
N_PAIRS = N_HEADS // 2


def _pair_major(w):
    return w.reshape(w.shape[0], 3, N_PAIRS, LANES).transpose(0, 2, 1, 3).reshape(w.shape[0], 3 * MIX_HALF)


def _pair_major_inv(w):
    return w.reshape(w.shape[0], N_PAIRS, 3, LANES).transpose(0, 2, 1, 3).reshape(w.shape[0], 3 * MIX_HALF)


def _causal(i, j, bq, bk):
    qpos = i * bq + lax.broadcasted_iota(jnp.int32, (bq, 1), 0)
    kpos = j * bk + lax.broadcasted_iota(jnp.int32, (1, bk), 1)
    return kpos <= qpos


def _split_bf16(p):
    hi = p.astype(BF16)
    return hi, (p - hi.astype(F32)).astype(BF16)


def _fox_fwd(zf, c_bc, c_row, n_batch, name):
    s, bq, bk = SEQ, FOX_BQ, FOX_BK
    nq = s // bq
    t = n_batch * s

    n_strip = bq // FOX_STRIP

    def body(q_ref, k_ref, v_ref, cq_ref, cr_ref, o_ref, o32_ref, lse_ref):
        hp = pl.program_id(1)
        strips = [slice(r * FOX_STRIP, (r + 1) * FOX_STRIP) for r in range(n_strip)]
        chains = [(e, r) for e in range(2) for r in range(n_strip)]
        qh, cq = {}, {}
        for e, r in chains:
            q = q_ref[strips[r], :] * ATT_SCALE
            qh[e, r] = jnp.where(_head_mask(e), q, jnp.zeros_like(q))
            cq[e, r] = cq_ref[strips[r], HEAD_DIM * e:HEAD_DIM * e + 1]

        def step(i, j, carry, masked):
            rows = pl.ds(j * bk, bk)
            kj, vj = k_ref[rows, :], v_ref[rows, :]
            ck = [cr_ref[pl.ds(2 * hp + e, 1), rows] for e in range(2)]
            out = []
            scores = [_dot_nt(qh[e, r], kj) for e, r in chains]
            for n, (e, r) in enumerate(chains):
                m, l, acc = carry[3 * n:3 * n + 3]
                sc = scores[n] + (cq[e, r] - ck[e])
                if masked:
                    qpos = i * bq + r * FOX_STRIP + lax.broadcasted_iota(jnp.int32, (FOX_STRIP, 1), 0)
                    kpos = j * bk + lax.broadcasted_iota(jnp.int32, (1, bk), 1)
                    sc = jnp.where(kpos <= qpos, sc, NEG)
                m_new = jnp.maximum(m, jnp.max(sc, axis=1, keepdims=True))
                alpha = jnp.exp(m - m_new)
                p = jnp.exp(sc - m_new)
                p_hi, p_lo = _split_bf16(p)
                out += [m_new, alpha * l + jnp.sum(p, axis=1, keepdims=True), alpha * acc + (_dot(p_hi, vj) + _dot(p_lo, vj))]
            return tuple(out)

        def run(i):
            carry = (jnp.full((FOX_STRIP, 1), NEG, F32), jnp.zeros((FOX_STRIP, 1), F32), jnp.zeros((FOX_STRIP, LANES), F32)) * len(chains)
            n_clear = (i * bq) // bk
            for j in range((i * bq + bq + bk - 1) // bk):
                carry = step(i, j, carry, masked=j >= n_clear)
            for r in range(n_strip):
                outs = [carry[3 * (e * n_strip + r) + 2] / carry[3 * (e * n_strip + r) + 1] for e in range(2)]
                lses = [carry[3 * (e * n_strip + r)] + jnp.log(carry[3 * (e * n_strip + r) + 1]) for e in range(2)]
                o = jnp.where(_head_mask(0), outs[0], outs[1])
                o_ref[strips[r], :] = o.astype(BF16)
                o32_ref[strips[r], :] = o
                lse_ref[strips[r], :] = jnp.where(_head_mask(0), lses[0], lses[1])

        for k in range(nq):
            pl.when(pl.program_id(2) == k)(functools.partial(run, k))

    def col(c0):
        return lambda b, hp, i: (b, 3 * hp + c0)

    blk = pl.BlockSpec((bq, LANES), lambda b, hp, i: (b * nq + i, hp))
    return pl.pallas_call(
        body, grid=(n_batch, N_PAIRS, nq),
        in_specs=[pl.BlockSpec((bq, LANES), lambda b, hp, i: (b * nq + i, 3 * hp)),
                  pl.BlockSpec((s, LANES), col(1)), pl.BlockSpec((s, LANES), col(2)), blk,
                  pl.BlockSpec((None, N_HEADS, s), lambda b, hp, i: (b, 0, 0))],
        out_specs=[blk, blk, blk],
        out_shape=[jax.ShapeDtypeStruct((t, MIX_HALF), BF16), jax.ShapeDtypeStruct((t, MIX_HALF), F32),
                   jax.ShapeDtypeStruct((t, MIX_HALF), F32)],
        name=name, compiler_params=_params("parallel", "parallel", "arbitrary"),
    )(zf, zf, zf, c_bc, c_row)


def _fox_bwd(zf, o32, dy, lse, c_bc, c_row, dz, n_batch, name):
    s, bq, bk = SEQ, FOX_BQ, FOX_BK
    nq, nk = s // bq, s // bk

    def body(q_ref, k_ref, v_ref, o_ref, do_ref, lse_ref, cq_ref, cr_ref, dz_in, dz_ref, dc_ref, dq_acc):
        del dz_in
        hp = pl.program_id(1)

        @pl.when(pl.program_id(2) == 0)
        def _():
            dq_acc[...] = jnp.zeros_like(dq_acc)

        kj, vj = k_ref[...], v_ref[...]
        km = [jnp.where(_head_mask(e), kj, jnp.zeros_like(kj)) for e in range(2)]

        def step(i, j, ck, carry, masked):
            rows = pl.ds(i * bq, bq)
            qi, doi = q_ref[rows, :] * ATT_SCALE, do_ref[rows, :]
            prod = doi.astype(F32) * o_ref[rows, :]
            out = []
            dq = jnp.zeros((bq, LANES), F32)
            for e in range(2):
                dk_a, dv_a, dc_a = carry[3 * e:3 * e + 3]
                mask = _head_mask(e)
                lane0 = HEAD_DIM * e
                dom = jnp.where(mask, doi, jnp.zeros_like(doi))
                delta = jnp.sum(jnp.where(mask, prod, 0.0), axis=1, keepdims=True)
                sc = _dot_nt(qi, km[e]) + (cq_ref[rows, lane0:lane0 + 1] - ck[e])
                if masked:
                    sc = jnp.where(_causal(i, j, bq, bk), sc, NEG)
                p = jnp.exp(sc - lse_ref[rows, lane0:lane0 + 1])
                ds = p * (_dot_nt(dom, vj) - delta)
                dsb = ds.astype(BF16)
                dq = dq + _dot(dsb, km[e])
                out += [dk_a + _dot_tn(dsb, qi), dv_a + _dot_tn(p.astype(BF16), dom), dc_a - jnp.sum(ds, axis=0, keepdims=True)]
            dq_acc[rows, :] += dq * ATT_SCALE
            return tuple(out)

        def run(j):
            cols = pl.ds(j * bk, bk)
            ck = [cr_ref[pl.ds(2 * hp + e, 1), cols] for e in range(2)]
            carry = (jnp.zeros((bk, LANES), F32), jnp.zeros((bk, LANES), F32), jnp.zeros((1, bk), F32)) * 2
            n_diag = (j * bk + bk + bq - 1) // bq
            for i in range((j * bk) // bq, nq):
                carry = step(i, j, ck, carry, masked=i < n_diag)
            for e in range(2):
                dc_ref[e:e + 1, :] = carry[3 * e + 2]
            dz_ref[cols, LANES:2 * LANES] = jnp.where(_head_mask(0), carry[0], carry[3]).astype(BF16)
            dz_ref[cols, 2 * LANES:3 * LANES] = (carry[1] + carry[4]).astype(BF16)
            if j == nk - 1:
                dz_ref[:, 0:LANES] = dq_acc[...].astype(BF16)

        for k in range(nk):
            pl.when(pl.program_id(2) == k)(functools.partial(run, k))

    def seq(idx):
        return pl.BlockSpec((s, LANES), lambda b, hp, j: (b, idx(hp)))

    def kblk(c0):
        return pl.BlockSpec((bk, LANES), lambda b, hp, j: (b * nk + j, 3 * hp + c0))

    return pl.pallas_call(
        body, grid=(n_batch, N_PAIRS, nk),
        in_specs=[seq(lambda hp: 3 * hp), kblk(1), kblk(2), seq(lambda hp: hp), seq(lambda hp: N_PAIRS + hp),
                  seq(lambda hp: hp), seq(lambda hp: hp),
                  pl.BlockSpec((None, N_HEADS, s), lambda b, hp, j: (b, 0, 0)), pl.BlockSpec(memory_space=pl.ANY)],
        out_specs=[pl.BlockSpec((s, PAIR_WIDTH), lambda b, hp, j: (b, N_PAIRS + hp)),
                   pl.BlockSpec((None, None, 2, bk), lambda b, hp, j: (b, hp, 0, j))],
        out_shape=[jax.ShapeDtypeStruct(dz.shape, dz.dtype), jax.ShapeDtypeStruct((n_batch, N_PAIRS, 2, s), F32)],
        scratch_shapes=[pltpu.VMEM((s, LANES), F32)],
        input_output_aliases={8: 0},
        name=name, compiler_params=_params("parallel", "parallel", "arbitrary"),
    )(zf, zf, zf, o32, dy, lse, c_bc, c_row, dz)


def _dil_bias(slope, dil):
    qi = lax.broadcasted_iota(jnp.int32, (BLOCK, 2 * BLOCK), 0)
    kj = lax.broadcasted_iota(jnp.int32, (BLOCK, 2 * BLOCK), 1)
    delta = qi + BLOCK - kj
    return jnp.where((delta >= 0) & (delta <= BLOCK), (-slope * dil) * delta.astype(F32), NEG)


def _alibi_slope(hp, e):
    slope = jnp.float32(0.0)
    for k in range(N_PAIRS):
        slope = jnp.where(hp == k, jnp.float32(2.0 ** -(2 * k + e + 1)), slope)
    return slope


def _first_block_bias(bias):
    return jnp.where(lax.broadcasted_iota(jnp.int32, bias.shape, 1) < BLOCK, NEG, bias)


def _fill_bias(bias_scr, hp):
    for di, dil in enumerate(DILATIONS):
        for e in range(2):
            bias_scr[2 * di + e] = _dil_bias(_alibi_slope(hp, e), dil)


def _pair_specs(rows):
    return [pl.BlockSpec((rows, LANES), lambda b, hp, c0=c0: (b, 3 * hp + c0)) for c0 in range(3)]


def _strided(start, size, dil):
    return pl.ds(start, size) if dil == 1 else pl.ds(start, size, stride=dil)


QUARTER = SEQ // 4


def _to_quarters(src, dst):
    for r in range(4):
        dst[r * QUARTER:(r + 1) * QUARTER, :] = src[pl.ds(r, QUARTER, stride=4), :]


def _from_quarters(src, dst):
    for r in range(4):
        dst[pl.ds(r, QUARTER, stride=4), :] = src[r * QUARTER:(r + 1) * QUARTER, :]


def _mix_weights(l1, l2, l3):
    m = jnp.maximum(jnp.maximum(l1, l2), l3)
    e1, e2, e3 = jnp.exp(l1 - m), jnp.exp(l2 - m), jnp.exp(l3 - m)
    inv = 1.0 / (e1 + e2 + e3)
    return e1 * inv, e2 * inv, e3 * inv


def _dil_fwd(zd, n_batch, name):
    s = SEQ
    t = n_batch * s

    def body(q_ref, k_ref, v_ref, y_ref, l1_ref, l2_ref, l3_ref, o_scr, qkv4, o4, l4, bias_scr):
        _fill_bias(bias_scr, pl.program_id(1))
        for a, ref in enumerate((q_ref, k_ref, v_ref)):
            _to_quarters(ref, qkv4.at[a])

        def unit(srcs, start, first, stride, di, o_dst, l_dst):
            qrows = _strided(start, BLOCK, stride)
            krows = qrows if first else _strided(start - BLOCK * stride, 2 * BLOCK, stride)
            q = (srcs[0][qrows, :] * ATT_SCALE).astype(BF16)
            kc = srcs[1][krows, :].astype(BF16)
            vc = srcs[2][krows, :].astype(BF16)
            if first:
                kc, vc = jnp.concatenate([kc, kc]), jnp.concatenate([vc, vc])
            outs, lses = [], []
            for e in range(2):
                bias = _first_block_bias(bias_scr[2 * di + e]) if first else bias_scr[2 * di + e]
                sc = _dot_nt(jnp.where(_head_mask(e), q, jnp.zeros_like(q)), kc) + bias
                m = jnp.max(sc, axis=1, keepdims=True)
                pe = jnp.exp(sc - m)
                l = jnp.sum(pe, axis=1, keepdims=True)
                outs.append(_dot((pe * (1.0 / l)).astype(BF16), vc))
                lses.append(m + jnp.log(l))
            o_dst[qrows, :] = jnp.where(_head_mask(0), outs[0], outs[1])
            l_dst[qrows, :] = jnp.where(_head_mask(0), lses[0], lses[1])

        for n in range(SEQ // BLOCK):
            unit((q_ref, k_ref, v_ref), n * BLOCK, n == 0, 1, 0, o_scr.at[0], l1_ref)
        quarters = tuple(qkv4.at[a] for a in range(3))
        for di in (1, 2):
            stride = DILATIONS[di] // 4
            for r in range(4):
                for g in range(stride):
                    for n in range(QUARTER // (BLOCK * stride)):
                        unit(quarters, r * QUARTER + n * BLOCK * stride + g, n == 0, stride, di, o4.at[di - 1], l4.at[di - 1])
        for di, l_ref in ((1, l2_ref), (2, l3_ref)):
            _from_quarters(o4.at[di - 1], o_scr.at[di])
            _from_quarters(l4.at[di - 1], l_ref)
        w = _mix_weights(l1_ref[...], l2_ref[...], l3_ref[...])
        y_ref[...] = (w[0] * o_scr[0] + w[1] * o_scr[1] + w[2] * o_scr[2]).astype(BF16)

    blk = pl.BlockSpec((s, LANES), lambda b, hp: (b, hp))
    res = pl.pallas_call(
        body, grid=(n_batch, N_PAIRS),
        in_specs=_pair_specs(s),
        out_specs=[blk] * 4,
        out_shape=[jax.ShapeDtypeStruct((t, MIX_HALF), BF16)] + [jax.ShapeDtypeStruct((t, MIX_HALF), F32)] * 3,
        scratch_shapes=[pltpu.VMEM((3, s, LANES), F32), pltpu.VMEM((3, s, LANES), F32), pltpu.VMEM((2, s, LANES), F32),
                        pltpu.VMEM((2, s, LANES), F32), pltpu.VMEM((6, BLOCK, 2 * BLOCK), F32)],
        name=name, compiler_params=_params("parallel", "arbitrary"),
    )(zd, zd, zd)
    return res[0], res[1:]


def _dil_bwd(zd, dy, ya, lses, n_batch, name):
    s = SEQ
    t = n_batch * s

    def body(q_ref, k_ref, v_ref, dy_ref, ya_ref, l1_ref, l2_ref, l3_ref, dz_ref, w_scr, dy_scr, dot_scr, acc, st4, acc4, bias_scr):
        for di, dil in enumerate(DILATIONS):
            bias_scr[di] = jnp.concatenate([_dil_bias(_alibi_slope(pl.program_id(1), e), dil) for e in range(2)])
        for di, w in enumerate(_mix_weights(l1_ref[...], l2_ref[...], l3_ref[...])):
            w_scr[di] = w
        dya = dy_ref[...].astype(F32)
        prod = dya * ya_ref[...].astype(F32)
        per_head = [jnp.sum(jnp.where(_head_mask(e), prod, 0.0), axis=1, keepdims=True) for e in range(2)]
        dy_scr[...] = dya
        dot_scr[...] = jnp.where(_head_mask(0), per_head[0], per_head[1])
        acc[...] = jnp.zeros_like(acc)
        acc4[...] = jnp.zeros_like(acc4)
        staged = (q_ref, k_ref, v_ref, w_scr.at[1], w_scr.at[2], l2_ref, l3_ref, dy_scr, dot_scr)
        for a, ref in enumerate(staged):
            _to_quarters(ref, st4.at[a])

        def unit(srcs, dst, start, first, stride, di):
            qrows = _strided(start, BLOCK, stride)
            krows = qrows if first else _strided(start - BLOCK * stride, 2 * BLOCK, stride)
            q = (srcs[0][qrows, :] * ATT_SCALE).astype(BF16)
            kc = srcs[1][krows, :].astype(BF16)
            vc = srcs[2][krows, :].astype(BF16)
            wq = srcs[3][qrows, :]
            lse = srcs[4][qrows, :]
            do = (wq * srcs[5][qrows, :]).astype(BF16)
            sub = wq * srcs[6][qrows, :]
            heads = lambda a: jnp.concatenate([jnp.where(_head_mask(e), a, jnp.zeros_like(a)) for e in range(2)])
            column = lambda a: jnp.concatenate([a[:, HEAD_DIM * e:HEAD_DIM * e + 1] for e in range(2)])
            qq, dd = heads(q), heads(do)
            bias = bias_scr[di]
            p = jnp.exp(_dot_nt(qq, kc) + (bias[:, BLOCK:] if first else bias) - column(lse))
            dsb = (p * (_dot_nt(dd, vc) - column(sub))).astype(BF16)
            dq = _dot(jnp.concatenate([dsb[:BLOCK], dsb[BLOCK:]], axis=1), heads(kc))
            dst.at[0][qrows, :] += dq * ATT_SCALE
            dst.at[1][krows, :] += _dot_tn(dsb, qq)
            dst.at[2][krows, :] += _dot_tn(p.astype(BF16), dd)

        token_order = (q_ref, k_ref, v_ref, w_scr.at[0], l1_ref, dy_scr, dot_scr)
        for n in range(SEQ // BLOCK):
            unit(token_order, acc, n * BLOCK, n == 0, 1, 0)
        for di in (1, 2):
            quarters = (st4.at[0], st4.at[1], st4.at[2], st4.at[2 + di], st4.at[4 + di], st4.at[7], st4.at[8])
            stride = DILATIONS[di] // 4
            for r in range(4):
                for g in range(stride):
                    for n in range(QUARTER // (BLOCK * stride)):
                        unit(quarters, acc4, r * QUARTER + n * BLOCK * stride + g, n == 0, stride, di)
        for k in range(3):
            for r in range(4):
                acc.at[k][pl.ds(r, QUARTER, stride=4), :] += acc4[k, r * QUARTER:(r + 1) * QUARTER, :]
            dz_ref[:, k * LANES:(k + 1) * LANES] = acc[k].astype(BF16)

    blk = pl.BlockSpec((s, LANES), lambda b, hp: (b, hp))
    pair = pl.BlockSpec((s, PAIR_WIDTH), lambda b, hp: (b, hp))
    return pl.pallas_call(
        body, grid=(n_batch, N_PAIRS),
        in_specs=_pair_specs(s) + [blk] * 5,
        out_specs=pair,
        out_shape=jax.ShapeDtypeStruct((t, 2 * 3 * MIX_HALF), BF16),
        scratch_shapes=[pltpu.VMEM((3, s, LANES), F32), pltpu.VMEM((s, LANES), F32), pltpu.VMEM((s, LANES), F32),
                        pltpu.VMEM((3, s, LANES), F32), pltpu.VMEM((9, s, LANES), F32), pltpu.VMEM((3, s, LANES), F32),
                        pltpu.VMEM((3, 2 * BLOCK, 2 * BLOCK), F32)],
        name=name, compiler_params=_params("parallel", "arbitrary"),
    )(zd, zd, zd, dy, ya, *lses)


X_BQ = 2048


def _xattn_probs(q, k):
    sc = _dot_nt(q, k) * X_SCALE
    pe = jnp.exp(sc - jnp.max(sc, axis=1, keepdims=True))
    return pe / jnp.sum(pe, axis=1, keepdims=True)


def _rms(xv, g):
    return (xv * lax.rsqrt(jnp.mean(xv * xv, axis=-1, keepdims=True) + EPS) * g).astype(BF16)


def _out_xattn_fwd(ya, yf, x0, w_out, g_xattn, w_xq, kx, vx, w_xo, g_mlp, name, tm=512):
    t, d = x0.shape
    per_example = SEQ // tm

    def body(ya_ref, yf_ref, x0_ref, wa_ref, wf_ref, g2_ref, wq_ref, k_ref, v_ref, wo_ref, g3_ref,
             x1_ref, h2_ref, q_ref, o_ref, x2_ref, h3_ref):
        x1 = x0_ref[...] + (_dot(ya_ref[...], wa_ref[...]) + _dot(yf_ref[...], wf_ref[...]))
        x1_ref[...] = x1
        h2 = _rms(x1, g2_ref[...])
        h2_ref[...] = h2
        q = _dot(h2, wq_ref[...]).astype(BF16)
        q_ref[...] = q
        for h in range(X_HEADS):
            cols = slice(h * X_HEAD_DIM, (h + 1) * X_HEAD_DIM)
            p = _xattn_probs(q[:, cols], k_ref[:, cols])
            o_ref[:, cols] = _dot(p.astype(BF16), v_ref[:, cols]).astype(BF16)
        x2 = x1 + _dot(o_ref[...], wo_ref[...])
        x2_ref[...] = x2
        h3_ref[...] = _rms(x2, g3_ref[...])

    row = lambda width: pl.BlockSpec((tm, width), lambda i: (i, 0))
    whole = lambda a: pl.BlockSpec(a.shape, lambda i: (0, 0))
    mem = pl.BlockSpec((N_MEM, d), lambda i: (i // per_example, 0))
    vec = pl.BlockSpec((1, d), lambda i: (0, 0))
    w_a, w_f = w_out[:MIX_HALF], w_out[MIX_HALF:]
    return pl.pallas_call(
        body, grid=(t // tm,),
        in_specs=[row(MIX_HALF), row(MIX_HALF), row(d), whole(w_a), whole(w_f), vec, whole(w_xq), mem, mem, whole(w_xo), vec],
        out_specs=[row(d)] * 6,
        out_shape=[jax.ShapeDtypeStruct((t, d), dt) for dt in (F32, BF16, BF16, BF16, F32, BF16)],
        name=name, compiler_params=_params("arbitrary"),
    )(ya, yf, x0, w_a, w_f, g_xattn.reshape(1, d), w_xq, kx, vx, w_xo, g_mlp.reshape(1, d))


def _xattn_bwd(qx, kx, vx, dox, n_batch, name):
    nq = SEQ // X_BQ

    def body(q_ref, k_ref, v_ref, do_ref, dq_ref, dk_ref, dv_ref, dk_acc, dv_acc):
        i = pl.program_id(2)

        @pl.when(i == 0)
        def _():
            dk_acc[...] = jnp.zeros_like(dk_acc)
            dv_acc[...] = jnp.zeros_like(dv_acc)

        q, k, do = q_ref[...], k_ref[...], do_ref[...]
        p = _xattn_probs(q, k)
        dp = _dot_nt(do, v_ref[...])
        dsb = (p * (dp - jnp.sum(p * dp, axis=1, keepdims=True))).astype(BF16)
        dq_ref[...] = (_dot(dsb, k) * X_SCALE).astype(BF16)
        dk_acc[...] += _dot_tn(dsb, q) * X_SCALE
        dv_acc[...] += _dot_tn(p.astype(BF16), do)

        @pl.when(i == nq - 1)
        def _():
            dk_ref[...] = dk_acc[...].astype(BF16)
            dv_ref[...] = dv_acc[...].astype(BF16)

    qblk = pl.BlockSpec((X_BQ, X_HEAD_DIM), lambda b, h, i: (b * nq + i, h))
    kblk = pl.BlockSpec((N_MEM, X_HEAD_DIM), lambda b, h, i: (b, h))
    return pl.pallas_call(
        body, grid=(n_batch, X_HEADS, nq), in_specs=[qblk, kblk, kblk, qblk], out_specs=[qblk, kblk, kblk],
        out_shape=[jax.ShapeDtypeStruct(qx.shape, BF16), jax.ShapeDtypeStruct(kx.shape, BF16), jax.ShapeDtypeStruct(kx.shape, BF16)],
        scratch_shapes=[pltpu.VMEM((N_MEM, X_HEAD_DIM), F32)] * 2,
        name=name, compiler_params=_params("parallel", "parallel", "arbitrary"),
    )(qx, kx, vx, dox)


def _adamw(w, g, m, v, name, rows):
    r, c = w.shape
    assert r % rows == 0, (name, w.shape, rows)

    def body(w_ref, g_ref, m_ref, v_ref, d_ref, nm_ref, nv_ref):
        gv = g_ref[...]
        m1 = ADAM_B1 * m_ref[...] + (1.0 - ADAM_B1) * gv
        v1 = ADAM_B2 * v_ref[...] + (1.0 - ADAM_B2) * jnp.square(gv)
        m_hat = m1 / (1.0 - ADAM_B1 ** ADAM_STEP)
        v_hat = v1 / (1.0 - ADAM_B2 ** ADAM_STEP)
        d_ref[...] = -ADAM_LR * (m_hat / (jnp.sqrt(v_hat) + ADAM_EPS) + ADAM_WD * w_ref[...])
        nm_ref[...] = m1
        nv_ref[...] = v1

    blk = pl.BlockSpec((rows, c), lambda i: (i, 0))
    return pl.pallas_call(
        body, grid=(r // rows,), in_specs=[blk] * 4, out_specs=[blk] * 3,
        out_shape=[jax.ShapeDtypeStruct((r, c), F32)] * 3,
        name=name, compiler_params=_params("arbitrary"),
    )(w, g, m, v)


def _relu2(acc):
    a = jnp.maximum(acc, 0.0)
    return acc, a * a


def _relu2_bwd(acc, u):
    return (2.0 * jnp.maximum(u.astype(F32), 0.0) * acc,)


def _local_step(x, mem, target, vecs, w_in, late_weights, hooks=None):
    n_batch = x.shape[0]
    t = n_batch * SEQ
    x0 = x.reshape(t, D_MODEL)
    mem2 = mem.reshape(n_batch * N_MEM, D_MODEL)
    tgt = target.reshape(t, D_MODEL)

    half = 3 * MIX_HALF
    w_qkv = jnp.concatenate([_pair_major(w_in[:, :half]), _pair_major(w_in[:, half:QKV_WIDTH])], axis=1)
    w_gate = jnp.pad(w_in[:, QKV_WIDTH:], ((0, 0), (0, GATE_PAD - N_HEADS)))
    b_pad = jnp.pad(vecs["b_forget"], (0, GATE_PAD - N_HEADS)).reshape(1, GATE_PAD)

    h1, zd, zf, gate = _in_proj(x0, vecs["g_mix"], jnp.concatenate([w_qkv, w_gate], axis=1), "in_proj")
    mn = _rmsnorm(mem2, vecs["g_mem"], "norm_mem")
    c_bc, c_row, sg = _gate_fwd(gate, b_pad, n_batch, "gate_fwd")
    ya, lses = _dil_fwd(zd, n_batch, "dil_fwd")
    yf, of32, lse_f = _fox_fwd(zf, c_bc, c_row, n_batch, "fox_fwd")
    wts = late_weights(yf)
    w_out = wts["w_out"]
    kx = _matmul(mn, wts["w_xk"], "xk", out_dtypes=(BF16,))[0]
    vx = _matmul(mn, wts["w_xv"], "xv", out_dtypes=(BF16,))[0]
    x1, h2, qx, ox, x2, h3 = _out_xattn_fwd(ya, yf, x0, w_out, vecs["g_xattn"], wts["w_xq"], kx, vx, wts["w_xo"],
                                            vecs["g_mlp"], "out_xattn")
    u, a2 = _matmul(h3, wts["w_up"], "mlp_up", out_dtypes=(BF16, BF16), epilogue=_relu2)
    loss, dx3, dx3b, dg_final = _loss_bwd(a2, wts["w_down"], x2, vecs["g_final"], tgt, "mlp_down_loss")

    du = _matmul(dx3b, wts["w_down"], "mlp_down_bwd", out_dtypes=(BF16,), extras=(u,), epilogue=_relu2_bwd, w_t=True)[0]
    shards = (N_CHIPS, 2 * D_MODEL, D_MODEL)
    g_mlp = _matmul_tn(h3, du, "gw_up", packed=(shards, lambda i, j: (j, 0, 0), None))
    g_mlp = _matmul_tn(a2, dx3b, "gw_down", packed=(shards, lambda i, j: (i, 1, 0), g_mlp))
    gw_up = g_mlp[:, :D_MODEL].transpose(1, 0, 2).reshape(D_MODEL, D_FF)
    gw_down = g_mlp[:, D_MODEL:].reshape(D_FF, D_MODEL)
    on_grads, on_swapped = hooks or (None, None)
    token = on_grads("mlp", g_mlp) if hooks else None
    dx2, dx2b, dg_mlp = _matmul_rms_bwd([du], [wts["w_up"]], x2, vecs["g_mlp"], dx3, "mlp_up_bwd", after=token)

    gw_xo = _matmul_tn(ox, dx2b, "gw_xo")
    token = on_swapped("mlp", dx2b) if hooks else None
    dox = _matmul(dx2b, wts["w_xo"], "xo_bwd", out_dtypes=(BF16,), w_t=True, after=token)[0]
    dqx, dkx, dvx = _xattn_bwd(qx, kx, vx, dox, n_batch, "xattn_bwd")
    gw_xq = _matmul_tn(h2, dqx, "gw_xq")
    gw_xk = _matmul_tn(mn, dkx, "gw_xk")
    gw_xv = _matmul_tn(mn, dvx, "gw_xv")
    dmn = _matmul(dkx, wts["w_xk"], "xk_bwd", w_t=True)[0]
    dmn = _matmul_res(dvx, wts["w_xv"], dmn, "xv_bwd", w_t=True)
    _, _, dg_mem = _rms_bwd(mem2, dmn, vecs["g_mem"], None, "norm_mem_bwd")
    dx1, dx1b, dg_xattn = _matmul_rms_bwd([dqx], [wts["w_xq"]], x1, vecs["g_xattn"], dx2, "xq_bwd", tm=1024)

    gw_out = jnp.concatenate([_matmul_tn(ya, dx1b, "gw_out_a"), _matmul_tn(yf, dx1b, "gw_out_f")], axis=0)
    token = on_grads("mid", dict(w_out=gw_out, w_xq=gw_xq, w_xk=gw_xk, w_xv=gw_xv, w_xo=gw_xo)) if hooks else None
    dy = _matmul(dx1b, w_out, "out_bwd", out_dtypes=(BF16,), w_t=True, after=token)[0]
    dz = _dil_bwd(zd, dy, ya, lses, n_batch, "dil_bwd")
    dz, dc = _fox_bwd(zf, of32, dy, lse_f, c_bc, c_row, dz, n_batch, "fox_bwd")
    dzg, db = _gate_bwd(dc.reshape(n_batch, N_HEADS, SEQ), sg, "gate_bwd")
    token = on_swapped("mid", dz) if hooks else None
    gw_pm = _matmul_tn(h1, dz, "gw_in_qkv", after=token)
    gw_in = jnp.concatenate([_pair_major_inv(gw_pm[:, :half]), _pair_major_inv(gw_pm[:, half:]),
                             _matmul_tn(h1, dzg, "gw_in_gate")[:, :N_HEADS]], axis=1)
    dx0, _, dg_mix = _matmul_rms_bwd([dz, dzg], [w_qkv, w_gate], x0, vecs["g_mix"], dx1, "in_bwd")

    gw = dict(w_in=gw_in, w_out=gw_out, w_xq=gw_xq, w_xk=gw_xk, w_xv=gw_xv, w_xo=gw_xo, w_up=gw_up, w_down=gw_down)
    gv = dict(g_mix=dg_mix, g_xattn=dg_xattn, g_mem=dg_mem, g_mlp=dg_mlp, g_final=dg_final, b_forget=db)
    return loss, dx0.reshape(x.shape), gw, gv


MESH = pl.DeviceIdType.MESH
ANY = pl.BlockSpec(memory_space=pl.ANY)


def _place():
    x, y, c = lax.axis_index("x"), lax.axis_index("y"), lax.axis_index("c")
    other_chips = [(1 - x, y), (x, 1 - y), (1 - x, 1 - y)]
    return x, y, c, other_chips


def _my_chip():
    return 2 * lax.axis_index("x") + lax.axis_index("y")


def _halves(rows, c, align):
    half = rows // 2
    assert rows % (2 * align) == 0, rows
    return pl.ds(pl.multiple_of(c * half, align), half), pl.ds(pl.multiple_of((1 - c) * half, align), half)


def _place_own(wall, pack):
    return lax.dynamic_update_slice(wall, pack[None], (_my_chip(), 0, 0))


HBM = pl.BlockSpec(memory_space=pltpu.HBM)
SEM = pl.BlockSpec(memory_space=pltpu.SEMAPHORE)
SPLIT_COPY = pltpu.CompilerParams(has_side_effects=pltpu.SideEffectType.DATAFLOW_SIDE_EFFECTING)


def _in_hbm(a):
    return pltpu.with_memory_space_constraint(a, pltpu.HBM)


def _start_call(start, src, land_shape, after, name):
    land = lax.empty(land_shape, src.dtype)

    def body(src_ref, land_ref, after_ref, send_sems, recv_sems, src_thru, land_thru, token):
        del after_ref, src_thru, land_thru
        start(src_ref, land_ref, send_sems, recv_sems)
        token[...] = jnp.zeros_like(token)

    return pl.pallas_call(
        body, name=name,
        out_shape=(pltpu.SemaphoreType.DMA((3,)), pltpu.SemaphoreType.DMA((3,)), pltpu.HBM(src.shape, src.dtype),
                   pltpu.HBM(land_shape, src.dtype), jax.ShapeDtypeStruct((8, LANES), F32)),
        in_specs=(HBM, HBM, ANY), out_specs=(SEM, SEM, HBM, HBM, pl.BlockSpec(memory_space=pltpu.VMEM)),
        input_output_aliases={0: 2, 1: 3}, compiler_params=SPLIT_COPY,
    )(_in_hbm(src), _in_hbm(land), after)


def _wait_call(body, started, after, name):
    send_sems, recv_sems, src, land, _ = started
    return pl.pallas_call(
        body, name=name,
        out_shape=(pltpu.HBM(src.shape, src.dtype), pltpu.HBM(land.shape, land.dtype)),
        in_specs=(HBM, HBM, SEM, SEM, ANY), out_specs=(HBM, HBM),
        input_output_aliases={0: 0, 1: 1}, compiler_params=SPLIT_COPY,
    )(src, land, send_sems, recv_sems, after)


def _gather_copies(p_ref, wall_ref, send_sems, recv_sems):
    x, y, c, chips = _place()
    me = 2 * x + y
    mine, _ = _halves(p_ref.shape[0], c, 16)
    out, back = [], []
    for k, chip in enumerate(chips):
        peer = dict(send_sem=send_sems.at[k], recv_sem=recv_sems.at[k], device_id=(chip[0], chip[1], c), device_id_type=MESH)
        out.append(pltpu.make_async_remote_copy(src_ref=p_ref.at[mine], dst_ref=wall_ref.at[me, mine], **peer))
        slab = wall_ref.at[2 * chip[0] + chip[1], mine]
        back.append(pltpu.make_async_remote_copy(src_ref=slab, dst_ref=slab, **peer))
    return out, back


def _gather_start(pack, after, name):
    def start(p_ref, wall_ref, send_sems, recv_sems):
        for cp in _gather_copies(p_ref, wall_ref, send_sems, recv_sems)[0]:
            cp.start()

    return _start_call(start, pack, (N_CHIPS,) + pack.shape, after, name)


def _gather_wait(started, after, name):
    def body(p_ref, wall_ref, send_sems, recv_sems, after_ref, p_dead, wall_out):
        del after_ref, p_dead, wall_out
        out, back = _gather_copies(p_ref, wall_ref, send_sems, recv_sems)
        for cp_out, cp_back in zip(out, back):
            cp_out.wait_send()
            cp_back.wait_recv()

    return _wait_call(body, started, after, name)


def _pass_on(wall, name):
    def body(w_in_ref, out_ref, send_sems, recv_sems):
        del w_in_ref
        x, y, c, chips = _place()
        mine, theirs = _halves(wall.shape[1], c, 16)
        sends = []
        for k, chip in enumerate(chips):
            slab = out_ref.at[2 * chip[0] + chip[1]]
            peer = dict(send_sem=send_sems.at[k], recv_sem=recv_sems.at[k], device_id=(x, y, 1 - c), device_id_type=MESH)
            cp = pltpu.make_async_remote_copy(src_ref=slab.at[mine], dst_ref=slab.at[mine], **peer)
            cp.start()
            sends.append((cp, pltpu.make_async_remote_copy(src_ref=slab.at[theirs], dst_ref=slab.at[theirs], **peer)))
        for cp, back in sends:
            back.wait_recv()
            cp.wait_send()

    return pl.pallas_call(
        body, in_specs=[ANY], out_specs=ANY, out_shape=jax.ShapeDtypeStruct(wall.shape, wall.dtype),
        scratch_shapes=[pltpu.SemaphoreType.DMA((3,))] * 2, input_output_aliases={0: 0}, name=name,
    )(wall)


def _swap_halves(g, name):
    half = g.shape[1] // 2

    def body(g_ref, out_ref, send_sem, recv_sem):
        x, y, c, _ = _place()
        _, theirs = _halves(g.shape[1], c, 8)
        cp = pltpu.make_async_remote_copy(src_ref=g_ref.at[:, theirs], dst_ref=out_ref, send_sem=send_sem, recv_sem=recv_sem,
                                          device_id=(x, y, 1 - c), device_id_type=MESH)
        cp.start()
        cp.wait()

    return pl.pallas_call(
        body, in_specs=[ANY], out_specs=ANY,
        out_shape=jax.ShapeDtypeStruct((N_CHIPS, half, g.shape[2]), F32),
        scratch_shapes=[pltpu.SemaphoreType.DMA, pltpu.SemaphoreType.DMA],
        name=name,
    )(g)


def _swap_copy(g_ref, land_ref, send_sems, recv_sems):
    x, y, c, _ = _place()
    _, theirs = _halves(g_ref.shape[1], c, 8)
    return pltpu.make_async_remote_copy(src_ref=g_ref.at[:, theirs], dst_ref=land_ref, send_sem=send_sems.at[0],
                                        recv_sem=recv_sems.at[0], device_id=(x, y, 1 - c), device_id_type=MESH)


def _swap_start(g, name):
    def start(g_ref, land_ref, send_sems, recv_sems):
        _swap_copy(g_ref, land_ref, send_sems, recv_sems).start()

    return _start_call(start, g, (N_CHIPS, g.shape[1] // 2, g.shape[2]), _core_index(), name)


def _swap_wait(started, after, name):
    def body(g_ref, land_ref, send_sems, recv_sems, after_ref, g_out, land_out):
        del after_ref, g_out, land_out
        cp = _swap_copy(g_ref, land_ref, send_sems, recv_sems)
        cp.wait_send()
        cp.wait_recv()

    return _wait_call(body, started, after, name)


def _core_index():
    return lax.axis_index("c").astype(jnp.int32).reshape(1)


def _row_tile(half):
    tile = max(t for t in range(16, 1025, 16) if half % t == 0)
    return tile, half // tile


def _add_sibling(g, got, name):
    half = g.shape[1] // 2
    tile, n_tiles = _row_tile(half)

    def body(c_ref, g_ref, got_ref, o_ref):
        o_ref[...] = (g_ref[...] + got_ref[...]).astype(BF16)

    width = g.shape[2]
    blk = pl.BlockSpec((None, tile, width), lambda s, i, c_ref: (s, i, 0))
    return pl.pallas_call(
        body,
        grid_spec=pltpu.PrefetchScalarGridSpec(
            num_scalar_prefetch=1, grid=(N_CHIPS, n_tiles),
            in_specs=[pl.BlockSpec((None, tile, width), lambda s, i, c_ref: (s, c_ref[0] * n_tiles + i, 0)), blk],
            out_specs=blk),
        out_shape=jax.ShapeDtypeStruct((N_CHIPS, half, width), BF16),
        name=name, compiler_params=_params("arbitrary", "arbitrary"),
    )(_core_index(), g, got)


def _exchange_copies(p_ref, land_ref, send_sems, recv_sems):
    x, y, c, chips = _place()
    me = 2 * x + y
    out, back = [], []
    for k, chip in enumerate(chips):
        peer = dict(send_sem=send_sems.at[k], recv_sem=recv_sems.at[k], device_id=(chip[0], chip[1], c), device_id_type=MESH)
        out.append(pltpu.make_async_remote_copy(src_ref=p_ref.at[2 * chip[0] + chip[1]], dst_ref=land_ref.at[me], **peer))
        slab = land_ref.at[2 * chip[0] + chip[1]]
        back.append(pltpu.make_async_remote_copy(src_ref=slab, dst_ref=slab, **peer))
    return out, back


def _with_own(got, part):
    me = _my_chip()
    return lax.dynamic_update_slice(got, lax.dynamic_slice(part, (me, 0, 0), (1,) + part.shape[1:]), (me, 0, 0))


def _exchange_start(part, name):
    def start(p_ref, land_ref, send_sems, recv_sems):
        for cp in _exchange_copies(p_ref, land_ref, send_sems, recv_sems)[0]:
            cp.start()

    return _start_call(start, part, part.shape, _core_index(), name)


def _exchange_wait(started, after, name):
    def body(p_ref, land_ref, send_sems, recv_sems, after_ref, p_dead, land_out):
        del after_ref, p_dead, land_out
        out, back = _exchange_copies(p_ref, land_ref, send_sems, recv_sems)
        for cp_out, cp_back in zip(out, back):
            cp_out.wait_send()
            cp_back.wait_recv()

    part, got = _wait_call(body, started, after, name)
    return _with_own(got, part)


def _sum_chips(parts, name):
    half, width = parts.shape[1:]
    tile, n_tiles = _row_tile(half)

    def body(c_ref, p0, p1, p2, p3, o_ref):
        f32 = lambda p: p[...].astype(F32)
        o_ref[...] = ((f32(p0) + f32(p1)) + f32(p2)) + f32(p3)

    def slab(s):
        return pl.BlockSpec((None, tile, width), lambda i, c_ref, s=s: (s, i, 0))

    return pl.pallas_call(
        body,
        grid_spec=pltpu.PrefetchScalarGridSpec(
            num_scalar_prefetch=1, grid=(n_tiles,),
            in_specs=[slab(s) for s in range(N_CHIPS)],
            out_specs=pl.BlockSpec((None, tile, width), lambda i, c_ref: (c_ref[0], i, 0))),
        out_shape=jax.ShapeDtypeStruct((2, half, width), F32),
        name=name, compiler_params=_params("arbitrary"),
    )(_core_index(), parts, parts, parts, parts)


def _share_halves(halves, name):
    def body(h_ref, out_ref, send_sem, recv_sem):
        del h_ref
        x, y, c, _ = _place()
        cp = pltpu.make_async_remote_copy(src_ref=out_ref.at[c], dst_ref=out_ref.at[c], send_sem=send_sem, recv_sem=recv_sem,
                                          device_id=(x, y, 1 - c), device_id_type=MESH)
        cp.start()
        pltpu.make_async_remote_copy(src_ref=out_ref.at[1 - c], dst_ref=out_ref.at[1 - c], send_sem=send_sem, recv_sem=recv_sem,
                                     device_id=(x, y, 1 - c), device_id_type=MESH).wait_recv()
        cp.wait_send()

    return pl.pallas_call(
        body, in_specs=[ANY], out_specs=ANY,
        out_shape=jax.ShapeDtypeStruct(halves.shape, halves.dtype),
        scratch_shapes=[pltpu.SemaphoreType.DMA] * 2,
        input_output_aliases={0: 0},
        name=name,
    )(halves)


def _reduce_parts(g, tag):
    return _add_sibling(g, _swap_halves(g, "swap_" + tag), "add_" + tag)


def _reduce_finish(got, tag):
    halves = _share_halves(_sum_chips(got, "sum_" + tag), "share_" + tag)
    return halves.reshape(2 * halves.shape[1], halves.shape[2])


SMALL_ROWS = 8


def _allreduce_small(v):
    def body(v_ref, out_ref, buf, send_sems, recv_sems):
        x, y, c, _ = _place()
        buf[4 * x + 2 * y + c] = v_ref[...]
        sends = []
        for k in range(1, N_DEV):
            px = 1 - x if k & 4 else x
            py = 1 - y if k & 2 else y
            pc = 1 - c if k & 1 else c
            cp = pltpu.make_async_remote_copy(src_ref=v_ref, dst_ref=buf.at[4 * x + 2 * y + c], send_sem=send_sems.at[k - 1],
                                              recv_sem=recv_sems.at[k - 1], device_id=(px, py, pc), device_id_type=MESH)
            cp.start()
            sends.append((cp, 4 * px + 2 * py + pc))
        for k, (cp, peer) in enumerate(sends):
            pltpu.make_async_remote_copy(src_ref=v_ref, dst_ref=buf.at[peer], send_sem=send_sems.at[k], recv_sem=recv_sems.at[k],
                                         device_id=(x, y, c), device_id_type=MESH).wait_recv()
        for cp, _ in sends:
            cp.wait_send()
        total = buf[0]
        for d in range(1, N_DEV):
            total = total + buf[d]
        out_ref[...] = total

    vmem = pl.BlockSpec(memory_space=pltpu.VMEM)
    return pl.pallas_call(
        body, in_specs=[vmem], out_specs=vmem,
        out_shape=jax.ShapeDtypeStruct(v.shape, v.dtype),
        scratch_shapes=[pltpu.VMEM((N_DEV,) + v.shape, v.dtype), pltpu.SemaphoreType.DMA((N_DEV - 1,)),
                        pltpu.SemaphoreType.DMA((N_DEV - 1,))],
        name="allreduce_small",
    )(v)


MATRICES = ("w_in", "w_out", "w_xq", "w_xk", "w_xv", "w_xo", "w_up", "w_down")
VECTORS = ("g_mix", "g_xattn", "g_mem", "g_mlp", "g_final", "b_forget")
WEIGHT_ORDER = ("g_mix", "w_in", "b_forget", "w_out", "g_xattn", "g_mem", "w_xq", "w_xk", "w_xv", "w_xo",
                "g_mlp", "w_up", "w_down", "g_final")
GROUPS = {"mlp": ("w_up", "w_down"), "mid": ("w_out", "w_xq", "w_xk", "w_xv", "w_xo"), "in": ("w_in",)}
LATE = GROUPS["mid"] + GROUPS["mlp"]
W_IN_SHARD = IN_WIDTH // N_CHIPS
SHARD_ROWS = {"w_out": 256, "w_xq": 256, "w_xk": 256, "w_xv": 256, "w_xo": 256, "w_up": 1024, "w_down": 1024}
PACK_ROWS = SHARD_ROWS
W_IN_PAD = -(-W_IN_SHARD // LANES) * LANES
ADAM_ROWS = 128


def _pack(parts, names):
    return jnp.concatenate([jnp.pad(parts[n], ((0, PACK_ROWS[n] - SHARD_ROWS[n]), (0, 0))) for n in names], axis=0)


def _unpack(a, names):
    out, pos = {}, 0
    for n in names:
        out[n] = a[..., pos:pos + SHARD_ROWS[n], :]
        pos += PACK_ROWS[n]
    return out


def _full_weights(wall, names):
    cols = lambda a: a.transpose(1, 0, 2).reshape(a.shape[1], -1)
    rows = lambda a: a.reshape(-1, a.shape[-1])
    if names == GROUPS["in"]:
        return {"w_in": cols(wall[:, :, :W_IN_SHARD])}
    return {n: cols(a) if n == "w_up" else rows(a) for n, a in _unpack(wall, names).items()}


def _shard_of(g, name, s):
    if name == "w_up":
        return g[:, s * D_MODEL:(s + 1) * D_MODEL]
    n = SHARD_ROWS[name]
    return g[s * n:(s + 1) * n]


def _pad_w_in(a):
    return jnp.pad(a, [(0, 0)] * (a.ndim - 1) + [(0, W_IN_PAD - W_IN_SHARD)])


def _pack_grads(gws, names):
    if names == GROUPS["in"]:
        return _pad_w_in(gws["w_in"].reshape(D_MODEL, N_CHIPS, W_IN_SHARD).transpose(1, 0, 2))
    return jnp.stack([_pack({n: _shard_of(gws[n], n, s) for n in names}, names) for s in range(N_CHIPS)])


def kernel(x, mem, g_mix, w_in, b_forget, w_out, g_xattn, g_mem, w_xq, w_xk, w_xv, w_xo, g_mlp, w_up, w_down, g_final, loss_target, m_g_mix, m_w_in, m_b_forget, m_w_out, m_g_xattn, m_g_mem, m_w_xq, m_w_xk, m_w_xv, m_w_xo, m_g_mlp, m_w_up, m_w_down, m_g_final, v_g_mix, v_w_in, v_b_forget, v_w_out, v_g_xattn, v_g_mem, v_w_xq, v_w_xk, v_w_xv, v_w_xo, v_g_mlp, v_w_up, v_w_down, v_g_final):
    given = dict(locals())
    weights = {n: given[n] for n in WEIGHT_ORDER}
    vecs = {n: weights[n] for n in VECTORS}

    shard = {n: weights[n].astype(BF16) for n in MATRICES}
    in_started = _gather_start(_pad_w_in(shard["w_in"]), _core_index(), "gather_in_start")
    late_pack = _pack(shard, LATE)
    in_pack, in_wall = _gather_wait(in_started, late_pack, "gather_in_wait")
    in_wall = _place_own(_pass_on(in_wall, "gather_in_pass"), in_pack)
    late = _gather_start(late_pack, in_wall, "gather_late_start")
    w_in_full = _full_weights(in_wall, GROUPS["in"])["w_in"]

    def late_weights(after):
        pack, wall = _gather_wait(late, after, "gather_late_wait")
        return _full_weights(_place_own(_pass_on(wall, "gather_late_pass"), pack), LATE)

    started = {}

    swapping = {}

    def on_grads(group, gws):
        packed = gws if group == "mlp" else _pack_grads(gws, GROUPS[group])
        swapping[group] = _swap_start(packed, "swap_%s_start" % group)
        return swapping[group][4]

    def on_swapped(group, after):
        g, got = _swap_wait(swapping[group], after, "swap_%s_wait" % group)
        started[group] = _exchange_start(_add_sibling(g, got, "add_" + group), "exchange_%s_start" % group)
        return started[group][4]

    loss, grad_x, gw, gv = _local_step(x, mem, loss_target, vecs, w_in_full, late_weights, (on_grads, on_swapped))

    part = _reduce_parts(_pack_grads(gw, GROUPS["in"]), "in")
    started["in"] = _exchange_start(part, "exchange_in_start")
    grads, delta, new_m, new_v = {}, {}, {}, {}

    def finish(group, after):
        got = _exchange_wait(started[group], after, "exchange_%s_wait" % group)
        done = _reduce_finish(got, group)
        for n, a in ({"w_in": done[:, :W_IN_SHARD]} if group == "in" else _unpack(done, GROUPS[group])).items():
            grads[n] = a.reshape(weights[n].shape)
            delta[n], new_m[n], new_v[n] = _adamw(weights[n], grads[n], given["m_" + n], given["v_" + n], "adamw_" + n, ADAM_ROWS)
        return new_v[GROUPS[group][-1]]

    after = finish("mlp", started["in"][4])
    after = finish("mid", after)

    row = lambda a: jnp.pad(a.reshape(-1), (0, D_MODEL - a.size)).reshape(1, D_MODEL)
    small = jnp.concatenate([gv[n] for n in VECTORS[:5]] + [row(gv["b_forget"][:, 0]), row(loss[0, :1]),
                             jnp.zeros((1, D_MODEL), F32)], axis=0)
    small = _allreduce_small(small)
    for k, n in enumerate(VECTORS[:5]):
        grads[n] = small[k]
    grads["b_forget"] = small[5, :N_HEADS]
    loss_total = small[6, 0]
    finish("in", after)

    stack = lambda prefix: jnp.concatenate([row(given[prefix + n]) for n in VECTORS] + [jnp.zeros((2, D_MODEL), F32)], axis=0)
    g_small = jnp.concatenate([small[:6], jnp.zeros((2, D_MODEL), F32)], axis=0)
    d, m1, v1 = _adamw(stack(""), g_small, stack("m_"), stack("v_"), "adamw_vectors", SMALL_ROWS)
    for k, n in enumerate(VECTORS):
        width = weights[n].shape[0]
        delta[n], new_m[n], new_v[n] = d[k, :width], m1[k, :width], v1[k, :width]

    return (loss_total, grad_x, *[grads[n] for n in WEIGHT_ORDER], *[delta[n] for n in WEIGHT_ORDER],
            *[new_m[n] for n in WEIGHT_ORDER], *[new_v[n] for n in WEIGHT_ORDER])
```

```python
import functools
import math

import jax
import jax.numpy as jnp
from jax import lax
from jax.experimental import pallas as pl
from jax.experimental.pallas import tpu as pltpu

F32 = jnp.float32
BF16 = jnp.bfloat16

D_MODEL = 1024
SEQ = 2048
N_MEM = 256
HEAD_DIM = 64
N_HEADS = 8
MIX_HALF = N_HEADS * HEAD_DIM
QKV_WIDTH = 6 * MIX_HALF
IN_WIDTH = QKV_WIDTH + N_HEADS
GATE_PAD = 128
BLOCK = 128
DILATIONS = (1, 4, 16)
X_HEADS = 4
X_HEAD_DIM = 256
D_FF = 4096
EPS = 1e-6
NEG = -1e30
ATT_SCALE = 1.0 / math.sqrt(HEAD_DIM)
X_SCALE = 1.0 / math.sqrt(X_HEAD_DIM)
LANES = 128
N_CHIPS = 4
N_DEV = 8

ADAM_LR = 0.001
ADAM_B1 = 0.9
ADAM_B2 = 0.999
ADAM_EPS = 1e-08
ADAM_WD = 0.01
ADAM_STEP = 10

VMEM_LIMIT = 48 * 1024 * 1024


def _params(*sem):
    return pltpu.CompilerParams(dimension_semantics=sem or None, vmem_limit_bytes=VMEM_LIMIT)


def _dot(a, b):
    return jnp.dot(a, b, preferred_element_type=F32)


def _dot_nt(a, b):
    return lax.dot_general(a, b, (((1,), (1,)), ((), ())), preferred_element_type=F32)


def _dot_tn(a, b):
    return lax.dot_general(a, b, (((0,), (0,)), ((), ())), preferred_element_type=F32)


def _dot_exact(x, e):
    hi = x.astype(BF16)
    r1 = x - hi.astype(F32)
    mid = r1.astype(BF16)
    lo = (r1 - mid.astype(F32)).astype(BF16)
    return _dot(hi, e) + _dot(mid, e) + _dot(lo, e)


def _head_mask(e):
    lane = lax.broadcasted_iota(jnp.int32, (1, LANES), 1)
    return (lane >= HEAD_DIM * e) & (lane < HEAD_DIM * (e + 1))


def _matmul(a, w, name, out_dtypes=(F32,), extras=(), epilogue=None, tm=1024, tn=1024, w_t=False, after=None):
    m, k = a.shape
    n = w.shape[0] if w_t else w.shape[1]
    tm, tn = min(tm, m), min(tn, n)
    assert m % tm == 0 and n % tn == 0, (name, a.shape, w.shape)
    n_ex = len(extras)
    order = () if after is None else (after,)

    def body(a_ref, w_ref, *rest):
        rest = rest[len(order):]
        acc = (_dot_nt if w_t else _dot)(a_ref[...], w_ref[...])
        res = (acc,) if epilogue is None else epilogue(acc, *[r[...] for r in rest[:n_ex]])
        for o_ref, r in zip(rest[n_ex:], res):
            o_ref[...] = r.astype(o_ref.dtype)

    tile = pl.BlockSpec((tm, tn), lambda i, j: (i, j))
    w_spec = pl.BlockSpec((tn, k), lambda i, j: (j, 0)) if w_t else pl.BlockSpec((k, tn), lambda i, j: (0, j))
    return pl.pallas_call(
        body, grid=(m // tm, n // tn),
        in_specs=[pl.BlockSpec((tm, k), lambda i, j: (i, 0)), w_spec] + [pl.BlockSpec(memory_space=pl.ANY)] * len(order) + [tile] * n_ex,
        out_specs=[tile] * len(out_dtypes),
        out_shape=[jax.ShapeDtypeStruct((m, n), dt) for dt in out_dtypes],
        name=name, compiler_params=_params("parallel", "arbitrary"),
    )(a, w, *order, *extras)


def _matmul_res(a, w, res, name, w_t=False):
    return _matmul(a, w, name, extras=(res,), epilogue=lambda acc, r: (r + acc,), w_t=w_t)[0]


def _matmul_tn(x, y, name, tm=1024, tn=1024, tk=2048, packed=None, after=None):
    t, m = x.shape
    _, n = y.shape
    tm, tn, tk = min(tm, m), min(tn, n), min(tk, t)
    assert m % tm == 0 and n % tn == 0 and t % tk == 0, (name, x.shape, y.shape)
    shape, place, into = packed or ((m, n), None, None)

    def body(x_ref, y_ref, *rest):
        o_ref = rest[-1]

        @pl.when(pl.program_id(2) == 0)
        def _():
            o_ref[...] = jnp.zeros_like(o_ref)

        o_ref[...] += _dot_tn(x_ref[...], y_ref[...])

    out_spec = (pl.BlockSpec((tm, tn), lambda i, j, k: (i, j)) if place is None
                else pl.BlockSpec((None, tm, tn), lambda i, j, k: place(i, j)))
    return pl.pallas_call(
        body, grid=(m // tm, n // tn, t // tk),
        in_specs=[pl.BlockSpec((tk, tm), lambda i, j, k: (k, i)), pl.BlockSpec((tk, tn), lambda i, j, k: (k, j))]
        + [pl.BlockSpec(memory_space=pl.ANY)] * ((into is not None) + (after is not None)),
        out_specs=out_spec, out_shape=jax.ShapeDtypeStruct(shape, F32),
        input_output_aliases={} if into is None else {2: 0},
        name=name, compiler_params=_params("parallel", "parallel", "arbitrary"),
    )(x, y, *(() if into is None else (into,)), *(() if after is None else (after,)))


def _rmsnorm(x, g, name, tm=512):
    t, d = x.shape
    tm = min(tm, t)

    def body(x_ref, g_ref, h_ref):
        xv = x_ref[...]
        r = lax.rsqrt(jnp.mean(xv * xv, axis=-1, keepdims=True) + EPS)
        h_ref[...] = (xv * r * g_ref[...]).astype(BF16)

    return pl.pallas_call(
        body, grid=(t // tm,),
        in_specs=[pl.BlockSpec((tm, d), lambda i: (i, 0)), pl.BlockSpec((1, d), lambda i: (0, 0))],
        out_specs=pl.BlockSpec((tm, d), lambda i: (i, 0)),
        out_shape=jax.ShapeDtypeStruct((t, d), BF16),
        name=name, compiler_params=_params("arbitrary"),
    )(x, g.reshape(1, d))


def _in_proj(x, g, w_all, name, tm=512):
    t, d = x.shape
    half = 3 * MIX_HALF

    def body(x_ref, g_ref, w_ref, h_ref, zd_ref, zf_ref, gate_ref):
        xv = x_ref[...]
        r = lax.rsqrt(jnp.mean(xv * xv, axis=-1, keepdims=True) + EPS)
        h = (xv * r * g_ref[...]).astype(BF16)
        h_ref[...] = h
        zd_ref[...] = _dot(h, w_ref[:, 0:half])
        zf_ref[...] = _dot(h, w_ref[:, half:2 * half]).astype(BF16)
        gate_ref[...] = _dot(h, w_ref[:, 2 * half:])

    row = lambda width: pl.BlockSpec((tm, width), lambda i: (i, 0))
    return pl.pallas_call(
        body, grid=(t // tm,),
        in_specs=[row(d), pl.BlockSpec((1, d), lambda i: (0, 0)), pl.BlockSpec(w_all.shape, lambda i: (0, 0))],
        out_specs=[row(d), row(half), row(half), row(GATE_PAD)],
        out_shape=[jax.ShapeDtypeStruct((t, d), BF16), jax.ShapeDtypeStruct((t, half), F32),
                   jax.ShapeDtypeStruct((t, half), BF16), jax.ShapeDtypeStruct((t, GATE_PAD), F32)],
        name=name, compiler_params=_params("arbitrary"),
    )(x, g.reshape(1, d), w_all)


def _rms_bwd_tile(xv, dh, g):
    d = xv.shape[-1]
    r = lax.rsqrt(jnp.mean(xv * xv, axis=-1, keepdims=True) + EPS)
    dyg = dh * g
    proj = jnp.sum(dyg * xv, axis=-1, keepdims=True)
    dx = r * dyg - xv * (r * r * r * (1.0 / d)) * proj
    return dx, dh * (xv * r)


def _rms_bwd(x, dh, g, dres, name, tm=512):
    t, d = x.shape
    tm = min(tm, t)
    has_res = dres is not None

    def body(x_ref, dh_ref, g_ref, *rest):
        if has_res:
            res_ref, dx_ref, dxb_ref, dg_ref = rest
        else:
            dx_ref, dxb_ref, dg_ref = rest
        dx, dg_rows = _rms_bwd_tile(x_ref[...], dh_ref[...], g_ref[...])
        if has_res:
            dx = res_ref[...] + dx
        dx_ref[...] = dx
        dxb_ref[...] = dx.astype(BF16)

        @pl.when(pl.program_id(0) == 0)
        def _():
            dg_ref[...] = jnp.zeros_like(dg_ref)

        dg_ref[...] += jnp.sum(dg_rows, axis=0, keepdims=True)

    row = pl.BlockSpec((tm, d), lambda i: (i, 0))
    vec = pl.BlockSpec((1, d), lambda i: (0, 0))
    return pl.pallas_call(
        body, grid=(t // tm,),
        in_specs=[row, row, vec] + ([row] if has_res else []),
        out_specs=[row, row, vec],
        out_shape=[jax.ShapeDtypeStruct((t, d), F32), jax.ShapeDtypeStruct((t, d), BF16), jax.ShapeDtypeStruct((1, d), F32)],
        name=name, compiler_params=_params("arbitrary"),
    )(x, dh, g.reshape(1, d), *((dres,) if has_res else ()))


def _row_dots(a_refs, w_refs, w_t):
    acc = None
    for a_ref, w_ref in zip(a_refs, w_refs):
        part = (_dot_nt if w_t else _dot)(a_ref[...], w_ref[...])
        acc = part if acc is None else acc + part
    return acc


def _row_specs(a_parts, w_parts, tm):
    specs = [pl.BlockSpec((tm, a.shape[1]), lambda i: (i, 0)) for a in a_parts]
    return specs + [pl.BlockSpec(w.shape, lambda i: (0, 0)) for w in w_parts]


def _matmul_rms_bwd(a_parts, w_parts, x, g, dres, name, tm=512, after=None):
    t, d = x.shape
    n = len(a_parts)
    order = () if after is None else (after,)

    def body(*refs):
        x_ref, g_ref, res_ref = refs[2 * n:2 * n + 3]
        dx_ref, dxb_ref, dg_ref = refs[2 * n + 3 + len(order):]
        dx, dg_rows = _rms_bwd_tile(x_ref[...], _row_dots(refs[:n], refs[n:2 * n], True), g_ref[...])
        dx = res_ref[...] + dx
        dx_ref[...] = dx
        dxb_ref[...] = dx.astype(BF16)

        @pl.when(pl.program_id(0) == 0)
        def _():
            dg_ref[...] = jnp.zeros_like(dg_ref)

        dg_ref[...] += jnp.sum(dg_rows, axis=0, keepdims=True)

    row = pl.BlockSpec((tm, d), lambda i: (i, 0))
    vec = pl.BlockSpec((1, d), lambda i: (0, 0))
    return pl.pallas_call(
        body, grid=(t // tm,),
        in_specs=_row_specs(a_parts, w_parts, tm) + [row, vec, row] + [pl.BlockSpec(memory_space=pl.ANY)] * len(order),
        out_specs=[row, row, vec],
        out_shape=[jax.ShapeDtypeStruct((t, d), F32), jax.ShapeDtypeStruct((t, d), BF16), jax.ShapeDtypeStruct((1, d), F32)],
        name=name, compiler_params=_params("arbitrary"),
    )(*a_parts, *w_parts, x, g.reshape(1, d), dres, *order)


def _loss_bwd(a, w, res, g, target, name, tm=512):
    t, d = res.shape

    def body(a_ref, w_ref, x_ref, g_ref, t_ref, loss_ref, dx_ref, dxb_ref, dg_ref):
        xv = x_ref[...] + _dot(a_ref[...], w_ref[...])
        gv = g_ref[...]
        r = lax.rsqrt(jnp.mean(xv * xv, axis=-1, keepdims=True) + EPS)
        err = xv * r * gv - t_ref[...]
        dx, dg_rows = _rms_bwd_tile(xv, err * (1.0 / d), gv)
        dx_ref[...] = dx
        dxb_ref[...] = dx.astype(BF16)

        @pl.when(pl.program_id(0) == 0)
        def _():
            dg_ref[...] = jnp.zeros_like(dg_ref)
            loss_ref[...] = jnp.zeros_like(loss_ref)

        dg_ref[...] += jnp.sum(dg_rows, axis=0, keepdims=True)
        part = jnp.sum(jnp.sum(err * err, axis=0, keepdims=True), axis=1, keepdims=True) * (0.5 / d)
        loss_ref[...] += jnp.broadcast_to(part, loss_ref.shape)

    row = pl.BlockSpec((tm, d), lambda i: (i, 0))
    vec = pl.BlockSpec((1, d), lambda i: (0, 0))
    return pl.pallas_call(
        body, grid=(t // tm,),
        in_specs=_row_specs([a], [w], tm) + [row, vec, row],
        out_specs=[pl.BlockSpec((1, LANES), lambda i: (0, 0)), row, row, vec],
        out_shape=[jax.ShapeDtypeStruct((1, LANES), F32), jax.ShapeDtypeStruct((t, d), F32),
                   jax.ShapeDtypeStruct((t, d), BF16), jax.ShapeDtypeStruct((1, d), F32)],
        name=name, compiler_params=_params("arbitrary"),
    )(a, w, res, g.reshape(1, d), target)


def _tri(upper):
    r = lax.broadcasted_iota(jnp.int32, (LANES, LANES), 0)
    c = lax.broadcasted_iota(jnp.int32, (LANES, LANES), 1)
    return jnp.where((r <= c) if upper else (r >= c), 1.0, 0.0).astype(BF16)


def _gate_fwd(gate, b_pad, n_batch, name):
    s = SEQ
    nblk = s // LANES

    def body(g_ref, b_ref, cbc_ref, crow_ref, sg_ref, ct_ref):
        gz = g_ref[...] + b_ref[...]
        logf = jnp.minimum(gz, 0.0) - jnp.log(1.0 + jnp.exp(-jnp.abs(gz)))
        logf_t = logf.T
        sg_ref[...] = (1.0 / (1.0 + jnp.exp(gz))).T[0:N_HEADS]
        upper = _tri(True)
        carry = jnp.zeros((LANES, 1), F32)
        for blk in range(nblk):
            seg = _dot_exact(logf_t[:, blk * LANES:(blk + 1) * LANES], upper) + carry
            carry = seg[:, LANES - 1:LANES]
            ct_ref[:, blk * LANES:(blk + 1) * LANES] = seg
        ct = ct_ref[...]
        crow_ref[...] = ct[0:N_HEADS]
        c_col = ct.T
        lane = lax.broadcasted_iota(jnp.int32, (1, MIX_HALF), 1)
        acc = jnp.zeros((s, MIX_HALF), F32)
        for h in range(N_HEADS):
            acc = jnp.where((lane >= HEAD_DIM * h) & (lane < HEAD_DIM * (h + 1)), c_col[:, h:h + 1], acc)
        cbc_ref[...] = acc

    return pl.pallas_call(
        body, grid=(n_batch,),
        in_specs=[pl.BlockSpec((s, GATE_PAD), lambda b: (b, 0)), pl.BlockSpec((1, GATE_PAD), lambda b: (0, 0))],
        out_specs=[pl.BlockSpec((s, MIX_HALF), lambda b: (b, 0)),
                   pl.BlockSpec((None, N_HEADS, s), lambda b: (b, 0, 0)),
                   pl.BlockSpec((None, N_HEADS, s), lambda b: (b, 0, 0))],
        out_shape=[jax.ShapeDtypeStruct((n_batch * s, MIX_HALF), F32),
                   jax.ShapeDtypeStruct((n_batch, N_HEADS, s), F32),
                   jax.ShapeDtypeStruct((n_batch, N_HEADS, s), F32)],
        scratch_shapes=[pltpu.VMEM((LANES, s), F32)],
        name=name, compiler_params=_params("arbitrary"),
    )(gate, b_pad)


def _gate_bwd(dc, sg, name):
    n_batch, _, s = dc.shape
    nblk = s // LANES

    def body(dc_ref, sg_ref, dz_ref, db_ref, dt_ref):
        lower = _tri(False)
        dcv = dc_ref[...]
        carry = jnp.zeros((N_HEADS, 1), F32)
        dt_ref[...] = jnp.zeros_like(dt_ref)
        for blk in reversed(range(nblk)):
            seg = _dot_exact(dcv[:, blk * LANES:(blk + 1) * LANES], lower) + carry
            carry = seg[:, 0:1]
            dt_ref[0:N_HEADS, blk * LANES:(blk + 1) * LANES] = seg * sg_ref[:, blk * LANES:(blk + 1) * LANES]
        dg_t = dt_ref[...]
        dz_ref[...] = dg_t.T.astype(BF16)

        @pl.when(pl.program_id(0) == 0)
        def _():
            db_ref[...] = jnp.zeros_like(db_ref)

        db_ref[...] += jnp.broadcast_to(jnp.sum(dg_t[0:N_HEADS], axis=1, keepdims=True), db_ref.shape)

    return pl.pallas_call(
        body, grid=(n_batch,),
        in_specs=[pl.BlockSpec((None, N_HEADS, s), lambda b: (b, 0, 0)), pl.BlockSpec((None, N_HEADS, s), lambda b: (b, 0, 0))],
        out_specs=[pl.BlockSpec((s, GATE_PAD), lambda b: (b, 0)), pl.BlockSpec((N_HEADS, LANES), lambda b: (0, 0))],
        out_shape=[jax.ShapeDtypeStruct((n_batch * s, GATE_PAD), BF16), jax.ShapeDtypeStruct((N_HEADS, LANES), F32)],
        scratch_shapes=[pltpu.VMEM((LANES, s), F32)],
        name=name, compiler_params=_params("arbitrary"),
    )(dc, sg)


FOX_BQ = 512
FOX_BK = 512
FOX_STRIP = 512
PAIR_WIDTH = 3 * LANES
N_PAIRS = N_HEADS // 2


def _pair_major(w):
    return w.reshape(w.shape[0], 3, N_PAIRS, LANES).transpose(0, 2, 1, 3).reshape(w.shape[0], 3 * MIX_HALF)


def _pair_major_inv(w):
    return w.reshape(w.shape[0], N_PAIRS, 3, LANES).transpose(0, 2, 1, 3).reshape(w.shape[0], 3 * MIX_HALF)


def _causal(i, j, bq, bk):
    qpos = i * bq + lax.broadcasted_iota(jnp.int32, (bq, 1), 0)
    kpos = j * bk + lax.broadcasted_iota(jnp.int32, (1, bk), 1)
    return kpos <= qpos


def _split_bf16(p):
    hi = p.astype(BF16)
    return hi, (p - hi.astype(F32)).astype(BF16)


def _fox_fwd(zf, c_bc, c_row, n_batch, name):
    s, bq, bk = SEQ, FOX_BQ, FOX_BK
    nq = s // bq
    t = n_batch * s

    n_strip = bq // FOX_STRIP

    def body(q_ref, k_ref, v_ref, cq_ref, cr_ref, o_ref, o32_ref, lse_ref):
        hp = pl.program_id(1)
        strips = [slice(r * FOX_STRIP, (r + 1) * FOX_STRIP) for r in range(n_strip)]
        chains = [(e, r) for e in range(2) for r in range(n_strip)]
        qh, cq = {}, {}
        for e, r in chains:
            q = q_ref[strips[r], :] * ATT_SCALE
            qh[e, r] = jnp.where(_head_mask(e), q, jnp.zeros_like(q))
            cq[e, r] = cq_ref[strips[r], HEAD_DIM * e:HEAD_DIM * e + 1]

        def step(i, j, carry, masked):
            rows = pl.ds(j * bk, bk)
            kj, vj = k_ref[rows, :], v_ref[rows, :]
            ck = [cr_ref[pl.ds(2 * hp + e, 1), rows] for e in range(2)]
            out = []
            scores = [_dot_nt(qh[e, r], kj) for e, r in chains]
            for n, (e, r) in enumerate(chains):
                m, l, acc = carry[3 * n:3 * n + 3]
                sc = scores[n] + (cq[e, r] - ck[e])
                if masked:
                    qpos = i * bq + r * FOX_STRIP + lax.broadcasted_iota(jnp.int32, (FOX_STRIP, 1), 0)
                    kpos = j * bk + lax.broadcasted_iota(jnp.int32, (1, bk), 1)
                    sc = jnp.where(kpos <= qpos, sc, NEG)
                m_new = jnp.maximum(m, jnp.max(sc, axis=1, keepdims=True))
                alpha = jnp.exp(m - m_new)
                p = jnp.exp(sc - m_new)
                p_hi, p_lo = _split_bf16(p)
                out += [m_new, alpha * l + jnp.sum(p, axis=1, keepdims=True), alpha * acc + (_dot(p_hi, vj) + _dot(p_lo, vj))]
            return tuple(out)

        def run(i):
            carry = (jnp.full((FOX_STRIP, 1), NEG, F32), jnp.zeros((FOX_STRIP, 1), F32), jnp.zeros((FOX_STRIP, LANES), F32)) * len(chains)
            n_clear = (i * bq) // bk
            for j in range((i * bq + bq + bk - 1) // bk):
                carry = step(i, j, carry, masked=j >= n_clear)
            for r in range(n_strip):
                outs = [carry[3 * (e * n_strip + r) + 2] / carry[3 * (e * n_strip + r) + 1] for e in range(2)]
                lses = [carry[3 * (e * n_strip + r)] + jnp.log(carry[3 * (e * n_strip + r) + 1]) for e in range(2)]
                o = jnp.where(_head_mask(0), outs[0], outs[1])
                o_ref[strips[r], :] = o.astype(BF16)
                o32_ref[strips[r], :] = o
                lse_ref[strips[r], :] = jnp.where(_head_mask(0), lses[0], lses[1])

        for k in range(nq):
            pl.when(pl.program_id(2) == k)(functools.partial(run, k))

    def col(c0):
        return lambda b, hp, i: (b, 3 * hp + c0)

    blk = pl.BlockSpec((bq, LANES), lambda b, hp, i: (b * nq + i, hp))
    return pl.pallas_call(
        body, grid=(n_batch, N_PAIRS, nq),
        in_specs=[pl.BlockSpec((bq, LANES), lambda b, hp, i: (b * nq + i, 3 * hp)),
                  pl.BlockSpec((s, LANES), col(1)), pl.BlockSpec((s, LANES), col(2)), blk,
                  pl.BlockSpec((None, N_HEADS, s), lambda b, hp, i: (b, 0, 0))],
        out_specs=[blk, blk, blk],
        out_shape=[jax.ShapeDtypeStruct((t, MIX_HALF), BF16), jax.ShapeDtypeStruct((t, MIX_HALF), F32),
                   jax.ShapeDtypeStruct((t, MIX_HALF), F32)],
        name=name, compiler_params=_params("parallel", "parallel", "arbitrary"),
    )(zf, zf, zf, c_bc, c_row)


def _fox_bwd(zf, o32, dy, lse, c_bc, c_row, dz, n_batch, name):
    s, bq, bk = SEQ, FOX_BQ, FOX_BK
    nq, nk = s // bq, s // bk

    def body(q_ref, k_ref, v_ref, o_ref, do_ref, lse_ref, cq_ref, cr_ref, dz_in, dz_ref, dc_ref, dq_acc):
        del dz_in
        hp = pl.program_id(1)

        @pl.when(pl.program_id(2) == 0)
        def _():
            dq_acc[...] = jnp.zeros_like(dq_acc)

        kj, vj = k_ref[...], v_ref[...]
        km = [jnp.where(_head_mask(e), kj, jnp.zeros_like(kj)) for e in range(2)]

        def step(i, j, ck, carry, masked):
            rows = pl.ds(i * bq, bq)
            qi, doi = q_ref[rows, :] * ATT_SCALE, do_ref[rows, :]
            prod = doi.astype(F32) * o_ref[rows, :]
            out = []
            dq = jnp.zeros((bq, LANES), F32)
            for e in range(2):
                dk_a, dv_a, dc_a = carry[3 * e:3 * e + 3]
                mask = _head_mask(e)
                lane0 = HEAD_DIM * e
                dom = jnp.where(mask, doi, jnp.zeros_like(doi))
                delta = jnp.sum(jnp.where(mask, prod, 0.0), axis=1, keepdims=True)
                sc = _dot_nt(qi, km[e]) + (cq_ref[rows, lane0:lane0 + 1] - ck[e])
                if masked:
                    sc = jnp.where(_causal(i, j, bq, bk), sc, NEG)
                p = jnp.exp(sc - lse_ref[rows, lane0:lane0 + 1])
                ds = p * (_dot_nt(dom, vj) - delta)
                dsb = ds.astype(BF16)
                dq = dq + _dot(dsb, km[e])
                out += [dk_a + _dot_tn(dsb, qi), dv_a + _dot_tn(p.astype(BF16), dom), dc_a - jnp.sum(ds, axis=0, keepdims=True)]
            dq_acc[rows, :] += dq * ATT_SCALE
            return tuple(out)

        def run(j):
            cols = pl.ds(j * bk, bk)
            ck = [cr_ref[pl.ds(2 * hp + e, 1), cols] for e in range(2)]
            carry = (jnp.zeros((bk, LANES), F32), jnp.zeros((bk, LANES), F32), jnp.zeros((1, bk), F32)) * 2
            n_diag = (j * bk + bk + bq - 1) // bq
            for i in range((j * bk) // bq, nq):
                carry = step(i, j, ck, carry, masked=i < n_diag)
            for e in range(2):
                dc_ref[e:e + 1, :] = carry[3 * e + 2]
            dz_ref[cols, LANES:2 * LANES] = jnp.where(_head_mask(0), carry[0], carry[3]).astype(BF16)
            dz_ref[cols, 2 * LANES:3 * LANES] = (carry[1] + carry[4]).astype(BF16)
            if j == nk - 1:
                dz_ref[:, 0:LANES] = dq_acc[...].astype(BF16)

        for k in range(nk):
            pl.when(pl.program_id(2) == k)(functools.partial(run, k))

    def seq(idx):
        return pl.BlockSpec((s, LANES), lambda b, hp, j: (b, idx(hp)))

    def kblk(c0):
        return pl.BlockSpec((bk, LANES), lambda b, hp, j: (b * nk + j, 3 * hp + c0))

    return pl.pallas_call(
        body, grid=(n_batch, N_PAIRS, nk),
        in_specs=[seq(lambda hp: 3 * hp), kblk(1), kblk(2), seq(lambda hp: hp), seq(lambda hp: N_PAIRS + hp),
                  seq(lambda hp: hp), seq(lambda hp: hp),
                  pl.BlockSpec((None, N_HEADS, s), lambda b, hp, j: (b, 0, 0)), pl.BlockSpec(memory_space=pl.ANY)],
        out_specs=[pl.BlockSpec((s, PAIR_WIDTH), lambda b, hp, j: (b, N_PAIRS + hp)),
                   pl.BlockSpec((None, None, 2, bk), lambda b, hp, j: (b, hp, 0, j))],
        out_shape=[jax.ShapeDtypeStruct(dz.shape, dz.dtype), jax.ShapeDtypeStruct((n_batch, N_PAIRS, 2, s), F32)],
        scratch_shapes=[pltpu.VMEM((s, LANES), F32)],
        input_output_aliases={8: 0},
        name=name, compiler_params=_params("parallel", "parallel", "arbitrary"),
    )(zf, zf, zf, o32, dy, lse, c_bc, c_row, dz)


def _dil_bias(slope, dil):
    qi = lax.broadcasted_iota(jnp.int32, (BLOCK, 2 * BLOCK), 0)
    kj = lax.broadcasted_iota(jnp.int32, (BLOCK, 2 * BLOCK), 1)
    delta = qi + BLOCK - kj
    return jnp.where((delta >= 0) & (delta <= BLOCK), (-slope * dil) * delta.astype(F32), NEG)


def _alibi_slope(hp, e):
    slope = jnp.float32(0.0)
    for k in range(N_PAIRS):
        slope = jnp.where(hp == k, jnp.float32(2.0 ** -(2 * k + e + 1)), slope)
    return slope


def _first_block_bias(bias):
    return jnp.where(lax.broadcasted_iota(jnp.int32, bias.shape, 1) < BLOCK, NEG, bias)


def _fill_bias(bias_scr, hp):
    for di, dil in enumerate(DILATIONS):
        for e in range(2):
            bias_scr[2 * di + e] = _dil_bias(_alibi_slope(hp, e), dil)


def _pair_specs(rows):
    return [pl.BlockSpec((rows, LANES), lambda b, hp, c0=c0: (b, 3 * hp + c0)) for c0 in range(3)]


def _strided(start, size, dil):
    return pl.ds(start, size) if dil == 1 else pl.ds(start, size, stride=dil)


QUARTER = SEQ // 4


def _to_quarters(src, dst):
    for r in range(4):
        dst[r * QUARTER:(r + 1) * QUARTER, :] = src[pl.ds(r, QUARTER, stride=4), :]


def _from_quarters(src, dst):
    for r in range(4):
        dst[pl.ds(r, QUARTER, stride=4), :] = src[r * QUARTER:(r + 1) * QUARTER, :]


def _mix_weights(l1, l2, l3):
    m = jnp.maximum(jnp.maximum(l1, l2), l3)
    e1, e2, e3 = jnp.exp(l1 - m), jnp.exp(l2 - m), jnp.exp(l3 - m)
    inv = 1.0 / (e1 + e2 + e3)
    return e1 * inv, e2 * inv, e3 * inv


def _dil_fwd(zd, n_batch, name):
    s = SEQ
    t = n_batch * s

    def body(q_ref, k_ref, v_ref, y_ref, l1_ref, l2_ref, l3_ref, o_scr, qkv4, o4, l4, bias_scr):
        _fill_bias(bias_scr, pl.program_id(1))
        for a, ref in enumerate((q_ref, k_ref, v_ref)):
            _to_quarters(ref, qkv4.at[a])

        def unit(srcs, start, first, stride, di, o_dst, l_dst):
            qrows = _strided(start, BLOCK, stride)
            krows = qrows if first else _strided(start - BLOCK * stride, 2 * BLOCK, stride)
            q = (srcs[0][qrows, :] * ATT_SCALE).astype(BF16)
            kc = srcs[1][krows, :].astype(BF16)
            vc = srcs[2][krows, :].astype(BF16)
            if first:
                kc, vc = jnp.concatenate([kc, kc]), jnp.concatenate([vc, vc])
            outs, lses = [], []
            for e in range(2):
                bias = _first_block_bias(bias_scr[2 * di + e]) if first else bias_scr[2 * di + e]
                sc = _dot_nt(jnp.where(_head_mask(e), q, jnp.zeros_like(q)), kc) + bias
                m = jnp.max(sc, axis=1, keepdims=True)
                pe = jnp.exp(sc - m)
                l = jnp.sum(pe, axis=1, keepdims=True)
                outs.append(_dot((pe * (1.0 / l)).astype(BF16), vc))
                lses.append(m + jnp.log(l))
            o_dst[qrows, :] = jnp.where(_head_mask(0), outs[0], outs[1])
            l_dst[qrows, :] = jnp.where(_head_mask(0), lses[0], lses[1])

        for n in range(SEQ // BLOCK):
            unit((q_ref, k_ref, v_ref), n * BLOCK, n == 0, 1, 0, o_scr.at[0], l1_ref)
        quarters = tuple(qkv4.at[a] for a in range(3))
        for di in (1, 2):
            stride = DILATIONS[di] // 4
            for r in range(4):
                for g in range(stride):
                    for n in range(QUARTER // (BLOCK * stride)):
                        unit(quarters, r * QUARTER + n * BLOCK * stride + g, n == 0, stride, di, o4.at[di - 1], l4.at[di - 1])
        for di, l_ref in ((1, l2_ref), (2, l3_ref)):
            _from_quarters(o4.at[di - 1], o_scr.at[di])
            _from_quarters(l4.at[di - 1], l_ref)
        w = _mix_weights(l1_ref[...], l2_ref[...], l3_ref[...])
        y_ref[...] = (w[0] * o_scr[0] + w[1] * o_scr[1] + w[2] * o_scr[2]).astype(BF16)

    blk = pl.BlockSpec((s, LANES), lambda b, hp: (b, hp))
    res = pl.pallas_call(
        body, grid=(n_batch, N_PAIRS),
        in_specs=_pair_specs(s),
        out_specs=[blk] * 4,
        out_shape=[jax.ShapeDtypeStruct((t, MIX_HALF), BF16)] + [jax.ShapeDtypeStruct((t, MIX_HALF), F32)] * 3,
        scratch_shapes=[pltpu.VMEM((3, s, LANES), F32), pltpu.VMEM((3, s, LANES), F32), pltpu.VMEM((2, s, LANES), F32),
                        pltpu.VMEM((2, s, LANES), F32), pltpu.VMEM((6, BLOCK, 2 * BLOCK), F32)],
        name=name, compiler_params=_params("parallel", "arbitrary"),
    )(zd, zd, zd)
    return res[0], res[1:]


def _dil_bwd(zd, dy, ya, lses, n_batch, name):
    s = SEQ
    t = n_batch * s

    def body(q_ref, k_ref, v_ref, dy_ref, ya_ref, l1_ref, l2_ref, l3_ref, dz_ref, w_scr, dy_scr, dot_scr, acc, st4, acc4, bias_scr):
        for di, dil in enumerate(DILATIONS):
            bias_scr[di] = jnp.concatenate([_dil_bias(_alibi_slope(pl.program_id(1), e), dil) for e in range(2)])
        for di, w in enumerate(_mix_weights(l1_ref[...], l2_ref[...], l3_ref[...])):
            w_scr[di] = w
        dya = dy_ref[...].astype(F32)
        prod = dya * ya_ref[...].astype(F32)
        per_head = [jnp.sum(jnp.where(_head_mask(e), prod, 0.0), axis=1, keepdims=True) for e in range(2)]
        dy_scr[...] = dya
        dot_scr[...] = jnp.where(_head_mask(0), per_head[0], per_head[1])
        acc[...] = jnp.zeros_like(acc)
        acc4[...] = jnp.zeros_like(acc4)
        staged = (q_ref, k_ref, v_ref, w_scr.at[1], w_scr.at[2], l2_ref, l3_ref, dy_scr, dot_scr)
        for a, ref in enumerate(staged):
            _to_quarters(ref, st4.at[a])

        def unit(srcs, dst, start, first, stride, di):
            qrows = _strided(start, BLOCK, stride)
            krows = qrows if first else _strided(start - BLOCK * stride, 2 * BLOCK, stride)
            q = (srcs[0][qrows, :] * ATT_SCALE).astype(BF16)
            kc = srcs[1][krows, :].astype(BF16)
            vc = srcs[2][krows, :].astype(BF16)
            wq = srcs[3][qrows, :]
            lse = srcs[4][qrows, :]
            do = (wq * srcs[5][qrows, :]).astype(BF16)
            sub = wq * srcs[6][qrows, :]
            heads = lambda a: jnp.concatenate([jnp.where(_head_mask(e), a, jnp.zeros_like(a)) for e in range(2)])
            column = lambda a: jnp.concatenate([a[:, HEAD_DIM * e:HEAD_DIM * e + 1] for e in range(2)])
            qq, dd = heads(q), heads(do)
            bias = bias_scr[di]
            p = jnp.exp(_dot_nt(qq, kc) + (bias[:, BLOCK:] if first else bias) - column(lse))
            dsb = (p * (_dot_nt(dd, vc) - column(sub))).astype(BF16)
            dq = _dot(jnp.concatenate([dsb[:BLOCK], dsb[BLOCK:]], axis=1), heads(kc))
            dst.at[0][qrows, :] += dq * ATT_SCALE
            dst.at[1][krows, :] += _dot_tn(dsb, qq)
            dst.at[2][krows, :] += _dot_tn(p.astype(BF16), dd)

        token_order = (q_ref, k_ref, v_ref, w_scr.at[0], l1_ref, dy_scr, dot_scr)
        for n in range(SEQ // BLOCK):
            unit(token_order, acc, n * BLOCK, n == 0, 1, 0)
        for di in (1, 2):
            quarters = (st4.at[0], st4.at[1], st4.at[2], st4.at[2 + di], st4.at[4 + di], st4.at[7], st4.at[8])
            stride = DILATIONS[di] // 4
            for r in range(4):
                for g in range(stride):
                    for n in range(QUARTER // (BLOCK * stride)):
                        unit(quarters, acc4, r * QUARTER + n * BLOCK * stride + g, n == 0, stride, di)
        for k in range(3):
            for r in range(4):
                acc.at[k][pl.ds(r, QUARTER, stride=4), :] += acc4[k, r * QUARTER:(r + 1) * QUARTER, :]
            dz_ref[:, k * LANES:(k + 1) * LANES] = acc[k].astype(BF16)

    blk = pl.BlockSpec((s, LANES), lambda b, hp: (b, hp))
    pair = pl.BlockSpec((s, PAIR_WIDTH), lambda b, hp: (b, hp))
    return pl.pallas_call(
        body, grid=(n_batch, N_PAIRS),
        in_specs=_pair_specs(s) + [blk] * 5,
        out_specs=pair,
        out_shape=jax.ShapeDtypeStruct((t, 2 * 3 * MIX_HALF), BF16),
        scratch_shapes=[pltpu.VMEM((3, s, LANES), F32), pltpu.VMEM((s, LANES), F32), pltpu.VMEM((s, LANES), F32),
                        pltpu.VMEM((3, s, LANES), F32), pltpu.VMEM((9, s, LANES), F32), pltpu.VMEM((3, s, LANES), F32),
                        pltpu.VMEM((3, 2 * BLOCK, 2 * BLOCK), F32)],
        name=name, compiler_params=_params("parallel", "arbitrary"),
    )(zd, zd, zd, dy, ya, *lses)


def _xattn_probs(q, k):
    sc = _dot_nt(q, k) * X_SCALE
    pe = jnp.exp(sc - jnp.max(sc, axis=1, keepdims=True))
    return pe / jnp.sum(pe, axis=1, keepdims=True)


def _rms(xv, g):
    return (xv * lax.rsqrt(jnp.mean(xv * xv, axis=-1, keepdims=True) + EPS) * g).astype(BF16)


def _out_xattn_fwd(ya, yf, x0, w_out, g_xattn, w_xq, kx, vx, w_xo, g_mlp, name, tm=512):
    t, d = x0.shape
    per_example = SEQ // tm

    def body(ya_ref, yf_ref, x0_ref, wa_ref, wf_ref, g2_ref, wq_ref, k_ref, v_ref, wo_ref, g3_ref,
             x1_ref, h2_ref, q_ref, o_ref, x2_ref, h3_ref):
        x1 = x0_ref[...] + (_dot(ya_ref[...], wa_ref[...]) + _dot(yf_ref[...], wf_ref[...]))
        x1_ref[...] = x1
        h2 = _rms(x1, g2_ref[...])
        h2_ref[...] = h2
        q = _dot(h2, wq_ref[...]).astype(BF16)
        q_ref[...] = q
        for h in range(X_HEADS):
            cols = slice(h * X_HEAD_DIM, (h + 1) * X_HEAD_DIM)
            p = _xattn_probs(q[:, cols], k_ref[:, cols])
            o_ref[:, cols] = _dot(p.astype(BF16), v_ref[:, cols]).astype(BF16)
        x2 = x1 + _dot(o_ref[...], wo_ref[...])
        x2_ref[...] = x2
        h3_ref[...] = _rms(x2, g3_ref[...])

    row = lambda width: pl.BlockSpec((tm, width), lambda i: (i, 0))
    whole = lambda a: pl.BlockSpec(a.shape, lambda i: (0, 0))
    mem = pl.BlockSpec((N_MEM, d), lambda i: (i // per_example, 0))
    vec = pl.BlockSpec((1, d), lambda i: (0, 0))
    w_a, w_f = w_out[:MIX_HALF], w_out[MIX_HALF:]
    return pl.pallas_call(
        body, grid=(t // tm,),
        in_specs=[row(MIX_HALF), row(MIX_HALF), row(d), whole(w_a), whole(w_f), vec, whole(w_xq), mem, mem, whole(w_xo), vec],
        out_specs=[row(d)] * 6,
        out_shape=[jax.ShapeDtypeStruct((t, d), dt) for dt in (F32, BF16, BF16, BF16, F32, BF16)],
        name=name, compiler_params=_params("arbitrary"),
    )(ya, yf, x0, w_a, w_f, g_xattn.reshape(1, d), w_xq, kx, vx, w_xo, g_mlp.reshape(1, d))


def _xattn_chain_bwd(dx2, dx2b, x1, g_xattn, qx, kx, vx, w_xo, w_xq, name, tm=512, after=None):
    t, d = x1.shape
    per_example = SEQ // tm
    order = () if after is None else (after,)

    def body(dx2_ref, dx2b_ref, x1_ref, g_ref, q_ref, k_ref, v_ref, wo_ref, wq_ref, *rest):
        dx1_ref, dx1b_ref, dg_ref, dq_ref, dk_ref, dv_ref, dk_acc, dv_acc = rest[len(order):]
        i = pl.program_id(0)

        @pl.when(i % per_example == 0)
        def _():
            dk_acc[...] = jnp.zeros_like(dk_acc)
            dv_acc[...] = jnp.zeros_like(dv_acc)

        do = _dot_nt(dx2b_ref[...], wo_ref[...]).astype(BF16)
        for h in range(X_HEADS):
            cols = slice(h * X_HEAD_DIM, (h + 1) * X_HEAD_DIM)
            q, k, do_h = q_ref[:, cols], k_ref[:, cols], do[:, cols]
            p = _xattn_probs(q, k)
            dp = _dot_nt(do_h, v_ref[:, cols])
            dsb = (p * (dp - jnp.sum(p * dp, axis=1, keepdims=True))).astype(BF16)
            dq_ref[:, cols] = (_dot(dsb, k) * X_SCALE).astype(BF16)
            dk_acc[:, cols] += _dot_tn(dsb, q) * X_SCALE
            dv_acc[:, cols] += _dot_tn(p.astype(BF16), do_h)

        @pl.when(i % per_example == per_example - 1)
        def _():
            dk_ref[...] = dk_acc[...].astype(BF16)
            dv_ref[...] = dv_acc[...].astype(BF16)

        dx, dg_rows = _rms_bwd_tile(x1_ref[...], _dot_nt(dq_ref[...], wq_ref[...]), g_ref[...])
        dx = dx2_ref[...] + dx
        dx1_ref[...] = dx
        dx1b_ref[...] = dx.astype(BF16)

        @pl.when(i == 0)
        def _():
            dg_ref[...] = jnp.zeros_like(dg_ref)

        dg_ref[...] += jnp.sum(dg_rows, axis=0, keepdims=True)

    row = pl.BlockSpec((tm, d), lambda i: (i, 0))
    vec = pl.BlockSpec((1, d), lambda i: (0, 0))
    mem = pl.BlockSpec((N_MEM, d), lambda i: (i // per_example, 0))
    whole = lambda a: pl.BlockSpec(a.shape, lambda i: (0, 0))
    return pl.pallas_call(
        body, grid=(t // tm,),
        in_specs=[row, row, row, vec, row, mem, mem, whole(w_xo), whole(w_xq)] + [pl.BlockSpec(memory_space=pl.ANY)] * len(order),
        out_specs=[row, row, vec, row, mem, mem],
        out_shape=[jax.ShapeDtypeStruct((t, d), F32), jax.ShapeDtypeStruct((t, d), BF16), jax.ShapeDtypeStruct((1, d), F32),
                   jax.ShapeDtypeStruct((t, d), BF16), jax.ShapeDtypeStruct(kx.shape, BF16), jax.ShapeDtypeStruct(kx.shape, BF16)],
        scratch_shapes=[pltpu.VMEM((N_MEM, d), F32)] * 2,
        name=name, compiler_params=_params("arbitrary"),
    )(dx2, dx2b, x1, g_xattn.reshape(1, d), qx, kx, vx, w_xo, w_xq, *order)


def _adamw(w, g, m, v, name, rows):
    r, c = w.shape
    assert r % rows == 0, (name, w.shape, rows)

    def body(w_ref, g_ref, m_ref, v_ref, d_ref, nm_ref, nv_ref):
        gv = g_ref[...]
        m1 = ADAM_B1 * m_ref[...] + (1.0 - ADAM_B1) * gv
        v1 = ADAM_B2 * v_ref[...] + (1.0 - ADAM_B2) * jnp.square(gv)
        m_hat = m1 / (1.0 - ADAM_B1 ** ADAM_STEP)
        v_hat = v1 / (1.0 - ADAM_B2 ** ADAM_STEP)
        d_ref[...] = -ADAM_LR * (m_hat / (jnp.sqrt(v_hat) + ADAM_EPS) + ADAM_WD * w_ref[...])
        nm_ref[...] = m1
        nv_ref[...] = v1

    blk = pl.BlockSpec((rows, c), lambda i: (i, 0))
    return pl.pallas_call(
        body, grid=(r // rows,), in_specs=[blk] * 4, out_specs=[blk] * 3,
        out_shape=[jax.ShapeDtypeStruct((r, c), F32)] * 3,
        name=name, compiler_params=_params("arbitrary"),
    )(w, g, m, v)


def _relu2(acc):
    a = jnp.maximum(acc, 0.0)
    return acc, a * a


def _relu2_bwd(acc, u):
    return (2.0 * jnp.maximum(u.astype(F32), 0.0) * acc,)


def _local_step(x, mem, target, vecs, w_in, late_weights, hooks=None):
    n_batch = x.shape[0]
    t = n_batch * SEQ
    x0 = x.reshape(t, D_MODEL)
    mem2 = mem.reshape(n_batch * N_MEM, D_MODEL)
    tgt = target.reshape(t, D_MODEL)

    half = 3 * MIX_HALF
    w_qkv = jnp.concatenate([_pair_major(w_in[:, :half]), _pair_major(w_in[:, half:QKV_WIDTH])], axis=1)
    w_gate = jnp.pad(w_in[:, QKV_WIDTH:], ((0, 0), (0, GATE_PAD - N_HEADS)))
    b_pad = jnp.pad(vecs["b_forget"], (0, GATE_PAD - N_HEADS)).reshape(1, GATE_PAD)

    h1, zd, zf, gate = _in_proj(x0, vecs["g_mix"], jnp.concatenate([w_qkv, w_gate], axis=1), "in_proj")
    mn = _rmsnorm(mem2, vecs["g_mem"], "norm_mem")
    c_bc, c_row, sg = _gate_fwd(gate, b_pad, n_batch, "gate_fwd")
    ya, lses = _dil_fwd(zd, n_batch, "dil_fwd")
    yf, of32, lse_f = _fox_fwd(zf, c_bc, c_row, n_batch, "fox_fwd")
    wts = late_weights(yf)
    w_out = wts["w_out"]
    kx = _matmul(mn, wts["w_xk"], "xk", out_dtypes=(BF16,))[0]
    vx = _matmul(mn, wts["w_xv"], "xv", out_dtypes=(BF16,))[0]
    x1, h2, qx, ox, x2, h3 = _out_xattn_fwd(ya, yf, x0, w_out, vecs["g_xattn"], wts["w_xq"], kx, vx, wts["w_xo"],
                                            vecs["g_mlp"], "out_xattn")
    u, a2 = _matmul(h3, wts["w_up"], "mlp_up", out_dtypes=(BF16, BF16), epilogue=_relu2)
    loss, dx3, dx3b, dg_final = _loss_bwd(a2, wts["w_down"], x2, vecs["g_final"], tgt, "mlp_down_loss")

    du = _matmul(dx3b, wts["w_down"], "mlp_down_bwd", out_dtypes=(BF16,), extras=(u,), epilogue=_relu2_bwd, w_t=True)[0]
    shards = (N_CHIPS, 2 * D_MODEL, D_MODEL)
    g_mlp = _matmul_tn(h3, du, "gw_up", packed=(shards, lambda i, j: (j, 0, 0), None))
    g_mlp = _matmul_tn(a2, dx3b, "gw_down", packed=(shards, lambda i, j: (i, 1, 0), g_mlp))
    gw_up = g_mlp[:, :D_MODEL].transpose(1, 0, 2).reshape(D_MODEL, D_FF)
    gw_down = g_mlp[:, D_MODEL:].reshape(D_FF, D_MODEL)
    on_grads, on_swapped = hooks or (None, None)
    token = on_grads("mlp", g_mlp) if hooks else None
    dx2, dx2b, dg_mlp = _matmul_rms_bwd([du], [wts["w_up"]], x2, vecs["g_mlp"], dx3, "mlp_up_bwd", after=token)

    gw_xo = _matmul_tn(ox, dx2b, "gw_xo")
    token = on_swapped("mlp", dx2b) if hooks else None
    dx1, dx1b, dg_xattn, dqx, dkx, dvx = _xattn_chain_bwd(dx2, dx2b, x1, vecs["g_xattn"], qx, kx, vx, wts["w_xo"],
                                                          wts["w_xq"], "xattn_chain_bwd", after=token)
    gw_xq = _matmul_tn(h2, dqx, "gw_xq")
    gw_xk = _matmul_tn(mn, dkx, "gw_xk")
    gw_xv = _matmul_tn(mn, dvx, "gw_xv")
    dmn = _matmul(dkx, wts["w_xk"], "xk_bwd", w_t=True)[0]
    dmn = _matmul_res(dvx, wts["w_xv"], dmn, "xv_bwd", w_t=True)
    _, _, dg_mem = _rms_bwd(mem2, dmn, vecs["g_mem"], None, "norm_mem_bwd")

    gw_out = jnp.concatenate([_matmul_tn(ya, dx1b, "gw_out_a"), _matmul_tn(yf, dx1b, "gw_out_f")], axis=0)
    token = on_grads("mid", dict(w_out=gw_out, w_xq=gw_xq, w_xk=gw_xk, w_xv=gw_xv, w_xo=gw_xo)) if hooks else None
    dy = _matmul(dx1b, w_out, "out_bwd", out_dtypes=(BF16,), w_t=True, after=token)[0]
    dz = _dil_bwd(zd, dy, ya, lses, n_batch, "dil_bwd")
    dz, dc = _fox_bwd(zf, of32, dy, lse_f, c_bc, c_row, dz, n_batch, "fox_bwd")
    dzg, db = _gate_bwd(dc.reshape(n_batch, N_HEADS, SEQ), sg, "gate_bwd")
    token = on_swapped("mid", dz) if hooks else None
    gw_pm = _matmul_tn(h1, dz, "gw_in_qkv", after=token)
    gw_in = jnp.concatenate([_pair_major_inv(gw_pm[:, :half]), _pair_major_inv(gw_pm[:, half:]),
                             _matmul_tn(h1, dzg, "gw_in_gate")[:, :N_HEADS]], axis=1)
    dx0, _, dg_mix = _matmul_rms_bwd([dz, dzg], [w_qkv, w_gate], x0, vecs["g_mix"], dx1, "in_bwd")

    gw = dict(w_in=gw_in, w_out=gw_out, w_xq=gw_xq, w_xk=gw_xk, w_xv=gw_xv, w_xo=gw_xo, w_up=gw_up, w_down=gw_down)
    gv = dict(g_mix=dg_mix, g_xattn=dg_xattn, g_mem=dg_mem, g_mlp=dg_mlp, g_final=dg_final, b_forget=db)
    return loss, dx0.reshape(x.shape), gw, gv


MESH = pl.DeviceIdType.MESH
ANY = pl.BlockSpec(memory_space=pl.ANY)


def _place():
    x, y, c = lax.axis_index("x"), lax.axis_index("y"), lax.axis_index("c")
    other_chips = [(1 - x, y), (x, 1 - y), (1 - x, 1 - y)]
    return x, y, c, other_chips


def _my_chip():
    return 2 * lax.axis_index("x") + lax.axis_index("y")


def _halves(rows, c, align):
    half = rows // 2
    assert rows % (2 * align) == 0, rows
    return pl.ds(pl.multiple_of(c * half, align), half), pl.ds(pl.multiple_of((1 - c) * half, align), half)


def _place_own(wall, pack):
    return lax.dynamic_update_slice(wall, pack[None], (_my_chip(), 0, 0))


HBM = pl.BlockSpec(memory_space=pltpu.HBM)
SEM = pl.BlockSpec(memory_space=pltpu.SEMAPHORE)
SPLIT_COPY = pltpu.CompilerParams(has_side_effects=pltpu.SideEffectType.DATAFLOW_SIDE_EFFECTING)


def _in_hbm(a):
    return pltpu.with_memory_space_constraint(a, pltpu.HBM)


def _start_call(start, src, land_shape, after, name):
    land = lax.empty(land_shape, src.dtype)

    def body(src_ref, land_ref, after_ref, send_sems, recv_sems, src_thru, land_thru, token):
        del after_ref, src_thru, land_thru
        start(src_ref, land_ref, send_sems, recv_sems)
        token[...] = jnp.zeros_like(token)

    return pl.pallas_call(
        body, name=name,
        out_shape=(pltpu.SemaphoreType.DMA((3,)), pltpu.SemaphoreType.DMA((3,)), pltpu.HBM(src.shape, src.dtype),
                   pltpu.HBM(land_shape, src.dtype), jax.ShapeDtypeStruct((8, LANES), F32)),
        in_specs=(HBM, HBM, ANY), out_specs=(SEM, SEM, HBM, HBM, pl.BlockSpec(memory_space=pltpu.VMEM)),
        input_output_aliases={0: 2, 1: 3}, compiler_params=SPLIT_COPY,
    )(_in_hbm(src), _in_hbm(land), after)


def _wait_call(body, started, after, name):
    send_sems, recv_sems, src, land, _ = started
    return pl.pallas_call(
        body, name=name,
        out_shape=(pltpu.HBM(src.shape, src.dtype), pltpu.HBM(land.shape, land.dtype)),
        in_specs=(HBM, HBM, SEM, SEM, ANY), out_specs=(HBM, HBM),
        input_output_aliases={0: 0, 1: 1}, compiler_params=SPLIT_COPY,
    )(src, land, send_sems, recv_sems, after)


def _gather_copies(p_ref, wall_ref, send_sems, recv_sems):
    x, y, c, chips = _place()
    me = 2 * x + y
    mine, _ = _halves(p_ref.shape[0], c, 16)
    out, back = [], []
    for k, chip in enumerate(chips):
        peer = dict(send_sem=send_sems.at[k], recv_sem=recv_sems.at[k], device_id=(chip[0], chip[1], c), device_id_type=MESH)
        out.append(pltpu.make_async_remote_copy(src_ref=p_ref.at[mine], dst_ref=wall_ref.at[me, mine], **peer))
        slab = wall_ref.at[2 * chip[0] + chip[1], mine]
        back.append(pltpu.make_async_remote_copy(src_ref=slab, dst_ref=slab, **peer))
    return out, back


def _gather_start(pack, after, name):
    def start(p_ref, wall_ref, send_sems, recv_sems):
        for cp in _gather_copies(p_ref, wall_ref, send_sems, recv_sems)[0]:
            cp.start()

    return _start_call(start, pack, (N_CHIPS,) + pack.shape, after, name)


def _gather_wait(started, after, name):
    def body(p_ref, wall_ref, send_sems, recv_sems, after_ref, p_dead, wall_out):
        del after_ref, p_dead, wall_out
        out, back = _gather_copies(p_ref, wall_ref, send_sems, recv_sems)
        for cp_out, cp_back in zip(out, back):
            cp_out.wait_send()
            cp_back.wait_recv()

    return _wait_call(body, started, after, name)


def _pass_on(wall, name):
    def body(w_in_ref, out_ref, send_sems, recv_sems):
        del w_in_ref
        x, y, c, chips = _place()
        mine, theirs = _halves(wall.shape[1], c, 16)
        sends = []
        for k, chip in enumerate(chips):
            slab = out_ref.at[2 * chip[0] + chip[1]]
            peer = dict(send_sem=send_sems.at[k], recv_sem=recv_sems.at[k], device_id=(x, y, 1 - c), device_id_type=MESH)
            cp = pltpu.make_async_remote_copy(src_ref=slab.at[mine], dst_ref=slab.at[mine], **peer)
            cp.start()
            sends.append((cp, pltpu.make_async_remote_copy(src_ref=slab.at[theirs], dst_ref=slab.at[theirs], **peer)))
        for cp, back in sends:
            back.wait_recv()
            cp.wait_send()

    return pl.pallas_call(
        body, in_specs=[ANY], out_specs=ANY, out_shape=jax.ShapeDtypeStruct(wall.shape, wall.dtype),
        scratch_shapes=[pltpu.SemaphoreType.DMA((3,))] * 2, input_output_aliases={0: 0}, name=name,
    )(wall)


def _swap_halves(g, name):
    half = g.shape[1] // 2

    def body(g_ref, out_ref, send_sem, recv_sem):
        x, y, c, _ = _place()
        _, theirs = _halves(g.shape[1], c, 8)
        cp = pltpu.make_async_remote_copy(src_ref=g_ref.at[:, theirs], dst_ref=out_ref, send_sem=send_sem, recv_sem=recv_sem,
                                          device_id=(x, y, 1 - c), device_id_type=MESH)
        cp.start()
        cp.wait()

    return pl.pallas_call(
        body, in_specs=[ANY], out_specs=ANY,
        out_shape=jax.ShapeDtypeStruct((N_CHIPS, half, g.shape[2]), F32),
        scratch_shapes=[pltpu.SemaphoreType.DMA, pltpu.SemaphoreType.DMA],
        name=name,
    )(g)


def _swap_copy(g_ref, land_ref, send_sems, recv_sems):
    x, y, c, _ = _place()
    _, theirs = _halves(g_ref.shape[1], c, 8)
    return pltpu.make_async_remote_copy(src_ref=g_ref.at[:, theirs], dst_ref=land_ref, send_sem=send_sems.at[0],
                                        recv_sem=recv_sems.at[0], device_id=(x, y, 1 - c), device_id_type=MESH)


def _swap_start(g, name):
    def start(g_ref, land_ref, send_sems, recv_sems):
        _swap_copy(g_ref, land_ref, send_sems, recv_sems).start()

    return _start_call(start, g, (N_CHIPS, g.shape[1] // 2, g.shape[2]), _core_index(), name)


def _swap_wait(started, after, name):
    def body(g_ref, land_ref, send_sems, recv_sems, after_ref, g_out, land_out):
        del after_ref, g_out, land_out
        cp = _swap_copy(g_ref, land_ref, send_sems, recv_sems)
        cp.wait_send()
        cp.wait_recv()

    return _wait_call(body, started, after, name)


def _core_index():
    return lax.axis_index("c").astype(jnp.int32).reshape(1)


def _row_tile(half):
    tile = max(t for t in range(16, 1025, 16) if half % t == 0)
    return tile, half // tile


def _add_sibling(g, got, name):
    half = g.shape[1] // 2
    tile, n_tiles = _row_tile(half)

    def body(c_ref, g_ref, got_ref, o_ref):
        o_ref[...] = (g_ref[...] + got_ref[...]).astype(BF16)

    width = g.shape[2]
    blk = pl.BlockSpec((None, tile, width), lambda s, i, c_ref: (s, i, 0))
    return pl.pallas_call(
        body,
        grid_spec=pltpu.PrefetchScalarGridSpec(
            num_scalar_prefetch=1, grid=(N_CHIPS, n_tiles),
            in_specs=[pl.BlockSpec((None, tile, width), lambda s, i, c_ref: (s, c_ref[0] * n_tiles + i, 0)), blk],
            out_specs=blk),
        out_shape=jax.ShapeDtypeStruct((N_CHIPS, half, width), BF16),
        name=name, compiler_params=_params("arbitrary", "arbitrary"),
    )(_core_index(), g, got)


def _exchange_copies(p_ref, land_ref, send_sems, recv_sems):
    x, y, c, chips = _place()
    me = 2 * x + y
    out, back = [], []
    for k, chip in enumerate(chips):
        peer = dict(send_sem=send_sems.at[k], recv_sem=recv_sems.at[k], device_id=(chip[0], chip[1], c), device_id_type=MESH)
        out.append(pltpu.make_async_remote_copy(src_ref=p_ref.at[2 * chip[0] + chip[1]], dst_ref=land_ref.at[me], **peer))
        slab = land_ref.at[2 * chip[0] + chip[1]]
        back.append(pltpu.make_async_remote_copy(src_ref=slab, dst_ref=slab, **peer))
    return out, back


def _with_own(got, part):
    me = _my_chip()
    return lax.dynamic_update_slice(got, lax.dynamic_slice(part, (me, 0, 0), (1,) + part.shape[1:]), (me, 0, 0))


def _exchange_start(part, name):
    def start(p_ref, land_ref, send_sems, recv_sems):
        for cp in _exchange_copies(p_ref, land_ref, send_sems, recv_sems)[0]:
            cp.start()

    return _start_call(start, part, part.shape, _core_index(), name)


def _exchange_wait(started, after, name):
    def body(p_ref, land_ref, send_sems, recv_sems, after_ref, p_dead, land_out):
        del after_ref, p_dead, land_out
        out, back = _exchange_copies(p_ref, land_ref, send_sems, recv_sems)
        for cp_out, cp_back in zip(out, back):
            cp_out.wait_send()
            cp_back.wait_recv()

    part, got = _wait_call(body, started, after, name)
    return _with_own(got, part)


def _sum_chips(parts, name):
    half, width = parts.shape[1:]
    tile, n_tiles = _row_tile(half)

    def body(c_ref, p0, p1, p2, p3, o_ref):
        f32 = lambda p: p[...].astype(F32)
        o_ref[...] = ((f32(p0) + f32(p1)) + f32(p2)) + f32(p3)

    def slab(s):
        return pl.BlockSpec((None, tile, width), lambda i, c_ref, s=s: (s, i, 0))

    return pl.pallas_call(
        body,
        grid_spec=pltpu.PrefetchScalarGridSpec(
            num_scalar_prefetch=1, grid=(n_tiles,),
            in_specs=[slab(s) for s in range(N_CHIPS)],
            out_specs=pl.BlockSpec((None, tile, width), lambda i, c_ref: (c_ref[0], i, 0))),
        out_shape=jax.ShapeDtypeStruct((2, half, width), F32),
        name=name, compiler_params=_params("arbitrary"),
    )(_core_index(), parts, parts, parts, parts)


def _share_halves(halves, name):
    def body(h_ref, out_ref, send_sem, recv_sem):
        del h_ref
        x, y, c, _ = _place()
        cp = pltpu.make_async_remote_copy(src_ref=out_ref.at[c], dst_ref=out_ref.at[c], send_sem=send_sem, recv_sem=recv_sem,
                                          device_id=(x, y, 1 - c), device_id_type=MESH)
        cp.start()
        pltpu.make_async_remote_copy(src_ref=out_ref.at[1 - c], dst_ref=out_ref.at[1 - c], send_sem=send_sem, recv_sem=recv_sem,
                                     device_id=(x, y, 1 - c), device_id_type=MESH).wait_recv()
        cp.wait_send()

    return pl.pallas_call(
        body, in_specs=[ANY], out_specs=ANY,
        out_shape=jax.ShapeDtypeStruct(halves.shape, halves.dtype),
        scratch_shapes=[pltpu.SemaphoreType.DMA] * 2,
        input_output_aliases={0: 0},
        name=name,
    )(halves)


def _reduce_parts(g, tag):
    return _add_sibling(g, _swap_halves(g, "swap_" + tag), "add_" + tag)


def _reduce_finish(got, tag):
    halves = _share_halves(_sum_chips(got, "sum_" + tag), "share_" + tag)
    return halves.reshape(2 * halves.shape[1], halves.shape[2])


SMALL_ROWS = 8


def _allreduce_small(v):
    def body(v_ref, out_ref, buf, send_sems, recv_sems):
        x, y, c, _ = _place()
        buf[4 * x + 2 * y + c] = v_ref[...]
        sends = []
        for k in range(1, N_DEV):
            px = 1 - x if k & 4 else x
            py = 1 - y if k & 2 else y
            pc = 1 - c if k & 1 else c
            cp = pltpu.make_async_remote_copy(src_ref=v_ref, dst_ref=buf.at[4 * x + 2 * y + c], send_sem=send_sems.at[k - 1],
                                              recv_sem=recv_sems.at[k - 1], device_id=(px, py, pc), device_id_type=MESH)
            cp.start()
            sends.append((cp, 4 * px + 2 * py + pc))
        for k, (cp, peer) in enumerate(sends):
            pltpu.make_async_remote_copy(src_ref=v_ref, dst_ref=buf.at[peer], send_sem=send_sems.at[k], recv_sem=recv_sems.at[k],
                                         device_id=(x, y, c), device_id_type=MESH).wait_recv()
        for cp, _ in sends:
            cp.wait_send()
        total = buf[0]
        for d in range(1, N_DEV):
            total = total + buf[d]
        out_ref[...] = total

    vmem = pl.BlockSpec(memory_space=pltpu.VMEM)
    return pl.pallas_call(
        body, in_specs=[vmem], out_specs=vmem,
        out_shape=jax.ShapeDtypeStruct(v.shape, v.dtype),
        scratch_shapes=[pltpu.VMEM((N_DEV,) + v.shape, v.dtype), pltpu.SemaphoreType.DMA((N_DEV - 1,)),
                        pltpu.SemaphoreType.DMA((N_DEV - 1,))],
        name="allreduce_small",
    )(v)


MATRICES = ("w_in", "w_out", "w_xq", "w_xk", "w_xv", "w_xo", "w_up", "w_down")
VECTORS = ("g_mix", "g_xattn", "g_mem", "g_mlp", "g_final", "b_forget")
WEIGHT_ORDER = ("g_mix", "w_in", "b_forget", "w_out", "g_xattn", "g_mem", "w_xq", "w_xk", "w_xv", "w_xo",
                "g_mlp", "w_up", "w_down", "g_final")
GROUPS = {"mlp": ("w_up", "w_down"), "mid": ("w_out", "w_xq", "w_xk", "w_xv", "w_xo"), "in": ("w_in",)}
LATE = GROUPS["mid"] + GROUPS["mlp"]
W_IN_SHARD = IN_WIDTH // N_CHIPS
SHARD_ROWS = {"w_out": 256, "w_xq": 256, "w_xk": 256, "w_xv": 256, "w_xo": 256, "w_up": 1024, "w_down": 1024}
PACK_ROWS = SHARD_ROWS
W_IN_PAD = -(-W_IN_SHARD // LANES) * LANES
ADAM_ROWS = 128


def _pack(parts, names):
    return jnp.concatenate([jnp.pad(parts[n], ((0, PACK_ROWS[n] - SHARD_ROWS[n]), (0, 0))) for n in names], axis=0)


def _unpack(a, names):
    out, pos = {}, 0
    for n in names:
        out[n] = a[..., pos:pos + SHARD_ROWS[n], :]
        pos += PACK_ROWS[n]
    return out


def _full_weights(wall, names):
    cols = lambda a: a.transpose(1, 0, 2).reshape(a.shape[1], -1)
    rows = lambda a: a.reshape(-1, a.shape[-1])
    if names == GROUPS["in"]:
        return {"w_in": cols(wall[:, :, :W_IN_SHARD])}
    return {n: cols(a) if n == "w_up" else rows(a) for n, a in _unpack(wall, names).items()}


def _shard_of(g, name, s):
    if name == "w_up":
        return g[:, s * D_MODEL:(s + 1) * D_MODEL]
    n = SHARD_ROWS[name]
    return g[s * n:(s + 1) * n]


def _pad_w_in(a):
    return jnp.pad(a, [(0, 0)] * (a.ndim - 1) + [(0, W_IN_PAD - W_IN_SHARD)])


def _pack_grads(gws, names):
    if names == GROUPS["in"]:
        return _pad_w_in(gws["w_in"].reshape(D_MODEL, N_CHIPS, W_IN_SHARD).transpose(1, 0, 2))
    return jnp.stack([_pack({n: _shard_of(gws[n], n, s) for n in names}, names) for s in range(N_CHIPS)])


def kernel(x, mem, g_mix, w_in, b_forget, w_out, g_xattn, g_mem, w_xq, w_xk, w_xv, w_xo, g_mlp, w_up, w_down, g_final, loss_target, m_g_mix, m_w_in, m_b_forget, m_w_out, m_g_xattn, m_g_mem, m_w_xq, m_w_xk, m_w_xv, m_w_xo, m_g_mlp, m_w_up, m_w_down, m_g_final, v_g_mix, v_w_in, v_b_forget, v_w_out, v_g_xattn, v_g_mem, v_w_xq, v_w_xk, v_w_xv, v_w_xo, v_g_mlp, v_w_up, v_w_down, v_g_final):
    given = dict(locals())
    weights = {n: given[n] for n in WEIGHT_ORDER}
    vecs = {n: weights[n] for n in VECTORS}

    shard = {n: weights[n].astype(BF16) for n in MATRICES}
    in_started = _gather_start(_pad_w_in(shard["w_in"]), _core_index(), "gather_in_start")
    late_pack = _pack(shard, LATE)
    in_pack, in_wall = _gather_wait(in_started, late_pack, "gather_in_wait")
    in_wall = _place_own(_pass_on(in_wall, "gather_in_pass"), in_pack)
    late = _gather_start(late_pack, in_wall, "gather_late_start")
    w_in_full = _full_weights(in_wall, GROUPS["in"])["w_in"]

    def late_weights(after):
        pack, wall = _gather_wait(late, after, "gather_late_wait")
        return _full_weights(_place_own(_pass_on(wall, "gather_late_pass"), pack), LATE)

    started = {}

    swapping = {}

    def on_grads(group, gws):
        packed = gws if group == "mlp" else _pack_grads(gws, GROUPS[group])
        swapping[group] = _swap_start(packed, "swap_%s_start" % group)
        return swapping[group][4]

    def on_swapped(group, after):
        g, got = _swap_wait(swapping[group], after, "swap_%s_wait" % group)
        started[group] = _exchange_start(_add_sibling(g, got, "add_" + group), "exchange_%s_start" % group)
        return started[group][4]

    loss, grad_x, gw, gv = _local_step(x, mem, loss_target, vecs, w_in_full, late_weights, (on_grads, on_swapped))

    part = _reduce_parts(_pack_grads(gw, GROUPS["in"]), "in")
    started["in"] = _exchange_start(part, "exchange_in_start")
    grads, delta, new_m, new_v = {}, {}, {}, {}

    def finish(group, after):
        got = _exchange_wait(started[group], after, "exchange_%s_wait" % group)
        done = _reduce_finish(got, group)
        for n, a in ({"w_in": done[:, :W_IN_SHARD]} if group == "in" else _unpack(done, GROUPS[group])).items():
            grads[n] = a.reshape(weights[n].shape)
            delta[n], new_m[n], new_v[n] = _adamw(weights[n], grads[n], given["m_" + n], given["v_" + n], "adamw_" + n, ADAM_ROWS)
        return new_v[GROUPS[group][-1]]

    after = finish("mlp", started["in"][4])
    after = finish("mid", after)

    row = lambda a: jnp.pad(a.reshape(-1), (0, D_MODEL - a.size)).reshape(1, D_MODEL)
    small = jnp.concatenate([gv[n] for n in VECTORS[:5]] + [row(gv["b_forget"][:, 0]), row(loss[0, :1]),
                             jnp.zeros((1, D_MODEL), F32)], axis=0)
    small = _allreduce_small(small)
    for k, n in enumerate(VECTORS[:5]):
        grads[n] = small[k]
    grads["b_forget"] = small[5, :N_HEADS]
    loss_total = small[6, 0]
    finish("in", after)

    stack = lambda prefix: jnp.concatenate([row(given[prefix + n]) for n in VECTORS] + [jnp.zeros((2, D_MODEL), F32)], axis=0)
    g_small = jnp.concatenate([small[:6], jnp.zeros((2, D_MODEL), F32)], axis=0)
    d, m1, v1 = _adamw(stack(""), g_small, stack("m_"), stack("v_"), "adamw_vectors", SMALL_ROWS)
    for k, n in enumerate(VECTORS):
        width = weights[n].shape[0]
        delta[n], new_m[n], new_v[n] = d[k, :width], m1[k, :width], v1[k, :width]

    return (loss_total, grad_x, *[grads[n] for n in WEIGHT_ORDER], *[delta[n] for n in WEIGHT_ORDER],
            *[new_m[n] for n in WEIGHT_ORDER], *[new_v[n] for n in WEIGHT_ORDER])
```

```python
import functools
import math

import jax
import jax.numpy as jnp
from jax import lax
from jax.experimental import pallas as pl
from jax.experimental.pallas import tpu as pltpu

F32 = jnp.float32
BF16 = jnp.bfloat16

D_MODEL = 1024
SEQ = 2048
N_MEM = 256
HEAD_DIM = 64
N_HEADS = 8
MIX_HALF = N_HEADS * HEAD_DIM
QKV_WIDTH = 6 * MIX_HALF
IN_WIDTH = QKV_WIDTH + N_HEADS
GATE_PAD = 128
BLOCK = 128
DILATIONS = (1, 4, 16)
X_HEADS = 4
X_HEAD_DIM = 256
D_FF = 4096
EPS = 1e-6
NEG = -1e30
ATT_SCALE = 1.0 / math.sqrt(HEAD_DIM)
X_SCALE = 1.0 / math.sqrt(X_HEAD_DIM)
LANES = 128
N_CHIPS = 4
N_DEV = 8

ADAM_LR = 0.001
ADAM_B1 = 0.9
ADAM_B2 = 0.999
ADAM_EPS = 1e-08
ADAM_WD = 0.01
ADAM_STEP = 10

VMEM_LIMIT = 48 * 1024 * 1024


def _params(*sem):
    return pltpu.CompilerParams(dimension_semantics=sem or None, vmem_limit_bytes=VMEM_LIMIT)


def _dot(a, b):
    return jnp.dot(a, b, preferred_element_type=F32)


def _dot_nt(a, b):
    return lax.dot_general(a, b, (((1,), (1,)), ((), ())), preferred_element_type=F32)


def _dot_tn(a, b):
    return lax.dot_general(a, b, (((0,), (0,)), ((), ())), preferred_element_type=F32)


def _dot_exact(x, e):
    hi = x.astype(BF16)
    r1 = x - hi.astype(F32)
    mid = r1.astype(BF16)
    lo = (r1 - mid.astype(F32)).astype(BF16)
    return _dot(hi, e) + _dot(mid, e) + _dot(lo, e)


def _head_mask(e):
    lane = lax.broadcasted_iota(jnp.int32, (1, LANES), 1)
    return (lane >= HEAD_DIM * e) & (lane < HEAD_DIM * (e + 1))


def _matmul(a, w, name, out_dtypes=(F32,), extras=(), epilogue=None, tm=1024, tn=1024, w_t=False, after=None):
    m, k = a.shape
    n = w.shape[0] if w_t else w.shape[1]
    tm, tn = min(tm, m), min(tn, n)
    assert m % tm == 0 and n % tn == 0, (name, a.shape, w.shape)
    n_ex = len(extras)
    order = () if after is None else (after,)

    def body(a_ref, w_ref, *rest):
        rest = rest[len(order):]
        acc = (_dot_nt if w_t else _dot)(a_ref[...], w_ref[...])
        res = (acc,) if epilogue is None else epilogue(acc, *[r[...] for r in rest[:n_ex]])
        for o_ref, r in zip(rest[n_ex:], res):
            o_ref[...] = r.astype(o_ref.dtype)

    tile = pl.BlockSpec((tm, tn), lambda i, j: (i, j))
    w_spec = pl.BlockSpec((tn, k), lambda i, j: (j, 0)) if w_t else pl.BlockSpec((k, tn), lambda i, j: (0, j))
    return pl.pallas_call(
        body, grid=(m // tm, n // tn),
        in_specs=[pl.BlockSpec((tm, k), lambda i, j: (i, 0)), w_spec] + [pl.BlockSpec(memory_space=pl.ANY)] * len(order) + [tile] * n_ex,
        out_specs=[tile] * len(out_dtypes),
        out_shape=[jax.ShapeDtypeStruct((m, n), dt) for dt in out_dtypes],
        name=name, compiler_params=_params("parallel", "arbitrary"),
    )(a, w, *order, *extras)


def _matmul_res(a, w, res, name, w_t=False):
    return _matmul(a, w, name, extras=(res,), epilogue=lambda acc, r: (r + acc,), w_t=w_t)[0]


def _matmul_tn(x, y, name, tm=1024, tn=1024, tk=2048, packed=None, after=None):
    t, m = x.shape
    _, n = y.shape
    tm, tn, tk = min(tm, m), min(tn, n), min(tk, t)
    assert m % tm == 0 and n % tn == 0 and t % tk == 0, (name, x.shape, y.shape)
    shape, place, into = packed or ((m, n), None, None)

    def body(x_ref, y_ref, *rest):
        o_ref = rest[-1]

        @pl.when(pl.program_id(2) == 0)
        def _():
            o_ref[...] = jnp.zeros_like(o_ref)

        o_ref[...] += _dot_tn(x_ref[...], y_ref[...])

    out_spec = (pl.BlockSpec((tm, tn), lambda i, j, k: (i, j)) if place is None
                else pl.BlockSpec((None, tm, tn), lambda i, j, k: place(i, j)))
    return pl.pallas_call(
        body, grid=(m // tm, n // tn, t // tk),
        in_specs=[pl.BlockSpec((tk, tm), lambda i, j, k: (k, i)), pl.BlockSpec((tk, tn), lambda i, j, k: (k, j))]
        + [pl.BlockSpec(memory_space=pl.ANY)] * ((into is not None) + (after is not None)),
        out_specs=out_spec, out_shape=jax.ShapeDtypeStruct(shape, F32),
        input_output_aliases={} if into is None else {2: 0},
        name=name, compiler_params=_params("parallel", "parallel", "arbitrary"),
    )(x, y, *(() if into is None else (into,)), *(() if after is None else (after,)))


def _rmsnorm(x, g, name, tm=512):
    t, d = x.shape
    tm = min(tm, t)

    def body(x_ref, g_ref, h_ref):
        xv = x_ref[...]
        r = lax.rsqrt(jnp.mean(xv * xv, axis=-1, keepdims=True) + EPS)
        h_ref[...] = (xv * r * g_ref[...]).astype(BF16)

    return pl.pallas_call(
        body, grid=(t // tm,),
        in_specs=[pl.BlockSpec((tm, d), lambda i: (i, 0)), pl.BlockSpec((1, d), lambda i: (0, 0))],
        out_specs=pl.BlockSpec((tm, d), lambda i: (i, 0)),
        out_shape=jax.ShapeDtypeStruct((t, d), BF16),
        name=name, compiler_params=_params("arbitrary"),
    )(x, g.reshape(1, d))


def _in_proj(x, g, w_all, name, tm=512):
    t, d = x.shape
    half = 3 * MIX_HALF

    def body(x_ref, g_ref, w_ref, h_ref, zd_ref, zf_ref, gate_ref):
        xv = x_ref[...]
        r = lax.rsqrt(jnp.mean(xv * xv, axis=-1, keepdims=True) + EPS)
        h = (xv * r * g_ref[...]).astype(BF16)
        h_ref[...] = h
        zd_ref[...] = _dot(h, w_ref[:, 0:half])
        zf_ref[...] = _dot(h, w_ref[:, half:2 * half]).astype(BF16)
        gate_ref[...] = _dot(h, w_ref[:, 2 * half:])

    row = lambda width: pl.BlockSpec((tm, width), lambda i: (i, 0))
    return pl.pallas_call(
        body, grid=(t // tm,),
        in_specs=[row(d), pl.BlockSpec((1, d), lambda i: (0, 0)), pl.BlockSpec(w_all.shape, lambda i: (0, 0))],
        out_specs=[row(d), row(half), row(half), row(GATE_PAD)],
        out_shape=[jax.ShapeDtypeStruct((t, d), BF16), jax.ShapeDtypeStruct((t, half), F32),
                   jax.ShapeDtypeStruct((t, half), BF16), jax.ShapeDtypeStruct((t, GATE_PAD), F32)],
        name=name, compiler_params=_params("arbitrary"),
    )(x, g.reshape(1, d), w_all)


def _rms_bwd_tile(xv, dh, g):
    d = xv.shape[-1]
    r = lax.rsqrt(jnp.mean(xv * xv, axis=-1, keepdims=True) + EPS)
    dyg = dh * g
    proj = jnp.sum(dyg * xv, axis=-1, keepdims=True)
    dx = r * dyg - xv * (r * r * r * (1.0 / d)) * proj
    return dx, dh * (xv * r)


def _rms_bwd(x, dh, g, dres, name, tm=512):
    t, d = x.shape
    tm = min(tm, t)
    has_res = dres is not None

    def body(x_ref, dh_ref, g_ref, *rest):
        if has_res:
            res_ref, dx_ref, dxb_ref, dg_ref = rest
        else:
            dx_ref, dxb_ref, dg_ref = rest
        dx, dg_rows = _rms_bwd_tile(x_ref[...], dh_ref[...], g_ref[...])
        if has_res:
            dx = res_ref[...] + dx
        dx_ref[...] = dx
        dxb_ref[...] = dx.astype(BF16)

        @pl.when(pl.program_id(0) == 0)
        def _():
            dg_ref[...] = jnp.zeros_like(dg_ref)

        dg_ref[...] += jnp.sum(dg_rows, axis=0, keepdims=True)

    row = pl.BlockSpec((tm, d), lambda i: (i, 0))
    vec = pl.BlockSpec((1, d), lambda i: (0, 0))
    return pl.pallas_call(
        body, grid=(t // tm,),
        in_specs=[row, row, vec] + ([row] if has_res else []),
        out_specs=[row, row, vec],
        out_shape=[jax.ShapeDtypeStruct((t, d), F32), jax.ShapeDtypeStruct((t, d), BF16), jax.ShapeDtypeStruct((1, d), F32)],
        name=name, compiler_params=_params("arbitrary"),
    )(x, dh, g.reshape(1, d), *((dres,) if has_res else ()))


def _row_dots(a_refs, w_refs, w_t):
    acc = None
    for a_ref, w_ref in zip(a_refs, w_refs):
        part = (_dot_nt if w_t else _dot)(a_ref[...], w_ref[...])
        acc = part if acc is None else acc + part
    return acc


def _row_specs(a_parts, w_parts, tm):
    specs = [pl.BlockSpec((tm, a.shape[1]), lambda i: (i, 0)) for a in a_parts]
    return specs + [pl.BlockSpec(w.shape, lambda i: (0, 0)) for w in w_parts]


def _matmul_rms_bwd(a_parts, w_parts, x, g, dres, name, tm=512, after=None):
    t, d = x.shape
    n = len(a_parts)
    order = () if after is None else (after,)

    def body(*refs):
        x_ref, g_ref, res_ref = refs[2 * n:2 * n + 3]
        dx_ref, dxb_ref, dg_ref = refs[2 * n + 3 + len(order):]
        dx, dg_rows = _rms_bwd_tile(x_ref[...], _row_dots(refs[:n], refs[n:2 * n], True), g_ref[...])
        dx = res_ref[...] + dx
        dx_ref[...] = dx
        dxb_ref[...] = dx.astype(BF16)

        @pl.when(pl.program_id(0) == 0)
        def _():
            dg_ref[...] = jnp.zeros_like(dg_ref)

        dg_ref[...] += jnp.sum(dg_rows, axis=0, keepdims=True)

    row = pl.BlockSpec((tm, d), lambda i: (i, 0))
    vec = pl.BlockSpec((1, d), lambda i: (0, 0))
    return pl.pallas_call(
        body, grid=(t // tm,),
        in_specs=_row_specs(a_parts, w_parts, tm) + [row, vec, row] + [pl.BlockSpec(memory_space=pl.ANY)] * len(order),
        out_specs=[row, row, vec],
        out_shape=[jax.ShapeDtypeStruct((t, d), F32), jax.ShapeDtypeStruct((t, d), BF16), jax.ShapeDtypeStruct((1, d), F32)],
        name=name, compiler_params=_params("arbitrary"),
    )(*a_parts, *w_parts, x, g.reshape(1, d), dres, *order)


def _loss_bwd(a, w, res, g, target, name, tm=512):
    t, d = res.shape

    def body(a_ref, w_ref, x_ref, g_ref, t_ref, loss_ref, dx_ref, dxb_ref, dg_ref):
        xv = x_ref[...] + _dot(a_ref[...], w_ref[...])
        gv = g_ref[...]
        r = lax.rsqrt(jnp.mean(xv * xv, axis=-1, keepdims=True) + EPS)
        err = xv * r * gv - t_ref[...]
        dx, dg_rows = _rms_bwd_tile(xv, err * (1.0 / d), gv)
        dx_ref[...] = dx
        dxb_ref[...] = dx.astype(BF16)

        @pl.when(pl.program_id(0) == 0)
        def _():
            dg_ref[...] = jnp.zeros_like(dg_ref)
            loss_ref[...] = jnp.zeros_like(loss_ref)

        dg_ref[...] += jnp.sum(dg_rows, axis=0, keepdims=True)
        part = jnp.sum(jnp.sum(err * err, axis=0, keepdims=True), axis=1, keepdims=True) * (0.5 / d)
        loss_ref[...] += jnp.broadcast_to(part, loss_ref.shape)

    row = pl.BlockSpec((tm, d), lambda i: (i, 0))
    vec = pl.BlockSpec((1, d), lambda i: (0, 0))
    return pl.pallas_call(
        body, grid=(t // tm,),
        in_specs=_row_specs([a], [w], tm) + [row, vec, row],
        out_specs=[pl.BlockSpec((1, LANES), lambda i: (0, 0)), row, row, vec],
        out_shape=[jax.ShapeDtypeStruct((1, LANES), F32), jax.ShapeDtypeStruct((t, d), F32),
                   jax.ShapeDtypeStruct((t, d), BF16), jax.ShapeDtypeStruct((1, d), F32)],
        name=name, compiler_params=_params("arbitrary"),
    )(a, w, res, g.reshape(1, d), target)


def _tri(upper):
    r = lax.broadcasted_iota(jnp.int32, (LANES, LANES), 0)
    c = lax.broadcasted_iota(jnp.int32, (LANES, LANES), 1)
    return jnp.where((r <= c) if upper else (r >= c), 1.0, 0.0).astype(BF16)


def _gate_fwd(gate, b_pad, n_batch, name):
    s = SEQ
    nblk = s // LANES

    def body(g_ref, b_ref, crow_ref, sg_ref):
        gz = g_ref[...] + b_ref[...]
        logf = jnp.minimum(gz, 0.0) - jnp.log(1.0 + jnp.exp(-jnp.abs(gz)))
        logf_t = logf.T
        sg_ref[...] = (1.0 / (1.0 + jnp.exp(gz))).T[0:N_HEADS]
        upper = _tri(True)
        carry = jnp.zeros((N_HEADS, 1), F32)
        for blk in range(nblk):
            seg = _dot_exact(logf_t[0:N_HEADS, blk * LANES:(blk + 1) * LANES], upper) + carry
            carry = seg[:, LANES - 1:LANES]
            crow_ref[:, blk * LANES:(blk + 1) * LANES] = seg

    return pl.pallas_call(
        body, grid=(n_batch,),
        in_specs=[pl.BlockSpec((s, GATE_PAD), lambda b: (b, 0)), pl.BlockSpec((1, GATE_PAD), lambda b: (0, 0))],
        out_specs=[pl.BlockSpec((None, N_HEADS, s), lambda b: (b, 0, 0)),
                   pl.BlockSpec((None, N_HEADS, s), lambda b: (b, 0, 0))],
        out_shape=[jax.ShapeDtypeStruct((n_batch, N_HEADS, s), F32),
                   jax.ShapeDtypeStruct((n_batch, N_HEADS, s), F32)],
        name=name, compiler_params=_params("arbitrary"),
    )(gate, b_pad)


def _gate_bwd(dc, sg, name):
    n_batch, _, s = dc.shape
    nblk = s // LANES

    def body(dc_ref, sg_ref, dz_ref, db_ref, dt_ref):
        lower = _tri(False)
        dcv = dc_ref[...]
        carry = jnp.zeros((N_HEADS, 1), F32)
        dt_ref[...] = jnp.zeros_like(dt_ref)
        for blk in reversed(range(nblk)):
            seg = _dot_exact(dcv[:, blk * LANES:(blk + 1) * LANES], lower) + carry
            carry = seg[:, 0:1]
            dt_ref[0:N_HEADS, blk * LANES:(blk + 1) * LANES] = seg * sg_ref[:, blk * LANES:(blk + 1) * LANES]
        dg_t = dt_ref[...]
        dz_ref[...] = dg_t.T.astype(BF16)

        @pl.when(pl.program_id(0) == 0)
        def _():
            db_ref[...] = jnp.zeros_like(db_ref)

        db_ref[...] += jnp.broadcast_to(jnp.sum(dg_t[0:N_HEADS], axis=1, keepdims=True), db_ref.shape)

    return pl.pallas_call(
        body, grid=(n_batch,),
        in_specs=[pl.BlockSpec((None, N_HEADS, s), lambda b: (b, 0, 0)), pl.BlockSpec((None, N_HEADS, s), lambda b: (b, 0, 0))],
        out_specs=[pl.BlockSpec((s, GATE_PAD), lambda b: (b, 0)), pl.BlockSpec((N_HEADS, LANES), lambda b: (0, 0))],
        out_shape=[jax.ShapeDtypeStruct((n_batch * s, GATE_PAD), BF16), jax.ShapeDtypeStruct((N_HEADS, LANES), F32)],
        scratch_shapes=[pltpu.VMEM((LANES, s), F32)],
        name=name, compiler_params=_params("arbitrary"),
    )(dc, sg)


FOX_BQ = 512
FOX_BK = 512
FOX_STRIP = 512
PAIR_WIDTH = 3 * LANES
N_PAIRS = N_HEADS // 2


def _pair_major(w):
    return w.reshape(w.shape[0], 3, N_PAIRS, LANES).transpose(0, 2, 1, 3).reshape(w.shape[0], 3 * MIX_HALF)


def _pair_major_inv(w):
    return w.reshape(w.shape[0], N_PAIRS, 3, LANES).transpose(0, 2, 1, 3).reshape(w.shape[0], 3 * MIX_HALF)


def _causal(i, j, bq, bk):
    qpos = i * bq + lax.broadcasted_iota(jnp.int32, (bq, 1), 0)
    kpos = j * bk + lax.broadcasted_iota(jnp.int32, (1, bk), 1)
    return kpos <= qpos


def _split_bf16(p):
    hi = p.astype(BF16)
    return hi, (p - hi.astype(F32)).astype(BF16)


def _fox_fwd(zf, c_row, n_batch, name):
    s, bq, bk = SEQ, FOX_BQ, FOX_BK
    nq = s // bq
    t = n_batch * s

    n_strip = bq // FOX_STRIP

    def body(q_ref, k_ref, v_ref, cr_ref, o_ref, o32_ref, lse_ref):
        hp = pl.program_id(1)
        strips = [slice(r * FOX_STRIP, (r + 1) * FOX_STRIP) for r in range(n_strip)]
        chains = [(e, r) for e in range(2) for r in range(n_strip)]
        qh = {}
        for e, r in chains:
            q = q_ref[strips[r], :] * ATT_SCALE
            qh[e, r] = jnp.where(_head_mask(e), q, jnp.zeros_like(q))

        def step(i, j, carry, masked):
            rows = pl.ds(j * bk, bk)
            kj, vj = k_ref[rows, :], v_ref[rows, :]
            ck = [cr_ref[pl.ds(2 * hp + e, 1), rows] for e in range(2)]
            out = []
            scores = [_dot_nt(qh[e, r], kj) for e, r in chains]
            for n, (e, r) in enumerate(chains):
                m, l, acc = carry[3 * n:3 * n + 3]
                sc = scores[n] - ck[e]
                if masked:
                    qpos = i * bq + r * FOX_STRIP + lax.broadcasted_iota(jnp.int32, (FOX_STRIP, 1), 0)
                    kpos = j * bk + lax.broadcasted_iota(jnp.int32, (1, bk), 1)
                    sc = jnp.where(kpos <= qpos, sc, NEG)
                m_new = jnp.maximum(m, jnp.max(sc, axis=1, keepdims=True))
                alpha = jnp.exp(m - m_new)
                p = jnp.exp(sc - m_new)
                p_hi, p_lo = _split_bf16(p)
                out += [m_new, alpha * l + jnp.sum(p, axis=1, keepdims=True), alpha * acc + (_dot(p_hi, vj) + _dot(p_lo, vj))]
            return tuple(out)

        def run(i):
            carry = (jnp.full((FOX_STRIP, 1), NEG, F32), jnp.zeros((FOX_STRIP, 1), F32), jnp.zeros((FOX_STRIP, LANES), F32)) * len(chains)
            n_clear = (i * bq) // bk
            for j in range((i * bq + bq + bk - 1) // bk):
                carry = step(i, j, carry, masked=j >= n_clear)
            for r in range(n_strip):
                outs = [carry[3 * (e * n_strip + r) + 2] / carry[3 * (e * n_strip + r) + 1] for e in range(2)]
                lses = [carry[3 * (e * n_strip + r)] + jnp.log(carry[3 * (e * n_strip + r) + 1]) for e in range(2)]
                o = jnp.where(_head_mask(0), outs[0], outs[1])
                o_ref[strips[r], :] = o.astype(BF16)
                o32_ref[strips[r], :] = o
                lse_ref[strips[r], :] = jnp.where(_head_mask(0), lses[0], lses[1])

        for k in range(nq):
            pl.when(pl.program_id(2) == k)(functools.partial(run, k))

    def col(c0):
        return lambda b, hp, i: (b, 3 * hp + c0)

    blk = pl.BlockSpec((bq, LANES), lambda b, hp, i: (b * nq + i, hp))
    return pl.pallas_call(
        body, grid=(n_batch, N_PAIRS, nq),
        in_specs=[pl.BlockSpec((bq, LANES), lambda b, hp, i: (b * nq + i, 3 * hp)),
                  pl.BlockSpec((s, LANES), col(1)), pl.BlockSpec((s, LANES), col(2)),
                  pl.BlockSpec((None, N_HEADS, s), lambda b, hp, i: (b, 0, 0))],
        out_specs=[blk, blk, blk],
        out_shape=[jax.ShapeDtypeStruct((t, MIX_HALF), BF16), jax.ShapeDtypeStruct((t, MIX_HALF), F32),
                   jax.ShapeDtypeStruct((t, MIX_HALF), F32)],
        name=name, compiler_params=_params("parallel", "parallel", "arbitrary"),
    )(zf, zf, zf, c_row)


def _fox_bwd(zf, o32, dy, lse, c_row, dz, n_batch, name):
    s, bq, bk = SEQ, FOX_BQ, FOX_BK
    nq, nk = s // bq, s // bk

    def body(q_ref, k_ref, v_ref, o_ref, do_ref, lse_ref, cr_ref, dz_in, dz_ref, dc_ref, dq_acc):
        del dz_in
        hp = pl.program_id(1)

        @pl.when(pl.program_id(2) == 0)
        def _():
            dq_acc[...] = jnp.zeros_like(dq_acc)

        kj, vj = k_ref[...], v_ref[...]
        km = [jnp.where(_head_mask(e), kj, jnp.zeros_like(kj)) for e in range(2)]

        def step(i, j, ck, carry, masked):
            rows = pl.ds(i * bq, bq)
            qi, doi = q_ref[rows, :] * ATT_SCALE, do_ref[rows, :]
            prod = doi.astype(F32) * o_ref[rows, :]
            out = []
            dq = jnp.zeros((bq, LANES), F32)
            for e in range(2):
                dk_a, dv_a, dc_a = carry[3 * e:3 * e + 3]
                mask = _head_mask(e)
                lane0 = HEAD_DIM * e
                dom = jnp.where(mask, doi, jnp.zeros_like(doi))
                delta = jnp.sum(jnp.where(mask, prod, 0.0), axis=1, keepdims=True)
                sc = _dot_nt(qi, km[e]) - ck[e]
                if masked:
                    sc = jnp.where(_causal(i, j, bq, bk), sc, NEG)
                p = jnp.exp(sc - lse_ref[rows, lane0:lane0 + 1])
                ds = p * (_dot_nt(dom, vj) - delta)
                dsb = ds.astype(BF16)
                dq = dq + _dot(dsb, km[e])
                out += [dk_a + _dot_tn(dsb, qi), dv_a + _dot_tn(p.astype(BF16), dom), dc_a - jnp.sum(ds, axis=0, keepdims=True)]
            dq_acc[rows, :] += dq * ATT_SCALE
            return tuple(out)

        def run(j):
            cols = pl.ds(j * bk, bk)
            ck = [cr_ref[pl.ds(2 * hp + e, 1), cols] for e in range(2)]
            carry = (jnp.zeros((bk, LANES), F32), jnp.zeros((bk, LANES), F32), jnp.zeros((1, bk), F32)) * 2
            n_diag = (j * bk + bk + bq - 1) // bq
            for i in range((j * bk) // bq, nq):
                carry = step(i, j, ck, carry, masked=i < n_diag)
            for e in range(2):
                dc_ref[e:e + 1, :] = carry[3 * e + 2]
            dz_ref[cols, LANES:2 * LANES] = jnp.where(_head_mask(0), carry[0], carry[3]).astype(BF16)
            dz_ref[cols, 2 * LANES:3 * LANES] = (carry[1] + carry[4]).astype(BF16)
            if j == nk - 1:
                dz_ref[:, 0:LANES] = dq_acc[...].astype(BF16)

        for k in range(nk):
            pl.when(pl.program_id(2) == k)(functools.partial(run, k))

    def seq(idx):
        return pl.BlockSpec((s, LANES), lambda b, hp, j: (b, idx(hp)))

    def kblk(c0):
        return pl.BlockSpec((bk, LANES), lambda b, hp, j: (b * nk + j, 3 * hp + c0))

    return pl.pallas_call(
        body, grid=(n_batch, N_PAIRS, nk),
        in_specs=[seq(lambda hp: 3 * hp), kblk(1), kblk(2), seq(lambda hp: hp), seq(lambda hp: N_PAIRS + hp),
                  seq(lambda hp: hp),
                  pl.BlockSpec((None, N_HEADS, s), lambda b, hp, j: (b, 0, 0)), pl.BlockSpec(memory_space=pl.ANY)],
        out_specs=[pl.BlockSpec((s, PAIR_WIDTH), lambda b, hp, j: (b, N_PAIRS + hp)),
                   pl.BlockSpec((None, None, 2, bk), lambda b, hp, j: (b, hp, 0, j))],
        out_shape=[jax.ShapeDtypeStruct(dz.shape, dz.dtype), jax.ShapeDtypeStruct((n_batch, N_PAIRS, 2, s), F32)],
        scratch_shapes=[pltpu.VMEM((s, LANES), F32)],
        input_output_aliases={7: 0},
        name=name, compiler_params=_params("parallel", "parallel", "arbitrary"),
    )(zf, zf, zf, o32, dy, lse, c_row, dz)


def _dil_bias(slope, dil):
    qi = lax.broadcasted_iota(jnp.int32, (BLOCK, 2 * BLOCK), 0)
    kj = lax.broadcasted_iota(jnp.int32, (BLOCK, 2 * BLOCK), 1)
    delta = qi + BLOCK - kj
    return jnp.where((delta >= 0) & (delta <= BLOCK), (-slope * dil) * delta.astype(F32), NEG)


def _alibi_slope(hp, e):
    slope = jnp.float32(0.0)
    for k in range(N_PAIRS):
        slope = jnp.where(hp == k, jnp.float32(2.0 ** -(2 * k + e + 1)), slope)
    return slope


def _first_block_bias(bias):
    return jnp.where(lax.broadcasted_iota(jnp.int32, bias.shape, 1) < BLOCK, NEG, bias)


def _fill_bias(bias_scr, hp):
    for di, dil in enumerate(DILATIONS):
        for e in range(2):
            bias_scr[2 * di + e] = _dil_bias(_alibi_slope(hp, e), dil)


def _pair_specs(rows):
    return [pl.BlockSpec((rows, LANES), lambda b, hp, c0=c0: (b, 3 * hp + c0)) for c0 in range(3)]


def _strided(start, size, dil):
    return pl.ds(start, size) if dil == 1 else pl.ds(start, size, stride=dil)


QUARTER = SEQ // 4


def _to_quarters(src, dst):
    for r in range(4):
        dst[r * QUARTER:(r + 1) * QUARTER, :] = src[pl.ds(r, QUARTER, stride=4), :]


def _from_quarters(src, dst):
    for r in range(4):
        dst[pl.ds(r, QUARTER, stride=4), :] = src[r * QUARTER:(r + 1) * QUARTER, :]


def _mix_weights(l1, l2, l3):
    m = jnp.maximum(jnp.maximum(l1, l2), l3)
    e1, e2, e3 = jnp.exp(l1 - m), jnp.exp(l2 - m), jnp.exp(l3 - m)
    inv = 1.0 / (e1 + e2 + e3)
    return e1 * inv, e2 * inv, e3 * inv


def _dil_fwd(zd, n_batch, name):
    s = SEQ
    t = n_batch * s

    def body(q_ref, k_ref, v_ref, y_ref, l1_ref, l2_ref, l3_ref, o_scr, qkv4, o4, l4, bias_scr):
        _fill_bias(bias_scr, pl.program_id(1))
        for a, ref in enumerate((q_ref, k_ref, v_ref)):
            _to_quarters(ref, qkv4.at[a])

        def unit(srcs, start, first, stride, di, o_dst, l_dst):
            qrows = _strided(start, BLOCK, stride)
            krows = qrows if first else _strided(start - BLOCK * stride, 2 * BLOCK, stride)
            q = (srcs[0][qrows, :] * ATT_SCALE).astype(BF16)
            kc = srcs[1][krows, :].astype(BF16)
            vc = srcs[2][krows, :].astype(BF16)
            if first:
                kc, vc = jnp.concatenate([kc, kc]), jnp.concatenate([vc, vc])
            outs, lses = [], []
            for e in range(2):
                bias = _first_block_bias(bias_scr[2 * di + e]) if first else bias_scr[2 * di + e]
                sc = _dot_nt(jnp.where(_head_mask(e), q, jnp.zeros_like(q)), kc) + bias
                m = jnp.max(sc, axis=1, keepdims=True)
                pe = jnp.exp(sc - m)
                l = jnp.sum(pe, axis=1, keepdims=True)
                outs.append(_dot((pe * (1.0 / l)).astype(BF16), vc))
                lses.append(m + jnp.log(l))
            o_dst[qrows, :] = jnp.where(_head_mask(0), outs[0], outs[1])
            l_dst[qrows, :] = jnp.where(_head_mask(0), lses[0], lses[1])

        for n in range(SEQ // BLOCK):
            unit((q_ref, k_ref, v_ref), n * BLOCK, n == 0, 1, 0, o_scr.at[0], l1_ref)
        quarters = tuple(qkv4.at[a] for a in range(3))
        for di in (1, 2):
            stride = DILATIONS[di] // 4
            for r in range(4):
                for g in range(stride):
                    for n in range(QUARTER // (BLOCK * stride)):
                        unit(quarters, r * QUARTER + n * BLOCK * stride + g, n == 0, stride, di, o4.at[di - 1], l4.at[di - 1])
        for di, l_ref in ((1, l2_ref), (2, l3_ref)):
            _from_quarters(o4.at[di - 1], o_scr.at[di])
            _from_quarters(l4.at[di - 1], l_ref)
        w = _mix_weights(l1_ref[...], l2_ref[...], l3_ref[...])
        y_ref[...] = (w[0] * o_scr[0] + w[1] * o_scr[1] + w[2] * o_scr[2]).astype(BF16)

    blk = pl.BlockSpec((s, LANES), lambda b, hp: (b, hp))
    res = pl.pallas_call(
        body, grid=(n_batch, N_PAIRS),
        in_specs=_pair_specs(s),
        out_specs=[blk] * 4,
        out_shape=[jax.ShapeDtypeStruct((t, MIX_HALF), BF16)] + [jax.ShapeDtypeStruct((t, MIX_HALF), F32)] * 3,
        scratch_shapes=[pltpu.VMEM((3, s, LANES), F32), pltpu.VMEM((3, s, LANES), F32), pltpu.VMEM((2, s, LANES), F32),
                        pltpu.VMEM((2, s, LANES), F32), pltpu.VMEM((6, BLOCK, 2 * BLOCK), F32)],
        name=name, compiler_params=_params("parallel", "arbitrary"),
    )(zd, zd, zd)
    return res[0], res[1:]


def _dil_bwd(zd, dy, ya, lses, n_batch, name):
    s = SEQ
    t = n_batch * s

    def body(q_ref, k_ref, v_ref, dy_ref, ya_ref, l1_ref, l2_ref, l3_ref, dz_ref, w_scr, dy_scr, dot_scr, acc, st4, acc4, bias_scr):
        for di, dil in enumerate(DILATIONS):
            bias_scr[di] = jnp.concatenate([_dil_bias(_alibi_slope(pl.program_id(1), e), dil) for e in range(2)])
        for di, w in enumerate(_mix_weights(l1_ref[...], l2_ref[...], l3_ref[...])):
            w_scr[di] = w
        dya = dy_ref[...].astype(F32)
        prod = dya * ya_ref[...].astype(F32)
        per_head = [jnp.sum(jnp.where(_head_mask(e), prod, 0.0), axis=1, keepdims=True) for e in range(2)]
        dy_scr[...] = dya
        dot_scr[...] = jnp.where(_head_mask(0), per_head[0], per_head[1])
        acc[...] = jnp.zeros_like(acc)
        acc4[...] = jnp.zeros_like(acc4)
        staged = (q_ref, k_ref, v_ref, w_scr.at[1], w_scr.at[2], l2_ref, l3_ref, dy_scr, dot_scr)
        for a, ref in enumerate(staged):
            _to_quarters(ref, st4.at[a])

        def unit(srcs, dst, start, first, stride, di):
            qrows = _strided(start, BLOCK, stride)
            krows = qrows if first else _strided(start - BLOCK * stride, 2 * BLOCK, stride)
            q = (srcs[0][qrows, :] * ATT_SCALE).astype(BF16)
            kc = srcs[1][krows, :].astype(BF16)
            vc = srcs[2][krows, :].astype(BF16)
            wq = srcs[3][qrows, :]
            lse = srcs[4][qrows, :]
            do = (wq * srcs[5][qrows, :]).astype(BF16)
            sub = wq * srcs[6][qrows, :]
            heads = lambda a: jnp.concatenate([jnp.where(_head_mask(e), a, jnp.zeros_like(a)) for e in range(2)])
            column = lambda a: jnp.concatenate([a[:, HEAD_DIM * e:HEAD_DIM * e + 1] for e in range(2)])
            qq, dd = heads(q), heads(do)
            bias = bias_scr[di]
            p = jnp.exp(_dot_nt(qq, kc) + (bias[:, BLOCK:] if first else bias) - column(lse))
            dsb = (p * (_dot_nt(dd, vc) - column(sub))).astype(BF16)
            dq = _dot(jnp.concatenate([dsb[:BLOCK], dsb[BLOCK:]], axis=1), heads(kc))
            dst.at[0][qrows, :] += dq * ATT_SCALE
            dst.at[1][krows, :] += _dot_tn(dsb, qq)
            dst.at[2][krows, :] += _dot_tn(p.astype(BF16), dd)

        token_order = (q_ref, k_ref, v_ref, w_scr.at[0], l1_ref, dy_scr, dot_scr)
        for n in range(SEQ // BLOCK):
            unit(token_order, acc, n * BLOCK, n == 0, 1, 0)
        for di in (1, 2):
            quarters = (st4.at[0], st4.at[1], st4.at[2], st4.at[2 + di], st4.at[4 + di], st4.at[7], st4.at[8])
            stride = DILATIONS[di] // 4
            for r in range(4):
                for g in range(stride):
                    for n in range(QUARTER // (BLOCK * stride)):
                        unit(quarters, acc4, r * QUARTER + n * BLOCK * stride + g, n == 0, stride, di)
        for k in range(3):
            for r in range(4):
                acc.at[k][pl.ds(r, QUARTER, stride=4), :] += acc4[k, r * QUARTER:(r + 1) * QUARTER, :]
            dz_ref[:, k * LANES:(k + 1) * LANES] = acc[k].astype(BF16)

    blk = pl.BlockSpec((s, LANES), lambda b, hp: (b, hp))
    pair = pl.BlockSpec((s, PAIR_WIDTH), lambda b, hp: (b, hp))
    return pl.pallas_call(
        body, grid=(n_batch, N_PAIRS),
        in_specs=_pair_specs(s) + [blk] * 5,
        out_specs=pair,
        out_shape=jax.ShapeDtypeStruct((t, 2 * 3 * MIX_HALF), BF16),
        scratch_shapes=[pltpu.VMEM((3, s, LANES), F32), pltpu.VMEM((s, LANES), F32), pltpu.VMEM((s, LANES), F32),
                        pltpu.VMEM((3, s, LANES), F32), pltpu.VMEM((9, s, LANES), F32), pltpu.VMEM((3, s, LANES), F32),
                        pltpu.VMEM((3, 2 * BLOCK, 2 * BLOCK), F32)],
        name=name, compiler_params=_params("parallel", "arbitrary"),
    )(zd, zd, zd, dy, ya, *lses)


def _xattn_probs(q, k):
    sc = _dot_nt(q, k) * X_SCALE
    pe = jnp.exp(sc - jnp.max(sc, axis=1, keepdims=True))
    return pe / jnp.sum(pe, axis=1, keepdims=True)


def _rms(xv, g):
    return (xv * lax.rsqrt(jnp.mean(xv * xv, axis=-1, keepdims=True) + EPS) * g).astype(BF16)


def _out_xattn_fwd(ya, yf, x0, w_out, g_xattn, w_xq, kx, vx, w_xo, g_mlp, name, tm=512):
    t, d = x0.shape
    per_example = SEQ // tm

    def body(ya_ref, yf_ref, x0_ref, wa_ref, wf_ref, g2_ref, wq_ref, k_ref, v_ref, wo_ref, g3_ref,
             x1_ref, h2_ref, q_ref, o_ref, x2_ref, h3_ref):
        x1 = x0_ref[...] + (_dot(ya_ref[...], wa_ref[...]) + _dot(yf_ref[...], wf_ref[...]))
        x1_ref[...] = x1
        h2 = _rms(x1, g2_ref[...])
        h2_ref[...] = h2
        q = _dot(h2, wq_ref[...]).astype(BF16)
        q_ref[...] = q
        for h in range(X_HEADS):
            cols = slice(h * X_HEAD_DIM, (h + 1) * X_HEAD_DIM)
            p = _xattn_probs(q[:, cols], k_ref[:, cols])
            o_ref[:, cols] = _dot(p.astype(BF16), v_ref[:, cols]).astype(BF16)
        x2 = x1 + _dot(o_ref[...], wo_ref[...])
        x2_ref[...] = x2
        h3_ref[...] = _rms(x2, g3_ref[...])

    row = lambda width: pl.BlockSpec((tm, width), lambda i: (i, 0))
    whole = lambda a: pl.BlockSpec(a.shape, lambda i: (0, 0))
    mem = pl.BlockSpec((N_MEM, d), lambda i: (i // per_example, 0))
    vec = pl.BlockSpec((1, d), lambda i: (0, 0))
    w_a, w_f = w_out[:MIX_HALF], w_out[MIX_HALF:]
    return pl.pallas_call(
        body, grid=(t // tm,),
        in_specs=[row(MIX_HALF), row(MIX_HALF), row(d), whole(w_a), whole(w_f), vec, whole(w_xq), mem, mem, whole(w_xo), vec],
        out_specs=[row(d)] * 6,
        out_shape=[jax.ShapeDtypeStruct((t, d), dt) for dt in (F32, BF16, BF16, BF16, F32, BF16)],
        name=name, compiler_params=_params("arbitrary"),
    )(ya, yf, x0, w_a, w_f, g_xattn.reshape(1, d), w_xq, kx, vx, w_xo, g_mlp.reshape(1, d))


def _xattn_chain_bwd(dx2, dx2b, x1, g_xattn, qx, kx, vx, w_xo, w_xq, name, tm=512, after=None):
    t, d = x1.shape
    per_example = SEQ // tm
    order = () if after is None else (after,)

    def body(dx2_ref, dx2b_ref, x1_ref, g_ref, q_ref, k_ref, v_ref, wo_ref, wq_ref, *rest):
        dx1_ref, dx1b_ref, dg_ref, dq_ref, dk_ref, dv_ref, dk_acc, dv_acc = rest[len(order):]
        i = pl.program_id(0)

        @pl.when(i % per_example == 0)
        def _():
            dk_acc[...] = jnp.zeros_like(dk_acc)
            dv_acc[...] = jnp.zeros_like(dv_acc)

        do = _dot_nt(dx2b_ref[...], wo_ref[...]).astype(BF16)
        for h in range(X_HEADS):
            cols = slice(h * X_HEAD_DIM, (h + 1) * X_HEAD_DIM)
            q, k, do_h = q_ref[:, cols], k_ref[:, cols], do[:, cols]
            p = _xattn_probs(q, k)
            dp = _dot_nt(do_h, v_ref[:, cols])
            dsb = (p * (dp - jnp.sum(p * dp, axis=1, keepdims=True))).astype(BF16)
            dq_ref[:, cols] = (_dot(dsb, k) * X_SCALE).astype(BF16)
            dk_acc[:, cols] += _dot_tn(dsb, q) * X_SCALE
            dv_acc[:, cols] += _dot_tn(p.astype(BF16), do_h)

        @pl.when(i % per_example == per_example - 1)
        def _():
            dk_ref[...] = dk_acc[...].astype(BF16)
            dv_ref[...] = dv_acc[...].astype(BF16)

        dx, dg_rows = _rms_bwd_tile(x1_ref[...], _dot_nt(dq_ref[...], wq_ref[...]), g_ref[...])
        dx = dx2_ref[...] + dx
        dx1_ref[...] = dx
        dx1b_ref[...] = dx.astype(BF16)

        @pl.when(i == 0)
        def _():
            dg_ref[...] = jnp.zeros_like(dg_ref)

        dg_ref[...] += jnp.sum(dg_rows, axis=0, keepdims=True)

    row = pl.BlockSpec((tm, d), lambda i: (i, 0))
    vec = pl.BlockSpec((1, d), lambda i: (0, 0))
    mem = pl.BlockSpec((N_MEM, d), lambda i: (i // per_example, 0))
    whole = lambda a: pl.BlockSpec(a.shape, lambda i: (0, 0))
    return pl.pallas_call(
        body, grid=(t // tm,),
        in_specs=[row, row, row, vec, row, mem, mem, whole(w_xo), whole(w_xq)] + [pl.BlockSpec(memory_space=pl.ANY)] * len(order),
        out_specs=[row, row, vec, row, mem, mem],
        out_shape=[jax.ShapeDtypeStruct((t, d), F32), jax.ShapeDtypeStruct((t, d), BF16), jax.ShapeDtypeStruct((1, d), F32),
                   jax.ShapeDtypeStruct((t, d), BF16), jax.ShapeDtypeStruct(kx.shape, BF16), jax.ShapeDtypeStruct(kx.shape, BF16)],
        scratch_shapes=[pltpu.VMEM((N_MEM, d), F32)] * 2,
        name=name, compiler_params=_params("arbitrary"),
    )(dx2, dx2b, x1, g_xattn.reshape(1, d), qx, kx, vx, w_xo, w_xq, *order)


def _adamw(w, g, m, v, name, rows):
    r, c = w.shape
    assert r % rows == 0, (name, w.shape, rows)

    def body(w_ref, g_ref, m_ref, v_ref, d_ref, nm_ref, nv_ref):
        gv = g_ref[...]
        m1 = ADAM_B1 * m_ref[...] + (1.0 - ADAM_B1) * gv
        v1 = ADAM_B2 * v_ref[...] + (1.0 - ADAM_B2) * jnp.square(gv)
        m_hat = m1 / (1.0 - ADAM_B1 ** ADAM_STEP)
        v_hat = v1 / (1.0 - ADAM_B2 ** ADAM_STEP)
        d_ref[...] = -ADAM_LR * (m_hat / (jnp.sqrt(v_hat) + ADAM_EPS) + ADAM_WD * w_ref[...])
        nm_ref[...] = m1
        nv_ref[...] = v1

    blk = pl.BlockSpec((rows, c), lambda i: (i, 0))
    return pl.pallas_call(
        body, grid=(r // rows,), in_specs=[blk] * 4, out_specs=[blk] * 3,
        out_shape=[jax.ShapeDtypeStruct((r, c), F32)] * 3,
        name=name, compiler_params=_params("arbitrary"),
    )(w, g, m, v)


def _relu2(acc):
    a = jnp.maximum(acc, 0.0)
    return acc, a * a


def _relu2_bwd(acc, u):
    return (2.0 * jnp.maximum(u.astype(F32), 0.0) * acc,)


def _local_step(x, mem, target, vecs, w_in, late_weights, hooks=None):
    n_batch = x.shape[0]
    t = n_batch * SEQ
    x0 = x.reshape(t, D_MODEL)
    mem2 = mem.reshape(n_batch * N_MEM, D_MODEL)
    tgt = target.reshape(t, D_MODEL)

    half = 3 * MIX_HALF
    w_qkv = jnp.concatenate([_pair_major(w_in[:, :half]), _pair_major(w_in[:, half:QKV_WIDTH])], axis=1)
    w_gate = jnp.pad(w_in[:, QKV_WIDTH:], ((0, 0), (0, GATE_PAD - N_HEADS)))
    b_pad = jnp.pad(vecs["b_forget"], (0, GATE_PAD - N_HEADS)).reshape(1, GATE_PAD)

    h1, zd, zf, gate = _in_proj(x0, vecs["g_mix"], jnp.concatenate([w_qkv, w_gate], axis=1), "in_proj")
    mn = _rmsnorm(mem2, vecs["g_mem"], "norm_mem")
    c_row, sg = _gate_fwd(gate, b_pad, n_batch, "gate_fwd")
    ya, lses = _dil_fwd(zd, n_batch, "dil_fwd")
    yf, of32, lse_f = _fox_fwd(zf, c_row, n_batch, "fox_fwd")
    wts = late_weights(yf)
    w_out = wts["w_out"]
    kx = _matmul(mn, wts["w_xk"], "xk", out_dtypes=(BF16,))[0]
    vx = _matmul(mn, wts["w_xv"], "xv", out_dtypes=(BF16,))[0]
    x1, h2, qx, ox, x2, h3 = _out_xattn_fwd(ya, yf, x0, w_out, vecs["g_xattn"], wts["w_xq"], kx, vx, wts["w_xo"],
                                            vecs["g_mlp"], "out_xattn")
    u, a2 = _matmul(h3, wts["w_up"], "mlp_up", out_dtypes=(BF16, BF16), epilogue=_relu2)
    loss, dx3, dx3b, dg_final = _loss_bwd(a2, wts["w_down"], x2, vecs["g_final"], tgt, "mlp_down_loss")

    du = _matmul(dx3b, wts["w_down"], "mlp_down_bwd", out_dtypes=(BF16,), extras=(u,), epilogue=_relu2_bwd, w_t=True)[0]
    shards = (N_CHIPS, 2 * D_MODEL, D_MODEL)
    g_mlp = _matmul_tn(h3, du, "gw_up", packed=(shards, lambda i, j: (j, 0, 0), None))
    g_mlp = _matmul_tn(a2, dx3b, "gw_down", packed=(shards, lambda i, j: (i, 1, 0), g_mlp))
    gw_up = g_mlp[:, :D_MODEL].transpose(1, 0, 2).reshape(D_MODEL, D_FF)
    gw_down = g_mlp[:, D_MODEL:].reshape(D_FF, D_MODEL)
    on_grads, on_swapped = hooks or (None, None)
    token = on_grads("mlp", g_mlp) if hooks else None
    dx2, dx2b, dg_mlp = _matmul_rms_bwd([du], [wts["w_up"]], x2, vecs["g_mlp"], dx3, "mlp_up_bwd", after=token)

    gw_xo = _matmul_tn(ox, dx2b, "gw_xo")
    token = on_swapped("mlp", dx2b) if hooks else None
    dx1, dx1b, dg_xattn, dqx, dkx, dvx = _xattn_chain_bwd(dx2, dx2b, x1, vecs["g_xattn"], qx, kx, vx, wts["w_xo"],
                                                          wts["w_xq"], "xattn_chain_bwd", after=token)
    gw_xq = _matmul_tn(h2, dqx, "gw_xq")
    gw_xk = _matmul_tn(mn, dkx, "gw_xk")
    gw_xv = _matmul_tn(mn, dvx, "gw_xv")
    dmn = _matmul(dkx, wts["w_xk"], "xk_bwd", w_t=True)[0]
    dmn = _matmul_res(dvx, wts["w_xv"], dmn, "xv_bwd", w_t=True)
    _, _, dg_mem = _rms_bwd(mem2, dmn, vecs["g_mem"], None, "norm_mem_bwd")

    gw_out = jnp.concatenate([_matmul_tn(ya, dx1b, "gw_out_a"), _matmul_tn(yf, dx1b, "gw_out_f")], axis=0)
    token = on_grads("mid", dict(w_out=gw_out, w_xq=gw_xq, w_xk=gw_xk, w_xv=gw_xv, w_xo=gw_xo)) if hooks else None
    dy = _matmul(dx1b, w_out, "out_bwd", out_dtypes=(BF16,), w_t=True, after=token)[0]
    dz = _dil_bwd(zd, dy, ya, lses, n_batch, "dil_bwd")
    dz, dc = _fox_bwd(zf, of32, dy, lse_f, c_row, dz, n_batch, "fox_bwd")
    dzg, db = _gate_bwd(dc.reshape(n_batch, N_HEADS, SEQ), sg, "gate_bwd")
    token = on_swapped("mid", dz) if hooks else None
    gw_pm = _matmul_tn(h1, dz, "gw_in_qkv", after=token)
    gw_in = jnp.concatenate([_pair_major_inv(gw_pm[:, :half]), _pair_major_inv(gw_pm[:, half:]),
                             _matmul_tn(h1, dzg, "gw_in_gate")[:, :N_HEADS]], axis=1)
    dx0, _, dg_mix = _matmul_rms_bwd([dz, dzg], [w_qkv, w_gate], x0, vecs["g_mix"], dx1, "in_bwd")

    gw = dict(w_in=gw_in, w_out=gw_out, w_xq=gw_xq, w_xk=gw_xk, w_xv=gw_xv, w_xo=gw_xo, w_up=gw_up, w_down=gw_down)
    gv = dict(g_mix=dg_mix, g_xattn=dg_xattn, g_mem=dg_mem, g_mlp=dg_mlp, g_final=dg_final, b_forget=db)
    return loss, dx0.reshape(x.shape), gw, gv


MESH = pl.DeviceIdType.MESH
ANY = pl.BlockSpec(memory_space=pl.ANY)


def _place():
    x, y, c = lax.axis_index("x"), lax.axis_index("y"), lax.axis_index("c")
    other_chips = [(1 - x, y), (x, 1 - y), (1 - x, 1 - y)]
    return x, y, c, other_chips


def _my_chip():
    return 2 * lax.axis_index("x") + lax.axis_index("y")


def _halves(rows, c, align):
    half = rows // 2
    assert rows % (2 * align) == 0, rows
    return pl.ds(pl.multiple_of(c * half, align), half), pl.ds(pl.multiple_of((1 - c) * half, align), half)


def _place_own(wall, pack):
    return lax.dynamic_update_slice(wall, pack[None], (_my_chip(), 0, 0))


HBM = pl.BlockSpec(memory_space=pltpu.HBM)
SEM = pl.BlockSpec(memory_space=pltpu.SEMAPHORE)
SPLIT_COPY = pltpu.CompilerParams(has_side_effects=pltpu.SideEffectType.DATAFLOW_SIDE_EFFECTING)


def _in_hbm(a):
    return pltpu.with_memory_space_constraint(a, pltpu.HBM)


def _start_call(start, src, land_shape, after, name):
    land = lax.empty(land_shape, src.dtype)

    def body(src_ref, land_ref, after_ref, send_sems, recv_sems, src_thru, land_thru, token):
        del after_ref, src_thru, land_thru
        start(src_ref, land_ref, send_sems, recv_sems)
        token[...] = jnp.zeros_like(token)

    return pl.pallas_call(
        body, name=name,
        out_shape=(pltpu.SemaphoreType.DMA((3,)), pltpu.SemaphoreType.DMA((3,)), pltpu.HBM(src.shape, src.dtype),
                   pltpu.HBM(land_shape, src.dtype), jax.ShapeDtypeStruct((8, LANES), F32)),
        in_specs=(HBM, HBM, ANY), out_specs=(SEM, SEM, HBM, HBM, pl.BlockSpec(memory_space=pltpu.VMEM)),
        input_output_aliases={0: 2, 1: 3}, compiler_params=SPLIT_COPY,
    )(_in_hbm(src), _in_hbm(land), after)


def _wait_call(body, started, after, name):
    send_sems, recv_sems, src, land, _ = started
    return pl.pallas_call(
        body, name=name,
        out_shape=(pltpu.HBM(src.shape, src.dtype), pltpu.HBM(land.shape, land.dtype)),
        in_specs=(HBM, HBM, SEM, SEM, ANY), out_specs=(HBM, HBM),
        input_output_aliases={0: 0, 1: 1}, compiler_params=SPLIT_COPY,
    )(src, land, send_sems, recv_sems, after)


def _gather_copies(p_ref, wall_ref, send_sems, recv_sems):
    x, y, c, chips = _place()
    me = 2 * x + y
    mine, _ = _halves(p_ref.shape[0], c, 16)
    out, back = [], []
    for k, chip in enumerate(chips):
        peer = dict(send_sem=send_sems.at[k], recv_sem=recv_sems.at[k], device_id=(chip[0], chip[1], c), device_id_type=MESH)
        out.append(pltpu.make_async_remote_copy(src_ref=p_ref.at[mine], dst_ref=wall_ref.at[me, mine], **peer))
        slab = wall_ref.at[2 * chip[0] + chip[1], mine]
        back.append(pltpu.make_async_remote_copy(src_ref=slab, dst_ref=slab, **peer))
    return out, back


def _gather_start(pack, after, name):
    def start(p_ref, wall_ref, send_sems, recv_sems):
        for cp in _gather_copies(p_ref, wall_ref, send_sems, recv_sems)[0]:
            cp.start()

    return _start_call(start, pack, (N_CHIPS,) + pack.shape, after, name)


def _gather_wait(started, after, name):
    def body(p_ref, wall_ref, send_sems, recv_sems, after_ref, p_dead, wall_out):
        del after_ref, p_dead, wall_out
        out, back = _gather_copies(p_ref, wall_ref, send_sems, recv_sems)
        for cp_out, cp_back in zip(out, back):
            cp_out.wait_send()
            cp_back.wait_recv()

    return _wait_call(body, started, after, name)


def _pass_on(wall, name):
    def body(w_in_ref, out_ref, send_sems, recv_sems):
        del w_in_ref
        x, y, c, chips = _place()
        mine, theirs = _halves(wall.shape[1], c, 16)
        sends = []
        for k, chip in enumerate(chips):
            slab = out_ref.at[2 * chip[0] + chip[1]]
            peer = dict(send_sem=send_sems.at[k], recv_sem=recv_sems.at[k], device_id=(x, y, 1 - c), device_id_type=MESH)
            cp = pltpu.make_async_remote_copy(src_ref=slab.at[mine], dst_ref=slab.at[mine], **peer)
            cp.start()
            sends.append((cp, pltpu.make_async_remote_copy(src_ref=slab.at[theirs], dst_ref=slab.at[theirs], **peer)))
        for cp, back in sends:
            back.wait_recv()
            cp.wait_send()

    return pl.pallas_call(
        body, in_specs=[ANY], out_specs=ANY, out_shape=jax.ShapeDtypeStruct(wall.shape, wall.dtype),
        scratch_shapes=[pltpu.SemaphoreType.DMA((3,))] * 2, input_output_aliases={0: 0}, name=name,
    )(wall)


def _swap_halves(g, name):
    half = g.shape[1] // 2

    def body(g_ref, out_ref, send_sem, recv_sem):
        x, y, c, _ = _place()
        _, theirs = _halves(g.shape[1], c, 8)
        cp = pltpu.make_async_remote_copy(src_ref=g_ref.at[:, theirs], dst_ref=out_ref, send_sem=send_sem, recv_sem=recv_sem,
                                          device_id=(x, y, 1 - c), device_id_type=MESH)
        cp.start()
        cp.wait()

    return pl.pallas_call(
        body, in_specs=[ANY], out_specs=ANY,
        out_shape=jax.ShapeDtypeStruct((N_CHIPS, half, g.shape[2]), F32),
        scratch_shapes=[pltpu.SemaphoreType.DMA, pltpu.SemaphoreType.DMA],
        name=name,
    )(g)


def _swap_copy(g_ref, land_ref, send_sems, recv_sems):
    x, y, c, _ = _place()
    _, theirs = _halves(g_ref.shape[1], c, 8)
    return pltpu.make_async_remote_copy(src_ref=g_ref.at[:, theirs], dst_ref=land_ref, send_sem=send_sems.at[0],
                                        recv_sem=recv_sems.at[0], device_id=(x, y, 1 - c), device_id_type=MESH)


def _swap_start(g, name):
    def start(g_ref, land_ref, send_sems, recv_sems):
        _swap_copy(g_ref, land_ref, send_sems, recv_sems).start()

    return _start_call(start, g, (N_CHIPS, g.shape[1] // 2, g.shape[2]), _core_index(), name)


def _swap_wait(started, after, name):
    def body(g_ref, land_ref, send_sems, recv_sems, after_ref, g_out, land_out):
        del after_ref, g_out, land_out
        cp = _swap_copy(g_ref, land_ref, send_sems, recv_sems)
        cp.wait_send()
        cp.wait_recv()

    return _wait_call(body, started, after, name)


def _core_index():
    return lax.axis_index("c").astype(jnp.int32).reshape(1)


def _row_tile(half):
    tile = max(t for t in range(16, 1025, 16) if half % t == 0)
    return tile, half // tile


def _add_sibling(g, got, name):
    half = g.shape[1] // 2
    tile, n_tiles = _row_tile(half)

    def body(c_ref, g_ref, got_ref, o_ref):
        o_ref[...] = (g_ref[...] + got_ref[...]).astype(BF16)

    width = g.shape[2]
    blk = pl.BlockSpec((None, tile, width), lambda s, i, c_ref: (s, i, 0))
    return pl.pallas_call(
        body,
        grid_spec=pltpu.PrefetchScalarGridSpec(
            num_scalar_prefetch=1, grid=(N_CHIPS, n_tiles),
            in_specs=[pl.BlockSpec((None, tile, width), lambda s, i, c_ref: (s, c_ref[0] * n_tiles + i, 0)), blk],
            out_specs=blk),
        out_shape=jax.ShapeDtypeStruct((N_CHIPS, half, width), BF16),
        name=name, compiler_params=_params("arbitrary", "arbitrary"),
    )(_core_index(), g, got)


def _exchange_copies(p_ref, land_ref, send_sems, recv_sems):
    x, y, c, chips = _place()
    me = 2 * x + y
    out, back = [], []
    for k, chip in enumerate(chips):
        peer = dict(send_sem=send_sems.at[k], recv_sem=recv_sems.at[k], device_id=(chip[0], chip[1], c), device_id_type=MESH)
        out.append(pltpu.make_async_remote_copy(src_ref=p_ref.at[2 * chip[0] + chip[1]], dst_ref=land_ref.at[me], **peer))
        slab = land_ref.at[2 * chip[0] + chip[1]]
        back.append(pltpu.make_async_remote_copy(src_ref=slab, dst_ref=slab, **peer))
    return out, back


def _with_own(got, part):
    me = _my_chip()
    return lax.dynamic_update_slice(got, lax.dynamic_slice(part, (me, 0, 0), (1,) + part.shape[1:]), (me, 0, 0))


def _exchange_start(part, name):
    def start(p_ref, land_ref, send_sems, recv_sems):
        for cp in _exchange_copies(p_ref, land_ref, send_sems, recv_sems)[0]:
            cp.start()

    return _start_call(start, part, part.shape, _core_index(), name)


def _exchange_wait(started, after, name):
    def body(p_ref, land_ref, send_sems, recv_sems, after_ref, p_dead, land_out):
        del after_ref, p_dead, land_out
        out, back = _exchange_copies(p_ref, land_ref, send_sems, recv_sems)
        for cp_out, cp_back in zip(out, back):
            cp_out.wait_send()
            cp_back.wait_recv()

    part, got = _wait_call(body, started, after, name)
    return _with_own(got, part)


def _sum_chips(parts, name):
    half, width = parts.shape[1:]
    tile, n_tiles = _row_tile(half)

    def body(c_ref, p0, p1, p2, p3, o_ref):
        f32 = lambda p: p[...].astype(F32)
        o_ref[...] = ((f32(p0) + f32(p1)) + f32(p2)) + f32(p3)

    def slab(s):
        return pl.BlockSpec((None, tile, width), lambda i, c_ref, s=s: (s, i, 0))

    return pl.pallas_call(
        body,
        grid_spec=pltpu.PrefetchScalarGridSpec(
            num_scalar_prefetch=1, grid=(n_tiles,),
            in_specs=[slab(s) for s in range(N_CHIPS)],
            out_specs=pl.BlockSpec((None, tile, width), lambda i, c_ref: (c_ref[0], i, 0))),
        out_shape=jax.ShapeDtypeStruct((2, half, width), F32),
        name=name, compiler_params=_params("arbitrary"),
    )(_core_index(), parts, parts, parts, parts)


def _share_halves(halves, name):
    def body(h_ref, out_ref, send_sem, recv_sem):
        del h_ref
        x, y, c, _ = _place()
        cp = pltpu.make_async_remote_copy(src_ref=out_ref.at[c], dst_ref=out_ref.at[c], send_sem=send_sem, recv_sem=recv_sem,
                                          device_id=(x, y, 1 - c), device_id_type=MESH)
        cp.start()
        pltpu.make_async_remote_copy(src_ref=out_ref.at[1 - c], dst_ref=out_ref.at[1 - c], send_sem=send_sem, recv_sem=recv_sem,
                                     device_id=(x, y, 1 - c), device_id_type=MESH).wait_recv()
        cp.wait_send()

    return pl.pallas_call(
        body, in_specs=[ANY], out_specs=ANY,
        out_shape=jax.ShapeDtypeStruct(halves.shape, halves.dtype),
        scratch_shapes=[pltpu.SemaphoreType.DMA] * 2,
        input_output_aliases={0: 0},
        name=name,
    )(halves)


def _reduce_parts(g, tag):
    return _add_sibling(g, _swap_halves(g, "swap_" + tag), "add_" + tag)


def _reduce_finish(got, tag):
    halves = _share_halves(_sum_chips(got, "sum_" + tag), "share_" + tag)
    return halves.reshape(2 * halves.shape[1], halves.shape[2])


SMALL_ROWS = 8


def _allreduce_small(v):
    def body(v_ref, out_ref, buf, send_sems, recv_sems):
        x, y, c, _ = _place()
        buf[4 * x + 2 * y + c] = v_ref[...]
        sends = []
        for k in range(1, N_DEV):
            px = 1 - x if k & 4 else x
            py = 1 - y if k & 2 else y
            pc = 1 - c if k & 1 else c
            cp = pltpu.make_async_remote_copy(src_ref=v_ref, dst_ref=buf.at[4 * x + 2 * y + c], send_sem=send_sems.at[k - 1],
                                              recv_sem=recv_sems.at[k - 1], device_id=(px, py, pc), device_id_type=MESH)
            cp.start()
            sends.append((cp, 4 * px + 2 * py + pc))
        for k, (cp, peer) in enumerate(sends):
            pltpu.make_async_remote_copy(src_ref=v_ref, dst_ref=buf.at[peer], send_sem=send_sems.at[k], recv_sem=recv_sems.at[k],
                                         device_id=(x, y, c), device_id_type=MESH).wait_recv()
        for cp, _ in sends:
            cp.wait_send()
        total = buf[0]
        for d in range(1, N_DEV):
            total = total + buf[d]
        out_ref[...] = total

    vmem = pl.BlockSpec(memory_space=pltpu.VMEM)
    return pl.pallas_call(
        body, in_specs=[vmem], out_specs=vmem,
        out_shape=jax.ShapeDtypeStruct(v.shape, v.dtype),
        scratch_shapes=[pltpu.VMEM((N_DEV,) + v.shape, v.dtype), pltpu.SemaphoreType.DMA((N_DEV - 1,)),
                        pltpu.SemaphoreType.DMA((N_DEV - 1,))],
        name="allreduce_small",
    )(v)


MATRICES = ("w_in", "w_out", "w_xq", "w_xk", "w_xv", "w_xo", "w_up", "w_down")
VECTORS = ("g_mix", "g_xattn", "g_mem", "g_mlp", "g_final", "b_forget")
WEIGHT_ORDER = ("g_mix", "w_in", "b_forget", "w_out", "g_xattn", "g_mem", "w_xq", "w_xk", "w_xv", "w_xo",
                "g_mlp", "w_up", "w_down", "g_final")
GROUPS = {"mlp": ("w_up", "w_down"), "mid": ("w_out", "w_xq", "w_xk", "w_xv", "w_xo"), "in": ("w_in",)}
LATE = GROUPS["mid"] + GROUPS["mlp"]
W_IN_SHARD = IN_WIDTH // N_CHIPS
SHARD_ROWS = {"w_out": 256, "w_xq": 256, "w_xk": 256, "w_xv": 256, "w_xo": 256, "w_up": 1024, "w_down": 1024}
PACK_ROWS = SHARD_ROWS
W_IN_PAD = -(-W_IN_SHARD // LANES) * LANES
ADAM_ROWS = 128


def _pack(parts, names):
    return jnp.concatenate([jnp.pad(parts[n], ((0, PACK_ROWS[n] - SHARD_ROWS[n]), (0, 0))) for n in names], axis=0)


def _unpack(a, names):
    out, pos = {}, 0
    for n in names:
        out[n] = a[..., pos:pos + SHARD_ROWS[n], :]
        pos += PACK_ROWS[n]
    return out


def _full_weights(wall, names):
    cols = lambda a: a.transpose(1, 0, 2).reshape(a.shape[1], -1)
    rows = lambda a: a.reshape(-1, a.shape[-1])
    if names == GROUPS["in"]:
        return {"w_in": cols(wall[:, :, :W_IN_SHARD])}
    return {n: cols(a) if n == "w_up" else rows(a) for n, a in _unpack(wall, names).items()}


def _shard_of(g, name, s):
    if name == "w_up":
        return g[:, s * D_MODEL:(s + 1) * D_MODEL]
    n = SHARD_ROWS[name]
    return g[s * n:(s + 1) * n]


def _pad_w_in(a):
    return jnp.pad(a, [(0, 0)] * (a.ndim - 1) + [(0, W_IN_PAD - W_IN_SHARD)])


def _pack_grads(gws, names):
    if names == GROUPS["in"]:
        return _pad_w_in(gws["w_in"].reshape(D_MODEL, N_CHIPS, W_IN_SHARD).transpose(1, 0, 2))
    return jnp.stack([_pack({n: _shard_of(gws[n], n, s) for n in names}, names) for s in range(N_CHIPS)])


def kernel(x, mem, g_mix, w_in, b_forget, w_out, g_xattn, g_mem, w_xq, w_xk, w_xv, w_xo, g_mlp, w_up, w_down, g_final, loss_target, m_g_mix, m_w_in, m_b_forget, m_w_out, m_g_xattn, m_g_mem, m_w_xq, m_w_xk, m_w_xv, m_w_xo, m_g_mlp, m_w_up, m_w_down, m_g_final, v_g_mix, v_w_in, v_b_forget, v_w_out, v_g_xattn, v_g_mem, v_w_xq, v_w_xk, v_w_xv, v_w_xo, v_g_mlp, v_w_up, v_w_down, v_g_final):
    given = dict(locals())
    weights = {n: given[n] for n in WEIGHT_ORDER}
    vecs = {n: weights[n] for n in VECTORS}

    shard = {n: weights[n].astype(BF16) for n in MATRICES}
    in_started = _gather_start(_pad_w_in(shard["w_in"]), _core_index(), "gather_in_start")
    late_pack = _pack(shard, LATE)
    in_pack, in_wall = _gather_wait(in_started, late_pack, "gather_in_wait")
    in_wall = _place_own(_pass_on(in_wall, "gather_in_pass"), in_pack)
    late = _gather_start(late_pack, in_wall, "gather_late_start")
    w_in_full = _full_weights(in_wall, GROUPS["in"])["w_in"]

    def late_weights(after):
        pack, wall = _gather_wait(late, after, "gather_late_wait")
        return _full_weights(_place_own(_pass_on(wall, "gather_late_pass"), pack), LATE)

    started = {}

    swapping = {}

    def on_grads(group, gws):
        packed = gws if group == "mlp" else _pack_grads(gws, GROUPS[group])
        swapping[group] = _swap_start(packed, "swap_%s_start" % group)
        return swapping[group][4]

    def on_swapped(group, after):
        g, got = _swap_wait(swapping[group], after, "swap_%s_wait" % group)
        started[group] = _exchange_start(_add_sibling(g, got, "add_" + group), "exchange_%s_start" % group)
        return started[group][4]

    loss, grad_x, gw, gv = _local_step(x, mem, loss_target, vecs, w_in_full, late_weights, (on_grads, on_swapped))

    part = _reduce_parts(_pack_grads(gw, GROUPS["in"]), "in")
    started["in"] = _exchange_start(part, "exchange_in_start")
    grads, delta, new_m, new_v = {}, {}, {}, {}

    def finish(group, after):
        got = _exchange_wait(started[group], after, "exchange_%s_wait" % group)
        done = _reduce_finish(got, group)
        for n, a in ({"w_in": done[:, :W_IN_SHARD]} if group == "in" else _unpack(done, GROUPS[group])).items():
            grads[n] = a.reshape(weights[n].shape)
            delta[n], new_m[n], new_v[n] = _adamw(weights[n], grads[n], given["m_" + n], given["v_" + n], "adamw_" + n, ADAM_ROWS)
        return new_v[GROUPS[group][-1]]

    after = finish("mlp", started["in"][4])
    after = finish("mid", after)

    row = lambda a: jnp.pad(a.reshape(-1), (0, D_MODEL - a.size)).reshape(1, D_MODEL)
    small = jnp.concatenate([gv[n] for n in VECTORS[:5]] + [row(gv["b_forget"][:, 0]), row(loss[0, :1]),
                             jnp.zeros((1, D_MODEL), F32)], axis=0)
    small = _allreduce_small(small)
    for k, n in enumerate(VECTORS[:5]):
        grads[n] = small[k]
    grads["b_forget"] = small[5, :N_HEADS]
    loss_total = small[6, 0]
    finish("in", after)

    stack = lambda prefix: jnp.concatenate([row(given[prefix + n]) for n in VECTORS] + [jnp.zeros((2, D_MODEL), F32)], axis=0)
    g_small = jnp.concatenate([small[:6], jnp.zeros((2, D_MODEL), F32)], axis=0)
    d, m1, v1 = _adamw(stack(""), g_small, stack("m_"), stack("v_"), "adamw_vectors", SMALL_ROWS)
    for k, n in enumerate(VECTORS):
        width = weights[n].shape[0]
        delta[n], new_m[n], new_v[n] = d[k, :width], m1[k, :width], v1[k, :width]

    return (loss_total, grad_x, *[grads[n] for n in WEIGHT_ORDER], *[delta[n] for n in WEIGHT_ORDER],
            *[new_m[n] for n in WEIGHT_ORDER], *[new_v[n] for n in WEIGHT_ORDER])
```

```python
import functools
import math

import jax
import jax.numpy as jnp
from jax import lax
from jax.experimental import pallas as pl
from jax.experimental.pallas import tpu as pltpu

F32 = jnp.float32
BF16 = jnp.bfloat16

D_MODEL = 1024
SEQ = 2048
N_MEM = 256
HEAD_DIM = 64
N_HEADS = 8
MIX_HALF = N_HEADS * HEAD_DIM
QKV_WIDTH = 6 * MIX_HALF
IN_WIDTH = QKV_WIDTH + N_HEADS
GATE_PAD = 128
BLOCK = 128
DILATIONS = (1, 4, 16)
X_HEADS = 4
X_HEAD_DIM = 256
D_FF = 4096
EPS = 1e-6
NEG = -1e30
ATT_SCALE = 1.0 / math.sqrt(HEAD_DIM)
X_SCALE = 1.0 / math.sqrt(X_HEAD_DIM)
LANES = 128
N_CHIPS = 4
N_DEV = 8

ADAM_LR = 0.001
ADAM_B1 = 0.9
ADAM_B2 = 0.999
ADAM_EPS = 1e-08
ADAM_WD = 0.01
ADAM_STEP = 10

VMEM_LIMIT = 48 * 1024 * 1024


def _params(*sem):
    return pltpu.CompilerParams(dimension_semantics=sem or None, vmem_limit_bytes=VMEM_LIMIT)


def _dot(a, b):
    return jnp.dot(a, b, preferred_element_type=F32)


def _dot_nt(a, b):
    return lax.dot_general(a, b, (((1,), (1,)), ((), ())), preferred_element_type=F32)


def _dot_tn(a, b):
    return lax.dot_general(a, b, (((0,), (0,)), ((), ())), preferred_element_type=F32)


def _dot_exact(x, e):
    hi = x.astype(BF16)
    r1 = x - hi.astype(F32)
    mid = r1.astype(BF16)
    lo = (r1 - mid.astype(F32)).astype(BF16)
    return _dot(hi, e) + _dot(mid, e) + _dot(lo, e)


def _head_mask(e):
    lane = lax.broadcasted_iota(jnp.int32, (1, LANES), 1)
    return (lane >= HEAD_DIM * e) & (lane < HEAD_DIM * (e + 1))


def _matmul(a, w, name, out_dtypes=(F32,), extras=(), epilogue=None, tm=1024, tn=1024, w_t=False, after=None):
    m, k = a.shape
    n = w.shape[0] if w_t else w.shape[1]
    tm, tn = min(tm, m), min(tn, n)
    assert m % tm == 0 and n % tn == 0, (name, a.shape, w.shape)
    n_ex = len(extras)
    order = () if after is None else (after,)

    def body(a_ref, w_ref, *rest):
        rest = rest[len(order):]
        acc = (_dot_nt if w_t else _dot)(a_ref[...], w_ref[...])
        res = (acc,) if epilogue is None else epilogue(acc, *[r[...] for r in rest[:n_ex]])
        for o_ref, r in zip(rest[n_ex:], res):
            o_ref[...] = r.astype(o_ref.dtype)

    tile = pl.BlockSpec((tm, tn), lambda i, j: (i, j))
    w_spec = pl.BlockSpec((tn, k), lambda i, j: (j, 0)) if w_t else pl.BlockSpec((k, tn), lambda i, j: (0, j))
    return pl.pallas_call(
        body, grid=(m // tm, n // tn),
        in_specs=[pl.BlockSpec((tm, k), lambda i, j: (i, 0)), w_spec] + [pl.BlockSpec(memory_space=pl.ANY)] * len(order) + [tile] * n_ex,
        out_specs=[tile] * len(out_dtypes),
        out_shape=[jax.ShapeDtypeStruct((m, n), dt) for dt in out_dtypes],
        name=name, compiler_params=_params("parallel", "arbitrary"),
    )(a, w, *order, *extras)


def _matmul_res(a, w, res, name, w_t=False):
    return _matmul(a, w, name, extras=(res,), epilogue=lambda acc, r: (r + acc,), w_t=w_t)[0]


def _matmul_tn(x, y, name, tm=1024, tn=1024, tk=2048, packed=None, after=None):
    t, m = x.shape
    _, n = y.shape
    tm, tn, tk = min(tm, m), min(tn, n), min(tk, t)
    assert m % tm == 0 and n % tn == 0 and t % tk == 0, (name, x.shape, y.shape)
    shape, place, into = packed or ((m, n), None, None)

    def body(x_ref, y_ref, *rest):
        o_ref = rest[-1]

        @pl.when(pl.program_id(2) == 0)
        def _():
            o_ref[...] = jnp.zeros_like(o_ref)

        o_ref[...] += _dot_tn(x_ref[...], y_ref[...])

    out_spec = (pl.BlockSpec((tm, tn), lambda i, j, k: (i, j)) if place is None
                else pl.BlockSpec((None, tm, tn), lambda i, j, k: place(i, j)))
    return pl.pallas_call(
        body, grid=(m // tm, n // tn, t // tk),
        in_specs=[pl.BlockSpec((tk, tm), lambda i, j, k: (k, i)), pl.BlockSpec((tk, tn), lambda i, j, k: (k, j))]
        + [pl.BlockSpec(memory_space=pl.ANY)] * ((into is not None) + (after is not None)),
        out_specs=out_spec, out_shape=jax.ShapeDtypeStruct(shape, F32),
        input_output_aliases={} if into is None else {2: 0},
        name=name, compiler_params=_params("parallel", "parallel", "arbitrary"),
    )(x, y, *(() if into is None else (into,)), *(() if after is None else (after,)))


def _rmsnorm(x, g, name, tm=512):
    t, d = x.shape
    tm = min(tm, t)

    def body(x_ref, g_ref, h_ref):
        xv = x_ref[...]
        r = lax.rsqrt(jnp.mean(xv * xv, axis=-1, keepdims=True) + EPS)
        h_ref[...] = (xv * r * g_ref[...]).astype(BF16)

    return pl.pallas_call(
        body, grid=(t // tm,),
        in_specs=[pl.BlockSpec((tm, d), lambda i: (i, 0)), pl.BlockSpec((1, d), lambda i: (0, 0))],
        out_specs=pl.BlockSpec((tm, d), lambda i: (i, 0)),
        out_shape=jax.ShapeDtypeStruct((t, d), BF16),
        name=name, compiler_params=_params("arbitrary"),
    )(x, g.reshape(1, d))


def _in_proj(x, g, w_all, name, tm=512):
    t, d = x.shape
    half = 3 * MIX_HALF

    def body(x_ref, g_ref, w_ref, h_ref, zd_ref, zf_ref, gate_ref):
        xv = x_ref[...]
        r = lax.rsqrt(jnp.mean(xv * xv, axis=-1, keepdims=True) + EPS)
        h = (xv * r * g_ref[...]).astype(BF16)
        h_ref[...] = h
        zd_ref[...] = _dot(h, w_ref[:, 0:half])
        zf_ref[...] = _dot(h, w_ref[:, half:2 * half]).astype(BF16)
        gate_ref[...] = _dot(h, w_ref[:, 2 * half:])

    row = lambda width: pl.BlockSpec((tm, width), lambda i: (i, 0))
    return pl.pallas_call(
        body, grid=(t // tm,),
        in_specs=[row(d), pl.BlockSpec((1, d), lambda i: (0, 0)), pl.BlockSpec(w_all.shape, lambda i: (0, 0))],
        out_specs=[row(d), row(half), row(half), row(GATE_PAD)],
        out_shape=[jax.ShapeDtypeStruct((t, d), BF16), jax.ShapeDtypeStruct((t, half), F32),
                   jax.ShapeDtypeStruct((t, half), BF16), jax.ShapeDtypeStruct((t, GATE_PAD), F32)],
        name=name, compiler_params=_params("arbitrary"),
    )(x, g.reshape(1, d), w_all)


def _rms_bwd_tile(xv, dh, g):
    d = xv.shape[-1]
    r = lax.rsqrt(jnp.mean(xv * xv, axis=-1, keepdims=True) + EPS)
    dyg = dh * g
    proj = jnp.sum(dyg * xv, axis=-1, keepdims=True)
    dx = r * dyg - xv * (r * r * r * (1.0 / d)) * proj
    return dx, dh * (xv * r)


def _rms_bwd(x, dh, g, dres, name, tm=512):
    t, d = x.shape
    tm = min(tm, t)
    has_res = dres is not None

    def body(x_ref, dh_ref, g_ref, *rest):
        if has_res:
            res_ref, dx_ref, dxb_ref, dg_ref = rest
        else:
            dx_ref, dxb_ref, dg_ref = rest
        dx, dg_rows = _rms_bwd_tile(x_ref[...], dh_ref[...], g_ref[...])
        if has_res:
            dx = res_ref[...] + dx
        dx_ref[...] = dx
        dxb_ref[...] = dx.astype(BF16)

        @pl.when(pl.program_id(0) == 0)
        def _():
            dg_ref[...] = jnp.zeros_like(dg_ref)

        dg_ref[...] += jnp.sum(dg_rows, axis=0, keepdims=True)

    row = pl.BlockSpec((tm, d), lambda i: (i, 0))
    vec = pl.BlockSpec((1, d), lambda i: (0, 0))
    return pl.pallas_call(
        body, grid=(t // tm,),
        in_specs=[row, row, vec] + ([row] if has_res else []),
        out_specs=[row, row, vec],
        out_shape=[jax.ShapeDtypeStruct((t, d), F32), jax.ShapeDtypeStruct((t, d), BF16), jax.ShapeDtypeStruct((1, d), F32)],
        name=name, compiler_params=_params("arbitrary"),
    )(x, dh, g.reshape(1, d), *((dres,) if has_res else ()))


def _row_dots(a_refs, w_refs, w_t):
    acc = None
    for a_ref, w_ref in zip(a_refs, w_refs):
        part = (_dot_nt if w_t else _dot)(a_ref[...], w_ref[...])
        acc = part if acc is None else acc + part
    return acc


def _row_specs(a_parts, w_parts, tm):
    specs = [pl.BlockSpec((tm, a.shape[1]), lambda i: (i, 0)) for a in a_parts]
    return specs + [pl.BlockSpec(w.shape, lambda i: (0, 0)) for w in w_parts]


def _matmul_rms_bwd(a_parts, w_parts, x, g, dres, name, tm=512, after=None):
    t, d = x.shape
    n = len(a_parts)
    order = () if after is None else (after,)

    def body(*refs):
        x_ref, g_ref, res_ref = refs[2 * n:2 * n + 3]
        dx_ref, dxb_ref, dg_ref = refs[2 * n + 3 + len(order):]
        dx, dg_rows = _rms_bwd_tile(x_ref[...], _row_dots(refs[:n], refs[n:2 * n], True), g_ref[...])
        dx = res_ref[...] + dx
        dx_ref[...] = dx
        dxb_ref[...] = dx.astype(BF16)

        @pl.when(pl.program_id(0) == 0)
        def _():
            dg_ref[...] = jnp.zeros_like(dg_ref)

        dg_ref[...] += jnp.sum(dg_rows, axis=0, keepdims=True)

    row = pl.BlockSpec((tm, d), lambda i: (i, 0))
    vec = pl.BlockSpec((1, d), lambda i: (0, 0))
    return pl.pallas_call(
        body, grid=(t // tm,),
        in_specs=_row_specs(a_parts, w_parts, tm) + [row, vec, row] + [pl.BlockSpec(memory_space=pl.ANY)] * len(order),
        out_specs=[row, row, vec],
        out_shape=[jax.ShapeDtypeStruct((t, d), F32), jax.ShapeDtypeStruct((t, d), BF16), jax.ShapeDtypeStruct((1, d), F32)],
        name=name, compiler_params=_params("arbitrary"),
    )(*a_parts, *w_parts, x, g.reshape(1, d), dres, *order)


def _loss_bwd(a, w, res, g, target, name, tm=512):
    t, d = res.shape

    def body(a_ref, w_ref, x_ref, g_ref, t_ref, loss_ref, dx_ref, dxb_ref, dg_ref):
        xv = x_ref[...] + _dot(a_ref[...], w_ref[...])
        gv = g_ref[...]
        r = lax.rsqrt(jnp.mean(xv * xv, axis=-1, keepdims=True) + EPS)
        err = xv * r * gv - t_ref[...]
        dx, dg_rows = _rms_bwd_tile(xv, err * (1.0 / d), gv)
        dx_ref[...] = dx
        dxb_ref[...] = dx.astype(BF16)

        @pl.when(pl.program_id(0) == 0)
        def _():
            dg_ref[...] = jnp.zeros_like(dg_ref)
            loss_ref[...] = jnp.zeros_like(loss_ref)

        dg_ref[...] += jnp.sum(dg_rows, axis=0, keepdims=True)
        part = jnp.sum(jnp.sum(err * err, axis=0, keepdims=True), axis=1, keepdims=True) * (0.5 / d)
        loss_ref[...] += jnp.broadcast_to(part, loss_ref.shape)

    row = pl.BlockSpec((tm, d), lambda i: (i, 0))
    vec = pl.BlockSpec((1, d), lambda i: (0, 0))
    return pl.pallas_call(
        body, grid=(t // tm,),
        in_specs=_row_specs([a], [w], tm) + [row, vec, row],
        out_specs=[pl.BlockSpec((1, LANES), lambda i: (0, 0)), row, row, vec],
        out_shape=[jax.ShapeDtypeStruct((1, LANES), F32), jax.ShapeDtypeStruct((t, d), F32),
                   jax.ShapeDtypeStruct((t, d), BF16), jax.ShapeDtypeStruct((1, d), F32)],
        name=name, compiler_params=_params("arbitrary"),
    )(a, w, res, g.reshape(1, d), target)


def _tri(upper):
    r = lax.broadcasted_iota(jnp.int32, (LANES, LANES), 0)
    c = lax.broadcasted_iota(jnp.int32, (LANES, LANES), 1)
    return jnp.where((r <= c) if upper else (r >= c), 1.0, 0.0).astype(BF16)


def _gate_fwd(gate, b_pad, n_batch, name):
    s = SEQ
    nblk = s // LANES

    def body(g_ref, b_ref, crow_ref, sg_ref):
        gz = g_ref[...] + b_ref[...]
        logf = jnp.minimum(gz, 0.0) - jnp.log(1.0 + jnp.exp(-jnp.abs(gz)))
        logf_t = logf.T
        sg_ref[...] = (1.0 / (1.0 + jnp.exp(gz))).T[0:N_HEADS]
        upper = _tri(True)
        carry = jnp.zeros((N_HEADS, 1), F32)
        for blk in range(nblk):
            seg = _dot_exact(logf_t[0:N_HEADS, blk * LANES:(blk + 1) * LANES], upper) + carry
            carry = seg[:, LANES - 1:LANES]
            crow_ref[:, blk * LANES:(blk + 1) * LANES] = seg

    return pl.pallas_call(
        body, grid=(n_batch,),
        in_specs=[pl.BlockSpec((s, GATE_PAD), lambda b: (b, 0)), pl.BlockSpec((1, GATE_PAD), lambda b: (0, 0))],
        out_specs=[pl.BlockSpec((None, N_HEADS, s), lambda b: (b, 0, 0)),
                   pl.BlockSpec((None, N_HEADS, s), lambda b: (b, 0, 0))],
        out_shape=[jax.ShapeDtypeStruct((n_batch, N_HEADS, s), F32),
                   jax.ShapeDtypeStruct((n_batch, N_HEADS, s), F32)],
        name=name, compiler_params=_params("arbitrary"),
    )(gate, b_pad)


def _gate_bwd(dc, sg, name):
    n_batch, _, s = dc.shape
    nblk = s // LANES

    def body(dc_ref, sg_ref, dz_ref, db_ref, dt_ref):
        lower = _tri(False)
        dcv = dc_ref[...]
        carry = jnp.zeros((N_HEADS, 1), F32)
        dt_ref[...] = jnp.zeros_like(dt_ref)
        for blk in reversed(range(nblk)):
            seg = _dot_exact(dcv[:, blk * LANES:(blk + 1) * LANES], lower) + carry
            carry = seg[:, 0:1]
            dt_ref[0:N_HEADS, blk * LANES:(blk + 1) * LANES] = seg * sg_ref[:, blk * LANES:(blk + 1) * LANES]
        dg_t = dt_ref[...]
        dz_ref[...] = dg_t.T.astype(BF16)

        @pl.when(pl.program_id(0) == 0)
        def _():
            db_ref[...] = jnp.zeros_like(db_ref)

        db_ref[...] += jnp.broadcast_to(jnp.sum(dg_t[0:N_HEADS], axis=1, keepdims=True), db_ref.shape)

    return pl.pallas_call(
        body, grid=(n_batch,),
        in_specs=[pl.BlockSpec((None, N_HEADS, s), lambda b: (b, 0, 0)), pl.BlockSpec((None, N_HEADS, s), lambda b: (b, 0, 0))],
        out_specs=[pl.BlockSpec((s, GATE_PAD), lambda b: (b, 0)), pl.BlockSpec((N_HEADS, LANES), lambda b: (0, 0))],
        out_shape=[jax.ShapeDtypeStruct((n_batch * s, GATE_PAD), BF16), jax.ShapeDtypeStruct((N_HEADS, LANES), F32)],
        scratch_shapes=[pltpu.VMEM((LANES, s), F32)],
        name=name, compiler_params=_params("arbitrary"),
    )(dc, sg)


FOX_BQ = 512
FOX_BK = 512
FOX_STRIP = 512
PAIR_WIDTH = 3 * LANES
N_PAIRS = N_HEADS // 2


def _pair_major(w):
    return w.reshape(w.shape[0], 3, N_PAIRS, LANES).transpose(0, 2, 1, 3).reshape(w.shape[0], 3 * MIX_HALF)


def _pair_major_inv(w):
    return w.reshape(w.shape[0], N_PAIRS, 3, LANES).transpose(0, 2, 1, 3).reshape(w.shape[0], 3 * MIX_HALF)


def _causal(i, j, bq, bk):
    qpos = i * bq + lax.broadcasted_iota(jnp.int32, (bq, 1), 0)
    kpos = j * bk + lax.broadcasted_iota(jnp.int32, (1, bk), 1)
    return kpos <= qpos


def _split_bf16(p):
    hi = p.astype(BF16)
    return hi, (p - hi.astype(F32)).astype(BF16)


def _fox_fwd(zf, c_row, n_batch, name):
    s, bq, bk = SEQ, FOX_BQ, FOX_BK
    nq = s // bq
    t = n_batch * s

    n_strip = bq // FOX_STRIP

    def body(q_ref, k_ref, v_ref, cr_ref, o_ref, o32_ref, lse_ref):
        hp = pl.program_id(1)
        strips = [slice(r * FOX_STRIP, (r + 1) * FOX_STRIP) for r in range(n_strip)]
        chains = [(e, r) for e in range(2) for r in range(n_strip)]
        qh = {}
        for e, r in chains:
            q = q_ref[strips[r], :] * ATT_SCALE
            qh[e, r] = jnp.where(_head_mask(e), q, jnp.zeros_like(q))

        def step(i, j, carry, masked):
            rows = pl.ds(j * bk, bk)
            kj, vj = k_ref[rows, :], v_ref[rows, :]
            ck = [cr_ref[pl.ds(2 * hp + e, 1), rows] for e in range(2)]
            out = []
            scores = [_dot_nt(qh[e, r], kj) for e, r in chains]
            for n, (e, r) in enumerate(chains):
                m, l, acc = carry[3 * n:3 * n + 3]
                sc = scores[n] - ck[e]
                if masked:
                    qpos = i * bq + r * FOX_STRIP + lax.broadcasted_iota(jnp.int32, (FOX_STRIP, 1), 0)
                    kpos = j * bk + lax.broadcasted_iota(jnp.int32, (1, bk), 1)
                    sc = jnp.where(kpos <= qpos, sc, NEG)
                m_new = jnp.maximum(m, jnp.max(sc, axis=1, keepdims=True))
                alpha = jnp.exp(m - m_new)
                p = jnp.exp(sc - m_new)
                p_hi, p_lo = _split_bf16(p)
                out += [m_new, alpha * l + jnp.sum(p, axis=1, keepdims=True), alpha * acc + (_dot(p_hi, vj) + _dot(p_lo, vj))]
            return tuple(out)

        def run(i):
            carry = (jnp.full((FOX_STRIP, 1), NEG, F32), jnp.zeros((FOX_STRIP, 1), F32), jnp.zeros((FOX_STRIP, LANES), F32)) * len(chains)
            n_clear = (i * bq) // bk
            for j in range((i * bq + bq + bk - 1) // bk):
                carry = step(i, j, carry, masked=j >= n_clear)
            for r in range(n_strip):
                outs = [carry[3 * (e * n_strip + r) + 2] / carry[3 * (e * n_strip + r) + 1] for e in range(2)]
                lses = [carry[3 * (e * n_strip + r)] + jnp.log(carry[3 * (e * n_strip + r) + 1]) for e in range(2)]
                o = jnp.where(_head_mask(0), outs[0], outs[1])
                o_ref[strips[r], :] = o.astype(BF16)
                o32_ref[strips[r], :] = o
                lse_ref[strips[r], :] = jnp.where(_head_mask(0), lses[0], lses[1])

        for k in range(nq):
            pl.when(pl.program_id(2) == k)(functools.partial(run, k))

    def col(c0):
        return lambda b, hp, i: (b, 3 * hp + c0)

    blk = pl.BlockSpec((bq, LANES), lambda b, hp, i: (b * nq + i, hp))
    return pl.pallas_call(
        body, grid=(n_batch, N_PAIRS, nq),
        in_specs=[pl.BlockSpec((bq, LANES), lambda b, hp, i: (b * nq + i, 3 * hp)),
                  pl.BlockSpec((s, LANES), col(1)), pl.BlockSpec((s, LANES), col(2)),
                  pl.BlockSpec((None, N_HEADS, s), lambda b, hp, i: (b, 0, 0))],
        out_specs=[blk, blk, blk],
        out_shape=[jax.ShapeDtypeStruct((t, MIX_HALF), BF16), jax.ShapeDtypeStruct((t, MIX_HALF), F32),
                   jax.ShapeDtypeStruct((t, MIX_HALF), F32)],
        name=name, compiler_params=_params("parallel", "parallel", "arbitrary"),
    )(zf, zf, zf, c_row)


def _fox_bwd(zf, o32, dy, lse, c_row, dz, n_batch, name):
    s, bq, bk = SEQ, FOX_BQ, FOX_BK
    nq, nk = s // bq, s // bk

    def body(q_ref, k_ref, v_ref, o_ref, do_ref, lse_ref, cr_ref, dz_in, dz_ref, dc_ref, dq_acc):
        del dz_in
        hp = pl.program_id(1)

        @pl.when(pl.program_id(2) == 0)
        def _():
            dq_acc[...] = jnp.zeros_like(dq_acc)

        kj, vj = k_ref[...], v_ref[...]
        km = [jnp.where(_head_mask(e), kj, jnp.zeros_like(kj)) for e in range(2)]

        def step(i, j, ck, carry, masked):
            rows = pl.ds(i * bq, bq)
            qi, doi = q_ref[rows, :] * ATT_SCALE, do_ref[rows, :]
            prod = doi.astype(F32) * o_ref[rows, :]
            out = []
            dq = jnp.zeros((bq, LANES), F32)
            for e in range(2):
                dk_a, dv_a, dc_a = carry[3 * e:3 * e + 3]
                mask = _head_mask(e)
                lane0 = HEAD_DIM * e
                dom = jnp.where(mask, doi, jnp.zeros_like(doi))
                delta = jnp.sum(jnp.where(mask, prod, 0.0), axis=1, keepdims=True)
                sc = _dot_nt(qi, km[e]) - ck[e]
                if masked:
                    sc = jnp.where(_causal(i, j, bq, bk), sc, NEG)
                p = jnp.exp(sc - lse_ref[rows, lane0:lane0 + 1])
                ds = p * (_dot_nt(dom, vj) - delta)
                dsb = ds.astype(BF16)
                dq = dq + _dot(dsb, km[e])
                out += [dk_a + _dot_tn(dsb, qi), dv_a + _dot_tn(p.astype(BF16), dom), dc_a - jnp.sum(ds, axis=0, keepdims=True)]
            dq_acc[rows, :] += dq * ATT_SCALE
            return tuple(out)

        def run(j):
            cols = pl.ds(j * bk, bk)
            ck = [cr_ref[pl.ds(2 * hp + e, 1), cols] for e in range(2)]
            carry = (jnp.zeros((bk, LANES), F32), jnp.zeros((bk, LANES), F32), jnp.zeros((1, bk), F32)) * 2
            n_diag = (j * bk + bk + bq - 1) // bq
            for i in range((j * bk) // bq, nq):
                carry = step(i, j, ck, carry, masked=i < n_diag)
            for e in range(2):
                dc_ref[e:e + 1, :] = carry[3 * e + 2]
            dz_ref[cols, LANES:2 * LANES] = jnp.where(_head_mask(0), carry[0], carry[3]).astype(BF16)
            dz_ref[cols, 2 * LANES:3 * LANES] = (carry[1] + carry[4]).astype(BF16)
            if j == nk - 1:
                dz_ref[:, 0:LANES] = dq_acc[...].astype(BF16)

        for k in range(nk):
            pl.when(pl.program_id(2) == k)(functools.partial(run, k))

    def seq(idx):
        return pl.BlockSpec((s, LANES), lambda b, hp, j: (b, idx(hp)))

    def kblk(c0):
        return pl.BlockSpec((bk, LANES), lambda b, hp, j: (b * nk + j, 3 * hp + c0))

    return pl.pallas_call(
        body, grid=(n_batch, N_PAIRS, nk),
        in_specs=[seq(lambda hp: 3 * hp), kblk(1), kblk(2), seq(lambda hp: hp), seq(lambda hp: N_PAIRS + hp),
                  seq(lambda hp: hp),
                  pl.BlockSpec((None, N_HEADS, s), lambda b, hp, j: (b, 0, 0)), pl.BlockSpec(memory_space=pl.ANY)],
        out_specs=[pl.BlockSpec((s, PAIR_WIDTH), lambda b, hp, j: (b, N_PAIRS + hp)),
                   pl.BlockSpec((None, None, 2, bk), lambda b, hp, j: (b, hp, 0, j))],
        out_shape=[jax.ShapeDtypeStruct(dz.shape, dz.dtype), jax.ShapeDtypeStruct((n_batch, N_PAIRS, 2, s), F32)],
        scratch_shapes=[pltpu.VMEM((s, LANES), F32)],
        input_output_aliases={7: 0},
        name=name, compiler_params=_params("parallel", "parallel", "arbitrary"),
    )(zf, zf, zf, o32, dy, lse, c_row, dz)


def _dil_bias(slope, dil):
    qi = lax.broadcasted_iota(jnp.int32, (BLOCK, 2 * BLOCK), 0)
    kj = lax.broadcasted_iota(jnp.int32, (BLOCK, 2 * BLOCK), 1)
    delta = qi + BLOCK - kj
    return jnp.where((delta >= 0) & (delta <= BLOCK), (-slope * dil) * delta.astype(F32), NEG)


def _alibi_slope(hp, e):
    slope = jnp.float32(0.0)
    for k in range(N_PAIRS):
        slope = jnp.where(hp == k, jnp.float32(2.0 ** -(2 * k + e + 1)), slope)
    return slope


def _first_block_bias(bias):
    return jnp.where(lax.broadcasted_iota(jnp.int32, bias.shape, 1) < BLOCK, NEG, bias)


def _fill_bias(bias_scr, hp):
    for di, dil in enumerate(DILATIONS):
        for e in range(2):
            bias_scr[2 * di + e] = _dil_bias(_alibi_slope(hp, e), dil)


def _pair_specs(rows):
    return [pl.BlockSpec((rows, LANES), lambda b, hp, c0=c0: (b, 3 * hp + c0)) for c0 in range(3)]


def _strided(start, size, dil):
    return pl.ds(start, size) if dil == 1 else pl.ds(start, size, stride=dil)


QUARTER = SEQ // 4


def _to_quarters(src, dst):
    for r in range(4):
        dst[r * QUARTER:(r + 1) * QUARTER, :] = src[pl.ds(r, QUARTER, stride=4), :]


def _from_quarters(src, dst):
    for r in range(4):
        dst[pl.ds(r, QUARTER, stride=4), :] = src[r * QUARTER:(r + 1) * QUARTER, :]


def _mix_weights(l1, l2, l3):
    m = jnp.maximum(jnp.maximum(l1, l2), l3)
    e1, e2, e3 = jnp.exp(l1 - m), jnp.exp(l2 - m), jnp.exp(l3 - m)
    inv = 1.0 / (e1 + e2 + e3)
    return e1 * inv, e2 * inv, e3 * inv


def _dil_fwd(zd, n_batch, name):
    s = SEQ
    t = n_batch * s

    def body(q_ref, k_ref, v_ref, y_ref, l1_ref, l2_ref, l3_ref, o_scr, qkv4, o4, l4, bias_scr):
        _fill_bias(bias_scr, pl.program_id(1))
        for a, ref in enumerate((q_ref, k_ref, v_ref)):
            _to_quarters(ref, qkv4.at[a])

        def unit(srcs, start, first, stride, di, o_dst, l_dst):
            qrows = _strided(start, BLOCK, stride)
            krows = qrows if first else _strided(start - BLOCK * stride, 2 * BLOCK, stride)
            q = (srcs[0][qrows, :] * ATT_SCALE).astype(BF16)
            kc = srcs[1][krows, :].astype(BF16)
            vc = srcs[2][krows, :].astype(BF16)
            if first:
                kc, vc = jnp.concatenate([kc, kc]), jnp.concatenate([vc, vc])
            outs, lses = [], []
            for e in range(2):
                bias = _first_block_bias(bias_scr[2 * di + e]) if first else bias_scr[2 * di + e]
                sc = _dot_nt(jnp.where(_head_mask(e), q, jnp.zeros_like(q)), kc) + bias
                m = jnp.max(sc, axis=1, keepdims=True)
                pe = jnp.exp(sc - m)
                l = jnp.sum(pe, axis=1, keepdims=True)
                outs.append(_dot((pe * (1.0 / l)).astype(BF16), vc))
                lses.append(m + jnp.log(l))
            o_dst[qrows, :] = jnp.where(_head_mask(0), outs[0], outs[1])
            l_dst[qrows, :] = jnp.where(_head_mask(0), lses[0], lses[1])

        for n in range(SEQ // BLOCK):
            unit((q_ref, k_ref, v_ref), n * BLOCK, n == 0, 1, 0, o_scr.at[0], l1_ref)
        quarters = tuple(qkv4.at[a] for a in range(3))
        for di in (1, 2):
            stride = DILATIONS[di] // 4
            for r in range(4):
                for g in range(stride):
                    for n in range(QUARTER // (BLOCK * stride)):
                        unit(quarters, r * QUARTER + n * BLOCK * stride + g, n == 0, stride, di, o4.at[di - 1], l4.at[di - 1])
        for di, l_ref in ((1, l2_ref), (2, l3_ref)):
            _from_quarters(o4.at[di - 1], o_scr.at[di])
            _from_quarters(l4.at[di - 1], l_ref)
        w = _mix_weights(l1_ref[...], l2_ref[...], l3_ref[...])
        y_ref[...] = (w[0] * o_scr[0] + w[1] * o_scr[1] + w[2] * o_scr[2]).astype(BF16)

    blk = pl.BlockSpec((s, LANES), lambda b, hp: (b, hp))
    res = pl.pallas_call(
        body, grid=(n_batch, N_PAIRS),
        in_specs=_pair_specs(s),
        out_specs=[blk] * 4,
        out_shape=[jax.ShapeDtypeStruct((t, MIX_HALF), BF16)] + [jax.ShapeDtypeStruct((t, MIX_HALF), F32)] * 3,
        scratch_shapes=[pltpu.VMEM((3, s, LANES), F32), pltpu.VMEM((3, s, LANES), F32), pltpu.VMEM((2, s, LANES), F32),
                        pltpu.VMEM((2, s, LANES), F32), pltpu.VMEM((6, BLOCK, 2 * BLOCK), F32)],
        name=name, compiler_params=_params("parallel", "arbitrary"),
    )(zd, zd, zd)
    return res[0], res[1:]


def _dil_bwd(zd, dy, ya, lses, n_batch, name):
    s = SEQ
    t = n_batch * s

    def body(q_ref, k_ref, v_ref, dy_ref, ya_ref, l1_ref, l2_ref, l3_ref, dz_ref, w_scr, dy_scr, dot_scr, acc, st4, acc4, bias_scr):
        for di, dil in enumerate(DILATIONS):
            bias_scr[di] = jnp.concatenate([_dil_bias(_alibi_slope(pl.program_id(1), e), dil) for e in range(2)])
        for di, w in enumerate(_mix_weights(l1_ref[...], l2_ref[...], l3_ref[...])):
            w_scr[di] = w
        dya = dy_ref[...].astype(F32)
        prod = dya * ya_ref[...].astype(F32)
        per_head = [jnp.sum(jnp.where(_head_mask(e), prod, 0.0), axis=1, keepdims=True) for e in range(2)]
        dy_scr[...] = dya
        dot_scr[...] = jnp.where(_head_mask(0), per_head[0], per_head[1])
        acc[...] = jnp.zeros_like(acc)
        acc4[...] = jnp.zeros_like(acc4)
        staged = (q_ref, k_ref, v_ref, w_scr.at[1], w_scr.at[2], l2_ref, l3_ref, dy_scr, dot_scr)
        for a, ref in enumerate(staged):
            _to_quarters(ref, st4.at[a])

        def unit(srcs, dst, start, first, stride, di):
            qrows = _strided(start, BLOCK, stride)
            krows = qrows if first else _strided(start - BLOCK * stride, 2 * BLOCK, stride)
            q = (srcs[0][qrows, :] * ATT_SCALE).astype(BF16)
            kc = srcs[1][krows, :].astype(BF16)
            vc = srcs[2][krows, :].astype(BF16)
            wq = srcs[3][qrows, :]
            lse = srcs[4][qrows, :]
            do = (wq * srcs[5][qrows, :]).astype(BF16)
            sub = wq * srcs[6][qrows, :]
            heads = lambda a: jnp.concatenate([jnp.where(_head_mask(e), a, jnp.zeros_like(a)) for e in range(2)])
            column = lambda a: jnp.concatenate([a[:, HEAD_DIM * e:HEAD_DIM * e + 1] for e in range(2)])
            qq, dd = heads(q), heads(do)
            bias = bias_scr[di]
            p = jnp.exp(_dot_nt(qq, kc) + (bias[:, BLOCK:] if first else bias) - column(lse))
            dsb = (p * (_dot_nt(dd, vc) - column(sub))).astype(BF16)
            dq = _dot(jnp.concatenate([dsb[:BLOCK], dsb[BLOCK:]], axis=1), heads(kc))
            dst.at[0][qrows, :] += dq * ATT_SCALE
            dst.at[1][krows, :] += _dot_tn(dsb, qq)
            dst.at[2][krows, :] += _dot_tn(p.astype(BF16), dd)

        token_order = (q_ref, k_ref, v_ref, w_scr.at[0], l1_ref, dy_scr, dot_scr)
        for n in range(SEQ // BLOCK):
            unit(token_order, acc, n * BLOCK, n == 0, 1, 0)
        for di in (1, 2):
            quarters = (st4.at[0], st4.at[1], st4.at[2], st4.at[2 + di], st4.at[4 + di], st4.at[7], st4.at[8])
            stride = DILATIONS[di] // 4
            for r in range(4):
                for g in range(stride):
                    for n in range(QUARTER // (BLOCK * stride)):
                        unit(quarters, acc4, r * QUARTER + n * BLOCK * stride + g, n == 0, stride, di)
        for k in range(3):
            for r in range(4):
                acc.at[k][pl.ds(r, QUARTER, stride=4), :] += acc4[k, r * QUARTER:(r + 1) * QUARTER, :]
            dz_ref[:, k * LANES:(k + 1) * LANES] = acc[k].astype(BF16)

    blk = pl.BlockSpec((s, LANES), lambda b, hp: (b, hp))
    pair = pl.BlockSpec((s, PAIR_WIDTH), lambda b, hp: (b, hp))
    return pl.pallas_call(
        body, grid=(n_batch, N_PAIRS),
        in_specs=_pair_specs(s) + [blk] * 5,
        out_specs=pair,
        out_shape=jax.ShapeDtypeStruct((t, 2 * 3 * MIX_HALF), BF16),
        scratch_shapes=[pltpu.VMEM((3, s, LANES), F32), pltpu.VMEM((s, LANES), F32), pltpu.VMEM((s, LANES), F32),
                        pltpu.VMEM((3, s, LANES), F32), pltpu.VMEM((9, s, LANES), F32), pltpu.VMEM((3, s, LANES), F32),
                        pltpu.VMEM((3, 2 * BLOCK, 2 * BLOCK), F32)],
        name=name, compiler_params=_params("parallel", "arbitrary"),
    )(zd, zd, zd, dy, ya, *lses)


def _xattn_probs(q, k):
    sc = _dot_nt(q, k) * X_SCALE
    pe = jnp.exp(sc - jnp.max(sc, axis=1, keepdims=True))
    return pe / jnp.sum(pe, axis=1, keepdims=True)


def _rms(xv, g):
    return (xv * lax.rsqrt(jnp.mean(xv * xv, axis=-1, keepdims=True) + EPS) * g).astype(BF16)


def _out_xattn_fwd(ya, yf, x0, w_out, g_xattn, w_xq, kx, vx, w_xo, g_mlp, name, tm=512):
    t, d = x0.shape
    per_example = SEQ // tm

    def body(ya_ref, yf_ref, x0_ref, wa_ref, wf_ref, g2_ref, wq_ref, k_ref, v_ref, wo_ref, g3_ref,
             x1_ref, h2_ref, q_ref, o_ref, x2_ref, h3_ref):
        x1 = x0_ref[...] + (_dot(ya_ref[...], wa_ref[...]) + _dot(yf_ref[...], wf_ref[...]))
        x1_ref[...] = x1
        h2 = _rms(x1, g2_ref[...])
        h2_ref[...] = h2
        q = _dot(h2, wq_ref[...]).astype(BF16)
        q_ref[...] = q
        for h in range(X_HEADS):
            cols = slice(h * X_HEAD_DIM, (h + 1) * X_HEAD_DIM)
            p = _xattn_probs(q[:, cols], k_ref[:, cols])
            o_ref[:, cols] = _dot(p.astype(BF16), v_ref[:, cols]).astype(BF16)
        x2 = x1 + _dot(o_ref[...], wo_ref[...])
        x2_ref[...] = x2
        h3_ref[...] = _rms(x2, g3_ref[...])

    row = lambda width: pl.BlockSpec((tm, width), lambda i: (i, 0))
    whole = lambda a: pl.BlockSpec(a.shape, lambda i: (0, 0))
    mem = pl.BlockSpec((N_MEM, d), lambda i: (i // per_example, 0))
    vec = pl.BlockSpec((1, d), lambda i: (0, 0))
    w_a, w_f = w_out[:MIX_HALF], w_out[MIX_HALF:]
    return pl.pallas_call(
        body, grid=(t // tm,),
        in_specs=[row(MIX_HALF), row(MIX_HALF), row(d), whole(w_a), whole(w_f), vec, whole(w_xq), mem, mem, whole(w_xo), vec],
        out_specs=[row(d)] * 6,
        out_shape=[jax.ShapeDtypeStruct((t, d), dt) for dt in (F32, BF16, BF16, BF16, F32, BF16)],
        name=name, compiler_params=_params("arbitrary"),
    )(ya, yf, x0, w_a, w_f, g_xattn.reshape(1, d), w_xq, kx, vx, w_xo, g_mlp.reshape(1, d))


def _xattn_chain_bwd(dx2, dx2b, x1, g_xattn, qx, kx, vx, w_xo, w_xq, name, tm=512, after=None):
    t, d = x1.shape
    per_example = SEQ // tm
    order = () if after is None else (after,)

    def body(dx2_ref, dx2b_ref, x1_ref, g_ref, q_ref, k_ref, v_ref, wo_ref, wq_ref, *rest):
        dx1_ref, dx1b_ref, dg_ref, dq_ref, dk_ref, dv_ref, dk_acc, dv_acc = rest[len(order):]
        i = pl.program_id(0)

        @pl.when(i % per_example == 0)
        def _():
            dk_acc[...] = jnp.zeros_like(dk_acc)
            dv_acc[...] = jnp.zeros_like(dv_acc)

        do = _dot_nt(dx2b_ref[...], wo_ref[...]).astype(BF16)
        for h in range(X_HEADS):
            cols = slice(h * X_HEAD_DIM, (h + 1) * X_HEAD_DIM)
            q, k, do_h = q_ref[:, cols], k_ref[:, cols], do[:, cols]
            p = _xattn_probs(q, k)
            dp = _dot_nt(do_h, v_ref[:, cols])
            dsb = (p * (dp - jnp.sum(p * dp, axis=1, keepdims=True))).astype(BF16)
            dq_ref[:, cols] = (_dot(dsb, k) * X_SCALE).astype(BF16)
            dk_acc[:, cols] += _dot_tn(dsb, q) * X_SCALE
            dv_acc[:, cols] += _dot_tn(p.astype(BF16), do_h)

        @pl.when(i % per_example == per_example - 1)
        def _():
            dk_ref[...] = dk_acc[...].astype(BF16)
            dv_ref[...] = dv_acc[...].astype(BF16)

        dx, dg_rows = _rms_bwd_tile(x1_ref[...], _dot_nt(dq_ref[...], wq_ref[...]), g_ref[...])
        dx = dx2_ref[...] + dx
        dx1_ref[...] = dx
        dx1b_ref[...] = dx.astype(BF16)

        @pl.when(i == 0)
        def _():
            dg_ref[...] = jnp.zeros_like(dg_ref)

        dg_ref[...] += jnp.sum(dg_rows, axis=0, keepdims=True)

    row = pl.BlockSpec((tm, d), lambda i: (i, 0))
    vec = pl.BlockSpec((1, d), lambda i: (0, 0))
    mem = pl.BlockSpec((N_MEM, d), lambda i: (i // per_example, 0))
    whole = lambda a: pl.BlockSpec(a.shape, lambda i: (0, 0))
    return pl.pallas_call(
        body, grid=(t // tm,),
        in_specs=[row, row, row, vec, row, mem, mem, whole(w_xo), whole(w_xq)] + [pl.BlockSpec(memory_space=pl.ANY)] * len(order),
        out_specs=[row, row, vec, row, mem, mem],
        out_shape=[jax.ShapeDtypeStruct((t, d), F32), jax.ShapeDtypeStruct((t, d), BF16), jax.ShapeDtypeStruct((1, d), F32),
                   jax.ShapeDtypeStruct((t, d), BF16), jax.ShapeDtypeStruct(kx.shape, BF16), jax.ShapeDtypeStruct(kx.shape, BF16)],
        scratch_shapes=[pltpu.VMEM((N_MEM, d), F32)] * 2,
        name=name, compiler_params=_params("arbitrary"),
    )(dx2, dx2b, x1, g_xattn.reshape(1, d), qx, kx, vx, w_xo, w_xq, *order)


def _adamw(w, g, m, v, name, rows):
    r, c = w.shape
    assert r % rows == 0, (name, w.shape, rows)

    def body(w_ref, g_ref, m_ref, v_ref, d_ref, nm_ref, nv_ref):
        gv = g_ref[...]
        m1 = ADAM_B1 * m_ref[...] + (1.0 - ADAM_B1) * gv
        v1 = ADAM_B2 * v_ref[...] + (1.0 - ADAM_B2) * jnp.square(gv)
        m_hat = m1 / (1.0 - ADAM_B1 ** ADAM_STEP)
        v_hat = v1 / (1.0 - ADAM_B2 ** ADAM_STEP)
        d_ref[...] = -ADAM_LR * (m_hat / (jnp.sqrt(v_hat) + ADAM_EPS) + ADAM_WD * w_ref[...])
        nm_ref[...] = m1
        nv_ref[...] = v1

    blk = pl.BlockSpec((rows, c), lambda i: (i, 0))
    return pl.pallas_call(
        body, grid=(r // rows,), in_specs=[blk] * 4, out_specs=[blk] * 3,
        out_shape=[jax.ShapeDtypeStruct((r, c), F32)] * 3,
        name=name, compiler_params=_params("arbitrary"),
    )(w, g, m, v)


def _relu2(acc):
    a = jnp.maximum(acc, 0.0)
    return acc, a * a


def _relu2_bwd(acc, u):
    return (2.0 * jnp.maximum(u.astype(F32), 0.0) * acc,)


def _local_step(x, mem, target, vecs, w_in, late_weights, hooks=None):
    n_batch = x.shape[0]
    t = n_batch * SEQ
    x0 = x.reshape(t, D_MODEL)
    mem2 = mem.reshape(n_batch * N_MEM, D_MODEL)
    tgt = target.reshape(t, D_MODEL)

    half = 3 * MIX_HALF
    w_qkv = jnp.concatenate([_pair_major(w_in[:, :half]), _pair_major(w_in[:, half:QKV_WIDTH])], axis=1)
    w_gate = jnp.pad(w_in[:, QKV_WIDTH:], ((0, 0), (0, GATE_PAD - N_HEADS)))
    b_pad = jnp.pad(vecs["b_forget"], (0, GATE_PAD - N_HEADS)).reshape(1, GATE_PAD)

    h1, zd, zf, gate = _in_proj(x0, vecs["g_mix"], jnp.concatenate([w_qkv, w_gate], axis=1), "in_proj")
    mn = _rmsnorm(mem2, vecs["g_mem"], "norm_mem")
    c_row, sg = _gate_fwd(gate, b_pad, n_batch, "gate_fwd")
    ya, lses = _dil_fwd(zd, n_batch, "dil_fwd")
    yf, of32, lse_f = _fox_fwd(zf, c_row, n_batch, "fox_fwd")
    wts = late_weights(yf)
    w_out = wts["w_out"]
    kx = _matmul(mn, wts["w_xk"], "xk", out_dtypes=(BF16,))[0]
    vx = _matmul(mn, wts["w_xv"], "xv", out_dtypes=(BF16,))[0]
    x1, h2, qx, ox, x2, h3 = _out_xattn_fwd(ya, yf, x0, w_out, vecs["g_xattn"], wts["w_xq"], kx, vx, wts["w_xo"],
                                            vecs["g_mlp"], "out_xattn")
    u, a2 = _matmul(h3, wts["w_up"], "mlp_up", out_dtypes=(BF16, BF16), epilogue=_relu2)
    loss, dx3, dx3b, dg_final = _loss_bwd(a2, wts["w_down"], x2, vecs["g_final"], tgt, "mlp_down_loss")

    du = _matmul(dx3b, wts["w_down"], "mlp_down_bwd", out_dtypes=(BF16,), extras=(u,), epilogue=_relu2_bwd, w_t=True)[0]
    shards = (N_CHIPS, 2 * D_MODEL, D_MODEL)
    g_mlp = _matmul_tn(h3, du, "gw_up", packed=(shards, lambda i, j: (j, 0, 0), None))
    g_mlp = _matmul_tn(a2, dx3b, "gw_down", packed=(shards, lambda i, j: (i, 1, 0), g_mlp))
    gw_up = g_mlp[:, :D_MODEL].transpose(1, 0, 2).reshape(D_MODEL, D_FF)
    gw_down = g_mlp[:, D_MODEL:].reshape(D_FF, D_MODEL)
    on_grads, on_swapped = hooks or (None, None)
    token = on_grads("mlp", g_mlp) if hooks else None
    dx2, dx2b, dg_mlp = _matmul_rms_bwd([du], [wts["w_up"]], x2, vecs["g_mlp"], dx3, "mlp_up_bwd", after=token)

    gw_xo = _matmul_tn(ox, dx2b, "gw_xo")
    token = on_swapped("mlp", dx2b) if hooks else None
    dx1, dx1b, dg_xattn, dqx, dkx, dvx = _xattn_chain_bwd(dx2, dx2b, x1, vecs["g_xattn"], qx, kx, vx, wts["w_xo"],
                                                          wts["w_xq"], "xattn_chain_bwd", after=token)
    gw_xq = _matmul_tn(h2, dqx, "gw_xq")
    gw_xk = _matmul_tn(mn, dkx, "gw_xk")
    gw_xv = _matmul_tn(mn, dvx, "gw_xv")
    dmn = _matmul(dkx, wts["w_xk"], "xk_bwd", w_t=True)[0]
    dmn = _matmul_res(dvx, wts["w_xv"], dmn, "xv_bwd", w_t=True)
    _, _, dg_mem = _rms_bwd(mem2, dmn, vecs["g_mem"], None, "norm_mem_bwd")

    gw_out = jnp.concatenate([_matmul_tn(ya, dx1b, "gw_out_a"), _matmul_tn(yf, dx1b, "gw_out_f")], axis=0)
    token = on_grads("mid", dict(w_out=gw_out, w_xq=gw_xq, w_xk=gw_xk, w_xv=gw_xv, w_xo=gw_xo)) if hooks else None
    dy = _matmul(dx1b, w_out, "out_bwd", out_dtypes=(BF16,), w_t=True, after=token)[0]
    dz = _dil_bwd(zd, dy, ya, lses, n_batch, "dil_bwd")
    dz, dc = _fox_bwd(zf, of32, dy, lse_f, c_row, dz, n_batch, "fox_bwd")
    dzg, db = _gate_bwd(dc.reshape(n_batch, N_HEADS, SEQ), sg, "gate_bwd")
    token = on_swapped("mid", dz) if hooks else None
    gw_pm = _matmul_tn(h1, dz, "gw_in_qkv", after=token)
    gw_in = jnp.concatenate([_pair_major_inv(gw_pm[:, :half]), _pair_major_inv(gw_pm[:, half:]),
                             _matmul_tn(h1, dzg, "gw_in_gate")[:, :N_HEADS]], axis=1)
    dx0, _, dg_mix = _matmul_rms_bwd([dz, dzg], [w_qkv, w_gate], x0, vecs["g_mix"], dx1, "in_bwd")

    gw = dict(w_in=gw_in, w_out=gw_out, w_xq=gw_xq, w_xk=gw_xk, w_xv=gw_xv, w_xo=gw_xo, w_up=gw_up, w_down=gw_down)
    gv = dict(g_mix=dg_mix, g_xattn=dg_xattn, g_mem=dg_mem, g_mlp=dg_mlp, g_final=dg_final, b_forget=db)
    return loss, dx0.reshape(x.shape), gw, gv


MESH = pl.DeviceIdType.MESH
ANY = pl.BlockSpec(memory_space=pl.ANY)


def _place():
    x, y, c = lax.axis_index("x"), lax.axis_index("y"), lax.axis_index("c")
    other_chips = [(1 - x, y), (x, 1 - y), (1 - x, 1 - y)]
    return x, y, c, other_chips


def _my_chip():
    return 2 * lax.axis_index("x") + lax.axis_index("y")


def _halves(rows, c, align):
    half = rows // 2
    assert rows % (2 * align) == 0, rows
    return pl.ds(pl.multiple_of(c * half, align), half), pl.ds(pl.multiple_of((1 - c) * half, align), half)


def _place_own(wall, pack):
    return lax.dynamic_update_slice(wall, pack[None], (_my_chip(), 0, 0))


HBM = pl.BlockSpec(memory_space=pltpu.HBM)
SEM = pl.BlockSpec(memory_space=pltpu.SEMAPHORE)
SPLIT_COPY = pltpu.CompilerParams(has_side_effects=pltpu.SideEffectType.DATAFLOW_SIDE_EFFECTING)


def _in_hbm(a):
    return pltpu.with_memory_space_constraint(a, pltpu.HBM)


def _start_call(start, src, land_shape, after, name):
    land = lax.empty(land_shape, src.dtype)

    def body(src_ref, land_ref, after_ref, send_sems, recv_sems, src_thru, land_thru, token):
        del after_ref, src_thru, land_thru
        start(src_ref, land_ref, send_sems, recv_sems)
        token[...] = jnp.zeros_like(token)

    return pl.pallas_call(
        body, name=name,
        out_shape=(pltpu.SemaphoreType.DMA((3,)), pltpu.SemaphoreType.DMA((3,)), pltpu.HBM(src.shape, src.dtype),
                   pltpu.HBM(land_shape, src.dtype), jax.ShapeDtypeStruct((8, LANES), F32)),
        in_specs=(HBM, HBM, ANY), out_specs=(SEM, SEM, HBM, HBM, pl.BlockSpec(memory_space=pltpu.VMEM)),
        input_output_aliases={0: 2, 1: 3}, compiler_params=SPLIT_COPY,
    )(_in_hbm(src), _in_hbm(land), after)


def _wait_call(body, started, after, name):
    send_sems, recv_sems, src, land, _ = started
    return pl.pallas_call(
        body, name=name,
        out_shape=(pltpu.HBM(src.shape, src.dtype), pltpu.HBM(land.shape, land.dtype)),
        in_specs=(HBM, HBM, SEM, SEM, ANY), out_specs=(HBM, HBM),
        input_output_aliases={0: 0, 1: 1}, compiler_params=SPLIT_COPY,
    )(src, land, send_sems, recv_sems, after)


def _gather_copies(p_ref, wall_ref, send_sems, recv_sems):
    x, y, c, chips = _place()
    me = 2 * x + y
    mine, _ = _halves(p_ref.shape[0], c, 16)
    out, back = [], []
    for k, chip in enumerate(chips):
        peer = dict(send_sem=send_sems.at[k], recv_sem=recv_sems.at[k], device_id=(chip[0], chip[1], c), device_id_type=MESH)
        out.append(pltpu.make_async_remote_copy(src_ref=p_ref.at[mine], dst_ref=wall_ref.at[me, mine], **peer))
        slab = wall_ref.at[2 * chip[0] + chip[1], mine]
        back.append(pltpu.make_async_remote_copy(src_ref=slab, dst_ref=slab, **peer))
    return out, back


def _gather_start(pack, after, name):
    def start(p_ref, wall_ref, send_sems, recv_sems):
        for cp in _gather_copies(p_ref, wall_ref, send_sems, recv_sems)[0]:
            cp.start()

    return _start_call(start, pack, (N_CHIPS,) + pack.shape, after, name)


def _gather_wait(started, after, name):
    def body(p_ref, wall_ref, send_sems, recv_sems, after_ref, p_dead, wall_out):
        del after_ref, p_dead, wall_out
        out, back = _gather_copies(p_ref, wall_ref, send_sems, recv_sems)
        for cp_out, cp_back in zip(out, back):
            cp_out.wait_send()
            cp_back.wait_recv()

    return _wait_call(body, started, after, name)


def _pass_on(wall, name):
    def body(w_in_ref, out_ref, send_sems, recv_sems):
        del w_in_ref
        x, y, c, chips = _place()
        mine, theirs = _halves(wall.shape[1], c, 16)
        sends = []
        for k, chip in enumerate(chips):
            slab = out_ref.at[2 * chip[0] + chip[1]]
            peer = dict(send_sem=send_sems.at[k], recv_sem=recv_sems.at[k], device_id=(x, y, 1 - c), device_id_type=MESH)
            cp = pltpu.make_async_remote_copy(src_ref=slab.at[mine], dst_ref=slab.at[mine], **peer)
            cp.start()
            sends.append((cp, pltpu.make_async_remote_copy(src_ref=slab.at[theirs], dst_ref=slab.at[theirs], **peer)))
        for cp, back in sends:
            back.wait_recv()
            cp.wait_send()

    return pl.pallas_call(
        body, in_specs=[ANY], out_specs=ANY, out_shape=jax.ShapeDtypeStruct(wall.shape, wall.dtype),
        scratch_shapes=[pltpu.SemaphoreType.DMA((3,))] * 2, input_output_aliases={0: 0}, name=name,
    )(wall)


def _swap_copy(g_ref, land_ref, send_sems, recv_sems):
    x, y, c, _ = _place()
    _, theirs = _halves(g_ref.shape[1], c, 8)
    return pltpu.make_async_remote_copy(src_ref=g_ref.at[:, theirs], dst_ref=land_ref, send_sem=send_sems.at[0],
                                        recv_sem=recv_sems.at[0], device_id=(x, y, 1 - c), device_id_type=MESH)


def _swap_start(g, name):
    def start(g_ref, land_ref, send_sems, recv_sems):
        _swap_copy(g_ref, land_ref, send_sems, recv_sems).start()

    return _start_call(start, g, (N_CHIPS, g.shape[1] // 2, g.shape[2]), _core_index(), name)


def _swap_wait(started, after, name):
    def body(g_ref, land_ref, send_sems, recv_sems, after_ref, g_out, land_out):
        del after_ref, g_out, land_out
        cp = _swap_copy(g_ref, land_ref, send_sems, recv_sems)
        cp.wait_send()
        cp.wait_recv()

    return _wait_call(body, started, after, name)


def _core_index():
    return lax.axis_index("c").astype(jnp.int32).reshape(1)


def _row_tile(half):
    tile = max(t for t in range(16, 1025, 16) if half % t == 0)
    return tile, half // tile


def _add_sibling(g, got, name):
    half = g.shape[1] // 2
    tile, n_tiles = _row_tile(half)

    def body(c_ref, g_ref, got_ref, o_ref):
        o_ref[...] = (g_ref[...] + got_ref[...]).astype(BF16)

    width = g.shape[2]
    blk = pl.BlockSpec((None, tile, width), lambda s, i, c_ref: (s, i, 0))
    return pl.pallas_call(
        body,
        grid_spec=pltpu.PrefetchScalarGridSpec(
            num_scalar_prefetch=1, grid=(N_CHIPS, n_tiles),
            in_specs=[pl.BlockSpec((None, tile, width), lambda s, i, c_ref: (s, c_ref[0] * n_tiles + i, 0)), blk],
            out_specs=blk),
        out_shape=jax.ShapeDtypeStruct((N_CHIPS, half, width), BF16),
        name=name, compiler_params=_params("arbitrary", "arbitrary"),
    )(_core_index(), g, got)


def _exchange_copies(p_ref, land_ref, send_sems, recv_sems):
    x, y, c, chips = _place()
    me = 2 * x + y
    out, back = [], []
    for k, chip in enumerate(chips):
        peer = dict(send_sem=send_sems.at[k], recv_sem=recv_sems.at[k], device_id=(chip[0], chip[1], c), device_id_type=MESH)
        out.append(pltpu.make_async_remote_copy(src_ref=p_ref.at[2 * chip[0] + chip[1]], dst_ref=land_ref.at[me], **peer))
        slab = land_ref.at[2 * chip[0] + chip[1]]
        back.append(pltpu.make_async_remote_copy(src_ref=slab, dst_ref=slab, **peer))
    return out, back


def _with_own(got, part):
    me = _my_chip()
    return lax.dynamic_update_slice(got, lax.dynamic_slice(part, (me, 0, 0), (1,) + part.shape[1:]), (me, 0, 0))


def _exchange_start(part, name):
    def start(p_ref, land_ref, send_sems, recv_sems):
        for cp in _exchange_copies(p_ref, land_ref, send_sems, recv_sems)[0]:
            cp.start()

    return _start_call(start, part, part.shape, _core_index(), name)


def _exchange_wait(started, after, name):
    def body(p_ref, land_ref, send_sems, recv_sems, after_ref, p_dead, land_out):
        del after_ref, p_dead, land_out
        out, back = _exchange_copies(p_ref, land_ref, send_sems, recv_sems)
        for cp_out, cp_back in zip(out, back):
            cp_out.wait_send()
            cp_back.wait_recv()

    part, got = _wait_call(body, started, after, name)
    return _with_own(got, part)


def _sum_chips(parts, name):
    half, width = parts.shape[1:]
    tile, n_tiles = _row_tile(half)

    def body(c_ref, p0, p1, p2, p3, o_ref):
        f32 = lambda p: p[...].astype(F32)
        o_ref[...] = ((f32(p0) + f32(p1)) + f32(p2)) + f32(p3)

    def slab(s):
        return pl.BlockSpec((None, tile, width), lambda i, c_ref, s=s: (s, i, 0))

    return pl.pallas_call(
        body,
        grid_spec=pltpu.PrefetchScalarGridSpec(
            num_scalar_prefetch=1, grid=(n_tiles,),
            in_specs=[slab(s) for s in range(N_CHIPS)],
            out_specs=pl.BlockSpec((None, tile, width), lambda i, c_ref: (c_ref[0], i, 0))),
        out_shape=jax.ShapeDtypeStruct((2, half, width), F32),
        name=name, compiler_params=_params("arbitrary"),
    )(_core_index(), parts, parts, parts, parts)


def _share_halves(halves, name):
    def body(h_ref, out_ref, send_sem, recv_sem):
        del h_ref
        x, y, c, _ = _place()
        cp = pltpu.make_async_remote_copy(src_ref=out_ref.at[c], dst_ref=out_ref.at[c], send_sem=send_sem, recv_sem=recv_sem,
                                          device_id=(x, y, 1 - c), device_id_type=MESH)
        cp.start()
        pltpu.make_async_remote_copy(src_ref=out_ref.at[1 - c], dst_ref=out_ref.at[1 - c], send_sem=send_sem, recv_sem=recv_sem,
                                     device_id=(x, y, 1 - c), device_id_type=MESH).wait_recv()
        cp.wait_send()

    return pl.pallas_call(
        body, in_specs=[ANY], out_specs=ANY,
        out_shape=jax.ShapeDtypeStruct(halves.shape, halves.dtype),
        scratch_shapes=[pltpu.SemaphoreType.DMA] * 2,
        input_output_aliases={0: 0},
        name=name,
    )(halves)


def _reduce_finish(got, tag):
    halves = _share_halves(_sum_chips(got, "sum_" + tag), "share_" + tag)
    return halves.reshape(2 * halves.shape[1], halves.shape[2])


SMALL_ROWS = 8


def _allreduce_small(v):
    def body(v_ref, out_ref, buf, send_sems, recv_sems):
        x, y, c, _ = _place()
        buf[4 * x + 2 * y + c] = v_ref[...]
        sends = []
        for k in range(1, N_DEV):
            px = 1 - x if k & 4 else x
            py = 1 - y if k & 2 else y
            pc = 1 - c if k & 1 else c
            cp = pltpu.make_async_remote_copy(src_ref=v_ref, dst_ref=buf.at[4 * x + 2 * y + c], send_sem=send_sems.at[k - 1],
                                              recv_sem=recv_sems.at[k - 1], device_id=(px, py, pc), device_id_type=MESH)
            cp.start()
            sends.append((cp, 4 * px + 2 * py + pc))
        for k, (cp, peer) in enumerate(sends):
            pltpu.make_async_remote_copy(src_ref=v_ref, dst_ref=buf.at[peer], send_sem=send_sems.at[k], recv_sem=recv_sems.at[k],
                                         device_id=(x, y, c), device_id_type=MESH).wait_recv()
        for cp, _ in sends:
            cp.wait_send()
        total = buf[0]
        for d in range(1, N_DEV):
            total = total + buf[d]
        out_ref[...] = total

    vmem = pl.BlockSpec(memory_space=pltpu.VMEM)
    return pl.pallas_call(
        body, in_specs=[vmem], out_specs=vmem,
        out_shape=jax.ShapeDtypeStruct(v.shape, v.dtype),
        scratch_shapes=[pltpu.VMEM((N_DEV,) + v.shape, v.dtype), pltpu.SemaphoreType.DMA((N_DEV - 1,)),
                        pltpu.SemaphoreType.DMA((N_DEV - 1,))],
        name="allreduce_small",
    )(v)


MATRICES = ("w_in", "w_out", "w_xq", "w_xk", "w_xv", "w_xo", "w_up", "w_down")
VECTORS = ("g_mix", "g_xattn", "g_mem", "g_mlp", "g_final", "b_forget")
WEIGHT_ORDER = ("g_mix", "w_in", "b_forget", "w_out", "g_xattn", "g_mem", "w_xq", "w_xk", "w_xv", "w_xo",
                "g_mlp", "w_up", "w_down", "g_final")
GROUPS = {"mlp": ("w_up", "w_down"), "mid": ("w_out", "w_xq", "w_xk", "w_xv", "w_xo"), "in": ("w_in",)}
LATE = GROUPS["mid"] + GROUPS["mlp"]
W_IN_SHARD = IN_WIDTH // N_CHIPS
SHARD_ROWS = {"w_out": 256, "w_xq": 256, "w_xk": 256, "w_xv": 256, "w_xo": 256, "w_up": 1024, "w_down": 1024}
PACK_ROWS = SHARD_ROWS
W_IN_PAD = -(-W_IN_SHARD // LANES) * LANES
ADAM_ROWS = 128


def _pack(parts, names):
    return jnp.concatenate([jnp.pad(parts[n], ((0, PACK_ROWS[n] - SHARD_ROWS[n]), (0, 0))) for n in names], axis=0)


def _unpack(a, names):
    out, pos = {}, 0
    for n in names:
        out[n] = a[..., pos:pos + SHARD_ROWS[n], :]
        pos += PACK_ROWS[n]
    return out


def _full_weights(wall, names):
    cols = lambda a: a.transpose(1, 0, 2).reshape(a.shape[1], -1)
    rows = lambda a: a.reshape(-1, a.shape[-1])
    if names == GROUPS["in"]:
        return {"w_in": cols(wall[:, :, :W_IN_SHARD])}
    return {n: cols(a) if n == "w_up" else rows(a) for n, a in _unpack(wall, names).items()}


def _shard_of(g, name, s):
    if name == "w_up":
        return g[:, s * D_MODEL:(s + 1) * D_MODEL]
    n = SHARD_ROWS[name]
    return g[s * n:(s + 1) * n]


def _pad_w_in(a):
    return jnp.pad(a, [(0, 0)] * (a.ndim - 1) + [(0, W_IN_PAD - W_IN_SHARD)])


def _pack_grads(gws, names):
    if names == GROUPS["in"]:
        return _pad_w_in(gws["w_in"].reshape(D_MODEL, N_CHIPS, W_IN_SHARD).transpose(1, 0, 2))
    return jnp.stack([_pack({n: _shard_of(gws[n], n, s) for n in names}, names) for s in range(N_CHIPS)])


def kernel(x, mem, g_mix, w_in, b_forget, w_out, g_xattn, g_mem, w_xq, w_xk, w_xv, w_xo, g_mlp, w_up, w_down, g_final, loss_target, m_g_mix, m_w_in, m_b_forget, m_w_out, m_g_xattn, m_g_mem, m_w_xq, m_w_xk, m_w_xv, m_w_xo, m_g_mlp, m_w_up, m_w_down, m_g_final, v_g_mix, v_w_in, v_b_forget, v_w_out, v_g_xattn, v_g_mem, v_w_xq, v_w_xk, v_w_xv, v_w_xo, v_g_mlp, v_w_up, v_w_down, v_g_final):
    given = dict(locals())
    weights = {n: given[n] for n in WEIGHT_ORDER}
    vecs = {n: weights[n] for n in VECTORS}

    shard = {n: weights[n].astype(BF16) for n in MATRICES}
    in_started = _gather_start(_pad_w_in(shard["w_in"]), _core_index(), "gather_in_start")
    late_pack = _pack(shard, LATE)
    in_pack, in_wall = _gather_wait(in_started, late_pack, "gather_in_wait")
    in_wall = _place_own(_pass_on(in_wall, "gather_in_pass"), in_pack)
    late = _gather_start(late_pack, in_wall, "gather_late_start")
    w_in_full = _full_weights(in_wall, GROUPS["in"])["w_in"]

    def late_weights(after):
        pack, wall = _gather_wait(late, after, "gather_late_wait")
        return _full_weights(_place_own(_pass_on(wall, "gather_late_pass"), pack), LATE)

    started = {}

    swapping = {}

    def on_grads(group, gws):
        packed = gws if group == "mlp" else _pack_grads(gws, GROUPS[group])
        swapping[group] = _swap_start(packed, "swap_%s_start" % group)
        return swapping[group][4]

    def on_swapped(group, after):
        g, got = _swap_wait(swapping[group], after, "swap_%s_wait" % group)
        started[group] = _exchange_start(_add_sibling(g, got, "add_" + group), "exchange_%s_start" % group)
        return started[group][4]

    loss, grad_x, gw, gv = _local_step(x, mem, loss_target, vecs, w_in_full, late_weights, (on_grads, on_swapped))

    on_grads("in", gw)
    grads, delta, new_m, new_v = {}, {}, {}, {}

    def finish(group, after):
        got = _exchange_wait(started[group], after, "exchange_%s_wait" % group)
        done = _reduce_finish(got, group)
        for n, a in ({"w_in": done[:, :W_IN_SHARD]} if group == "in" else _unpack(done, GROUPS[group])).items():
            grads[n] = a.reshape(weights[n].shape)
            delta[n], new_m[n], new_v[n] = _adamw(weights[n], grads[n], given["m_" + n], given["v_" + n], "adamw_" + n, ADAM_ROWS)
        return new_v[GROUPS[group][-1]]

    after = finish("mlp", swapping["in"][4])
    after = finish("mid", on_swapped("in", after))

    row = lambda a: jnp.pad(a.reshape(-1), (0, D_MODEL - a.size)).reshape(1, D_MODEL)
    small = jnp.concatenate([gv[n] for n in VECTORS[:5]] + [row(gv["b_forget"][:, 0]), row(loss[0, :1]),
                             jnp.zeros((1, D_MODEL), F32)], axis=0)
    small = _allreduce_small(small)
    for k, n in enumerate(VECTORS[:5]):
        grads[n] = small[k]
    grads["b_forget"] = small[5, :N_HEADS]
    loss_total = small[6, 0]
    finish("in", after)

    stack = lambda prefix: jnp.concatenate([row(given[prefix + n]) for n in VECTORS] + [jnp.zeros((2, D_MODEL), F32)], axis=0)
    g_small = jnp.concatenate([small[:6], jnp.zeros((2, D_MODEL), F32)], axis=0)
    d, m1, v1 = _adamw(stack(""), g_small, stack("m_"), stack("v_"), "adamw_vectors", SMALL_ROWS)
    for k, n in enumerate(VECTORS):
        width = weights[n].shape[0]
        delta[n], new_m[n], new_v[n] = d[k, :width], m1[k, :width], v1[k, :width]

    return (loss_total, grad_x, *[grads[n] for n in WEIGHT_ORDER], *[delta[n] for n in WEIGHT_ORDER],
            *[new_m[n] for n in WEIGHT_ORDER], *[new_v[n] for n in WEIGHT_ORDER])
```

```python
import functools
import math

import jax
import jax.numpy as jnp
from jax import lax
from jax.experimental import pallas as pl
from jax.experimental.pallas import tpu as pltpu

F32 = jnp.float32
BF16 = jnp.bfloat16

D_MODEL = 1024
SEQ = 2048
N_MEM = 256
HEAD_DIM = 64
N_HEADS = 8
MIX_HALF = N_HEADS * HEAD_DIM
QKV_WIDTH = 6 * MIX_HALF
IN_WIDTH = QKV_WIDTH + N_HEADS
GATE_PAD = 128
BLOCK = 128
DILATIONS = (1, 4, 16)
X_HEADS = 4
X_HEAD_DIM = 256
D_FF = 4096
EPS = 1e-6
NEG = -1e30
ATT_SCALE = 1.0 / math.sqrt(HEAD_DIM)
X_SCALE = 1.0 / math.sqrt(X_HEAD_DIM)
LANES = 128
N_CHIPS = 4
N_DEV = 8

ADAM_LR = 0.001
ADAM_B1 = 0.9
ADAM_B2 = 0.999
ADAM_EPS = 1e-08
ADAM_WD = 0.01
ADAM_STEP = 10

VMEM_LIMIT = 48 * 1024 * 1024


def _params(*sem):
    return pltpu.CompilerParams(dimension_semantics=sem or None, vmem_limit_bytes=VMEM_LIMIT)


def _dot(a, b):
    return jnp.dot(a, b, preferred_element_type=F32)


def _dot_nt(a, b):
    return lax.dot_general(a, b, (((1,), (1,)), ((), ())), preferred_element_type=F32)


def _dot_tn(a, b):
    return lax.dot_general(a, b, (((0,), (0,)), ((), ())), preferred_element_type=F32)


def _dot_exact(x, e):
    hi = x.astype(BF16)
    r1 = x - hi.astype(F32)
    mid = r1.astype(BF16)
    lo = (r1 - mid.astype(F32)).astype(BF16)
    return _dot(hi, e) + _dot(mid, e) + _dot(lo, e)


def _head_mask(e):
    lane = lax.broadcasted_iota(jnp.int32, (1, LANES), 1)
    return (lane >= HEAD_DIM * e) & (lane < HEAD_DIM * (e + 1))


def _matmul(a, w, name, out_dtypes=(F32,), extras=(), epilogue=None, tm=1024, tn=1024, w_t=False, after=None):
    m, k = a.shape
    n = w.shape[0] if w_t else w.shape[1]
    tm, tn = min(tm, m), min(tn, n)
    assert m % tm == 0 and n % tn == 0, (name, a.shape, w.shape)
    n_ex = len(extras)
    order = () if after is None else (after,)

    def body(a_ref, w_ref, *rest):
        rest = rest[len(order):]
        acc = (_dot_nt if w_t else _dot)(a_ref[...], w_ref[...])
        res = (acc,) if epilogue is None else epilogue(acc, *[r[...] for r in rest[:n_ex]])
        for o_ref, r in zip(rest[n_ex:], res):
            o_ref[...] = r.astype(o_ref.dtype)

    tile = pl.BlockSpec((tm, tn), lambda i, j: (i, j))
    w_spec = pl.BlockSpec((tn, k), lambda i, j: (j, 0)) if w_t else pl.BlockSpec((k, tn), lambda i, j: (0, j))
    return pl.pallas_call(
        body, grid=(m // tm, n // tn),
        in_specs=[pl.BlockSpec((tm, k), lambda i, j: (i, 0)), w_spec] + [pl.BlockSpec(memory_space=pl.ANY)] * len(order) + [tile] * n_ex,
        out_specs=[tile] * len(out_dtypes),
        out_shape=[jax.ShapeDtypeStruct((m, n), dt) for dt in out_dtypes],
        name=name, compiler_params=_params("parallel", "arbitrary"),
    )(a, w, *order, *extras)


def _matmul_res(a, w, res, name, w_t=False):
    return _matmul(a, w, name, extras=(res,), epilogue=lambda acc, r: (r + acc,), w_t=w_t)[0]


def _matmul_tn(x, y, name, tm=1024, tn=1024, tk=2048, packed=None, after=None):
    t, m = x.shape
    _, n = y.shape
    tm, tn, tk = min(tm, m), min(tn, n), min(tk, t)
    assert m % tm == 0 and n % tn == 0 and t % tk == 0, (name, x.shape, y.shape)
    shape, place, into = packed or ((m, n), None, None)

    def body(x_ref, y_ref, *rest):
        o_ref = rest[-1]

        @pl.when(pl.program_id(2) == 0)
        def _():
            o_ref[...] = jnp.zeros_like(o_ref)

        o_ref[...] += _dot_tn(x_ref[...], y_ref[...])

    out_spec = (pl.BlockSpec((tm, tn), lambda i, j, k: (i, j)) if place is None
                else pl.BlockSpec((None, tm, tn), lambda i, j, k: place(i, j)))
    return pl.pallas_call(
        body, grid=(m // tm, n // tn, t // tk),
        in_specs=[pl.BlockSpec((tk, tm), lambda i, j, k: (k, i)), pl.BlockSpec((tk, tn), lambda i, j, k: (k, j))]
        + [pl.BlockSpec(memory_space=pl.ANY)] * ((into is not None) + (after is not None)),
        out_specs=out_spec, out_shape=jax.ShapeDtypeStruct(shape, F32),
        input_output_aliases={} if into is None else {2: 0},
        name=name, compiler_params=_params("parallel", "parallel", "arbitrary"),
    )(x, y, *(() if into is None else (into,)), *(() if after is None else (after,)))


def _rmsnorm(x, g, name, tm=512):
    t, d = x.shape
    tm = min(tm, t)

    def body(x_ref, g_ref, h_ref):
        xv = x_ref[...]
        r = lax.rsqrt(jnp.mean(xv * xv, axis=-1, keepdims=True) + EPS)
        h_ref[...] = (xv * r * g_ref[...]).astype(BF16)

    return pl.pallas_call(
        body, grid=(t // tm,),
        in_specs=[pl.BlockSpec((tm, d), lambda i: (i, 0)), pl.BlockSpec((1, d), lambda i: (0, 0))],
        out_specs=pl.BlockSpec((tm, d), lambda i: (i, 0)),
        out_shape=jax.ShapeDtypeStruct((t, d), BF16),
        name=name, compiler_params=_params("arbitrary"),
    )(x, g.reshape(1, d))


def _in_proj(x, g, w_all, name, tm=512):
    t, d = x.shape
    half = 3 * MIX_HALF

    def body(x_ref, g_ref, w_ref, h_ref, zd_ref, zf_ref, gate_ref):
        xv = x_ref[...]
        r = lax.rsqrt(jnp.mean(xv * xv, axis=-1, keepdims=True) + EPS)
        h = (xv * r * g_ref[...]).astype(BF16)
        h_ref[...] = h
        zd_ref[...] = _dot(h, w_ref[:, 0:half])
        zf_ref[...] = _dot(h, w_ref[:, half:2 * half]).astype(BF16)
        gate_ref[...] = _dot(h, w_ref[:, 2 * half:])

    row = lambda width: pl.BlockSpec((tm, width), lambda i: (i, 0))
    return pl.pallas_call(
        body, grid=(t // tm,),
        in_specs=[row(d), pl.BlockSpec((1, d), lambda i: (0, 0)), pl.BlockSpec(w_all.shape, lambda i: (0, 0))],
        out_specs=[row(d), row(half), row(half), row(GATE_PAD)],
        out_shape=[jax.ShapeDtypeStruct((t, d), BF16), jax.ShapeDtypeStruct((t, half), F32),
                   jax.ShapeDtypeStruct((t, half), BF16), jax.ShapeDtypeStruct((t, GATE_PAD), F32)],
        name=name, compiler_params=_params("arbitrary"),
    )(x, g.reshape(1, d), w_all)


def _rms_bwd_tile(xv, dh, g):
    d = xv.shape[-1]
    r = lax.rsqrt(jnp.mean(xv * xv, axis=-1, keepdims=True) + EPS)
    dyg = dh * g
    proj = jnp.sum(dyg * xv, axis=-1, keepdims=True)
    dx = r * dyg - xv * (r * r * r * (1.0 / d)) * proj
    return dx, dh * (xv * r)


def _rms_bwd(x, dh, g, dres, name, tm=512):
    t, d = x.shape
    tm = min(tm, t)
    has_res = dres is not None

    def body(x_ref, dh_ref, g_ref, *rest):
        if has_res:
            res_ref, dx_ref, dxb_ref, dg_ref = rest
        else:
            dx_ref, dxb_ref, dg_ref = rest
        dx, dg_rows = _rms_bwd_tile(x_ref[...], dh_ref[...], g_ref[...])
        if has_res:
            dx = res_ref[...] + dx
        dx_ref[...] = dx
        dxb_ref[...] = dx.astype(BF16)

        @pl.when(pl.program_id(0) == 0)
        def _():
            dg_ref[...] = jnp.zeros_like(dg_ref)

        dg_ref[...] += jnp.sum(dg_rows, axis=0, keepdims=True)

    row = pl.BlockSpec((tm, d), lambda i: (i, 0))
    vec = pl.BlockSpec((1, d), lambda i: (0, 0))
    return pl.pallas_call(
        body, grid=(t // tm,),
        in_specs=[row, row, vec] + ([row] if has_res else []),
        out_specs=[row, row, vec],
        out_shape=[jax.ShapeDtypeStruct((t, d), F32), jax.ShapeDtypeStruct((t, d), BF16), jax.ShapeDtypeStruct((1, d), F32)],
        name=name, compiler_params=_params("arbitrary"),
    )(x, dh, g.reshape(1, d), *((dres,) if has_res else ()))


def _row_dots(a_refs, w_refs, w_t):
    acc = None
    for a_ref, w_ref in zip(a_refs, w_refs):
        part = (_dot_nt if w_t else _dot)(a_ref[...], w_ref[...])
        acc = part if acc is None else acc + part
    return acc


def _row_specs(a_parts, w_parts, tm):
    specs = [pl.BlockSpec((tm, a.shape[1]), lambda i: (i, 0)) for a in a_parts]
    return specs + [pl.BlockSpec(w.shape, lambda i: (0, 0)) for w in w_parts]


def _matmul_rms_bwd(a_parts, w_parts, x, g, dres, name, tm=512, after=None):
    t, d = x.shape
    n = len(a_parts)
    order = () if after is None else (after,)

    def body(*refs):
        x_ref, g_ref, res_ref = refs[2 * n:2 * n + 3]
        dx_ref, dxb_ref, dg_ref = refs[2 * n + 3 + len(order):]
        dx, dg_rows = _rms_bwd_tile(x_ref[...], _row_dots(refs[:n], refs[n:2 * n], True), g_ref[...])
        dx = res_ref[...] + dx
        dx_ref[...] = dx
        dxb_ref[...] = dx.astype(BF16)

        @pl.when(pl.program_id(0) == 0)
        def _():
            dg_ref[...] = jnp.zeros_like(dg_ref)

        dg_ref[...] += jnp.sum(dg_rows, axis=0, keepdims=True)

    row = pl.BlockSpec((tm, d), lambda i: (i, 0))
    vec = pl.BlockSpec((1, d), lambda i: (0, 0))
    return pl.pallas_call(
        body, grid=(t // tm,),
        in_specs=_row_specs(a_parts, w_parts, tm) + [row, vec, row] + [pl.BlockSpec(memory_space=pl.ANY)] * len(order),
        out_specs=[row, row, vec],
        out_shape=[jax.ShapeDtypeStruct((t, d), F32), jax.ShapeDtypeStruct((t, d), BF16), jax.ShapeDtypeStruct((1, d), F32)],
        name=name, compiler_params=_params("arbitrary"),
    )(*a_parts, *w_parts, x, g.reshape(1, d), dres, *order)


def _loss_bwd(a, w, res, g, target, name, tm=512):
    t, d = res.shape

    def body(a_ref, w_ref, x_ref, g_ref, t_ref, loss_ref, dx_ref, dxb_ref, dg_ref):
        xv = x_ref[...] + _dot(a_ref[...], w_ref[...])
        gv = g_ref[...]
        r = lax.rsqrt(jnp.mean(xv * xv, axis=-1, keepdims=True) + EPS)
        err = xv * r * gv - t_ref[...]
        dx, dg_rows = _rms_bwd_tile(xv, err * (1.0 / d), gv)
        dx_ref[...] = dx
        dxb_ref[...] = dx.astype(BF16)

        @pl.when(pl.program_id(0) == 0)
        def _():
            dg_ref[...] = jnp.zeros_like(dg_ref)
            loss_ref[...] = jnp.zeros_like(loss_ref)

        dg_ref[...] += jnp.sum(dg_rows, axis=0, keepdims=True)
        part = jnp.sum(jnp.sum(err * err, axis=0, keepdims=True), axis=1, keepdims=True) * (0.5 / d)
        loss_ref[...] += jnp.broadcast_to(part, loss_ref.shape)

    row = pl.BlockSpec((tm, d), lambda i: (i, 0))
    vec = pl.BlockSpec((1, d), lambda i: (0, 0))
    return pl.pallas_call(
        body, grid=(t // tm,),
        in_specs=_row_specs([a], [w], tm) + [row, vec, row],
        out_specs=[pl.BlockSpec((1, LANES), lambda i: (0, 0)), row, row, vec],
        out_shape=[jax.ShapeDtypeStruct((1, LANES), F32), jax.ShapeDtypeStruct((t, d), F32),
                   jax.ShapeDtypeStruct((t, d), BF16), jax.ShapeDtypeStruct((1, d), F32)],
        name=name, compiler_params=_params("arbitrary"),
    )(a, w, res, g.reshape(1, d), target)


def _tri(upper):
    r = lax.broadcasted_iota(jnp.int32, (LANES, LANES), 0)
    c = lax.broadcasted_iota(jnp.int32, (LANES, LANES), 1)
    return jnp.where((r <= c) if upper else (r >= c), 1.0, 0.0).astype(BF16)


def _gate_fwd(gate, b_pad, n_batch, name):
    s = SEQ
    nblk = s // LANES

    def body(g_ref, b_ref, crow_ref, sg_ref):
        gz = g_ref[...] + b_ref[...]
        logf = jnp.minimum(gz, 0.0) - jnp.log(1.0 + jnp.exp(-jnp.abs(gz)))
        logf_t = logf.T
        sg_ref[...] = (1.0 / (1.0 + jnp.exp(gz))).T[0:N_HEADS]
        upper = _tri(True)
        carry = jnp.zeros((N_HEADS, 1), F32)
        for blk in range(nblk):
            seg = _dot_exact(logf_t[0:N_HEADS, blk * LANES:(blk + 1) * LANES], upper) + carry
            carry = seg[:, LANES - 1:LANES]
            crow_ref[:, blk * LANES:(blk + 1) * LANES] = seg

    return pl.pallas_call(
        body, grid=(n_batch,),
        in_specs=[pl.BlockSpec((s, GATE_PAD), lambda b: (b, 0)), pl.BlockSpec((1, GATE_PAD), lambda b: (0, 0))],
        out_specs=[pl.BlockSpec((None, N_HEADS, s), lambda b: (b, 0, 0)),
                   pl.BlockSpec((None, N_HEADS, s), lambda b: (b, 0, 0))],
        out_shape=[jax.ShapeDtypeStruct((n_batch, N_HEADS, s), F32),
                   jax.ShapeDtypeStruct((n_batch, N_HEADS, s), F32)],
        name=name, compiler_params=_params("arbitrary"),
    )(gate, b_pad)


def _gate_bwd(dc, sg, name):
    n_batch, _, s = dc.shape
    nblk = s // LANES

    def body(dc_ref, sg_ref, dz_ref, db_ref, dt_ref):
        lower = _tri(False)
        dcv = dc_ref[...]
        carry = jnp.zeros((N_HEADS, 1), F32)
        dt_ref[...] = jnp.zeros_like(dt_ref)
        for blk in reversed(range(nblk)):
            seg = _dot_exact(dcv[:, blk * LANES:(blk + 1) * LANES], lower) + carry
            carry = seg[:, 0:1]
            dt_ref[0:N_HEADS, blk * LANES:(blk + 1) * LANES] = seg * sg_ref[:, blk * LANES:(blk + 1) * LANES]
        dg_t = dt_ref[...]
        dz_ref[...] = dg_t.T.astype(BF16)

        @pl.when(pl.program_id(0) == 0)
        def _():
            db_ref[...] = jnp.zeros_like(db_ref)

        db_ref[...] += jnp.broadcast_to(jnp.sum(dg_t[0:N_HEADS], axis=1, keepdims=True), db_ref.shape)

    return pl.pallas_call(
        body, grid=(n_batch,),
        in_specs=[pl.BlockSpec((None, N_HEADS, s), lambda b: (b, 0, 0)), pl.BlockSpec((None, N_HEADS, s), lambda b: (b, 0, 0))],
        out_specs=[pl.BlockSpec((s, GATE_PAD), lambda b: (b, 0)), pl.BlockSpec((N_HEADS, LANES), lambda b: (0, 0))],
        out_shape=[jax.ShapeDtypeStruct((n_batch * s, GATE_PAD), BF16), jax.ShapeDtypeStruct((N_HEADS, LANES), F32)],
        scratch_shapes=[pltpu.VMEM((LANES, s), F32)],
        name=name, compiler_params=_params("arbitrary"),
    )(dc, sg)


FOX_BQ = 512
FOX_BK = 512
FOX_STRIP = 512
PAIR_WIDTH = 3 * LANES
N_PAIRS = N_HEADS // 2


def _pair_major(w):
    return w.reshape(w.shape[0], 3, N_PAIRS, LANES).transpose(0, 2, 1, 3).reshape(w.shape[0], 3 * MIX_HALF)


def _pair_major_inv(w):
    return w.reshape(w.shape[0], N_PAIRS, 3, LANES).transpose(0, 2, 1, 3).reshape(w.shape[0], 3 * MIX_HALF)


def _causal(i, j, bq, bk):
    qpos = i * bq + lax.broadcasted_iota(jnp.int32, (bq, 1), 0)
    kpos = j * bk + lax.broadcasted_iota(jnp.int32, (1, bk), 1)
    return kpos <= qpos


def _split_bf16(p):
    hi = p.astype(BF16)
    return hi, (p - hi.astype(F32)).astype(BF16)


def _fox_fwd(zf, c_row, n_batch, name):
    s, bq, bk = SEQ, FOX_BQ, FOX_BK
    nq = s // bq
    t = n_batch * s

    n_strip = bq // FOX_STRIP

    def body(q_ref, k_ref, v_ref, cr_ref, o_ref, o32_ref, lse_ref):
        hp = pl.program_id(1)
        strips = [slice(r * FOX_STRIP, (r + 1) * FOX_STRIP) for r in range(n_strip)]
        chains = [(e, r) for e in range(2) for r in range(n_strip)]
        qh = {}
        for e, r in chains:
            q = q_ref[strips[r], :] * ATT_SCALE
            qh[e, r] = jnp.where(_head_mask(e), q, jnp.zeros_like(q))

        def step(i, j, carry, masked):
            rows = pl.ds(j * bk, bk)
            kj, vj = k_ref[rows, :], v_ref[rows, :]
            ck = [cr_ref[pl.ds(2 * hp + e, 1), rows] for e in range(2)]
            out = []
            scores = [_dot_nt(qh[e, r], kj) for e, r in chains]
            for n, (e, r) in enumerate(chains):
                m, l, acc = carry[3 * n:3 * n + 3]
                sc = scores[n] - ck[e]
                if masked:
                    qpos = i * bq + r * FOX_STRIP + lax.broadcasted_iota(jnp.int32, (FOX_STRIP, 1), 0)
                    kpos = j * bk + lax.broadcasted_iota(jnp.int32, (1, bk), 1)
                    sc = jnp.where(kpos <= qpos, sc, NEG)
                m_new = jnp.maximum(m, jnp.max(sc, axis=1, keepdims=True))
                alpha = jnp.exp(m - m_new)
                p = jnp.exp(sc - m_new)
                p_hi, p_lo = _split_bf16(p)
                out += [m_new, alpha * l + jnp.sum(p, axis=1, keepdims=True), alpha * acc + (_dot(p_hi, vj) + _dot(p_lo, vj))]
            return tuple(out)

        def run(i):
            carry = (jnp.full((FOX_STRIP, 1), NEG, F32), jnp.zeros((FOX_STRIP, 1), F32), jnp.zeros((FOX_STRIP, LANES), F32)) * len(chains)
            n_clear = (i * bq) // bk
            for j in range((i * bq + bq + bk - 1) // bk):
                carry = step(i, j, carry, masked=j >= n_clear)
            for r in range(n_strip):
                outs = [carry[3 * (e * n_strip + r) + 2] / carry[3 * (e * n_strip + r) + 1] for e in range(2)]
                lses = [carry[3 * (e * n_strip + r)] + jnp.log(carry[3 * (e * n_strip + r) + 1]) for e in range(2)]
                o = jnp.where(_head_mask(0), outs[0], outs[1])
                o_ref[strips[r], :] = o.astype(BF16)
                o32_ref[strips[r], :] = o
                lse_ref[strips[r], :] = jnp.where(_head_mask(0), lses[0], lses[1])

        for k in range(nq):
            pl.when(pl.program_id(2) == k)(functools.partial(run, k))

    def col(c0):
        return lambda b, hp, i: (b, 3 * hp + c0)

    blk = pl.BlockSpec((bq, LANES), lambda b, hp, i: (b * nq + i, hp))
    return pl.pallas_call(
        body, grid=(n_batch, N_PAIRS, nq),
        in_specs=[pl.BlockSpec((bq, LANES), lambda b, hp, i: (b * nq + i, 3 * hp)),
                  pl.BlockSpec((s, LANES), col(1)), pl.BlockSpec((s, LANES), col(2)),
                  pl.BlockSpec((None, N_HEADS, s), lambda b, hp, i: (b, 0, 0))],
        out_specs=[blk, blk, blk],
        out_shape=[jax.ShapeDtypeStruct((t, MIX_HALF), BF16), jax.ShapeDtypeStruct((t, MIX_HALF), F32),
                   jax.ShapeDtypeStruct((t, MIX_HALF), F32)],
        name=name, compiler_params=_params("parallel", "parallel", "arbitrary"),
    )(zf, zf, zf, c_row)


def _fox_bwd(zf, o32, dy, lse, c_row, dz, n_batch, name):
    s, bq, bk = SEQ, FOX_BQ, FOX_BK
    nq, nk = s // bq, s // bk

    def body(q_ref, k_ref, v_ref, o_ref, do_ref, lse_ref, cr_ref, dz_in, dz_ref, dc_ref, dq_acc):
        del dz_in
        hp = pl.program_id(1)

        @pl.when(pl.program_id(2) == 0)
        def _():
            dq_acc[...] = jnp.zeros_like(dq_acc)

        kj, vj = k_ref[...], v_ref[...]
        km = [jnp.where(_head_mask(e), kj, jnp.zeros_like(kj)) for e in range(2)]

        def step(i, j, ck, carry, masked):
            rows = pl.ds(i * bq, bq)
            qi, doi = q_ref[rows, :] * ATT_SCALE, do_ref[rows, :]
            prod = doi.astype(F32) * o_ref[rows, :]
            out = []
            dq = jnp.zeros((bq, LANES), F32)
            for e in range(2):
                dk_a, dv_a, dc_a = carry[3 * e:3 * e + 3]
                mask = _head_mask(e)
                lane0 = HEAD_DIM * e
                dom = jnp.where(mask, doi, jnp.zeros_like(doi))
                delta = jnp.sum(jnp.where(mask, prod, 0.0), axis=1, keepdims=True)
                sc = _dot_nt(qi, km[e]) - ck[e]
                if masked:
                    sc = jnp.where(_causal(i, j, bq, bk), sc, NEG)
                p = jnp.exp(sc - lse_ref[rows, lane0:lane0 + 1])
                ds = p * (_dot_nt(dom, vj) - delta)
                dsb = ds.astype(BF16)
                dq = dq + _dot(dsb, km[e])
                out += [dk_a + _dot_tn(dsb, qi), dv_a + _dot_tn(p.astype(BF16), dom), dc_a - jnp.sum(ds, axis=0, keepdims=True)]
            dq_acc[rows, :] += dq * ATT_SCALE
            return tuple(out)

        def run(j):
            cols = pl.ds(j * bk, bk)
            ck = [cr_ref[pl.ds(2 * hp + e, 1), cols] for e in range(2)]
            carry = (jnp.zeros((bk, LANES), F32), jnp.zeros((bk, LANES), F32), jnp.zeros((1, bk), F32)) * 2
            n_diag = (j * bk + bk + bq - 1) // bq
            for i in range((j * bk) // bq, nq):
                carry = step(i, j, ck, carry, masked=i < n_diag)
            for e in range(2):
                dc_ref[e:e + 1, :] = carry[3 * e + 2]
            dz_ref[cols, LANES:2 * LANES] = jnp.where(_head_mask(0), carry[0], carry[3]).astype(BF16)
            dz_ref[cols, 2 * LANES:3 * LANES] = (carry[1] + carry[4]).astype(BF16)
            if j == nk - 1:
                dz_ref[:, 0:LANES] = dq_acc[...].astype(BF16)

        for k in range(nk):
            pl.when(pl.program_id(2) == k)(functools.partial(run, k))

    def seq(idx):
        return pl.BlockSpec((s, LANES), lambda b, hp, j: (b, idx(hp)))

    def kblk(c0):
        return pl.BlockSpec((bk, LANES), lambda b, hp, j: (b * nk + j, 3 * hp + c0))

    return pl.pallas_call(
        body, grid=(n_batch, N_PAIRS, nk),
        in_specs=[seq(lambda hp: 3 * hp), kblk(1), kblk(2), seq(lambda hp: hp), seq(lambda hp: N_PAIRS + hp),
                  seq(lambda hp: hp),
                  pl.BlockSpec((None, N_HEADS, s), lambda b, hp, j: (b, 0, 0)), pl.BlockSpec(memory_space=pl.ANY)],
        out_specs=[pl.BlockSpec((s, PAIR_WIDTH), lambda b, hp, j: (b, N_PAIRS + hp)),
                   pl.BlockSpec((None, None, 2, bk), lambda b, hp, j: (b, hp, 0, j))],
        out_shape=[jax.ShapeDtypeStruct(dz.shape, dz.dtype), jax.ShapeDtypeStruct((n_batch, N_PAIRS, 2, s), F32)],
        scratch_shapes=[pltpu.VMEM((s, LANES), F32)],
        input_output_aliases={7: 0},
        name=name, compiler_params=_params("parallel", "parallel", "arbitrary"),
    )(zf, zf, zf, o32, dy, lse, c_row, dz)


def _dil_bias(slope, dil):
    qi = lax.broadcasted_iota(jnp.int32, (BLOCK, 2 * BLOCK), 0)
    kj = lax.broadcasted_iota(jnp.int32, (BLOCK, 2 * BLOCK), 1)
    delta = qi + BLOCK - kj
    return jnp.where((delta >= 0) & (delta <= BLOCK), (-slope * dil) * delta.astype(F32), NEG)


def _alibi_slope(hp, e):
    slope = jnp.float32(0.0)
    for k in range(N_PAIRS):
        slope = jnp.where(hp == k, jnp.float32(2.0 ** -(2 * k + e + 1)), slope)
    return slope


def _first_block_bias(bias):
    return jnp.where(lax.broadcasted_iota(jnp.int32, bias.shape, 1) < BLOCK, NEG, bias)


def _fill_bias(bias_scr, hp):
    for di, dil in enumerate(DILATIONS):
        for e in range(2):
            bias_scr[2 * di + e] = _dil_bias(_alibi_slope(hp, e), dil)


def _pair_specs(rows):
    return [pl.BlockSpec((rows, LANES), lambda b, hp, c0=c0: (b, 3 * hp + c0)) for c0 in range(3)]


def _strided(start, size, dil):
    return pl.ds(start, size) if dil == 1 else pl.ds(start, size, stride=dil)


QUARTER = SEQ // 4


def _to_quarters(src, dst):
    for r in range(4):
        dst[r * QUARTER:(r + 1) * QUARTER, :] = src[pl.ds(r, QUARTER, stride=4), :]


def _from_quarters(src, dst):
    for r in range(4):
        dst[pl.ds(r, QUARTER, stride=4), :] = src[r * QUARTER:(r + 1) * QUARTER, :]


def _mix_weights(l1, l2, l3):
    m = jnp.maximum(jnp.maximum(l1, l2), l3)
    e1, e2, e3 = jnp.exp(l1 - m), jnp.exp(l2 - m), jnp.exp(l3 - m)
    inv = 1.0 / (e1 + e2 + e3)
    return e1 * inv, e2 * inv, e3 * inv


def _dil_fwd(zd, n_batch, name):
    s = SEQ
    t = n_batch * s

    def body(q_ref, k_ref, v_ref, y_ref, l1_ref, l2_ref, l3_ref, o_scr, qkv4, o4, l4, bias_scr):
        _fill_bias(bias_scr, pl.program_id(1))
        for a, ref in enumerate((q_ref, k_ref, v_ref)):
            _to_quarters(ref, qkv4.at[a])

        def unit(srcs, start, first, stride, di, o_dst, l_dst):
            qrows = _strided(start, BLOCK, stride)
            krows = qrows if first else _strided(start - BLOCK * stride, 2 * BLOCK, stride)
            q = (srcs[0][qrows, :] * ATT_SCALE).astype(BF16)
            kc = srcs[1][krows, :].astype(BF16)
            vc = srcs[2][krows, :].astype(BF16)
            if first:
                kc, vc = jnp.concatenate([kc, kc]), jnp.concatenate([vc, vc])
            outs, lses = [], []
            for e in range(2):
                bias = _first_block_bias(bias_scr[2 * di + e]) if first else bias_scr[2 * di + e]
                sc = _dot_nt(jnp.where(_head_mask(e), q, jnp.zeros_like(q)), kc) + bias
                m = jnp.max(sc, axis=1, keepdims=True)
                pe = jnp.exp(sc - m)
                l = jnp.sum(pe, axis=1, keepdims=True)
                outs.append(_dot((pe * (1.0 / l)).astype(BF16), vc))
                lses.append(m + jnp.log(l))
            o_dst[qrows, :] = jnp.where(_head_mask(0), outs[0], outs[1])
            l_dst[qrows, :] = jnp.where(_head_mask(0), lses[0], lses[1])

        for n in range(SEQ // BLOCK):
            unit((q_ref, k_ref, v_ref), n * BLOCK, n == 0, 1, 0, o_scr.at[0], l1_ref)
        quarters = tuple(qkv4.at[a] for a in range(3))
        for di in (1, 2):
            stride = DILATIONS[di] // 4
            for r in range(4):
                for g in range(stride):
                    for n in range(QUARTER // (BLOCK * stride)):
                        unit(quarters, r * QUARTER + n * BLOCK * stride + g, n == 0, stride, di, o4.at[di - 1], l4.at[di - 1])
        for di, l_ref in ((1, l2_ref), (2, l3_ref)):
            _from_quarters(o4.at[di - 1], o_scr.at[di])
            _from_quarters(l4.at[di - 1], l_ref)
        w = _mix_weights(l1_ref[...], l2_ref[...], l3_ref[...])
        y_ref[...] = (w[0] * o_scr[0] + w[1] * o_scr[1] + w[2] * o_scr[2]).astype(BF16)

    blk = pl.BlockSpec((s, LANES), lambda b, hp: (b, hp))
    res = pl.pallas_call(
        body, grid=(n_batch, N_PAIRS),
        in_specs=_pair_specs(s),
        out_specs=[blk] * 4,
        out_shape=[jax.ShapeDtypeStruct((t, MIX_HALF), BF16)] + [jax.ShapeDtypeStruct((t, MIX_HALF), F32)] * 3,
        scratch_shapes=[pltpu.VMEM((3, s, LANES), F32), pltpu.VMEM((3, s, LANES), F32), pltpu.VMEM((2, s, LANES), F32),
                        pltpu.VMEM((2, s, LANES), F32), pltpu.VMEM((6, BLOCK, 2 * BLOCK), F32)],
        name=name, compiler_params=_params("parallel", "arbitrary"),
    )(zd, zd, zd)
    return res[0], res[1:]


def _dil_bwd(zd, dy, ya, lses, n_batch, name):
    s = SEQ
    t = n_batch * s

    def body(q_ref, k_ref, v_ref, dy_ref, ya_ref, l1_ref, l2_ref, l3_ref, dz_ref, w_scr, dy_scr, dot_scr, acc, st4, acc4, bias_scr):
        for di, dil in enumerate(DILATIONS):
            bias_scr[di] = jnp.concatenate([_dil_bias(_alibi_slope(pl.program_id(1), e), dil) for e in range(2)])
        for di, w in enumerate(_mix_weights(l1_ref[...], l2_ref[...], l3_ref[...])):
            w_scr[di] = w
        dya = dy_ref[...].astype(F32)
        prod = dya * ya_ref[...].astype(F32)
        per_head = [jnp.sum(jnp.where(_head_mask(e), prod, 0.0), axis=1, keepdims=True) for e in range(2)]
        dy_scr[...] = dya
        dot_scr[...] = jnp.where(_head_mask(0), per_head[0], per_head[1])
        acc[...] = jnp.zeros_like(acc)
        acc4[...] = jnp.zeros_like(acc4)
        staged = (q_ref, k_ref, v_ref, w_scr.at[1], w_scr.at[2], l2_ref, l3_ref, dy_scr, dot_scr)
        for a, ref in enumerate(staged):
            _to_quarters(ref, st4.at[a])

        def unit(srcs, dst, start, first, stride, di):
            qrows = _strided(start, BLOCK, stride)
            krows = qrows if first else _strided(start - BLOCK * stride, 2 * BLOCK, stride)
            q = (srcs[0][qrows, :] * ATT_SCALE).astype(BF16)
            kc = srcs[1][krows, :].astype(BF16)
            vc = srcs[2][krows, :].astype(BF16)
            wq = srcs[3][qrows, :]
            lse = srcs[4][qrows, :]
            do = (wq * srcs[5][qrows, :]).astype(BF16)
            sub = wq * srcs[6][qrows, :]
            heads = lambda a: jnp.concatenate([jnp.where(_head_mask(e), a, jnp.zeros_like(a)) for e in range(2)])
            column = lambda a: jnp.concatenate([a[:, HEAD_DIM * e:HEAD_DIM * e + 1] for e in range(2)])
            qq, dd = heads(q), heads(do)
            bias = bias_scr[di]
            p = jnp.exp(_dot_nt(qq, kc) + (bias[:, BLOCK:] if first else bias) - column(lse))
            dsb = (p * (_dot_nt(dd, vc) - column(sub))).astype(BF16)
            dq = _dot(jnp.concatenate([dsb[:BLOCK], dsb[BLOCK:]], axis=1), heads(kc))
            dst.at[0][qrows, :] += dq * ATT_SCALE
            dst.at[1][krows, :] += _dot_tn(dsb, qq)
            dst.at[2][krows, :] += _dot_tn(p.astype(BF16), dd)

        token_order = (q_ref, k_ref, v_ref, w_scr.at[0], l1_ref, dy_scr, dot_scr)
        for n in range(SEQ // BLOCK):
            unit(token_order, acc, n * BLOCK, n == 0, 1, 0)
        for di in (1, 2):
            quarters = (st4.at[0], st4.at[1], st4.at[2], st4.at[2 + di], st4.at[4 + di], st4.at[7], st4.at[8])
            stride = DILATIONS[di] // 4
            for r in range(4):
                for g in range(stride):
                    for n in range(QUARTER // (BLOCK * stride)):
                        unit(quarters, acc4, r * QUARTER + n * BLOCK * stride + g, n == 0, stride, di)
        for k in range(3):
            for r in range(4):
                acc.at[k][pl.ds(r, QUARTER, stride=4), :] += acc4[k, r * QUARTER:(r + 1) * QUARTER, :]
            dz_ref[:, k * LANES:(k + 1) * LANES] = acc[k].astype(BF16)

    blk = pl.BlockSpec((s, LANES), lambda b, hp: (b, hp))
    pair = pl.BlockSpec((s, PAIR_WIDTH), lambda b, hp: (b, hp))
    return pl.pallas_call(
        body, grid=(n_batch, N_PAIRS),
        in_specs=_pair_specs(s) + [blk] * 5,
        out_specs=pair,
        out_shape=jax.ShapeDtypeStruct((t, 2 * 3 * MIX_HALF), BF16),
        scratch_shapes=[pltpu.VMEM((3, s, LANES), F32), pltpu.VMEM((s, LANES), F32), pltpu.VMEM((s, LANES), F32),
                        pltpu.VMEM((3, s, LANES), F32), pltpu.VMEM((9, s, LANES), F32), pltpu.VMEM((3, s, LANES), F32),
                        pltpu.VMEM((3, 2 * BLOCK, 2 * BLOCK), F32)],
        name=name, compiler_params=_params("parallel", "arbitrary"),
    )(zd, zd, zd, dy, ya, *lses)


def _xattn_probs(q, k):
    sc = _dot_nt(q, k) * X_SCALE
    pe = jnp.exp(sc - jnp.max(sc, axis=1, keepdims=True))
    return pe / jnp.sum(pe, axis=1, keepdims=True)


def _rms(xv, g):
    return (xv * lax.rsqrt(jnp.mean(xv * xv, axis=-1, keepdims=True) + EPS) * g).astype(BF16)


def _out_xattn_fwd(ya, yf, x0, w_out, g_xattn, w_xq, kx, vx, w_xo, g_mlp, name, tm=512):
    t, d = x0.shape
    per_example = SEQ // tm

    def body(ya_ref, yf_ref, x0_ref, wa_ref, wf_ref, g2_ref, wq_ref, k_ref, v_ref, wo_ref, g3_ref,
             x1_ref, h2_ref, q_ref, o_ref, x2_ref, h3_ref):
        x1 = x0_ref[...] + (_dot(ya_ref[...], wa_ref[...]) + _dot(yf_ref[...], wf_ref[...]))
        x1_ref[...] = x1
        h2 = _rms(x1, g2_ref[...])
        h2_ref[...] = h2
        q = _dot(h2, wq_ref[...]).astype(BF16)
        q_ref[...] = q
        for h in range(X_HEADS):
            cols = slice(h * X_HEAD_DIM, (h + 1) * X_HEAD_DIM)
            p = _xattn_probs(q[:, cols], k_ref[:, cols])
            o_ref[:, cols] = _dot(p.astype(BF16), v_ref[:, cols]).astype(BF16)
        x2 = x1 + _dot(o_ref[...], wo_ref[...])
        x2_ref[...] = x2
        h3_ref[...] = _rms(x2, g3_ref[...])

    row = lambda width: pl.BlockSpec((tm, width), lambda i: (i, 0))
    whole = lambda a: pl.BlockSpec(a.shape, lambda i: (0, 0))
    mem = pl.BlockSpec((N_MEM, d), lambda i: (i // per_example, 0))
    vec = pl.BlockSpec((1, d), lambda i: (0, 0))
    w_a, w_f = w_out[:MIX_HALF], w_out[MIX_HALF:]
    return pl.pallas_call(
        body, grid=(t // tm,),
        in_specs=[row(MIX_HALF), row(MIX_HALF), row(d), whole(w_a), whole(w_f), vec, whole(w_xq), mem, mem, whole(w_xo), vec],
        out_specs=[row(d)] * 6,
        out_shape=[jax.ShapeDtypeStruct((t, d), dt) for dt in (F32, BF16, BF16, BF16, F32, BF16)],
        name=name, compiler_params=_params("arbitrary"),
    )(ya, yf, x0, w_a, w_f, g_xattn.reshape(1, d), w_xq, kx, vx, w_xo, g_mlp.reshape(1, d))


def _xattn_chain_bwd(dx2, dx2b, x1, g_xattn, qx, kx, vx, w_xo, w_xq, name, tm=512, after=None):
    t, d = x1.shape
    per_example = SEQ // tm
    order = () if after is None else (after,)

    def body(dx2_ref, dx2b_ref, x1_ref, g_ref, q_ref, k_ref, v_ref, wo_ref, wq_ref, *rest):
        dx1_ref, dx1b_ref, dg_ref, dq_ref, dk_ref, dv_ref, dk_acc, dv_acc = rest[len(order):]
        i = pl.program_id(0)

        @pl.when(i % per_example == 0)
        def _():
            dk_acc[...] = jnp.zeros_like(dk_acc)
            dv_acc[...] = jnp.zeros_like(dv_acc)

        do = _dot_nt(dx2b_ref[...], wo_ref[...]).astype(BF16)
        for h in range(X_HEADS):
            cols = slice(h * X_HEAD_DIM, (h + 1) * X_HEAD_DIM)
            q, k, do_h = q_ref[:, cols], k_ref[:, cols], do[:, cols]
            p = _xattn_probs(q, k)
            dp = _dot_nt(do_h, v_ref[:, cols])
            dsb = (p * (dp - jnp.sum(p * dp, axis=1, keepdims=True))).astype(BF16)
            dq_ref[:, cols] = (_dot(dsb, k) * X_SCALE).astype(BF16)
            dk_acc[:, cols] += _dot_tn(dsb, q) * X_SCALE
            dv_acc[:, cols] += _dot_tn(p.astype(BF16), do_h)

        @pl.when(i % per_example == per_example - 1)
        def _():
            dk_ref[...] = dk_acc[...].astype(BF16)
            dv_ref[...] = dv_acc[...].astype(BF16)

        dx, dg_rows = _rms_bwd_tile(x1_ref[...], _dot_nt(dq_ref[...], wq_ref[...]), g_ref[...])
        dx = dx2_ref[...] + dx
        dx1_ref[...] = dx
        dx1b_ref[...] = dx.astype(BF16)

        @pl.when(i == 0)
        def _():
            dg_ref[...] = jnp.zeros_like(dg_ref)

        dg_ref[...] += jnp.sum(dg_rows, axis=0, keepdims=True)

    row = pl.BlockSpec((tm, d), lambda i: (i, 0))
    vec = pl.BlockSpec((1, d), lambda i: (0, 0))
    mem = pl.BlockSpec((N_MEM, d), lambda i: (i // per_example, 0))
    whole = lambda a: pl.BlockSpec(a.shape, lambda i: (0, 0))
    return pl.pallas_call(
        body, grid=(t // tm,),
        in_specs=[row, row, row, vec, row, mem, mem, whole(w_xo), whole(w_xq)] + [pl.BlockSpec(memory_space=pl.ANY)] * len(order),
        out_specs=[row, row, vec, row, mem, mem],
        out_shape=[jax.ShapeDtypeStruct((t, d), F32), jax.ShapeDtypeStruct((t, d), BF16), jax.ShapeDtypeStruct((1, d), F32),
                   jax.ShapeDtypeStruct((t, d), BF16), jax.ShapeDtypeStruct(kx.shape, BF16), jax.ShapeDtypeStruct(kx.shape, BF16)],
        scratch_shapes=[pltpu.VMEM((N_MEM, d), F32)] * 2,
        name=name, compiler_params=_params("arbitrary"),
    )(dx2, dx2b, x1, g_xattn.reshape(1, d), qx, kx, vx, w_xo, w_xq, *order)


def _adamw(w, g, m, v, name, rows, g_row0=0):
    r, c = w.shape
    assert r % rows == 0 and g_row0 % rows == 0, (name, w.shape, rows, g_row0)

    def body(w_ref, g_ref, m_ref, v_ref, g_out_ref, d_ref, nm_ref, nv_ref):
        gv = g_ref[:, :c]
        g_out_ref[...] = gv
        m1 = ADAM_B1 * m_ref[...] + (1.0 - ADAM_B1) * gv
        v1 = ADAM_B2 * v_ref[...] + (1.0 - ADAM_B2) * jnp.square(gv)
        m_hat = m1 / (1.0 - ADAM_B1 ** ADAM_STEP)
        v_hat = v1 / (1.0 - ADAM_B2 ** ADAM_STEP)
        d_ref[...] = -ADAM_LR * (m_hat / (jnp.sqrt(v_hat) + ADAM_EPS) + ADAM_WD * w_ref[...])
        nm_ref[...] = m1
        nv_ref[...] = v1

    blk = pl.BlockSpec((rows, c), lambda i: (i, 0))
    g_blk = pl.BlockSpec((rows, g.shape[1]), lambda i: (i + g_row0 // rows, 0))
    return pl.pallas_call(
        body, grid=(r // rows,), in_specs=[blk, g_blk, blk, blk], out_specs=[blk] * 4,
        out_shape=[jax.ShapeDtypeStruct((r, c), F32)] * 4,
        name=name, compiler_params=_params("arbitrary"),
    )(w, g, m, v)


def _relu2(acc):
    a = jnp.maximum(acc, 0.0)
    return acc, a * a


def _relu2_bwd(acc, u):
    return (2.0 * jnp.maximum(u.astype(F32), 0.0) * acc,)


def _local_step(x, mem, target, vecs, w_in, late_weights, hooks=None):
    n_batch = x.shape[0]
    t = n_batch * SEQ
    x0 = x.reshape(t, D_MODEL)
    mem2 = mem.reshape(n_batch * N_MEM, D_MODEL)
    tgt = target.reshape(t, D_MODEL)

    half = 3 * MIX_HALF
    w_qkv = jnp.concatenate([_pair_major(w_in[:, :half]), _pair_major(w_in[:, half:QKV_WIDTH])], axis=1)
    w_gate = jnp.pad(w_in[:, QKV_WIDTH:], ((0, 0), (0, GATE_PAD - N_HEADS)))
    b_pad = jnp.pad(vecs["b_forget"], (0, GATE_PAD - N_HEADS)).reshape(1, GATE_PAD)

    h1, zd, zf, gate = _in_proj(x0, vecs["g_mix"], jnp.concatenate([w_qkv, w_gate], axis=1), "in_proj")
    mn = _rmsnorm(mem2, vecs["g_mem"], "norm_mem")
    c_row, sg = _gate_fwd(gate, b_pad, n_batch, "gate_fwd")
    ya, lses = _dil_fwd(zd, n_batch, "dil_fwd")
    yf, of32, lse_f = _fox_fwd(zf, c_row, n_batch, "fox_fwd")
    wts = late_weights(yf)
    w_out = wts["w_out"]
    kx = _matmul(mn, wts["w_xk"], "xk", out_dtypes=(BF16,))[0]
    vx = _matmul(mn, wts["w_xv"], "xv", out_dtypes=(BF16,))[0]
    x1, h2, qx, ox, x2, h3 = _out_xattn_fwd(ya, yf, x0, w_out, vecs["g_xattn"], wts["w_xq"], kx, vx, wts["w_xo"],
                                            vecs["g_mlp"], "out_xattn")
    u, a2 = _matmul(h3, wts["w_up"], "mlp_up", out_dtypes=(BF16, BF16), epilogue=_relu2)
    loss, dx3, dx3b, dg_final = _loss_bwd(a2, wts["w_down"], x2, vecs["g_final"], tgt, "mlp_down_loss")

    du = _matmul(dx3b, wts["w_down"], "mlp_down_bwd", out_dtypes=(BF16,), extras=(u,), epilogue=_relu2_bwd, w_t=True)[0]
    shards = (N_CHIPS, 2 * D_MODEL, D_MODEL)
    g_mlp = _matmul_tn(h3, du, "gw_up", packed=(shards, lambda i, j: (j, 0, 0), None))
    g_mlp = _matmul_tn(a2, dx3b, "gw_down", packed=(shards, lambda i, j: (i, 1, 0), g_mlp))
    gw_up = g_mlp[:, :D_MODEL].transpose(1, 0, 2).reshape(D_MODEL, D_FF)
    gw_down = g_mlp[:, D_MODEL:].reshape(D_FF, D_MODEL)
    on_grads, on_swapped = hooks or (None, None)
    token = on_grads("mlp", g_mlp) if hooks else None
    dx2, dx2b, dg_mlp = _matmul_rms_bwd([du], [wts["w_up"]], x2, vecs["g_mlp"], dx3, "mlp_up_bwd", after=token)

    gw_xo = _matmul_tn(ox, dx2b, "gw_xo")
    token = on_swapped("mlp", dx2b) if hooks else None
    dx1, dx1b, dg_xattn, dqx, dkx, dvx = _xattn_chain_bwd(dx2, dx2b, x1, vecs["g_xattn"], qx, kx, vx, wts["w_xo"],
                                                          wts["w_xq"], "xattn_chain_bwd", after=token)
    gw_xq = _matmul_tn(h2, dqx, "gw_xq")
    gw_xk = _matmul_tn(mn, dkx, "gw_xk")
    gw_xv = _matmul_tn(mn, dvx, "gw_xv")
    dmn = _matmul(dkx, wts["w_xk"], "xk_bwd", w_t=True)[0]
    dmn = _matmul_res(dvx, wts["w_xv"], dmn, "xv_bwd", w_t=True)
    _, _, dg_mem = _rms_bwd(mem2, dmn, vecs["g_mem"], None, "norm_mem_bwd")

    gw_out = jnp.concatenate([_matmul_tn(ya, dx1b, "gw_out_a"), _matmul_tn(yf, dx1b, "gw_out_f")], axis=0)
    token = on_grads("mid", dict(w_out=gw_out, w_xq=gw_xq, w_xk=gw_xk, w_xv=gw_xv, w_xo=gw_xo)) if hooks else None
    dy = _matmul(dx1b, w_out, "out_bwd", out_dtypes=(BF16,), w_t=True, after=token)[0]
    dz = _dil_bwd(zd, dy, ya, lses, n_batch, "dil_bwd")
    dz, dc = _fox_bwd(zf, of32, dy, lse_f, c_row, dz, n_batch, "fox_bwd")
    dzg, db = _gate_bwd(dc.reshape(n_batch, N_HEADS, SEQ), sg, "gate_bwd")
    token = on_swapped("mid", dz) if hooks else None
    gw_pm = _matmul_tn(h1, dz, "gw_in_qkv", after=token)
    gw_in = jnp.concatenate([_pair_major_inv(gw_pm[:, :half]), _pair_major_inv(gw_pm[:, half:]),
                             _matmul_tn(h1, dzg, "gw_in_gate")[:, :N_HEADS]], axis=1)
    dx0, _, dg_mix = _matmul_rms_bwd([dz, dzg], [w_qkv, w_gate], x0, vecs["g_mix"], dx1, "in_bwd")

    gw = dict(w_in=gw_in, w_out=gw_out, w_xq=gw_xq, w_xk=gw_xk, w_xv=gw_xv, w_xo=gw_xo, w_up=gw_up, w_down=gw_down)
    gv = dict(g_mix=dg_mix, g_xattn=dg_xattn, g_mem=dg_mem, g_mlp=dg_mlp, g_final=dg_final, b_forget=db)
    return loss, dx0.reshape(x.shape), gw, gv


MESH = pl.DeviceIdType.MESH
ANY = pl.BlockSpec(memory_space=pl.ANY)


def _place():
    x, y, c = lax.axis_index("x"), lax.axis_index("y"), lax.axis_index("c")
    other_chips = [(1 - x, y), (x, 1 - y), (1 - x, 1 - y)]
    return x, y, c, other_chips


def _my_chip():
    return 2 * lax.axis_index("x") + lax.axis_index("y")


def _halves(rows, c, align):
    half = rows // 2
    assert rows % (2 * align) == 0, rows
    return pl.ds(pl.multiple_of(c * half, align), half), pl.ds(pl.multiple_of((1 - c) * half, align), half)


def _place_own(wall, pack):
    return lax.dynamic_update_slice(wall, pack[None], (_my_chip(), 0, 0))


HBM = pl.BlockSpec(memory_space=pltpu.HBM)
SEM = pl.BlockSpec(memory_space=pltpu.SEMAPHORE)
SPLIT_COPY = pltpu.CompilerParams(has_side_effects=pltpu.SideEffectType.DATAFLOW_SIDE_EFFECTING)


def _in_hbm(a):
    return pltpu.with_memory_space_constraint(a, pltpu.HBM)


def _start_call(start, src, land_shape, after, name):
    land = lax.empty(land_shape, src.dtype)

    def body(src_ref, land_ref, after_ref, send_sems, recv_sems, src_thru, land_thru, token):
        del after_ref, src_thru, land_thru
        start(src_ref, land_ref, send_sems, recv_sems)
        token[...] = jnp.zeros_like(token)

    return pl.pallas_call(
        body, name=name,
        out_shape=(pltpu.SemaphoreType.DMA((3,)), pltpu.SemaphoreType.DMA((3,)), pltpu.HBM(src.shape, src.dtype),
                   pltpu.HBM(land_shape, src.dtype), jax.ShapeDtypeStruct((8, LANES), F32)),
        in_specs=(HBM, HBM, ANY), out_specs=(SEM, SEM, HBM, HBM, pl.BlockSpec(memory_space=pltpu.VMEM)),
        input_output_aliases={0: 2, 1: 3}, compiler_params=SPLIT_COPY,
    )(_in_hbm(src), _in_hbm(land), after)


def _wait_call(body, started, after, name):
    send_sems, recv_sems, src, land, _ = started
    return pl.pallas_call(
        body, name=name,
        out_shape=(pltpu.HBM(src.shape, src.dtype), pltpu.HBM(land.shape, land.dtype)),
        in_specs=(HBM, HBM, SEM, SEM, ANY), out_specs=(HBM, HBM),
        input_output_aliases={0: 0, 1: 1}, compiler_params=SPLIT_COPY,
    )(src, land, send_sems, recv_sems, after)


def _gather_copies(p_ref, wall_ref, send_sems, recv_sems):
    x, y, c, chips = _place()
    me = 2 * x + y
    mine, _ = _halves(p_ref.shape[0], c, 16)
    out, back = [], []
    for k, chip in enumerate(chips):
        peer = dict(send_sem=send_sems.at[k], recv_sem=recv_sems.at[k], device_id=(chip[0], chip[1], c), device_id_type=MESH)
        out.append(pltpu.make_async_remote_copy(src_ref=p_ref.at[mine], dst_ref=wall_ref.at[me, mine], **peer))
        slab = wall_ref.at[2 * chip[0] + chip[1], mine]
        back.append(pltpu.make_async_remote_copy(src_ref=slab, dst_ref=slab, **peer))
    return out, back


def _gather_start(pack, after, name):
    def start(p_ref, wall_ref, send_sems, recv_sems):
        for cp in _gather_copies(p_ref, wall_ref, send_sems, recv_sems)[0]:
            cp.start()

    return _start_call(start, pack, (N_CHIPS,) + pack.shape, after, name)


def _gather_wait(started, after, name):
    def body(p_ref, wall_ref, send_sems, recv_sems, after_ref, p_dead, wall_out):
        del after_ref, p_dead, wall_out
        out, back = _gather_copies(p_ref, wall_ref, send_sems, recv_sems)
        for cp_out, cp_back in zip(out, back):
            cp_out.wait_send()
            cp_back.wait_recv()

    return _wait_call(body, started, after, name)


def _pass_on(wall, name):
    def body(w_in_ref, out_ref, send_sems, recv_sems):
        del w_in_ref
        x, y, c, chips = _place()
        mine, theirs = _halves(wall.shape[1], c, 16)
        sends = []
        for k, chip in enumerate(chips):
            slab = out_ref.at[2 * chip[0] + chip[1]]
            peer = dict(send_sem=send_sems.at[k], recv_sem=recv_sems.at[k], device_id=(x, y, 1 - c), device_id_type=MESH)
            cp = pltpu.make_async_remote_copy(src_ref=slab.at[mine], dst_ref=slab.at[mine], **peer)
            cp.start()
            sends.append((cp, pltpu.make_async_remote_copy(src_ref=slab.at[theirs], dst_ref=slab.at[theirs], **peer)))
        for cp, back in sends:
            back.wait_recv()
            cp.wait_send()

    return pl.pallas_call(
        body, in_specs=[ANY], out_specs=ANY, out_shape=jax.ShapeDtypeStruct(wall.shape, wall.dtype),
        scratch_shapes=[pltpu.SemaphoreType.DMA((3,))] * 2, input_output_aliases={0: 0}, name=name,
    )(wall)


def _swap_halves(g, name):
    half = g.shape[1] // 2

    def body(g_ref, out_ref, send_sem, recv_sem):
        x, y, c, _ = _place()
        _, theirs = _halves(g.shape[1], c, 8)
        cp = pltpu.make_async_remote_copy(src_ref=g_ref.at[:, theirs], dst_ref=out_ref, send_sem=send_sem, recv_sem=recv_sem,
                                          device_id=(x, y, 1 - c), device_id_type=MESH)
        cp.start()
        cp.wait()

    return pl.pallas_call(
        body, in_specs=[ANY], out_specs=ANY,
        out_shape=jax.ShapeDtypeStruct((N_CHIPS, half, g.shape[2]), F32),
        scratch_shapes=[pltpu.SemaphoreType.DMA, pltpu.SemaphoreType.DMA],
        name=name,
    )(g)


def _swap_copy(g_ref, land_ref, send_sems, recv_sems):
    x, y, c, _ = _place()
    _, theirs = _halves(g_ref.shape[1], c, 8)
    return pltpu.make_async_remote_copy(src_ref=g_ref.at[:, theirs], dst_ref=land_ref, send_sem=send_sems.at[0],
                                        recv_sem=recv_sems.at[0], device_id=(x, y, 1 - c), device_id_type=MESH)


def _swap_start(g, name):
    def start(g_ref, land_ref, send_sems, recv_sems):
        _swap_copy(g_ref, land_ref, send_sems, recv_sems).start()

    return _start_call(start, g, (N_CHIPS, g.shape[1] // 2, g.shape[2]), _core_index(), name)


def _swap_wait(started, after, name):
    def body(g_ref, land_ref, send_sems, recv_sems, after_ref, g_out, land_out):
        del after_ref, g_out, land_out
        cp = _swap_copy(g_ref, land_ref, send_sems, recv_sems)
        cp.wait_send()
        cp.wait_recv()

    return _wait_call(body, started, after, name)


def _core_index():
    return lax.axis_index("c").astype(jnp.int32).reshape(1)


def _row_tile(half):
    tile = max(t for t in range(16, 1025, 16) if half % t == 0)
    return tile, half // tile


def _add_sibling(g, got, name):
    half = g.shape[1] // 2
    tile, n_tiles = _row_tile(half)

    def body(c_ref, g_ref, got_ref, o_ref):
        o_ref[...] = (g_ref[...] + got_ref[...]).astype(BF16)

    width = g.shape[2]
    blk = pl.BlockSpec((None, tile, width), lambda s, i, c_ref: (s, i, 0))
    return pl.pallas_call(
        body,
        grid_spec=pltpu.PrefetchScalarGridSpec(
            num_scalar_prefetch=1, grid=(N_CHIPS, n_tiles),
            in_specs=[pl.BlockSpec((None, tile, width), lambda s, i, c_ref: (s, c_ref[0] * n_tiles + i, 0)), blk],
            out_specs=blk),
        out_shape=jax.ShapeDtypeStruct((N_CHIPS, half, width), BF16),
        name=name, compiler_params=_params("arbitrary", "arbitrary"),
    )(_core_index(), g, got)


def _exchange_copies(p_ref, land_ref, send_sems, recv_sems):
    x, y, c, chips = _place()
    me = 2 * x + y
    out, back = [], []
    for k, chip in enumerate(chips):
        peer = dict(send_sem=send_sems.at[k], recv_sem=recv_sems.at[k], device_id=(chip[0], chip[1], c), device_id_type=MESH)
        out.append(pltpu.make_async_remote_copy(src_ref=p_ref.at[2 * chip[0] + chip[1]], dst_ref=land_ref.at[me], **peer))
        slab = land_ref.at[2 * chip[0] + chip[1]]
        back.append(pltpu.make_async_remote_copy(src_ref=slab, dst_ref=slab, **peer))
    return out, back


def _with_own(got, part):
    me = _my_chip()
    return lax.dynamic_update_slice(got, lax.dynamic_slice(part, (me, 0, 0), (1,) + part.shape[1:]), (me, 0, 0))


def _exchange_start(part, name):
    def start(p_ref, land_ref, send_sems, recv_sems):
        for cp in _exchange_copies(p_ref, land_ref, send_sems, recv_sems)[0]:
            cp.start()

    return _start_call(start, part, part.shape, _core_index(), name)


def _exchange_wait(started, after, name):
    def body(p_ref, land_ref, send_sems, recv_sems, after_ref, p_dead, land_out):
        del after_ref, p_dead, land_out
        out, back = _exchange_copies(p_ref, land_ref, send_sems, recv_sems)
        for cp_out, cp_back in zip(out, back):
            cp_out.wait_send()
            cp_back.wait_recv()

    part, got = _wait_call(body, started, after, name)
    return _with_own(got, part)


def _sum_chips(parts, name):
    half, width = parts.shape[1:]
    tile, n_tiles = _row_tile(half)

    def body(c_ref, p0, p1, p2, p3, o_ref):
        f32 = lambda p: p[...].astype(F32)
        o_ref[...] = ((f32(p0) + f32(p1)) + f32(p2)) + f32(p3)

    def slab(s):
        return pl.BlockSpec((None, tile, width), lambda i, c_ref, s=s: (s, i, 0))

    return pl.pallas_call(
        body,
        grid_spec=pltpu.PrefetchScalarGridSpec(
            num_scalar_prefetch=1, grid=(n_tiles,),
            in_specs=[slab(s) for s in range(N_CHIPS)],
            out_specs=pl.BlockSpec((None, tile, width), lambda i, c_ref: (c_ref[0], i, 0))),
        out_shape=jax.ShapeDtypeStruct((2, half, width), F32),
        name=name, compiler_params=_params("arbitrary"),
    )(_core_index(), parts, parts, parts, parts)


def _share_halves(halves, name):
    def body(h_ref, out_ref, send_sem, recv_sem):
        del h_ref
        x, y, c, _ = _place()
        cp = pltpu.make_async_remote_copy(src_ref=out_ref.at[c], dst_ref=out_ref.at[c], send_sem=send_sem, recv_sem=recv_sem,
                                          device_id=(x, y, 1 - c), device_id_type=MESH)
        cp.start()
        pltpu.make_async_remote_copy(src_ref=out_ref.at[1 - c], dst_ref=out_ref.at[1 - c], send_sem=send_sem, recv_sem=recv_sem,
                                     device_id=(x, y, 1 - c), device_id_type=MESH).wait_recv()
        cp.wait_send()

    return pl.pallas_call(
        body, in_specs=[ANY], out_specs=ANY,
        out_shape=jax.ShapeDtypeStruct(halves.shape, halves.dtype),
        scratch_shapes=[pltpu.SemaphoreType.DMA] * 2,
        input_output_aliases={0: 0},
        name=name,
    )(halves)


def _reduce_parts(g, tag):
    return _add_sibling(g, _swap_halves(g, "swap_" + tag), "add_" + tag)


def _reduce_finish(got, tag):
    halves = _share_halves(_sum_chips(got, "sum_" + tag), "share_" + tag)
    return halves.reshape(2 * halves.shape[1], halves.shape[2])


SMALL_ROWS = 8


def _allreduce_small(v):
    def body(v_ref, out_ref, buf, send_sems, recv_sems):
        x, y, c, _ = _place()
        buf[4 * x + 2 * y + c] = v_ref[...]
        sends = []
        for k in range(1, N_DEV):
            px = 1 - x if k & 4 else x
            py = 1 - y if k & 2 else y
            pc = 1 - c if k & 1 else c
            cp = pltpu.make_async_remote_copy(src_ref=v_ref, dst_ref=buf.at[4 * x + 2 * y + c], send_sem=send_sems.at[k - 1],
                                              recv_sem=recv_sems.at[k - 1], device_id=(px, py, pc), device_id_type=MESH)
            cp.start()
            sends.append((cp, 4 * px + 2 * py + pc))
        for k, (cp, peer) in enumerate(sends):
            pltpu.make_async_remote_copy(src_ref=v_ref, dst_ref=buf.at[peer], send_sem=send_sems.at[k], recv_sem=recv_sems.at[k],
                                         device_id=(x, y, c), device_id_type=MESH).wait_recv()
        for cp, _ in sends:
            cp.wait_send()
        total = buf[0]
        for d in range(1, N_DEV):
            total = total + buf[d]
        out_ref[...] = total

    vmem = pl.BlockSpec(memory_space=pltpu.VMEM)
    return pl.pallas_call(
        body, in_specs=[vmem], out_specs=vmem,
        out_shape=jax.ShapeDtypeStruct(v.shape, v.dtype),
        scratch_shapes=[pltpu.VMEM((N_DEV,) + v.shape, v.dtype), pltpu.SemaphoreType.DMA((N_DEV - 1,)),
                        pltpu.SemaphoreType.DMA((N_DEV - 1,))],
        name="allreduce_small",
    )(v)


MATRICES = ("w_in", "w_out", "w_xq", "w_xk", "w_xv", "w_xo", "w_up", "w_down")
VECTORS = ("g_mix", "g_xattn", "g_mem", "g_mlp", "g_final", "b_forget")
WEIGHT_ORDER = ("g_mix", "w_in", "b_forget", "w_out", "g_xattn", "g_mem", "w_xq", "w_xk", "w_xv", "w_xo",
                "g_mlp", "w_up", "w_down", "g_final")
GROUPS = {"mlp": ("w_up", "w_down"), "mid": ("w_out", "w_xq", "w_xk", "w_xv", "w_xo"), "in": ("w_in",)}
LATE = GROUPS["mid"] + GROUPS["mlp"]
W_IN_SHARD = IN_WIDTH // N_CHIPS
SHARD_ROWS = {"w_out": 256, "w_xq": 256, "w_xk": 256, "w_xv": 256, "w_xo": 256, "w_up": 1024, "w_down": 1024}
PACK_ROWS = SHARD_ROWS
W_IN_PAD = -(-W_IN_SHARD // LANES) * LANES
ADAM_ROWS = 128


def _pack(parts, names):
    return jnp.concatenate([jnp.pad(parts[n], ((0, PACK_ROWS[n] - SHARD_ROWS[n]), (0, 0))) for n in names], axis=0)


def _unpack(a, names):
    out, pos = {}, 0
    for n in names:
        out[n] = a[..., pos:pos + SHARD_ROWS[n], :]
        pos += PACK_ROWS[n]
    return out


def _full_weights(wall, names):
    cols = lambda a: a.transpose(1, 0, 2).reshape(a.shape[1], -1)
    rows = lambda a: a.reshape(-1, a.shape[-1])
    if names == GROUPS["in"]:
        return {"w_in": cols(wall[:, :, :W_IN_SHARD])}
    return {n: cols(a) if n == "w_up" else rows(a) for n, a in _unpack(wall, names).items()}


def _shard_of(g, name, s):
    if name == "w_up":
        return g[:, s * D_MODEL:(s + 1) * D_MODEL]
    n = SHARD_ROWS[name]
    return g[s * n:(s + 1) * n]


def _pad_w_in(a):
    return jnp.pad(a, [(0, 0)] * (a.ndim - 1) + [(0, W_IN_PAD - W_IN_SHARD)])


def _pack_grads(gws, names):
    if names == GROUPS["in"]:
        return _pad_w_in(gws["w_in"].reshape(D_MODEL, N_CHIPS, W_IN_SHARD).transpose(1, 0, 2))
    return jnp.stack([_pack({n: _shard_of(gws[n], n, s) for n in names}, names) for s in range(N_CHIPS)])


def kernel(x, mem, g_mix, w_in, b_forget, w_out, g_xattn, g_mem, w_xq, w_xk, w_xv, w_xo, g_mlp, w_up, w_down, g_final, loss_target, m_g_mix, m_w_in, m_b_forget, m_w_out, m_g_xattn, m_g_mem, m_w_xq, m_w_xk, m_w_xv, m_w_xo, m_g_mlp, m_w_up, m_w_down, m_g_final, v_g_mix, v_w_in, v_b_forget, v_w_out, v_g_xattn, v_g_mem, v_w_xq, v_w_xk, v_w_xv, v_w_xo, v_g_mlp, v_w_up, v_w_down, v_g_final):
    given = dict(locals())
    weights = {n: given[n] for n in WEIGHT_ORDER}
    vecs = {n: weights[n] for n in VECTORS}

    shard = {n: weights[n].astype(BF16) for n in MATRICES}
    in_started = _gather_start(_pad_w_in(shard["w_in"]), _core_index(), "gather_in_start")
    late_pack = _pack(shard, LATE)
    in_pack, in_wall = _gather_wait(in_started, late_pack, "gather_in_wait")
    in_wall = _place_own(_pass_on(in_wall, "gather_in_pass"), in_pack)
    late = _gather_start(late_pack, in_wall, "gather_late_start")
    w_in_full = _full_weights(in_wall, GROUPS["in"])["w_in"]

    def late_weights(after):
        pack, wall = _gather_wait(late, after, "gather_late_wait")
        return _full_weights(_place_own(_pass_on(wall, "gather_late_pass"), pack), LATE)

    started = {}

    swapping = {}

    def on_grads(group, gws):
        packed = gws if group == "mlp" else _pack_grads(gws, GROUPS[group])
        swapping[group] = _swap_start(packed, "swap_%s_start" % group)
        return swapping[group][4]

    def on_swapped(group, after):
        g, got = _swap_wait(swapping[group], after, "swap_%s_wait" % group)
        started[group] = _exchange_start(_add_sibling(g, got, "add_" + group), "exchange_%s_start" % group)
        return started[group][4]

    loss, grad_x, gw, gv = _local_step(x, mem, loss_target, vecs, w_in_full, late_weights, (on_grads, on_swapped))

    part = _reduce_parts(_pack_grads(gw, GROUPS["in"]), "in")
    started["in"] = _exchange_start(part, "exchange_in_start")
    grads, delta, new_m, new_v = {}, {}, {}, {}

    def finish(group, after):
        got = _exchange_wait(started[group], after, "exchange_%s_wait" % group)
        done = _reduce_finish(got, group)
        pos = 0
        for n in GROUPS[group]:
            grads[n], delta[n], new_m[n], new_v[n] = _adamw(
                weights[n], done, given["m_" + n], given["v_" + n], "adamw_" + n, ADAM_ROWS, pos)
            pos += weights[n].shape[0]
        return new_v[GROUPS[group][-1]]

    after = finish("mlp", started["in"][4])
    after = finish("mid", after)

    row = lambda a: jnp.pad(a.reshape(-1), (0, D_MODEL - a.size)).reshape(1, D_MODEL)
    small = jnp.concatenate([gv[n] for n in VECTORS[:5]] + [row(gv["b_forget"][:, 0]), row(loss[0, :1]),
                             jnp.zeros((1, D_MODEL), F32)], axis=0)
    small = _allreduce_small(small)
    for k, n in enumerate(VECTORS[:5]):
        grads[n] = small[k]
    grads["b_forget"] = small[5, :N_HEADS]
    loss_total = small[6, 0]
    finish("in", after)

    stack = lambda prefix: jnp.concatenate([row(given[prefix + n]) for n in VECTORS] + [jnp.zeros((2, D_MODEL), F32)], axis=0)
    g_small = jnp.concatenate([small[:6], jnp.zeros((2, D_MODEL), F32)], axis=0)
    _, d, m1, v1 = _adamw(stack(""), g_small, stack("m_"), stack("v_"), "adamw_vectors", SMALL_ROWS)
    for k, n in enumerate(VECTORS):
        width = weights[n].shape[0]
        delta[n], new_m[n], new_v[n] = d[k, :width], m1[k, :width], v1[k, :width]

    return (loss_total, grad_x, *[grads[n] for n in WEIGHT_ORDER], *[delta[n] for n in WEIGHT_ORDER],
            *[new_m[n] for n in WEIGHT_ORDER], *[new_v[n] for n in WEIGHT_ORDER])
```

```python
import functools
import math

import jax
import jax.numpy as jnp
from jax import lax
from jax.experimental import pallas as pl
from jax.experimental.pallas import tpu as pltpu

F32 = jnp.float32
BF16 = jnp.bfloat16

D_MODEL = 1024
SEQ = 2048
N_MEM = 256
HEAD_DIM = 64
N_HEADS = 8
MIX_HALF = N_HEADS * HEAD_DIM
QKV_WIDTH = 6 * MIX_HALF
IN_WIDTH = QKV_WIDTH + N_HEADS
GATE_PAD = 128
BLOCK = 128
DILATIONS = (1, 4, 16)
X_HEADS = 4
X_HEAD_DIM = 256
D_FF = 4096
EPS = 1e-6
NEG = -1e30
ATT_SCALE = 1.0 / math.sqrt(HEAD_DIM)
X_SCALE = 1.0 / math.sqrt(X_HEAD_DIM)
LANES = 128
N_CHIPS = 4
N_DEV = 8

ADAM_LR = 0.001
ADAM_B1 = 0.9
ADAM_B2 = 0.999
ADAM_EPS = 1e-08
ADAM_WD = 0.01
ADAM_STEP = 10

VMEM_LIMIT = 48 * 1024 * 1024


def _params(*sem):
    return pltpu.CompilerParams(dimension_semantics=sem or None, vmem_limit_bytes=VMEM_LIMIT)


def _dot(a, b):
    return jnp.dot(a, b, preferred_element_type=F32)


def _dot_nt(a, b):
    return lax.dot_general(a, b, (((1,), (1,)), ((), ())), preferred_element_type=F32)


def _dot_tn(a, b):
    return lax.dot_general(a, b, (((0,), (0,)), ((), ())), preferred_element_type=F32)


def _dot_exact(x, e):
    hi = x.astype(BF16)
    r1 = x - hi.astype(F32)
    mid = r1.astype(BF16)
    lo = (r1 - mid.astype(F32)).astype(BF16)
    return _dot(hi, e) + _dot(mid, e) + _dot(lo, e)


def _head_mask(e):
    lane = lax.broadcasted_iota(jnp.int32, (1, LANES), 1)
    return (lane >= HEAD_DIM * e) & (lane < HEAD_DIM * (e + 1))


def _matmul(a, w, name, out_dtypes=(F32,), extras=(), epilogue=None, tm=1024, tn=1024, w_t=False, after=None):
    m, k = a.shape
    n = w.shape[0] if w_t else w.shape[1]
    tm, tn = min(tm, m), min(tn, n)
    assert m % tm == 0 and n % tn == 0, (name, a.shape, w.shape)
    n_ex = len(extras)
    order = () if after is None else (after,)

    def body(a_ref, w_ref, *rest):
        rest = rest[len(order):]
        acc = (_dot_nt if w_t else _dot)(a_ref[...], w_ref[...])
        res = (acc,) if epilogue is None else epilogue(acc, *[r[...] for r in rest[:n_ex]])
        for o_ref, r in zip(rest[n_ex:], res):
            o_ref[...] = r.astype(o_ref.dtype)

    tile = pl.BlockSpec((tm, tn), lambda i, j: (i, j))
    w_spec = pl.BlockSpec((tn, k), lambda i, j: (j, 0)) if w_t else pl.BlockSpec((k, tn), lambda i, j: (0, j))
    return pl.pallas_call(
        body, grid=(m // tm, n // tn),
        in_specs=[pl.BlockSpec((tm, k), lambda i, j: (i, 0)), w_spec] + [pl.BlockSpec(memory_space=pl.ANY)] * len(order) + [tile] * n_ex,
        out_specs=[tile] * len(out_dtypes),
        out_shape=[jax.ShapeDtypeStruct((m, n), dt) for dt in out_dtypes],
        name=name, compiler_params=_params("parallel", "arbitrary"),
    )(a, w, *order, *extras)


def _matmul_res(a, w, res, name, w_t=False):
    return _matmul(a, w, name, extras=(res,), epilogue=lambda acc, r: (r + acc,), w_t=w_t)[0]


def _matmul_tn(x, y, name, tm=1024, tn=1024, tk=2048, packed=None, after=None):
    t, m = x.shape
    _, n = y.shape
    tm, tn, tk = min(tm, m), min(tn, n), min(tk, t)
    assert m % tm == 0 and n % tn == 0 and t % tk == 0, (name, x.shape, y.shape)
    shape, place, into = packed or ((m, n), None, None)

    def body(x_ref, y_ref, *rest):
        o_ref = rest[-1]

        @pl.when(pl.program_id(2) == 0)
        def _():
            o_ref[...] = jnp.zeros_like(o_ref)

        o_ref[...] += _dot_tn(x_ref[...], y_ref[...])

    out_spec = (pl.BlockSpec((tm, tn), lambda i, j, k: (i, j)) if place is None
                else pl.BlockSpec((None, tm, tn), lambda i, j, k: place(i, j)))
    return pl.pallas_call(
        body, grid=(m // tm, n // tn, t // tk),
        in_specs=[pl.BlockSpec((tk, tm), lambda i, j, k: (k, i)), pl.BlockSpec((tk, tn), lambda i, j, k: (k, j))]
        + [pl.BlockSpec(memory_space=pl.ANY)] * ((into is not None) + (after is not None)),
        out_specs=out_spec, out_shape=jax.ShapeDtypeStruct(shape, F32),
        input_output_aliases={} if into is None else {2: 0},
        name=name, compiler_params=_params("parallel", "parallel", "arbitrary"),
    )(x, y, *(() if into is None else (into,)), *(() if after is None else (after,)))


def _rmsnorm(x, g, name, tm=512):
    t, d = x.shape
    tm = min(tm, t)

    def body(x_ref, g_ref, h_ref):
        xv = x_ref[...]
        r = lax.rsqrt(jnp.mean(xv * xv, axis=-1, keepdims=True) + EPS)
        h_ref[...] = (xv * r * g_ref[...]).astype(BF16)

    return pl.pallas_call(
        body, grid=(t // tm,),
        in_specs=[pl.BlockSpec((tm, d), lambda i: (i, 0)), pl.BlockSpec((1, d), lambda i: (0, 0))],
        out_specs=pl.BlockSpec((tm, d), lambda i: (i, 0)),
        out_shape=jax.ShapeDtypeStruct((t, d), BF16),
        name=name, compiler_params=_params("arbitrary"),
    )(x, g.reshape(1, d))


def _in_proj(x, g, w_all, name, tm=512):
    t, d = x.shape
    half = 3 * MIX_HALF

    def body(x_ref, g_ref, w_ref, h_ref, zd_ref, zf_ref, gate_ref):
        xv = x_ref[...]
        r = lax.rsqrt(jnp.mean(xv * xv, axis=-1, keepdims=True) + EPS)
        h = (xv * r * g_ref[...]).astype(BF16)
        h_ref[...] = h
        zd_ref[...] = _dot(h, w_ref[:, 0:half])
        zf_ref[...] = _dot(h, w_ref[:, half:2 * half]).astype(BF16)
        gate_ref[...] = _dot(h, w_ref[:, 2 * half:])

    row = lambda width: pl.BlockSpec((tm, width), lambda i: (i, 0))
    return pl.pallas_call(
        body, grid=(t // tm,),
        in_specs=[row(d), pl.BlockSpec((1, d), lambda i: (0, 0)), pl.BlockSpec(w_all.shape, lambda i: (0, 0))],
        out_specs=[row(d), row(half), row(half), row(GATE_PAD)],
        out_shape=[jax.ShapeDtypeStruct((t, d), BF16), jax.ShapeDtypeStruct((t, half), F32),
                   jax.ShapeDtypeStruct((t, half), BF16), jax.ShapeDtypeStruct((t, GATE_PAD), F32)],
        name=name, compiler_params=_params("arbitrary"),
    )(x, g.reshape(1, d), w_all)


def _rms_bwd_tile(xv, dh, g):
    d = xv.shape[-1]
    r = lax.rsqrt(jnp.mean(xv * xv, axis=-1, keepdims=True) + EPS)
    dyg = dh * g
    proj = jnp.sum(dyg * xv, axis=-1, keepdims=True)
    dx = r * dyg - xv * (r * r * r * (1.0 / d)) * proj
    return dx, dh * (xv * r)


def _rms_bwd(x, dh, g, dres, name, tm=512):
    t, d = x.shape
    tm = min(tm, t)
    has_res = dres is not None

    def body(x_ref, dh_ref, g_ref, *rest):
        if has_res:
            res_ref, dx_ref, dxb_ref, dg_ref = rest
        else:
            dx_ref, dxb_ref, dg_ref = rest
        dx, dg_rows = _rms_bwd_tile(x_ref[...], dh_ref[...], g_ref[...])
        if has_res:
            dx = res_ref[...] + dx
        dx_ref[...] = dx
        dxb_ref[...] = dx.astype(BF16)

        @pl.when(pl.program_id(0) == 0)
        def _():
            dg_ref[...] = jnp.zeros_like(dg_ref)

        dg_ref[...] += jnp.sum(dg_rows, axis=0, keepdims=True)

    row = pl.BlockSpec((tm, d), lambda i: (i, 0))
    vec = pl.BlockSpec((1, d), lambda i: (0, 0))
    return pl.pallas_call(
        body, grid=(t // tm,),
        in_specs=[row, row, vec] + ([row] if has_res else []),
        out_specs=[row, row, vec],
        out_shape=[jax.ShapeDtypeStruct((t, d), F32), jax.ShapeDtypeStruct((t, d), BF16), jax.ShapeDtypeStruct((1, d), F32)],
        name=name, compiler_params=_params("arbitrary"),
    )(x, dh, g.reshape(1, d), *((dres,) if has_res else ()))


def _row_dots(a_refs, w_refs, w_t):
    acc = None
    for a_ref, w_ref in zip(a_refs, w_refs):
        part = (_dot_nt if w_t else _dot)(a_ref[...], w_ref[...])
        acc = part if acc is None else acc + part
    return acc


def _row_specs(a_parts, w_parts, tm):
    specs = [pl.BlockSpec((tm, a.shape[1]), lambda i: (i, 0)) for a in a_parts]
    return specs + [pl.BlockSpec(w.shape, lambda i: (0, 0)) for w in w_parts]


def _matmul_rms_bwd(a_parts, w_parts, x, g, dres, name, tm=512, after=None):
    t, d = x.shape
    n = len(a_parts)
    order = () if after is None else (after,)

    def body(*refs):
        x_ref, g_ref, res_ref = refs[2 * n:2 * n + 3]
        dx_ref, dxb_ref, dg_ref = refs[2 * n + 3 + len(order):]
        dx, dg_rows = _rms_bwd_tile(x_ref[...], _row_dots(refs[:n], refs[n:2 * n], True), g_ref[...])
        dx = res_ref[...] + dx
        dx_ref[...] = dx
        dxb_ref[...] = dx.astype(BF16)

        @pl.when(pl.program_id(0) == 0)
        def _():
            dg_ref[...] = jnp.zeros_like(dg_ref)

        dg_ref[...] += jnp.sum(dg_rows, axis=0, keepdims=True)

    row = pl.BlockSpec((tm, d), lambda i: (i, 0))
    vec = pl.BlockSpec((1, d), lambda i: (0, 0))
    return pl.pallas_call(
        body, grid=(t // tm,),
        in_specs=_row_specs(a_parts, w_parts, tm) + [row, vec, row] + [pl.BlockSpec(memory_space=pl.ANY)] * len(order),
        out_specs=[row, row, vec],
        out_shape=[jax.ShapeDtypeStruct((t, d), F32), jax.ShapeDtypeStruct((t, d), BF16), jax.ShapeDtypeStruct((1, d), F32)],
        name=name, compiler_params=_params("arbitrary"),
    )(*a_parts, *w_parts, x, g.reshape(1, d), dres, *order)


def _loss_bwd(a, w, res, g, target, name, tm=512):
    t, d = res.shape

    def body(a_ref, w_ref, x_ref, g_ref, t_ref, loss_ref, dx_ref, dxb_ref, dg_ref):
        xv = x_ref[...] + _dot(a_ref[...], w_ref[...])
        gv = g_ref[...]
        r = lax.rsqrt(jnp.mean(xv * xv, axis=-1, keepdims=True) + EPS)
        err = xv * r * gv - t_ref[...]
        dx, dg_rows = _rms_bwd_tile(xv, err * (1.0 / d), gv)
        dx_ref[...] = dx
        dxb_ref[...] = dx.astype(BF16)

        @pl.when(pl.program_id(0) == 0)
        def _():
            dg_ref[...] = jnp.zeros_like(dg_ref)
            loss_ref[...] = jnp.zeros_like(loss_ref)

        dg_ref[...] += jnp.sum(dg_rows, axis=0, keepdims=True)
        part = jnp.sum(jnp.sum(err * err, axis=0, keepdims=True), axis=1, keepdims=True) * (0.5 / d)
        loss_ref[...] += jnp.broadcast_to(part, loss_ref.shape)

    row = pl.BlockSpec((tm, d), lambda i: (i, 0))
    vec = pl.BlockSpec((1, d), lambda i: (0, 0))
    return pl.pallas_call(
        body, grid=(t // tm,),
        in_specs=_row_specs([a], [w], tm) + [row, vec, row],
        out_specs=[pl.BlockSpec((1, LANES), lambda i: (0, 0)), row, row, vec],
        out_shape=[jax.ShapeDtypeStruct((1, LANES), F32), jax.ShapeDtypeStruct((t, d), F32),
                   jax.ShapeDtypeStruct((t, d), BF16), jax.ShapeDtypeStruct((1, d), F32)],
        name=name, compiler_params=_params("arbitrary"),
    )(a, w, res, g.reshape(1, d), target)


def _tri(upper):
    r = lax.broadcasted_iota(jnp.int32, (LANES, LANES), 0)
    c = lax.broadcasted_iota(jnp.int32, (LANES, LANES), 1)
    return jnp.where((r <= c) if upper else (r >= c), 1.0, 0.0).astype(BF16)


def _gate_fwd(gate, b_pad, n_batch, name):
    s = SEQ
    nblk = s // LANES

    def body(g_ref, b_ref, crow_ref, sg_ref):
        gz = g_ref[...] + b_ref[...]
        logf = jnp.minimum(gz, 0.0) - jnp.log(1.0 + jnp.exp(-jnp.abs(gz)))
        logf_t = logf.T
        sg_ref[...] = (1.0 / (1.0 + jnp.exp(gz))).T[0:N_HEADS]
        upper = _tri(True)
        carry = jnp.zeros((N_HEADS, 1), F32)
        for blk in range(nblk):
            seg = _dot_exact(logf_t[0:N_HEADS, blk * LANES:(blk + 1) * LANES], upper) + carry
            carry = seg[:, LANES - 1:LANES]
            crow_ref[:, blk * LANES:(blk + 1) * LANES] = seg

    return pl.pallas_call(
        body, grid=(n_batch,),
        in_specs=[pl.BlockSpec((s, GATE_PAD), lambda b: (b, 0)), pl.BlockSpec((1, GATE_PAD), lambda b: (0, 0))],
        out_specs=[pl.BlockSpec((None, N_HEADS, s), lambda b: (b, 0, 0)),
                   pl.BlockSpec((None, N_HEADS, s), lambda b: (b, 0, 0))],
        out_shape=[jax.ShapeDtypeStruct((n_batch, N_HEADS, s), F32),
                   jax.ShapeDtypeStruct((n_batch, N_HEADS, s), F32)],
        name=name, compiler_params=_params("arbitrary"),
    )(gate, b_pad)


def _gate_bwd(dc, sg, name):
    n_batch, _, s = dc.shape
    nblk = s // LANES

    def body(dc_ref, sg_ref, dz_ref, db_ref, dt_ref):
        lower = _tri(False)
        dcv = dc_ref[...]
        carry = jnp.zeros((N_HEADS, 1), F32)
        dt_ref[...] = jnp.zeros_like(dt_ref)
        for blk in reversed(range(nblk)):
            seg = _dot_exact(dcv[:, blk * LANES:(blk + 1) * LANES], lower) + carry
            carry = seg[:, 0:1]
            dt_ref[0:N_HEADS, blk * LANES:(blk + 1) * LANES] = seg * sg_ref[:, blk * LANES:(blk + 1) * LANES]
        dg_t = dt_ref[...]
        dz_ref[...] = dg_t.T.astype(BF16)

        @pl.when(pl.program_id(0) == 0)
        def _():
            db_ref[...] = jnp.zeros_like(db_ref)

        db_ref[...] += jnp.broadcast_to(jnp.sum(dg_t[0:N_HEADS], axis=1, keepdims=True), db_ref.shape)

    return pl.pallas_call(
        body, grid=(n_batch,),
        in_specs=[pl.BlockSpec((None, N_HEADS, s), lambda b: (b, 0, 0)), pl.BlockSpec((None, N_HEADS, s), lambda b: (b, 0, 0))],
        out_specs=[pl.BlockSpec((s, GATE_PAD), lambda b: (b, 0)), pl.BlockSpec((N_HEADS, LANES), lambda b: (0, 0))],
        out_shape=[jax.ShapeDtypeStruct((n_batch * s, GATE_PAD), BF16), jax.ShapeDtypeStruct((N_HEADS, LANES), F32)],
        scratch_shapes=[pltpu.VMEM((LANES, s), F32)],
        name=name, compiler_params=_params("arbitrary"),
    )(dc, sg)


FOX_BQ = 512
FOX_BK = 512
FOX_STRIP = 512
PAIR_WIDTH = 3 * LANES
N_PAIRS = N_HEADS // 2


def _pair_major(w):
    return w.reshape(w.shape[0], 3, N_PAIRS, LANES).transpose(0, 2, 1, 3).reshape(w.shape[0], 3 * MIX_HALF)


def _pair_major_inv(w):
    return w.reshape(w.shape[0], N_PAIRS, 3, LANES).transpose(0, 2, 1, 3).reshape(w.shape[0], 3 * MIX_HALF)


def _causal(i, j, bq, bk):
    qpos = i * bq + lax.broadcasted_iota(jnp.int32, (bq, 1), 0)
    kpos = j * bk + lax.broadcasted_iota(jnp.int32, (1, bk), 1)
    return kpos <= qpos


def _split_bf16(p):
    hi = p.astype(BF16)
    return hi, (p - hi.astype(F32)).astype(BF16)


def _fox_fwd(zf, c_row, n_batch, name, after):
    s, bq, bk = SEQ, FOX_BQ, FOX_BK
    nq = s // bq
    t = n_batch * s

    n_strip = bq // FOX_STRIP

    def body(q_ref, k_ref, v_ref, cr_ref, after_ref, o_ref, o32_ref, lse_ref):
        del after_ref
        hp = pl.program_id(1)
        strips = [slice(r * FOX_STRIP, (r + 1) * FOX_STRIP) for r in range(n_strip)]
        chains = [(e, r) for e in range(2) for r in range(n_strip)]
        qh = {}
        for e, r in chains:
            q = q_ref[strips[r], :] * ATT_SCALE
            qh[e, r] = jnp.where(_head_mask(e), q, jnp.zeros_like(q))

        def step(i, j, carry, masked):
            rows = pl.ds(j * bk, bk)
            kj, vj = k_ref[rows, :], v_ref[rows, :]
            ck = [cr_ref[pl.ds(2 * hp + e, 1), rows] for e in range(2)]
            out = []
            scores = [_dot_nt(qh[e, r], kj) for e, r in chains]
            for n, (e, r) in enumerate(chains):
                m, l, acc = carry[3 * n:3 * n + 3]
                sc = scores[n] - ck[e]
                if masked:
                    qpos = i * bq + r * FOX_STRIP + lax.broadcasted_iota(jnp.int32, (FOX_STRIP, 1), 0)
                    kpos = j * bk + lax.broadcasted_iota(jnp.int32, (1, bk), 1)
                    sc = jnp.where(kpos <= qpos, sc, NEG)
                m_new = jnp.maximum(m, jnp.max(sc, axis=1, keepdims=True))
                alpha = jnp.exp(m - m_new)
                p = jnp.exp(sc - m_new)
                p_hi, p_lo = _split_bf16(p)
                out += [m_new, alpha * l + jnp.sum(p, axis=1, keepdims=True), alpha * acc + (_dot(p_hi, vj) + _dot(p_lo, vj))]
            return tuple(out)

        def run(i):
            carry = (jnp.full((FOX_STRIP, 1), NEG, F32), jnp.zeros((FOX_STRIP, 1), F32), jnp.zeros((FOX_STRIP, LANES), F32)) * len(chains)
            n_clear = (i * bq) // bk
            for j in range((i * bq + bq + bk - 1) // bk):
                carry = step(i, j, carry, masked=j >= n_clear)
            for r in range(n_strip):
                outs = [carry[3 * (e * n_strip + r) + 2] / carry[3 * (e * n_strip + r) + 1] for e in range(2)]
                lses = [carry[3 * (e * n_strip + r)] + jnp.log(carry[3 * (e * n_strip + r) + 1]) for e in range(2)]
                o = jnp.where(_head_mask(0), outs[0], outs[1])
                o_ref[strips[r], :] = o.astype(BF16)
                o32_ref[strips[r], :] = o
                lse_ref[strips[r], :] = jnp.where(_head_mask(0), lses[0], lses[1])

        for k in range(nq):
            pl.when(pl.program_id(2) == k)(functools.partial(run, k))

    def col(c0):
        return lambda b, hp, i: (b, 3 * hp + c0)

    blk = pl.BlockSpec((bq, LANES), lambda b, hp, i: (b * nq + i, hp))
    return pl.pallas_call(
        body, grid=(n_batch, N_PAIRS, nq),
        in_specs=[pl.BlockSpec((bq, LANES), lambda b, hp, i: (b * nq + i, 3 * hp)),
                  pl.BlockSpec((s, LANES), col(1)), pl.BlockSpec((s, LANES), col(2)),
                  pl.BlockSpec((None, N_HEADS, s), lambda b, hp, i: (b, 0, 0)), ANY],
        out_specs=[blk, blk, blk],
        out_shape=[jax.ShapeDtypeStruct((t, MIX_HALF), BF16), jax.ShapeDtypeStruct((t, MIX_HALF), F32),
                   jax.ShapeDtypeStruct((t, MIX_HALF), F32)],
        name=name, compiler_params=_params("parallel", "parallel", "arbitrary"),
    )(zf, zf, zf, c_row, after)


def _fox_bwd(zf, o32, dy, lse, c_row, dz, n_batch, name):
    s, bq, bk = SEQ, FOX_BQ, FOX_BK
    nq, nk = s // bq, s // bk

    def body(q_ref, k_ref, v_ref, o_ref, do_ref, lse_ref, cr_ref, dz_in, dz_ref, dc_ref, dq_acc):
        del dz_in
        hp = pl.program_id(1)

        @pl.when(pl.program_id(2) == 0)
        def _():
            dq_acc[...] = jnp.zeros_like(dq_acc)

        kj, vj = k_ref[...], v_ref[...]
        km = [jnp.where(_head_mask(e), kj, jnp.zeros_like(kj)) for e in range(2)]

        def step(i, j, ck, carry, masked):
            rows = pl.ds(i * bq, bq)
            qi, doi = q_ref[rows, :] * ATT_SCALE, do_ref[rows, :]
            prod = doi.astype(F32) * o_ref[rows, :]
            out = []
            dq = jnp.zeros((bq, LANES), F32)
            for e in range(2):
                dk_a, dv_a, dc_a = carry[3 * e:3 * e + 3]
                mask = _head_mask(e)
                lane0 = HEAD_DIM * e
                dom = jnp.where(mask, doi, jnp.zeros_like(doi))
                delta = jnp.sum(jnp.where(mask, prod, 0.0), axis=1, keepdims=True)
                sc = _dot_nt(qi, km[e]) - ck[e]
                if masked:
                    sc = jnp.where(_causal(i, j, bq, bk), sc, NEG)
                p = jnp.exp(sc - lse_ref[rows, lane0:lane0 + 1])
                ds = p * (_dot_nt(dom, vj) - delta)
                dsb = ds.astype(BF16)
                dq = dq + _dot(dsb, km[e])
                out += [dk_a + _dot_tn(dsb, qi), dv_a + _dot_tn(p.astype(BF16), dom), dc_a - jnp.sum(ds, axis=0, keepdims=True)]
            dq_acc[rows, :] += dq * ATT_SCALE
            return tuple(out)

        def run(j):
            cols = pl.ds(j * bk, bk)
            ck = [cr_ref[pl.ds(2 * hp + e, 1), cols] for e in range(2)]
            carry = (jnp.zeros((bk, LANES), F32), jnp.zeros((bk, LANES), F32), jnp.zeros((1, bk), F32)) * 2
            n_diag = (j * bk + bk + bq - 1) // bq
            for i in range((j * bk) // bq, nq):
                carry = step(i, j, ck, carry, masked=i < n_diag)
            for e in range(2):
                dc_ref[e:e + 1, :] = carry[3 * e + 2]
            dz_ref[cols, LANES:2 * LANES] = jnp.where(_head_mask(0), carry[0], carry[3]).astype(BF16)
            dz_ref[cols, 2 * LANES:3 * LANES] = (carry[1] + carry[4]).astype(BF16)
            if j == nk - 1:
                dz_ref[:, 0:LANES] = dq_acc[...].astype(BF16)

        for k in range(nk):
            pl.when(pl.program_id(2) == k)(functools.partial(run, k))

    def seq(idx):
        return pl.BlockSpec((s, LANES), lambda b, hp, j: (b, idx(hp)))

    def kblk(c0):
        return pl.BlockSpec((bk, LANES), lambda b, hp, j: (b * nk + j, 3 * hp + c0))

    return pl.pallas_call(
        body, grid=(n_batch, N_PAIRS, nk),
        in_specs=[seq(lambda hp: 3 * hp), kblk(1), kblk(2), seq(lambda hp: hp), seq(lambda hp: N_PAIRS + hp),
                  seq(lambda hp: hp),
                  pl.BlockSpec((None, N_HEADS, s), lambda b, hp, j: (b, 0, 0)), pl.BlockSpec(memory_space=pl.ANY)],
        out_specs=[pl.BlockSpec((s, PAIR_WIDTH), lambda b, hp, j: (b, N_PAIRS + hp)),
                   pl.BlockSpec((None, None, 2, bk), lambda b, hp, j: (b, hp, 0, j))],
        out_shape=[jax.ShapeDtypeStruct(dz.shape, dz.dtype), jax.ShapeDtypeStruct((n_batch, N_PAIRS, 2, s), F32)],
        scratch_shapes=[pltpu.VMEM((s, LANES), F32)],
        input_output_aliases={7: 0},
        name=name, compiler_params=_params("parallel", "parallel", "arbitrary"),
    )(zf, zf, zf, o32, dy, lse, c_row, dz)


def _dil_bias(slope, dil):
    qi = lax.broadcasted_iota(jnp.int32, (BLOCK, 2 * BLOCK), 0)
    kj = lax.broadcasted_iota(jnp.int32, (BLOCK, 2 * BLOCK), 1)
    delta = qi + BLOCK - kj
    return jnp.where((delta >= 0) & (delta <= BLOCK), (-slope * dil) * delta.astype(F32), NEG)


def _alibi_slope(hp, e):
    slope = jnp.float32(0.0)
    for k in range(N_PAIRS):
        slope = jnp.where(hp == k, jnp.float32(2.0 ** -(2 * k + e + 1)), slope)
    return slope


def _first_block_bias(bias):
    return jnp.where(lax.broadcasted_iota(jnp.int32, bias.shape, 1) < BLOCK, NEG, bias)


def _fill_bias(bias_scr, hp):
    for di, dil in enumerate(DILATIONS):
        for e in range(2):
            bias_scr[2 * di + e] = _dil_bias(_alibi_slope(hp, e), dil)


def _pair_specs(rows):
    return [pl.BlockSpec((rows, LANES), lambda b, hp, c0=c0: (b, 3 * hp + c0)) for c0 in range(3)]


def _strided(start, size, dil):
    return pl.ds(start, size) if dil == 1 else pl.ds(start, size, stride=dil)


QUARTER = SEQ // 4


def _to_quarters(src, dst):
    for r in range(4):
        dst[r * QUARTER:(r + 1) * QUARTER, :] = src[pl.ds(r, QUARTER, stride=4), :]


def _from_quarters(src, dst):
    for r in range(4):
        dst[pl.ds(r, QUARTER, stride=4), :] = src[r * QUARTER:(r + 1) * QUARTER, :]


def _mix_weights(l1, l2, l3):
    m = jnp.maximum(jnp.maximum(l1, l2), l3)
    e1, e2, e3 = jnp.exp(l1 - m), jnp.exp(l2 - m), jnp.exp(l3 - m)
    inv = 1.0 / (e1 + e2 + e3)
    return e1 * inv, e2 * inv, e3 * inv


def _dil_fwd(zd, n_batch, name):
    s = SEQ
    t = n_batch * s

    def body(q_ref, k_ref, v_ref, y_ref, l1_ref, l2_ref, l3_ref, o_scr, qkv4, o4, l4, bias_scr):
        _fill_bias(bias_scr, pl.program_id(1))
        for a, ref in enumerate((q_ref, k_ref, v_ref)):
            _to_quarters(ref, qkv4.at[a])

        def unit(srcs, start, first, stride, di, o_dst, l_dst):
            qrows = _strided(start, BLOCK, stride)
            krows = qrows if first else _strided(start - BLOCK * stride, 2 * BLOCK, stride)
            q = (srcs[0][qrows, :] * ATT_SCALE).astype(BF16)
            kc = srcs[1][krows, :].astype(BF16)
            vc = srcs[2][krows, :].astype(BF16)
            if first:
                kc, vc = jnp.concatenate([kc, kc]), jnp.concatenate([vc, vc])
            outs, lses = [], []
            for e in range(2):
                bias = _first_block_bias(bias_scr[2 * di + e]) if first else bias_scr[2 * di + e]
                sc = _dot_nt(jnp.where(_head_mask(e), q, jnp.zeros_like(q)), kc) + bias
                m = jnp.max(sc, axis=1, keepdims=True)
                pe = jnp.exp(sc - m)
                l = jnp.sum(pe, axis=1, keepdims=True)
                outs.append(_dot((pe * (1.0 / l)).astype(BF16), vc))
                lses.append(m + jnp.log(l))
            o_dst[qrows, :] = jnp.where(_head_mask(0), outs[0], outs[1])
            l_dst[qrows, :] = jnp.where(_head_mask(0), lses[0], lses[1])

        for n in range(SEQ // BLOCK):
            unit((q_ref, k_ref, v_ref), n * BLOCK, n == 0, 1, 0, o_scr.at[0], l1_ref)
        quarters = tuple(qkv4.at[a] for a in range(3))
        for di in (1, 2):
            stride = DILATIONS[di] // 4
            for r in range(4):
                for g in range(stride):
                    for n in range(QUARTER // (BLOCK * stride)):
                        unit(quarters, r * QUARTER + n * BLOCK * stride + g, n == 0, stride, di, o4.at[di - 1], l4.at[di - 1])
        for di, l_ref in ((1, l2_ref), (2, l3_ref)):
            _from_quarters(o4.at[di - 1], o_scr.at[di])
            _from_quarters(l4.at[di - 1], l_ref)
        w = _mix_weights(l1_ref[...], l2_ref[...], l3_ref[...])
        y_ref[...] = (w[0] * o_scr[0] + w[1] * o_scr[1] + w[2] * o_scr[2]).astype(BF16)

    blk = pl.BlockSpec((s, LANES), lambda b, hp: (b, hp))
    res = pl.pallas_call(
        body, grid=(n_batch, N_PAIRS),
        in_specs=_pair_specs(s),
        out_specs=[blk] * 4,
        out_shape=[jax.ShapeDtypeStruct((t, MIX_HALF), BF16)] + [jax.ShapeDtypeStruct((t, MIX_HALF), F32)] * 3,
        scratch_shapes=[pltpu.VMEM((3, s, LANES), F32), pltpu.VMEM((3, s, LANES), F32), pltpu.VMEM((2, s, LANES), F32),
                        pltpu.VMEM((2, s, LANES), F32), pltpu.VMEM((6, BLOCK, 2 * BLOCK), F32)],
        name=name, compiler_params=_params("parallel", "arbitrary"),
    )(zd, zd, zd)
    return res[0], res[1:]


def _dil_bwd(zd, dy, ya, lses, n_batch, name):
    s = SEQ
    t = n_batch * s

    def body(q_ref, k_ref, v_ref, dy_ref, ya_ref, l1_ref, l2_ref, l3_ref, dz_ref, w_scr, dy_scr, dot_scr, acc, st4, acc4, bias_scr):
        for di, dil in enumerate(DILATIONS):
            bias_scr[di] = jnp.concatenate([_dil_bias(_alibi_slope(pl.program_id(1), e), dil) for e in range(2)])
        for di, w in enumerate(_mix_weights(l1_ref[...], l2_ref[...], l3_ref[...])):
            w_scr[di] = w
        dya = dy_ref[...].astype(F32)
        prod = dya * ya_ref[...].astype(F32)
        per_head = [jnp.sum(jnp.where(_head_mask(e), prod, 0.0), axis=1, keepdims=True) for e in range(2)]
        dy_scr[...] = dya
        dot_scr[...] = jnp.where(_head_mask(0), per_head[0], per_head[1])
        acc[...] = jnp.zeros_like(acc)
        acc4[...] = jnp.zeros_like(acc4)
        staged = (q_ref, k_ref, v_ref, w_scr.at[1], w_scr.at[2], l2_ref, l3_ref, dy_scr, dot_scr)
        for a, ref in enumerate(staged):
            _to_quarters(ref, st4.at[a])

        def unit(srcs, dst, start, first, stride, di):
            qrows = _strided(start, BLOCK, stride)
            krows = qrows if first else _strided(start - BLOCK * stride, 2 * BLOCK, stride)
            q = (srcs[0][qrows, :] * ATT_SCALE).astype(BF16)
            kc = srcs[1][krows, :].astype(BF16)
            vc = srcs[2][krows, :].astype(BF16)
            wq = srcs[3][qrows, :]
            lse = srcs[4][qrows, :]
            do = (wq * srcs[5][qrows, :]).astype(BF16)
            sub = wq * srcs[6][qrows, :]
            heads = lambda a: jnp.concatenate([jnp.where(_head_mask(e), a, jnp.zeros_like(a)) for e in range(2)])
            column = lambda a: jnp.concatenate([a[:, HEAD_DIM * e:HEAD_DIM * e + 1] for e in range(2)])
            qq, dd = heads(q), heads(do)
            bias = bias_scr[di]
            p = jnp.exp(_dot_nt(qq, kc) + (bias[:, BLOCK:] if first else bias) - column(lse))
            dsb = (p * (_dot_nt(dd, vc) - column(sub))).astype(BF16)
            dq = _dot(jnp.concatenate([dsb[:BLOCK], dsb[BLOCK:]], axis=1), heads(kc))
            dst.at[0][qrows, :] += dq * ATT_SCALE
            dst.at[1][krows, :] += _dot_tn(dsb, qq)
            dst.at[2][krows, :] += _dot_tn(p.astype(BF16), dd)

        token_order = (q_ref, k_ref, v_ref, w_scr.at[0], l1_ref, dy_scr, dot_scr)
        for n in range(SEQ // BLOCK):
            unit(token_order, acc, n * BLOCK, n == 0, 1, 0)
        for di in (1, 2):
            quarters = (st4.at[0], st4.at[1], st4.at[2], st4.at[2 + di], st4.at[4 + di], st4.at[7], st4.at[8])
            stride = DILATIONS[di] // 4
            for r in range(4):
                for g in range(stride):
                    for n in range(QUARTER // (BLOCK * stride)):
                        unit(quarters, acc4, r * QUARTER + n * BLOCK * stride + g, n == 0, stride, di)
        for k in range(3):
            for r in range(4):
                acc.at[k][pl.ds(r, QUARTER, stride=4), :] += acc4[k, r * QUARTER:(r + 1) * QUARTER, :]
            dz_ref[:, k * LANES:(k + 1) * LANES] = acc[k].astype(BF16)

    blk = pl.BlockSpec((s, LANES), lambda b, hp: (b, hp))
    pair = pl.BlockSpec((s, PAIR_WIDTH), lambda b, hp: (b, hp))
    return pl.pallas_call(
        body, grid=(n_batch, N_PAIRS),
        in_specs=_pair_specs(s) + [blk] * 5,
        out_specs=pair,
        out_shape=jax.ShapeDtypeStruct((t, 2 * 3 * MIX_HALF), BF16),
        scratch_shapes=[pltpu.VMEM((3, s, LANES), F32), pltpu.VMEM((s, LANES), F32), pltpu.VMEM((s, LANES), F32),
                        pltpu.VMEM((3, s, LANES), F32), pltpu.VMEM((9, s, LANES), F32), pltpu.VMEM((3, s, LANES), F32),
                        pltpu.VMEM((3, 2 * BLOCK, 2 * BLOCK), F32)],
        name=name, compiler_params=_params("parallel", "arbitrary"),
    )(zd, zd, zd, dy, ya, *lses)


def _xattn_probs(q, k):
    sc = _dot_nt(q, k) * X_SCALE
    pe = jnp.exp(sc - jnp.max(sc, axis=1, keepdims=True))
    return pe / jnp.sum(pe, axis=1, keepdims=True)


def _rms(xv, g):
    return (xv * lax.rsqrt(jnp.mean(xv * xv, axis=-1, keepdims=True) + EPS) * g).astype(BF16)


def _out_xattn_fwd(ya, yf, x0, w_out, g_xattn, w_xq, kx, vx, w_xo, g_mlp, name, tm=512):
    t, d = x0.shape
    per_example = SEQ // tm

    def body(ya_ref, yf_ref, x0_ref, wa_ref, wf_ref, g2_ref, wq_ref, k_ref, v_ref, wo_ref, g3_ref,
             x1_ref, h2_ref, q_ref, o_ref, x2_ref, h3_ref):
        x1 = x0_ref[...] + (_dot(ya_ref[...], wa_ref[...]) + _dot(yf_ref[...], wf_ref[...]))
        x1_ref[...] = x1
        h2 = _rms(x1, g2_ref[...])
        h2_ref[...] = h2
        q = _dot(h2, wq_ref[...]).astype(BF16)
        q_ref[...] = q
        for h in range(X_HEADS):
            cols = slice(h * X_HEAD_DIM, (h + 1) * X_HEAD_DIM)
            p = _xattn_probs(q[:, cols], k_ref[:, cols])
            o_ref[:, cols] = _dot(p.astype(BF16), v_ref[:, cols]).astype(BF16)
        x2 = x1 + _dot(o_ref[...], wo_ref[...])
        x2_ref[...] = x2
        h3_ref[...] = _rms(x2, g3_ref[...])

    row = lambda width: pl.BlockSpec((tm, width), lambda i: (i, 0))
    whole = lambda a: pl.BlockSpec(a.shape, lambda i: (0, 0))
    mem = pl.BlockSpec((N_MEM, d), lambda i: (i // per_example, 0))
    vec = pl.BlockSpec((1, d), lambda i: (0, 0))
    w_a, w_f = w_out[:MIX_HALF], w_out[MIX_HALF:]
    return pl.pallas_call(
        body, grid=(t // tm,),
        in_specs=[row(MIX_HALF), row(MIX_HALF), row(d), whole(w_a), whole(w_f), vec, whole(w_xq), mem, mem, whole(w_xo), vec],
        out_specs=[row(d)] * 6,
        out_shape=[jax.ShapeDtypeStruct((t, d), dt) for dt in (F32, BF16, BF16, BF16, F32, BF16)],
        name=name, compiler_params=_params("arbitrary"),
    )(ya, yf, x0, w_a, w_f, g_xattn.reshape(1, d), w_xq, kx, vx, w_xo, g_mlp.reshape(1, d))


def _xattn_chain_bwd(dx2, dx2b, x1, g_xattn, qx, kx, vx, w_xo, w_xq, name, tm=512, after=None):
    t, d = x1.shape
    per_example = SEQ // tm
    order = () if after is None else (after,)

    def body(dx2_ref, dx2b_ref, x1_ref, g_ref, q_ref, k_ref, v_ref, wo_ref, wq_ref, *rest):
        dx1_ref, dx1b_ref, dg_ref, dq_ref, dk_ref, dv_ref, dk_acc, dv_acc = rest[len(order):]
        i = pl.program_id(0)

        @pl.when(i % per_example == 0)
        def _():
            dk_acc[...] = jnp.zeros_like(dk_acc)
            dv_acc[...] = jnp.zeros_like(dv_acc)

        do = _dot_nt(dx2b_ref[...], wo_ref[...]).astype(BF16)
        for h in range(X_HEADS):
            cols = slice(h * X_HEAD_DIM, (h + 1) * X_HEAD_DIM)
            q, k, do_h = q_ref[:, cols], k_ref[:, cols], do[:, cols]
            p = _xattn_probs(q, k)
            dp = _dot_nt(do_h, v_ref[:, cols])
            dsb = (p * (dp - jnp.sum(p * dp, axis=1, keepdims=True))).astype(BF16)
            dq_ref[:, cols] = (_dot(dsb, k) * X_SCALE).astype(BF16)
            dk_acc[:, cols] += _dot_tn(dsb, q) * X_SCALE
            dv_acc[:, cols] += _dot_tn(p.astype(BF16), do_h)

        @pl.when(i % per_example == per_example - 1)
        def _():
            dk_ref[...] = dk_acc[...].astype(BF16)
            dv_ref[...] = dv_acc[...].astype(BF16)

        dx, dg_rows = _rms_bwd_tile(x1_ref[...], _dot_nt(dq_ref[...], wq_ref[...]), g_ref[...])
        dx = dx2_ref[...] + dx
        dx1_ref[...] = dx
        dx1b_ref[...] = dx.astype(BF16)

        @pl.when(i == 0)
        def _():
            dg_ref[...] = jnp.zeros_like(dg_ref)

        dg_ref[...] += jnp.sum(dg_rows, axis=0, keepdims=True)

    row = pl.BlockSpec((tm, d), lambda i: (i, 0))
    vec = pl.BlockSpec((1, d), lambda i: (0, 0))
    mem = pl.BlockSpec((N_MEM, d), lambda i: (i // per_example, 0))
    whole = lambda a: pl.BlockSpec(a.shape, lambda i: (0, 0))
    return pl.pallas_call(
        body, grid=(t // tm,),
        in_specs=[row, row, row, vec, row, mem, mem, whole(w_xo), whole(w_xq)] + [pl.BlockSpec(memory_space=pl.ANY)] * len(order),
        out_specs=[row, row, vec, row, mem, mem],
        out_shape=[jax.ShapeDtypeStruct((t, d), F32), jax.ShapeDtypeStruct((t, d), BF16), jax.ShapeDtypeStruct((1, d), F32),
                   jax.ShapeDtypeStruct((t, d), BF16), jax.ShapeDtypeStruct(kx.shape, BF16), jax.ShapeDtypeStruct(kx.shape, BF16)],
        scratch_shapes=[pltpu.VMEM((N_MEM, d), F32)] * 2,
        name=name, compiler_params=_params("arbitrary"),
    )(dx2, dx2b, x1, g_xattn.reshape(1, d), qx, kx, vx, w_xo, w_xq, *order)


def _adamw(w, g, m, v, name, rows):
    r, c = w.shape
    assert r % rows == 0, (name, w.shape, rows)

    def body(w_ref, g_ref, m_ref, v_ref, d_ref, nm_ref, nv_ref):
        gv = g_ref[...]
        m1 = ADAM_B1 * m_ref[...] + (1.0 - ADAM_B1) * gv
        v1 = ADAM_B2 * v_ref[...] + (1.0 - ADAM_B2) * jnp.square(gv)
        m_hat = m1 / (1.0 - ADAM_B1 ** ADAM_STEP)
        v_hat = v1 / (1.0 - ADAM_B2 ** ADAM_STEP)
        d_ref[...] = -ADAM_LR * (m_hat / (jnp.sqrt(v_hat) + ADAM_EPS) + ADAM_WD * w_ref[...])
        nm_ref[...] = m1
        nv_ref[...] = v1

    blk = pl.BlockSpec((rows, c), lambda i: (i, 0))
    return pl.pallas_call(
        body, grid=(r // rows,), in_specs=[blk] * 4, out_specs=[blk] * 3,
        out_shape=[jax.ShapeDtypeStruct((r, c), F32)] * 3,
        name=name, compiler_params=_params("arbitrary"),
    )(w, g, m, v)


def _relu2(acc):
    a = jnp.maximum(acc, 0.0)
    return acc, a * a


def _relu2_bwd(acc, u):
    return (2.0 * jnp.maximum(u.astype(F32), 0.0) * acc,)


def _local_step(x, mem, target, vecs, w_in, late_weights, hooks=None):
    n_batch = x.shape[0]
    t = n_batch * SEQ
    x0 = x.reshape(t, D_MODEL)
    mem2 = mem.reshape(n_batch * N_MEM, D_MODEL)
    tgt = target.reshape(t, D_MODEL)

    half = 3 * MIX_HALF
    w_qkv = jnp.concatenate([_pair_major(w_in[:, :half]), _pair_major(w_in[:, half:QKV_WIDTH])], axis=1)
    w_gate = jnp.pad(w_in[:, QKV_WIDTH:], ((0, 0), (0, GATE_PAD - N_HEADS)))
    b_pad = jnp.pad(vecs["b_forget"], (0, GATE_PAD - N_HEADS)).reshape(1, GATE_PAD)

    h1, zd, zf, gate = _in_proj(x0, vecs["g_mix"], jnp.concatenate([w_qkv, w_gate], axis=1), "in_proj")
    mn = _rmsnorm(mem2, vecs["g_mem"], "norm_mem")
    c_row, sg = _gate_fwd(gate, b_pad, n_batch, "gate_fwd")
    ya, lses = _dil_fwd(zd, n_batch, "dil_fwd")
    late_start, late_finish = late_weights
    yf, of32, lse_f = _fox_fwd(zf, c_row, n_batch, "fox_fwd", late_start(ya))
    wts = late_finish(yf)
    w_out = wts["w_out"]
    kx = _matmul(mn, wts["w_xk"], "xk", out_dtypes=(BF16,))[0]
    vx = _matmul(mn, wts["w_xv"], "xv", out_dtypes=(BF16,))[0]
    x1, h2, qx, ox, x2, h3 = _out_xattn_fwd(ya, yf, x0, w_out, vecs["g_xattn"], wts["w_xq"], kx, vx, wts["w_xo"],
                                            vecs["g_mlp"], "out_xattn")
    u, a2 = _matmul(h3, wts["w_up"], "mlp_up", out_dtypes=(BF16, BF16), epilogue=_relu2)
    loss, dx3, dx3b, dg_final = _loss_bwd(a2, wts["w_down"], x2, vecs["g_final"], tgt, "mlp_down_loss")

    du = _matmul(dx3b, wts["w_down"], "mlp_down_bwd", out_dtypes=(BF16,), extras=(u,), epilogue=_relu2_bwd, w_t=True)[0]
    shards = (N_CHIPS, 2 * D_MODEL, D_MODEL)
    g_mlp = _matmul_tn(h3, du, "gw_up", packed=(shards, lambda i, j: (j, 0, 0), None))
    g_mlp = _matmul_tn(a2, dx3b, "gw_down", packed=(shards, lambda i, j: (i, 1, 0), g_mlp))
    gw_up = g_mlp[:, :D_MODEL].transpose(1, 0, 2).reshape(D_MODEL, D_FF)
    gw_down = g_mlp[:, D_MODEL:].reshape(D_FF, D_MODEL)
    on_grads, on_swapped = hooks or (None, None)
    token = on_grads("mlp", g_mlp) if hooks else None
    dx2, dx2b, dg_mlp = _matmul_rms_bwd([du], [wts["w_up"]], x2, vecs["g_mlp"], dx3, "mlp_up_bwd", after=token)

    gw_xo = _matmul_tn(ox, dx2b, "gw_xo")
    token = on_swapped("mlp", dx2b) if hooks else None
    dx1, dx1b, dg_xattn, dqx, dkx, dvx = _xattn_chain_bwd(dx2, dx2b, x1, vecs["g_xattn"], qx, kx, vx, wts["w_xo"],
                                                          wts["w_xq"], "xattn_chain_bwd", after=token)
    gw_xq = _matmul_tn(h2, dqx, "gw_xq")
    gw_xk = _matmul_tn(mn, dkx, "gw_xk")
    gw_xv = _matmul_tn(mn, dvx, "gw_xv")
    dmn = _matmul(dkx, wts["w_xk"], "xk_bwd", w_t=True)[0]
    dmn = _matmul_res(dvx, wts["w_xv"], dmn, "xv_bwd", w_t=True)
    _, _, dg_mem = _rms_bwd(mem2, dmn, vecs["g_mem"], None, "norm_mem_bwd")

    gw_out = jnp.concatenate([_matmul_tn(ya, dx1b, "gw_out_a"), _matmul_tn(yf, dx1b, "gw_out_f")], axis=0)
    token = on_grads("mid", dict(w_out=gw_out, w_xq=gw_xq, w_xk=gw_xk, w_xv=gw_xv, w_xo=gw_xo)) if hooks else None
    dy = _matmul(dx1b, w_out, "out_bwd", out_dtypes=(BF16,), w_t=True, after=token)[0]
    dz = _dil_bwd(zd, dy, ya, lses, n_batch, "dil_bwd")
    dz, dc = _fox_bwd(zf, of32, dy, lse_f, c_row, dz, n_batch, "fox_bwd")
    dzg, db = _gate_bwd(dc.reshape(n_batch, N_HEADS, SEQ), sg, "gate_bwd")
    token = on_swapped("mid", dz) if hooks else None
    gw_pm = _matmul_tn(h1, dz, "gw_in_qkv", after=token)
    gw_in = jnp.concatenate([_pair_major_inv(gw_pm[:, :half]), _pair_major_inv(gw_pm[:, half:]),
                             _matmul_tn(h1, dzg, "gw_in_gate")[:, :N_HEADS]], axis=1)
    dx0, _, dg_mix = _matmul_rms_bwd([dz, dzg], [w_qkv, w_gate], x0, vecs["g_mix"], dx1, "in_bwd")

    gw = dict(w_in=gw_in, w_out=gw_out, w_xq=gw_xq, w_xk=gw_xk, w_xv=gw_xv, w_xo=gw_xo, w_up=gw_up, w_down=gw_down)
    gv = dict(g_mix=dg_mix, g_xattn=dg_xattn, g_mem=dg_mem, g_mlp=dg_mlp, g_final=dg_final, b_forget=db)
    return loss, dx0.reshape(x.shape), gw, gv


MESH = pl.DeviceIdType.MESH
ANY = pl.BlockSpec(memory_space=pl.ANY)


def _place():
    x, y, c = lax.axis_index("x"), lax.axis_index("y"), lax.axis_index("c")
    other_chips = [(1 - x, y), (x, 1 - y), (1 - x, 1 - y)]
    return x, y, c, other_chips


def _my_chip():
    return 2 * lax.axis_index("x") + lax.axis_index("y")


def _halves(rows, c, align):
    half = rows // 2
    assert rows % (2 * align) == 0, rows
    return pl.ds(pl.multiple_of(c * half, align), half), pl.ds(pl.multiple_of((1 - c) * half, align), half)


def _place_own(wall, pack):
    return lax.dynamic_update_slice(wall, pack[None], (_my_chip(), 0, 0))


HBM = pl.BlockSpec(memory_space=pltpu.HBM)
SEM = pl.BlockSpec(memory_space=pltpu.SEMAPHORE)
SPLIT_COPY = pltpu.CompilerParams(has_side_effects=pltpu.SideEffectType.DATAFLOW_SIDE_EFFECTING)


def _in_hbm(a):
    return pltpu.with_memory_space_constraint(a, pltpu.HBM)


def _start_call(start, src, land_shape, after, name):
    land = lax.empty(land_shape, src.dtype)

    def body(src_ref, land_ref, after_ref, send_sems, recv_sems, src_thru, land_thru, token):
        del after_ref, src_thru, land_thru
        start(src_ref, land_ref, send_sems, recv_sems)
        token[...] = jnp.zeros_like(token)

    return pl.pallas_call(
        body, name=name,
        out_shape=(pltpu.SemaphoreType.DMA((3,)), pltpu.SemaphoreType.DMA((3,)), pltpu.HBM(src.shape, src.dtype),
                   pltpu.HBM(land_shape, src.dtype), jax.ShapeDtypeStruct((8, LANES), F32)),
        in_specs=(HBM, HBM, ANY), out_specs=(SEM, SEM, HBM, HBM, pl.BlockSpec(memory_space=pltpu.VMEM)),
        input_output_aliases={0: 2, 1: 3}, compiler_params=SPLIT_COPY,
    )(_in_hbm(src), _in_hbm(land), after)


def _wait_call(body, started, after, name):
    send_sems, recv_sems, src, land, _ = started
    return pl.pallas_call(
        body, name=name,
        out_shape=(pltpu.HBM(src.shape, src.dtype), pltpu.HBM(land.shape, land.dtype)),
        in_specs=(HBM, HBM, SEM, SEM, ANY), out_specs=(HBM, HBM),
        input_output_aliases={0: 0, 1: 1}, compiler_params=SPLIT_COPY,
    )(src, land, send_sems, recv_sems, after)


def _gather_copies(p_ref, wall_ref, send_sems, recv_sems):
    x, y, c, chips = _place()
    me = 2 * x + y
    mine, _ = _halves(p_ref.shape[0], c, 16)
    out, back = [], []
    for k, chip in enumerate(chips):
        peer = dict(send_sem=send_sems.at[k], recv_sem=recv_sems.at[k], device_id=(chip[0], chip[1], c), device_id_type=MESH)
        out.append(pltpu.make_async_remote_copy(src_ref=p_ref.at[mine], dst_ref=wall_ref.at[me, mine], **peer))
        slab = wall_ref.at[2 * chip[0] + chip[1], mine]
        back.append(pltpu.make_async_remote_copy(src_ref=slab, dst_ref=slab, **peer))
    return out, back


def _gather_start(pack, after, name):
    def start(p_ref, wall_ref, send_sems, recv_sems):
        for cp in _gather_copies(p_ref, wall_ref, send_sems, recv_sems)[0]:
            cp.start()

    return _start_call(start, pack, (N_CHIPS,) + pack.shape, after, name)


def _gather_wait(started, after, name):
    def body(p_ref, wall_ref, send_sems, recv_sems, after_ref, p_dead, wall_out):
        del after_ref, p_dead, wall_out
        out, back = _gather_copies(p_ref, wall_ref, send_sems, recv_sems)
        for cp_out, cp_back in zip(out, back):
            cp_out.wait_send()
            cp_back.wait_recv()

    return _wait_call(body, started, after, name)


def _pass_on(wall, name):
    def body(w_in_ref, out_ref, send_sems, recv_sems):
        del w_in_ref
        x, y, c, chips = _place()
        mine, theirs = _halves(wall.shape[1], c, 16)
        sends = []
        for k, chip in enumerate(chips):
            slab = out_ref.at[2 * chip[0] + chip[1]]
            peer = dict(send_sem=send_sems.at[k], recv_sem=recv_sems.at[k], device_id=(x, y, 1 - c), device_id_type=MESH)
            cp = pltpu.make_async_remote_copy(src_ref=slab.at[mine], dst_ref=slab.at[mine], **peer)
            cp.start()
            sends.append((cp, pltpu.make_async_remote_copy(src_ref=slab.at[theirs], dst_ref=slab.at[theirs], **peer)))
        for cp, back in sends:
            back.wait_recv()
            cp.wait_send()

    return pl.pallas_call(
        body, in_specs=[ANY], out_specs=ANY, out_shape=jax.ShapeDtypeStruct(wall.shape, wall.dtype),
        scratch_shapes=[pltpu.SemaphoreType.DMA((3,))] * 2, input_output_aliases={0: 0}, name=name,
    )(wall)


def _pass_copies(wall_ref, send_sems, recv_sems):
    x, y, c, chips = _place()
    mine, theirs = _halves(wall_ref.shape[1], c, 16)
    out, back = [], []
    for k, chip in enumerate(chips):
        slab = wall_ref.at[2 * chip[0] + chip[1]]
        peer = dict(send_sem=send_sems.at[k], recv_sem=recv_sems.at[k], device_id=(x, y, 1 - c), device_id_type=MESH)
        out.append(pltpu.make_async_remote_copy(src_ref=slab.at[mine], dst_ref=slab.at[mine], **peer))
        back.append(pltpu.make_async_remote_copy(src_ref=slab.at[theirs], dst_ref=slab.at[theirs], **peer))
    return out, back


def _pass_start(wall, name):
    def body(wall_ref, send_sems, recv_sems, wall_thru, token):
        del wall_thru
        for cp in _pass_copies(wall_ref, send_sems, recv_sems)[0]:
            cp.start()
        token[...] = jnp.zeros_like(token)

    return pl.pallas_call(
        body, name=name,
        out_shape=(pltpu.SemaphoreType.DMA((3,)), pltpu.SemaphoreType.DMA((3,)), pltpu.HBM(wall.shape, wall.dtype),
                   jax.ShapeDtypeStruct((8, LANES), F32)),
        in_specs=(HBM,), out_specs=(SEM, SEM, HBM, pl.BlockSpec(memory_space=pltpu.VMEM)),
        input_output_aliases={0: 2}, compiler_params=SPLIT_COPY,
    )(_in_hbm(wall))


def _pass_wait(started, after, name):
    send_sems, recv_sems, wall, _ = started

    def body(wall_ref, send_sems, recv_sems, after_ref, wall_out):
        del after_ref, wall_out
        for cp_out, cp_back in zip(*_pass_copies(wall_ref, send_sems, recv_sems)):
            cp_out.wait_send()
            cp_back.wait_recv()

    return pl.pallas_call(
        body, name=name, out_shape=pltpu.HBM(wall.shape, wall.dtype),
        in_specs=(HBM, SEM, SEM, ANY), out_specs=HBM,
        input_output_aliases={0: 0}, compiler_params=SPLIT_COPY,
    )(wall, send_sems, recv_sems, after)


def _swap_halves(g, name):
    half = g.shape[1] // 2

    def body(g_ref, out_ref, send_sem, recv_sem):
        x, y, c, _ = _place()
        _, theirs = _halves(g.shape[1], c, 8)
        cp = pltpu.make_async_remote_copy(src_ref=g_ref.at[:, theirs], dst_ref=out_ref, send_sem=send_sem, recv_sem=recv_sem,
                                          device_id=(x, y, 1 - c), device_id_type=MESH)
        cp.start()
        cp.wait()

    return pl.pallas_call(
        body, in_specs=[ANY], out_specs=ANY,
        out_shape=jax.ShapeDtypeStruct((N_CHIPS, half, g.shape[2]), F32),
        scratch_shapes=[pltpu.SemaphoreType.DMA, pltpu.SemaphoreType.DMA],
        name=name,
    )(g)


def _swap_copy(g_ref, land_ref, send_sems, recv_sems):
    x, y, c, _ = _place()
    _, theirs = _halves(g_ref.shape[1], c, 8)
    return pltpu.make_async_remote_copy(src_ref=g_ref.at[:, theirs], dst_ref=land_ref, send_sem=send_sems.at[0],
                                        recv_sem=recv_sems.at[0], device_id=(x, y, 1 - c), device_id_type=MESH)


def _swap_start(g, name):
    def start(g_ref, land_ref, send_sems, recv_sems):
        _swap_copy(g_ref, land_ref, send_sems, recv_sems).start()

    return _start_call(start, g, (N_CHIPS, g.shape[1] // 2, g.shape[2]), _core_index(), name)


def _swap_wait(started, after, name):
    def body(g_ref, land_ref, send_sems, recv_sems, after_ref, g_out, land_out):
        del after_ref, g_out, land_out
        cp = _swap_copy(g_ref, land_ref, send_sems, recv_sems)
        cp.wait_send()
        cp.wait_recv()

    return _wait_call(body, started, after, name)


def _core_index():
    return lax.axis_index("c").astype(jnp.int32).reshape(1)


def _row_tile(half):
    tile = max(t for t in range(16, 1025, 16) if half % t == 0)
    return tile, half // tile


def _add_sibling(g, got, name):
    half = g.shape[1] // 2
    tile, n_tiles = _row_tile(half)

    def body(c_ref, g_ref, got_ref, o_ref):
        o_ref[...] = (g_ref[...] + got_ref[...]).astype(BF16)

    width = g.shape[2]
    blk = pl.BlockSpec((None, tile, width), lambda s, i, c_ref: (s, i, 0))
    return pl.pallas_call(
        body,
        grid_spec=pltpu.PrefetchScalarGridSpec(
            num_scalar_prefetch=1, grid=(N_CHIPS, n_tiles),
            in_specs=[pl.BlockSpec((None, tile, width), lambda s, i, c_ref: (s, c_ref[0] * n_tiles + i, 0)), blk],
            out_specs=blk),
        out_shape=jax.ShapeDtypeStruct((N_CHIPS, half, width), BF16),
        name=name, compiler_params=_params("arbitrary", "arbitrary"),
    )(_core_index(), g, got)


def _exchange_copies(p_ref, land_ref, send_sems, recv_sems):
    x, y, c, chips = _place()
    me = 2 * x + y
    out, back = [], []
    for k, chip in enumerate(chips):
        peer = dict(send_sem=send_sems.at[k], recv_sem=recv_sems.at[k], device_id=(chip[0], chip[1], c), device_id_type=MESH)
        out.append(pltpu.make_async_remote_copy(src_ref=p_ref.at[2 * chip[0] + chip[1]], dst_ref=land_ref.at[me], **peer))
        slab = land_ref.at[2 * chip[0] + chip[1]]
        back.append(pltpu.make_async_remote_copy(src_ref=slab, dst_ref=slab, **peer))
    return out, back


def _with_own(got, part):
    me = _my_chip()
    return lax.dynamic_update_slice(got, lax.dynamic_slice(part, (me, 0, 0), (1,) + part.shape[1:]), (me, 0, 0))


def _exchange_start(part, name):
    def start(p_ref, land_ref, send_sems, recv_sems):
        for cp in _exchange_copies(p_ref, land_ref, send_sems, recv_sems)[0]:
            cp.start()

    return _start_call(start, part, part.shape, _core_index(), name)


def _exchange_wait(started, after, name):
    def body(p_ref, land_ref, send_sems, recv_sems, after_ref, p_dead, land_out):
        del after_ref, p_dead, land_out
        out, back = _exchange_copies(p_ref, land_ref, send_sems, recv_sems)
        for cp_out, cp_back in zip(out, back):
            cp_out.wait_send()
            cp_back.wait_recv()

    part, got = _wait_call(body, started, after, name)
    return _with_own(got, part)


def _sum_chips(parts, name):
    half, width = parts.shape[1:]
    tile, n_tiles = _row_tile(half)

    def body(c_ref, p0, p1, p2, p3, o_ref):
        f32 = lambda p: p[...].astype(F32)
        o_ref[...] = ((f32(p0) + f32(p1)) + f32(p2)) + f32(p3)

    def slab(s):
        return pl.BlockSpec((None, tile, width), lambda i, c_ref, s=s: (s, i, 0))

    return pl.pallas_call(
        body,
        grid_spec=pltpu.PrefetchScalarGridSpec(
            num_scalar_prefetch=1, grid=(n_tiles,),
            in_specs=[slab(s) for s in range(N_CHIPS)],
            out_specs=pl.BlockSpec((None, tile, width), lambda i, c_ref: (c_ref[0], i, 0))),
        out_shape=jax.ShapeDtypeStruct((2, half, width), F32),
        name=name, compiler_params=_params("arbitrary"),
    )(_core_index(), parts, parts, parts, parts)


def _share_halves(halves, name):
    def body(h_ref, out_ref, send_sem, recv_sem):
        del h_ref
        x, y, c, _ = _place()
        cp = pltpu.make_async_remote_copy(src_ref=out_ref.at[c], dst_ref=out_ref.at[c], send_sem=send_sem, recv_sem=recv_sem,
                                          device_id=(x, y, 1 - c), device_id_type=MESH)
        cp.start()
        pltpu.make_async_remote_copy(src_ref=out_ref.at[1 - c], dst_ref=out_ref.at[1 - c], send_sem=send_sem, recv_sem=recv_sem,
                                     device_id=(x, y, 1 - c), device_id_type=MESH).wait_recv()
        cp.wait_send()

    return pl.pallas_call(
        body, in_specs=[ANY], out_specs=ANY,
        out_shape=jax.ShapeDtypeStruct(halves.shape, halves.dtype),
        scratch_shapes=[pltpu.SemaphoreType.DMA] * 2,
        input_output_aliases={0: 0},
        name=name,
    )(halves)


def _reduce_parts(g, tag):
    return _add_sibling(g, _swap_halves(g, "swap_" + tag), "add_" + tag)


def _reduce_finish(got, tag):
    halves = _share_halves(_sum_chips(got, "sum_" + tag), "share_" + tag)
    return halves.reshape(2 * halves.shape[1], halves.shape[2])


SMALL_ROWS = 8


def _allreduce_small(v):
    def body(v_ref, out_ref, buf, send_sems, recv_sems):
        x, y, c, _ = _place()
        buf[4 * x + 2 * y + c] = v_ref[...]
        sends = []
        for k in range(1, N_DEV):
            px = 1 - x if k & 4 else x
            py = 1 - y if k & 2 else y
            pc = 1 - c if k & 1 else c
            cp = pltpu.make_async_remote_copy(src_ref=v_ref, dst_ref=buf.at[4 * x + 2 * y + c], send_sem=send_sems.at[k - 1],
                                              recv_sem=recv_sems.at[k - 1], device_id=(px, py, pc), device_id_type=MESH)
            cp.start()
            sends.append((cp, 4 * px + 2 * py + pc))
        for k, (cp, peer) in enumerate(sends):
            pltpu.make_async_remote_copy(src_ref=v_ref, dst_ref=buf.at[peer], send_sem=send_sems.at[k], recv_sem=recv_sems.at[k],
                                         device_id=(x, y, c), device_id_type=MESH).wait_recv()
        for cp, _ in sends:
            cp.wait_send()
        total = buf[0]
        for d in range(1, N_DEV):
            total = total + buf[d]
        out_ref[...] = total

    vmem = pl.BlockSpec(memory_space=pltpu.VMEM)
    return pl.pallas_call(
        body, in_specs=[vmem], out_specs=vmem,
        out_shape=jax.ShapeDtypeStruct(v.shape, v.dtype),
        scratch_shapes=[pltpu.VMEM((N_DEV,) + v.shape, v.dtype), pltpu.SemaphoreType.DMA((N_DEV - 1,)),
                        pltpu.SemaphoreType.DMA((N_DEV - 1,))],
        name="allreduce_small",
    )(v)


MATRICES = ("w_in", "w_out", "w_xq", "w_xk", "w_xv", "w_xo", "w_up", "w_down")
VECTORS = ("g_mix", "g_xattn", "g_mem", "g_mlp", "g_final", "b_forget")
WEIGHT_ORDER = ("g_mix", "w_in", "b_forget", "w_out", "g_xattn", "g_mem", "w_xq", "w_xk", "w_xv", "w_xo",
                "g_mlp", "w_up", "w_down", "g_final")
GROUPS = {"mlp": ("w_up", "w_down"), "mid": ("w_out", "w_xq", "w_xk", "w_xv", "w_xo"), "in": ("w_in",)}
LATE = GROUPS["mid"] + GROUPS["mlp"]
W_IN_SHARD = IN_WIDTH // N_CHIPS
SHARD_ROWS = {"w_out": 256, "w_xq": 256, "w_xk": 256, "w_xv": 256, "w_xo": 256, "w_up": 1024, "w_down": 1024}
PACK_ROWS = SHARD_ROWS
W_IN_PAD = -(-W_IN_SHARD // LANES) * LANES
ADAM_ROWS = 128


def _pack(parts, names):
    return jnp.concatenate([jnp.pad(parts[n], ((0, PACK_ROWS[n] - SHARD_ROWS[n]), (0, 0))) for n in names], axis=0)


def _unpack(a, names):
    out, pos = {}, 0
    for n in names:
        out[n] = a[..., pos:pos + SHARD_ROWS[n], :]
        pos += PACK_ROWS[n]
    return out


def _full_weights(wall, names):
    cols = lambda a: a.transpose(1, 0, 2).reshape(a.shape[1], -1)
    rows = lambda a: a.reshape(-1, a.shape[-1])
    if names == GROUPS["in"]:
        return {"w_in": cols(wall[:, :, :W_IN_SHARD])}
    return {n: cols(a) if n == "w_up" else rows(a) for n, a in _unpack(wall, names).items()}


def _shard_of(g, name, s):
    if name == "w_up":
        return g[:, s * D_MODEL:(s + 1) * D_MODEL]
    n = SHARD_ROWS[name]
    return g[s * n:(s + 1) * n]


def _pad_w_in(a):
    return jnp.pad(a, [(0, 0)] * (a.ndim - 1) + [(0, W_IN_PAD - W_IN_SHARD)])


def _pack_grads(gws, names):
    if names == GROUPS["in"]:
        return _pad_w_in(gws["w_in"].reshape(D_MODEL, N_CHIPS, W_IN_SHARD).transpose(1, 0, 2))
    return jnp.stack([_pack({n: _shard_of(gws[n], n, s) for n in names}, names) for s in range(N_CHIPS)])


def kernel(x, mem, g_mix, w_in, b_forget, w_out, g_xattn, g_mem, w_xq, w_xk, w_xv, w_xo, g_mlp, w_up, w_down, g_final, loss_target, m_g_mix, m_w_in, m_b_forget, m_w_out, m_g_xattn, m_g_mem, m_w_xq, m_w_xk, m_w_xv, m_w_xo, m_g_mlp, m_w_up, m_w_down, m_g_final, v_g_mix, v_w_in, v_b_forget, v_w_out, v_g_xattn, v_g_mem, v_w_xq, v_w_xk, v_w_xv, v_w_xo, v_g_mlp, v_w_up, v_w_down, v_g_final):
    given = dict(locals())
    weights = {n: given[n] for n in WEIGHT_ORDER}
    vecs = {n: weights[n] for n in VECTORS}

    shard = {n: weights[n].astype(BF16) for n in MATRICES}
    in_started = _gather_start(_pad_w_in(shard["w_in"]), _core_index(), "gather_in_start")
    late_pack = _pack(shard, LATE)
    in_pack, in_wall = _gather_wait(in_started, late_pack, "gather_in_wait")
    in_wall = _place_own(_pass_on(in_wall, "gather_in_pass"), in_pack)
    late = _gather_start(late_pack, in_wall, "gather_late_start")
    w_in_full = _full_weights(in_wall, GROUPS["in"])["w_in"]

    passing = {}

    def late_start(after):
        pack, wall = _gather_wait(late, after, "gather_late_wait")
        passing["late"] = pack, _pass_start(wall, "gather_late_pass_start")
        return passing["late"][1][3]

    def late_finish(after):
        pack, started = passing["late"]
        return _full_weights(_place_own(_pass_wait(started, after, "gather_late_pass_wait"), pack), LATE)

    started = {}

    swapping = {}

    def on_grads(group, gws):
        packed = gws if group == "mlp" else _pack_grads(gws, GROUPS[group])
        swapping[group] = _swap_start(packed, "swap_%s_start" % group)
        return swapping[group][4]

    def on_swapped(group, after):
        g, got = _swap_wait(swapping[group], after, "swap_%s_wait" % group)
        started[group] = _exchange_start(_add_sibling(g, got, "add_" + group), "exchange_%s_start" % group)
        return started[group][4]

    loss, grad_x, gw, gv = _local_step(x, mem, loss_target, vecs, w_in_full, (late_start, late_finish), (on_grads, on_swapped))

    part = _reduce_parts(_pack_grads(gw, GROUPS["in"]), "in")
    started["in"] = _exchange_start(part, "exchange_in_start")
    grads, delta, new_m, new_v = {}, {}, {}, {}

    def finish(group, after):
        got = _exchange_wait(started[group], after, "exchange_%s_wait" % group)
        done = _reduce_finish(got, group)
        for n, a in ({"w_in": done[:, :W_IN_SHARD]} if group == "in" else _unpack(done, GROUPS[group])).items():
            grads[n] = a.reshape(weights[n].shape)
            delta[n], new_m[n], new_v[n] = _adamw(weights[n], grads[n], given["m_" + n], given["v_" + n], "adamw_" + n, ADAM_ROWS)
        return new_v[GROUPS[group][-1]]

    after = finish("mlp", started["in"][4])
    after = finish("mid", after)

    row = lambda a: jnp.pad(a.reshape(-1), (0, D_MODEL - a.size)).reshape(1, D_MODEL)
    small = jnp.concatenate([gv[n] for n in VECTORS[:5]] + [row(gv["b_forget"][:, 0]), row(loss[0, :1]),
                             jnp.zeros((1, D_MODEL), F32)], axis=0)
    small = _allreduce_small(small)
    for k, n in enumerate(VECTORS[:5]):
        grads[n] = small[k]
    grads["b_forget"] = small[5, :N_HEADS]
    loss_total = small[6, 0]
    finish("in", after)

    stack = lambda prefix: jnp.concatenate([row(given[prefix + n]) for n in VECTORS] + [jnp.zeros((2, D_MODEL), F32)], axis=0)
    g_small = jnp.concatenate([small[:6], jnp.zeros((2, D_MODEL), F32)], axis=0)
    d, m1, v1 = _adamw(stack(""), g_small, stack("m_"), stack("v_"), "adamw_vectors", SMALL_ROWS)
    for k, n in enumerate(VECTORS):
        width = weights[n].shape[0]
        delta[n], new_m[n], new_v[n] = d[k, :width], m1[k, :width], v1[k, :width]

    return (loss_total, grad_x, *[grads[n] for n in WEIGHT_ORDER], *[delta[n] for n in WEIGHT_ORDER],
            *[new_m[n] for n in WEIGHT_ORDER], *[new_v[n] for n in WEIGHT_ORDER])
```

```python
import functools
import math

import jax
import jax.numpy as jnp
from jax import lax
from jax.experimental import pallas as pl
from jax.experimental.pallas import tpu as pltpu

F32 = jnp.float32
BF16 = jnp.bfloat16

D_MODEL = 1024
SEQ = 2048
N_MEM = 256
HEAD_DIM = 64
N_HEADS = 8
MIX_HALF = N_HEADS * HEAD_DIM
QKV_WIDTH = 6 * MIX_HALF
IN_WIDTH = QKV_WIDTH + N_HEADS
GATE_PAD = 128
BLOCK = 128
DILATIONS = (1, 4, 16)
X_HEADS = 4
X_HEAD_DIM = 256
D_FF = 4096
EPS = 1e-6
NEG = -1e30
ATT_SCALE = 1.0 / math.sqrt(HEAD_DIM)
X_SCALE = 1.0 / math.sqrt(X_HEAD_DIM)
LANES = 128
N_CHIPS = 4
N_DEV = 8

ADAM_LR = 0.001
ADAM_B1 = 0.9
ADAM_B2 = 0.999
ADAM_EPS = 1e-08
ADAM_WD = 0.01
ADAM_STEP = 10

VMEM_LIMIT = 48 * 1024 * 1024


def _params(*sem):
    return pltpu.CompilerParams(dimension_semantics=sem or None, vmem_limit_bytes=VMEM_LIMIT)


def _dot(a, b):
    return jnp.dot(a, b, preferred_element_type=F32)


def _dot_nt(a, b):
    return lax.dot_general(a, b, (((1,), (1,)), ((), ())), preferred_element_type=F32)


def _dot_tn(a, b):
    return lax.dot_general(a, b, (((0,), (0,)), ((), ())), preferred_element_type=F32)


def _dot_exact(x, e):
    hi = x.astype(BF16)
    r1 = x - hi.astype(F32)
    mid = r1.astype(BF16)
    lo = (r1 - mid.astype(F32)).astype(BF16)
    return _dot(hi, e) + _dot(mid, e) + _dot(lo, e)


def _head_mask(e):
    lane = lax.broadcasted_iota(jnp.int32, (1, LANES), 1)
    return (lane >= HEAD_DIM * e) & (lane < HEAD_DIM * (e + 1))


def _matmul(a, w, name, out_dtypes=(F32,), extras=(), epilogue=None, tm=1024, tn=1024, w_t=False, after=None):
    m, k = a.shape
    n = w.shape[0] if w_t else w.shape[1]
    tm, tn = min(tm, m), min(tn, n)
    assert m % tm == 0 and n % tn == 0, (name, a.shape, w.shape)
    n_ex = len(extras)
    order = () if after is None else (after,)

    def body(a_ref, w_ref, *rest):
        rest = rest[len(order):]
        acc = (_dot_nt if w_t else _dot)(a_ref[...], w_ref[...])
        res = (acc,) if epilogue is None else epilogue(acc, *[r[...] for r in rest[:n_ex]])
        for o_ref, r in zip(rest[n_ex:], res):
            o_ref[...] = r.astype(o_ref.dtype)

    tile = pl.BlockSpec((tm, tn), lambda i, j: (i, j))
    w_spec = pl.BlockSpec((tn, k), lambda i, j: (j, 0)) if w_t else pl.BlockSpec((k, tn), lambda i, j: (0, j))
    return pl.pallas_call(
        body, grid=(m // tm, n // tn),
        in_specs=[pl.BlockSpec((tm, k), lambda i, j: (i, 0)), w_spec] + [pl.BlockSpec(memory_space=pl.ANY)] * len(order) + [tile] * n_ex,
        out_specs=[tile] * len(out_dtypes),
        out_shape=[jax.ShapeDtypeStruct((m, n), dt) for dt in out_dtypes],
        name=name, compiler_params=_params("parallel", "arbitrary"),
    )(a, w, *order, *extras)


def _matmul_res(a, w, res, name, w_t=False):
    return _matmul(a, w, name, extras=(res,), epilogue=lambda acc, r: (r + acc,), w_t=w_t)[0]


def _matmul_tn(x, y, name, tm=1024, tn=1024, tk=2048, packed=None, after=None):
    t, m = x.shape
    _, n = y.shape
    tm, tn, tk = min(tm, m), min(tn, n), min(tk, t)
    assert m % tm == 0 and n % tn == 0 and t % tk == 0, (name, x.shape, y.shape)
    shape, place, into = packed or ((m, n), None, None)

    def body(x_ref, y_ref, *rest):
        o_ref = rest[-1]

        @pl.when(pl.program_id(2) == 0)
        def _():
            o_ref[...] = jnp.zeros_like(o_ref)

        o_ref[...] += _dot_tn(x_ref[...], y_ref[...])

    out_spec = (pl.BlockSpec((tm, tn), lambda i, j, k: (i, j)) if place is None
                else pl.BlockSpec((None, tm, tn), lambda i, j, k: place(i, j)))
    return pl.pallas_call(
        body, grid=(m // tm, n // tn, t // tk),
        in_specs=[pl.BlockSpec((tk, tm), lambda i, j, k: (k, i)), pl.BlockSpec((tk, tn), lambda i, j, k: (k, j))]
        + [pl.BlockSpec(memory_space=pl.ANY)] * ((into is not None) + (after is not None)),
        out_specs=out_spec, out_shape=jax.ShapeDtypeStruct(shape, F32),
        input_output_aliases={} if into is None else {2: 0},
        name=name, compiler_params=_params("parallel", "parallel", "arbitrary"),
    )(x, y, *(() if into is None else (into,)), *(() if after is None else (after,)))


def _rmsnorm(x, g, name, tm=512):
    t, d = x.shape
    tm = min(tm, t)

    def body(x_ref, g_ref, h_ref):
        xv = x_ref[...]
        r = lax.rsqrt(jnp.mean(xv * xv, axis=-1, keepdims=True) + EPS)
        h_ref[...] = (xv * r * g_ref[...]).astype(BF16)

    return pl.pallas_call(
        body, grid=(t // tm,),
        in_specs=[pl.BlockSpec((tm, d), lambda i: (i, 0)), pl.BlockSpec((1, d), lambda i: (0, 0))],
        out_specs=pl.BlockSpec((tm, d), lambda i: (i, 0)),
        out_shape=jax.ShapeDtypeStruct((t, d), BF16),
        name=name, compiler_params=_params("arbitrary"),
    )(x, g.reshape(1, d))


def _in_proj(x, g, w_all, name, tm=512):
    t, d = x.shape
    half = 3 * MIX_HALF

    def body(x_ref, g_ref, w_ref, h_ref, zd_ref, zf_ref, gate_ref):
        xv = x_ref[...]
        r = lax.rsqrt(jnp.mean(xv * xv, axis=-1, keepdims=True) + EPS)
        h = (xv * r * g_ref[...]).astype(BF16)
        h_ref[...] = h
        zd_ref[...] = _dot(h, w_ref[:, 0:half])
        zf_ref[...] = _dot(h, w_ref[:, half:2 * half]).astype(BF16)
        gate_ref[...] = _dot(h, w_ref[:, 2 * half:])

    row = lambda width: pl.BlockSpec((tm, width), lambda i: (i, 0))
    return pl.pallas_call(
        body, grid=(t // tm,),
        in_specs=[row(d), pl.BlockSpec((1, d), lambda i: (0, 0)), pl.BlockSpec(w_all.shape, lambda i: (0, 0))],
        out_specs=[row(d), row(half), row(half), row(GATE_PAD)],
        out_shape=[jax.ShapeDtypeStruct((t, d), BF16), jax.ShapeDtypeStruct((t, half), F32),
                   jax.ShapeDtypeStruct((t, half), BF16), jax.ShapeDtypeStruct((t, GATE_PAD), F32)],
        name=name, compiler_params=_params("arbitrary"),
    )(x, g.reshape(1, d), w_all)


def _rms_bwd_tile(xv, dh, g):
    d = xv.shape[-1]
    r = lax.rsqrt(jnp.mean(xv * xv, axis=-1, keepdims=True) + EPS)
    dyg = dh * g
    proj = jnp.sum(dyg * xv, axis=-1, keepdims=True)
    dx = r * dyg - xv * (r * r * r * (1.0 / d)) * proj
    return dx, dh * (xv * r)


def _rms_bwd(x, dh, g, dres, name, tm=512):
    t, d = x.shape
    tm = min(tm, t)
    has_res = dres is not None

    def body(x_ref, dh_ref, g_ref, *rest):
        if has_res:
            res_ref, dx_ref, dxb_ref, dg_ref = rest
        else:
            dx_ref, dxb_ref, dg_ref = rest
        dx, dg_rows = _rms_bwd_tile(x_ref[...], dh_ref[...], g_ref[...])
        if has_res:
            dx = res_ref[...] + dx
        dx_ref[...] = dx
        dxb_ref[...] = dx.astype(BF16)

        @pl.when(pl.program_id(0) == 0)
        def _():
            dg_ref[...] = jnp.zeros_like(dg_ref)

        dg_ref[...] += jnp.sum(dg_rows, axis=0, keepdims=True)

    row = pl.BlockSpec((tm, d), lambda i: (i, 0))
    vec = pl.BlockSpec((1, d), lambda i: (0, 0))
    return pl.pallas_call(
        body, grid=(t // tm,),
        in_specs=[row, row, vec] + ([row] if has_res else []),
        out_specs=[row, row, vec],
        out_shape=[jax.ShapeDtypeStruct((t, d), F32), jax.ShapeDtypeStruct((t, d), BF16), jax.ShapeDtypeStruct((1, d), F32)],
        name=name, compiler_params=_params("arbitrary"),
    )(x, dh, g.reshape(1, d), *((dres,) if has_res else ()))


def _row_dots(a_refs, w_refs, w_t):
    acc = None
    for a_ref, w_ref in zip(a_refs, w_refs):
        part = (_dot_nt if w_t else _dot)(a_ref[...], w_ref[...])
        acc = part if acc is None else acc + part
    return acc


def _row_specs(a_parts, w_parts, tm):
    specs = [pl.BlockSpec((tm, a.shape[1]), lambda i: (i, 0)) for a in a_parts]
    return specs + [pl.BlockSpec(w.shape, lambda i: (0, 0)) for w in w_parts]


def _matmul_rms_bwd(a_parts, w_parts, x, g, dres, name, tm=512, after=None):
    t, d = x.shape
    n = len(a_parts)
    order = () if after is None else (after,)

    def body(*refs):
        x_ref, g_ref, res_ref = refs[2 * n:2 * n + 3]
        dx_ref, dxb_ref, dg_ref = refs[2 * n + 3 + len(order):]
        dx, dg_rows = _rms_bwd_tile(x_ref[...], _row_dots(refs[:n], refs[n:2 * n], True), g_ref[...])
        dx = res_ref[...] + dx
        dx_ref[...] = dx
        dxb_ref[...] = dx.astype(BF16)

        @pl.when(pl.program_id(0) == 0)
        def _():
            dg_ref[...] = jnp.zeros_like(dg_ref)

        dg_ref[...] += jnp.sum(dg_rows, axis=0, keepdims=True)

    row = pl.BlockSpec((tm, d), lambda i: (i, 0))
    vec = pl.BlockSpec((1, d), lambda i: (0, 0))
    return pl.pallas_call(
        body, grid=(t // tm,),
        in_specs=_row_specs(a_parts, w_parts, tm) + [row, vec, row] + [pl.BlockSpec(memory_space=pl.ANY)] * len(order),
        out_specs=[row, row, vec],
        out_shape=[jax.ShapeDtypeStruct((t, d), F32), jax.ShapeDtypeStruct((t, d), BF16), jax.ShapeDtypeStruct((1, d), F32)],
        name=name, compiler_params=_params("arbitrary"),
    )(*a_parts, *w_parts, x, g.reshape(1, d), dres, *order)


def _loss_bwd(a, w, res, g, target, name, tm=512):
    t, d = res.shape

    def body(a_ref, w_ref, x_ref, g_ref, t_ref, loss_ref, dx_ref, dxb_ref, dg_ref):
        xv = x_ref[...] + _dot(a_ref[...], w_ref[...])
        gv = g_ref[...]
        r = lax.rsqrt(jnp.mean(xv * xv, axis=-1, keepdims=True) + EPS)
        err = xv * r * gv - t_ref[...]
        dx, dg_rows = _rms_bwd_tile(xv, err * (1.0 / d), gv)
        dx_ref[...] = dx
        dxb_ref[...] = dx.astype(BF16)

        @pl.when(pl.program_id(0) == 0)
        def _():
            dg_ref[...] = jnp.zeros_like(dg_ref)
            loss_ref[...] = jnp.zeros_like(loss_ref)

        dg_ref[...] += jnp.sum(dg_rows, axis=0, keepdims=True)
        part = jnp.sum(jnp.sum(err * err, axis=0, keepdims=True), axis=1, keepdims=True) * (0.5 / d)
        loss_ref[...] += jnp.broadcast_to(part, loss_ref.shape)

    row = pl.BlockSpec((tm, d), lambda i: (i, 0))
    vec = pl.BlockSpec((1, d), lambda i: (0, 0))
    return pl.pallas_call(
        body, grid=(t // tm,),
        in_specs=_row_specs([a], [w], tm) + [row, vec, row],
        out_specs=[pl.BlockSpec((1, LANES), lambda i: (0, 0)), row, row, vec],
        out_shape=[jax.ShapeDtypeStruct((1, LANES), F32), jax.ShapeDtypeStruct((t, d), F32),
                   jax.ShapeDtypeStruct((t, d), BF16), jax.ShapeDtypeStruct((1, d), F32)],
        name=name, compiler_params=_params("arbitrary"),
    )(a, w, res, g.reshape(1, d), target)


def _tri(upper):
    r = lax.broadcasted_iota(jnp.int32, (LANES, LANES), 0)
    c = lax.broadcasted_iota(jnp.int32, (LANES, LANES), 1)
    return jnp.where((r <= c) if upper else (r >= c), 1.0, 0.0).astype(BF16)


def _gate_fwd(gate, b_pad, n_batch, name):
    s = SEQ
    nblk = s // LANES

    def body(g_ref, b_ref, crow_ref, sg_ref):
        gz = g_ref[...] + b_ref[...]
        logf = jnp.minimum(gz, 0.0) - jnp.log(1.0 + jnp.exp(-jnp.abs(gz)))
        logf_t = logf.T
        sg_ref[...] = (1.0 / (1.0 + jnp.exp(gz))).T[0:N_HEADS]
        upper = _tri(True)
        carry = jnp.zeros((N_HEADS, 1), F32)
        for blk in range(nblk):
            seg = _dot_exact(logf_t[0:N_HEADS, blk * LANES:(blk + 1) * LANES], upper) + carry
            carry = seg[:, LANES - 1:LANES]
            crow_ref[:, blk * LANES:(blk + 1) * LANES] = seg

    return pl.pallas_call(
        body, grid=(n_batch,),
        in_specs=[pl.BlockSpec((s, GATE_PAD), lambda b: (b, 0)), pl.BlockSpec((1, GATE_PAD), lambda b: (0, 0))],
        out_specs=[pl.BlockSpec((None, N_HEADS, s), lambda b: (b, 0, 0)),
                   pl.BlockSpec((None, N_HEADS, s), lambda b: (b, 0, 0))],
        out_shape=[jax.ShapeDtypeStruct((n_batch, N_HEADS, s), F32),
                   jax.ShapeDtypeStruct((n_batch, N_HEADS, s), F32)],
        name=name, compiler_params=_params("arbitrary"),
    )(gate, b_pad)


def _gate_bwd(dc, sg, name):
    n_batch, _, s = dc.shape
    nblk = s // LANES

    def body(dc_ref, sg_ref, dz_ref, db_ref, dt_ref):
        lower = _tri(False)
        dcv = dc_ref[...]
        carry = jnp.zeros((N_HEADS, 1), F32)
        dt_ref[...] = jnp.zeros_like(dt_ref)
        for blk in reversed(range(nblk)):
            seg = _dot_exact(dcv[:, blk * LANES:(blk + 1) * LANES], lower) + carry
            carry = seg[:, 0:1]
            dt_ref[0:N_HEADS, blk * LANES:(blk + 1) * LANES] = seg * sg_ref[:, blk * LANES:(blk + 1) * LANES]
        dg_t = dt_ref[...]
        dz_ref[...] = dg_t.T.astype(BF16)

        @pl.when(pl.program_id(0) == 0)
        def _():
            db_ref[...] = jnp.zeros_like(db_ref)

        db_ref[...] += jnp.broadcast_to(jnp.sum(dg_t[0:N_HEADS], axis=1, keepdims=True), db_ref.shape)

    return pl.pallas_call(
        body, grid=(n_batch,),
        in_specs=[pl.BlockSpec((None, N_HEADS, s), lambda b: (b, 0, 0)), pl.BlockSpec((None, N_HEADS, s), lambda b: (b, 0, 0))],
        out_specs=[pl.BlockSpec((s, GATE_PAD), lambda b: (b, 0)), pl.BlockSpec((N_HEADS, LANES), lambda b: (0, 0))],
        out_shape=[jax.ShapeDtypeStruct((n_batch * s, GATE_PAD), BF16), jax.ShapeDtypeStruct((N_HEADS, LANES), F32)],
        scratch_shapes=[pltpu.VMEM((LANES, s), F32)],
        name=name, compiler_params=_params("arbitrary"),
    )(dc, sg)


FOX_BQ = 512
FOX_BK = 512
FOX_STRIP = 512
PAIR_WIDTH = 3 * LANES
N_PAIRS = N_HEADS // 2


def _pair_major(w):
    return w.reshape(w.shape[0], 3, N_PAIRS, LANES).transpose(0, 2, 1, 3).reshape(w.shape[0], 3 * MIX_HALF)


def _pair_major_inv(w):
    return w.reshape(w.shape[0], N_PAIRS, 3, LANES).transpose(0, 2, 1, 3).reshape(w.shape[0], 3 * MIX_HALF)


def _causal(i, j, bq, bk):
    qpos = i * bq + lax.broadcasted_iota(jnp.int32, (bq, 1), 0)
    kpos = j * bk + lax.broadcasted_iota(jnp.int32, (1, bk), 1)
    return kpos <= qpos


def _split_bf16(p):
    hi = p.astype(BF16)
    return hi, (p - hi.astype(F32)).astype(BF16)


def _fox_fwd(zf, c_row, n_batch, name, after):
    s, bq, bk = SEQ, FOX_BQ, FOX_BK
    nq = s // bq
    t = n_batch * s

    n_strip = bq // FOX_STRIP

    def body(q_ref, k_ref, v_ref, cr_ref, after_ref, o_ref, o32_ref, lse_ref):
        del after_ref
        hp = pl.program_id(1)
        strips = [slice(r * FOX_STRIP, (r + 1) * FOX_STRIP) for r in range(n_strip)]
        chains = [(e, r) for e in range(2) for r in range(n_strip)]
        qh = {}
        for e, r in chains:
            q = q_ref[strips[r], :] * ATT_SCALE
            qh[e, r] = jnp.where(_head_mask(e), q, jnp.zeros_like(q))

        def step(i, j, carry, masked):
            rows = pl.ds(j * bk, bk)
            kj, vj = k_ref[rows, :], v_ref[rows, :]
            ck = [cr_ref[pl.ds(2 * hp + e, 1), rows] for e in range(2)]
            out = []
            scores = [_dot_nt(qh[e, r], kj) for e, r in chains]
            for n, (e, r) in enumerate(chains):
                m, l, acc = carry[3 * n:3 * n + 3]
                sc = scores[n] - ck[e]
                if masked:
                    qpos = i * bq + r * FOX_STRIP + lax.broadcasted_iota(jnp.int32, (FOX_STRIP, 1), 0)
                    kpos = j * bk + lax.broadcasted_iota(jnp.int32, (1, bk), 1)
                    sc = jnp.where(kpos <= qpos, sc, NEG)
                m_new = jnp.maximum(m, jnp.max(sc, axis=1, keepdims=True))
                alpha = jnp.exp(m - m_new)
                p = jnp.exp(sc - m_new)
                p_hi, p_lo = _split_bf16(p)
                out += [m_new, alpha * l + jnp.sum(p, axis=1, keepdims=True), alpha * acc + (_dot(p_hi, vj) + _dot(p_lo, vj))]
            return tuple(out)

        def run(i):
            carry = (jnp.full((FOX_STRIP, 1), NEG, F32), jnp.zeros((FOX_STRIP, 1), F32), jnp.zeros((FOX_STRIP, LANES), F32)) * len(chains)
            n_clear = (i * bq) // bk
            for j in range((i * bq + bq + bk - 1) // bk):
                carry = step(i, j, carry, masked=j >= n_clear)
            for r in range(n_strip):
                outs = [carry[3 * (e * n_strip + r) + 2] / carry[3 * (e * n_strip + r) + 1] for e in range(2)]
                lses = [carry[3 * (e * n_strip + r)] + jnp.log(carry[3 * (e * n_strip + r) + 1]) for e in range(2)]
                o = jnp.where(_head_mask(0), outs[0], outs[1])
                o_ref[strips[r], :] = o.astype(BF16)
                o32_ref[strips[r], :] = o
                lse_ref[strips[r], :] = jnp.where(_head_mask(0), lses[0], lses[1])

        for k in range(nq):
            pl.when(pl.program_id(2) == k)(functools.partial(run, k))

    def col(c0):
        return lambda b, hp, i: (b, 3 * hp + c0)

    blk = pl.BlockSpec((bq, LANES), lambda b, hp, i: (b * nq + i, hp))
    return pl.pallas_call(
        body, grid=(n_batch, N_PAIRS, nq),
        in_specs=[pl.BlockSpec((bq, LANES), lambda b, hp, i: (b * nq + i, 3 * hp)),
                  pl.BlockSpec((s, LANES), col(1)), pl.BlockSpec((s, LANES), col(2)),
                  pl.BlockSpec((None, N_HEADS, s), lambda b, hp, i: (b, 0, 0)), ANY],
        out_specs=[blk, blk, blk],
        out_shape=[jax.ShapeDtypeStruct((t, MIX_HALF), BF16), jax.ShapeDtypeStruct((t, MIX_HALF), F32),
                   jax.ShapeDtypeStruct((t, MIX_HALF), F32)],
        name=name, compiler_params=_params("parallel", "parallel", "arbitrary"),
    )(zf, zf, zf, c_row, after)


def _fox_bwd(zf, o32, dy, lse, c_row, dz, n_batch, name):
    s, bq, bk = SEQ, FOX_BQ, FOX_BK
    nq, nk = s // bq, s // bk

    def body(q_ref, k_ref, v_ref, o_ref, do_ref, lse_ref, cr_ref, dz_in, dz_ref, dc_ref, dq_acc):
        del dz_in
        hp = pl.program_id(1)

        @pl.when(pl.program_id(2) == 0)
        def _():
            dq_acc[...] = jnp.zeros_like(dq_acc)

        kj, vj = k_ref[...], v_ref[...]
        km = [jnp.where(_head_mask(e), kj, jnp.zeros_like(kj)) for e in range(2)]

        def step(i, j, ck, carry, masked):
            rows = pl.ds(i * bq, bq)
            qi, doi = q_ref[rows, :] * ATT_SCALE, do_ref[rows, :]
            prod = doi.astype(F32) * o_ref[rows, :]
            out = []
            dq = jnp.zeros((bq, LANES), F32)
            for e in range(2):
                dk_a, dv_a, dc_a = carry[3 * e:3 * e + 3]
                mask = _head_mask(e)
                lane0 = HEAD_DIM * e
                dom = jnp.where(mask, doi, jnp.zeros_like(doi))
                delta = jnp.sum(jnp.where(mask, prod, 0.0), axis=1, keepdims=True)
                sc = _dot_nt(qi, km[e]) - ck[e]
                if masked:
                    sc = jnp.where(_causal(i, j, bq, bk), sc, NEG)
                p = jnp.exp(sc - lse_ref[rows, lane0:lane0 + 1])
                ds = p * (_dot_nt(dom, vj) - delta)
                dsb = ds.astype(BF16)
                dq = dq + _dot(dsb, km[e])
                out += [dk_a + _dot_tn(dsb, qi), dv_a + _dot_tn(p.astype(BF16), dom), dc_a - jnp.sum(ds, axis=0, keepdims=True)]
            dq_acc[rows, :] += dq * ATT_SCALE
            return tuple(out)

        def run(j):
            cols = pl.ds(j * bk, bk)
            ck = [cr_ref[pl.ds(2 * hp + e, 1), cols] for e in range(2)]
            carry = (jnp.zeros((bk, LANES), F32), jnp.zeros((bk, LANES), F32), jnp.zeros((1, bk), F32)) * 2
            n_diag = (j * bk + bk + bq - 1) // bq
            for i in range((j * bk) // bq, nq):
                carry = step(i, j, ck, carry, masked=i < n_diag)
            for e in range(2):
                dc_ref[e:e + 1, :] = carry[3 * e + 2]
            dz_ref[cols, LANES:2 * LANES] = jnp.where(_head_mask(0), carry[0], carry[3]).astype(BF16)
            dz_ref[cols, 2 * LANES:3 * LANES] = (carry[1] + carry[4]).astype(BF16)
            if j == nk - 1:
                dz_ref[:, 0:LANES] = dq_acc[...].astype(BF16)

        for k in range(nk):
            pl.when(pl.program_id(2) == k)(functools.partial(run, k))

    def seq(idx):
        return pl.BlockSpec((s, LANES), lambda b, hp, j: (b, idx(hp)))

    def kblk(c0):
        return pl.BlockSpec((bk, LANES), lambda b, hp, j: (b * nk + j, 3 * hp + c0))

    return pl.pallas_call(
        body, grid=(n_batch, N_PAIRS, nk),
        in_specs=[seq(lambda hp: 3 * hp), kblk(1), kblk(2), seq(lambda hp: hp), seq(lambda hp: N_PAIRS + hp),
                  seq(lambda hp: hp),
                  pl.BlockSpec((None, N_HEADS, s), lambda b, hp, j: (b, 0, 0)), pl.BlockSpec(memory_space=pl.ANY)],
        out_specs=[pl.BlockSpec((s, PAIR_WIDTH), lambda b, hp, j: (b, N_PAIRS + hp)),
                   pl.BlockSpec((None, None, 2, bk), lambda b, hp, j: (b, hp, 0, j))],
        out_shape=[jax.ShapeDtypeStruct(dz.shape, dz.dtype), jax.ShapeDtypeStruct((n_batch, N_PAIRS, 2, s), F32)],
        scratch_shapes=[pltpu.VMEM((s, LANES), F32)],
        input_output_aliases={7: 0},
        name=name, compiler_params=_params("parallel", "parallel", "arbitrary"),
    )(zf, zf, zf, o32, dy, lse, c_row, dz)


def _dil_bias(slope, dil):
    qi = lax.broadcasted_iota(jnp.int32, (BLOCK, 2 * BLOCK), 0)
    kj = lax.broadcasted_iota(jnp.int32, (BLOCK, 2 * BLOCK), 1)
    delta = qi + BLOCK - kj
    return jnp.where((delta >= 0) & (delta <= BLOCK), (-slope * dil) * delta.astype(F32), NEG)


def _alibi_slope(hp, e):
    slope = jnp.float32(0.0)
    for k in range(N_PAIRS):
        slope = jnp.where(hp == k, jnp.float32(2.0 ** -(2 * k + e + 1)), slope)
    return slope


def _first_block_bias(bias):
    return jnp.where(lax.broadcasted_iota(jnp.int32, bias.shape, 1) < BLOCK, NEG, bias)


def _fill_bias(bias_scr, hp):
    for di, dil in enumerate(DILATIONS):
        for e in range(2):
            bias_scr[2 * di + e] = _dil_bias(_alibi_slope(hp, e), dil)


def _pair_specs(rows):
    return [pl.BlockSpec((rows, LANES), lambda b, hp, c0=c0: (b, 3 * hp + c0)) for c0 in range(3)]


def _strided(start, size, dil):
    return pl.ds(start, size) if dil == 1 else pl.ds(start, size, stride=dil)


QUARTER = SEQ // 4


def _to_quarters(src, dst):
    for r in range(4):
        dst[r * QUARTER:(r + 1) * QUARTER, :] = src[pl.ds(r, QUARTER, stride=4), :]


def _from_quarters(src, dst):
    for r in range(4):
        dst[pl.ds(r, QUARTER, stride=4), :] = src[r * QUARTER:(r + 1) * QUARTER, :]


def _mix_weights(l1, l2, l3):
    m = jnp.maximum(jnp.maximum(l1, l2), l3)
    e1, e2, e3 = jnp.exp(l1 - m), jnp.exp(l2 - m), jnp.exp(l3 - m)
    inv = 1.0 / (e1 + e2 + e3)
    return e1 * inv, e2 * inv, e3 * inv


def _dil_fwd(zd, n_batch, name):
    s = SEQ
    t = n_batch * s

    def body(q_ref, k_ref, v_ref, y_ref, l1_ref, l2_ref, l3_ref, o_scr, qkv4, o4, l4, bias_scr):
        _fill_bias(bias_scr, pl.program_id(1))
        for a, ref in enumerate((q_ref, k_ref, v_ref)):
            _to_quarters(ref, qkv4.at[a])

        def unit(srcs, start, first, stride, di, o_dst, l_dst):
            qrows = _strided(start, BLOCK, stride)
            krows = qrows if first else _strided(start - BLOCK * stride, 2 * BLOCK, stride)
            q = (srcs[0][qrows, :] * ATT_SCALE).astype(BF16)
            kc = srcs[1][krows, :].astype(BF16)
            vc = srcs[2][krows, :].astype(BF16)
            if first:
                kc, vc = jnp.concatenate([kc, kc]), jnp.concatenate([vc, vc])
            outs, lses = [], []
            for e in range(2):
                bias = _first_block_bias(bias_scr[2 * di + e]) if first else bias_scr[2 * di + e]
                sc = _dot_nt(jnp.where(_head_mask(e), q, jnp.zeros_like(q)), kc) + bias
                m = jnp.max(sc, axis=1, keepdims=True)
                pe = jnp.exp(sc - m)
                l = jnp.sum(pe, axis=1, keepdims=True)
                outs.append(_dot((pe * (1.0 / l)).astype(BF16), vc))
                lses.append(m + jnp.log(l))
            o_dst[qrows, :] = jnp.where(_head_mask(0), outs[0], outs[1])
            l_dst[qrows, :] = jnp.where(_head_mask(0), lses[0], lses[1])

        for n in range(SEQ // BLOCK):
            unit((q_ref, k_ref, v_ref), n * BLOCK, n == 0, 1, 0, o_scr.at[0], l1_ref)
        quarters = tuple(qkv4.at[a] for a in range(3))
        for di in (1, 2):
            stride = DILATIONS[di] // 4
            for r in range(4):
                for g in range(stride):
                    for n in range(QUARTER // (BLOCK * stride)):
                        unit(quarters, r * QUARTER + n * BLOCK * stride + g, n == 0, stride, di, o4.at[di - 1], l4.at[di - 1])
        for di, l_ref in ((1, l2_ref), (2, l3_ref)):
            _from_quarters(o4.at[di - 1], o_scr.at[di])
            _from_quarters(l4.at[di - 1], l_ref)
        w = _mix_weights(l1_ref[...], l2_ref[...], l3_ref[...])
        y_ref[...] = (w[0] * o_scr[0] + w[1] * o_scr[1] + w[2] * o_scr[2]).astype(BF16)

    blk = pl.BlockSpec((s, LANES), lambda b, hp: (b, hp))
    res = pl.pallas_call(
        body, grid=(n_batch, N_PAIRS),
        in_specs=_pair_specs(s),
        out_specs=[blk] * 4,
        out_shape=[jax.ShapeDtypeStruct((t, MIX_HALF), BF16)] + [jax.ShapeDtypeStruct((t, MIX_HALF), F32)] * 3,
        scratch_shapes=[pltpu.VMEM((3, s, LANES), F32), pltpu.VMEM((3, s, LANES), F32), pltpu.VMEM((2, s, LANES), F32),
                        pltpu.VMEM((2, s, LANES), F32), pltpu.VMEM((6, BLOCK, 2 * BLOCK), F32)],
        name=name, compiler_params=_params("parallel", "arbitrary"),
    )(zd, zd, zd)
    return res[0], res[1:]


def _dil_bwd(zd, dy, ya, lses, n_batch, name):
    s = SEQ
    t = n_batch * s

    def body(q_ref, k_ref, v_ref, dy_ref, ya_ref, l1_ref, l2_ref, l3_ref, dz_ref, w_scr, dy_scr, dot_scr, acc, st4, acc4, bias_scr):
        for di, dil in enumerate(DILATIONS):
            bias_scr[di] = jnp.concatenate([_dil_bias(_alibi_slope(pl.program_id(1), e), dil) for e in range(2)])
        for di, w in enumerate(_mix_weights(l1_ref[...], l2_ref[...], l3_ref[...])):
            w_scr[di] = w
        dya = dy_ref[...].astype(F32)
        prod = dya * ya_ref[...].astype(F32)
        per_head = [jnp.sum(jnp.where(_head_mask(e), prod, 0.0), axis=1, keepdims=True) for e in range(2)]
        dy_scr[...] = dya
        dot_scr[...] = jnp.where(_head_mask(0), per_head[0], per_head[1])
        acc[...] = jnp.zeros_like(acc)
        acc4[...] = jnp.zeros_like(acc4)
        staged = (q_ref, k_ref, v_ref, w_scr.at[1], w_scr.at[2], l2_ref, l3_ref, dy_scr, dot_scr)
        for a, ref in enumerate(staged):
            _to_quarters(ref, st4.at[a])

        def unit(srcs, dst, start, first, stride, di):
            qrows = _strided(start, BLOCK, stride)
            krows = qrows if first else _strided(start - BLOCK * stride, 2 * BLOCK, stride)
            q = (srcs[0][qrows, :] * ATT_SCALE).astype(BF16)
            kc = srcs[1][krows, :].astype(BF16)
            vc = srcs[2][krows, :].astype(BF16)
            wq = srcs[3][qrows, :]
            lse = srcs[4][qrows, :]
            do = (wq * srcs[5][qrows, :]).astype(BF16)
            sub = wq * srcs[6][qrows, :]
            heads = lambda a: jnp.concatenate([jnp.where(_head_mask(e), a, jnp.zeros_like(a)) for e in range(2)])
            column = lambda a: jnp.concatenate([a[:, HEAD_DIM * e:HEAD_DIM * e + 1] for e in range(2)])
            qq, dd = heads(q), heads(do)
            bias = bias_scr[di]
            p = jnp.exp(_dot_nt(qq, kc) + (bias[:, BLOCK:] if first else bias) - column(lse))
            dsb = (p * (_dot_nt(dd, vc) - column(sub))).astype(BF16)
            dq = _dot(jnp.concatenate([dsb[:BLOCK], dsb[BLOCK:]], axis=1), heads(kc))
            dst.at[0][qrows, :] += dq * ATT_SCALE
            dst.at[1][krows, :] += _dot_tn(dsb, qq)
            dst.at[2][krows, :] += _dot_tn(p.astype(BF16), dd)

        token_order = (q_ref, k_ref, v_ref, w_scr.at[0], l1_ref, dy_scr, dot_scr)
        for n in range(SEQ // BLOCK):
            unit(token_order, acc, n * BLOCK, n == 0, 1, 0)
        for di in (1, 2):
            quarters = (st4.at[0], st4.at[1], st4.at[2], st4.at[2 + di], st4.at[4 + di], st4.at[7], st4.at[8])
            stride = DILATIONS[di] // 4
            for r in range(4):
                for g in range(stride):
                    for n in range(QUARTER // (BLOCK * stride)):
                        unit(quarters, acc4, r * QUARTER + n * BLOCK * stride + g, n == 0, stride, di)
        for k in range(3):
            for r in range(4):
                acc.at[k][pl.ds(r, QUARTER, stride=4), :] += acc4[k, r * QUARTER:(r + 1) * QUARTER, :]
            dz_ref[:, k * LANES:(k + 1) * LANES] = acc[k].astype(BF16)

    blk = pl.BlockSpec((s, LANES), lambda b, hp: (b, hp))
    pair = pl.BlockSpec((s, PAIR_WIDTH), lambda b, hp: (b, hp))
    return pl.pallas_call(
        body, grid=(n_batch, N_PAIRS),
        in_specs=_pair_specs(s) + [blk] * 5,
        out_specs=pair,
        out_shape=jax.ShapeDtypeStruct((t, 2 * 3 * MIX_HALF), BF16),
        scratch_shapes=[pltpu.VMEM((3, s, LANES), F32), pltpu.VMEM((s, LANES), F32), pltpu.VMEM((s, LANES), F32),
                        pltpu.VMEM((3, s, LANES), F32), pltpu.VMEM((9, s, LANES), F32), pltpu.VMEM((3, s, LANES), F32),
                        pltpu.VMEM((3, 2 * BLOCK, 2 * BLOCK), F32)],
        name=name, compiler_params=_params("parallel", "arbitrary"),
    )(zd, zd, zd, dy, ya, *lses)


def _xattn_probs(q, k):
    sc = _dot_nt(q, k) * X_SCALE
    pe = jnp.exp(sc - jnp.max(sc, axis=1, keepdims=True))
    return pe / jnp.sum(pe, axis=1, keepdims=True)


def _rms(xv, g):
    return (xv * lax.rsqrt(jnp.mean(xv * xv, axis=-1, keepdims=True) + EPS) * g).astype(BF16)


def _out_xattn_fwd(ya, yf, x0, w_out, g_xattn, w_xq, kx, vx, w_xo, g_mlp, name, tm=512):
    t, d = x0.shape
    per_example = SEQ // tm

    def body(ya_ref, yf_ref, x0_ref, wa_ref, wf_ref, g2_ref, wq_ref, k_ref, v_ref, wo_ref, g3_ref,
             x1_ref, h2_ref, q_ref, o_ref, x2_ref, h3_ref):
        x1 = x0_ref[...] + (_dot(ya_ref[...], wa_ref[...]) + _dot(yf_ref[...], wf_ref[...]))
        x1_ref[...] = x1
        h2 = _rms(x1, g2_ref[...])
        h2_ref[...] = h2
        q = _dot(h2, wq_ref[...]).astype(BF16)
        q_ref[...] = q
        for h in range(X_HEADS):
            cols = slice(h * X_HEAD_DIM, (h + 1) * X_HEAD_DIM)
            p = _xattn_probs(q[:, cols], k_ref[:, cols])
            o_ref[:, cols] = _dot(p.astype(BF16), v_ref[:, cols]).astype(BF16)
        x2 = x1 + _dot(o_ref[...], wo_ref[...])
        x2_ref[...] = x2
        h3_ref[...] = _rms(x2, g3_ref[...])

    row = lambda width: pl.BlockSpec((tm, width), lambda i: (i, 0))
    whole = lambda a: pl.BlockSpec(a.shape, lambda i: (0, 0))
    mem = pl.BlockSpec((N_MEM, d), lambda i: (i // per_example, 0))
    vec = pl.BlockSpec((1, d), lambda i: (0, 0))
    w_a, w_f = w_out[:MIX_HALF], w_out[MIX_HALF:]
    return pl.pallas_call(
        body, grid=(t // tm,),
        in_specs=[row(MIX_HALF), row(MIX_HALF), row(d), whole(w_a), whole(w_f), vec, whole(w_xq), mem, mem, whole(w_xo), vec],
        out_specs=[row(d)] * 6,
        out_shape=[jax.ShapeDtypeStruct((t, d), dt) for dt in (F32, BF16, BF16, BF16, F32, BF16)],
        name=name, compiler_params=_params("arbitrary"),
    )(ya, yf, x0, w_a, w_f, g_xattn.reshape(1, d), w_xq, kx, vx, w_xo, g_mlp.reshape(1, d))


def _xattn_chain_bwd(dx2, dx2b, x1, g_xattn, qx, kx, vx, w_xo, w_xq, name, tm=512, after=None):
    t, d = x1.shape
    per_example = SEQ // tm
    order = () if after is None else (after,)

    def body(dx2_ref, dx2b_ref, x1_ref, g_ref, q_ref, k_ref, v_ref, wo_ref, wq_ref, *rest):
        dx1_ref, dx1b_ref, dg_ref, dq_ref, dk_ref, dv_ref, dk_acc, dv_acc = rest[len(order):]
        i = pl.program_id(0)

        @pl.when(i % per_example == 0)
        def _():
            dk_acc[...] = jnp.zeros_like(dk_acc)
            dv_acc[...] = jnp.zeros_like(dv_acc)

        do = _dot_nt(dx2b_ref[...], wo_ref[...]).astype(BF16)
        for h in range(X_HEADS):
            cols = slice(h * X_HEAD_DIM, (h + 1) * X_HEAD_DIM)
            q, k, do_h = q_ref[:, cols], k_ref[:, cols], do[:, cols]
            p = _xattn_probs(q, k)
            dp = _dot_nt(do_h, v_ref[:, cols])
            dsb = (p * (dp - jnp.sum(p * dp, axis=1, keepdims=True))).astype(BF16)
            dq_ref[:, cols] = (_dot(dsb, k) * X_SCALE).astype(BF16)
            dk_acc[:, cols] += _dot_tn(dsb, q) * X_SCALE
            dv_acc[:, cols] += _dot_tn(p.astype(BF16), do_h)

        @pl.when(i % per_example == per_example - 1)
        def _():
            dk_ref[...] = dk_acc[...].astype(BF16)
            dv_ref[...] = dv_acc[...].astype(BF16)

        dx, dg_rows = _rms_bwd_tile(x1_ref[...], _dot_nt(dq_ref[...], wq_ref[...]), g_ref[...])
        dx = dx2_ref[...] + dx
        dx1_ref[...] = dx
        dx1b_ref[...] = dx.astype(BF16)

        @pl.when(i == 0)
        def _():
            dg_ref[...] = jnp.zeros_like(dg_ref)

        dg_ref[...] += jnp.sum(dg_rows, axis=0, keepdims=True)

    row = pl.BlockSpec((tm, d), lambda i: (i, 0))
    vec = pl.BlockSpec((1, d), lambda i: (0, 0))
    mem = pl.BlockSpec((N_MEM, d), lambda i: (i // per_example, 0))
    whole = lambda a: pl.BlockSpec(a.shape, lambda i: (0, 0))
    return pl.pallas_call(
        body, grid=(t // tm,),
        in_specs=[row, row, row, vec, row, mem, mem, whole(w_xo), whole(w_xq)] + [pl.BlockSpec(memory_space=pl.ANY)] * len(order),
        out_specs=[row, row, vec, row, mem, mem],
        out_shape=[jax.ShapeDtypeStruct((t, d), F32), jax.ShapeDtypeStruct((t, d), BF16), jax.ShapeDtypeStruct((1, d), F32),
                   jax.ShapeDtypeStruct((t, d), BF16), jax.ShapeDtypeStruct(kx.shape, BF16), jax.ShapeDtypeStruct(kx.shape, BF16)],
        scratch_shapes=[pltpu.VMEM((N_MEM, d), F32)] * 2,
        name=name, compiler_params=_params("arbitrary"),
    )(dx2, dx2b, x1, g_xattn.reshape(1, d), qx, kx, vx, w_xo, w_xq, *order)


def _adamw(w, g, m, v, name, rows):
    r, c = w.shape
    assert r % rows == 0, (name, w.shape, rows)

    def body(w_ref, g_ref, m_ref, v_ref, d_ref, nm_ref, nv_ref):
        gv = g_ref[...]
        m1 = ADAM_B1 * m_ref[...] + (1.0 - ADAM_B1) * gv
        v1 = ADAM_B2 * v_ref[...] + (1.0 - ADAM_B2) * jnp.square(gv)
        m_hat = m1 / (1.0 - ADAM_B1 ** ADAM_STEP)
        v_hat = v1 / (1.0 - ADAM_B2 ** ADAM_STEP)
        d_ref[...] = -ADAM_LR * (m_hat / (jnp.sqrt(v_hat) + ADAM_EPS) + ADAM_WD * w_ref[...])
        nm_ref[...] = m1
        nv_ref[...] = v1

    blk = pl.BlockSpec((rows, c), lambda i: (i, 0))
    return pl.pallas_call(
        body, grid=(r // rows,), in_specs=[blk] * 4, out_specs=[blk] * 3,
        out_shape=[jax.ShapeDtypeStruct((r, c), F32)] * 3,
        name=name, compiler_params=_params("arbitrary"),
    )(w, g, m, v)


def _relu2(acc):
    a = jnp.maximum(acc, 0.0)
    return acc, a * a


def _relu2_bwd(acc, u):
    return (2.0 * jnp.maximum(u.astype(F32), 0.0) * acc,)


def _local_step(x, mem, target, vecs, w_in, late_weights, hooks=None):
    n_batch = x.shape[0]
    t = n_batch * SEQ
    x0 = x.reshape(t, D_MODEL)
    mem2 = mem.reshape(n_batch * N_MEM, D_MODEL)
    tgt = target.reshape(t, D_MODEL)

    half = 3 * MIX_HALF
    w_qkv = jnp.concatenate([_pair_major(w_in[:, :half]), _pair_major(w_in[:, half:QKV_WIDTH])], axis=1)
    w_gate = jnp.pad(w_in[:, QKV_WIDTH:], ((0, 0), (0, GATE_PAD - N_HEADS)))
    b_pad = jnp.pad(vecs["b_forget"], (0, GATE_PAD - N_HEADS)).reshape(1, GATE_PAD)

    h1, zd, zf, gate = _in_proj(x0, vecs["g_mix"], jnp.concatenate([w_qkv, w_gate], axis=1), "in_proj")
    mn = _rmsnorm(mem2, vecs["g_mem"], "norm_mem")
    c_row, sg = _gate_fwd(gate, b_pad, n_batch, "gate_fwd")
    ya, lses = _dil_fwd(zd, n_batch, "dil_fwd")
    late_start, late_finish = late_weights
    yf, of32, lse_f = _fox_fwd(zf, c_row, n_batch, "fox_fwd", late_start(ya))
    wts = late_finish(yf)
    w_out = wts["w_out"]
    kx = _matmul(mn, wts["w_xk"], "xk", out_dtypes=(BF16,))[0]
    vx = _matmul(mn, wts["w_xv"], "xv", out_dtypes=(BF16,))[0]
    x1, h2, qx, ox, x2, h3 = _out_xattn_fwd(ya, yf, x0, w_out, vecs["g_xattn"], wts["w_xq"], kx, vx, wts["w_xo"],
                                            vecs["g_mlp"], "out_xattn")
    u, a2 = _matmul(h3, wts["w_up"], "mlp_up", out_dtypes=(BF16, BF16), epilogue=_relu2)
    loss, dx3, dx3b, dg_final = _loss_bwd(a2, wts["w_down"], x2, vecs["g_final"], tgt, "mlp_down_loss")

    du = _matmul(dx3b, wts["w_down"], "mlp_down_bwd", out_dtypes=(BF16,), extras=(u,), epilogue=_relu2_bwd, w_t=True)[0]
    shards = (N_CHIPS, 2 * D_MODEL, D_MODEL)
    g_mlp = _matmul_tn(h3, du, "gw_up", packed=(shards, lambda i, j: (j, 0, 0), None))
    g_mlp = _matmul_tn(a2, dx3b, "gw_down", packed=(shards, lambda i, j: (i, 1, 0), g_mlp))
    gw_up = g_mlp[:, :D_MODEL].transpose(1, 0, 2).reshape(D_MODEL, D_FF)
    gw_down = g_mlp[:, D_MODEL:].reshape(D_FF, D_MODEL)
    on_grads, on_swapped = hooks or (None, None)
    token = on_grads("mlp", g_mlp) if hooks else None
    dx2, dx2b, dg_mlp = _matmul_rms_bwd([du], [wts["w_up"]], x2, vecs["g_mlp"], dx3, "mlp_up_bwd", after=token)

    gw_xo = _matmul_tn(ox, dx2b, "gw_xo")
    token = on_swapped("mlp", dx2b) if hooks else None
    dx1, dx1b, dg_xattn, dqx, dkx, dvx = _xattn_chain_bwd(dx2, dx2b, x1, vecs["g_xattn"], qx, kx, vx, wts["w_xo"],
                                                          wts["w_xq"], "xattn_chain_bwd", after=token)
    gw_xq = _matmul_tn(h2, dqx, "gw_xq")
    gw_xk = _matmul_tn(mn, dkx, "gw_xk")
    gw_xv = _matmul_tn(mn, dvx, "gw_xv")
    dmn = _matmul(dkx, wts["w_xk"], "xk_bwd", w_t=True)[0]
    dmn = _matmul_res(dvx, wts["w_xv"], dmn, "xv_bwd", w_t=True)
    _, _, dg_mem = _rms_bwd(mem2, dmn, vecs["g_mem"], None, "norm_mem_bwd")

    gw_out = jnp.concatenate([_matmul_tn(ya, dx1b, "gw_out_a"), _matmul_tn(yf, dx1b, "gw_out_f")], axis=0)
    token = on_grads("mid", dict(w_out=gw_out, w_xq=gw_xq, w_xk=gw_xk, w_xv=gw_xv, w_xo=gw_xo)) if hooks else None
    dy = _matmul(dx1b, w_out, "out_bwd", out_dtypes=(BF16,), w_t=True, after=token)[0]
    dz = _dil_bwd(zd, dy, ya, lses, n_batch, "dil_bwd")
    dz, dc = _fox_bwd(zf, of32, dy, lse_f, c_row, dz, n_batch, "fox_bwd")
    dzg, db = _gate_bwd(dc.reshape(n_batch, N_HEADS, SEQ), sg, "gate_bwd")
    token = on_swapped("mid", dz) if hooks else None
    gw_pm = _matmul_tn(h1, dz, "gw_in_qkv", after=token)
    gw_in = jnp.concatenate([_pair_major_inv(gw_pm[:, :half]), _pair_major_inv(gw_pm[:, half:]),
                             _matmul_tn(h1, dzg, "gw_in_gate")[:, :N_HEADS]], axis=1)
    dx0, _, dg_mix = _matmul_rms_bwd([dz, dzg], [w_qkv, w_gate], x0, vecs["g_mix"], dx1, "in_bwd")

    gw = dict(w_in=gw_in, w_out=gw_out, w_xq=gw_xq, w_xk=gw_xk, w_xv=gw_xv, w_xo=gw_xo, w_up=gw_up, w_down=gw_down)
    gv = dict(g_mix=dg_mix, g_xattn=dg_xattn, g_mem=dg_mem, g_mlp=dg_mlp, g_final=dg_final, b_forget=db)
    return loss, dx0.reshape(x.shape), gw, gv


MESH = pl.DeviceIdType.MESH
ANY = pl.BlockSpec(memory_space=pl.ANY)


def _place():
    x, y, c = lax.axis_index("x"), lax.axis_index("y"), lax.axis_index("c")
    other_chips = [(1 - x, y), (x, 1 - y), (1 - x, 1 - y)]
    return x, y, c, other_chips


def _my_chip():
    return 2 * lax.axis_index("x") + lax.axis_index("y")


def _halves(rows, c, align):
    half = rows // 2
    assert rows % (2 * align) == 0, rows
    return pl.ds(pl.multiple_of(c * half, align), half), pl.ds(pl.multiple_of((1 - c) * half, align), half)


def _place_own(wall, pack):
    return lax.dynamic_update_slice(wall, pack[None], (_my_chip(), 0, 0))


HBM = pl.BlockSpec(memory_space=pltpu.HBM)
SEM = pl.BlockSpec(memory_space=pltpu.SEMAPHORE)
SPLIT_COPY = pltpu.CompilerParams(has_side_effects=pltpu.SideEffectType.DATAFLOW_SIDE_EFFECTING)


def _in_hbm(a):
    return pltpu.with_memory_space_constraint(a, pltpu.HBM)


def _start_call(start, src, land_shape, after, name):
    land = lax.empty(land_shape, src.dtype)

    def body(src_ref, land_ref, after_ref, send_sems, recv_sems, src_thru, land_thru, token):
        del after_ref, src_thru, land_thru
        start(src_ref, land_ref, send_sems, recv_sems)
        token[...] = jnp.zeros_like(token)

    return pl.pallas_call(
        body, name=name,
        out_shape=(pltpu.SemaphoreType.DMA((3,)), pltpu.SemaphoreType.DMA((3,)), pltpu.HBM(src.shape, src.dtype),
                   pltpu.HBM(land_shape, src.dtype), jax.ShapeDtypeStruct((8, LANES), F32)),
        in_specs=(HBM, HBM, ANY), out_specs=(SEM, SEM, HBM, HBM, pl.BlockSpec(memory_space=pltpu.VMEM)),
        input_output_aliases={0: 2, 1: 3}, compiler_params=SPLIT_COPY,
    )(_in_hbm(src), _in_hbm(land), after)


def _wait_call(body, started, after, name):
    send_sems, recv_sems, src, land, _ = started
    return pl.pallas_call(
        body, name=name,
        out_shape=(pltpu.HBM(src.shape, src.dtype), pltpu.HBM(land.shape, land.dtype)),
        in_specs=(HBM, HBM, SEM, SEM, ANY), out_specs=(HBM, HBM),
        input_output_aliases={0: 0, 1: 1}, compiler_params=SPLIT_COPY,
    )(src, land, send_sems, recv_sems, after)


def _gather_copies(p_ref, wall_ref, send_sems, recv_sems):
    x, y, c, chips = _place()
    me = 2 * x + y
    mine, _ = _halves(p_ref.shape[0], c, 16)
    out, back = [], []
    for k, chip in enumerate(chips):
        peer = dict(send_sem=send_sems.at[k], recv_sem=recv_sems.at[k], device_id=(chip[0], chip[1], c), device_id_type=MESH)
        out.append(pltpu.make_async_remote_copy(src_ref=p_ref.at[mine], dst_ref=wall_ref.at[me, mine], **peer))
        slab = wall_ref.at[2 * chip[0] + chip[1], mine]
        back.append(pltpu.make_async_remote_copy(src_ref=slab, dst_ref=slab, **peer))
    return out, back


def _gather_start(pack, after, name):
    def start(p_ref, wall_ref, send_sems, recv_sems):
        for cp in _gather_copies(p_ref, wall_ref, send_sems, recv_sems)[0]:
            cp.start()

    return _start_call(start, pack, (N_CHIPS,) + pack.shape, after, name)


def _gather_wait(started, after, name):
    def body(p_ref, wall_ref, send_sems, recv_sems, after_ref, p_dead, wall_out):
        del after_ref, p_dead, wall_out
        out, back = _gather_copies(p_ref, wall_ref, send_sems, recv_sems)
        for cp_out, cp_back in zip(out, back):
            cp_out.wait_send()
            cp_back.wait_recv()

    return _wait_call(body, started, after, name)


def _pass_on(wall, name):
    def body(w_in_ref, out_ref, send_sems, recv_sems):
        del w_in_ref
        x, y, c, chips = _place()
        mine, theirs = _halves(wall.shape[1], c, 16)
        sends = []
        for k, chip in enumerate(chips):
            slab = out_ref.at[2 * chip[0] + chip[1]]
            peer = dict(send_sem=send_sems.at[k], recv_sem=recv_sems.at[k], device_id=(x, y, 1 - c), device_id_type=MESH)
            cp = pltpu.make_async_remote_copy(src_ref=slab.at[mine], dst_ref=slab.at[mine], **peer)
            cp.start()
            sends.append((cp, pltpu.make_async_remote_copy(src_ref=slab.at[theirs], dst_ref=slab.at[theirs], **peer)))
        for cp, back in sends:
            back.wait_recv()
            cp.wait_send()

    return pl.pallas_call(
        body, in_specs=[ANY], out_specs=ANY, out_shape=jax.ShapeDtypeStruct(wall.shape, wall.dtype),
        scratch_shapes=[pltpu.SemaphoreType.DMA((3,))] * 2, input_output_aliases={0: 0}, name=name,
    )(wall)


def _pass_copies(wall_ref, send_sems, recv_sems):
    x, y, c, chips = _place()
    mine, theirs = _halves(wall_ref.shape[1], c, 16)
    out, back = [], []
    for k, chip in enumerate(chips):
        slab = wall_ref.at[2 * chip[0] + chip[1]]
        peer = dict(send_sem=send_sems.at[k], recv_sem=recv_sems.at[k], device_id=(x, y, 1 - c), device_id_type=MESH)
        out.append(pltpu.make_async_remote_copy(src_ref=slab.at[mine], dst_ref=slab.at[mine], **peer))
        back.append(pltpu.make_async_remote_copy(src_ref=slab.at[theirs], dst_ref=slab.at[theirs], **peer))
    return out, back


def _pass_start(wall, name):
    def body(wall_ref, send_sems, recv_sems, wall_thru, token):
        del wall_thru
        for cp in _pass_copies(wall_ref, send_sems, recv_sems)[0]:
            cp.start()
        token[...] = jnp.zeros_like(token)

    return pl.pallas_call(
        body, name=name,
        out_shape=(pltpu.SemaphoreType.DMA((3,)), pltpu.SemaphoreType.DMA((3,)), pltpu.HBM(wall.shape, wall.dtype),
                   jax.ShapeDtypeStruct((8, LANES), F32)),
        in_specs=(HBM,), out_specs=(SEM, SEM, HBM, pl.BlockSpec(memory_space=pltpu.VMEM)),
        input_output_aliases={0: 2}, compiler_params=SPLIT_COPY,
    )(_in_hbm(wall))


def _pass_wait(started, after, name):
    send_sems, recv_sems, wall, _ = started

    def body(wall_ref, send_sems, recv_sems, after_ref, wall_out):
        del after_ref, wall_out
        for cp_out, cp_back in zip(*_pass_copies(wall_ref, send_sems, recv_sems)):
            cp_out.wait_send()
            cp_back.wait_recv()

    return pl.pallas_call(
        body, name=name, out_shape=pltpu.HBM(wall.shape, wall.dtype),
        in_specs=(HBM, SEM, SEM, ANY), out_specs=HBM,
        input_output_aliases={0: 0}, compiler_params=SPLIT_COPY,
    )(wall, send_sems, recv_sems, after)


def _swap_halves(g, name):
    half = g.shape[1] // 2

    def body(g_ref, out_ref, send_sem, recv_sem):
        x, y, c, _ = _place()
        _, theirs = _halves(g.shape[1], c, 8)
        cp = pltpu.make_async_remote_copy(src_ref=g_ref.at[:, theirs], dst_ref=out_ref, send_sem=send_sem, recv_sem=recv_sem,
                                          device_id=(x, y, 1 - c), device_id_type=MESH)
        cp.start()
        cp.wait()

    return pl.pallas_call(
        body, in_specs=[ANY], out_specs=ANY,
        out_shape=jax.ShapeDtypeStruct((N_CHIPS, half, g.shape[2]), F32),
        scratch_shapes=[pltpu.SemaphoreType.DMA, pltpu.SemaphoreType.DMA],
        name=name,
    )(g)


def _swap_copy(g_ref, land_ref, send_sems, recv_sems):
    x, y, c, _ = _place()
    _, theirs = _halves(g_ref.shape[1], c, 8)
    return pltpu.make_async_remote_copy(src_ref=g_ref.at[:, theirs], dst_ref=land_ref, send_sem=send_sems.at[0],
                                        recv_sem=recv_sems.at[0], device_id=(x, y, 1 - c), device_id_type=MESH)


def _swap_start(g, name):
    def start(g_ref, land_ref, send_sems, recv_sems):
        _swap_copy(g_ref, land_ref, send_sems, recv_sems).start()

    return _start_call(start, g, (N_CHIPS, g.shape[1] // 2, g.shape[2]), _core_index(), name)


def _swap_wait(started, after, name):
    def body(g_ref, land_ref, send_sems, recv_sems, after_ref, g_out, land_out):
        del after_ref, g_out, land_out
        cp = _swap_copy(g_ref, land_ref, send_sems, recv_sems)
        cp.wait_send()
        cp.wait_recv()

    return _wait_call(body, started, after, name)


def _core_index():
    return lax.axis_index("c").astype(jnp.int32).reshape(1)


def _row_tile(half):
    tile = max(t for t in range(16, 1025, 16) if half % t == 0)
    return tile, half // tile


def _add_sibling(g, got, name):
    half = g.shape[1] // 2
    tile, n_tiles = _row_tile(half)

    def body(c_ref, g_ref, got_ref, o_ref):
        o_ref[...] = (g_ref[...] + got_ref[...]).astype(BF16)

    width = g.shape[2]
    blk = pl.BlockSpec((None, tile, width), lambda s, i, c_ref: (s, i, 0))
    return pl.pallas_call(
        body,
        grid_spec=pltpu.PrefetchScalarGridSpec(
            num_scalar_prefetch=1, grid=(N_CHIPS, n_tiles),
            in_specs=[pl.BlockSpec((None, tile, width), lambda s, i, c_ref: (s, c_ref[0] * n_tiles + i, 0)), blk],
            out_specs=blk),
        out_shape=jax.ShapeDtypeStruct((N_CHIPS, half, width), BF16),
        name=name, compiler_params=_params("arbitrary", "arbitrary"),
    )(_core_index(), g, got)


def _exchange_copies(p_ref, land_ref, send_sems, recv_sems):
    x, y, c, chips = _place()
    me = 2 * x + y
    out, back = [], []
    for k, chip in enumerate(chips):
        peer = dict(send_sem=send_sems.at[k], recv_sem=recv_sems.at[k], device_id=(chip[0], chip[1], c), device_id_type=MESH)
        out.append(pltpu.make_async_remote_copy(src_ref=p_ref.at[2 * chip[0] + chip[1]], dst_ref=land_ref.at[me], **peer))
        slab = land_ref.at[2 * chip[0] + chip[1]]
        back.append(pltpu.make_async_remote_copy(src_ref=slab, dst_ref=slab, **peer))
    return out, back


def _with_own(got, part):
    me = _my_chip()
    return lax.dynamic_update_slice(got, lax.dynamic_slice(part, (me, 0, 0), (1,) + part.shape[1:]), (me, 0, 0))


def _exchange_start(part, name):
    def start(p_ref, land_ref, send_sems, recv_sems):
        for cp in _exchange_copies(p_ref, land_ref, send_sems, recv_sems)[0]:
            cp.start()

    return _start_call(start, part, part.shape, _core_index(), name)


def _exchange_wait(started, after, name):
    def body(p_ref, land_ref, send_sems, recv_sems, after_ref, p_dead, land_out):
        del after_ref, p_dead, land_out
        out, back = _exchange_copies(p_ref, land_ref, send_sems, recv_sems)
        for cp_out, cp_back in zip(out, back):
            cp_out.wait_send()
            cp_back.wait_recv()

    part, got = _wait_call(body, started, after, name)
    return _with_own(got, part)


def _sum_chips(parts, name):
    half, width = parts.shape[1:]
    tile, n_tiles = _row_tile(half)

    def body(c_ref, p0, p1, p2, p3, o_ref):
        f32 = lambda p: p[...].astype(F32)
        o_ref[...] = ((f32(p0) + f32(p1)) + f32(p2)) + f32(p3)

    def slab(s):
        return pl.BlockSpec((None, tile, width), lambda i, c_ref, s=s: (s, i, 0))

    return pl.pallas_call(
        body,
        grid_spec=pltpu.PrefetchScalarGridSpec(
            num_scalar_prefetch=1, grid=(n_tiles,),
            in_specs=[slab(s) for s in range(N_CHIPS)],
            out_specs=pl.BlockSpec((None, tile, width), lambda i, c_ref: (c_ref[0], i, 0))),
        out_shape=jax.ShapeDtypeStruct((2, half, width), F32),
        name=name, compiler_params=_params("arbitrary"),
    )(_core_index(), parts, parts, parts, parts)


def _share_halves(halves, name):
    def body(h_ref, out_ref, send_sem, recv_sem):
        del h_ref
        x, y, c, _ = _place()
        cp = pltpu.make_async_remote_copy(src_ref=out_ref.at[c], dst_ref=out_ref.at[c], send_sem=send_sem, recv_sem=recv_sem,
                                          device_id=(x, y, 1 - c), device_id_type=MESH)
        cp.start()
        pltpu.make_async_remote_copy(src_ref=out_ref.at[1 - c], dst_ref=out_ref.at[1 - c], send_sem=send_sem, recv_sem=recv_sem,
                                     device_id=(x, y, 1 - c), device_id_type=MESH).wait_recv()
        cp.wait_send()

    return pl.pallas_call(
        body, in_specs=[ANY], out_specs=ANY,
        out_shape=jax.ShapeDtypeStruct(halves.shape, halves.dtype),
        scratch_shapes=[pltpu.SemaphoreType.DMA] * 2,
        input_output_aliases={0: 0},
        name=name,
    )(halves)


def _share_copies(h_ref, send_sems, recv_sems):
    x, y, c, _ = _place()
    peer = dict(send_sem=send_sems.at[0], recv_sem=recv_sems.at[0], device_id=(x, y, 1 - c), device_id_type=MESH)
    return (pltpu.make_async_remote_copy(src_ref=h_ref.at[c], dst_ref=h_ref.at[c], **peer),
            pltpu.make_async_remote_copy(src_ref=h_ref.at[1 - c], dst_ref=h_ref.at[1 - c], **peer))


def _share_start(halves, name):
    def body(h_ref, send_sems, recv_sems, h_thru, token):
        del h_thru
        _share_copies(h_ref, send_sems, recv_sems)[0].start()
        token[...] = jnp.zeros_like(token)

    return pl.pallas_call(
        body, name=name,
        out_shape=(pltpu.SemaphoreType.DMA((1,)), pltpu.SemaphoreType.DMA((1,)), pltpu.HBM(halves.shape, halves.dtype),
                   jax.ShapeDtypeStruct((8, LANES), F32)),
        in_specs=(HBM,), out_specs=(SEM, SEM, HBM, pl.BlockSpec(memory_space=pltpu.VMEM)),
        input_output_aliases={0: 2}, compiler_params=SPLIT_COPY,
    )(_in_hbm(halves))


def _share_wait(started, after, name):
    send_sems, recv_sems, halves, _ = started

    def body(h_ref, send_sems, recv_sems, after_ref, h_out):
        del after_ref, h_out
        out, back = _share_copies(h_ref, send_sems, recv_sems)
        out.wait_send()
        back.wait_recv()

    done = pl.pallas_call(
        body, name=name, out_shape=pltpu.HBM(halves.shape, halves.dtype),
        in_specs=(HBM, SEM, SEM, ANY), out_specs=HBM,
        input_output_aliases={0: 0}, compiler_params=SPLIT_COPY,
    )(halves, send_sems, recv_sems, after)
    return done.reshape(2 * done.shape[1], done.shape[2])


def _reduce_parts(g, tag):
    return _add_sibling(g, _swap_halves(g, "swap_" + tag), "add_" + tag)


def _reduce_finish(got, tag):
    halves = _share_halves(_sum_chips(got, "sum_" + tag), "share_" + tag)
    return halves.reshape(2 * halves.shape[1], halves.shape[2])


SMALL_ROWS = 8


def _allreduce_small(v):
    def body(v_ref, out_ref, buf, send_sems, recv_sems):
        x, y, c, _ = _place()
        buf[4 * x + 2 * y + c] = v_ref[...]
        sends = []
        for k in range(1, N_DEV):
            px = 1 - x if k & 4 else x
            py = 1 - y if k & 2 else y
            pc = 1 - c if k & 1 else c
            cp = pltpu.make_async_remote_copy(src_ref=v_ref, dst_ref=buf.at[4 * x + 2 * y + c], send_sem=send_sems.at[k - 1],
                                              recv_sem=recv_sems.at[k - 1], device_id=(px, py, pc), device_id_type=MESH)
            cp.start()
            sends.append((cp, 4 * px + 2 * py + pc))
        for k, (cp, peer) in enumerate(sends):
            pltpu.make_async_remote_copy(src_ref=v_ref, dst_ref=buf.at[peer], send_sem=send_sems.at[k], recv_sem=recv_sems.at[k],
                                         device_id=(x, y, c), device_id_type=MESH).wait_recv()
        for cp, _ in sends:
            cp.wait_send()
        total = buf[0]
        for d in range(1, N_DEV):
            total = total + buf[d]
        out_ref[...] = total

    vmem = pl.BlockSpec(memory_space=pltpu.VMEM)
    return pl.pallas_call(
        body, in_specs=[vmem], out_specs=vmem,
        out_shape=jax.ShapeDtypeStruct(v.shape, v.dtype),
        scratch_shapes=[pltpu.VMEM((N_DEV,) + v.shape, v.dtype), pltpu.SemaphoreType.DMA((N_DEV - 1,)),
                        pltpu.SemaphoreType.DMA((N_DEV - 1,))],
        name="allreduce_small",
    )(v)


MATRICES = ("w_in", "w_out", "w_xq", "w_xk", "w_xv", "w_xo", "w_up", "w_down")
VECTORS = ("g_mix", "g_xattn", "g_mem", "g_mlp", "g_final", "b_forget")
WEIGHT_ORDER = ("g_mix", "w_in", "b_forget", "w_out", "g_xattn", "g_mem", "w_xq", "w_xk", "w_xv", "w_xo",
                "g_mlp", "w_up", "w_down", "g_final")
GROUPS = {"mlp": ("w_up", "w_down"), "mid": ("w_out", "w_xq", "w_xk", "w_xv", "w_xo"), "in": ("w_in",)}
LATE = GROUPS["mid"] + GROUPS["mlp"]
W_IN_SHARD = IN_WIDTH // N_CHIPS
SHARD_ROWS = {"w_out": 256, "w_xq": 256, "w_xk": 256, "w_xv": 256, "w_xo": 256, "w_up": 1024, "w_down": 1024}
PACK_ROWS = SHARD_ROWS
W_IN_PAD = -(-W_IN_SHARD // LANES) * LANES
ADAM_ROWS = 128


def _pack(parts, names):
    return jnp.concatenate([jnp.pad(parts[n], ((0, PACK_ROWS[n] - SHARD_ROWS[n]), (0, 0))) for n in names], axis=0)


def _unpack(a, names):
    out, pos = {}, 0
    for n in names:
        out[n] = a[..., pos:pos + SHARD_ROWS[n], :]
        pos += PACK_ROWS[n]
    return out


def _full_weights(wall, names):
    cols = lambda a: a.transpose(1, 0, 2).reshape(a.shape[1], -1)
    rows = lambda a: a.reshape(-1, a.shape[-1])
    if names == GROUPS["in"]:
        return {"w_in": cols(wall[:, :, :W_IN_SHARD])}
    return {n: cols(a) if n == "w_up" else rows(a) for n, a in _unpack(wall, names).items()}


def _shard_of(g, name, s):
    if name == "w_up":
        return g[:, s * D_MODEL:(s + 1) * D_MODEL]
    n = SHARD_ROWS[name]
    return g[s * n:(s + 1) * n]


def _pad_w_in(a):
    return jnp.pad(a, [(0, 0)] * (a.ndim - 1) + [(0, W_IN_PAD - W_IN_SHARD)])


def _pack_grads(gws, names):
    if names == GROUPS["in"]:
        return _pad_w_in(gws["w_in"].reshape(D_MODEL, N_CHIPS, W_IN_SHARD).transpose(1, 0, 2))
    return jnp.stack([_pack({n: _shard_of(gws[n], n, s) for n in names}, names) for s in range(N_CHIPS)])


def kernel(x, mem, g_mix, w_in, b_forget, w_out, g_xattn, g_mem, w_xq, w_xk, w_xv, w_xo, g_mlp, w_up, w_down, g_final, loss_target, m_g_mix, m_w_in, m_b_forget, m_w_out, m_g_xattn, m_g_mem, m_w_xq, m_w_xk, m_w_xv, m_w_xo, m_g_mlp, m_w_up, m_w_down, m_g_final, v_g_mix, v_w_in, v_b_forget, v_w_out, v_g_xattn, v_g_mem, v_w_xq, v_w_xk, v_w_xv, v_w_xo, v_g_mlp, v_w_up, v_w_down, v_g_final):
    given = dict(locals())
    weights = {n: given[n] for n in WEIGHT_ORDER}
    vecs = {n: weights[n] for n in VECTORS}

    shard = {n: weights[n].astype(BF16) for n in MATRICES}
    in_started = _gather_start(_pad_w_in(shard["w_in"]), _core_index(), "gather_in_start")
    late_pack = _pack(shard, LATE)
    in_pack, in_wall = _gather_wait(in_started, late_pack, "gather_in_wait")
    in_wall = _place_own(_pass_on(in_wall, "gather_in_pass"), in_pack)
    late = _gather_start(late_pack, in_wall, "gather_late_start")
    w_in_full = _full_weights(in_wall, GROUPS["in"])["w_in"]

    passing = {}

    def late_start(after):
        pack, wall = _gather_wait(late, after, "gather_late_wait")
        passing["late"] = pack, _pass_start(wall, "gather_late_pass_start")
        return passing["late"][1][3]

    def late_finish(after):
        pack, started = passing["late"]
        return _full_weights(_place_own(_pass_wait(started, after, "gather_late_pass_wait"), pack), LATE)

    started = {}

    swapping = {}

    def on_grads(group, gws):
        packed = gws if group == "mlp" else _pack_grads(gws, GROUPS[group])
        swapping[group] = _swap_start(packed, "swap_%s_start" % group)
        return swapping[group][4]

    def on_swapped(group, after):
        g, got = _swap_wait(swapping[group], after, "swap_%s_wait" % group)
        started[group] = _exchange_start(_add_sibling(g, got, "add_" + group), "exchange_%s_start" % group)
        return started[group][4]

    loss, grad_x, gw, gv = _local_step(x, mem, loss_target, vecs, w_in_full, (late_start, late_finish), (on_grads, on_swapped))

    part = _reduce_parts(_pack_grads(gw, GROUPS["in"]), "in")
    started["in"] = _exchange_start(part, "exchange_in_start")
    grads, delta, new_m, new_v = {}, {}, {}, {}

    sharing = {}

    def reduce(group, after):
        got = _exchange_wait(started[group], after, "exchange_%s_wait" % group)
        sharing[group] = _share_start(_sum_chips(got, "sum_" + group), "share_%s_start" % group)
        return sharing[group][3]

    def update(group, after):
        done = _share_wait(sharing[group], after, "share_%s_wait" % group)
        for n, a in ({"w_in": done[:, :W_IN_SHARD]} if group == "in" else _unpack(done, GROUPS[group])).items():
            grads[n] = a.reshape(weights[n].shape)
            delta[n], new_m[n], new_v[n] = _adamw(weights[n], grads[n], given["m_" + n], given["v_" + n], "adamw_" + n, ADAM_ROWS)
        return new_v[GROUPS[group][-1]]

    after = reduce("mlp", started["in"][4])
    after = reduce("mid", after)
    after = update("mlp", after)
    after = reduce("in", after)
    after = update("mid", after)

    row = lambda a: jnp.pad(a.reshape(-1), (0, D_MODEL - a.size)).reshape(1, D_MODEL)
    small = jnp.concatenate([gv[n] for n in VECTORS[:5]] + [row(gv["b_forget"][:, 0]), row(loss[0, :1]),
                             jnp.zeros((1, D_MODEL), F32)], axis=0)
    small = _allreduce_small(small)
    for k, n in enumerate(VECTORS[:5]):
        grads[n] = small[k]
    grads["b_forget"] = small[5, :N_HEADS]
    loss_total = small[6, 0]
    update("in", after)

    stack = lambda prefix: jnp.concatenate([row(given[prefix + n]) for n in VECTORS] + [jnp.zeros((2, D_MODEL), F32)], axis=0)
    g_small = jnp.concatenate([small[:6], jnp.zeros((2, D_MODEL), F32)], axis=0)
    d, m1, v1 = _adamw(stack(""), g_small, stack("m_"), stack("v_"), "adamw_vectors", SMALL_ROWS)
    for k, n in enumerate(VECTORS):
        width = weights[n].shape[0]
        delta[n], new_m[n], new_v[n] = d[k, :width], m1[k, :width], v1[k, :width]

    return (loss_total, grad_x, *[grads[n] for n in WEIGHT_ORDER], *[delta[n] for n in WEIGHT_ORDER],
            *[new_m[n] for n in WEIGHT_ORDER], *[new_v[n] for n in WEIGHT_ORDER])
```
